```python
import jax, jax.numpy as jnp
from jax import lax
import numpy as np

D_MODEL = 1024
BATCH = 8
SEQ = 8192
DEPTH = 1

N_ATTN_HEADS = 8
ATTN_HEAD_DIM = 64
ATTN_WIDTH = N_ATTN_HEADS * ATTN_HEAD_DIM
N_SGU_GROUPS = 8
SGU_WIDTH = D_MODEL // 2
SGU_GROUP_DIM = SGU_WIDTH // N_SGU_GROUPS
CHUNK = 128
Q_BLOCK = 128
D_FF = 4 * D_MODEL
N_BRANCH = 2
EPS = 1e-6
IN_SPLITS = (2 * SGU_WIDTH, ATTN_WIDTH, ATTN_WIDTH, ATTN_WIDTH, N_ATTN_HEADS, N_BRANCH * D_MODEL)
IN_WIDTH = sum(IN_SPLITS)
IN_OFFSETS = tuple(int(o) for o in np.cumsum(IN_SPLITS)[:-1])

kernel_name = "hybrid_gmlp_fox_gated_block"


def rmsnorm(x, g):
    x32 = x.astype(jnp.float32)
    y = x32 * lax.rsqrt(jnp.mean(x32 * x32, axis=-1, keepdims=True) + EPS)
    return y.astype(x.dtype) * g


def layernorm(x, g, b):
    x32 = x.astype(jnp.float32)
    mu = jnp.mean(x32, axis=-1, keepdims=True)
    xc = x32 - mu
    y = xc * lax.rsqrt(jnp.mean(xc * xc, axis=-1, keepdims=True) + EPS)
    return y.astype(x.dtype) * g + b


def chunked_sgu(z, g_sgu, b_sgu, w_spatial, b_spatial):
    B, S, _ = z.shape
    u, v = z[..., :SGU_WIDTH], z[..., SGU_WIDTH:]
    v = layernorm(v, g_sgu, b_sgu)
    vc = v.reshape(B, S // CHUNK, CHUNK, N_SGU_GROUPS, SGU_GROUP_DIM)
    causal = jnp.tril(jnp.ones((CHUNK, CHUNK), dtype=bool))
    ws = jnp.where(causal[None], w_spatial, jnp.zeros_like(w_spatial))
    s = jnp.einsum('gts,bcsgd->bctgd', ws, vc)
    s = s + b_spatial.T[None, None, :, :, None]
    return u * s.reshape(B, S, SGU_WIDTH)


def forgetting_attention(q, k, v, cum):
    B, H, S, d = q.shape
    nb = S // Q_BLOCK
    qb = q.reshape(B, H, nb, Q_BLOCK, d).transpose(2, 0, 1, 3, 4)
    cb = cum.reshape(B, H, nb, Q_BLOCK).transpose(2, 0, 1, 3)
    key_pos = jnp.arange(S)
    scale = d ** -0.5

    def one_block(args):
        q_blk, c_blk, i = args
        s = jnp.einsum('bhqd,bhkd->bhqk', q_blk, k).astype(jnp.float32) * scale
        s = s + c_blk[..., :, None] - cum[..., None, :]
        q_pos = i * Q_BLOCK + jnp.arange(Q_BLOCK)
        mask = key_pos[None, :] <= q_pos[:, None]
        s = jnp.where(mask, s, -jnp.inf)
        p = jax.nn.softmax(s, axis=-1)
        return jnp.einsum('bhqk,bhkd->bhqd', p.astype(v.dtype), v)

    out = lax.map(one_block, (qb, cb, jnp.arange(nb)))
    return out.transpose(1, 2, 0, 3, 4).reshape(B, H, S, d)


def _fwd_setup_inputs(seed: int = 0) -> dict:
    key = jax.random.key(seed)
    ks = jax.random.split(key, 20)
    L = DEPTH

    def nrm(k, shape, scale):
        return jax.random.normal(k, shape, jnp.float32) * scale

    def gain(k, shape):
        return 1.0 + 0.05 * jax.random.normal(k, shape, jnp.float32)

    return {
        "x": jax.random.normal(ks[0], (BATCH, SEQ, D_MODEL), jnp.float32),
        "g_mix_pre": gain(ks[1], (L, D_MODEL)),
        "w_in": nrm(ks[2], (L, D_MODEL, IN_WIDTH), D_MODEL ** -0.5),
        "b_forget": 2.0 + 0.5 * jax.random.normal(ks[3], (L, N_ATTN_HEADS), jnp.float32),
        "g_sgu": gain(ks[4], (L, SGU_WIDTH)),
        "b_sgu": nrm(ks[5], (L, SGU_WIDTH), 0.02),
        "w_spatial": nrm(ks[6], (L, N_SGU_GROUPS, CHUNK, CHUNK), CHUNK ** -0.5),
        "b_spatial": 1.0 + 0.1 * jax.random.normal(ks[7], (L, N_SGU_GROUPS, CHUNK), jnp.float32),
        "w_branch_sgu": nrm(ks[8], (L, SGU_WIDTH, D_MODEL), SGU_WIDTH ** -0.5),
        "w_branch_attn": nrm(ks[9], (L, ATTN_WIDTH, D_MODEL), ATTN_WIDTH ** -0.5),
        "w_out": nrm(ks[10], (L, D_MODEL, D_MODEL), D_MODEL ** -0.5),
        "g_mix_post": gain(ks[11], (L, D_MODEL)),
        "g_ffn_pre": gain(ks[12], (L, D_MODEL)),
        "w_up": nrm(ks[13], (L, D_MODEL, D_FF), D_MODEL ** -0.5),
        "w_down": nrm(ks[14], (L, D_FF, D_MODEL), D_FF ** -0.5),
        "g_ffn_post": gain(ks[15], (L, D_MODEL)),
    }


def _fwd_reference(x, g_mix_pre, w_in, b_forget, g_sgu, b_sgu, w_spatial, b_spatial,
              w_branch_sgu, w_branch_attn, w_out, g_mix_post, g_ffn_pre, w_up, w_down,
              g_ffn_post):
    B, S, _ = x.shape
    h = x
    for l in range(DEPTH):
        xn = rmsnorm(h, g_mix_pre[l])
        proj = xn @ w_in[l]
        z_sgu, q, k, v, f_logit, gate_logit = jnp.split(proj, IN_OFFSETS, axis=-1)

        y_sgu = chunked_sgu(jax.nn.gelu(z_sgu), g_sgu[l], b_sgu[l], w_spatial[l], b_spatial[l])

        def heads(t):
            return t.reshape(B, S, N_ATTN_HEADS, ATTN_HEAD_DIM).transpose(0, 2, 1, 3)
        log_f = jax.nn.log_sigmoid((f_logit + b_forget[l]).astype(jnp.float32))
        cum = jnp.cumsum(log_f, axis=1).transpose(0, 2, 1)
        y_attn = forgetting_attention(heads(q), heads(k), heads(v), cum)
        y_attn = y_attn.transpose(0, 2, 1, 3).reshape(B, S, ATTN_WIDTH)

        gates = jax.nn.sigmoid(gate_logit)
        merged = (gates[..., :D_MODEL] * (y_sgu @ w_branch_sgu[l])
                  + gates[..., D_MODEL:] * (y_attn @ w_branch_attn[l]))
        h = h + rmsnorm(merged @ w_out[l], g_mix_post[l])

        xn2 = rmsnorm(h, g_ffn_pre[l])
        hid = jnp.square(jax.nn.relu(xn2 @ w_up[l]))
        h = h + rmsnorm(hid @ w_down[l], g_ffn_post[l])
    return h


import jax as _jax
import jax.numpy as _jnp

TWIN_FORMAT = 'train_step'
FWD_PARAMS = ['x', 'g_mix_pre', 'w_in', 'b_forget', 'g_sgu', 'b_sgu', 'w_spatial', 'b_spatial', 'w_branch_sgu', 'w_branch_attn', 'w_out', 'g_mix_post', 'g_ffn_pre', 'w_up', 'w_down', 'g_ffn_post']
TWIN_WEIGHTS = ['g_mix_pre', 'w_in', 'b_forget', 'g_sgu', 'b_sgu', 'w_spatial', 'b_spatial', 'w_branch_sgu', 'w_branch_attn', 'w_out', 'g_mix_post', 'g_ffn_pre', 'w_up', 'w_down', 'g_ffn_post']
TWIN_DIFF_INPUT = 'x'
TWIN_INPUTS = ['x', 'g_mix_pre', 'w_in', 'b_forget', 'g_sgu', 'b_sgu', 'w_spatial', 'b_spatial', 'w_branch_sgu', 'w_branch_attn', 'w_out', 'g_mix_post', 'g_ffn_pre', 'w_up', 'w_down', 'g_ffn_post', 'loss_target', 'm_g_mix_pre', 'm_w_in', 'm_b_forget', 'm_g_sgu', 'm_b_sgu', 'm_w_spatial', 'm_b_spatial', 'm_w_branch_sgu', 'm_w_branch_attn', 'm_w_out', 'm_g_mix_post', 'm_g_ffn_pre', 'm_w_up', 'm_w_down', 'm_g_ffn_post', 'v_g_mix_pre', 'v_w_in', 'v_b_forget', 'v_g_sgu', 'v_b_sgu', 'v_w_spatial', 'v_b_spatial', 'v_w_branch_sgu', 'v_w_branch_attn', 'v_w_out', 'v_g_mix_post', 'v_g_ffn_pre', 'v_w_up', 'v_w_down', 'v_g_ffn_post']
TWIN_OUTPUTS = ['loss', 'grad_x', 'grad_g_mix_pre', 'grad_w_in', 'grad_b_forget', 'grad_g_sgu', 'grad_b_sgu', 'grad_w_spatial', 'grad_b_spatial', 'grad_w_branch_sgu', 'grad_w_branch_attn', 'grad_w_out', 'grad_g_mix_post', 'grad_g_ffn_pre', 'grad_w_up', 'grad_w_down', 'grad_g_ffn_post', 'delta_g_mix_pre', 'delta_w_in', 'delta_b_forget', 'delta_g_sgu', 'delta_b_sgu', 'delta_w_spatial', 'delta_b_spatial', 'delta_w_branch_sgu', 'delta_w_branch_attn', 'delta_w_out', 'delta_g_mix_post', 'delta_g_ffn_pre', 'delta_w_up', 'delta_w_down', 'delta_g_ffn_post', 'new_m_g_mix_pre', 'new_m_w_in', 'new_m_b_forget', 'new_m_g_sgu', 'new_m_b_sgu', 'new_m_w_spatial', 'new_m_b_spatial', 'new_m_w_branch_sgu', 'new_m_w_branch_attn', 'new_m_w_out', 'new_m_g_mix_post', 'new_m_g_ffn_pre', 'new_m_w_up', 'new_m_w_down', 'new_m_g_ffn_post', 'new_v_g_mix_pre', 'new_v_w_in', 'new_v_b_forget', 'new_v_g_sgu', 'new_v_b_sgu', 'new_v_w_spatial', 'new_v_b_spatial', 'new_v_w_branch_sgu', 'new_v_w_branch_attn', 'new_v_w_out', 'new_v_g_mix_post', 'new_v_g_ffn_pre', 'new_v_w_up', 'new_v_w_down', 'new_v_g_ffn_post']
TWIN_LEAF_KINDS = {'loss': 'loss', 'grad_x': 'grad_x', 'grad_g_mix_pre': 'grad_w', 'grad_w_in': 'grad_w', 'grad_b_forget': 'grad_w', 'grad_g_sgu': 'grad_w', 'grad_b_sgu': 'grad_w', 'grad_w_spatial': 'grad_w', 'grad_b_spatial': 'grad_w', 'grad_w_branch_sgu': 'grad_w', 'grad_w_branch_attn': 'grad_w', 'grad_w_out': 'grad_w', 'grad_g_mix_post': 'grad_w', 'grad_g_ffn_pre': 'grad_w', 'grad_w_up': 'grad_w', 'grad_w_down': 'grad_w', 'grad_g_ffn_post': 'grad_w', 'delta_g_mix_pre': 'delta_w', 'delta_w_in': 'delta_w', 'delta_b_forget': 'delta_w', 'delta_g_sgu': 'delta_w', 'delta_b_sgu': 'delta_w', 'delta_w_spatial': 'delta_w', 'delta_b_spatial': 'delta_w', 'delta_w_branch_sgu': 'delta_w', 'delta_w_branch_attn': 'delta_w', 'delta_w_out': 'delta_w', 'delta_g_mix_post': 'delta_w', 'delta_g_ffn_pre': 'delta_w', 'delta_w_up': 'delta_w', 'delta_w_down': 'delta_w', 'delta_g_ffn_post': 'delta_w', 'new_m_g_mix_pre': 'new_m', 'new_m_w_in': 'new_m', 'new_m_b_forget': 'new_m', 'new_m_g_sgu': 'new_m', 'new_m_b_sgu': 'new_m', 'new_m_w_spatial': 'new_m', 'new_m_b_spatial': 'new_m', 'new_m_w_branch_sgu': 'new_m', 'new_m_w_branch_attn': 'new_m', 'new_m_w_out': 'new_m', 'new_m_g_mix_post': 'new_m', 'new_m_g_ffn_pre': 'new_m', 'new_m_w_up': 'new_m', 'new_m_w_down': 'new_m', 'new_m_g_ffn_post': 'new_m', 'new_v_g_mix_pre': 'new_v', 'new_v_w_in': 'new_v', 'new_v_b_forget': 'new_v', 'new_v_g_sgu': 'new_v', 'new_v_b_sgu': 'new_v', 'new_v_w_spatial': 'new_v', 'new_v_b_spatial': 'new_v', 'new_v_w_branch_sgu': 'new_v', 'new_v_w_branch_attn': 'new_v', 'new_v_w_out': 'new_v', 'new_v_g_mix_post': 'new_v', 'new_v_g_ffn_pre': 'new_v', 'new_v_w_up': 'new_v', 'new_v_w_down': 'new_v', 'new_v_g_ffn_post': 'new_v'}


def _forward(args):
    return _fwd_reference(*[args[k] for k in FWD_PARAMS])


def _output_shape():
    def fwd():
        inp = _fwd_setup_inputs(0)
        return _fwd_reference(*[inp[k] for k in FWD_PARAMS])
    out = _jax.eval_shape(fwd)
    return out.shape, out.dtype

N_MICROBATCH = 1
ADAM_LR = 0.001
ADAM_B1 = 0.9
ADAM_B2 = 0.999
ADAM_EPS = 1e-08
ADAM_WD = 0.01
ADAM_STEP = 10
PER_EXAMPLE_BATCH_AXIS = {'x': 0, 'loss_target': 0}
SHARED_INPUTS = []
_WEIGHT_DTYPES = {'g_mix_pre': _jnp.float32, 'w_in': _jnp.float32, 'b_forget': _jnp.float32, 'g_sgu': _jnp.float32, 'b_sgu': _jnp.float32, 'w_spatial': _jnp.float32, 'b_spatial': _jnp.float32, 'w_branch_sgu': _jnp.float32, 'w_branch_attn': _jnp.float32, 'w_out': _jnp.float32, 'g_mix_post': _jnp.float32, 'g_ffn_pre': _jnp.float32, 'w_up': _jnp.float32, 'w_down': _jnp.float32, 'g_ffn_post': _jnp.float32}
MOMENT_SCALE = {'g_mix_pre': 1.252545e+00, 'w_in': 5.465075e-01, 'b_forget': 5.100081e+00, 'g_sgu': 7.114825e-01, 'b_sgu': 5.212439e-01, 'w_spatial': 2.800294e-01, 'b_spatial': 5.293327e-01, 'w_branch_sgu': 8.002182e+00, 'w_branch_attn': 4.605937e-01, 'w_out': 8.174414e+00, 'g_mix_post': 6.470121e+01, 'g_ffn_pre': 2.842986e+00, 'w_up': 1.481120e+00, 'w_down': 8.000405e+00, 'g_ffn_post': 6.625816e+01}


def _to_microbatches(a, axis):
    t = _jnp.moveaxis(a, axis, 0)
    t = t.reshape((N_MICROBATCH, t.shape[0] // N_MICROBATCH) + t.shape[1:])
    return _jnp.moveaxis(t, 1, axis + 1)


def setup_inputs(seed: int = 0) -> dict:
    inp = _fwd_setup_inputs(seed)
    key = _jax.random.fold_in(_jax.random.key(seed), 7919)
    shape, _ = _output_shape()
    out = dict(inp)
    out["loss_target"] = _jax.random.normal(_jax.random.fold_in(key, 0), shape, _jnp.float32)
    for i, name in enumerate(TWIN_WEIGHTS):
        w = inp[name].astype(_jnp.float32)
        if MOMENT_SCALE is None:
            s = _jnp.sqrt(_jnp.mean(_jnp.square(w)) + 1e-30)
        else:
            s = MOMENT_SCALE[name]
        km, kv = _jax.random.split(_jax.random.fold_in(key, i + 1))
        out[name] = w
        out["m_" + name] = s * _jax.random.normal(km, w.shape, _jnp.float32)
        out["v_" + name] = (s * s) * _jax.random.uniform(kv, w.shape, _jnp.float32, 0.5, 1.5)
    if N_MICROBATCH > 1:
        for name, axis in PER_EXAMPLE_BATCH_AXIS.items():
            out[name] = _to_microbatches(out[name], axis)
    return {'x': out['x'], 'g_mix_pre': out['g_mix_pre'], 'w_in': out['w_in'], 'b_forget': out['b_forget'], 'g_sgu': out['g_sgu'], 'b_sgu': out['b_sgu'], 'w_spatial': out['w_spatial'], 'b_spatial': out['b_spatial'], 'w_branch_sgu': out['w_branch_sgu'], 'w_branch_attn': out['w_branch_attn'], 'w_out': out['w_out'], 'g_mix_post': out['g_mix_post'], 'g_ffn_pre': out['g_ffn_pre'], 'w_up': out['w_up'], 'w_down': out['w_down'], 'g_ffn_post': out['g_ffn_post'], 'loss_target': out['loss_target'], 'm_g_mix_pre': out['m_g_mix_pre'], 'm_w_in': out['m_w_in'], 'm_b_forget': out['m_b_forget'], 'm_g_sgu': out['m_g_sgu'], 'm_b_sgu': out['m_b_sgu'], 'm_w_spatial': out['m_w_spatial'], 'm_b_spatial': out['m_b_spatial'], 'm_w_branch_sgu': out['m_w_branch_sgu'], 'm_w_branch_attn': out['m_w_branch_attn'], 'm_w_out': out['m_w_out'], 'm_g_mix_post': out['m_g_mix_post'], 'm_g_ffn_pre': out['m_g_ffn_pre'], 'm_w_up': out['m_w_up'], 'm_w_down': out['m_w_down'], 'm_g_ffn_post': out['m_g_ffn_post'], 'v_g_mix_pre': out['v_g_mix_pre'], 'v_w_in': out['v_w_in'], 'v_b_forget': out['v_b_forget'], 'v_g_sgu': out['v_g_sgu'], 'v_b_sgu': out['v_b_sgu'], 'v_w_spatial': out['v_w_spatial'], 'v_b_spatial': out['v_b_spatial'], 'v_w_branch_sgu': out['v_w_branch_sgu'], 'v_w_branch_attn': out['v_w_branch_attn'], 'v_w_out': out['v_w_out'], 'v_g_mix_post': out['v_g_mix_post'], 'v_g_ffn_pre': out['v_g_ffn_pre'], 'v_w_up': out['v_w_up'], 'v_w_down': out['v_w_down'], 'v_g_ffn_post': out['v_g_ffn_post']}


def _loss(weights, diff, rest, loss_target):
    with _jax.named_scope("forward"):
        args = {**rest, TWIN_DIFF_INPUT: diff, **{k: w.astype(_WEIGHT_DTYPES[k]) for k, w in weights.items()}}
        y = _forward(args)
    with _jax.named_scope("loss_head"):
        err = _jnp.square(y.astype(_jnp.float32) - loss_target)
        return 0.5 * _jnp.sum(_jnp.mean(err, axis=-1)) if err.ndim else 0.5 * err


def _adamw(w, g, m, v):
    m = ADAM_B1 * m + (1.0 - ADAM_B1) * g
    v = ADAM_B2 * v + (1.0 - ADAM_B2) * _jnp.square(g)
    m_hat = m / (1.0 - ADAM_B1 ** ADAM_STEP)
    v_hat = v / (1.0 - ADAM_B2 ** ADAM_STEP)
    delta = -ADAM_LR * (m_hat / (_jnp.sqrt(v_hat) + ADAM_EPS) + ADAM_WD * w)
    return delta, m, v


def reference(x, g_mix_pre, w_in, b_forget, g_sgu, b_sgu, w_spatial, b_spatial, w_branch_sgu, w_branch_attn, w_out, g_mix_post, g_ffn_pre, w_up, w_down, g_ffn_post, loss_target, m_g_mix_pre, m_w_in, m_b_forget, m_g_sgu, m_b_sgu, m_w_spatial, m_b_spatial, m_w_branch_sgu, m_w_branch_attn, m_w_out, m_g_mix_post, m_g_ffn_pre, m_w_up, m_w_down, m_g_ffn_post, v_g_mix_pre, v_w_in, v_b_forget, v_g_sgu, v_b_sgu, v_w_spatial, v_b_spatial, v_w_branch_sgu, v_w_branch_attn, v_w_out, v_g_mix_post, v_g_ffn_pre, v_w_up, v_w_down, v_g_ffn_post):
    given = dict(x=x, g_mix_pre=g_mix_pre, w_in=w_in, b_forget=b_forget, g_sgu=g_sgu, b_sgu=b_sgu, w_spatial=w_spatial, b_spatial=b_spatial, w_branch_sgu=w_branch_sgu, w_branch_attn=w_branch_attn, w_out=w_out, g_mix_post=g_mix_post, g_ffn_pre=g_ffn_pre, w_up=w_up, w_down=w_down, g_ffn_post=g_ffn_post, loss_target=loss_target, m_g_mix_pre=m_g_mix_pre, m_w_in=m_w_in, m_b_forget=m_b_forget, m_g_sgu=m_g_sgu, m_b_sgu=m_b_sgu, m_w_spatial=m_w_spatial, m_b_spatial=m_b_spatial, m_w_branch_sgu=m_w_branch_sgu, m_w_branch_attn=m_w_branch_attn, m_w_out=m_w_out, m_g_mix_post=m_g_mix_post, m_g_ffn_pre=m_g_ffn_pre, m_w_up=m_w_up, m_w_down=m_w_down, m_g_ffn_post=m_g_ffn_post, v_g_mix_pre=v_g_mix_pre, v_w_in=v_w_in, v_b_forget=v_b_forget, v_g_sgu=v_g_sgu, v_b_sgu=v_b_sgu, v_w_spatial=v_w_spatial, v_b_spatial=v_b_spatial, v_w_branch_sgu=v_w_branch_sgu, v_w_branch_attn=v_w_branch_attn, v_w_out=v_w_out, v_g_mix_post=v_g_mix_post, v_g_ffn_pre=v_g_ffn_pre, v_w_up=v_w_up, v_w_down=v_w_down, v_g_ffn_post=v_g_ffn_post)
    weights = {n: given[n] for n in TWIN_WEIGHTS}
    shared = {n: given[n] for n in SHARED_INPUTS}
    per_example = {n: given[n] for n in ['x']}
    grad_fn = _jax.value_and_grad(_loss, argnums=(0, 1))

    def one_microbatch(ex, loss_target):
        ex = dict(ex)
        diff = ex.pop(TWIN_DIFF_INPUT)
        return grad_fn(weights, diff, {**shared, **ex}, loss_target)

    if N_MICROBATCH == 1:
        loss, (grad_w, grad_x) = one_microbatch(per_example, given["loss_target"])
    else:
        def body(carry, xs):
            loss_sum, grad_sum = carry
            l_k, (gw_k, gx_k) = one_microbatch(xs[0], xs[1])
            with _jax.named_scope("update"):
                return (loss_sum + l_k, _jax.tree.map(_jnp.add, grad_sum, gw_k)), gx_k

        init = (_jnp.zeros((), _jnp.float32), _jax.tree.map(_jnp.zeros_like, weights))
        (loss, grad_w), grad_x = _jax.lax.scan(body, init, (per_example, given["loss_target"]))
    with _jax.named_scope("update"):
        delta_w, new_m, new_v = {}, {}, {}
        for n in TWIN_WEIGHTS:
            delta_w[n], new_m[n], new_v[n] = _adamw(weights[n], grad_w[n], given["m_" + n], given["v_" + n])
    return (loss, grad_x, *[grad_w[n] for n in TWIN_WEIGHTS], *[delta_w[n] for n in TWIN_WEIGHTS],
            *[new_m[n] for n in TWIN_WEIGHTS], *[new_v[n] for n in TWIN_WEIGHTS])
```

```python
import functools

import jax
import jax.numpy as jnp
from jax import lax
from jax.experimental import pallas as pl
from jax.experimental.pallas import tpu as pltpu

F32 = jnp.float32
BF = jnp.bfloat16
MESH = pl.DeviceIdType.MESH

D_MODEL = 1024
N_HEADS = 8
HEAD_DIM = 64
ATTN_W = N_HEADS * HEAD_DIM
SGU_W = 512
N_GROUPS = 8
CHUNK = 128
D_FF = 4096
EPS = 1e-6
Q_SCALE = HEAD_DIM ** -0.5
N_CHIPS = 4
LANES = 128

ADAM_LR = 0.001
ADAM_B1 = 0.9
ADAM_B2 = 0.999
ADAM_EPS = 1e-08
ADAM_WD = 0.01
ADAM_STEP = 10

VMEM_LIMIT = 48 * 1024 * 1024
NEG = -1e30

LANE_ROWSUM = HEAD_DIM
LANE_COLSUM = HEAD_DIM + 3


def _params(*sem):
    return pltpu.CompilerParams(dimension_semantics=sem, vmem_limit_bytes=VMEM_LIMIT)


def _dot(a, b):
    return jnp.dot(a, b, preferred_element_type=F32)


def _dot_nt(a, b):
    return lax.dot_general(a, b, (((1,), (1,)), ((), ())), preferred_element_type=F32)


def _dot_tn(a, b):
    return lax.dot_general(a, b, (((0,), (0,)), ((), ())), preferred_element_type=F32)


def _split3(c):
    hi = c.astype(BF).astype(F32)
    r = c - hi
    mid = r.astype(BF).astype(F32)
    lo = (r - mid).astype(BF).astype(F32)
    return hi, mid, lo


def _gelu(x):
    k = 0.7978845608028654
    return 0.5 * x * (1.0 + jnp.tanh(k * (x + 0.044715 * (x * x * x))))


def _gelu_grad(x):
    k = 0.7978845608028654
    x2 = x * x
    t = jnp.tanh(k * (x + 0.044715 * (x2 * x)))
    return 0.5 * (1.0 + t) + 0.5 * x * (1.0 - t * t) * (k * (1.0 + 3.0 * 0.044715 * x2))


def _rms_bwd(a, g, dy):
    r = lax.rsqrt(jnp.mean(a * a, axis=-1, keepdims=True) + EPS)
    n = a * r
    dn = dy * g
    da = r * (dn - n * jnp.mean(dn * n, axis=-1, keepdims=True))
    return da, dy * n


def _matmul(pairs, *, nt, out_dtypes, tm, tn, name, epilogue=None, extras=()):
    n_pairs = len(pairs)
    n_extra = len(extras)
    M = pairs[0][0].shape[0]
    N = pairs[0][1].shape[0] if nt else pairs[0][1].shape[1]
    tm, tn = min(tm, M), min(tn, N)
    assert M % tm == 0 and N % tn == 0

    def body(*refs):
        acc = None
        for p in range(n_pairs):
            a_ref, b_ref = refs[2 * p], refs[2 * p + 1]
            d = _dot_nt(a_ref[...], b_ref[...]) if nt else _dot(a_ref[...], b_ref[...])
            acc = d if acc is None else acc + d
        e_refs = refs[2 * n_pairs:2 * n_pairs + n_extra]
        o_refs = refs[2 * n_pairs + n_extra:]
        outs = (acc,) if epilogue is None else epilogue(acc, *[e[...] for e in e_refs])
        for o_ref, o in zip(o_refs, outs, strict=True):
            o_ref[...] = o.astype(o_ref.dtype)

    in_specs, args = [], []
    for a, b in pairs:
        K = a.shape[1]
        in_specs.append(pl.BlockSpec((tm, K), lambda i, j: (i, 0)))
        in_specs.append(pl.BlockSpec((tn, K), lambda i, j: (j, 0)) if nt else pl.BlockSpec((K, tn), lambda i, j: (0, j)))
        args += [a, b]
    for e in extras:
        in_specs.append(pl.BlockSpec((tm, tn), lambda i, j: (i, j)))
        args.append(e)
    outs = pl.pallas_call(
        body, name=name, grid=(M // tm, N // tn), in_specs=in_specs,
        out_specs=[pl.BlockSpec((tm, tn), lambda i, j: (i, j)) for _ in out_dtypes],
        out_shape=[jax.ShapeDtypeStruct((M, N), dt) for dt in out_dtypes],
        compiler_params=_params("parallel", "parallel"),
    )(*args)
    return outs if len(outs) > 1 else outs[0]


def _matmul_tn(a, b, *, name, tm=1024, tn=1024, tk=512):
    T, K1 = a.shape
    N = b.shape[1]
    tm, tn, tk = min(tm, K1), min(tn, N), min(tk, T)
    assert K1 % tm == 0 and N % tn == 0 and T % tk == 0

    def body(a_ref, b_ref, o_ref):
        @pl.when(pl.program_id(2) == 0)
        def _():
            o_ref[...] = jnp.zeros_like(o_ref)

        o_ref[...] += _dot_tn(a_ref[...], b_ref[...])

    return pl.pallas_call(
        body, name=name, grid=(K1 // tm, N // tn, T // tk),
        in_specs=[pl.BlockSpec((tk, tm), lambda i, j, k: (k, i)), pl.BlockSpec((tk, tn), lambda i, j, k: (k, j))],
        out_specs=pl.BlockSpec((tm, tn), lambda i, j, k: (i, j)),
        out_shape=jax.ShapeDtypeStruct((K1, N), F32),
        compiler_params=_params("parallel", "parallel", "arbitrary"),
    )(a, b)


def _branch_merge(ysgu, yattn, w_bs, w_ba, gl, *, tm=512, tn=512):
    T = ysgu.shape[0]
    tm = min(tm, T)
    nj = D_MODEL // tn

    def body(ys_ref, ya_ref, wbs_ref, wba_ref, gla_ref, glb_ref, a_ref, b_ref, m_ref):
        a = _dot(ys_ref[...], wbs_ref[...])
        b = _dot(ya_ref[...], wba_ref[...])
        a_ref[...] = a
        b_ref[...] = b
        m_ref[...] = (jax.nn.sigmoid(gla_ref[...]) * a + jax.nn.sigmoid(glb_ref[...]) * b).astype(BF)

    return pl.pallas_call(
        body, name="branch_merge", grid=(T // tm, nj),
        in_specs=[
            pl.BlockSpec((tm, SGU_W), lambda i, j: (i, 0)),
            pl.BlockSpec((tm, ATTN_W), lambda i, j: (i, 0)),
            pl.BlockSpec((SGU_W, tn), lambda i, j: (0, j)),
            pl.BlockSpec((ATTN_W, tn), lambda i, j: (0, j)),
            pl.BlockSpec((tm, tn), lambda i, j: (i, j)),
            pl.BlockSpec((tm, tn), lambda i, j: (i, j + nj)),
        ],
        out_specs=[pl.BlockSpec((tm, tn), lambda i, j: (i, j))] * 3,
        out_shape=[jax.ShapeDtypeStruct((T, D_MODEL), F32), jax.ShapeDtypeStruct((T, D_MODEL), F32),
                   jax.ShapeDtypeStruct((T, D_MODEL), BF)],
        compiler_params=_params("parallel", "parallel"),
    )(ysgu, yattn, w_bs, w_ba, gl, gl)


def _row_spec(tr, width):
    return pl.BlockSpec((tr, width), lambda i: (i, 0))


def _vec_spec(width):
    return pl.BlockSpec((1, width), lambda i: (0, 0))


def _rms_fwd(x, g, *, tr=256):
    T = x.shape[0]
    tr = min(tr, T)

    def body(x_ref, g_ref, o_ref):
        xv = x_ref[...]
        r = lax.rsqrt(jnp.mean(xv * xv, axis=-1, keepdims=True) + EPS)
        o_ref[...] = ((xv * r) * g_ref[...]).astype(BF)

    return pl.pallas_call(
        body, name="rms_fwd", grid=(T // tr,),
        in_specs=[_row_spec(tr, D_MODEL), _vec_spec(D_MODEL)], out_specs=_row_spec(tr, D_MODEL),
        out_shape=jax.ShapeDtypeStruct((T, D_MODEL), BF), compiler_params=_params("parallel"),
    )(x, g)


def _mixer_out_fwd(o, x, g_post, g_pre, *, tr=256):
    T = x.shape[0]
    tr = min(tr, T)

    def body(o_ref, x_ref, gpost_ref, gpre_ref, h1_ref, xn2_ref):
        ov = o_ref[...]
        r = lax.rsqrt(jnp.mean(ov * ov, axis=-1, keepdims=True) + EPS)
        h1 = x_ref[...] + (ov * r) * gpost_ref[...]
        h1_ref[...] = h1
        r2 = lax.rsqrt(jnp.mean(h1 * h1, axis=-1, keepdims=True) + EPS)
        xn2_ref[...] = ((h1 * r2) * gpre_ref[...]).astype(BF)

    return pl.pallas_call(
        body, name="mixer_out_fwd", grid=(T // tr,),
        in_specs=[_row_spec(tr, D_MODEL), _row_spec(tr, D_MODEL), _vec_spec(D_MODEL), _vec_spec(D_MODEL)],
        out_specs=[_row_spec(tr, D_MODEL), _row_spec(tr, D_MODEL)],
        out_shape=[jax.ShapeDtypeStruct((T, D_MODEL), F32), jax.ShapeDtypeStruct((T, D_MODEL), BF)],
        compiler_params=_params("parallel"),
    )(o, x, g_post, g_pre)


def _loss_head(dn, h1, target, g_post, *, tr=256):
    T = dn.shape[0]
    tr = min(tr, T)

    def body(dn_ref, h1_ref, t_ref, g_ref, sq_ref, dy_ref, ddn_ref, dg_ref):
        @pl.when(pl.program_id(0) == 0)
        def _():
            sq_ref[...] = jnp.zeros_like(sq_ref)
            dg_ref[...] = jnp.zeros_like(dg_ref)

        a = dn_ref[...]
        g = g_ref[...]
        r = lax.rsqrt(jnp.mean(a * a, axis=-1, keepdims=True) + EPS)
        err = h1_ref[...] + (a * r) * g - t_ref[...]
        sq_ref[...] += jnp.sum(err * err, axis=0, keepdims=True)
        dy = err * (1.0 / D_MODEL)
        dy_ref[...] = dy
        da, dgp = _rms_bwd(a, g, dy)
        ddn_ref[...] = da.astype(BF)
        dg_ref[...] += jnp.sum(dgp, axis=0, keepdims=True)

    return pl.pallas_call(
        body, name="loss_head", grid=(T // tr,),
        in_specs=[_row_spec(tr, D_MODEL)] * 3 + [_vec_spec(D_MODEL)],
        out_specs=[_vec_spec(D_MODEL), _row_spec(tr, D_MODEL), _row_spec(tr, D_MODEL), _vec_spec(D_MODEL)],
        out_shape=[jax.ShapeDtypeStruct((1, D_MODEL), F32), jax.ShapeDtypeStruct((T, D_MODEL), F32),
                   jax.ShapeDtypeStruct((T, D_MODEL), BF), jax.ShapeDtypeStruct((1, D_MODEL), F32)],
        compiler_params=_params("arbitrary"),
    )(dn, h1, target, g_post)


def _mixer_out_bwd(h1, dxn2, dy, o, g_pre, g_post, *, tr=256):
    T = h1.shape[0]
    tr = min(tr, T)

    def body(h1_ref, dxn2_ref, dy_ref, o_ref, gpre_ref, gpost_ref, dh1_ref, do_ref, dgpre_ref, dgpost_ref):
        @pl.when(pl.program_id(0) == 0)
        def _():
            dgpre_ref[...] = jnp.zeros_like(dgpre_ref)
            dgpost_ref[...] = jnp.zeros_like(dgpost_ref)

        da, dgp = _rms_bwd(h1_ref[...], gpre_ref[...], dxn2_ref[...])
        dh1 = dy_ref[...] + da
        dh1_ref[...] = dh1
        dgpre_ref[...] += jnp.sum(dgp, axis=0, keepdims=True)
        do, dgp2 = _rms_bwd(o_ref[...], gpost_ref[...], dh1)
        do_ref[...] = do.astype(BF)
        dgpost_ref[...] += jnp.sum(dgp2, axis=0, keepdims=True)

    return pl.pallas_call(
        body, name="mixer_out_bwd", grid=(T // tr,),
        in_specs=[_row_spec(tr, D_MODEL)] * 4 + [_vec_spec(D_MODEL)] * 2,
        out_specs=[_row_spec(tr, D_MODEL), _row_spec(tr, D_MODEL), _vec_spec(D_MODEL), _vec_spec(D_MODEL)],
        out_shape=[jax.ShapeDtypeStruct((T, D_MODEL), F32), jax.ShapeDtypeStruct((T, D_MODEL), BF),
                   jax.ShapeDtypeStruct((1, D_MODEL), F32), jax.ShapeDtypeStruct((1, D_MODEL), F32)],
        compiler_params=_params("arbitrary"),
    )(h1, dxn2, dy, o, g_pre, g_post)


def _input_norm_bwd(x, dxn, dh1, g, *, tr=256):
    T = x.shape[0]
    tr = min(tr, T)

    def body(x_ref, dxn_ref, dh1_ref, g_ref, dx_ref, dg_ref):
        @pl.when(pl.program_id(0) == 0)
        def _():
            dg_ref[...] = jnp.zeros_like(dg_ref)

        da, dgp = _rms_bwd(x_ref[...], g_ref[...], dxn_ref[...])
        dx_ref[...] = dh1_ref[...] + da
        dg_ref[...] += jnp.sum(dgp, axis=0, keepdims=True)

    return pl.pallas_call(
        body, name="input_norm_bwd", grid=(T // tr,),
        in_specs=[_row_spec(tr, D_MODEL)] * 3 + [_vec_spec(D_MODEL)],
        out_specs=[_row_spec(tr, D_MODEL), _vec_spec(D_MODEL)],
        out_shape=[jax.ShapeDtypeStruct((T, D_MODEL), F32), jax.ShapeDtypeStruct((1, D_MODEL), F32)],
        compiler_params=_params("arbitrary"),
    )(x, dxn, dh1, g)


def _gate_bwd(dm, a, b, gl, *, tr=256):
    T = dm.shape[0]
    tr = min(tr, T)

    def body(dm_ref, a_ref, b_ref, gla_ref, glb_ref, da_ref, db_ref, dgla_ref, dglb_ref):
        dmv = dm_ref[...]
        ga = jax.nn.sigmoid(gla_ref[...])
        gb = jax.nn.sigmoid(glb_ref[...])
        da_ref[...] = (dmv * ga).astype(BF)
        db_ref[...] = (dmv * gb).astype(BF)
        dgla_ref[...] = (dmv * a_ref[...] * (ga * (1.0 - ga))).astype(BF)
        dglb_ref[...] = (dmv * b_ref[...] * (gb * (1.0 - gb))).astype(BF)

    spec = _row_spec(tr, D_MODEL)
    spec_b = pl.BlockSpec((tr, D_MODEL), lambda i: (i, 1))
    da, db, dgla, dglb = pl.pallas_call(
        body, name="gate_bwd", grid=(T // tr,),
        in_specs=[spec, spec, spec, spec, spec_b], out_specs=[spec] * 4,
        out_shape=[jax.ShapeDtypeStruct((T, D_MODEL), BF)] * 4, compiler_params=_params("parallel"),
    )(dm, a, b, gl, gl)
    return da, db, dgla, dglb


def _sgu_norm(z_tile, g, b):
    gz = _gelu(z_tile)
    u, vv = gz[:, :SGU_W], gz[:, SGU_W:]
    xc = vv - jnp.mean(vv, axis=-1, keepdims=True)
    rstd = lax.rsqrt(jnp.mean(xc * xc, axis=-1, keepdims=True) + EPS)
    xhat = xc * rstd
    return u, xhat, rstd, xhat * g + b


def _sgu_mix(w_ref, v_bf, first_half):
    parts = []
    for p in range(N_GROUPS // 2):
        vp = v_bf[:, p * LANES:(p + 1) * LANES]
        parts.append(jnp.where(first_half, _dot(w_ref[2 * p], vp), _dot(w_ref[2 * p + 1], vp)))
    return jnp.concatenate(parts, axis=1)


def _sgu_fwd(z, g_sgu, b_sgu, ws, bias_plane, *, tm=512):
    T = z.shape[0]
    tm = min(tm, T)

    def body(z_ref, g_ref, b_ref, ws_ref, bp_ref, y_ref):
        u, _, _, vn = _sgu_norm(z_ref[...], g_ref[...], b_ref[...])
        vn_bf = vn.astype(BF)
        first_half = lax.broadcasted_iota(jnp.int32, (CHUNK, LANES), 1) < HEAD_DIM
        for c in range(tm // CHUNK):
            rows = slice(c * CHUNK, (c + 1) * CHUNK)
            s = _sgu_mix(ws_ref, vn_bf[rows, :], first_half) + bp_ref[...]
            y_ref[rows, :] = (u[rows, :] * s).astype(BF)

    return pl.pallas_call(
        body, name="sgu_fwd", grid=(T // tm,),
        in_specs=[_row_spec(tm, 2 * SGU_W), _vec_spec(SGU_W), _vec_spec(SGU_W),
                  pl.BlockSpec((N_GROUPS, CHUNK, CHUNK), lambda i: (0, 0, 0)),
                  pl.BlockSpec((CHUNK, SGU_W), lambda i: (0, 0))],
        out_specs=_row_spec(tm, SGU_W), out_shape=jax.ShapeDtypeStruct((T, SGU_W), BF),
        compiler_params=_params("parallel"),
    )(z, g_sgu, b_sgu, ws, bias_plane)


def _sgu_bwd(dy, z, g_sgu, b_sgu, ws, ws_t, bias_plane, *, tm=512):
    T = z.shape[0]
    tm = min(tm, T)
    n_steps = T // tm

    def body(dy_ref, z_ref, g_ref, b_ref, ws_ref, wst_ref, bp_ref, dz_ref, dws_ref, dbs_ref, dg_ref, db_ref, dbp_ref):
        step = pl.program_id(0)

        @pl.when(step == 0)
        def _():
            dws_ref[...] = jnp.zeros_like(dws_ref)
            dg_ref[...] = jnp.zeros_like(dg_ref)
            db_ref[...] = jnp.zeros_like(db_ref)
            dbp_ref[...] = jnp.zeros_like(dbp_ref)

        g = g_ref[...]
        zt = z_ref[...]
        u, xhat, rstd, vn = _sgu_norm(zt, g, b_ref[...])
        vn_bf = vn.astype(BF)
        first_half = lax.broadcasted_iota(jnp.int32, (CHUNK, LANES), 1) < HEAD_DIM
        dyv = dy_ref[...]
        dg_acc = jnp.zeros((1, SGU_W), F32)
        db_acc = jnp.zeros((1, SGU_W), F32)
        for c in range(tm // CHUNK):
            rows = slice(c * CHUNK, (c + 1) * CHUNK)
            v_c = vn_bf[rows, :]
            s = _sgu_mix(ws_ref, v_c, first_half) + bp_ref[...]
            dy_c = dyv[rows, :]
            du = dy_c * s
            dsv = dy_c * u[rows, :]
            dbp_ref[...] += dsv
            ds_bf = dsv.astype(BF)
            zero = jnp.zeros((CHUNK, LANES), BF)
            for p in range(N_GROUPS // 2):
                dsp = ds_bf[:, p * LANES:(p + 1) * LANES]
                vp = v_c[:, p * LANES:(p + 1) * LANES]
                dws_ref[2 * p] += _dot_nt(jnp.where(first_half, dsp, zero), vp)
                dws_ref[2 * p + 1] += _dot_nt(jnp.where(first_half, zero, dsp), vp)
            dvn = _sgu_mix(wst_ref, ds_bf, first_half)
            xh = xhat[rows, :]
            dxh = dvn * g
            dvv = rstd[rows, :] * (dxh - jnp.mean(dxh, axis=-1, keepdims=True)
                                   - xh * jnp.mean(dxh * xh, axis=-1, keepdims=True))
            dg_acc += jnp.sum(dvn * xh, axis=0, keepdims=True)
            db_acc += jnp.sum(dvn, axis=0, keepdims=True)
            dgz = jnp.concatenate([du, dvv], axis=1)
            dz_ref[rows, :] = (dgz * _gelu_grad(zt[rows, :])).astype(BF)
        dg_ref[...] += dg_acc
        db_ref[...] += db_acc

        @pl.when(step == n_steps - 1)
        def _():
            r = lax.broadcasted_iota(jnp.int32, (CHUNK, CHUNK), 0)
            cidx = lax.broadcasted_iota(jnp.int32, (CHUNK, CHUNK), 1)
            causal = (cidx <= r).astype(F32)
            for gi in range(N_GROUPS):
                dws_ref[gi] = dws_ref[gi] * causal
            lane = lax.broadcasted_iota(jnp.int32, (CHUNK, LANES), 1)
            out = jnp.zeros((CHUNK, LANES), F32)
            dbp = dbp_ref[...]
            for gi in range(N_GROUPS):
                col = jnp.sum(dbp[:, gi * HEAD_DIM:(gi + 1) * HEAD_DIM], axis=1, keepdims=True)
                out = jnp.where(lane == gi, col, out)
            dbs_ref[...] = out

    w_spec = pl.BlockSpec((N_GROUPS, CHUNK, CHUNK), lambda i: (0, 0, 0))
    plane = pl.BlockSpec((CHUNK, SGU_W), lambda i: (0, 0))
    return pl.pallas_call(
        body, name="sgu_bwd", grid=(n_steps,),
        in_specs=[_row_spec(tm, SGU_W), _row_spec(tm, 2 * SGU_W), _vec_spec(SGU_W), _vec_spec(SGU_W), w_spec, w_spec, plane],
        out_specs=[_row_spec(tm, 2 * SGU_W), w_spec, pl.BlockSpec((CHUNK, LANES), lambda i: (0, 0)),
                   _vec_spec(SGU_W), _vec_spec(SGU_W)],
        out_shape=[jax.ShapeDtypeStruct((T, 2 * SGU_W), BF), jax.ShapeDtypeStruct((N_GROUPS, CHUNK, CHUNK), F32),
                   jax.ShapeDtypeStruct((CHUNK, LANES), F32), jax.ShapeDtypeStruct((1, SGU_W), F32),
                   jax.ShapeDtypeStruct((1, SGU_W), F32)],
        scratch_shapes=[pltpu.VMEM((CHUNK, SGU_W), F32)],
        compiler_params=_params("arbitrary"),
    )(dy, z, g_sgu, b_sgu, ws, ws_t, bias_plane)


def _tri(n, upper):
    r = lax.broadcasted_iota(jnp.int32, (n, n), 0)
    c = lax.broadcasted_iota(jnp.int32, (n, n), 1)
    return ((c >= r) if upper else (c <= r)).astype(BF)


def _scan_dot(tri, x):
    hi, mid, lo = _split3(x)
    return (_dot(tri, hi.astype(BF)) + _dot(tri, mid.astype(BF))) + _dot(tri, lo.astype(BF))


def _with_lanes(base, lane, start, cols):
    out = base
    for k, col in enumerate(cols):
        out = jnp.where(lane == start + k, col, out)
    return out


def _attn_prep(qkv, fl, b_forget, *, tp=256):
    T = qkv.shape[0]
    tp = min(tp, T)

    def body(qkv_ref, fl_ref, bf_ref, qf_ref, kl_ref, vl_ref, carry_ref):
        @pl.when(pl.program_id(0) == 0)
        def _():
            carry_ref[...] = jnp.zeros_like(carry_ref)

        x = fl_ref[...] + bf_ref[...]
        logf = jnp.minimum(x, 0.0) - jnp.log(1.0 + jnp.exp(-jnp.abs(x)))
        cum = _scan_dot(_tri(tp, upper=False), logf) + carry_ref[...]
        carry_ref[...] = cum[tp - 1:tp, :]
        lane = lax.broadcasted_iota(jnp.int32, (tp, HEAD_DIM), 1)
        ones3 = jnp.where(lane < 3, 1.0, 0.0)
        qkvv = qkv_ref[...]
        for h in range(N_HEADS):
            c3 = _split3(cum[:, h:h + 1])
            ext_q = _with_lanes(jnp.where((lane >= 3) & (lane < 6), 1.0, 0.0), lane, 0, c3)
            ext_k = _with_lanes(jnp.where((lane < 3) | ((lane >= 6) & (lane < 9)), 1.0, 0.0), lane, 3, [-c for c in c3])
            qh = qkvv[:, h * HEAD_DIM:(h + 1) * HEAD_DIM].astype(F32) * Q_SCALE
            kh = qkvv[:, ATTN_W + h * HEAD_DIM:ATTN_W + (h + 1) * HEAD_DIM].astype(F32)
            vh = qkvv[:, 2 * ATTN_W + h * HEAD_DIM:2 * ATTN_W + (h + 1) * HEAD_DIM].astype(F32)
            qf_ref[h] = jnp.concatenate([qh, ext_q], axis=1).astype(BF)
            kl_ref[h] = jnp.concatenate([kh, ext_k], axis=1).astype(BF)
            vl_ref[h] = jnp.concatenate([vh, ones3], axis=1).astype(BF)

    head_spec = pl.BlockSpec((N_HEADS, tp, LANES), lambda i: (0, i, 0))
    return pl.pallas_call(
        body, name="attn_prep", grid=(T // tp,),
        in_specs=[_row_spec(tp, 3 * ATTN_W), _row_spec(tp, LANES), _vec_spec(LANES)],
        out_specs=[head_spec] * 3, out_shape=[jax.ShapeDtypeStruct((N_HEADS, T, LANES), BF)] * 3,
        scratch_shapes=[pltpu.VMEM((1, LANES), F32)], compiler_params=_params("arbitrary"),
    )(qkv, fl, b_forget)


def _pair_block(t):
    return pl.BlockSpec((2, t, LANES), lambda p, i: (p, i, 0))


def _pair_full(T):
    return pl.BlockSpec((2, T, LANES), lambda p, i: (p, 0, 0))


def _packed_block(t):
    return pl.BlockSpec((t, LANES), lambda p, i: (i, p))


def _attn_fwd(qf, kl, vl, *, tq=256):
    T = qf.shape[1]
    tq = min(tq, T)

    def body(qf_ref, kl_ref, vl_ref, o_ref, of_ref, ql_ref):
        i = pl.program_id(1)
        r = lax.broadcasted_iota(jnp.int32, (tq, tq), 0)
        c = lax.broadcasted_iota(jnp.int32, (tq, tq), 1)
        causal = c <= r
        lane = lax.broadcasted_iota(jnp.int32, (tq, LANES), 1)
        outs = []
        for hh in range(2):
            q = qf_ref[hh]

            def update(s, vj, m, acc):
                m_new = jnp.maximum(m, jnp.max(s, axis=1, keepdims=True))
                p = jnp.exp(s - m_new)
                return m_new, jnp.exp(m - m_new) * acc + _dot(p.astype(BF), vj)

            def step(j, carry, q=q, hh=hh):
                rows = pl.ds(pl.multiple_of(j * tq, tq), tq)
                return update(_dot_nt(q, kl_ref[hh, rows, :]), vl_ref[hh, rows, :], *carry)

            m, acc = lax.fori_loop(0, i, step, (jnp.full((tq, 1), NEG, F32), jnp.zeros((tq, LANES), F32)))
            rows = pl.ds(pl.multiple_of(i * tq, tq), tq)
            s = jnp.where(causal, _dot_nt(q, kl_ref[hh, rows, :]), NEG)
            m, acc = update(s, vl_ref[hh, rows, :], m, acc)
            l = acc[:, HEAD_DIM:HEAD_DIM + 1]
            outs.append(acc[:, :HEAD_DIM] / l)
            lse3 = _split3(-(m + jnp.log(l)))
            ql_ref[hh] = _with_lanes(q.astype(F32), lane, HEAD_DIM + 6, lse3).astype(BF)
        o = jnp.concatenate(outs, axis=1)
        o_ref[...] = o.astype(BF)
        of_ref[...] = o

    return pl.pallas_call(
        body, name="attn_fwd", grid=(N_HEADS // 2, T // tq),
        in_specs=[_pair_block(tq), _pair_full(T), _pair_full(T)],
        out_specs=[_packed_block(tq), _packed_block(tq), _pair_block(tq)],
        out_shape=[jax.ShapeDtypeStruct((T, ATTN_W), BF), jax.ShapeDtypeStruct((T, ATTN_W), F32),
                   jax.ShapeDtypeStruct((N_HEADS, T, LANES), BF)],
        compiler_params=_params("parallel", "arbitrary"),
    )(qf, kl, vl)


def _attn_bwd_prep(dya, of, *, tr=256):
    T = dya.shape[0]
    tr = min(tr, T)

    def body(d_ref, o_ref, do_ref):
        lane = lax.broadcasted_iota(jnp.int32, (tr, HEAD_DIM), 1)
        dv, ov = d_ref[...], o_ref[...]
        for h in range(N_HEADS):
            d = dv[:, h * HEAD_DIM:(h + 1) * HEAD_DIM]
            delta = jnp.sum(d * ov[:, h * HEAD_DIM:(h + 1) * HEAD_DIM], axis=1, keepdims=True)
            ext = _with_lanes(jnp.zeros((tr, HEAD_DIM), F32), lane, 0, _split3(-delta))
            do_ref[h] = jnp.concatenate([d, ext], axis=1).astype(BF)

    return pl.pallas_call(
        body, name="attn_bwd_prep", grid=(T // tr,),
        in_specs=[_row_spec(tr, ATTN_W), _row_spec(tr, ATTN_W)],
        out_specs=pl.BlockSpec((N_HEADS, tr, LANES), lambda i: (0, i, 0)),
        out_shape=jax.ShapeDtypeStruct((N_HEADS, T, LANES), BF), compiler_params=_params("parallel"),
    )(dya, of)


def _attn_bwd_dq(ql, do, kl, vl, *, tq=256):
    T = ql.shape[1]
    tq = min(tq, T)

    def body(ql_ref, do_ref, kl_ref, vl_ref, dq_ref, ext_ref):
        i = pl.program_id(1)
        r = lax.broadcasted_iota(jnp.int32, (tq, tq), 0)
        c = lax.broadcasted_iota(jnp.int32, (tq, tq), 1)
        causal = c <= r
        dqs, exts = [], []
        for hh in range(2):
            q, d = ql_ref[hh], do_ref[hh]

            def block(rows, masked, q=q, d=d, hh=hh):
                kj = kl_ref[hh, rows, :]
                p = jnp.exp(_dot_nt(q, kj))
                if masked:
                    p = jnp.where(causal, p, 0.0)
                ds = p * _dot_nt(d, vl_ref[hh, rows, :])
                return _dot(ds.astype(BF), kj)

            def step(j, acc, block=block):
                return acc + block(pl.ds(pl.multiple_of(j * tq, tq), tq), False)

            acc = lax.fori_loop(0, i, step, jnp.zeros((tq, LANES), F32))
            acc = acc + block(pl.ds(pl.multiple_of(i * tq, tq), tq), True)
            dqs.append(acc[:, :HEAD_DIM] * Q_SCALE)
            exts.append(acc[:, HEAD_DIM:])
        dq_ref[...] = jnp.concatenate(dqs, axis=1).astype(BF)
        ext_ref[...] = jnp.concatenate(exts, axis=1)

    return pl.pallas_call(
        body, name="attn_bwd_dq", grid=(N_HEADS // 2, T // tq),
        in_specs=[_pair_block(tq), _pair_block(tq), _pair_full(T), _pair_full(T)],
        out_specs=[_packed_block(tq), _packed_block(tq)],
        out_shape=[jax.ShapeDtypeStruct((T, ATTN_W), BF), jax.ShapeDtypeStruct((T, ATTN_W), F32)],
        compiler_params=_params("parallel", "arbitrary"),
    )(ql, do, kl, vl)


def _attn_bwd_dkv(kl, vl, ql, do, *, tk=256):
    T = ql.shape[1]
    tk = min(tk, T)
    nq = T // tk

    def body(kl_ref, vl_ref, ql_ref, do_ref, dk_ref, dv_ref, ext_ref):
        j = pl.program_id(1)
        r = lax.broadcasted_iota(jnp.int32, (tk, tk), 0)
        c = lax.broadcasted_iota(jnp.int32, (tk, tk), 1)
        causal_t = r <= c
        dks, dvs, exts = [], [], []
        for hh in range(2):
            k, v = kl_ref[hh], vl_ref[hh]

            def block(rows, masked, carry, k=k, v=v, hh=hh):
                dk, dv = carry
                qi, di = ql_ref[hh, rows, :], do_ref[hh, rows, :]
                p_t = jnp.exp(_dot_nt(k, qi))
                if masked:
                    p_t = jnp.where(causal_t, p_t, 0.0)
                ds_t = p_t * _dot_nt(v, di)
                return dk + _dot(ds_t.astype(BF), qi), dv + _dot(p_t.astype(BF), di)

            def step(i, carry, block=block):
                return block(pl.ds(pl.multiple_of(i * tk, tk), tk), False, carry)

            zero = jnp.zeros((tk, LANES), F32)
            carry = block(pl.ds(pl.multiple_of(j * tk, tk), tk), True, (zero, zero))
            dk, dv = lax.fori_loop(j + 1, nq, step, carry)
            dks.append(dk[:, :HEAD_DIM])
            dvs.append(dv[:, :HEAD_DIM])
            exts.append(dk[:, HEAD_DIM:])
        dk_ref[...] = jnp.concatenate(dks, axis=1).astype(BF)
        dv_ref[...] = jnp.concatenate(dvs, axis=1).astype(BF)
        ext_ref[...] = jnp.concatenate(exts, axis=1)

    return pl.pallas_call(
        body, name="attn_bwd_dkv", grid=(N_HEADS // 2, nq),
        in_specs=[_pair_block(tk), _pair_block(tk), _pair_full(T), _pair_full(T)],
        out_specs=[_packed_block(tk)] * 3,
        out_shape=[jax.ShapeDtypeStruct((T, ATTN_W), BF), jax.ShapeDtypeStruct((T, ATTN_W), BF),
                   jax.ShapeDtypeStruct((T, ATTN_W), F32)],
        compiler_params=_params("parallel", "arbitrary"),
    )(kl, vl, ql, do)


def _forget_bwd(ext_q, ext_k, fl, b_forget, *, tp=256):
    T = fl.shape[0]
    tp = min(tp, T)
    n = T // tp

    def body(eq_ref, ek_ref, fl_ref, bf_ref, dfl_ref, dbf_ref, carry_ref):
        @pl.when(pl.program_id(0) == 0)
        def _():
            carry_ref[...] = jnp.zeros_like(carry_ref)
            dbf_ref[...] = jnp.zeros_like(dbf_ref)

        lane = lax.broadcasted_iota(jnp.int32, (tp, LANES), 1)
        eq, ek = eq_ref[...], ek_ref[...]
        cols = [eq[:, h * HEAD_DIM:h * HEAD_DIM + 1] - ek[:, h * HEAD_DIM + 3:h * HEAD_DIM + 4] for h in range(N_HEADS)]
        dcum = _with_lanes(jnp.zeros((tp, LANES), F32), lane, 0, cols)
        suffix = _scan_dot(_tri(tp, upper=True), dcum) + carry_ref[...]
        carry_ref[...] = suffix[0:1, :]
        x = fl_ref[...] + bf_ref[...]
        dfl = jnp.where(lane < N_HEADS, suffix / (1.0 + jnp.exp(x)), 0.0)
        dfl_ref[...] = dfl.astype(BF)
        dbf_ref[...] += jnp.sum(dfl, axis=0, keepdims=True)

    rev = lambda w: pl.BlockSpec((tp, w), lambda i: (n - 1 - i, 0))
    return pl.pallas_call(
        body, name="forget_bwd", grid=(n,),
        in_specs=[rev(ATTN_W), rev(ATTN_W), rev(LANES), _vec_spec(LANES)],
        out_specs=[rev(LANES), _vec_spec(LANES)],
        out_shape=[jax.ShapeDtypeStruct((T, LANES), BF), jax.ShapeDtypeStruct((1, LANES), F32)],
        scratch_shapes=[pltpu.VMEM((1, LANES), F32)], compiler_params=_params("arbitrary"),
    )(ext_q, ext_k, fl, b_forget)


def _adamw(w, g, m, v, *, name, tr=256):
    rows, cols = w.shape
    tr = tr if rows % tr == 0 else rows

    def body(w_ref, g_ref, m_ref, v_ref, d_ref, nm_ref, nv_ref):
        gv = g_ref[...]
        nm = ADAM_B1 * m_ref[...] + (1.0 - ADAM_B1) * gv
        nv = ADAM_B2 * v_ref[...] + (1.0 - ADAM_B2) * (gv * gv)
        m_hat = nm / (1.0 - ADAM_B1 ** ADAM_STEP)
        v_hat = nv / (1.0 - ADAM_B2 ** ADAM_STEP)
        d_ref[...] = -ADAM_LR * (m_hat / (jnp.sqrt(v_hat) + ADAM_EPS) + ADAM_WD * w_ref[...])
        nm_ref[...] = nm
        nv_ref[...] = nv

    spec = pl.BlockSpec((tr, cols), lambda i: (i, 0))
    return pl.pallas_call(
        body, name=name, grid=(rows // tr,), in_specs=[spec] * 4, out_specs=[spec] * 3,
        out_shape=[jax.ShapeDtypeStruct((rows, cols), F32)] * 3, compiler_params=_params("parallel"),
    )(w, g, m, v)


HBM = pl.BlockSpec(memory_space=pltpu.HBM)


def _place():
    x, y, c = lax.axis_index("x"), lax.axis_index("y"), lax.axis_index("c")
    others = [(1 - x, y), (x, 1 - y), (1 - x, 1 - y)]
    return x, y, c, others


def _chip(xy):
    return 2 * xy[0] + xy[1]


def _gather_weights(wp):
    R = wp.shape[0]
    Rh = R // 2

    def body(w_ref, g_ref, send_sems, recv_sems, local_sem):
        x, y, c, others = _place()
        sibling = (x, y, 1 - c)
        mine_rows = pl.ds(pl.multiple_of(c * Rh, 16), Rh)
        sibling_rows = pl.ds(pl.multiple_of((1 - c) * Rh, 16), Rh)

        def copy(k, src, dst, to):
            return pltpu.make_async_remote_copy(src_ref=src, dst_ref=dst, send_sem=send_sems.at[k], recv_sem=recv_sems.at[k],
                                                device_id=to, device_id_type=MESH)

        own = pltpu.make_async_copy(w_ref, g_ref.at[_chip((x, y))], local_sem)
        own.start()
        first = [copy(j, w_ref.at[mine_rows, :], g_ref.at[_chip((x, y)), mine_rows, :], (*o, c)) for j, o in enumerate(others)]
        for cp in first:
            cp.start()
        passed = []
        for j, o in enumerate(others):
            landed = g_ref.at[_chip(o), mine_rows, :]
            copy(j, landed, landed, (*o, c)).wait_recv()
            passed.append(copy(3 + j, landed, landed, sibling))
            passed[-1].start()
        for j, o in enumerate(others):
            landed = g_ref.at[_chip(o), sibling_rows, :]
            copy(3 + j, landed, landed, sibling).wait_recv()
        for cp in first + passed:
            cp.wait_send()
        own.wait()

    return pl.pallas_call(
        body, name="gather_weights", in_specs=[HBM], out_specs=HBM,
        out_shape=jax.ShapeDtypeStruct((N_CHIPS, R, LANES), wp.dtype),
        scratch_shapes=[pltpu.SemaphoreType.DMA((6,)), pltpu.SemaphoreType.DMA((6,)), pltpu.SemaphoreType.DMA],
    )(wp)


def _exchange_halves(gf, sf):
    Rh = gf.shape[1] // 2
    Rsh = sf.shape[0] // 2

    def body(g_ref, s_ref, rg_ref, rs_ref, send_sems, recv_sems):
        x, y, c, _ = _place()
        sibling = (x, y, 1 - c)
        big = pltpu.make_async_remote_copy(
            src_ref=g_ref.at[:, pl.ds(pl.multiple_of((1 - c) * Rh, 8), Rh), :], dst_ref=rg_ref,
            send_sem=send_sems.at[0], recv_sem=recv_sems.at[0], device_id=sibling, device_id_type=MESH)
        small = pltpu.make_async_remote_copy(
            src_ref=s_ref.at[pl.ds(pl.multiple_of((1 - c) * Rsh, 8), Rsh), :], dst_ref=rs_ref,
            send_sem=send_sems.at[1], recv_sem=recv_sems.at[1], device_id=sibling, device_id_type=MESH)
        big.start()
        small.start()
        big.wait()
        small.wait()

    return pl.pallas_call(
        body, name="exchange_halves", in_specs=[HBM, HBM], out_specs=[HBM, HBM],
        out_shape=[jax.ShapeDtypeStruct((N_CHIPS, Rh, LANES), F32), jax.ShapeDtypeStruct((Rsh, LANES), F32)],
        scratch_shapes=[pltpu.SemaphoreType.DMA((2,)), pltpu.SemaphoreType.DMA((2,))],
    )(gf, sf)


def _scatter_to_owners(cab, csa):
    Rh = cab.shape[1]
    Rsh = csa.shape[0]

    def body(b_ref, s_ref, rb_ref, rs_ref, send_sems, recv_sems, local_sems):
        x, y, c, others = _place()
        me = _chip((x, y))
        own_b = pltpu.make_async_copy(b_ref.at[me], rb_ref.at[me], local_sems.at[0])
        own_s = pltpu.make_async_copy(s_ref, rs_ref.at[me], local_sems.at[1])
        own_b.start()
        own_s.start()
        sends = []
        for j, o in enumerate(others):
            sends.append(pltpu.make_async_remote_copy(
                src_ref=b_ref.at[_chip(o)], dst_ref=rb_ref.at[me], send_sem=send_sems.at[j], recv_sem=recv_sems.at[j],
                device_id=(*o, c), device_id_type=MESH))
            sends.append(pltpu.make_async_remote_copy(
                src_ref=s_ref, dst_ref=rs_ref.at[me], send_sem=send_sems.at[3 + j], recv_sem=recv_sems.at[3 + j],
                device_id=(*o, c), device_id_type=MESH))
        for cp in sends:
            cp.start()
        for j, o in enumerate(others):
            pltpu.make_async_remote_copy(
                src_ref=b_ref.at[me], dst_ref=rb_ref.at[_chip(o)], send_sem=send_sems.at[j], recv_sem=recv_sems.at[j],
                device_id=(*o, c), device_id_type=MESH).wait_recv()
            pltpu.make_async_remote_copy(
                src_ref=s_ref, dst_ref=rs_ref.at[_chip(o)], send_sem=send_sems.at[3 + j], recv_sem=recv_sems.at[3 + j],
                device_id=(*o, c), device_id_type=MESH).wait_recv()
        for cp in sends:
            cp.wait_send()
        own_b.wait()
        own_s.wait()

    return pl.pallas_call(
        body, name="scatter_to_owners", in_specs=[HBM, HBM], out_specs=[HBM, HBM],
        out_shape=[jax.ShapeDtypeStruct((N_CHIPS, Rh, LANES), BF), jax.ShapeDtypeStruct((N_CHIPS, Rsh, LANES), F32)],
        scratch_shapes=[pltpu.SemaphoreType.DMA((6,)), pltpu.SemaphoreType.DMA((6,)), pltpu.SemaphoreType.DMA((2,))],
    )(cab, csa)


def _join_halves(tb, ts):
    Rh = tb.shape[0]
    Rsh = ts.shape[0]

    def body(b_ref, s_ref, gb_ref, gs_ref, send_sems, recv_sems, local_sems):
        x, y, c, _ = _place()
        sibling = (x, y, 1 - c)
        rows_b = pl.ds(pl.multiple_of(c * Rh, 8), Rh)
        rows_s = pl.ds(pl.multiple_of(c * Rsh, 8), Rsh)
        own_b = pltpu.make_async_copy(b_ref, gb_ref.at[rows_b, :], local_sems.at[0])
        own_s = pltpu.make_async_copy(s_ref, gs_ref.at[rows_s, :], local_sems.at[1])
        own_b.start()
        own_s.start()
        big = pltpu.make_async_remote_copy(src_ref=b_ref, dst_ref=gb_ref.at[rows_b, :], send_sem=send_sems.at[0],
                                           recv_sem=recv_sems.at[0], device_id=sibling, device_id_type=MESH)
        small = pltpu.make_async_remote_copy(src_ref=s_ref, dst_ref=gs_ref.at[rows_s, :], send_sem=send_sems.at[1],
                                             recv_sem=recv_sems.at[1], device_id=sibling, device_id_type=MESH)
        big.start()
        small.start()
        big.wait()
        small.wait()
        own_b.wait()
        own_s.wait()

    return pl.pallas_call(
        body, name="join_halves", in_specs=[HBM, HBM], out_specs=[HBM, HBM],
        out_shape=[jax.ShapeDtypeStruct((2 * Rh, LANES), F32), jax.ShapeDtypeStruct((2 * Rsh, LANES), F32)],
        scratch_shapes=[pltpu.SemaphoreType.DMA((2,)), pltpu.SemaphoreType.DMA((2,)), pltpu.SemaphoreType.DMA((2,))],
    )(tb, ts)


def _row_tile(rows, cap=1152, mult=16):
    return max(t for t in range(mult, min(rows, cap) + 1, mult) if rows % t == 0)


def _add_sibling(gf, rg, sf, rs, core):
    Rh = rg.shape[1]
    Rsh = rs.shape[0]
    tr = _row_tile(Rh)
    nb = Rh // tr

    def big_body(core_ref, g_ref, r_ref, o_ref, ob_ref):
        s = g_ref[...] + r_ref[...]
        o_ref[...] = s
        ob_ref[...] = s.astype(BF)

    spec = pl.BlockSpec((N_CHIPS, tr, LANES), lambda i, core_ref: (0, i, 0))
    ca, cab = pl.pallas_call(
        big_body, name="add_sibling",
        grid_spec=pltpu.PrefetchScalarGridSpec(
            num_scalar_prefetch=1, grid=(nb,),
            in_specs=[pl.BlockSpec((N_CHIPS, tr, LANES), lambda i, core_ref: (0, core_ref[0] * nb + i, 0)), spec],
            out_specs=[spec, spec]),
        out_shape=[jax.ShapeDtypeStruct((N_CHIPS, Rh, LANES), F32), jax.ShapeDtypeStruct((N_CHIPS, Rh, LANES), BF)],
        compiler_params=_params("parallel"),
    )(core, gf, rg)

    def small_body(core_ref, s_ref, r_ref, o_ref):
        o_ref[...] = s_ref[...] + r_ref[...]

    sspec = pl.BlockSpec((Rsh, LANES), lambda i, core_ref: (0, 0))
    csa = pl.pallas_call(
        small_body, name="add_sibling_small",
        grid_spec=pltpu.PrefetchScalarGridSpec(
            num_scalar_prefetch=1, grid=(1,),
            in_specs=[pl.BlockSpec((Rsh, LANES), lambda i, core_ref: (core_ref[0], 0)), sspec], out_specs=sspec),
        out_shape=jax.ShapeDtypeStruct((Rsh, LANES), F32), compiler_params=_params("arbitrary"),
    )(core, sf, rs)
    return ca, cab, csa


def _add_chips(ca, rb, rsb, chip):
    Rh = ca.shape[1]
    tr = _row_tile(Rh)
    Rsh = rsb.shape[1]

    def big_body(chip_ref, own_ref, r_ref, o_ref):
        acc = own_ref[0]
        for k in range(N_CHIPS):
            acc = acc + jnp.where(chip_ref[0] == k, 0.0, r_ref[k].astype(F32))
        o_ref[...] = acc

    tb = pl.pallas_call(
        big_body, name="add_chips",
        grid_spec=pltpu.PrefetchScalarGridSpec(
            num_scalar_prefetch=1, grid=(Rh // tr,),
            in_specs=[pl.BlockSpec((1, tr, LANES), lambda i, chip_ref: (chip_ref[0], i, 0)),
                      pl.BlockSpec((N_CHIPS, tr, LANES), lambda i, chip_ref: (0, i, 0))],
            out_specs=pl.BlockSpec((tr, LANES), lambda i, chip_ref: (i, 0))),
        out_shape=jax.ShapeDtypeStruct((Rh, LANES), F32), compiler_params=_params("parallel"),
    )(chip, ca, rb)

    def small_body(r_ref, o_ref):
        o_ref[...] = ((r_ref[0] + r_ref[1]) + r_ref[2]) + r_ref[3]

    ts = pl.pallas_call(
        small_body, name="add_chips_small", out_shape=jax.ShapeDtypeStruct((Rsh, LANES), F32),
    )(rsb)
    return tb, ts


SHARDED = (("w_in", (D_MODEL, 4616), 1), ("w_branch_sgu", (SGU_W, D_MODEL), 1), ("w_branch_attn", (ATTN_W, D_MODEL), 1),
           ("w_out", (D_MODEL, D_MODEL), 0), ("w_up", (D_MODEL, D_FF), 1), ("w_down", (D_FF, D_MODEL), 0))
SMALL = (("g_mix_pre", (1, D_MODEL)), ("b_forget", (1, N_HEADS)), ("g_sgu", (1, SGU_W)), ("b_sgu", (1, SGU_W)),
         ("w_spatial", (N_GROUPS * CHUNK, CHUNK)), ("b_spatial", (N_GROUPS, CHUNK)), ("g_mix_post", (1, D_MODEL)),
         ("g_ffn_pre", (1, D_MODEL)), ("g_ffn_post", (1, D_MODEL)))
PACK_ALIGN = 256


def _shard_shape(shape, axis):
    return tuple(s // N_CHIPS if a == axis else s for a, s in enumerate(shape))


def _padded_rows(rows):
    return -(-rows // PACK_ALIGN) * PACK_ALIGN


def _pack_rows(parts, axis):
    rows = sum(p.shape[axis] for p in parts)
    pad = _padded_rows(rows) - rows
    if pad:
        shape = list(parts[0].shape)
        shape[axis] = pad
        parts = list(parts) + [jnp.zeros(shape, parts[0].dtype)]
    return jnp.concatenate(parts, axis=axis)


def _pack_shards(shards, dtype):
    return _pack_rows([shards[name].astype(dtype).reshape(-1, LANES) for name, _, _ in SHARDED], 0)


def _unpack_shards(packed):
    out, row = {}, 0
    for name, shape, axis in SHARDED:
        sshape = _shard_shape(shape, axis)
        n = sshape[0] * sshape[1] // LANES
        if packed.ndim == 2:
            out[name] = packed[row:row + n].reshape(sshape)
        else:
            parts = packed[:, row:row + n].reshape((N_CHIPS,) + sshape)
            out[name] = parts.reshape(shape) if axis == 0 else parts.transpose(1, 0, 2).reshape(shape)
        row += n
    return out


def _pack_full_grads(grads):
    parts = []
    for name, shape, axis in SHARDED:
        g = grads[name]
        sshape = _shard_shape(shape, axis)
        if axis == 0:
            g = g.reshape((N_CHIPS,) + sshape)
        else:
            g = g.reshape(shape[0], N_CHIPS, sshape[1]).transpose(1, 0, 2)
        parts.append(g.reshape(N_CHIPS, -1, LANES))
    return _pack_rows(parts, 1)


def _small_rows(shape):
    return -(-(shape[0] * shape[1]) // (8 * LANES)) * 8


def _pack_small(values):
    parts = []
    for name, shape in SMALL:
        flat = values[name].reshape(-1)
        n = _small_rows(shape)
        parts.append(jnp.pad(flat, (0, n * LANES - flat.shape[0])).reshape(n, LANES))
    return _pack_rows(parts, 0)


def _unpack_small(packed):
    out, row = {}, 0
    for name, shape in SMALL:
        n = _small_rows(shape)
        out[name] = packed[row:row + n].reshape(-1)[:shape[0] * shape[1]].reshape(shape)
        row += n
    return out


IN_Z, IN_Q, IN_K, IN_V, IN_F, IN_G, IN_END = 0, 1024, 1536, 2048, 2560, 2568, 4616


def _local_step(x, target, w, small):
    w_in = w["w_in"]
    w_z, w_qkv, w_g = w_in[:, IN_Z:IN_Q], w_in[:, IN_Q:IN_F], w_in[:, IN_G:IN_END]
    w_q, w_k, w_v = w_in[:, IN_Q:IN_K], w_in[:, IN_K:IN_V], w_in[:, IN_V:IN_F]
    w_f = jnp.pad(w_in[:, IN_F:IN_G], ((0, 0), (0, LANES - N_HEADS)))
    b_forget = jnp.pad(small["b_forget"], ((0, 0), (0, LANES - N_HEADS)))
    causal = jnp.tril(jnp.ones((CHUNK, CHUNK), bool))
    ws = jnp.where(causal[None], small["w_spatial"].reshape(N_GROUPS, CHUNK, CHUNK), 0.0).astype(BF)
    ws_t = ws.transpose(0, 2, 1)
    bias_plane = jnp.repeat(small["b_spatial"].T, HEAD_DIM, axis=1)

    xn = _rms_fwd(x, small["g_mix_pre"])
    z = _matmul([(xn, w_z)], nt=False, out_dtypes=[F32], tm=512, tn=512, name="proj_z")
    qkv = _matmul([(xn, w_qkv)], nt=False, out_dtypes=[BF], tm=512, tn=512, name="proj_qkv")
    gl = _matmul([(xn, w_g)], nt=False, out_dtypes=[F32], tm=512, tn=512, name="proj_gate")
    fl = _matmul([(xn, w_f)], nt=False, out_dtypes=[F32], tm=512, tn=LANES, name="proj_forget")
    ysgu = _sgu_fwd(z, small["g_sgu"], small["b_sgu"], ws, bias_plane)
    qf, kl, vl = _attn_prep(qkv, fl, b_forget)
    yattn, yattn_f, ql = _attn_fwd(qf, kl, vl)
    a, b, merged = _branch_merge(ysgu, yattn, w["w_branch_sgu"], w["w_branch_attn"], gl)
    o = _matmul([(merged, w["w_out"])], nt=False, out_dtypes=[F32], tm=512, tn=512, name="proj_out")
    h1, xn2 = _mixer_out_fwd(o, x, small["g_mix_post"], small["g_ffn_pre"])

    def relu2(acc):
        r = jnp.maximum(acc, 0.0)
        return r * r, r

    hid, relu = _matmul([(xn2, w["w_up"])], nt=False, out_dtypes=[BF, BF], tm=512, tn=512, name="ffn_up", epilogue=relu2)
    dn = _matmul([(hid, w["w_down"])], nt=False, out_dtypes=[F32], tm=512, tn=512, name="ffn_down")
    sq, dy, ddn, dg_ffn_post = _loss_head(dn, h1, target, small["g_ffn_post"])

    dup = _matmul([(ddn, w["w_down"])], nt=True, out_dtypes=[BF], tm=512, tn=512, name="ffn_down_bwd",
                  epilogue=lambda acc, r: (acc * (2.0 * r.astype(F32)),), extras=[relu])
    dw_down = _matmul_tn(hid, ddn, name="dw_down")
    dxn2 = _matmul([(dup, w["w_up"])], nt=True, out_dtypes=[F32], tm=512, tn=512, name="ffn_up_bwd")
    dw_up = _matmul_tn(xn2, dup, name="dw_up")
    dh1, do, dg_ffn_pre, dg_mix_post = _mixer_out_bwd(h1, dxn2, dy, o, small["g_ffn_pre"], small["g_mix_post"])

    dmerged = _matmul([(do, w["w_out"])], nt=True, out_dtypes=[F32], tm=512, tn=512, name="proj_out_bwd")
    dw_out = _matmul_tn(merged, do, name="dw_out")
    da, db, dgla, dglb = _gate_bwd(dmerged, a, b, gl)
    dysgu = _matmul([(da, w["w_branch_sgu"])], nt=True, out_dtypes=[F32], tm=512, tn=512, name="branch_sgu_bwd")
    dyattn = _matmul([(db, w["w_branch_attn"])], nt=True, out_dtypes=[F32], tm=512, tn=512, name="branch_attn_bwd")
    dw_bs = _matmul_tn(ysgu, da, name="dw_branch_sgu")
    dw_ba = _matmul_tn(yattn, db, name="dw_branch_attn")
    dz, dws, dbs, dg_sgu, db_sgu = _sgu_bwd(dysgu, z, small["g_sgu"], small["b_sgu"], ws, ws_t, bias_plane)
    dout = _attn_bwd_prep(dyattn, yattn_f)
    dq, ext_q = _attn_bwd_dq(ql, dout, kl, vl)
    dk, dv, ext_k = _attn_bwd_dkv(kl, vl, ql, dout)
    dfl, dbf = _forget_bwd(ext_q, ext_k, fl, b_forget)
    dxn = _matmul([(dz, w_z), (dq, w_q), (dk, w_k), (dv, w_v), (dgla, w_g[:, :D_MODEL]), (dglb, w_g[:, D_MODEL:]), (dfl, w_f)],
                  nt=True, out_dtypes=[F32], tm=512, tn=512, name="proj_in_bwd")
    dw_in = jnp.concatenate(
        [_matmul_tn(xn, dz, name="dw_in_z"), _matmul_tn(xn, dq, name="dw_in_q"), _matmul_tn(xn, dk, name="dw_in_k"),
         _matmul_tn(xn, dv, name="dw_in_v"), _matmul_tn(xn, dfl, name="dw_in_f")[:, :N_HEADS],
         _matmul_tn(xn, dgla, name="dw_in_ga"), _matmul_tn(xn, dglb, name="dw_in_gb")], axis=1)
    dx, dg_mix_pre = _input_norm_bwd(x, dxn, dh1, small["g_mix_pre"])

    grads = {"w_in": dw_in, "w_branch_sgu": dw_bs, "w_branch_attn": dw_ba, "w_out": dw_out, "w_up": dw_up, "w_down": dw_down}
    small_grads = {"g_mix_pre": dg_mix_pre, "b_forget": dbf[:, :N_HEADS], "g_sgu": dg_sgu, "b_sgu": db_sgu,
                   "w_spatial": dws.reshape(N_GROUPS * CHUNK, CHUNK), "b_spatial": dbs[:, :N_GROUPS].T,
                   "g_mix_post": dg_mix_post, "g_ffn_pre": dg_ffn_pre, "g_ffn_post": dg_ffn_post}
    return sq, dx, grads, small_grads


NAMES = ("g_mix_pre", "w_in", "b_forget", "g_sgu", "b_sgu", "w_spatial", "b_spatial", "w_branch_sgu", "w_branch_attn",
         "w_out", "g_mix_post", "g_ffn_pre", "w_up", "w_down", "g_ffn_post")


def kernel(x, g_mix_pre, w_in, b_forget, g_sgu, b_sgu, w_spatial, b_spatial, w_branch_sgu, w_branch_attn, w_out, g_mix_post, g_ffn_pre, w_up, w_down, g_ffn_post, loss_target, m_g_mix_pre, m_w_in, m_b_forget, m_g_sgu, m_b_sgu, m_w_spatial, m_b_spatial, m_w_branch_sgu, m_w_branch_attn, m_w_out, m_g_mix_post, m_g_ffn_pre, m_w_up, m_w_down, m_g_ffn_post, v_g_mix_pre, v_w_in, v_b_forget, v_g_sgu, v_b_sgu, v_w_spatial, v_b_spatial, v_w_branch_sgu, v_w_branch_attn, v_w_out, v_g_mix_post, v_g_ffn_pre, v_w_up, v_w_down, v_g_ffn_post):
    weights = dict(zip(NAMES, (g_mix_pre, w_in, b_forget, g_sgu, b_sgu, w_spatial, b_spatial, w_branch_sgu, w_branch_attn,
                               w_out, g_mix_post, g_ffn_pre, w_up, w_down, g_ffn_post), strict=True))
    first = dict(zip(NAMES, (m_g_mix_pre, m_w_in, m_b_forget, m_g_sgu, m_b_sgu, m_w_spatial, m_b_spatial, m_w_branch_sgu,
                             m_w_branch_attn, m_w_out, m_g_mix_post, m_g_ffn_pre, m_w_up, m_w_down, m_g_ffn_post), strict=True))
    second = dict(zip(NAMES, (v_g_mix_pre, v_w_in, v_b_forget, v_g_sgu, v_b_sgu, v_w_spatial, v_b_spatial, v_w_branch_sgu,
                              v_w_branch_attn, v_w_out, v_g_mix_post, v_g_ffn_pre, v_w_up, v_w_down, v_g_ffn_post), strict=True))
    shard_shapes = {name: _shard_shape(shape, axis) for name, shape, axis in SHARDED}
    small_shapes = dict(SMALL)
    view = lambda name, a: a.reshape(shard_shapes.get(name) or small_shapes[name])

    core = lax.axis_index("c").astype(jnp.int32).reshape(1)
    chip = (2 * lax.axis_index("x") + lax.axis_index("y")).astype(jnp.int32).reshape(1)

    shards = {name: view(name, weights[name]) for name, _, _ in SHARDED}
    full = _unpack_shards(_gather_weights(_pack_shards(shards, BF)))
    small = {name: view(name, weights[name]) for name, _ in SMALL}

    sq, dx, grads, small_grads = _local_step(x[0], loss_target[0], full, small)
    loss = lax.psum(0.5 * jnp.sum(sq) / D_MODEL, ("x", "y", "c"))

    gf = _pack_full_grads(grads)
    sf = _pack_small(small_grads)
    rg, rs = _exchange_halves(gf, sf)
    ca, cab, csa = _add_sibling(gf, rg, sf, rs, core)
    rb, rsb = _scatter_to_owners(cab, csa)
    tb, ts = _add_chips(ca, rb, rsb, chip)
    g_packed, s_packed = _join_halves(tb, ts)
    grad = {**_unpack_shards(g_packed), **_unpack_small(s_packed)}

    delta, new_m, new_v = {}, {}, {}
    for name in NAMES:
        delta[name], new_m[name], new_v[name] = _adamw(
            view(name, weights[name]), grad[name], view(name, first[name]), view(name, second[name]), name="adamw_" + name)

    like = lambda d: [d[name].reshape(weights[name].shape) for name in NAMES]
    return (loss, dx[None], *like(grad), *like(delta), *like(new_m), *like(new_v))
```

```python
import functools

import jax
import jax.numpy as jnp
from jax import lax
from jax.experimental import pallas as pl
from jax.experimental.pallas import tpu as pltpu

F32 = jnp.float32
BF = jnp.bfloat16
MESH = pl.DeviceIdType.MESH

D_MODEL = 1024
N_HEADS = 8
HEAD_DIM = 64
ATTN_W = N_HEADS * HEAD_DIM
SGU_W = 512
N_GROUPS = 8
CHUNK = 128
D_FF = 4096
EPS = 1e-6
Q_SCALE = HEAD_DIM ** -0.5
N_CHIPS = 4
LANES = 128

ADAM_LR = 0.001
ADAM_B1 = 0.9
ADAM_B2 = 0.999
ADAM_EPS = 1e-08
ADAM_WD = 0.01
ADAM_STEP = 10

VMEM_LIMIT = 48 * 1024 * 1024
NEG = -1e30

LANE_ROWSUM = HEAD_DIM
LANE_COLSUM = HEAD_DIM + 3


def _params(*sem):
    return pltpu.CompilerParams(dimension_semantics=sem, vmem_limit_bytes=VMEM_LIMIT)


def _dot(a, b):
    return jnp.dot(a, b, preferred_element_type=F32)


def _dot_nt(a, b):
    return lax.dot_general(a, b, (((1,), (1,)), ((), ())), preferred_element_type=F32)


def _dot_tn(a, b):
    return lax.dot_general(a, b, (((0,), (0,)), ((), ())), preferred_element_type=F32)


def _split3(c):
    hi = c.astype(BF).astype(F32)
    r = c - hi
    mid = r.astype(BF).astype(F32)
    lo = (r - mid).astype(BF).astype(F32)
    return hi, mid, lo


def _gelu(x):
    k = 0.7978845608028654
    return 0.5 * x * (1.0 + jnp.tanh(k * (x + 0.044715 * (x * x * x))))


def _gelu_grad(x):
    k = 0.7978845608028654
    x2 = x * x
    t = jnp.tanh(k * (x + 0.044715 * (x2 * x)))
    return 0.5 * (1.0 + t) + 0.5 * x * (1.0 - t * t) * (k * (1.0 + 3.0 * 0.044715 * x2))


def _rms_bwd(a, g, dy):
    r = lax.rsqrt(jnp.mean(a * a, axis=-1, keepdims=True) + EPS)
    n = a * r
    dn = dy * g
    da = r * (dn - n * jnp.mean(dn * n, axis=-1, keepdims=True))
    return da, dy * n


def _matmul(pairs, *, nt, out_dtypes, tm, tn, name, epilogue=None, extras=()):
    n_pairs = len(pairs)
    n_extra = len(extras)
    M = pairs[0][0].shape[0]
    N = pairs[0][1].shape[0] if nt else pairs[0][1].shape[1]
    tm, tn = min(tm, M), min(tn, N)
    assert M % tm == 0 and N % tn == 0

    def body(*refs):
        acc = None
        for p in range(n_pairs):
            a_ref, b_ref = refs[2 * p], refs[2 * p + 1]
            d = _dot_nt(a_ref[...], b_ref[...]) if nt else _dot(a_ref[...], b_ref[...])
            acc = d if acc is None else acc + d
        e_refs = refs[2 * n_pairs:2 * n_pairs + n_extra]
        o_refs = refs[2 * n_pairs + n_extra:]
        outs = (acc,) if epilogue is None else epilogue(acc, *[e[...] for e in e_refs])
        for o_ref, o in zip(o_refs, outs, strict=True):
            o_ref[...] = o.astype(o_ref.dtype)

    in_specs, args = [], []
    for a, b in pairs:
        K = a.shape[1]
        in_specs.append(pl.BlockSpec((tm, K), lambda i, j: (i, 0)))
        in_specs.append(pl.BlockSpec((tn, K), lambda i, j: (j, 0)) if nt else pl.BlockSpec((K, tn), lambda i, j: (0, j)))
        args += [a, b]
    for e in extras:
        in_specs.append(pl.BlockSpec((tm, tn), lambda i, j: (i, j)))
        args.append(e)
    outs = pl.pallas_call(
        body, name=name, grid=(M // tm, N // tn), in_specs=in_specs,
        out_specs=[pl.BlockSpec((tm, tn), lambda i, j: (i, j)) for _ in out_dtypes],
        out_shape=[jax.ShapeDtypeStruct((M, N), dt) for dt in out_dtypes],
        compiler_params=_params("parallel", "parallel"),
    )(*args)
    return outs if len(outs) > 1 else outs[0]


def _matmul_tn(a, b, *, name, tm=1024, tn=1024, tk=512):
    T, K1 = a.shape
    N = b.shape[1]
    tm, tn, tk = min(tm, K1), min(tn, N), min(tk, T)
    assert K1 % tm == 0 and N % tn == 0 and T % tk == 0

    def body(a_ref, b_ref, o_ref):
        @pl.when(pl.program_id(2) == 0)
        def _():
            o_ref[...] = jnp.zeros_like(o_ref)

        o_ref[...] += _dot_tn(a_ref[...], b_ref[...])

    return pl.pallas_call(
        body, name=name, grid=(K1 // tm, N // tn, T // tk),
        in_specs=[pl.BlockSpec((tk, tm), lambda i, j, k: (k, i)), pl.BlockSpec((tk, tn), lambda i, j, k: (k, j))],
        out_specs=pl.BlockSpec((tm, tn), lambda i, j, k: (i, j)),
        out_shape=jax.ShapeDtypeStruct((K1, N), F32),
        compiler_params=_params("parallel", "parallel", "arbitrary"),
    )(a, b)


def _branch_merge(ysgu, yattn, w_bs, w_ba, gl, *, tm=512, tn=512):
    T = ysgu.shape[0]
    tm = min(tm, T)
    nj = D_MODEL // tn

    def body(ys_ref, ya_ref, wbs_ref, wba_ref, gla_ref, glb_ref, a_ref, b_ref, m_ref):
        a = _dot(ys_ref[...], wbs_ref[...])
        b = _dot(ya_ref[...], wba_ref[...])
        a_ref[...] = a
        b_ref[...] = b
        m_ref[...] = (jax.nn.sigmoid(gla_ref[...]) * a + jax.nn.sigmoid(glb_ref[...]) * b).astype(BF)

    return pl.pallas_call(
        body, name="branch_merge", grid=(T // tm, nj),
        in_specs=[
            pl.BlockSpec((tm, SGU_W), lambda i, j: (i, 0)),
            pl.BlockSpec((tm, ATTN_W), lambda i, j: (i, 0)),
            pl.BlockSpec((SGU_W, tn), lambda i, j: (0, j)),
            pl.BlockSpec((ATTN_W, tn), lambda i, j: (0, j)),
            pl.BlockSpec((tm, tn), lambda i, j: (i, j)),
            pl.BlockSpec((tm, tn), lambda i, j: (i, j + nj)),
        ],
        out_specs=[pl.BlockSpec((tm, tn), lambda i, j: (i, j))] * 3,
        out_shape=[jax.ShapeDtypeStruct((T, D_MODEL), F32), jax.ShapeDtypeStruct((T, D_MODEL), F32),
                   jax.ShapeDtypeStruct((T, D_MODEL), BF)],
        compiler_params=_params("parallel", "parallel"),
    )(ysgu, yattn, w_bs, w_ba, gl, gl)


def _row_spec(tr, width):
    return pl.BlockSpec((tr, width), lambda i: (i, 0))


def _vec_spec(width):
    return pl.BlockSpec((1, width), lambda i: (0, 0))


def _rms_fwd(x, g, *, tr=256):
    T = x.shape[0]
    tr = min(tr, T)

    def body(x_ref, g_ref, o_ref):
        xv = x_ref[...]
        r = lax.rsqrt(jnp.mean(xv * xv, axis=-1, keepdims=True) + EPS)
        o_ref[...] = ((xv * r) * g_ref[...]).astype(BF)

    return pl.pallas_call(
        body, name="rms_fwd", grid=(T // tr,),
        in_specs=[_row_spec(tr, D_MODEL), _vec_spec(D_MODEL)], out_specs=_row_spec(tr, D_MODEL),
        out_shape=jax.ShapeDtypeStruct((T, D_MODEL), BF), compiler_params=_params("parallel"),
    )(x, g)


def _mixer_out_fwd(o, x, g_post, g_pre, *, tr=256):
    T = x.shape[0]
    tr = min(tr, T)

    def body(o_ref, x_ref, gpost_ref, gpre_ref, h1_ref, xn2_ref):
        ov = o_ref[...]
        r = lax.rsqrt(jnp.mean(ov * ov, axis=-1, keepdims=True) + EPS)
        h1 = x_ref[...] + (ov * r) * gpost_ref[...]
        h1_ref[...] = h1
        r2 = lax.rsqrt(jnp.mean(h1 * h1, axis=-1, keepdims=True) + EPS)
        xn2_ref[...] = ((h1 * r2) * gpre_ref[...]).astype(BF)

    return pl.pallas_call(
        body, name="mixer_out_fwd", grid=(T // tr,),
        in_specs=[_row_spec(tr, D_MODEL), _row_spec(tr, D_MODEL), _vec_spec(D_MODEL), _vec_spec(D_MODEL)],
        out_specs=[_row_spec(tr, D_MODEL), _row_spec(tr, D_MODEL)],
        out_shape=[jax.ShapeDtypeStruct((T, D_MODEL), F32), jax.ShapeDtypeStruct((T, D_MODEL), BF)],
        compiler_params=_params("parallel"),
    )(o, x, g_post, g_pre)


def _loss_head(dn, h1, target, g_post, *, tr=256):
    T = dn.shape[0]
    tr = min(tr, T)

    def body(dn_ref, h1_ref, t_ref, g_ref, sq_ref, dy_ref, ddn_ref, dg_ref):
        @pl.when(pl.program_id(0) == 0)
        def _():
            sq_ref[...] = jnp.zeros_like(sq_ref)
            dg_ref[...] = jnp.zeros_like(dg_ref)

        a = dn_ref[...]
        g = g_ref[...]
        r = lax.rsqrt(jnp.mean(a * a, axis=-1, keepdims=True) + EPS)
        err = h1_ref[...] + (a * r) * g - t_ref[...]
        sq_ref[...] += jnp.sum(err * err, axis=0, keepdims=True)
        dy = err * (1.0 / D_MODEL)
        dy_ref[...] = dy
        da, dgp = _rms_bwd(a, g, dy)
        ddn_ref[...] = da.astype(BF)
        dg_ref[...] += jnp.sum(dgp, axis=0, keepdims=True)

    return pl.pallas_call(
        body, name="loss_head", grid=(T // tr,),
        in_specs=[_row_spec(tr, D_MODEL)] * 3 + [_vec_spec(D_MODEL)],
        out_specs=[_vec_spec(D_MODEL), _row_spec(tr, D_MODEL), _row_spec(tr, D_MODEL), _vec_spec(D_MODEL)],
        out_shape=[jax.ShapeDtypeStruct((1, D_MODEL), F32), jax.ShapeDtypeStruct((T, D_MODEL), F32),
                   jax.ShapeDtypeStruct((T, D_MODEL), BF), jax.ShapeDtypeStruct((1, D_MODEL), F32)],
        compiler_params=_params("arbitrary"),
    )(dn, h1, target, g_post)


def _mixer_out_bwd(h1, dxn2, dy, o, g_pre, g_post, *, tr=256):
    T = h1.shape[0]
    tr = min(tr, T)

    def body(h1_ref, dxn2_ref, dy_ref, o_ref, gpre_ref, gpost_ref, dh1_ref, do_ref, dgpre_ref, dgpost_ref):
        @pl.when(pl.program_id(0) == 0)
        def _():
            dgpre_ref[...] = jnp.zeros_like(dgpre_ref)
            dgpost_ref[...] = jnp.zeros_like(dgpost_ref)

        da, dgp = _rms_bwd(h1_ref[...], gpre_ref[...], dxn2_ref[...])
        dh1 = dy_ref[...] + da
        dh1_ref[...] = dh1
        dgpre_ref[...] += jnp.sum(dgp, axis=0, keepdims=True)
        do, dgp2 = _rms_bwd(o_ref[...], gpost_ref[...], dh1)
        do_ref[...] = do.astype(BF)
        dgpost_ref[...] += jnp.sum(dgp2, axis=0, keepdims=True)

    return pl.pallas_call(
        body, name="mixer_out_bwd", grid=(T // tr,),
        in_specs=[_row_spec(tr, D_MODEL)] * 4 + [_vec_spec(D_MODEL)] * 2,
        out_specs=[_row_spec(tr, D_MODEL), _row_spec(tr, D_MODEL), _vec_spec(D_MODEL), _vec_spec(D_MODEL)],
        out_shape=[jax.ShapeDtypeStruct((T, D_MODEL), F32), jax.ShapeDtypeStruct((T, D_MODEL), BF),
                   jax.ShapeDtypeStruct((1, D_MODEL), F32), jax.ShapeDtypeStruct((1, D_MODEL), F32)],
        compiler_params=_params("arbitrary"),
    )(h1, dxn2, dy, o, g_pre, g_post)


def _input_norm_bwd(x, dxn, dh1, g, *, tr=256):
    T = x.shape[0]
    tr = min(tr, T)

    def body(x_ref, dxn_ref, dh1_ref, g_ref, dx_ref, dg_ref):
        @pl.when(pl.program_id(0) == 0)
        def _():
            dg_ref[...] = jnp.zeros_like(dg_ref)

        da, dgp = _rms_bwd(x_ref[...], g_ref[...], dxn_ref[...])
        dx_ref[...] = dh1_ref[...] + da
        dg_ref[...] += jnp.sum(dgp, axis=0, keepdims=True)

    return pl.pallas_call(
        body, name="input_norm_bwd", grid=(T // tr,),
        in_specs=[_row_spec(tr, D_MODEL)] * 3 + [_vec_spec(D_MODEL)],
        out_specs=[_row_spec(tr, D_MODEL), _vec_spec(D_MODEL)],
        out_shape=[jax.ShapeDtypeStruct((T, D_MODEL), F32), jax.ShapeDtypeStruct((1, D_MODEL), F32)],
        compiler_params=_params("arbitrary"),
    )(x, dxn, dh1, g)


def _gate_bwd(dm, a, b, gl, *, tr=256):
    T = dm.shape[0]
    tr = min(tr, T)

    def body(dm_ref, a_ref, b_ref, gla_ref, glb_ref, da_ref, db_ref, dgla_ref, dglb_ref):
        dmv = dm_ref[...]
        ga = jax.nn.sigmoid(gla_ref[...])
        gb = jax.nn.sigmoid(glb_ref[...])
        da_ref[...] = (dmv * ga).astype(BF)
        db_ref[...] = (dmv * gb).astype(BF)
        dgla_ref[...] = (dmv * a_ref[...] * (ga * (1.0 - ga))).astype(BF)
        dglb_ref[...] = (dmv * b_ref[...] * (gb * (1.0 - gb))).astype(BF)

    spec = _row_spec(tr, D_MODEL)
    spec_b = pl.BlockSpec((tr, D_MODEL), lambda i: (i, 1))
    da, db, dgla, dglb = pl.pallas_call(
        body, name="gate_bwd", grid=(T // tr,),
        in_specs=[spec, spec, spec, spec, spec_b], out_specs=[spec] * 4,
        out_shape=[jax.ShapeDtypeStruct((T, D_MODEL), BF)] * 4, compiler_params=_params("parallel"),
    )(dm, a, b, gl, gl)
    return da, db, dgla, dglb


def _sgu_norm(z_tile, g, b):
    gz = _gelu(z_tile)
    u, vv = gz[:, :SGU_W], gz[:, SGU_W:]
    xc = vv - jnp.mean(vv, axis=-1, keepdims=True)
    rstd = lax.rsqrt(jnp.mean(xc * xc, axis=-1, keepdims=True) + EPS)
    xhat = xc * rstd
    return u, xhat, rstd, xhat * g + b


def _sgu_mix(w_ref, v_bf, first_half):
    parts = []
    for p in range(N_GROUPS // 2):
        vp = v_bf[:, p * LANES:(p + 1) * LANES]
        parts.append(jnp.where(first_half, _dot(w_ref[2 * p], vp), _dot(w_ref[2 * p + 1], vp)))
    return jnp.concatenate(parts, axis=1)


def _sgu_fwd(z, g_sgu, b_sgu, ws, bias_plane, *, tm=512):
    T = z.shape[0]
    tm = min(tm, T)

    def body(z_ref, g_ref, b_ref, ws_ref, bp_ref, y_ref):
        u, _, _, vn = _sgu_norm(z_ref[...], g_ref[...], b_ref[...])
        vn_bf = vn.astype(BF)
        first_half = lax.broadcasted_iota(jnp.int32, (CHUNK, LANES), 1) < HEAD_DIM
        for c in range(tm // CHUNK):
            rows = slice(c * CHUNK, (c + 1) * CHUNK)
            s = _sgu_mix(ws_ref, vn_bf[rows, :], first_half) + bp_ref[...]
            y_ref[rows, :] = (u[rows, :] * s).astype(BF)

    return pl.pallas_call(
        body, name="sgu_fwd", grid=(T // tm,),
        in_specs=[_row_spec(tm, 2 * SGU_W), _vec_spec(SGU_W), _vec_spec(SGU_W),
                  pl.BlockSpec((N_GROUPS, CHUNK, CHUNK), lambda i: (0, 0, 0)),
                  pl.BlockSpec((CHUNK, SGU_W), lambda i: (0, 0))],
        out_specs=_row_spec(tm, SGU_W), out_shape=jax.ShapeDtypeStruct((T, SGU_W), BF),
        compiler_params=_params("parallel"),
    )(z, g_sgu, b_sgu, ws, bias_plane)


def _sgu_bwd(dy, z, g_sgu, b_sgu, ws, ws_t, bias_plane, *, tm=512):
    T = z.shape[0]
    tm = min(tm, T)
    n_steps = T // tm

    def body(dy_ref, z_ref, g_ref, b_ref, ws_ref, wst_ref, bp_ref, dz_ref, dws_ref, dbs_ref, dg_ref, db_ref, dbp_ref):
        step = pl.program_id(0)

        @pl.when(step == 0)
        def _():
            dws_ref[...] = jnp.zeros_like(dws_ref)
            dg_ref[...] = jnp.zeros_like(dg_ref)
            db_ref[...] = jnp.zeros_like(db_ref)
            dbp_ref[...] = jnp.zeros_like(dbp_ref)

        g = g_ref[...]
        zt = z_ref[...]
        u, xhat, rstd, vn = _sgu_norm(zt, g, b_ref[...])
        vn_bf = vn.astype(BF)
        first_half = lax.broadcasted_iota(jnp.int32, (CHUNK, LANES), 1) < HEAD_DIM
        dyv = dy_ref[...]
        dg_acc = jnp.zeros((1, SGU_W), F32)
        db_acc = jnp.zeros((1, SGU_W), F32)
        for c in range(tm // CHUNK):
            rows = slice(c * CHUNK, (c + 1) * CHUNK)
            v_c = vn_bf[rows, :]
            s = _sgu_mix(ws_ref, v_c, first_half) + bp_ref[...]
            dy_c = dyv[rows, :]
            du = dy_c * s
            dsv = dy_c * u[rows, :]
            dbp_ref[...] += dsv
            ds_bf = dsv.astype(BF)
            zero = jnp.zeros((CHUNK, LANES), BF)
            for p in range(N_GROUPS // 2):
                dsp = ds_bf[:, p * LANES:(p + 1) * LANES]
                vp = v_c[:, p * LANES:(p + 1) * LANES]
                dws_ref[2 * p] += _dot_nt(jnp.where(first_half, dsp, zero), vp)
                dws_ref[2 * p + 1] += _dot_nt(jnp.where(first_half, zero, dsp), vp)
            dvn = _sgu_mix(wst_ref, ds_bf, first_half)
            xh = xhat[rows, :]
            dxh = dvn * g
            dvv = rstd[rows, :] * (dxh - jnp.mean(dxh, axis=-1, keepdims=True)
                                   - xh * jnp.mean(dxh * xh, axis=-1, keepdims=True))
            dg_acc += jnp.sum(dvn * xh, axis=0, keepdims=True)
            db_acc += jnp.sum(dvn, axis=0, keepdims=True)
            dgz = jnp.concatenate([du, dvv], axis=1)
            dz_ref[rows, :] = (dgz * _gelu_grad(zt[rows, :])).astype(BF)
        dg_ref[...] += dg_acc
        db_ref[...] += db_acc

        @pl.when(step == n_steps - 1)
        def _():
            r = lax.broadcasted_iota(jnp.int32, (CHUNK, CHUNK), 0)
            cidx = lax.broadcasted_iota(jnp.int32, (CHUNK, CHUNK), 1)
            causal = (cidx <= r).astype(F32)
            for gi in range(N_GROUPS):
                dws_ref[gi] = dws_ref[gi] * causal
            lane = lax.broadcasted_iota(jnp.int32, (CHUNK, LANES), 1)
            out = jnp.zeros((CHUNK, LANES), F32)
            dbp = dbp_ref[...]
            for gi in range(N_GROUPS):
                col = jnp.sum(dbp[:, gi * HEAD_DIM:(gi + 1) * HEAD_DIM], axis=1, keepdims=True)
                out = jnp.where(lane == gi, col, out)
            dbs_ref[...] = out

    w_spec = pl.BlockSpec((N_GROUPS, CHUNK, CHUNK), lambda i: (0, 0, 0))
    plane = pl.BlockSpec((CHUNK, SGU_W), lambda i: (0, 0))
    return pl.pallas_call(
        body, name="sgu_bwd", grid=(n_steps,),
        in_specs=[_row_spec(tm, SGU_W), _row_spec(tm, 2 * SGU_W), _vec_spec(SGU_W), _vec_spec(SGU_W), w_spec, w_spec, plane],
        out_specs=[_row_spec(tm, 2 * SGU_W), w_spec, pl.BlockSpec((CHUNK, LANES), lambda i: (0, 0)),
                   _vec_spec(SGU_W), _vec_spec(SGU_W)],
        out_shape=[jax.ShapeDtypeStruct((T, 2 * SGU_W), BF), jax.ShapeDtypeStruct((N_GROUPS, CHUNK, CHUNK), F32),
                   jax.ShapeDtypeStruct((CHUNK, LANES), F32), jax.ShapeDtypeStruct((1, SGU_W), F32),
                   jax.ShapeDtypeStruct((1, SGU_W), F32)],
        scratch_shapes=[pltpu.VMEM((CHUNK, SGU_W), F32)],
        compiler_params=_params("arbitrary"),
    )(dy, z, g_sgu, b_sgu, ws, ws_t, bias_plane)


def _tri(n, upper):
    r = lax.broadcasted_iota(jnp.int32, (n, n), 0)
    c = lax.broadcasted_iota(jnp.int32, (n, n), 1)
    return ((c >= r) if upper else (c <= r)).astype(BF)


def _scan_dot(tri, x):
    hi, mid, lo = _split3(x)
    return (_dot(tri, hi.astype(BF)) + _dot(tri, mid.astype(BF))) + _dot(tri, lo.astype(BF))


def _with_lanes(base, lane, start, cols):
    out = base
    for k, col in enumerate(cols):
        out = jnp.where(lane == start + k, col, out)
    return out


ATTN_TILE = 512
SKIP_BELOW = -110.0
NORM_SLACK = 1.001


def _attn_prep(qkv, fl, b_forget, *, tp=ATTN_TILE):
    T = qkv.shape[0]
    tp = min(tp, T)

    def body(qkv_ref, fl_ref, bf_ref, qf_ref, kl_ref, vl_ref, st_ref, carry_ref):
        @pl.when(pl.program_id(0) == 0)
        def _():
            carry_ref[...] = jnp.zeros_like(carry_ref)

        x = fl_ref[...] + bf_ref[...]
        logf = jnp.minimum(x, 0.0) - jnp.log(1.0 + jnp.exp(-jnp.abs(x)))
        cum = _scan_dot(_tri(tp, upper=False), logf) + carry_ref[...]
        carry_ref[...] = cum[tp - 1:tp, :]
        lane = lax.broadcasted_iota(jnp.int32, (tp, HEAD_DIM), 1)
        ones3 = jnp.where(lane < 3, 1.0, 0.0)
        qkvv = qkv_ref[...]
        st_row = lax.broadcasted_iota(jnp.int32, (N_HEADS, LANES), 0)
        st_lane = lax.broadcasted_iota(jnp.int32, (N_HEADS, LANES), 1)
        stats = jnp.zeros((N_HEADS, LANES), F32)
        for h in range(N_HEADS):
            ch = cum[:, h:h + 1]
            c3 = _split3(ch)
            ext_q = _with_lanes(jnp.where((lane >= 3) & (lane < 6), 1.0, 0.0), lane, 0, c3)
            ext_k = _with_lanes(jnp.where((lane < 3) | ((lane >= 6) & (lane < 9)), 1.0, 0.0), lane, 3, [-c for c in c3])
            qh = qkvv[:, h * HEAD_DIM:(h + 1) * HEAD_DIM].astype(F32) * Q_SCALE
            kh = qkvv[:, ATTN_W + h * HEAD_DIM:ATTN_W + (h + 1) * HEAD_DIM].astype(F32)
            vh = qkvv[:, 2 * ATTN_W + h * HEAD_DIM:2 * ATTN_W + (h + 1) * HEAD_DIM].astype(F32)
            qf_ref[h] = jnp.concatenate([qh, ext_q], axis=1).astype(BF)
            kl_ref[h] = jnp.concatenate([kh, ext_k], axis=1).astype(BF)
            vl_ref[h] = jnp.concatenate([vh, ones3], axis=1).astype(BF)
            qn = jnp.sqrt(jnp.max(jnp.sum(qh * qh, axis=1, keepdims=True), axis=0, keepdims=True))
            kn = jnp.sqrt(jnp.max(jnp.sum(kh * kh, axis=1, keepdims=True), axis=0, keepdims=True))
            tile_stats = (qn, kn, jnp.max(ch, axis=0, keepdims=True), jnp.min(ch, axis=0, keepdims=True))
            for k, val in enumerate(tile_stats):
                stats = jnp.where((st_row == h) & (st_lane == k), val, stats)
        st_ref[0] = stats

    head_spec = pl.BlockSpec((N_HEADS, tp, LANES), lambda i: (0, i, 0))
    return pl.pallas_call(
        body, name="attn_prep", grid=(T // tp,),
        in_specs=[_row_spec(tp, 3 * ATTN_W), _row_spec(tp, LANES), _vec_spec(LANES)],
        out_specs=[head_spec] * 3 + [pl.BlockSpec((1, N_HEADS, LANES), lambda i: (i, 0, 0))],
        out_shape=[jax.ShapeDtypeStruct((N_HEADS, T, LANES), BF)] * 3 + [jax.ShapeDtypeStruct((T // tp, N_HEADS, LANES), F32)],
        scratch_shapes=[pltpu.VMEM((1, LANES), F32)], compiler_params=_params("arbitrary"),
    )(qkv, fl, b_forget)


def _attn_ranges(stats):
    qn, kn, cmax, cmin = (stats[:, :, k].T for k in range(4))
    n = qn.shape[1]
    reach = NORM_SLACK * qn * (jnp.max(kn, axis=1, keepdims=True) + kn) + cmax
    i = jnp.arange(n)[None, :, None]
    j = jnp.arange(n)[None, None, :]
    need = ((reach[:, :, None] - cmin[:, None, :] >= SKIP_BELOW) | (i == j)) & (j <= i)
    first = jnp.min(jnp.where(need, j, n), axis=2).reshape(N_HEADS // 2, 2, n).min(axis=1)
    last = jnp.max(jnp.where(need, i, -1), axis=1).reshape(N_HEADS // 2, 2, n).max(axis=1)
    return first.reshape(-1).astype(F32), last.reshape(-1).astype(F32)


def _pair_block(t):
    return pl.BlockSpec((2, t, LANES), lambda p, i, *_: (p, i, 0))


def _pair_full(T):
    return pl.BlockSpec((2, T, LANES), lambda p, i, *_: (p, 0, 0))


def _packed_block(t):
    return pl.BlockSpec((t, LANES), lambda p, i, *_: (i, p))


def _causal(t, keys_in_rows=False):
    r = lax.broadcasted_iota(jnp.int32, (t, t), 0)
    c = lax.broadcasted_iota(jnp.int32, (t, t), 1)
    return (r <= c) if keys_in_rows else (c <= r)


def _tile_rows(j, t):
    return pl.ds(pl.multiple_of(j * t, t), t)


def _attn_call(body, name, first_or_last, operands, in_specs, out_specs, out_shape, scratch_shapes, n_tiles):
    return pl.pallas_call(
        body, name=name,
        grid_spec=pltpu.PrefetchScalarGridSpec(
            num_scalar_prefetch=1, grid=(N_HEADS // 2, n_tiles), in_specs=in_specs, out_specs=out_specs,
            scratch_shapes=scratch_shapes),
        out_shape=out_shape, compiler_params=_params("parallel", "arbitrary"),
    )(first_or_last, *operands)


def _attn_fwd(qf, kl, vl, first, *, tq=ATTN_TILE):
    T = qf.shape[1]
    tq = min(tq, T)
    n = T // tq

    def body(first_ref, qf_ref, kl_ref, vl_ref, o_ref, of_ref, ql_ref, m_ref, acc_ref):
        i = pl.program_id(1)
        m_ref[...] = jnp.full_like(m_ref, NEG)
        acc_ref[...] = jnp.zeros_like(acc_ref)

        def update(hh, s, vj):
            m_old = m_ref[hh]
            m_new = jnp.maximum(m_old, jnp.max(s, axis=1, keepdims=True))
            p = jnp.exp(s - m_new)
            acc_ref[hh] = jnp.exp(m_old - m_new) * acc_ref[hh] + _dot(p.astype(BF), vj)
            m_ref[hh] = m_new

        def step(j, carry):
            rows = _tile_rows(j, tq)
            for hh in range(2):
                update(hh, _dot_nt(qf_ref[hh], kl_ref[hh, rows, :]), vl_ref[hh, rows, :])
            return carry

        lax.fori_loop(first_ref[pl.program_id(0) * n + i].astype(jnp.int32), i, step, 0)
        rows = _tile_rows(i, tq)
        causal = _causal(tq)
        lane = lax.broadcasted_iota(jnp.int32, (tq, LANES), 1)
        outs = []
        for hh in range(2):
            q = qf_ref[hh]
            update(hh, jnp.where(causal, _dot_nt(q, kl_ref[hh, rows, :]), NEG), vl_ref[hh, rows, :])
            acc = acc_ref[hh]
            l = acc[:, HEAD_DIM:HEAD_DIM + 1]
            outs.append(acc[:, :HEAD_DIM] / l)
            lse3 = _split3(-(m_ref[hh] + jnp.log(l)))
            ql_ref[hh] = _with_lanes(q.astype(F32), lane, HEAD_DIM + 6, lse3).astype(BF)
        o = jnp.concatenate(outs, axis=1)
        o_ref[...] = o.astype(BF)
        of_ref[...] = o

    return _attn_call(
        body, "attn_fwd", first, (qf, kl, vl), [_pair_block(tq), _pair_full(T), _pair_full(T)],
        [_packed_block(tq), _packed_block(tq), _pair_block(tq)],
        [jax.ShapeDtypeStruct((T, ATTN_W), BF), jax.ShapeDtypeStruct((T, ATTN_W), F32),
         jax.ShapeDtypeStruct((N_HEADS, T, LANES), BF)],
        [pltpu.VMEM((2, tq, 1), F32), pltpu.VMEM((2, tq, LANES), F32)], n)


def _attn_bwd_prep(dya, of, *, tr=256):
    T = dya.shape[0]
    tr = min(tr, T)

    def body(d_ref, o_ref, do_ref):
        lane = lax.broadcasted_iota(jnp.int32, (tr, HEAD_DIM), 1)
        dv, ov = d_ref[...], o_ref[...]
        for h in range(N_HEADS):
            d = dv[:, h * HEAD_DIM:(h + 1) * HEAD_DIM]
            delta = jnp.sum(d * ov[:, h * HEAD_DIM:(h + 1) * HEAD_DIM], axis=1, keepdims=True)
            ext = _with_lanes(jnp.zeros((tr, HEAD_DIM), F32), lane, 0, _split3(-delta))
            do_ref[h] = jnp.concatenate([d, ext], axis=1).astype(BF)

    return pl.pallas_call(
        body, name="attn_bwd_prep", grid=(T // tr,),
        in_specs=[_row_spec(tr, ATTN_W), _row_spec(tr, ATTN_W)],
        out_specs=pl.BlockSpec((N_HEADS, tr, LANES), lambda i: (0, i, 0)),
        out_shape=jax.ShapeDtypeStruct((N_HEADS, T, LANES), BF), compiler_params=_params("parallel"),
    )(dya, of)


def _attn_bwd_dq(ql, do, kl, vl, first, *, tq=ATTN_TILE):
    T = ql.shape[1]
    tq = min(tq, T)
    n = T // tq

    def body(first_ref, ql_ref, do_ref, kl_ref, vl_ref, dq_ref, ext_ref, acc_ref):
        i = pl.program_id(1)
        acc_ref[...] = jnp.zeros_like(acc_ref)

        def block(hh, rows, mask):
            kj = kl_ref[hh, rows, :]
            p = jnp.exp(_dot_nt(ql_ref[hh], kj))
            if mask is not None:
                p = jnp.where(mask, p, 0.0)
            ds = p * _dot_nt(do_ref[hh], vl_ref[hh, rows, :])
            acc_ref[hh] += _dot(ds.astype(BF), kj)

        def step(j, carry):
            for hh in range(2):
                block(hh, _tile_rows(j, tq), None)
            return carry

        lax.fori_loop(first_ref[pl.program_id(0) * n + i].astype(jnp.int32), i, step, 0)
        causal = _causal(tq)
        for hh in range(2):
            block(hh, _tile_rows(i, tq), causal)
        dq_ref[...] = jnp.concatenate([acc_ref[hh][:, :HEAD_DIM] * Q_SCALE for hh in range(2)], axis=1).astype(BF)
        ext_ref[...] = jnp.concatenate([acc_ref[hh][:, HEAD_DIM:] for hh in range(2)], axis=1)

    return _attn_call(
        body, "attn_bwd_dq", first, (ql, do, kl, vl),
        [_pair_block(tq), _pair_block(tq), _pair_full(T), _pair_full(T)], [_packed_block(tq), _packed_block(tq)],
        [jax.ShapeDtypeStruct((T, ATTN_W), BF), jax.ShapeDtypeStruct((T, ATTN_W), F32)],
        [pltpu.VMEM((2, tq, LANES), F32)], n)


def _attn_bwd_dkv(kl, vl, ql, do, last, *, tk=ATTN_TILE):
    T = ql.shape[1]
    tk = min(tk, T)
    n = T // tk

    def body(last_ref, kl_ref, vl_ref, ql_ref, do_ref, dk_ref, dv_ref, ext_ref, dk_acc, dv_acc):
        j = pl.program_id(1)
        dk_acc[...] = jnp.zeros_like(dk_acc)
        dv_acc[...] = jnp.zeros_like(dv_acc)

        def block(hh, rows, mask):
            qi, di = ql_ref[hh, rows, :], do_ref[hh, rows, :]
            p_t = jnp.exp(_dot_nt(kl_ref[hh], qi))
            if mask is not None:
                p_t = jnp.where(mask, p_t, 0.0)
            ds_t = p_t * _dot_nt(vl_ref[hh], di)
            dk_acc[hh] += _dot(ds_t.astype(BF), qi)
            dv_acc[hh] += _dot(p_t.astype(BF), di)

        causal_t = _causal(tk, keys_in_rows=True)
        for hh in range(2):
            block(hh, _tile_rows(j, tk), causal_t)

        def step(i, carry):
            for hh in range(2):
                block(hh, _tile_rows(i, tk), None)
            return carry

        lax.fori_loop(j + 1, last_ref[pl.program_id(0) * n + j].astype(jnp.int32) + 1, step, 0)
        dk_ref[...] = jnp.concatenate([dk_acc[hh][:, :HEAD_DIM] for hh in range(2)], axis=1).astype(BF)
        dv_ref[...] = jnp.concatenate([dv_acc[hh][:, :HEAD_DIM] for hh in range(2)], axis=1).astype(BF)
        ext_ref[...] = jnp.concatenate([dk_acc[hh][:, HEAD_DIM:] for hh in range(2)], axis=1)

    return _attn_call(
        body, "attn_bwd_dkv", last, (kl, vl, ql, do),
        [_pair_block(tk), _pair_block(tk), _pair_full(T), _pair_full(T)], [_packed_block(tk)] * 3,
        [jax.ShapeDtypeStruct((T, ATTN_W), BF), jax.ShapeDtypeStruct((T, ATTN_W), BF),
         jax.ShapeDtypeStruct((T, ATTN_W), F32)],
        [pltpu.VMEM((2, tk, LANES), F32), pltpu.VMEM((2, tk, LANES), F32)], n)


def _forget_bwd(ext_q, ext_k, fl, b_forget, *, tp=256):
    T = fl.shape[0]
    tp = min(tp, T)
    n = T // tp

    def body(eq_ref, ek_ref, fl_ref, bf_ref, dfl_ref, dbf_ref, carry_ref):
        @pl.when(pl.program_id(0) == 0)
        def _():
            carry_ref[...] = jnp.zeros_like(carry_ref)
            dbf_ref[...] = jnp.zeros_like(dbf_ref)

        lane = lax.broadcasted_iota(jnp.int32, (tp, LANES), 1)
        eq, ek = eq_ref[...], ek_ref[...]
        cols = [eq[:, h * HEAD_DIM:h * HEAD_DIM + 1] - ek[:, h * HEAD_DIM + 3:h * HEAD_DIM + 4] for h in range(N_HEADS)]
        dcum = _with_lanes(jnp.zeros((tp, LANES), F32), lane, 0, cols)
        suffix = _scan_dot(_tri(tp, upper=True), dcum) + carry_ref[...]
        carry_ref[...] = suffix[0:1, :]
        x = fl_ref[...] + bf_ref[...]
        dfl = jnp.where(lane < N_HEADS, suffix / (1.0 + jnp.exp(x)), 0.0)
        dfl_ref[...] = dfl.astype(BF)
        dbf_ref[...] += jnp.sum(dfl, axis=0, keepdims=True)

    rev = lambda w: pl.BlockSpec((tp, w), lambda i: (n - 1 - i, 0))
    return pl.pallas_call(
        body, name="forget_bwd", grid=(n,),
        in_specs=[rev(ATTN_W), rev(ATTN_W), rev(LANES), _vec_spec(LANES)],
        out_specs=[rev(LANES), _vec_spec(LANES)],
        out_shape=[jax.ShapeDtypeStruct((T, LANES), BF), jax.ShapeDtypeStruct((1, LANES), F32)],
        scratch_shapes=[pltpu.VMEM((1, LANES), F32)], compiler_params=_params("arbitrary"),
    )(ext_q, ext_k, fl, b_forget)


def _adamw(w, g, m, v, *, name, tr=256):
    rows, cols = w.shape
    tr = tr if rows % tr == 0 else rows

    def body(w_ref, g_ref, m_ref, v_ref, d_ref, nm_ref, nv_ref):
        gv = g_ref[...]
        nm = ADAM_B1 * m_ref[...] + (1.0 - ADAM_B1) * gv
        nv = ADAM_B2 * v_ref[...] + (1.0 - ADAM_B2) * (gv * gv)
        m_hat = nm / (1.0 - ADAM_B1 ** ADAM_STEP)
        v_hat = nv / (1.0 - ADAM_B2 ** ADAM_STEP)
        d_ref[...] = -ADAM_LR * (m_hat / (jnp.sqrt(v_hat) + ADAM_EPS) + ADAM_WD * w_ref[...])
        nm_ref[...] = nm
        nv_ref[...] = nv

    spec = pl.BlockSpec((tr, cols), lambda i: (i, 0))
    return pl.pallas_call(
        body, name=name, grid=(rows // tr,), in_specs=[spec] * 4, out_specs=[spec] * 3,
        out_shape=[jax.ShapeDtypeStruct((rows, cols), F32)] * 3, compiler_params=_params("parallel"),
    )(w, g, m, v)


HBM = pl.BlockSpec(memory_space=pltpu.HBM)


def _place():
    x, y, c = lax.axis_index("x"), lax.axis_index("y"), lax.axis_index("c")
    others = [(1 - x, y), (x, 1 - y), (1 - x, 1 - y)]
    return x, y, c, others


def _chip(xy):
    return 2 * xy[0] + xy[1]


def _gather_weights(wp):
    R = wp.shape[0]
    Rh = R // 2

    def body(w_ref, g_ref, send_sems, recv_sems, local_sem):
        x, y, c, others = _place()
        sibling = (x, y, 1 - c)
        mine_rows = pl.ds(pl.multiple_of(c * Rh, 16), Rh)
        sibling_rows = pl.ds(pl.multiple_of((1 - c) * Rh, 16), Rh)

        def copy(k, src, dst, to):
            return pltpu.make_async_remote_copy(src_ref=src, dst_ref=dst, send_sem=send_sems.at[k], recv_sem=recv_sems.at[k],
                                                device_id=to, device_id_type=MESH)

        own = pltpu.make_async_copy(w_ref, g_ref.at[_chip((x, y))], local_sem)
        own.start()
        first = [copy(j, w_ref.at[mine_rows, :], g_ref.at[_chip((x, y)), mine_rows, :], (*o, c)) for j, o in enumerate(others)]
        for cp in first:
            cp.start()
        passed = []
        for j, o in enumerate(others):
            landed = g_ref.at[_chip(o), mine_rows, :]
            copy(j, landed, landed, (*o, c)).wait_recv()
            passed.append(copy(3 + j, landed, landed, sibling))
            passed[-1].start()
        for j, o in enumerate(others):
            landed = g_ref.at[_chip(o), sibling_rows, :]
            copy(3 + j, landed, landed, sibling).wait_recv()
        for cp in first + passed:
            cp.wait_send()
        own.wait()

    return pl.pallas_call(
        body, name="gather_weights", in_specs=[HBM], out_specs=HBM,
        out_shape=jax.ShapeDtypeStruct((N_CHIPS, R, LANES), wp.dtype),
        scratch_shapes=[pltpu.SemaphoreType.DMA((6,)), pltpu.SemaphoreType.DMA((6,)), pltpu.SemaphoreType.DMA],
    )(wp)


def _exchange_halves(gf, sf):
    Rh = gf.shape[1] // 2
    Rsh = sf.shape[0] // 2

    def body(g_ref, s_ref, rg_ref, rs_ref, send_sems, recv_sems):
        x, y, c, _ = _place()
        sibling = (x, y, 1 - c)
        big = pltpu.make_async_remote_copy(
            src_ref=g_ref.at[:, pl.ds(pl.multiple_of((1 - c) * Rh, 8), Rh), :], dst_ref=rg_ref,
            send_sem=send_sems.at[0], recv_sem=recv_sems.at[0], device_id=sibling, device_id_type=MESH)
        small = pltpu.make_async_remote_copy(
            src_ref=s_ref.at[pl.ds(pl.multiple_of((1 - c) * Rsh, 8), Rsh), :], dst_ref=rs_ref,
            send_sem=send_sems.at[1], recv_sem=recv_sems.at[1], device_id=sibling, device_id_type=MESH)
        big.start()
        small.start()
        big.wait()
        small.wait()

    return pl.pallas_call(
        body, name="exchange_halves", in_specs=[HBM, HBM], out_specs=[HBM, HBM],
        out_shape=[jax.ShapeDtypeStruct((N_CHIPS, Rh, LANES), F32), jax.ShapeDtypeStruct((Rsh, LANES), F32)],
        scratch_shapes=[pltpu.SemaphoreType.DMA((2,)), pltpu.SemaphoreType.DMA((2,))],
    )(gf, sf)


def _scatter_to_owners(cab, csa):
    Rh = cab.shape[1]
    Rsh = csa.shape[0]

    def body(b_ref, s_ref, rb_ref, rs_ref, send_sems, recv_sems, local_sems):
        x, y, c, others = _place()
        me = _chip((x, y))
        own_b = pltpu.make_async_copy(b_ref.at[me], rb_ref.at[me], local_sems.at[0])
        own_s = pltpu.make_async_copy(s_ref, rs_ref.at[me], local_sems.at[1])
        own_b.start()
        own_s.start()
        sends = []
        for j, o in enumerate(others):
            sends.append(pltpu.make_async_remote_copy(
                src_ref=b_ref.at[_chip(o)], dst_ref=rb_ref.at[me], send_sem=send_sems.at[j], recv_sem=recv_sems.at[j],
                device_id=(*o, c), device_id_type=MESH))
            sends.append(pltpu.make_async_remote_copy(
                src_ref=s_ref, dst_ref=rs_ref.at[me], send_sem=send_sems.at[3 + j], recv_sem=recv_sems.at[3 + j],
                device_id=(*o, c), device_id_type=MESH))
        for cp in sends:
            cp.start()
        for j, o in enumerate(others):
            pltpu.make_async_remote_copy(
                src_ref=b_ref.at[me], dst_ref=rb_ref.at[_chip(o)], send_sem=send_sems.at[j], recv_sem=recv_sems.at[j],
                device_id=(*o, c), device_id_type=MESH).wait_recv()
            pltpu.make_async_remote_copy(
                src_ref=s_ref, dst_ref=rs_ref.at[_chip(o)], send_sem=send_sems.at[3 + j], recv_sem=recv_sems.at[3 + j],
                device_id=(*o, c), device_id_type=MESH).wait_recv()
        for cp in sends:
            cp.wait_send()
        own_b.wait()
        own_s.wait()

    return pl.pallas_call(
        body, name="scatter_to_owners", in_specs=[HBM, HBM], out_specs=[HBM, HBM],
        out_shape=[jax.ShapeDtypeStruct((N_CHIPS, Rh, LANES), BF), jax.ShapeDtypeStruct((N_CHIPS, Rsh, LANES), F32)],
        scratch_shapes=[pltpu.SemaphoreType.DMA((6,)), pltpu.SemaphoreType.DMA((6,)), pltpu.SemaphoreType.DMA((2,))],
    )(cab, csa)


def _join_halves(tb, ts):
    Rh = tb.shape[0]
    Rsh = ts.shape[0]

    def body(b_ref, s_ref, gb_ref, gs_ref, send_sems, recv_sems, local_sems):
        x, y, c, _ = _place()
        sibling = (x, y, 1 - c)
        rows_b = pl.ds(pl.multiple_of(c * Rh, 8), Rh)
        rows_s = pl.ds(pl.multiple_of(c * Rsh, 8), Rsh)
        own_b = pltpu.make_async_copy(b_ref, gb_ref.at[rows_b, :], local_sems.at[0])
        own_s = pltpu.make_async_copy(s_ref, gs_ref.at[rows_s, :], local_sems.at[1])
        own_b.start()
        own_s.start()
        big = pltpu.make_async_remote_copy(src_ref=b_ref, dst_ref=gb_ref.at[rows_b, :], send_sem=send_sems.at[0],
                                           recv_sem=recv_sems.at[0], device_id=sibling, device_id_type=MESH)
        small = pltpu.make_async_remote_copy(src_ref=s_ref, dst_ref=gs_ref.at[rows_s, :], send_sem=send_sems.at[1],
                                             recv_sem=recv_sems.at[1], device_id=sibling, device_id_type=MESH)
        big.start()
        small.start()
        big.wait()
        small.wait()
        own_b.wait()
        own_s.wait()

    return pl.pallas_call(
        body, name="join_halves", in_specs=[HBM, HBM], out_specs=[HBM, HBM],
        out_shape=[jax.ShapeDtypeStruct((2 * Rh, LANES), F32), jax.ShapeDtypeStruct((2 * Rsh, LANES), F32)],
        scratch_shapes=[pltpu.SemaphoreType.DMA((2,)), pltpu.SemaphoreType.DMA((2,)), pltpu.SemaphoreType.DMA((2,))],
    )(tb, ts)


def _row_tile(rows, cap=1152, mult=16):
    return max(t for t in range(mult, min(rows, cap) + 1, mult) if rows % t == 0)


def _add_sibling(gf, rg, sf, rs, core):
    Rh = rg.shape[1]
    Rsh = rs.shape[0]
    tr = _row_tile(Rh)
    nb = Rh // tr

    def big_body(core_ref, g_ref, r_ref, o_ref, ob_ref):
        s = g_ref[...] + r_ref[...]
        o_ref[...] = s
        ob_ref[...] = s.astype(BF)

    spec = pl.BlockSpec((N_CHIPS, tr, LANES), lambda i, core_ref: (0, i, 0))
    ca, cab = pl.pallas_call(
        big_body, name="add_sibling",
        grid_spec=pltpu.PrefetchScalarGridSpec(
            num_scalar_prefetch=1, grid=(nb,),
            in_specs=[pl.BlockSpec((N_CHIPS, tr, LANES), lambda i, core_ref: (0, core_ref[0] * nb + i, 0)), spec],
            out_specs=[spec, spec]),
        out_shape=[jax.ShapeDtypeStruct((N_CHIPS, Rh, LANES), F32), jax.ShapeDtypeStruct((N_CHIPS, Rh, LANES), BF)],
        compiler_params=_params("parallel"),
    )(core, gf, rg)

    def small_body(core_ref, s_ref, r_ref, o_ref):
        o_ref[...] = s_ref[...] + r_ref[...]

    sspec = pl.BlockSpec((Rsh, LANES), lambda i, core_ref: (0, 0))
    csa = pl.pallas_call(
        small_body, name="add_sibling_small",
        grid_spec=pltpu.PrefetchScalarGridSpec(
            num_scalar_prefetch=1, grid=(1,),
            in_specs=[pl.BlockSpec((Rsh, LANES), lambda i, core_ref: (core_ref[0], 0)), sspec], out_specs=sspec),
        out_shape=jax.ShapeDtypeStruct((Rsh, LANES), F32), compiler_params=_params("arbitrary"),
    )(core, sf, rs)
    return ca, cab, csa


def _add_chips(ca, rb, rsb, chip):
    Rh = ca.shape[1]
    tr = _row_tile(Rh)
    Rsh = rsb.shape[1]

    def big_body(chip_ref, own_ref, r_ref, o_ref):
        acc = own_ref[0]
        for k in range(N_CHIPS):
            acc = acc + jnp.where(chip_ref[0] == k, 0.0, r_ref[k].astype(F32))
        o_ref[...] = acc

    tb = pl.pallas_call(
        big_body, name="add_chips",
        grid_spec=pltpu.PrefetchScalarGridSpec(
            num_scalar_prefetch=1, grid=(Rh // tr,),
            in_specs=[pl.BlockSpec((1, tr, LANES), lambda i, chip_ref: (chip_ref[0], i, 0)),
                      pl.BlockSpec((N_CHIPS, tr, LANES), lambda i, chip_ref: (0, i, 0))],
            out_specs=pl.BlockSpec((tr, LANES), lambda i, chip_ref: (i, 0))),
        out_shape=jax.ShapeDtypeStruct((Rh, LANES), F32), compiler_params=_params("parallel"),
    )(chip, ca, rb)

    def small_body(r_ref, o_ref):
        o_ref[...] = ((r_ref[0] + r_ref[1]) + r_ref[2]) + r_ref[3]

    ts = pl.pallas_call(
        small_body, name="add_chips_small", out_shape=jax.ShapeDtypeStruct((Rsh, LANES), F32),
    )(rsb)
    return tb, ts


SHARDED = (("w_in", (D_MODEL, 4616), 1), ("w_branch_sgu", (SGU_W, D_MODEL), 1), ("w_branch_attn", (ATTN_W, D_MODEL), 1),
           ("w_out", (D_MODEL, D_MODEL), 0), ("w_up", (D_MODEL, D_FF), 1), ("w_down", (D_FF, D_MODEL), 0))
SMALL = (("g_mix_pre", (1, D_MODEL)), ("b_forget", (1, N_HEADS)), ("g_sgu", (1, SGU_W)), ("b_sgu", (1, SGU_W)),
         ("w_spatial", (N_GROUPS * CHUNK, CHUNK)), ("b_spatial", (N_GROUPS, CHUNK)), ("g_mix_post", (1, D_MODEL)),
         ("g_ffn_pre", (1, D_MODEL)), ("g_ffn_post", (1, D_MODEL)))
PACK_ALIGN = 256


def _shard_shape(shape, axis):
    return tuple(s // N_CHIPS if a == axis else s for a, s in enumerate(shape))


def _padded_rows(rows):
    return -(-rows // PACK_ALIGN) * PACK_ALIGN


def _pack_rows(parts, axis):
    rows = sum(p.shape[axis] for p in parts)
    pad = _padded_rows(rows) - rows
    if pad:
        shape = list(parts[0].shape)
        shape[axis] = pad
        parts = list(parts) + [jnp.zeros(shape, parts[0].dtype)]
    return jnp.concatenate(parts, axis=axis)


def _pack_shards(shards, dtype):
    return _pack_rows([shards[name].astype(dtype).reshape(-1, LANES) for name, _, _ in SHARDED], 0)


def _unpack_shards(packed):
    out, row = {}, 0
    for name, shape, axis in SHARDED:
        sshape = _shard_shape(shape, axis)
        n = sshape[0] * sshape[1] // LANES
        if packed.ndim == 2:
            out[name] = packed[row:row + n].reshape(sshape)
        else:
            parts = packed[:, row:row + n].reshape((N_CHIPS,) + sshape)
            out[name] = parts.reshape(shape) if axis == 0 else parts.transpose(1, 0, 2).reshape(shape)
        row += n
    return out


def _pack_full_grads(grads):
    parts = []
    for name, shape, axis in SHARDED:
        g = grads[name]
        sshape = _shard_shape(shape, axis)
        if axis == 0:
            g = g.reshape((N_CHIPS,) + sshape)
        else:
            g = g.reshape(shape[0], N_CHIPS, sshape[1]).transpose(1, 0, 2)
        parts.append(g.reshape(N_CHIPS, -1, LANES))
    return _pack_rows(parts, 1)


def _small_rows(shape):
    return -(-(shape[0] * shape[1]) // (8 * LANES)) * 8


def _pack_small(values):
    parts = []
    for name, shape in SMALL:
        flat = values[name].reshape(-1)
        n = _small_rows(shape)
        parts.append(jnp.pad(flat, (0, n * LANES - flat.shape[0])).reshape(n, LANES))
    return _pack_rows(parts, 0)


def _unpack_small(packed):
    out, row = {}, 0
    for name, shape in SMALL:
        n = _small_rows(shape)
        out[name] = packed[row:row + n].reshape(-1)[:shape[0] * shape[1]].reshape(shape)
        row += n
    return out


IN_Z, IN_Q, IN_K, IN_V, IN_F, IN_G, IN_END = 0, 1024, 1536, 2048, 2560, 2568, 4616


def _local_step(x, target, w, small):
    w_in = w["w_in"]
    w_z, w_qkv, w_g = w_in[:, IN_Z:IN_Q], w_in[:, IN_Q:IN_F], w_in[:, IN_G:IN_END]
    w_q, w_k, w_v = w_in[:, IN_Q:IN_K], w_in[:, IN_K:IN_V], w_in[:, IN_V:IN_F]
    w_f = jnp.pad(w_in[:, IN_F:IN_G], ((0, 0), (0, LANES - N_HEADS)))
    b_forget = jnp.pad(small["b_forget"], ((0, 0), (0, LANES - N_HEADS)))
    causal = jnp.tril(jnp.ones((CHUNK, CHUNK), bool))
    ws = jnp.where(causal[None], small["w_spatial"].reshape(N_GROUPS, CHUNK, CHUNK), 0.0).astype(BF)
    ws_t = ws.transpose(0, 2, 1)
    bias_plane = jnp.repeat(small["b_spatial"].T, HEAD_DIM, axis=1)

    xn = _rms_fwd(x, small["g_mix_pre"])
    z = _matmul([(xn, w_z)], nt=False, out_dtypes=[F32], tm=512, tn=512, name="proj_z")
    qkv = _matmul([(xn, w_qkv)], nt=False, out_dtypes=[BF], tm=512, tn=512, name="proj_qkv")
    gl = _matmul([(xn, w_g)], nt=False, out_dtypes=[F32], tm=512, tn=512, name="proj_gate")
    fl = _matmul([(xn, w_f)], nt=False, out_dtypes=[F32], tm=512, tn=LANES, name="proj_forget")
    ysgu = _sgu_fwd(z, small["g_sgu"], small["b_sgu"], ws, bias_plane)
    qf, kl, vl, tile_stats = _attn_prep(qkv, fl, b_forget)
    first_key_tile, last_query_tile = _attn_ranges(tile_stats)
    yattn, yattn_f, ql = _attn_fwd(qf, kl, vl, first_key_tile)
    a, b, merged = _branch_merge(ysgu, yattn, w["w_branch_sgu"], w["w_branch_attn"], gl)
    o = _matmul([(merged, w["w_out"])], nt=False, out_dtypes=[F32], tm=512, tn=512, name="proj_out")
    h1, xn2 = _mixer_out_fwd(o, x, small["g_mix_post"], small["g_ffn_pre"])

    def relu2(acc):
        r = jnp.maximum(acc, 0.0)
        return r * r, r

    hid, relu = _matmul([(xn2, w["w_up"])], nt=False, out_dtypes=[BF, BF], tm=512, tn=512, name="ffn_up", epilogue=relu2)
    dn = _matmul([(hid, w["w_down"])], nt=False, out_dtypes=[F32], tm=512, tn=512, name="ffn_down")
    sq, dy, ddn, dg_ffn_post = _loss_head(dn, h1, target, small["g_ffn_post"])

    dup = _matmul([(ddn, w["w_down"])], nt=True, out_dtypes=[BF], tm=512, tn=512, name="ffn_down_bwd",
                  epilogue=lambda acc, r: (acc * (2.0 * r.astype(F32)),), extras=[relu])
    dw_down = _matmul_tn(hid, ddn, name="dw_down")
    dxn2 = _matmul([(dup, w["w_up"])], nt=True, out_dtypes=[F32], tm=512, tn=512, name="ffn_up_bwd")
    dw_up = _matmul_tn(xn2, dup, name="dw_up")
    dh1, do, dg_ffn_pre, dg_mix_post = _mixer_out_bwd(h1, dxn2, dy, o, small["g_ffn_pre"], small["g_mix_post"])

    dmerged = _matmul([(do, w["w_out"])], nt=True, out_dtypes=[F32], tm=512, tn=512, name="proj_out_bwd")
    dw_out = _matmul_tn(merged, do, name="dw_out")
    da, db, dgla, dglb = _gate_bwd(dmerged, a, b, gl)
    dysgu = _matmul([(da, w["w_branch_sgu"])], nt=True, out_dtypes=[F32], tm=512, tn=512, name="branch_sgu_bwd")
    dyattn = _matmul([(db, w["w_branch_attn"])], nt=True, out_dtypes=[F32], tm=512, tn=512, name="branch_attn_bwd")
    dw_bs = _matmul_tn(ysgu, da, name="dw_branch_sgu")
    dw_ba = _matmul_tn(yattn, db, name="dw_branch_attn")
    dz, dws, dbs, dg_sgu, db_sgu = _sgu_bwd(dysgu, z, small["g_sgu"], small["b_sgu"], ws, ws_t, bias_plane)
    dout = _attn_bwd_prep(dyattn, yattn_f)
    dq, ext_q = _attn_bwd_dq(ql, dout, kl, vl, first_key_tile)
    dk, dv, ext_k = _attn_bwd_dkv(kl, vl, ql, dout, last_query_tile)
    dfl, dbf = _forget_bwd(ext_q, ext_k, fl, b_forget)
    dxn = _matmul([(dz, w_z), (dq, w_q), (dk, w_k), (dv, w_v), (dgla, w_g[:, :D_MODEL]), (dglb, w_g[:, D_MODEL:]), (dfl, w_f)],
                  nt=True, out_dtypes=[F32], tm=512, tn=512, name="proj_in_bwd")
    dw_in = jnp.concatenate(
        [_matmul_tn(xn, dz, name="dw_in_z"), _matmul_tn(xn, dq, name="dw_in_q"), _matmul_tn(xn, dk, name="dw_in_k"),
         _matmul_tn(xn, dv, name="dw_in_v"), _matmul_tn(xn, dfl, name="dw_in_f")[:, :N_HEADS],
         _matmul_tn(xn, dgla, name="dw_in_ga"), _matmul_tn(xn, dglb, name="dw_in_gb")], axis=1)
    dx, dg_mix_pre = _input_norm_bwd(x, dxn, dh1, small["g_mix_pre"])

    grads = {"w_in": dw_in, "w_branch_sgu": dw_bs, "w_branch_attn": dw_ba, "w_out": dw_out, "w_up": dw_up, "w_down": dw_down}
    small_grads = {"g_mix_pre": dg_mix_pre, "b_forget": dbf[:, :N_HEADS], "g_sgu": dg_sgu, "b_sgu": db_sgu,
                   "w_spatial": dws.reshape(N_GROUPS * CHUNK, CHUNK), "b_spatial": dbs[:, :N_GROUPS].T,
                   "g_mix_post": dg_mix_post, "g_ffn_pre": dg_ffn_pre, "g_ffn_post": dg_ffn_post}
    return sq, dx, grads, small_grads


NAMES = ("g_mix_pre", "w_in", "b_forget", "g_sgu", "b_sgu", "w_spatial", "b_spatial", "w_branch_sgu", "w_branch_attn",
         "w_out", "g_mix_post", "g_ffn_pre", "w_up", "w_down", "g_ffn_post")


def kernel(x, g_mix_pre, w_in, b_forget, g_sgu, b_sgu, w_spatial, b_spatial, w_branch_sgu, w_branch_attn, w_out, g_mix_post, g_ffn_pre, w_up, w_down, g_ffn_post, loss_target, m_g_mix_pre, m_w_in, m_b_forget, m_g_sgu, m_b_sgu, m_w_spatial, m_b_spatial, m_w_branch_sgu, m_w_branch_attn, m_w_out, m_g_mix_post, m_g_ffn_pre, m_w_up, m_w_down, m_g_ffn_post, v_g_mix_pre, v_w_in, v_b_forget, v_g_sgu, v_b_sgu, v_w_spatial, v_b_spatial, v_w_branch_sgu, v_w_branch_attn, v_w_out, v_g_mix_post, v_g_ffn_pre, v_w_up, v_w_down, v_g_ffn_post):
    weights = dict(zip(NAMES, (g_mix_pre, w_in, b_forget, g_sgu, b_sgu, w_spatial, b_spatial, w_branch_sgu, w_branch_attn,
                               w_out, g_mix_post, g_ffn_pre, w_up, w_down, g_ffn_post), strict=True))
    first = dict(zip(NAMES, (m_g_mix_pre, m_w_in, m_b_forget, m_g_sgu, m_b_sgu, m_w_spatial, m_b_spatial, m_w_branch_sgu,
                             m_w_branch_attn, m_w_out, m_g_mix_post, m_g_ffn_pre, m_w_up, m_w_down, m_g_ffn_post), strict=True))
    second = dict(zip(NAMES, (v_g_mix_pre, v_w_in, v_b_forget, v_g_sgu, v_b_sgu, v_w_spatial, v_b_spatial, v_w_branch_sgu,
                              v_w_branch_attn, v_w_out, v_g_mix_post, v_g_ffn_pre, v_w_up, v_w_down, v_g_ffn_post), strict=True))
    shard_shapes = {name: _shard_shape(shape, axis) for name, shape, axis in SHARDED}
    small_shapes = dict(SMALL)
    view = lambda name, a: a.reshape(shard_shapes.get(name) or small_shapes[name])

    core = lax.axis_index("c").astype(jnp.int32).reshape(1)
    chip = (2 * lax.axis_index("x") + lax.axis_index("y")).astype(jnp.int32).reshape(1)

    shards = {name: view(name, weights[name]) for name, _, _ in SHARDED}
    full = _unpack_shards(_gather_weights(_pack_shards(shards, BF)))
    small = {name: view(name, weights[name]) for name, _ in SMALL}

    sq, dx, grads, small_grads = _local_step(x[0], loss_target[0], full, small)
    loss = lax.psum(0.5 * jnp.sum(sq) / D_MODEL, ("x", "y", "c"))

    gf = _pack_full_grads(grads)
    sf = _pack_small(small_grads)
    rg, rs = _exchange_halves(gf, sf)
    ca, cab, csa = _add_sibling(gf, rg, sf, rs, core)
    rb, rsb = _scatter_to_owners(cab, csa)
    tb, ts = _add_chips(ca, rb, rsb, chip)
    g_packed, s_packed = _join_halves(tb, ts)
    grad = {**_unpack_shards(g_packed), **_unpack_small(s_packed)}

    delta, new_m, new_v = {}, {}, {}
    for name in NAMES:
        delta[name], new_m[name], new_v[name] = _adamw(
            view(name, weights[name]), grad[name], view(name, first[name]), view(name, second[name]), name="adamw_" + name)

    like = lambda d: [d[name].reshape(weights[name].shape) for name in NAMES]
    return (loss, dx[None], *like(grad), *like(delta), *like(new_m), *like(new_v))
```

```python
import functools

import jax
import jax.numpy as jnp
from jax import lax
from jax.experimental import pallas as pl
from jax.experimental.pallas import tpu as pltpu

F32 = jnp.float32
BF = jnp.bfloat16
MESH = pl.DeviceIdType.MESH

D_MODEL = 1024
N_HEADS = 8
HEAD_DIM = 64
ATTN_W = N_HEADS * HEAD_DIM
SGU_W = 512
N_GROUPS = 8
CHUNK = 128
D_FF = 4096
EPS = 1e-6
Q_SCALE = HEAD_DIM ** -0.5
N_CHIPS = 4
LANES = 128

ADAM_LR = 0.001
ADAM_B1 = 0.9
ADAM_B2 = 0.999
ADAM_EPS = 1e-08
ADAM_WD = 0.01
ADAM_STEP = 10

VMEM_LIMIT = 48 * 1024 * 1024
NEG = -1e30

LANE_ROWSUM = HEAD_DIM
LANE_COLSUM = HEAD_DIM + 3


def _params(*sem):
    return pltpu.CompilerParams(dimension_semantics=sem, vmem_limit_bytes=VMEM_LIMIT)


def _dot(a, b):
    return jnp.dot(a, b, preferred_element_type=F32)


def _dot_nt(a, b):
    return lax.dot_general(a, b, (((1,), (1,)), ((), ())), preferred_element_type=F32)


def _dot_tn(a, b):
    return lax.dot_general(a, b, (((0,), (0,)), ((), ())), preferred_element_type=F32)


def _split3(c):
    hi = c.astype(BF).astype(F32)
    r = c - hi
    mid = r.astype(BF).astype(F32)
    lo = (r - mid).astype(BF).astype(F32)
    return hi, mid, lo


def _gelu(x):
    k = 0.7978845608028654
    return 0.5 * x * (1.0 + jnp.tanh(k * (x + 0.044715 * (x * x * x))))


def _gelu_grad(x):
    k = 0.7978845608028654
    x2 = x * x
    t = jnp.tanh(k * (x + 0.044715 * (x2 * x)))
    return 0.5 * (1.0 + t) + 0.5 * x * (1.0 - t * t) * (k * (1.0 + 3.0 * 0.044715 * x2))


def _rms_bwd(a, g, dy):
    r = lax.rsqrt(jnp.mean(a * a, axis=-1, keepdims=True) + EPS)
    n = a * r
    dn = dy * g
    da = r * (dn - n * jnp.mean(dn * n, axis=-1, keepdims=True))
    return da, dy * n


def _matmul(pairs, *, nt, out_dtypes, tm, tn, name, epilogue=None, extras=()):
    n_pairs = len(pairs)
    n_extra = len(extras)
    M = pairs[0][0].shape[0]
    N = pairs[0][1].shape[0] if nt else pairs[0][1].shape[1]
    tm, tn = min(tm, M), min(tn, N)
    assert M % tm == 0 and N % tn == 0

    def body(*refs):
        acc = None
        for p in range(n_pairs):
            a_ref, b_ref = refs[2 * p], refs[2 * p + 1]
            d = _dot_nt(a_ref[...], b_ref[...]) if nt else _dot(a_ref[...], b_ref[...])
            acc = d if acc is None else acc + d
        e_refs = refs[2 * n_pairs:2 * n_pairs + n_extra]
        o_refs = refs[2 * n_pairs + n_extra:]
        outs = (acc,) if epilogue is None else epilogue(acc, *[e[...] for e in e_refs])
        for o_ref, o in zip(o_refs, outs, strict=True):
            o_ref[...] = o.astype(o_ref.dtype)

    in_specs, args = [], []
    for a, b in pairs:
        K = a.shape[1]
        in_specs.append(pl.BlockSpec((tm, K), lambda i, j: (i, 0)))
        in_specs.append(pl.BlockSpec((tn, K), lambda i, j: (j, 0)) if nt else pl.BlockSpec((K, tn), lambda i, j: (0, j)))
        args += [a, b]
    for e in extras:
        in_specs.append(pl.BlockSpec((tm, tn), lambda i, j: (i, j)))
        args.append(e)
    outs = pl.pallas_call(
        body, name=name, grid=(M // tm, N // tn), in_specs=in_specs,
        out_specs=[pl.BlockSpec((tm, tn), lambda i, j: (i, j)) for _ in out_dtypes],
        out_shape=[jax.ShapeDtypeStruct((M, N), dt) for dt in out_dtypes],
        compiler_params=_params("parallel", "parallel"),
    )(*args)
    return outs if len(outs) > 1 else outs[0]


def _matmul_tn(a, b, *, name, tm=1024, tn=1024, tk=512):
    T, K1 = a.shape
    N = b.shape[1]
    tm, tn, tk = min(tm, K1), min(tn, N), min(tk, T)
    assert K1 % tm == 0 and N % tn == 0 and T % tk == 0

    def body(a_ref, b_ref, o_ref):
        @pl.when(pl.program_id(2) == 0)
        def _():
            o_ref[...] = jnp.zeros_like(o_ref)

        o_ref[...] += _dot_tn(a_ref[...], b_ref[...])

    return pl.pallas_call(
        body, name=name, grid=(K1 // tm, N // tn, T // tk),
        in_specs=[pl.BlockSpec((tk, tm), lambda i, j, k: (k, i)), pl.BlockSpec((tk, tn), lambda i, j, k: (k, j))],
        out_specs=pl.BlockSpec((tm, tn), lambda i, j, k: (i, j)),
        out_shape=jax.ShapeDtypeStruct((K1, N), F32),
        compiler_params=_params("parallel", "parallel", "arbitrary"),
    )(a, b)


def _branch_merge(ysgu, yattn, w_bs, w_ba, gl, *, tm=512, tn=512):
    T = ysgu.shape[0]
    tm = min(tm, T)
    nj = D_MODEL // tn

    def body(ys_ref, ya_ref, wbs_ref, wba_ref, gla_ref, glb_ref, a_ref, b_ref, m_ref):
        a = _dot(ys_ref[...], wbs_ref[...])
        b = _dot(ya_ref[...], wba_ref[...])
        a_ref[...] = a
        b_ref[...] = b
        m_ref[...] = (jax.nn.sigmoid(gla_ref[...]) * a + jax.nn.sigmoid(glb_ref[...]) * b).astype(BF)

    return pl.pallas_call(
        body, name="branch_merge", grid=(T // tm, nj),
        in_specs=[
            pl.BlockSpec((tm, SGU_W), lambda i, j: (i, 0)),
            pl.BlockSpec((tm, ATTN_W), lambda i, j: (i, 0)),
            pl.BlockSpec((SGU_W, tn), lambda i, j: (0, j)),
            pl.BlockSpec((ATTN_W, tn), lambda i, j: (0, j)),
            pl.BlockSpec((tm, tn), lambda i, j: (i, j)),
            pl.BlockSpec((tm, tn), lambda i, j: (i, j + nj)),
        ],
        out_specs=[pl.BlockSpec((tm, tn), lambda i, j: (i, j))] * 3,
        out_shape=[jax.ShapeDtypeStruct((T, D_MODEL), F32), jax.ShapeDtypeStruct((T, D_MODEL), F32),
                   jax.ShapeDtypeStruct((T, D_MODEL), BF)],
        compiler_params=_params("parallel", "parallel"),
    )(ysgu, yattn, w_bs, w_ba, gl, gl)


def _row_spec(tr, width):
    return pl.BlockSpec((tr, width), lambda i: (i, 0))


def _vec_spec(width):
    return pl.BlockSpec((1, width), lambda i: (0, 0))


def _rms_fwd(x, g, *, tr=256):
    T = x.shape[0]
    tr = min(tr, T)

    def body(x_ref, g_ref, o_ref):
        xv = x_ref[...]
        r = lax.rsqrt(jnp.mean(xv * xv, axis=-1, keepdims=True) + EPS)
        o_ref[...] = ((xv * r) * g_ref[...]).astype(BF)

    return pl.pallas_call(
        body, name="rms_fwd", grid=(T // tr,),
        in_specs=[_row_spec(tr, D_MODEL), _vec_spec(D_MODEL)], out_specs=_row_spec(tr, D_MODEL),
        out_shape=jax.ShapeDtypeStruct((T, D_MODEL), BF), compiler_params=_params("parallel"),
    )(x, g)


def _mixer_out_fwd(o, x, g_post, g_pre, *, tr=256):
    T = x.shape[0]
    tr = min(tr, T)

    def body(o_ref, x_ref, gpost_ref, gpre_ref, h1_ref, xn2_ref):
        ov = o_ref[...]
        r = lax.rsqrt(jnp.mean(ov * ov, axis=-1, keepdims=True) + EPS)
        h1 = x_ref[...] + (ov * r) * gpost_ref[...]
        h1_ref[...] = h1
        r2 = lax.rsqrt(jnp.mean(h1 * h1, axis=-1, keepdims=True) + EPS)
        xn2_ref[...] = ((h1 * r2) * gpre_ref[...]).astype(BF)

    return pl.pallas_call(
        body, name="mixer_out_fwd", grid=(T // tr,),
        in_specs=[_row_spec(tr, D_MODEL), _row_spec(tr, D_MODEL), _vec_spec(D_MODEL), _vec_spec(D_MODEL)],
        out_specs=[_row_spec(tr, D_MODEL), _row_spec(tr, D_MODEL)],
        out_shape=[jax.ShapeDtypeStruct((T, D_MODEL), F32), jax.ShapeDtypeStruct((T, D_MODEL), BF)],
        compiler_params=_params("parallel"),
    )(o, x, g_post, g_pre)


def _loss_head(dn, h1, target, g_post, *, tr=256):
    T = dn.shape[0]
    tr = min(tr, T)

    def body(dn_ref, h1_ref, t_ref, g_ref, sq_ref, dy_ref, ddn_ref, dg_ref):
        @pl.when(pl.program_id(0) == 0)
        def _():
            sq_ref[...] = jnp.zeros_like(sq_ref)
            dg_ref[...] = jnp.zeros_like(dg_ref)

        a = dn_ref[...]
        g = g_ref[...]
        r = lax.rsqrt(jnp.mean(a * a, axis=-1, keepdims=True) + EPS)
        err = h1_ref[...] + (a * r) * g - t_ref[...]
        sq_ref[...] += jnp.sum(err * err, axis=0, keepdims=True)
        dy = err * (1.0 / D_MODEL)
        dy_ref[...] = dy
        da, dgp = _rms_bwd(a, g, dy)
        ddn_ref[...] = da.astype(BF)
        dg_ref[...] += jnp.sum(dgp, axis=0, keepdims=True)

    return pl.pallas_call(
        body, name="loss_head", grid=(T // tr,),
        in_specs=[_row_spec(tr, D_MODEL)] * 3 + [_vec_spec(D_MODEL)],
        out_specs=[_vec_spec(D_MODEL), _row_spec(tr, D_MODEL), _row_spec(tr, D_MODEL), _vec_spec(D_MODEL)],
        out_shape=[jax.ShapeDtypeStruct((1, D_MODEL), F32), jax.ShapeDtypeStruct((T, D_MODEL), F32),
                   jax.ShapeDtypeStruct((T, D_MODEL), BF), jax.ShapeDtypeStruct((1, D_MODEL), F32)],
        compiler_params=_params("arbitrary"),
    )(dn, h1, target, g_post)


def _mixer_out_bwd(h1, dxn2, dy, o, g_pre, g_post, *, tr=256):
    T = h1.shape[0]
    tr = min(tr, T)

    def body(h1_ref, dxn2_ref, dy_ref, o_ref, gpre_ref, gpost_ref, dh1_ref, do_ref, dgpre_ref, dgpost_ref):
        @pl.when(pl.program_id(0) == 0)
        def _():
            dgpre_ref[...] = jnp.zeros_like(dgpre_ref)
            dgpost_ref[...] = jnp.zeros_like(dgpost_ref)

        da, dgp = _rms_bwd(h1_ref[...], gpre_ref[...], dxn2_ref[...])
        dh1 = dy_ref[...] + da
        dh1_ref[...] = dh1
        dgpre_ref[...] += jnp.sum(dgp, axis=0, keepdims=True)
        do, dgp2 = _rms_bwd(o_ref[...], gpost_ref[...], dh1)
        do_ref[...] = do.astype(BF)
        dgpost_ref[...] += jnp.sum(dgp2, axis=0, keepdims=True)

    return pl.pallas_call(
        body, name="mixer_out_bwd", grid=(T // tr,),
        in_specs=[_row_spec(tr, D_MODEL)] * 4 + [_vec_spec(D_MODEL)] * 2,
        out_specs=[_row_spec(tr, D_MODEL), _row_spec(tr, D_MODEL), _vec_spec(D_MODEL), _vec_spec(D_MODEL)],
        out_shape=[jax.ShapeDtypeStruct((T, D_MODEL), F32), jax.ShapeDtypeStruct((T, D_MODEL), BF),
                   jax.ShapeDtypeStruct((1, D_MODEL), F32), jax.ShapeDtypeStruct((1, D_MODEL), F32)],
        compiler_params=_params("arbitrary"),
    )(h1, dxn2, dy, o, g_pre, g_post)


def _input_norm_bwd(x, dxn, dh1, g, *, tr=256):
    T = x.shape[0]
    tr = min(tr, T)

    def body(x_ref, dxn_ref, dh1_ref, g_ref, dx_ref, dg_ref):
        @pl.when(pl.program_id(0) == 0)
        def _():
            dg_ref[...] = jnp.zeros_like(dg_ref)

        da, dgp = _rms_bwd(x_ref[...], g_ref[...], dxn_ref[...])
        dx_ref[...] = dh1_ref[...] + da
        dg_ref[...] += jnp.sum(dgp, axis=0, keepdims=True)

    return pl.pallas_call(
        body, name="input_norm_bwd", grid=(T // tr,),
        in_specs=[_row_spec(tr, D_MODEL)] * 3 + [_vec_spec(D_MODEL)],
        out_specs=[_row_spec(tr, D_MODEL), _vec_spec(D_MODEL)],
        out_shape=[jax.ShapeDtypeStruct((T, D_MODEL), F32), jax.ShapeDtypeStruct((1, D_MODEL), F32)],
        compiler_params=_params("arbitrary"),
    )(x, dxn, dh1, g)


def _gate_bwd(dm, a, b, gl, *, tr=256):
    T = dm.shape[0]
    tr = min(tr, T)

    def body(dm_ref, a_ref, b_ref, gla_ref, glb_ref, da_ref, db_ref, dgla_ref, dglb_ref):
        dmv = dm_ref[...]
        ga = jax.nn.sigmoid(gla_ref[...])
        gb = jax.nn.sigmoid(glb_ref[...])
        da_ref[...] = (dmv * ga).astype(BF)
        db_ref[...] = (dmv * gb).astype(BF)
        dgla_ref[...] = (dmv * a_ref[...] * (ga * (1.0 - ga))).astype(BF)
        dglb_ref[...] = (dmv * b_ref[...] * (gb * (1.0 - gb))).astype(BF)

    spec = _row_spec(tr, D_MODEL)
    spec_b = pl.BlockSpec((tr, D_MODEL), lambda i: (i, 1))
    da, db, dgla, dglb = pl.pallas_call(
        body, name="gate_bwd", grid=(T // tr,),
        in_specs=[spec, spec, spec, spec, spec_b], out_specs=[spec] * 4,
        out_shape=[jax.ShapeDtypeStruct((T, D_MODEL), BF)] * 4, compiler_params=_params("parallel"),
    )(dm, a, b, gl, gl)
    return da, db, dgla, dglb


def _sgu_norm(z_tile, g, b):
    gz = _gelu(z_tile)
    u, vv = gz[:, :SGU_W], gz[:, SGU_W:]
    xc = vv - jnp.mean(vv, axis=-1, keepdims=True)
    rstd = lax.rsqrt(jnp.mean(xc * xc, axis=-1, keepdims=True) + EPS)
    xhat = xc * rstd
    return u, xhat, rstd, xhat * g + b


def _sgu_mix(w_ref, v_bf, first_half):
    parts = []
    for p in range(N_GROUPS // 2):
        vp = v_bf[:, p * LANES:(p + 1) * LANES]
        parts.append(jnp.where(first_half, _dot(w_ref[2 * p], vp), _dot(w_ref[2 * p + 1], vp)))
    return jnp.concatenate(parts, axis=1)


def _sgu_fwd(z, g_sgu, b_sgu, ws, bias_plane, *, tm=512):
    T = z.shape[0]
    tm = min(tm, T)

    def body(z_ref, g_ref, b_ref, ws_ref, bp_ref, y_ref):
        u, _, _, vn = _sgu_norm(z_ref[...], g_ref[...], b_ref[...])
        vn_bf = vn.astype(BF)
        first_half = lax.broadcasted_iota(jnp.int32, (CHUNK, LANES), 1) < HEAD_DIM
        for c in range(tm // CHUNK):
            rows = slice(c * CHUNK, (c + 1) * CHUNK)
            s = _sgu_mix(ws_ref, vn_bf[rows, :], first_half) + bp_ref[...]
            y_ref[rows, :] = (u[rows, :] * s).astype(BF)

    return pl.pallas_call(
        body, name="sgu_fwd", grid=(T // tm,),
        in_specs=[_row_spec(tm, 2 * SGU_W), _vec_spec(SGU_W), _vec_spec(SGU_W),
                  pl.BlockSpec((N_GROUPS, CHUNK, CHUNK), lambda i: (0, 0, 0)),
                  pl.BlockSpec((CHUNK, SGU_W), lambda i: (0, 0))],
        out_specs=_row_spec(tm, SGU_W), out_shape=jax.ShapeDtypeStruct((T, SGU_W), BF),
        compiler_params=_params("parallel"),
    )(z, g_sgu, b_sgu, ws, bias_plane)


def _sgu_bwd(dy, z, g_sgu, b_sgu, ws, ws_t, bias_plane, *, tm=512):
    T = z.shape[0]
    tm = min(tm, T)
    n_steps = T // tm

    def body(dy_ref, z_ref, g_ref, b_ref, ws_ref, wst_ref, bp_ref, dz_ref, dws_ref, dbs_ref, dg_ref, db_ref, dbp_ref):
        step = pl.program_id(0)

        @pl.when(step == 0)
        def _():
            dws_ref[...] = jnp.zeros_like(dws_ref)
            dg_ref[...] = jnp.zeros_like(dg_ref)
            db_ref[...] = jnp.zeros_like(db_ref)
            dbp_ref[...] = jnp.zeros_like(dbp_ref)

        g = g_ref[...]
        zt = z_ref[...]
        u, xhat, rstd, vn = _sgu_norm(zt, g, b_ref[...])
        vn_bf = vn.astype(BF)
        first_half = lax.broadcasted_iota(jnp.int32, (CHUNK, LANES), 1) < HEAD_DIM
        dyv = dy_ref[...]
        dg_acc = jnp.zeros((1, SGU_W), F32)
        db_acc = jnp.zeros((1, SGU_W), F32)
        for c in range(tm // CHUNK):
            rows = slice(c * CHUNK, (c + 1) * CHUNK)
            v_c = vn_bf[rows, :]
            s = _sgu_mix(ws_ref, v_c, first_half) + bp_ref[...]
            dy_c = dyv[rows, :]
            du = dy_c * s
            dsv = dy_c * u[rows, :]
            dbp_ref[...] += dsv
            ds_bf = dsv.astype(BF)
            zero = jnp.zeros((CHUNK, LANES), BF)
            for p in range(N_GROUPS // 2):
                dsp = ds_bf[:, p * LANES:(p + 1) * LANES]
                vp = v_c[:, p * LANES:(p + 1) * LANES]
                dws_ref[2 * p] += _dot_nt(jnp.where(first_half, dsp, zero), vp)
                dws_ref[2 * p + 1] += _dot_nt(jnp.where(first_half, zero, dsp), vp)
            dvn = _sgu_mix(wst_ref, ds_bf, first_half)
            xh = xhat[rows, :]
            dxh = dvn * g
            dvv = rstd[rows, :] * (dxh - jnp.mean(dxh, axis=-1, keepdims=True)
                                   - xh * jnp.mean(dxh * xh, axis=-1, keepdims=True))
            dg_acc += jnp.sum(dvn * xh, axis=0, keepdims=True)
            db_acc += jnp.sum(dvn, axis=0, keepdims=True)
            dgz = jnp.concatenate([du, dvv], axis=1)
            dz_ref[rows, :] = (dgz * _gelu_grad(zt[rows, :])).astype(BF)
        dg_ref[...] += dg_acc
        db_ref[...] += db_acc

        @pl.when(step == n_steps - 1)
        def _():
            r = lax.broadcasted_iota(jnp.int32, (CHUNK, CHUNK), 0)
            cidx = lax.broadcasted_iota(jnp.int32, (CHUNK, CHUNK), 1)
            causal = (cidx <= r).astype(F32)
            for gi in range(N_GROUPS):
                dws_ref[gi] = dws_ref[gi] * causal
            lane = lax.broadcasted_iota(jnp.int32, (CHUNK, LANES), 1)
            out = jnp.zeros((CHUNK, LANES), F32)
            dbp = dbp_ref[...]
            for gi in range(N_GROUPS):
                col = jnp.sum(dbp[:, gi * HEAD_DIM:(gi + 1) * HEAD_DIM], axis=1, keepdims=True)
                out = jnp.where(lane == gi, col, out)
            dbs_ref[...] = out

    w_spec = pl.BlockSpec((N_GROUPS, CHUNK, CHUNK), lambda i: (0, 0, 0))
    plane = pl.BlockSpec((CHUNK, SGU_W), lambda i: (0, 0))
    return pl.pallas_call(
        body, name="sgu_bwd", grid=(n_steps,),
        in_specs=[_row_spec(tm, SGU_W), _row_spec(tm, 2 * SGU_W), _vec_spec(SGU_W), _vec_spec(SGU_W), w_spec, w_spec, plane],
        out_specs=[_row_spec(tm, 2 * SGU_W), w_spec, pl.BlockSpec((CHUNK, LANES), lambda i: (0, 0)),
                   _vec_spec(SGU_W), _vec_spec(SGU_W)],
        out_shape=[jax.ShapeDtypeStruct((T, 2 * SGU_W), BF), jax.ShapeDtypeStruct((N_GROUPS, CHUNK, CHUNK), F32),
                   jax.ShapeDtypeStruct((CHUNK, LANES), F32), jax.ShapeDtypeStruct((1, SGU_W), F32),
                   jax.ShapeDtypeStruct((1, SGU_W), F32)],
        scratch_shapes=[pltpu.VMEM((CHUNK, SGU_W), F32)],
        compiler_params=_params("arbitrary"),
    )(dy, z, g_sgu, b_sgu, ws, ws_t, bias_plane)


def _tri(n, upper):
    r = lax.broadcasted_iota(jnp.int32, (n, n), 0)
    c = lax.broadcasted_iota(jnp.int32, (n, n), 1)
    return ((c >= r) if upper else (c <= r)).astype(BF)


def _scan_dot(tri, x):
    hi, mid, lo = _split3(x)
    return (_dot(tri, hi.astype(BF)) + _dot(tri, mid.astype(BF))) + _dot(tri, lo.astype(BF))


def _with_lanes(base, lane, start, cols):
    out = base
    for k, col in enumerate(cols):
        if col is not None:
            out = jnp.where(lane == start + k, col, out)
    return out


def _logit_bound(q_norm, k_norm):
    return NORM_SLACK * q_norm * k_norm + 1.0


ATTN_TILE = 512
SKIP_BELOW = -110.0
NORM_SLACK = 1.001
BOUNDED_GAP = 60.0


def _attn_prep(qkv, fl, b_forget, *, tp=ATTN_TILE):
    T = qkv.shape[0]
    tp = min(tp, T)

    def body(qkv_ref, fl_ref, bf_ref, qf_ref, kl_ref, vl_ref, st_ref, carry_ref, kmax_ref):
        @pl.when(pl.program_id(0) == 0)
        def _():
            carry_ref[...] = jnp.zeros_like(carry_ref)
            kmax_ref[...] = jnp.zeros_like(kmax_ref)

        x = fl_ref[...] + bf_ref[...]
        logf = jnp.minimum(x, 0.0) - jnp.log(1.0 + jnp.exp(-jnp.abs(x)))
        cum = _scan_dot(_tri(tp, upper=False), logf) + carry_ref[...]
        carry_ref[...] = cum[tp - 1:tp, :]
        lane = lax.broadcasted_iota(jnp.int32, (tp, HEAD_DIM), 1)
        ones3 = jnp.where(lane < 3, 1.0, 0.0)
        qkvv = qkv_ref[...]
        st_row = lax.broadcasted_iota(jnp.int32, (N_HEADS, LANES), 0)
        st_lane = lax.broadcasted_iota(jnp.int32, (N_HEADS, LANES), 1)
        stats = jnp.zeros((N_HEADS, LANES), F32)
        kmax_lane = lax.broadcasted_iota(jnp.int32, (1, LANES), 1)
        for h in range(N_HEADS):
            ch = cum[:, h:h + 1]
            c3 = _split3(ch)
            qh = qkvv[:, h * HEAD_DIM:(h + 1) * HEAD_DIM].astype(F32) * Q_SCALE
            kh = qkvv[:, ATTN_W + h * HEAD_DIM:ATTN_W + (h + 1) * HEAD_DIM].astype(F32)
            vh = qkvv[:, 2 * ATTN_W + h * HEAD_DIM:2 * ATTN_W + (h + 1) * HEAD_DIM].astype(F32)
            q_norm = jnp.sqrt(jnp.sum(qh * qh, axis=1, keepdims=True))
            qn = jnp.max(q_norm, axis=0, keepdims=True)
            kn = jnp.sqrt(jnp.max(jnp.sum(kh * kh, axis=1, keepdims=True), axis=0, keepdims=True))
            k_seen = jnp.maximum(kmax_ref[:, h:h + 1], kn)
            kmax_ref[...] = jnp.where(kmax_lane == h, k_seen, kmax_ref[...])
            bound3 = _split3(-_logit_bound(q_norm, k_seen))
            ext_q = _with_lanes(jnp.where((lane >= 3) & (lane < 6), 1.0, 0.0), lane, 0, list(c3) + [None] * 3 + list(bound3))
            ext_k = _with_lanes(jnp.where((lane < 3) | ((lane >= 6) & (lane < 9)), 1.0, 0.0), lane, 3, [-c for c in c3])
            qf_ref[h] = jnp.concatenate([qh, ext_q], axis=1).astype(BF)
            kl_ref[h] = jnp.concatenate([kh, ext_k], axis=1).astype(BF)
            vl_ref[h] = jnp.concatenate([vh, ones3], axis=1).astype(BF)
            tile_stats = (qn, kn, jnp.max(ch, axis=0, keepdims=True), jnp.min(ch, axis=0, keepdims=True), k_seen)
            for k, val in enumerate(tile_stats):
                stats = jnp.where((st_row == h) & (st_lane == k), val, stats)
        st_ref[0] = stats

    head_spec = pl.BlockSpec((N_HEADS, tp, LANES), lambda i: (0, i, 0))
    return pl.pallas_call(
        body, name="attn_prep", grid=(T // tp,),
        in_specs=[_row_spec(tp, 3 * ATTN_W), _row_spec(tp, LANES), _vec_spec(LANES)],
        out_specs=[head_spec] * 3 + [pl.BlockSpec((1, N_HEADS, LANES), lambda i: (i, 0, 0))],
        out_shape=[jax.ShapeDtypeStruct((N_HEADS, T, LANES), BF)] * 3 + [jax.ShapeDtypeStruct((T // tp, N_HEADS, LANES), F32)],
        scratch_shapes=[pltpu.VMEM((1, LANES), F32), pltpu.VMEM((1, LANES), F32)], compiler_params=_params("arbitrary"),
    )(qkv, fl, b_forget)


def _attn_ranges(stats):
    qn, kn, cmax, cmin, k_seen = (stats[:, :, k].T for k in range(5))
    n = qn.shape[1]
    bounded = (2.0 * _logit_bound(qn, k_seen) <= BOUNDED_GAP).reshape(N_HEADS // 2, 2, n).all(axis=1)
    reach = NORM_SLACK * qn * (jnp.max(kn, axis=1, keepdims=True) + kn) + cmax
    i = jnp.arange(n)[None, :, None]
    j = jnp.arange(n)[None, None, :]
    need = ((reach[:, :, None] - cmin[:, None, :] >= SKIP_BELOW) | (i == j)) & (j <= i)
    first = jnp.min(jnp.where(need, j, n), axis=2).reshape(N_HEADS // 2, 2, n).min(axis=1)
    last = jnp.max(jnp.where(need, i, -1), axis=1).reshape(N_HEADS // 2, 2, n).max(axis=1)
    return first.reshape(-1).astype(F32), last.reshape(-1).astype(F32), bounded.reshape(-1).astype(F32)


def _pair_block(t):
    return pl.BlockSpec((2, t, LANES), lambda p, i, *_: (p, i, 0))


def _pair_full(T):
    return pl.BlockSpec((2, T, LANES), lambda p, i, *_: (p, 0, 0))


def _packed_block(t):
    return pl.BlockSpec((t, LANES), lambda p, i, *_: (i, p))


def _causal(t, keys_in_rows=False):
    r = lax.broadcasted_iota(jnp.int32, (t, t), 0)
    c = lax.broadcasted_iota(jnp.int32, (t, t), 1)
    return (r <= c) if keys_in_rows else (c <= r)


def _tile_rows(j, t):
    return pl.ds(pl.multiple_of(j * t, t), t)


def _attn_call(body, name, tile_scalars, operands, in_specs, out_specs, out_shape, scratch_shapes, n_tiles):
    return pl.pallas_call(
        body, name=name,
        grid_spec=pltpu.PrefetchScalarGridSpec(
            num_scalar_prefetch=len(tile_scalars), grid=(N_HEADS // 2, n_tiles), in_specs=in_specs, out_specs=out_specs,
            scratch_shapes=scratch_shapes),
        out_shape=out_shape, compiler_params=_params("parallel", "arbitrary"),
    )(*tile_scalars, *operands)


def _attn_fwd(qf, kl, vl, first, bounded, *, tq=ATTN_TILE):
    T = qf.shape[1]
    tq = min(tq, T)
    n = T // tq

    def body(first_ref, bounded_ref, qf_ref, kl_ref, vl_ref, o_ref, of_ref, ql_ref, m_ref, acc_ref):
        i = pl.program_id(1)
        tile = pl.program_id(0) * n + i
        start = first_ref[tile].astype(jnp.int32)
        is_bounded = bounded_ref[tile] > 0.5
        acc_ref[...] = jnp.zeros_like(acc_ref)
        diagonal = _tile_rows(i, tq)
        causal = _causal(tq)

        def logits(hh, rows):
            return _dot_nt(qf_ref[hh], kl_ref[hh, rows, :])

        @pl.when(is_bounded)
        def _():
            m_ref[...] = jnp.zeros_like(m_ref)

            def update(hh, s, rows):
                acc_ref[hh] += _dot(jnp.exp(s).astype(BF), vl_ref[hh, rows, :])

            def step(j, carry):
                for hh in range(2):
                    update(hh, logits(hh, _tile_rows(j, tq)), _tile_rows(j, tq))
                return carry

            lax.fori_loop(start, i, step, 0)
            for hh in range(2):
                update(hh, jnp.where(causal, logits(hh, diagonal), NEG), diagonal)

        @pl.when(jnp.logical_not(is_bounded))
        def _():
            m_ref[...] = jnp.full_like(m_ref, NEG)

            def update(hh, s, rows):
                m_old = m_ref[hh]
                m_new = jnp.maximum(m_old, jnp.max(s, axis=1, keepdims=True))
                p = jnp.exp(s - m_new)
                acc_ref[hh] = jnp.exp(m_old - m_new) * acc_ref[hh] + _dot(p.astype(BF), vl_ref[hh, rows, :])
                m_ref[hh] = m_new

            def step(j, carry):
                for hh in range(2):
                    update(hh, logits(hh, _tile_rows(j, tq)), _tile_rows(j, tq))
                return carry

            lax.fori_loop(start, i, step, 0)
            for hh in range(2):
                update(hh, jnp.where(causal, logits(hh, diagonal), NEG), diagonal)

        lane = lax.broadcasted_iota(jnp.int32, (tq, LANES), 1)
        outs = []
        for hh in range(2):
            q = qf_ref[hh].astype(F32)
            acc = acc_ref[hh]
            l = acc[:, HEAD_DIM:HEAD_DIM + 1]
            outs.append(acc[:, :HEAD_DIM] / l)
            at = HEAD_DIM + 6
            neg_bound = (q[:, at:at + 1] + q[:, at + 1:at + 2]) + q[:, at + 2:at + 3]
            ql_ref[hh] = _with_lanes(q, lane, at, _split3(neg_bound - (m_ref[hh] + jnp.log(l)))).astype(BF)
        o = jnp.concatenate(outs, axis=1)
        o_ref[...] = o.astype(BF)
        of_ref[...] = o

    return _attn_call(
        body, "attn_fwd", (first, bounded), (qf, kl, vl), [_pair_block(tq), _pair_full(T), _pair_full(T)],
        [_packed_block(tq), _packed_block(tq), _pair_block(tq)],
        [jax.ShapeDtypeStruct((T, ATTN_W), BF), jax.ShapeDtypeStruct((T, ATTN_W), F32),
         jax.ShapeDtypeStruct((N_HEADS, T, LANES), BF)],
        [pltpu.VMEM((2, tq, 1), F32), pltpu.VMEM((2, tq, LANES), F32)], n)


def _attn_bwd_prep(dya, of, *, tr=256):
    T = dya.shape[0]
    tr = min(tr, T)

    def body(d_ref, o_ref, do_ref):
        lane = lax.broadcasted_iota(jnp.int32, (tr, HEAD_DIM), 1)
        dv, ov = d_ref[...], o_ref[...]
        for h in range(N_HEADS):
            d = dv[:, h * HEAD_DIM:(h + 1) * HEAD_DIM]
            delta = jnp.sum(d * ov[:, h * HEAD_DIM:(h + 1) * HEAD_DIM], axis=1, keepdims=True)
            ext = _with_lanes(jnp.zeros((tr, HEAD_DIM), F32), lane, 0, _split3(-delta))
            do_ref[h] = jnp.concatenate([d, ext], axis=1).astype(BF)

    return pl.pallas_call(
        body, name="attn_bwd_prep", grid=(T // tr,),
        in_specs=[_row_spec(tr, ATTN_W), _row_spec(tr, ATTN_W)],
        out_specs=pl.BlockSpec((N_HEADS, tr, LANES), lambda i: (0, i, 0)),
        out_shape=jax.ShapeDtypeStruct((N_HEADS, T, LANES), BF), compiler_params=_params("parallel"),
    )(dya, of)


def _attn_bwd_dq(ql, do, kl, vl, first, *, tq=ATTN_TILE):
    T = ql.shape[1]
    tq = min(tq, T)
    n = T // tq

    def body(first_ref, ql_ref, do_ref, kl_ref, vl_ref, dq_ref, ext_ref, acc_ref):
        i = pl.program_id(1)
        acc_ref[...] = jnp.zeros_like(acc_ref)

        def block(hh, rows, mask):
            kj = kl_ref[hh, rows, :]
            p = jnp.exp(_dot_nt(ql_ref[hh], kj))
            if mask is not None:
                p = jnp.where(mask, p, 0.0)
            ds = p * _dot_nt(do_ref[hh], vl_ref[hh, rows, :])
            acc_ref[hh] += _dot(ds.astype(BF), kj)

        def step(j, carry):
            for hh in range(2):
                block(hh, _tile_rows(j, tq), None)
            return carry

        lax.fori_loop(first_ref[pl.program_id(0) * n + i].astype(jnp.int32), i, step, 0)
        causal = _causal(tq)
        for hh in range(2):
            block(hh, _tile_rows(i, tq), causal)
        dq_ref[...] = jnp.concatenate([acc_ref[hh][:, :HEAD_DIM] * Q_SCALE for hh in range(2)], axis=1).astype(BF)
        ext_ref[...] = jnp.concatenate([acc_ref[hh][:, HEAD_DIM:] for hh in range(2)], axis=1)

    return _attn_call(
        body, "attn_bwd_dq", (first,), (ql, do, kl, vl),
        [_pair_block(tq), _pair_block(tq), _pair_full(T), _pair_full(T)], [_packed_block(tq), _packed_block(tq)],
        [jax.ShapeDtypeStruct((T, ATTN_W), BF), jax.ShapeDtypeStruct((T, ATTN_W), F32)],
        [pltpu.VMEM((2, tq, LANES), F32)], n)


def _attn_bwd_dkv(kl, vl, ql, do, last, *, tk=ATTN_TILE):
    T = ql.shape[1]
    tk = min(tk, T)
    n = T // tk

    def body(last_ref, kl_ref, vl_ref, ql_ref, do_ref, dk_ref, dv_ref, ext_ref, dk_acc, dv_acc):
        j = pl.program_id(1)
        dk_acc[...] = jnp.zeros_like(dk_acc)
        dv_acc[...] = jnp.zeros_like(dv_acc)

        def block(hh, rows, mask):
            qi, di = ql_ref[hh, rows, :], do_ref[hh, rows, :]
            p_t = jnp.exp(_dot_nt(kl_ref[hh], qi))
            if mask is not None:
                p_t = jnp.where(mask, p_t, 0.0)
            ds_t = p_t * _dot_nt(vl_ref[hh], di)
            dk_acc[hh] += _dot(ds_t.astype(BF), qi)
            dv_acc[hh] += _dot(p_t.astype(BF), di)

        causal_t = _causal(tk, keys_in_rows=True)
        for hh in range(2):
            block(hh, _tile_rows(j, tk), causal_t)

        def step(i, carry):
            for hh in range(2):
                block(hh, _tile_rows(i, tk), None)
            return carry

        lax.fori_loop(j + 1, last_ref[pl.program_id(0) * n + j].astype(jnp.int32) + 1, step, 0)
        dk_ref[...] = jnp.concatenate([dk_acc[hh][:, :HEAD_DIM] for hh in range(2)], axis=1).astype(BF)
        dv_ref[...] = jnp.concatenate([dv_acc[hh][:, :HEAD_DIM] for hh in range(2)], axis=1).astype(BF)
        ext_ref[...] = jnp.concatenate([dk_acc[hh][:, HEAD_DIM:] for hh in range(2)], axis=1)

    return _attn_call(
        body, "attn_bwd_dkv", (last,), (kl, vl, ql, do),
        [_pair_block(tk), _pair_block(tk), _pair_full(T), _pair_full(T)], [_packed_block(tk)] * 3,
        [jax.ShapeDtypeStruct((T, ATTN_W), BF), jax.ShapeDtypeStruct((T, ATTN_W), BF),
         jax.ShapeDtypeStruct((T, ATTN_W), F32)],
        [pltpu.VMEM((2, tk, LANES), F32), pltpu.VMEM((2, tk, LANES), F32)], n)


def _forget_bwd(ext_q, ext_k, fl, b_forget, *, tp=256):
    T = fl.shape[0]
    tp = min(tp, T)
    n = T // tp

    def body(eq_ref, ek_ref, fl_ref, bf_ref, dfl_ref, dbf_ref, carry_ref):
        @pl.when(pl.program_id(0) == 0)
        def _():
            carry_ref[...] = jnp.zeros_like(carry_ref)
            dbf_ref[...] = jnp.zeros_like(dbf_ref)

        lane = lax.broadcasted_iota(jnp.int32, (tp, LANES), 1)
        eq, ek = eq_ref[...], ek_ref[...]
        cols = [eq[:, h * HEAD_DIM:h * HEAD_DIM + 1] - ek[:, h * HEAD_DIM + 3:h * HEAD_DIM + 4] for h in range(N_HEADS)]
        dcum = _with_lanes(jnp.zeros((tp, LANES), F32), lane, 0, cols)
        suffix = _scan_dot(_tri(tp, upper=True), dcum) + carry_ref[...]
        carry_ref[...] = suffix[0:1, :]
        x = fl_ref[...] + bf_ref[...]
        dfl = jnp.where(lane < N_HEADS, suffix / (1.0 + jnp.exp(x)), 0.0)
        dfl_ref[...] = dfl.astype(BF)
        dbf_ref[...] += jnp.sum(dfl, axis=0, keepdims=True)

    rev = lambda w: pl.BlockSpec((tp, w), lambda i: (n - 1 - i, 0))
    return pl.pallas_call(
        body, name="forget_bwd", grid=(n,),
        in_specs=[rev(ATTN_W), rev(ATTN_W), rev(LANES), _vec_spec(LANES)],
        out_specs=[rev(LANES), _vec_spec(LANES)],
        out_shape=[jax.ShapeDtypeStruct((T, LANES), BF), jax.ShapeDtypeStruct((1, LANES), F32)],
        scratch_shapes=[pltpu.VMEM((1, LANES), F32)], compiler_params=_params("arbitrary"),
    )(ext_q, ext_k, fl, b_forget)


def _adamw(w, g, m, v, *, name, tr=256):
    rows, cols = w.shape
    tr = tr if rows % tr == 0 else rows

    def body(w_ref, g_ref, m_ref, v_ref, d_ref, nm_ref, nv_ref):
        gv = g_ref[...]
        nm = ADAM_B1 * m_ref[...] + (1.0 - ADAM_B1) * gv
        nv = ADAM_B2 * v_ref[...] + (1.0 - ADAM_B2) * (gv * gv)
        m_hat = nm / (1.0 - ADAM_B1 ** ADAM_STEP)
        v_hat = nv / (1.0 - ADAM_B2 ** ADAM_STEP)
        d_ref[...] = -ADAM_LR * (m_hat / (jnp.sqrt(v_hat) + ADAM_EPS) + ADAM_WD * w_ref[...])
        nm_ref[...] = nm
        nv_ref[...] = nv

    spec = pl.BlockSpec((tr, cols), lambda i: (i, 0))
    return pl.pallas_call(
        body, name=name, grid=(rows // tr,), in_specs=[spec] * 4, out_specs=[spec] * 3,
        out_shape=[jax.ShapeDtypeStruct((rows, cols), F32)] * 3, compiler_params=_params("parallel"),
    )(w, g, m, v)


HBM = pl.BlockSpec(memory_space=pltpu.HBM)


def _place():
    x, y, c = lax.axis_index("x"), lax.axis_index("y"), lax.axis_index("c")
    others = [(1 - x, y), (x, 1 - y), (1 - x, 1 - y)]
    return x, y, c, others


def _chip(xy):
    return 2 * xy[0] + xy[1]


def _gather_weights(wp):
    R = wp.shape[0]
    Rh = R // 2

    def body(w_ref, g_ref, send_sems, recv_sems):
        x, y, c, others = _place()
        sibling = (x, y, 1 - c)
        mine_rows = pl.ds(pl.multiple_of(c * Rh, 16), Rh)
        sibling_rows = pl.ds(pl.multiple_of((1 - c) * Rh, 16), Rh)

        def copy(k, src, dst, to):
            return pltpu.make_async_remote_copy(src_ref=src, dst_ref=dst, send_sem=send_sems.at[k], recv_sem=recv_sems.at[k],
                                                device_id=to, device_id_type=MESH)

        first = [copy(j, w_ref.at[mine_rows, :], g_ref.at[_chip((x, y)), mine_rows, :], (*o, c)) for j, o in enumerate(others)]
        for cp in first:
            cp.start()
        passed = []
        for j, o in enumerate(others):
            landed = g_ref.at[_chip(o), mine_rows, :]
            copy(j, landed, landed, (*o, c)).wait_recv()
            passed.append(copy(3 + j, landed, landed, sibling))
            passed[-1].start()
        for j, o in enumerate(others):
            landed = g_ref.at[_chip(o), sibling_rows, :]
            copy(3 + j, landed, landed, sibling).wait_recv()
        for cp in first + passed:
            cp.wait_send()

    gathered = pl.pallas_call(
        body, name="gather_weights", in_specs=[HBM], out_specs=HBM,
        out_shape=jax.ShapeDtypeStruct((N_CHIPS, R, LANES), wp.dtype),
        scratch_shapes=[pltpu.SemaphoreType.DMA((6,)), pltpu.SemaphoreType.DMA((6,))],
    )(wp)
    own_slot = 2 * lax.axis_index("x") + lax.axis_index("y")
    return lax.dynamic_update_slice(gathered, wp[None], (own_slot, 0, 0))


def _exchange_halves(gf, sf):
    Rh = gf.shape[1] // 2
    Rsh = sf.shape[0] // 2

    def body(g_ref, s_ref, rg_ref, rs_ref, send_sems, recv_sems):
        x, y, c, _ = _place()
        sibling = (x, y, 1 - c)
        big = pltpu.make_async_remote_copy(
            src_ref=g_ref.at[:, pl.ds(pl.multiple_of((1 - c) * Rh, 8), Rh), :], dst_ref=rg_ref,
            send_sem=send_sems.at[0], recv_sem=recv_sems.at[0], device_id=sibling, device_id_type=MESH)
        small = pltpu.make_async_remote_copy(
            src_ref=s_ref.at[pl.ds(pl.multiple_of((1 - c) * Rsh, 8), Rsh), :], dst_ref=rs_ref,
            send_sem=send_sems.at[1], recv_sem=recv_sems.at[1], device_id=sibling, device_id_type=MESH)
        big.start()
        small.start()
        big.wait()
        small.wait()

    return pl.pallas_call(
        body, name="exchange_halves", in_specs=[HBM, HBM], out_specs=[HBM, HBM],
        out_shape=[jax.ShapeDtypeStruct((N_CHIPS, Rh, LANES), F32), jax.ShapeDtypeStruct((Rsh, LANES), F32)],
        scratch_shapes=[pltpu.SemaphoreType.DMA((2,)), pltpu.SemaphoreType.DMA((2,))],
    )(gf, sf)


def _scatter_to_owners(cab, csa):
    Rh = cab.shape[1]
    Rsh = csa.shape[0]

    def body(b_ref, s_ref, rb_ref, rs_ref, send_sems, recv_sems):
        x, y, c, others = _place()
        me = _chip((x, y))
        sends = []
        for j, o in enumerate(others):
            sends.append(pltpu.make_async_remote_copy(
                src_ref=b_ref.at[_chip(o)], dst_ref=rb_ref.at[me], send_sem=send_sems.at[j], recv_sem=recv_sems.at[j],
                device_id=(*o, c), device_id_type=MESH))
            sends.append(pltpu.make_async_remote_copy(
                src_ref=s_ref, dst_ref=rs_ref.at[me], send_sem=send_sems.at[3 + j], recv_sem=recv_sems.at[3 + j],
                device_id=(*o, c), device_id_type=MESH))
        for cp in sends:
            cp.start()
        for j, o in enumerate(others):
            pltpu.make_async_remote_copy(
                src_ref=b_ref.at[me], dst_ref=rb_ref.at[_chip(o)], send_sem=send_sems.at[j], recv_sem=recv_sems.at[j],
                device_id=(*o, c), device_id_type=MESH).wait_recv()
            pltpu.make_async_remote_copy(
                src_ref=s_ref, dst_ref=rs_ref.at[_chip(o)], send_sem=send_sems.at[3 + j], recv_sem=recv_sems.at[3 + j],
                device_id=(*o, c), device_id_type=MESH).wait_recv()
        for cp in sends:
            cp.wait_send()

    return pl.pallas_call(
        body, name="scatter_to_owners", in_specs=[HBM, HBM], out_specs=[HBM, HBM],
        out_shape=[jax.ShapeDtypeStruct((N_CHIPS, Rh, LANES), BF), jax.ShapeDtypeStruct((N_CHIPS, Rsh, LANES), F32)],
        scratch_shapes=[pltpu.SemaphoreType.DMA((6,)), pltpu.SemaphoreType.DMA((6,))],
    )(cab, csa)


def _join_halves(tb, ts):
    def body(b_ref, s_ref, gb_ref, gs_ref, send_sems, recv_sems):
        x, y, c, _ = _place()
        sibling = (x, y, 1 - c)
        big = pltpu.make_async_remote_copy(src_ref=b_ref, dst_ref=gb_ref, send_sem=send_sems.at[0],
                                           recv_sem=recv_sems.at[0], device_id=sibling, device_id_type=MESH)
        small = pltpu.make_async_remote_copy(src_ref=s_ref, dst_ref=gs_ref, send_sem=send_sems.at[1],
                                             recv_sem=recv_sems.at[1], device_id=sibling, device_id_type=MESH)
        big.start()
        small.start()
        big.wait()
        small.wait()

    other_b, other_s = pl.pallas_call(
        body, name="join_halves", in_specs=[HBM, HBM], out_specs=[HBM, HBM],
        out_shape=[jax.ShapeDtypeStruct(tb.shape, F32), jax.ShapeDtypeStruct(ts.shape, F32)],
        scratch_shapes=[pltpu.SemaphoreType.DMA((2,)), pltpu.SemaphoreType.DMA((2,))],
    )(tb, ts)
    core = lax.axis_index("c")

    def in_row_order(mine, other):
        rows = mine.shape[0]
        out = lax.dynamic_update_slice(jnp.zeros((2 * rows, LANES), F32), mine, (core * rows, 0))
        return lax.dynamic_update_slice(out, other, ((1 - core) * rows, 0))

    return in_row_order(tb, other_b), in_row_order(ts, other_s)


def _row_tile(rows, cap=1152, mult=16):
    return max(t for t in range(mult, min(rows, cap) + 1, mult) if rows % t == 0)


def _add_sibling(gf, rg, sf, rs, core):
    Rh = rg.shape[1]
    Rsh = rs.shape[0]
    tr = _row_tile(Rh)
    nb = Rh // tr

    def big_body(core_ref, g_ref, r_ref, o_ref, ob_ref):
        s = g_ref[...] + r_ref[...]
        o_ref[...] = s
        ob_ref[...] = s.astype(BF)

    spec = pl.BlockSpec((N_CHIPS, tr, LANES), lambda i, core_ref: (0, i, 0))
    ca, cab = pl.pallas_call(
        big_body, name="add_sibling",
        grid_spec=pltpu.PrefetchScalarGridSpec(
            num_scalar_prefetch=1, grid=(nb,),
            in_specs=[pl.BlockSpec((N_CHIPS, tr, LANES), lambda i, core_ref: (0, core_ref[0] * nb + i, 0)), spec],
            out_specs=[spec, spec]),
        out_shape=[jax.ShapeDtypeStruct((N_CHIPS, Rh, LANES), F32), jax.ShapeDtypeStruct((N_CHIPS, Rh, LANES), BF)],
        compiler_params=_params("parallel"),
    )(core, gf, rg)

    def small_body(core_ref, s_ref, r_ref, o_ref):
        o_ref[...] = s_ref[...] + r_ref[...]

    sspec = pl.BlockSpec((Rsh, LANES), lambda i, core_ref: (0, 0))
    csa = pl.pallas_call(
        small_body, name="add_sibling_small",
        grid_spec=pltpu.PrefetchScalarGridSpec(
            num_scalar_prefetch=1, grid=(1,),
            in_specs=[pl.BlockSpec((Rsh, LANES), lambda i, core_ref: (core_ref[0], 0)), sspec], out_specs=sspec),
        out_shape=jax.ShapeDtypeStruct((Rsh, LANES), F32), compiler_params=_params("arbitrary"),
    )(core, sf, rs)
    return ca, cab, csa


def _add_chips(ca, rb, csa, rsb, chip):
    Rh = ca.shape[1]
    tr = _row_tile(Rh)
    Rsh = rsb.shape[1]

    def written(k, chip_ref):
        return jnp.where(chip_ref[0] == k, (k + 1) % N_CHIPS, k)

    def big_body(chip_ref, own_ref, *refs):
        o_ref = refs[N_CHIPS]
        acc = own_ref[0]
        for k in range(N_CHIPS):
            acc = acc + jnp.where(chip_ref[0] == k, 0.0, refs[k][0].astype(F32))
        o_ref[...] = acc

    tb = pl.pallas_call(
        big_body, name="add_chips",
        grid_spec=pltpu.PrefetchScalarGridSpec(
            num_scalar_prefetch=1, grid=(Rh // tr,),
            in_specs=[pl.BlockSpec((1, tr, LANES), lambda i, chip_ref: (chip_ref[0], i, 0))]
            + [pl.BlockSpec((1, tr, LANES), functools.partial(lambda i, chip_ref, k: (written(k, chip_ref), i, 0), k=k))
               for k in range(N_CHIPS)],
            out_specs=pl.BlockSpec((tr, LANES), lambda i, chip_ref: (i, 0))),
        out_shape=jax.ShapeDtypeStruct((Rh, LANES), F32), compiler_params=_params("parallel"),
    )(chip, ca, *([rb] * N_CHIPS))

    def small_body(chip_ref, own_ref, *refs):
        o_ref = refs[N_CHIPS]
        terms = [jnp.where(chip_ref[0] == k, own_ref[...], refs[k][0]) for k in range(N_CHIPS)]
        o_ref[...] = ((terms[0] + terms[1]) + terms[2]) + terms[3]

    ts = pl.pallas_call(
        small_body, name="add_chips_small",
        grid_spec=pltpu.PrefetchScalarGridSpec(
            num_scalar_prefetch=1, grid=(1,),
            in_specs=[pl.BlockSpec((Rsh, LANES), lambda i, chip_ref: (0, 0))]
            + [pl.BlockSpec((1, Rsh, LANES), functools.partial(lambda i, chip_ref, k: (written(k, chip_ref), 0, 0), k=k))
               for k in range(N_CHIPS)],
            out_specs=pl.BlockSpec((Rsh, LANES), lambda i, chip_ref: (0, 0))),
        out_shape=jax.ShapeDtypeStruct((Rsh, LANES), F32), compiler_params=_params("arbitrary"),
    )(chip, csa, *([rsb] * N_CHIPS))
    return tb, ts


SHARDED = (("w_in", (D_MODEL, 4616), 1), ("w_branch_sgu", (SGU_W, D_MODEL), 1), ("w_branch_attn", (ATTN_W, D_MODEL), 1),
           ("w_out", (D_MODEL, D_MODEL), 0), ("w_up", (D_MODEL, D_FF), 1), ("w_down", (D_FF, D_MODEL), 0))
SMALL = (("g_mix_pre", (1, D_MODEL)), ("b_forget", (1, N_HEADS)), ("g_sgu", (1, SGU_W)), ("b_sgu", (1, SGU_W)),
         ("w_spatial", (N_GROUPS * CHUNK, CHUNK)), ("b_spatial", (N_GROUPS, CHUNK)), ("g_mix_post", (1, D_MODEL)),
         ("g_ffn_pre", (1, D_MODEL)), ("g_ffn_post", (1, D_MODEL)))
PACK_ALIGN = 256


def _shard_shape(shape, axis):
    return tuple(s // N_CHIPS if a == axis else s for a, s in enumerate(shape))


def _padded_rows(rows):
    return -(-rows // PACK_ALIGN) * PACK_ALIGN


def _pack_rows(parts, axis):
    rows = sum(p.shape[axis] for p in parts)
    pad = _padded_rows(rows) - rows
    if pad:
        shape = list(parts[0].shape)
        shape[axis] = pad
        parts = list(parts) + [jnp.zeros(shape, parts[0].dtype)]
    return jnp.concatenate(parts, axis=axis)


def _pack_shards(shards, dtype):
    return _pack_rows([shards[name].astype(dtype).reshape(-1, LANES) for name, _, _ in SHARDED], 0)


def _unpack_shards(packed):
    out, row = {}, 0
    for name, shape, axis in SHARDED:
        sshape = _shard_shape(shape, axis)
        n = sshape[0] * sshape[1] // LANES
        if packed.ndim == 2:
            out[name] = packed[row:row + n].reshape(sshape)
        else:
            parts = packed[:, row:row + n].reshape((N_CHIPS,) + sshape)
            out[name] = parts.reshape(shape) if axis == 0 else parts.transpose(1, 0, 2).reshape(shape)
        row += n
    return out


def _pack_full_grads(grads):
    parts = []
    for name, shape, axis in SHARDED:
        g = grads[name]
        sshape = _shard_shape(shape, axis)
        if axis == 0:
            g = g.reshape((N_CHIPS,) + sshape)
        else:
            g = g.reshape(shape[0], N_CHIPS, sshape[1]).transpose(1, 0, 2)
        parts.append(g.reshape(N_CHIPS, -1, LANES))
    return _pack_rows(parts, 1)


def _small_rows(shape):
    return -(-(shape[0] * shape[1]) // (8 * LANES)) * 8


def _pack_small(values):
    parts = []
    for name, shape in SMALL:
        flat = values[name].reshape(-1)
        n = _small_rows(shape)
        parts.append(jnp.pad(flat, (0, n * LANES - flat.shape[0])).reshape(n, LANES))
    return _pack_rows(parts, 0)


def _unpack_small(packed):
    out, row = {}, 0
    for name, shape in SMALL:
        n = _small_rows(shape)
        out[name] = packed[row:row + n].reshape(-1)[:shape[0] * shape[1]].reshape(shape)
        row += n
    return out


IN_Z, IN_Q, IN_K, IN_V, IN_F, IN_G, IN_END = 0, 1024, 1536, 2048, 2560, 2568, 4616


def _local_step(x, target, w, small):
    w_in = w["w_in"]
    w_z, w_qkv, w_g = w_in[:, IN_Z:IN_Q], w_in[:, IN_Q:IN_F], w_in[:, IN_G:IN_END]
    w_q, w_k, w_v = w_in[:, IN_Q:IN_K], w_in[:, IN_K:IN_V], w_in[:, IN_V:IN_F]
    w_f = jnp.pad(w_in[:, IN_F:IN_G], ((0, 0), (0, LANES - N_HEADS)))
    b_forget = jnp.pad(small["b_forget"], ((0, 0), (0, LANES - N_HEADS)))
    causal = jnp.tril(jnp.ones((CHUNK, CHUNK), bool))
    ws = jnp.where(causal[None], small["w_spatial"].reshape(N_GROUPS, CHUNK, CHUNK), 0.0).astype(BF)
    ws_t = ws.transpose(0, 2, 1)
    bias_plane = jnp.repeat(small["b_spatial"].T, HEAD_DIM, axis=1)

    xn = _rms_fwd(x, small["g_mix_pre"])
    z = _matmul([(xn, w_z)], nt=False, out_dtypes=[F32], tm=512, tn=512, name="proj_z")
    qkv = _matmul([(xn, w_qkv)], nt=False, out_dtypes=[BF], tm=512, tn=512, name="proj_qkv")
    gl = _matmul([(xn, w_g)], nt=False, out_dtypes=[F32], tm=512, tn=512, name="proj_gate")
    fl = _matmul([(xn, w_f)], nt=False, out_dtypes=[F32], tm=512, tn=LANES, name="proj_forget")
    ysgu = _sgu_fwd(z, small["g_sgu"], small["b_sgu"], ws, bias_plane)
    qf, kl, vl, tile_stats = _attn_prep(qkv, fl, b_forget)
    first_key_tile, last_query_tile, bounded = _attn_ranges(tile_stats)
    yattn, yattn_f, ql = _attn_fwd(qf, kl, vl, first_key_tile, bounded)
    a, b, merged = _branch_merge(ysgu, yattn, w["w_branch_sgu"], w["w_branch_attn"], gl)
    o = _matmul([(merged, w["w_out"])], nt=False, out_dtypes=[F32], tm=512, tn=512, name="proj_out")
    h1, xn2 = _mixer_out_fwd(o, x, small["g_mix_post"], small["g_ffn_pre"])

    def relu2(acc):
        r = jnp.maximum(acc, 0.0)
        return r * r, r

    hid, relu = _matmul([(xn2, w["w_up"])], nt=False, out_dtypes=[BF, BF], tm=512, tn=512, name="ffn_up", epilogue=relu2)
    dn = _matmul([(hid, w["w_down"])], nt=False, out_dtypes=[F32], tm=512, tn=512, name="ffn_down")
    sq, dy, ddn, dg_ffn_post = _loss_head(dn, h1, target, small["g_ffn_post"])

    dup = _matmul([(ddn, w["w_down"])], nt=True, out_dtypes=[BF], tm=512, tn=512, name="ffn_down_bwd",
                  epilogue=lambda acc, r: (acc * (2.0 * r.astype(F32)),), extras=[relu])
    dw_down = _matmul_tn(hid, ddn, name="dw_down")
    dxn2 = _matmul([(dup, w["w_up"])], nt=True, out_dtypes=[F32], tm=512, tn=512, name="ffn_up_bwd")
    dw_up = _matmul_tn(xn2, dup, name="dw_up")
    dh1, do, dg_ffn_pre, dg_mix_post = _mixer_out_bwd(h1, dxn2, dy, o, small["g_ffn_pre"], small["g_mix_post"])

    dmerged = _matmul([(do, w["w_out"])], nt=True, out_dtypes=[F32], tm=512, tn=512, name="proj_out_bwd")
    dw_out = _matmul_tn(merged, do, name="dw_out")
    da, db, dgla, dglb = _gate_bwd(dmerged, a, b, gl)
    dysgu = _matmul([(da, w["w_branch_sgu"])], nt=True, out_dtypes=[F32], tm=512, tn=512, name="branch_sgu_bwd")
    dyattn = _matmul([(db, w["w_branch_attn"])], nt=True, out_dtypes=[F32], tm=512, tn=512, name="branch_attn_bwd")
    dw_bs = _matmul_tn(ysgu, da, name="dw_branch_sgu")
    dw_ba = _matmul_tn(yattn, db, name="dw_branch_attn")
    dz, dws, dbs, dg_sgu, db_sgu = _sgu_bwd(dysgu, z, small["g_sgu"], small["b_sgu"], ws, ws_t, bias_plane)
    dout = _attn_bwd_prep(dyattn, yattn_f)
    dq, ext_q = _attn_bwd_dq(ql, dout, kl, vl, first_key_tile)
    dk, dv, ext_k = _attn_bwd_dkv(kl, vl, ql, dout, last_query_tile)
    dfl, dbf = _forget_bwd(ext_q, ext_k, fl, b_forget)
    dxn = _matmul([(dz, w_z), (dq, w_q), (dk, w_k), (dv, w_v), (dgla, w_g[:, :D_MODEL]), (dglb, w_g[:, D_MODEL:]), (dfl, w_f)],
                  nt=True, out_dtypes=[F32], tm=512, tn=512, name="proj_in_bwd")
    dw_in = jnp.concatenate(
        [_matmul_tn(xn, dz, name="dw_in_z"), _matmul_tn(xn, dq, name="dw_in_q"), _matmul_tn(xn, dk, name="dw_in_k"),
         _matmul_tn(xn, dv, name="dw_in_v"), _matmul_tn(xn, dfl, name="dw_in_f")[:, :N_HEADS],
         _matmul_tn(xn, dgla, name="dw_in_ga"), _matmul_tn(xn, dglb, name="dw_in_gb")], axis=1)
    dx, dg_mix_pre = _input_norm_bwd(x, dxn, dh1, small["g_mix_pre"])

    grads = {"w_in": dw_in, "w_branch_sgu": dw_bs, "w_branch_attn": dw_ba, "w_out": dw_out, "w_up": dw_up, "w_down": dw_down}
    small_grads = {"g_mix_pre": dg_mix_pre, "b_forget": dbf[:, :N_HEADS], "g_sgu": dg_sgu, "b_sgu": db_sgu,
                   "w_spatial": dws.reshape(N_GROUPS * CHUNK, CHUNK), "b_spatial": dbs[:, :N_GROUPS].T,
                   "g_mix_post": dg_mix_post, "g_ffn_pre": dg_ffn_pre, "g_ffn_post": dg_ffn_post}
    return sq, dx, grads, small_grads


NAMES = ("g_mix_pre", "w_in", "b_forget", "g_sgu", "b_sgu", "w_spatial", "b_spatial", "w_branch_sgu", "w_branch_attn",
         "w_out", "g_mix_post", "g_ffn_pre", "w_up", "w_down", "g_ffn_post")


def kernel(x, g_mix_pre, w_in, b_forget, g_sgu, b_sgu, w_spatial, b_spatial, w_branch_sgu, w_branch_attn, w_out, g_mix_post, g_ffn_pre, w_up, w_down, g_ffn_post, loss_target, m_g_mix_pre, m_w_in, m_b_forget, m_g_sgu, m_b_sgu, m_w_spatial, m_b_spatial, m_w_branch_sgu, m_w_branch_attn, m_w_out, m_g_mix_post, m_g_ffn_pre, m_w_up, m_w_down, m_g_ffn_post, v_g_mix_pre, v_w_in, v_b_forget, v_g_sgu, v_b_sgu, v_w_spatial, v_b_spatial, v_w_branch_sgu, v_w_branch_attn, v_w_out, v_g_mix_post, v_g_ffn_pre, v_w_up, v_w_down, v_g_ffn_post):
    weights = dict(zip(NAMES, (g_mix_pre, w_in, b_forget, g_sgu, b_sgu, w_spatial, b_spatial, w_branch_sgu, w_branch_attn,
                               w_out, g_mix_post, g_ffn_pre, w_up, w_down, g_ffn_post), strict=True))
    first = dict(zip(NAMES, (m_g_mix_pre, m_w_in, m_b_forget, m_g_sgu, m_b_sgu, m_w_spatial, m_b_spatial, m_w_branch_sgu,
                             m_w_branch_attn, m_w_out, m_g_mix_post, m_g_ffn_pre, m_w_up, m_w_down, m_g_ffn_post), strict=True))
    second = dict(zip(NAMES, (v_g_mix_pre, v_w_in, v_b_forget, v_g_sgu, v_b_sgu, v_w_spatial, v_b_spatial, v_w_branch_sgu,
                              v_w_branch_attn, v_w_out, v_g_mix_post, v_g_ffn_pre, v_w_up, v_w_down, v_g_ffn_post), strict=True))
    shard_shapes = {name: _shard_shape(shape, axis) for name, shape, axis in SHARDED}
    small_shapes = dict(SMALL)
    view = lambda name, a: a.reshape(shard_shapes.get(name) or small_shapes[name])

    core = lax.axis_index("c").astype(jnp.int32).reshape(1)
    chip = (2 * lax.axis_index("x") + lax.axis_index("y")).astype(jnp.int32).reshape(1)

    shards = {name: view(name, weights[name]) for name, _, _ in SHARDED}
    full = _unpack_shards(_gather_weights(_pack_shards(shards, BF)))
    small = {name: view(name, weights[name]) for name, _ in SMALL}

    sq, dx, grads, small_grads = _local_step(x[0], loss_target[0], full, small)
    loss = lax.psum(0.5 * jnp.sum(sq) / D_MODEL, ("x", "y", "c"))

    gf = _pack_full_grads(grads)
    sf = _pack_small(small_grads)
    rg, rs = _exchange_halves(gf, sf)
    ca, cab, csa = _add_sibling(gf, rg, sf, rs, core)
    rb, rsb = _scatter_to_owners(cab, csa)
    tb, ts = _add_chips(ca, rb, csa, rsb, chip)
    g_packed, s_packed = _join_halves(tb, ts)
    grad = {**_unpack_shards(g_packed), **_unpack_small(s_packed)}

    delta, new_m, new_v = {}, {}, {}
    for name in NAMES:
        delta[name], new_m[name], new_v[name] = _adamw(
            view(name, weights[name]), grad[name], view(name, first[name]), view(name, second[name]), name="adamw_" + name)

    like = lambda d: [d[name].reshape(weights[name].shape) for name in NAMES]
    return (loss, dx[None], *like(grad), *like(delta), *like(new_m), *like(new_v))
```

```python
import functools

import jax
import jax.numpy as jnp
from jax import lax
from jax.experimental import pallas as pl
from jax.experimental.pallas import tpu as pltpu

F32 = jnp.float32
BF = jnp.bfloat16
MESH = pl.DeviceIdType.MESH

D_MODEL = 1024
N_HEADS = 8
HEAD_DIM = 64
ATTN_W = N_HEADS * HEAD_DIM
SGU_W = 512
N_GROUPS = 8
CHUNK = 128
D_FF = 4096
EPS = 1e-6
Q_SCALE = HEAD_DIM ** -0.5
N_CHIPS = 4
LANES = 128

ADAM_LR = 0.001
ADAM_B1 = 0.9
ADAM_B2 = 0.999
ADAM_EPS = 1e-08
ADAM_WD = 0.01
ADAM_STEP = 10

VMEM_LIMIT = 48 * 1024 * 1024
NEG = -1e30

LANE_ROWSUM = HEAD_DIM
LANE_COLSUM = HEAD_DIM + 3


def _params(*sem):
    return pltpu.CompilerParams(dimension_semantics=sem, vmem_limit_bytes=VMEM_LIMIT)


def _dot(a, b):
    return jnp.dot(a, b, preferred_element_type=F32)


def _dot_nt(a, b):
    return lax.dot_general(a, b, (((1,), (1,)), ((), ())), preferred_element_type=F32)


def _dot_tn(a, b):
    return lax.dot_general(a, b, (((0,), (0,)), ((), ())), preferred_element_type=F32)


def _split3(c):
    hi = c.astype(BF).astype(F32)
    r = c - hi
    mid = r.astype(BF).astype(F32)
    lo = (r - mid).astype(BF).astype(F32)
    return hi, mid, lo


def _gelu(x):
    k = 0.7978845608028654
    return 0.5 * x * (1.0 + jnp.tanh(k * (x + 0.044715 * (x * x * x))))


def _gelu_grad(x):
    k = 0.7978845608028654
    x2 = x * x
    t = jnp.tanh(k * (x + 0.044715 * (x2 * x)))
    return 0.5 * (1.0 + t) + 0.5 * x * (1.0 - t * t) * (k * (1.0 + 3.0 * 0.044715 * x2))


def _rms_bwd(a, g, dy):
    r = lax.rsqrt(jnp.mean(a * a, axis=-1, keepdims=True) + EPS)
    n = a * r
    dn = dy * g
    da = r * (dn - n * jnp.mean(dn * n, axis=-1, keepdims=True))
    return da, dy * n


MM_ROWS = 1024
MM_COLS = 512


def _matmul(pairs, *, nt, out_dtypes, name, tm=MM_ROWS, tn=MM_COLS, epilogue=None, extras=()):
    n_pairs = len(pairs)
    n_extra = len(extras)
    M = pairs[0][0].shape[0]
    N = pairs[0][1].shape[0] if nt else pairs[0][1].shape[1]
    tm, tn = min(tm, M), min(tn, N)
    assert M % tm == 0 and N % tn == 0

    def body(*refs):
        acc = None
        for p in range(n_pairs):
            a_ref, b_ref = refs[2 * p], refs[2 * p + 1]
            d = _dot_nt(a_ref[...], b_ref[...]) if nt else _dot(a_ref[...], b_ref[...])
            acc = d if acc is None else acc + d
        e_refs = refs[2 * n_pairs:2 * n_pairs + n_extra]
        o_refs = refs[2 * n_pairs + n_extra:]
        outs = (acc,) if epilogue is None else epilogue(acc, *[e[...] for e in e_refs])
        for o_ref, o in zip(o_refs, outs, strict=True):
            o_ref[...] = o.astype(o_ref.dtype)

    in_specs, args = [], []
    for a, b in pairs:
        K = a.shape[1]
        in_specs.append(pl.BlockSpec((tm, K), lambda i, j: (i, 0)))
        in_specs.append(pl.BlockSpec((tn, K), lambda i, j: (j, 0)) if nt else pl.BlockSpec((K, tn), lambda i, j: (0, j)))
        args += [a, b]
    for e in extras:
        in_specs.append(pl.BlockSpec((tm, tn), lambda i, j: (i, j)))
        args.append(e)
    outs = pl.pallas_call(
        body, name=name, grid=(M // tm, N // tn), in_specs=in_specs,
        out_specs=[pl.BlockSpec((tm, tn), lambda i, j: (i, j)) for _ in out_dtypes],
        out_shape=[jax.ShapeDtypeStruct((M, N), dt) for dt in out_dtypes],
        compiler_params=_params("parallel", "parallel"),
    )(*args)
    return outs if len(outs) > 1 else outs[0]


def _matmul_tn(a, b, *, name, tm=1024, tn=1024, tk=512):
    T, K1 = a.shape
    N = b.shape[1]
    tm, tn, tk = min(tm, K1), min(tn, N), min(tk, T)
    assert K1 % tm == 0 and N % tn == 0 and T % tk == 0

    def body(a_ref, b_ref, o_ref):
        @pl.when(pl.program_id(2) == 0)
        def _():
            o_ref[...] = jnp.zeros_like(o_ref)

        o_ref[...] += _dot_tn(a_ref[...], b_ref[...])

    return pl.pallas_call(
        body, name=name, grid=(K1 // tm, N // tn, T // tk),
        in_specs=[pl.BlockSpec((tk, tm), lambda i, j, k: (k, i)), pl.BlockSpec((tk, tn), lambda i, j, k: (k, j))],
        out_specs=pl.BlockSpec((tm, tn), lambda i, j, k: (i, j)),
        out_shape=jax.ShapeDtypeStruct((K1, N), F32),
        compiler_params=_params("parallel", "parallel", "arbitrary"),
    )(a, b)


def _branch_merge(ysgu, yattn, w_bs, w_ba, gl, *, tm=MM_ROWS, tn=MM_COLS):
    T = ysgu.shape[0]
    tm = min(tm, T)
    nj = D_MODEL // tn

    def body(ys_ref, ya_ref, wbs_ref, wba_ref, gla_ref, glb_ref, a_ref, b_ref, m_ref):
        a = _dot(ys_ref[...], wbs_ref[...])
        b = _dot(ya_ref[...], wba_ref[...])
        a_ref[...] = a
        b_ref[...] = b
        m_ref[...] = (jax.nn.sigmoid(gla_ref[...]) * a + jax.nn.sigmoid(glb_ref[...]) * b).astype(BF)

    return pl.pallas_call(
        body, name="branch_merge", grid=(T // tm, nj),
        in_specs=[
            pl.BlockSpec((tm, SGU_W), lambda i, j: (i, 0)),
            pl.BlockSpec((tm, ATTN_W), lambda i, j: (i, 0)),
            pl.BlockSpec((SGU_W, tn), lambda i, j: (0, j)),
            pl.BlockSpec((ATTN_W, tn), lambda i, j: (0, j)),
            pl.BlockSpec((tm, tn), lambda i, j: (i, j)),
            pl.BlockSpec((tm, tn), lambda i, j: (i, j + nj)),
        ],
        out_specs=[pl.BlockSpec((tm, tn), lambda i, j: (i, j))] * 3,
        out_shape=[jax.ShapeDtypeStruct((T, D_MODEL), F32), jax.ShapeDtypeStruct((T, D_MODEL), F32),
                   jax.ShapeDtypeStruct((T, D_MODEL), BF)],
        compiler_params=_params("parallel", "parallel"),
    )(ysgu, yattn, w_bs, w_ba, gl, gl)


def _row_spec(tr, width):
    return pl.BlockSpec((tr, width), lambda i: (i, 0))


def _vec_spec(width):
    return pl.BlockSpec((1, width), lambda i: (0, 0))


def _rms_fwd(x, g, *, tr=256):
    T = x.shape[0]
    tr = min(tr, T)

    def body(x_ref, g_ref, o_ref):
        xv = x_ref[...]
        r = lax.rsqrt(jnp.mean(xv * xv, axis=-1, keepdims=True) + EPS)
        o_ref[...] = ((xv * r) * g_ref[...]).astype(BF)

    return pl.pallas_call(
        body, name="rms_fwd", grid=(T // tr,),
        in_specs=[_row_spec(tr, D_MODEL), _vec_spec(D_MODEL)], out_specs=_row_spec(tr, D_MODEL),
        out_shape=jax.ShapeDtypeStruct((T, D_MODEL), BF), compiler_params=_params("parallel"),
    )(x, g)


def _mixer_out_fwd(o, x, g_post, g_pre, *, tr=256):
    T = x.shape[0]
    tr = min(tr, T)

    def body(o_ref, x_ref, gpost_ref, gpre_ref, h1_ref, xn2_ref):
        ov = o_ref[...]
        r = lax.rsqrt(jnp.mean(ov * ov, axis=-1, keepdims=True) + EPS)
        h1 = x_ref[...] + (ov * r) * gpost_ref[...]
        h1_ref[...] = h1
        r2 = lax.rsqrt(jnp.mean(h1 * h1, axis=-1, keepdims=True) + EPS)
        xn2_ref[...] = ((h1 * r2) * gpre_ref[...]).astype(BF)

    return pl.pallas_call(
        body, name="mixer_out_fwd", grid=(T // tr,),
        in_specs=[_row_spec(tr, D_MODEL), _row_spec(tr, D_MODEL), _vec_spec(D_MODEL), _vec_spec(D_MODEL)],
        out_specs=[_row_spec(tr, D_MODEL), _row_spec(tr, D_MODEL)],
        out_shape=[jax.ShapeDtypeStruct((T, D_MODEL), F32), jax.ShapeDtypeStruct((T, D_MODEL), BF)],
        compiler_params=_params("parallel"),
    )(o, x, g_post, g_pre)


def _loss_head(dn, h1, target, g_post, *, tr=256):
    T = dn.shape[0]
    tr = min(tr, T)

    def body(dn_ref, h1_ref, t_ref, g_ref, sq_ref, dy_ref, ddn_ref, dg_ref):
        @pl.when(pl.program_id(0) == 0)
        def _():
            sq_ref[...] = jnp.zeros_like(sq_ref)
            dg_ref[...] = jnp.zeros_like(dg_ref)

        a = dn_ref[...]
        g = g_ref[...]
        r = lax.rsqrt(jnp.mean(a * a, axis=-1, keepdims=True) + EPS)
        err = h1_ref[...] + (a * r) * g - t_ref[...]
        sq_ref[...] += jnp.sum(err * err, axis=0, keepdims=True)
        dy = err * (1.0 / D_MODEL)
        dy_ref[...] = dy
        da, dgp = _rms_bwd(a, g, dy)
        ddn_ref[...] = da.astype(BF)
        dg_ref[...] += jnp.sum(dgp, axis=0, keepdims=True)

    return pl.pallas_call(
        body, name="loss_head", grid=(T // tr,),
        in_specs=[_row_spec(tr, D_MODEL)] * 3 + [_vec_spec(D_MODEL)],
        out_specs=[_vec_spec(D_MODEL), _row_spec(tr, D_MODEL), _row_spec(tr, D_MODEL), _vec_spec(D_MODEL)],
        out_shape=[jax.ShapeDtypeStruct((1, D_MODEL), F32), jax.ShapeDtypeStruct((T, D_MODEL), F32),
                   jax.ShapeDtypeStruct((T, D_MODEL), BF), jax.ShapeDtypeStruct((1, D_MODEL), F32)],
        compiler_params=_params("arbitrary"),
    )(dn, h1, target, g_post)


def _mixer_out_bwd(h1, dxn2, dy, o, g_pre, g_post, *, tr=256):
    T = h1.shape[0]
    tr = min(tr, T)

    def body(h1_ref, dxn2_ref, dy_ref, o_ref, gpre_ref, gpost_ref, dh1_ref, do_ref, dgpre_ref, dgpost_ref):
        @pl.when(pl.program_id(0) == 0)
        def _():
            dgpre_ref[...] = jnp.zeros_like(dgpre_ref)
            dgpost_ref[...] = jnp.zeros_like(dgpost_ref)

        da, dgp = _rms_bwd(h1_ref[...], gpre_ref[...], dxn2_ref[...])
        dh1 = dy_ref[...] + da
        dh1_ref[...] = dh1
        dgpre_ref[...] += jnp.sum(dgp, axis=0, keepdims=True)
        do, dgp2 = _rms_bwd(o_ref[...], gpost_ref[...], dh1)
        do_ref[...] = do.astype(BF)
        dgpost_ref[...] += jnp.sum(dgp2, axis=0, keepdims=True)

    return pl.pallas_call(
        body, name="mixer_out_bwd", grid=(T // tr,),
        in_specs=[_row_spec(tr, D_MODEL)] * 4 + [_vec_spec(D_MODEL)] * 2,
        out_specs=[_row_spec(tr, D_MODEL), _row_spec(tr, D_MODEL), _vec_spec(D_MODEL), _vec_spec(D_MODEL)],
        out_shape=[jax.ShapeDtypeStruct((T, D_MODEL), F32), jax.ShapeDtypeStruct((T, D_MODEL), BF),
                   jax.ShapeDtypeStruct((1, D_MODEL), F32), jax.ShapeDtypeStruct((1, D_MODEL), F32)],
        compiler_params=_params("arbitrary"),
    )(h1, dxn2, dy, o, g_pre, g_post)


def _input_norm_bwd(x, dxn, dh1, g, *, tr=256):
    T = x.shape[0]
    tr = min(tr, T)

    def body(x_ref, dxn_ref, dh1_ref, g_ref, dx_ref, dg_ref):
        @pl.when(pl.program_id(0) == 0)
        def _():
            dg_ref[...] = jnp.zeros_like(dg_ref)

        da, dgp = _rms_bwd(x_ref[...], g_ref[...], dxn_ref[...])
        dx_ref[...] = dh1_ref[...] + da
        dg_ref[...] += jnp.sum(dgp, axis=0, keepdims=True)

    return pl.pallas_call(
        body, name="input_norm_bwd", grid=(T // tr,),
        in_specs=[_row_spec(tr, D_MODEL)] * 3 + [_vec_spec(D_MODEL)],
        out_specs=[_row_spec(tr, D_MODEL), _vec_spec(D_MODEL)],
        out_shape=[jax.ShapeDtypeStruct((T, D_MODEL), F32), jax.ShapeDtypeStruct((1, D_MODEL), F32)],
        compiler_params=_params("arbitrary"),
    )(x, dxn, dh1, g)


def _gate_bwd(dm, a, b, gl, *, tr=256):
    T = dm.shape[0]
    tr = min(tr, T)

    def body(dm_ref, a_ref, b_ref, gla_ref, glb_ref, da_ref, db_ref, dgla_ref, dglb_ref):
        dmv = dm_ref[...]
        ga = jax.nn.sigmoid(gla_ref[...])
        gb = jax.nn.sigmoid(glb_ref[...])
        da_ref[...] = (dmv * ga).astype(BF)
        db_ref[...] = (dmv * gb).astype(BF)
        dgla_ref[...] = (dmv * a_ref[...] * (ga * (1.0 - ga))).astype(BF)
        dglb_ref[...] = (dmv * b_ref[...] * (gb * (1.0 - gb))).astype(BF)

    spec = _row_spec(tr, D_MODEL)
    spec_b = pl.BlockSpec((tr, D_MODEL), lambda i: (i, 1))
    da, db, dgla, dglb = pl.pallas_call(
        body, name="gate_bwd", grid=(T // tr,),
        in_specs=[spec, spec, spec, spec, spec_b], out_specs=[spec] * 4,
        out_shape=[jax.ShapeDtypeStruct((T, D_MODEL), BF)] * 4, compiler_params=_params("parallel"),
    )(dm, a, b, gl, gl)
    return da, db, dgla, dglb


def _sgu_norm(z_tile, g, b):
    gz = _gelu(z_tile)
    u, vv = gz[:, :SGU_W], gz[:, SGU_W:]
    xc = vv - jnp.mean(vv, axis=-1, keepdims=True)
    rstd = lax.rsqrt(jnp.mean(xc * xc, axis=-1, keepdims=True) + EPS)
    xhat = xc * rstd
    return u, xhat, rstd, xhat * g + b


def _sgu_mix(w_ref, v_bf, first_half):
    parts = []
    for p in range(N_GROUPS // 2):
        vp = v_bf[:, p * LANES:(p + 1) * LANES]
        parts.append(jnp.where(first_half, _dot(w_ref[2 * p], vp), _dot(w_ref[2 * p + 1], vp)))
    return jnp.concatenate(parts, axis=1)


def _sgu_fwd(z, g_sgu, b_sgu, ws, bias_plane, *, tm=512):
    T = z.shape[0]
    tm = min(tm, T)

    def body(z_ref, g_ref, b_ref, ws_ref, bp_ref, y_ref):
        u, _, _, vn = _sgu_norm(z_ref[...], g_ref[...], b_ref[...])
        vn_bf = vn.astype(BF)
        first_half = lax.broadcasted_iota(jnp.int32, (CHUNK, LANES), 1) < HEAD_DIM
        for c in range(tm // CHUNK):
            rows = slice(c * CHUNK, (c + 1) * CHUNK)
            s = _sgu_mix(ws_ref, vn_bf[rows, :], first_half) + bp_ref[...]
            y_ref[rows, :] = (u[rows, :] * s).astype(BF)

    return pl.pallas_call(
        body, name="sgu_fwd", grid=(T // tm,),
        in_specs=[_row_spec(tm, 2 * SGU_W), _vec_spec(SGU_W), _vec_spec(SGU_W),
                  pl.BlockSpec((N_GROUPS, CHUNK, CHUNK), lambda i: (0, 0, 0)),
                  pl.BlockSpec((CHUNK, SGU_W), lambda i: (0, 0))],
        out_specs=_row_spec(tm, SGU_W), out_shape=jax.ShapeDtypeStruct((T, SGU_W), BF),
        compiler_params=_params("parallel"),
    )(z, g_sgu, b_sgu, ws, bias_plane)


def _sgu_bwd(dy, z, g_sgu, b_sgu, ws, ws_t, bias_plane, *, tm=512):
    T = z.shape[0]
    tm = min(tm, T)
    n_steps = T // tm

    def body(dy_ref, z_ref, g_ref, b_ref, ws_ref, wst_ref, bp_ref, dz_ref, dws_ref, dbs_ref, dg_ref, db_ref, dbp_ref):
        step = pl.program_id(0)

        @pl.when(step == 0)
        def _():
            dws_ref[...] = jnp.zeros_like(dws_ref)
            dg_ref[...] = jnp.zeros_like(dg_ref)
            db_ref[...] = jnp.zeros_like(db_ref)
            dbp_ref[...] = jnp.zeros_like(dbp_ref)

        g = g_ref[...]
        zt = z_ref[...]
        u, xhat, rstd, vn = _sgu_norm(zt, g, b_ref[...])
        vn_bf = vn.astype(BF)
        first_half = lax.broadcasted_iota(jnp.int32, (CHUNK, LANES), 1) < HEAD_DIM
        dyv = dy_ref[...]
        dg_acc = jnp.zeros((1, SGU_W), F32)
        db_acc = jnp.zeros((1, SGU_W), F32)
        for c in range(tm // CHUNK):
            rows = slice(c * CHUNK, (c + 1) * CHUNK)
            v_c = vn_bf[rows, :]
            s = _sgu_mix(ws_ref, v_c, first_half) + bp_ref[...]
            dy_c = dyv[rows, :]
            du = dy_c * s
            dsv = dy_c * u[rows, :]
            dbp_ref[...] += dsv
            ds_bf = dsv.astype(BF)
            zero = jnp.zeros((CHUNK, LANES), BF)
            for p in range(N_GROUPS // 2):
                dsp = ds_bf[:, p * LANES:(p + 1) * LANES]
                vp = v_c[:, p * LANES:(p + 1) * LANES]
                dws_ref[2 * p] += _dot_nt(jnp.where(first_half, dsp, zero), vp)
                dws_ref[2 * p + 1] += _dot_nt(jnp.where(first_half, zero, dsp), vp)
            dvn = _sgu_mix(wst_ref, ds_bf, first_half)
            xh = xhat[rows, :]
            dxh = dvn * g
            dvv = rstd[rows, :] * (dxh - jnp.mean(dxh, axis=-1, keepdims=True)
                                   - xh * jnp.mean(dxh * xh, axis=-1, keepdims=True))
            dg_acc += jnp.sum(dvn * xh, axis=0, keepdims=True)
            db_acc += jnp.sum(dvn, axis=0, keepdims=True)
            dgz = jnp.concatenate([du, dvv], axis=1)
            dz_ref[rows, :] = (dgz * _gelu_grad(zt[rows, :])).astype(BF)
        dg_ref[...] += dg_acc
        db_ref[...] += db_acc

        @pl.when(step == n_steps - 1)
        def _():
            r = lax.broadcasted_iota(jnp.int32, (CHUNK, CHUNK), 0)
            cidx = lax.broadcasted_iota(jnp.int32, (CHUNK, CHUNK), 1)
            causal = (cidx <= r).astype(F32)
            for gi in range(N_GROUPS):
                dws_ref[gi] = dws_ref[gi] * causal
            lane = lax.broadcasted_iota(jnp.int32, (CHUNK, LANES), 1)
            out = jnp.zeros((CHUNK, LANES), F32)
            dbp = dbp_ref[...]
            for gi in range(N_GROUPS):
                col = jnp.sum(dbp[:, gi * HEAD_DIM:(gi + 1) * HEAD_DIM], axis=1, keepdims=True)
                out = jnp.where(lane == gi, col, out)
            dbs_ref[...] = out

    w_spec = pl.BlockSpec((N_GROUPS, CHUNK, CHUNK), lambda i: (0, 0, 0))
    plane = pl.BlockSpec((CHUNK, SGU_W), lambda i: (0, 0))
    return pl.pallas_call(
        body, name="sgu_bwd", grid=(n_steps,),
        in_specs=[_row_spec(tm, SGU_W), _row_spec(tm, 2 * SGU_W), _vec_spec(SGU_W), _vec_spec(SGU_W), w_spec, w_spec, plane],
        out_specs=[_row_spec(tm, 2 * SGU_W), w_spec, pl.BlockSpec((CHUNK, LANES), lambda i: (0, 0)),
                   _vec_spec(SGU_W), _vec_spec(SGU_W)],
        out_shape=[jax.ShapeDtypeStruct((T, 2 * SGU_W), BF), jax.ShapeDtypeStruct((N_GROUPS, CHUNK, CHUNK), F32),
                   jax.ShapeDtypeStruct((CHUNK, LANES), F32), jax.ShapeDtypeStruct((1, SGU_W), F32),
                   jax.ShapeDtypeStruct((1, SGU_W), F32)],
        scratch_shapes=[pltpu.VMEM((CHUNK, SGU_W), F32)],
        compiler_params=_params("arbitrary"),
    )(dy, z, g_sgu, b_sgu, ws, ws_t, bias_plane)


def _tri(n, upper):
    r = lax.broadcasted_iota(jnp.int32, (n, n), 0)
    c = lax.broadcasted_iota(jnp.int32, (n, n), 1)
    return ((c >= r) if upper else (c <= r)).astype(BF)


def _scan_dot(tri, x):
    hi, mid, lo = _split3(x)
    return (_dot(tri, hi.astype(BF)) + _dot(tri, mid.astype(BF))) + _dot(tri, lo.astype(BF))


def _with_lanes(base, lane, start, cols):
    out = base
    for k, col in enumerate(cols):
        if col is not None:
            out = jnp.where(lane == start + k, col, out)
    return out


def _logit_bound(q_norm, k_norm):
    return NORM_SLACK * q_norm * k_norm + 1.0


ATTN_TILE = 256
SKIP_BELOW = -110.0
NORM_SLACK = 1.001
BOUNDED_GAP = 60.0


def _attn_prep(qkv, fl, b_forget, *, tp=ATTN_TILE):
    T = qkv.shape[0]
    tp = min(tp, T)

    def body(qkv_ref, fl_ref, bf_ref, qf_ref, kl_ref, vl_ref, st_ref, carry_ref, kmax_ref):
        @pl.when(pl.program_id(0) == 0)
        def _():
            carry_ref[...] = jnp.zeros_like(carry_ref)
            kmax_ref[...] = jnp.zeros_like(kmax_ref)

        x = fl_ref[...] + bf_ref[...]
        logf = jnp.minimum(x, 0.0) - jnp.log(1.0 + jnp.exp(-jnp.abs(x)))
        cum = _scan_dot(_tri(tp, upper=False), logf) + carry_ref[...]
        carry_ref[...] = cum[tp - 1:tp, :]
        lane = lax.broadcasted_iota(jnp.int32, (tp, HEAD_DIM), 1)
        ones3 = jnp.where(lane < 3, 1.0, 0.0)
        qkvv = qkv_ref[...]
        st_row = lax.broadcasted_iota(jnp.int32, (N_HEADS, LANES), 0)
        st_lane = lax.broadcasted_iota(jnp.int32, (N_HEADS, LANES), 1)
        stats = jnp.zeros((N_HEADS, LANES), F32)
        kmax_lane = lax.broadcasted_iota(jnp.int32, (1, LANES), 1)
        for h in range(N_HEADS):
            ch = cum[:, h:h + 1]
            c3 = _split3(ch)
            qh = qkvv[:, h * HEAD_DIM:(h + 1) * HEAD_DIM].astype(F32) * Q_SCALE
            kh = qkvv[:, ATTN_W + h * HEAD_DIM:ATTN_W + (h + 1) * HEAD_DIM].astype(F32)
            vh = qkvv[:, 2 * ATTN_W + h * HEAD_DIM:2 * ATTN_W + (h + 1) * HEAD_DIM].astype(F32)
            q_norm = jnp.sqrt(jnp.sum(qh * qh, axis=1, keepdims=True))
            qn = jnp.max(q_norm, axis=0, keepdims=True)
            kn = jnp.sqrt(jnp.max(jnp.sum(kh * kh, axis=1, keepdims=True), axis=0, keepdims=True))
            k_seen = jnp.maximum(kmax_ref[:, h:h + 1], kn)
            kmax_ref[...] = jnp.where(kmax_lane == h, k_seen, kmax_ref[...])
            bound3 = _split3(-_logit_bound(q_norm, k_seen))
            ext_q = _with_lanes(jnp.where((lane >= 3) & (lane < 6), 1.0, 0.0), lane, 0, list(c3) + [None] * 3 + list(bound3))
            ext_k = _with_lanes(jnp.where((lane < 3) | ((lane >= 6) & (lane < 9)), 1.0, 0.0), lane, 3, [-c for c in c3])
            qf_ref[h] = jnp.concatenate([qh, ext_q], axis=1).astype(BF)
            kl_ref[h] = jnp.concatenate([kh, ext_k], axis=1).astype(BF)
            vl_ref[h] = jnp.concatenate([vh, ones3], axis=1).astype(BF)
            tile_stats = (qn, kn, jnp.max(ch, axis=0, keepdims=True), jnp.min(ch, axis=0, keepdims=True), k_seen)
            for k, val in enumerate(tile_stats):
                stats = jnp.where((st_row == h) & (st_lane == k), val, stats)
        st_ref[0] = stats

    head_spec = pl.BlockSpec((N_HEADS, tp, LANES), lambda i: (0, i, 0))
    return pl.pallas_call(
        body, name="attn_prep", grid=(T // tp,),
        in_specs=[_row_spec(tp, 3 * ATTN_W), _row_spec(tp, LANES), _vec_spec(LANES)],
        out_specs=[head_spec] * 3 + [pl.BlockSpec((1, N_HEADS, LANES), lambda i: (i, 0, 0))],
        out_shape=[jax.ShapeDtypeStruct((N_HEADS, T, LANES), BF)] * 3 + [jax.ShapeDtypeStruct((T // tp, N_HEADS, LANES), F32)],
        scratch_shapes=[pltpu.VMEM((1, LANES), F32), pltpu.VMEM((1, LANES), F32)], compiler_params=_params("arbitrary"),
    )(qkv, fl, b_forget)


def _attn_ranges(stats):
    qn, kn, cmax, cmin, k_seen = (stats[:, :, k].T for k in range(5))
    n = qn.shape[1]
    bounded = (2.0 * _logit_bound(qn, k_seen) <= BOUNDED_GAP).reshape(N_HEADS // 2, 2, n).all(axis=1)
    reach = NORM_SLACK * qn * (jnp.max(kn, axis=1, keepdims=True) + kn) + cmax
    i = jnp.arange(n)[None, :, None]
    j = jnp.arange(n)[None, None, :]
    need = ((reach[:, :, None] - cmin[:, None, :] >= SKIP_BELOW) | (i == j)) & (j <= i)
    first = jnp.min(jnp.where(need, j, n), axis=2).reshape(N_HEADS // 2, 2, n).min(axis=1)
    last = jnp.max(jnp.where(need, i, -1), axis=1).reshape(N_HEADS // 2, 2, n).max(axis=1)
    return first.reshape(-1).astype(F32), last.reshape(-1).astype(F32), bounded.reshape(-1).astype(F32)


def _pair_block(t):
    return pl.BlockSpec((2, t, LANES), lambda p, i, *_: (p, i, 0))


def _pair_full(T):
    return pl.BlockSpec((2, T, LANES), lambda p, i, *_: (p, 0, 0))


def _packed_block(t):
    return pl.BlockSpec((t, LANES), lambda p, i, *_: (i, p))


def _causal(t, keys_in_rows=False):
    r = lax.broadcasted_iota(jnp.int32, (t, t), 0)
    c = lax.broadcasted_iota(jnp.int32, (t, t), 1)
    return (r <= c) if keys_in_rows else (c <= r)


def _tile_rows(j, t):
    return pl.ds(pl.multiple_of(j * t, t), t)


def _attn_call(body, name, tile_scalars, operands, in_specs, out_specs, out_shape, scratch_shapes, n_tiles):
    return pl.pallas_call(
        body, name=name,
        grid_spec=pltpu.PrefetchScalarGridSpec(
            num_scalar_prefetch=len(tile_scalars), grid=(N_HEADS // 2, n_tiles), in_specs=in_specs, out_specs=out_specs,
            scratch_shapes=scratch_shapes),
        out_shape=out_shape, compiler_params=_params("parallel", "arbitrary"),
    )(*tile_scalars, *operands)


def _attn_fwd(qf, kl, vl, first, bounded, *, tq=ATTN_TILE):
    T = qf.shape[1]
    tq = min(tq, T)
    n = T // tq

    def body(first_ref, bounded_ref, qf_ref, kl_ref, vl_ref, o_ref, of_ref, ql_ref, m_ref, acc_ref):
        i = pl.program_id(1)
        tile = pl.program_id(0) * n + i
        start = first_ref[tile].astype(jnp.int32)
        is_bounded = bounded_ref[tile] > 0.5
        acc_ref[...] = jnp.zeros_like(acc_ref)
        diagonal = _tile_rows(i, tq)
        causal = _causal(tq)

        def logits(hh, rows):
            return _dot_nt(qf_ref[hh], kl_ref[hh, rows, :])

        @pl.when(is_bounded)
        def _():
            m_ref[...] = jnp.zeros_like(m_ref)

            def update(hh, s, rows):
                acc_ref[hh] += _dot(jnp.exp(s).astype(BF), vl_ref[hh, rows, :])

            def step(j, carry):
                for hh in range(2):
                    update(hh, logits(hh, _tile_rows(j, tq)), _tile_rows(j, tq))
                return carry

            lax.fori_loop(start, i, step, 0)
            for hh in range(2):
                update(hh, jnp.where(causal, logits(hh, diagonal), NEG), diagonal)

        @pl.when(jnp.logical_not(is_bounded))
        def _():
            m_ref[...] = jnp.full_like(m_ref, NEG)

            def update(hh, s, rows):
                m_old = m_ref[hh]
                m_new = jnp.maximum(m_old, jnp.max(s, axis=1, keepdims=True))
                p = jnp.exp(s - m_new)
                acc_ref[hh] = jnp.exp(m_old - m_new) * acc_ref[hh] + _dot(p.astype(BF), vl_ref[hh, rows, :])
                m_ref[hh] = m_new

            def step(j, carry):
                for hh in range(2):
                    update(hh, logits(hh, _tile_rows(j, tq)), _tile_rows(j, tq))
                return carry

            lax.fori_loop(start, i, step, 0)
            for hh in range(2):
                update(hh, jnp.where(causal, logits(hh, diagonal), NEG), diagonal)

        lane = lax.broadcasted_iota(jnp.int32, (tq, LANES), 1)
        outs = []
        for hh in range(2):
            q = qf_ref[hh].astype(F32)
            acc = acc_ref[hh]
            l = acc[:, HEAD_DIM:HEAD_DIM + 1]
            outs.append(acc[:, :HEAD_DIM] / l)
            at = HEAD_DIM + 6
            neg_bound = (q[:, at:at + 1] + q[:, at + 1:at + 2]) + q[:, at + 2:at + 3]
            ql_ref[hh] = _with_lanes(q, lane, at, _split3(neg_bound - (m_ref[hh] + jnp.log(l)))).astype(BF)
        o = jnp.concatenate(outs, axis=1)
        o_ref[...] = o.astype(BF)
        of_ref[...] = o

    return _attn_call(
        body, "attn_fwd", (first, bounded), (qf, kl, vl), [_pair_block(tq), _pair_full(T), _pair_full(T)],
        [_packed_block(tq), _packed_block(tq), _pair_block(tq)],
        [jax.ShapeDtypeStruct((T, ATTN_W), BF), jax.ShapeDtypeStruct((T, ATTN_W), F32),
         jax.ShapeDtypeStruct((N_HEADS, T, LANES), BF)],
        [pltpu.VMEM((2, tq, 1), F32), pltpu.VMEM((2, tq, LANES), F32)], n)


def _attn_bwd_prep(dya, of, *, tr=256):
    T = dya.shape[0]
    tr = min(tr, T)

    def body(d_ref, o_ref, do_ref):
        lane = lax.broadcasted_iota(jnp.int32, (tr, HEAD_DIM), 1)
        dv, ov = d_ref[...], o_ref[...]
        for h in range(N_HEADS):
            d = dv[:, h * HEAD_DIM:(h + 1) * HEAD_DIM]
            delta = jnp.sum(d * ov[:, h * HEAD_DIM:(h + 1) * HEAD_DIM], axis=1, keepdims=True)
            ext = _with_lanes(jnp.zeros((tr, HEAD_DIM), F32), lane, 0, _split3(-delta))
            do_ref[h] = jnp.concatenate([d, ext], axis=1).astype(BF)

    return pl.pallas_call(
        body, name="attn_bwd_prep", grid=(T // tr,),
        in_specs=[_row_spec(tr, ATTN_W), _row_spec(tr, ATTN_W)],
        out_specs=pl.BlockSpec((N_HEADS, tr, LANES), lambda i: (0, i, 0)),
        out_shape=jax.ShapeDtypeStruct((N_HEADS, T, LANES), BF), compiler_params=_params("parallel"),
    )(dya, of)


def _attn_bwd_dq(ql, do, kl, vl, first, *, tq=ATTN_TILE):
    T = ql.shape[1]
    tq = min(tq, T)
    n = T // tq

    def body(first_ref, ql_ref, do_ref, kl_ref, vl_ref, dq_ref, ext_ref, acc_ref):
        i = pl.program_id(1)
        acc_ref[...] = jnp.zeros_like(acc_ref)

        def block(hh, rows, mask):
            kj = kl_ref[hh, rows, :]
            p = jnp.exp(_dot_nt(ql_ref[hh], kj))
            if mask is not None:
                p = jnp.where(mask, p, 0.0)
            ds = p * _dot_nt(do_ref[hh], vl_ref[hh, rows, :])
            acc_ref[hh] += _dot(ds.astype(BF), kj)

        def step(j, carry):
            for hh in range(2):
                block(hh, _tile_rows(j, tq), None)
            return carry

        lax.fori_loop(first_ref[pl.program_id(0) * n + i].astype(jnp.int32), i, step, 0)
        causal = _causal(tq)
        for hh in range(2):
            block(hh, _tile_rows(i, tq), causal)
        dq_ref[...] = jnp.concatenate([acc_ref[hh][:, :HEAD_DIM] * Q_SCALE for hh in range(2)], axis=1).astype(BF)
        ext_ref[...] = jnp.concatenate([acc_ref[hh][:, HEAD_DIM:] for hh in range(2)], axis=1)

    return _attn_call(
        body, "attn_bwd_dq", (first,), (ql, do, kl, vl),
        [_pair_block(tq), _pair_block(tq), _pair_full(T), _pair_full(T)], [_packed_block(tq), _packed_block(tq)],
        [jax.ShapeDtypeStruct((T, ATTN_W), BF), jax.ShapeDtypeStruct((T, ATTN_W), F32)],
        [pltpu.VMEM((2, tq, LANES), F32)], n)


def _attn_bwd_dkv(kl, vl, ql, do, last, *, tk=ATTN_TILE):
    T = ql.shape[1]
    tk = min(tk, T)
    n = T // tk

    def body(last_ref, kl_ref, vl_ref, ql_ref, do_ref, dk_ref, dv_ref, ext_ref, dk_acc, dv_acc):
        j = pl.program_id(1)
        dk_acc[...] = jnp.zeros_like(dk_acc)
        dv_acc[...] = jnp.zeros_like(dv_acc)

        def block(hh, rows, mask):
            qi, di = ql_ref[hh, rows, :], do_ref[hh, rows, :]
            p_t = jnp.exp(_dot_nt(kl_ref[hh], qi))
            if mask is not None:
                p_t = jnp.where(mask, p_t, 0.0)
            ds_t = p_t * _dot_nt(vl_ref[hh], di)
            dk_acc[hh] += _dot(ds_t.astype(BF), qi)
            dv_acc[hh] += _dot(p_t.astype(BF), di)

        causal_t = _causal(tk, keys_in_rows=True)
        for hh in range(2):
            block(hh, _tile_rows(j, tk), causal_t)

        def step(i, carry):
            for hh in range(2):
                block(hh, _tile_rows(i, tk), None)
            return carry

        lax.fori_loop(j + 1, last_ref[pl.program_id(0) * n + j].astype(jnp.int32) + 1, step, 0)
        dk_ref[...] = jnp.concatenate([dk_acc[hh][:, :HEAD_DIM] for hh in range(2)], axis=1).astype(BF)
        dv_ref[...] = jnp.concatenate([dv_acc[hh][:, :HEAD_DIM] for hh in range(2)], axis=1).astype(BF)
        ext_ref[...] = jnp.concatenate([dk_acc[hh][:, HEAD_DIM:] for hh in range(2)], axis=1)

    return _attn_call(
        body, "attn_bwd_dkv", (last,), (kl, vl, ql, do),
        [_pair_block(tk), _pair_block(tk), _pair_full(T), _pair_full(T)], [_packed_block(tk)] * 3,
        [jax.ShapeDtypeStruct((T, ATTN_W), BF), jax.ShapeDtypeStruct((T, ATTN_W), BF),
         jax.ShapeDtypeStruct((T, ATTN_W), F32)],
        [pltpu.VMEM((2, tk, LANES), F32), pltpu.VMEM((2, tk, LANES), F32)], n)


def _forget_bwd(ext_q, ext_k, fl, b_forget, *, tp=256):
    T = fl.shape[0]
    tp = min(tp, T)
    n = T // tp

    def body(eq_ref, ek_ref, fl_ref, bf_ref, dfl_ref, dbf_ref, carry_ref):
        @pl.when(pl.program_id(0) == 0)
        def _():
            carry_ref[...] = jnp.zeros_like(carry_ref)
            dbf_ref[...] = jnp.zeros_like(dbf_ref)

        lane = lax.broadcasted_iota(jnp.int32, (tp, LANES), 1)
        eq, ek = eq_ref[...], ek_ref[...]
        cols = [eq[:, h * HEAD_DIM:h * HEAD_DIM + 1] - ek[:, h * HEAD_DIM + 3:h * HEAD_DIM + 4] for h in range(N_HEADS)]
        dcum = _with_lanes(jnp.zeros((tp, LANES), F32), lane, 0, cols)
        suffix = _scan_dot(_tri(tp, upper=True), dcum) + carry_ref[...]
        carry_ref[...] = suffix[0:1, :]
        x = fl_ref[...] + bf_ref[...]
        dfl = jnp.where(lane < N_HEADS, suffix / (1.0 + jnp.exp(x)), 0.0)
        dfl_ref[...] = dfl.astype(BF)
        dbf_ref[...] += jnp.sum(dfl, axis=0, keepdims=True)

    rev = lambda w: pl.BlockSpec((tp, w), lambda i: (n - 1 - i, 0))
    return pl.pallas_call(
        body, name="forget_bwd", grid=(n,),
        in_specs=[rev(ATTN_W), rev(ATTN_W), rev(LANES), _vec_spec(LANES)],
        out_specs=[rev(LANES), _vec_spec(LANES)],
        out_shape=[jax.ShapeDtypeStruct((T, LANES), BF), jax.ShapeDtypeStruct((1, LANES), F32)],
        scratch_shapes=[pltpu.VMEM((1, LANES), F32)], compiler_params=_params("arbitrary"),
    )(ext_q, ext_k, fl, b_forget)


def _adamw(w, g, m, v, *, name, tr=256):
    rows, cols = w.shape
    tr = tr if rows % tr == 0 else rows

    def body(w_ref, g_ref, m_ref, v_ref, d_ref, nm_ref, nv_ref):
        gv = g_ref[...]
        nm = ADAM_B1 * m_ref[...] + (1.0 - ADAM_B1) * gv
        nv = ADAM_B2 * v_ref[...] + (1.0 - ADAM_B2) * (gv * gv)
        m_hat = nm / (1.0 - ADAM_B1 ** ADAM_STEP)
        v_hat = nv / (1.0 - ADAM_B2 ** ADAM_STEP)
        d_ref[...] = -ADAM_LR * (m_hat / (jnp.sqrt(v_hat) + ADAM_EPS) + ADAM_WD * w_ref[...])
        nm_ref[...] = nm
        nv_ref[...] = nv

    spec = pl.BlockSpec((tr, cols), lambda i: (i, 0))
    return pl.pallas_call(
        body, name=name, grid=(rows // tr,), in_specs=[spec] * 4, out_specs=[spec] * 3,
        out_shape=[jax.ShapeDtypeStruct((rows, cols), F32)] * 3, compiler_params=_params("parallel"),
    )(w, g, m, v)


HBM = pl.BlockSpec(memory_space=pltpu.HBM)


def _place():
    x, y, c = lax.axis_index("x"), lax.axis_index("y"), lax.axis_index("c")
    others = [(1 - x, y), (x, 1 - y), (1 - x, 1 - y)]
    return x, y, c, others


def _chip(xy):
    return 2 * xy[0] + xy[1]


def _gather_weights(wp):
    R = wp.shape[0]
    Rh = R // 2

    def body(w_ref, g_ref, send_sems, recv_sems):
        x, y, c, others = _place()
        sibling = (x, y, 1 - c)
        mine_rows = pl.ds(pl.multiple_of(c * Rh, 16), Rh)
        sibling_rows = pl.ds(pl.multiple_of((1 - c) * Rh, 16), Rh)

        def copy(k, src, dst, to):
            return pltpu.make_async_remote_copy(src_ref=src, dst_ref=dst, send_sem=send_sems.at[k], recv_sem=recv_sems.at[k],
                                                device_id=to, device_id_type=MESH)

        first = [copy(j, w_ref.at[mine_rows, :], g_ref.at[_chip((x, y)), mine_rows, :], (*o, c)) for j, o in enumerate(others)]
        for cp in first:
            cp.start()
        passed = []
        for j, o in enumerate(others):
            landed = g_ref.at[_chip(o), mine_rows, :]
            copy(j, landed, landed, (*o, c)).wait_recv()
            passed.append(copy(3 + j, landed, landed, sibling))
            passed[-1].start()
        for j, o in enumerate(others):
            landed = g_ref.at[_chip(o), sibling_rows, :]
            copy(3 + j, landed, landed, sibling).wait_recv()
        for cp in first + passed:
            cp.wait_send()

    gathered = pl.pallas_call(
        body, name="gather_weights", in_specs=[HBM], out_specs=HBM,
        out_shape=jax.ShapeDtypeStruct((N_CHIPS, R, LANES), wp.dtype),
        scratch_shapes=[pltpu.SemaphoreType.DMA((6,)), pltpu.SemaphoreType.DMA((6,))],
    )(wp)
    own_slot = 2 * lax.axis_index("x") + lax.axis_index("y")
    return lax.dynamic_update_slice(gathered, wp[None], (own_slot, 0, 0))


def _exchange_halves(gf, sf):
    Rh = gf.shape[1] // 2
    Rsh = sf.shape[0] // 2

    def body(g_ref, s_ref, rg_ref, rs_ref, send_sems, recv_sems):
        x, y, c, _ = _place()
        sibling = (x, y, 1 - c)
        big = pltpu.make_async_remote_copy(
            src_ref=g_ref.at[:, pl.ds(pl.multiple_of((1 - c) * Rh, 8), Rh), :], dst_ref=rg_ref,
            send_sem=send_sems.at[0], recv_sem=recv_sems.at[0], device_id=sibling, device_id_type=MESH)
        small = pltpu.make_async_remote_copy(
            src_ref=s_ref.at[pl.ds(pl.multiple_of((1 - c) * Rsh, 8), Rsh), :], dst_ref=rs_ref,
            send_sem=send_sems.at[1], recv_sem=recv_sems.at[1], device_id=sibling, device_id_type=MESH)
        big.start()
        small.start()
        big.wait()
        small.wait()

    return pl.pallas_call(
        body, name="exchange_halves", in_specs=[HBM, HBM], out_specs=[HBM, HBM],
        out_shape=[jax.ShapeDtypeStruct((N_CHIPS, Rh, LANES), F32), jax.ShapeDtypeStruct((Rsh, LANES), F32)],
        scratch_shapes=[pltpu.SemaphoreType.DMA((2,)), pltpu.SemaphoreType.DMA((2,))],
    )(gf, sf)


def _scatter_to_owners(cab, csa):
    Rh = cab.shape[1]
    Rsh = csa.shape[0]

    def body(b_ref, s_ref, rb_ref, rs_ref, send_sems, recv_sems):
        x, y, c, others = _place()
        me = _chip((x, y))
        sends = []
        for j, o in enumerate(others):
            sends.append(pltpu.make_async_remote_copy(
                src_ref=b_ref.at[_chip(o)], dst_ref=rb_ref.at[me], send_sem=send_sems.at[j], recv_sem=recv_sems.at[j],
                device_id=(*o, c), device_id_type=MESH))
            sends.append(pltpu.make_async_remote_copy(
                src_ref=s_ref, dst_ref=rs_ref.at[me], send_sem=send_sems.at[3 + j], recv_sem=recv_sems.at[3 + j],
                device_id=(*o, c), device_id_type=MESH))
        for cp in sends:
            cp.start()
        for j, o in enumerate(others):
            pltpu.make_async_remote_copy(
                src_ref=b_ref.at[me], dst_ref=rb_ref.at[_chip(o)], send_sem=send_sems.at[j], recv_sem=recv_sems.at[j],
                device_id=(*o, c), device_id_type=MESH).wait_recv()
            pltpu.make_async_remote_copy(
                src_ref=s_ref, dst_ref=rs_ref.at[_chip(o)], send_sem=send_sems.at[3 + j], recv_sem=recv_sems.at[3 + j],
                device_id=(*o, c), device_id_type=MESH).wait_recv()
        for cp in sends:
            cp.wait_send()

    return pl.pallas_call(
        body, name="scatter_to_owners", in_specs=[HBM, HBM], out_specs=[HBM, HBM],
        out_shape=[jax.ShapeDtypeStruct((N_CHIPS, Rh, LANES), BF), jax.ShapeDtypeStruct((N_CHIPS, Rsh, LANES), F32)],
        scratch_shapes=[pltpu.SemaphoreType.DMA((6,)), pltpu.SemaphoreType.DMA((6,))],
    )(cab, csa)


def _join_halves(tb, ts):
    def body(b_ref, s_ref, gb_ref, gs_ref, send_sems, recv_sems):
        x, y, c, _ = _place()
        sibling = (x, y, 1 - c)
        big = pltpu.make_async_remote_copy(src_ref=b_ref, dst_ref=gb_ref, send_sem=send_sems.at[0],
                                           recv_sem=recv_sems.at[0], device_id=sibling, device_id_type=MESH)
        small = pltpu.make_async_remote_copy(src_ref=s_ref, dst_ref=gs_ref, send_sem=send_sems.at[1],
                                             recv_sem=recv_sems.at[1], device_id=sibling, device_id_type=MESH)
        big.start()
        small.start()
        big.wait()
        small.wait()

    other_b, other_s = pl.pallas_call(
        body, name="join_halves", in_specs=[HBM, HBM], out_specs=[HBM, HBM],
        out_shape=[jax.ShapeDtypeStruct(tb.shape, F32), jax.ShapeDtypeStruct(ts.shape, F32)],
        scratch_shapes=[pltpu.SemaphoreType.DMA((2,)), pltpu.SemaphoreType.DMA((2,))],
    )(tb, ts)
    core = lax.axis_index("c")

    def in_row_order(mine, other):
        rows = mine.shape[0]
        out = lax.dynamic_update_slice(jnp.zeros((2 * rows, LANES), F32), mine, (core * rows, 0))
        return lax.dynamic_update_slice(out, other, ((1 - core) * rows, 0))

    return in_row_order(tb, other_b), in_row_order(ts, other_s)


def _row_tile(rows, cap=1152, mult=16):
    return max(t for t in range(mult, min(rows, cap) + 1, mult) if rows % t == 0)


def _add_sibling(gf, rg, sf, rs, core):
    Rh = rg.shape[1]
    Rsh = rs.shape[0]
    tr = _row_tile(Rh)
    nb = Rh // tr

    def big_body(core_ref, g_ref, r_ref, o_ref, ob_ref):
        s = g_ref[...] + r_ref[...]
        o_ref[...] = s
        ob_ref[...] = s.astype(BF)

    spec = pl.BlockSpec((N_CHIPS, tr, LANES), lambda i, core_ref: (0, i, 0))
    ca, cab = pl.pallas_call(
        big_body, name="add_sibling",
        grid_spec=pltpu.PrefetchScalarGridSpec(
            num_scalar_prefetch=1, grid=(nb,),
            in_specs=[pl.BlockSpec((N_CHIPS, tr, LANES), lambda i, core_ref: (0, core_ref[0] * nb + i, 0)), spec],
            out_specs=[spec, spec]),
        out_shape=[jax.ShapeDtypeStruct((N_CHIPS, Rh, LANES), F32), jax.ShapeDtypeStruct((N_CHIPS, Rh, LANES), BF)],
        compiler_params=_params("parallel"),
    )(core, gf, rg)

    def small_body(core_ref, s_ref, r_ref, o_ref):
        o_ref[...] = s_ref[...] + r_ref[...]

    sspec = pl.BlockSpec((Rsh, LANES), lambda i, core_ref: (0, 0))
    csa = pl.pallas_call(
        small_body, name="add_sibling_small",
        grid_spec=pltpu.PrefetchScalarGridSpec(
            num_scalar_prefetch=1, grid=(1,),
            in_specs=[pl.BlockSpec((Rsh, LANES), lambda i, core_ref: (core_ref[0], 0)), sspec], out_specs=sspec),
        out_shape=jax.ShapeDtypeStruct((Rsh, LANES), F32), compiler_params=_params("arbitrary"),
    )(core, sf, rs)
    return ca, cab, csa


def _add_chips(ca, rb, csa, rsb, chip):
    Rh = ca.shape[1]
    tr = _row_tile(Rh)
    Rsh = rsb.shape[1]

    def written(k, chip_ref):
        return jnp.where(chip_ref[0] == k, (k + 1) % N_CHIPS, k)

    def big_body(chip_ref, own_ref, *refs):
        o_ref = refs[N_CHIPS]
        acc = own_ref[0]
        for k in range(N_CHIPS):
            acc = acc + jnp.where(chip_ref[0] == k, 0.0, refs[k][0].astype(F32))
        o_ref[...] = acc

    tb = pl.pallas_call(
        big_body, name="add_chips",
        grid_spec=pltpu.PrefetchScalarGridSpec(
            num_scalar_prefetch=1, grid=(Rh // tr,),
            in_specs=[pl.BlockSpec((1, tr, LANES), lambda i, chip_ref: (chip_ref[0], i, 0))]
            + [pl.BlockSpec((1, tr, LANES), functools.partial(lambda i, chip_ref, k: (written(k, chip_ref), i, 0), k=k))
               for k in range(N_CHIPS)],
            out_specs=pl.BlockSpec((tr, LANES), lambda i, chip_ref: (i, 0))),
        out_shape=jax.ShapeDtypeStruct((Rh, LANES), F32), compiler_params=_params("parallel"),
    )(chip, ca, *([rb] * N_CHIPS))

    def small_body(chip_ref, own_ref, *refs):
        o_ref = refs[N_CHIPS]
        terms = [jnp.where(chip_ref[0] == k, own_ref[...], refs[k][0]) for k in range(N_CHIPS)]
        o_ref[...] = ((terms[0] + terms[1]) + terms[2]) + terms[3]

    ts = pl.pallas_call(
        small_body, name="add_chips_small",
        grid_spec=pltpu.PrefetchScalarGridSpec(
            num_scalar_prefetch=1, grid=(1,),
            in_specs=[pl.BlockSpec((Rsh, LANES), lambda i, chip_ref: (0, 0))]
            + [pl.BlockSpec((1, Rsh, LANES), functools.partial(lambda i, chip_ref, k: (written(k, chip_ref), 0, 0), k=k))
               for k in range(N_CHIPS)],
            out_specs=pl.BlockSpec((Rsh, LANES), lambda i, chip_ref: (0, 0))),
        out_shape=jax.ShapeDtypeStruct((Rsh, LANES), F32), compiler_params=_params("arbitrary"),
    )(chip, csa, *([rsb] * N_CHIPS))
    return tb, ts


SHARDED = (("w_in", (D_MODEL, 4616), 1), ("w_branch_sgu", (SGU_W, D_MODEL), 1), ("w_branch_attn", (ATTN_W, D_MODEL), 1),
           ("w_out", (D_MODEL, D_MODEL), 0), ("w_up", (D_MODEL, D_FF), 1), ("w_down", (D_FF, D_MODEL), 0))
SMALL = (("g_mix_pre", (1, D_MODEL)), ("b_forget", (1, N_HEADS)), ("g_sgu", (1, SGU_W)), ("b_sgu", (1, SGU_W)),
         ("w_spatial", (N_GROUPS * CHUNK, CHUNK)), ("b_spatial", (N_GROUPS, CHUNK)), ("g_mix_post", (1, D_MODEL)),
         ("g_ffn_pre", (1, D_MODEL)), ("g_ffn_post", (1, D_MODEL)))
PACK_ALIGN = 256


def _shard_shape(shape, axis):
    return tuple(s // N_CHIPS if a == axis else s for a, s in enumerate(shape))


def _padded_rows(rows):
    return -(-rows // PACK_ALIGN) * PACK_ALIGN


def _pack_rows(parts, axis):
    rows = sum(p.shape[axis] for p in parts)
    pad = _padded_rows(rows) - rows
    if pad:
        shape = list(parts[0].shape)
        shape[axis] = pad
        parts = list(parts) + [jnp.zeros(shape, parts[0].dtype)]
    return jnp.concatenate(parts, axis=axis)


def _pack_shards(shards, dtype):
    return _pack_rows([shards[name].astype(dtype).reshape(-1, LANES) for name, _, _ in SHARDED], 0)


def _unpack_shards(packed):
    out, row = {}, 0
    for name, shape, axis in SHARDED:
        sshape = _shard_shape(shape, axis)
        n = sshape[0] * sshape[1] // LANES
        if packed.ndim == 2:
            out[name] = packed[row:row + n].reshape(sshape)
        else:
            parts = packed[:, row:row + n].reshape((N_CHIPS,) + sshape)
            out[name] = parts.reshape(shape) if axis == 0 else parts.transpose(1, 0, 2).reshape(shape)
        row += n
    return out


def _pack_full_grads(grads):
    parts = []
    for name, shape, axis in SHARDED:
        g = grads[name]
        sshape = _shard_shape(shape, axis)
        if axis == 0:
            g = g.reshape((N_CHIPS,) + sshape)
        else:
            g = g.reshape(shape[0], N_CHIPS, sshape[1]).transpose(1, 0, 2)
        parts.append(g.reshape(N_CHIPS, -1, LANES))
    return _pack_rows(parts, 1)


def _small_rows(shape):
    return -(-(shape[0] * shape[1]) // (8 * LANES)) * 8


def _pack_small(values):
    parts = []
    for name, shape in SMALL:
        flat = values[name].reshape(-1)
        n = _small_rows(shape)
        parts.append(jnp.pad(flat, (0, n * LANES - flat.shape[0])).reshape(n, LANES))
    return _pack_rows(parts, 0)


def _unpack_small(packed):
    out, row = {}, 0
    for name, shape in SMALL:
        n = _small_rows(shape)
        out[name] = packed[row:row + n].reshape(-1)[:shape[0] * shape[1]].reshape(shape)
        row += n
    return out


IN_Z, IN_Q, IN_K, IN_V, IN_F, IN_G, IN_END = 0, 1024, 1536, 2048, 2560, 2568, 4616


def _local_step(x, target, w, small):
    w_in = w["w_in"]
    w_z, w_qkv, w_g = w_in[:, IN_Z:IN_Q], w_in[:, IN_Q:IN_F], w_in[:, IN_G:IN_END]
    w_q, w_k, w_v = w_in[:, IN_Q:IN_K], w_in[:, IN_K:IN_V], w_in[:, IN_V:IN_F]
    w_f = jnp.pad(w_in[:, IN_F:IN_G], ((0, 0), (0, LANES - N_HEADS)))
    b_forget = jnp.pad(small["b_forget"], ((0, 0), (0, LANES - N_HEADS)))
    causal = jnp.tril(jnp.ones((CHUNK, CHUNK), bool))
    ws = jnp.where(causal[None], small["w_spatial"].reshape(N_GROUPS, CHUNK, CHUNK), 0.0).astype(BF)
    ws_t = ws.transpose(0, 2, 1)
    bias_plane = jnp.repeat(small["b_spatial"].T, HEAD_DIM, axis=1)

    xn = _rms_fwd(x, small["g_mix_pre"])
    z = _matmul([(xn, w_z)], nt=False, out_dtypes=[F32], name="proj_z")
    qkv = _matmul([(xn, w_qkv)], nt=False, out_dtypes=[BF], name="proj_qkv")
    gl = _matmul([(xn, w_g)], nt=False, out_dtypes=[F32], name="proj_gate")
    fl = _matmul([(xn, w_f)], nt=False, out_dtypes=[F32], name="proj_forget")
    ysgu = _sgu_fwd(z, small["g_sgu"], small["b_sgu"], ws, bias_plane)
    qf, kl, vl, tile_stats = _attn_prep(qkv, fl, b_forget)
    first_key_tile, last_query_tile, bounded = _attn_ranges(tile_stats)
    yattn, yattn_f, ql = _attn_fwd(qf, kl, vl, first_key_tile, bounded)
    a, b, merged = _branch_merge(ysgu, yattn, w["w_branch_sgu"], w["w_branch_attn"], gl)
    o = _matmul([(merged, w["w_out"])], nt=False, out_dtypes=[F32], name="proj_out")
    h1, xn2 = _mixer_out_fwd(o, x, small["g_mix_post"], small["g_ffn_pre"])

    def relu2(acc):
        r = jnp.maximum(acc, 0.0)
        return r * r, r

    hid, relu = _matmul([(xn2, w["w_up"])], nt=False, out_dtypes=[BF, BF], name="ffn_up", epilogue=relu2)
    dn = _matmul([(hid, w["w_down"])], nt=False, out_dtypes=[F32], name="ffn_down")
    sq, dy, ddn, dg_ffn_post = _loss_head(dn, h1, target, small["g_ffn_post"])

    dup = _matmul([(ddn, w["w_down"])], nt=True, out_dtypes=[BF], name="ffn_down_bwd",
                  epilogue=lambda acc, r: (acc * (2.0 * r.astype(F32)),), extras=[relu])
    dw_down = _matmul_tn(hid, ddn, name="dw_down")
    dxn2 = _matmul([(dup, w["w_up"])], nt=True, out_dtypes=[F32], name="ffn_up_bwd")
    dw_up = _matmul_tn(xn2, dup, name="dw_up")
    dh1, do, dg_ffn_pre, dg_mix_post = _mixer_out_bwd(h1, dxn2, dy, o, small["g_ffn_pre"], small["g_mix_post"])

    dmerged = _matmul([(do, w["w_out"])], nt=True, out_dtypes=[F32], name="proj_out_bwd")
    dw_out = _matmul_tn(merged, do, name="dw_out")
    da, db, dgla, dglb = _gate_bwd(dmerged, a, b, gl)
    dysgu = _matmul([(da, w["w_branch_sgu"])], nt=True, out_dtypes=[F32], name="branch_sgu_bwd")
    dyattn = _matmul([(db, w["w_branch_attn"])], nt=True, out_dtypes=[F32], name="branch_attn_bwd")
    dw_bs = _matmul_tn(ysgu, da, name="dw_branch_sgu")
    dw_ba = _matmul_tn(yattn, db, name="dw_branch_attn")
    dz, dws, dbs, dg_sgu, db_sgu = _sgu_bwd(dysgu, z, small["g_sgu"], small["b_sgu"], ws, ws_t, bias_plane)
    dout = _attn_bwd_prep(dyattn, yattn_f)
    dq, ext_q = _attn_bwd_dq(ql, dout, kl, vl, first_key_tile)
    dk, dv, ext_k = _attn_bwd_dkv(kl, vl, ql, dout, last_query_tile)
    dfl, dbf = _forget_bwd(ext_q, ext_k, fl, b_forget)
    dxn = _matmul([(dz, w_z), (dq, w_q), (dk, w_k), (dv, w_v), (dgla, w_g[:, :D_MODEL]), (dglb, w_g[:, D_MODEL:]), (dfl, w_f)],
                  nt=True, out_dtypes=[F32], name="proj_in_bwd")
    dw_in = jnp.concatenate(
        [_matmul_tn(xn, dz, name="dw_in_z"), _matmul_tn(xn, dq, name="dw_in_q"), _matmul_tn(xn, dk, name="dw_in_k"),
         _matmul_tn(xn, dv, name="dw_in_v"), _matmul_tn(xn, dfl, name="dw_in_f")[:, :N_HEADS],
         _matmul_tn(xn, dgla, name="dw_in_ga"), _matmul_tn(xn, dglb, name="dw_in_gb")], axis=1)
    dx, dg_mix_pre = _input_norm_bwd(x, dxn, dh1, small["g_mix_pre"])

    grads = {"w_in": dw_in, "w_branch_sgu": dw_bs, "w_branch_attn": dw_ba, "w_out": dw_out, "w_up": dw_up, "w_down": dw_down}
    small_grads = {"g_mix_pre": dg_mix_pre, "b_forget": dbf[:, :N_HEADS], "g_sgu": dg_sgu, "b_sgu": db_sgu,
                   "w_spatial": dws.reshape(N_GROUPS * CHUNK, CHUNK), "b_spatial": dbs[:, :N_GROUPS].T,
                   "g_mix_post": dg_mix_post, "g_ffn_pre": dg_ffn_pre, "g_ffn_post": dg_ffn_post}
    return sq, dx, grads, small_grads


NAMES = ("g_mix_pre", "w_in", "b_forget", "g_sgu", "b_sgu", "w_spatial", "b_spatial", "w_branch_sgu", "w_branch_attn",
         "w_out", "g_mix_post", "g_ffn_pre", "w_up", "w_down", "g_ffn_post")


def kernel(x, g_mix_pre, w_in, b_forget, g_sgu, b_sgu, w_spatial, b_spatial, w_branch_sgu, w_branch_attn, w_out, g_mix_post, g_ffn_pre, w_up, w_down, g_ffn_post, loss_target, m_g_mix_pre, m_w_in, m_b_forget, m_g_sgu, m_b_sgu, m_w_spatial, m_b_spatial, m_w_branch_sgu, m_w_branch_attn, m_w_out, m_g_mix_post, m_g_ffn_pre, m_w_up, m_w_down, m_g_ffn_post, v_g_mix_pre, v_w_in, v_b_forget, v_g_sgu, v_b_sgu, v_w_spatial, v_b_spatial, v_w_branch_sgu, v_w_branch_attn, v_w_out, v_g_mix_post, v_g_ffn_pre, v_w_up, v_w_down, v_g_ffn_post):
    weights = dict(zip(NAMES, (g_mix_pre, w_in, b_forget, g_sgu, b_sgu, w_spatial, b_spatial, w_branch_sgu, w_branch_attn,
                               w_out, g_mix_post, g_ffn_pre, w_up, w_down, g_ffn_post), strict=True))
    first = dict(zip(NAMES, (m_g_mix_pre, m_w_in, m_b_forget, m_g_sgu, m_b_sgu, m_w_spatial, m_b_spatial, m_w_branch_sgu,
                             m_w_branch_attn, m_w_out, m_g_mix_post, m_g_ffn_pre, m_w_up, m_w_down, m_g_ffn_post), strict=True))
    second = dict(zip(NAMES, (v_g_mix_pre, v_w_in, v_b_forget, v_g_sgu, v_b_sgu, v_w_spatial, v_b_spatial, v_w_branch_sgu,
                              v_w_branch_attn, v_w_out, v_g_mix_post, v_g_ffn_pre, v_w_up, v_w_down, v_g_ffn_post), strict=True))
    shard_shapes = {name: _shard_shape(shape, axis) for name, shape, axis in SHARDED}
    small_shapes = dict(SMALL)
    view = lambda name, a: a.reshape(shard_shapes.get(name) or small_shapes[name])

    core = lax.axis_index("c").astype(jnp.int32).reshape(1)
    chip = (2 * lax.axis_index("x") + lax.axis_index("y")).astype(jnp.int32).reshape(1)

    shards = {name: view(name, weights[name]) for name, _, _ in SHARDED}
    full = _unpack_shards(_gather_weights(_pack_shards(shards, BF)))
    small = {name: view(name, weights[name]) for name, _ in SMALL}

    sq, dx, grads, small_grads = _local_step(x[0], loss_target[0], full, small)
    loss = lax.psum(0.5 * jnp.sum(sq) / D_MODEL, ("x", "y", "c"))

    gf = _pack_full_grads(grads)
    sf = _pack_small(small_grads)
    rg, rs = _exchange_halves(gf, sf)
    ca, cab, csa = _add_sibling(gf, rg, sf, rs, core)
    rb, rsb = _scatter_to_owners(cab, csa)
    tb, ts = _add_chips(ca, rb, csa, rsb, chip)
    g_packed, s_packed = _join_halves(tb, ts)
    grad = {**_unpack_shards(g_packed), **_unpack_small(s_packed)}

    delta, new_m, new_v = {}, {}, {}
    for name in NAMES:
        delta[name], new_m[name], new_v[name] = _adamw(
            view(name, weights[name]), grad[name], view(name, first[name]), view(name, second[name]), name="adamw_" + name)

    like = lambda d: [d[name].reshape(weights[name].shape) for name in NAMES]
    return (loss, dx[None], *like(grad), *like(delta), *like(new_m), *like(new_v))
```

```python
import functools

import jax
import jax.numpy as jnp
from jax import lax
from jax.experimental import pallas as pl
from jax.experimental.pallas import tpu as pltpu

F32 = jnp.float32
BF = jnp.bfloat16
MESH = pl.DeviceIdType.MESH

D_MODEL = 1024
N_HEADS = 8
HEAD_DIM = 64
ATTN_W = N_HEADS * HEAD_DIM
SGU_W = 512
N_GROUPS = 8
CHUNK = 128
D_FF = 4096
EPS = 1e-6
Q_SCALE = HEAD_DIM ** -0.5
N_CHIPS = 4
LANES = 128

ADAM_LR = 0.001
ADAM_B1 = 0.9
ADAM_B2 = 0.999
ADAM_EPS = 1e-08
ADAM_WD = 0.01
ADAM_STEP = 10

VMEM_LIMIT = 48 * 1024 * 1024
NEG = -1e30

LANE_ROWSUM = HEAD_DIM
LANE_COLSUM = HEAD_DIM + 3


def _params(*sem):
    return pltpu.CompilerParams(dimension_semantics=sem, vmem_limit_bytes=VMEM_LIMIT)


def _dot(a, b):
    return jnp.dot(a, b, preferred_element_type=F32)


def _dot_nt(a, b):
    return lax.dot_general(a, b, (((1,), (1,)), ((), ())), preferred_element_type=F32)


def _dot_tn(a, b):
    return lax.dot_general(a, b, (((0,), (0,)), ((), ())), preferred_element_type=F32)


def _split3(c):
    hi = c.astype(BF).astype(F32)
    r = c - hi
    mid = r.astype(BF).astype(F32)
    lo = (r - mid).astype(BF).astype(F32)
    return hi, mid, lo


def _gelu(x):
    k = 0.7978845608028654
    return 0.5 * x * (1.0 + jnp.tanh(k * (x + 0.044715 * (x * x * x))))


def _gelu_grad(x):
    k = 0.7978845608028654
    x2 = x * x
    t = jnp.tanh(k * (x + 0.044715 * (x2 * x)))
    return 0.5 * (1.0 + t) + 0.5 * x * (1.0 - t * t) * (k * (1.0 + 3.0 * 0.044715 * x2))


def _rms_bwd(a, g, dy):
    r = lax.rsqrt(jnp.mean(a * a, axis=-1, keepdims=True) + EPS)
    n = a * r
    dn = dy * g
    da = r * (dn - n * jnp.mean(dn * n, axis=-1, keepdims=True))
    return da, dy * n


MM_ROWS = 1024
MM_COLS = 512


def _matmul(pairs, *, nt, out_dtypes, name, tm=MM_ROWS, tn=MM_COLS, epilogue=None, extras=()):
    n_pairs = len(pairs)
    n_extra = len(extras)
    M = pairs[0][0].shape[0]
    N = pairs[0][1].shape[0] if nt else pairs[0][1].shape[1]
    tm, tn = min(tm, M), min(tn, N)
    assert M % tm == 0 and N % tn == 0

    def body(*refs):
        acc = None
        for p in range(n_pairs):
            a_ref, b_ref = refs[2 * p], refs[2 * p + 1]
            d = _dot_nt(a_ref[...], b_ref[...]) if nt else _dot(a_ref[...], b_ref[...])
            acc = d if acc is None else acc + d
        e_refs = refs[2 * n_pairs:2 * n_pairs + n_extra]
        o_refs = refs[2 * n_pairs + n_extra:]
        outs = (acc,) if epilogue is None else epilogue(acc, *[e[...] for e in e_refs])
        for o_ref, o in zip(o_refs, outs, strict=True):
            o_ref[...] = o.astype(o_ref.dtype)

    in_specs, args = [], []
    for a, b in pairs:
        K = a.shape[1]
        in_specs.append(pl.BlockSpec((tm, K), lambda i, j: (i, 0)))
        in_specs.append(pl.BlockSpec((tn, K), lambda i, j: (j, 0)) if nt else pl.BlockSpec((K, tn), lambda i, j: (0, j)))
        args += [a, b]
    for e in extras:
        in_specs.append(pl.BlockSpec((tm, tn), lambda i, j: (i, j)))
        args.append(e)
    outs = pl.pallas_call(
        body, name=name, grid=(M // tm, N // tn), in_specs=in_specs,
        out_specs=[pl.BlockSpec((tm, tn), lambda i, j: (i, j)) for _ in out_dtypes],
        out_shape=[jax.ShapeDtypeStruct((M, N), dt) for dt in out_dtypes],
        compiler_params=_params("parallel", "parallel"),
    )(*args)
    return outs if len(outs) > 1 else outs[0]


def _matmul_tn(a, b, *, name, tm=1024, tn=1024, tk=512):
    T, K1 = a.shape
    N = b.shape[1]
    tm, tn, tk = min(tm, K1), min(tn, N), min(tk, T)
    assert K1 % tm == 0 and N % tn == 0 and T % tk == 0

    def body(a_ref, b_ref, o_ref):
        @pl.when(pl.program_id(2) == 0)
        def _():
            o_ref[...] = jnp.zeros_like(o_ref)

        o_ref[...] += _dot_tn(a_ref[...], b_ref[...])

    return pl.pallas_call(
        body, name=name, grid=(K1 // tm, N // tn, T // tk),
        in_specs=[pl.BlockSpec((tk, tm), lambda i, j, k: (k, i)), pl.BlockSpec((tk, tn), lambda i, j, k: (k, j))],
        out_specs=pl.BlockSpec((tm, tn), lambda i, j, k: (i, j)),
        out_shape=jax.ShapeDtypeStruct((K1, N), F32),
        compiler_params=_params("parallel", "parallel", "arbitrary"),
    )(a, b)


def _branch_merge(ysgu, yattn, w_bs, w_ba, gl, *, tm=MM_ROWS, tn=MM_COLS):
    T = ysgu.shape[0]
    tm = min(tm, T)
    nj = D_MODEL // tn

    def body(ys_ref, ya_ref, wbs_ref, wba_ref, gla_ref, glb_ref, a_ref, b_ref, m_ref):
        a = _dot(ys_ref[...], wbs_ref[...])
        b = _dot(ya_ref[...], wba_ref[...])
        a_ref[...] = a
        b_ref[...] = b
        m_ref[...] = (jax.nn.sigmoid(gla_ref[...]) * a + jax.nn.sigmoid(glb_ref[...]) * b).astype(BF)

    return pl.pallas_call(
        body, name="branch_merge", grid=(T // tm, nj),
        in_specs=[
            pl.BlockSpec((tm, SGU_W), lambda i, j: (i, 0)),
            pl.BlockSpec((tm, ATTN_W), lambda i, j: (i, 0)),
            pl.BlockSpec((SGU_W, tn), lambda i, j: (0, j)),
            pl.BlockSpec((ATTN_W, tn), lambda i, j: (0, j)),
            pl.BlockSpec((tm, tn), lambda i, j: (i, j)),
            pl.BlockSpec((tm, tn), lambda i, j: (i, j + nj)),
        ],
        out_specs=[pl.BlockSpec((tm, tn), lambda i, j: (i, j))] * 3,
        out_shape=[jax.ShapeDtypeStruct((T, D_MODEL), F32), jax.ShapeDtypeStruct((T, D_MODEL), F32),
                   jax.ShapeDtypeStruct((T, D_MODEL), BF)],
        compiler_params=_params("parallel", "parallel"),
    )(ysgu, yattn, w_bs, w_ba, gl, gl)


def _row_spec(tr, width):
    return pl.BlockSpec((tr, width), lambda i: (i, 0))


def _vec_spec(width):
    return pl.BlockSpec((1, width), lambda i: (0, 0))


def _rms_fwd(x, g, *, tr=256):
    T = x.shape[0]
    tr = min(tr, T)

    def body(x_ref, g_ref, o_ref):
        xv = x_ref[...]
        r = lax.rsqrt(jnp.mean(xv * xv, axis=-1, keepdims=True) + EPS)
        o_ref[...] = ((xv * r) * g_ref[...]).astype(BF)

    return pl.pallas_call(
        body, name="rms_fwd", grid=(T // tr,),
        in_specs=[_row_spec(tr, D_MODEL), _vec_spec(D_MODEL)], out_specs=_row_spec(tr, D_MODEL),
        out_shape=jax.ShapeDtypeStruct((T, D_MODEL), BF), compiler_params=_params("parallel"),
    )(x, g)


def _mixer_out_fwd(o, x, g_post, g_pre, *, tr=256):
    T = x.shape[0]
    tr = min(tr, T)

    def body(o_ref, x_ref, gpost_ref, gpre_ref, h1_ref, xn2_ref):
        ov = o_ref[...]
        r = lax.rsqrt(jnp.mean(ov * ov, axis=-1, keepdims=True) + EPS)
        h1 = x_ref[...] + (ov * r) * gpost_ref[...]
        h1_ref[...] = h1
        r2 = lax.rsqrt(jnp.mean(h1 * h1, axis=-1, keepdims=True) + EPS)
        xn2_ref[...] = ((h1 * r2) * gpre_ref[...]).astype(BF)

    return pl.pallas_call(
        body, name="mixer_out_fwd", grid=(T // tr,),
        in_specs=[_row_spec(tr, D_MODEL), _row_spec(tr, D_MODEL), _vec_spec(D_MODEL), _vec_spec(D_MODEL)],
        out_specs=[_row_spec(tr, D_MODEL), _row_spec(tr, D_MODEL)],
        out_shape=[jax.ShapeDtypeStruct((T, D_MODEL), F32), jax.ShapeDtypeStruct((T, D_MODEL), BF)],
        compiler_params=_params("parallel"),
    )(o, x, g_post, g_pre)


def _loss_head(dn, h1, target, g_post, *, tr=256):
    T = dn.shape[0]
    tr = min(tr, T)

    def body(dn_ref, h1_ref, t_ref, g_ref, sq_ref, dy_ref, ddn_ref, dg_ref):
        @pl.when(pl.program_id(0) == 0)
        def _():
            sq_ref[...] = jnp.zeros_like(sq_ref)
            dg_ref[...] = jnp.zeros_like(dg_ref)

        a = dn_ref[...]
        g = g_ref[...]
        r = lax.rsqrt(jnp.mean(a * a, axis=-1, keepdims=True) + EPS)
        err = h1_ref[...] + (a * r) * g - t_ref[...]
        sq_ref[...] += jnp.sum(err * err, axis=0, keepdims=True)
        dy = err * (1.0 / D_MODEL)
        dy_ref[...] = dy
        da, dgp = _rms_bwd(a, g, dy)
        ddn_ref[...] = da.astype(BF)
        dg_ref[...] += jnp.sum(dgp, axis=0, keepdims=True)

    return pl.pallas_call(
        body, name="loss_head", grid=(T // tr,),
        in_specs=[_row_spec(tr, D_MODEL)] * 3 + [_vec_spec(D_MODEL)],
        out_specs=[_vec_spec(D_MODEL), _row_spec(tr, D_MODEL), _row_spec(tr, D_MODEL), _vec_spec(D_MODEL)],
        out_shape=[jax.ShapeDtypeStruct((1, D_MODEL), F32), jax.ShapeDtypeStruct((T, D_MODEL), F32),
                   jax.ShapeDtypeStruct((T, D_MODEL), BF), jax.ShapeDtypeStruct((1, D_MODEL), F32)],
        compiler_params=_params("arbitrary"),
    )(dn, h1, target, g_post)


def _mixer_out_bwd(h1, dxn2, dy, o, g_pre, g_post, *, tr=256):
    T = h1.shape[0]
    tr = min(tr, T)

    def body(h1_ref, dxn2_ref, dy_ref, o_ref, gpre_ref, gpost_ref, dh1_ref, do_ref, dgpre_ref, dgpost_ref):
        @pl.when(pl.program_id(0) == 0)
        def _():
            dgpre_ref[...] = jnp.zeros_like(dgpre_ref)
            dgpost_ref[...] = jnp.zeros_like(dgpost_ref)

        da, dgp = _rms_bwd(h1_ref[...], gpre_ref[...], dxn2_ref[...])
        dh1 = dy_ref[...] + da
        dh1_ref[...] = dh1
        dgpre_ref[...] += jnp.sum(dgp, axis=0, keepdims=True)
        do, dgp2 = _rms_bwd(o_ref[...], gpost_ref[...], dh1)
        do_ref[...] = do.astype(BF)
        dgpost_ref[...] += jnp.sum(dgp2, axis=0, keepdims=True)

    return pl.pallas_call(
        body, name="mixer_out_bwd", grid=(T // tr,),
        in_specs=[_row_spec(tr, D_MODEL)] * 4 + [_vec_spec(D_MODEL)] * 2,
        out_specs=[_row_spec(tr, D_MODEL), _row_spec(tr, D_MODEL), _vec_spec(D_MODEL), _vec_spec(D_MODEL)],
        out_shape=[jax.ShapeDtypeStruct((T, D_MODEL), F32), jax.ShapeDtypeStruct((T, D_MODEL), BF),
                   jax.ShapeDtypeStruct((1, D_MODEL), F32), jax.ShapeDtypeStruct((1, D_MODEL), F32)],
        compiler_params=_params("arbitrary"),
    )(h1, dxn2, dy, o, g_pre, g_post)


def _input_norm_bwd(x, dxn, dh1, g, *, tr=256):
    T = x.shape[0]
    tr = min(tr, T)

    def body(x_ref, dxn_ref, dh1_ref, g_ref, dx_ref, dg_ref):
        @pl.when(pl.program_id(0) == 0)
        def _():
            dg_ref[...] = jnp.zeros_like(dg_ref)

        da, dgp = _rms_bwd(x_ref[...], g_ref[...], dxn_ref[...])
        dx_ref[...] = dh1_ref[...] + da
        dg_ref[...] += jnp.sum(dgp, axis=0, keepdims=True)

    return pl.pallas_call(
        body, name="input_norm_bwd", grid=(T // tr,),
        in_specs=[_row_spec(tr, D_MODEL)] * 3 + [_vec_spec(D_MODEL)],
        out_specs=[_row_spec(tr, D_MODEL), _vec_spec(D_MODEL)],
        out_shape=[jax.ShapeDtypeStruct((T, D_MODEL), F32), jax.ShapeDtypeStruct((1, D_MODEL), F32)],
        compiler_params=_params("arbitrary"),
    )(x, dxn, dh1, g)


def _gate_bwd(dm, a, b, gl, *, tr=256):
    T = dm.shape[0]
    tr = min(tr, T)

    def body(dm_ref, a_ref, b_ref, gla_ref, glb_ref, da_ref, db_ref, dgla_ref, dglb_ref):
        dmv = dm_ref[...]
        ga = jax.nn.sigmoid(gla_ref[...])
        gb = jax.nn.sigmoid(glb_ref[...])
        da_ref[...] = (dmv * ga).astype(BF)
        db_ref[...] = (dmv * gb).astype(BF)
        dgla_ref[...] = (dmv * a_ref[...] * (ga * (1.0 - ga))).astype(BF)
        dglb_ref[...] = (dmv * b_ref[...] * (gb * (1.0 - gb))).astype(BF)

    spec = _row_spec(tr, D_MODEL)
    spec_b = pl.BlockSpec((tr, D_MODEL), lambda i: (i, 1))
    da, db, dgla, dglb = pl.pallas_call(
        body, name="gate_bwd", grid=(T // tr,),
        in_specs=[spec, spec, spec, spec, spec_b], out_specs=[spec] * 4,
        out_shape=[jax.ShapeDtypeStruct((T, D_MODEL), BF)] * 4, compiler_params=_params("parallel"),
    )(dm, a, b, gl, gl)
    return da, db, dgla, dglb


def _sgu_norm(z_tile, g, b):
    gz = _gelu(z_tile)
    u, vv = gz[:, :SGU_W], gz[:, SGU_W:]
    xc = vv - jnp.mean(vv, axis=-1, keepdims=True)
    rstd = lax.rsqrt(jnp.mean(xc * xc, axis=-1, keepdims=True) + EPS)
    xhat = xc * rstd
    return u, xhat, rstd, xhat * g + b


def _sgu_mix(w_ref, v_bf, first_half):
    parts = []
    for p in range(N_GROUPS // 2):
        vp = v_bf[:, p * LANES:(p + 1) * LANES]
        parts.append(jnp.where(first_half, _dot(w_ref[2 * p], vp), _dot(w_ref[2 * p + 1], vp)))
    return jnp.concatenate(parts, axis=1)


def _sgu_fwd(z, g_sgu, b_sgu, ws, bias_plane, *, tm=512):
    T = z.shape[0]
    tm = min(tm, T)

    def body(z_ref, g_ref, b_ref, ws_ref, bp_ref, y_ref):
        u, _, _, vn = _sgu_norm(z_ref[...], g_ref[...], b_ref[...])
        vn_bf = vn.astype(BF)
        first_half = lax.broadcasted_iota(jnp.int32, (CHUNK, LANES), 1) < HEAD_DIM
        for c in range(tm // CHUNK):
            rows = slice(c * CHUNK, (c + 1) * CHUNK)
            s = _sgu_mix(ws_ref, vn_bf[rows, :], first_half) + bp_ref[...]
            y_ref[rows, :] = (u[rows, :] * s).astype(BF)

    return pl.pallas_call(
        body, name="sgu_fwd", grid=(T // tm,),
        in_specs=[_row_spec(tm, 2 * SGU_W), _vec_spec(SGU_W), _vec_spec(SGU_W),
                  pl.BlockSpec((N_GROUPS, CHUNK, CHUNK), lambda i: (0, 0, 0)),
                  pl.BlockSpec((CHUNK, SGU_W), lambda i: (0, 0))],
        out_specs=_row_spec(tm, SGU_W), out_shape=jax.ShapeDtypeStruct((T, SGU_W), BF),
        compiler_params=_params("parallel"),
    )(z, g_sgu, b_sgu, ws, bias_plane)


def _sgu_bwd(dy, z, g_sgu, b_sgu, ws, ws_t, bias_plane, *, tm=512):
    T = z.shape[0]
    tm = min(tm, T)
    n_steps = T // tm

    def body(dy_ref, z_ref, g_ref, b_ref, ws_ref, wst_ref, bp_ref, dz_ref, dws_ref, dbs_ref, dg_ref, db_ref, dbp_ref):
        step = pl.program_id(0)

        @pl.when(step == 0)
        def _():
            dws_ref[...] = jnp.zeros_like(dws_ref)
            dg_ref[...] = jnp.zeros_like(dg_ref)
            db_ref[...] = jnp.zeros_like(db_ref)
            dbp_ref[...] = jnp.zeros_like(dbp_ref)

        g = g_ref[...]
        zt = z_ref[...]
        u, xhat, rstd, vn = _sgu_norm(zt, g, b_ref[...])
        vn_bf = vn.astype(BF)
        first_half = lax.broadcasted_iota(jnp.int32, (CHUNK, LANES), 1) < HEAD_DIM
        dyv = dy_ref[...]
        dg_acc = jnp.zeros((1, SGU_W), F32)
        db_acc = jnp.zeros((1, SGU_W), F32)
        for c in range(tm // CHUNK):
            rows = slice(c * CHUNK, (c + 1) * CHUNK)
            v_c = vn_bf[rows, :]
            s = _sgu_mix(ws_ref, v_c, first_half) + bp_ref[...]
            dy_c = dyv[rows, :]
            du = dy_c * s
            dsv = dy_c * u[rows, :]
            dbp_ref[...] += dsv
            ds_bf = dsv.astype(BF)
            zero = jnp.zeros((CHUNK, LANES), BF)
            for p in range(N_GROUPS // 2):
                dsp = ds_bf[:, p * LANES:(p + 1) * LANES]
                vp = v_c[:, p * LANES:(p + 1) * LANES]
                dws_ref[2 * p] += _dot_nt(jnp.where(first_half, dsp, zero), vp)
                dws_ref[2 * p + 1] += _dot_nt(jnp.where(first_half, zero, dsp), vp)
            dvn = _sgu_mix(wst_ref, ds_bf, first_half)
            xh = xhat[rows, :]
            dxh = dvn * g
            dvv = rstd[rows, :] * (dxh - jnp.mean(dxh, axis=-1, keepdims=True)
                                   - xh * jnp.mean(dxh * xh, axis=-1, keepdims=True))
            dg_acc += jnp.sum(dvn * xh, axis=0, keepdims=True)
            db_acc += jnp.sum(dvn, axis=0, keepdims=True)
            dgz = jnp.concatenate([du, dvv], axis=1)
            dz_ref[rows, :] = (dgz * _gelu_grad(zt[rows, :])).astype(BF)
        dg_ref[...] += dg_acc
        db_ref[...] += db_acc

        @pl.when(step == n_steps - 1)
        def _():
            r = lax.broadcasted_iota(jnp.int32, (CHUNK, CHUNK), 0)
            cidx = lax.broadcasted_iota(jnp.int32, (CHUNK, CHUNK), 1)
            causal = (cidx <= r).astype(F32)
            for gi in range(N_GROUPS):
                dws_ref[gi] = dws_ref[gi] * causal
            lane = lax.broadcasted_iota(jnp.int32, (CHUNK, LANES), 1)
            out = jnp.zeros((CHUNK, LANES), F32)
            dbp = dbp_ref[...]
            for gi in range(N_GROUPS):
                col = jnp.sum(dbp[:, gi * HEAD_DIM:(gi + 1) * HEAD_DIM], axis=1, keepdims=True)
                out = jnp.where(lane == gi, col, out)
            dbs_ref[...] = out

    w_spec = pl.BlockSpec((N_GROUPS, CHUNK, CHUNK), lambda i: (0, 0, 0))
    plane = pl.BlockSpec((CHUNK, SGU_W), lambda i: (0, 0))
    return pl.pallas_call(
        body, name="sgu_bwd", grid=(n_steps,),
        in_specs=[_row_spec(tm, SGU_W), _row_spec(tm, 2 * SGU_W), _vec_spec(SGU_W), _vec_spec(SGU_W), w_spec, w_spec, plane],
        out_specs=[_row_spec(tm, 2 * SGU_W), w_spec, pl.BlockSpec((CHUNK, LANES), lambda i: (0, 0)),
                   _vec_spec(SGU_W), _vec_spec(SGU_W)],
        out_shape=[jax.ShapeDtypeStruct((T, 2 * SGU_W), BF), jax.ShapeDtypeStruct((N_GROUPS, CHUNK, CHUNK), F32),
                   jax.ShapeDtypeStruct((CHUNK, LANES), F32), jax.ShapeDtypeStruct((1, SGU_W), F32),
                   jax.ShapeDtypeStruct((1, SGU_W), F32)],
        scratch_shapes=[pltpu.VMEM((CHUNK, SGU_W), F32)],
        compiler_params=_params("arbitrary"),
    )(dy, z, g_sgu, b_sgu, ws, ws_t, bias_plane)


def _tri(n, upper):
    r = lax.broadcasted_iota(jnp.int32, (n, n), 0)
    c = lax.broadcasted_iota(jnp.int32, (n, n), 1)
    return ((c >= r) if upper else (c <= r)).astype(BF)


def _scan_dot(tri, x):
    hi, mid, lo = _split3(x)
    return (_dot(tri, hi.astype(BF)) + _dot(tri, mid.astype(BF))) + _dot(tri, lo.astype(BF))


def _with_lanes(base, lane, start, cols):
    out = base
    for k, col in enumerate(cols):
        if col is not None:
            out = jnp.where(lane == start + k, col, out)
    return out


def _logit_bound(q_norm, k_norm):
    return NORM_SLACK * q_norm * k_norm + 1.0


ATTN_TILE = 512
SKIP_BELOW = -110.0
NORM_SLACK = 1.001
BOUNDED_GAP = 60.0


def _attn_prep(qkv, fl, b_forget, *, tp=ATTN_TILE):
    T = qkv.shape[0]
    tp = min(tp, T)

    def body(qkv_ref, fl_ref, bf_ref, qf_ref, kl_ref, vl_ref, st_ref, carry_ref, kmax_ref):
        @pl.when(pl.program_id(0) == 0)
        def _():
            carry_ref[...] = jnp.zeros_like(carry_ref)
            kmax_ref[...] = jnp.zeros_like(kmax_ref)

        x = fl_ref[...] + bf_ref[...]
        logf = jnp.minimum(x, 0.0) - jnp.log(1.0 + jnp.exp(-jnp.abs(x)))
        cum = _scan_dot(_tri(tp, upper=False), logf) + carry_ref[...]
        carry_ref[...] = cum[tp - 1:tp, :]
        lane = lax.broadcasted_iota(jnp.int32, (tp, HEAD_DIM), 1)
        ones3 = jnp.where(lane < 3, 1.0, 0.0)
        qkvv = qkv_ref[...]
        st_row = lax.broadcasted_iota(jnp.int32, (N_HEADS, LANES), 0)
        st_lane = lax.broadcasted_iota(jnp.int32, (N_HEADS, LANES), 1)
        stats = jnp.zeros((N_HEADS, LANES), F32)
        kmax_lane = lax.broadcasted_iota(jnp.int32, (1, LANES), 1)
        for h in range(N_HEADS):
            ch = cum[:, h:h + 1]
            c3 = _split3(ch)
            qh = qkvv[:, h * HEAD_DIM:(h + 1) * HEAD_DIM].astype(F32) * Q_SCALE
            kh = qkvv[:, ATTN_W + h * HEAD_DIM:ATTN_W + (h + 1) * HEAD_DIM].astype(F32)
            vh = qkvv[:, 2 * ATTN_W + h * HEAD_DIM:2 * ATTN_W + (h + 1) * HEAD_DIM].astype(F32)
            q_norm = jnp.sqrt(jnp.sum(qh * qh, axis=1, keepdims=True))
            qn = jnp.max(q_norm, axis=0, keepdims=True)
            kn = jnp.sqrt(jnp.max(jnp.sum(kh * kh, axis=1, keepdims=True), axis=0, keepdims=True))
            k_seen = jnp.maximum(kmax_ref[:, h:h + 1], kn)
            kmax_ref[...] = jnp.where(kmax_lane == h, k_seen, kmax_ref[...])
            bound3 = _split3(-_logit_bound(q_norm, k_seen))
            ext_q = _with_lanes(jnp.where((lane >= 3) & (lane < 6), 1.0, 0.0), lane, 0, list(c3) + [None] * 3 + list(bound3))
            ext_k = _with_lanes(jnp.where((lane < 3) | ((lane >= 6) & (lane < 9)), 1.0, 0.0), lane, 3, [-c for c in c3])
            qf_ref[h] = jnp.concatenate([qh, ext_q], axis=1).astype(BF)
            kl_ref[h] = jnp.concatenate([kh, ext_k], axis=1).astype(BF)
            vl_ref[h] = jnp.concatenate([vh, ones3], axis=1).astype(BF)
            tile_stats = (qn, kn, jnp.max(ch, axis=0, keepdims=True), jnp.min(ch, axis=0, keepdims=True), k_seen)
            for k, val in enumerate(tile_stats):
                stats = jnp.where((st_row == h) & (st_lane == k), val, stats)
        st_ref[0] = stats

    head_spec = pl.BlockSpec((N_HEADS, tp, LANES), lambda i: (0, i, 0))
    return pl.pallas_call(
        body, name="attn_prep", grid=(T // tp,),
        in_specs=[_row_spec(tp, 3 * ATTN_W), _row_spec(tp, LANES), _vec_spec(LANES)],
        out_specs=[head_spec] * 3 + [pl.BlockSpec((1, N_HEADS, LANES), lambda i: (i, 0, 0))],
        out_shape=[jax.ShapeDtypeStruct((N_HEADS, T, LANES), BF)] * 3 + [jax.ShapeDtypeStruct((T // tp, N_HEADS, LANES), F32)],
        scratch_shapes=[pltpu.VMEM((1, LANES), F32), pltpu.VMEM((1, LANES), F32)], compiler_params=_params("arbitrary"),
    )(qkv, fl, b_forget)


def _attn_ranges(stats):
    qn, kn, cmax, cmin, k_seen = (stats[:, :, k].T for k in range(5))
    n = qn.shape[1]
    bounded = (2.0 * _logit_bound(qn, k_seen) <= BOUNDED_GAP).reshape(N_HEADS // 2, 2, n).all(axis=1)
    reach = NORM_SLACK * qn * (jnp.max(kn, axis=1, keepdims=True) + kn) + cmax
    i = jnp.arange(n)[None, :, None]
    j = jnp.arange(n)[None, None, :]
    need = ((reach[:, :, None] - cmin[:, None, :] >= SKIP_BELOW) | (i == j)) & (j <= i)
    first = jnp.min(jnp.where(need, j, n), axis=2).reshape(N_HEADS // 2, 2, n).min(axis=1)
    last = jnp.max(jnp.where(need, i, -1), axis=1).reshape(N_HEADS // 2, 2, n).max(axis=1)
    return first.reshape(-1).astype(F32), last.reshape(-1).astype(F32), bounded.reshape(-1).astype(F32)


def _pair_block(t):
    return pl.BlockSpec((2, t, LANES), lambda p, i, *_: (p, i, 0))


def _pair_full(T):
    return pl.BlockSpec((2, T, LANES), lambda p, i, *_: (p, 0, 0))


def _packed_block(t):
    return pl.BlockSpec((t, LANES), lambda p, i, *_: (i, p))


def _causal(t, keys_in_rows=False):
    r = lax.broadcasted_iota(jnp.int32, (t, t), 0)
    c = lax.broadcasted_iota(jnp.int32, (t, t), 1)
    return (r <= c) if keys_in_rows else (c <= r)


def _tile_rows(j, t):
    return pl.ds(pl.multiple_of(j * t, t), t)


def _attn_call(body, name, tile_scalars, operands, in_specs, out_specs, out_shape, scratch_shapes, n_tiles):
    return pl.pallas_call(
        body, name=name,
        grid_spec=pltpu.PrefetchScalarGridSpec(
            num_scalar_prefetch=len(tile_scalars), grid=(N_HEADS // 2, n_tiles), in_specs=in_specs, out_specs=out_specs,
            scratch_shapes=scratch_shapes),
        out_shape=out_shape, compiler_params=_params("parallel", "arbitrary"),
    )(*tile_scalars, *operands)


def _attn_fwd(qf, kl, vl, first, bounded, *, tq=ATTN_TILE):
    T = qf.shape[1]
    tq = min(tq, T)
    n = T // tq

    def body(first_ref, bounded_ref, qf_ref, kl_ref, vl_ref, o_ref, of_ref, ql_ref, m_ref, acc_ref):
        i = pl.program_id(1)
        tile = pl.program_id(0) * n + i
        start = first_ref[tile].astype(jnp.int32)
        is_bounded = bounded_ref[tile] > 0.5
        acc_ref[...] = jnp.zeros_like(acc_ref)
        diagonal = _tile_rows(i, tq)
        causal = _causal(tq)

        def logits(hh, rows):
            return _dot_nt(qf_ref[hh], kl_ref[hh, rows, :])

        @pl.when(is_bounded)
        def _():
            m_ref[...] = jnp.zeros_like(m_ref)

            def update(hh, s, rows):
                acc_ref[hh] += _dot(jnp.exp(s).astype(BF), vl_ref[hh, rows, :])

            def step(j, carry):
                for hh in range(2):
                    update(hh, logits(hh, _tile_rows(j, tq)), _tile_rows(j, tq))
                return carry

            lax.fori_loop(start, i, step, 0)
            for hh in range(2):
                update(hh, jnp.where(causal, logits(hh, diagonal), NEG), diagonal)

        @pl.when(jnp.logical_not(is_bounded))
        def _():
            m_ref[...] = jnp.full_like(m_ref, NEG)

            def update(hh, s, rows):
                m_old = m_ref[hh]
                m_new = jnp.maximum(m_old, jnp.max(s, axis=1, keepdims=True))
                p = jnp.exp(s - m_new)
                acc_ref[hh] = jnp.exp(m_old - m_new) * acc_ref[hh] + _dot(p.astype(BF), vl_ref[hh, rows, :])
                m_ref[hh] = m_new

            def step(j, carry):
                for hh in range(2):
                    update(hh, logits(hh, _tile_rows(j, tq)), _tile_rows(j, tq))
                return carry

            lax.fori_loop(start, i, step, 0)
            for hh in range(2):
                update(hh, jnp.where(causal, logits(hh, diagonal), NEG), diagonal)

        lane = lax.broadcasted_iota(jnp.int32, (tq, LANES), 1)
        outs = []
        for hh in range(2):
            q = qf_ref[hh].astype(F32)
            acc = acc_ref[hh]
            l = acc[:, HEAD_DIM:HEAD_DIM + 1]
            outs.append(acc[:, :HEAD_DIM] / l)
            at = HEAD_DIM + 6
            neg_bound = (q[:, at:at + 1] + q[:, at + 1:at + 2]) + q[:, at + 2:at + 3]
            ql_ref[hh] = _with_lanes(q, lane, at, _split3(neg_bound - (m_ref[hh] + jnp.log(l)))).astype(BF)
        o = jnp.concatenate(outs, axis=1)
        o_ref[...] = o.astype(BF)
        of_ref[...] = o

    return _attn_call(
        body, "attn_fwd", (first, bounded), (qf, kl, vl), [_pair_block(tq), _pair_full(T), _pair_full(T)],
        [_packed_block(tq), _packed_block(tq), _pair_block(tq)],
        [jax.ShapeDtypeStruct((T, ATTN_W), BF), jax.ShapeDtypeStruct((T, ATTN_W), F32),
         jax.ShapeDtypeStruct((N_HEADS, T, LANES), BF)],
        [pltpu.VMEM((2, tq, 1), F32), pltpu.VMEM((2, tq, LANES), F32)], n)


def _attn_bwd_prep(dya, of, *, tr=256):
    T = dya.shape[0]
    tr = min(tr, T)

    def body(d_ref, o_ref, do_ref):
        lane = lax.broadcasted_iota(jnp.int32, (tr, HEAD_DIM), 1)
        dv, ov = d_ref[...], o_ref[...]
        for h in range(N_HEADS):
            d = dv[:, h * HEAD_DIM:(h + 1) * HEAD_DIM]
            delta = jnp.sum(d * ov[:, h * HEAD_DIM:(h + 1) * HEAD_DIM], axis=1, keepdims=True)
            ext = _with_lanes(jnp.zeros((tr, HEAD_DIM), F32), lane, 0, _split3(-delta))
            do_ref[h] = jnp.concatenate([d, ext], axis=1).astype(BF)

    return pl.pallas_call(
        body, name="attn_bwd_prep", grid=(T // tr,),
        in_specs=[_row_spec(tr, ATTN_W), _row_spec(tr, ATTN_W)],
        out_specs=pl.BlockSpec((N_HEADS, tr, LANES), lambda i: (0, i, 0)),
        out_shape=jax.ShapeDtypeStruct((N_HEADS, T, LANES), BF), compiler_params=_params("parallel"),
    )(dya, of)


def _attn_bwd_dq(ql, do, kl, vl, first, *, tq=ATTN_TILE):
    T = ql.shape[1]
    tq = min(tq, T)
    n = T // tq

    def body(first_ref, ql_ref, do_ref, kl_ref, vl_ref, dq_ref, ext_ref, acc_ref):
        i = pl.program_id(1)
        acc_ref[...] = jnp.zeros_like(acc_ref)

        def block(hh, rows, mask):
            kj = kl_ref[hh, rows, :]
            p = jnp.exp(_dot_nt(ql_ref[hh], kj))
            if mask is not None:
                p = jnp.where(mask, p, 0.0)
            ds = p * _dot_nt(do_ref[hh], vl_ref[hh, rows, :])
            acc_ref[hh] += _dot(ds.astype(BF), kj)

        def step(j, carry):
            for hh in range(2):
                block(hh, _tile_rows(j, tq), None)
            return carry

        lax.fori_loop(first_ref[pl.program_id(0) * n + i].astype(jnp.int32), i, step, 0)
        causal = _causal(tq)
        for hh in range(2):
            block(hh, _tile_rows(i, tq), causal)
        dq_ref[...] = jnp.concatenate([acc_ref[hh][:, :HEAD_DIM] * Q_SCALE for hh in range(2)], axis=1).astype(BF)
        ext_ref[...] = jnp.concatenate([acc_ref[hh][:, HEAD_DIM:] for hh in range(2)], axis=1)

    return _attn_call(
        body, "attn_bwd_dq", (first,), (ql, do, kl, vl),
        [_pair_block(tq), _pair_block(tq), _pair_full(T), _pair_full(T)], [_packed_block(tq), _packed_block(tq)],
        [jax.ShapeDtypeStruct((T, ATTN_W), BF), jax.ShapeDtypeStruct((T, ATTN_W), F32)],
        [pltpu.VMEM((2, tq, LANES), F32)], n)


def _attn_bwd_dkv(kl, vl, ql, do, last, *, tk=ATTN_TILE):
    T = ql.shape[1]
    tk = min(tk, T)
    n = T // tk

    def body(last_ref, kl_ref, vl_ref, ql_ref, do_ref, dk_ref, dv_ref, ext_ref, dk_acc, dv_acc):
        j = pl.program_id(1)
        dk_acc[...] = jnp.zeros_like(dk_acc)
        dv_acc[...] = jnp.zeros_like(dv_acc)

        def block(hh, rows, mask):
            qi, di = ql_ref[hh, rows, :], do_ref[hh, rows, :]
            p_t = jnp.exp(_dot_nt(kl_ref[hh], qi))
            if mask is not None:
                p_t = jnp.where(mask, p_t, 0.0)
            ds_t = p_t * _dot_nt(vl_ref[hh], di)
            dk_acc[hh] += _dot(ds_t.astype(BF), qi)
            dv_acc[hh] += _dot(p_t.astype(BF), di)

        causal_t = _causal(tk, keys_in_rows=True)
        for hh in range(2):
            block(hh, _tile_rows(j, tk), causal_t)

        def step(i, carry):
            for hh in range(2):
                block(hh, _tile_rows(i, tk), None)
            return carry

        lax.fori_loop(j + 1, last_ref[pl.program_id(0) * n + j].astype(jnp.int32) + 1, step, 0)
        dk_ref[...] = jnp.concatenate([dk_acc[hh][:, :HEAD_DIM] for hh in range(2)], axis=1).astype(BF)
        dv_ref[...] = jnp.concatenate([dv_acc[hh][:, :HEAD_DIM] for hh in range(2)], axis=1).astype(BF)
        ext_ref[...] = jnp.concatenate([dk_acc[hh][:, HEAD_DIM:] for hh in range(2)], axis=1)

    return _attn_call(
        body, "attn_bwd_dkv", (last,), (kl, vl, ql, do),
        [_pair_block(tk), _pair_block(tk), _pair_full(T), _pair_full(T)], [_packed_block(tk)] * 3,
        [jax.ShapeDtypeStruct((T, ATTN_W), BF), jax.ShapeDtypeStruct((T, ATTN_W), BF),
         jax.ShapeDtypeStruct((T, ATTN_W), F32)],
        [pltpu.VMEM((2, tk, LANES), F32), pltpu.VMEM((2, tk, LANES), F32)], n)


def _forget_bwd(ext_q, ext_k, fl, b_forget, *, tp=256):
    T = fl.shape[0]
    tp = min(tp, T)
    n = T // tp

    def body(eq_ref, ek_ref, fl_ref, bf_ref, dfl_ref, dbf_ref, carry_ref):
        @pl.when(pl.program_id(0) == 0)
        def _():
            carry_ref[...] = jnp.zeros_like(carry_ref)
            dbf_ref[...] = jnp.zeros_like(dbf_ref)

        lane = lax.broadcasted_iota(jnp.int32, (tp, LANES), 1)
        eq, ek = eq_ref[...], ek_ref[...]
        cols = [eq[:, h * HEAD_DIM:h * HEAD_DIM + 1] - ek[:, h * HEAD_DIM + 3:h * HEAD_DIM + 4] for h in range(N_HEADS)]
        dcum = _with_lanes(jnp.zeros((tp, LANES), F32), lane, 0, cols)
        suffix = _scan_dot(_tri(tp, upper=True), dcum) + carry_ref[...]
        carry_ref[...] = suffix[0:1, :]
        x = fl_ref[...] + bf_ref[...]
        dfl = jnp.where(lane < N_HEADS, suffix / (1.0 + jnp.exp(x)), 0.0)
        dfl_ref[...] = dfl.astype(BF)
        dbf_ref[...] += jnp.sum(dfl, axis=0, keepdims=True)

    rev = lambda w: pl.BlockSpec((tp, w), lambda i: (n - 1 - i, 0))
    return pl.pallas_call(
        body, name="forget_bwd", grid=(n,),
        in_specs=[rev(ATTN_W), rev(ATTN_W), rev(LANES), _vec_spec(LANES)],
        out_specs=[rev(LANES), _vec_spec(LANES)],
        out_shape=[jax.ShapeDtypeStruct((T, LANES), BF), jax.ShapeDtypeStruct((1, LANES), F32)],
        scratch_shapes=[pltpu.VMEM((1, LANES), F32)], compiler_params=_params("arbitrary"),
    )(ext_q, ext_k, fl, b_forget)


def _adamw(w, g, m, v, *, name, tr=256):
    rows, cols = w.shape
    tr = tr if rows % tr == 0 else rows

    def body(w_ref, g_ref, m_ref, v_ref, d_ref, nm_ref, nv_ref):
        gv = g_ref[...]
        nm = ADAM_B1 * m_ref[...] + (1.0 - ADAM_B1) * gv
        nv = ADAM_B2 * v_ref[...] + (1.0 - ADAM_B2) * (gv * gv)
        m_hat = nm / (1.0 - ADAM_B1 ** ADAM_STEP)
        v_hat = nv / (1.0 - ADAM_B2 ** ADAM_STEP)
        d_ref[...] = -ADAM_LR * (m_hat / (jnp.sqrt(v_hat) + ADAM_EPS) + ADAM_WD * w_ref[...])
        nm_ref[...] = nm
        nv_ref[...] = nv

    spec = pl.BlockSpec((tr, cols), lambda i: (i, 0))
    return pl.pallas_call(
        body, name=name, grid=(rows // tr,), in_specs=[spec] * 4, out_specs=[spec] * 3,
        out_shape=[jax.ShapeDtypeStruct((rows, cols), F32)] * 3, compiler_params=_params("parallel"),
    )(w, g, m, v)


HBM = pl.BlockSpec(memory_space=pltpu.HBM)


def _place():
    x, y, c = lax.axis_index("x"), lax.axis_index("y"), lax.axis_index("c")
    others = [(1 - x, y), (x, 1 - y), (1 - x, 1 - y)]
    return x, y, c, others


def _chip(xy):
    return 2 * xy[0] + xy[1]


def _gather_weights(wp):
    R = wp.shape[0]
    Rh = R // 2

    def body(w_ref, g_ref, send_sems, recv_sems):
        x, y, c, others = _place()
        sibling = (x, y, 1 - c)
        mine_rows = pl.ds(pl.multiple_of(c * Rh, 16), Rh)
        sibling_rows = pl.ds(pl.multiple_of((1 - c) * Rh, 16), Rh)

        def copy(k, src, dst, to):
            return pltpu.make_async_remote_copy(src_ref=src, dst_ref=dst, send_sem=send_sems.at[k], recv_sem=recv_sems.at[k],
                                                device_id=to, device_id_type=MESH)

        first = [copy(j, w_ref.at[mine_rows, :], g_ref.at[_chip((x, y)), mine_rows, :], (*o, c)) for j, o in enumerate(others)]
        for cp in first:
            cp.start()
        passed = []
        for j, o in enumerate(others):
            landed = g_ref.at[_chip(o), mine_rows, :]
            copy(j, landed, landed, (*o, c)).wait_recv()
            passed.append(copy(3 + j, landed, landed, sibling))
            passed[-1].start()
        for j, o in enumerate(others):
            landed = g_ref.at[_chip(o), sibling_rows, :]
            copy(3 + j, landed, landed, sibling).wait_recv()
        for cp in first + passed:
            cp.wait_send()

    gathered = pl.pallas_call(
        body, name="gather_weights", in_specs=[HBM], out_specs=HBM,
        out_shape=jax.ShapeDtypeStruct((N_CHIPS, R, LANES), wp.dtype),
        scratch_shapes=[pltpu.SemaphoreType.DMA((6,)), pltpu.SemaphoreType.DMA((6,))],
    )(wp)
    own_slot = 2 * lax.axis_index("x") + lax.axis_index("y")
    return lax.dynamic_update_slice(gathered, wp[None], (own_slot, 0, 0))


def _exchange_halves(gf, sf):
    Rh = gf.shape[1] // 2
    Rsh = sf.shape[0] // 2

    def body(g_ref, s_ref, rg_ref, rs_ref, send_sems, recv_sems):
        x, y, c, _ = _place()
        sibling = (x, y, 1 - c)
        big = pltpu.make_async_remote_copy(
            src_ref=g_ref.at[:, pl.ds(pl.multiple_of((1 - c) * Rh, 8), Rh), :], dst_ref=rg_ref,
            send_sem=send_sems.at[0], recv_sem=recv_sems.at[0], device_id=sibling, device_id_type=MESH)
        small = pltpu.make_async_remote_copy(
            src_ref=s_ref.at[pl.ds(pl.multiple_of((1 - c) * Rsh, 8), Rsh), :], dst_ref=rs_ref,
            send_sem=send_sems.at[1], recv_sem=recv_sems.at[1], device_id=sibling, device_id_type=MESH)
        big.start()
        small.start()
        big.wait()
        small.wait()

    return pl.pallas_call(
        body, name="exchange_halves", in_specs=[HBM, HBM], out_specs=[HBM, HBM],
        out_shape=[jax.ShapeDtypeStruct((N_CHIPS, Rh, LANES), F32), jax.ShapeDtypeStruct((Rsh, LANES), F32)],
        scratch_shapes=[pltpu.SemaphoreType.DMA((2,)), pltpu.SemaphoreType.DMA((2,))],
    )(gf, sf)


def _scatter_to_owners(cab, csa):
    Rh = cab.shape[1]
    Rsh = csa.shape[0]

    def body(b_ref, s_ref, rb_ref, rs_ref, send_sems, recv_sems):
        x, y, c, others = _place()
        me = _chip((x, y))
        sends = []
        for j, o in enumerate(others):
            sends.append(pltpu.make_async_remote_copy(
                src_ref=b_ref.at[_chip(o)], dst_ref=rb_ref.at[me], send_sem=send_sems.at[j], recv_sem=recv_sems.at[j],
                device_id=(*o, c), device_id_type=MESH))
            sends.append(pltpu.make_async_remote_copy(
                src_ref=s_ref, dst_ref=rs_ref.at[me], send_sem=send_sems.at[3 + j], recv_sem=recv_sems.at[3 + j],
                device_id=(*o, c), device_id_type=MESH))
        for cp in sends:
            cp.start()
        for j, o in enumerate(others):
            pltpu.make_async_remote_copy(
                src_ref=b_ref.at[me], dst_ref=rb_ref.at[_chip(o)], send_sem=send_sems.at[j], recv_sem=recv_sems.at[j],
                device_id=(*o, c), device_id_type=MESH).wait_recv()
            pltpu.make_async_remote_copy(
                src_ref=s_ref, dst_ref=rs_ref.at[_chip(o)], send_sem=send_sems.at[3 + j], recv_sem=recv_sems.at[3 + j],
                device_id=(*o, c), device_id_type=MESH).wait_recv()
        for cp in sends:
            cp.wait_send()

    return pl.pallas_call(
        body, name="scatter_to_owners", in_specs=[HBM, HBM], out_specs=[HBM, HBM],
        out_shape=[jax.ShapeDtypeStruct((N_CHIPS, Rh, LANES), BF), jax.ShapeDtypeStruct((N_CHIPS, Rsh, LANES), F32)],
        scratch_shapes=[pltpu.SemaphoreType.DMA((6,)), pltpu.SemaphoreType.DMA((6,))],
    )(cab, csa)


def _join_halves(tb, ts):
    def body(b_ref, s_ref, gb_ref, gs_ref, send_sems, recv_sems):
        x, y, c, _ = _place()
        sibling = (x, y, 1 - c)
        big = pltpu.make_async_remote_copy(src_ref=b_ref, dst_ref=gb_ref, send_sem=send_sems.at[0],
                                           recv_sem=recv_sems.at[0], device_id=sibling, device_id_type=MESH)
        small = pltpu.make_async_remote_copy(src_ref=s_ref, dst_ref=gs_ref, send_sem=send_sems.at[1],
                                             recv_sem=recv_sems.at[1], device_id=sibling, device_id_type=MESH)
        big.start()
        small.start()
        big.wait()
        small.wait()

    other_b, other_s = pl.pallas_call(
        body, name="join_halves", in_specs=[HBM, HBM], out_specs=[HBM, HBM],
        out_shape=[jax.ShapeDtypeStruct(tb.shape, F32), jax.ShapeDtypeStruct(ts.shape, F32)],
        scratch_shapes=[pltpu.SemaphoreType.DMA((2,)), pltpu.SemaphoreType.DMA((2,))],
    )(tb, ts)
    core = lax.axis_index("c")

    def in_row_order(mine, other):
        rows = mine.shape[0]
        out = lax.dynamic_update_slice(jnp.zeros((2 * rows, LANES), F32), mine, (core * rows, 0))
        return lax.dynamic_update_slice(out, other, ((1 - core) * rows, 0))

    return in_row_order(tb, other_b), in_row_order(ts, other_s)


def _row_tile(rows, cap=1152, mult=16):
    return max(t for t in range(mult, min(rows, cap) + 1, mult) if rows % t == 0)


def _add_sibling(gf, rg, sf, rs, core):
    Rh = rg.shape[1]
    Rsh = rs.shape[0]
    tr = _row_tile(Rh)
    nb = Rh // tr

    def big_body(core_ref, g_ref, r_ref, o_ref, ob_ref):
        s = g_ref[...] + r_ref[...]
        o_ref[...] = s
        ob_ref[...] = s.astype(BF)

    spec = pl.BlockSpec((N_CHIPS, tr, LANES), lambda i, core_ref: (0, i, 0))
    ca, cab = pl.pallas_call(
        big_body, name="add_sibling",
        grid_spec=pltpu.PrefetchScalarGridSpec(
            num_scalar_prefetch=1, grid=(nb,),
            in_specs=[pl.BlockSpec((N_CHIPS, tr, LANES), lambda i, core_ref: (0, core_ref[0] * nb + i, 0)), spec],
            out_specs=[spec, spec]),
        out_shape=[jax.ShapeDtypeStruct((N_CHIPS, Rh, LANES), F32), jax.ShapeDtypeStruct((N_CHIPS, Rh, LANES), BF)],
        compiler_params=_params("parallel"),
    )(core, gf, rg)

    def small_body(core_ref, s_ref, r_ref, o_ref):
        o_ref[...] = s_ref[...] + r_ref[...]

    sspec = pl.BlockSpec((Rsh, LANES), lambda i, core_ref: (0, 0))
    csa = pl.pallas_call(
        small_body, name="add_sibling_small",
        grid_spec=pltpu.PrefetchScalarGridSpec(
            num_scalar_prefetch=1, grid=(1,),
            in_specs=[pl.BlockSpec((Rsh, LANES), lambda i, core_ref: (core_ref[0], 0)), sspec], out_specs=sspec),
        out_shape=jax.ShapeDtypeStruct((Rsh, LANES), F32), compiler_params=_params("arbitrary"),
    )(core, sf, rs)
    return ca, cab, csa


def _add_chips(ca, rb, csa, rsb, chip):
    Rh = ca.shape[1]
    tr = _row_tile(Rh)
    Rsh = rsb.shape[1]

    def written(k, chip_ref):
        return jnp.where(chip_ref[0] == k, (k + 1) % N_CHIPS, k)

    def big_body(chip_ref, own_ref, *refs):
        o_ref = refs[N_CHIPS]
        acc = own_ref[0]
        for k in range(N_CHIPS):
            acc = acc + jnp.where(chip_ref[0] == k, 0.0, refs[k][0].astype(F32))
        o_ref[...] = acc

    tb = pl.pallas_call(
        big_body, name="add_chips",
        grid_spec=pltpu.PrefetchScalarGridSpec(
            num_scalar_prefetch=1, grid=(Rh // tr,),
            in_specs=[pl.BlockSpec((1, tr, LANES), lambda i, chip_ref: (chip_ref[0], i, 0))]
            + [pl.BlockSpec((1, tr, LANES), functools.partial(lambda i, chip_ref, k: (written(k, chip_ref), i, 0), k=k))
               for k in range(N_CHIPS)],
            out_specs=pl.BlockSpec((tr, LANES), lambda i, chip_ref: (i, 0))),
        out_shape=jax.ShapeDtypeStruct((Rh, LANES), F32), compiler_params=_params("parallel"),
    )(chip, ca, *([rb] * N_CHIPS))

    def small_body(chip_ref, own_ref, *refs):
        o_ref = refs[N_CHIPS]
        terms = [jnp.where(chip_ref[0] == k, own_ref[...], refs[k][0]) for k in range(N_CHIPS)]
        o_ref[...] = ((terms[0] + terms[1]) + terms[2]) + terms[3]

    ts = pl.pallas_call(
        small_body, name="add_chips_small",
        grid_spec=pltpu.PrefetchScalarGridSpec(
            num_scalar_prefetch=1, grid=(1,),
            in_specs=[pl.BlockSpec((Rsh, LANES), lambda i, chip_ref: (0, 0))]
            + [pl.BlockSpec((1, Rsh, LANES), functools.partial(lambda i, chip_ref, k: (written(k, chip_ref), 0, 0), k=k))
               for k in range(N_CHIPS)],
            out_specs=pl.BlockSpec((Rsh, LANES), lambda i, chip_ref: (0, 0))),
        out_shape=jax.ShapeDtypeStruct((Rsh, LANES), F32), compiler_params=_params("arbitrary"),
    )(chip, csa, *([rsb] * N_CHIPS))
    return tb, ts


SHARDED = (("w_in", (D_MODEL, 4616), 1), ("w_branch_sgu", (SGU_W, D_MODEL), 1), ("w_branch_attn", (ATTN_W, D_MODEL), 1),
           ("w_out", (D_MODEL, D_MODEL), 0), ("w_up", (D_MODEL, D_FF), 1), ("w_down", (D_FF, D_MODEL), 0))
SMALL = (("g_mix_pre", (1, D_MODEL)), ("b_forget", (1, N_HEADS)), ("g_sgu", (1, SGU_W)), ("b_sgu", (1, SGU_W)),
         ("w_spatial", (N_GROUPS * CHUNK, CHUNK)), ("b_spatial", (N_GROUPS, CHUNK)), ("g_mix_post", (1, D_MODEL)),
         ("g_ffn_pre", (1, D_MODEL)), ("g_ffn_post", (1, D_MODEL)))
PACK_ALIGN = 256


def _shard_shape(shape, axis):
    return tuple(s // N_CHIPS if a == axis else s for a, s in enumerate(shape))


def _padded_rows(rows):
    return -(-rows // PACK_ALIGN) * PACK_ALIGN


def _pack_rows(parts, axis):
    rows = sum(p.shape[axis] for p in parts)
    pad = _padded_rows(rows) - rows
    if pad:
        shape = list(parts[0].shape)
        shape[axis] = pad
        parts = list(parts) + [jnp.zeros(shape, parts[0].dtype)]
    return jnp.concatenate(parts, axis=axis)


def _pack_shards(shards, dtype):
    return _pack_rows([shards[name].astype(dtype).reshape(-1, LANES) for name, _, _ in SHARDED], 0)


def _unpack_shards(packed):
    out, row = {}, 0
    for name, shape, axis in SHARDED:
        sshape = _shard_shape(shape, axis)
        n = sshape[0] * sshape[1] // LANES
        if packed.ndim == 2:
            out[name] = packed[row:row + n].reshape(sshape)
        else:
            parts = packed[:, row:row + n].reshape((N_CHIPS,) + sshape)
            out[name] = parts.reshape(shape) if axis == 0 else parts.transpose(1, 0, 2).reshape(shape)
        row += n
    return out


def _pack_full_grads(grads):
    parts = []
    for name, shape, axis in SHARDED:
        g = grads[name]
        sshape = _shard_shape(shape, axis)
        if axis == 0:
            g = g.reshape((N_CHIPS,) + sshape)
        else:
            g = g.reshape(shape[0], N_CHIPS, sshape[1]).transpose(1, 0, 2)
        parts.append(g.reshape(N_CHIPS, -1, LANES))
    return _pack_rows(parts, 1)


def _small_rows(shape):
    return -(-(shape[0] * shape[1]) // (8 * LANES)) * 8


def _pack_small(values):
    parts = []
    for name, shape in SMALL:
        flat = values[name].reshape(-1)
        n = _small_rows(shape)
        parts.append(jnp.pad(flat, (0, n * LANES - flat.shape[0])).reshape(n, LANES))
    return _pack_rows(parts, 0)


def _unpack_small(packed):
    out, row = {}, 0
    for name, shape in SMALL:
        n = _small_rows(shape)
        out[name] = packed[row:row + n].reshape(-1)[:shape[0] * shape[1]].reshape(shape)
        row += n
    return out


IN_Z, IN_Q, IN_K, IN_V, IN_F, IN_G, IN_END = 0, 1024, 1536, 2048, 2560, 2568, 4616


def _local_step(x, target, w, small):
    w_in = w["w_in"]
    w_z, w_qkv, w_g = w_in[:, IN_Z:IN_Q], w_in[:, IN_Q:IN_F], w_in[:, IN_G:IN_END]
    w_q, w_k, w_v = w_in[:, IN_Q:IN_K], w_in[:, IN_K:IN_V], w_in[:, IN_V:IN_F]
    w_f = jnp.pad(w_in[:, IN_F:IN_G], ((0, 0), (0, LANES - N_HEADS)))
    b_forget = jnp.pad(small["b_forget"], ((0, 0), (0, LANES - N_HEADS)))
    causal = jnp.tril(jnp.ones((CHUNK, CHUNK), bool))
    ws = jnp.where(causal[None], small["w_spatial"].reshape(N_GROUPS, CHUNK, CHUNK), 0.0).astype(BF)
    ws_t = ws.transpose(0, 2, 1)
    bias_plane = jnp.repeat(small["b_spatial"].T, HEAD_DIM, axis=1)

    xn = _rms_fwd(x, small["g_mix_pre"])
    z = _matmul([(xn, w_z)], nt=False, out_dtypes=[F32], name="proj_z")
    qkv = _matmul([(xn, w_qkv)], nt=False, out_dtypes=[BF], name="proj_qkv")
    gl = _matmul([(xn, w_g)], nt=False, out_dtypes=[F32], name="proj_gate")
    fl = _matmul([(xn, w_f)], nt=False, out_dtypes=[F32], name="proj_forget")
    ysgu = _sgu_fwd(z, small["g_sgu"], small["b_sgu"], ws, bias_plane)
    qf, kl, vl, tile_stats = _attn_prep(qkv, fl, b_forget)
    first_key_tile, last_query_tile, bounded = _attn_ranges(tile_stats)
    yattn, yattn_f, ql = _attn_fwd(qf, kl, vl, first_key_tile, bounded)
    a, b, merged = _branch_merge(ysgu, yattn, w["w_branch_sgu"], w["w_branch_attn"], gl)
    o = _matmul([(merged, w["w_out"])], nt=False, out_dtypes=[F32], name="proj_out")
    h1, xn2 = _mixer_out_fwd(o, x, small["g_mix_post"], small["g_ffn_pre"])

    def relu2(acc):
        r = jnp.maximum(acc, 0.0)
        return r * r, r

    hid, relu = _matmul([(xn2, w["w_up"])], nt=False, out_dtypes=[BF, BF], name="ffn_up", epilogue=relu2)
    dn = _matmul([(hid, w["w_down"])], nt=False, out_dtypes=[F32], name="ffn_down")
    sq, dy, ddn, dg_ffn_post = _loss_head(dn, h1, target, small["g_ffn_post"])

    dup = _matmul([(ddn, w["w_down"])], nt=True, out_dtypes=[BF], name="ffn_down_bwd",
                  epilogue=lambda acc, r: (acc * (2.0 * r.astype(F32)),), extras=[relu])
    dw_down = _matmul_tn(hid, ddn, name="dw_down")
    dxn2 = _matmul([(dup, w["w_up"])], nt=True, out_dtypes=[F32], name="ffn_up_bwd")
    dw_up = _matmul_tn(xn2, dup, name="dw_up")
    dh1, do, dg_ffn_pre, dg_mix_post = _mixer_out_bwd(h1, dxn2, dy, o, small["g_ffn_pre"], small["g_mix_post"])

    dmerged = _matmul([(do, w["w_out"])], nt=True, out_dtypes=[F32], name="proj_out_bwd")
    dw_out = _matmul_tn(merged, do, name="dw_out")
    da, db, dgla, dglb = _gate_bwd(dmerged, a, b, gl)
    dysgu = _matmul([(da, w["w_branch_sgu"])], nt=True, out_dtypes=[F32], name="branch_sgu_bwd")
    dyattn = _matmul([(db, w["w_branch_attn"])], nt=True, out_dtypes=[F32], name="branch_attn_bwd")
    dw_bs = _matmul_tn(ysgu, da, name="dw_branch_sgu")
    dw_ba = _matmul_tn(yattn, db, name="dw_branch_attn")
    dz, dws, dbs, dg_sgu, db_sgu = _sgu_bwd(dysgu, z, small["g_sgu"], small["b_sgu"], ws, ws_t, bias_plane)
    dout = _attn_bwd_prep(dyattn, yattn_f)
    dq, ext_q = _attn_bwd_dq(ql, dout, kl, vl, first_key_tile)
    dk, dv, ext_k = _attn_bwd_dkv(kl, vl, ql, dout, last_query_tile)
    dfl, dbf = _forget_bwd(ext_q, ext_k, fl, b_forget)
    dxn = _matmul([(dz, w_z), (dq, w_q), (dk, w_k), (dv, w_v), (dgla, w_g[:, :D_MODEL]), (dglb, w_g[:, D_MODEL:]), (dfl, w_f)],
                  nt=True, out_dtypes=[F32], name="proj_in_bwd")
    dw_in = jnp.concatenate(
        [_matmul_tn(xn, dz, name="dw_in_z"), _matmul_tn(xn, dq, name="dw_in_q"), _matmul_tn(xn, dk, name="dw_in_k"),
         _matmul_tn(xn, dv, name="dw_in_v"), _matmul_tn(xn, dfl, name="dw_in_f")[:, :N_HEADS],
         _matmul_tn(xn, dgla, name="dw_in_ga"), _matmul_tn(xn, dglb, name="dw_in_gb")], axis=1)
    dx, dg_mix_pre = _input_norm_bwd(x, dxn, dh1, small["g_mix_pre"])

    grads = {"w_in": dw_in, "w_branch_sgu": dw_bs, "w_branch_attn": dw_ba, "w_out": dw_out, "w_up": dw_up, "w_down": dw_down}
    small_grads = {"g_mix_pre": dg_mix_pre, "b_forget": dbf[:, :N_HEADS], "g_sgu": dg_sgu, "b_sgu": db_sgu,
                   "w_spatial": dws.reshape(N_GROUPS * CHUNK, CHUNK), "b_spatial": dbs[:, :N_GROUPS].T,
                   "g_mix_post": dg_mix_post, "g_ffn_pre": dg_ffn_pre, "g_ffn_post": dg_ffn_post}
    return sq, dx, grads, small_grads


NAMES = ("g_mix_pre", "w_in", "b_forget", "g_sgu", "b_sgu", "w_spatial", "b_spatial", "w_branch_sgu", "w_branch_attn",
         "w_out", "g_mix_post", "g_ffn_pre", "w_up", "w_down", "g_ffn_post")


def kernel(x, g_mix_pre, w_in, b_forget, g_sgu, b_sgu, w_spatial, b_spatial, w_branch_sgu, w_branch_attn, w_out, g_mix_post, g_ffn_pre, w_up, w_down, g_ffn_post, loss_target, m_g_mix_pre, m_w_in, m_b_forget, m_g_sgu, m_b_sgu, m_w_spatial, m_b_spatial, m_w_branch_sgu, m_w_branch_attn, m_w_out, m_g_mix_post, m_g_ffn_pre, m_w_up, m_w_down, m_g_ffn_post, v_g_mix_pre, v_w_in, v_b_forget, v_g_sgu, v_b_sgu, v_w_spatial, v_b_spatial, v_w_branch_sgu, v_w_branch_attn, v_w_out, v_g_mix_post, v_g_ffn_pre, v_w_up, v_w_down, v_g_ffn_post):
    weights = dict(zip(NAMES, (g_mix_pre, w_in, b_forget, g_sgu, b_sgu, w_spatial, b_spatial, w_branch_sgu, w_branch_attn,
                               w_out, g_mix_post, g_ffn_pre, w_up, w_down, g_ffn_post), strict=True))
    first = dict(zip(NAMES, (m_g_mix_pre, m_w_in, m_b_forget, m_g_sgu, m_b_sgu, m_w_spatial, m_b_spatial, m_w_branch_sgu,
                             m_w_branch_attn, m_w_out, m_g_mix_post, m_g_ffn_pre, m_w_up, m_w_down, m_g_ffn_post), strict=True))
    second = dict(zip(NAMES, (v_g_mix_pre, v_w_in, v_b_forget, v_g_sgu, v_b_sgu, v_w_spatial, v_b_spatial, v_w_branch_sgu,
                              v_w_branch_attn, v_w_out, v_g_mix_post, v_g_ffn_pre, v_w_up, v_w_down, v_g_ffn_post), strict=True))
    shard_shapes = {name: _shard_shape(shape, axis) for name, shape, axis in SHARDED}
    small_shapes = dict(SMALL)
    view = lambda name, a: a.reshape(shard_shapes.get(name) or small_shapes[name])

    core = lax.axis_index("c").astype(jnp.int32).reshape(1)
    chip = (2 * lax.axis_index("x") + lax.axis_index("y")).astype(jnp.int32).reshape(1)

    shards = {name: view(name, weights[name]) for name, _, _ in SHARDED}
    full = _unpack_shards(_gather_weights(_pack_shards(shards, BF)))
    small = {name: view(name, weights[name]) for name, _ in SMALL}

    sq, dx, grads, small_grads = _local_step(x[0], loss_target[0], full, small)
    loss = lax.psum(0.5 * jnp.sum(sq) / D_MODEL, ("x", "y", "c"))

    gf = _pack_full_grads(grads)
    sf = _pack_small(small_grads)
    rg, rs = _exchange_halves(gf, sf)
    ca, cab, csa = _add_sibling(gf, rg, sf, rs, core)
    rb, rsb = _scatter_to_owners(cab, csa)
    tb, ts = _add_chips(ca, rb, csa, rsb, chip)
    g_packed, s_packed = _join_halves(tb, ts)
    grad = {**_unpack_shards(g_packed), **_unpack_small(s_packed)}

    delta, new_m, new_v = {}, {}, {}
    for name in NAMES:
        delta[name], new_m[name], new_v[name] = _adamw(
            view(name, weights[name]), grad[name], view(name, first[name]), view(name, second[name]), name="adamw_" + name)

    like = lambda d: [d[name].reshape(weights[name].shape) for name in NAMES]
    return (loss, dx[None], *like(grad), *like(delta), *like(new_m), *like(new_v))
```

```python
import functools

import jax
import jax.numpy as jnp
from jax import lax
from jax.experimental import pallas as pl
from jax.experimental.pallas import tpu as pltpu

F32 = jnp.float32
BF = jnp.bfloat16
MESH = pl.DeviceIdType.MESH

D_MODEL = 1024
N_HEADS = 8
HEAD_DIM = 64
ATTN_W = N_HEADS * HEAD_DIM
SGU_W = 512
N_GROUPS = 8
CHUNK = 128
D_FF = 4096
EPS = 1e-6
Q_SCALE = HEAD_DIM ** -0.5
N_CHIPS = 4
LANES = 128

ADAM_LR = 0.001
ADAM_B1 = 0.9
ADAM_B2 = 0.999
ADAM_EPS = 1e-08
ADAM_WD = 0.01
ADAM_STEP = 10

VMEM_LIMIT = 48 * 1024 * 1024
NEG = -1e30

LANE_ROWSUM = HEAD_DIM
LANE_COLSUM = HEAD_DIM + 3


def _params(*sem):
    return pltpu.CompilerParams(dimension_semantics=sem, vmem_limit_bytes=VMEM_LIMIT)


def _dot(a, b):
    return jnp.dot(a, b, preferred_element_type=F32)


def _dot_nt(a, b):
    return lax.dot_general(a, b, (((1,), (1,)), ((), ())), preferred_element_type=F32)


def _dot_tn(a, b):
    return lax.dot_general(a, b, (((0,), (0,)), ((), ())), preferred_element_type=F32)


def _split3(c):
    hi = c.astype(BF).astype(F32)
    r = c - hi
    mid = r.astype(BF).astype(F32)
    lo = (r - mid).astype(BF).astype(F32)
    return hi, mid, lo


def _gelu(x):
    k = 0.7978845608028654
    return 0.5 * x * (1.0 + jnp.tanh(k * (x + 0.044715 * (x * x * x))))


def _gelu_grad(x):
    k = 0.7978845608028654
    x2 = x * x
    t = jnp.tanh(k * (x + 0.044715 * (x2 * x)))
    return 0.5 * (1.0 + t) + 0.5 * x * (1.0 - t * t) * (k * (1.0 + 3.0 * 0.044715 * x2))


def _rms_bwd(a, g, dy):
    r = lax.rsqrt(jnp.mean(a * a, axis=-1, keepdims=True) + EPS)
    n = a * r
    dn = dy * g
    da = r * (dn - n * jnp.mean(dn * n, axis=-1, keepdims=True))
    return da, dy * n


MM_ROWS = 1024
MM_COLS = 512


def _matmul(pairs, *, nt, out_dtypes, name, tm=MM_ROWS, tn=MM_COLS, epilogue=None, extras=()):
    n_pairs = len(pairs)
    n_extra = len(extras)
    M = pairs[0][0].shape[0]
    N = pairs[0][1].shape[0] if nt else pairs[0][1].shape[1]
    tm, tn = min(tm, M), min(tn, N)
    assert M % tm == 0 and N % tn == 0

    def body(*refs):
        acc = None
        for p in range(n_pairs):
            a_ref, b_ref = refs[2 * p], refs[2 * p + 1]
            d = _dot_nt(a_ref[...], b_ref[...]) if nt else _dot(a_ref[...], b_ref[...])
            acc = d if acc is None else acc + d
        e_refs = refs[2 * n_pairs:2 * n_pairs + n_extra]
        o_refs = refs[2 * n_pairs + n_extra:]
        outs = (acc,) if epilogue is None else epilogue(acc, *[e[...] for e in e_refs])
        for o_ref, o in zip(o_refs, outs, strict=True):
            o_ref[...] = o.astype(o_ref.dtype)

    in_specs, args = [], []
    for a, b in pairs:
        K = a.shape[1]
        in_specs.append(pl.BlockSpec((tm, K), lambda i, j: (i, 0)))
        in_specs.append(pl.BlockSpec((tn, K), lambda i, j: (j, 0)) if nt else pl.BlockSpec((K, tn), lambda i, j: (0, j)))
        args += [a, b]
    for e in extras:
        in_specs.append(pl.BlockSpec((tm, tn), lambda i, j: (i, j)))
        args.append(e)
    outs = pl.pallas_call(
        body, name=name, grid=(M // tm, N // tn), in_specs=in_specs,
        out_specs=[pl.BlockSpec((tm, tn), lambda i, j: (i, j)) for _ in out_dtypes],
        out_shape=[jax.ShapeDtypeStruct((M, N), dt) for dt in out_dtypes],
        compiler_params=_params("parallel", "parallel"),
    )(*args)
    return outs if len(outs) > 1 else outs[0]


def _matmul_tn(a, b, *, name, tm=1024, tn=1024, tk=512, slots=False):
    T, K1 = a.shape
    N = b.shape[1]
    tm, tn, tk = min(tm, K1), min(tn, N // N_CHIPS if slots else N), min(tk, T)
    assert K1 % tm == 0 and (N // N_CHIPS if slots else N) % tn == 0 and T % tk == 0
    per_slot = N // N_CHIPS // tn

    def body(a_ref, b_ref, o_ref):
        @pl.when(pl.program_id(2) == 0)
        def _():
            o_ref[...] = jnp.zeros_like(o_ref)

        o_ref[...] += _dot_tn(a_ref[...], b_ref[...])

    if slots:
        out_spec = pl.BlockSpec((None, tm, tn), lambda i, j, k: (j // per_slot, i, j % per_slot))
        out_shape = jax.ShapeDtypeStruct((N_CHIPS, K1, N // N_CHIPS), F32)
    else:
        out_spec = pl.BlockSpec((tm, tn), lambda i, j, k: (i, j))
        out_shape = jax.ShapeDtypeStruct((K1, N), F32)
    return pl.pallas_call(
        body, name=name, grid=(K1 // tm, N // tn, T // tk),
        in_specs=[pl.BlockSpec((tk, tm), lambda i, j, k: (k, i)), pl.BlockSpec((tk, tn), lambda i, j, k: (k, j))],
        out_specs=out_spec, out_shape=out_shape,
        compiler_params=_params("parallel", "parallel", "arbitrary"),
    )(a, b)


def _branch_merge(ysgu, yattn, w_bs, w_ba, gl, *, tm=MM_ROWS, tn=MM_COLS):
    T = ysgu.shape[0]
    tm = min(tm, T)
    nj = D_MODEL // tn

    def body(ys_ref, ya_ref, wbs_ref, wba_ref, gla_ref, glb_ref, a_ref, b_ref, m_ref):
        a = _dot(ys_ref[...], wbs_ref[...])
        b = _dot(ya_ref[...], wba_ref[...])
        a_ref[...] = a
        b_ref[...] = b
        m_ref[...] = (jax.nn.sigmoid(gla_ref[...]) * a + jax.nn.sigmoid(glb_ref[...]) * b).astype(BF)

    return pl.pallas_call(
        body, name="branch_merge", grid=(T // tm, nj),
        in_specs=[
            pl.BlockSpec((tm, SGU_W), lambda i, j: (i, 0)),
            pl.BlockSpec((tm, ATTN_W), lambda i, j: (i, 0)),
            pl.BlockSpec((SGU_W, tn), lambda i, j: (0, j)),
            pl.BlockSpec((ATTN_W, tn), lambda i, j: (0, j)),
            pl.BlockSpec((tm, tn), lambda i, j: (i, j)),
            pl.BlockSpec((tm, tn), lambda i, j: (i, j + nj)),
        ],
        out_specs=[pl.BlockSpec((tm, tn), lambda i, j: (i, j))] * 3,
        out_shape=[jax.ShapeDtypeStruct((T, D_MODEL), F32), jax.ShapeDtypeStruct((T, D_MODEL), F32),
                   jax.ShapeDtypeStruct((T, D_MODEL), BF)],
        compiler_params=_params("parallel", "parallel"),
    )(ysgu, yattn, w_bs, w_ba, gl, gl)


def _row_spec(tr, width):
    return pl.BlockSpec((tr, width), lambda i: (i, 0))


def _vec_spec(width):
    return pl.BlockSpec((1, width), lambda i: (0, 0))


def _rms_fwd(x, g, *, tr=256):
    T = x.shape[0]
    tr = min(tr, T)

    def body(x_ref, g_ref, o_ref):
        xv = x_ref[...]
        r = lax.rsqrt(jnp.mean(xv * xv, axis=-1, keepdims=True) + EPS)
        o_ref[...] = ((xv * r) * g_ref[...]).astype(BF)

    return pl.pallas_call(
        body, name="rms_fwd", grid=(T // tr,),
        in_specs=[_row_spec(tr, D_MODEL), _vec_spec(D_MODEL)], out_specs=_row_spec(tr, D_MODEL),
        out_shape=jax.ShapeDtypeStruct((T, D_MODEL), BF), compiler_params=_params("parallel"),
    )(x, g)


def _mixer_out_fwd(o, x, g_post, g_pre, *, tr=256):
    T = x.shape[0]
    tr = min(tr, T)

    def body(o_ref, x_ref, gpost_ref, gpre_ref, h1_ref, xn2_ref):
        ov = o_ref[...]
        r = lax.rsqrt(jnp.mean(ov * ov, axis=-1, keepdims=True) + EPS)
        h1 = x_ref[...] + (ov * r) * gpost_ref[...]
        h1_ref[...] = h1
        r2 = lax.rsqrt(jnp.mean(h1 * h1, axis=-1, keepdims=True) + EPS)
        xn2_ref[...] = ((h1 * r2) * gpre_ref[...]).astype(BF)

    return pl.pallas_call(
        body, name="mixer_out_fwd", grid=(T // tr,),
        in_specs=[_row_spec(tr, D_MODEL), _row_spec(tr, D_MODEL), _vec_spec(D_MODEL), _vec_spec(D_MODEL)],
        out_specs=[_row_spec(tr, D_MODEL), _row_spec(tr, D_MODEL)],
        out_shape=[jax.ShapeDtypeStruct((T, D_MODEL), F32), jax.ShapeDtypeStruct((T, D_MODEL), BF)],
        compiler_params=_params("parallel"),
    )(o, x, g_post, g_pre)


def _loss_head(dn, h1, target, g_post, *, tr=256):
    T = dn.shape[0]
    tr = min(tr, T)

    def body(dn_ref, h1_ref, t_ref, g_ref, sq_ref, dy_ref, ddn_ref, dg_ref):
        @pl.when(pl.program_id(0) == 0)
        def _():
            sq_ref[...] = jnp.zeros_like(sq_ref)
            dg_ref[...] = jnp.zeros_like(dg_ref)

        a = dn_ref[...]
        g = g_ref[...]
        r = lax.rsqrt(jnp.mean(a * a, axis=-1, keepdims=True) + EPS)
        err = h1_ref[...] + (a * r) * g - t_ref[...]
        sq_ref[...] += jnp.sum(err * err, axis=0, keepdims=True)
        dy = err * (1.0 / D_MODEL)
        dy_ref[...] = dy
        da, dgp = _rms_bwd(a, g, dy)
        ddn_ref[...] = da.astype(BF)
        dg_ref[...] += jnp.sum(dgp, axis=0, keepdims=True)

    return pl.pallas_call(
        body, name="loss_head", grid=(T // tr,),
        in_specs=[_row_spec(tr, D_MODEL)] * 3 + [_vec_spec(D_MODEL)],
        out_specs=[_vec_spec(D_MODEL), _row_spec(tr, D_MODEL), _row_spec(tr, D_MODEL), _vec_spec(D_MODEL)],
        out_shape=[jax.ShapeDtypeStruct((1, D_MODEL), F32), jax.ShapeDtypeStruct((T, D_MODEL), F32),
                   jax.ShapeDtypeStruct((T, D_MODEL), BF), jax.ShapeDtypeStruct((1, D_MODEL), F32)],
        compiler_params=_params("arbitrary"),
    )(dn, h1, target, g_post)


def _mixer_out_bwd(h1, dxn2, dy, o, g_pre, g_post, *, tr=256):
    T = h1.shape[0]
    tr = min(tr, T)

    def body(h1_ref, dxn2_ref, dy_ref, o_ref, gpre_ref, gpost_ref, dh1_ref, do_ref, dgpre_ref, dgpost_ref):
        @pl.when(pl.program_id(0) == 0)
        def _():
            dgpre_ref[...] = jnp.zeros_like(dgpre_ref)
            dgpost_ref[...] = jnp.zeros_like(dgpost_ref)

        da, dgp = _rms_bwd(h1_ref[...], gpre_ref[...], dxn2_ref[...])
        dh1 = dy_ref[...] + da
        dh1_ref[...] = dh1
        dgpre_ref[...] += jnp.sum(dgp, axis=0, keepdims=True)
        do, dgp2 = _rms_bwd(o_ref[...], gpost_ref[...], dh1)
        do_ref[...] = do.astype(BF)
        dgpost_ref[...] += jnp.sum(dgp2, axis=0, keepdims=True)

    return pl.pallas_call(
        body, name="mixer_out_bwd", grid=(T // tr,),
        in_specs=[_row_spec(tr, D_MODEL)] * 4 + [_vec_spec(D_MODEL)] * 2,
        out_specs=[_row_spec(tr, D_MODEL), _row_spec(tr, D_MODEL), _vec_spec(D_MODEL), _vec_spec(D_MODEL)],
        out_shape=[jax.ShapeDtypeStruct((T, D_MODEL), F32), jax.ShapeDtypeStruct((T, D_MODEL), BF),
                   jax.ShapeDtypeStruct((1, D_MODEL), F32), jax.ShapeDtypeStruct((1, D_MODEL), F32)],
        compiler_params=_params("arbitrary"),
    )(h1, dxn2, dy, o, g_pre, g_post)


def _input_norm_bwd(x, dxn, dh1, g, *, tr=256):
    T = x.shape[0]
    tr = min(tr, T)

    def body(x_ref, dxn_ref, dh1_ref, g_ref, dx_ref, dg_ref):
        @pl.when(pl.program_id(0) == 0)
        def _():
            dg_ref[...] = jnp.zeros_like(dg_ref)

        da, dgp = _rms_bwd(x_ref[...], g_ref[...], dxn_ref[...])
        dx_ref[...] = dh1_ref[...] + da
        dg_ref[...] += jnp.sum(dgp, axis=0, keepdims=True)

    return pl.pallas_call(
        body, name="input_norm_bwd", grid=(T // tr,),
        in_specs=[_row_spec(tr, D_MODEL)] * 3 + [_vec_spec(D_MODEL)],
        out_specs=[_row_spec(tr, D_MODEL), _vec_spec(D_MODEL)],
        out_shape=[jax.ShapeDtypeStruct((T, D_MODEL), F32), jax.ShapeDtypeStruct((1, D_MODEL), F32)],
        compiler_params=_params("arbitrary"),
    )(x, dxn, dh1, g)


def _gate_bwd(dm, a, b, gl, *, tr=256):
    T = dm.shape[0]
    tr = min(tr, T)

    def body(dm_ref, a_ref, b_ref, gla_ref, glb_ref, da_ref, db_ref, dgla_ref, dglb_ref):
        dmv = dm_ref[...]
        ga = jax.nn.sigmoid(gla_ref[...])
        gb = jax.nn.sigmoid(glb_ref[...])
        da_ref[...] = (dmv * ga).astype(BF)
        db_ref[...] = (dmv * gb).astype(BF)
        dgla_ref[...] = (dmv * a_ref[...] * (ga * (1.0 - ga))).astype(BF)
        dglb_ref[...] = (dmv * b_ref[...] * (gb * (1.0 - gb))).astype(BF)

    spec = _row_spec(tr, D_MODEL)
    spec_b = pl.BlockSpec((tr, D_MODEL), lambda i: (i, 1))
    da, db, dgla, dglb = pl.pallas_call(
        body, name="gate_bwd", grid=(T // tr,),
        in_specs=[spec, spec, spec, spec, spec_b], out_specs=[spec] * 4,
        out_shape=[jax.ShapeDtypeStruct((T, D_MODEL), BF)] * 4, compiler_params=_params("parallel"),
    )(dm, a, b, gl, gl)
    return da, db, dgla, dglb


def _sgu_norm(z_tile, g, b):
    gz = _gelu(z_tile)
    u, vv = gz[:, :SGU_W], gz[:, SGU_W:]
    xc = vv - jnp.mean(vv, axis=-1, keepdims=True)
    rstd = lax.rsqrt(jnp.mean(xc * xc, axis=-1, keepdims=True) + EPS)
    xhat = xc * rstd
    return u, xhat, rstd, xhat * g + b


def _sgu_mix(w_ref, v_bf, first_half):
    parts = []
    for p in range(N_GROUPS // 2):
        vp = v_bf[:, p * LANES:(p + 1) * LANES]
        parts.append(jnp.where(first_half, _dot(w_ref[2 * p], vp), _dot(w_ref[2 * p + 1], vp)))
    return jnp.concatenate(parts, axis=1)


def _sgu_fwd(z, g_sgu, b_sgu, ws, bias_plane, *, tm=512):
    T = z.shape[0]
    tm = min(tm, T)

    def body(z_ref, g_ref, b_ref, ws_ref, bp_ref, y_ref):
        u, _, _, vn = _sgu_norm(z_ref[...], g_ref[...], b_ref[...])
        vn_bf = vn.astype(BF)
        first_half = lax.broadcasted_iota(jnp.int32, (CHUNK, LANES), 1) < HEAD_DIM
        for c in range(tm // CHUNK):
            rows = slice(c * CHUNK, (c + 1) * CHUNK)
            s = _sgu_mix(ws_ref, vn_bf[rows, :], first_half) + bp_ref[...]
            y_ref[rows, :] = (u[rows, :] * s).astype(BF)

    return pl.pallas_call(
        body, name="sgu_fwd", grid=(T // tm,),
        in_specs=[_row_spec(tm, 2 * SGU_W), _vec_spec(SGU_W), _vec_spec(SGU_W),
                  pl.BlockSpec((N_GROUPS, CHUNK, CHUNK), lambda i: (0, 0, 0)),
                  pl.BlockSpec((CHUNK, SGU_W), lambda i: (0, 0))],
        out_specs=_row_spec(tm, SGU_W), out_shape=jax.ShapeDtypeStruct((T, SGU_W), BF),
        compiler_params=_params("parallel"),
    )(z, g_sgu, b_sgu, ws, bias_plane)


def _sgu_bwd(dy, z, g_sgu, b_sgu, ws, ws_t, bias_plane, *, tm=512):
    T = z.shape[0]
    tm = min(tm, T)
    n_steps = T // tm

    def body(dy_ref, z_ref, g_ref, b_ref, ws_ref, wst_ref, bp_ref, dz_ref, dws_ref, dbs_ref, dg_ref, db_ref, dbp_ref):
        step = pl.program_id(0)

        @pl.when(step == 0)
        def _():
            dws_ref[...] = jnp.zeros_like(dws_ref)
            dg_ref[...] = jnp.zeros_like(dg_ref)
            db_ref[...] = jnp.zeros_like(db_ref)
            dbp_ref[...] = jnp.zeros_like(dbp_ref)

        g = g_ref[...]
        zt = z_ref[...]
        u, xhat, rstd, vn = _sgu_norm(zt, g, b_ref[...])
        vn_bf = vn.astype(BF)
        first_half = lax.broadcasted_iota(jnp.int32, (CHUNK, LANES), 1) < HEAD_DIM
        dyv = dy_ref[...]
        dg_acc = jnp.zeros((1, SGU_W), F32)
        db_acc = jnp.zeros((1, SGU_W), F32)
        for c in range(tm // CHUNK):
            rows = slice(c * CHUNK, (c + 1) * CHUNK)
            v_c = vn_bf[rows, :]
            s = _sgu_mix(ws_ref, v_c, first_half) + bp_ref[...]
            dy_c = dyv[rows, :]
            du = dy_c * s
            dsv = dy_c * u[rows, :]
            dbp_ref[...] += dsv
            ds_bf = dsv.astype(BF)
            zero = jnp.zeros((CHUNK, LANES), BF)
            for p in range(N_GROUPS // 2):
                dsp = ds_bf[:, p * LANES:(p + 1) * LANES]
                vp = v_c[:, p * LANES:(p + 1) * LANES]
                dws_ref[2 * p] += _dot_nt(jnp.where(first_half, dsp, zero), vp)
                dws_ref[2 * p + 1] += _dot_nt(jnp.where(first_half, zero, dsp), vp)
            dvn = _sgu_mix(wst_ref, ds_bf, first_half)
            xh = xhat[rows, :]
            dxh = dvn * g
            dvv = rstd[rows, :] * (dxh - jnp.mean(dxh, axis=-1, keepdims=True)
                                   - xh * jnp.mean(dxh * xh, axis=-1, keepdims=True))
            dg_acc += jnp.sum(dvn * xh, axis=0, keepdims=True)
            db_acc += jnp.sum(dvn, axis=0, keepdims=True)
            dgz = jnp.concatenate([du, dvv], axis=1)
            dz_ref[rows, :] = (dgz * _gelu_grad(zt[rows, :])).astype(BF)
        dg_ref[...] += dg_acc
        db_ref[...] += db_acc

        @pl.when(step == n_steps - 1)
        def _():
            r = lax.broadcasted_iota(jnp.int32, (CHUNK, CHUNK), 0)
            cidx = lax.broadcasted_iota(jnp.int32, (CHUNK, CHUNK), 1)
            causal = (cidx <= r).astype(F32)
            for gi in range(N_GROUPS):
                dws_ref[gi] = dws_ref[gi] * causal
            lane = lax.broadcasted_iota(jnp.int32, (CHUNK, LANES), 1)
            out = jnp.zeros((CHUNK, LANES), F32)
            dbp = dbp_ref[...]
            for gi in range(N_GROUPS):
                col = jnp.sum(dbp[:, gi * HEAD_DIM:(gi + 1) * HEAD_DIM], axis=1, keepdims=True)
                out = jnp.where(lane == gi, col, out)
            dbs_ref[...] = out

    w_spec = pl.BlockSpec((N_GROUPS, CHUNK, CHUNK), lambda i: (0, 0, 0))
    plane = pl.BlockSpec((CHUNK, SGU_W), lambda i: (0, 0))
    return pl.pallas_call(
        body, name="sgu_bwd", grid=(n_steps,),
        in_specs=[_row_spec(tm, SGU_W), _row_spec(tm, 2 * SGU_W), _vec_spec(SGU_W), _vec_spec(SGU_W), w_spec, w_spec, plane],
        out_specs=[_row_spec(tm, 2 * SGU_W), w_spec, pl.BlockSpec((CHUNK, LANES), lambda i: (0, 0)),
                   _vec_spec(SGU_W), _vec_spec(SGU_W)],
        out_shape=[jax.ShapeDtypeStruct((T, 2 * SGU_W), BF), jax.ShapeDtypeStruct((N_GROUPS, CHUNK, CHUNK), F32),
                   jax.ShapeDtypeStruct((CHUNK, LANES), F32), jax.ShapeDtypeStruct((1, SGU_W), F32),
                   jax.ShapeDtypeStruct((1, SGU_W), F32)],
        scratch_shapes=[pltpu.VMEM((CHUNK, SGU_W), F32)],
        compiler_params=_params("arbitrary"),
    )(dy, z, g_sgu, b_sgu, ws, ws_t, bias_plane)


def _tri(n, upper):
    r = lax.broadcasted_iota(jnp.int32, (n, n), 0)
    c = lax.broadcasted_iota(jnp.int32, (n, n), 1)
    return ((c >= r) if upper else (c <= r)).astype(BF)


def _scan_dot(tri, x):
    hi, mid, lo = _split3(x)
    return (_dot(tri, hi.astype(BF)) + _dot(tri, mid.astype(BF))) + _dot(tri, lo.astype(BF))


def _with_lanes(base, lane, start, cols):
    out = base
    for k, col in enumerate(cols):
        if col is not None:
            out = jnp.where(lane == start + k, col, out)
    return out


def _logit_bound(q_norm, k_norm):
    return NORM_SLACK * q_norm * k_norm + 1.0


ATTN_TILE = 512
SKIP_BELOW = -110.0
NORM_SLACK = 1.001
BOUNDED_GAP = 60.0


def _attn_prep(qkv, fl, b_forget, *, tp=ATTN_TILE):
    T = qkv.shape[0]
    tp = min(tp, T)

    def body(qkv_ref, fl_ref, bf_ref, qf_ref, kl_ref, vl_ref, st_ref, carry_ref, kmax_ref):
        @pl.when(pl.program_id(0) == 0)
        def _():
            carry_ref[...] = jnp.zeros_like(carry_ref)
            kmax_ref[...] = jnp.zeros_like(kmax_ref)

        x = fl_ref[...] + bf_ref[...]
        logf = jnp.minimum(x, 0.0) - jnp.log(1.0 + jnp.exp(-jnp.abs(x)))
        cum = _scan_dot(_tri(tp, upper=False), logf) + carry_ref[...]
        carry_ref[...] = cum[tp - 1:tp, :]
        lane = lax.broadcasted_iota(jnp.int32, (tp, HEAD_DIM), 1)
        ones3 = jnp.where(lane < 3, 1.0, 0.0)
        qkvv = qkv_ref[...]
        st_row = lax.broadcasted_iota(jnp.int32, (N_HEADS, LANES), 0)
        st_lane = lax.broadcasted_iota(jnp.int32, (N_HEADS, LANES), 1)
        stats = jnp.zeros((N_HEADS, LANES), F32)
        kmax_lane = lax.broadcasted_iota(jnp.int32, (1, LANES), 1)
        for h in range(N_HEADS):
            ch = cum[:, h:h + 1]
            c3 = _split3(ch)
            qh = qkvv[:, h * HEAD_DIM:(h + 1) * HEAD_DIM].astype(F32) * Q_SCALE
            kh = qkvv[:, ATTN_W + h * HEAD_DIM:ATTN_W + (h + 1) * HEAD_DIM].astype(F32)
            vh = qkvv[:, 2 * ATTN_W + h * HEAD_DIM:2 * ATTN_W + (h + 1) * HEAD_DIM].astype(F32)
            q_norm = jnp.sqrt(jnp.sum(qh * qh, axis=1, keepdims=True))
            qn = jnp.max(q_norm, axis=0, keepdims=True)
            kn = jnp.sqrt(jnp.max(jnp.sum(kh * kh, axis=1, keepdims=True), axis=0, keepdims=True))
            k_seen = jnp.maximum(kmax_ref[:, h:h + 1], kn)
            kmax_ref[...] = jnp.where(kmax_lane == h, k_seen, kmax_ref[...])
            bound3 = _split3(-_logit_bound(q_norm, k_seen))
            ext_q = _with_lanes(jnp.where((lane >= 3) & (lane < 6), 1.0, 0.0), lane, 0, list(c3) + [None] * 3 + list(bound3))
            ext_k = _with_lanes(jnp.where((lane < 3) | ((lane >= 6) & (lane < 9)), 1.0, 0.0), lane, 3, [-c for c in c3])
            qf_ref[h] = jnp.concatenate([qh, ext_q], axis=1).astype(BF)
            kl_ref[h] = jnp.concatenate([kh, ext_k], axis=1).astype(BF)
            vl_ref[h] = jnp.concatenate([vh, ones3], axis=1).astype(BF)
            tile_stats = (qn, kn, jnp.max(ch, axis=0, keepdims=True), jnp.min(ch, axis=0, keepdims=True), k_seen)
            for k, val in enumerate(tile_stats):
                stats = jnp.where((st_row == h) & (st_lane == k), val, stats)
        st_ref[0] = stats

    head_spec = pl.BlockSpec((N_HEADS, tp, LANES), lambda i: (0, i, 0))
    return pl.pallas_call(
        body, name="attn_prep", grid=(T // tp,),
        in_specs=[_row_spec(tp, 3 * ATTN_W), _row_spec(tp, LANES), _vec_spec(LANES)],
        out_specs=[head_spec] * 3 + [pl.BlockSpec((1, N_HEADS, LANES), lambda i: (i, 0, 0))],
        out_shape=[jax.ShapeDtypeStruct((N_HEADS, T, LANES), BF)] * 3 + [jax.ShapeDtypeStruct((T // tp, N_HEADS, LANES), F32)],
        scratch_shapes=[pltpu.VMEM((1, LANES), F32), pltpu.VMEM((1, LANES), F32)], compiler_params=_params("arbitrary"),
    )(qkv, fl, b_forget)


def _attn_ranges(stats):
    qn, kn, cmax, cmin, k_seen = (stats[:, :, k].T for k in range(5))
    n = qn.shape[1]
    bounded = (2.0 * _logit_bound(qn, k_seen) <= BOUNDED_GAP).reshape(N_HEADS // 2, 2, n).all(axis=1)
    reach = NORM_SLACK * qn * (jnp.max(kn, axis=1, keepdims=True) + kn) + cmax
    i = jnp.arange(n)[None, :, None]
    j = jnp.arange(n)[None, None, :]
    need = ((reach[:, :, None] - cmin[:, None, :] >= SKIP_BELOW) | (i == j)) & (j <= i)
    first = jnp.min(jnp.where(need, j, n), axis=2).reshape(N_HEADS // 2, 2, n).min(axis=1)
    last = jnp.max(jnp.where(need, i, -1), axis=1).reshape(N_HEADS // 2, 2, n).max(axis=1)
    return first.reshape(-1).astype(F32), last.reshape(-1).astype(F32), bounded.reshape(-1).astype(F32)


def _pair_block(t):
    return pl.BlockSpec((2, t, LANES), lambda p, i, *_: (p, i, 0))


def _pair_full(T):
    return pl.BlockSpec((2, T, LANES), lambda p, i, *_: (p, 0, 0))


def _packed_block(t):
    return pl.BlockSpec((t, LANES), lambda p, i, *_: (i, p))


def _causal(t, keys_in_rows=False):
    r = lax.broadcasted_iota(jnp.int32, (t, t), 0)
    c = lax.broadcasted_iota(jnp.int32, (t, t), 1)
    return (r <= c) if keys_in_rows else (c <= r)


def _tile_rows(j, t):
    return pl.ds(pl.multiple_of(j * t, t), t)


def _attn_call(body, name, tile_scalars, operands, in_specs, out_specs, out_shape, scratch_shapes, n_tiles):
    return pl.pallas_call(
        body, name=name,
        grid_spec=pltpu.PrefetchScalarGridSpec(
            num_scalar_prefetch=len(tile_scalars), grid=(N_HEADS // 2, n_tiles), in_specs=in_specs, out_specs=out_specs,
            scratch_shapes=scratch_shapes),
        out_shape=out_shape, compiler_params=_params("parallel", "arbitrary"),
    )(*tile_scalars, *operands)


def _attn_fwd(qf, kl, vl, first, bounded, *, tq=ATTN_TILE):
    T = qf.shape[1]
    tq = min(tq, T)
    n = T // tq

    def body(first_ref, bounded_ref, qf_ref, kl_ref, vl_ref, o_ref, of_ref, ql_ref, m_ref, acc_ref):
        i = pl.program_id(1)
        tile = pl.program_id(0) * n + i
        start = first_ref[tile].astype(jnp.int32)
        is_bounded = bounded_ref[tile] > 0.5
        acc_ref[...] = jnp.zeros_like(acc_ref)
        diagonal = _tile_rows(i, tq)
        causal = _causal(tq)

        def logits(hh, rows):
            return _dot_nt(qf_ref[hh], kl_ref[hh, rows, :])

        @pl.when(is_bounded)
        def _():
            m_ref[...] = jnp.zeros_like(m_ref)

            def update(hh, s, rows):
                acc_ref[hh] += _dot(jnp.exp(s).astype(BF), vl_ref[hh, rows, :])

            def step(j, carry):
                for hh in range(2):
                    update(hh, logits(hh, _tile_rows(j, tq)), _tile_rows(j, tq))
                return carry

            lax.fori_loop(start, i, step, 0)
            for hh in range(2):
                update(hh, jnp.where(causal, logits(hh, diagonal), NEG), diagonal)

        @pl.when(jnp.logical_not(is_bounded))
        def _():
            m_ref[...] = jnp.full_like(m_ref, NEG)

            def update(hh, s, rows):
                m_old = m_ref[hh]
                m_new = jnp.maximum(m_old, jnp.max(s, axis=1, keepdims=True))
                p = jnp.exp(s - m_new)
                acc_ref[hh] = jnp.exp(m_old - m_new) * acc_ref[hh] + _dot(p.astype(BF), vl_ref[hh, rows, :])
                m_ref[hh] = m_new

            def step(j, carry):
                for hh in range(2):
                    update(hh, logits(hh, _tile_rows(j, tq)), _tile_rows(j, tq))
                return carry

            lax.fori_loop(start, i, step, 0)
            for hh in range(2):
                update(hh, jnp.where(causal, logits(hh, diagonal), NEG), diagonal)

        lane = lax.broadcasted_iota(jnp.int32, (tq, LANES), 1)
        outs = []
        for hh in range(2):
            q = qf_ref[hh].astype(F32)
            acc = acc_ref[hh]
            l = acc[:, HEAD_DIM:HEAD_DIM + 1]
            outs.append(acc[:, :HEAD_DIM] / l)
            at = HEAD_DIM + 6
            neg_bound = (q[:, at:at + 1] + q[:, at + 1:at + 2]) + q[:, at + 2:at + 3]
            ql_ref[hh] = _with_lanes(q, lane, at, _split3(neg_bound - (m_ref[hh] + jnp.log(l)))).astype(BF)
        o = jnp.concatenate(outs, axis=1)
        o_ref[...] = o.astype(BF)
        of_ref[...] = o

    return _attn_call(
        body, "attn_fwd", (first, bounded), (qf, kl, vl), [_pair_block(tq), _pair_full(T), _pair_full(T)],
        [_packed_block(tq), _packed_block(tq), _pair_block(tq)],
        [jax.ShapeDtypeStruct((T, ATTN_W), BF), jax.ShapeDtypeStruct((T, ATTN_W), F32),
         jax.ShapeDtypeStruct((N_HEADS, T, LANES), BF)],
        [pltpu.VMEM((2, tq, 1), F32), pltpu.VMEM((2, tq, LANES), F32)], n)


def _attn_bwd_prep(dya, of, *, tr=256):
    T = dya.shape[0]
    tr = min(tr, T)

    def body(d_ref, o_ref, do_ref):
        lane = lax.broadcasted_iota(jnp.int32, (tr, HEAD_DIM), 1)
        dv, ov = d_ref[...], o_ref[...]
        for h in range(N_HEADS):
            d = dv[:, h * HEAD_DIM:(h + 1) * HEAD_DIM]
            delta = jnp.sum(d * ov[:, h * HEAD_DIM:(h + 1) * HEAD_DIM], axis=1, keepdims=True)
            ext = _with_lanes(jnp.zeros((tr, HEAD_DIM), F32), lane, 0, _split3(-delta))
            do_ref[h] = jnp.concatenate([d, ext], axis=1).astype(BF)

    return pl.pallas_call(
        body, name="attn_bwd_prep", grid=(T // tr,),
        in_specs=[_row_spec(tr, ATTN_W), _row_spec(tr, ATTN_W)],
        out_specs=pl.BlockSpec((N_HEADS, tr, LANES), lambda i: (0, i, 0)),
        out_shape=jax.ShapeDtypeStruct((N_HEADS, T, LANES), BF), compiler_params=_params("parallel"),
    )(dya, of)


def _attn_bwd_dq(ql, do, kl, vl, first, *, tq=ATTN_TILE):
    T = ql.shape[1]
    tq = min(tq, T)
    n = T // tq

    def body(first_ref, ql_ref, do_ref, kl_ref, vl_ref, dq_ref, ext_ref, acc_ref):
        i = pl.program_id(1)
        acc_ref[...] = jnp.zeros_like(acc_ref)

        def block(hh, rows, mask):
            kj = kl_ref[hh, rows, :]
            p = jnp.exp(_dot_nt(ql_ref[hh], kj))
            if mask is not None:
                p = jnp.where(mask, p, 0.0)
            ds = p * _dot_nt(do_ref[hh], vl_ref[hh, rows, :])
            acc_ref[hh] += _dot(ds.astype(BF), kj)

        def step(j, carry):
            for hh in range(2):
                block(hh, _tile_rows(j, tq), None)
            return carry

        lax.fori_loop(first_ref[pl.program_id(0) * n + i].astype(jnp.int32), i, step, 0)
        causal = _causal(tq)
        for hh in range(2):
            block(hh, _tile_rows(i, tq), causal)
        dq_ref[...] = jnp.concatenate([acc_ref[hh][:, :HEAD_DIM] * Q_SCALE for hh in range(2)], axis=1).astype(BF)
        ext_ref[...] = jnp.concatenate([acc_ref[hh][:, HEAD_DIM:] for hh in range(2)], axis=1)

    return _attn_call(
        body, "attn_bwd_dq", (first,), (ql, do, kl, vl),
        [_pair_block(tq), _pair_block(tq), _pair_full(T), _pair_full(T)], [_packed_block(tq), _packed_block(tq)],
        [jax.ShapeDtypeStruct((T, ATTN_W), BF), jax.ShapeDtypeStruct((T, ATTN_W), F32)],
        [pltpu.VMEM((2, tq, LANES), F32)], n)


def _attn_bwd_dkv(kl, vl, ql, do, last, *, tk=ATTN_TILE):
    T = ql.shape[1]
    tk = min(tk, T)
    n = T // tk

    def body(last_ref, kl_ref, vl_ref, ql_ref, do_ref, dk_ref, dv_ref, ext_ref, dk_acc, dv_acc):
        j = pl.program_id(1)
        dk_acc[...] = jnp.zeros_like(dk_acc)
        dv_acc[...] = jnp.zeros_like(dv_acc)

        def block(hh, rows, mask):
            qi, di = ql_ref[hh, rows, :], do_ref[hh, rows, :]
            p_t = jnp.exp(_dot_nt(kl_ref[hh], qi))
            if mask is not None:
                p_t = jnp.where(mask, p_t, 0.0)
            ds_t = p_t * _dot_nt(vl_ref[hh], di)
            dk_acc[hh] += _dot(ds_t.astype(BF), qi)
            dv_acc[hh] += _dot(p_t.astype(BF), di)

        causal_t = _causal(tk, keys_in_rows=True)
        for hh in range(2):
            block(hh, _tile_rows(j, tk), causal_t)

        def step(i, carry):
            for hh in range(2):
                block(hh, _tile_rows(i, tk), None)
            return carry

        lax.fori_loop(j + 1, last_ref[pl.program_id(0) * n + j].astype(jnp.int32) + 1, step, 0)
        dk_ref[...] = jnp.concatenate([dk_acc[hh][:, :HEAD_DIM] for hh in range(2)], axis=1).astype(BF)
        dv_ref[...] = jnp.concatenate([dv_acc[hh][:, :HEAD_DIM] for hh in range(2)], axis=1).astype(BF)
        ext_ref[...] = jnp.concatenate([dk_acc[hh][:, HEAD_DIM:] for hh in range(2)], axis=1)

    return _attn_call(
        body, "attn_bwd_dkv", (last,), (kl, vl, ql, do),
        [_pair_block(tk), _pair_block(tk), _pair_full(T), _pair_full(T)], [_packed_block(tk)] * 3,
        [jax.ShapeDtypeStruct((T, ATTN_W), BF), jax.ShapeDtypeStruct((T, ATTN_W), BF),
         jax.ShapeDtypeStruct((T, ATTN_W), F32)],
        [pltpu.VMEM((2, tk, LANES), F32), pltpu.VMEM((2, tk, LANES), F32)], n)


def _forget_bwd(ext_q, ext_k, fl, b_forget, *, tp=256):
    T = fl.shape[0]
    tp = min(tp, T)
    n = T // tp

    def body(eq_ref, ek_ref, fl_ref, bf_ref, dfl_ref, dbf_ref, carry_ref):
        @pl.when(pl.program_id(0) == 0)
        def _():
            carry_ref[...] = jnp.zeros_like(carry_ref)
            dbf_ref[...] = jnp.zeros_like(dbf_ref)

        lane = lax.broadcasted_iota(jnp.int32, (tp, LANES), 1)
        eq, ek = eq_ref[...], ek_ref[...]
        cols = [eq[:, h * HEAD_DIM:h * HEAD_DIM + 1] - ek[:, h * HEAD_DIM + 3:h * HEAD_DIM + 4] for h in range(N_HEADS)]
        dcum = _with_lanes(jnp.zeros((tp, LANES), F32), lane, 0, cols)
        suffix = _scan_dot(_tri(tp, upper=True), dcum) + carry_ref[...]
        carry_ref[...] = suffix[0:1, :]
        x = fl_ref[...] + bf_ref[...]
        dfl = jnp.where(lane < N_HEADS, suffix / (1.0 + jnp.exp(x)), 0.0)
        dfl_ref[...] = dfl.astype(BF)
        dbf_ref[...] += jnp.sum(dfl, axis=0, keepdims=True)

    rev = lambda w: pl.BlockSpec((tp, w), lambda i: (n - 1 - i, 0))
    return pl.pallas_call(
        body, name="forget_bwd", grid=(n,),
        in_specs=[rev(ATTN_W), rev(ATTN_W), rev(LANES), _vec_spec(LANES)],
        out_specs=[rev(LANES), _vec_spec(LANES)],
        out_shape=[jax.ShapeDtypeStruct((T, LANES), BF), jax.ShapeDtypeStruct((1, LANES), F32)],
        scratch_shapes=[pltpu.VMEM((1, LANES), F32)], compiler_params=_params("arbitrary"),
    )(ext_q, ext_k, fl, b_forget)


def _adamw(w, g, m, v, *, name, tr=256):
    rows, cols = w.shape
    tr = tr if rows % tr == 0 else rows

    def body(w_ref, g_ref, m_ref, v_ref, d_ref, nm_ref, nv_ref):
        gv = g_ref[...]
        nm = ADAM_B1 * m_ref[...] + (1.0 - ADAM_B1) * gv
        nv = ADAM_B2 * v_ref[...] + (1.0 - ADAM_B2) * (gv * gv)
        m_hat = nm / (1.0 - ADAM_B1 ** ADAM_STEP)
        v_hat = nv / (1.0 - ADAM_B2 ** ADAM_STEP)
        d_ref[...] = -ADAM_LR * (m_hat / (jnp.sqrt(v_hat) + ADAM_EPS) + ADAM_WD * w_ref[...])
        nm_ref[...] = nm
        nv_ref[...] = nv

    spec = pl.BlockSpec((tr, cols), lambda i: (i, 0))
    return pl.pallas_call(
        body, name=name, grid=(rows // tr,), in_specs=[spec] * 4, out_specs=[spec] * 3,
        out_shape=[jax.ShapeDtypeStruct((rows, cols), F32)] * 3, compiler_params=_params("parallel"),
    )(w, g, m, v)


HBM = pl.BlockSpec(memory_space=pltpu.HBM)
BF16_ROWS = 16


def _place():
    x, y, c = lax.axis_index("x"), lax.axis_index("y"), lax.axis_index("c")
    others = [(1 - x, y), (x, 1 - y), (1 - x, 1 - y)]
    return x, y, c, others


def _chip(xy):
    return 2 * xy[0] + xy[1]


def _row_halves(c, rows):
    half = rows // 2
    assert half % BF16_ROWS == 0
    return (pl.ds(pl.multiple_of(c * half, BF16_ROWS), half), pl.ds(pl.multiple_of((1 - c) * half, BF16_ROWS), half))


def _remote(src, dst, send_sems, recv_sems, k, to):
    return pltpu.make_async_remote_copy(src_ref=src, dst_ref=dst, send_sem=send_sems.at[k], recv_sem=recv_sems.at[k],
                                        device_id=to, device_id_type=MESH)


def _gather_weights(shards):
    n = len(shards)

    def body(*refs):
        w_refs, g_refs, (send_sems, recv_sems) = refs[:n], refs[n:2 * n], refs[2 * n:]
        x, y, c, others = _place()
        sibling, me = (x, y, 1 - c), _chip((x, y))
        first = []
        for a in range(n):
            mine, _ = _row_halves(c, w_refs[a].shape[0])
            for j, o in enumerate(others):
                first.append(_remote(w_refs[a].at[mine, :], g_refs[a].at[me, mine, :], send_sems, recv_sems, 6 * a + j, (*o, c)))
        for cp in first:
            cp.start()
        passed = []
        for j, o in enumerate(others):
            for a in range(n):
                mine, _ = _row_halves(c, w_refs[a].shape[0])
                landed = g_refs[a].at[_chip(o), mine, :]
                _remote(landed, landed, send_sems, recv_sems, 6 * a + j, (*o, c)).wait_recv()
                passed.append(_remote(landed, landed, send_sems, recv_sems, 6 * a + 3 + j, sibling))
                passed[-1].start()
        for j, o in enumerate(others):
            for a in range(n):
                _, theirs = _row_halves(c, w_refs[a].shape[0])
                landed = g_refs[a].at[_chip(o), theirs, :]
                _remote(landed, landed, send_sems, recv_sems, 6 * a + 3 + j, sibling).wait_recv()
        for cp in first + passed:
            cp.wait_send()

    return pl.pallas_call(
        body, name="gather_weights", in_specs=[HBM] * n, out_specs=[HBM] * n,
        out_shape=[jax.ShapeDtypeStruct((N_CHIPS,) + s.shape, s.dtype) for s in shards],
        scratch_shapes=[pltpu.SemaphoreType.DMA((6 * n,)), pltpu.SemaphoreType.DMA((6 * n,))],
    )(*shards)


def _exchange_halves(grads, small):
    n = len(grads)

    def body(*refs):
        g_refs, r_refs, (send_sems, recv_sems) = refs[:n + 1], refs[n + 1:2 * n + 2], refs[2 * n + 2:]
        x, y, c, _ = _place()
        copies = []
        for a in range(n + 1):
            _, theirs = _row_halves(c, g_refs[a].shape[-2])
            src = g_refs[a].at[:, theirs, :] if a < n else g_refs[a].at[theirs, :]
            copies.append(_remote(src, r_refs[a], send_sems, recv_sems, a, (x, y, 1 - c)))
            copies[-1].start()
        for cp in copies:
            cp.wait()

    def half(s):
        return jax.ShapeDtypeStruct(s.shape[:-2] + (s.shape[-2] // 2, s.shape[-1]), F32)

    outs = pl.pallas_call(
        body, name="exchange_halves", in_specs=[HBM] * (n + 1), out_specs=[HBM] * (n + 1),
        out_shape=[half(g) for g in grads] + [half(small)],
        scratch_shapes=[pltpu.SemaphoreType.DMA((n + 1,)), pltpu.SemaphoreType.DMA((n + 1,))],
    )(*grads, small)
    return outs[:n], outs[n]


def _scatter_to_owners(chip_sums, small_sum):
    n = len(chip_sums)

    def body(*refs):
        b_refs, r_refs, (send_sems, recv_sems) = refs[:n + 1], refs[n + 1:2 * n + 2], refs[2 * n + 2:]
        x, y, c, others = _place()
        me = _chip((x, y))
        sends = []
        for a in range(n + 1):
            for j, o in enumerate(others):
                src = b_refs[a].at[_chip(o)] if a < n else b_refs[a]
                sends.append(_remote(src, r_refs[a].at[me], send_sems, recv_sems, 3 * a + j, (*o, c)))
                sends[-1].start()
        for a in range(n + 1):
            for j, o in enumerate(others):
                landed = r_refs[a].at[_chip(o)]
                _remote(landed, landed, send_sems, recv_sems, 3 * a + j, (*o, c)).wait_recv()
        for cp in sends:
            cp.wait_send()

    outs = pl.pallas_call(
        body, name="scatter_to_owners", in_specs=[HBM] * (n + 1), out_specs=[HBM] * (n + 1),
        out_shape=[jax.ShapeDtypeStruct(b.shape, BF) for b in chip_sums]
        + [jax.ShapeDtypeStruct((N_CHIPS,) + small_sum.shape, F32)],
        scratch_shapes=[pltpu.SemaphoreType.DMA((3 * (n + 1),)), pltpu.SemaphoreType.DMA((3 * (n + 1),))],
    )(*chip_sums, small_sum)
    return outs[:n], outs[n]


def _join_halves(totals):
    n = len(totals)

    def body(*refs):
        in_refs, out_refs, (send_sems, recv_sems) = refs[:n], refs[n:2 * n], refs[2 * n:]
        x, y, c, _ = _place()
        copies = []
        for a in range(n):
            mine, _ = _row_halves(c, in_refs[a].shape[0])
            copies.append(_remote(in_refs[a].at[mine, :], out_refs[a].at[mine, :], send_sems, recv_sems, a, (x, y, 1 - c)))
            copies[-1].start()
        for cp in copies:
            cp.wait()

    return pl.pallas_call(
        body, name="join_halves", in_specs=[HBM] * n, out_specs=[HBM] * n,
        out_shape=[jax.ShapeDtypeStruct(t.shape, F32) for t in totals], input_output_aliases={a: a for a in range(n)},
        scratch_shapes=[pltpu.SemaphoreType.DMA((n,)), pltpu.SemaphoreType.DMA((n,))],
    )(*totals)


ADD_ROWS = 128


def _add_sibling(g, r, place, *, name):
    lead, (half, cols) = g.shape[:-2], r.shape[-2:]
    tr = min(ADD_ROWS, half)
    nb = half // tr
    zeros = (0,) * len(lead)

    def body(place_ref, g_ref, r_ref, o_ref, ob_ref):
        s = g_ref[...] + r_ref[...]
        o_ref[...] = s
        ob_ref[...] = s.astype(BF)

    spec = pl.BlockSpec(lead + (tr, cols), lambda i, p: zeros + (i, 0))
    return pl.pallas_call(
        body, name=name,
        grid_spec=pltpu.PrefetchScalarGridSpec(
            num_scalar_prefetch=1, grid=(nb,),
            in_specs=[pl.BlockSpec(lead + (tr, cols), lambda i, p: zeros + (p[1] * nb + i, 0)), spec], out_specs=[spec, spec]),
        out_shape=[jax.ShapeDtypeStruct(r.shape, F32), jax.ShapeDtypeStruct(r.shape, BF)],
        compiler_params=_params("parallel"),
    )(place, g, r)


def _add_chips(own, received, place, *, name, own_slots):
    half, cols = received.shape[-2:]
    tr = min(ADD_ROWS, half)
    nb = half // tr

    def written(k, p):
        return jnp.where(p[0] == k, (k + 1) % N_CHIPS, k)

    def body(place_ref, own_ref, *refs):
        o_ref = refs[N_CHIPS]
        mine = own_ref[0] if own_slots else own_ref[...]
        if own_slots:
            acc = mine
            for k in range(N_CHIPS):
                acc = acc + jnp.where(place_ref[0] == k, 0.0, refs[k][0].astype(F32))
        else:
            terms = [jnp.where(place_ref[0] == k, mine, refs[k][0]) for k in range(N_CHIPS)]
            acc = ((terms[0] + terms[1]) + terms[2]) + terms[3]
        o_ref[...] = acc

    own_spec = (pl.BlockSpec((1, tr, cols), lambda i, p: (p[0], i, 0)) if own_slots
                else pl.BlockSpec((tr, cols), lambda i, p: (i, 0)))
    return pl.pallas_call(
        body, name=name,
        grid_spec=pltpu.PrefetchScalarGridSpec(
            num_scalar_prefetch=1, grid=(nb,),
            in_specs=[own_spec] + [pl.BlockSpec((1, tr, cols), functools.partial(lambda i, p, k: (written(k, p), i, 0), k=k))
                                   for k in range(N_CHIPS)],
            out_specs=pl.BlockSpec((tr, cols), lambda i, p: (p[1] * nb + i, 0))),
        out_shape=jax.ShapeDtypeStruct((2 * half, cols), F32), compiler_params=_params("parallel"),
    )(place, own, *([received] * N_CHIPS))


SHARDED = (("w_in", (D_MODEL, 4616), 1), ("w_branch_sgu", (SGU_W, D_MODEL), 1), ("w_branch_attn", (ATTN_W, D_MODEL), 1),
           ("w_out", (D_MODEL, D_MODEL), 0), ("w_up", (D_MODEL, D_FF), 1), ("w_down", (D_FF, D_MODEL), 0))
SMALL = (("g_mix_pre", (1, D_MODEL)), ("b_forget", (1, N_HEADS)), ("g_sgu", (1, SGU_W)), ("b_sgu", (1, SGU_W)),
         ("w_spatial", (N_GROUPS * CHUNK, CHUNK)), ("b_spatial", (N_GROUPS, CHUNK)), ("g_mix_post", (1, D_MODEL)),
         ("g_ffn_pre", (1, D_MODEL)), ("g_ffn_post", (1, D_MODEL)))
SMALL_ALIGN = 2 * ADD_ROWS


def _shard_shape(shape, axis):
    return tuple(s // N_CHIPS if a == axis else s for a, s in enumerate(shape))


def _slots_to_full(slots, axis):
    return slots.reshape(-1, slots.shape[2]) if axis == 0 else slots.transpose(1, 0, 2).reshape(slots.shape[1], -1)


def _full_to_slots(full, axis):
    if axis == 0:
        return full.reshape(N_CHIPS, -1, full.shape[1])
    return full.reshape(full.shape[0], N_CHIPS, -1).transpose(1, 0, 2)


def _small_rows(shape):
    return -(-(shape[0] * shape[1]) // (8 * LANES)) * 8


def _pack_small(values):
    parts = []
    for name, shape in SMALL:
        flat = values[name].reshape(-1)
        n = _small_rows(shape)
        parts.append(jnp.pad(flat, (0, n * LANES - flat.shape[0])).reshape(n, LANES))
    rows = sum(p.shape[0] for p in parts)
    pad = -(-rows // SMALL_ALIGN) * SMALL_ALIGN - rows
    return jnp.concatenate(parts + [jnp.zeros((pad, LANES), F32)], axis=0)


def _unpack_small(packed):
    out, row = {}, 0
    for name, shape in SMALL:
        n = _small_rows(shape)
        out[name] = packed[row:row + n].reshape(-1)[:shape[0] * shape[1]].reshape(shape)
        row += n
    return out


IN_Z, IN_Q, IN_K, IN_V, IN_F, IN_G, IN_END = 0, 1024, 1536, 2048, 2560, 2568, 4616


def _local_step(x, target, w, small):
    w_in = w["w_in"]
    w_z, w_qkv, w_g = w_in[:, IN_Z:IN_Q], w_in[:, IN_Q:IN_F], w_in[:, IN_G:IN_END]
    w_q, w_k, w_v = w_in[:, IN_Q:IN_K], w_in[:, IN_K:IN_V], w_in[:, IN_V:IN_F]
    w_f = jnp.pad(w_in[:, IN_F:IN_G], ((0, 0), (0, LANES - N_HEADS)))
    b_forget = jnp.pad(small["b_forget"], ((0, 0), (0, LANES - N_HEADS)))
    causal = jnp.tril(jnp.ones((CHUNK, CHUNK), bool))
    ws = jnp.where(causal[None], small["w_spatial"].reshape(N_GROUPS, CHUNK, CHUNK), 0.0).astype(BF)
    ws_t = ws.transpose(0, 2, 1)
    bias_plane = jnp.repeat(small["b_spatial"].T, HEAD_DIM, axis=1)

    xn = _rms_fwd(x, small["g_mix_pre"])
    z = _matmul([(xn, w_z)], nt=False, out_dtypes=[F32], name="proj_z")
    qkv = _matmul([(xn, w_qkv)], nt=False, out_dtypes=[BF], name="proj_qkv")
    gl = _matmul([(xn, w_g)], nt=False, out_dtypes=[F32], name="proj_gate")
    fl = _matmul([(xn, w_f)], nt=False, out_dtypes=[F32], name="proj_forget")
    ysgu = _sgu_fwd(z, small["g_sgu"], small["b_sgu"], ws, bias_plane)
    qf, kl, vl, tile_stats = _attn_prep(qkv, fl, b_forget)
    first_key_tile, last_query_tile, bounded = _attn_ranges(tile_stats)
    yattn, yattn_f, ql = _attn_fwd(qf, kl, vl, first_key_tile, bounded)
    a, b, merged = _branch_merge(ysgu, yattn, w["w_branch_sgu"], w["w_branch_attn"], gl)
    o = _matmul([(merged, w["w_out"])], nt=False, out_dtypes=[F32], name="proj_out")
    h1, xn2 = _mixer_out_fwd(o, x, small["g_mix_post"], small["g_ffn_pre"])

    def relu2(acc):
        r = jnp.maximum(acc, 0.0)
        return r * r, r

    hid, relu = _matmul([(xn2, w["w_up"])], nt=False, out_dtypes=[BF, BF], name="ffn_up", epilogue=relu2)
    dn = _matmul([(hid, w["w_down"])], nt=False, out_dtypes=[F32], name="ffn_down")
    sq, dy, ddn, dg_ffn_post = _loss_head(dn, h1, target, small["g_ffn_post"])

    dup = _matmul([(ddn, w["w_down"])], nt=True, out_dtypes=[BF], name="ffn_down_bwd",
                  epilogue=lambda acc, r: (acc * (2.0 * r.astype(F32)),), extras=[relu])
    dw_down = _matmul_tn(hid, ddn, name="dw_down")
    dxn2 = _matmul([(dup, w["w_up"])], nt=True, out_dtypes=[F32], name="ffn_up_bwd")
    dw_up = _matmul_tn(xn2, dup, name="dw_up", slots=True)
    dh1, do, dg_ffn_pre, dg_mix_post = _mixer_out_bwd(h1, dxn2, dy, o, small["g_ffn_pre"], small["g_mix_post"])

    dmerged = _matmul([(do, w["w_out"])], nt=True, out_dtypes=[F32], name="proj_out_bwd")
    dw_out = _matmul_tn(merged, do, name="dw_out")
    da, db, dgla, dglb = _gate_bwd(dmerged, a, b, gl)
    dysgu = _matmul([(da, w["w_branch_sgu"])], nt=True, out_dtypes=[F32], name="branch_sgu_bwd")
    dyattn = _matmul([(db, w["w_branch_attn"])], nt=True, out_dtypes=[F32], name="branch_attn_bwd")
    dw_bs = _matmul_tn(ysgu, da, name="dw_branch_sgu", slots=True)
    dw_ba = _matmul_tn(yattn, db, name="dw_branch_attn", slots=True)
    dz, dws, dbs, dg_sgu, db_sgu = _sgu_bwd(dysgu, z, small["g_sgu"], small["b_sgu"], ws, ws_t, bias_plane)
    dout = _attn_bwd_prep(dyattn, yattn_f)
    dq, ext_q = _attn_bwd_dq(ql, dout, kl, vl, first_key_tile)
    dk, dv, ext_k = _attn_bwd_dkv(kl, vl, ql, dout, last_query_tile)
    dfl, dbf = _forget_bwd(ext_q, ext_k, fl, b_forget)
    dxn = _matmul([(dz, w_z), (dq, w_q), (dk, w_k), (dv, w_v), (dgla, w_g[:, :D_MODEL]), (dglb, w_g[:, D_MODEL:]), (dfl, w_f)],
                  nt=True, out_dtypes=[F32], name="proj_in_bwd")
    dw_in = jnp.concatenate(
        [_matmul_tn(xn, dz, name="dw_in_z"), _matmul_tn(xn, dq, name="dw_in_q"), _matmul_tn(xn, dk, name="dw_in_k"),
         _matmul_tn(xn, dv, name="dw_in_v"), _matmul_tn(xn, dfl, name="dw_in_f")[:, :N_HEADS],
         _matmul_tn(xn, dgla, name="dw_in_ga"), _matmul_tn(xn, dglb, name="dw_in_gb")], axis=1)
    dx, dg_mix_pre = _input_norm_bwd(x, dxn, dh1, small["g_mix_pre"])

    grads = {"w_in": _full_to_slots(dw_in, 1), "w_branch_sgu": dw_bs, "w_branch_attn": dw_ba,
             "w_out": _full_to_slots(dw_out, 0), "w_up": dw_up, "w_down": _full_to_slots(dw_down, 0)}
    small_grads = {"g_mix_pre": dg_mix_pre, "b_forget": dbf[:, :N_HEADS], "g_sgu": dg_sgu, "b_sgu": db_sgu,
                   "w_spatial": dws.reshape(N_GROUPS * CHUNK, CHUNK), "b_spatial": dbs[:, :N_GROUPS].T,
                   "g_mix_post": dg_mix_post, "g_ffn_pre": dg_ffn_pre, "g_ffn_post": dg_ffn_post}
    return sq, dx, grads, small_grads


NAMES = ("g_mix_pre", "w_in", "b_forget", "g_sgu", "b_sgu", "w_spatial", "b_spatial", "w_branch_sgu", "w_branch_attn",
         "w_out", "g_mix_post", "g_ffn_pre", "w_up", "w_down", "g_ffn_post")


def kernel(x, g_mix_pre, w_in, b_forget, g_sgu, b_sgu, w_spatial, b_spatial, w_branch_sgu, w_branch_attn, w_out, g_mix_post, g_ffn_pre, w_up, w_down, g_ffn_post, loss_target, m_g_mix_pre, m_w_in, m_b_forget, m_g_sgu, m_b_sgu, m_w_spatial, m_b_spatial, m_w_branch_sgu, m_w_branch_attn, m_w_out, m_g_mix_post, m_g_ffn_pre, m_w_up, m_w_down, m_g_ffn_post, v_g_mix_pre, v_w_in, v_b_forget, v_g_sgu, v_b_sgu, v_w_spatial, v_b_spatial, v_w_branch_sgu, v_w_branch_attn, v_w_out, v_g_mix_post, v_g_ffn_pre, v_w_up, v_w_down, v_g_ffn_post):
    weights = dict(zip(NAMES, (g_mix_pre, w_in, b_forget, g_sgu, b_sgu, w_spatial, b_spatial, w_branch_sgu, w_branch_attn,
                               w_out, g_mix_post, g_ffn_pre, w_up, w_down, g_ffn_post), strict=True))
    first = dict(zip(NAMES, (m_g_mix_pre, m_w_in, m_b_forget, m_g_sgu, m_b_sgu, m_w_spatial, m_b_spatial, m_w_branch_sgu,
                             m_w_branch_attn, m_w_out, m_g_mix_post, m_g_ffn_pre, m_w_up, m_w_down, m_g_ffn_post), strict=True))
    second = dict(zip(NAMES, (v_g_mix_pre, v_w_in, v_b_forget, v_g_sgu, v_b_sgu, v_w_spatial, v_b_spatial, v_w_branch_sgu,
                              v_w_branch_attn, v_w_out, v_g_mix_post, v_g_ffn_pre, v_w_up, v_w_down, v_g_ffn_post), strict=True))
    shard_shapes = {name: _shard_shape(shape, axis) for name, shape, axis in SHARDED}
    small_shapes = dict(SMALL)
    view = lambda name, a: a.reshape(shard_shapes.get(name) or small_shapes[name])

    chip = 2 * lax.axis_index("x") + lax.axis_index("y")
    place = jnp.stack([chip, lax.axis_index("c")]).astype(jnp.int32)

    shards = [view(name, weights[name]).astype(BF) for name, _, _ in SHARDED]
    slot = jnp.arange(N_CHIPS)[:, None, None]
    full = {name: _slots_to_full(jnp.where(slot == chip, mine[None], got), axis)
            for (name, _, axis), mine, got in zip(SHARDED, shards, _gather_weights(shards), strict=True)}
    small = {name: view(name, weights[name]) for name, _ in SMALL}

    sq, dx, grads, small_grads = _local_step(x[0], loss_target[0], full, small)
    loss = lax.psum(0.5 * jnp.sum(sq) / D_MODEL, ("x", "y", "c"))

    mine = [grads[name] for name, _, _ in SHARDED] + [_pack_small(small_grads)]
    theirs, theirs_small = _exchange_halves(mine[:-1], mine[-1])
    sums = [_add_sibling(g, r, place, name="add_sibling_" + tag)
            for g, r, tag in zip(mine, list(theirs) + [theirs_small], [name for name, _, _ in SHARDED] + ["small"], strict=True)]
    received, received_small = _scatter_to_owners([b for _, b in sums[:-1]], sums[-1][0])
    totals = [_add_chips(s, r, place, name="add_chips_" + name, own_slots=True)
              for (s, _), r, (name, _, _) in zip(sums[:-1], received, SHARDED, strict=True)]
    totals.append(_add_chips(sums[-1][0], received_small, place, name="add_chips_small", own_slots=False))
    joined = _join_halves(totals)
    grad = {**{name: g for (name, _, _), g in zip(SHARDED, joined[:-1], strict=True)}, **_unpack_small(joined[-1])}

    delta, new_m, new_v = {}, {}, {}
    for name in NAMES:
        delta[name], new_m[name], new_v[name] = _adamw(
            view(name, weights[name]), grad[name], view(name, first[name]), view(name, second[name]), name="adamw_" + name)

    like = lambda d: [d[name].reshape(weights[name].shape) for name in NAMES]
    return (loss, dx[None], *like(grad), *like(delta), *like(new_m), *like(new_v))
```

```python
import functools

import jax
import jax.numpy as jnp
from jax import lax
from jax.experimental import pallas as pl
from jax.experimental.pallas import tpu as pltpu

F32 = jnp.float32
BF = jnp.bfloat16
MESH = pl.DeviceIdType.MESH

D_MODEL = 1024
N_HEADS = 8
HEAD_DIM = 64
ATTN_W = N_HEADS * HEAD_DIM
SGU_W = 512
N_GROUPS = 8
CHUNK = 128
D_FF = 4096
EPS = 1e-6
Q_SCALE = HEAD_DIM ** -0.5
N_CHIPS = 4
LANES = 128

ADAM_LR = 0.001
ADAM_B1 = 0.9
ADAM_B2 = 0.999
ADAM_EPS = 1e-08
ADAM_WD = 0.01
ADAM_STEP = 10

VMEM_LIMIT = 48 * 1024 * 1024
ATTN_BWD_VMEM = 58 * 1024 * 1024
NEG = -1e30

LANE_ROWSUM = HEAD_DIM
LANE_COLSUM = HEAD_DIM + 3


def _params(*sem):
    return pltpu.CompilerParams(dimension_semantics=sem, vmem_limit_bytes=VMEM_LIMIT)


def _dot(a, b):
    return jnp.dot(a, b, preferred_element_type=F32)


def _dot_nt(a, b):
    return lax.dot_general(a, b, (((1,), (1,)), ((), ())), preferred_element_type=F32)


def _dot_tn(a, b):
    return lax.dot_general(a, b, (((0,), (0,)), ((), ())), preferred_element_type=F32)


def _split3(c):
    hi = c.astype(BF).astype(F32)
    r = c - hi
    mid = r.astype(BF).astype(F32)
    lo = (r - mid).astype(BF).astype(F32)
    return hi, mid, lo


def _gelu(x):
    k = 0.7978845608028654
    return 0.5 * x * (1.0 + jnp.tanh(k * (x + 0.044715 * (x * x * x))))


def _gelu_grad(x):
    k = 0.7978845608028654
    x2 = x * x
    t = jnp.tanh(k * (x + 0.044715 * (x2 * x)))
    return 0.5 * (1.0 + t) + 0.5 * x * (1.0 - t * t) * (k * (1.0 + 3.0 * 0.044715 * x2))


def _rms_bwd(a, g, dy):
    r = lax.rsqrt(jnp.mean(a * a, axis=-1, keepdims=True) + EPS)
    n = a * r
    dn = dy * g
    da = r * (dn - n * jnp.mean(dn * n, axis=-1, keepdims=True))
    return da, dy * n


MM_ROWS = 1024
MM_COLS = 512


def _matmul(pairs, *, nt, out_dtypes, name, tm=MM_ROWS, tn=MM_COLS, epilogue=None, extras=()):
    n_pairs = len(pairs)
    n_extra = len(extras)
    M = pairs[0][0].shape[0]
    N = pairs[0][1].shape[0] if nt else pairs[0][1].shape[1]
    tm, tn = min(tm, M), min(tn, N)
    assert M % tm == 0 and N % tn == 0

    def body(*refs):
        acc = None
        for p in range(n_pairs):
            a_ref, b_ref = refs[2 * p], refs[2 * p + 1]
            d = _dot_nt(a_ref[...], b_ref[...]) if nt else _dot(a_ref[...], b_ref[...])
            acc = d if acc is None else acc + d
        e_refs = refs[2 * n_pairs:2 * n_pairs + n_extra]
        o_refs = refs[2 * n_pairs + n_extra:]
        outs = (acc,) if epilogue is None else epilogue(acc, *[e[...] for e in e_refs])
        for o_ref, o in zip(o_refs, outs, strict=True):
            o_ref[...] = o.astype(o_ref.dtype)

    in_specs, args = [], []
    for a, b in pairs:
        K = a.shape[1]
        in_specs.append(pl.BlockSpec((tm, K), lambda i, j: (i, 0)))
        in_specs.append(pl.BlockSpec((tn, K), lambda i, j: (j, 0)) if nt else pl.BlockSpec((K, tn), lambda i, j: (0, j)))
        args += [a, b]
    for e in extras:
        in_specs.append(pl.BlockSpec((tm, tn), lambda i, j: (i, j)))
        args.append(e)
    outs = pl.pallas_call(
        body, name=name, grid=(M // tm, N // tn), in_specs=in_specs,
        out_specs=[pl.BlockSpec((tm, tn), lambda i, j: (i, j)) for _ in out_dtypes],
        out_shape=[jax.ShapeDtypeStruct((M, N), dt) for dt in out_dtypes],
        compiler_params=_params("parallel", "parallel"),
    )(*args)
    return outs if len(outs) > 1 else outs[0]


def _matmul_tn(a, b, *, name, tm=1024, tn=1024, tk=512, slots=False):
    T, K1 = a.shape
    N = b.shape[1]
    tm, tn, tk = min(tm, K1), min(tn, N // N_CHIPS if slots else N), min(tk, T)
    assert K1 % tm == 0 and (N // N_CHIPS if slots else N) % tn == 0 and T % tk == 0
    per_slot = N // N_CHIPS // tn

    def body(a_ref, b_ref, o_ref):
        @pl.when(pl.program_id(2) == 0)
        def _():
            o_ref[...] = jnp.zeros_like(o_ref)

        o_ref[...] += _dot_tn(a_ref[...], b_ref[...])

    if slots:
        out_spec = pl.BlockSpec((None, tm, tn), lambda i, j, k: (j // per_slot, i, j % per_slot))
        out_shape = jax.ShapeDtypeStruct((N_CHIPS, K1, N // N_CHIPS), F32)
    else:
        out_spec = pl.BlockSpec((tm, tn), lambda i, j, k: (i, j))
        out_shape = jax.ShapeDtypeStruct((K1, N), F32)
    return pl.pallas_call(
        body, name=name, grid=(K1 // tm, N // tn, T // tk),
        in_specs=[pl.BlockSpec((tk, tm), lambda i, j, k: (k, i)), pl.BlockSpec((tk, tn), lambda i, j, k: (k, j))],
        out_specs=out_spec, out_shape=out_shape,
        compiler_params=_params("parallel", "parallel", "arbitrary"),
    )(a, b)


def _branch_merge(ysgu, yattn, w_bs, w_ba, gl, *, tm=MM_ROWS, tn=MM_COLS):
    T = ysgu.shape[0]
    tm = min(tm, T)
    nj = D_MODEL // tn

    def body(ys_ref, ya_ref, wbs_ref, wba_ref, gla_ref, glb_ref, a_ref, b_ref, m_ref):
        a = _dot(ys_ref[...], wbs_ref[...])
        b = _dot(ya_ref[...], wba_ref[...])
        a_ref[...] = a
        b_ref[...] = b
        m_ref[...] = (jax.nn.sigmoid(gla_ref[...]) * a + jax.nn.sigmoid(glb_ref[...]) * b).astype(BF)

    return pl.pallas_call(
        body, name="branch_merge", grid=(T // tm, nj),
        in_specs=[
            pl.BlockSpec((tm, SGU_W), lambda i, j: (i, 0)),
            pl.BlockSpec((tm, ATTN_W), lambda i, j: (i, 0)),
            pl.BlockSpec((SGU_W, tn), lambda i, j: (0, j)),
            pl.BlockSpec((ATTN_W, tn), lambda i, j: (0, j)),
            pl.BlockSpec((tm, tn), lambda i, j: (i, j)),
            pl.BlockSpec((tm, tn), lambda i, j: (i, j + nj)),
        ],
        out_specs=[pl.BlockSpec((tm, tn), lambda i, j: (i, j))] * 3,
        out_shape=[jax.ShapeDtypeStruct((T, D_MODEL), F32), jax.ShapeDtypeStruct((T, D_MODEL), F32),
                   jax.ShapeDtypeStruct((T, D_MODEL), BF)],
        compiler_params=_params("parallel", "parallel"),
    )(ysgu, yattn, w_bs, w_ba, gl, gl)


def _row_spec(tr, width):
    return pl.BlockSpec((tr, width), lambda i: (i, 0))


def _vec_spec(width):
    return pl.BlockSpec((1, width), lambda i: (0, 0))


def _rms_fwd(x, g, *, tr=256):
    T = x.shape[0]
    tr = min(tr, T)

    def body(x_ref, g_ref, o_ref):
        xv = x_ref[...]
        r = lax.rsqrt(jnp.mean(xv * xv, axis=-1, keepdims=True) + EPS)
        o_ref[...] = ((xv * r) * g_ref[...]).astype(BF)

    return pl.pallas_call(
        body, name="rms_fwd", grid=(T // tr,),
        in_specs=[_row_spec(tr, D_MODEL), _vec_spec(D_MODEL)], out_specs=_row_spec(tr, D_MODEL),
        out_shape=jax.ShapeDtypeStruct((T, D_MODEL), BF), compiler_params=_params("parallel"),
    )(x, g)


def _mixer_out_fwd(o, x, g_post, g_pre, *, tr=256):
    T = x.shape[0]
    tr = min(tr, T)

    def body(o_ref, x_ref, gpost_ref, gpre_ref, h1_ref, xn2_ref):
        ov = o_ref[...]
        r = lax.rsqrt(jnp.mean(ov * ov, axis=-1, keepdims=True) + EPS)
        h1 = x_ref[...] + (ov * r) * gpost_ref[...]
        h1_ref[...] = h1
        r2 = lax.rsqrt(jnp.mean(h1 * h1, axis=-1, keepdims=True) + EPS)
        xn2_ref[...] = ((h1 * r2) * gpre_ref[...]).astype(BF)

    return pl.pallas_call(
        body, name="mixer_out_fwd", grid=(T // tr,),
        in_specs=[_row_spec(tr, D_MODEL), _row_spec(tr, D_MODEL), _vec_spec(D_MODEL), _vec_spec(D_MODEL)],
        out_specs=[_row_spec(tr, D_MODEL), _row_spec(tr, D_MODEL)],
        out_shape=[jax.ShapeDtypeStruct((T, D_MODEL), F32), jax.ShapeDtypeStruct((T, D_MODEL), BF)],
        compiler_params=_params("parallel"),
    )(o, x, g_post, g_pre)


def _loss_head(dn, h1, target, g_post, *, tr=256):
    T = dn.shape[0]
    tr = min(tr, T)

    def body(dn_ref, h1_ref, t_ref, g_ref, sq_ref, dy_ref, ddn_ref, dg_ref):
        @pl.when(pl.program_id(0) == 0)
        def _():
            sq_ref[...] = jnp.zeros_like(sq_ref)
            dg_ref[...] = jnp.zeros_like(dg_ref)

        a = dn_ref[...]
        g = g_ref[...]
        r = lax.rsqrt(jnp.mean(a * a, axis=-1, keepdims=True) + EPS)
        err = h1_ref[...] + (a * r) * g - t_ref[...]
        sq_ref[...] += jnp.sum(err * err, axis=0, keepdims=True)
        dy = err * (1.0 / D_MODEL)
        dy_ref[...] = dy
        da, dgp = _rms_bwd(a, g, dy)
        ddn_ref[...] = da.astype(BF)
        dg_ref[...] += jnp.sum(dgp, axis=0, keepdims=True)

    return pl.pallas_call(
        body, name="loss_head", grid=(T // tr,),
        in_specs=[_row_spec(tr, D_MODEL)] * 3 + [_vec_spec(D_MODEL)],
        out_specs=[_vec_spec(D_MODEL), _row_spec(tr, D_MODEL), _row_spec(tr, D_MODEL), _vec_spec(D_MODEL)],
        out_shape=[jax.ShapeDtypeStruct((1, D_MODEL), F32), jax.ShapeDtypeStruct((T, D_MODEL), F32),
                   jax.ShapeDtypeStruct((T, D_MODEL), BF), jax.ShapeDtypeStruct((1, D_MODEL), F32)],
        compiler_params=_params("arbitrary"),
    )(dn, h1, target, g_post)


def _mixer_out_bwd(h1, dxn2, dy, o, g_pre, g_post, *, tr=256):
    T = h1.shape[0]
    tr = min(tr, T)

    def body(h1_ref, dxn2_ref, dy_ref, o_ref, gpre_ref, gpost_ref, dh1_ref, do_ref, dgpre_ref, dgpost_ref):
        @pl.when(pl.program_id(0) == 0)
        def _():
            dgpre_ref[...] = jnp.zeros_like(dgpre_ref)
            dgpost_ref[...] = jnp.zeros_like(dgpost_ref)

        da, dgp = _rms_bwd(h1_ref[...], gpre_ref[...], dxn2_ref[...])
        dh1 = dy_ref[...] + da
        dh1_ref[...] = dh1
        dgpre_ref[...] += jnp.sum(dgp, axis=0, keepdims=True)
        do, dgp2 = _rms_bwd(o_ref[...], gpost_ref[...], dh1)
        do_ref[...] = do.astype(BF)
        dgpost_ref[...] += jnp.sum(dgp2, axis=0, keepdims=True)

    return pl.pallas_call(
        body, name="mixer_out_bwd", grid=(T // tr,),
        in_specs=[_row_spec(tr, D_MODEL)] * 4 + [_vec_spec(D_MODEL)] * 2,
        out_specs=[_row_spec(tr, D_MODEL), _row_spec(tr, D_MODEL), _vec_spec(D_MODEL), _vec_spec(D_MODEL)],
        out_shape=[jax.ShapeDtypeStruct((T, D_MODEL), F32), jax.ShapeDtypeStruct((T, D_MODEL), BF),
                   jax.ShapeDtypeStruct((1, D_MODEL), F32), jax.ShapeDtypeStruct((1, D_MODEL), F32)],
        compiler_params=_params("arbitrary"),
    )(h1, dxn2, dy, o, g_pre, g_post)


def _input_norm_bwd(x, dxn, dh1, g, *, tr=256):
    T = x.shape[0]
    tr = min(tr, T)

    def body(x_ref, dxn_ref, dh1_ref, g_ref, dx_ref, dg_ref):
        @pl.when(pl.program_id(0) == 0)
        def _():
            dg_ref[...] = jnp.zeros_like(dg_ref)

        da, dgp = _rms_bwd(x_ref[...], g_ref[...], dxn_ref[...])
        dx_ref[...] = dh1_ref[...] + da
        dg_ref[...] += jnp.sum(dgp, axis=0, keepdims=True)

    return pl.pallas_call(
        body, name="input_norm_bwd", grid=(T // tr,),
        in_specs=[_row_spec(tr, D_MODEL)] * 3 + [_vec_spec(D_MODEL)],
        out_specs=[_row_spec(tr, D_MODEL), _vec_spec(D_MODEL)],
        out_shape=[jax.ShapeDtypeStruct((T, D_MODEL), F32), jax.ShapeDtypeStruct((1, D_MODEL), F32)],
        compiler_params=_params("arbitrary"),
    )(x, dxn, dh1, g)


def _gate_bwd(dm, a, b, gl, *, tr=256):
    T = dm.shape[0]
    tr = min(tr, T)

    def body(dm_ref, a_ref, b_ref, gla_ref, glb_ref, da_ref, db_ref, dgla_ref, dglb_ref):
        dmv = dm_ref[...]
        ga = jax.nn.sigmoid(gla_ref[...])
        gb = jax.nn.sigmoid(glb_ref[...])
        da_ref[...] = (dmv * ga).astype(BF)
        db_ref[...] = (dmv * gb).astype(BF)
        dgla_ref[...] = (dmv * a_ref[...] * (ga * (1.0 - ga))).astype(BF)
        dglb_ref[...] = (dmv * b_ref[...] * (gb * (1.0 - gb))).astype(BF)

    spec = _row_spec(tr, D_MODEL)
    spec_b = pl.BlockSpec((tr, D_MODEL), lambda i: (i, 1))
    da, db, dgla, dglb = pl.pallas_call(
        body, name="gate_bwd", grid=(T // tr,),
        in_specs=[spec, spec, spec, spec, spec_b], out_specs=[spec] * 4,
        out_shape=[jax.ShapeDtypeStruct((T, D_MODEL), BF)] * 4, compiler_params=_params("parallel"),
    )(dm, a, b, gl, gl)
    return da, db, dgla, dglb


def _sgu_norm(z_tile, g, b):
    gz = _gelu(z_tile)
    u, vv = gz[:, :SGU_W], gz[:, SGU_W:]
    xc = vv - jnp.mean(vv, axis=-1, keepdims=True)
    rstd = lax.rsqrt(jnp.mean(xc * xc, axis=-1, keepdims=True) + EPS)
    xhat = xc * rstd
    return u, xhat, rstd, xhat * g + b


def _sgu_mix(w_ref, v_bf, first_half):
    parts = []
    for p in range(N_GROUPS // 2):
        vp = v_bf[:, p * LANES:(p + 1) * LANES]
        parts.append(jnp.where(first_half, _dot(w_ref[2 * p], vp), _dot(w_ref[2 * p + 1], vp)))
    return jnp.concatenate(parts, axis=1)


def _sgu_fwd(z, g_sgu, b_sgu, ws, bias_plane, *, tm=512):
    T = z.shape[0]
    tm = min(tm, T)

    def body(z_ref, g_ref, b_ref, ws_ref, bp_ref, y_ref):
        u, _, _, vn = _sgu_norm(z_ref[...], g_ref[...], b_ref[...])
        vn_bf = vn.astype(BF)
        first_half = lax.broadcasted_iota(jnp.int32, (CHUNK, LANES), 1) < HEAD_DIM
        for c in range(tm // CHUNK):
            rows = slice(c * CHUNK, (c + 1) * CHUNK)
            s = _sgu_mix(ws_ref, vn_bf[rows, :], first_half) + bp_ref[...]
            y_ref[rows, :] = (u[rows, :] * s).astype(BF)

    return pl.pallas_call(
        body, name="sgu_fwd", grid=(T // tm,),
        in_specs=[_row_spec(tm, 2 * SGU_W), _vec_spec(SGU_W), _vec_spec(SGU_W),
                  pl.BlockSpec((N_GROUPS, CHUNK, CHUNK), lambda i: (0, 0, 0)),
                  pl.BlockSpec((CHUNK, SGU_W), lambda i: (0, 0))],
        out_specs=_row_spec(tm, SGU_W), out_shape=jax.ShapeDtypeStruct((T, SGU_W), BF),
        compiler_params=_params("parallel"),
    )(z, g_sgu, b_sgu, ws, bias_plane)


def _sgu_bwd(dy, z, g_sgu, b_sgu, ws, ws_t, bias_plane, *, tm=512):
    T = z.shape[0]
    tm = min(tm, T)
    n_steps = T // tm

    def body(dy_ref, z_ref, g_ref, b_ref, ws_ref, wst_ref, bp_ref, dz_ref, dws_ref, dbs_ref, dg_ref, db_ref, dbp_ref):
        step = pl.program_id(0)

        @pl.when(step == 0)
        def _():
            dws_ref[...] = jnp.zeros_like(dws_ref)
            dg_ref[...] = jnp.zeros_like(dg_ref)
            db_ref[...] = jnp.zeros_like(db_ref)
            dbp_ref[...] = jnp.zeros_like(dbp_ref)

        g = g_ref[...]
        zt = z_ref[...]
        u, xhat, rstd, vn = _sgu_norm(zt, g, b_ref[...])
        vn_bf = vn.astype(BF)
        first_half = lax.broadcasted_iota(jnp.int32, (CHUNK, LANES), 1) < HEAD_DIM
        dyv = dy_ref[...]
        dg_acc = jnp.zeros((1, SGU_W), F32)
        db_acc = jnp.zeros((1, SGU_W), F32)
        for c in range(tm // CHUNK):
            rows = slice(c * CHUNK, (c + 1) * CHUNK)
            v_c = vn_bf[rows, :]
            s = _sgu_mix(ws_ref, v_c, first_half) + bp_ref[...]
            dy_c = dyv[rows, :]
            du = dy_c * s
            dsv = dy_c * u[rows, :]
            dbp_ref[...] += dsv
            ds_bf = dsv.astype(BF)
            zero = jnp.zeros((CHUNK, LANES), BF)
            for p in range(N_GROUPS // 2):
                dsp = ds_bf[:, p * LANES:(p + 1) * LANES]
                vp = v_c[:, p * LANES:(p + 1) * LANES]
                dws_ref[2 * p] += _dot_nt(jnp.where(first_half, dsp, zero), vp)
                dws_ref[2 * p + 1] += _dot_nt(jnp.where(first_half, zero, dsp), vp)
            dvn = _sgu_mix(wst_ref, ds_bf, first_half)
            xh = xhat[rows, :]
            dxh = dvn * g
            dvv = rstd[rows, :] * (dxh - jnp.mean(dxh, axis=-1, keepdims=True)
                                   - xh * jnp.mean(dxh * xh, axis=-1, keepdims=True))
            dg_acc += jnp.sum(dvn * xh, axis=0, keepdims=True)
            db_acc += jnp.sum(dvn, axis=0, keepdims=True)
            dgz = jnp.concatenate([du, dvv], axis=1)
            dz_ref[rows, :] = (dgz * _gelu_grad(zt[rows, :])).astype(BF)
        dg_ref[...] += dg_acc
        db_ref[...] += db_acc

        @pl.when(step == n_steps - 1)
        def _():
            r = lax.broadcasted_iota(jnp.int32, (CHUNK, CHUNK), 0)
            cidx = lax.broadcasted_iota(jnp.int32, (CHUNK, CHUNK), 1)
            causal = (cidx <= r).astype(F32)
            for gi in range(N_GROUPS):
                dws_ref[gi] = dws_ref[gi] * causal
            lane = lax.broadcasted_iota(jnp.int32, (CHUNK, LANES), 1)
            out = jnp.zeros((CHUNK, LANES), F32)
            dbp = dbp_ref[...]
            for gi in range(N_GROUPS):
                col = jnp.sum(dbp[:, gi * HEAD_DIM:(gi + 1) * HEAD_DIM], axis=1, keepdims=True)
                out = jnp.where(lane == gi, col, out)
            dbs_ref[...] = out

    w_spec = pl.BlockSpec((N_GROUPS, CHUNK, CHUNK), lambda i: (0, 0, 0))
    plane = pl.BlockSpec((CHUNK, SGU_W), lambda i: (0, 0))
    return pl.pallas_call(
        body, name="sgu_bwd", grid=(n_steps,),
        in_specs=[_row_spec(tm, SGU_W), _row_spec(tm, 2 * SGU_W), _vec_spec(SGU_W), _vec_spec(SGU_W), w_spec, w_spec, plane],
        out_specs=[_row_spec(tm, 2 * SGU_W), w_spec, pl.BlockSpec((CHUNK, LANES), lambda i: (0, 0)),
                   _vec_spec(SGU_W), _vec_spec(SGU_W)],
        out_shape=[jax.ShapeDtypeStruct((T, 2 * SGU_W), BF), jax.ShapeDtypeStruct((N_GROUPS, CHUNK, CHUNK), F32),
                   jax.ShapeDtypeStruct((CHUNK, LANES), F32), jax.ShapeDtypeStruct((1, SGU_W), F32),
                   jax.ShapeDtypeStruct((1, SGU_W), F32)],
        scratch_shapes=[pltpu.VMEM((CHUNK, SGU_W), F32)],
        compiler_params=_params("arbitrary"),
    )(dy, z, g_sgu, b_sgu, ws, ws_t, bias_plane)


def _tri(n, upper):
    r = lax.broadcasted_iota(jnp.int32, (n, n), 0)
    c = lax.broadcasted_iota(jnp.int32, (n, n), 1)
    return ((c >= r) if upper else (c <= r)).astype(BF)


def _scan_dot(tri, x):
    hi, mid, lo = _split3(x)
    return (_dot(tri, hi.astype(BF)) + _dot(tri, mid.astype(BF))) + _dot(tri, lo.astype(BF))


def _with_lanes(base, lane, start, cols):
    out = base
    for k, col in enumerate(cols):
        if col is not None:
            out = jnp.where(lane == start + k, col, out)
    return out


def _logit_bound(q_norm, k_norm):
    return NORM_SLACK * q_norm * k_norm + 1.0


ATTN_TILE = 512
SKIP_BELOW = -110.0
NORM_SLACK = 1.001
BOUNDED_GAP = 60.0


def _attn_prep(qkv, fl, b_forget, *, tp=ATTN_TILE):
    T = qkv.shape[0]
    tp = min(tp, T)

    def body(qkv_ref, fl_ref, bf_ref, qf_ref, kl_ref, vl_ref, st_ref, carry_ref, kmax_ref):
        @pl.when(pl.program_id(0) == 0)
        def _():
            carry_ref[...] = jnp.zeros_like(carry_ref)
            kmax_ref[...] = jnp.zeros_like(kmax_ref)

        x = fl_ref[...] + bf_ref[...]
        logf = jnp.minimum(x, 0.0) - jnp.log(1.0 + jnp.exp(-jnp.abs(x)))
        cum = _scan_dot(_tri(tp, upper=False), logf) + carry_ref[...]
        carry_ref[...] = cum[tp - 1:tp, :]
        lane = lax.broadcasted_iota(jnp.int32, (tp, HEAD_DIM), 1)
        ones3 = jnp.where(lane < 3, 1.0, 0.0)
        qkvv = qkv_ref[...]
        st_row = lax.broadcasted_iota(jnp.int32, (N_HEADS, LANES), 0)
        st_lane = lax.broadcasted_iota(jnp.int32, (N_HEADS, LANES), 1)
        stats = jnp.zeros((N_HEADS, LANES), F32)
        kmax_lane = lax.broadcasted_iota(jnp.int32, (1, LANES), 1)
        for h in range(N_HEADS):
            ch = cum[:, h:h + 1]
            c3 = _split3(ch)
            qh = qkvv[:, h * HEAD_DIM:(h + 1) * HEAD_DIM].astype(F32) * Q_SCALE
            kh = qkvv[:, ATTN_W + h * HEAD_DIM:ATTN_W + (h + 1) * HEAD_DIM].astype(F32)
            vh = qkvv[:, 2 * ATTN_W + h * HEAD_DIM:2 * ATTN_W + (h + 1) * HEAD_DIM].astype(F32)
            q_norm = jnp.sqrt(jnp.sum(qh * qh, axis=1, keepdims=True))
            qn = jnp.max(q_norm, axis=0, keepdims=True)
            kn = jnp.sqrt(jnp.max(jnp.sum(kh * kh, axis=1, keepdims=True), axis=0, keepdims=True))
            k_seen = jnp.maximum(kmax_ref[:, h:h + 1], kn)
            kmax_ref[...] = jnp.where(kmax_lane == h, k_seen, kmax_ref[...])
            bound3 = _split3(-_logit_bound(q_norm, k_seen))
            ext_q = _with_lanes(jnp.where((lane >= 3) & (lane < 6), 1.0, 0.0), lane, 0, list(c3) + [None] * 3 + list(bound3))
            ext_k = _with_lanes(jnp.where((lane < 3) | ((lane >= 6) & (lane < 9)), 1.0, 0.0), lane, 3, [-c for c in c3])
            qf_ref[h] = jnp.concatenate([qh, ext_q], axis=1).astype(BF)
            kl_ref[h] = jnp.concatenate([kh, ext_k], axis=1).astype(BF)
            vl_ref[h] = jnp.concatenate([vh, ones3], axis=1).astype(BF)
            tile_stats = (qn, kn, jnp.max(ch, axis=0, keepdims=True), jnp.min(ch, axis=0, keepdims=True), k_seen)
            for k, val in enumerate(tile_stats):
                stats = jnp.where((st_row == h) & (st_lane == k), val, stats)
        st_ref[0] = stats

    head_spec = pl.BlockSpec((N_HEADS, tp, LANES), lambda i: (0, i, 0))
    return pl.pallas_call(
        body, name="attn_prep", grid=(T // tp,),
        in_specs=[_row_spec(tp, 3 * ATTN_W), _row_spec(tp, LANES), _vec_spec(LANES)],
        out_specs=[head_spec] * 3 + [pl.BlockSpec((1, N_HEADS, LANES), lambda i: (i, 0, 0))],
        out_shape=[jax.ShapeDtypeStruct((N_HEADS, T, LANES), BF)] * 3 + [jax.ShapeDtypeStruct((T // tp, N_HEADS, LANES), F32)],
        scratch_shapes=[pltpu.VMEM((1, LANES), F32), pltpu.VMEM((1, LANES), F32)], compiler_params=_params("arbitrary"),
    )(qkv, fl, b_forget)


def _attn_ranges(stats):
    qn, kn, cmax, cmin, k_seen = (stats[:, :, k].T for k in range(5))
    n = qn.shape[1]
    bounded = (2.0 * _logit_bound(qn, k_seen) <= BOUNDED_GAP).reshape(N_HEADS // 2, 2, n).all(axis=1)
    reach = NORM_SLACK * qn * (jnp.max(kn, axis=1, keepdims=True) + kn) + cmax
    i = jnp.arange(n)[None, :, None]
    j = jnp.arange(n)[None, None, :]
    need = ((reach[:, :, None] - cmin[:, None, :] >= SKIP_BELOW) | (i == j)) & (j <= i)
    first = jnp.min(jnp.where(need, j, n), axis=2).reshape(N_HEADS // 2, 2, n).min(axis=1)
    last = jnp.max(jnp.where(need, i, -1), axis=1).reshape(N_HEADS // 2, 2, n).max(axis=1)
    return first.reshape(-1).astype(F32), last.reshape(-1).astype(F32), bounded.reshape(-1).astype(F32)


def _pair_block(t):
    return pl.BlockSpec((2, t, LANES), lambda p, i, *_: (p, i, 0))


def _pair_full(T):
    return pl.BlockSpec((2, T, LANES), lambda p, i, *_: (p, 0, 0))


def _packed_block(t):
    return pl.BlockSpec((t, LANES), lambda p, i, *_: (i, p))


def _causal(t, keys_in_rows=False):
    r = lax.broadcasted_iota(jnp.int32, (t, t), 0)
    c = lax.broadcasted_iota(jnp.int32, (t, t), 1)
    return (r <= c) if keys_in_rows else (c <= r)


def _tile_rows(j, t):
    return pl.ds(pl.multiple_of(j * t, t), t)


def _attn_call(body, name, tile_scalars, operands, in_specs, out_specs, out_shape, scratch_shapes, n_tiles):
    return pl.pallas_call(
        body, name=name,
        grid_spec=pltpu.PrefetchScalarGridSpec(
            num_scalar_prefetch=len(tile_scalars), grid=(N_HEADS // 2, n_tiles), in_specs=in_specs, out_specs=out_specs,
            scratch_shapes=scratch_shapes),
        out_shape=out_shape, compiler_params=_params("parallel", "arbitrary"),
    )(*tile_scalars, *operands)


def _attn_fwd(qf, kl, vl, first, bounded, *, tq=ATTN_TILE):
    T = qf.shape[1]
    tq = min(tq, T)
    n = T // tq

    def body(first_ref, bounded_ref, qf_ref, kl_ref, vl_ref, o_ref, of_ref, ql_ref, m_ref, acc_ref):
        i = pl.program_id(1)
        tile = pl.program_id(0) * n + i
        start = first_ref[tile].astype(jnp.int32)
        is_bounded = bounded_ref[tile] > 0.5
        acc_ref[...] = jnp.zeros_like(acc_ref)
        diagonal = _tile_rows(i, tq)
        causal = _causal(tq)

        def logits(hh, rows):
            return _dot_nt(qf_ref[hh], kl_ref[hh, rows, :])

        @pl.when(is_bounded)
        def _():
            m_ref[...] = jnp.zeros_like(m_ref)

            def update(hh, s, rows):
                acc_ref[hh] += _dot(jnp.exp(s).astype(BF), vl_ref[hh, rows, :])

            def step(j, carry):
                for hh in range(2):
                    update(hh, logits(hh, _tile_rows(j, tq)), _tile_rows(j, tq))
                return carry

            lax.fori_loop(start, i, step, 0)
            for hh in range(2):
                update(hh, jnp.where(causal, logits(hh, diagonal), NEG), diagonal)

        @pl.when(jnp.logical_not(is_bounded))
        def _():
            m_ref[...] = jnp.full_like(m_ref, NEG)

            def update(hh, s, rows):
                m_old = m_ref[hh]
                m_new = jnp.maximum(m_old, jnp.max(s, axis=1, keepdims=True))
                p = jnp.exp(s - m_new)
                acc_ref[hh] = jnp.exp(m_old - m_new) * acc_ref[hh] + _dot(p.astype(BF), vl_ref[hh, rows, :])
                m_ref[hh] = m_new

            def step(j, carry):
                for hh in range(2):
                    update(hh, logits(hh, _tile_rows(j, tq)), _tile_rows(j, tq))
                return carry

            lax.fori_loop(start, i, step, 0)
            for hh in range(2):
                update(hh, jnp.where(causal, logits(hh, diagonal), NEG), diagonal)

        lane = lax.broadcasted_iota(jnp.int32, (tq, LANES), 1)
        outs = []
        for hh in range(2):
            q = qf_ref[hh].astype(F32)
            acc = acc_ref[hh]
            l = acc[:, HEAD_DIM:HEAD_DIM + 1]
            outs.append(acc[:, :HEAD_DIM] / l)
            at = HEAD_DIM + 6
            neg_bound = (q[:, at:at + 1] + q[:, at + 1:at + 2]) + q[:, at + 2:at + 3]
            ql_ref[hh] = _with_lanes(q, lane, at, _split3(neg_bound - (m_ref[hh] + jnp.log(l)))).astype(BF)
        o = jnp.concatenate(outs, axis=1)
        o_ref[...] = o.astype(BF)
        of_ref[...] = o

    return _attn_call(
        body, "attn_fwd", (first, bounded), (qf, kl, vl), [_pair_block(tq), _pair_full(T), _pair_full(T)],
        [_packed_block(tq), _packed_block(tq), _pair_block(tq)],
        [jax.ShapeDtypeStruct((T, ATTN_W), BF), jax.ShapeDtypeStruct((T, ATTN_W), F32),
         jax.ShapeDtypeStruct((N_HEADS, T, LANES), BF)],
        [pltpu.VMEM((2, tq, 1), F32), pltpu.VMEM((2, tq, LANES), F32)], n)


def _attn_bwd_prep(dya, of, *, tr=256):
    T = dya.shape[0]
    tr = min(tr, T)

    def body(d_ref, o_ref, do_ref):
        lane = lax.broadcasted_iota(jnp.int32, (tr, HEAD_DIM), 1)
        dv, ov = d_ref[...], o_ref[...]
        for h in range(N_HEADS):
            d = dv[:, h * HEAD_DIM:(h + 1) * HEAD_DIM]
            delta = jnp.sum(d * ov[:, h * HEAD_DIM:(h + 1) * HEAD_DIM], axis=1, keepdims=True)
            ext = _with_lanes(jnp.zeros((tr, HEAD_DIM), F32), lane, 0, _split3(-delta))
            do_ref[h] = jnp.concatenate([d, ext], axis=1).astype(BF)

    return pl.pallas_call(
        body, name="attn_bwd_prep", grid=(T // tr,),
        in_specs=[_row_spec(tr, ATTN_W), _row_spec(tr, ATTN_W)],
        out_specs=pl.BlockSpec((N_HEADS, tr, LANES), lambda i: (0, i, 0)),
        out_shape=jax.ShapeDtypeStruct((N_HEADS, T, LANES), BF), compiler_params=_params("parallel"),
    )(dya, of)


def _attn_bwd(kl, vl, ql, do, last, *, tk=ATTN_TILE):
    T = ql.shape[1]
    tk = min(tk, T)
    n = T // tk

    def body(last_ref, kl_ref, vl_ref, ql_ref, do_ref, dq_ref, dk_ref, dv_ref, extq_ref, extk_ref, dq_acc, dk_acc, dv_acc):
        j = pl.program_id(1)

        @pl.when(j == 0)
        def _():
            dq_acc[...] = jnp.zeros_like(dq_acc)

        dk_acc[...] = jnp.zeros_like(dk_acc)
        dv_acc[...] = jnp.zeros_like(dv_acc)

        def block(hh, rows, mask):
            qi, di, k = ql_ref[hh, rows, :], do_ref[hh, rows, :], kl_ref[hh]
            p_t = jnp.exp(_dot_nt(k, qi))
            if mask is not None:
                p_t = jnp.where(mask, p_t, 0.0)
            ds_t = (p_t * _dot_nt(vl_ref[hh], di)).astype(BF)
            dk_acc[hh] += _dot(ds_t, qi)
            dv_acc[hh] += _dot(p_t.astype(BF), di)
            dq_acc[hh, rows, :] += _dot_tn(ds_t, k)

        causal_t = _causal(tk, keys_in_rows=True)
        for hh in range(2):
            block(hh, _tile_rows(j, tk), causal_t)

        def step(i, carry):
            for hh in range(2):
                block(hh, _tile_rows(i, tk), None)
            return carry

        lax.fori_loop(j + 1, last_ref[pl.program_id(0) * n + j].astype(jnp.int32) + 1, step, 0)
        dk_ref[...] = jnp.concatenate([dk_acc[hh][:, :HEAD_DIM] for hh in range(2)], axis=1).astype(BF)
        dv_ref[...] = jnp.concatenate([dv_acc[hh][:, :HEAD_DIM] for hh in range(2)], axis=1).astype(BF)
        extk_ref[...] = jnp.concatenate([dk_acc[hh][:, HEAD_DIM:] for hh in range(2)], axis=1)

        @pl.when(j == n - 1)
        def _():
            dq_ref[...] = jnp.concatenate([dq_acc[hh][:, :HEAD_DIM] * Q_SCALE for hh in range(2)], axis=1).astype(BF)
            extq_ref[...] = jnp.concatenate([dq_acc[hh][:, HEAD_DIM:] for hh in range(2)], axis=1)

    whole = pl.BlockSpec((T, LANES), lambda p, j, *_: (0, p))
    return pl.pallas_call(
        body, name="attn_bwd",
        grid_spec=pltpu.PrefetchScalarGridSpec(
            num_scalar_prefetch=1, grid=(N_HEADS // 2, n),
            in_specs=[_pair_block(tk), _pair_block(tk), _pair_full(T), _pair_full(T)],
            out_specs=[whole, _packed_block(tk), _packed_block(tk), whole, _packed_block(tk)],
            scratch_shapes=[pltpu.VMEM((2, T, LANES), F32), pltpu.VMEM((2, tk, LANES), F32), pltpu.VMEM((2, tk, LANES), F32)]),
        out_shape=[jax.ShapeDtypeStruct((T, ATTN_W), BF)] * 3 + [jax.ShapeDtypeStruct((T, ATTN_W), F32)] * 2,
        compiler_params=pltpu.CompilerParams(dimension_semantics=("parallel", "arbitrary"), vmem_limit_bytes=ATTN_BWD_VMEM),
    )(last, kl, vl, ql, do)


def _forget_bwd(ext_q, ext_k, fl, b_forget, *, tp=256):
    T = fl.shape[0]
    tp = min(tp, T)
    n = T // tp

    def body(eq_ref, ek_ref, fl_ref, bf_ref, dfl_ref, dbf_ref, carry_ref):
        @pl.when(pl.program_id(0) == 0)
        def _():
            carry_ref[...] = jnp.zeros_like(carry_ref)
            dbf_ref[...] = jnp.zeros_like(dbf_ref)

        lane = lax.broadcasted_iota(jnp.int32, (tp, LANES), 1)
        eq, ek = eq_ref[...], ek_ref[...]
        cols = [eq[:, h * HEAD_DIM:h * HEAD_DIM + 1] - ek[:, h * HEAD_DIM + 3:h * HEAD_DIM + 4] for h in range(N_HEADS)]
        dcum = _with_lanes(jnp.zeros((tp, LANES), F32), lane, 0, cols)
        suffix = _scan_dot(_tri(tp, upper=True), dcum) + carry_ref[...]
        carry_ref[...] = suffix[0:1, :]
        x = fl_ref[...] + bf_ref[...]
        dfl = jnp.where(lane < N_HEADS, suffix / (1.0 + jnp.exp(x)), 0.0)
        dfl_ref[...] = dfl.astype(BF)
        dbf_ref[...] += jnp.sum(dfl, axis=0, keepdims=True)

    rev = lambda w: pl.BlockSpec((tp, w), lambda i: (n - 1 - i, 0))
    return pl.pallas_call(
        body, name="forget_bwd", grid=(n,),
        in_specs=[rev(ATTN_W), rev(ATTN_W), rev(LANES), _vec_spec(LANES)],
        out_specs=[rev(LANES), _vec_spec(LANES)],
        out_shape=[jax.ShapeDtypeStruct((T, LANES), BF), jax.ShapeDtypeStruct((1, LANES), F32)],
        scratch_shapes=[pltpu.VMEM((1, LANES), F32)], compiler_params=_params("arbitrary"),
    )(ext_q, ext_k, fl, b_forget)


def _adamw(w, g, m, v, *, name, tr=256):
    rows, cols = w.shape
    tr = tr if rows % tr == 0 else rows

    def body(w_ref, g_ref, m_ref, v_ref, d_ref, nm_ref, nv_ref):
        gv = g_ref[...]
        nm = ADAM_B1 * m_ref[...] + (1.0 - ADAM_B1) * gv
        nv = ADAM_B2 * v_ref[...] + (1.0 - ADAM_B2) * (gv * gv)
        m_hat = nm / (1.0 - ADAM_B1 ** ADAM_STEP)
        v_hat = nv / (1.0 - ADAM_B2 ** ADAM_STEP)
        d_ref[...] = -ADAM_LR * (m_hat / (jnp.sqrt(v_hat) + ADAM_EPS) + ADAM_WD * w_ref[...])
        nm_ref[...] = nm
        nv_ref[...] = nv

    spec = pl.BlockSpec((tr, cols), lambda i: (i, 0))
    return pl.pallas_call(
        body, name=name, grid=(rows // tr,), in_specs=[spec] * 4, out_specs=[spec] * 3,
        out_shape=[jax.ShapeDtypeStruct((rows, cols), F32)] * 3, compiler_params=_params("parallel"),
    )(w, g, m, v)


HBM = pl.BlockSpec(memory_space=pltpu.HBM)
BF16_ROWS = 16


def _place():
    x, y, c = lax.axis_index("x"), lax.axis_index("y"), lax.axis_index("c")
    others = [(1 - x, y), (x, 1 - y), (1 - x, 1 - y)]
    return x, y, c, others


def _chip(xy):
    return 2 * xy[0] + xy[1]


def _row_halves(c, rows):
    half = rows // 2
    assert half % BF16_ROWS == 0
    return (pl.ds(pl.multiple_of(c * half, BF16_ROWS), half), pl.ds(pl.multiple_of((1 - c) * half, BF16_ROWS), half))


def _remote(src, dst, send_sems, recv_sems, k, to):
    return pltpu.make_async_remote_copy(src_ref=src, dst_ref=dst, send_sem=send_sems.at[k], recv_sem=recv_sems.at[k],
                                        device_id=to, device_id_type=MESH)


def _gather_weights(shards):
    n = len(shards)

    def body(*refs):
        w_refs, g_refs, (send_sems, recv_sems) = refs[:n], refs[n:2 * n], refs[2 * n:]
        x, y, c, others = _place()
        sibling, me = (x, y, 1 - c), _chip((x, y))
        first = []
        for a in range(n):
            mine, _ = _row_halves(c, w_refs[a].shape[0])
            for j, o in enumerate(others):
                first.append(_remote(w_refs[a].at[mine, :], g_refs[a].at[me, mine, :], send_sems, recv_sems, 6 * a + j, (*o, c)))
        for cp in first:
            cp.start()
        passed = []
        for j, o in enumerate(others):
            for a in range(n):
                mine, _ = _row_halves(c, w_refs[a].shape[0])
                landed = g_refs[a].at[_chip(o), mine, :]
                _remote(landed, landed, send_sems, recv_sems, 6 * a + j, (*o, c)).wait_recv()
                passed.append(_remote(landed, landed, send_sems, recv_sems, 6 * a + 3 + j, sibling))
                passed[-1].start()
        for j, o in enumerate(others):
            for a in range(n):
                _, theirs = _row_halves(c, w_refs[a].shape[0])
                landed = g_refs[a].at[_chip(o), theirs, :]
                _remote(landed, landed, send_sems, recv_sems, 6 * a + 3 + j, sibling).wait_recv()
        for cp in first + passed:
            cp.wait_send()

    return pl.pallas_call(
        body, name="gather_weights", in_specs=[HBM] * n, out_specs=[HBM] * n,
        out_shape=[jax.ShapeDtypeStruct((N_CHIPS,) + s.shape, s.dtype) for s in shards],
        scratch_shapes=[pltpu.SemaphoreType.DMA((6 * n,)), pltpu.SemaphoreType.DMA((6 * n,))],
    )(*shards)


def _exchange_halves(grads, small):
    n = len(grads)

    def body(*refs):
        g_refs, r_refs, (send_sems, recv_sems) = refs[:n + 1], refs[n + 1:2 * n + 2], refs[2 * n + 2:]
        x, y, c, _ = _place()
        copies = []
        for a in range(n + 1):
            _, theirs = _row_halves(c, g_refs[a].shape[-2])
            src = g_refs[a].at[:, theirs, :] if a < n else g_refs[a].at[theirs, :]
            copies.append(_remote(src, r_refs[a], send_sems, recv_sems, a, (x, y, 1 - c)))
            copies[-1].start()
        for cp in copies:
            cp.wait()

    def half(s):
        return jax.ShapeDtypeStruct(s.shape[:-2] + (s.shape[-2] // 2, s.shape[-1]), F32)

    outs = pl.pallas_call(
        body, name="exchange_halves", in_specs=[HBM] * (n + 1), out_specs=[HBM] * (n + 1),
        out_shape=[half(g) for g in grads] + [half(small)],
        scratch_shapes=[pltpu.SemaphoreType.DMA((n + 1,)), pltpu.SemaphoreType.DMA((n + 1,))],
    )(*grads, small)
    return outs[:n], outs[n]


def _scatter_to_owners(chip_sums, small_sum):
    n = len(chip_sums)

    def body(*refs):
        b_refs, r_refs, (send_sems, recv_sems) = refs[:n + 1], refs[n + 1:2 * n + 2], refs[2 * n + 2:]
        x, y, c, others = _place()
        me = _chip((x, y))
        sends = []
        for a in range(n + 1):
            for j, o in enumerate(others):
                src = b_refs[a].at[_chip(o)] if a < n else b_refs[a]
                sends.append(_remote(src, r_refs[a].at[me], send_sems, recv_sems, 3 * a + j, (*o, c)))
                sends[-1].start()
        for a in range(n + 1):
            for j, o in enumerate(others):
                landed = r_refs[a].at[_chip(o)]
                _remote(landed, landed, send_sems, recv_sems, 3 * a + j, (*o, c)).wait_recv()
        for cp in sends:
            cp.wait_send()

    outs = pl.pallas_call(
        body, name="scatter_to_owners", in_specs=[HBM] * (n + 1), out_specs=[HBM] * (n + 1),
        out_shape=[jax.ShapeDtypeStruct(b.shape, BF) for b in chip_sums]
        + [jax.ShapeDtypeStruct((N_CHIPS,) + small_sum.shape, F32)],
        scratch_shapes=[pltpu.SemaphoreType.DMA((3 * (n + 1),)), pltpu.SemaphoreType.DMA((3 * (n + 1),))],
    )(*chip_sums, small_sum)
    return outs[:n], outs[n]


def _join_halves(totals):
    n = len(totals)

    def body(*refs):
        in_refs, out_refs, (send_sems, recv_sems) = refs[:n], refs[n:2 * n], refs[2 * n:]
        x, y, c, _ = _place()
        copies = []
        for a in range(n):
            mine, _ = _row_halves(c, in_refs[a].shape[0])
            copies.append(_remote(in_refs[a].at[mine, :], out_refs[a].at[mine, :], send_sems, recv_sems, a, (x, y, 1 - c)))
            copies[-1].start()
        for cp in copies:
            cp.wait()

    return pl.pallas_call(
        body, name="join_halves", in_specs=[HBM] * n, out_specs=[HBM] * n,
        out_shape=[jax.ShapeDtypeStruct(t.shape, F32) for t in totals], input_output_aliases={a: a for a in range(n)},
        scratch_shapes=[pltpu.SemaphoreType.DMA((n,)), pltpu.SemaphoreType.DMA((n,))],
    )(*totals)


ADD_ROWS = 128


def _add_sibling(g, r, place, *, name):
    lead, (half, cols) = g.shape[:-2], r.shape[-2:]
    tr = min(ADD_ROWS, half)
    nb = half // tr
    zeros = (0,) * len(lead)

    def body(place_ref, g_ref, r_ref, o_ref, ob_ref):
        s = g_ref[...] + r_ref[...]
        o_ref[...] = s
        ob_ref[...] = s.astype(BF)

    spec = pl.BlockSpec(lead + (tr, cols), lambda i, p: zeros + (i, 0))
    return pl.pallas_call(
        body, name=name,
        grid_spec=pltpu.PrefetchScalarGridSpec(
            num_scalar_prefetch=1, grid=(nb,),
            in_specs=[pl.BlockSpec(lead + (tr, cols), lambda i, p: zeros + (p[1] * nb + i, 0)), spec], out_specs=[spec, spec]),
        out_shape=[jax.ShapeDtypeStruct(r.shape, F32), jax.ShapeDtypeStruct(r.shape, BF)],
        compiler_params=_params("parallel"),
    )(place, g, r)


def _add_chips(own, received, place, *, name, own_slots):
    half, cols = received.shape[-2:]
    tr = min(ADD_ROWS, half)
    nb = half // tr

    def written(k, p):
        return jnp.where(p[0] == k, (k + 1) % N_CHIPS, k)

    def body(place_ref, own_ref, *refs):
        o_ref = refs[N_CHIPS]
        mine = own_ref[0] if own_slots else own_ref[...]
        if own_slots:
            acc = mine
            for k in range(N_CHIPS):
                acc = acc + jnp.where(place_ref[0] == k, 0.0, refs[k][0].astype(F32))
        else:
            terms = [jnp.where(place_ref[0] == k, mine, refs[k][0]) for k in range(N_CHIPS)]
            acc = ((terms[0] + terms[1]) + terms[2]) + terms[3]
        o_ref[...] = acc

    own_spec = (pl.BlockSpec((1, tr, cols), lambda i, p: (p[0], i, 0)) if own_slots
                else pl.BlockSpec((tr, cols), lambda i, p: (i, 0)))
    return pl.pallas_call(
        body, name=name,
        grid_spec=pltpu.PrefetchScalarGridSpec(
            num_scalar_prefetch=1, grid=(nb,),
            in_specs=[own_spec] + [pl.BlockSpec((1, tr, cols), functools.partial(lambda i, p, k: (written(k, p), i, 0), k=k))
                                   for k in range(N_CHIPS)],
            out_specs=pl.BlockSpec((tr, cols), lambda i, p: (p[1] * nb + i, 0))),
        out_shape=jax.ShapeDtypeStruct((2 * half, cols), F32), compiler_params=_params("parallel"),
    )(place, own, *([received] * N_CHIPS))


SHARDED = (("w_in", (D_MODEL, 4616), 1), ("w_branch_sgu", (SGU_W, D_MODEL), 1), ("w_branch_attn", (ATTN_W, D_MODEL), 1),
           ("w_out", (D_MODEL, D_MODEL), 0), ("w_up", (D_MODEL, D_FF), 1), ("w_down", (D_FF, D_MODEL), 0))
SMALL = (("g_mix_pre", (1, D_MODEL)), ("b_forget", (1, N_HEADS)), ("g_sgu", (1, SGU_W)), ("b_sgu", (1, SGU_W)),
         ("w_spatial", (N_GROUPS * CHUNK, CHUNK)), ("b_spatial", (N_GROUPS, CHUNK)), ("g_mix_post", (1, D_MODEL)),
         ("g_ffn_pre", (1, D_MODEL)), ("g_ffn_post", (1, D_MODEL)))
SMALL_ALIGN = 2 * ADD_ROWS


def _shard_shape(shape, axis):
    return tuple(s // N_CHIPS if a == axis else s for a, s in enumerate(shape))


def _slots_to_full(slots, axis):
    return slots.reshape(-1, slots.shape[2]) if axis == 0 else slots.transpose(1, 0, 2).reshape(slots.shape[1], -1)


def _full_to_slots(full, axis):
    if axis == 0:
        return full.reshape(N_CHIPS, -1, full.shape[1])
    return full.reshape(full.shape[0], N_CHIPS, -1).transpose(1, 0, 2)


def _small_rows(shape):
    return -(-(shape[0] * shape[1]) // (8 * LANES)) * 8


def _pack_small(values):
    parts = []
    for name, shape in SMALL:
        flat = values[name].reshape(-1)
        n = _small_rows(shape)
        parts.append(jnp.pad(flat, (0, n * LANES - flat.shape[0])).reshape(n, LANES))
    rows = sum(p.shape[0] for p in parts)
    pad = -(-rows // SMALL_ALIGN) * SMALL_ALIGN - rows
    return jnp.concatenate(parts + [jnp.zeros((pad, LANES), F32)], axis=0)


def _unpack_small(packed):
    out, row = {}, 0
    for name, shape in SMALL:
        n = _small_rows(shape)
        out[name] = packed[row:row + n].reshape(-1)[:shape[0] * shape[1]].reshape(shape)
        row += n
    return out


IN_Z, IN_Q, IN_K, IN_V, IN_F, IN_G, IN_END = 0, 1024, 1536, 2048, 2560, 2568, 4616


def _local_step(x, target, w, small):
    w_in = w["w_in"]
    w_z, w_qkv, w_g = w_in[:, IN_Z:IN_Q], w_in[:, IN_Q:IN_F], w_in[:, IN_G:IN_END]
    w_q, w_k, w_v = w_in[:, IN_Q:IN_K], w_in[:, IN_K:IN_V], w_in[:, IN_V:IN_F]
    w_f = jnp.pad(w_in[:, IN_F:IN_G], ((0, 0), (0, LANES - N_HEADS)))
    b_forget = jnp.pad(small["b_forget"], ((0, 0), (0, LANES - N_HEADS)))
    causal = jnp.tril(jnp.ones((CHUNK, CHUNK), bool))
    ws = jnp.where(causal[None], small["w_spatial"].reshape(N_GROUPS, CHUNK, CHUNK), 0.0).astype(BF)
    ws_t = ws.transpose(0, 2, 1)
    bias_plane = jnp.repeat(small["b_spatial"].T, HEAD_DIM, axis=1)

    xn = _rms_fwd(x, small["g_mix_pre"])
    z = _matmul([(xn, w_z)], nt=False, out_dtypes=[F32], name="proj_z")
    qkv = _matmul([(xn, w_qkv)], nt=False, out_dtypes=[BF], name="proj_qkv")
    gl = _matmul([(xn, w_g)], nt=False, out_dtypes=[F32], name="proj_gate")
    fl = _matmul([(xn, w_f)], nt=False, out_dtypes=[F32], name="proj_forget")
    ysgu = _sgu_fwd(z, small["g_sgu"], small["b_sgu"], ws, bias_plane)
    qf, kl, vl, tile_stats = _attn_prep(qkv, fl, b_forget)
    first_key_tile, last_query_tile, bounded = _attn_ranges(tile_stats)
    yattn, yattn_f, ql = _attn_fwd(qf, kl, vl, first_key_tile, bounded)
    a, b, merged = _branch_merge(ysgu, yattn, w["w_branch_sgu"], w["w_branch_attn"], gl)
    o = _matmul([(merged, w["w_out"])], nt=False, out_dtypes=[F32], name="proj_out")
    h1, xn2 = _mixer_out_fwd(o, x, small["g_mix_post"], small["g_ffn_pre"])

    def relu2(acc):
        r = jnp.maximum(acc, 0.0)
        return r * r, r

    hid, relu = _matmul([(xn2, w["w_up"])], nt=False, out_dtypes=[BF, BF], name="ffn_up", epilogue=relu2)
    dn = _matmul([(hid, w["w_down"])], nt=False, out_dtypes=[F32], name="ffn_down")
    sq, dy, ddn, dg_ffn_post = _loss_head(dn, h1, target, small["g_ffn_post"])

    dup = _matmul([(ddn, w["w_down"])], nt=True, out_dtypes=[BF], name="ffn_down_bwd",
                  epilogue=lambda acc, r: (acc * (2.0 * r.astype(F32)),), extras=[relu])
    dw_down = _matmul_tn(hid, ddn, name="dw_down")
    dxn2 = _matmul([(dup, w["w_up"])], nt=True, out_dtypes=[F32], name="ffn_up_bwd")
    dw_up = _matmul_tn(xn2, dup, name="dw_up", slots=True)
    dh1, do, dg_ffn_pre, dg_mix_post = _mixer_out_bwd(h1, dxn2, dy, o, small["g_ffn_pre"], small["g_mix_post"])

    dmerged = _matmul([(do, w["w_out"])], nt=True, out_dtypes=[F32], name="proj_out_bwd")
    dw_out = _matmul_tn(merged, do, name="dw_out")
    da, db, dgla, dglb = _gate_bwd(dmerged, a, b, gl)
    dysgu = _matmul([(da, w["w_branch_sgu"])], nt=True, out_dtypes=[F32], name="branch_sgu_bwd")
    dyattn = _matmul([(db, w["w_branch_attn"])], nt=True, out_dtypes=[F32], name="branch_attn_bwd")
    dw_bs = _matmul_tn(ysgu, da, name="dw_branch_sgu")
    dw_ba = _matmul_tn(yattn, db, name="dw_branch_attn")
    dz, dws, dbs, dg_sgu, db_sgu = _sgu_bwd(dysgu, z, small["g_sgu"], small["b_sgu"], ws, ws_t, bias_plane)
    dout = _attn_bwd_prep(dyattn, yattn_f)
    dq, dk, dv, ext_q, ext_k = _attn_bwd(kl, vl, ql, dout, last_query_tile)
    dfl, dbf = _forget_bwd(ext_q, ext_k, fl, b_forget)
    dxn = _matmul([(dz, w_z), (dq, w_q), (dk, w_k), (dv, w_v), (dgla, w_g[:, :D_MODEL]), (dglb, w_g[:, D_MODEL:]), (dfl, w_f)],
                  nt=True, out_dtypes=[F32], name="proj_in_bwd")
    dw_in = jnp.concatenate(
        [_matmul_tn(xn, dz, name="dw_in_z"), _matmul_tn(xn, dq, name="dw_in_q"), _matmul_tn(xn, dk, name="dw_in_k"),
         _matmul_tn(xn, dv, name="dw_in_v"), _matmul_tn(xn, dfl, name="dw_in_f")[:, :N_HEADS],
         _matmul_tn(xn, dgla, name="dw_in_ga"), _matmul_tn(xn, dglb, name="dw_in_gb")], axis=1)
    dx, dg_mix_pre = _input_norm_bwd(x, dxn, dh1, small["g_mix_pre"])

    grads = {"w_in": _full_to_slots(dw_in, 1), "w_branch_sgu": _full_to_slots(dw_bs, 1), "w_branch_attn": _full_to_slots(dw_ba, 1),
             "w_out": _full_to_slots(dw_out, 0), "w_up": dw_up, "w_down": _full_to_slots(dw_down, 0)}
    small_grads = {"g_mix_pre": dg_mix_pre, "b_forget": dbf[:, :N_HEADS], "g_sgu": dg_sgu, "b_sgu": db_sgu,
                   "w_spatial": dws.reshape(N_GROUPS * CHUNK, CHUNK), "b_spatial": dbs[:, :N_GROUPS].T,
                   "g_mix_post": dg_mix_post, "g_ffn_pre": dg_ffn_pre, "g_ffn_post": dg_ffn_post}
    return sq, dx, grads, small_grads


NAMES = ("g_mix_pre", "w_in", "b_forget", "g_sgu", "b_sgu", "w_spatial", "b_spatial", "w_branch_sgu", "w_branch_attn",
         "w_out", "g_mix_post", "g_ffn_pre", "w_up", "w_down", "g_ffn_post")


def kernel(x, g_mix_pre, w_in, b_forget, g_sgu, b_sgu, w_spatial, b_spatial, w_branch_sgu, w_branch_attn, w_out, g_mix_post, g_ffn_pre, w_up, w_down, g_ffn_post, loss_target, m_g_mix_pre, m_w_in, m_b_forget, m_g_sgu, m_b_sgu, m_w_spatial, m_b_spatial, m_w_branch_sgu, m_w_branch_attn, m_w_out, m_g_mix_post, m_g_ffn_pre, m_w_up, m_w_down, m_g_ffn_post, v_g_mix_pre, v_w_in, v_b_forget, v_g_sgu, v_b_sgu, v_w_spatial, v_b_spatial, v_w_branch_sgu, v_w_branch_attn, v_w_out, v_g_mix_post, v_g_ffn_pre, v_w_up, v_w_down, v_g_ffn_post):
    weights = dict(zip(NAMES, (g_mix_pre, w_in, b_forget, g_sgu, b_sgu, w_spatial, b_spatial, w_branch_sgu, w_branch_attn,
                               w_out, g_mix_post, g_ffn_pre, w_up, w_down, g_ffn_post), strict=True))
    first = dict(zip(NAMES, (m_g_mix_pre, m_w_in, m_b_forget, m_g_sgu, m_b_sgu, m_w_spatial, m_b_spatial, m_w_branch_sgu,
                             m_w_branch_attn, m_w_out, m_g_mix_post, m_g_ffn_pre, m_w_up, m_w_down, m_g_ffn_post), strict=True))
    second = dict(zip(NAMES, (v_g_mix_pre, v_w_in, v_b_forget, v_g_sgu, v_b_sgu, v_w_spatial, v_b_spatial, v_w_branch_sgu,
                              v_w_branch_attn, v_w_out, v_g_mix_post, v_g_ffn_pre, v_w_up, v_w_down, v_g_ffn_post), strict=True))
    shard_shapes = {name: _shard_shape(shape, axis) for name, shape, axis in SHARDED}
    small_shapes = dict(SMALL)
    view = lambda name, a: a.reshape(shard_shapes.get(name) or small_shapes[name])

    chip = 2 * lax.axis_index("x") + lax.axis_index("y")
    place = jnp.stack([chip, lax.axis_index("c")]).astype(jnp.int32)

    shards = [view(name, weights[name]).astype(BF) for name, _, _ in SHARDED]
    slot = jnp.arange(N_CHIPS)[:, None, None]
    full = {name: _slots_to_full(jnp.where(slot == chip, mine[None], got), axis)
            for (name, _, axis), mine, got in zip(SHARDED, shards, _gather_weights(shards), strict=True)}
    small = {name: view(name, weights[name]) for name, _ in SMALL}

    sq, dx, grads, small_grads = _local_step(x[0], loss_target[0], full, small)
    loss = lax.psum(0.5 * jnp.sum(sq) / D_MODEL, ("x", "y", "c"))

    mine = [grads[name] for name, _, _ in SHARDED] + [_pack_small(small_grads)]
    theirs, theirs_small = _exchange_halves(mine[:-1], mine[-1])
    sums = [_add_sibling(g, r, place, name="add_sibling_" + tag)
            for g, r, tag in zip(mine, list(theirs) + [theirs_small], [name for name, _, _ in SHARDED] + ["small"], strict=True)]
    received, received_small = _scatter_to_owners([b for _, b in sums[:-1]], sums[-1][0])
    totals = [_add_chips(s, r, place, name="add_chips_" + name, own_slots=True)
              for (s, _), r, (name, _, _) in zip(sums[:-1], received, SHARDED, strict=True)]
    totals.append(_add_chips(sums[-1][0], received_small, place, name="add_chips_small", own_slots=False))
    joined = _join_halves(totals)
    grad = {**{name: g for (name, _, _), g in zip(SHARDED, joined[:-1], strict=True)}, **_unpack_small(joined[-1])}

    delta, new_m, new_v = {}, {}, {}
    for name in NAMES:
        delta[name], new_m[name], new_v[name] = _adamw(
            view(name, weights[name]), grad[name], view(name, first[name]), view(name, second[name]), name="adamw_" + name)

    like = lambda d: [d[name].reshape(weights[name].shape) for name in NAMES]
    return (loss, dx[None], *like(grad), *like(delta), *like(new_m), *like(new_v))
```

```python
import functools

import jax
import jax.numpy as jnp
from jax import lax
from jax.experimental import pallas as pl
from jax.experimental.pallas import tpu as pltpu

F32 = jnp.float32
BF = jnp.bfloat16
MESH = pl.DeviceIdType.MESH

D_MODEL = 1024
N_HEADS = 8
HEAD_DIM = 64
ATTN_W = N_HEADS * HEAD_DIM
SGU_W = 512
N_GROUPS = 8
CHUNK = 128
D_FF = 4096
EPS = 1e-6
Q_SCALE = HEAD_DIM ** -0.5
N_CHIPS = 4
LANES = 128

ADAM_LR = 0.001
ADAM_B1 = 0.9
ADAM_B2 = 0.999
ADAM_EPS = 1e-08
ADAM_WD = 0.01
ADAM_STEP = 10

VMEM_LIMIT = 48 * 1024 * 1024
ATTN_BWD_VMEM = 58 * 1024 * 1024
NEG = -1e30

LANE_ROWSUM = HEAD_DIM
LANE_COLSUM = HEAD_DIM + 3


def _params(*sem):
    return pltpu.CompilerParams(dimension_semantics=sem, vmem_limit_bytes=VMEM_LIMIT)


def _dot(a, b):
    return jnp.dot(a, b, preferred_element_type=F32)


def _dot_nt(a, b):
    return lax.dot_general(a, b, (((1,), (1,)), ((), ())), preferred_element_type=F32)


def _dot_tn(a, b):
    return lax.dot_general(a, b, (((0,), (0,)), ((), ())), preferred_element_type=F32)


def _split3(c):
    hi = c.astype(BF).astype(F32)
    r = c - hi
    mid = r.astype(BF).astype(F32)
    lo = (r - mid).astype(BF).astype(F32)
    return hi, mid, lo


def _gelu(x):
    k = 0.7978845608028654
    return 0.5 * x * (1.0 + jnp.tanh(k * (x + 0.044715 * (x * x * x))))


def _gelu_grad(x):
    k = 0.7978845608028654
    x2 = x * x
    t = jnp.tanh(k * (x + 0.044715 * (x2 * x)))
    return 0.5 * (1.0 + t) + 0.5 * x * (1.0 - t * t) * (k * (1.0 + 3.0 * 0.044715 * x2))


def _rms_bwd(a, g, dy):
    r = lax.rsqrt(jnp.mean(a * a, axis=-1, keepdims=True) + EPS)
    n = a * r
    dn = dy * g
    da = r * (dn - n * jnp.mean(dn * n, axis=-1, keepdims=True))
    return da, dy * n


MM_ROWS = 1024
MM_COLS = 512


def _matmul(pairs, *, nt, out_dtypes, name, tm=MM_ROWS, tn=MM_COLS, epilogue=None, extras=()):
    n_pairs = len(pairs)
    n_extra = len(extras)
    M = pairs[0][0].shape[0]
    N = pairs[0][1].shape[0] if nt else pairs[0][1].shape[1]
    tm, tn = min(tm, M), min(tn, N)
    assert M % tm == 0 and N % tn == 0

    def body(*refs):
        acc = None
        for p in range(n_pairs):
            a_ref, b_ref = refs[2 * p], refs[2 * p + 1]
            d = _dot_nt(a_ref[...], b_ref[...]) if nt else _dot(a_ref[...], b_ref[...])
            acc = d if acc is None else acc + d
        e_refs = refs[2 * n_pairs:2 * n_pairs + n_extra]
        o_refs = refs[2 * n_pairs + n_extra:]
        outs = (acc,) if epilogue is None else epilogue(acc, *[e[...] for e in e_refs])
        for o_ref, o in zip(o_refs, outs, strict=True):
            o_ref[...] = o.astype(o_ref.dtype)

    in_specs, args = [], []
    for a, b in pairs:
        K = a.shape[1]
        in_specs.append(pl.BlockSpec((tm, K), lambda i, j: (i, 0)))
        in_specs.append(pl.BlockSpec((tn, K), lambda i, j: (j, 0)) if nt else pl.BlockSpec((K, tn), lambda i, j: (0, j)))
        args += [a, b]
    for e in extras:
        in_specs.append(pl.BlockSpec((tm, tn), lambda i, j: (i, j)))
        args.append(e)
    outs = pl.pallas_call(
        body, name=name, grid=(M // tm, N // tn), in_specs=in_specs,
        out_specs=[pl.BlockSpec((tm, tn), lambda i, j: (i, j)) for _ in out_dtypes],
        out_shape=[jax.ShapeDtypeStruct((M, N), dt) for dt in out_dtypes],
        compiler_params=_params("parallel", "parallel"),
    )(*args)
    return outs if len(outs) > 1 else outs[0]


def _matmul_tn(a, b, *, name, tm=1024, tn=1024, tk=512, slots=False):
    T, K1 = a.shape
    N = b.shape[1]
    tm, tn, tk = min(tm, K1), min(tn, N // N_CHIPS if slots else N), min(tk, T)
    assert K1 % tm == 0 and (N // N_CHIPS if slots else N) % tn == 0 and T % tk == 0
    per_slot = N // N_CHIPS // tn

    def body(a_ref, b_ref, o_ref):
        @pl.when(pl.program_id(2) == 0)
        def _():
            o_ref[...] = jnp.zeros_like(o_ref)

        o_ref[...] += _dot_tn(a_ref[...], b_ref[...])

    if slots:
        out_spec = pl.BlockSpec((None, tm, tn), lambda i, j, k: (j // per_slot, i, j % per_slot))
        out_shape = jax.ShapeDtypeStruct((N_CHIPS, K1, N // N_CHIPS), F32)
    else:
        out_spec = pl.BlockSpec((tm, tn), lambda i, j, k: (i, j))
        out_shape = jax.ShapeDtypeStruct((K1, N), F32)
    return pl.pallas_call(
        body, name=name, grid=(K1 // tm, N // tn, T // tk),
        in_specs=[pl.BlockSpec((tk, tm), lambda i, j, k: (k, i)), pl.BlockSpec((tk, tn), lambda i, j, k: (k, j))],
        out_specs=out_spec, out_shape=out_shape,
        compiler_params=_params("parallel", "parallel", "arbitrary"),
    )(a, b)


def _branch_merge(ysgu, yattn, w_bs, w_ba, gl, *, tm=MM_ROWS, tn=MM_COLS):
    T = ysgu.shape[0]
    tm = min(tm, T)
    nj = D_MODEL // tn

    def body(ys_ref, ya_ref, wbs_ref, wba_ref, gla_ref, glb_ref, a_ref, b_ref, m_ref):
        a = _dot(ys_ref[...], wbs_ref[...])
        b = _dot(ya_ref[...], wba_ref[...])
        a_ref[...] = a
        b_ref[...] = b
        m_ref[...] = (jax.nn.sigmoid(gla_ref[...]) * a + jax.nn.sigmoid(glb_ref[...]) * b).astype(BF)

    return pl.pallas_call(
        body, name="branch_merge", grid=(T // tm, nj),
        in_specs=[
            pl.BlockSpec((tm, SGU_W), lambda i, j: (i, 0)),
            pl.BlockSpec((tm, ATTN_W), lambda i, j: (i, 0)),
            pl.BlockSpec((SGU_W, tn), lambda i, j: (0, j)),
            pl.BlockSpec((ATTN_W, tn), lambda i, j: (0, j)),
            pl.BlockSpec((tm, tn), lambda i, j: (i, j)),
            pl.BlockSpec((tm, tn), lambda i, j: (i, j + nj)),
        ],
        out_specs=[pl.BlockSpec((tm, tn), lambda i, j: (i, j))] * 3,
        out_shape=[jax.ShapeDtypeStruct((T, D_MODEL), F32), jax.ShapeDtypeStruct((T, D_MODEL), F32),
                   jax.ShapeDtypeStruct((T, D_MODEL), BF)],
        compiler_params=_params("parallel", "parallel"),
    )(ysgu, yattn, w_bs, w_ba, gl, gl)


def _row_spec(tr, width):
    return pl.BlockSpec((tr, width), lambda i: (i, 0))


def _vec_spec(width):
    return pl.BlockSpec((1, width), lambda i: (0, 0))


def _rms_fwd(x, g, *, tr=256):
    T = x.shape[0]
    tr = min(tr, T)

    def body(x_ref, g_ref, o_ref):
        xv = x_ref[...]
        r = lax.rsqrt(jnp.mean(xv * xv, axis=-1, keepdims=True) + EPS)
        o_ref[...] = ((xv * r) * g_ref[...]).astype(BF)

    return pl.pallas_call(
        body, name="rms_fwd", grid=(T // tr,),
        in_specs=[_row_spec(tr, D_MODEL), _vec_spec(D_MODEL)], out_specs=_row_spec(tr, D_MODEL),
        out_shape=jax.ShapeDtypeStruct((T, D_MODEL), BF), compiler_params=_params("parallel"),
    )(x, g)


def _mixer_out_fwd(o, x, g_post, g_pre, *, tr=256):
    T = x.shape[0]
    tr = min(tr, T)

    def body(o_ref, x_ref, gpost_ref, gpre_ref, h1_ref, xn2_ref):
        ov = o_ref[...]
        r = lax.rsqrt(jnp.mean(ov * ov, axis=-1, keepdims=True) + EPS)
        h1 = x_ref[...] + (ov * r) * gpost_ref[...]
        h1_ref[...] = h1
        r2 = lax.rsqrt(jnp.mean(h1 * h1, axis=-1, keepdims=True) + EPS)
        xn2_ref[...] = ((h1 * r2) * gpre_ref[...]).astype(BF)

    return pl.pallas_call(
        body, name="mixer_out_fwd", grid=(T // tr,),
        in_specs=[_row_spec(tr, D_MODEL), _row_spec(tr, D_MODEL), _vec_spec(D_MODEL), _vec_spec(D_MODEL)],
        out_specs=[_row_spec(tr, D_MODEL), _row_spec(tr, D_MODEL)],
        out_shape=[jax.ShapeDtypeStruct((T, D_MODEL), F32), jax.ShapeDtypeStruct((T, D_MODEL), BF)],
        compiler_params=_params("parallel"),
    )(o, x, g_post, g_pre)


def _loss_head(dn, h1, target, g_post, *, tr=256):
    T = dn.shape[0]
    tr = min(tr, T)

    def body(dn_ref, h1_ref, t_ref, g_ref, sq_ref, dy_ref, ddn_ref, dg_ref):
        @pl.when(pl.program_id(0) == 0)
        def _():
            sq_ref[...] = jnp.zeros_like(sq_ref)
            dg_ref[...] = jnp.zeros_like(dg_ref)

        a = dn_ref[...]
        g = g_ref[...]
        r = lax.rsqrt(jnp.mean(a * a, axis=-1, keepdims=True) + EPS)
        err = h1_ref[...] + (a * r) * g - t_ref[...]
        sq_ref[...] += jnp.sum(err * err, axis=0, keepdims=True)
        dy = err * (1.0 / D_MODEL)
        dy_ref[...] = dy
        da, dgp = _rms_bwd(a, g, dy)
        ddn_ref[...] = da.astype(BF)
        dg_ref[...] += jnp.sum(dgp, axis=0, keepdims=True)

    return pl.pallas_call(
        body, name="loss_head", grid=(T // tr,),
        in_specs=[_row_spec(tr, D_MODEL)] * 3 + [_vec_spec(D_MODEL)],
        out_specs=[_vec_spec(D_MODEL), _row_spec(tr, D_MODEL), _row_spec(tr, D_MODEL), _vec_spec(D_MODEL)],
        out_shape=[jax.ShapeDtypeStruct((1, D_MODEL), F32), jax.ShapeDtypeStruct((T, D_MODEL), F32),
                   jax.ShapeDtypeStruct((T, D_MODEL), BF), jax.ShapeDtypeStruct((1, D_MODEL), F32)],
        compiler_params=_params("arbitrary"),
    )(dn, h1, target, g_post)


def _mixer_out_bwd(h1, dxn2, dy, o, g_pre, g_post, *, tr=256):
    T = h1.shape[0]
    tr = min(tr, T)

    def body(h1_ref, dxn2_ref, dy_ref, o_ref, gpre_ref, gpost_ref, dh1_ref, do_ref, dgpre_ref, dgpost_ref):
        @pl.when(pl.program_id(0) == 0)
        def _():
            dgpre_ref[...] = jnp.zeros_like(dgpre_ref)
            dgpost_ref[...] = jnp.zeros_like(dgpost_ref)

        da, dgp = _rms_bwd(h1_ref[...], gpre_ref[...], dxn2_ref[...])
        dh1 = dy_ref[...] + da
        dh1_ref[...] = dh1
        dgpre_ref[...] += jnp.sum(dgp, axis=0, keepdims=True)
        do, dgp2 = _rms_bwd(o_ref[...], gpost_ref[...], dh1)
        do_ref[...] = do.astype(BF)
        dgpost_ref[...] += jnp.sum(dgp2, axis=0, keepdims=True)

    return pl.pallas_call(
        body, name="mixer_out_bwd", grid=(T // tr,),
        in_specs=[_row_spec(tr, D_MODEL)] * 4 + [_vec_spec(D_MODEL)] * 2,
        out_specs=[_row_spec(tr, D_MODEL), _row_spec(tr, D_MODEL), _vec_spec(D_MODEL), _vec_spec(D_MODEL)],
        out_shape=[jax.ShapeDtypeStruct((T, D_MODEL), F32), jax.ShapeDtypeStruct((T, D_MODEL), BF),
                   jax.ShapeDtypeStruct((1, D_MODEL), F32), jax.ShapeDtypeStruct((1, D_MODEL), F32)],
        compiler_params=_params("arbitrary"),
    )(h1, dxn2, dy, o, g_pre, g_post)


def _input_norm_bwd(x, dxn, dh1, g, *, tr=256):
    T = x.shape[0]
    tr = min(tr, T)

    def body(x_ref, dxn_ref, dh1_ref, g_ref, dx_ref, dg_ref):
        @pl.when(pl.program_id(0) == 0)
        def _():
            dg_ref[...] = jnp.zeros_like(dg_ref)

        da, dgp = _rms_bwd(x_ref[...], g_ref[...], dxn_ref[...])
        dx_ref[...] = dh1_ref[...] + da
        dg_ref[...] += jnp.sum(dgp, axis=0, keepdims=True)

    return pl.pallas_call(
        body, name="input_norm_bwd", grid=(T // tr,),
        in_specs=[_row_spec(tr, D_MODEL)] * 3 + [_vec_spec(D_MODEL)],
        out_specs=[_row_spec(tr, D_MODEL), _vec_spec(D_MODEL)],
        out_shape=[jax.ShapeDtypeStruct((T, D_MODEL), F32), jax.ShapeDtypeStruct((1, D_MODEL), F32)],
        compiler_params=_params("arbitrary"),
    )(x, dxn, dh1, g)


def _gate_bwd(dm, a, b, gl, *, tr=256):
    T = dm.shape[0]
    tr = min(tr, T)

    def body(dm_ref, a_ref, b_ref, gla_ref, glb_ref, da_ref, db_ref, dgla_ref, dglb_ref):
        dmv = dm_ref[...]
        ga = jax.nn.sigmoid(gla_ref[...])
        gb = jax.nn.sigmoid(glb_ref[...])
        da_ref[...] = (dmv * ga).astype(BF)
        db_ref[...] = (dmv * gb).astype(BF)
        dgla_ref[...] = (dmv * a_ref[...] * (ga * (1.0 - ga))).astype(BF)
        dglb_ref[...] = (dmv * b_ref[...] * (gb * (1.0 - gb))).astype(BF)

    spec = _row_spec(tr, D_MODEL)
    spec_b = pl.BlockSpec((tr, D_MODEL), lambda i: (i, 1))
    da, db, dgla, dglb = pl.pallas_call(
        body, name="gate_bwd", grid=(T // tr,),
        in_specs=[spec, spec, spec, spec, spec_b], out_specs=[spec] * 4,
        out_shape=[jax.ShapeDtypeStruct((T, D_MODEL), BF)] * 4, compiler_params=_params("parallel"),
    )(dm, a, b, gl, gl)
    return da, db, dgla, dglb


def _sgu_norm(z_tile, g, b):
    gz = _gelu(z_tile)
    u, vv = gz[:, :SGU_W], gz[:, SGU_W:]
    xc = vv - jnp.mean(vv, axis=-1, keepdims=True)
    rstd = lax.rsqrt(jnp.mean(xc * xc, axis=-1, keepdims=True) + EPS)
    xhat = xc * rstd
    return u, xhat, rstd, xhat * g + b


def _sgu_mix(w_ref, v_bf, first_half):
    parts = []
    for p in range(N_GROUPS // 2):
        vp = v_bf[:, p * LANES:(p + 1) * LANES]
        parts.append(jnp.where(first_half, _dot(w_ref[2 * p], vp), _dot(w_ref[2 * p + 1], vp)))
    return jnp.concatenate(parts, axis=1)


def _sgu_fwd(z, g_sgu, b_sgu, ws, bias_plane, *, tm=512):
    T = z.shape[0]
    tm = min(tm, T)

    def body(z_ref, g_ref, b_ref, ws_ref, bp_ref, y_ref):
        u, _, _, vn = _sgu_norm(z_ref[...], g_ref[...], b_ref[...])
        vn_bf = vn.astype(BF)
        first_half = lax.broadcasted_iota(jnp.int32, (CHUNK, LANES), 1) < HEAD_DIM
        for c in range(tm // CHUNK):
            rows = slice(c * CHUNK, (c + 1) * CHUNK)
            s = _sgu_mix(ws_ref, vn_bf[rows, :], first_half) + bp_ref[...]
            y_ref[rows, :] = (u[rows, :] * s).astype(BF)

    return pl.pallas_call(
        body, name="sgu_fwd", grid=(T // tm,),
        in_specs=[_row_spec(tm, 2 * SGU_W), _vec_spec(SGU_W), _vec_spec(SGU_W),
                  pl.BlockSpec((N_GROUPS, CHUNK, CHUNK), lambda i: (0, 0, 0)),
                  pl.BlockSpec((CHUNK, SGU_W), lambda i: (0, 0))],
        out_specs=_row_spec(tm, SGU_W), out_shape=jax.ShapeDtypeStruct((T, SGU_W), BF),
        compiler_params=_params("parallel"),
    )(z, g_sgu, b_sgu, ws, bias_plane)


def _sgu_bwd(dy, z, g_sgu, b_sgu, ws, ws_t, bias_plane, *, tm=512):
    T = z.shape[0]
    tm = min(tm, T)
    n_steps = T // tm

    def body(dy_ref, z_ref, g_ref, b_ref, ws_ref, wst_ref, bp_ref, dz_ref, dws_ref, dbs_ref, dg_ref, db_ref, dbp_ref):
        step = pl.program_id(0)

        @pl.when(step == 0)
        def _():
            dws_ref[...] = jnp.zeros_like(dws_ref)
            dg_ref[...] = jnp.zeros_like(dg_ref)
            db_ref[...] = jnp.zeros_like(db_ref)
            dbp_ref[...] = jnp.zeros_like(dbp_ref)

        g = g_ref[...]
        zt = z_ref[...]
        u, xhat, rstd, vn = _sgu_norm(zt, g, b_ref[...])
        vn_bf = vn.astype(BF)
        first_half = lax.broadcasted_iota(jnp.int32, (CHUNK, LANES), 1) < HEAD_DIM
        dyv = dy_ref[...]
        dg_acc = jnp.zeros((1, SGU_W), F32)
        db_acc = jnp.zeros((1, SGU_W), F32)
        for c in range(tm // CHUNK):
            rows = slice(c * CHUNK, (c + 1) * CHUNK)
            v_c = vn_bf[rows, :]
            s = _sgu_mix(ws_ref, v_c, first_half) + bp_ref[...]
            dy_c = dyv[rows, :]
            du = dy_c * s
            dsv = dy_c * u[rows, :]
            dbp_ref[...] += dsv
            ds_bf = dsv.astype(BF)
            zero = jnp.zeros((CHUNK, LANES), BF)
            for p in range(N_GROUPS // 2):
                dsp = ds_bf[:, p * LANES:(p + 1) * LANES]
                vp = v_c[:, p * LANES:(p + 1) * LANES]
                dws_ref[2 * p] += _dot_nt(jnp.where(first_half, dsp, zero), vp)
                dws_ref[2 * p + 1] += _dot_nt(jnp.where(first_half, zero, dsp), vp)
            dvn = _sgu_mix(wst_ref, ds_bf, first_half)
            xh = xhat[rows, :]
            dxh = dvn * g
            dvv = rstd[rows, :] * (dxh - jnp.mean(dxh, axis=-1, keepdims=True)
                                   - xh * jnp.mean(dxh * xh, axis=-1, keepdims=True))
            dg_acc += jnp.sum(dvn * xh, axis=0, keepdims=True)
            db_acc += jnp.sum(dvn, axis=0, keepdims=True)
            dgz = jnp.concatenate([du, dvv], axis=1)
            dz_ref[rows, :] = (dgz * _gelu_grad(zt[rows, :])).astype(BF)
        dg_ref[...] += dg_acc
        db_ref[...] += db_acc

        @pl.when(step == n_steps - 1)
        def _():
            r = lax.broadcasted_iota(jnp.int32, (CHUNK, CHUNK), 0)
            cidx = lax.broadcasted_iota(jnp.int32, (CHUNK, CHUNK), 1)
            causal = (cidx <= r).astype(F32)
            for gi in range(N_GROUPS):
                dws_ref[gi] = dws_ref[gi] * causal
            lane = lax.broadcasted_iota(jnp.int32, (CHUNK, LANES), 1)
            out = jnp.zeros((CHUNK, LANES), F32)
            dbp = dbp_ref[...]
            for gi in range(N_GROUPS):
                col = jnp.sum(dbp[:, gi * HEAD_DIM:(gi + 1) * HEAD_DIM], axis=1, keepdims=True)
                out = jnp.where(lane == gi, col, out)
            dbs_ref[...] = out

    w_spec = pl.BlockSpec((N_GROUPS, CHUNK, CHUNK), lambda i: (0, 0, 0))
    plane = pl.BlockSpec((CHUNK, SGU_W), lambda i: (0, 0))
    return pl.pallas_call(
        body, name="sgu_bwd", grid=(n_steps,),
        in_specs=[_row_spec(tm, SGU_W), _row_spec(tm, 2 * SGU_W), _vec_spec(SGU_W), _vec_spec(SGU_W), w_spec, w_spec, plane],
        out_specs=[_row_spec(tm, 2 * SGU_W), w_spec, pl.BlockSpec((CHUNK, LANES), lambda i: (0, 0)),
                   _vec_spec(SGU_W), _vec_spec(SGU_W)],
        out_shape=[jax.ShapeDtypeStruct((T, 2 * SGU_W), BF), jax.ShapeDtypeStruct((N_GROUPS, CHUNK, CHUNK), F32),
                   jax.ShapeDtypeStruct((CHUNK, LANES), F32), jax.ShapeDtypeStruct((1, SGU_W), F32),
                   jax.ShapeDtypeStruct((1, SGU_W), F32)],
        scratch_shapes=[pltpu.VMEM((CHUNK, SGU_W), F32)],
        compiler_params=_params("arbitrary"),
    )(dy, z, g_sgu, b_sgu, ws, ws_t, bias_plane)


def _tri(n, upper):
    r = lax.broadcasted_iota(jnp.int32, (n, n), 0)
    c = lax.broadcasted_iota(jnp.int32, (n, n), 1)
    return ((c >= r) if upper else (c <= r)).astype(BF)


def _scan_dot(tri, x):
    hi, mid, lo = _split3(x)
    return (_dot(tri, hi.astype(BF)) + _dot(tri, mid.astype(BF))) + _dot(tri, lo.astype(BF))


def _with_lanes(base, lane, start, cols):
    out = base
    for k, col in enumerate(cols):
        if col is not None:
            out = jnp.where(lane == start + k, col, out)
    return out


def _logit_bound(q_norm, k_norm):
    return NORM_SLACK * q_norm * k_norm + 1.0


ATTN_TILE = 512
SKIP_BELOW = -110.0
NORM_SLACK = 1.001
BOUNDED_GAP = 60.0


def _attn_prep(qkv, fl, b_forget, *, tp=ATTN_TILE):
    T = qkv.shape[0]
    tp = min(tp, T)

    def body(qkv_ref, fl_ref, bf_ref, qf_ref, kl_ref, vl_ref, st_ref, carry_ref, kmax_ref):
        @pl.when(pl.program_id(0) == 0)
        def _():
            carry_ref[...] = jnp.zeros_like(carry_ref)
            kmax_ref[...] = jnp.zeros_like(kmax_ref)

        x = fl_ref[...] + bf_ref[...]
        logf = jnp.minimum(x, 0.0) - jnp.log(1.0 + jnp.exp(-jnp.abs(x)))
        cum = _scan_dot(_tri(tp, upper=False), logf) + carry_ref[...]
        carry_ref[...] = cum[tp - 1:tp, :]
        lane = lax.broadcasted_iota(jnp.int32, (tp, HEAD_DIM), 1)
        ones3 = jnp.where(lane < 3, 1.0, 0.0)
        qkvv = qkv_ref[...]
        st_row = lax.broadcasted_iota(jnp.int32, (N_HEADS, LANES), 0)
        st_lane = lax.broadcasted_iota(jnp.int32, (N_HEADS, LANES), 1)
        stats = jnp.zeros((N_HEADS, LANES), F32)
        kmax_lane = lax.broadcasted_iota(jnp.int32, (1, LANES), 1)
        for h in range(N_HEADS):
            ch = cum[:, h:h + 1]
            c3 = _split3(ch)
            qh = qkvv[:, h * HEAD_DIM:(h + 1) * HEAD_DIM].astype(F32) * Q_SCALE
            kh = qkvv[:, ATTN_W + h * HEAD_DIM:ATTN_W + (h + 1) * HEAD_DIM].astype(F32)
            vh = qkvv[:, 2 * ATTN_W + h * HEAD_DIM:2 * ATTN_W + (h + 1) * HEAD_DIM].astype(F32)
            q_norm = jnp.sqrt(jnp.sum(qh * qh, axis=1, keepdims=True))
            qn = jnp.max(q_norm, axis=0, keepdims=True)
            kn = jnp.sqrt(jnp.max(jnp.sum(kh * kh, axis=1, keepdims=True), axis=0, keepdims=True))
            k_seen = jnp.maximum(kmax_ref[:, h:h + 1], kn)
            kmax_ref[...] = jnp.where(kmax_lane == h, k_seen, kmax_ref[...])
            bound3 = _split3(-_logit_bound(q_norm, k_seen))
            ext_q = _with_lanes(jnp.where((lane >= 3) & (lane < 6), 1.0, 0.0), lane, 0, list(c3) + [None] * 3 + list(bound3))
            ext_k = _with_lanes(jnp.where((lane < 3) | ((lane >= 6) & (lane < 9)), 1.0, 0.0), lane, 3, [-c for c in c3])
            qf_ref[h] = jnp.concatenate([qh, ext_q], axis=1).astype(BF)
            kl_ref[h] = jnp.concatenate([kh, ext_k], axis=1).astype(BF)
            vl_ref[h] = jnp.concatenate([vh, ones3], axis=1).astype(BF)
            tile_stats = (qn, kn, jnp.max(ch, axis=0, keepdims=True), jnp.min(ch, axis=0, keepdims=True), k_seen)
            for k, val in enumerate(tile_stats):
                stats = jnp.where((st_row == h) & (st_lane == k), val, stats)
        st_ref[0] = stats

    head_spec = pl.BlockSpec((N_HEADS, tp, LANES), lambda i: (0, i, 0))
    return pl.pallas_call(
        body, name="attn_prep", grid=(T // tp,),
        in_specs=[_row_spec(tp, 3 * ATTN_W), _row_spec(tp, LANES), _vec_spec(LANES)],
        out_specs=[head_spec] * 3 + [pl.BlockSpec((1, N_HEADS, LANES), lambda i: (i, 0, 0))],
        out_shape=[jax.ShapeDtypeStruct((N_HEADS, T, LANES), BF)] * 3 + [jax.ShapeDtypeStruct((T // tp, N_HEADS, LANES), F32)],
        scratch_shapes=[pltpu.VMEM((1, LANES), F32), pltpu.VMEM((1, LANES), F32)], compiler_params=_params("arbitrary"),
    )(qkv, fl, b_forget)


def _attn_ranges(stats):
    qn, kn, cmax, cmin, k_seen = (stats[:, :, k].T for k in range(5))
    n = qn.shape[1]
    bounded = (2.0 * _logit_bound(qn, k_seen) <= BOUNDED_GAP).reshape(N_HEADS // 2, 2, n).all(axis=1)
    reach = NORM_SLACK * qn * (jnp.max(kn, axis=1, keepdims=True) + kn) + cmax
    i = jnp.arange(n)[None, :, None]
    j = jnp.arange(n)[None, None, :]
    need = ((reach[:, :, None] - cmin[:, None, :] >= SKIP_BELOW) | (i == j)) & (j <= i)
    first = jnp.min(jnp.where(need, j, n), axis=2).reshape(N_HEADS // 2, 2, n).min(axis=1)
    last = jnp.max(jnp.where(need, i, -1), axis=1).reshape(N_HEADS // 2, 2, n).max(axis=1)
    return first.reshape(-1).astype(F32), last.reshape(-1).astype(F32), bounded.reshape(-1).astype(F32)


def _pair_block(t):
    return pl.BlockSpec((2, t, LANES), lambda p, i, *_: (p, i, 0))


def _pair_full(T):
    return pl.BlockSpec((2, T, LANES), lambda p, i, *_: (p, 0, 0))


def _packed_block(t):
    return pl.BlockSpec((t, LANES), lambda p, i, *_: (i, p))


def _causal(t, keys_in_rows=False):
    r = lax.broadcasted_iota(jnp.int32, (t, t), 0)
    c = lax.broadcasted_iota(jnp.int32, (t, t), 1)
    return (r <= c) if keys_in_rows else (c <= r)


def _tile_rows(j, t):
    return pl.ds(pl.multiple_of(j * t, t), t)


def _attn_call(body, name, tile_scalars, operands, in_specs, out_specs, out_shape, scratch_shapes, n_tiles):
    return pl.pallas_call(
        body, name=name,
        grid_spec=pltpu.PrefetchScalarGridSpec(
            num_scalar_prefetch=len(tile_scalars), grid=(N_HEADS // 2, n_tiles), in_specs=in_specs, out_specs=out_specs,
            scratch_shapes=scratch_shapes),
        out_shape=out_shape, compiler_params=_params("arbitrary", "arbitrary"),
    )(*tile_scalars, *operands)


def _attn_fwd(qf, kl, vl, first, bounded, shards, *, tq=ATTN_TILE):
    T = qf.shape[1]
    tq = min(tq, T)
    n = T // tq
    n_steps = (N_HEADS // 2) * n
    k = len(shards)

    def body(first_ref, bounded_ref, qf_ref, kl_ref, vl_ref, *refs):
        w_refs, (o_ref, of_ref, ql_ref), g_refs = refs[:k], refs[k:k + 3], refs[k + 3:2 * k + 3]
        m_ref, acc_ref, send_sems, recv_sems = refs[2 * k + 3:]
        i = pl.program_id(1)
        tile = pl.program_id(0) * n + i
        gather_start, gather_forward, gather_finish = _gather_phases(w_refs, g_refs, send_sems, recv_sems)
        pl.when(tile == 0)(gather_start)
        pl.when(tile == (3 * n_steps) // 4)(gather_forward)
        start = first_ref[tile].astype(jnp.int32)
        is_bounded = bounded_ref[tile] > 0.5
        acc_ref[...] = jnp.zeros_like(acc_ref)
        diagonal = _tile_rows(i, tq)
        causal = _causal(tq)

        def logits(hh, rows):
            return _dot_nt(qf_ref[hh], kl_ref[hh, rows, :])

        @pl.when(is_bounded)
        def _():
            m_ref[...] = jnp.zeros_like(m_ref)

            def update(hh, s, rows):
                acc_ref[hh] += _dot(jnp.exp(s).astype(BF), vl_ref[hh, rows, :])

            def step(j, carry):
                for hh in range(2):
                    update(hh, logits(hh, _tile_rows(j, tq)), _tile_rows(j, tq))
                return carry

            lax.fori_loop(start, i, step, 0)
            for hh in range(2):
                update(hh, jnp.where(causal, logits(hh, diagonal), NEG), diagonal)

        @pl.when(jnp.logical_not(is_bounded))
        def _():
            m_ref[...] = jnp.full_like(m_ref, NEG)

            def update(hh, s, rows):
                m_old = m_ref[hh]
                m_new = jnp.maximum(m_old, jnp.max(s, axis=1, keepdims=True))
                p = jnp.exp(s - m_new)
                acc_ref[hh] = jnp.exp(m_old - m_new) * acc_ref[hh] + _dot(p.astype(BF), vl_ref[hh, rows, :])
                m_ref[hh] = m_new

            def step(j, carry):
                for hh in range(2):
                    update(hh, logits(hh, _tile_rows(j, tq)), _tile_rows(j, tq))
                return carry

            lax.fori_loop(start, i, step, 0)
            for hh in range(2):
                update(hh, jnp.where(causal, logits(hh, diagonal), NEG), diagonal)

        lane = lax.broadcasted_iota(jnp.int32, (tq, LANES), 1)
        outs = []
        for hh in range(2):
            q = qf_ref[hh].astype(F32)
            acc = acc_ref[hh]
            l = acc[:, HEAD_DIM:HEAD_DIM + 1]
            outs.append(acc[:, :HEAD_DIM] / l)
            at = HEAD_DIM + 6
            neg_bound = (q[:, at:at + 1] + q[:, at + 1:at + 2]) + q[:, at + 2:at + 3]
            ql_ref[hh] = _with_lanes(q, lane, at, _split3(neg_bound - (m_ref[hh] + jnp.log(l)))).astype(BF)
        o = jnp.concatenate(outs, axis=1)
        o_ref[...] = o.astype(BF)
        of_ref[...] = o
        pl.when(tile == n_steps - 1)(gather_finish)

    outs = _attn_call(
        body, "attn_fwd", (first, bounded), (qf, kl, vl, *shards),
        [_pair_block(tq), _pair_full(T), _pair_full(T)] + [HBM] * k,
        [_packed_block(tq), _packed_block(tq), _pair_block(tq)] + [HBM] * k,
        [jax.ShapeDtypeStruct((T, ATTN_W), BF), jax.ShapeDtypeStruct((T, ATTN_W), F32),
         jax.ShapeDtypeStruct((N_HEADS, T, LANES), BF)] + _gathered_shapes(shards),
        [pltpu.VMEM((2, tq, 1), F32), pltpu.VMEM((2, tq, LANES), F32)] + _gather_semaphores(k), n)
    return outs[0], outs[1], outs[2], outs[3:]


def _attn_bwd_prep(dya, of, *, tr=256):
    T = dya.shape[0]
    tr = min(tr, T)

    def body(d_ref, o_ref, do_ref):
        lane = lax.broadcasted_iota(jnp.int32, (tr, HEAD_DIM), 1)
        dv, ov = d_ref[...], o_ref[...]
        for h in range(N_HEADS):
            d = dv[:, h * HEAD_DIM:(h + 1) * HEAD_DIM]
            delta = jnp.sum(d * ov[:, h * HEAD_DIM:(h + 1) * HEAD_DIM], axis=1, keepdims=True)
            ext = _with_lanes(jnp.zeros((tr, HEAD_DIM), F32), lane, 0, _split3(-delta))
            do_ref[h] = jnp.concatenate([d, ext], axis=1).astype(BF)

    return pl.pallas_call(
        body, name="attn_bwd_prep", grid=(T // tr,),
        in_specs=[_row_spec(tr, ATTN_W), _row_spec(tr, ATTN_W)],
        out_specs=pl.BlockSpec((N_HEADS, tr, LANES), lambda i: (0, i, 0)),
        out_shape=jax.ShapeDtypeStruct((N_HEADS, T, LANES), BF), compiler_params=_params("parallel"),
    )(dya, of)


def _attn_bwd(kl, vl, ql, do, last, chip_sums, *, tk=ATTN_TILE):
    T = ql.shape[1]
    tk = min(tk, T)
    n = T // tk
    n_steps = (N_HEADS // 2) * n
    m = len(chip_sums)

    def body(last_ref, kl_ref, vl_ref, ql_ref, do_ref, *refs):
        b_refs, (dq_ref, dk_ref, dv_ref, extq_ref, extk_ref), r_refs = refs[:m], refs[m:m + 5], refs[m + 5:2 * m + 5]
        dq_acc, dk_acc, dv_acc, send_sems, recv_sems = refs[2 * m + 5:]
        j = pl.program_id(1)
        tile = pl.program_id(0) * n + j
        scatter_start, scatter_finish = _scatter_phases(b_refs, r_refs, send_sems, recv_sems)
        pl.when(tile == 0)(scatter_start)

        @pl.when(j == 0)
        def _():
            dq_acc[...] = jnp.zeros_like(dq_acc)

        dk_acc[...] = jnp.zeros_like(dk_acc)
        dv_acc[...] = jnp.zeros_like(dv_acc)

        def block(hh, rows, mask):
            qi, di, k = ql_ref[hh, rows, :], do_ref[hh, rows, :], kl_ref[hh]
            p_t = jnp.exp(_dot_nt(k, qi))
            if mask is not None:
                p_t = jnp.where(mask, p_t, 0.0)
            ds_t = (p_t * _dot_nt(vl_ref[hh], di)).astype(BF)
            dk_acc[hh] += _dot(ds_t, qi)
            dv_acc[hh] += _dot(p_t.astype(BF), di)
            dq_acc[hh, rows, :] += _dot_tn(ds_t, k)

        causal_t = _causal(tk, keys_in_rows=True)
        for hh in range(2):
            block(hh, _tile_rows(j, tk), causal_t)

        def step(i, carry):
            for hh in range(2):
                block(hh, _tile_rows(i, tk), None)
            return carry

        lax.fori_loop(j + 1, last_ref[pl.program_id(0) * n + j].astype(jnp.int32) + 1, step, 0)
        dk_ref[...] = jnp.concatenate([dk_acc[hh][:, :HEAD_DIM] for hh in range(2)], axis=1).astype(BF)
        dv_ref[...] = jnp.concatenate([dv_acc[hh][:, :HEAD_DIM] for hh in range(2)], axis=1).astype(BF)
        extk_ref[...] = jnp.concatenate([dk_acc[hh][:, HEAD_DIM:] for hh in range(2)], axis=1)

        @pl.when(j == n - 1)
        def _():
            dq_ref[...] = jnp.concatenate([dq_acc[hh][:, :HEAD_DIM] * Q_SCALE for hh in range(2)], axis=1).astype(BF)
            extq_ref[...] = jnp.concatenate([dq_acc[hh][:, HEAD_DIM:] for hh in range(2)], axis=1)

        pl.when(tile == n_steps - 1)(scatter_finish)

    whole = pl.BlockSpec((T, LANES), lambda p, j, *_: (0, p))
    outs = pl.pallas_call(
        body, name="attn_bwd",
        grid_spec=pltpu.PrefetchScalarGridSpec(
            num_scalar_prefetch=1, grid=(N_HEADS // 2, n),
            in_specs=[_pair_block(tk), _pair_block(tk), _pair_full(T), _pair_full(T)] + [HBM] * m,
            out_specs=[whole, _packed_block(tk), _packed_block(tk), whole, _packed_block(tk)] + [HBM] * m,
            scratch_shapes=[pltpu.VMEM((2, T, LANES), F32), pltpu.VMEM((2, tk, LANES), F32), pltpu.VMEM((2, tk, LANES), F32)]
            + _scatter_semaphores(m)),
        out_shape=[jax.ShapeDtypeStruct((T, ATTN_W), BF)] * 3 + [jax.ShapeDtypeStruct((T, ATTN_W), F32)] * 2
        + _scattered_shapes(chip_sums),
        compiler_params=pltpu.CompilerParams(dimension_semantics=("arbitrary", "arbitrary"), vmem_limit_bytes=ATTN_BWD_VMEM),
    )(last, kl, vl, ql, do, *chip_sums)
    return outs[:5], outs[5:]


def _forget_bwd(ext_q, ext_k, fl, b_forget, *, tp=256):
    T = fl.shape[0]
    tp = min(tp, T)
    n = T // tp

    def body(eq_ref, ek_ref, fl_ref, bf_ref, dfl_ref, dbf_ref, carry_ref):
        @pl.when(pl.program_id(0) == 0)
        def _():
            carry_ref[...] = jnp.zeros_like(carry_ref)
            dbf_ref[...] = jnp.zeros_like(dbf_ref)

        lane = lax.broadcasted_iota(jnp.int32, (tp, LANES), 1)
        eq, ek = eq_ref[...], ek_ref[...]
        cols = [eq[:, h * HEAD_DIM:h * HEAD_DIM + 1] - ek[:, h * HEAD_DIM + 3:h * HEAD_DIM + 4] for h in range(N_HEADS)]
        dcum = _with_lanes(jnp.zeros((tp, LANES), F32), lane, 0, cols)
        suffix = _scan_dot(_tri(tp, upper=True), dcum) + carry_ref[...]
        carry_ref[...] = suffix[0:1, :]
        x = fl_ref[...] + bf_ref[...]
        dfl = jnp.where(lane < N_HEADS, suffix / (1.0 + jnp.exp(x)), 0.0)
        dfl_ref[...] = dfl.astype(BF)
        dbf_ref[...] += jnp.sum(dfl, axis=0, keepdims=True)

    rev = lambda w: pl.BlockSpec((tp, w), lambda i: (n - 1 - i, 0))
    return pl.pallas_call(
        body, name="forget_bwd", grid=(n,),
        in_specs=[rev(ATTN_W), rev(ATTN_W), rev(LANES), _vec_spec(LANES)],
        out_specs=[rev(LANES), _vec_spec(LANES)],
        out_shape=[jax.ShapeDtypeStruct((T, LANES), BF), jax.ShapeDtypeStruct((1, LANES), F32)],
        scratch_shapes=[pltpu.VMEM((1, LANES), F32)], compiler_params=_params("arbitrary"),
    )(ext_q, ext_k, fl, b_forget)


def _adamw(w, g, m, v, *, name, tr=256):
    rows, cols = w.shape
    tr = tr if rows % tr == 0 else rows

    def body(w_ref, g_ref, m_ref, v_ref, d_ref, nm_ref, nv_ref):
        gv = g_ref[...]
        nm = ADAM_B1 * m_ref[...] + (1.0 - ADAM_B1) * gv
        nv = ADAM_B2 * v_ref[...] + (1.0 - ADAM_B2) * (gv * gv)
        m_hat = nm / (1.0 - ADAM_B1 ** ADAM_STEP)
        v_hat = nv / (1.0 - ADAM_B2 ** ADAM_STEP)
        d_ref[...] = -ADAM_LR * (m_hat / (jnp.sqrt(v_hat) + ADAM_EPS) + ADAM_WD * w_ref[...])
        nm_ref[...] = nm
        nv_ref[...] = nv

    spec = pl.BlockSpec((tr, cols), lambda i: (i, 0))
    return pl.pallas_call(
        body, name=name, grid=(rows // tr,), in_specs=[spec] * 4, out_specs=[spec] * 3,
        out_shape=[jax.ShapeDtypeStruct((rows, cols), F32)] * 3, compiler_params=_params("parallel"),
    )(w, g, m, v)


HBM = pl.BlockSpec(memory_space=pltpu.HBM)
BF16_ROWS = 16


def _place():
    x, y, c = lax.axis_index("x"), lax.axis_index("y"), lax.axis_index("c")
    others = [(1 - x, y), (x, 1 - y), (1 - x, 1 - y)]
    return x, y, c, others


def _chip(xy):
    return 2 * xy[0] + xy[1]


def _row_halves(c, rows):
    half = rows // 2
    assert half % BF16_ROWS == 0
    return (pl.ds(pl.multiple_of(c * half, BF16_ROWS), half), pl.ds(pl.multiple_of((1 - c) * half, BF16_ROWS), half))


def _remote(src, dst, send_sems, recv_sems, k, to):
    return pltpu.make_async_remote_copy(src_ref=src, dst_ref=dst, send_sem=send_sems.at[k], recv_sem=recv_sems.at[k],
                                        device_id=to, device_id_type=MESH)


def _gather_weights(shards):
    n = len(shards)

    def body(*refs):
        for phase in _gather_phases(refs[:n], refs[n:2 * n], *refs[2 * n:]):
            phase()

    return pl.pallas_call(
        body, name="gather_weights", in_specs=[HBM] * n, out_specs=[HBM] * n,
        out_shape=_gathered_shapes(shards), scratch_shapes=_gather_semaphores(n),
    )(*shards)


def _gathered_shapes(shards):
    return [jax.ShapeDtypeStruct((N_CHIPS,) + s.shape, s.dtype) for s in shards]


def _gather_semaphores(n):
    return [pltpu.SemaphoreType.DMA((6 * n,)), pltpu.SemaphoreType.DMA((6 * n,))]


def _gather_phases(w_refs, g_refs, send_sems, recv_sems):
    n = len(w_refs)
    x, y, c, others = _place()
    sibling, me = (x, y, 1 - c), _chip((x, y))
    halves = [_row_halves(c, w.shape[0]) for w in w_refs]

    def sent(a, j, o):
        mine, _ = halves[a]
        return _remote(w_refs[a].at[mine, :], g_refs[a].at[me, mine, :], send_sems, recv_sems, 6 * a + j, (*o, c))

    def passed(a, j, o):
        landed = g_refs[a].at[_chip(o), halves[a][0], :]
        return _remote(landed, landed, send_sems, recv_sems, 6 * a + 3 + j, sibling)

    def start():
        for a in range(n):
            for j, o in enumerate(others):
                sent(a, j, o).start()

    def forward():
        for j, o in enumerate(others):
            for a in range(n):
                landed = g_refs[a].at[_chip(o), halves[a][0], :]
                _remote(landed, landed, send_sems, recv_sems, 6 * a + j, (*o, c)).wait_recv()
                passed(a, j, o).start()

    def finish():
        for j, o in enumerate(others):
            for a in range(n):
                landed = g_refs[a].at[_chip(o), halves[a][1], :]
                _remote(landed, landed, send_sems, recv_sems, 6 * a + 3 + j, sibling).wait_recv()
        for a in range(n):
            for j, o in enumerate(others):
                sent(a, j, o).wait_send()
                passed(a, j, o).wait_send()

    return start, forward, finish


def _exchange_halves(arrays, *, name):
    n = len(arrays)

    def body(*refs):
        g_refs, r_refs, (send_sems, recv_sems) = refs[:n], refs[n:2 * n], refs[2 * n:]
        x, y, c, _ = _place()
        copies = []
        for a in range(n):
            _, theirs = _row_halves(c, g_refs[a].shape[-2])
            src = g_refs[a].at[:, theirs, :] if len(g_refs[a].shape) == 3 else g_refs[a].at[theirs, :]
            copies.append(_remote(src, r_refs[a], send_sems, recv_sems, a, (x, y, 1 - c)))
            copies[-1].start()
        for cp in copies:
            cp.wait()

    def half(s):
        return jax.ShapeDtypeStruct(s.shape[:-2] + (s.shape[-2] // 2, s.shape[-1]), F32)

    return pl.pallas_call(
        body, name=name, in_specs=[HBM] * n, out_specs=[HBM] * n, out_shape=[half(g) for g in arrays],
        scratch_shapes=[pltpu.SemaphoreType.DMA((n,)), pltpu.SemaphoreType.DMA((n,))],
    )(*arrays)


def _scatter_to_owners(chip_sums):
    n = len(chip_sums)

    def body(*refs):
        for phase in _scatter_phases(refs[:n], refs[n:2 * n], *refs[2 * n:]):
            phase()

    return pl.pallas_call(
        body, name="scatter_to_owners", in_specs=[HBM] * n, out_specs=[HBM] * n,
        out_shape=_scattered_shapes(chip_sums), scratch_shapes=_scatter_semaphores(n),
    )(*chip_sums)


def _scattered_shapes(chip_sums):
    return [jax.ShapeDtypeStruct(b.shape if b.ndim == 3 else (N_CHIPS,) + b.shape, b.dtype) for b in chip_sums]


def _scatter_semaphores(n):
    return [pltpu.SemaphoreType.DMA((3 * n,)), pltpu.SemaphoreType.DMA((3 * n,))]


def _scatter_phases(b_refs, r_refs, send_sems, recv_sems):
    n = len(b_refs)
    x, y, c, others = _place()
    me = _chip((x, y))

    def sent(a, j, o):
        src = b_refs[a].at[_chip(o)] if len(b_refs[a].shape) == 3 else b_refs[a]
        return _remote(src, r_refs[a].at[me], send_sems, recv_sems, 3 * a + j, (*o, c))

    def start():
        for a in range(n):
            for j, o in enumerate(others):
                sent(a, j, o).start()

    def finish():
        for a in range(n):
            for j, o in enumerate(others):
                landed = r_refs[a].at[_chip(o)]
                _remote(landed, landed, send_sems, recv_sems, 3 * a + j, (*o, c)).wait_recv()
        for a in range(n):
            for j, o in enumerate(others):
                sent(a, j, o).wait_send()

    return start, finish


def _join_halves(totals):
    n = len(totals)

    def body(*refs):
        in_refs, out_refs, (send_sems, recv_sems) = refs[:n], refs[n:2 * n], refs[2 * n:]
        x, y, c, _ = _place()
        copies = []
        for a in range(n):
            mine, _ = _row_halves(c, in_refs[a].shape[0])
            copies.append(_remote(in_refs[a].at[mine, :], out_refs[a].at[mine, :], send_sems, recv_sems, a, (x, y, 1 - c)))
            copies[-1].start()
        for cp in copies:
            cp.wait()

    return pl.pallas_call(
        body, name="join_halves", in_specs=[HBM] * n, out_specs=[HBM] * n,
        out_shape=[jax.ShapeDtypeStruct(t.shape, F32) for t in totals], input_output_aliases={a: a for a in range(n)},
        scratch_shapes=[pltpu.SemaphoreType.DMA((n,)), pltpu.SemaphoreType.DMA((n,))],
    )(*totals)


ADD_ROWS = 128


def _add_sibling(g, r, place, *, name):
    lead, (half, cols) = g.shape[:-2], r.shape[-2:]
    tr = min(ADD_ROWS, half)
    nb = half // tr
    zeros = (0,) * len(lead)

    def body(place_ref, g_ref, r_ref, o_ref, ob_ref):
        s = g_ref[...] + r_ref[...]
        o_ref[...] = s
        ob_ref[...] = s.astype(BF)

    spec = pl.BlockSpec(lead + (tr, cols), lambda i, p: zeros + (i, 0))
    return pl.pallas_call(
        body, name=name,
        grid_spec=pltpu.PrefetchScalarGridSpec(
            num_scalar_prefetch=1, grid=(nb,),
            in_specs=[pl.BlockSpec(lead + (tr, cols), lambda i, p: zeros + (p[1] * nb + i, 0)), spec], out_specs=[spec, spec]),
        out_shape=[jax.ShapeDtypeStruct(r.shape, F32), jax.ShapeDtypeStruct(r.shape, BF)],
        compiler_params=_params("parallel"),
    )(place, g, r)


def _add_chips(own, received, place, *, name, own_slots):
    half, cols = received.shape[-2:]
    tr = min(ADD_ROWS, half)
    nb = half // tr

    def written(k, p):
        return jnp.where(p[0] == k, (k + 1) % N_CHIPS, k)

    def body(place_ref, own_ref, *refs):
        o_ref = refs[N_CHIPS]
        mine = own_ref[0] if own_slots else own_ref[...]
        if own_slots:
            acc = mine
            for k in range(N_CHIPS):
                acc = acc + jnp.where(place_ref[0] == k, 0.0, refs[k][0].astype(F32))
        else:
            terms = [jnp.where(place_ref[0] == k, mine, refs[k][0]) for k in range(N_CHIPS)]
            acc = ((terms[0] + terms[1]) + terms[2]) + terms[3]
        o_ref[...] = acc

    own_spec = (pl.BlockSpec((1, tr, cols), lambda i, p: (p[0], i, 0)) if own_slots
                else pl.BlockSpec((tr, cols), lambda i, p: (i, 0)))
    return pl.pallas_call(
        body, name=name,
        grid_spec=pltpu.PrefetchScalarGridSpec(
            num_scalar_prefetch=1, grid=(nb,),
            in_specs=[own_spec] + [pl.BlockSpec((1, tr, cols), functools.partial(lambda i, p, k: (written(k, p), i, 0), k=k))
                                   for k in range(N_CHIPS)],
            out_specs=pl.BlockSpec((tr, cols), lambda i, p: (p[1] * nb + i, 0))),
        out_shape=jax.ShapeDtypeStruct((2 * half, cols), F32), compiler_params=_params("parallel"),
    )(place, own, *([received] * N_CHIPS))


SHARDED = (("w_in", (D_MODEL, 4616), 1), ("w_branch_sgu", (SGU_W, D_MODEL), 1), ("w_branch_attn", (ATTN_W, D_MODEL), 1),
           ("w_out", (D_MODEL, D_MODEL), 0), ("w_up", (D_MODEL, D_FF), 1), ("w_down", (D_FF, D_MODEL), 0))
SMALL = (("g_mix_pre", (1, D_MODEL)), ("b_forget", (1, N_HEADS)), ("g_sgu", (1, SGU_W)), ("b_sgu", (1, SGU_W)),
         ("w_spatial", (N_GROUPS * CHUNK, CHUNK)), ("b_spatial", (N_GROUPS, CHUNK)), ("g_mix_post", (1, D_MODEL)),
         ("g_ffn_pre", (1, D_MODEL)), ("g_ffn_post", (1, D_MODEL)))
SMALL_ALIGN = 2 * ADD_ROWS


def _shard_shape(shape, axis):
    return tuple(s // N_CHIPS if a == axis else s for a, s in enumerate(shape))


def _slots_to_full(slots, axis):
    return slots.reshape(-1, slots.shape[2]) if axis == 0 else slots.transpose(1, 0, 2).reshape(slots.shape[1], -1)


def _full_to_slots(full, axis):
    if axis == 0:
        return full.reshape(N_CHIPS, -1, full.shape[1])
    return full.reshape(full.shape[0], N_CHIPS, -1).transpose(1, 0, 2)


def _small_rows(shape):
    return -(-(shape[0] * shape[1]) // (8 * LANES)) * 8


def _pack_small(values):
    parts = []
    for name, shape in SMALL:
        flat = values[name].reshape(-1)
        n = _small_rows(shape)
        parts.append(jnp.pad(flat, (0, n * LANES - flat.shape[0])).reshape(n, LANES))
    rows = sum(p.shape[0] for p in parts)
    pad = -(-rows // SMALL_ALIGN) * SMALL_ALIGN - rows
    return jnp.concatenate(parts + [jnp.zeros((pad, LANES), F32)], axis=0)


def _unpack_small(packed):
    out, row = {}, 0
    for name, shape in SMALL:
        n = _small_rows(shape)
        out[name] = packed[row:row + n].reshape(-1)[:shape[0] * shape[1]].reshape(shape)
        row += n
    return out


IN_Z, IN_Q, IN_K, IN_V, IN_F, IN_G, IN_END = 0, 1024, 1536, 2048, 2560, 2568, 4616


LATE_WEIGHTS = ("w_branch_sgu", "w_branch_attn", "w_out", "w_up", "w_down")
EARLY_GRADS = ("w_up", "w_down")


def _assemble(name, shard, gathered, chip):
    axis = {n: a for n, _, a in SHARDED}[name]
    slot = jnp.arange(N_CHIPS)[:, None, None]
    return _slots_to_full(jnp.where(slot == chip, shard[None], gathered), axis)


def _local_step(x, target, w_in, shards, small, place):
    w_z, w_qkv, w_g = w_in[:, IN_Z:IN_Q], w_in[:, IN_Q:IN_F], w_in[:, IN_G:IN_END]
    w_q, w_k, w_v = w_in[:, IN_Q:IN_K], w_in[:, IN_K:IN_V], w_in[:, IN_V:IN_F]
    w_f = jnp.pad(w_in[:, IN_F:IN_G], ((0, 0), (0, LANES - N_HEADS)))
    b_forget = jnp.pad(small["b_forget"], ((0, 0), (0, LANES - N_HEADS)))
    causal = jnp.tril(jnp.ones((CHUNK, CHUNK), bool))
    ws = jnp.where(causal[None], small["w_spatial"].reshape(N_GROUPS, CHUNK, CHUNK), 0.0).astype(BF)
    ws_t = ws.transpose(0, 2, 1)
    bias_plane = jnp.repeat(small["b_spatial"].T, HEAD_DIM, axis=1)

    xn = _rms_fwd(x, small["g_mix_pre"])
    z = _matmul([(xn, w_z)], nt=False, out_dtypes=[F32], name="proj_z")
    qkv = _matmul([(xn, w_qkv)], nt=False, out_dtypes=[BF], name="proj_qkv")
    gl = _matmul([(xn, w_g)], nt=False, out_dtypes=[F32], name="proj_gate")
    fl = _matmul([(xn, w_f)], nt=False, out_dtypes=[F32], name="proj_forget")
    ysgu = _sgu_fwd(z, small["g_sgu"], small["b_sgu"], ws, bias_plane)
    qf, kl, vl, tile_stats = _attn_prep(qkv, fl, b_forget)
    first_key_tile, last_query_tile, bounded = _attn_ranges(tile_stats)
    yattn, yattn_f, ql, gathered = _attn_fwd(qf, kl, vl, first_key_tile, bounded, [shards[name] for name in LATE_WEIGHTS])
    w = {name: _assemble(name, shards[name], got, place[0]) for name, got in zip(LATE_WEIGHTS, gathered, strict=True)}
    a, b, merged = _branch_merge(ysgu, yattn, w["w_branch_sgu"], w["w_branch_attn"], gl)
    o = _matmul([(merged, w["w_out"])], nt=False, out_dtypes=[F32], name="proj_out")
    h1, xn2 = _mixer_out_fwd(o, x, small["g_mix_post"], small["g_ffn_pre"])

    def relu2(acc):
        r = jnp.maximum(acc, 0.0)
        return r * r, r

    hid, relu = _matmul([(xn2, w["w_up"])], nt=False, out_dtypes=[BF, BF], name="ffn_up", epilogue=relu2)
    dn = _matmul([(hid, w["w_down"])], nt=False, out_dtypes=[F32], name="ffn_down")
    sq, dy, ddn, dg_ffn_post = _loss_head(dn, h1, target, small["g_ffn_post"])

    dup = _matmul([(ddn, w["w_down"])], nt=True, out_dtypes=[BF], name="ffn_down_bwd",
                  epilogue=lambda acc, r: (acc * (2.0 * r.astype(F32)),), extras=[relu])
    dw_down = _matmul_tn(hid, ddn, name="dw_down")
    dxn2 = _matmul([(dup, w["w_up"])], nt=True, out_dtypes=[F32], name="ffn_up_bwd")
    dw_up = _matmul_tn(xn2, dup, name="dw_up", slots=True)
    early = {"w_up": dw_up, "w_down": _full_to_slots(dw_down, 0)}
    early_theirs = _exchange_halves([early[name] for name in EARLY_GRADS], name="exchange_halves_early")
    early_sums = {name: _add_sibling(early[name], theirs, place, name="add_sibling_" + name)
                  for name, theirs in zip(EARLY_GRADS, early_theirs, strict=True)}
    dh1, do, dg_ffn_pre, dg_mix_post = _mixer_out_bwd(h1, dxn2, dy, o, small["g_ffn_pre"], small["g_mix_post"])

    dmerged = _matmul([(do, w["w_out"])], nt=True, out_dtypes=[F32], name="proj_out_bwd")
    dw_out = _matmul_tn(merged, do, name="dw_out")
    da, db, dgla, dglb = _gate_bwd(dmerged, a, b, gl)
    dysgu = _matmul([(da, w["w_branch_sgu"])], nt=True, out_dtypes=[F32], name="branch_sgu_bwd")
    dyattn = _matmul([(db, w["w_branch_attn"])], nt=True, out_dtypes=[F32], name="branch_attn_bwd")
    dw_bs = _matmul_tn(ysgu, da, name="dw_branch_sgu")
    dw_ba = _matmul_tn(yattn, db, name="dw_branch_attn")
    dz, dws, dbs, dg_sgu, db_sgu = _sgu_bwd(dysgu, z, small["g_sgu"], small["b_sgu"], ws, ws_t, bias_plane)
    dout = _attn_bwd_prep(dyattn, yattn_f)
    (dq, dk, dv, ext_q, ext_k), early_received = _attn_bwd(
        kl, vl, ql, dout, last_query_tile, [early_sums[name][1] for name in EARLY_GRADS])
    dfl, dbf = _forget_bwd(ext_q, ext_k, fl, b_forget)
    dxn = _matmul([(dz, w_z), (dq, w_q), (dk, w_k), (dv, w_v), (dgla, w_g[:, :D_MODEL]), (dglb, w_g[:, D_MODEL:]), (dfl, w_f)],
                  nt=True, out_dtypes=[F32], name="proj_in_bwd")
    dw_in = jnp.concatenate(
        [_matmul_tn(xn, dz, name="dw_in_z"), _matmul_tn(xn, dq, name="dw_in_q"), _matmul_tn(xn, dk, name="dw_in_k"),
         _matmul_tn(xn, dv, name="dw_in_v"), _matmul_tn(xn, dfl, name="dw_in_f")[:, :N_HEADS],
         _matmul_tn(xn, dgla, name="dw_in_ga"), _matmul_tn(xn, dglb, name="dw_in_gb")], axis=1)
    dx, dg_mix_pre = _input_norm_bwd(x, dxn, dh1, small["g_mix_pre"])

    grads = {"w_in": _full_to_slots(dw_in, 1), "w_branch_sgu": _full_to_slots(dw_bs, 1),
             "w_branch_attn": _full_to_slots(dw_ba, 1), "w_out": _full_to_slots(dw_out, 0)}
    early_state = {name: (early_sums[name][0], got) for name, got in zip(EARLY_GRADS, early_received, strict=True)}
    small_grads = {"g_mix_pre": dg_mix_pre, "b_forget": dbf[:, :N_HEADS], "g_sgu": dg_sgu, "b_sgu": db_sgu,
                   "w_spatial": dws.reshape(N_GROUPS * CHUNK, CHUNK), "b_spatial": dbs[:, :N_GROUPS].T,
                   "g_mix_post": dg_mix_post, "g_ffn_pre": dg_ffn_pre, "g_ffn_post": dg_ffn_post}
    return sq, dx, grads, early_state, small_grads


NAMES = ("g_mix_pre", "w_in", "b_forget", "g_sgu", "b_sgu", "w_spatial", "b_spatial", "w_branch_sgu", "w_branch_attn",
         "w_out", "g_mix_post", "g_ffn_pre", "w_up", "w_down", "g_ffn_post")


def kernel(x, g_mix_pre, w_in, b_forget, g_sgu, b_sgu, w_spatial, b_spatial, w_branch_sgu, w_branch_attn, w_out, g_mix_post, g_ffn_pre, w_up, w_down, g_ffn_post, loss_target, m_g_mix_pre, m_w_in, m_b_forget, m_g_sgu, m_b_sgu, m_w_spatial, m_b_spatial, m_w_branch_sgu, m_w_branch_attn, m_w_out, m_g_mix_post, m_g_ffn_pre, m_w_up, m_w_down, m_g_ffn_post, v_g_mix_pre, v_w_in, v_b_forget, v_g_sgu, v_b_sgu, v_w_spatial, v_b_spatial, v_w_branch_sgu, v_w_branch_attn, v_w_out, v_g_mix_post, v_g_ffn_pre, v_w_up, v_w_down, v_g_ffn_post):
    weights = dict(zip(NAMES, (g_mix_pre, w_in, b_forget, g_sgu, b_sgu, w_spatial, b_spatial, w_branch_sgu, w_branch_attn,
                               w_out, g_mix_post, g_ffn_pre, w_up, w_down, g_ffn_post), strict=True))
    first = dict(zip(NAMES, (m_g_mix_pre, m_w_in, m_b_forget, m_g_sgu, m_b_sgu, m_w_spatial, m_b_spatial, m_w_branch_sgu,
                             m_w_branch_attn, m_w_out, m_g_mix_post, m_g_ffn_pre, m_w_up, m_w_down, m_g_ffn_post), strict=True))
    second = dict(zip(NAMES, (v_g_mix_pre, v_w_in, v_b_forget, v_g_sgu, v_b_sgu, v_w_spatial, v_b_spatial, v_w_branch_sgu,
                              v_w_branch_attn, v_w_out, v_g_mix_post, v_g_ffn_pre, v_w_up, v_w_down, v_g_ffn_post), strict=True))
    shard_shapes = {name: _shard_shape(shape, axis) for name, shape, axis in SHARDED}
    small_shapes = dict(SMALL)
    view = lambda name, a: a.reshape(shard_shapes.get(name) or small_shapes[name])

    chip = 2 * lax.axis_index("x") + lax.axis_index("y")
    place = jnp.stack([chip, lax.axis_index("c")]).astype(jnp.int32)

    shards = {name: view(name, weights[name]).astype(BF) for name, _, _ in SHARDED}
    w_in_full = _assemble("w_in", shards["w_in"], _gather_weights([shards["w_in"]])[0], chip)
    small = {name: view(name, weights[name]) for name, _ in SMALL}

    sq, dx, grads, early, small_grads = _local_step(x[0], loss_target[0], w_in_full, shards, small, place)
    loss = lax.psum(0.5 * jnp.sum(sq) / D_MODEL, ("x", "y", "c"))

    late = [name for name, _, _ in SHARDED if name not in early]
    mine = [grads[name] for name in late] + [_pack_small(small_grads)]
    theirs = _exchange_halves(mine, name="exchange_halves")
    sums = [_add_sibling(g, r, place, name="add_sibling_" + tag) for g, r, tag in zip(mine, theirs, late + ["small"], strict=True)]
    received = _scatter_to_owners([b for _, b in sums[:-1]] + [sums[-1][0]])
    totals = {name: _add_chips(s, r, place, name="add_chips_" + name, own_slots=True)
              for name, (s, _), r in zip(late, sums[:-1], received[:-1], strict=True)}
    totals.update({name: _add_chips(s, r, place, name="add_chips_" + name, own_slots=True) for name, (s, r) in early.items()})
    small_total = _add_chips(sums[-1][0], received[-1], place, name="add_chips_small", own_slots=False)
    joined = _join_halves([totals[name] for name, _, _ in SHARDED] + [small_total])
    grad = {**{name: g for (name, _, _), g in zip(SHARDED, joined[:-1], strict=True)}, **_unpack_small(joined[-1])}

    delta, new_m, new_v = {}, {}, {}
    for name in NAMES:
        delta[name], new_m[name], new_v[name] = _adamw(
            view(name, weights[name]), grad[name], view(name, first[name]), view(name, second[name]), name="adamw_" + name)

    like = lambda d: [d[name].reshape(weights[name].shape) for name in NAMES]
    return (loss, dx[None], *like(grad), *like(delta), *like(new_m), *like(new_v))
```

```python
import functools

import jax
import jax.numpy as jnp
from jax import lax
from jax.experimental import pallas as pl
from jax.experimental.pallas import tpu as pltpu

F32 = jnp.float32
BF = jnp.bfloat16
MESH = pl.DeviceIdType.MESH

D_MODEL = 1024
N_HEADS = 8
HEAD_DIM = 64
ATTN_W = N_HEADS * HEAD_DIM
SGU_W = 512
N_GROUPS = 8
CHUNK = 128
D_FF = 4096
EPS = 1e-6
Q_SCALE = HEAD_DIM ** -0.5
N_CHIPS = 4
LANES = 128

ADAM_LR = 0.001
ADAM_B1 = 0.9
ADAM_B2 = 0.999
ADAM_EPS = 1e-08
ADAM_WD = 0.01
ADAM_STEP = 10

VMEM_LIMIT = 48 * 1024 * 1024
ATTN_BWD_VMEM = 58 * 1024 * 1024
NEG = -1e30

LANE_ROWSUM = HEAD_DIM
LANE_COLSUM = HEAD_DIM + 3


def _params(*sem):
    return pltpu.CompilerParams(dimension_semantics=sem, vmem_limit_bytes=VMEM_LIMIT)


def _dot(a, b):
    return jnp.dot(a, b, preferred_element_type=F32)


def _dot_nt(a, b):
    return lax.dot_general(a, b, (((1,), (1,)), ((), ())), preferred_element_type=F32)


def _dot_tn(a, b):
    return lax.dot_general(a, b, (((0,), (0,)), ((), ())), preferred_element_type=F32)


def _split3(c):
    hi = c.astype(BF).astype(F32)
    r = c - hi
    mid = r.astype(BF).astype(F32)
    lo = (r - mid).astype(BF).astype(F32)
    return hi, mid, lo


def _gelu(x):
    k = 0.7978845608028654
    return 0.5 * x * (1.0 + jnp.tanh(k * (x + 0.044715 * (x * x * x))))


def _gelu_grad(x):
    k = 0.7978845608028654
    x2 = x * x
    t = jnp.tanh(k * (x + 0.044715 * (x2 * x)))
    return 0.5 * (1.0 + t) + 0.5 * x * (1.0 - t * t) * (k * (1.0 + 3.0 * 0.044715 * x2))


def _rms_bwd(a, g, dy):
    r = lax.rsqrt(jnp.mean(a * a, axis=-1, keepdims=True) + EPS)
    n = a * r
    dn = dy * g
    da = r * (dn - n * jnp.mean(dn * n, axis=-1, keepdims=True))
    return da, dy * n


MM_ROWS = 1024
MM_COLS = 512


def _matmul(pairs, *, nt, out_dtypes, name, tm=MM_ROWS, tn=MM_COLS, epilogue=None, extras=()):
    n_pairs = len(pairs)
    n_extra = len(extras)
    M = pairs[0][0].shape[0]
    N = pairs[0][1].shape[0] if nt else pairs[0][1].shape[1]
    tm, tn = min(tm, M), min(tn, N)
    assert M % tm == 0 and N % tn == 0

    def body(*refs):
        acc = None
        for p in range(n_pairs):
            a_ref, b_ref = refs[2 * p], refs[2 * p + 1]
            d = _dot_nt(a_ref[...], b_ref[...]) if nt else _dot(a_ref[...], b_ref[...])
            acc = d if acc is None else acc + d
        e_refs = refs[2 * n_pairs:2 * n_pairs + n_extra]
        o_refs = refs[2 * n_pairs + n_extra:]
        outs = (acc,) if epilogue is None else epilogue(acc, *[e[...] for e in e_refs])
        for o_ref, o in zip(o_refs, outs, strict=True):
            o_ref[...] = o.astype(o_ref.dtype)

    in_specs, args = [], []
    for a, b in pairs:
        K = a.shape[1]
        in_specs.append(pl.BlockSpec((tm, K), lambda i, j: (i, 0)))
        in_specs.append(pl.BlockSpec((tn, K), lambda i, j: (j, 0)) if nt else pl.BlockSpec((K, tn), lambda i, j: (0, j)))
        args += [a, b]
    for e in extras:
        in_specs.append(pl.BlockSpec((tm, tn), lambda i, j: (i, j)))
        args.append(e)
    outs = pl.pallas_call(
        body, name=name, grid=(M // tm, N // tn), in_specs=in_specs,
        out_specs=[pl.BlockSpec((tm, tn), lambda i, j: (i, j)) for _ in out_dtypes],
        out_shape=[jax.ShapeDtypeStruct((M, N), dt) for dt in out_dtypes],
        compiler_params=_params("parallel", "parallel"),
    )(*args)
    return outs if len(outs) > 1 else outs[0]


def _matmul_tn(a, b, *, name, tm=1024, tn=1024, tk=512, slots=False):
    T, K1 = a.shape
    N = b.shape[1]
    tm, tn, tk = min(tm, K1), min(tn, N // N_CHIPS if slots else N), min(tk, T)
    assert K1 % tm == 0 and (N // N_CHIPS if slots else N) % tn == 0 and T % tk == 0
    per_slot = N // N_CHIPS // tn

    def body(a_ref, b_ref, o_ref):
        @pl.when(pl.program_id(2) == 0)
        def _():
            o_ref[...] = jnp.zeros_like(o_ref)

        o_ref[...] += _dot_tn(a_ref[...], b_ref[...])

    if slots:
        out_spec = pl.BlockSpec((None, tm, tn), lambda i, j, k: (j // per_slot, i, j % per_slot))
        out_shape = jax.ShapeDtypeStruct((N_CHIPS, K1, N // N_CHIPS), F32)
    else:
        out_spec = pl.BlockSpec((tm, tn), lambda i, j, k: (i, j))
        out_shape = jax.ShapeDtypeStruct((K1, N), F32)
    return pl.pallas_call(
        body, name=name, grid=(K1 // tm, N // tn, T // tk),
        in_specs=[pl.BlockSpec((tk, tm), lambda i, j, k: (k, i)), pl.BlockSpec((tk, tn), lambda i, j, k: (k, j))],
        out_specs=out_spec, out_shape=out_shape,
        compiler_params=_params("parallel", "parallel", "arbitrary"),
    )(a, b)


def _branch_merge(ysgu, yattn, w_bs, w_ba, gl, *, tm=MM_ROWS, tn=MM_COLS):
    T = ysgu.shape[0]
    tm = min(tm, T)
    nj = D_MODEL // tn

    def body(ys_ref, ya_ref, wbs_ref, wba_ref, gla_ref, glb_ref, a_ref, b_ref, m_ref):
        a = _dot(ys_ref[...], wbs_ref[...])
        b = _dot(ya_ref[...], wba_ref[...])
        a_ref[...] = a.astype(BF)
        b_ref[...] = b.astype(BF)
        m_ref[...] = (jax.nn.sigmoid(gla_ref[...].astype(F32)) * a + jax.nn.sigmoid(glb_ref[...].astype(F32)) * b).astype(BF)

    return pl.pallas_call(
        body, name="branch_merge", grid=(T // tm, nj),
        in_specs=[
            pl.BlockSpec((tm, SGU_W), lambda i, j: (i, 0)),
            pl.BlockSpec((tm, ATTN_W), lambda i, j: (i, 0)),
            pl.BlockSpec((SGU_W, tn), lambda i, j: (0, j)),
            pl.BlockSpec((ATTN_W, tn), lambda i, j: (0, j)),
            pl.BlockSpec((tm, tn), lambda i, j: (i, j)),
            pl.BlockSpec((tm, tn), lambda i, j: (i, j + nj)),
        ],
        out_specs=[pl.BlockSpec((tm, tn), lambda i, j: (i, j))] * 3,
        out_shape=[jax.ShapeDtypeStruct((T, D_MODEL), BF)] * 3,
        compiler_params=_params("parallel", "parallel"),
    )(ysgu, yattn, w_bs, w_ba, gl, gl)


def _row_spec(tr, width):
    return pl.BlockSpec((tr, width), lambda i: (i, 0))


def _vec_spec(width):
    return pl.BlockSpec((1, width), lambda i: (0, 0))


def _rms_fwd(x, g, *, tr=256):
    T = x.shape[0]
    tr = min(tr, T)

    def body(x_ref, g_ref, o_ref):
        xv = x_ref[...]
        r = lax.rsqrt(jnp.mean(xv * xv, axis=-1, keepdims=True) + EPS)
        o_ref[...] = ((xv * r) * g_ref[...]).astype(BF)

    return pl.pallas_call(
        body, name="rms_fwd", grid=(T // tr,),
        in_specs=[_row_spec(tr, D_MODEL), _vec_spec(D_MODEL)], out_specs=_row_spec(tr, D_MODEL),
        out_shape=jax.ShapeDtypeStruct((T, D_MODEL), BF), compiler_params=_params("parallel"),
    )(x, g)


def _mixer_out_fwd(o, x, g_post, g_pre, *, tr=256):
    T = x.shape[0]
    tr = min(tr, T)

    def body(o_ref, x_ref, gpost_ref, gpre_ref, h1_ref, xn2_ref):
        ov = o_ref[...]
        r = lax.rsqrt(jnp.mean(ov * ov, axis=-1, keepdims=True) + EPS)
        h1 = x_ref[...] + (ov * r) * gpost_ref[...]
        h1_ref[...] = h1
        r2 = lax.rsqrt(jnp.mean(h1 * h1, axis=-1, keepdims=True) + EPS)
        xn2_ref[...] = ((h1 * r2) * gpre_ref[...]).astype(BF)

    return pl.pallas_call(
        body, name="mixer_out_fwd", grid=(T // tr,),
        in_specs=[_row_spec(tr, D_MODEL), _row_spec(tr, D_MODEL), _vec_spec(D_MODEL), _vec_spec(D_MODEL)],
        out_specs=[_row_spec(tr, D_MODEL), _row_spec(tr, D_MODEL)],
        out_shape=[jax.ShapeDtypeStruct((T, D_MODEL), F32), jax.ShapeDtypeStruct((T, D_MODEL), BF)],
        compiler_params=_params("parallel"),
    )(o, x, g_post, g_pre)


def _loss_head(dn, h1, target, g_post, *, tr=256):
    T = dn.shape[0]
    tr = min(tr, T)

    def body(dn_ref, h1_ref, t_ref, g_ref, sq_ref, dy_ref, ddn_ref, dg_ref):
        @pl.when(pl.program_id(0) == 0)
        def _():
            sq_ref[...] = jnp.zeros_like(sq_ref)
            dg_ref[...] = jnp.zeros_like(dg_ref)

        a = dn_ref[...]
        g = g_ref[...]
        r = lax.rsqrt(jnp.mean(a * a, axis=-1, keepdims=True) + EPS)
        err = h1_ref[...] + (a * r) * g - t_ref[...]
        sq_ref[...] += jnp.sum(err * err, axis=0, keepdims=True)
        dy = err * (1.0 / D_MODEL)
        dy_ref[...] = dy
        da, dgp = _rms_bwd(a, g, dy)
        ddn_ref[...] = da.astype(BF)
        dg_ref[...] += jnp.sum(dgp, axis=0, keepdims=True)

    return pl.pallas_call(
        body, name="loss_head", grid=(T // tr,),
        in_specs=[_row_spec(tr, D_MODEL)] * 3 + [_vec_spec(D_MODEL)],
        out_specs=[_vec_spec(D_MODEL), _row_spec(tr, D_MODEL), _row_spec(tr, D_MODEL), _vec_spec(D_MODEL)],
        out_shape=[jax.ShapeDtypeStruct((1, D_MODEL), F32), jax.ShapeDtypeStruct((T, D_MODEL), F32),
                   jax.ShapeDtypeStruct((T, D_MODEL), BF), jax.ShapeDtypeStruct((1, D_MODEL), F32)],
        compiler_params=_params("arbitrary"),
    )(dn, h1, target, g_post)


def _mixer_out_bwd(h1, dxn2, dy, o, g_pre, g_post, *, tr=256):
    T = h1.shape[0]
    tr = min(tr, T)

    def body(h1_ref, dxn2_ref, dy_ref, o_ref, gpre_ref, gpost_ref, dh1_ref, do_ref, dgpre_ref, dgpost_ref):
        @pl.when(pl.program_id(0) == 0)
        def _():
            dgpre_ref[...] = jnp.zeros_like(dgpre_ref)
            dgpost_ref[...] = jnp.zeros_like(dgpost_ref)

        da, dgp = _rms_bwd(h1_ref[...], gpre_ref[...], dxn2_ref[...])
        dh1 = dy_ref[...] + da
        dh1_ref[...] = dh1
        dgpre_ref[...] += jnp.sum(dgp, axis=0, keepdims=True)
        do, dgp2 = _rms_bwd(o_ref[...], gpost_ref[...], dh1)
        do_ref[...] = do.astype(BF)
        dgpost_ref[...] += jnp.sum(dgp2, axis=0, keepdims=True)

    return pl.pallas_call(
        body, name="mixer_out_bwd", grid=(T // tr,),
        in_specs=[_row_spec(tr, D_MODEL)] * 4 + [_vec_spec(D_MODEL)] * 2,
        out_specs=[_row_spec(tr, D_MODEL), _row_spec(tr, D_MODEL), _vec_spec(D_MODEL), _vec_spec(D_MODEL)],
        out_shape=[jax.ShapeDtypeStruct((T, D_MODEL), F32), jax.ShapeDtypeStruct((T, D_MODEL), BF),
                   jax.ShapeDtypeStruct((1, D_MODEL), F32), jax.ShapeDtypeStruct((1, D_MODEL), F32)],
        compiler_params=_params("arbitrary"),
    )(h1, dxn2, dy, o, g_pre, g_post)


def _input_norm_bwd(x, dxn, dh1, g, *, tr=256):
    T = x.shape[0]
    tr = min(tr, T)

    def body(x_ref, dxn_ref, dh1_ref, g_ref, dx_ref, dg_ref):
        @pl.when(pl.program_id(0) == 0)
        def _():
            dg_ref[...] = jnp.zeros_like(dg_ref)

        da, dgp = _rms_bwd(x_ref[...], g_ref[...], dxn_ref[...])
        dx_ref[...] = dh1_ref[...] + da
        dg_ref[...] += jnp.sum(dgp, axis=0, keepdims=True)

    return pl.pallas_call(
        body, name="input_norm_bwd", grid=(T // tr,),
        in_specs=[_row_spec(tr, D_MODEL)] * 3 + [_vec_spec(D_MODEL)],
        out_specs=[_row_spec(tr, D_MODEL), _vec_spec(D_MODEL)],
        out_shape=[jax.ShapeDtypeStruct((T, D_MODEL), F32), jax.ShapeDtypeStruct((1, D_MODEL), F32)],
        compiler_params=_params("arbitrary"),
    )(x, dxn, dh1, g)


def _gate_bwd(dm, a, b, gl, *, tr=256):
    T = dm.shape[0]
    tr = min(tr, T)

    def body(dm_ref, a_ref, b_ref, gla_ref, glb_ref, da_ref, db_ref, dgla_ref, dglb_ref):
        dmv = dm_ref[...]
        ga = jax.nn.sigmoid(gla_ref[...].astype(F32))
        gb = jax.nn.sigmoid(glb_ref[...].astype(F32))
        da_ref[...] = (dmv * ga).astype(BF)
        db_ref[...] = (dmv * gb).astype(BF)
        dgla_ref[...] = (dmv * a_ref[...].astype(F32) * (ga * (1.0 - ga))).astype(BF)
        dglb_ref[...] = (dmv * b_ref[...].astype(F32) * (gb * (1.0 - gb))).astype(BF)

    spec = _row_spec(tr, D_MODEL)
    spec_b = pl.BlockSpec((tr, D_MODEL), lambda i: (i, 1))
    da, db, dgla, dglb = pl.pallas_call(
        body, name="gate_bwd", grid=(T // tr,),
        in_specs=[spec, spec, spec, spec, spec_b], out_specs=[spec] * 4,
        out_shape=[jax.ShapeDtypeStruct((T, D_MODEL), BF)] * 4, compiler_params=_params("parallel"),
    )(dm, a, b, gl, gl)
    return da, db, dgla, dglb


def _sgu_norm(z_tile, g, b):
    gz = _gelu(z_tile)
    u, vv = gz[:, :SGU_W], gz[:, SGU_W:]
    xc = vv - jnp.mean(vv, axis=-1, keepdims=True)
    rstd = lax.rsqrt(jnp.mean(xc * xc, axis=-1, keepdims=True) + EPS)
    xhat = xc * rstd
    return u, xhat, rstd, xhat * g + b


def _sgu_mix(w_ref, v_bf, first_half):
    parts = []
    for p in range(N_GROUPS // 2):
        vp = v_bf[:, p * LANES:(p + 1) * LANES]
        parts.append(jnp.where(first_half, _dot(w_ref[2 * p], vp), _dot(w_ref[2 * p + 1], vp)))
    return jnp.concatenate(parts, axis=1)


def _sgu_fwd(z, g_sgu, b_sgu, ws, bias_plane, *, tm=512):
    T = z.shape[0]
    tm = min(tm, T)

    def body(z_ref, g_ref, b_ref, ws_ref, bp_ref, y_ref):
        u, _, _, vn = _sgu_norm(z_ref[...], g_ref[...], b_ref[...])
        vn_bf = vn.astype(BF)
        first_half = lax.broadcasted_iota(jnp.int32, (CHUNK, LANES), 1) < HEAD_DIM
        for c in range(tm // CHUNK):
            rows = slice(c * CHUNK, (c + 1) * CHUNK)
            s = _sgu_mix(ws_ref, vn_bf[rows, :], first_half) + bp_ref[...]
            y_ref[rows, :] = (u[rows, :] * s).astype(BF)

    return pl.pallas_call(
        body, name="sgu_fwd", grid=(T // tm,),
        in_specs=[_row_spec(tm, 2 * SGU_W), _vec_spec(SGU_W), _vec_spec(SGU_W),
                  pl.BlockSpec((N_GROUPS, CHUNK, CHUNK), lambda i: (0, 0, 0)),
                  pl.BlockSpec((CHUNK, SGU_W), lambda i: (0, 0))],
        out_specs=_row_spec(tm, SGU_W), out_shape=jax.ShapeDtypeStruct((T, SGU_W), BF),
        compiler_params=_params("parallel"),
    )(z, g_sgu, b_sgu, ws, bias_plane)


def _sgu_bwd(dy, z, g_sgu, b_sgu, ws, ws_t, bias_plane, *, tm=512):
    T = z.shape[0]
    tm = min(tm, T)
    n_steps = T // tm

    def body(dy_ref, z_ref, g_ref, b_ref, ws_ref, wst_ref, bp_ref, dz_ref, dws_ref, dbs_ref, dg_ref, db_ref, dbp_ref):
        step = pl.program_id(0)

        @pl.when(step == 0)
        def _():
            dws_ref[...] = jnp.zeros_like(dws_ref)
            dg_ref[...] = jnp.zeros_like(dg_ref)
            db_ref[...] = jnp.zeros_like(db_ref)
            dbp_ref[...] = jnp.zeros_like(dbp_ref)

        g = g_ref[...]
        zt = z_ref[...]
        u, xhat, rstd, vn = _sgu_norm(zt, g, b_ref[...])
        vn_bf = vn.astype(BF)
        first_half = lax.broadcasted_iota(jnp.int32, (CHUNK, LANES), 1) < HEAD_DIM
        dyv = dy_ref[...]
        dg_acc = jnp.zeros((1, SGU_W), F32)
        db_acc = jnp.zeros((1, SGU_W), F32)
        for c in range(tm // CHUNK):
            rows = slice(c * CHUNK, (c + 1) * CHUNK)
            v_c = vn_bf[rows, :]
            s = _sgu_mix(ws_ref, v_c, first_half) + bp_ref[...]
            dy_c = dyv[rows, :]
            du = dy_c * s
            dsv = dy_c * u[rows, :]
            dbp_ref[...] += dsv
            ds_bf = dsv.astype(BF)
            zero = jnp.zeros((CHUNK, LANES), BF)
            for p in range(N_GROUPS // 2):
                dsp = ds_bf[:, p * LANES:(p + 1) * LANES]
                vp = v_c[:, p * LANES:(p + 1) * LANES]
                dws_ref[2 * p] += _dot_nt(jnp.where(first_half, dsp, zero), vp)
                dws_ref[2 * p + 1] += _dot_nt(jnp.where(first_half, zero, dsp), vp)
            dvn = _sgu_mix(wst_ref, ds_bf, first_half)
            xh = xhat[rows, :]
            dxh = dvn * g
            dvv = rstd[rows, :] * (dxh - jnp.mean(dxh, axis=-1, keepdims=True)
                                   - xh * jnp.mean(dxh * xh, axis=-1, keepdims=True))
            dg_acc += jnp.sum(dvn * xh, axis=0, keepdims=True)
            db_acc += jnp.sum(dvn, axis=0, keepdims=True)
            dgz = jnp.concatenate([du, dvv], axis=1)
            dz_ref[rows, :] = (dgz * _gelu_grad(zt[rows, :])).astype(BF)
        dg_ref[...] += dg_acc
        db_ref[...] += db_acc

        @pl.when(step == n_steps - 1)
        def _():
            r = lax.broadcasted_iota(jnp.int32, (CHUNK, CHUNK), 0)
            cidx = lax.broadcasted_iota(jnp.int32, (CHUNK, CHUNK), 1)
            causal = (cidx <= r).astype(F32)
            for gi in range(N_GROUPS):
                dws_ref[gi] = dws_ref[gi] * causal
            lane = lax.broadcasted_iota(jnp.int32, (CHUNK, LANES), 1)
            out = jnp.zeros((CHUNK, LANES), F32)
            dbp = dbp_ref[...]
            for gi in range(N_GROUPS):
                col = jnp.sum(dbp[:, gi * HEAD_DIM:(gi + 1) * HEAD_DIM], axis=1, keepdims=True)
                out = jnp.where(lane == gi, col, out)
            dbs_ref[...] = out

    w_spec = pl.BlockSpec((N_GROUPS, CHUNK, CHUNK), lambda i: (0, 0, 0))
    plane = pl.BlockSpec((CHUNK, SGU_W), lambda i: (0, 0))
    return pl.pallas_call(
        body, name="sgu_bwd", grid=(n_steps,),
        in_specs=[_row_spec(tm, SGU_W), _row_spec(tm, 2 * SGU_W), _vec_spec(SGU_W), _vec_spec(SGU_W), w_spec, w_spec, plane],
        out_specs=[_row_spec(tm, 2 * SGU_W), w_spec, pl.BlockSpec((CHUNK, LANES), lambda i: (0, 0)),
                   _vec_spec(SGU_W), _vec_spec(SGU_W)],
        out_shape=[jax.ShapeDtypeStruct((T, 2 * SGU_W), BF), jax.ShapeDtypeStruct((N_GROUPS, CHUNK, CHUNK), F32),
                   jax.ShapeDtypeStruct((CHUNK, LANES), F32), jax.ShapeDtypeStruct((1, SGU_W), F32),
                   jax.ShapeDtypeStruct((1, SGU_W), F32)],
        scratch_shapes=[pltpu.VMEM((CHUNK, SGU_W), F32)],
        compiler_params=_params("arbitrary"),
    )(dy, z, g_sgu, b_sgu, ws, ws_t, bias_plane)


def _tri(n, upper):
    r = lax.broadcasted_iota(jnp.int32, (n, n), 0)
    c = lax.broadcasted_iota(jnp.int32, (n, n), 1)
    return ((c >= r) if upper else (c <= r)).astype(BF)


def _scan_dot(tri, x):
    hi, mid, lo = _split3(x)
    return (_dot(tri, hi.astype(BF)) + _dot(tri, mid.astype(BF))) + _dot(tri, lo.astype(BF))


def _with_lanes(base, lane, start, cols):
    out = base
    for k, col in enumerate(cols):
        if col is not None:
            out = jnp.where(lane == start + k, col, out)
    return out


def _logit_bound(q_norm, k_norm):
    return NORM_SLACK * q_norm * k_norm + 1.0


ATTN_TILE = 512
SKIP_BELOW = -110.0
NORM_SLACK = 1.001
BOUNDED_GAP = 60.0


def _attn_prep(qkv, fl, b_forget, *, tp=ATTN_TILE):
    T = qkv.shape[0]
    tp = min(tp, T)

    def body(qkv_ref, fl_ref, bf_ref, qf_ref, kl_ref, vl_ref, st_ref, carry_ref, kmax_ref):
        @pl.when(pl.program_id(0) == 0)
        def _():
            carry_ref[...] = jnp.zeros_like(carry_ref)
            kmax_ref[...] = jnp.zeros_like(kmax_ref)

        x = fl_ref[...] + bf_ref[...]
        logf = jnp.minimum(x, 0.0) - jnp.log(1.0 + jnp.exp(-jnp.abs(x)))
        cum = _scan_dot(_tri(tp, upper=False), logf) + carry_ref[...]
        carry_ref[...] = cum[tp - 1:tp, :]
        lane = lax.broadcasted_iota(jnp.int32, (tp, HEAD_DIM), 1)
        ones3 = jnp.where(lane < 3, 1.0, 0.0)
        qkvv = qkv_ref[...]
        st_row = lax.broadcasted_iota(jnp.int32, (N_HEADS, LANES), 0)
        st_lane = lax.broadcasted_iota(jnp.int32, (N_HEADS, LANES), 1)
        stats = jnp.zeros((N_HEADS, LANES), F32)
        kmax_lane = lax.broadcasted_iota(jnp.int32, (1, LANES), 1)
        for h in range(N_HEADS):
            ch = cum[:, h:h + 1]
            c3 = _split3(ch)
            qh = qkvv[:, h * HEAD_DIM:(h + 1) * HEAD_DIM].astype(F32) * Q_SCALE
            kh = qkvv[:, ATTN_W + h * HEAD_DIM:ATTN_W + (h + 1) * HEAD_DIM].astype(F32)
            vh = qkvv[:, 2 * ATTN_W + h * HEAD_DIM:2 * ATTN_W + (h + 1) * HEAD_DIM].astype(F32)
            q_norm = jnp.sqrt(jnp.sum(qh * qh, axis=1, keepdims=True))
            qn = jnp.max(q_norm, axis=0, keepdims=True)
            kn = jnp.sqrt(jnp.max(jnp.sum(kh * kh, axis=1, keepdims=True), axis=0, keepdims=True))
            k_seen = jnp.maximum(kmax_ref[:, h:h + 1], kn)
            kmax_ref[...] = jnp.where(kmax_lane == h, k_seen, kmax_ref[...])
            bound3 = _split3(-_logit_bound(q_norm, k_seen))
            ext_q = _with_lanes(jnp.where((lane >= 3) & (lane < 6), 1.0, 0.0), lane, 0, list(c3) + [None] * 3 + list(bound3))
            ext_k = _with_lanes(jnp.where((lane < 3) | ((lane >= 6) & (lane < 9)), 1.0, 0.0), lane, 3, [-c for c in c3])
            qf_ref[h] = jnp.concatenate([qh, ext_q], axis=1).astype(BF)
            kl_ref[h] = jnp.concatenate([kh, ext_k], axis=1).astype(BF)
            vl_ref[h] = jnp.concatenate([vh, ones3], axis=1).astype(BF)
            tile_stats = (qn, kn, jnp.max(ch, axis=0, keepdims=True), jnp.min(ch, axis=0, keepdims=True), k_seen)
            for k, val in enumerate(tile_stats):
                stats = jnp.where((st_row == h) & (st_lane == k), val, stats)
        st_ref[0] = stats

    head_spec = pl.BlockSpec((N_HEADS, tp, LANES), lambda i: (0, i, 0))
    return pl.pallas_call(
        body, name="attn_prep", grid=(T // tp,),
        in_specs=[_row_spec(tp, 3 * ATTN_W), _row_spec(tp, LANES), _vec_spec(LANES)],
        out_specs=[head_spec] * 3 + [pl.BlockSpec((1, N_HEADS, LANES), lambda i: (i, 0, 0))],
        out_shape=[jax.ShapeDtypeStruct((N_HEADS, T, LANES), BF)] * 3 + [jax.ShapeDtypeStruct((T // tp, N_HEADS, LANES), F32)],
        scratch_shapes=[pltpu.VMEM((1, LANES), F32), pltpu.VMEM((1, LANES), F32)], compiler_params=_params("arbitrary"),
    )(qkv, fl, b_forget)


def _attn_ranges(stats):
    qn, kn, cmax, cmin, k_seen = (stats[:, :, k].T for k in range(5))
    n = qn.shape[1]
    bounded = (2.0 * _logit_bound(qn, k_seen) <= BOUNDED_GAP).reshape(N_HEADS // 2, 2, n).all(axis=1)
    reach = NORM_SLACK * qn * (jnp.max(kn, axis=1, keepdims=True) + kn) + cmax
    i = jnp.arange(n)[None, :, None]
    j = jnp.arange(n)[None, None, :]
    need = ((reach[:, :, None] - cmin[:, None, :] >= SKIP_BELOW) | (i == j)) & (j <= i)
    first = jnp.min(jnp.where(need, j, n), axis=2).reshape(N_HEADS // 2, 2, n).min(axis=1)
    last = jnp.max(jnp.where(need, i, -1), axis=1).reshape(N_HEADS // 2, 2, n).max(axis=1)
    return first.reshape(-1).astype(F32), last.reshape(-1).astype(F32), bounded.reshape(-1).astype(F32)


def _pair_block(t):
    return pl.BlockSpec((2, t, LANES), lambda p, i, *_: (p, i, 0))


def _pair_full(T):
    return pl.BlockSpec((2, T, LANES), lambda p, i, *_: (p, 0, 0))


def _packed_block(t):
    return pl.BlockSpec((t, LANES), lambda p, i, *_: (i, p))


def _causal(t, keys_in_rows=False):
    r = lax.broadcasted_iota(jnp.int32, (t, t), 0)
    c = lax.broadcasted_iota(jnp.int32, (t, t), 1)
    return (r <= c) if keys_in_rows else (c <= r)


def _tile_rows(j, t):
    return pl.ds(pl.multiple_of(j * t, t), t)


def _attn_call(body, name, tile_scalars, operands, in_specs, out_specs, out_shape, scratch_shapes, n_tiles):
    return pl.pallas_call(
        body, name=name,
        grid_spec=pltpu.PrefetchScalarGridSpec(
            num_scalar_prefetch=len(tile_scalars), grid=(N_HEADS // 2, n_tiles), in_specs=in_specs, out_specs=out_specs,
            scratch_shapes=scratch_shapes),
        out_shape=out_shape, compiler_params=_params("arbitrary", "arbitrary"),
    )(*tile_scalars, *operands)


def _attn_fwd(qf, kl, vl, first, bounded, shards, *, tq=ATTN_TILE):
    T = qf.shape[1]
    tq = min(tq, T)
    n = T // tq
    n_steps = (N_HEADS // 2) * n
    k = len(shards)

    def body(first_ref, bounded_ref, qf_ref, kl_ref, vl_ref, *refs):
        w_refs, (o_ref, of_ref, ql_ref), g_refs = refs[:k], refs[k:k + 3], refs[k + 3:2 * k + 3]
        m_ref, acc_ref, send_sems, recv_sems = refs[2 * k + 3:]
        i = pl.program_id(1)
        tile = pl.program_id(0) * n + i
        gather_start, gather_forward, gather_finish = _gather_phases(w_refs, g_refs, send_sems, recv_sems)
        pl.when(tile == 0)(gather_start)
        pl.when(tile == (3 * n_steps) // 4)(gather_forward)
        start = first_ref[tile].astype(jnp.int32)
        is_bounded = bounded_ref[tile] > 0.5
        acc_ref[...] = jnp.zeros_like(acc_ref)
        diagonal = _tile_rows(i, tq)
        causal = _causal(tq)

        def logits(hh, rows):
            return _dot_nt(qf_ref[hh], kl_ref[hh, rows, :])

        @pl.when(is_bounded)
        def _():
            m_ref[...] = jnp.zeros_like(m_ref)

            def update(hh, s, rows):
                acc_ref[hh] += _dot(jnp.exp(s).astype(BF), vl_ref[hh, rows, :])

            def step(j, carry):
                for hh in range(2):
                    update(hh, logits(hh, _tile_rows(j, tq)), _tile_rows(j, tq))
                return carry

            lax.fori_loop(start, i, step, 0)
            for hh in range(2):
                update(hh, jnp.where(causal, logits(hh, diagonal), NEG), diagonal)

        @pl.when(jnp.logical_not(is_bounded))
        def _():
            m_ref[...] = jnp.full_like(m_ref, NEG)

            def update(hh, s, rows):
                m_old = m_ref[hh]
                m_new = jnp.maximum(m_old, jnp.max(s, axis=1, keepdims=True))
                p = jnp.exp(s - m_new)
                acc_ref[hh] = jnp.exp(m_old - m_new) * acc_ref[hh] + _dot(p.astype(BF), vl_ref[hh, rows, :])
                m_ref[hh] = m_new

            def step(j, carry):
                for hh in range(2):
                    update(hh, logits(hh, _tile_rows(j, tq)), _tile_rows(j, tq))
                return carry

            lax.fori_loop(start, i, step, 0)
            for hh in range(2):
                update(hh, jnp.where(causal, logits(hh, diagonal), NEG), diagonal)

        lane = lax.broadcasted_iota(jnp.int32, (tq, LANES), 1)
        outs = []
        for hh in range(2):
            q = qf_ref[hh].astype(F32)
            acc = acc_ref[hh]
            l = acc[:, HEAD_DIM:HEAD_DIM + 1]
            outs.append(acc[:, :HEAD_DIM] / l)
            at = HEAD_DIM + 6
            neg_bound = (q[:, at:at + 1] + q[:, at + 1:at + 2]) + q[:, at + 2:at + 3]
            ql_ref[hh] = _with_lanes(q, lane, at, _split3(neg_bound - (m_ref[hh] + jnp.log(l)))).astype(BF)
        o = jnp.concatenate(outs, axis=1)
        o_ref[...] = o.astype(BF)
        of_ref[...] = o
        pl.when(tile == n_steps - 1)(gather_finish)

    outs = _attn_call(
        body, "attn_fwd", (first, bounded), (qf, kl, vl, *shards),
        [_pair_block(tq), _pair_full(T), _pair_full(T)] + [HBM] * k,
        [_packed_block(tq), _packed_block(tq), _pair_block(tq)] + [HBM] * k,
        [jax.ShapeDtypeStruct((T, ATTN_W), BF), jax.ShapeDtypeStruct((T, ATTN_W), F32),
         jax.ShapeDtypeStruct((N_HEADS, T, LANES), BF)] + _gathered_shapes(shards),
        [pltpu.VMEM((2, tq, 1), F32), pltpu.VMEM((2, tq, LANES), F32)] + _gather_semaphores(k), n)
    return outs[0], outs[1], outs[2], outs[3:]


def _attn_bwd_prep(dya, of, *, tr=256):
    T = dya.shape[0]
    tr = min(tr, T)

    def body(d_ref, o_ref, do_ref):
        lane = lax.broadcasted_iota(jnp.int32, (tr, HEAD_DIM), 1)
        dv, ov = d_ref[...], o_ref[...]
        for h in range(N_HEADS):
            d = dv[:, h * HEAD_DIM:(h + 1) * HEAD_DIM]
            delta = jnp.sum(d * ov[:, h * HEAD_DIM:(h + 1) * HEAD_DIM], axis=1, keepdims=True)
            ext = _with_lanes(jnp.zeros((tr, HEAD_DIM), F32), lane, 0, _split3(-delta))
            do_ref[h] = jnp.concatenate([d, ext], axis=1).astype(BF)

    return pl.pallas_call(
        body, name="attn_bwd_prep", grid=(T // tr,),
        in_specs=[_row_spec(tr, ATTN_W), _row_spec(tr, ATTN_W)],
        out_specs=pl.BlockSpec((N_HEADS, tr, LANES), lambda i: (0, i, 0)),
        out_shape=jax.ShapeDtypeStruct((N_HEADS, T, LANES), BF), compiler_params=_params("parallel"),
    )(dya, of)


def _attn_bwd(kl, vl, ql, do, last, chip_sums, *, tk=ATTN_TILE):
    T = ql.shape[1]
    tk = min(tk, T)
    n = T // tk
    n_steps = (N_HEADS // 2) * n
    m = len(chip_sums)

    def body(last_ref, kl_ref, vl_ref, ql_ref, do_ref, *refs):
        b_refs, (dq_ref, dk_ref, dv_ref, extq_ref, extk_ref), r_refs = refs[:m], refs[m:m + 5], refs[m + 5:2 * m + 5]
        dq_acc, dk_acc, dv_acc, send_sems, recv_sems = refs[2 * m + 5:]
        j = pl.program_id(1)
        tile = pl.program_id(0) * n + j
        scatter_start, scatter_finish = _scatter_phases(b_refs, r_refs, send_sems, recv_sems)
        pl.when(tile == 0)(scatter_start)

        @pl.when(j == 0)
        def _():
            dq_acc[...] = jnp.zeros_like(dq_acc)

        dk_acc[...] = jnp.zeros_like(dk_acc)
        dv_acc[...] = jnp.zeros_like(dv_acc)

        def block(hh, rows, mask):
            qi, di, k = ql_ref[hh, rows, :], do_ref[hh, rows, :], kl_ref[hh]
            p_t = jnp.exp(_dot_nt(k, qi))
            if mask is not None:
                p_t = jnp.where(mask, p_t, 0.0)
            ds_t = (p_t * _dot_nt(vl_ref[hh], di)).astype(BF)
            dk_acc[hh] += _dot(ds_t, qi)
            dv_acc[hh] += _dot(p_t.astype(BF), di)
            dq_acc[hh, rows, :] += _dot_tn(ds_t, k)

        causal_t = _causal(tk, keys_in_rows=True)
        for hh in range(2):
            block(hh, _tile_rows(j, tk), causal_t)

        def step(i, carry):
            for hh in range(2):
                block(hh, _tile_rows(i, tk), None)
            return carry

        lax.fori_loop(j + 1, last_ref[pl.program_id(0) * n + j].astype(jnp.int32) + 1, step, 0)
        dk_ref[...] = jnp.concatenate([dk_acc[hh][:, :HEAD_DIM] for hh in range(2)], axis=1).astype(BF)
        dv_ref[...] = jnp.concatenate([dv_acc[hh][:, :HEAD_DIM] for hh in range(2)], axis=1).astype(BF)
        extk_ref[...] = jnp.concatenate([dk_acc[hh][:, HEAD_DIM:] for hh in range(2)], axis=1)

        @pl.when(j == n - 1)
        def _():
            dq_ref[...] = jnp.concatenate([dq_acc[hh][:, :HEAD_DIM] * Q_SCALE for hh in range(2)], axis=1).astype(BF)
            extq_ref[...] = jnp.concatenate([dq_acc[hh][:, HEAD_DIM:] for hh in range(2)], axis=1)

        pl.when(tile == n_steps - 1)(scatter_finish)

    whole = pl.BlockSpec((T, LANES), lambda p, j, *_: (0, p))
    outs = pl.pallas_call(
        body, name="attn_bwd",
        grid_spec=pltpu.PrefetchScalarGridSpec(
            num_scalar_prefetch=1, grid=(N_HEADS // 2, n),
            in_specs=[_pair_block(tk), _pair_block(tk), _pair_full(T), _pair_full(T)] + [HBM] * m,
            out_specs=[whole, _packed_block(tk), _packed_block(tk), whole, _packed_block(tk)] + [HBM] * m,
            scratch_shapes=[pltpu.VMEM((2, T, LANES), F32), pltpu.VMEM((2, tk, LANES), F32), pltpu.VMEM((2, tk, LANES), F32)]
            + _scatter_semaphores(m)),
        out_shape=[jax.ShapeDtypeStruct((T, ATTN_W), BF)] * 3 + [jax.ShapeDtypeStruct((T, ATTN_W), F32)] * 2
        + _scattered_shapes(chip_sums),
        compiler_params=pltpu.CompilerParams(dimension_semantics=("arbitrary", "arbitrary"), vmem_limit_bytes=ATTN_BWD_VMEM),
    )(last, kl, vl, ql, do, *chip_sums)
    return outs[:5], outs[5:]


def _forget_bwd(ext_q, ext_k, fl, b_forget, *, tp=256):
    T = fl.shape[0]
    tp = min(tp, T)
    n = T // tp

    def body(eq_ref, ek_ref, fl_ref, bf_ref, dfl_ref, dbf_ref, carry_ref):
        @pl.when(pl.program_id(0) == 0)
        def _():
            carry_ref[...] = jnp.zeros_like(carry_ref)
            dbf_ref[...] = jnp.zeros_like(dbf_ref)

        lane = lax.broadcasted_iota(jnp.int32, (tp, LANES), 1)
        eq, ek = eq_ref[...], ek_ref[...]
        cols = [eq[:, h * HEAD_DIM:h * HEAD_DIM + 1] - ek[:, h * HEAD_DIM + 3:h * HEAD_DIM + 4] for h in range(N_HEADS)]
        dcum = _with_lanes(jnp.zeros((tp, LANES), F32), lane, 0, cols)
        suffix = _scan_dot(_tri(tp, upper=True), dcum) + carry_ref[...]
        carry_ref[...] = suffix[0:1, :]
        x = fl_ref[...] + bf_ref[...]
        dfl = jnp.where(lane < N_HEADS, suffix / (1.0 + jnp.exp(x)), 0.0)
        dfl_ref[...] = dfl.astype(BF)
        dbf_ref[...] += jnp.sum(dfl, axis=0, keepdims=True)

    rev = lambda w: pl.BlockSpec((tp, w), lambda i: (n - 1 - i, 0))
    return pl.pallas_call(
        body, name="forget_bwd", grid=(n,),
        in_specs=[rev(ATTN_W), rev(ATTN_W), rev(LANES), _vec_spec(LANES)],
        out_specs=[rev(LANES), _vec_spec(LANES)],
        out_shape=[jax.ShapeDtypeStruct((T, LANES), BF), jax.ShapeDtypeStruct((1, LANES), F32)],
        scratch_shapes=[pltpu.VMEM((1, LANES), F32)], compiler_params=_params("arbitrary"),
    )(ext_q, ext_k, fl, b_forget)


def _adamw(w, g, m, v, *, name, tr=256):
    rows, cols = w.shape
    tr = tr if rows % tr == 0 else rows

    def body(w_ref, g_ref, m_ref, v_ref, d_ref, nm_ref, nv_ref):
        gv = g_ref[...]
        nm = ADAM_B1 * m_ref[...] + (1.0 - ADAM_B1) * gv
        nv = ADAM_B2 * v_ref[...] + (1.0 - ADAM_B2) * (gv * gv)
        m_hat = nm / (1.0 - ADAM_B1 ** ADAM_STEP)
        v_hat = nv / (1.0 - ADAM_B2 ** ADAM_STEP)
        d_ref[...] = -ADAM_LR * (m_hat / (jnp.sqrt(v_hat) + ADAM_EPS) + ADAM_WD * w_ref[...])
        nm_ref[...] = nm
        nv_ref[...] = nv

    spec = pl.BlockSpec((tr, cols), lambda i: (i, 0))
    return pl.pallas_call(
        body, name=name, grid=(rows // tr,), in_specs=[spec] * 4, out_specs=[spec] * 3,
        out_shape=[jax.ShapeDtypeStruct((rows, cols), F32)] * 3, compiler_params=_params("parallel"),
    )(w, g, m, v)


HBM = pl.BlockSpec(memory_space=pltpu.HBM)
BF16_ROWS = 16


def _place():
    x, y, c = lax.axis_index("x"), lax.axis_index("y"), lax.axis_index("c")
    others = [(1 - x, y), (x, 1 - y), (1 - x, 1 - y)]
    return x, y, c, others


def _chip(xy):
    return 2 * xy[0] + xy[1]


def _row_halves(c, rows):
    half = rows // 2
    assert half % BF16_ROWS == 0
    return (pl.ds(pl.multiple_of(c * half, BF16_ROWS), half), pl.ds(pl.multiple_of((1 - c) * half, BF16_ROWS), half))


def _remote(src, dst, send_sems, recv_sems, k, to):
    return pltpu.make_async_remote_copy(src_ref=src, dst_ref=dst, send_sem=send_sems.at[k], recv_sem=recv_sems.at[k],
                                        device_id=to, device_id_type=MESH)


def _gather_weights(shards):
    n = len(shards)

    def body(*refs):
        for phase in _gather_phases(refs[:n], refs[n:2 * n], *refs[2 * n:]):
            phase()

    return pl.pallas_call(
        body, name="gather_weights", in_specs=[HBM] * n, out_specs=[HBM] * n,
        out_shape=_gathered_shapes(shards), scratch_shapes=_gather_semaphores(n),
    )(*shards)


def _gathered_shapes(shards):
    return [jax.ShapeDtypeStruct((N_CHIPS,) + s.shape, s.dtype) for s in shards]


def _gather_semaphores(n):
    return [pltpu.SemaphoreType.DMA((6 * n,)), pltpu.SemaphoreType.DMA((6 * n,))]


def _gather_phases(w_refs, g_refs, send_sems, recv_sems):
    n = len(w_refs)
    x, y, c, others = _place()
    sibling, me = (x, y, 1 - c), _chip((x, y))
    halves = [_row_halves(c, w.shape[0]) for w in w_refs]

    def sent(a, j, o):
        mine, _ = halves[a]
        return _remote(w_refs[a].at[mine, :], g_refs[a].at[me, mine, :], send_sems, recv_sems, 6 * a + j, (*o, c))

    def passed(a, j, o):
        landed = g_refs[a].at[_chip(o), halves[a][0], :]
        return _remote(landed, landed, send_sems, recv_sems, 6 * a + 3 + j, sibling)

    def start():
        for a in range(n):
            for j, o in enumerate(others):
                sent(a, j, o).start()

    def forward():
        for j, o in enumerate(others):
            for a in range(n):
                landed = g_refs[a].at[_chip(o), halves[a][0], :]
                _remote(landed, landed, send_sems, recv_sems, 6 * a + j, (*o, c)).wait_recv()
                passed(a, j, o).start()

    def finish():
        for j, o in enumerate(others):
            for a in range(n):
                landed = g_refs[a].at[_chip(o), halves[a][1], :]
                _remote(landed, landed, send_sems, recv_sems, 6 * a + 3 + j, sibling).wait_recv()
        for a in range(n):
            for j, o in enumerate(others):
                sent(a, j, o).wait_send()
                passed(a, j, o).wait_send()

    return start, forward, finish


def _exchange_halves(arrays, *, name):
    n = len(arrays)

    def body(*refs):
        g_refs, r_refs, (send_sems, recv_sems) = refs[:n], refs[n:2 * n], refs[2 * n:]
        x, y, c, _ = _place()
        copies = []
        for a in range(n):
            _, theirs = _row_halves(c, g_refs[a].shape[-2])
            src = g_refs[a].at[:, theirs, :] if len(g_refs[a].shape) == 3 else g_refs[a].at[theirs, :]
            copies.append(_remote(src, r_refs[a], send_sems, recv_sems, a, (x, y, 1 - c)))
            copies[-1].start()
        for cp in copies:
            cp.wait()

    def half(s):
        return jax.ShapeDtypeStruct(s.shape[:-2] + (s.shape[-2] // 2, s.shape[-1]), F32)

    return pl.pallas_call(
        body, name=name, in_specs=[HBM] * n, out_specs=[HBM] * n, out_shape=[half(g) for g in arrays],
        scratch_shapes=[pltpu.SemaphoreType.DMA((n,)), pltpu.SemaphoreType.DMA((n,))],
    )(*arrays)


def _scatter_to_owners(chip_sums):
    n = len(chip_sums)

    def body(*refs):
        for phase in _scatter_phases(refs[:n], refs[n:2 * n], *refs[2 * n:]):
            phase()

    return pl.pallas_call(
        body, name="scatter_to_owners", in_specs=[HBM] * n, out_specs=[HBM] * n,
        out_shape=_scattered_shapes(chip_sums), scratch_shapes=_scatter_semaphores(n),
    )(*chip_sums)


def _scattered_shapes(chip_sums):
    return [jax.ShapeDtypeStruct(b.shape if b.ndim == 3 else (N_CHIPS,) + b.shape, b.dtype) for b in chip_sums]


def _scatter_semaphores(n):
    return [pltpu.SemaphoreType.DMA((3 * n,)), pltpu.SemaphoreType.DMA((3 * n,))]


def _scatter_phases(b_refs, r_refs, send_sems, recv_sems):
    n = len(b_refs)
    x, y, c, others = _place()
    me = _chip((x, y))

    def sent(a, j, o):
        src = b_refs[a].at[_chip(o)] if len(b_refs[a].shape) == 3 else b_refs[a]
        return _remote(src, r_refs[a].at[me], send_sems, recv_sems, 3 * a + j, (*o, c))

    def start():
        for a in range(n):
            for j, o in enumerate(others):
                sent(a, j, o).start()

    def finish():
        for a in range(n):
            for j, o in enumerate(others):
                landed = r_refs[a].at[_chip(o)]
                _remote(landed, landed, send_sems, recv_sems, 3 * a + j, (*o, c)).wait_recv()
        for a in range(n):
            for j, o in enumerate(others):
                sent(a, j, o).wait_send()

    return start, finish


def _join_halves(totals):
    n = len(totals)

    def body(*refs):
        in_refs, out_refs, (send_sems, recv_sems) = refs[:n], refs[n:2 * n], refs[2 * n:]
        x, y, c, _ = _place()
        copies = []
        for a in range(n):
            mine, _ = _row_halves(c, in_refs[a].shape[0])
            copies.append(_remote(in_refs[a].at[mine, :], out_refs[a].at[mine, :], send_sems, recv_sems, a, (x, y, 1 - c)))
            copies[-1].start()
        for cp in copies:
            cp.wait()

    return pl.pallas_call(
        body, name="join_halves", in_specs=[HBM] * n, out_specs=[HBM] * n,
        out_shape=[jax.ShapeDtypeStruct(t.shape, F32) for t in totals], input_output_aliases={a: a for a in range(n)},
        scratch_shapes=[pltpu.SemaphoreType.DMA((n,)), pltpu.SemaphoreType.DMA((n,))],
    )(*totals)


ADD_ROWS = 128


def _add_sibling(g, r, place, *, name):
    lead, (half, cols) = g.shape[:-2], r.shape[-2:]
    tr = min(ADD_ROWS, half)
    nb = half // tr
    zeros = (0,) * len(lead)

    def body(place_ref, g_ref, r_ref, o_ref, ob_ref):
        s = g_ref[...] + r_ref[...]
        o_ref[...] = s
        ob_ref[...] = s.astype(BF)

    spec = pl.BlockSpec(lead + (tr, cols), lambda i, p: zeros + (i, 0))
    return pl.pallas_call(
        body, name=name,
        grid_spec=pltpu.PrefetchScalarGridSpec(
            num_scalar_prefetch=1, grid=(nb,),
            in_specs=[pl.BlockSpec(lead + (tr, cols), lambda i, p: zeros + (p[1] * nb + i, 0)), spec], out_specs=[spec, spec]),
        out_shape=[jax.ShapeDtypeStruct(r.shape, F32), jax.ShapeDtypeStruct(r.shape, BF)],
        compiler_params=_params("parallel"),
    )(place, g, r)


def _add_chips(own, received, place, *, name, own_slots):
    half, cols = received.shape[-2:]
    tr = min(ADD_ROWS, half)
    nb = half // tr

    def written(k, p):
        return jnp.where(p[0] == k, (k + 1) % N_CHIPS, k)

    def body(place_ref, own_ref, *refs):
        o_ref = refs[N_CHIPS]
        mine = own_ref[0] if own_slots else own_ref[...]
        if own_slots:
            acc = mine
            for k in range(N_CHIPS):
                acc = acc + jnp.where(place_ref[0] == k, 0.0, refs[k][0].astype(F32))
        else:
            terms = [jnp.where(place_ref[0] == k, mine, refs[k][0]) for k in range(N_CHIPS)]
            acc = ((terms[0] + terms[1]) + terms[2]) + terms[3]
        o_ref[...] = acc

    own_spec = (pl.BlockSpec((1, tr, cols), lambda i, p: (p[0], i, 0)) if own_slots
                else pl.BlockSpec((tr, cols), lambda i, p: (i, 0)))
    return pl.pallas_call(
        body, name=name,
        grid_spec=pltpu.PrefetchScalarGridSpec(
            num_scalar_prefetch=1, grid=(nb,),
            in_specs=[own_spec] + [pl.BlockSpec((1, tr, cols), functools.partial(lambda i, p, k: (written(k, p), i, 0), k=k))
                                   for k in range(N_CHIPS)],
            out_specs=pl.BlockSpec((tr, cols), lambda i, p: (p[1] * nb + i, 0))),
        out_shape=jax.ShapeDtypeStruct((2 * half, cols), F32), compiler_params=_params("parallel"),
    )(place, own, *([received] * N_CHIPS))


SHARDED = (("w_in", (D_MODEL, 4616), 1), ("w_branch_sgu", (SGU_W, D_MODEL), 1), ("w_branch_attn", (ATTN_W, D_MODEL), 1),
           ("w_out", (D_MODEL, D_MODEL), 0), ("w_up", (D_MODEL, D_FF), 1), ("w_down", (D_FF, D_MODEL), 0))
SMALL = (("g_mix_pre", (1, D_MODEL)), ("b_forget", (1, N_HEADS)), ("g_sgu", (1, SGU_W)), ("b_sgu", (1, SGU_W)),
         ("w_spatial", (N_GROUPS * CHUNK, CHUNK)), ("b_spatial", (N_GROUPS, CHUNK)), ("g_mix_post", (1, D_MODEL)),
         ("g_ffn_pre", (1, D_MODEL)), ("g_ffn_post", (1, D_MODEL)))
SMALL_ALIGN = 2 * ADD_ROWS


def _shard_shape(shape, axis):
    return tuple(s // N_CHIPS if a == axis else s for a, s in enumerate(shape))


def _slots_to_full(slots, axis):
    return slots.reshape(-1, slots.shape[2]) if axis == 0 else slots.transpose(1, 0, 2).reshape(slots.shape[1], -1)


def _full_to_slots(full, axis):
    if axis == 0:
        return full.reshape(N_CHIPS, -1, full.shape[1])
    return full.reshape(full.shape[0], N_CHIPS, -1).transpose(1, 0, 2)


def _small_rows(shape):
    return -(-(shape[0] * shape[1]) // (8 * LANES)) * 8


def _pack_small(values):
    parts = []
    for name, shape in SMALL:
        flat = values[name].reshape(-1)
        n = _small_rows(shape)
        parts.append(jnp.pad(flat, (0, n * LANES - flat.shape[0])).reshape(n, LANES))
    rows = sum(p.shape[0] for p in parts)
    pad = -(-rows // SMALL_ALIGN) * SMALL_ALIGN - rows
    return jnp.concatenate(parts + [jnp.zeros((pad, LANES), F32)], axis=0)


def _unpack_small(packed):
    out, row = {}, 0
    for name, shape in SMALL:
        n = _small_rows(shape)
        out[name] = packed[row:row + n].reshape(-1)[:shape[0] * shape[1]].reshape(shape)
        row += n
    return out


IN_Z, IN_Q, IN_K, IN_V, IN_F, IN_G, IN_END = 0, 1024, 1536, 2048, 2560, 2568, 4616


LATE_WEIGHTS = ("w_branch_sgu", "w_branch_attn", "w_out", "w_up", "w_down")
EARLY_GRADS = LATE_WEIGHTS


def _assemble(name, shard, gathered, chip):
    axis = {n: a for n, _, a in SHARDED}[name]
    slot = jnp.arange(N_CHIPS)[:, None, None]
    return _slots_to_full(jnp.where(slot == chip, shard[None], gathered), axis)


def _local_step(x, target, w_in, shards, small, place):
    w_z, w_qkv, w_g = w_in[:, IN_Z:IN_Q], w_in[:, IN_Q:IN_F], w_in[:, IN_G:IN_END]
    w_q, w_k, w_v = w_in[:, IN_Q:IN_K], w_in[:, IN_K:IN_V], w_in[:, IN_V:IN_F]
    w_f = jnp.pad(w_in[:, IN_F:IN_G], ((0, 0), (0, LANES - N_HEADS)))
    b_forget = jnp.pad(small["b_forget"], ((0, 0), (0, LANES - N_HEADS)))
    causal = jnp.tril(jnp.ones((CHUNK, CHUNK), bool))
    ws = jnp.where(causal[None], small["w_spatial"].reshape(N_GROUPS, CHUNK, CHUNK), 0.0).astype(BF)
    ws_t = ws.transpose(0, 2, 1)
    bias_plane = jnp.repeat(small["b_spatial"].T, HEAD_DIM, axis=1)

    xn = _rms_fwd(x, small["g_mix_pre"])
    z = _matmul([(xn, w_z)], nt=False, out_dtypes=[F32], name="proj_z")
    qkv = _matmul([(xn, w_qkv)], nt=False, out_dtypes=[BF], name="proj_qkv")
    gl = _matmul([(xn, w_g)], nt=False, out_dtypes=[BF], name="proj_gate")
    fl = _matmul([(xn, w_f)], nt=False, out_dtypes=[F32], name="proj_forget")
    ysgu = _sgu_fwd(z, small["g_sgu"], small["b_sgu"], ws, bias_plane)
    qf, kl, vl, tile_stats = _attn_prep(qkv, fl, b_forget)
    first_key_tile, last_query_tile, bounded = _attn_ranges(tile_stats)
    yattn, yattn_f, ql, gathered = _attn_fwd(qf, kl, vl, first_key_tile, bounded, [shards[name] for name in LATE_WEIGHTS])
    w = {name: _assemble(name, shards[name], got, place[0]) for name, got in zip(LATE_WEIGHTS, gathered, strict=True)}
    a, b, merged = _branch_merge(ysgu, yattn, w["w_branch_sgu"], w["w_branch_attn"], gl)
    o = _matmul([(merged, w["w_out"])], nt=False, out_dtypes=[F32], name="proj_out")
    h1, xn2 = _mixer_out_fwd(o, x, small["g_mix_post"], small["g_ffn_pre"])

    def relu2(acc):
        r = jnp.maximum(acc, 0.0)
        return (r * r,)

    hid = _matmul([(xn2, w["w_up"])], nt=False, out_dtypes=[BF], name="ffn_up", epilogue=relu2)
    dn = _matmul([(hid, w["w_down"])], nt=False, out_dtypes=[F32], name="ffn_down")
    sq, dy, ddn, dg_ffn_post = _loss_head(dn, h1, target, small["g_ffn_post"])

    dup = _matmul([(ddn, w["w_down"])], nt=True, out_dtypes=[BF], name="ffn_down_bwd",
                  epilogue=lambda acc, h: (acc * (2.0 * jnp.sqrt(h.astype(F32))),), extras=[hid])
    dw_down = _matmul_tn(hid, ddn, name="dw_down")
    dxn2 = _matmul([(dup, w["w_up"])], nt=True, out_dtypes=[F32], name="ffn_up_bwd")
    dw_up = _matmul_tn(xn2, dup, name="dw_up", slots=True)
    dh1, do, dg_ffn_pre, dg_mix_post = _mixer_out_bwd(h1, dxn2, dy, o, small["g_ffn_pre"], small["g_mix_post"])

    dmerged = _matmul([(do, w["w_out"])], nt=True, out_dtypes=[F32], name="proj_out_bwd")
    dw_out = _matmul_tn(merged, do, name="dw_out")
    da, db, dgla, dglb = _gate_bwd(dmerged, a, b, gl)
    dysgu = _matmul([(da, w["w_branch_sgu"])], nt=True, out_dtypes=[F32], name="branch_sgu_bwd")
    dyattn = _matmul([(db, w["w_branch_attn"])], nt=True, out_dtypes=[F32], name="branch_attn_bwd")
    dw_bs = _matmul_tn(ysgu, da, name="dw_branch_sgu")
    dw_ba = _matmul_tn(yattn, db, name="dw_branch_attn")
    early = {"w_branch_sgu": _full_to_slots(dw_bs, 1), "w_branch_attn": _full_to_slots(dw_ba, 1),
             "w_out": _full_to_slots(dw_out, 0), "w_up": dw_up, "w_down": _full_to_slots(dw_down, 0)}
    early_theirs = _exchange_halves([early[name] for name in EARLY_GRADS], name="exchange_halves_early")
    early_sums = {name: _add_sibling(early[name], theirs, place, name="add_sibling_" + name)
                  for name, theirs in zip(EARLY_GRADS, early_theirs, strict=True)}
    dz, dws, dbs, dg_sgu, db_sgu = _sgu_bwd(dysgu, z, small["g_sgu"], small["b_sgu"], ws, ws_t, bias_plane)
    dout = _attn_bwd_prep(dyattn, yattn_f)
    (dq, dk, dv, ext_q, ext_k), early_received = _attn_bwd(
        kl, vl, ql, dout, last_query_tile, [early_sums[name][1] for name in EARLY_GRADS])
    dfl, dbf = _forget_bwd(ext_q, ext_k, fl, b_forget)
    dxn = _matmul([(dz, w_z), (dq, w_q), (dk, w_k), (dv, w_v), (dgla, w_g[:, :D_MODEL]), (dglb, w_g[:, D_MODEL:]), (dfl, w_f)],
                  nt=True, out_dtypes=[F32], name="proj_in_bwd")
    dw_in = jnp.concatenate(
        [_matmul_tn(xn, dz, name="dw_in_z"), _matmul_tn(xn, dq, name="dw_in_q"), _matmul_tn(xn, dk, name="dw_in_k"),
         _matmul_tn(xn, dv, name="dw_in_v"), _matmul_tn(xn, dfl, name="dw_in_f")[:, :N_HEADS],
         _matmul_tn(xn, dgla, name="dw_in_ga"), _matmul_tn(xn, dglb, name="dw_in_gb")], axis=1)
    dx, dg_mix_pre = _input_norm_bwd(x, dxn, dh1, small["g_mix_pre"])

    grads = {"w_in": _full_to_slots(dw_in, 1)}
    early_state = {name: (early_sums[name][0], got) for name, got in zip(EARLY_GRADS, early_received, strict=True)}
    small_grads = {"g_mix_pre": dg_mix_pre, "b_forget": dbf[:, :N_HEADS], "g_sgu": dg_sgu, "b_sgu": db_sgu,
                   "w_spatial": dws.reshape(N_GROUPS * CHUNK, CHUNK), "b_spatial": dbs[:, :N_GROUPS].T,
                   "g_mix_post": dg_mix_post, "g_ffn_pre": dg_ffn_pre, "g_ffn_post": dg_ffn_post}
    return sq, dx, grads, early_state, small_grads


NAMES = ("g_mix_pre", "w_in", "b_forget", "g_sgu", "b_sgu", "w_spatial", "b_spatial", "w_branch_sgu", "w_branch_attn",
         "w_out", "g_mix_post", "g_ffn_pre", "w_up", "w_down", "g_ffn_post")


def kernel(x, g_mix_pre, w_in, b_forget, g_sgu, b_sgu, w_spatial, b_spatial, w_branch_sgu, w_branch_attn, w_out, g_mix_post, g_ffn_pre, w_up, w_down, g_ffn_post, loss_target, m_g_mix_pre, m_w_in, m_b_forget, m_g_sgu, m_b_sgu, m_w_spatial, m_b_spatial, m_w_branch_sgu, m_w_branch_attn, m_w_out, m_g_mix_post, m_g_ffn_pre, m_w_up, m_w_down, m_g_ffn_post, v_g_mix_pre, v_w_in, v_b_forget, v_g_sgu, v_b_sgu, v_w_spatial, v_b_spatial, v_w_branch_sgu, v_w_branch_attn, v_w_out, v_g_mix_post, v_g_ffn_pre, v_w_up, v_w_down, v_g_ffn_post):
    weights = dict(zip(NAMES, (g_mix_pre, w_in, b_forget, g_sgu, b_sgu, w_spatial, b_spatial, w_branch_sgu, w_branch_attn,
                               w_out, g_mix_post, g_ffn_pre, w_up, w_down, g_ffn_post), strict=True))
    first = dict(zip(NAMES, (m_g_mix_pre, m_w_in, m_b_forget, m_g_sgu, m_b_sgu, m_w_spatial, m_b_spatial, m_w_branch_sgu,
                             m_w_branch_attn, m_w_out, m_g_mix_post, m_g_ffn_pre, m_w_up, m_w_down, m_g_ffn_post), strict=True))
    second = dict(zip(NAMES, (v_g_mix_pre, v_w_in, v_b_forget, v_g_sgu, v_b_sgu, v_w_spatial, v_b_spatial, v_w_branch_sgu,
                              v_w_branch_attn, v_w_out, v_g_mix_post, v_g_ffn_pre, v_w_up, v_w_down, v_g_ffn_post), strict=True))
    shard_shapes = {name: _shard_shape(shape, axis) for name, shape, axis in SHARDED}
    small_shapes = dict(SMALL)
    view = lambda name, a: a.reshape(shard_shapes.get(name) or small_shapes[name])

    chip = 2 * lax.axis_index("x") + lax.axis_index("y")
    place = jnp.stack([chip, lax.axis_index("c")]).astype(jnp.int32)

    shards = {name: view(name, weights[name]).astype(BF) for name, _, _ in SHARDED}
    w_in_full = _assemble("w_in", shards["w_in"], _gather_weights([shards["w_in"]])[0], chip)
    small = {name: view(name, weights[name]) for name, _ in SMALL}

    sq, dx, grads, early, small_grads = _local_step(x[0], loss_target[0], w_in_full, shards, small, place)
    loss = lax.psum(0.5 * jnp.sum(sq) / D_MODEL, ("x", "y", "c"))

    late = [name for name, _, _ in SHARDED if name not in early]
    mine = [grads[name] for name in late] + [_pack_small(small_grads)]
    theirs = _exchange_halves(mine, name="exchange_halves")
    sums = [_add_sibling(g, r, place, name="add_sibling_" + tag) for g, r, tag in zip(mine, theirs, late + ["small"], strict=True)]
    received = _scatter_to_owners([b for _, b in sums[:-1]] + [sums[-1][0]])
    totals = {name: _add_chips(s, r, place, name="add_chips_" + name, own_slots=True)
              for name, (s, _), r in zip(late, sums[:-1], received[:-1], strict=True)}
    totals.update({name: _add_chips(s, r, place, name="add_chips_" + name, own_slots=True) for name, (s, r) in early.items()})
    small_total = _add_chips(sums[-1][0], received[-1], place, name="add_chips_small", own_slots=False)
    joined = _join_halves([totals[name] for name, _, _ in SHARDED] + [small_total])
    grad = {**{name: g for (name, _, _), g in zip(SHARDED, joined[:-1], strict=True)}, **_unpack_small(joined[-1])}

    delta, new_m, new_v = {}, {}, {}
    for name in NAMES:
        delta[name], new_m[name], new_v[name] = _adamw(
            view(name, weights[name]), grad[name], view(name, first[name]), view(name, second[name]), name="adamw_" + name)

    like = lambda d: [d[name].reshape(weights[name].shape) for name in NAMES]
    return (loss, dx[None], *like(grad), *like(delta), *like(new_m), *like(new_v))
```

```python
import functools

import jax
import jax.numpy as jnp
from jax import lax
from jax.experimental import pallas as pl
from jax.experimental.pallas import tpu as pltpu

F32 = jnp.float32
BF = jnp.bfloat16
MESH = pl.DeviceIdType.MESH

D_MODEL = 1024
N_HEADS = 8
HEAD_DIM = 64
ATTN_W = N_HEADS * HEAD_DIM
SGU_W = 512
N_GROUPS = 8
CHUNK = 128
D_FF = 4096
EPS = 1e-6
Q_SCALE = HEAD_DIM ** -0.5
N_CHIPS = 4
LANES = 128

ADAM_LR = 0.001
ADAM_B1 = 0.9
ADAM_B2 = 0.999
ADAM_EPS = 1e-08
ADAM_WD = 0.01
ADAM_STEP = 10

VMEM_LIMIT = 48 * 1024 * 1024
ATTN_BWD_VMEM = 58 * 1024 * 1024
NEG = -1e30

LANE_ROWSUM = HEAD_DIM
LANE_COLSUM = HEAD_DIM + 3


def _params(*sem):
    return pltpu.CompilerParams(dimension_semantics=sem, vmem_limit_bytes=VMEM_LIMIT)


def _dot(a, b):
    return jnp.dot(a, b, preferred_element_type=F32)


def _dot_nt(a, b):
    return lax.dot_general(a, b, (((1,), (1,)), ((), ())), preferred_element_type=F32)


def _dot_tn(a, b):
    return lax.dot_general(a, b, (((0,), (0,)), ((), ())), preferred_element_type=F32)


def _split3(c):
    hi = c.astype(BF).astype(F32)
    r = c - hi
    mid = r.astype(BF).astype(F32)
    lo = (r - mid).astype(BF).astype(F32)
    return hi, mid, lo


def _gelu(x):
    k = 0.7978845608028654
    return 0.5 * x * (1.0 + jnp.tanh(k * (x + 0.044715 * (x * x * x))))


def _gelu_grad(x):
    k = 0.7978845608028654
    x2 = x * x
    t = jnp.tanh(k * (x + 0.044715 * (x2 * x)))
    return 0.5 * (1.0 + t) + 0.5 * x * (1.0 - t * t) * (k * (1.0 + 3.0 * 0.044715 * x2))


def _rms_bwd(a, g, dy):
    r = lax.rsqrt(jnp.mean(a * a, axis=-1, keepdims=True) + EPS)
    n = a * r
    dn = dy * g
    da = r * (dn - n * jnp.mean(dn * n, axis=-1, keepdims=True))
    return da, dy * n


MM_ROWS = 1024
MM_COLS = 512


def _matmul(pairs, *, nt, out_dtypes, name, tm=MM_ROWS, tn=MM_COLS, epilogue=None, extras=()):
    n_pairs = len(pairs)
    n_extra = len(extras)
    M = pairs[0][0].shape[0]
    N = pairs[0][1].shape[0] if nt else pairs[0][1].shape[1]
    tm, tn = min(tm, M), min(tn, N)
    assert M % tm == 0 and N % tn == 0

    def body(*refs):
        acc = None
        for p in range(n_pairs):
            a_ref, b_ref = refs[2 * p], refs[2 * p + 1]
            d = _dot_nt(a_ref[...], b_ref[...]) if nt else _dot(a_ref[...], b_ref[...])
            acc = d if acc is None else acc + d
        e_refs = refs[2 * n_pairs:2 * n_pairs + n_extra]
        o_refs = refs[2 * n_pairs + n_extra:]
        outs = (acc,) if epilogue is None else epilogue(acc, *[e[...] for e in e_refs])
        for o_ref, o in zip(o_refs, outs, strict=True):
            o_ref[...] = o.astype(o_ref.dtype)

    in_specs, args = [], []
    for a, b in pairs:
        K = a.shape[1]
        in_specs.append(pl.BlockSpec((tm, K), lambda i, j: (i, 0)))
        in_specs.append(pl.BlockSpec((tn, K), lambda i, j: (j, 0)) if nt else pl.BlockSpec((K, tn), lambda i, j: (0, j)))
        args += [a, b]
    for e in extras:
        in_specs.append(pl.BlockSpec((tm, tn), lambda i, j: (i, j)))
        args.append(e)
    outs = pl.pallas_call(
        body, name=name, grid=(M // tm, N // tn), in_specs=in_specs,
        out_specs=[pl.BlockSpec((tm, tn), lambda i, j: (i, j)) for _ in out_dtypes],
        out_shape=[jax.ShapeDtypeStruct((M, N), dt) for dt in out_dtypes],
        compiler_params=_params("parallel", "parallel"),
    )(*args)
    return outs if len(outs) > 1 else outs[0]


def _matmul_tn(a, b, *, name, tm=1024, tn=1024, tk=2048, slots=False):
    T, K1 = a.shape
    N = b.shape[1]
    tm, tn, tk = min(tm, K1), min(tn, N // N_CHIPS if slots else N), min(tk, T)
    assert K1 % tm == 0 and (N // N_CHIPS if slots else N) % tn == 0 and T % tk == 0
    per_slot = N // N_CHIPS // tn

    def body(a_ref, b_ref, o_ref):
        @pl.when(pl.program_id(2) == 0)
        def _():
            o_ref[...] = jnp.zeros_like(o_ref)

        o_ref[...] += _dot_tn(a_ref[...], b_ref[...])

    if slots:
        out_spec = pl.BlockSpec((None, tm, tn), lambda i, j, k: (j // per_slot, i, j % per_slot))
        out_shape = jax.ShapeDtypeStruct((N_CHIPS, K1, N // N_CHIPS), F32)
    else:
        out_spec = pl.BlockSpec((tm, tn), lambda i, j, k: (i, j))
        out_shape = jax.ShapeDtypeStruct((K1, N), F32)
    return pl.pallas_call(
        body, name=name, grid=(K1 // tm, N // tn, T // tk),
        in_specs=[pl.BlockSpec((tk, tm), lambda i, j, k: (k, i)), pl.BlockSpec((tk, tn), lambda i, j, k: (k, j))],
        out_specs=out_spec, out_shape=out_shape,
        compiler_params=_params("parallel", "parallel", "arbitrary"),
    )(a, b)


def _branch_merge(ysgu, yattn, w_bs, w_ba, gl, *, tm=MM_ROWS, tn=MM_COLS):
    T = ysgu.shape[0]
    tm = min(tm, T)
    nj = D_MODEL // tn

    def body(ys_ref, ya_ref, wbs_ref, wba_ref, gla_ref, glb_ref, a_ref, b_ref, m_ref):
        a = _dot(ys_ref[...], wbs_ref[...])
        b = _dot(ya_ref[...], wba_ref[...])
        a_ref[...] = a.astype(BF)
        b_ref[...] = b.astype(BF)
        m_ref[...] = (jax.nn.sigmoid(gla_ref[...].astype(F32)) * a + jax.nn.sigmoid(glb_ref[...].astype(F32)) * b).astype(BF)

    return pl.pallas_call(
        body, name="branch_merge", grid=(T // tm, nj),
        in_specs=[
            pl.BlockSpec((tm, SGU_W), lambda i, j: (i, 0)),
            pl.BlockSpec((tm, ATTN_W), lambda i, j: (i, 0)),
            pl.BlockSpec((SGU_W, tn), lambda i, j: (0, j)),
            pl.BlockSpec((ATTN_W, tn), lambda i, j: (0, j)),
            pl.BlockSpec((tm, tn), lambda i, j: (i, j)),
            pl.BlockSpec((tm, tn), lambda i, j: (i, j + nj)),
        ],
        out_specs=[pl.BlockSpec((tm, tn), lambda i, j: (i, j))] * 3,
        out_shape=[jax.ShapeDtypeStruct((T, D_MODEL), BF)] * 3,
        compiler_params=_params("parallel", "parallel"),
    )(ysgu, yattn, w_bs, w_ba, gl, gl)


def _row_spec(tr, width):
    return pl.BlockSpec((tr, width), lambda i: (i, 0))


def _vec_spec(width):
    return pl.BlockSpec((1, width), lambda i: (0, 0))


def _rms_fwd(x, g, *, tr=256):
    T = x.shape[0]
    tr = min(tr, T)

    def body(x_ref, g_ref, o_ref):
        xv = x_ref[...]
        r = lax.rsqrt(jnp.mean(xv * xv, axis=-1, keepdims=True) + EPS)
        o_ref[...] = ((xv * r) * g_ref[...]).astype(BF)

    return pl.pallas_call(
        body, name="rms_fwd", grid=(T // tr,),
        in_specs=[_row_spec(tr, D_MODEL), _vec_spec(D_MODEL)], out_specs=_row_spec(tr, D_MODEL),
        out_shape=jax.ShapeDtypeStruct((T, D_MODEL), BF), compiler_params=_params("parallel"),
    )(x, g)


def _mixer_out_fwd(o, x, g_post, g_pre, *, tr=256):
    T = x.shape[0]
    tr = min(tr, T)

    def body(o_ref, x_ref, gpost_ref, gpre_ref, h1_ref, xn2_ref):
        ov = o_ref[...]
        r = lax.rsqrt(jnp.mean(ov * ov, axis=-1, keepdims=True) + EPS)
        h1 = x_ref[...] + (ov * r) * gpost_ref[...]
        h1_ref[...] = h1
        r2 = lax.rsqrt(jnp.mean(h1 * h1, axis=-1, keepdims=True) + EPS)
        xn2_ref[...] = ((h1 * r2) * gpre_ref[...]).astype(BF)

    return pl.pallas_call(
        body, name="mixer_out_fwd", grid=(T // tr,),
        in_specs=[_row_spec(tr, D_MODEL), _row_spec(tr, D_MODEL), _vec_spec(D_MODEL), _vec_spec(D_MODEL)],
        out_specs=[_row_spec(tr, D_MODEL), _row_spec(tr, D_MODEL)],
        out_shape=[jax.ShapeDtypeStruct((T, D_MODEL), F32), jax.ShapeDtypeStruct((T, D_MODEL), BF)],
        compiler_params=_params("parallel"),
    )(o, x, g_post, g_pre)


def _loss_head(dn, h1, target, g_post, *, tr=256):
    T = dn.shape[0]
    tr = min(tr, T)

    def body(dn_ref, h1_ref, t_ref, g_ref, sq_ref, dy_ref, ddn_ref, dg_ref):
        @pl.when(pl.program_id(0) == 0)
        def _():
            sq_ref[...] = jnp.zeros_like(sq_ref)
            dg_ref[...] = jnp.zeros_like(dg_ref)

        a = dn_ref[...]
        g = g_ref[...]
        r = lax.rsqrt(jnp.mean(a * a, axis=-1, keepdims=True) + EPS)
        err = h1_ref[...] + (a * r) * g - t_ref[...]
        sq_ref[...] += jnp.sum(err * err, axis=0, keepdims=True)
        dy = err * (1.0 / D_MODEL)
        dy_ref[...] = dy
        da, dgp = _rms_bwd(a, g, dy)
        ddn_ref[...] = da.astype(BF)
        dg_ref[...] += jnp.sum(dgp, axis=0, keepdims=True)

    return pl.pallas_call(
        body, name="loss_head", grid=(T // tr,),
        in_specs=[_row_spec(tr, D_MODEL)] * 3 + [_vec_spec(D_MODEL)],
        out_specs=[_vec_spec(D_MODEL), _row_spec(tr, D_MODEL), _row_spec(tr, D_MODEL), _vec_spec(D_MODEL)],
        out_shape=[jax.ShapeDtypeStruct((1, D_MODEL), F32), jax.ShapeDtypeStruct((T, D_MODEL), F32),
                   jax.ShapeDtypeStruct((T, D_MODEL), BF), jax.ShapeDtypeStruct((1, D_MODEL), F32)],
        compiler_params=_params("arbitrary"),
    )(dn, h1, target, g_post)


def _mixer_out_bwd(h1, dxn2, dy, o, g_pre, g_post, *, tr=256):
    T = h1.shape[0]
    tr = min(tr, T)

    def body(h1_ref, dxn2_ref, dy_ref, o_ref, gpre_ref, gpost_ref, dh1_ref, do_ref, dgpre_ref, dgpost_ref):
        @pl.when(pl.program_id(0) == 0)
        def _():
            dgpre_ref[...] = jnp.zeros_like(dgpre_ref)
            dgpost_ref[...] = jnp.zeros_like(dgpost_ref)

        da, dgp = _rms_bwd(h1_ref[...], gpre_ref[...], dxn2_ref[...])
        dh1 = dy_ref[...] + da
        dh1_ref[...] = dh1
        dgpre_ref[...] += jnp.sum(dgp, axis=0, keepdims=True)
        do, dgp2 = _rms_bwd(o_ref[...], gpost_ref[...], dh1)
        do_ref[...] = do.astype(BF)
        dgpost_ref[...] += jnp.sum(dgp2, axis=0, keepdims=True)

    return pl.pallas_call(
        body, name="mixer_out_bwd", grid=(T // tr,),
        in_specs=[_row_spec(tr, D_MODEL)] * 4 + [_vec_spec(D_MODEL)] * 2,
        out_specs=[_row_spec(tr, D_MODEL), _row_spec(tr, D_MODEL), _vec_spec(D_MODEL), _vec_spec(D_MODEL)],
        out_shape=[jax.ShapeDtypeStruct((T, D_MODEL), F32), jax.ShapeDtypeStruct((T, D_MODEL), BF),
                   jax.ShapeDtypeStruct((1, D_MODEL), F32), jax.ShapeDtypeStruct((1, D_MODEL), F32)],
        compiler_params=_params("arbitrary"),
    )(h1, dxn2, dy, o, g_pre, g_post)


def _input_norm_bwd(x, dxn, dh1, g, *, tr=256):
    T = x.shape[0]
    tr = min(tr, T)

    def body(x_ref, dxn_ref, dh1_ref, g_ref, dx_ref, dg_ref):
        @pl.when(pl.program_id(0) == 0)
        def _():
            dg_ref[...] = jnp.zeros_like(dg_ref)

        da, dgp = _rms_bwd(x_ref[...], g_ref[...], dxn_ref[...])
        dx_ref[...] = dh1_ref[...] + da
        dg_ref[...] += jnp.sum(dgp, axis=0, keepdims=True)

    return pl.pallas_call(
        body, name="input_norm_bwd", grid=(T // tr,),
        in_specs=[_row_spec(tr, D_MODEL)] * 3 + [_vec_spec(D_MODEL)],
        out_specs=[_row_spec(tr, D_MODEL), _vec_spec(D_MODEL)],
        out_shape=[jax.ShapeDtypeStruct((T, D_MODEL), F32), jax.ShapeDtypeStruct((1, D_MODEL), F32)],
        compiler_params=_params("arbitrary"),
    )(x, dxn, dh1, g)


def _gate_bwd(dm, a, b, gl, *, tr=256):
    T = dm.shape[0]
    tr = min(tr, T)

    def body(dm_ref, a_ref, b_ref, gla_ref, glb_ref, da_ref, db_ref, dgla_ref, dglb_ref):
        dmv = dm_ref[...]
        ga = jax.nn.sigmoid(gla_ref[...].astype(F32))
        gb = jax.nn.sigmoid(glb_ref[...].astype(F32))
        da_ref[...] = (dmv * ga).astype(BF)
        db_ref[...] = (dmv * gb).astype(BF)
        dgla_ref[...] = (dmv * a_ref[...].astype(F32) * (ga * (1.0 - ga))).astype(BF)
        dglb_ref[...] = (dmv * b_ref[...].astype(F32) * (gb * (1.0 - gb))).astype(BF)

    spec = _row_spec(tr, D_MODEL)
    spec_b = pl.BlockSpec((tr, D_MODEL), lambda i: (i, 1))
    da, db, dgla, dglb = pl.pallas_call(
        body, name="gate_bwd", grid=(T // tr,),
        in_specs=[spec, spec, spec, spec, spec_b], out_specs=[spec] * 4,
        out_shape=[jax.ShapeDtypeStruct((T, D_MODEL), BF)] * 4, compiler_params=_params("parallel"),
    )(dm, a, b, gl, gl)
    return da, db, dgla, dglb


def _sgu_norm(z_tile, g, b):
    gz = _gelu(z_tile)
    u, vv = gz[:, :SGU_W], gz[:, SGU_W:]
    xc = vv - jnp.mean(vv, axis=-1, keepdims=True)
    rstd = lax.rsqrt(jnp.mean(xc * xc, axis=-1, keepdims=True) + EPS)
    xhat = xc * rstd
    return u, xhat, rstd, xhat * g + b


def _sgu_mix(w_ref, v_bf, first_half):
    parts = []
    for p in range(N_GROUPS // 2):
        vp = v_bf[:, p * LANES:(p + 1) * LANES]
        parts.append(jnp.where(first_half, _dot(w_ref[2 * p], vp), _dot(w_ref[2 * p + 1], vp)))
    return jnp.concatenate(parts, axis=1)


def _sgu_fwd(z, g_sgu, b_sgu, ws, bias_plane, *, tm=512):
    T = z.shape[0]
    tm = min(tm, T)

    def body(z_ref, g_ref, b_ref, ws_ref, bp_ref, y_ref):
        u, _, _, vn = _sgu_norm(z_ref[...], g_ref[...], b_ref[...])
        vn_bf = vn.astype(BF)
        first_half = lax.broadcasted_iota(jnp.int32, (CHUNK, LANES), 1) < HEAD_DIM
        for c in range(tm // CHUNK):
            rows = slice(c * CHUNK, (c + 1) * CHUNK)
            s = _sgu_mix(ws_ref, vn_bf[rows, :], first_half) + bp_ref[...]
            y_ref[rows, :] = (u[rows, :] * s).astype(BF)

    return pl.pallas_call(
        body, name="sgu_fwd", grid=(T // tm,),
        in_specs=[_row_spec(tm, 2 * SGU_W), _vec_spec(SGU_W), _vec_spec(SGU_W),
                  pl.BlockSpec((N_GROUPS, CHUNK, CHUNK), lambda i: (0, 0, 0)),
                  pl.BlockSpec((CHUNK, SGU_W), lambda i: (0, 0))],
        out_specs=_row_spec(tm, SGU_W), out_shape=jax.ShapeDtypeStruct((T, SGU_W), BF),
        compiler_params=_params("parallel"),
    )(z, g_sgu, b_sgu, ws, bias_plane)


def _sgu_bwd(dy, z, g_sgu, b_sgu, ws, ws_t, bias_plane, *, tm=512):
    T = z.shape[0]
    tm = min(tm, T)
    n_steps = T // tm

    def body(dy_ref, z_ref, g_ref, b_ref, ws_ref, wst_ref, bp_ref, dz_ref, dws_ref, dbs_ref, dg_ref, db_ref, dbp_ref):
        step = pl.program_id(0)

        @pl.when(step == 0)
        def _():
            dws_ref[...] = jnp.zeros_like(dws_ref)
            dg_ref[...] = jnp.zeros_like(dg_ref)
            db_ref[...] = jnp.zeros_like(db_ref)
            dbp_ref[...] = jnp.zeros_like(dbp_ref)

        g = g_ref[...]
        zt = z_ref[...]
        u, xhat, rstd, vn = _sgu_norm(zt, g, b_ref[...])
        vn_bf = vn.astype(BF)
        first_half = lax.broadcasted_iota(jnp.int32, (CHUNK, LANES), 1) < HEAD_DIM
        dyv = dy_ref[...]
        dg_acc = jnp.zeros((1, SGU_W), F32)
        db_acc = jnp.zeros((1, SGU_W), F32)
        for c in range(tm // CHUNK):
            rows = slice(c * CHUNK, (c + 1) * CHUNK)
            v_c = vn_bf[rows, :]
            s = _sgu_mix(ws_ref, v_c, first_half) + bp_ref[...]
            dy_c = dyv[rows, :]
            du = dy_c * s
            dsv = dy_c * u[rows, :]
            dbp_ref[...] += dsv
            ds_bf = dsv.astype(BF)
            zero = jnp.zeros((CHUNK, LANES), BF)
            for p in range(N_GROUPS // 2):
                dsp = ds_bf[:, p * LANES:(p + 1) * LANES]
                vp = v_c[:, p * LANES:(p + 1) * LANES]
                dws_ref[2 * p] += _dot_nt(jnp.where(first_half, dsp, zero), vp)
                dws_ref[2 * p + 1] += _dot_nt(jnp.where(first_half, zero, dsp), vp)
            dvn = _sgu_mix(wst_ref, ds_bf, first_half)
            xh = xhat[rows, :]
            dxh = dvn * g
            dvv = rstd[rows, :] * (dxh - jnp.mean(dxh, axis=-1, keepdims=True)
                                   - xh * jnp.mean(dxh * xh, axis=-1, keepdims=True))
            dg_acc += jnp.sum(dvn * xh, axis=0, keepdims=True)
            db_acc += jnp.sum(dvn, axis=0, keepdims=True)
            dgz = jnp.concatenate([du, dvv], axis=1)
            dz_ref[rows, :] = (dgz * _gelu_grad(zt[rows, :])).astype(BF)
        dg_ref[...] += dg_acc
        db_ref[...] += db_acc

        @pl.when(step == n_steps - 1)
        def _():
            r = lax.broadcasted_iota(jnp.int32, (CHUNK, CHUNK), 0)
            cidx = lax.broadcasted_iota(jnp.int32, (CHUNK, CHUNK), 1)
            causal = (cidx <= r).astype(F32)
            for gi in range(N_GROUPS):
                dws_ref[gi] = dws_ref[gi] * causal
            lane = lax.broadcasted_iota(jnp.int32, (CHUNK, LANES), 1)
            out = jnp.zeros((CHUNK, LANES), F32)
            dbp = dbp_ref[...]
            for gi in range(N_GROUPS):
                col = jnp.sum(dbp[:, gi * HEAD_DIM:(gi + 1) * HEAD_DIM], axis=1, keepdims=True)
                out = jnp.where(lane == gi, col, out)
            dbs_ref[...] = out

    w_spec = pl.BlockSpec((N_GROUPS, CHUNK, CHUNK), lambda i: (0, 0, 0))
    plane = pl.BlockSpec((CHUNK, SGU_W), lambda i: (0, 0))
    return pl.pallas_call(
        body, name="sgu_bwd", grid=(n_steps,),
        in_specs=[_row_spec(tm, SGU_W), _row_spec(tm, 2 * SGU_W), _vec_spec(SGU_W), _vec_spec(SGU_W), w_spec, w_spec, plane],
        out_specs=[_row_spec(tm, 2 * SGU_W), w_spec, pl.BlockSpec((CHUNK, LANES), lambda i: (0, 0)),
                   _vec_spec(SGU_W), _vec_spec(SGU_W)],
        out_shape=[jax.ShapeDtypeStruct((T, 2 * SGU_W), BF), jax.ShapeDtypeStruct((N_GROUPS, CHUNK, CHUNK), F32),
                   jax.ShapeDtypeStruct((CHUNK, LANES), F32), jax.ShapeDtypeStruct((1, SGU_W), F32),
                   jax.ShapeDtypeStruct((1, SGU_W), F32)],
        scratch_shapes=[pltpu.VMEM((CHUNK, SGU_W), F32)],
        compiler_params=_params("arbitrary"),
    )(dy, z, g_sgu, b_sgu, ws, ws_t, bias_plane)


def _tri(n, upper):
    r = lax.broadcasted_iota(jnp.int32, (n, n), 0)
    c = lax.broadcasted_iota(jnp.int32, (n, n), 1)
    return ((c >= r) if upper else (c <= r)).astype(BF)


def _scan_dot(tri, x):
    hi, mid, lo = _split3(x)
    return (_dot(tri, hi.astype(BF)) + _dot(tri, mid.astype(BF))) + _dot(tri, lo.astype(BF))


def _with_lanes(base, lane, start, cols):
    out = base
    for k, col in enumerate(cols):
        if col is not None:
            out = jnp.where(lane == start + k, col, out)
    return out


def _logit_bound(q_norm, k_norm):
    return NORM_SLACK * q_norm * k_norm + 1.0


ATTN_TILE = 512
SKIP_BELOW = -110.0
NORM_SLACK = 1.001
BOUNDED_GAP = 60.0


def _attn_prep(qkv, fl, b_forget, *, tp=ATTN_TILE):
    T = qkv.shape[0]
    tp = min(tp, T)

    def body(qkv_ref, fl_ref, bf_ref, qf_ref, kl_ref, vl_ref, st_ref, carry_ref, kmax_ref):
        @pl.when(pl.program_id(0) == 0)
        def _():
            carry_ref[...] = jnp.zeros_like(carry_ref)
            kmax_ref[...] = jnp.zeros_like(kmax_ref)

        x = fl_ref[...] + bf_ref[...]
        logf = jnp.minimum(x, 0.0) - jnp.log(1.0 + jnp.exp(-jnp.abs(x)))
        cum = _scan_dot(_tri(tp, upper=False), logf) + carry_ref[...]
        carry_ref[...] = cum[tp - 1:tp, :]
        lane = lax.broadcasted_iota(jnp.int32, (tp, HEAD_DIM), 1)
        ones3 = jnp.where(lane < 3, 1.0, 0.0)
        qkvv = qkv_ref[...]
        st_row = lax.broadcasted_iota(jnp.int32, (N_HEADS, LANES), 0)
        st_lane = lax.broadcasted_iota(jnp.int32, (N_HEADS, LANES), 1)
        stats = jnp.zeros((N_HEADS, LANES), F32)
        kmax_lane = lax.broadcasted_iota(jnp.int32, (1, LANES), 1)
        for h in range(N_HEADS):
            ch = cum[:, h:h + 1]
            c3 = _split3(ch)
            qh = qkvv[:, h * HEAD_DIM:(h + 1) * HEAD_DIM].astype(F32) * Q_SCALE
            kh = qkvv[:, ATTN_W + h * HEAD_DIM:ATTN_W + (h + 1) * HEAD_DIM].astype(F32)
            vh = qkvv[:, 2 * ATTN_W + h * HEAD_DIM:2 * ATTN_W + (h + 1) * HEAD_DIM].astype(F32)
            q_norm = jnp.sqrt(jnp.sum(qh * qh, axis=1, keepdims=True))
            qn = jnp.max(q_norm, axis=0, keepdims=True)
            kn = jnp.sqrt(jnp.max(jnp.sum(kh * kh, axis=1, keepdims=True), axis=0, keepdims=True))
            k_seen = jnp.maximum(kmax_ref[:, h:h + 1], kn)
            kmax_ref[...] = jnp.where(kmax_lane == h, k_seen, kmax_ref[...])
            bound3 = _split3(-_logit_bound(q_norm, k_seen))
            ext_q = _with_lanes(jnp.where((lane >= 3) & (lane < 6), 1.0, 0.0), lane, 0, list(c3) + [None] * 3 + list(bound3))
            ext_k = _with_lanes(jnp.where((lane < 3) | ((lane >= 6) & (lane < 9)), 1.0, 0.0), lane, 3, [-c for c in c3])
            qf_ref[h] = jnp.concatenate([qh, ext_q], axis=1).astype(BF)
            kl_ref[h] = jnp.concatenate([kh, ext_k], axis=1).astype(BF)
            vl_ref[h] = jnp.concatenate([vh, ones3], axis=1).astype(BF)
            tile_stats = (qn, kn, jnp.max(ch, axis=0, keepdims=True), jnp.min(ch, axis=0, keepdims=True), k_seen)
            for k, val in enumerate(tile_stats):
                stats = jnp.where((st_row == h) & (st_lane == k), val, stats)
        st_ref[0] = stats

    head_spec = pl.BlockSpec((N_HEADS, tp, LANES), lambda i: (0, i, 0))
    return pl.pallas_call(
        body, name="attn_prep", grid=(T // tp,),
        in_specs=[_row_spec(tp, 3 * ATTN_W), _row_spec(tp, LANES), _vec_spec(LANES)],
        out_specs=[head_spec] * 3 + [pl.BlockSpec((1, N_HEADS, LANES), lambda i: (i, 0, 0))],
        out_shape=[jax.ShapeDtypeStruct((N_HEADS, T, LANES), BF)] * 3 + [jax.ShapeDtypeStruct((T // tp, N_HEADS, LANES), F32)],
        scratch_shapes=[pltpu.VMEM((1, LANES), F32), pltpu.VMEM((1, LANES), F32)], compiler_params=_params("arbitrary"),
    )(qkv, fl, b_forget)


def _attn_ranges(stats):
    qn, kn, cmax, cmin, k_seen = (stats[:, :, k].T for k in range(5))
    n = qn.shape[1]
    bounded = (2.0 * _logit_bound(qn, k_seen) <= BOUNDED_GAP).reshape(N_HEADS // 2, 2, n).all(axis=1)
    reach = NORM_SLACK * qn * (jnp.max(kn, axis=1, keepdims=True) + kn) + cmax
    i = jnp.arange(n)[None, :, None]
    j = jnp.arange(n)[None, None, :]
    need = ((reach[:, :, None] - cmin[:, None, :] >= SKIP_BELOW) | (i == j)) & (j <= i)
    first = jnp.min(jnp.where(need, j, n), axis=2).reshape(N_HEADS // 2, 2, n).min(axis=1)
    last = jnp.max(jnp.where(need, i, -1), axis=1).reshape(N_HEADS // 2, 2, n).max(axis=1)
    return first.reshape(-1).astype(F32), last.reshape(-1).astype(F32), bounded.reshape(-1).astype(F32)


def _pair_block(t):
    return pl.BlockSpec((2, t, LANES), lambda p, i, *_: (p, i, 0))


def _pair_full(T):
    return pl.BlockSpec((2, T, LANES), lambda p, i, *_: (p, 0, 0))


def _packed_block(t):
    return pl.BlockSpec((t, LANES), lambda p, i, *_: (i, p))


def _causal(t, keys_in_rows=False):
    r = lax.broadcasted_iota(jnp.int32, (t, t), 0)
    c = lax.broadcasted_iota(jnp.int32, (t, t), 1)
    return (r <= c) if keys_in_rows else (c <= r)


def _tile_rows(j, t):
    return pl.ds(pl.multiple_of(j * t, t), t)


def _attn_call(body, name, tile_scalars, operands, in_specs, out_specs, out_shape, scratch_shapes, n_tiles):
    return pl.pallas_call(
        body, name=name,
        grid_spec=pltpu.PrefetchScalarGridSpec(
            num_scalar_prefetch=len(tile_scalars), grid=(N_HEADS // 2, n_tiles), in_specs=in_specs, out_specs=out_specs,
            scratch_shapes=scratch_shapes),
        out_shape=out_shape, compiler_params=_params("arbitrary", "arbitrary"),
    )(*tile_scalars, *operands)


def _attn_fwd(qf, kl, vl, first, bounded, shards, *, tq=ATTN_TILE):
    T = qf.shape[1]
    tq = min(tq, T)
    n = T // tq
    n_steps = (N_HEADS // 2) * n
    k = len(shards)

    def body(first_ref, bounded_ref, qf_ref, kl_ref, vl_ref, *refs):
        w_refs, (o_ref, of_ref, ql_ref), g_refs = refs[:k], refs[k:k + 3], refs[k + 3:2 * k + 3]
        m_ref, acc_ref, send_sems, recv_sems = refs[2 * k + 3:]
        i = pl.program_id(1)
        tile = pl.program_id(0) * n + i
        gather_start, gather_forward, gather_finish = _gather_phases(w_refs, g_refs, send_sems, recv_sems)
        pl.when(tile == 0)(gather_start)
        pl.when(tile == (3 * n_steps) // 4)(gather_forward)
        start = first_ref[tile].astype(jnp.int32)
        is_bounded = bounded_ref[tile] > 0.5
        acc_ref[...] = jnp.zeros_like(acc_ref)
        diagonal = _tile_rows(i, tq)
        causal = _causal(tq)

        def logits(hh, rows):
            return _dot_nt(qf_ref[hh], kl_ref[hh, rows, :])

        @pl.when(is_bounded)
        def _():
            m_ref[...] = jnp.zeros_like(m_ref)

            def update(hh, s, rows):
                acc_ref[hh] += _dot(jnp.exp(s).astype(BF), vl_ref[hh, rows, :])

            def step(j, carry):
                for hh in range(2):
                    update(hh, logits(hh, _tile_rows(j, tq)), _tile_rows(j, tq))
                return carry

            lax.fori_loop(start, i, step, 0)
            for hh in range(2):
                update(hh, jnp.where(causal, logits(hh, diagonal), NEG), diagonal)

        @pl.when(jnp.logical_not(is_bounded))
        def _():
            m_ref[...] = jnp.full_like(m_ref, NEG)

            def update(hh, s, rows):
                m_old = m_ref[hh]
                m_new = jnp.maximum(m_old, jnp.max(s, axis=1, keepdims=True))
                p = jnp.exp(s - m_new)
                acc_ref[hh] = jnp.exp(m_old - m_new) * acc_ref[hh] + _dot(p.astype(BF), vl_ref[hh, rows, :])
                m_ref[hh] = m_new

            def step(j, carry):
                for hh in range(2):
                    update(hh, logits(hh, _tile_rows(j, tq)), _tile_rows(j, tq))
                return carry

            lax.fori_loop(start, i, step, 0)
            for hh in range(2):
                update(hh, jnp.where(causal, logits(hh, diagonal), NEG), diagonal)

        lane = lax.broadcasted_iota(jnp.int32, (tq, LANES), 1)
        outs = []
        for hh in range(2):
            q = qf_ref[hh].astype(F32)
            acc = acc_ref[hh]
            l = acc[:, HEAD_DIM:HEAD_DIM + 1]
            outs.append(acc[:, :HEAD_DIM] / l)
            at = HEAD_DIM + 6
            neg_bound = (q[:, at:at + 1] + q[:, at + 1:at + 2]) + q[:, at + 2:at + 3]
            ql_ref[hh] = _with_lanes(q, lane, at, _split3(neg_bound - (m_ref[hh] + jnp.log(l)))).astype(BF)
        o = jnp.concatenate(outs, axis=1)
        o_ref[...] = o.astype(BF)
        of_ref[...] = o
        pl.when(tile == n_steps - 1)(gather_finish)

    outs = _attn_call(
        body, "attn_fwd", (first, bounded), (qf, kl, vl, *shards),
        [_pair_block(tq), _pair_full(T), _pair_full(T)] + [HBM] * k,
        [_packed_block(tq), _packed_block(tq), _pair_block(tq)] + [HBM] * k,
        [jax.ShapeDtypeStruct((T, ATTN_W), BF), jax.ShapeDtypeStruct((T, ATTN_W), F32),
         jax.ShapeDtypeStruct((N_HEADS, T, LANES), BF)] + _gathered_shapes(shards),
        [pltpu.VMEM((2, tq, 1), F32), pltpu.VMEM((2, tq, LANES), F32)] + _gather_semaphores(k), n)
    return outs[0], outs[1], outs[2], outs[3:]


def _attn_bwd_prep(dya, of, *, tr=256):
    T = dya.shape[0]
    tr = min(tr, T)

    def body(d_ref, o_ref, do_ref):
        lane = lax.broadcasted_iota(jnp.int32, (tr, HEAD_DIM), 1)
        dv, ov = d_ref[...], o_ref[...]
        for h in range(N_HEADS):
            d = dv[:, h * HEAD_DIM:(h + 1) * HEAD_DIM]
            delta = jnp.sum(d * ov[:, h * HEAD_DIM:(h + 1) * HEAD_DIM], axis=1, keepdims=True)
            ext = _with_lanes(jnp.zeros((tr, HEAD_DIM), F32), lane, 0, _split3(-delta))
            do_ref[h] = jnp.concatenate([d, ext], axis=1).astype(BF)

    return pl.pallas_call(
        body, name="attn_bwd_prep", grid=(T // tr,),
        in_specs=[_row_spec(tr, ATTN_W), _row_spec(tr, ATTN_W)],
        out_specs=pl.BlockSpec((N_HEADS, tr, LANES), lambda i: (0, i, 0)),
        out_shape=jax.ShapeDtypeStruct((N_HEADS, T, LANES), BF), compiler_params=_params("parallel"),
    )(dya, of)


def _attn_bwd(kl, vl, ql, do, last, chip_sums, *, tk=ATTN_TILE):
    T = ql.shape[1]
    tk = min(tk, T)
    n = T // tk
    n_steps = (N_HEADS // 2) * n
    m = len(chip_sums)

    def body(last_ref, kl_ref, vl_ref, ql_ref, do_ref, *refs):
        b_refs, (dq_ref, dk_ref, dv_ref, extq_ref, extk_ref), r_refs = refs[:m], refs[m:m + 5], refs[m + 5:2 * m + 5]
        dq_acc, dk_acc, dv_acc, send_sems, recv_sems = refs[2 * m + 5:]
        j = pl.program_id(1)
        tile = pl.program_id(0) * n + j
        scatter_start, scatter_finish = _scatter_phases(b_refs, r_refs, send_sems, recv_sems)
        pl.when(tile == 0)(scatter_start)

        @pl.when(j == 0)
        def _():
            dq_acc[...] = jnp.zeros_like(dq_acc)

        dk_acc[...] = jnp.zeros_like(dk_acc)
        dv_acc[...] = jnp.zeros_like(dv_acc)

        def block(hh, rows, mask):
            qi, di, k = ql_ref[hh, rows, :], do_ref[hh, rows, :], kl_ref[hh]
            p_t = jnp.exp(_dot_nt(k, qi))
            if mask is not None:
                p_t = jnp.where(mask, p_t, 0.0)
            ds_t = (p_t * _dot_nt(vl_ref[hh], di)).astype(BF)
            dk_acc[hh] += _dot(ds_t, qi)
            dv_acc[hh] += _dot(p_t.astype(BF), di)
            dq_acc[hh, rows, :] += _dot_tn(ds_t, k)

        causal_t = _causal(tk, keys_in_rows=True)
        for hh in range(2):
            block(hh, _tile_rows(j, tk), causal_t)

        def step(i, carry):
            for hh in range(2):
                block(hh, _tile_rows(i, tk), None)
            return carry

        lax.fori_loop(j + 1, last_ref[pl.program_id(0) * n + j].astype(jnp.int32) + 1, step, 0)
        dk_ref[...] = jnp.concatenate([dk_acc[hh][:, :HEAD_DIM] for hh in range(2)], axis=1).astype(BF)
        dv_ref[...] = jnp.concatenate([dv_acc[hh][:, :HEAD_DIM] for hh in range(2)], axis=1).astype(BF)
        extk_ref[...] = jnp.concatenate([dk_acc[hh][:, HEAD_DIM:] for hh in range(2)], axis=1)

        @pl.when(j == n - 1)
        def _():
            dq_ref[...] = jnp.concatenate([dq_acc[hh][:, :HEAD_DIM] * Q_SCALE for hh in range(2)], axis=1).astype(BF)
            extq_ref[...] = jnp.concatenate([dq_acc[hh][:, HEAD_DIM:] for hh in range(2)], axis=1)

        pl.when(tile == n_steps - 1)(scatter_finish)

    whole = pl.BlockSpec((T, LANES), lambda p, j, *_: (0, p))
    outs = pl.pallas_call(
        body, name="attn_bwd",
        grid_spec=pltpu.PrefetchScalarGridSpec(
            num_scalar_prefetch=1, grid=(N_HEADS // 2, n),
            in_specs=[_pair_block(tk), _pair_block(tk), _pair_full(T), _pair_full(T)] + [HBM] * m,
            out_specs=[whole, _packed_block(tk), _packed_block(tk), whole, _packed_block(tk)] + [HBM] * m,
            scratch_shapes=[pltpu.VMEM((2, T, LANES), F32), pltpu.VMEM((2, tk, LANES), F32), pltpu.VMEM((2, tk, LANES), F32)]
            + _scatter_semaphores(m)),
        out_shape=[jax.ShapeDtypeStruct((T, ATTN_W), BF)] * 3 + [jax.ShapeDtypeStruct((T, ATTN_W), F32)] * 2
        + _scattered_shapes(chip_sums),
        compiler_params=pltpu.CompilerParams(dimension_semantics=("arbitrary", "arbitrary"), vmem_limit_bytes=ATTN_BWD_VMEM),
    )(last, kl, vl, ql, do, *chip_sums)
    return outs[:5], outs[5:]


def _forget_bwd(ext_q, ext_k, fl, b_forget, *, tp=256):
    T = fl.shape[0]
    tp = min(tp, T)
    n = T // tp

    def body(eq_ref, ek_ref, fl_ref, bf_ref, dfl_ref, dbf_ref, carry_ref):
        @pl.when(pl.program_id(0) == 0)
        def _():
            carry_ref[...] = jnp.zeros_like(carry_ref)
            dbf_ref[...] = jnp.zeros_like(dbf_ref)

        lane = lax.broadcasted_iota(jnp.int32, (tp, LANES), 1)
        eq, ek = eq_ref[...], ek_ref[...]
        cols = [eq[:, h * HEAD_DIM:h * HEAD_DIM + 1] - ek[:, h * HEAD_DIM + 3:h * HEAD_DIM + 4] for h in range(N_HEADS)]
        dcum = _with_lanes(jnp.zeros((tp, LANES), F32), lane, 0, cols)
        suffix = _scan_dot(_tri(tp, upper=True), dcum) + carry_ref[...]
        carry_ref[...] = suffix[0:1, :]
        x = fl_ref[...] + bf_ref[...]
        dfl = jnp.where(lane < N_HEADS, suffix / (1.0 + jnp.exp(x)), 0.0)
        dfl_ref[...] = dfl.astype(BF)
        dbf_ref[...] += jnp.sum(dfl, axis=0, keepdims=True)

    rev = lambda w: pl.BlockSpec((tp, w), lambda i: (n - 1 - i, 0))
    return pl.pallas_call(
        body, name="forget_bwd", grid=(n,),
        in_specs=[rev(ATTN_W), rev(ATTN_W), rev(LANES), _vec_spec(LANES)],
        out_specs=[rev(LANES), _vec_spec(LANES)],
        out_shape=[jax.ShapeDtypeStruct((T, LANES), BF), jax.ShapeDtypeStruct((1, LANES), F32)],
        scratch_shapes=[pltpu.VMEM((1, LANES), F32)], compiler_params=_params("arbitrary"),
    )(ext_q, ext_k, fl, b_forget)


def _adamw(w, g, m, v, *, name, tr=256):
    _, rows, cols = w.shape
    tr = tr if rows % tr == 0 else rows

    def body(w_ref, g_ref, m_ref, v_ref, go_ref, d_ref, nm_ref, nv_ref):
        gv = g_ref[...]
        go_ref[...] = gv
        nm = ADAM_B1 * m_ref[...] + (1.0 - ADAM_B1) * gv
        nv = ADAM_B2 * v_ref[...] + (1.0 - ADAM_B2) * (gv * gv)
        m_hat = nm / (1.0 - ADAM_B1 ** ADAM_STEP)
        v_hat = nv / (1.0 - ADAM_B2 ** ADAM_STEP)
        d_ref[...] = -ADAM_LR * (m_hat / (jnp.sqrt(v_hat) + ADAM_EPS) + ADAM_WD * w_ref[...])
        nm_ref[...] = nm
        nv_ref[...] = nv

    spec = pl.BlockSpec((None, tr, cols), lambda i: (0, i, 0))
    return pl.pallas_call(
        body, name=name, grid=(rows // tr,), in_specs=[spec, pl.BlockSpec((tr, cols), lambda i: (i, 0)), spec, spec],
        out_specs=[spec] * 4, out_shape=[jax.ShapeDtypeStruct((1, rows, cols), F32)] * 4,
        compiler_params=_params("parallel"),
    )(w, g, m, v)


HBM = pl.BlockSpec(memory_space=pltpu.HBM)
BF16_ROWS = 16


def _place():
    x, y, c = lax.axis_index("x"), lax.axis_index("y"), lax.axis_index("c")
    others = [(1 - x, y), (x, 1 - y), (1 - x, 1 - y)]
    return x, y, c, others


def _chip(xy):
    return 2 * xy[0] + xy[1]


def _row_halves(c, rows):
    half = rows // 2
    assert half % BF16_ROWS == 0
    return (pl.ds(pl.multiple_of(c * half, BF16_ROWS), half), pl.ds(pl.multiple_of((1 - c) * half, BF16_ROWS), half))


def _remote(src, dst, send_sems, recv_sems, k, to):
    return pltpu.make_async_remote_copy(src_ref=src, dst_ref=dst, send_sem=send_sems.at[k], recv_sem=recv_sems.at[k],
                                        device_id=to, device_id_type=MESH)


def _gather_weights(shards):
    n = len(shards)

    def body(*refs):
        for phase in _gather_phases(refs[:n], refs[n:2 * n], *refs[2 * n:]):
            phase()

    return pl.pallas_call(
        body, name="gather_weights", in_specs=[HBM] * n, out_specs=[HBM] * n,
        out_shape=_gathered_shapes(shards), scratch_shapes=_gather_semaphores(n),
    )(*shards)


def _gathered_shapes(shards):
    return [jax.ShapeDtypeStruct((N_CHIPS,) + s.shape, s.dtype) for s in shards]


def _gather_semaphores(n):
    return [pltpu.SemaphoreType.DMA((6 * n,)), pltpu.SemaphoreType.DMA((6 * n,))]


def _gather_phases(w_refs, g_refs, send_sems, recv_sems):
    n = len(w_refs)
    x, y, c, others = _place()
    sibling, me = (x, y, 1 - c), _chip((x, y))
    halves = [_row_halves(c, w.shape[0]) for w in w_refs]

    def sent(a, j, o):
        mine, _ = halves[a]
        return _remote(w_refs[a].at[mine, :], g_refs[a].at[me, mine, :], send_sems, recv_sems, 6 * a + j, (*o, c))

    def passed(a, j, o):
        landed = g_refs[a].at[_chip(o), halves[a][0], :]
        return _remote(landed, landed, send_sems, recv_sems, 6 * a + 3 + j, sibling)

    def start():
        for a in range(n):
            for j, o in enumerate(others):
                sent(a, j, o).start()

    def forward():
        for j, o in enumerate(others):
            for a in range(n):
                landed = g_refs[a].at[_chip(o), halves[a][0], :]
                _remote(landed, landed, send_sems, recv_sems, 6 * a + j, (*o, c)).wait_recv()
                passed(a, j, o).start()

    def finish():
        for j, o in enumerate(others):
            for a in range(n):
                landed = g_refs[a].at[_chip(o), halves[a][1], :]
                _remote(landed, landed, send_sems, recv_sems, 6 * a + 3 + j, sibling).wait_recv()
        for a in range(n):
            for j, o in enumerate(others):
                sent(a, j, o).wait_send()
                passed(a, j, o).wait_send()

    return start, forward, finish


def _exchange_halves(arrays, *, name):
    n = len(arrays)

    def body(*refs):
        g_refs, r_refs, (send_sems, recv_sems) = refs[:n], refs[n:2 * n], refs[2 * n:]
        x, y, c, _ = _place()
        copies = []
        for a in range(n):
            _, theirs = _row_halves(c, g_refs[a].shape[-2])
            src = g_refs[a].at[:, theirs, :] if len(g_refs[a].shape) == 3 else g_refs[a].at[theirs, :]
            copies.append(_remote(src, r_refs[a], send_sems, recv_sems, a, (x, y, 1 - c)))
            copies[-1].start()
        for cp in copies:
            cp.wait()

    def half(s):
        return jax.ShapeDtypeStruct(s.shape[:-2] + (s.shape[-2] // 2, s.shape[-1]), F32)

    return pl.pallas_call(
        body, name=name, in_specs=[HBM] * n, out_specs=[HBM] * n, out_shape=[half(g) for g in arrays],
        scratch_shapes=[pltpu.SemaphoreType.DMA((n,)), pltpu.SemaphoreType.DMA((n,))],
    )(*arrays)


def _scatter_to_owners(chip_sums):
    n = len(chip_sums)

    def body(*refs):
        for phase in _scatter_phases(refs[:n], refs[n:2 * n], *refs[2 * n:]):
            phase()

    return pl.pallas_call(
        body, name="scatter_to_owners", in_specs=[HBM] * n, out_specs=[HBM] * n,
        out_shape=_scattered_shapes(chip_sums), scratch_shapes=_scatter_semaphores(n),
    )(*chip_sums)


def _scattered_shapes(chip_sums):
    return [jax.ShapeDtypeStruct(b.shape if b.ndim == 3 else (N_CHIPS,) + b.shape, b.dtype) for b in chip_sums]


def _scatter_semaphores(n):
    return [pltpu.SemaphoreType.DMA((3 * n,)), pltpu.SemaphoreType.DMA((3 * n,))]


def _scatter_phases(b_refs, r_refs, send_sems, recv_sems):
    n = len(b_refs)
    x, y, c, others = _place()
    me = _chip((x, y))

    def sent(a, j, o):
        src = b_refs[a].at[_chip(o)] if len(b_refs[a].shape) == 3 else b_refs[a]
        return _remote(src, r_refs[a].at[me], send_sems, recv_sems, 3 * a + j, (*o, c))

    def start():
        for a in range(n):
            for j, o in enumerate(others):
                sent(a, j, o).start()

    def finish():
        for a in range(n):
            for j, o in enumerate(others):
                landed = r_refs[a].at[_chip(o)]
                _remote(landed, landed, send_sems, recv_sems, 3 * a + j, (*o, c)).wait_recv()
        for a in range(n):
            for j, o in enumerate(others):
                sent(a, j, o).wait_send()

    return start, finish


def _join_halves(totals):
    n = len(totals)

    def body(*refs):
        in_refs, out_refs, (send_sems, recv_sems) = refs[:n], refs[n:2 * n], refs[2 * n:]
        x, y, c, _ = _place()
        copies = []
        for a in range(n):
            mine, _ = _row_halves(c, in_refs[a].shape[0])
            copies.append(_remote(in_refs[a].at[mine, :], out_refs[a].at[mine, :], send_sems, recv_sems, a, (x, y, 1 - c)))
            copies[-1].start()
        for cp in copies:
            cp.wait()

    return pl.pallas_call(
        body, name="join_halves", in_specs=[HBM] * n, out_specs=[HBM] * n,
        out_shape=[jax.ShapeDtypeStruct(t.shape, F32) for t in totals], input_output_aliases={a: a for a in range(n)},
        scratch_shapes=[pltpu.SemaphoreType.DMA((n,)), pltpu.SemaphoreType.DMA((n,))],
    )(*totals)


ADD_ROWS = 128


def _add_sibling(g, r, place, *, name):
    lead, (half, cols) = g.shape[:-2], r.shape[-2:]
    tr = min(ADD_ROWS, half)
    nb = half // tr
    zeros = (0,) * len(lead)

    def body(place_ref, g_ref, r_ref, o_ref, ob_ref):
        s = g_ref[...] + r_ref[...]
        o_ref[...] = s
        ob_ref[...] = s.astype(BF)

    spec = pl.BlockSpec(lead + (tr, cols), lambda i, p: zeros + (i, 0))
    return pl.pallas_call(
        body, name=name,
        grid_spec=pltpu.PrefetchScalarGridSpec(
            num_scalar_prefetch=1, grid=(nb,),
            in_specs=[pl.BlockSpec(lead + (tr, cols), lambda i, p: zeros + (p[1] * nb + i, 0)), spec], out_specs=[spec, spec]),
        out_shape=[jax.ShapeDtypeStruct(r.shape, F32), jax.ShapeDtypeStruct(r.shape, BF)],
        compiler_params=_params("parallel"),
    )(place, g, r)


def _add_chips(own, received, place, *, name, own_slots):
    half, cols = received.shape[-2:]
    tr = min(ADD_ROWS, half)
    nb = half // tr

    def written(k, p):
        return jnp.where(p[0] == k, (k + 1) % N_CHIPS, k)

    def body(place_ref, own_ref, *refs):
        o_ref = refs[N_CHIPS]
        mine = own_ref[0] if own_slots else own_ref[...]
        if own_slots:
            acc = mine
            for k in range(N_CHIPS):
                acc = acc + jnp.where(place_ref[0] == k, 0.0, refs[k][0].astype(F32))
        else:
            terms = [jnp.where(place_ref[0] == k, mine, refs[k][0]) for k in range(N_CHIPS)]
            acc = ((terms[0] + terms[1]) + terms[2]) + terms[3]
        o_ref[...] = acc

    own_spec = (pl.BlockSpec((1, tr, cols), lambda i, p: (p[0], i, 0)) if own_slots
                else pl.BlockSpec((tr, cols), lambda i, p: (i, 0)))
    return pl.pallas_call(
        body, name=name,
        grid_spec=pltpu.PrefetchScalarGridSpec(
            num_scalar_prefetch=1, grid=(nb,),
            in_specs=[own_spec] + [pl.BlockSpec((1, tr, cols), functools.partial(lambda i, p, k: (written(k, p), i, 0), k=k))
                                   for k in range(N_CHIPS)],
            out_specs=pl.BlockSpec((tr, cols), lambda i, p: (p[1] * nb + i, 0))),
        out_shape=jax.ShapeDtypeStruct((2 * half, cols), F32), compiler_params=_params("parallel"),
    )(place, own, *([received] * N_CHIPS))


SHARDED = (("w_in", (D_MODEL, 4616), 1), ("w_branch_sgu", (SGU_W, D_MODEL), 1), ("w_branch_attn", (ATTN_W, D_MODEL), 1),
           ("w_out", (D_MODEL, D_MODEL), 0), ("w_up", (D_MODEL, D_FF), 1), ("w_down", (D_FF, D_MODEL), 0))
SMALL = (("g_mix_pre", (1, D_MODEL)), ("b_forget", (1, N_HEADS)), ("g_sgu", (1, SGU_W)), ("b_sgu", (1, SGU_W)),
         ("w_spatial", (N_GROUPS * CHUNK, CHUNK)), ("b_spatial", (N_GROUPS, CHUNK)), ("g_mix_post", (1, D_MODEL)),
         ("g_ffn_pre", (1, D_MODEL)), ("g_ffn_post", (1, D_MODEL)))
SMALL_ALIGN = 2 * ADD_ROWS


def _shard_shape(shape, axis):
    return tuple(s // N_CHIPS if a == axis else s for a, s in enumerate(shape))


def _slots_to_full(slots, axis):
    return slots.reshape(-1, slots.shape[2]) if axis == 0 else slots.transpose(1, 0, 2).reshape(slots.shape[1], -1)


def _full_to_slots(full, axis):
    if axis == 0:
        return full.reshape(N_CHIPS, -1, full.shape[1])
    return full.reshape(full.shape[0], N_CHIPS, -1).transpose(1, 0, 2)


def _small_rows(shape):
    return -(-(shape[0] * shape[1]) // (8 * LANES)) * 8


def _pack_small(values):
    parts = []
    for name, shape in SMALL:
        flat = values[name].reshape(-1)
        n = _small_rows(shape)
        parts.append(jnp.pad(flat, (0, n * LANES - flat.shape[0])).reshape(n, LANES))
    rows = sum(p.shape[0] for p in parts)
    pad = -(-rows // SMALL_ALIGN) * SMALL_ALIGN - rows
    return jnp.concatenate(parts + [jnp.zeros((pad, LANES), F32)], axis=0)


def _unpack_small(packed):
    out, row = {}, 0
    for name, shape in SMALL:
        n = _small_rows(shape)
        out[name] = packed[row:row + n].reshape(-1)[:shape[0] * shape[1]].reshape(shape)
        row += n
    return out


IN_Z, IN_Q, IN_K, IN_V, IN_F, IN_G, IN_END = 0, 1024, 1536, 2048, 2560, 2568, 4616


LATE_WEIGHTS = ("w_branch_sgu", "w_branch_attn", "w_out", "w_up", "w_down")
EARLY_GRADS = LATE_WEIGHTS


def _assemble(name, shard, gathered, chip):
    axis = {n: a for n, _, a in SHARDED}[name]
    slot = jnp.arange(N_CHIPS)[:, None, None]
    return _slots_to_full(jnp.where(slot == chip, shard[None], gathered), axis)


def _local_step(x, target, w_in, shards, small, place):
    w_z, w_qkv, w_g = w_in[:, IN_Z:IN_Q], w_in[:, IN_Q:IN_F], w_in[:, IN_G:IN_END]
    w_q, w_k, w_v = w_in[:, IN_Q:IN_K], w_in[:, IN_K:IN_V], w_in[:, IN_V:IN_F]
    w_f = jnp.pad(w_in[:, IN_F:IN_G], ((0, 0), (0, LANES - N_HEADS)))
    b_forget = jnp.pad(small["b_forget"], ((0, 0), (0, LANES - N_HEADS)))
    causal = jnp.tril(jnp.ones((CHUNK, CHUNK), bool))
    ws = jnp.where(causal[None], small["w_spatial"].reshape(N_GROUPS, CHUNK, CHUNK), 0.0).astype(BF)
    ws_t = ws.transpose(0, 2, 1)
    bias_plane = jnp.repeat(small["b_spatial"].T, HEAD_DIM, axis=1)

    xn = _rms_fwd(x, small["g_mix_pre"])
    z = _matmul([(xn, w_z)], nt=False, out_dtypes=[F32], name="proj_z")
    qkv = _matmul([(xn, w_qkv)], nt=False, out_dtypes=[BF], name="proj_qkv")
    gl = _matmul([(xn, w_g)], nt=False, out_dtypes=[BF], name="proj_gate")
    fl = _matmul([(xn, w_f)], nt=False, out_dtypes=[F32], name="proj_forget")
    ysgu = _sgu_fwd(z, small["g_sgu"], small["b_sgu"], ws, bias_plane)
    qf, kl, vl, tile_stats = _attn_prep(qkv, fl, b_forget)
    first_key_tile, last_query_tile, bounded = _attn_ranges(tile_stats)
    yattn, yattn_f, ql, gathered = _attn_fwd(qf, kl, vl, first_key_tile, bounded, [shards[name] for name in LATE_WEIGHTS])
    w = {name: _assemble(name, shards[name], got, place[0]) for name, got in zip(LATE_WEIGHTS, gathered, strict=True)}
    a, b, merged = _branch_merge(ysgu, yattn, w["w_branch_sgu"], w["w_branch_attn"], gl)
    o = _matmul([(merged, w["w_out"])], nt=False, out_dtypes=[F32], name="proj_out")
    h1, xn2 = _mixer_out_fwd(o, x, small["g_mix_post"], small["g_ffn_pre"])

    def relu2(acc):
        r = jnp.maximum(acc, 0.0)
        return (r * r,)

    hid = _matmul([(xn2, w["w_up"])], nt=False, out_dtypes=[BF], name="ffn_up", epilogue=relu2)
    dn = _matmul([(hid, w["w_down"])], nt=False, out_dtypes=[F32], name="ffn_down")
    sq, dy, ddn, dg_ffn_post = _loss_head(dn, h1, target, small["g_ffn_post"])

    dup = _matmul([(ddn, w["w_down"])], nt=True, out_dtypes=[BF], name="ffn_down_bwd",
                  epilogue=lambda acc, h: (acc * (2.0 * jnp.sqrt(h.astype(F32))),), extras=[hid])
    dw_down = _matmul_tn(hid, ddn, name="dw_down")
    dxn2 = _matmul([(dup, w["w_up"])], nt=True, out_dtypes=[F32], name="ffn_up_bwd")
    dw_up = _matmul_tn(xn2, dup, name="dw_up", slots=True)
    dh1, do, dg_ffn_pre, dg_mix_post = _mixer_out_bwd(h1, dxn2, dy, o, small["g_ffn_pre"], small["g_mix_post"])

    dmerged = _matmul([(do, w["w_out"])], nt=True, out_dtypes=[F32], name="proj_out_bwd")
    dw_out = _matmul_tn(merged, do, name="dw_out")
    da, db, dgla, dglb = _gate_bwd(dmerged, a, b, gl)
    dysgu = _matmul([(da, w["w_branch_sgu"])], nt=True, out_dtypes=[F32], name="branch_sgu_bwd")
    dyattn = _matmul([(db, w["w_branch_attn"])], nt=True, out_dtypes=[F32], name="branch_attn_bwd")
    dw_bs = _matmul_tn(ysgu, da, name="dw_branch_sgu")
    dw_ba = _matmul_tn(yattn, db, name="dw_branch_attn")
    early = {"w_branch_sgu": _full_to_slots(dw_bs, 1), "w_branch_attn": _full_to_slots(dw_ba, 1),
             "w_out": _full_to_slots(dw_out, 0), "w_up": dw_up, "w_down": _full_to_slots(dw_down, 0)}
    early_theirs = _exchange_halves([early[name] for name in EARLY_GRADS], name="exchange_halves_early")
    early_sums = {name: _add_sibling(early[name], theirs, place, name="add_sibling_" + name)
                  for name, theirs in zip(EARLY_GRADS, early_theirs, strict=True)}
    dz, dws, dbs, dg_sgu, db_sgu = _sgu_bwd(dysgu, z, small["g_sgu"], small["b_sgu"], ws, ws_t, bias_plane)
    dout = _attn_bwd_prep(dyattn, yattn_f)
    (dq, dk, dv, ext_q, ext_k), early_received = _attn_bwd(
        kl, vl, ql, dout, last_query_tile, [early_sums[name][1] for name in EARLY_GRADS])
    dfl, dbf = _forget_bwd(ext_q, ext_k, fl, b_forget)
    dxn = _matmul([(dz, w_z), (dq, w_q), (dk, w_k), (dv, w_v), (dgla, w_g[:, :D_MODEL]), (dglb, w_g[:, D_MODEL:]), (dfl, w_f)],
                  nt=True, out_dtypes=[F32], name="proj_in_bwd")
    dw_in = jnp.concatenate(
        [_matmul_tn(xn, dz, name="dw_in_z"), _matmul_tn(xn, dq, name="dw_in_q"), _matmul_tn(xn, dk, name="dw_in_k"),
         _matmul_tn(xn, dv, name="dw_in_v"), _matmul_tn(xn, dfl, name="dw_in_f")[:, :N_HEADS],
         _matmul_tn(xn, dgla, name="dw_in_ga"), _matmul_tn(xn, dglb, name="dw_in_gb")], axis=1)
    dx, dg_mix_pre = _input_norm_bwd(x, dxn, dh1, small["g_mix_pre"])

    grads = {"w_in": _full_to_slots(dw_in, 1)}
    early_state = {name: (early_sums[name][0], got) for name, got in zip(EARLY_GRADS, early_received, strict=True)}
    small_grads = {"g_mix_pre": dg_mix_pre, "b_forget": dbf[:, :N_HEADS], "g_sgu": dg_sgu, "b_sgu": db_sgu,
                   "w_spatial": dws.reshape(N_GROUPS * CHUNK, CHUNK), "b_spatial": dbs[:, :N_GROUPS].T,
                   "g_mix_post": dg_mix_post, "g_ffn_pre": dg_ffn_pre, "g_ffn_post": dg_ffn_post}
    return sq, dx, grads, early_state, small_grads


NAMES = ("g_mix_pre", "w_in", "b_forget", "g_sgu", "b_sgu", "w_spatial", "b_spatial", "w_branch_sgu", "w_branch_attn",
         "w_out", "g_mix_post", "g_ffn_pre", "w_up", "w_down", "g_ffn_post")


def kernel(x, g_mix_pre, w_in, b_forget, g_sgu, b_sgu, w_spatial, b_spatial, w_branch_sgu, w_branch_attn, w_out, g_mix_post, g_ffn_pre, w_up, w_down, g_ffn_post, loss_target, m_g_mix_pre, m_w_in, m_b_forget, m_g_sgu, m_b_sgu, m_w_spatial, m_b_spatial, m_w_branch_sgu, m_w_branch_attn, m_w_out, m_g_mix_post, m_g_ffn_pre, m_w_up, m_w_down, m_g_ffn_post, v_g_mix_pre, v_w_in, v_b_forget, v_g_sgu, v_b_sgu, v_w_spatial, v_b_spatial, v_w_branch_sgu, v_w_branch_attn, v_w_out, v_g_mix_post, v_g_ffn_pre, v_w_up, v_w_down, v_g_ffn_post):
    weights = dict(zip(NAMES, (g_mix_pre, w_in, b_forget, g_sgu, b_sgu, w_spatial, b_spatial, w_branch_sgu, w_branch_attn,
                               w_out, g_mix_post, g_ffn_pre, w_up, w_down, g_ffn_post), strict=True))
    first = dict(zip(NAMES, (m_g_mix_pre, m_w_in, m_b_forget, m_g_sgu, m_b_sgu, m_w_spatial, m_b_spatial, m_w_branch_sgu,
                             m_w_branch_attn, m_w_out, m_g_mix_post, m_g_ffn_pre, m_w_up, m_w_down, m_g_ffn_post), strict=True))
    second = dict(zip(NAMES, (v_g_mix_pre, v_w_in, v_b_forget, v_g_sgu, v_b_sgu, v_w_spatial, v_b_spatial, v_w_branch_sgu,
                              v_w_branch_attn, v_w_out, v_g_mix_post, v_g_ffn_pre, v_w_up, v_w_down, v_g_ffn_post), strict=True))
    shard_shapes = {name: _shard_shape(shape, axis) for name, shape, axis in SHARDED}
    small_shapes = dict(SMALL)
    view = lambda name, a: a.reshape(shard_shapes.get(name) or small_shapes[name])

    chip = 2 * lax.axis_index("x") + lax.axis_index("y")
    place = jnp.stack([chip, lax.axis_index("c")]).astype(jnp.int32)

    shards = {name: view(name, weights[name]).astype(BF) for name, _, _ in SHARDED}
    w_in_full = _assemble("w_in", shards["w_in"], _gather_weights([shards["w_in"]])[0], chip)
    small = {name: view(name, weights[name]) for name, _ in SMALL}

    sq, dx, grads, early, small_grads = _local_step(x[0], loss_target[0], w_in_full, shards, small, place)
    loss = lax.psum(0.5 * jnp.sum(sq) / D_MODEL, ("x", "y", "c"))

    late = [name for name, _, _ in SHARDED if name not in early]
    mine = [grads[name] for name in late] + [_pack_small(small_grads)]
    theirs = _exchange_halves(mine, name="exchange_halves")
    sums = [_add_sibling(g, r, place, name="add_sibling_" + tag) for g, r, tag in zip(mine, theirs, late + ["small"], strict=True)]
    received = _scatter_to_owners([b for _, b in sums[:-1]] + [sums[-1][0]])
    totals = {name: _add_chips(s, r, place, name="add_chips_" + name, own_slots=True)
              for name, (s, _), r in zip(late, sums[:-1], received[:-1], strict=True)}
    totals.update({name: _add_chips(s, r, place, name="add_chips_" + name, own_slots=True) for name, (s, r) in early.items()})
    small_total = _add_chips(sums[-1][0], received[-1], place, name="add_chips_small", own_slots=False)
    joined = _join_halves([totals[name] for name, _, _ in SHARDED] + [small_total])
    grad = {**{name: g for (name, _, _), g in zip(SHARDED, joined[:-1], strict=True)}, **_unpack_small(joined[-1])}

    grad_out, delta, new_m, new_v = {}, {}, {}, {}
    for name in NAMES:
        rows, cols = grad[name].shape
        as_given = lambda a: a.reshape(1, rows, cols)
        grad_out[name], delta[name], new_m[name], new_v[name] = _adamw(
            as_given(weights[name]), grad[name], as_given(first[name]), as_given(second[name]), name="adamw_" + name)

    like = lambda d: [d[name].reshape(weights[name].shape) for name in NAMES]
    return (loss, dx[None], *like(grad_out), *like(delta), *like(new_m), *like(new_v))
```

```python
import functools

import jax
import jax.numpy as jnp
from jax import lax
from jax.experimental import pallas as pl
from jax.experimental.pallas import tpu as pltpu

F32 = jnp.float32
BF = jnp.bfloat16
MESH = pl.DeviceIdType.MESH

D_MODEL = 1024
N_HEADS = 8
HEAD_DIM = 64
ATTN_W = N_HEADS * HEAD_DIM
SGU_W = 512
N_GROUPS = 8
CHUNK = 128
D_FF = 4096
EPS = 1e-6
Q_SCALE = HEAD_DIM ** -0.5
N_CHIPS = 4
LANES = 128

ADAM_LR = 0.001
ADAM_B1 = 0.9
ADAM_B2 = 0.999
ADAM_EPS = 1e-08
ADAM_WD = 0.01
ADAM_STEP = 10

VMEM_LIMIT = 48 * 1024 * 1024
ATTN_BWD_VMEM = 58 * 1024 * 1024
NEG = -1e30

LANE_ROWSUM = HEAD_DIM
LANE_COLSUM = HEAD_DIM + 3


def _params(*sem):
    return pltpu.CompilerParams(dimension_semantics=sem, vmem_limit_bytes=VMEM_LIMIT)


def _dot(a, b):
    return jnp.dot(a, b, preferred_element_type=F32)


def _dot_nt(a, b):
    return lax.dot_general(a, b, (((1,), (1,)), ((), ())), preferred_element_type=F32)


def _dot_tn(a, b):
    return lax.dot_general(a, b, (((0,), (0,)), ((), ())), preferred_element_type=F32)


def _split3(c):
    hi = c.astype(BF).astype(F32)
    r = c - hi
    mid = r.astype(BF).astype(F32)
    lo = (r - mid).astype(BF).astype(F32)
    return hi, mid, lo


def _gelu(x):
    k = 0.7978845608028654
    return 0.5 * x * (1.0 + jnp.tanh(k * (x + 0.044715 * (x * x * x))))


def _gelu_grad(x):
    k = 0.7978845608028654
    x2 = x * x
    t = jnp.tanh(k * (x + 0.044715 * (x2 * x)))
    return 0.5 * (1.0 + t) + 0.5 * x * (1.0 - t * t) * (k * (1.0 + 3.0 * 0.044715 * x2))


def _rms_bwd(a, g, dy):
    r = lax.rsqrt(jnp.mean(a * a, axis=-1, keepdims=True) + EPS)
    n = a * r
    dn = dy * g
    da = r * (dn - n * jnp.mean(dn * n, axis=-1, keepdims=True))
    return da, dy * n


MM_ROWS = 1024
MM_COLS = 512


def _matmul(pairs, *, nt, out_dtypes, name, tm=MM_ROWS, tn=MM_COLS, epilogue=None, extras=(), scatter=()):
    n_pairs, n_extra, n_out, n_scatter = len(pairs), len(extras), len(out_dtypes), len(scatter)
    M = pairs[0][0].shape[0]
    N = pairs[0][1].shape[0] if nt else pairs[0][1].shape[1]
    tm, tn = min(tm, M), min(tn, N)
    assert M % tm == 0 and N % tn == 0
    grid = (M // tm, N // tn)

    def body(*refs):
        n_in = 2 * n_pairs + n_extra
        if n_scatter:
            step = pl.program_id(0) * grid[1] + pl.program_id(1)
            first = n_in + n_scatter + n_out
            scatter_start, scatter_finish = _scatter_phases(
                refs[n_in:n_in + n_scatter], refs[first:first + n_scatter], *refs[first + n_scatter:])
            pl.when(step == 0)(scatter_start)
        acc = None
        for p in range(n_pairs):
            a_ref, b_ref = refs[2 * p], refs[2 * p + 1]
            d = _dot_nt(a_ref[...], b_ref[...]) if nt else _dot(a_ref[...], b_ref[...])
            acc = d if acc is None else acc + d
        e_refs = refs[2 * n_pairs:n_in]
        o_refs = refs[n_in + n_scatter:n_in + n_scatter + n_out]
        outs = (acc,) if epilogue is None else epilogue(acc, *[e[...] for e in e_refs])
        for o_ref, o in zip(o_refs, outs, strict=True):
            o_ref[...] = o.astype(o_ref.dtype)
        if n_scatter:
            pl.when(step == grid[0] * grid[1] - 1)(scatter_finish)

    in_specs, args = [], []
    for a, b in pairs:
        K = a.shape[1]
        in_specs.append(pl.BlockSpec((tm, K), lambda i, j: (i, 0)))
        in_specs.append(pl.BlockSpec((tn, K), lambda i, j: (j, 0)) if nt else pl.BlockSpec((K, tn), lambda i, j: (0, j)))
        args += [a, b]
    for e in extras:
        in_specs.append(pl.BlockSpec((tm, tn), lambda i, j: (i, j)))
        args.append(e)
    order = ("arbitrary", "arbitrary") if n_scatter else ("parallel", "parallel")
    outs = pl.pallas_call(
        body, name=name, grid=grid, in_specs=in_specs + [HBM] * n_scatter,
        out_specs=[pl.BlockSpec((tm, tn), lambda i, j: (i, j)) for _ in out_dtypes] + [HBM] * n_scatter,
        out_shape=[jax.ShapeDtypeStruct((M, N), dt) for dt in out_dtypes] + (_scattered_shapes(scatter) if n_scatter else []),
        scratch_shapes=_scatter_semaphores(n_scatter) if n_scatter else [],
        compiler_params=_params(*order),
    )(*args, *scatter)
    if n_scatter:
        return outs[:n_out], outs[n_out:]
    return outs if len(outs) > 1 else outs[0]


def _matmul_tn(a, b, *, name, tm=1024, tn=1024, tk=2048, slots=False):
    T, K1 = a.shape
    N = b.shape[1]
    tm, tn, tk = min(tm, K1), min(tn, N // N_CHIPS if slots else N), min(tk, T)
    assert K1 % tm == 0 and (N // N_CHIPS if slots else N) % tn == 0 and T % tk == 0
    per_slot = N // N_CHIPS // tn

    def body(a_ref, b_ref, o_ref):
        @pl.when(pl.program_id(2) == 0)
        def _():
            o_ref[...] = jnp.zeros_like(o_ref)

        o_ref[...] += _dot_tn(a_ref[...], b_ref[...])

    if slots:
        out_spec = pl.BlockSpec((None, tm, tn), lambda i, j, k: (j // per_slot, i, j % per_slot))
        out_shape = jax.ShapeDtypeStruct((N_CHIPS, K1, N // N_CHIPS), F32)
    else:
        out_spec = pl.BlockSpec((tm, tn), lambda i, j, k: (i, j))
        out_shape = jax.ShapeDtypeStruct((K1, N), F32)
    return pl.pallas_call(
        body, name=name, grid=(K1 // tm, N // tn, T // tk),
        in_specs=[pl.BlockSpec((tk, tm), lambda i, j, k: (k, i)), pl.BlockSpec((tk, tn), lambda i, j, k: (k, j))],
        out_specs=out_spec, out_shape=out_shape,
        compiler_params=_params("parallel", "parallel", "arbitrary"),
    )(a, b)


def _branch_merge(ysgu, yattn, w_bs, w_ba, gl, *, tm=MM_ROWS, tn=MM_COLS):
    T = ysgu.shape[0]
    tm = min(tm, T)
    nj = D_MODEL // tn

    def body(ys_ref, ya_ref, wbs_ref, wba_ref, gla_ref, glb_ref, a_ref, b_ref, m_ref):
        a = _dot(ys_ref[...], wbs_ref[...])
        b = _dot(ya_ref[...], wba_ref[...])
        a_ref[...] = a.astype(BF)
        b_ref[...] = b.astype(BF)
        m_ref[...] = (jax.nn.sigmoid(gla_ref[...].astype(F32)) * a + jax.nn.sigmoid(glb_ref[...].astype(F32)) * b).astype(BF)

    return pl.pallas_call(
        body, name="branch_merge", grid=(T // tm, nj),
        in_specs=[
            pl.BlockSpec((tm, SGU_W), lambda i, j: (i, 0)),
            pl.BlockSpec((tm, ATTN_W), lambda i, j: (i, 0)),
            pl.BlockSpec((SGU_W, tn), lambda i, j: (0, j)),
            pl.BlockSpec((ATTN_W, tn), lambda i, j: (0, j)),
            pl.BlockSpec((tm, tn), lambda i, j: (i, j)),
            pl.BlockSpec((tm, tn), lambda i, j: (i, j + nj)),
        ],
        out_specs=[pl.BlockSpec((tm, tn), lambda i, j: (i, j))] * 3,
        out_shape=[jax.ShapeDtypeStruct((T, D_MODEL), BF)] * 3,
        compiler_params=_params("parallel", "parallel"),
    )(ysgu, yattn, w_bs, w_ba, gl, gl)


def _row_spec(tr, width):
    return pl.BlockSpec((tr, width), lambda i: (i, 0))


def _vec_spec(width):
    return pl.BlockSpec((1, width), lambda i: (0, 0))


def _rms_fwd(x, g, *, tr=256):
    T = x.shape[0]
    tr = min(tr, T)

    def body(x_ref, g_ref, o_ref):
        xv = x_ref[...]
        r = lax.rsqrt(jnp.mean(xv * xv, axis=-1, keepdims=True) + EPS)
        o_ref[...] = ((xv * r) * g_ref[...]).astype(BF)

    return pl.pallas_call(
        body, name="rms_fwd", grid=(T // tr,),
        in_specs=[_row_spec(tr, D_MODEL), _vec_spec(D_MODEL)], out_specs=_row_spec(tr, D_MODEL),
        out_shape=jax.ShapeDtypeStruct((T, D_MODEL), BF), compiler_params=_params("parallel"),
    )(x, g)


def _mixer_out_fwd(o, x, g_post, g_pre, *, tr=256):
    T = x.shape[0]
    tr = min(tr, T)

    def body(o_ref, x_ref, gpost_ref, gpre_ref, h1_ref, xn2_ref):
        ov = o_ref[...]
        r = lax.rsqrt(jnp.mean(ov * ov, axis=-1, keepdims=True) + EPS)
        h1 = x_ref[...] + (ov * r) * gpost_ref[...]
        h1_ref[...] = h1
        r2 = lax.rsqrt(jnp.mean(h1 * h1, axis=-1, keepdims=True) + EPS)
        xn2_ref[...] = ((h1 * r2) * gpre_ref[...]).astype(BF)

    return pl.pallas_call(
        body, name="mixer_out_fwd", grid=(T // tr,),
        in_specs=[_row_spec(tr, D_MODEL), _row_spec(tr, D_MODEL), _vec_spec(D_MODEL), _vec_spec(D_MODEL)],
        out_specs=[_row_spec(tr, D_MODEL), _row_spec(tr, D_MODEL)],
        out_shape=[jax.ShapeDtypeStruct((T, D_MODEL), F32), jax.ShapeDtypeStruct((T, D_MODEL), BF)],
        compiler_params=_params("parallel"),
    )(o, x, g_post, g_pre)


def _loss_head(dn, h1, target, g_post, *, tr=256):
    T = dn.shape[0]
    tr = min(tr, T)

    def body(dn_ref, h1_ref, t_ref, g_ref, sq_ref, dy_ref, ddn_ref, dg_ref):
        @pl.when(pl.program_id(0) == 0)
        def _():
            sq_ref[...] = jnp.zeros_like(sq_ref)
            dg_ref[...] = jnp.zeros_like(dg_ref)

        a = dn_ref[...]
        g = g_ref[...]
        r = lax.rsqrt(jnp.mean(a * a, axis=-1, keepdims=True) + EPS)
        err = h1_ref[...] + (a * r) * g - t_ref[...]
        sq_ref[...] += jnp.sum(err * err, axis=0, keepdims=True)
        dy = err * (1.0 / D_MODEL)
        dy_ref[...] = dy
        da, dgp = _rms_bwd(a, g, dy)
        ddn_ref[...] = da.astype(BF)
        dg_ref[...] += jnp.sum(dgp, axis=0, keepdims=True)

    return pl.pallas_call(
        body, name="loss_head", grid=(T // tr,),
        in_specs=[_row_spec(tr, D_MODEL)] * 3 + [_vec_spec(D_MODEL)],
        out_specs=[_vec_spec(D_MODEL), _row_spec(tr, D_MODEL), _row_spec(tr, D_MODEL), _vec_spec(D_MODEL)],
        out_shape=[jax.ShapeDtypeStruct((1, D_MODEL), F32), jax.ShapeDtypeStruct((T, D_MODEL), F32),
                   jax.ShapeDtypeStruct((T, D_MODEL), BF), jax.ShapeDtypeStruct((1, D_MODEL), F32)],
        compiler_params=_params("arbitrary"),
    )(dn, h1, target, g_post)


def _mixer_out_bwd(h1, dxn2, dy, o, g_pre, g_post, *, tr=256):
    T = h1.shape[0]
    tr = min(tr, T)

    def body(h1_ref, dxn2_ref, dy_ref, o_ref, gpre_ref, gpost_ref, dh1_ref, do_ref, dgpre_ref, dgpost_ref):
        @pl.when(pl.program_id(0) == 0)
        def _():
            dgpre_ref[...] = jnp.zeros_like(dgpre_ref)
            dgpost_ref[...] = jnp.zeros_like(dgpost_ref)

        da, dgp = _rms_bwd(h1_ref[...], gpre_ref[...], dxn2_ref[...])
        dh1 = dy_ref[...] + da
        dh1_ref[...] = dh1
        dgpre_ref[...] += jnp.sum(dgp, axis=0, keepdims=True)
        do, dgp2 = _rms_bwd(o_ref[...], gpost_ref[...], dh1)
        do_ref[...] = do.astype(BF)
        dgpost_ref[...] += jnp.sum(dgp2, axis=0, keepdims=True)

    return pl.pallas_call(
        body, name="mixer_out_bwd", grid=(T // tr,),
        in_specs=[_row_spec(tr, D_MODEL)] * 4 + [_vec_spec(D_MODEL)] * 2,
        out_specs=[_row_spec(tr, D_MODEL), _row_spec(tr, D_MODEL), _vec_spec(D_MODEL), _vec_spec(D_MODEL)],
        out_shape=[jax.ShapeDtypeStruct((T, D_MODEL), F32), jax.ShapeDtypeStruct((T, D_MODEL), BF),
                   jax.ShapeDtypeStruct((1, D_MODEL), F32), jax.ShapeDtypeStruct((1, D_MODEL), F32)],
        compiler_params=_params("arbitrary"),
    )(h1, dxn2, dy, o, g_pre, g_post)


def _input_norm_bwd(x, dxn, dh1, g, *, tr=256):
    T = x.shape[0]
    tr = min(tr, T)

    def body(x_ref, dxn_ref, dh1_ref, g_ref, dx_ref, dg_ref):
        @pl.when(pl.program_id(0) == 0)
        def _():
            dg_ref[...] = jnp.zeros_like(dg_ref)

        da, dgp = _rms_bwd(x_ref[...], g_ref[...], dxn_ref[...])
        dx_ref[...] = dh1_ref[...] + da
        dg_ref[...] += jnp.sum(dgp, axis=0, keepdims=True)

    return pl.pallas_call(
        body, name="input_norm_bwd", grid=(T // tr,),
        in_specs=[_row_spec(tr, D_MODEL)] * 3 + [_vec_spec(D_MODEL)],
        out_specs=[_row_spec(tr, D_MODEL), _vec_spec(D_MODEL)],
        out_shape=[jax.ShapeDtypeStruct((T, D_MODEL), F32), jax.ShapeDtypeStruct((1, D_MODEL), F32)],
        compiler_params=_params("arbitrary"),
    )(x, dxn, dh1, g)


def _gate_bwd(dm, a, b, gl, *, tr=256):
    T = dm.shape[0]
    tr = min(tr, T)

    def body(dm_ref, a_ref, b_ref, gla_ref, glb_ref, da_ref, db_ref, dgla_ref, dglb_ref):
        dmv = dm_ref[...]
        ga = jax.nn.sigmoid(gla_ref[...].astype(F32))
        gb = jax.nn.sigmoid(glb_ref[...].astype(F32))
        da_ref[...] = (dmv * ga).astype(BF)
        db_ref[...] = (dmv * gb).astype(BF)
        dgla_ref[...] = (dmv * a_ref[...].astype(F32) * (ga * (1.0 - ga))).astype(BF)
        dglb_ref[...] = (dmv * b_ref[...].astype(F32) * (gb * (1.0 - gb))).astype(BF)

    spec = _row_spec(tr, D_MODEL)
    spec_b = pl.BlockSpec((tr, D_MODEL), lambda i: (i, 1))
    da, db, dgla, dglb = pl.pallas_call(
        body, name="gate_bwd", grid=(T // tr,),
        in_specs=[spec, spec, spec, spec, spec_b], out_specs=[spec] * 4,
        out_shape=[jax.ShapeDtypeStruct((T, D_MODEL), BF)] * 4, compiler_params=_params("parallel"),
    )(dm, a, b, gl, gl)
    return da, db, dgla, dglb


def _sgu_norm(z_tile, g, b):
    gz = _gelu(z_tile)
    u, vv = gz[:, :SGU_W], gz[:, SGU_W:]
    xc = vv - jnp.mean(vv, axis=-1, keepdims=True)
    rstd = lax.rsqrt(jnp.mean(xc * xc, axis=-1, keepdims=True) + EPS)
    xhat = xc * rstd
    return u, xhat, rstd, xhat * g + b


def _sgu_mix(w_ref, v_bf, first_half):
    parts = []
    for p in range(N_GROUPS // 2):
        vp = v_bf[:, p * LANES:(p + 1) * LANES]
        parts.append(jnp.where(first_half, _dot(w_ref[2 * p], vp), _dot(w_ref[2 * p + 1], vp)))
    return jnp.concatenate(parts, axis=1)


def _sgu_fwd(z, g_sgu, b_sgu, ws, bias_plane, *, tm=512):
    T = z.shape[0]
    tm = min(tm, T)

    def body(z_ref, g_ref, b_ref, ws_ref, bp_ref, y_ref):
        u, _, _, vn = _sgu_norm(z_ref[...], g_ref[...], b_ref[...])
        vn_bf = vn.astype(BF)
        first_half = lax.broadcasted_iota(jnp.int32, (CHUNK, LANES), 1) < HEAD_DIM
        for c in range(tm // CHUNK):
            rows = slice(c * CHUNK, (c + 1) * CHUNK)
            s = _sgu_mix(ws_ref, vn_bf[rows, :], first_half) + bp_ref[...]
            y_ref[rows, :] = (u[rows, :] * s).astype(BF)

    return pl.pallas_call(
        body, name="sgu_fwd", grid=(T // tm,),
        in_specs=[_row_spec(tm, 2 * SGU_W), _vec_spec(SGU_W), _vec_spec(SGU_W),
                  pl.BlockSpec((N_GROUPS, CHUNK, CHUNK), lambda i: (0, 0, 0)),
                  pl.BlockSpec((CHUNK, SGU_W), lambda i: (0, 0))],
        out_specs=_row_spec(tm, SGU_W), out_shape=jax.ShapeDtypeStruct((T, SGU_W), BF),
        compiler_params=_params("parallel"),
    )(z, g_sgu, b_sgu, ws, bias_plane)


def _sgu_bwd(dy, z, g_sgu, b_sgu, ws, ws_t, bias_plane, *, tm=512):
    T = z.shape[0]
    tm = min(tm, T)
    n_steps = T // tm

    def body(dy_ref, z_ref, g_ref, b_ref, ws_ref, wst_ref, bp_ref, dz_ref, dws_ref, dbs_ref, dg_ref, db_ref, dbp_ref):
        step = pl.program_id(0)

        @pl.when(step == 0)
        def _():
            dws_ref[...] = jnp.zeros_like(dws_ref)
            dg_ref[...] = jnp.zeros_like(dg_ref)
            db_ref[...] = jnp.zeros_like(db_ref)
            dbp_ref[...] = jnp.zeros_like(dbp_ref)

        g = g_ref[...]
        zt = z_ref[...]
        u, xhat, rstd, vn = _sgu_norm(zt, g, b_ref[...])
        vn_bf = vn.astype(BF)
        first_half = lax.broadcasted_iota(jnp.int32, (CHUNK, LANES), 1) < HEAD_DIM
        dyv = dy_ref[...]
        dg_acc = jnp.zeros((1, SGU_W), F32)
        db_acc = jnp.zeros((1, SGU_W), F32)
        for c in range(tm // CHUNK):
            rows = slice(c * CHUNK, (c + 1) * CHUNK)
            v_c = vn_bf[rows, :]
            s = _sgu_mix(ws_ref, v_c, first_half) + bp_ref[...]
            dy_c = dyv[rows, :]
            du = dy_c * s
            dsv = dy_c * u[rows, :]
            dbp_ref[...] += dsv
            ds_bf = dsv.astype(BF)
            zero = jnp.zeros((CHUNK, LANES), BF)
            for p in range(N_GROUPS // 2):
                dsp = ds_bf[:, p * LANES:(p + 1) * LANES]
                vp = v_c[:, p * LANES:(p + 1) * LANES]
                dws_ref[2 * p] += _dot_nt(jnp.where(first_half, dsp, zero), vp)
                dws_ref[2 * p + 1] += _dot_nt(jnp.where(first_half, zero, dsp), vp)
            dvn = _sgu_mix(wst_ref, ds_bf, first_half)
            xh = xhat[rows, :]
            dxh = dvn * g
            dvv = rstd[rows, :] * (dxh - jnp.mean(dxh, axis=-1, keepdims=True)
                                   - xh * jnp.mean(dxh * xh, axis=-1, keepdims=True))
            dg_acc += jnp.sum(dvn * xh, axis=0, keepdims=True)
            db_acc += jnp.sum(dvn, axis=0, keepdims=True)
            dgz = jnp.concatenate([du, dvv], axis=1)
            dz_ref[rows, :] = (dgz * _gelu_grad(zt[rows, :])).astype(BF)
        dg_ref[...] += dg_acc
        db_ref[...] += db_acc

        @pl.when(step == n_steps - 1)
        def _():
            r = lax.broadcasted_iota(jnp.int32, (CHUNK, CHUNK), 0)
            cidx = lax.broadcasted_iota(jnp.int32, (CHUNK, CHUNK), 1)
            causal = (cidx <= r).astype(F32)
            for gi in range(N_GROUPS):
                dws_ref[gi] = dws_ref[gi] * causal
            lane = lax.broadcasted_iota(jnp.int32, (CHUNK, LANES), 1)
            out = jnp.zeros((CHUNK, LANES), F32)
            dbp = dbp_ref[...]
            for gi in range(N_GROUPS):
                col = jnp.sum(dbp[:, gi * HEAD_DIM:(gi + 1) * HEAD_DIM], axis=1, keepdims=True)
                out = jnp.where(lane == gi, col, out)
            dbs_ref[...] = out

    w_spec = pl.BlockSpec((N_GROUPS, CHUNK, CHUNK), lambda i: (0, 0, 0))
    plane = pl.BlockSpec((CHUNK, SGU_W), lambda i: (0, 0))
    return pl.pallas_call(
        body, name="sgu_bwd", grid=(n_steps,),
        in_specs=[_row_spec(tm, SGU_W), _row_spec(tm, 2 * SGU_W), _vec_spec(SGU_W), _vec_spec(SGU_W), w_spec, w_spec, plane],
        out_specs=[_row_spec(tm, 2 * SGU_W), w_spec, pl.BlockSpec((CHUNK, LANES), lambda i: (0, 0)),
                   _vec_spec(SGU_W), _vec_spec(SGU_W)],
        out_shape=[jax.ShapeDtypeStruct((T, 2 * SGU_W), BF), jax.ShapeDtypeStruct((N_GROUPS, CHUNK, CHUNK), F32),
                   jax.ShapeDtypeStruct((CHUNK, LANES), F32), jax.ShapeDtypeStruct((1, SGU_W), F32),
                   jax.ShapeDtypeStruct((1, SGU_W), F32)],
        scratch_shapes=[pltpu.VMEM((CHUNK, SGU_W), F32)],
        compiler_params=_params("arbitrary"),
    )(dy, z, g_sgu, b_sgu, ws, ws_t, bias_plane)


def _tri(n, upper):
    r = lax.broadcasted_iota(jnp.int32, (n, n), 0)
    c = lax.broadcasted_iota(jnp.int32, (n, n), 1)
    return ((c >= r) if upper else (c <= r)).astype(BF)


def _scan_dot(tri, x):
    hi, mid, lo = _split3(x)
    return (_dot(tri, hi.astype(BF)) + _dot(tri, mid.astype(BF))) + _dot(tri, lo.astype(BF))


def _with_lanes(base, lane, start, cols):
    out = base
    for k, col in enumerate(cols):
        if col is not None:
            out = jnp.where(lane == start + k, col, out)
    return out


def _logit_bound(q_norm, k_norm):
    return NORM_SLACK * q_norm * k_norm + 1.0


ATTN_TILE = 512
SKIP_BELOW = -110.0
NORM_SLACK = 1.001
BOUNDED_GAP = 60.0


def _attn_prep(qkv, fl, b_forget, *, tp=ATTN_TILE):
    T = qkv.shape[0]
    tp = min(tp, T)

    def body(qkv_ref, fl_ref, bf_ref, qf_ref, kl_ref, vl_ref, st_ref, carry_ref, kmax_ref):
        @pl.when(pl.program_id(0) == 0)
        def _():
            carry_ref[...] = jnp.zeros_like(carry_ref)
            kmax_ref[...] = jnp.zeros_like(kmax_ref)

        x = fl_ref[...] + bf_ref[...]
        logf = jnp.minimum(x, 0.0) - jnp.log(1.0 + jnp.exp(-jnp.abs(x)))
        cum = _scan_dot(_tri(tp, upper=False), logf) + carry_ref[...]
        carry_ref[...] = cum[tp - 1:tp, :]
        lane = lax.broadcasted_iota(jnp.int32, (tp, HEAD_DIM), 1)
        ones3 = jnp.where(lane < 3, 1.0, 0.0)
        qkvv = qkv_ref[...]
        st_row = lax.broadcasted_iota(jnp.int32, (N_HEADS, LANES), 0)
        st_lane = lax.broadcasted_iota(jnp.int32, (N_HEADS, LANES), 1)
        stats = jnp.zeros((N_HEADS, LANES), F32)
        kmax_lane = lax.broadcasted_iota(jnp.int32, (1, LANES), 1)
        for h in range(N_HEADS):
            ch = cum[:, h:h + 1]
            c3 = _split3(ch)
            qh = qkvv[:, h * HEAD_DIM:(h + 1) * HEAD_DIM].astype(F32) * Q_SCALE
            kh = qkvv[:, ATTN_W + h * HEAD_DIM:ATTN_W + (h + 1) * HEAD_DIM].astype(F32)
            vh = qkvv[:, 2 * ATTN_W + h * HEAD_DIM:2 * ATTN_W + (h + 1) * HEAD_DIM].astype(F32)
            q_norm = jnp.sqrt(jnp.sum(qh * qh, axis=1, keepdims=True))
            qn = jnp.max(q_norm, axis=0, keepdims=True)
            kn = jnp.sqrt(jnp.max(jnp.sum(kh * kh, axis=1, keepdims=True), axis=0, keepdims=True))
            k_seen = jnp.maximum(kmax_ref[:, h:h + 1], kn)
            kmax_ref[...] = jnp.where(kmax_lane == h, k_seen, kmax_ref[...])
            bound3 = _split3(-_logit_bound(q_norm, k_seen))
            ext_q = _with_lanes(jnp.where((lane >= 3) & (lane < 6), 1.0, 0.0), lane, 0, list(c3) + [None] * 3 + list(bound3))
            ext_k = _with_lanes(jnp.where((lane < 3) | ((lane >= 6) & (lane < 9)), 1.0, 0.0), lane, 3, [-c for c in c3])
            qf_ref[h] = jnp.concatenate([qh, ext_q], axis=1).astype(BF)
            kl_ref[h] = jnp.concatenate([kh, ext_k], axis=1).astype(BF)
            vl_ref[h] = jnp.concatenate([vh, ones3], axis=1).astype(BF)
            tile_stats = (qn, kn, jnp.max(ch, axis=0, keepdims=True), jnp.min(ch, axis=0, keepdims=True), k_seen)
            for k, val in enumerate(tile_stats):
                stats = jnp.where((st_row == h) & (st_lane == k), val, stats)
        st_ref[0] = stats

    head_spec = pl.BlockSpec((N_HEADS, tp, LANES), lambda i: (0, i, 0))
    return pl.pallas_call(
        body, name="attn_prep", grid=(T // tp,),
        in_specs=[_row_spec(tp, 3 * ATTN_W), _row_spec(tp, LANES), _vec_spec(LANES)],
        out_specs=[head_spec] * 3 + [pl.BlockSpec((1, N_HEADS, LANES), lambda i: (i, 0, 0))],
        out_shape=[jax.ShapeDtypeStruct((N_HEADS, T, LANES), BF)] * 3 + [jax.ShapeDtypeStruct((T // tp, N_HEADS, LANES), F32)],
        scratch_shapes=[pltpu.VMEM((1, LANES), F32), pltpu.VMEM((1, LANES), F32)], compiler_params=_params("arbitrary"),
    )(qkv, fl, b_forget)


def _attn_ranges(stats):
    qn, kn, cmax, cmin, k_seen = (stats[:, :, k].T for k in range(5))
    n = qn.shape[1]
    bounded = (2.0 * _logit_bound(qn, k_seen) <= BOUNDED_GAP).reshape(N_HEADS // 2, 2, n).all(axis=1)
    reach = NORM_SLACK * qn * (jnp.max(kn, axis=1, keepdims=True) + kn) + cmax
    i = jnp.arange(n)[None, :, None]
    j = jnp.arange(n)[None, None, :]
    need = ((reach[:, :, None] - cmin[:, None, :] >= SKIP_BELOW) | (i == j)) & (j <= i)
    first = jnp.min(jnp.where(need, j, n), axis=2).reshape(N_HEADS // 2, 2, n).min(axis=1)
    last = jnp.max(jnp.where(need, i, -1), axis=1).reshape(N_HEADS // 2, 2, n).max(axis=1)
    return first.reshape(-1).astype(F32), last.reshape(-1).astype(F32), bounded.reshape(-1).astype(F32)


def _pair_block(t):
    return pl.BlockSpec((2, t, LANES), lambda p, i, *_: (p, i, 0))


def _pair_full(T):
    return pl.BlockSpec((2, T, LANES), lambda p, i, *_: (p, 0, 0))


def _packed_block(t):
    return pl.BlockSpec((t, LANES), lambda p, i, *_: (i, p))


def _causal(t, keys_in_rows=False):
    r = lax.broadcasted_iota(jnp.int32, (t, t), 0)
    c = lax.broadcasted_iota(jnp.int32, (t, t), 1)
    return (r <= c) if keys_in_rows else (c <= r)


def _tile_rows(j, t):
    return pl.ds(pl.multiple_of(j * t, t), t)


def _attn_call(body, name, tile_scalars, operands, in_specs, out_specs, out_shape, scratch_shapes, n_tiles):
    return pl.pallas_call(
        body, name=name,
        grid_spec=pltpu.PrefetchScalarGridSpec(
            num_scalar_prefetch=len(tile_scalars), grid=(N_HEADS // 2, n_tiles), in_specs=in_specs, out_specs=out_specs,
            scratch_shapes=scratch_shapes),
        out_shape=out_shape, compiler_params=_params("arbitrary", "arbitrary"),
    )(*tile_scalars, *operands)


def _attn_fwd(qf, kl, vl, first, bounded, shards, *, tq=ATTN_TILE):
    T = qf.shape[1]
    tq = min(tq, T)
    n = T // tq
    n_steps = (N_HEADS // 2) * n
    k = len(shards)

    def body(first_ref, bounded_ref, qf_ref, kl_ref, vl_ref, *refs):
        w_refs, (o_ref, of_ref, ql_ref), g_refs = refs[:k], refs[k:k + 3], refs[k + 3:2 * k + 3]
        m_ref, acc_ref, send_sems, recv_sems = refs[2 * k + 3:]
        i = pl.program_id(1)
        tile = pl.program_id(0) * n + i
        gather_start, gather_forward, gather_finish = _gather_phases(w_refs, g_refs, send_sems, recv_sems)
        pl.when(tile == 0)(gather_start)
        pl.when(tile == (3 * n_steps) // 4)(gather_forward)
        start = first_ref[tile].astype(jnp.int32)
        is_bounded = bounded_ref[tile] > 0.5
        acc_ref[...] = jnp.zeros_like(acc_ref)
        diagonal = _tile_rows(i, tq)
        causal = _causal(tq)

        def logits(hh, rows):
            return _dot_nt(qf_ref[hh], kl_ref[hh, rows, :])

        @pl.when(is_bounded)
        def _():
            m_ref[...] = jnp.zeros_like(m_ref)

            def update(hh, s, rows):
                acc_ref[hh] += _dot(jnp.exp(s).astype(BF), vl_ref[hh, rows, :])

            def step(j, carry):
                for hh in range(2):
                    update(hh, logits(hh, _tile_rows(j, tq)), _tile_rows(j, tq))
                return carry

            lax.fori_loop(start, i, step, 0)
            for hh in range(2):
                update(hh, jnp.where(causal, logits(hh, diagonal), NEG), diagonal)

        @pl.when(jnp.logical_not(is_bounded))
        def _():
            m_ref[...] = jnp.full_like(m_ref, NEG)

            def update(hh, s, rows):
                m_old = m_ref[hh]
                m_new = jnp.maximum(m_old, jnp.max(s, axis=1, keepdims=True))
                p = jnp.exp(s - m_new)
                acc_ref[hh] = jnp.exp(m_old - m_new) * acc_ref[hh] + _dot(p.astype(BF), vl_ref[hh, rows, :])
                m_ref[hh] = m_new

            def step(j, carry):
                for hh in range(2):
                    update(hh, logits(hh, _tile_rows(j, tq)), _tile_rows(j, tq))
                return carry

            lax.fori_loop(start, i, step, 0)
            for hh in range(2):
                update(hh, jnp.where(causal, logits(hh, diagonal), NEG), diagonal)

        lane = lax.broadcasted_iota(jnp.int32, (tq, LANES), 1)
        outs = []
        for hh in range(2):
            q = qf_ref[hh].astype(F32)
            acc = acc_ref[hh]
            l = acc[:, HEAD_DIM:HEAD_DIM + 1]
            outs.append(acc[:, :HEAD_DIM] / l)
            at = HEAD_DIM + 6
            neg_bound = (q[:, at:at + 1] + q[:, at + 1:at + 2]) + q[:, at + 2:at + 3]
            ql_ref[hh] = _with_lanes(q, lane, at, _split3(neg_bound - (m_ref[hh] + jnp.log(l)))).astype(BF)
        o = jnp.concatenate(outs, axis=1)
        o_ref[...] = o.astype(BF)
        of_ref[...] = o
        pl.when(tile == n_steps - 1)(gather_finish)

    outs = _attn_call(
        body, "attn_fwd", (first, bounded), (qf, kl, vl, *shards),
        [_pair_block(tq), _pair_full(T), _pair_full(T)] + [HBM] * k,
        [_packed_block(tq), _packed_block(tq), _pair_block(tq)] + [HBM] * k,
        [jax.ShapeDtypeStruct((T, ATTN_W), BF), jax.ShapeDtypeStruct((T, ATTN_W), F32),
         jax.ShapeDtypeStruct((N_HEADS, T, LANES), BF)] + _gathered_shapes(shards),
        [pltpu.VMEM((2, tq, 1), F32), pltpu.VMEM((2, tq, LANES), F32)] + _gather_semaphores(k), n)
    return outs[0], outs[1], outs[2], outs[3:]


def _attn_bwd_prep(dya, of, *, tr=256):
    T = dya.shape[0]
    tr = min(tr, T)

    def body(d_ref, o_ref, do_ref):
        lane = lax.broadcasted_iota(jnp.int32, (tr, HEAD_DIM), 1)
        dv, ov = d_ref[...], o_ref[...]
        for h in range(N_HEADS):
            d = dv[:, h * HEAD_DIM:(h + 1) * HEAD_DIM]
            delta = jnp.sum(d * ov[:, h * HEAD_DIM:(h + 1) * HEAD_DIM], axis=1, keepdims=True)
            ext = _with_lanes(jnp.zeros((tr, HEAD_DIM), F32), lane, 0, _split3(-delta))
            do_ref[h] = jnp.concatenate([d, ext], axis=1).astype(BF)

    return pl.pallas_call(
        body, name="attn_bwd_prep", grid=(T // tr,),
        in_specs=[_row_spec(tr, ATTN_W), _row_spec(tr, ATTN_W)],
        out_specs=pl.BlockSpec((N_HEADS, tr, LANES), lambda i: (0, i, 0)),
        out_shape=jax.ShapeDtypeStruct((N_HEADS, T, LANES), BF), compiler_params=_params("parallel"),
    )(dya, of)


def _attn_bwd(kl, vl, ql, do, last, chip_sums, *, tk=ATTN_TILE):
    T = ql.shape[1]
    tk = min(tk, T)
    n = T // tk
    n_steps = (N_HEADS // 2) * n
    m = len(chip_sums)

    def body(last_ref, kl_ref, vl_ref, ql_ref, do_ref, *refs):
        b_refs, (dq_ref, dk_ref, dv_ref, extq_ref, extk_ref), r_refs = refs[:m], refs[m:m + 5], refs[m + 5:2 * m + 5]
        dq_acc, dk_acc, dv_acc, send_sems, recv_sems = refs[2 * m + 5:]
        j = pl.program_id(1)
        tile = pl.program_id(0) * n + j
        scatter_start, scatter_finish = _scatter_phases(b_refs, r_refs, send_sems, recv_sems)
        pl.when(tile == 0)(scatter_start)

        @pl.when(j == 0)
        def _():
            dq_acc[...] = jnp.zeros_like(dq_acc)

        dk_acc[...] = jnp.zeros_like(dk_acc)
        dv_acc[...] = jnp.zeros_like(dv_acc)

        def block(hh, rows, mask):
            qi, di, k = ql_ref[hh, rows, :], do_ref[hh, rows, :], kl_ref[hh]
            p_t = jnp.exp(_dot_nt(k, qi))
            if mask is not None:
                p_t = jnp.where(mask, p_t, 0.0)
            ds_t = (p_t * _dot_nt(vl_ref[hh], di)).astype(BF)
            dk_acc[hh] += _dot(ds_t, qi)
            dv_acc[hh] += _dot(p_t.astype(BF), di)
            dq_acc[hh, rows, :] += _dot_tn(ds_t, k)

        causal_t = _causal(tk, keys_in_rows=True)
        for hh in range(2):
            block(hh, _tile_rows(j, tk), causal_t)

        def step(i, carry):
            for hh in range(2):
                block(hh, _tile_rows(i, tk), None)
            return carry

        lax.fori_loop(j + 1, last_ref[pl.program_id(0) * n + j].astype(jnp.int32) + 1, step, 0)
        dk_ref[...] = jnp.concatenate([dk_acc[hh][:, :HEAD_DIM] for hh in range(2)], axis=1).astype(BF)
        dv_ref[...] = jnp.concatenate([dv_acc[hh][:, :HEAD_DIM] for hh in range(2)], axis=1).astype(BF)
        extk_ref[...] = jnp.concatenate([dk_acc[hh][:, HEAD_DIM:] for hh in range(2)], axis=1)

        @pl.when(j == n - 1)
        def _():
            dq_ref[...] = jnp.concatenate([dq_acc[hh][:, :HEAD_DIM] * Q_SCALE for hh in range(2)], axis=1).astype(BF)
            extq_ref[...] = jnp.concatenate([dq_acc[hh][:, HEAD_DIM:] for hh in range(2)], axis=1)

        pl.when(tile == n_steps - 1)(scatter_finish)

    whole = pl.BlockSpec((T, LANES), lambda p, j, *_: (0, p))
    outs = pl.pallas_call(
        body, name="attn_bwd",
        grid_spec=pltpu.PrefetchScalarGridSpec(
            num_scalar_prefetch=1, grid=(N_HEADS // 2, n),
            in_specs=[_pair_block(tk), _pair_block(tk), _pair_full(T), _pair_full(T)] + [HBM] * m,
            out_specs=[whole, _packed_block(tk), _packed_block(tk), whole, _packed_block(tk)] + [HBM] * m,
            scratch_shapes=[pltpu.VMEM((2, T, LANES), F32), pltpu.VMEM((2, tk, LANES), F32), pltpu.VMEM((2, tk, LANES), F32)]
            + _scatter_semaphores(m)),
        out_shape=[jax.ShapeDtypeStruct((T, ATTN_W), BF)] * 3 + [jax.ShapeDtypeStruct((T, ATTN_W), F32)] * 2
        + _scattered_shapes(chip_sums),
        compiler_params=pltpu.CompilerParams(dimension_semantics=("arbitrary", "arbitrary"), vmem_limit_bytes=ATTN_BWD_VMEM),
    )(last, kl, vl, ql, do, *chip_sums)
    return outs[:5], outs[5:]


def _forget_bwd(ext_q, ext_k, fl, b_forget, *, tp=256):
    T = fl.shape[0]
    tp = min(tp, T)
    n = T // tp

    def body(eq_ref, ek_ref, fl_ref, bf_ref, dfl_ref, dbf_ref, carry_ref):
        @pl.when(pl.program_id(0) == 0)
        def _():
            carry_ref[...] = jnp.zeros_like(carry_ref)
            dbf_ref[...] = jnp.zeros_like(dbf_ref)

        lane = lax.broadcasted_iota(jnp.int32, (tp, LANES), 1)
        eq, ek = eq_ref[...], ek_ref[...]
        cols = [eq[:, h * HEAD_DIM:h * HEAD_DIM + 1] - ek[:, h * HEAD_DIM + 3:h * HEAD_DIM + 4] for h in range(N_HEADS)]
        dcum = _with_lanes(jnp.zeros((tp, LANES), F32), lane, 0, cols)
        suffix = _scan_dot(_tri(tp, upper=True), dcum) + carry_ref[...]
        carry_ref[...] = suffix[0:1, :]
        x = fl_ref[...] + bf_ref[...]
        dfl = jnp.where(lane < N_HEADS, suffix / (1.0 + jnp.exp(x)), 0.0)
        dfl_ref[...] = dfl.astype(BF)
        dbf_ref[...] += jnp.sum(dfl, axis=0, keepdims=True)

    rev = lambda w: pl.BlockSpec((tp, w), lambda i: (n - 1 - i, 0))
    return pl.pallas_call(
        body, name="forget_bwd", grid=(n,),
        in_specs=[rev(ATTN_W), rev(ATTN_W), rev(LANES), _vec_spec(LANES)],
        out_specs=[rev(LANES), _vec_spec(LANES)],
        out_shape=[jax.ShapeDtypeStruct((T, LANES), BF), jax.ShapeDtypeStruct((1, LANES), F32)],
        scratch_shapes=[pltpu.VMEM((1, LANES), F32)], compiler_params=_params("arbitrary"),
    )(ext_q, ext_k, fl, b_forget)


def _adamw(w, g, m, v, *, name, tr=256):
    _, rows, cols = w.shape
    tr = tr if rows % tr == 0 else rows

    def body(w_ref, g_ref, m_ref, v_ref, go_ref, d_ref, nm_ref, nv_ref):
        gv = g_ref[...]
        go_ref[...] = gv
        nm = ADAM_B1 * m_ref[...] + (1.0 - ADAM_B1) * gv
        nv = ADAM_B2 * v_ref[...] + (1.0 - ADAM_B2) * (gv * gv)
        m_hat = nm / (1.0 - ADAM_B1 ** ADAM_STEP)
        v_hat = nv / (1.0 - ADAM_B2 ** ADAM_STEP)
        d_ref[...] = -ADAM_LR * (m_hat / (jnp.sqrt(v_hat) + ADAM_EPS) + ADAM_WD * w_ref[...])
        nm_ref[...] = nm
        nv_ref[...] = nv

    spec = pl.BlockSpec((None, tr, cols), lambda i: (0, i, 0))
    return pl.pallas_call(
        body, name=name, grid=(rows // tr,), in_specs=[spec, pl.BlockSpec((tr, cols), lambda i: (i, 0)), spec, spec],
        out_specs=[spec] * 4, out_shape=[jax.ShapeDtypeStruct((1, rows, cols), F32)] * 4,
        compiler_params=_params("parallel"),
    )(w, g, m, v)


HBM = pl.BlockSpec(memory_space=pltpu.HBM)
BF16_ROWS = 16


def _place():
    x, y, c = lax.axis_index("x"), lax.axis_index("y"), lax.axis_index("c")
    others = [(1 - x, y), (x, 1 - y), (1 - x, 1 - y)]
    return x, y, c, others


def _chip(xy):
    return 2 * xy[0] + xy[1]


def _row_halves(c, rows):
    half = rows // 2
    assert half % BF16_ROWS == 0
    return (pl.ds(pl.multiple_of(c * half, BF16_ROWS), half), pl.ds(pl.multiple_of((1 - c) * half, BF16_ROWS), half))


def _remote(src, dst, send_sems, recv_sems, k, to):
    return pltpu.make_async_remote_copy(src_ref=src, dst_ref=dst, send_sem=send_sems.at[k], recv_sem=recv_sems.at[k],
                                        device_id=to, device_id_type=MESH)


def _gather_weights(shards):
    n = len(shards)

    def body(*refs):
        for phase in _gather_phases(refs[:n], refs[n:2 * n], *refs[2 * n:]):
            phase()

    return pl.pallas_call(
        body, name="gather_weights", in_specs=[HBM] * n, out_specs=[HBM] * n,
        out_shape=_gathered_shapes(shards), scratch_shapes=_gather_semaphores(n),
    )(*shards)


def _gathered_shapes(shards):
    return [jax.ShapeDtypeStruct((N_CHIPS,) + s.shape, s.dtype) for s in shards]


def _gather_semaphores(n):
    return [pltpu.SemaphoreType.DMA((6 * n,)), pltpu.SemaphoreType.DMA((6 * n,))]


def _gather_phases(w_refs, g_refs, send_sems, recv_sems):
    n = len(w_refs)
    x, y, c, others = _place()
    sibling, me = (x, y, 1 - c), _chip((x, y))
    halves = [_row_halves(c, w.shape[0]) for w in w_refs]

    def sent(a, j, o):
        mine, _ = halves[a]
        return _remote(w_refs[a].at[mine, :], g_refs[a].at[me, mine, :], send_sems, recv_sems, 6 * a + j, (*o, c))

    def passed(a, j, o):
        landed = g_refs[a].at[_chip(o), halves[a][0], :]
        return _remote(landed, landed, send_sems, recv_sems, 6 * a + 3 + j, sibling)

    def start():
        for a in range(n):
            for j, o in enumerate(others):
                sent(a, j, o).start()

    def forward():
        for j, o in enumerate(others):
            for a in range(n):
                landed = g_refs[a].at[_chip(o), halves[a][0], :]
                _remote(landed, landed, send_sems, recv_sems, 6 * a + j, (*o, c)).wait_recv()
                passed(a, j, o).start()

    def finish():
        for j, o in enumerate(others):
            for a in range(n):
                landed = g_refs[a].at[_chip(o), halves[a][1], :]
                _remote(landed, landed, send_sems, recv_sems, 6 * a + 3 + j, sibling).wait_recv()
        for a in range(n):
            for j, o in enumerate(others):
                sent(a, j, o).wait_send()
                passed(a, j, o).wait_send()

    return start, forward, finish


def _exchange_halves(arrays, *, name):
    n = len(arrays)

    def body(*refs):
        g_refs, r_refs, (send_sems, recv_sems) = refs[:n], refs[n:2 * n], refs[2 * n:]
        x, y, c, _ = _place()
        copies = []
        for a in range(n):
            _, theirs = _row_halves(c, g_refs[a].shape[-2])
            src = g_refs[a].at[:, theirs, :] if len(g_refs[a].shape) == 3 else g_refs[a].at[theirs, :]
            copies.append(_remote(src, r_refs[a], send_sems, recv_sems, a, (x, y, 1 - c)))
            copies[-1].start()
        for cp in copies:
            cp.wait()

    def half(s):
        return jax.ShapeDtypeStruct(s.shape[:-2] + (s.shape[-2] // 2, s.shape[-1]), F32)

    return pl.pallas_call(
        body, name=name, in_specs=[HBM] * n, out_specs=[HBM] * n, out_shape=[half(g) for g in arrays],
        scratch_shapes=[pltpu.SemaphoreType.DMA((n,)), pltpu.SemaphoreType.DMA((n,))],
    )(*arrays)


def _scatter_to_owners(chip_sums):
    n = len(chip_sums)

    def body(*refs):
        for phase in _scatter_phases(refs[:n], refs[n:2 * n], *refs[2 * n:]):
            phase()

    return pl.pallas_call(
        body, name="scatter_to_owners", in_specs=[HBM] * n, out_specs=[HBM] * n,
        out_shape=_scattered_shapes(chip_sums), scratch_shapes=_scatter_semaphores(n),
    )(*chip_sums)


def _scattered_shapes(chip_sums):
    return [jax.ShapeDtypeStruct(b.shape if b.ndim == 3 else (N_CHIPS,) + b.shape, b.dtype) for b in chip_sums]


def _scatter_semaphores(n):
    return [pltpu.SemaphoreType.DMA((3 * n,)), pltpu.SemaphoreType.DMA((3 * n,))]


def _scatter_phases(b_refs, r_refs, send_sems, recv_sems):
    n = len(b_refs)
    x, y, c, others = _place()
    me = _chip((x, y))

    def sent(a, j, o):
        src = b_refs[a].at[_chip(o)] if len(b_refs[a].shape) == 3 else b_refs[a]
        return _remote(src, r_refs[a].at[me], send_sems, recv_sems, 3 * a + j, (*o, c))

    def start():
        for a in range(n):
            for j, o in enumerate(others):
                sent(a, j, o).start()

    def finish():
        for a in range(n):
            for j, o in enumerate(others):
                landed = r_refs[a].at[_chip(o)]
                _remote(landed, landed, send_sems, recv_sems, 3 * a + j, (*o, c)).wait_recv()
        for a in range(n):
            for j, o in enumerate(others):
                sent(a, j, o).wait_send()

    return start, finish


def _join_halves(totals):
    n = len(totals)

    def body(*refs):
        in_refs, out_refs, (send_sems, recv_sems) = refs[:n], refs[n:2 * n], refs[2 * n:]
        x, y, c, _ = _place()
        copies = []
        for a in range(n):
            mine, _ = _row_halves(c, in_refs[a].shape[0])
            copies.append(_remote(in_refs[a].at[mine, :], out_refs[a].at[mine, :], send_sems, recv_sems, a, (x, y, 1 - c)))
            copies[-1].start()
        for cp in copies:
            cp.wait()

    return pl.pallas_call(
        body, name="join_halves", in_specs=[HBM] * n, out_specs=[HBM] * n,
        out_shape=[jax.ShapeDtypeStruct(t.shape, F32) for t in totals], input_output_aliases={a: a for a in range(n)},
        scratch_shapes=[pltpu.SemaphoreType.DMA((n,)), pltpu.SemaphoreType.DMA((n,))],
    )(*totals)


ADD_ROWS = 128


def _add_sibling(g, r, place, *, name):
    lead, (half, cols) = g.shape[:-2], r.shape[-2:]
    tr = min(ADD_ROWS, half)
    nb = half // tr
    zeros = (0,) * len(lead)

    def body(place_ref, g_ref, r_ref, o_ref, ob_ref):
        s = g_ref[...] + r_ref[...]
        o_ref[...] = s
        ob_ref[...] = s.astype(BF)

    spec = pl.BlockSpec(lead + (tr, cols), lambda i, p: zeros + (i, 0))
    return pl.pallas_call(
        body, name=name,
        grid_spec=pltpu.PrefetchScalarGridSpec(
            num_scalar_prefetch=1, grid=(nb,),
            in_specs=[pl.BlockSpec(lead + (tr, cols), lambda i, p: zeros + (p[1] * nb + i, 0)), spec], out_specs=[spec, spec]),
        out_shape=[jax.ShapeDtypeStruct(r.shape, F32), jax.ShapeDtypeStruct(r.shape, BF)],
        compiler_params=_params("parallel"),
    )(place, g, r)


def _add_chips(own, received, place, *, name, own_slots):
    half, cols = received.shape[-2:]
    tr = min(ADD_ROWS, half)
    nb = half // tr

    def written(k, p):
        return jnp.where(p[0] == k, (k + 1) % N_CHIPS, k)

    def body(place_ref, own_ref, *refs):
        o_ref = refs[N_CHIPS]
        mine = own_ref[0] if own_slots else own_ref[...]
        if own_slots:
            acc = mine
            for k in range(N_CHIPS):
                acc = acc + jnp.where(place_ref[0] == k, 0.0, refs[k][0].astype(F32))
        else:
            terms = [jnp.where(place_ref[0] == k, mine, refs[k][0]) for k in range(N_CHIPS)]
            acc = ((terms[0] + terms[1]) + terms[2]) + terms[3]
        o_ref[...] = acc

    own_spec = (pl.BlockSpec((1, tr, cols), lambda i, p: (p[0], i, 0)) if own_slots
                else pl.BlockSpec((tr, cols), lambda i, p: (i, 0)))
    return pl.pallas_call(
        body, name=name,
        grid_spec=pltpu.PrefetchScalarGridSpec(
            num_scalar_prefetch=1, grid=(nb,),
            in_specs=[own_spec] + [pl.BlockSpec((1, tr, cols), functools.partial(lambda i, p, k: (written(k, p), i, 0), k=k))
                                   for k in range(N_CHIPS)],
            out_specs=pl.BlockSpec((tr, cols), lambda i, p: (p[1] * nb + i, 0))),
        out_shape=jax.ShapeDtypeStruct((2 * half, cols), F32), compiler_params=_params("parallel"),
    )(place, own, *([received] * N_CHIPS))


SHARDED = (("w_in", (D_MODEL, 4616), 1), ("w_branch_sgu", (SGU_W, D_MODEL), 1), ("w_branch_attn", (ATTN_W, D_MODEL), 1),
           ("w_out", (D_MODEL, D_MODEL), 0), ("w_up", (D_MODEL, D_FF), 1), ("w_down", (D_FF, D_MODEL), 0))
SMALL = (("g_mix_pre", (1, D_MODEL)), ("b_forget", (1, N_HEADS)), ("g_sgu", (1, SGU_W)), ("b_sgu", (1, SGU_W)),
         ("w_spatial", (N_GROUPS * CHUNK, CHUNK)), ("b_spatial", (N_GROUPS, CHUNK)), ("g_mix_post", (1, D_MODEL)),
         ("g_ffn_pre", (1, D_MODEL)), ("g_ffn_post", (1, D_MODEL)))
SMALL_ALIGN = 2 * ADD_ROWS


def _shard_shape(shape, axis):
    return tuple(s // N_CHIPS if a == axis else s for a, s in enumerate(shape))


def _slots_to_full(slots, axis):
    return slots.reshape(-1, slots.shape[2]) if axis == 0 else slots.transpose(1, 0, 2).reshape(slots.shape[1], -1)


def _full_to_slots(full, axis):
    if axis == 0:
        return full.reshape(N_CHIPS, -1, full.shape[1])
    return full.reshape(full.shape[0], N_CHIPS, -1).transpose(1, 0, 2)


def _small_rows(shape):
    return -(-(shape[0] * shape[1]) // (8 * LANES)) * 8


def _pack_small(values):
    parts = []
    for name, shape in SMALL:
        flat = values[name].reshape(-1)
        n = _small_rows(shape)
        parts.append(jnp.pad(flat, (0, n * LANES - flat.shape[0])).reshape(n, LANES))
    rows = sum(p.shape[0] for p in parts)
    pad = -(-rows // SMALL_ALIGN) * SMALL_ALIGN - rows
    return jnp.concatenate(parts + [jnp.zeros((pad, LANES), F32)], axis=0)


def _unpack_small(packed):
    out, row = {}, 0
    for name, shape in SMALL:
        n = _small_rows(shape)
        out[name] = packed[row:row + n].reshape(-1)[:shape[0] * shape[1]].reshape(shape)
        row += n
    return out


IN_Z, IN_Q, IN_K, IN_V, IN_F, IN_G, IN_END = 0, 1024, 1536, 2048, 2560, 2568, 4616


LATE_WEIGHTS = ("w_branch_sgu", "w_branch_attn", "w_out", "w_up", "w_down")
EARLY_GRADS = LATE_WEIGHTS


def _assemble(name, shard, gathered, chip):
    axis = {n: a for n, _, a in SHARDED}[name]
    slot = jnp.arange(N_CHIPS)[:, None, None]
    return _slots_to_full(jnp.where(slot == chip, shard[None], gathered), axis)


def _local_step(x, target, w_in, shards, small, place):
    w_z, w_qkv, w_g = w_in[:, IN_Z:IN_Q], w_in[:, IN_Q:IN_F], w_in[:, IN_G:IN_END]
    w_q, w_k, w_v = w_in[:, IN_Q:IN_K], w_in[:, IN_K:IN_V], w_in[:, IN_V:IN_F]
    w_f = jnp.pad(w_in[:, IN_F:IN_G], ((0, 0), (0, LANES - N_HEADS)))
    b_forget = jnp.pad(small["b_forget"], ((0, 0), (0, LANES - N_HEADS)))
    causal = jnp.tril(jnp.ones((CHUNK, CHUNK), bool))
    ws = jnp.where(causal[None], small["w_spatial"].reshape(N_GROUPS, CHUNK, CHUNK), 0.0).astype(BF)
    ws_t = ws.transpose(0, 2, 1)
    bias_plane = jnp.repeat(small["b_spatial"].T, HEAD_DIM, axis=1)

    xn = _rms_fwd(x, small["g_mix_pre"])
    z = _matmul([(xn, w_z)], nt=False, out_dtypes=[F32], name="proj_z")
    qkv = _matmul([(xn, w_qkv)], nt=False, out_dtypes=[BF], name="proj_qkv")
    gl = _matmul([(xn, w_g)], nt=False, out_dtypes=[BF], name="proj_gate")
    fl = _matmul([(xn, w_f)], nt=False, out_dtypes=[F32], name="proj_forget")
    ysgu = _sgu_fwd(z, small["g_sgu"], small["b_sgu"], ws, bias_plane)
    qf, kl, vl, tile_stats = _attn_prep(qkv, fl, b_forget)
    first_key_tile, last_query_tile, bounded = _attn_ranges(tile_stats)
    yattn, yattn_f, ql, gathered = _attn_fwd(qf, kl, vl, first_key_tile, bounded, [shards[name] for name in LATE_WEIGHTS])
    w = {name: _assemble(name, shards[name], got, place[0]) for name, got in zip(LATE_WEIGHTS, gathered, strict=True)}
    a, b, merged = _branch_merge(ysgu, yattn, w["w_branch_sgu"], w["w_branch_attn"], gl)
    o = _matmul([(merged, w["w_out"])], nt=False, out_dtypes=[F32], name="proj_out")
    h1, xn2 = _mixer_out_fwd(o, x, small["g_mix_post"], small["g_ffn_pre"])

    def relu2(acc):
        r = jnp.maximum(acc, 0.0)
        return (r * r,)

    hid = _matmul([(xn2, w["w_up"])], nt=False, out_dtypes=[BF], name="ffn_up", epilogue=relu2)
    dn = _matmul([(hid, w["w_down"])], nt=False, out_dtypes=[F32], name="ffn_down")
    sq, dy, ddn, dg_ffn_post = _loss_head(dn, h1, target, small["g_ffn_post"])

    dup = _matmul([(ddn, w["w_down"])], nt=True, out_dtypes=[BF], name="ffn_down_bwd",
                  epilogue=lambda acc, h: (acc * (2.0 * jnp.sqrt(h.astype(F32))),), extras=[hid])
    dw_down = _matmul_tn(hid, ddn, name="dw_down")
    dxn2 = _matmul([(dup, w["w_up"])], nt=True, out_dtypes=[F32], name="ffn_up_bwd")
    dw_up = _matmul_tn(xn2, dup, name="dw_up", slots=True)
    dh1, do, dg_ffn_pre, dg_mix_post = _mixer_out_bwd(h1, dxn2, dy, o, small["g_ffn_pre"], small["g_mix_post"])

    dmerged = _matmul([(do, w["w_out"])], nt=True, out_dtypes=[F32], name="proj_out_bwd")
    dw_out = _matmul_tn(merged, do, name="dw_out")
    da, db, dgla, dglb = _gate_bwd(dmerged, a, b, gl)
    dysgu = _matmul([(da, w["w_branch_sgu"])], nt=True, out_dtypes=[F32], name="branch_sgu_bwd")
    dyattn = _matmul([(db, w["w_branch_attn"])], nt=True, out_dtypes=[F32], name="branch_attn_bwd")
    dw_bs = _matmul_tn(ysgu, da, name="dw_branch_sgu")
    dw_ba = _matmul_tn(yattn, db, name="dw_branch_attn")
    early = {"w_branch_sgu": _full_to_slots(dw_bs, 1), "w_branch_attn": _full_to_slots(dw_ba, 1),
             "w_out": _full_to_slots(dw_out, 0), "w_up": dw_up, "w_down": _full_to_slots(dw_down, 0)}
    early_theirs = _exchange_halves([early[name] for name in EARLY_GRADS], name="exchange_halves_early")
    early_sums = {name: _add_sibling(early[name], theirs, place, name="add_sibling_" + name)
                  for name, theirs in zip(EARLY_GRADS, early_theirs, strict=True)}
    dz, dws, dbs, dg_sgu, db_sgu = _sgu_bwd(dysgu, z, small["g_sgu"], small["b_sgu"], ws, ws_t, bias_plane)
    dout = _attn_bwd_prep(dyattn, yattn_f)
    (dq, dk, dv, ext_q, ext_k), early_received = _attn_bwd(
        kl, vl, ql, dout, last_query_tile, [early_sums[name][1] for name in EARLY_GRADS])
    dfl, dbf = _forget_bwd(ext_q, ext_k, fl, b_forget)
    dw_in = _full_to_slots(jnp.concatenate(
        [_matmul_tn(xn, dz, name="dw_in_z"), _matmul_tn(xn, dq, name="dw_in_q"), _matmul_tn(xn, dk, name="dw_in_k"),
         _matmul_tn(xn, dv, name="dw_in_v"), _matmul_tn(xn, dfl, name="dw_in_f")[:, :N_HEADS],
         _matmul_tn(xn, dgla, name="dw_in_ga"), _matmul_tn(xn, dglb, name="dw_in_gb")], axis=1), 1)
    (dw_in_theirs,) = _exchange_halves([dw_in], name="exchange_halves_w_in")
    dw_in_sum = _add_sibling(dw_in, dw_in_theirs, place, name="add_sibling_w_in")
    (dxn,), (dw_in_received,) = _matmul(
        [(dz, w_z), (dq, w_q), (dk, w_k), (dv, w_v), (dgla, w_g[:, :D_MODEL]), (dglb, w_g[:, D_MODEL:]), (dfl, w_f)],
        nt=True, out_dtypes=[F32], name="proj_in_bwd", scatter=[dw_in_sum[1]])
    dx, dg_mix_pre = _input_norm_bwd(x, dxn, dh1, small["g_mix_pre"])

    reduced = {name: (early_sums[name][0], got) for name, got in zip(EARLY_GRADS, early_received, strict=True)}
    reduced["w_in"] = (dw_in_sum[0], dw_in_received)
    small_grads = {"g_mix_pre": dg_mix_pre, "b_forget": dbf[:, :N_HEADS], "g_sgu": dg_sgu, "b_sgu": db_sgu,
                   "w_spatial": dws.reshape(N_GROUPS * CHUNK, CHUNK), "b_spatial": dbs[:, :N_GROUPS].T,
                   "g_mix_post": dg_mix_post, "g_ffn_pre": dg_ffn_pre, "g_ffn_post": dg_ffn_post}
    return sq, dx, reduced, small_grads


NAMES = ("g_mix_pre", "w_in", "b_forget", "g_sgu", "b_sgu", "w_spatial", "b_spatial", "w_branch_sgu", "w_branch_attn",
         "w_out", "g_mix_post", "g_ffn_pre", "w_up", "w_down", "g_ffn_post")


def kernel(x, g_mix_pre, w_in, b_forget, g_sgu, b_sgu, w_spatial, b_spatial, w_branch_sgu, w_branch_attn, w_out, g_mix_post, g_ffn_pre, w_up, w_down, g_ffn_post, loss_target, m_g_mix_pre, m_w_in, m_b_forget, m_g_sgu, m_b_sgu, m_w_spatial, m_b_spatial, m_w_branch_sgu, m_w_branch_attn, m_w_out, m_g_mix_post, m_g_ffn_pre, m_w_up, m_w_down, m_g_ffn_post, v_g_mix_pre, v_w_in, v_b_forget, v_g_sgu, v_b_sgu, v_w_spatial, v_b_spatial, v_w_branch_sgu, v_w_branch_attn, v_w_out, v_g_mix_post, v_g_ffn_pre, v_w_up, v_w_down, v_g_ffn_post):
    weights = dict(zip(NAMES, (g_mix_pre, w_in, b_forget, g_sgu, b_sgu, w_spatial, b_spatial, w_branch_sgu, w_branch_attn,
                               w_out, g_mix_post, g_ffn_pre, w_up, w_down, g_ffn_post), strict=True))
    first = dict(zip(NAMES, (m_g_mix_pre, m_w_in, m_b_forget, m_g_sgu, m_b_sgu, m_w_spatial, m_b_spatial, m_w_branch_sgu,
                             m_w_branch_attn, m_w_out, m_g_mix_post, m_g_ffn_pre, m_w_up, m_w_down, m_g_ffn_post), strict=True))
    second = dict(zip(NAMES, (v_g_mix_pre, v_w_in, v_b_forget, v_g_sgu, v_b_sgu, v_w_spatial, v_b_spatial, v_w_branch_sgu,
                              v_w_branch_attn, v_w_out, v_g_mix_post, v_g_ffn_pre, v_w_up, v_w_down, v_g_ffn_post), strict=True))
    shard_shapes = {name: _shard_shape(shape, axis) for name, shape, axis in SHARDED}
    small_shapes = dict(SMALL)
    view = lambda name, a: a.reshape(shard_shapes.get(name) or small_shapes[name])

    chip = 2 * lax.axis_index("x") + lax.axis_index("y")
    place = jnp.stack([chip, lax.axis_index("c")]).astype(jnp.int32)

    shards = {name: view(name, weights[name]).astype(BF) for name, _, _ in SHARDED}
    w_in_full = _assemble("w_in", shards["w_in"], _gather_weights([shards["w_in"]])[0], chip)
    small = {name: view(name, weights[name]) for name, _ in SMALL}

    sq, dx, reduced, small_grads = _local_step(x[0], loss_target[0], w_in_full, shards, small, place)
    loss = lax.psum(0.5 * jnp.sum(sq) / D_MODEL, ("x", "y", "c"))

    small_mine = _pack_small(small_grads)
    (small_theirs,) = _exchange_halves([small_mine], name="exchange_halves_small")
    small_sum, _ = _add_sibling(small_mine, small_theirs, place, name="add_sibling_small")
    (small_received,) = _scatter_to_owners([small_sum])
    totals = {name: _add_chips(s, r, place, name="add_chips_" + name, own_slots=True) for name, (s, r) in reduced.items()}
    small_total = _add_chips(small_sum, small_received, place, name="add_chips_small", own_slots=False)
    joined = _join_halves([totals[name] for name, _, _ in SHARDED] + [small_total])
    grad = {**{name: g for (name, _, _), g in zip(SHARDED, joined[:-1], strict=True)}, **_unpack_small(joined[-1])}

    grad_out, delta, new_m, new_v = {}, {}, {}, {}
    for name in NAMES:
        rows, cols = grad[name].shape
        as_given = lambda a: a.reshape(1, rows, cols)
        grad_out[name], delta[name], new_m[name], new_v[name] = _adamw(
            as_given(weights[name]), grad[name], as_given(first[name]), as_given(second[name]), name="adamw_" + name)

    like = lambda d: [d[name].reshape(weights[name].shape) for name in NAMES]
    return (loss, dx[None], *like(grad_out), *like(delta), *like(new_m), *like(new_v))
```

```python
import functools

import jax
import jax.numpy as jnp
from jax import lax
from jax.experimental import pallas as pl
from jax.experimental.pallas import tpu as pltpu

F32 = jnp.float32
BF = jnp.bfloat16
MESH = pl.DeviceIdType.MESH

D_MODEL = 1024
N_HEADS = 8
HEAD_DIM = 64
ATTN_W = N_HEADS * HEAD_DIM
SGU_W = 512
N_GROUPS = 8
CHUNK = 128
D_FF = 4096
EPS = 1e-6
Q_SCALE = HEAD_DIM ** -0.5
N_CHIPS = 4
LANES = 128

ADAM_LR = 0.001
ADAM_B1 = 0.9
ADAM_B2 = 0.999
ADAM_EPS = 1e-08
ADAM_WD = 0.01
ADAM_STEP = 10

VMEM_LIMIT = 48 * 1024 * 1024
ATTN_BWD_VMEM = 58 * 1024 * 1024
NEG = -1e30

LANE_ROWSUM = HEAD_DIM
LANE_COLSUM = HEAD_DIM + 3


def _params(*sem):
    return pltpu.CompilerParams(dimension_semantics=sem, vmem_limit_bytes=VMEM_LIMIT)


def _dot(a, b):
    return jnp.dot(a, b, preferred_element_type=F32)


def _dot_nt(a, b):
    return lax.dot_general(a, b, (((1,), (1,)), ((), ())), preferred_element_type=F32)


def _dot_tn(a, b):
    return lax.dot_general(a, b, (((0,), (0,)), ((), ())), preferred_element_type=F32)


def _split3(c):
    hi = c.astype(BF).astype(F32)
    r = c - hi
    mid = r.astype(BF).astype(F32)
    lo = (r - mid).astype(BF).astype(F32)
    return hi, mid, lo


def _gelu(x):
    k = 0.7978845608028654
    return 0.5 * x * (1.0 + jnp.tanh(k * (x + 0.044715 * (x * x * x))))


def _gelu_grad(x):
    k = 0.7978845608028654
    x2 = x * x
    t = jnp.tanh(k * (x + 0.044715 * (x2 * x)))
    return 0.5 * (1.0 + t) + 0.5 * x * (1.0 - t * t) * (k * (1.0 + 3.0 * 0.044715 * x2))


def _rms_bwd(a, g, dy):
    r = lax.rsqrt(jnp.mean(a * a, axis=-1, keepdims=True) + EPS)
    n = a * r
    dn = dy * g
    da = r * (dn - n * jnp.mean(dn * n, axis=-1, keepdims=True))
    return da, dy * n


MM_ROWS = 1024
MM_COLS = 512


def _matmul(pairs, *, nt, out_dtypes, name, tm=MM_ROWS, tn=MM_COLS, epilogue=None, extras=(), scatter=()):
    n_pairs, n_extra, n_out, n_scatter = len(pairs), len(extras), len(out_dtypes), len(scatter)
    M = pairs[0][0].shape[0]
    N = pairs[0][1].shape[0] if nt else pairs[0][1].shape[1]
    tm, tn = min(tm, M), min(tn, N)
    assert M % tm == 0 and N % tn == 0
    grid = (M // tm, N // tn)

    def body(*refs):
        n_in = 2 * n_pairs + n_extra
        if n_scatter:
            step = pl.program_id(0) * grid[1] + pl.program_id(1)
            first = n_in + n_scatter + n_out
            scatter_start, scatter_finish = _scatter_phases(
                refs[n_in:n_in + n_scatter], refs[first:first + n_scatter], *refs[first + n_scatter:])
            pl.when(step == 0)(scatter_start)
        acc = None
        for p in range(n_pairs):
            a_ref, b_ref = refs[2 * p], refs[2 * p + 1]
            d = _dot_nt(a_ref[...], b_ref[...]) if nt else _dot(a_ref[...], b_ref[...])
            acc = d if acc is None else acc + d
        e_refs = refs[2 * n_pairs:n_in]
        o_refs = refs[n_in + n_scatter:n_in + n_scatter + n_out]
        outs = (acc,) if epilogue is None else epilogue(acc, *[e[...] for e in e_refs])
        for o_ref, o in zip(o_refs, outs, strict=True):
            o_ref[...] = o.astype(o_ref.dtype)
        if n_scatter:
            pl.when(step == grid[0] * grid[1] - 1)(scatter_finish)

    in_specs, args = [], []
    for a, b in pairs:
        K = a.shape[1]
        in_specs.append(pl.BlockSpec((tm, K), lambda i, j: (i, 0)))
        in_specs.append(pl.BlockSpec((tn, K), lambda i, j: (j, 0)) if nt else pl.BlockSpec((K, tn), lambda i, j: (0, j)))
        args += [a, b]
    for e in extras:
        in_specs.append(pl.BlockSpec((tm, tn), lambda i, j: (i, j)))
        args.append(e)
    order = ("arbitrary", "arbitrary") if n_scatter else ("parallel", "parallel")
    outs = pl.pallas_call(
        body, name=name, grid=grid, in_specs=in_specs + [HBM] * n_scatter,
        out_specs=[pl.BlockSpec((tm, tn), lambda i, j: (i, j)) for _ in out_dtypes] + [HBM] * n_scatter,
        out_shape=[jax.ShapeDtypeStruct((M, N), dt) for dt in out_dtypes] + (_scattered_shapes(scatter) if n_scatter else []),
        scratch_shapes=_scatter_semaphores(n_scatter) if n_scatter else [],
        compiler_params=_params(*order),
    )(*args, *scatter)
    if n_scatter:
        return outs[:n_out], outs[n_out:]
    return outs if len(outs) > 1 else outs[0]


def _matmul_tn(a, b, *, name, tm=1024, tn=1024, tk=2048, slots=False):
    T, K1 = a.shape
    N = b.shape[1]
    tm, tn, tk = min(tm, K1), min(tn, N // N_CHIPS if slots else N), min(tk, T)
    assert K1 % tm == 0 and (N // N_CHIPS if slots else N) % tn == 0 and T % tk == 0
    per_slot = N // N_CHIPS // tn

    def body(a_ref, b_ref, o_ref):
        @pl.when(pl.program_id(2) == 0)
        def _():
            o_ref[...] = jnp.zeros_like(o_ref)

        o_ref[...] += _dot_tn(a_ref[...], b_ref[...])

    if slots:
        out_spec = pl.BlockSpec((None, tm, tn), lambda i, j, k: (j // per_slot, i, j % per_slot))
        out_shape = jax.ShapeDtypeStruct((N_CHIPS, K1, N // N_CHIPS), F32)
    else:
        out_spec = pl.BlockSpec((tm, tn), lambda i, j, k: (i, j))
        out_shape = jax.ShapeDtypeStruct((K1, N), F32)
    return pl.pallas_call(
        body, name=name, grid=(K1 // tm, N // tn, T // tk),
        in_specs=[pl.BlockSpec((tk, tm), lambda i, j, k: (k, i)), pl.BlockSpec((tk, tn), lambda i, j, k: (k, j))],
        out_specs=out_spec, out_shape=out_shape,
        compiler_params=_params("parallel", "parallel", "arbitrary"),
    )(a, b)


def _branch_merge(ysgu, yattn, w_bs, w_ba, gl, *, tm=MM_ROWS, tn=MM_COLS):
    T = ysgu.shape[0]
    tm = min(tm, T)
    nj = D_MODEL // tn

    def body(ys_ref, ya_ref, wbs_ref, wba_ref, gla_ref, glb_ref, a_ref, b_ref, m_ref):
        a = _dot(ys_ref[...], wbs_ref[...])
        b = _dot(ya_ref[...], wba_ref[...])
        a_ref[...] = a.astype(BF)
        b_ref[...] = b.astype(BF)
        m_ref[...] = (jax.nn.sigmoid(gla_ref[...].astype(F32)) * a + jax.nn.sigmoid(glb_ref[...].astype(F32)) * b).astype(BF)

    return pl.pallas_call(
        body, name="branch_merge", grid=(T // tm, nj),
        in_specs=[
            pl.BlockSpec((tm, SGU_W), lambda i, j: (i, 0)),
            pl.BlockSpec((tm, ATTN_W), lambda i, j: (i, 0)),
            pl.BlockSpec((SGU_W, tn), lambda i, j: (0, j)),
            pl.BlockSpec((ATTN_W, tn), lambda i, j: (0, j)),
            pl.BlockSpec((tm, tn), lambda i, j: (i, j)),
            pl.BlockSpec((tm, tn), lambda i, j: (i, j + nj)),
        ],
        out_specs=[pl.BlockSpec((tm, tn), lambda i, j: (i, j))] * 3,
        out_shape=[jax.ShapeDtypeStruct((T, D_MODEL), BF)] * 3,
        compiler_params=_params("parallel", "parallel"),
    )(ysgu, yattn, w_bs, w_ba, gl, gl)


def _row_spec(tr, width):
    return pl.BlockSpec((tr, width), lambda i: (i, 0))


def _vec_spec(width):
    return pl.BlockSpec((1, width), lambda i: (0, 0))


def _rms_fwd(x, g, shards, *, tr=256):
    T = x.shape[0]
    tr = min(tr, T)
    n_steps = T // tr
    k = len(shards)

    def body(x_ref, g_ref, *refs):
        step = pl.program_id(0)
        gather_start, gather_forward, gather_finish = _gather_phases(refs[:k], refs[k + 1:2 * k + 1], *refs[2 * k + 1:])
        pl.when(step == 0)(gather_start)
        pl.when(step == (3 * n_steps) // 4)(gather_forward)
        xv = x_ref[...]
        r = lax.rsqrt(jnp.mean(xv * xv, axis=-1, keepdims=True) + EPS)
        refs[k][...] = ((xv * r) * g_ref[...]).astype(BF)
        pl.when(step == n_steps - 1)(gather_finish)

    outs = pl.pallas_call(
        body, name="rms_fwd", grid=(n_steps,),
        in_specs=[_row_spec(tr, D_MODEL), _vec_spec(D_MODEL)] + [HBM] * k, out_specs=[_row_spec(tr, D_MODEL)] + [HBM] * k,
        out_shape=[jax.ShapeDtypeStruct((T, D_MODEL), BF)] + _gathered_shapes(shards),
        scratch_shapes=_gather_semaphores(k), compiler_params=_params("arbitrary"),
    )(x, g, *shards)
    return outs[0], outs[1:]


def _mixer_out_fwd(o, x, g_post, g_pre, *, tr=256):
    T = x.shape[0]
    tr = min(tr, T)

    def body(o_ref, x_ref, gpost_ref, gpre_ref, h1_ref, xn2_ref):
        ov = o_ref[...]
        r = lax.rsqrt(jnp.mean(ov * ov, axis=-1, keepdims=True) + EPS)
        h1 = x_ref[...] + (ov * r) * gpost_ref[...]
        h1_ref[...] = h1
        r2 = lax.rsqrt(jnp.mean(h1 * h1, axis=-1, keepdims=True) + EPS)
        xn2_ref[...] = ((h1 * r2) * gpre_ref[...]).astype(BF)

    return pl.pallas_call(
        body, name="mixer_out_fwd", grid=(T // tr,),
        in_specs=[_row_spec(tr, D_MODEL), _row_spec(tr, D_MODEL), _vec_spec(D_MODEL), _vec_spec(D_MODEL)],
        out_specs=[_row_spec(tr, D_MODEL), _row_spec(tr, D_MODEL)],
        out_shape=[jax.ShapeDtypeStruct((T, D_MODEL), F32), jax.ShapeDtypeStruct((T, D_MODEL), BF)],
        compiler_params=_params("parallel"),
    )(o, x, g_post, g_pre)


def _loss_head(dn, h1, target, g_post, *, tr=256):
    T = dn.shape[0]
    tr = min(tr, T)

    def body(dn_ref, h1_ref, t_ref, g_ref, sq_ref, dy_ref, ddn_ref, dg_ref):
        @pl.when(pl.program_id(0) == 0)
        def _():
            sq_ref[...] = jnp.zeros_like(sq_ref)
            dg_ref[...] = jnp.zeros_like(dg_ref)

        a = dn_ref[...]
        g = g_ref[...]
        r = lax.rsqrt(jnp.mean(a * a, axis=-1, keepdims=True) + EPS)
        err = h1_ref[...] + (a * r) * g - t_ref[...]
        sq_ref[...] += jnp.sum(err * err, axis=0, keepdims=True)
        dy = err * (1.0 / D_MODEL)
        dy_ref[...] = dy
        da, dgp = _rms_bwd(a, g, dy)
        ddn_ref[...] = da.astype(BF)
        dg_ref[...] += jnp.sum(dgp, axis=0, keepdims=True)

    return pl.pallas_call(
        body, name="loss_head", grid=(T // tr,),
        in_specs=[_row_spec(tr, D_MODEL)] * 3 + [_vec_spec(D_MODEL)],
        out_specs=[_vec_spec(D_MODEL), _row_spec(tr, D_MODEL), _row_spec(tr, D_MODEL), _vec_spec(D_MODEL)],
        out_shape=[jax.ShapeDtypeStruct((1, D_MODEL), F32), jax.ShapeDtypeStruct((T, D_MODEL), F32),
                   jax.ShapeDtypeStruct((T, D_MODEL), BF), jax.ShapeDtypeStruct((1, D_MODEL), F32)],
        compiler_params=_params("arbitrary"),
    )(dn, h1, target, g_post)


def _mixer_out_bwd(h1, dxn2, dy, o, g_pre, g_post, *, tr=256):
    T = h1.shape[0]
    tr = min(tr, T)

    def body(h1_ref, dxn2_ref, dy_ref, o_ref, gpre_ref, gpost_ref, dh1_ref, do_ref, dgpre_ref, dgpost_ref):
        @pl.when(pl.program_id(0) == 0)
        def _():
            dgpre_ref[...] = jnp.zeros_like(dgpre_ref)
            dgpost_ref[...] = jnp.zeros_like(dgpost_ref)

        da, dgp = _rms_bwd(h1_ref[...], gpre_ref[...], dxn2_ref[...])
        dh1 = dy_ref[...] + da
        dh1_ref[...] = dh1
        dgpre_ref[...] += jnp.sum(dgp, axis=0, keepdims=True)
        do, dgp2 = _rms_bwd(o_ref[...], gpost_ref[...], dh1)
        do_ref[...] = do.astype(BF)
        dgpost_ref[...] += jnp.sum(dgp2, axis=0, keepdims=True)

    return pl.pallas_call(
        body, name="mixer_out_bwd", grid=(T // tr,),
        in_specs=[_row_spec(tr, D_MODEL)] * 4 + [_vec_spec(D_MODEL)] * 2,
        out_specs=[_row_spec(tr, D_MODEL), _row_spec(tr, D_MODEL), _vec_spec(D_MODEL), _vec_spec(D_MODEL)],
        out_shape=[jax.ShapeDtypeStruct((T, D_MODEL), F32), jax.ShapeDtypeStruct((T, D_MODEL), BF),
                   jax.ShapeDtypeStruct((1, D_MODEL), F32), jax.ShapeDtypeStruct((1, D_MODEL), F32)],
        compiler_params=_params("arbitrary"),
    )(h1, dxn2, dy, o, g_pre, g_post)


def _input_norm_bwd(x, dxn, dh1, g, *, tr=256):
    T = x.shape[0]
    tr = min(tr, T)

    def body(x_ref, dxn_ref, dh1_ref, g_ref, dx_ref, dg_ref):
        @pl.when(pl.program_id(0) == 0)
        def _():
            dg_ref[...] = jnp.zeros_like(dg_ref)

        da, dgp = _rms_bwd(x_ref[...], g_ref[...], dxn_ref[...])
        dx_ref[...] = dh1_ref[...] + da
        dg_ref[...] += jnp.sum(dgp, axis=0, keepdims=True)

    return pl.pallas_call(
        body, name="input_norm_bwd", grid=(T // tr,),
        in_specs=[_row_spec(tr, D_MODEL)] * 3 + [_vec_spec(D_MODEL)],
        out_specs=[_row_spec(tr, D_MODEL), _vec_spec(D_MODEL)],
        out_shape=[jax.ShapeDtypeStruct((T, D_MODEL), F32), jax.ShapeDtypeStruct((1, D_MODEL), F32)],
        compiler_params=_params("arbitrary"),
    )(x, dxn, dh1, g)


def _gate_bwd(dm, a, b, gl, *, tr=256):
    T = dm.shape[0]
    tr = min(tr, T)

    def body(dm_ref, a_ref, b_ref, gla_ref, glb_ref, da_ref, db_ref, dgla_ref, dglb_ref):
        dmv = dm_ref[...]
        ga = jax.nn.sigmoid(gla_ref[...].astype(F32))
        gb = jax.nn.sigmoid(glb_ref[...].astype(F32))
        da_ref[...] = (dmv * ga).astype(BF)
        db_ref[...] = (dmv * gb).astype(BF)
        dgla_ref[...] = (dmv * a_ref[...].astype(F32) * (ga * (1.0 - ga))).astype(BF)
        dglb_ref[...] = (dmv * b_ref[...].astype(F32) * (gb * (1.0 - gb))).astype(BF)

    spec = _row_spec(tr, D_MODEL)
    spec_b = pl.BlockSpec((tr, D_MODEL), lambda i: (i, 1))
    da, db, dgla, dglb = pl.pallas_call(
        body, name="gate_bwd", grid=(T // tr,),
        in_specs=[spec, spec, spec, spec, spec_b], out_specs=[spec] * 4,
        out_shape=[jax.ShapeDtypeStruct((T, D_MODEL), BF)] * 4, compiler_params=_params("parallel"),
    )(dm, a, b, gl, gl)
    return da, db, dgla, dglb


def _sgu_norm(z_tile, g, b):
    gz = _gelu(z_tile)
    u, vv = gz[:, :SGU_W], gz[:, SGU_W:]
    xc = vv - jnp.mean(vv, axis=-1, keepdims=True)
    rstd = lax.rsqrt(jnp.mean(xc * xc, axis=-1, keepdims=True) + EPS)
    xhat = xc * rstd
    return u, xhat, rstd, xhat * g + b


def _sgu_mix(w_ref, v_bf, first_half):
    parts = []
    for p in range(N_GROUPS // 2):
        vp = v_bf[:, p * LANES:(p + 1) * LANES]
        parts.append(jnp.where(first_half, _dot(w_ref[2 * p], vp), _dot(w_ref[2 * p + 1], vp)))
    return jnp.concatenate(parts, axis=1)


def _sgu_fwd(z, g_sgu, b_sgu, ws, bias_plane, *, tm=512):
    T = z.shape[0]
    tm = min(tm, T)

    def body(z_ref, g_ref, b_ref, ws_ref, bp_ref, y_ref):
        u, _, _, vn = _sgu_norm(z_ref[...], g_ref[...], b_ref[...])
        vn_bf = vn.astype(BF)
        first_half = lax.broadcasted_iota(jnp.int32, (CHUNK, LANES), 1) < HEAD_DIM
        for c in range(tm // CHUNK):
            rows = slice(c * CHUNK, (c + 1) * CHUNK)
            s = _sgu_mix(ws_ref, vn_bf[rows, :], first_half) + bp_ref[...]
            y_ref[rows, :] = (u[rows, :] * s).astype(BF)

    return pl.pallas_call(
        body, name="sgu_fwd", grid=(T // tm,),
        in_specs=[_row_spec(tm, 2 * SGU_W), _vec_spec(SGU_W), _vec_spec(SGU_W),
                  pl.BlockSpec((N_GROUPS, CHUNK, CHUNK), lambda i: (0, 0, 0)),
                  pl.BlockSpec((CHUNK, SGU_W), lambda i: (0, 0))],
        out_specs=_row_spec(tm, SGU_W), out_shape=jax.ShapeDtypeStruct((T, SGU_W), BF),
        compiler_params=_params("parallel"),
    )(z, g_sgu, b_sgu, ws, bias_plane)


def _sgu_bwd(dy, z, g_sgu, b_sgu, ws, ws_t, bias_plane, exchange, *, tm=512):
    T = z.shape[0]
    tm = min(tm, T)
    n_steps = T // tm
    k = len(exchange)

    def body(dy_ref, z_ref, g_ref, b_ref, ws_ref, wst_ref, bp_ref, *refs):
        x_refs, (dz_ref, dws_ref, dbs_ref, dg_ref, db_ref), r_refs = refs[:k], refs[k:k + 5], refs[k + 5:2 * k + 5]
        dbp_ref, send_sems, recv_sems = refs[2 * k + 5:]
        step = pl.program_id(0)
        exchange_start, exchange_finish = _exchange_phases(x_refs, r_refs, send_sems, recv_sems)
        pl.when(step == 0)(exchange_start)

        @pl.when(step == 0)
        def _():
            dws_ref[...] = jnp.zeros_like(dws_ref)
            dg_ref[...] = jnp.zeros_like(dg_ref)
            db_ref[...] = jnp.zeros_like(db_ref)
            dbp_ref[...] = jnp.zeros_like(dbp_ref)

        g = g_ref[...]
        zt = z_ref[...]
        u, xhat, rstd, vn = _sgu_norm(zt, g, b_ref[...])
        vn_bf = vn.astype(BF)
        first_half = lax.broadcasted_iota(jnp.int32, (CHUNK, LANES), 1) < HEAD_DIM
        dyv = dy_ref[...]
        dg_acc = jnp.zeros((1, SGU_W), F32)
        db_acc = jnp.zeros((1, SGU_W), F32)
        for c in range(tm // CHUNK):
            rows = slice(c * CHUNK, (c + 1) * CHUNK)
            v_c = vn_bf[rows, :]
            s = _sgu_mix(ws_ref, v_c, first_half) + bp_ref[...]
            dy_c = dyv[rows, :]
            du = dy_c * s
            dsv = dy_c * u[rows, :]
            dbp_ref[...] += dsv
            ds_bf = dsv.astype(BF)
            zero = jnp.zeros((CHUNK, LANES), BF)
            for p in range(N_GROUPS // 2):
                dsp = ds_bf[:, p * LANES:(p + 1) * LANES]
                vp = v_c[:, p * LANES:(p + 1) * LANES]
                dws_ref[2 * p] += _dot_nt(jnp.where(first_half, dsp, zero), vp)
                dws_ref[2 * p + 1] += _dot_nt(jnp.where(first_half, zero, dsp), vp)
            dvn = _sgu_mix(wst_ref, ds_bf, first_half)
            xh = xhat[rows, :]
            dxh = dvn * g
            dvv = rstd[rows, :] * (dxh - jnp.mean(dxh, axis=-1, keepdims=True)
                                   - xh * jnp.mean(dxh * xh, axis=-1, keepdims=True))
            dg_acc += jnp.sum(dvn * xh, axis=0, keepdims=True)
            db_acc += jnp.sum(dvn, axis=0, keepdims=True)
            dgz = jnp.concatenate([du, dvv], axis=1)
            dz_ref[rows, :] = (dgz * _gelu_grad(zt[rows, :])).astype(BF)
        dg_ref[...] += dg_acc
        db_ref[...] += db_acc

        @pl.when(step == n_steps - 1)
        def _():
            r = lax.broadcasted_iota(jnp.int32, (CHUNK, CHUNK), 0)
            cidx = lax.broadcasted_iota(jnp.int32, (CHUNK, CHUNK), 1)
            causal = (cidx <= r).astype(F32)
            for gi in range(N_GROUPS):
                dws_ref[gi] = dws_ref[gi] * causal
            lane = lax.broadcasted_iota(jnp.int32, (CHUNK, LANES), 1)
            out = jnp.zeros((CHUNK, LANES), F32)
            dbp = dbp_ref[...]
            for gi in range(N_GROUPS):
                col = jnp.sum(dbp[:, gi * HEAD_DIM:(gi + 1) * HEAD_DIM], axis=1, keepdims=True)
                out = jnp.where(lane == gi, col, out)
            dbs_ref[...] = out
            exchange_finish()

    w_spec = pl.BlockSpec((N_GROUPS, CHUNK, CHUNK), lambda i: (0, 0, 0))
    plane = pl.BlockSpec((CHUNK, SGU_W), lambda i: (0, 0))
    outs = pl.pallas_call(
        body, name="sgu_bwd", grid=(n_steps,),
        in_specs=[_row_spec(tm, SGU_W), _row_spec(tm, 2 * SGU_W), _vec_spec(SGU_W), _vec_spec(SGU_W), w_spec, w_spec, plane]
        + [HBM] * k,
        out_specs=[_row_spec(tm, 2 * SGU_W), w_spec, pl.BlockSpec((CHUNK, LANES), lambda i: (0, 0)),
                   _vec_spec(SGU_W), _vec_spec(SGU_W)] + [HBM] * k,
        out_shape=[jax.ShapeDtypeStruct((T, 2 * SGU_W), BF), jax.ShapeDtypeStruct((N_GROUPS, CHUNK, CHUNK), F32),
                   jax.ShapeDtypeStruct((CHUNK, LANES), F32), jax.ShapeDtypeStruct((1, SGU_W), F32),
                   jax.ShapeDtypeStruct((1, SGU_W), F32)] + _exchanged_shapes(exchange),
        scratch_shapes=[pltpu.VMEM((CHUNK, SGU_W), F32)] + _exchange_semaphores(k),
        compiler_params=_params("arbitrary"),
    )(dy, z, g_sgu, b_sgu, ws, ws_t, bias_plane, *exchange)
    return outs[:5], outs[5:]


def _tri(n, upper):
    r = lax.broadcasted_iota(jnp.int32, (n, n), 0)
    c = lax.broadcasted_iota(jnp.int32, (n, n), 1)
    return ((c >= r) if upper else (c <= r)).astype(BF)


def _scan_dot(tri, x):
    hi, mid, lo = _split3(x)
    return (_dot(tri, hi.astype(BF)) + _dot(tri, mid.astype(BF))) + _dot(tri, lo.astype(BF))


def _with_lanes(base, lane, start, cols):
    out = base
    for k, col in enumerate(cols):
        if col is not None:
            out = jnp.where(lane == start + k, col, out)
    return out


def _logit_bound(q_norm, k_norm):
    return NORM_SLACK * q_norm * k_norm + 1.0


ATTN_TILE = 512
SKIP_BELOW = -110.0
NORM_SLACK = 1.001
BOUNDED_GAP = 60.0


def _attn_prep(qkv, fl, b_forget, *, tp=ATTN_TILE):
    T = qkv.shape[0]
    tp = min(tp, T)

    def body(qkv_ref, fl_ref, bf_ref, qf_ref, kl_ref, vl_ref, st_ref, carry_ref, kmax_ref):
        @pl.when(pl.program_id(0) == 0)
        def _():
            carry_ref[...] = jnp.zeros_like(carry_ref)
            kmax_ref[...] = jnp.zeros_like(kmax_ref)

        x = fl_ref[...] + bf_ref[...]
        logf = jnp.minimum(x, 0.0) - jnp.log(1.0 + jnp.exp(-jnp.abs(x)))
        cum = _scan_dot(_tri(tp, upper=False), logf) + carry_ref[...]
        carry_ref[...] = cum[tp - 1:tp, :]
        lane = lax.broadcasted_iota(jnp.int32, (tp, HEAD_DIM), 1)
        ones3 = jnp.where(lane < 3, 1.0, 0.0)
        qkvv = qkv_ref[...]
        st_row = lax.broadcasted_iota(jnp.int32, (N_HEADS, LANES), 0)
        st_lane = lax.broadcasted_iota(jnp.int32, (N_HEADS, LANES), 1)
        stats = jnp.zeros((N_HEADS, LANES), F32)
        kmax_lane = lax.broadcasted_iota(jnp.int32, (1, LANES), 1)
        for h in range(N_HEADS):
            ch = cum[:, h:h + 1]
            c3 = _split3(ch)
            qh = qkvv[:, h * HEAD_DIM:(h + 1) * HEAD_DIM].astype(F32) * Q_SCALE
            kh = qkvv[:, ATTN_W + h * HEAD_DIM:ATTN_W + (h + 1) * HEAD_DIM].astype(F32)
            vh = qkvv[:, 2 * ATTN_W + h * HEAD_DIM:2 * ATTN_W + (h + 1) * HEAD_DIM].astype(F32)
            q_norm = jnp.sqrt(jnp.sum(qh * qh, axis=1, keepdims=True))
            qn = jnp.max(q_norm, axis=0, keepdims=True)
            kn = jnp.sqrt(jnp.max(jnp.sum(kh * kh, axis=1, keepdims=True), axis=0, keepdims=True))
            k_seen = jnp.maximum(kmax_ref[:, h:h + 1], kn)
            kmax_ref[...] = jnp.where(kmax_lane == h, k_seen, kmax_ref[...])
            bound3 = _split3(-_logit_bound(q_norm, k_seen))
            ext_q = _with_lanes(jnp.where((lane >= 3) & (lane < 6), 1.0, 0.0), lane, 0, list(c3) + [None] * 3 + list(bound3))
            ext_k = _with_lanes(jnp.where((lane < 3) | ((lane >= 6) & (lane < 9)), 1.0, 0.0), lane, 3, [-c for c in c3])
            qf_ref[h] = jnp.concatenate([qh, ext_q], axis=1).astype(BF)
            kl_ref[h] = jnp.concatenate([kh, ext_k], axis=1).astype(BF)
            vl_ref[h] = jnp.concatenate([vh, ones3], axis=1).astype(BF)
            tile_stats = (qn, kn, jnp.max(ch, axis=0, keepdims=True), jnp.min(ch, axis=0, keepdims=True), k_seen)
            for k, val in enumerate(tile_stats):
                stats = jnp.where((st_row == h) & (st_lane == k), val, stats)
        st_ref[0] = stats

    head_spec = pl.BlockSpec((N_HEADS, tp, LANES), lambda i: (0, i, 0))
    return pl.pallas_call(
        body, name="attn_prep", grid=(T // tp,),
        in_specs=[_row_spec(tp, 3 * ATTN_W), _row_spec(tp, LANES), _vec_spec(LANES)],
        out_specs=[head_spec] * 3 + [pl.BlockSpec((1, N_HEADS, LANES), lambda i: (i, 0, 0))],
        out_shape=[jax.ShapeDtypeStruct((N_HEADS, T, LANES), BF)] * 3 + [jax.ShapeDtypeStruct((T // tp, N_HEADS, LANES), F32)],
        scratch_shapes=[pltpu.VMEM((1, LANES), F32), pltpu.VMEM((1, LANES), F32)], compiler_params=_params("arbitrary"),
    )(qkv, fl, b_forget)


def _attn_ranges(stats):
    qn, kn, cmax, cmin, k_seen = (stats[:, :, k].T for k in range(5))
    n = qn.shape[1]
    bounded = (2.0 * _logit_bound(qn, k_seen) <= BOUNDED_GAP).reshape(N_HEADS // 2, 2, n).all(axis=1)
    reach = NORM_SLACK * qn * (jnp.max(kn, axis=1, keepdims=True) + kn) + cmax
    i = jnp.arange(n)[None, :, None]
    j = jnp.arange(n)[None, None, :]
    need = ((reach[:, :, None] - cmin[:, None, :] >= SKIP_BELOW) | (i == j)) & (j <= i)
    first = jnp.min(jnp.where(need, j, n), axis=2).reshape(N_HEADS // 2, 2, n).min(axis=1)
    last = jnp.max(jnp.where(need, i, -1), axis=1).reshape(N_HEADS // 2, 2, n).max(axis=1)
    return first.reshape(-1).astype(F32), last.reshape(-1).astype(F32), bounded.reshape(-1).astype(F32)


def _pair_block(t):
    return pl.BlockSpec((2, t, LANES), lambda p, i, *_: (p, i, 0))


def _pair_full(T):
    return pl.BlockSpec((2, T, LANES), lambda p, i, *_: (p, 0, 0))


def _packed_block(t):
    return pl.BlockSpec((t, LANES), lambda p, i, *_: (i, p))


def _causal(t, keys_in_rows=False):
    r = lax.broadcasted_iota(jnp.int32, (t, t), 0)
    c = lax.broadcasted_iota(jnp.int32, (t, t), 1)
    return (r <= c) if keys_in_rows else (c <= r)


def _tile_rows(j, t):
    return pl.ds(pl.multiple_of(j * t, t), t)


def _attn_call(body, name, tile_scalars, operands, in_specs, out_specs, out_shape, scratch_shapes, n_tiles):
    return pl.pallas_call(
        body, name=name,
        grid_spec=pltpu.PrefetchScalarGridSpec(
            num_scalar_prefetch=len(tile_scalars), grid=(N_HEADS // 2, n_tiles), in_specs=in_specs, out_specs=out_specs,
            scratch_shapes=scratch_shapes),
        out_shape=out_shape, compiler_params=_params("arbitrary", "arbitrary"),
    )(*tile_scalars, *operands)


def _attn_fwd(qf, kl, vl, first, bounded, shards, *, tq=ATTN_TILE):
    T = qf.shape[1]
    tq = min(tq, T)
    n = T // tq
    n_steps = (N_HEADS // 2) * n
    k = len(shards)

    def body(first_ref, bounded_ref, qf_ref, kl_ref, vl_ref, *refs):
        w_refs, (o_ref, of_ref, ql_ref), g_refs = refs[:k], refs[k:k + 3], refs[k + 3:2 * k + 3]
        m_ref, acc_ref, send_sems, recv_sems = refs[2 * k + 3:]
        i = pl.program_id(1)
        tile = pl.program_id(0) * n + i
        gather_start, gather_forward, gather_finish = _gather_phases(w_refs, g_refs, send_sems, recv_sems)
        pl.when(tile == 0)(gather_start)
        pl.when(tile == (3 * n_steps) // 4)(gather_forward)
        start = first_ref[tile].astype(jnp.int32)
        is_bounded = bounded_ref[tile] > 0.5
        acc_ref[...] = jnp.zeros_like(acc_ref)
        diagonal = _tile_rows(i, tq)
        causal = _causal(tq)

        def logits(hh, rows):
            return _dot_nt(qf_ref[hh], kl_ref[hh, rows, :])

        @pl.when(is_bounded)
        def _():
            m_ref[...] = jnp.zeros_like(m_ref)

            def update(hh, s, rows):
                acc_ref[hh] += _dot(jnp.exp(s).astype(BF), vl_ref[hh, rows, :])

            def step(j, carry):
                for hh in range(2):
                    update(hh, logits(hh, _tile_rows(j, tq)), _tile_rows(j, tq))
                return carry

            lax.fori_loop(start, i, step, 0)
            for hh in range(2):
                update(hh, jnp.where(causal, logits(hh, diagonal), NEG), diagonal)

        @pl.when(jnp.logical_not(is_bounded))
        def _():
            m_ref[...] = jnp.full_like(m_ref, NEG)

            def update(hh, s, rows):
                m_old = m_ref[hh]
                m_new = jnp.maximum(m_old, jnp.max(s, axis=1, keepdims=True))
                p = jnp.exp(s - m_new)
                acc_ref[hh] = jnp.exp(m_old - m_new) * acc_ref[hh] + _dot(p.astype(BF), vl_ref[hh, rows, :])
                m_ref[hh] = m_new

            def step(j, carry):
                for hh in range(2):
                    update(hh, logits(hh, _tile_rows(j, tq)), _tile_rows(j, tq))
                return carry

            lax.fori_loop(start, i, step, 0)
            for hh in range(2):
                update(hh, jnp.where(causal, logits(hh, diagonal), NEG), diagonal)

        lane = lax.broadcasted_iota(jnp.int32, (tq, LANES), 1)
        outs = []
        for hh in range(2):
            q = qf_ref[hh].astype(F32)
            acc = acc_ref[hh]
            l = acc[:, HEAD_DIM:HEAD_DIM + 1]
            outs.append(acc[:, :HEAD_DIM] / l)
            at = HEAD_DIM + 6
            neg_bound = (q[:, at:at + 1] + q[:, at + 1:at + 2]) + q[:, at + 2:at + 3]
            ql_ref[hh] = _with_lanes(q, lane, at, _split3(neg_bound - (m_ref[hh] + jnp.log(l)))).astype(BF)
        o = jnp.concatenate(outs, axis=1)
        o_ref[...] = o.astype(BF)
        of_ref[...] = o
        pl.when(tile == n_steps - 1)(gather_finish)

    outs = _attn_call(
        body, "attn_fwd", (first, bounded), (qf, kl, vl, *shards),
        [_pair_block(tq), _pair_full(T), _pair_full(T)] + [HBM] * k,
        [_packed_block(tq), _packed_block(tq), _pair_block(tq)] + [HBM] * k,
        [jax.ShapeDtypeStruct((T, ATTN_W), BF), jax.ShapeDtypeStruct((T, ATTN_W), F32),
         jax.ShapeDtypeStruct((N_HEADS, T, LANES), BF)] + _gathered_shapes(shards),
        [pltpu.VMEM((2, tq, 1), F32), pltpu.VMEM((2, tq, LANES), F32)] + _gather_semaphores(k), n)
    return outs[0], outs[1], outs[2], outs[3:]


def _attn_bwd_prep(dya, of, *, tr=256):
    T = dya.shape[0]
    tr = min(tr, T)

    def body(d_ref, o_ref, do_ref):
        lane = lax.broadcasted_iota(jnp.int32, (tr, HEAD_DIM), 1)
        dv, ov = d_ref[...], o_ref[...]
        for h in range(N_HEADS):
            d = dv[:, h * HEAD_DIM:(h + 1) * HEAD_DIM]
            delta = jnp.sum(d * ov[:, h * HEAD_DIM:(h + 1) * HEAD_DIM], axis=1, keepdims=True)
            ext = _with_lanes(jnp.zeros((tr, HEAD_DIM), F32), lane, 0, _split3(-delta))
            do_ref[h] = jnp.concatenate([d, ext], axis=1).astype(BF)

    return pl.pallas_call(
        body, name="attn_bwd_prep", grid=(T // tr,),
        in_specs=[_row_spec(tr, ATTN_W), _row_spec(tr, ATTN_W)],
        out_specs=pl.BlockSpec((N_HEADS, tr, LANES), lambda i: (0, i, 0)),
        out_shape=jax.ShapeDtypeStruct((N_HEADS, T, LANES), BF), compiler_params=_params("parallel"),
    )(dya, of)


def _attn_bwd(kl, vl, ql, do, last, chip_sums, *, tk=ATTN_TILE):
    T = ql.shape[1]
    tk = min(tk, T)
    n = T // tk
    n_steps = (N_HEADS // 2) * n
    m = len(chip_sums)

    def body(last_ref, kl_ref, vl_ref, ql_ref, do_ref, *refs):
        b_refs, (dq_ref, dk_ref, dv_ref, extq_ref, extk_ref), r_refs = refs[:m], refs[m:m + 5], refs[m + 5:2 * m + 5]
        dq_acc, dk_acc, dv_acc, send_sems, recv_sems = refs[2 * m + 5:]
        j = pl.program_id(1)
        tile = pl.program_id(0) * n + j
        scatter_start, scatter_finish = _scatter_phases(b_refs, r_refs, send_sems, recv_sems)
        pl.when(tile == 0)(scatter_start)

        @pl.when(j == 0)
        def _():
            dq_acc[...] = jnp.zeros_like(dq_acc)

        dk_acc[...] = jnp.zeros_like(dk_acc)
        dv_acc[...] = jnp.zeros_like(dv_acc)

        def block(hh, rows, mask):
            qi, di, k = ql_ref[hh, rows, :], do_ref[hh, rows, :], kl_ref[hh]
            p_t = jnp.exp(_dot_nt(k, qi))
            if mask is not None:
                p_t = jnp.where(mask, p_t, 0.0)
            ds_t = (p_t * _dot_nt(vl_ref[hh], di)).astype(BF)
            dk_acc[hh] += _dot(ds_t, qi)
            dv_acc[hh] += _dot(p_t.astype(BF), di)
            dq_acc[hh, rows, :] += _dot_tn(ds_t, k)

        causal_t = _causal(tk, keys_in_rows=True)
        for hh in range(2):
            block(hh, _tile_rows(j, tk), causal_t)

        def step(i, carry):
            for hh in range(2):
                block(hh, _tile_rows(i, tk), None)
            return carry

        lax.fori_loop(j + 1, last_ref[pl.program_id(0) * n + j].astype(jnp.int32) + 1, step, 0)
        dk_ref[...] = jnp.concatenate([dk_acc[hh][:, :HEAD_DIM] for hh in range(2)], axis=1).astype(BF)
        dv_ref[...] = jnp.concatenate([dv_acc[hh][:, :HEAD_DIM] for hh in range(2)], axis=1).astype(BF)
        extk_ref[...] = jnp.concatenate([dk_acc[hh][:, HEAD_DIM:] for hh in range(2)], axis=1)

        @pl.when(j == n - 1)
        def _():
            dq_ref[...] = jnp.concatenate([dq_acc[hh][:, :HEAD_DIM] * Q_SCALE for hh in range(2)], axis=1).astype(BF)
            extq_ref[...] = jnp.concatenate([dq_acc[hh][:, HEAD_DIM:] for hh in range(2)], axis=1)

        pl.when(tile == n_steps - 1)(scatter_finish)

    whole = pl.BlockSpec((T, LANES), lambda p, j, *_: (0, p))
    outs = pl.pallas_call(
        body, name="attn_bwd",
        grid_spec=pltpu.PrefetchScalarGridSpec(
            num_scalar_prefetch=1, grid=(N_HEADS // 2, n),
            in_specs=[_pair_block(tk), _pair_block(tk), _pair_full(T), _pair_full(T)] + [HBM] * m,
            out_specs=[whole, _packed_block(tk), _packed_block(tk), whole, _packed_block(tk)] + [HBM] * m,
            scratch_shapes=[pltpu.VMEM((2, T, LANES), F32), pltpu.VMEM((2, tk, LANES), F32), pltpu.VMEM((2, tk, LANES), F32)]
            + _scatter_semaphores(m)),
        out_shape=[jax.ShapeDtypeStruct((T, ATTN_W), BF)] * 3 + [jax.ShapeDtypeStruct((T, ATTN_W), F32)] * 2
        + _scattered_shapes(chip_sums),
        compiler_params=pltpu.CompilerParams(dimension_semantics=("arbitrary", "arbitrary"), vmem_limit_bytes=ATTN_BWD_VMEM),
    )(last, kl, vl, ql, do, *chip_sums)
    return outs[:5], outs[5:]


def _forget_bwd(ext_q, ext_k, fl, b_forget, *, tp=256):
    T = fl.shape[0]
    tp = min(tp, T)
    n = T // tp

    def body(eq_ref, ek_ref, fl_ref, bf_ref, dfl_ref, dbf_ref, carry_ref):
        @pl.when(pl.program_id(0) == 0)
        def _():
            carry_ref[...] = jnp.zeros_like(carry_ref)
            dbf_ref[...] = jnp.zeros_like(dbf_ref)

        lane = lax.broadcasted_iota(jnp.int32, (tp, LANES), 1)
        eq, ek = eq_ref[...], ek_ref[...]
        cols = [eq[:, h * HEAD_DIM:h * HEAD_DIM + 1] - ek[:, h * HEAD_DIM + 3:h * HEAD_DIM + 4] for h in range(N_HEADS)]
        dcum = _with_lanes(jnp.zeros((tp, LANES), F32), lane, 0, cols)
        suffix = _scan_dot(_tri(tp, upper=True), dcum) + carry_ref[...]
        carry_ref[...] = suffix[0:1, :]
        x = fl_ref[...] + bf_ref[...]
        dfl = jnp.where(lane < N_HEADS, suffix / (1.0 + jnp.exp(x)), 0.0)
        dfl_ref[...] = dfl.astype(BF)
        dbf_ref[...] += jnp.sum(dfl, axis=0, keepdims=True)

    rev = lambda w: pl.BlockSpec((tp, w), lambda i: (n - 1 - i, 0))
    return pl.pallas_call(
        body, name="forget_bwd", grid=(n,),
        in_specs=[rev(ATTN_W), rev(ATTN_W), rev(LANES), _vec_spec(LANES)],
        out_specs=[rev(LANES), _vec_spec(LANES)],
        out_shape=[jax.ShapeDtypeStruct((T, LANES), BF), jax.ShapeDtypeStruct((1, LANES), F32)],
        scratch_shapes=[pltpu.VMEM((1, LANES), F32)], compiler_params=_params("arbitrary"),
    )(ext_q, ext_k, fl, b_forget)


def _adamw(w, g, m, v, *, name, tr=256):
    _, rows, cols = w.shape
    tr = tr if rows % tr == 0 else rows

    def body(w_ref, g_ref, m_ref, v_ref, go_ref, d_ref, nm_ref, nv_ref):
        gv = g_ref[...]
        go_ref[...] = gv
        nm = ADAM_B1 * m_ref[...] + (1.0 - ADAM_B1) * gv
        nv = ADAM_B2 * v_ref[...] + (1.0 - ADAM_B2) * (gv * gv)
        m_hat = nm / (1.0 - ADAM_B1 ** ADAM_STEP)
        v_hat = nv / (1.0 - ADAM_B2 ** ADAM_STEP)
        d_ref[...] = -ADAM_LR * (m_hat / (jnp.sqrt(v_hat) + ADAM_EPS) + ADAM_WD * w_ref[...])
        nm_ref[...] = nm
        nv_ref[...] = nv

    spec = pl.BlockSpec((None, tr, cols), lambda i: (0, i, 0))
    return pl.pallas_call(
        body, name=name, grid=(rows // tr,), in_specs=[spec, pl.BlockSpec((tr, cols), lambda i: (i, 0)), spec, spec],
        out_specs=[spec] * 4, out_shape=[jax.ShapeDtypeStruct((1, rows, cols), F32)] * 4,
        compiler_params=_params("parallel"),
    )(w, g, m, v)


HBM = pl.BlockSpec(memory_space=pltpu.HBM)
BF16_ROWS = 16


def _place():
    x, y, c = lax.axis_index("x"), lax.axis_index("y"), lax.axis_index("c")
    others = [(1 - x, y), (x, 1 - y), (1 - x, 1 - y)]
    return x, y, c, others


def _chip(xy):
    return 2 * xy[0] + xy[1]


def _row_halves(c, rows):
    half = rows // 2
    assert half % BF16_ROWS == 0
    return (pl.ds(pl.multiple_of(c * half, BF16_ROWS), half), pl.ds(pl.multiple_of((1 - c) * half, BF16_ROWS), half))


def _remote(src, dst, send_sems, recv_sems, k, to):
    return pltpu.make_async_remote_copy(src_ref=src, dst_ref=dst, send_sem=send_sems.at[k], recv_sem=recv_sems.at[k],
                                        device_id=to, device_id_type=MESH)


def _gathered_shapes(shards):
    return [jax.ShapeDtypeStruct((N_CHIPS,) + s.shape, s.dtype) for s in shards]


def _gather_semaphores(n):
    return [pltpu.SemaphoreType.DMA((6 * n,)), pltpu.SemaphoreType.DMA((6 * n,))]


def _gather_phases(w_refs, g_refs, send_sems, recv_sems):
    n = len(w_refs)
    x, y, c, others = _place()
    sibling, me = (x, y, 1 - c), _chip((x, y))
    halves = [_row_halves(c, w.shape[0]) for w in w_refs]

    def sent(a, j, o):
        mine, _ = halves[a]
        return _remote(w_refs[a].at[mine, :], g_refs[a].at[me, mine, :], send_sems, recv_sems, 6 * a + j, (*o, c))

    def passed(a, j, o):
        landed = g_refs[a].at[_chip(o), halves[a][0], :]
        return _remote(landed, landed, send_sems, recv_sems, 6 * a + 3 + j, sibling)

    def start():
        for a in range(n):
            for j, o in enumerate(others):
                sent(a, j, o).start()

    def forward():
        for j, o in enumerate(others):
            for a in range(n):
                landed = g_refs[a].at[_chip(o), halves[a][0], :]
                _remote(landed, landed, send_sems, recv_sems, 6 * a + j, (*o, c)).wait_recv()
                passed(a, j, o).start()

    def finish():
        for j, o in enumerate(others):
            for a in range(n):
                landed = g_refs[a].at[_chip(o), halves[a][1], :]
                _remote(landed, landed, send_sems, recv_sems, 6 * a + 3 + j, sibling).wait_recv()
        for a in range(n):
            for j, o in enumerate(others):
                sent(a, j, o).wait_send()
                passed(a, j, o).wait_send()

    return start, forward, finish


def _exchange_halves(arrays, *, name):
    n = len(arrays)

    def body(*refs):
        for phase in _exchange_phases(refs[:n], refs[n:2 * n], *refs[2 * n:]):
            phase()

    return pl.pallas_call(
        body, name=name, in_specs=[HBM] * n, out_specs=[HBM] * n, out_shape=_exchanged_shapes(arrays),
        scratch_shapes=_exchange_semaphores(n),
    )(*arrays)


def _exchanged_shapes(arrays):
    return [jax.ShapeDtypeStruct(s.shape[:-2] + (s.shape[-2] // 2, s.shape[-1]), F32) for s in arrays]


def _exchange_semaphores(n):
    return [pltpu.SemaphoreType.DMA((n,)), pltpu.SemaphoreType.DMA((n,))]


def _exchange_phases(g_refs, r_refs, send_sems, recv_sems):
    x, y, c, _ = _place()

    def copy(a):
        _, theirs = _row_halves(c, g_refs[a].shape[-2])
        src = g_refs[a].at[:, theirs, :] if len(g_refs[a].shape) == 3 else g_refs[a].at[theirs, :]
        return _remote(src, r_refs[a], send_sems, recv_sems, a, (x, y, 1 - c))

    def start():
        for a in range(len(g_refs)):
            copy(a).start()

    def finish():
        for a in range(len(g_refs)):
            copy(a).wait()

    return start, finish


def _scatter_to_owners(chip_sums):
    n = len(chip_sums)

    def body(*refs):
        for phase in _scatter_phases(refs[:n], refs[n:2 * n], *refs[2 * n:]):
            phase()

    return pl.pallas_call(
        body, name="scatter_to_owners", in_specs=[HBM] * n, out_specs=[HBM] * n,
        out_shape=_scattered_shapes(chip_sums), scratch_shapes=_scatter_semaphores(n),
    )(*chip_sums)


def _scattered_shapes(chip_sums):
    return [jax.ShapeDtypeStruct(b.shape if b.ndim == 3 else (N_CHIPS,) + b.shape, b.dtype) for b in chip_sums]


def _scatter_semaphores(n):
    return [pltpu.SemaphoreType.DMA((3 * n,)), pltpu.SemaphoreType.DMA((3 * n,))]


def _scatter_phases(b_refs, r_refs, send_sems, recv_sems):
    n = len(b_refs)
    x, y, c, others = _place()
    me = _chip((x, y))

    def sent(a, j, o):
        src = b_refs[a].at[_chip(o)] if len(b_refs[a].shape) == 3 else b_refs[a]
        return _remote(src, r_refs[a].at[me], send_sems, recv_sems, 3 * a + j, (*o, c))

    def start():
        for a in range(n):
            for j, o in enumerate(others):
                sent(a, j, o).start()

    def finish():
        for a in range(n):
            for j, o in enumerate(others):
                landed = r_refs[a].at[_chip(o)]
                _remote(landed, landed, send_sems, recv_sems, 3 * a + j, (*o, c)).wait_recv()
        for a in range(n):
            for j, o in enumerate(others):
                sent(a, j, o).wait_send()

    return start, finish


def _join_halves(totals):
    n = len(totals)

    def body(*refs):
        in_refs, out_refs, (send_sems, recv_sems) = refs[:n], refs[n:2 * n], refs[2 * n:]
        x, y, c, _ = _place()
        copies = []
        for a in range(n):
            mine, _ = _row_halves(c, in_refs[a].shape[0])
            copies.append(_remote(in_refs[a].at[mine, :], out_refs[a].at[mine, :], send_sems, recv_sems, a, (x, y, 1 - c)))
            copies[-1].start()
        for cp in copies:
            cp.wait()

    return pl.pallas_call(
        body, name="join_halves", in_specs=[HBM] * n, out_specs=[HBM] * n,
        out_shape=[jax.ShapeDtypeStruct(t.shape, F32) for t in totals], input_output_aliases={a: a for a in range(n)},
        scratch_shapes=[pltpu.SemaphoreType.DMA((n,)), pltpu.SemaphoreType.DMA((n,))],
    )(*totals)


ADD_ROWS = 128


def _add_sibling(g, r, place, *, name):
    lead, (half, cols) = g.shape[:-2], r.shape[-2:]
    tr = min(ADD_ROWS, half)
    nb = half // tr
    zeros = (0,) * len(lead)

    def body(place_ref, g_ref, r_ref, o_ref, ob_ref):
        s = g_ref[...] + r_ref[...]
        o_ref[...] = s
        ob_ref[...] = s.astype(BF)

    spec = pl.BlockSpec(lead + (tr, cols), lambda i, p: zeros + (i, 0))
    return pl.pallas_call(
        body, name=name,
        grid_spec=pltpu.PrefetchScalarGridSpec(
            num_scalar_prefetch=1, grid=(nb,),
            in_specs=[pl.BlockSpec(lead + (tr, cols), lambda i, p: zeros + (p[1] * nb + i, 0)), spec], out_specs=[spec, spec]),
        out_shape=[jax.ShapeDtypeStruct(r.shape, F32), jax.ShapeDtypeStruct(r.shape, BF)],
        compiler_params=_params("parallel"),
    )(place, g, r)


def _add_chips(own, received, place, *, name, own_slots):
    half, cols = received.shape[-2:]
    tr = min(ADD_ROWS, half)
    nb = half // tr

    def written(k, p):
        return jnp.where(p[0] == k, (k + 1) % N_CHIPS, k)

    def body(place_ref, own_ref, *refs):
        o_ref = refs[N_CHIPS]
        mine = own_ref[0] if own_slots else own_ref[...]
        if own_slots:
            acc = mine
            for k in range(N_CHIPS):
                acc = acc + jnp.where(place_ref[0] == k, 0.0, refs[k][0].astype(F32))
        else:
            terms = [jnp.where(place_ref[0] == k, mine, refs[k][0]) for k in range(N_CHIPS)]
            acc = ((terms[0] + terms[1]) + terms[2]) + terms[3]
        o_ref[...] = acc

    own_spec = (pl.BlockSpec((1, tr, cols), lambda i, p: (p[0], i, 0)) if own_slots
                else pl.BlockSpec((tr, cols), lambda i, p: (i, 0)))
    return pl.pallas_call(
        body, name=name,
        grid_spec=pltpu.PrefetchScalarGridSpec(
            num_scalar_prefetch=1, grid=(nb,),
            in_specs=[own_spec] + [pl.BlockSpec((1, tr, cols), functools.partial(lambda i, p, k: (written(k, p), i, 0), k=k))
                                   for k in range(N_CHIPS)],
            out_specs=pl.BlockSpec((tr, cols), lambda i, p: (p[1] * nb + i, 0))),
        out_shape=jax.ShapeDtypeStruct((2 * half, cols), F32), compiler_params=_params("parallel"),
    )(place, own, *([received] * N_CHIPS))


SHARDED = (("w_in", (D_MODEL, 4616), 1), ("w_branch_sgu", (SGU_W, D_MODEL), 1), ("w_branch_attn", (ATTN_W, D_MODEL), 1),
           ("w_out", (D_MODEL, D_MODEL), 0), ("w_up", (D_MODEL, D_FF), 1), ("w_down", (D_FF, D_MODEL), 0))
SMALL = (("g_mix_pre", (1, D_MODEL)), ("b_forget", (1, N_HEADS)), ("g_sgu", (1, SGU_W)), ("b_sgu", (1, SGU_W)),
         ("w_spatial", (N_GROUPS * CHUNK, CHUNK)), ("b_spatial", (N_GROUPS, CHUNK)), ("g_mix_post", (1, D_MODEL)),
         ("g_ffn_pre", (1, D_MODEL)), ("g_ffn_post", (1, D_MODEL)))
SMALL_ALIGN = 2 * ADD_ROWS


def _shard_shape(shape, axis):
    return tuple(s // N_CHIPS if a == axis else s for a, s in enumerate(shape))


def _slots_to_full(slots, axis):
    return slots.reshape(-1, slots.shape[2]) if axis == 0 else slots.transpose(1, 0, 2).reshape(slots.shape[1], -1)


def _full_to_slots(full, axis):
    if axis == 0:
        return full.reshape(N_CHIPS, -1, full.shape[1])
    return full.reshape(full.shape[0], N_CHIPS, -1).transpose(1, 0, 2)


def _small_rows(shape):
    return -(-(shape[0] * shape[1]) // (8 * LANES)) * 8


def _pack_small(values):
    parts = []
    for name, shape in SMALL:
        flat = values[name].reshape(-1)
        n = _small_rows(shape)
        parts.append(jnp.pad(flat, (0, n * LANES - flat.shape[0])).reshape(n, LANES))
    rows = sum(p.shape[0] for p in parts)
    pad = -(-rows // SMALL_ALIGN) * SMALL_ALIGN - rows
    return jnp.concatenate(parts + [jnp.zeros((pad, LANES), F32)], axis=0)


def _unpack_small(packed):
    out, row = {}, 0
    for name, shape in SMALL:
        n = _small_rows(shape)
        out[name] = packed[row:row + n].reshape(-1)[:shape[0] * shape[1]].reshape(shape)
        row += n
    return out


IN_Z, IN_Q, IN_K, IN_V, IN_F, IN_G, IN_END = 0, 1024, 1536, 2048, 2560, 2568, 4616


LATE_WEIGHTS = ("w_branch_sgu", "w_branch_attn", "w_out", "w_up", "w_down")
EARLY_GRADS = LATE_WEIGHTS


def _assemble(name, shard, gathered, chip):
    axis = {n: a for n, _, a in SHARDED}[name]
    slot = jnp.arange(N_CHIPS)[:, None, None]
    return _slots_to_full(jnp.where(slot == chip, shard[None], gathered), axis)


def _local_step(x, target, shards, small, place):
    b_forget = jnp.pad(small["b_forget"], ((0, 0), (0, LANES - N_HEADS)))
    causal = jnp.tril(jnp.ones((CHUNK, CHUNK), bool))
    ws = jnp.where(causal[None], small["w_spatial"].reshape(N_GROUPS, CHUNK, CHUNK), 0.0).astype(BF)
    ws_t = ws.transpose(0, 2, 1)
    bias_plane = jnp.repeat(small["b_spatial"].T, HEAD_DIM, axis=1)

    xn, (w_in_slots,) = _rms_fwd(x, small["g_mix_pre"], [shards["w_in"]])
    w_in = _assemble("w_in", shards["w_in"], w_in_slots, place[0])
    w_z, w_qkv, w_g = w_in[:, IN_Z:IN_Q], w_in[:, IN_Q:IN_F], w_in[:, IN_G:IN_END]
    w_q, w_k, w_v = w_in[:, IN_Q:IN_K], w_in[:, IN_K:IN_V], w_in[:, IN_V:IN_F]
    w_f = jnp.pad(w_in[:, IN_F:IN_G], ((0, 0), (0, LANES - N_HEADS)))
    z = _matmul([(xn, w_z)], nt=False, out_dtypes=[F32], name="proj_z")
    qkv = _matmul([(xn, w_qkv)], nt=False, out_dtypes=[BF], name="proj_qkv")
    gl = _matmul([(xn, w_g)], nt=False, out_dtypes=[BF], name="proj_gate")
    fl = _matmul([(xn, w_f)], nt=False, out_dtypes=[F32], name="proj_forget")
    ysgu = _sgu_fwd(z, small["g_sgu"], small["b_sgu"], ws, bias_plane)
    qf, kl, vl, tile_stats = _attn_prep(qkv, fl, b_forget)
    first_key_tile, last_query_tile, bounded = _attn_ranges(tile_stats)
    yattn, yattn_f, ql, gathered = _attn_fwd(qf, kl, vl, first_key_tile, bounded, [shards[name] for name in LATE_WEIGHTS])
    w = {name: _assemble(name, shards[name], got, place[0]) for name, got in zip(LATE_WEIGHTS, gathered, strict=True)}
    a, b, merged = _branch_merge(ysgu, yattn, w["w_branch_sgu"], w["w_branch_attn"], gl)
    o = _matmul([(merged, w["w_out"])], nt=False, out_dtypes=[F32], name="proj_out")
    h1, xn2 = _mixer_out_fwd(o, x, small["g_mix_post"], small["g_ffn_pre"])

    def relu2(acc):
        r = jnp.maximum(acc, 0.0)
        return (r * r,)

    hid = _matmul([(xn2, w["w_up"])], nt=False, out_dtypes=[BF], name="ffn_up", epilogue=relu2)
    dn = _matmul([(hid, w["w_down"])], nt=False, out_dtypes=[F32], name="ffn_down")
    sq, dy, ddn, dg_ffn_post = _loss_head(dn, h1, target, small["g_ffn_post"])

    dup = _matmul([(ddn, w["w_down"])], nt=True, out_dtypes=[BF], name="ffn_down_bwd",
                  epilogue=lambda acc, h: (acc * (2.0 * jnp.sqrt(h.astype(F32))),), extras=[hid])
    dw_down = _matmul_tn(hid, ddn, name="dw_down")
    dxn2 = _matmul([(dup, w["w_up"])], nt=True, out_dtypes=[F32], name="ffn_up_bwd")
    dw_up = _matmul_tn(xn2, dup, name="dw_up", slots=True)
    dh1, do, dg_ffn_pre, dg_mix_post = _mixer_out_bwd(h1, dxn2, dy, o, small["g_ffn_pre"], small["g_mix_post"])

    dmerged = _matmul([(do, w["w_out"])], nt=True, out_dtypes=[F32], name="proj_out_bwd")
    dw_out = _matmul_tn(merged, do, name="dw_out")
    da, db, dgla, dglb = _gate_bwd(dmerged, a, b, gl)
    dysgu = _matmul([(da, w["w_branch_sgu"])], nt=True, out_dtypes=[F32], name="branch_sgu_bwd")
    dyattn = _matmul([(db, w["w_branch_attn"])], nt=True, out_dtypes=[F32], name="branch_attn_bwd")
    dw_bs = _matmul_tn(ysgu, da, name="dw_branch_sgu")
    dw_ba = _matmul_tn(yattn, db, name="dw_branch_attn")
    early = {"w_branch_sgu": _full_to_slots(dw_bs, 1), "w_branch_attn": _full_to_slots(dw_ba, 1),
             "w_out": _full_to_slots(dw_out, 0), "w_up": dw_up, "w_down": _full_to_slots(dw_down, 0)}
    (dz, dws, dbs, dg_sgu, db_sgu), early_theirs = _sgu_bwd(
        dysgu, z, small["g_sgu"], small["b_sgu"], ws, ws_t, bias_plane, [early[name] for name in EARLY_GRADS])
    early_sums = {name: _add_sibling(early[name], theirs, place, name="add_sibling_" + name)
                  for name, theirs in zip(EARLY_GRADS, early_theirs, strict=True)}
    dout = _attn_bwd_prep(dyattn, yattn_f)
    (dq, dk, dv, ext_q, ext_k), early_received = _attn_bwd(
        kl, vl, ql, dout, last_query_tile, [early_sums[name][1] for name in EARLY_GRADS])
    dfl, dbf = _forget_bwd(ext_q, ext_k, fl, b_forget)
    dw_in = _full_to_slots(jnp.concatenate(
        [_matmul_tn(xn, dz, name="dw_in_z"), _matmul_tn(xn, dq, name="dw_in_q"), _matmul_tn(xn, dk, name="dw_in_k"),
         _matmul_tn(xn, dv, name="dw_in_v"), _matmul_tn(xn, dfl, name="dw_in_f")[:, :N_HEADS],
         _matmul_tn(xn, dgla, name="dw_in_ga"), _matmul_tn(xn, dglb, name="dw_in_gb")], axis=1), 1)
    (dw_in_theirs,) = _exchange_halves([dw_in], name="exchange_halves_w_in")
    dw_in_sum = _add_sibling(dw_in, dw_in_theirs, place, name="add_sibling_w_in")
    (dxn,), (dw_in_received,) = _matmul(
        [(dz, w_z), (dq, w_q), (dk, w_k), (dv, w_v), (dgla, w_g[:, :D_MODEL]), (dglb, w_g[:, D_MODEL:]), (dfl, w_f)],
        nt=True, out_dtypes=[F32], name="proj_in_bwd", scatter=[dw_in_sum[1]])
    dx, dg_mix_pre = _input_norm_bwd(x, dxn, dh1, small["g_mix_pre"])

    reduced = {name: (early_sums[name][0], got) for name, got in zip(EARLY_GRADS, early_received, strict=True)}
    reduced["w_in"] = (dw_in_sum[0], dw_in_received)
    small_grads = {"g_mix_pre": dg_mix_pre, "b_forget": dbf[:, :N_HEADS], "g_sgu": dg_sgu, "b_sgu": db_sgu,
                   "w_spatial": dws.reshape(N_GROUPS * CHUNK, CHUNK), "b_spatial": dbs[:, :N_GROUPS].T,
                   "g_mix_post": dg_mix_post, "g_ffn_pre": dg_ffn_pre, "g_ffn_post": dg_ffn_post}
    return sq, dx, reduced, small_grads


NAMES = ("g_mix_pre", "w_in", "b_forget", "g_sgu", "b_sgu", "w_spatial", "b_spatial", "w_branch_sgu", "w_branch_attn",
         "w_out", "g_mix_post", "g_ffn_pre", "w_up", "w_down", "g_ffn_post")


def kernel(x, g_mix_pre, w_in, b_forget, g_sgu, b_sgu, w_spatial, b_spatial, w_branch_sgu, w_branch_attn, w_out, g_mix_post, g_ffn_pre, w_up, w_down, g_ffn_post, loss_target, m_g_mix_pre, m_w_in, m_b_forget, m_g_sgu, m_b_sgu, m_w_spatial, m_b_spatial, m_w_branch_sgu, m_w_branch_attn, m_w_out, m_g_mix_post, m_g_ffn_pre, m_w_up, m_w_down, m_g_ffn_post, v_g_mix_pre, v_w_in, v_b_forget, v_g_sgu, v_b_sgu, v_w_spatial, v_b_spatial, v_w_branch_sgu, v_w_branch_attn, v_w_out, v_g_mix_post, v_g_ffn_pre, v_w_up, v_w_down, v_g_ffn_post):
    weights = dict(zip(NAMES, (g_mix_pre, w_in, b_forget, g_sgu, b_sgu, w_spatial, b_spatial, w_branch_sgu, w_branch_attn,
                               w_out, g_mix_post, g_ffn_pre, w_up, w_down, g_ffn_post), strict=True))
    first = dict(zip(NAMES, (m_g_mix_pre, m_w_in, m_b_forget, m_g_sgu, m_b_sgu, m_w_spatial, m_b_spatial, m_w_branch_sgu,
                             m_w_branch_attn, m_w_out, m_g_mix_post, m_g_ffn_pre, m_w_up, m_w_down, m_g_ffn_post), strict=True))
    second = dict(zip(NAMES, (v_g_mix_pre, v_w_in, v_b_forget, v_g_sgu, v_b_sgu, v_w_spatial, v_b_spatial, v_w_branch_sgu,
                              v_w_branch_attn, v_w_out, v_g_mix_post, v_g_ffn_pre, v_w_up, v_w_down, v_g_ffn_post), strict=True))
    shard_shapes = {name: _shard_shape(shape, axis) for name, shape, axis in SHARDED}
    small_shapes = dict(SMALL)
    view = lambda name, a: a.reshape(shard_shapes.get(name) or small_shapes[name])

    place = jnp.stack([2 * lax.axis_index("x") + lax.axis_index("y"), lax.axis_index("c")]).astype(jnp.int32)

    shards = {name: view(name, weights[name]).astype(BF) for name, _, _ in SHARDED}
    small = {name: view(name, weights[name]) for name, _ in SMALL}
    sq, dx, reduced, small_grads = _local_step(x[0], loss_target[0], shards, small, place)
    loss = lax.psum(0.5 * jnp.sum(sq) / D_MODEL, ("x", "y", "c"))

    small_mine = _pack_small(small_grads)
    (small_theirs,) = _exchange_halves([small_mine], name="exchange_halves_small")
    small_sum, _ = _add_sibling(small_mine, small_theirs, place, name="add_sibling_small")
    (small_received,) = _scatter_to_owners([small_sum])
    totals = {name: _add_chips(s, r, place, name="add_chips_" + name, own_slots=True) for name, (s, r) in reduced.items()}
    small_total = _add_chips(small_sum, small_received, place, name="add_chips_small", own_slots=False)
    joined = _join_halves([totals[name] for name, _, _ in SHARDED] + [small_total])
    grad = {**{name: g for (name, _, _), g in zip(SHARDED, joined[:-1], strict=True)}, **_unpack_small(joined[-1])}

    grad_out, delta, new_m, new_v = {}, {}, {}, {}
    for name in NAMES:
        rows, cols = grad[name].shape
        as_given = lambda a: a.reshape(1, rows, cols)
        grad_out[name], delta[name], new_m[name], new_v[name] = _adamw(
            as_given(weights[name]), grad[name], as_given(first[name]), as_given(second[name]), name="adamw_" + name)

    like = lambda d: [d[name].reshape(weights[name].shape) for name in NAMES]
    return (loss, dx[None], *like(grad_out), *like(delta), *like(new_m), *like(new_v))
```

```python
import functools

import jax
import jax.numpy as jnp
from jax import lax
from jax.experimental import pallas as pl
from jax.experimental.pallas import tpu as pltpu

F32 = jnp.float32
BF = jnp.bfloat16
MESH = pl.DeviceIdType.MESH

D_MODEL = 1024
N_HEADS = 8
HEAD_DIM = 64
ATTN_W = N_HEADS * HEAD_DIM
SGU_W = 512
N_GROUPS = 8
CHUNK = 128
D_FF = 4096
EPS = 1e-6
Q_SCALE = HEAD_DIM ** -0.5
N_CHIPS = 4
LANES = 128

ADAM_LR = 0.001
ADAM_B1 = 0.9
ADAM_B2 = 0.999
ADAM_EPS = 1e-08
ADAM_WD = 0.01
ADAM_STEP = 10

VMEM_LIMIT = 48 * 1024 * 1024
ATTN_BWD_VMEM = 58 * 1024 * 1024
NEG = -1e30

LANE_ROWSUM = HEAD_DIM
LANE_COLSUM = HEAD_DIM + 3


def _params(*sem):
    return pltpu.CompilerParams(dimension_semantics=sem, vmem_limit_bytes=VMEM_LIMIT)


def _dot(a, b):
    return jnp.dot(a, b, preferred_element_type=F32)


def _dot_nt(a, b):
    return lax.dot_general(a, b, (((1,), (1,)), ((), ())), preferred_element_type=F32)


def _dot_tn(a, b):
    return lax.dot_general(a, b, (((0,), (0,)), ((), ())), preferred_element_type=F32)


def _split3(c):
    hi = c.astype(BF).astype(F32)
    r = c - hi
    mid = r.astype(BF).astype(F32)
    lo = (r - mid).astype(BF).astype(F32)
    return hi, mid, lo


def _gelu(x):
    k = 0.7978845608028654
    return 0.5 * x * (1.0 + jnp.tanh(k * (x + 0.044715 * (x * x * x))))


def _gelu_grad(x):
    k = 0.7978845608028654
    x2 = x * x
    t = jnp.tanh(k * (x + 0.044715 * (x2 * x)))
    return 0.5 * (1.0 + t) + 0.5 * x * (1.0 - t * t) * (k * (1.0 + 3.0 * 0.044715 * x2))


def _rms_bwd(a, g, dy):
    r = lax.rsqrt(jnp.mean(a * a, axis=-1, keepdims=True) + EPS)
    n = a * r
    dn = dy * g
    da = r * (dn - n * jnp.mean(dn * n, axis=-1, keepdims=True))
    return da, dy * n


MM_ROWS = 1024
MM_COLS = 512
FFN_ROWS = 2048


def _matmul(pairs, *, nt, out_dtypes, name, tm=MM_ROWS, tn=MM_COLS, epilogue=None, extras=(), scatter=()):
    n_pairs, n_extra, n_out, n_scatter = len(pairs), len(extras), len(out_dtypes), len(scatter)
    M = pairs[0][0].shape[0]
    N = pairs[0][1].shape[0] if nt else pairs[0][1].shape[1]
    tm, tn = min(tm, M), min(tn, N)
    assert M % tm == 0 and N % tn == 0
    grid = (M // tm, N // tn)

    def body(*refs):
        n_in = 2 * n_pairs + n_extra
        if n_scatter:
            step = pl.program_id(0) * grid[1] + pl.program_id(1)
            first = n_in + n_scatter + n_out
            scatter_start, scatter_finish = _scatter_phases(
                refs[n_in:n_in + n_scatter], refs[first:first + n_scatter], *refs[first + n_scatter:])
            pl.when(step == 0)(scatter_start)
        acc = None
        for p in range(n_pairs):
            a_ref, b_ref = refs[2 * p], refs[2 * p + 1]
            d = _dot_nt(a_ref[...], b_ref[...]) if nt else _dot(a_ref[...], b_ref[...])
            acc = d if acc is None else acc + d
        e_refs = refs[2 * n_pairs:n_in]
        o_refs = refs[n_in + n_scatter:n_in + n_scatter + n_out]
        outs = (acc,) if epilogue is None else epilogue(acc, *[e[...] for e in e_refs])
        for o_ref, o in zip(o_refs, outs, strict=True):
            o_ref[...] = o.astype(o_ref.dtype)
        if n_scatter:
            pl.when(step == grid[0] * grid[1] - 1)(scatter_finish)

    in_specs, args = [], []
    for a, b in pairs:
        K = a.shape[1]
        in_specs.append(pl.BlockSpec((tm, K), lambda i, j: (i, 0)))
        in_specs.append(pl.BlockSpec((tn, K), lambda i, j: (j, 0)) if nt else pl.BlockSpec((K, tn), lambda i, j: (0, j)))
        args += [a, b]
    for e in extras:
        e, col = e if isinstance(e, tuple) else (e, 0)
        in_specs.append(pl.BlockSpec((tm, tn), functools.partial(lambda i, j, off: (i, j + off), off=col // tn)))
        args.append(e)
    order = ("arbitrary", "arbitrary") if n_scatter else ("parallel", "parallel")
    outs = pl.pallas_call(
        body, name=name, grid=grid, in_specs=in_specs + [HBM] * n_scatter,
        out_specs=[pl.BlockSpec((tm, tn), lambda i, j: (i, j)) for _ in out_dtypes] + [HBM] * n_scatter,
        out_shape=[jax.ShapeDtypeStruct((M, N), dt) for dt in out_dtypes] + (_scattered_shapes(scatter) if n_scatter else []),
        scratch_shapes=_scatter_semaphores(n_scatter) if n_scatter else [],
        compiler_params=_params(*order),
    )(*args, *scatter)
    if n_scatter:
        return outs[:n_out], outs[n_out:]
    return outs if len(outs) > 1 else outs[0]


def _matmul_tn(a, b, *, name, tm=1024, tn=1024, tk=2048, slots=False):
    T, K1 = a.shape
    N = b.shape[1]
    tm, tn, tk = min(tm, K1), min(tn, N // N_CHIPS if slots else N), min(tk, T)
    assert K1 % tm == 0 and (N // N_CHIPS if slots else N) % tn == 0 and T % tk == 0
    per_slot = N // N_CHIPS // tn

    def body(a_ref, b_ref, o_ref):
        @pl.when(pl.program_id(2) == 0)
        def _():
            o_ref[...] = jnp.zeros_like(o_ref)

        o_ref[...] += _dot_tn(a_ref[...], b_ref[...])

    if slots:
        out_spec = pl.BlockSpec((None, tm, tn), lambda i, j, k: (j // per_slot, i, j % per_slot))
        out_shape = jax.ShapeDtypeStruct((N_CHIPS, K1, N // N_CHIPS), F32)
    else:
        out_spec = pl.BlockSpec((tm, tn), lambda i, j, k: (i, j))
        out_shape = jax.ShapeDtypeStruct((K1, N), F32)
    return pl.pallas_call(
        body, name=name, grid=(K1 // tm, N // tn, T // tk),
        in_specs=[pl.BlockSpec((tk, tm), lambda i, j, k: (k, i)), pl.BlockSpec((tk, tn), lambda i, j, k: (k, j))],
        out_specs=out_spec, out_shape=out_shape,
        compiler_params=_params("parallel", "parallel", "arbitrary"),
    )(a, b)


def _branch_merge(ysgu, yattn, w_bs, w_ba, gl, *, tm=MM_ROWS, tn=MM_COLS):
    T = ysgu.shape[0]
    tm = min(tm, T)
    nj = D_MODEL // tn

    def body(ys_ref, ya_ref, wbs_ref, wba_ref, gla_ref, glb_ref, a_ref, b_ref, m_ref):
        a = _dot(ys_ref[...], wbs_ref[...])
        b = _dot(ya_ref[...], wba_ref[...])
        a_ref[...] = a.astype(BF)
        b_ref[...] = b.astype(BF)
        m_ref[...] = (jax.nn.sigmoid(gla_ref[...].astype(F32)) * a + jax.nn.sigmoid(glb_ref[...].astype(F32)) * b).astype(BF)

    return pl.pallas_call(
        body, name="branch_merge", grid=(T // tm, nj),
        in_specs=[
            pl.BlockSpec((tm, SGU_W), lambda i, j: (i, 0)),
            pl.BlockSpec((tm, ATTN_W), lambda i, j: (i, 0)),
            pl.BlockSpec((SGU_W, tn), lambda i, j: (0, j)),
            pl.BlockSpec((ATTN_W, tn), lambda i, j: (0, j)),
            pl.BlockSpec((tm, tn), lambda i, j: (i, j)),
            pl.BlockSpec((tm, tn), lambda i, j: (i, j + nj)),
        ],
        out_specs=[pl.BlockSpec((tm, tn), lambda i, j: (i, j))] * 3,
        out_shape=[jax.ShapeDtypeStruct((T, D_MODEL), BF)] * 3,
        compiler_params=_params("parallel", "parallel"),
    )(ysgu, yattn, w_bs, w_ba, gl, gl)


def _row_spec(tr, width):
    return pl.BlockSpec((tr, width), lambda i: (i, 0))


def _vec_spec(width):
    return pl.BlockSpec((1, width), lambda i: (0, 0))


def _rms_fwd(x, g, shards, *, tr=256):
    T = x.shape[0]
    tr = min(tr, T)
    n_steps = T // tr
    k = len(shards)

    def body(x_ref, g_ref, *refs):
        step = pl.program_id(0)
        gather_start, gather_forward, gather_finish = _gather_phases(refs[:k], refs[k + 1:2 * k + 1], *refs[2 * k + 1:])
        pl.when(step == 0)(gather_start)
        pl.when(step == (3 * n_steps) // 4)(gather_forward)
        xv = x_ref[...]
        r = lax.rsqrt(jnp.mean(xv * xv, axis=-1, keepdims=True) + EPS)
        refs[k][...] = ((xv * r) * g_ref[...]).astype(BF)
        pl.when(step == n_steps - 1)(gather_finish)

    outs = pl.pallas_call(
        body, name="rms_fwd", grid=(n_steps,),
        in_specs=[_row_spec(tr, D_MODEL), _vec_spec(D_MODEL)] + [HBM] * k, out_specs=[_row_spec(tr, D_MODEL)] + [HBM] * k,
        out_shape=[jax.ShapeDtypeStruct((T, D_MODEL), BF)] + _gathered_shapes(shards),
        scratch_shapes=_gather_semaphores(k), compiler_params=_params("arbitrary"),
    )(x, g, *shards)
    return outs[0], outs[1:]


def _mixer_out_fwd(o, x, g_post, g_pre, *, tr=256):
    T = x.shape[0]
    tr = min(tr, T)

    def body(o_ref, x_ref, gpost_ref, gpre_ref, h1_ref, xn2_ref):
        ov = o_ref[...]
        r = lax.rsqrt(jnp.mean(ov * ov, axis=-1, keepdims=True) + EPS)
        h1 = x_ref[...] + (ov * r) * gpost_ref[...]
        h1_ref[...] = h1
        r2 = lax.rsqrt(jnp.mean(h1 * h1, axis=-1, keepdims=True) + EPS)
        xn2_ref[...] = ((h1 * r2) * gpre_ref[...]).astype(BF)

    return pl.pallas_call(
        body, name="mixer_out_fwd", grid=(T // tr,),
        in_specs=[_row_spec(tr, D_MODEL), _row_spec(tr, D_MODEL), _vec_spec(D_MODEL), _vec_spec(D_MODEL)],
        out_specs=[_row_spec(tr, D_MODEL), _row_spec(tr, D_MODEL)],
        out_shape=[jax.ShapeDtypeStruct((T, D_MODEL), F32), jax.ShapeDtypeStruct((T, D_MODEL), BF)],
        compiler_params=_params("parallel"),
    )(o, x, g_post, g_pre)


def _loss_head(dn, h1, target, g_post, *, tr=256):
    T = dn.shape[0]
    tr = min(tr, T)

    def body(dn_ref, h1_ref, t_ref, g_ref, sq_ref, dy_ref, ddn_ref, dg_ref):
        @pl.when(pl.program_id(0) == 0)
        def _():
            sq_ref[...] = jnp.zeros_like(sq_ref)
            dg_ref[...] = jnp.zeros_like(dg_ref)

        a = dn_ref[...]
        g = g_ref[...]
        r = lax.rsqrt(jnp.mean(a * a, axis=-1, keepdims=True) + EPS)
        err = h1_ref[...] + (a * r) * g - t_ref[...]
        sq_ref[...] += jnp.sum(err * err, axis=0, keepdims=True)
        dy = err * (1.0 / D_MODEL)
        dy_ref[...] = dy
        da, dgp = _rms_bwd(a, g, dy)
        ddn_ref[...] = da.astype(BF)
        dg_ref[...] += jnp.sum(dgp, axis=0, keepdims=True)

    return pl.pallas_call(
        body, name="loss_head", grid=(T // tr,),
        in_specs=[_row_spec(tr, D_MODEL)] * 3 + [_vec_spec(D_MODEL)],
        out_specs=[_vec_spec(D_MODEL), _row_spec(tr, D_MODEL), _row_spec(tr, D_MODEL), _vec_spec(D_MODEL)],
        out_shape=[jax.ShapeDtypeStruct((1, D_MODEL), F32), jax.ShapeDtypeStruct((T, D_MODEL), F32),
                   jax.ShapeDtypeStruct((T, D_MODEL), BF), jax.ShapeDtypeStruct((1, D_MODEL), F32)],
        compiler_params=_params("arbitrary"),
    )(dn, h1, target, g_post)


def _mixer_out_bwd(h1, dxn2, dy, o, g_pre, g_post, *, tr=256):
    T = h1.shape[0]
    tr = min(tr, T)

    def body(h1_ref, dxn2_ref, dy_ref, o_ref, gpre_ref, gpost_ref, dh1_ref, do_ref, dgpre_ref, dgpost_ref):
        @pl.when(pl.program_id(0) == 0)
        def _():
            dgpre_ref[...] = jnp.zeros_like(dgpre_ref)
            dgpost_ref[...] = jnp.zeros_like(dgpost_ref)

        da, dgp = _rms_bwd(h1_ref[...], gpre_ref[...], dxn2_ref[...])
        dh1 = dy_ref[...] + da
        dh1_ref[...] = dh1
        dgpre_ref[...] += jnp.sum(dgp, axis=0, keepdims=True)
        do, dgp2 = _rms_bwd(o_ref[...], gpost_ref[...], dh1)
        do_ref[...] = do.astype(BF)
        dgpost_ref[...] += jnp.sum(dgp2, axis=0, keepdims=True)

    return pl.pallas_call(
        body, name="mixer_out_bwd", grid=(T // tr,),
        in_specs=[_row_spec(tr, D_MODEL)] * 4 + [_vec_spec(D_MODEL)] * 2,
        out_specs=[_row_spec(tr, D_MODEL), _row_spec(tr, D_MODEL), _vec_spec(D_MODEL), _vec_spec(D_MODEL)],
        out_shape=[jax.ShapeDtypeStruct((T, D_MODEL), F32), jax.ShapeDtypeStruct((T, D_MODEL), BF),
                   jax.ShapeDtypeStruct((1, D_MODEL), F32), jax.ShapeDtypeStruct((1, D_MODEL), F32)],
        compiler_params=_params("arbitrary"),
    )(h1, dxn2, dy, o, g_pre, g_post)


def _input_norm_bwd(x, dxn, dh1, g, *, tr=256):
    T = x.shape[0]
    tr = min(tr, T)

    def body(x_ref, dxn_ref, dh1_ref, g_ref, dx_ref, dg_ref):
        @pl.when(pl.program_id(0) == 0)
        def _():
            dg_ref[...] = jnp.zeros_like(dg_ref)

        da, dgp = _rms_bwd(x_ref[...], g_ref[...], dxn_ref[...])
        dx_ref[...] = dh1_ref[...] + da
        dg_ref[...] += jnp.sum(dgp, axis=0, keepdims=True)

    return pl.pallas_call(
        body, name="input_norm_bwd", grid=(T // tr,),
        in_specs=[_row_spec(tr, D_MODEL)] * 3 + [_vec_spec(D_MODEL)],
        out_specs=[_row_spec(tr, D_MODEL), _vec_spec(D_MODEL)],
        out_shape=[jax.ShapeDtypeStruct((T, D_MODEL), F32), jax.ShapeDtypeStruct((1, D_MODEL), F32)],
        compiler_params=_params("arbitrary"),
    )(x, dxn, dh1, g)


def _sgu_norm(z_tile, g, b):
    gz = _gelu(z_tile)
    u, vv = gz[:, :SGU_W], gz[:, SGU_W:]
    xc = vv - jnp.mean(vv, axis=-1, keepdims=True)
    rstd = lax.rsqrt(jnp.mean(xc * xc, axis=-1, keepdims=True) + EPS)
    xhat = xc * rstd
    return u, xhat, rstd, xhat * g + b


def _sgu_mix(w_ref, v_bf, first_half):
    parts = []
    for p in range(N_GROUPS // 2):
        vp = v_bf[:, p * LANES:(p + 1) * LANES]
        parts.append(jnp.where(first_half, _dot(w_ref[2 * p], vp), _dot(w_ref[2 * p + 1], vp)))
    return jnp.concatenate(parts, axis=1)


def _sgu_fwd(z, g_sgu, b_sgu, ws, bias_plane, *, tm=512):
    T = z.shape[0]
    tm = min(tm, T)

    def body(z_ref, g_ref, b_ref, ws_ref, bp_ref, y_ref):
        u, _, _, vn = _sgu_norm(z_ref[...], g_ref[...], b_ref[...])
        vn_bf = vn.astype(BF)
        first_half = lax.broadcasted_iota(jnp.int32, (CHUNK, LANES), 1) < HEAD_DIM
        for c in range(tm // CHUNK):
            rows = slice(c * CHUNK, (c + 1) * CHUNK)
            s = _sgu_mix(ws_ref, vn_bf[rows, :], first_half) + bp_ref[...]
            y_ref[rows, :] = (u[rows, :] * s).astype(BF)

    return pl.pallas_call(
        body, name="sgu_fwd", grid=(T // tm,),
        in_specs=[_row_spec(tm, 2 * SGU_W), _vec_spec(SGU_W), _vec_spec(SGU_W),
                  pl.BlockSpec((N_GROUPS, CHUNK, CHUNK), lambda i: (0, 0, 0)),
                  pl.BlockSpec((CHUNK, SGU_W), lambda i: (0, 0))],
        out_specs=_row_spec(tm, SGU_W), out_shape=jax.ShapeDtypeStruct((T, SGU_W), BF),
        compiler_params=_params("parallel"),
    )(z, g_sgu, b_sgu, ws, bias_plane)


def _sgu_bwd(dy, z, g_sgu, b_sgu, ws, ws_t, bias_plane, exchange, *, tm=512):
    T = z.shape[0]
    tm = min(tm, T)
    n_steps = T // tm
    k = len(exchange)

    def body(dy_ref, z_ref, g_ref, b_ref, ws_ref, wst_ref, bp_ref, *refs):
        x_refs, (dz_ref, dws_ref, dbs_ref, dg_ref, db_ref), r_refs = refs[:k], refs[k:k + 5], refs[k + 5:2 * k + 5]
        dbp_ref, send_sems, recv_sems = refs[2 * k + 5:]
        step = pl.program_id(0)
        exchange_start, exchange_finish = _exchange_phases(x_refs, r_refs, send_sems, recv_sems)
        pl.when(step == 0)(exchange_start)

        @pl.when(step == 0)
        def _():
            dws_ref[...] = jnp.zeros_like(dws_ref)
            dg_ref[...] = jnp.zeros_like(dg_ref)
            db_ref[...] = jnp.zeros_like(db_ref)
            dbp_ref[...] = jnp.zeros_like(dbp_ref)

        g = g_ref[...]
        zt = z_ref[...]
        u, xhat, rstd, vn = _sgu_norm(zt, g, b_ref[...])
        vn_bf = vn.astype(BF)
        first_half = lax.broadcasted_iota(jnp.int32, (CHUNK, LANES), 1) < HEAD_DIM
        dyv = dy_ref[...]
        dg_acc = jnp.zeros((1, SGU_W), F32)
        db_acc = jnp.zeros((1, SGU_W), F32)
        for c in range(tm // CHUNK):
            rows = slice(c * CHUNK, (c + 1) * CHUNK)
            v_c = vn_bf[rows, :]
            s = _sgu_mix(ws_ref, v_c, first_half) + bp_ref[...]
            dy_c = dyv[rows, :]
            du = dy_c * s
            dsv = dy_c * u[rows, :]
            dbp_ref[...] += dsv
            ds_bf = dsv.astype(BF)
            zero = jnp.zeros((CHUNK, LANES), BF)
            for p in range(N_GROUPS // 2):
                dsp = ds_bf[:, p * LANES:(p + 1) * LANES]
                vp = v_c[:, p * LANES:(p + 1) * LANES]
                dws_ref[2 * p] += _dot_nt(jnp.where(first_half, dsp, zero), vp)
                dws_ref[2 * p + 1] += _dot_nt(jnp.where(first_half, zero, dsp), vp)
            dvn = _sgu_mix(wst_ref, ds_bf, first_half)
            xh = xhat[rows, :]
            dxh = dvn * g
            dvv = rstd[rows, :] * (dxh - jnp.mean(dxh, axis=-1, keepdims=True)
                                   - xh * jnp.mean(dxh * xh, axis=-1, keepdims=True))
            dg_acc += jnp.sum(dvn * xh, axis=0, keepdims=True)
            db_acc += jnp.sum(dvn, axis=0, keepdims=True)
            dgz = jnp.concatenate([du, dvv], axis=1)
            dz_ref[rows, :] = (dgz * _gelu_grad(zt[rows, :])).astype(BF)
        dg_ref[...] += dg_acc
        db_ref[...] += db_acc

        @pl.when(step == n_steps - 1)
        def _():
            r = lax.broadcasted_iota(jnp.int32, (CHUNK, CHUNK), 0)
            cidx = lax.broadcasted_iota(jnp.int32, (CHUNK, CHUNK), 1)
            causal = (cidx <= r).astype(F32)
            for gi in range(N_GROUPS):
                dws_ref[gi] = dws_ref[gi] * causal
            lane = lax.broadcasted_iota(jnp.int32, (CHUNK, LANES), 1)
            out = jnp.zeros((CHUNK, LANES), F32)
            dbp = dbp_ref[...]
            for gi in range(N_GROUPS):
                col = jnp.sum(dbp[:, gi * HEAD_DIM:(gi + 1) * HEAD_DIM], axis=1, keepdims=True)
                out = jnp.where(lane == gi, col, out)
            dbs_ref[...] = out
            exchange_finish()

    w_spec = pl.BlockSpec((N_GROUPS, CHUNK, CHUNK), lambda i: (0, 0, 0))
    plane = pl.BlockSpec((CHUNK, SGU_W), lambda i: (0, 0))
    outs = pl.pallas_call(
        body, name="sgu_bwd", grid=(n_steps,),
        in_specs=[_row_spec(tm, SGU_W), _row_spec(tm, 2 * SGU_W), _vec_spec(SGU_W), _vec_spec(SGU_W), w_spec, w_spec, plane]
        + [HBM] * k,
        out_specs=[_row_spec(tm, 2 * SGU_W), w_spec, pl.BlockSpec((CHUNK, LANES), lambda i: (0, 0)),
                   _vec_spec(SGU_W), _vec_spec(SGU_W)] + [HBM] * k,
        out_shape=[jax.ShapeDtypeStruct((T, 2 * SGU_W), BF), jax.ShapeDtypeStruct((N_GROUPS, CHUNK, CHUNK), F32),
                   jax.ShapeDtypeStruct((CHUNK, LANES), F32), jax.ShapeDtypeStruct((1, SGU_W), F32),
                   jax.ShapeDtypeStruct((1, SGU_W), F32)] + _exchanged_shapes(exchange),
        scratch_shapes=[pltpu.VMEM((CHUNK, SGU_W), F32)] + _exchange_semaphores(k),
        compiler_params=_params("arbitrary"),
    )(dy, z, g_sgu, b_sgu, ws, ws_t, bias_plane, *exchange)
    return outs[:5], outs[5:]


def _tri(n, upper):
    r = lax.broadcasted_iota(jnp.int32, (n, n), 0)
    c = lax.broadcasted_iota(jnp.int32, (n, n), 1)
    return ((c >= r) if upper else (c <= r)).astype(BF)


def _scan_dot(tri, x):
    hi, mid, lo = _split3(x)
    return (_dot(tri, hi.astype(BF)) + _dot(tri, mid.astype(BF))) + _dot(tri, lo.astype(BF))


def _with_lanes(base, lane, start, cols):
    out = base
    for k, col in enumerate(cols):
        if col is not None:
            out = jnp.where(lane == start + k, col, out)
    return out


def _logit_bound(q_norm, k_norm):
    return NORM_SLACK * q_norm * k_norm + 1.0


ATTN_TILE = 512
SKIP_BELOW = -110.0
NORM_SLACK = 1.001
BOUNDED_GAP = 60.0


def _attn_prep(qkv, fl, b_forget, *, tp=ATTN_TILE):
    T = qkv.shape[0]
    tp = min(tp, T)

    def body(qkv_ref, fl_ref, bf_ref, qf_ref, kl_ref, vl_ref, st_ref, carry_ref, kmax_ref):
        @pl.when(pl.program_id(0) == 0)
        def _():
            carry_ref[...] = jnp.zeros_like(carry_ref)
            kmax_ref[...] = jnp.zeros_like(kmax_ref)

        x = fl_ref[...] + bf_ref[...]
        logf = jnp.minimum(x, 0.0) - jnp.log(1.0 + jnp.exp(-jnp.abs(x)))
        cum = _scan_dot(_tri(tp, upper=False), logf) + carry_ref[...]
        carry_ref[...] = cum[tp - 1:tp, :]
        lane = lax.broadcasted_iota(jnp.int32, (tp, HEAD_DIM), 1)
        ones3 = jnp.where(lane < 3, 1.0, 0.0)
        qkvv = qkv_ref[...]
        st_row = lax.broadcasted_iota(jnp.int32, (N_HEADS, LANES), 0)
        st_lane = lax.broadcasted_iota(jnp.int32, (N_HEADS, LANES), 1)
        stats = jnp.zeros((N_HEADS, LANES), F32)
        kmax_lane = lax.broadcasted_iota(jnp.int32, (1, LANES), 1)
        for h in range(N_HEADS):
            ch = cum[:, h:h + 1]
            c3 = _split3(ch)
            qh = qkvv[:, h * HEAD_DIM:(h + 1) * HEAD_DIM].astype(F32) * Q_SCALE
            kh = qkvv[:, ATTN_W + h * HEAD_DIM:ATTN_W + (h + 1) * HEAD_DIM].astype(F32)
            vh = qkvv[:, 2 * ATTN_W + h * HEAD_DIM:2 * ATTN_W + (h + 1) * HEAD_DIM].astype(F32)
            q_norm = jnp.sqrt(jnp.sum(qh * qh, axis=1, keepdims=True))
            qn = jnp.max(q_norm, axis=0, keepdims=True)
            kn = jnp.sqrt(jnp.max(jnp.sum(kh * kh, axis=1, keepdims=True), axis=0, keepdims=True))
            k_seen = jnp.maximum(kmax_ref[:, h:h + 1], kn)
            kmax_ref[...] = jnp.where(kmax_lane == h, k_seen, kmax_ref[...])
            bound3 = _split3(-_logit_bound(q_norm, k_seen))
            ext_q = _with_lanes(jnp.where((lane >= 3) & (lane < 6), 1.0, 0.0), lane, 0, list(c3) + [None] * 3 + list(bound3))
            ext_k = _with_lanes(jnp.where((lane < 3) | ((lane >= 6) & (lane < 9)), 1.0, 0.0), lane, 3, [-c for c in c3])
            qf_ref[h] = jnp.concatenate([qh, ext_q], axis=1).astype(BF)
            kl_ref[h] = jnp.concatenate([kh, ext_k], axis=1).astype(BF)
            vl_ref[h] = jnp.concatenate([vh, ones3], axis=1).astype(BF)
            tile_stats = (qn, kn, jnp.max(ch, axis=0, keepdims=True), jnp.min(ch, axis=0, keepdims=True), k_seen)
            for k, val in enumerate(tile_stats):
                stats = jnp.where((st_row == h) & (st_lane == k), val, stats)
        st_ref[0] = stats

    head_spec = pl.BlockSpec((N_HEADS, tp, LANES), lambda i: (0, i, 0))
    return pl.pallas_call(
        body, name="attn_prep", grid=(T // tp,),
        in_specs=[_row_spec(tp, 3 * ATTN_W), _row_spec(tp, LANES), _vec_spec(LANES)],
        out_specs=[head_spec] * 3 + [pl.BlockSpec((1, N_HEADS, LANES), lambda i: (i, 0, 0))],
        out_shape=[jax.ShapeDtypeStruct((N_HEADS, T, LANES), BF)] * 3 + [jax.ShapeDtypeStruct((T // tp, N_HEADS, LANES), F32)],
        scratch_shapes=[pltpu.VMEM((1, LANES), F32), pltpu.VMEM((1, LANES), F32)], compiler_params=_params("arbitrary"),
    )(qkv, fl, b_forget)


def _attn_ranges(stats):
    qn, kn, cmax, cmin, k_seen = (stats[:, :, k].T for k in range(5))
    n = qn.shape[1]
    bounded = (2.0 * _logit_bound(qn, k_seen) <= BOUNDED_GAP).reshape(N_HEADS // 2, 2, n).all(axis=1)
    reach = NORM_SLACK * qn * (jnp.max(kn, axis=1, keepdims=True) + kn) + cmax
    i = jnp.arange(n)[None, :, None]
    j = jnp.arange(n)[None, None, :]
    need = ((reach[:, :, None] - cmin[:, None, :] >= SKIP_BELOW) | (i == j)) & (j <= i)
    first = jnp.min(jnp.where(need, j, n), axis=2).reshape(N_HEADS // 2, 2, n).min(axis=1)
    last = jnp.max(jnp.where(need, i, -1), axis=1).reshape(N_HEADS // 2, 2, n).max(axis=1)
    return first.reshape(-1).astype(F32), last.reshape(-1).astype(F32), bounded.reshape(-1).astype(F32)


def _pair_block(t):
    return pl.BlockSpec((2, t, LANES), lambda p, i, *_: (p, i, 0))


def _pair_full(T):
    return pl.BlockSpec((2, T, LANES), lambda p, i, *_: (p, 0, 0))


def _packed_block(t):
    return pl.BlockSpec((t, LANES), lambda p, i, *_: (i, p))


def _causal(t, keys_in_rows=False):
    r = lax.broadcasted_iota(jnp.int32, (t, t), 0)
    c = lax.broadcasted_iota(jnp.int32, (t, t), 1)
    return (r <= c) if keys_in_rows else (c <= r)


def _tile_rows(j, t):
    return pl.ds(pl.multiple_of(j * t, t), t)


def _attn_call(body, name, tile_scalars, operands, in_specs, out_specs, out_shape, scratch_shapes, n_tiles):
    return pl.pallas_call(
        body, name=name,
        grid_spec=pltpu.PrefetchScalarGridSpec(
            num_scalar_prefetch=len(tile_scalars), grid=(N_HEADS // 2, n_tiles), in_specs=in_specs, out_specs=out_specs,
            scratch_shapes=scratch_shapes),
        out_shape=out_shape, compiler_params=_params("arbitrary", "arbitrary"),
    )(*tile_scalars, *operands)


def _attn_fwd(qf, kl, vl, first, bounded, shards, *, tq=ATTN_TILE):
    T = qf.shape[1]
    tq = min(tq, T)
    n = T // tq
    n_steps = (N_HEADS // 2) * n
    k = len(shards)

    def body(first_ref, bounded_ref, qf_ref, kl_ref, vl_ref, *refs):
        w_refs, (o_ref, of_ref, ql_ref), g_refs = refs[:k], refs[k:k + 3], refs[k + 3:2 * k + 3]
        m_ref, acc_ref, send_sems, recv_sems = refs[2 * k + 3:]
        i = pl.program_id(1)
        tile = pl.program_id(0) * n + i
        gather_start, gather_forward, gather_finish = _gather_phases(w_refs, g_refs, send_sems, recv_sems)
        pl.when(tile == 0)(gather_start)
        pl.when(tile == (3 * n_steps) // 4)(gather_forward)
        start = first_ref[tile].astype(jnp.int32)
        is_bounded = bounded_ref[tile] > 0.5
        acc_ref[...] = jnp.zeros_like(acc_ref)
        diagonal = _tile_rows(i, tq)
        causal = _causal(tq)

        def logits(hh, rows):
            return _dot_nt(qf_ref[hh], kl_ref[hh, rows, :])

        @pl.when(is_bounded)
        def _():
            m_ref[...] = jnp.zeros_like(m_ref)

            def update(hh, s, rows):
                acc_ref[hh] += _dot(jnp.exp(s).astype(BF), vl_ref[hh, rows, :])

            def step(j, carry):
                for hh in range(2):
                    update(hh, logits(hh, _tile_rows(j, tq)), _tile_rows(j, tq))
                return carry

            lax.fori_loop(start, i, step, 0)
            for hh in range(2):
                update(hh, jnp.where(causal, logits(hh, diagonal), NEG), diagonal)

        @pl.when(jnp.logical_not(is_bounded))
        def _():
            m_ref[...] = jnp.full_like(m_ref, NEG)

            def update(hh, s, rows):
                m_old = m_ref[hh]
                m_new = jnp.maximum(m_old, jnp.max(s, axis=1, keepdims=True))
                p = jnp.exp(s - m_new)
                acc_ref[hh] = jnp.exp(m_old - m_new) * acc_ref[hh] + _dot(p.astype(BF), vl_ref[hh, rows, :])
                m_ref[hh] = m_new

            def step(j, carry):
                for hh in range(2):
                    update(hh, logits(hh, _tile_rows(j, tq)), _tile_rows(j, tq))
                return carry

            lax.fori_loop(start, i, step, 0)
            for hh in range(2):
                update(hh, jnp.where(causal, logits(hh, diagonal), NEG), diagonal)

        lane = lax.broadcasted_iota(jnp.int32, (tq, LANES), 1)
        outs = []
        for hh in range(2):
            q = qf_ref[hh].astype(F32)
            acc = acc_ref[hh]
            l = acc[:, HEAD_DIM:HEAD_DIM + 1]
            outs.append(acc[:, :HEAD_DIM] / l)
            at = HEAD_DIM + 6
            neg_bound = (q[:, at:at + 1] + q[:, at + 1:at + 2]) + q[:, at + 2:at + 3]
            ql_ref[hh] = _with_lanes(q, lane, at, _split3(neg_bound - (m_ref[hh] + jnp.log(l)))).astype(BF)
        o = jnp.concatenate(outs, axis=1)
        o_ref[...] = o.astype(BF)
        of_ref[...] = o
        pl.when(tile == n_steps - 1)(gather_finish)

    outs = _attn_call(
        body, "attn_fwd", (first, bounded), (qf, kl, vl, *shards),
        [_pair_block(tq), _pair_full(T), _pair_full(T)] + [HBM] * k,
        [_packed_block(tq), _packed_block(tq), _pair_block(tq)] + [HBM] * k,
        [jax.ShapeDtypeStruct((T, ATTN_W), BF), jax.ShapeDtypeStruct((T, ATTN_W), F32),
         jax.ShapeDtypeStruct((N_HEADS, T, LANES), BF)] + _gathered_shapes(shards),
        [pltpu.VMEM((2, tq, 1), F32), pltpu.VMEM((2, tq, LANES), F32)] + _gather_semaphores(k), n)
    return outs[0], outs[1], outs[2], outs[3:]


def _attn_bwd_prep(dya, of, *, tr=256):
    T = dya.shape[0]
    tr = min(tr, T)

    def body(d_ref, o_ref, do_ref):
        lane = lax.broadcasted_iota(jnp.int32, (tr, HEAD_DIM), 1)
        dv, ov = d_ref[...], o_ref[...]
        for h in range(N_HEADS):
            d = dv[:, h * HEAD_DIM:(h + 1) * HEAD_DIM]
            delta = jnp.sum(d * ov[:, h * HEAD_DIM:(h + 1) * HEAD_DIM], axis=1, keepdims=True)
            ext = _with_lanes(jnp.zeros((tr, HEAD_DIM), F32), lane, 0, _split3(-delta))
            do_ref[h] = jnp.concatenate([d, ext], axis=1).astype(BF)

    return pl.pallas_call(
        body, name="attn_bwd_prep", grid=(T // tr,),
        in_specs=[_row_spec(tr, ATTN_W), _row_spec(tr, ATTN_W)],
        out_specs=pl.BlockSpec((N_HEADS, tr, LANES), lambda i: (0, i, 0)),
        out_shape=jax.ShapeDtypeStruct((N_HEADS, T, LANES), BF), compiler_params=_params("parallel"),
    )(dya, of)


def _attn_bwd(kl, vl, ql, do, last, chip_sums, *, tk=ATTN_TILE):
    T = ql.shape[1]
    tk = min(tk, T)
    n = T // tk
    n_steps = (N_HEADS // 2) * n
    m = len(chip_sums)

    def body(last_ref, kl_ref, vl_ref, ql_ref, do_ref, *refs):
        b_refs, (dq_ref, dk_ref, dv_ref, extq_ref, extk_ref), r_refs = refs[:m], refs[m:m + 5], refs[m + 5:2 * m + 5]
        dq_acc, dk_acc, dv_acc, send_sems, recv_sems = refs[2 * m + 5:]
        j = pl.program_id(1)
        tile = pl.program_id(0) * n + j
        scatter_start, scatter_finish = _scatter_phases(b_refs, r_refs, send_sems, recv_sems)
        pl.when(tile == 0)(scatter_start)

        @pl.when(j == 0)
        def _():
            dq_acc[...] = jnp.zeros_like(dq_acc)

        dk_acc[...] = jnp.zeros_like(dk_acc)
        dv_acc[...] = jnp.zeros_like(dv_acc)

        def block(hh, rows, mask):
            qi, di, k = ql_ref[hh, rows, :], do_ref[hh, rows, :], kl_ref[hh]
            p_t = jnp.exp(_dot_nt(k, qi))
            if mask is not None:
                p_t = jnp.where(mask, p_t, 0.0)
            ds_t = (p_t * _dot_nt(vl_ref[hh], di)).astype(BF)
            dk_acc[hh] += _dot(ds_t, qi)
            dv_acc[hh] += _dot(p_t.astype(BF), di)
            dq_acc[hh, rows, :] += _dot_tn(ds_t, k)

        causal_t = _causal(tk, keys_in_rows=True)
        for hh in range(2):
            block(hh, _tile_rows(j, tk), causal_t)

        def step(i, carry):
            for hh in range(2):
                block(hh, _tile_rows(i, tk), None)
            return carry

        lax.fori_loop(j + 1, last_ref[pl.program_id(0) * n + j].astype(jnp.int32) + 1, step, 0)
        dk_ref[...] = jnp.concatenate([dk_acc[hh][:, :HEAD_DIM] for hh in range(2)], axis=1).astype(BF)
        dv_ref[...] = jnp.concatenate([dv_acc[hh][:, :HEAD_DIM] for hh in range(2)], axis=1).astype(BF)
        extk_ref[...] = jnp.concatenate([dk_acc[hh][:, HEAD_DIM:] for hh in range(2)], axis=1)

        @pl.when(j == n - 1)
        def _():
            dq_ref[...] = jnp.concatenate([dq_acc[hh][:, :HEAD_DIM] * Q_SCALE for hh in range(2)], axis=1).astype(BF)
            extq_ref[...] = jnp.concatenate([dq_acc[hh][:, HEAD_DIM:] for hh in range(2)], axis=1)

        pl.when(tile == n_steps - 1)(scatter_finish)

    whole = pl.BlockSpec((T, LANES), lambda p, j, *_: (0, p))
    outs = pl.pallas_call(
        body, name="attn_bwd",
        grid_spec=pltpu.PrefetchScalarGridSpec(
            num_scalar_prefetch=1, grid=(N_HEADS // 2, n),
            in_specs=[_pair_block(tk), _pair_block(tk), _pair_full(T), _pair_full(T)] + [HBM] * m,
            out_specs=[whole, _packed_block(tk), _packed_block(tk), whole, _packed_block(tk)] + [HBM] * m,
            scratch_shapes=[pltpu.VMEM((2, T, LANES), F32), pltpu.VMEM((2, tk, LANES), F32), pltpu.VMEM((2, tk, LANES), F32)]
            + _scatter_semaphores(m)),
        out_shape=[jax.ShapeDtypeStruct((T, ATTN_W), BF)] * 3 + [jax.ShapeDtypeStruct((T, ATTN_W), F32)] * 2
        + _scattered_shapes(chip_sums),
        compiler_params=pltpu.CompilerParams(dimension_semantics=("arbitrary", "arbitrary"), vmem_limit_bytes=ATTN_BWD_VMEM),
    )(last, kl, vl, ql, do, *chip_sums)
    return outs[:5], outs[5:]


def _forget_bwd(ext_q, ext_k, fl, b_forget, *, tp=256):
    T = fl.shape[0]
    tp = min(tp, T)
    n = T // tp

    def body(eq_ref, ek_ref, fl_ref, bf_ref, dfl_ref, dbf_ref, carry_ref):
        @pl.when(pl.program_id(0) == 0)
        def _():
            carry_ref[...] = jnp.zeros_like(carry_ref)
            dbf_ref[...] = jnp.zeros_like(dbf_ref)

        lane = lax.broadcasted_iota(jnp.int32, (tp, LANES), 1)
        eq, ek = eq_ref[...], ek_ref[...]
        cols = [eq[:, h * HEAD_DIM:h * HEAD_DIM + 1] - ek[:, h * HEAD_DIM + 3:h * HEAD_DIM + 4] for h in range(N_HEADS)]
        dcum = _with_lanes(jnp.zeros((tp, LANES), F32), lane, 0, cols)
        suffix = _scan_dot(_tri(tp, upper=True), dcum) + carry_ref[...]
        carry_ref[...] = suffix[0:1, :]
        x = fl_ref[...] + bf_ref[...]
        dfl = jnp.where(lane < N_HEADS, suffix / (1.0 + jnp.exp(x)), 0.0)
        dfl_ref[...] = dfl.astype(BF)
        dbf_ref[...] += jnp.sum(dfl, axis=0, keepdims=True)

    rev = lambda w: pl.BlockSpec((tp, w), lambda i: (n - 1 - i, 0))
    return pl.pallas_call(
        body, name="forget_bwd", grid=(n,),
        in_specs=[rev(ATTN_W), rev(ATTN_W), rev(LANES), _vec_spec(LANES)],
        out_specs=[rev(LANES), _vec_spec(LANES)],
        out_shape=[jax.ShapeDtypeStruct((T, LANES), BF), jax.ShapeDtypeStruct((1, LANES), F32)],
        scratch_shapes=[pltpu.VMEM((1, LANES), F32)], compiler_params=_params("arbitrary"),
    )(ext_q, ext_k, fl, b_forget)


def _adamw(w, g, m, v, *, name, tr=256):
    _, rows, cols = w.shape
    tr = tr if rows % tr == 0 else rows

    def body(w_ref, g_ref, m_ref, v_ref, go_ref, d_ref, nm_ref, nv_ref):
        gv = g_ref[...]
        go_ref[...] = gv
        nm = ADAM_B1 * m_ref[...] + (1.0 - ADAM_B1) * gv
        nv = ADAM_B2 * v_ref[...] + (1.0 - ADAM_B2) * (gv * gv)
        m_hat = nm / (1.0 - ADAM_B1 ** ADAM_STEP)
        v_hat = nv / (1.0 - ADAM_B2 ** ADAM_STEP)
        d_ref[...] = -ADAM_LR * (m_hat / (jnp.sqrt(v_hat) + ADAM_EPS) + ADAM_WD * w_ref[...])
        nm_ref[...] = nm
        nv_ref[...] = nv

    spec = pl.BlockSpec((None, tr, cols), lambda i: (0, i, 0))
    return pl.pallas_call(
        body, name=name, grid=(rows // tr,), in_specs=[spec, pl.BlockSpec((tr, cols), lambda i: (i, 0)), spec, spec],
        out_specs=[spec] * 4, out_shape=[jax.ShapeDtypeStruct((1, rows, cols), F32)] * 4,
        compiler_params=_params("parallel"),
    )(w, g, m, v)


HBM = pl.BlockSpec(memory_space=pltpu.HBM)
BF16_ROWS = 16


def _place():
    x, y, c = lax.axis_index("x"), lax.axis_index("y"), lax.axis_index("c")
    others = [(1 - x, y), (x, 1 - y), (1 - x, 1 - y)]
    return x, y, c, others


def _chip(xy):
    return 2 * xy[0] + xy[1]


def _row_halves(c, rows):
    half = rows // 2
    assert half % BF16_ROWS == 0
    return (pl.ds(pl.multiple_of(c * half, BF16_ROWS), half), pl.ds(pl.multiple_of((1 - c) * half, BF16_ROWS), half))


def _remote(src, dst, send_sems, recv_sems, k, to):
    return pltpu.make_async_remote_copy(src_ref=src, dst_ref=dst, send_sem=send_sems.at[k], recv_sem=recv_sems.at[k],
                                        device_id=to, device_id_type=MESH)


def _gathered_shapes(shards):
    return [jax.ShapeDtypeStruct((N_CHIPS,) + s.shape, s.dtype) for s in shards]


def _gather_semaphores(n):
    return [pltpu.SemaphoreType.DMA((6 * n,)), pltpu.SemaphoreType.DMA((6 * n,))]


def _gather_phases(w_refs, g_refs, send_sems, recv_sems):
    n = len(w_refs)
    x, y, c, others = _place()
    sibling, me = (x, y, 1 - c), _chip((x, y))
    halves = [_row_halves(c, w.shape[0]) for w in w_refs]

    def sent(a, j, o):
        mine, _ = halves[a]
        return _remote(w_refs[a].at[mine, :], g_refs[a].at[me, mine, :], send_sems, recv_sems, 6 * a + j, (*o, c))

    def passed(a, j, o):
        landed = g_refs[a].at[_chip(o), halves[a][0], :]
        return _remote(landed, landed, send_sems, recv_sems, 6 * a + 3 + j, sibling)

    def start():
        for a in range(n):
            for j, o in enumerate(others):
                sent(a, j, o).start()

    def forward():
        for j, o in enumerate(others):
            for a in range(n):
                landed = g_refs[a].at[_chip(o), halves[a][0], :]
                _remote(landed, landed, send_sems, recv_sems, 6 * a + j, (*o, c)).wait_recv()
                passed(a, j, o).start()

    def finish():
        for j, o in enumerate(others):
            for a in range(n):
                landed = g_refs[a].at[_chip(o), halves[a][1], :]
                _remote(landed, landed, send_sems, recv_sems, 6 * a + 3 + j, sibling).wait_recv()
        for a in range(n):
            for j, o in enumerate(others):
                sent(a, j, o).wait_send()
                passed(a, j, o).wait_send()

    return start, forward, finish


def _exchange_halves(arrays, *, name):
    n = len(arrays)

    def body(*refs):
        for phase in _exchange_phases(refs[:n], refs[n:2 * n], *refs[2 * n:]):
            phase()

    return pl.pallas_call(
        body, name=name, in_specs=[HBM] * n, out_specs=[HBM] * n, out_shape=_exchanged_shapes(arrays),
        scratch_shapes=_exchange_semaphores(n),
    )(*arrays)


def _exchanged_shapes(arrays):
    return [jax.ShapeDtypeStruct(s.shape[:-2] + (s.shape[-2] // 2, s.shape[-1]), F32) for s in arrays]


def _exchange_semaphores(n):
    return [pltpu.SemaphoreType.DMA((n,)), pltpu.SemaphoreType.DMA((n,))]


def _exchange_phases(g_refs, r_refs, send_sems, recv_sems):
    x, y, c, _ = _place()

    def copy(a):
        _, theirs = _row_halves(c, g_refs[a].shape[-2])
        src = g_refs[a].at[:, theirs, :] if len(g_refs[a].shape) == 3 else g_refs[a].at[theirs, :]
        return _remote(src, r_refs[a], send_sems, recv_sems, a, (x, y, 1 - c))

    def start():
        for a in range(len(g_refs)):
            copy(a).start()

    def finish():
        for a in range(len(g_refs)):
            copy(a).wait()

    return start, finish


def _scatter_to_owners(chip_sums):
    n = len(chip_sums)

    def body(*refs):
        for phase in _scatter_phases(refs[:n], refs[n:2 * n], *refs[2 * n:]):
            phase()

    return pl.pallas_call(
        body, name="scatter_to_owners", in_specs=[HBM] * n, out_specs=[HBM] * n,
        out_shape=_scattered_shapes(chip_sums), scratch_shapes=_scatter_semaphores(n),
    )(*chip_sums)


def _scattered_shapes(chip_sums):
    return [jax.ShapeDtypeStruct(b.shape if b.ndim == 3 else (N_CHIPS,) + b.shape, b.dtype) for b in chip_sums]


def _scatter_semaphores(n):
    return [pltpu.SemaphoreType.DMA((3 * n,)), pltpu.SemaphoreType.DMA((3 * n,))]


def _scatter_phases(b_refs, r_refs, send_sems, recv_sems):
    n = len(b_refs)
    x, y, c, others = _place()
    me = _chip((x, y))

    def sent(a, j, o):
        src = b_refs[a].at[_chip(o)] if len(b_refs[a].shape) == 3 else b_refs[a]
        return _remote(src, r_refs[a].at[me], send_sems, recv_sems, 3 * a + j, (*o, c))

    def start():
        for a in range(n):
            for j, o in enumerate(others):
                sent(a, j, o).start()

    def finish():
        for a in range(n):
            for j, o in enumerate(others):
                landed = r_refs[a].at[_chip(o)]
                _remote(landed, landed, send_sems, recv_sems, 3 * a + j, (*o, c)).wait_recv()
        for a in range(n):
            for j, o in enumerate(others):
                sent(a, j, o).wait_send()

    return start, finish


def _join_halves(totals):
    n = len(totals)

    def body(*refs):
        in_refs, out_refs, (send_sems, recv_sems) = refs[:n], refs[n:2 * n], refs[2 * n:]
        x, y, c, _ = _place()
        copies = []
        for a in range(n):
            mine, _ = _row_halves(c, in_refs[a].shape[0])
            copies.append(_remote(in_refs[a].at[mine, :], out_refs[a].at[mine, :], send_sems, recv_sems, a, (x, y, 1 - c)))
            copies[-1].start()
        for cp in copies:
            cp.wait()

    return pl.pallas_call(
        body, name="join_halves", in_specs=[HBM] * n, out_specs=[HBM] * n,
        out_shape=[jax.ShapeDtypeStruct(t.shape, F32) for t in totals], input_output_aliases={a: a for a in range(n)},
        scratch_shapes=[pltpu.SemaphoreType.DMA((n,)), pltpu.SemaphoreType.DMA((n,))],
    )(*totals)


ADD_ROWS = 128


def _add_sibling(g, r, place, *, name):
    lead, (half, cols) = g.shape[:-2], r.shape[-2:]
    tr = min(ADD_ROWS, half)
    nb = half // tr
    zeros = (0,) * len(lead)

    def body(place_ref, g_ref, r_ref, o_ref, ob_ref):
        s = g_ref[...] + r_ref[...]
        o_ref[...] = s
        ob_ref[...] = s.astype(BF)

    spec = pl.BlockSpec(lead + (tr, cols), lambda i, p: zeros + (i, 0))
    return pl.pallas_call(
        body, name=name,
        grid_spec=pltpu.PrefetchScalarGridSpec(
            num_scalar_prefetch=1, grid=(nb,),
            in_specs=[pl.BlockSpec(lead + (tr, cols), lambda i, p: zeros + (p[1] * nb + i, 0)), spec], out_specs=[spec, spec]),
        out_shape=[jax.ShapeDtypeStruct(r.shape, F32), jax.ShapeDtypeStruct(r.shape, BF)],
        compiler_params=_params("parallel"),
    )(place, g, r)


def _add_chips(own, received, place, *, name, own_slots):
    half, cols = received.shape[-2:]
    tr = min(ADD_ROWS, half)
    nb = half // tr

    def written(k, p):
        return jnp.where(p[0] == k, (k + 1) % N_CHIPS, k)

    def body(place_ref, own_ref, *refs):
        o_ref = refs[N_CHIPS]
        mine = own_ref[0] if own_slots else own_ref[...]
        if own_slots:
            acc = mine
            for k in range(N_CHIPS):
                acc = acc + jnp.where(place_ref[0] == k, 0.0, refs[k][0].astype(F32))
        else:
            terms = [jnp.where(place_ref[0] == k, mine, refs[k][0]) for k in range(N_CHIPS)]
            acc = ((terms[0] + terms[1]) + terms[2]) + terms[3]
        o_ref[...] = acc

    own_spec = (pl.BlockSpec((1, tr, cols), lambda i, p: (p[0], i, 0)) if own_slots
                else pl.BlockSpec((tr, cols), lambda i, p: (i, 0)))
    return pl.pallas_call(
        body, name=name,
        grid_spec=pltpu.PrefetchScalarGridSpec(
            num_scalar_prefetch=1, grid=(nb,),
            in_specs=[own_spec] + [pl.BlockSpec((1, tr, cols), functools.partial(lambda i, p, k: (written(k, p), i, 0), k=k))
                                   for k in range(N_CHIPS)],
            out_specs=pl.BlockSpec((tr, cols), lambda i, p: (p[1] * nb + i, 0))),
        out_shape=jax.ShapeDtypeStruct((2 * half, cols), F32), compiler_params=_params("parallel"),
    )(place, own, *([received] * N_CHIPS))


SHARDED = (("w_in", (D_MODEL, 4616), 1), ("w_branch_sgu", (SGU_W, D_MODEL), 1), ("w_branch_attn", (ATTN_W, D_MODEL), 1),
           ("w_out", (D_MODEL, D_MODEL), 0), ("w_up", (D_MODEL, D_FF), 1), ("w_down", (D_FF, D_MODEL), 0))
SMALL = (("g_mix_pre", (1, D_MODEL)), ("b_forget", (1, N_HEADS)), ("g_sgu", (1, SGU_W)), ("b_sgu", (1, SGU_W)),
         ("w_spatial", (N_GROUPS * CHUNK, CHUNK)), ("b_spatial", (N_GROUPS, CHUNK)), ("g_mix_post", (1, D_MODEL)),
         ("g_ffn_pre", (1, D_MODEL)), ("g_ffn_post", (1, D_MODEL)))
SMALL_ALIGN = 2 * ADD_ROWS


def _shard_shape(shape, axis):
    return tuple(s // N_CHIPS if a == axis else s for a, s in enumerate(shape))


def _slots_to_full(slots, axis):
    return slots.reshape(-1, slots.shape[2]) if axis == 0 else slots.transpose(1, 0, 2).reshape(slots.shape[1], -1)


def _full_to_slots(full, axis):
    if axis == 0:
        return full.reshape(N_CHIPS, -1, full.shape[1])
    return full.reshape(full.shape[0], N_CHIPS, -1).transpose(1, 0, 2)


def _small_rows(shape):
    return -(-(shape[0] * shape[1]) // (8 * LANES)) * 8


def _pack_small(values):
    parts = []
    for name, shape in SMALL:
        flat = values[name].reshape(-1)
        n = _small_rows(shape)
        parts.append(jnp.pad(flat, (0, n * LANES - flat.shape[0])).reshape(n, LANES))
    rows = sum(p.shape[0] for p in parts)
    pad = -(-rows // SMALL_ALIGN) * SMALL_ALIGN - rows
    return jnp.concatenate(parts + [jnp.zeros((pad, LANES), F32)], axis=0)


def _unpack_small(packed):
    out, row = {}, 0
    for name, shape in SMALL:
        n = _small_rows(shape)
        out[name] = packed[row:row + n].reshape(-1)[:shape[0] * shape[1]].reshape(shape)
        row += n
    return out


IN_Z, IN_Q, IN_K, IN_V, IN_F, IN_G, IN_END = 0, 1024, 1536, 2048, 2560, 2568, 4616


LATE_WEIGHTS = ("w_branch_sgu", "w_branch_attn", "w_out", "w_up", "w_down")
EARLY_GRADS = LATE_WEIGHTS


def _assemble(name, shard, gathered, chip):
    axis = {n: a for n, _, a in SHARDED}[name]
    slot = jnp.arange(N_CHIPS)[:, None, None]
    return _slots_to_full(jnp.where(slot == chip, shard[None], gathered), axis)


def _local_step(x, target, shards, small, place):
    b_forget = jnp.pad(small["b_forget"], ((0, 0), (0, LANES - N_HEADS)))
    causal = jnp.tril(jnp.ones((CHUNK, CHUNK), bool))
    ws = jnp.where(causal[None], small["w_spatial"].reshape(N_GROUPS, CHUNK, CHUNK), 0.0).astype(BF)
    ws_t = ws.transpose(0, 2, 1)
    bias_plane = jnp.repeat(small["b_spatial"].T, HEAD_DIM, axis=1)

    xn, (w_in_slots,) = _rms_fwd(x, small["g_mix_pre"], [shards["w_in"]])
    w_in = _assemble("w_in", shards["w_in"], w_in_slots, place[0])
    w_z, w_qkv, w_g = w_in[:, IN_Z:IN_Q], w_in[:, IN_Q:IN_F], w_in[:, IN_G:IN_END]
    w_q, w_k, w_v = w_in[:, IN_Q:IN_K], w_in[:, IN_K:IN_V], w_in[:, IN_V:IN_F]
    w_f = jnp.pad(w_in[:, IN_F:IN_G], ((0, 0), (0, LANES - N_HEADS)))
    z = _matmul([(xn, w_z)], nt=False, out_dtypes=[F32], name="proj_z")
    qkv = _matmul([(xn, w_qkv)], nt=False, out_dtypes=[BF], name="proj_qkv")
    gl = _matmul([(xn, w_g)], nt=False, out_dtypes=[BF], name="proj_gate")
    fl = _matmul([(xn, w_f)], nt=False, out_dtypes=[F32], name="proj_forget")
    ysgu = _sgu_fwd(z, small["g_sgu"], small["b_sgu"], ws, bias_plane)
    qf, kl, vl, tile_stats = _attn_prep(qkv, fl, b_forget)
    first_key_tile, last_query_tile, bounded = _attn_ranges(tile_stats)
    yattn, yattn_f, ql, gathered = _attn_fwd(qf, kl, vl, first_key_tile, bounded, [shards[name] for name in LATE_WEIGHTS])
    w = {name: _assemble(name, shards[name], got, place[0]) for name, got in zip(LATE_WEIGHTS, gathered, strict=True)}
    a, b, merged = _branch_merge(ysgu, yattn, w["w_branch_sgu"], w["w_branch_attn"], gl)
    o = _matmul([(merged, w["w_out"])], nt=False, out_dtypes=[F32], name="proj_out")
    h1, xn2 = _mixer_out_fwd(o, x, small["g_mix_post"], small["g_ffn_pre"])

    def relu2(acc):
        r = jnp.maximum(acc, 0.0)
        return (r * r,)

    hid = _matmul([(xn2, w["w_up"])], nt=False, out_dtypes=[BF], name="ffn_up", epilogue=relu2, tm=FFN_ROWS)
    dn = _matmul([(hid, w["w_down"])], nt=False, out_dtypes=[F32], name="ffn_down")
    sq, dy, ddn, dg_ffn_post = _loss_head(dn, h1, target, small["g_ffn_post"])

    dup = _matmul([(ddn, w["w_down"])], nt=True, out_dtypes=[BF], name="ffn_down_bwd", tm=FFN_ROWS,
                  epilogue=lambda acc, h: (acc * (2.0 * jnp.sqrt(h.astype(F32))),), extras=[hid])
    dw_down = _matmul_tn(hid, ddn, name="dw_down")
    dxn2 = _matmul([(dup, w["w_up"])], nt=True, out_dtypes=[F32], name="ffn_up_bwd")
    dw_up = _matmul_tn(xn2, dup, name="dw_up", slots=True)
    dh1, do, dg_ffn_pre, dg_mix_post = _mixer_out_bwd(h1, dxn2, dy, o, small["g_ffn_pre"], small["g_mix_post"])

    def gate_bwd(dm, a_t, b_t, gla, glb):
        ga, gb = jax.nn.sigmoid(gla.astype(F32)), jax.nn.sigmoid(glb.astype(F32))
        return dm * ga, dm * gb, dm * a_t.astype(F32) * (ga * (1.0 - ga)), dm * b_t.astype(F32) * (gb * (1.0 - gb))

    da, db, dgla, dglb = _matmul([(do, w["w_out"])], nt=True, out_dtypes=[BF] * 4, name="proj_out_bwd",
                                 epilogue=gate_bwd, extras=[a, b, (gl, 0), (gl, D_MODEL)])
    dw_out = _matmul_tn(merged, do, name="dw_out")
    dysgu = _matmul([(da, w["w_branch_sgu"])], nt=True, out_dtypes=[F32], name="branch_sgu_bwd")
    dyattn = _matmul([(db, w["w_branch_attn"])], nt=True, out_dtypes=[F32], name="branch_attn_bwd")
    dw_bs = _matmul_tn(ysgu, da, name="dw_branch_sgu")
    dw_ba = _matmul_tn(yattn, db, name="dw_branch_attn")
    early = {"w_branch_sgu": _full_to_slots(dw_bs, 1), "w_branch_attn": _full_to_slots(dw_ba, 1),
             "w_out": _full_to_slots(dw_out, 0), "w_up": dw_up, "w_down": _full_to_slots(dw_down, 0)}
    (dz, dws, dbs, dg_sgu, db_sgu), early_theirs = _sgu_bwd(
        dysgu, z, small["g_sgu"], small["b_sgu"], ws, ws_t, bias_plane, [early[name] for name in EARLY_GRADS])
    early_sums = {name: _add_sibling(early[name], theirs, place, name="add_sibling_" + name)
                  for name, theirs in zip(EARLY_GRADS, early_theirs, strict=True)}
    dout = _attn_bwd_prep(dyattn, yattn_f)
    (dq, dk, dv, ext_q, ext_k), early_received = _attn_bwd(
        kl, vl, ql, dout, last_query_tile, [early_sums[name][1] for name in EARLY_GRADS])
    dfl, dbf = _forget_bwd(ext_q, ext_k, fl, b_forget)
    dw_in = _full_to_slots(jnp.concatenate(
        [_matmul_tn(xn, dz, name="dw_in_z"), _matmul_tn(xn, dq, name="dw_in_q"), _matmul_tn(xn, dk, name="dw_in_k"),
         _matmul_tn(xn, dv, name="dw_in_v"), _matmul_tn(xn, dfl, name="dw_in_f")[:, :N_HEADS],
         _matmul_tn(xn, dgla, name="dw_in_ga"), _matmul_tn(xn, dglb, name="dw_in_gb")], axis=1), 1)
    (dw_in_theirs,) = _exchange_halves([dw_in], name="exchange_halves_w_in")
    dw_in_sum = _add_sibling(dw_in, dw_in_theirs, place, name="add_sibling_w_in")
    (dxn,), (dw_in_received,) = _matmul(
        [(dz, w_z), (dq, w_q), (dk, w_k), (dv, w_v), (dgla, w_g[:, :D_MODEL]), (dglb, w_g[:, D_MODEL:]), (dfl, w_f)],
        nt=True, out_dtypes=[F32], name="proj_in_bwd", scatter=[dw_in_sum[1]])
    dx, dg_mix_pre = _input_norm_bwd(x, dxn, dh1, small["g_mix_pre"])

    reduced = {name: (early_sums[name][0], got) for name, got in zip(EARLY_GRADS, early_received, strict=True)}
    reduced["w_in"] = (dw_in_sum[0], dw_in_received)
    small_grads = {"g_mix_pre": dg_mix_pre, "b_forget": dbf[:, :N_HEADS], "g_sgu": dg_sgu, "b_sgu": db_sgu,
                   "w_spatial": dws.reshape(N_GROUPS * CHUNK, CHUNK), "b_spatial": dbs[:, :N_GROUPS].T,
                   "g_mix_post": dg_mix_post, "g_ffn_pre": dg_ffn_pre, "g_ffn_post": dg_ffn_post}
    return sq, dx, reduced, small_grads


NAMES = ("g_mix_pre", "w_in", "b_forget", "g_sgu", "b_sgu", "w_spatial", "b_spatial", "w_branch_sgu", "w_branch_attn",
         "w_out", "g_mix_post", "g_ffn_pre", "w_up", "w_down", "g_ffn_post")


def kernel(x, g_mix_pre, w_in, b_forget, g_sgu, b_sgu, w_spatial, b_spatial, w_branch_sgu, w_branch_attn, w_out, g_mix_post, g_ffn_pre, w_up, w_down, g_ffn_post, loss_target, m_g_mix_pre, m_w_in, m_b_forget, m_g_sgu, m_b_sgu, m_w_spatial, m_b_spatial, m_w_branch_sgu, m_w_branch_attn, m_w_out, m_g_mix_post, m_g_ffn_pre, m_w_up, m_w_down, m_g_ffn_post, v_g_mix_pre, v_w_in, v_b_forget, v_g_sgu, v_b_sgu, v_w_spatial, v_b_spatial, v_w_branch_sgu, v_w_branch_attn, v_w_out, v_g_mix_post, v_g_ffn_pre, v_w_up, v_w_down, v_g_ffn_post):
    weights = dict(zip(NAMES, (g_mix_pre, w_in, b_forget, g_sgu, b_sgu, w_spatial, b_spatial, w_branch_sgu, w_branch_attn,
                               w_out, g_mix_post, g_ffn_pre, w_up, w_down, g_ffn_post), strict=True))
    first = dict(zip(NAMES, (m_g_mix_pre, m_w_in, m_b_forget, m_g_sgu, m_b_sgu, m_w_spatial, m_b_spatial, m_w_branch_sgu,
                             m_w_branch_attn, m_w_out, m_g_mix_post, m_g_ffn_pre, m_w_up, m_w_down, m_g_ffn_post), strict=True))
    second = dict(zip(NAMES, (v_g_mix_pre, v_w_in, v_b_forget, v_g_sgu, v_b_sgu, v_w_spatial, v_b_spatial, v_w_branch_sgu,
                              v_w_branch_attn, v_w_out, v_g_mix_post, v_g_ffn_pre, v_w_up, v_w_down, v_g_ffn_post), strict=True))
    shard_shapes = {name: _shard_shape(shape, axis) for name, shape, axis in SHARDED}
    small_shapes = dict(SMALL)
    view = lambda name, a: a.reshape(shard_shapes.get(name) or small_shapes[name])

    place = jnp.stack([2 * lax.axis_index("x") + lax.axis_index("y"), lax.axis_index("c")]).astype(jnp.int32)

    shards = {name: view(name, weights[name]).astype(BF) for name, _, _ in SHARDED}
    small = {name: view(name, weights[name]) for name, _ in SMALL}
    sq, dx, reduced, small_grads = _local_step(x[0], loss_target[0], shards, small, place)
    loss = lax.psum(0.5 * jnp.sum(sq) / D_MODEL, ("x", "y", "c"))

    small_mine = _pack_small(small_grads)
    (small_theirs,) = _exchange_halves([small_mine], name="exchange_halves_small")
    small_sum, _ = _add_sibling(small_mine, small_theirs, place, name="add_sibling_small")
    (small_received,) = _scatter_to_owners([small_sum])
    totals = {name: _add_chips(s, r, place, name="add_chips_" + name, own_slots=True) for name, (s, r) in reduced.items()}
    small_total = _add_chips(small_sum, small_received, place, name="add_chips_small", own_slots=False)
    joined = _join_halves([totals[name] for name, _, _ in SHARDED] + [small_total])
    grad = {**{name: g for (name, _, _), g in zip(SHARDED, joined[:-1], strict=True)}, **_unpack_small(joined[-1])}

    grad_out, delta, new_m, new_v = {}, {}, {}, {}
    for name in NAMES:
        rows, cols = grad[name].shape
        as_given = lambda a: a.reshape(1, rows, cols)
        grad_out[name], delta[name], new_m[name], new_v[name] = _adamw(
            as_given(weights[name]), grad[name], as_given(first[name]), as_given(second[name]), name="adamw_" + name)

    like = lambda d: [d[name].reshape(weights[name].shape) for name in NAMES]
    return (loss, dx[None], *like(grad_out), *like(delta), *like(new_m), *like(new_v))
```

```python
import functools

import jax
import jax.numpy as jnp
from jax import lax
from jax.experimental import pallas as pl
from jax.experimental.pallas import tpu as pltpu

F32 = jnp.float32
BF = jnp.bfloat16
MESH = pl.DeviceIdType.MESH

D_MODEL = 1024
N_HEADS = 8
HEAD_DIM = 64
ATTN_W = N_HEADS * HEAD_DIM
SGU_W = 512
N_GROUPS = 8
CHUNK = 128
D_FF = 4096
EPS = 1e-6
Q_SCALE = HEAD_DIM ** -0.5
N_CHIPS = 4
LANES = 128

ADAM_LR = 0.001
ADAM_B1 = 0.9
ADAM_B2 = 0.999
ADAM_EPS = 1e-08
ADAM_WD = 0.01
ADAM_STEP = 10

VMEM_LIMIT = 48 * 1024 * 1024
BIG_VMEM = 58 * 1024 * 1024
NEG = -1e30

LANE_ROWSUM = HEAD_DIM
LANE_COLSUM = HEAD_DIM + 3


def _params(*sem):
    return pltpu.CompilerParams(dimension_semantics=sem, vmem_limit_bytes=VMEM_LIMIT)


def _dot(a, b):
    return jnp.dot(a, b, preferred_element_type=F32)


def _dot_nt(a, b):
    return lax.dot_general(a, b, (((1,), (1,)), ((), ())), preferred_element_type=F32)


def _dot_tn(a, b):
    return lax.dot_general(a, b, (((0,), (0,)), ((), ())), preferred_element_type=F32)


def _split3(c):
    hi = c.astype(BF).astype(F32)
    r = c - hi
    mid = r.astype(BF).astype(F32)
    lo = (r - mid).astype(BF).astype(F32)
    return hi, mid, lo


def _gelu(x):
    k = 0.7978845608028654
    return 0.5 * x * (1.0 + jnp.tanh(k * (x + 0.044715 * (x * x * x))))


def _gelu_grad(x):
    k = 0.7978845608028654
    x2 = x * x
    t = jnp.tanh(k * (x + 0.044715 * (x2 * x)))
    return 0.5 * (1.0 + t) + 0.5 * x * (1.0 - t * t) * (k * (1.0 + 3.0 * 0.044715 * x2))


def _rms_bwd(a, g, dy):
    r = lax.rsqrt(jnp.mean(a * a, axis=-1, keepdims=True) + EPS)
    n = a * r
    dn = dy * g
    da = r * (dn - n * jnp.mean(dn * n, axis=-1, keepdims=True))
    return da, dy * n


MM_ROWS = 1024
MM_COLS = 512
FFN_ROWS = 2048


def _matmul(pairs, *, nt, out_dtypes, name, tm=MM_ROWS, tn=MM_COLS, epilogue=None, extras=(), scatter=()):
    n_pairs, n_extra, n_out, n_scatter = len(pairs), len(extras), len(out_dtypes), len(scatter)
    M = pairs[0][0].shape[0]
    N = pairs[0][1].shape[0] if nt else pairs[0][1].shape[1]
    tm, tn = min(tm, M), min(tn, N)
    assert M % tm == 0 and N % tn == 0
    grid = (M // tm, N // tn)

    def body(*refs):
        n_in = 2 * n_pairs + n_extra
        if n_scatter:
            step = pl.program_id(0) * grid[1] + pl.program_id(1)
            first = n_in + n_scatter + n_out
            scatter_start, scatter_finish = _scatter_phases(
                refs[n_in:n_in + n_scatter], refs[first:first + n_scatter], *refs[first + n_scatter:])
            pl.when(step == 0)(scatter_start)
        acc = None
        for p in range(n_pairs):
            a_ref, b_ref = refs[2 * p], refs[2 * p + 1]
            d = _dot_nt(a_ref[...], b_ref[...]) if nt else _dot(a_ref[...], b_ref[...])
            acc = d if acc is None else acc + d
        e_refs = refs[2 * n_pairs:n_in]
        o_refs = refs[n_in + n_scatter:n_in + n_scatter + n_out]
        outs = (acc,) if epilogue is None else epilogue(acc, *[e[...] for e in e_refs])
        for o_ref, o in zip(o_refs, outs, strict=True):
            o_ref[...] = o.astype(o_ref.dtype)
        if n_scatter:
            pl.when(step == grid[0] * grid[1] - 1)(scatter_finish)

    in_specs, args = [], []
    for a, b in pairs:
        K = a.shape[1]
        in_specs.append(pl.BlockSpec((tm, K), lambda i, j: (i, 0)))
        in_specs.append(pl.BlockSpec((tn, K), lambda i, j: (j, 0)) if nt else pl.BlockSpec((K, tn), lambda i, j: (0, j)))
        args += [a, b]
    for e in extras:
        e, col = e if isinstance(e, tuple) else (e, 0)
        in_specs.append(pl.BlockSpec((tm, tn), functools.partial(lambda i, j, off: (i, j + off), off=col // tn)))
        args.append(e)
    order = ("arbitrary", "arbitrary") if n_scatter else ("parallel", "parallel")
    outs = pl.pallas_call(
        body, name=name, grid=grid, in_specs=in_specs + [HBM] * n_scatter,
        out_specs=[pl.BlockSpec((tm, tn), lambda i, j: (i, j)) for _ in out_dtypes] + [HBM] * n_scatter,
        out_shape=[jax.ShapeDtypeStruct((M, N), dt) for dt in out_dtypes] + (_scattered_shapes(scatter) if n_scatter else []),
        scratch_shapes=_scatter_semaphores(n_scatter) if n_scatter else [],
        compiler_params=_params(*order),
    )(*args, *scatter)
    if n_scatter:
        return outs[:n_out], outs[n_out:]
    return outs if len(outs) > 1 else outs[0]


def _matmul_tn(a, b, *, name, tm=1024, tn=1024, tk=2048, slots=False):
    T, K1 = a.shape
    N = b.shape[1]
    tm, tn, tk = min(tm, K1), min(tn, N // N_CHIPS if slots else N), min(tk, T)
    assert K1 % tm == 0 and (N // N_CHIPS if slots else N) % tn == 0 and T % tk == 0
    per_slot = N // N_CHIPS // tn

    def body(a_ref, b_ref, o_ref):
        @pl.when(pl.program_id(2) == 0)
        def _():
            o_ref[...] = jnp.zeros_like(o_ref)

        o_ref[...] += _dot_tn(a_ref[...], b_ref[...])

    if slots:
        out_spec = pl.BlockSpec((None, tm, tn), lambda i, j, k: (j // per_slot, i, j % per_slot))
        out_shape = jax.ShapeDtypeStruct((N_CHIPS, K1, N // N_CHIPS), F32)
    else:
        out_spec = pl.BlockSpec((tm, tn), lambda i, j, k: (i, j))
        out_shape = jax.ShapeDtypeStruct((K1, N), F32)
    return pl.pallas_call(
        body, name=name, grid=(K1 // tm, N // tn, T // tk),
        in_specs=[pl.BlockSpec((tk, tm), lambda i, j, k: (k, i)), pl.BlockSpec((tk, tn), lambda i, j, k: (k, j))],
        out_specs=out_spec, out_shape=out_shape,
        compiler_params=_params("parallel", "parallel", "arbitrary"),
    )(a, b)


def _branch_merge(ysgu, yattn, w_bs, w_ba, gl, *, tm=MM_ROWS, tn=MM_COLS):
    T = ysgu.shape[0]
    tm = min(tm, T)
    nj = D_MODEL // tn

    def body(ys_ref, ya_ref, wbs_ref, wba_ref, gla_ref, glb_ref, a_ref, b_ref, m_ref):
        a = _dot(ys_ref[...], wbs_ref[...])
        b = _dot(ya_ref[...], wba_ref[...])
        a_ref[...] = a.astype(BF)
        b_ref[...] = b.astype(BF)
        m_ref[...] = (jax.nn.sigmoid(gla_ref[...].astype(F32)) * a + jax.nn.sigmoid(glb_ref[...].astype(F32)) * b).astype(BF)

    return pl.pallas_call(
        body, name="branch_merge", grid=(T // tm, nj),
        in_specs=[
            pl.BlockSpec((tm, SGU_W), lambda i, j: (i, 0)),
            pl.BlockSpec((tm, ATTN_W), lambda i, j: (i, 0)),
            pl.BlockSpec((SGU_W, tn), lambda i, j: (0, j)),
            pl.BlockSpec((ATTN_W, tn), lambda i, j: (0, j)),
            pl.BlockSpec((tm, tn), lambda i, j: (i, j)),
            pl.BlockSpec((tm, tn), lambda i, j: (i, j + nj)),
        ],
        out_specs=[pl.BlockSpec((tm, tn), lambda i, j: (i, j))] * 3,
        out_shape=[jax.ShapeDtypeStruct((T, D_MODEL), BF)] * 3,
        compiler_params=_params("parallel", "parallel"),
    )(ysgu, yattn, w_bs, w_ba, gl, gl)


def _row_spec(tr, width):
    return pl.BlockSpec((tr, width), lambda i: (i, 0))


def _vec_spec(width):
    return pl.BlockSpec((1, width), lambda i: (0, 0))


def _rms_fwd(x, g, shards, *, tr=256):
    T = x.shape[0]
    tr = min(tr, T)
    n_steps = T // tr
    k = len(shards)

    def body(x_ref, g_ref, *refs):
        step = pl.program_id(0)
        gather_start, gather_forward, gather_finish = _gather_phases(refs[:k], refs[k + 1:2 * k + 1], *refs[2 * k + 1:])
        pl.when(step == 0)(gather_start)
        pl.when(step == (3 * n_steps) // 4)(gather_forward)
        xv = x_ref[...]
        r = lax.rsqrt(jnp.mean(xv * xv, axis=-1, keepdims=True) + EPS)
        refs[k][...] = ((xv * r) * g_ref[...]).astype(BF)
        pl.when(step == n_steps - 1)(gather_finish)

    outs = pl.pallas_call(
        body, name="rms_fwd", grid=(n_steps,),
        in_specs=[_row_spec(tr, D_MODEL), _vec_spec(D_MODEL)] + [HBM] * k, out_specs=[_row_spec(tr, D_MODEL)] + [HBM] * k,
        out_shape=[jax.ShapeDtypeStruct((T, D_MODEL), BF)] + _gathered_shapes(shards),
        scratch_shapes=_gather_semaphores(k), compiler_params=_params("arbitrary"),
    )(x, g, *shards)
    return outs[0], outs[1:]


def _mixer_out_fwd(o, x, g_post, g_pre, *, tr=256):
    T = x.shape[0]
    tr = min(tr, T)

    def body(o_ref, x_ref, gpost_ref, gpre_ref, h1_ref, xn2_ref):
        ov = o_ref[...]
        r = lax.rsqrt(jnp.mean(ov * ov, axis=-1, keepdims=True) + EPS)
        h1 = x_ref[...] + (ov * r) * gpost_ref[...]
        h1_ref[...] = h1
        r2 = lax.rsqrt(jnp.mean(h1 * h1, axis=-1, keepdims=True) + EPS)
        xn2_ref[...] = ((h1 * r2) * gpre_ref[...]).astype(BF)

    return pl.pallas_call(
        body, name="mixer_out_fwd", grid=(T // tr,),
        in_specs=[_row_spec(tr, D_MODEL), _row_spec(tr, D_MODEL), _vec_spec(D_MODEL), _vec_spec(D_MODEL)],
        out_specs=[_row_spec(tr, D_MODEL), _row_spec(tr, D_MODEL)],
        out_shape=[jax.ShapeDtypeStruct((T, D_MODEL), F32), jax.ShapeDtypeStruct((T, D_MODEL), BF)],
        compiler_params=_params("parallel"),
    )(o, x, g_post, g_pre)


def _matmul_rows(a, b, *, nt, rows, vecs, row_outs, n_sums, epilogue, name, tm=512):
    M, K = a.shape
    N = b.shape[0] if nt else b.shape[1]
    tm = min(tm, M)
    n_rows, n_vecs, n_out = len(rows), len(vecs), len(row_outs)

    def body(a_ref, b_ref, *refs):
        r_refs, v_refs = refs[:n_rows], refs[n_rows:n_rows + n_vecs]
        o_refs, s_refs = refs[n_rows + n_vecs:n_rows + n_vecs + n_out], refs[n_rows + n_vecs + n_out:]

        @pl.when(pl.program_id(0) == 0)
        def _():
            for s_ref in s_refs:
                s_ref[...] = jnp.zeros_like(s_ref)

        acc = _dot_nt(a_ref[...], b_ref[...]) if nt else _dot(a_ref[...], b_ref[...])
        outs = epilogue(acc, *[r[...] for r in r_refs], *[v[...] for v in v_refs])
        for o_ref, o in zip(o_refs, outs[:n_out], strict=True):
            o_ref[...] = o.astype(o_ref.dtype)
        for s_ref, term in zip(s_refs, outs[n_out:], strict=True):
            s_ref[...] += jnp.sum(term, axis=0, keepdims=True)

    return pl.pallas_call(
        body, name=name, grid=(M // tm,),
        in_specs=[_row_spec(tm, K), pl.BlockSpec(b.shape, lambda i: (0, 0))] + [_row_spec(tm, N)] * n_rows + [_vec_spec(N)] * n_vecs,
        out_specs=[_row_spec(tm, N)] * n_out + [_vec_spec(N)] * n_sums,
        out_shape=[jax.ShapeDtypeStruct((M, N), dt) for dt in row_outs] + [jax.ShapeDtypeStruct((1, N), F32)] * n_sums,
        compiler_params=pltpu.CompilerParams(dimension_semantics=("arbitrary",), vmem_limit_bytes=BIG_VMEM),
    )(a, b, *rows, *vecs)


def _loss_head(dn, h1, target, g):
    r = lax.rsqrt(jnp.mean(dn * dn, axis=-1, keepdims=True) + EPS)
    err = h1 + (dn * r) * g - target
    dy = err * (1.0 / D_MODEL)
    ddn, dg_terms = _rms_bwd(dn, g, dy)
    return dy, ddn, err * err, dg_terms


def _mixer_out_bwd(dxn2, h1, dy, o, g_pre, g_post):
    da, dg_pre_terms = _rms_bwd(h1, g_pre, dxn2)
    dh1 = dy + da
    do, dg_post_terms = _rms_bwd(o, g_post, dh1)
    return dh1, do, dg_pre_terms, dg_post_terms


def _input_norm_bwd(x, dxn, dh1, g, *, tr=256):
    T = x.shape[0]
    tr = min(tr, T)

    def body(x_ref, dxn_ref, dh1_ref, g_ref, dx_ref, dg_ref):
        @pl.when(pl.program_id(0) == 0)
        def _():
            dg_ref[...] = jnp.zeros_like(dg_ref)

        da, dgp = _rms_bwd(x_ref[...], g_ref[...], dxn_ref[...])
        dx_ref[...] = dh1_ref[...] + da
        dg_ref[...] += jnp.sum(dgp, axis=0, keepdims=True)

    return pl.pallas_call(
        body, name="input_norm_bwd", grid=(T // tr,),
        in_specs=[_row_spec(tr, D_MODEL)] * 3 + [_vec_spec(D_MODEL)],
        out_specs=[_row_spec(tr, D_MODEL), _vec_spec(D_MODEL)],
        out_shape=[jax.ShapeDtypeStruct((T, D_MODEL), F32), jax.ShapeDtypeStruct((1, D_MODEL), F32)],
        compiler_params=_params("arbitrary"),
    )(x, dxn, dh1, g)


def _sgu_norm(z_tile, g, b):
    gz = _gelu(z_tile)
    u, vv = gz[:, :SGU_W], gz[:, SGU_W:]
    xc = vv - jnp.mean(vv, axis=-1, keepdims=True)
    rstd = lax.rsqrt(jnp.mean(xc * xc, axis=-1, keepdims=True) + EPS)
    xhat = xc * rstd
    return u, xhat, rstd, xhat * g + b


def _sgu_mix(w_ref, v_bf, first_half):
    parts = []
    for p in range(N_GROUPS // 2):
        vp = v_bf[:, p * LANES:(p + 1) * LANES]
        parts.append(jnp.where(first_half, _dot(w_ref[2 * p], vp), _dot(w_ref[2 * p + 1], vp)))
    return jnp.concatenate(parts, axis=1)


def _sgu_fwd(z, g_sgu, b_sgu, ws, bias_plane, *, tm=512):
    T = z.shape[0]
    tm = min(tm, T)

    def body(z_ref, g_ref, b_ref, ws_ref, bp_ref, y_ref):
        u, _, _, vn = _sgu_norm(z_ref[...], g_ref[...], b_ref[...])
        vn_bf = vn.astype(BF)
        first_half = lax.broadcasted_iota(jnp.int32, (CHUNK, LANES), 1) < HEAD_DIM
        for c in range(tm // CHUNK):
            rows = slice(c * CHUNK, (c + 1) * CHUNK)
            s = _sgu_mix(ws_ref, vn_bf[rows, :], first_half) + bp_ref[...]
            y_ref[rows, :] = (u[rows, :] * s).astype(BF)

    return pl.pallas_call(
        body, name="sgu_fwd", grid=(T // tm,),
        in_specs=[_row_spec(tm, 2 * SGU_W), _vec_spec(SGU_W), _vec_spec(SGU_W),
                  pl.BlockSpec((N_GROUPS, CHUNK, CHUNK), lambda i: (0, 0, 0)),
                  pl.BlockSpec((CHUNK, SGU_W), lambda i: (0, 0))],
        out_specs=_row_spec(tm, SGU_W), out_shape=jax.ShapeDtypeStruct((T, SGU_W), BF),
        compiler_params=_params("parallel"),
    )(z, g_sgu, b_sgu, ws, bias_plane)


def _sgu_bwd(dy, z, g_sgu, b_sgu, ws, ws_t, bias_plane, exchange, *, tm=512):
    T = z.shape[0]
    tm = min(tm, T)
    n_steps = T // tm
    k = len(exchange)

    def body(dy_ref, z_ref, g_ref, b_ref, ws_ref, wst_ref, bp_ref, *refs):
        x_refs, (dz_ref, dws_ref, dbs_ref, dg_ref, db_ref), r_refs = refs[:k], refs[k:k + 5], refs[k + 5:2 * k + 5]
        dbp_ref, send_sems, recv_sems = refs[2 * k + 5:]
        step = pl.program_id(0)
        exchange_start, exchange_finish = _exchange_phases(x_refs, r_refs, send_sems, recv_sems)
        pl.when(step == 0)(exchange_start)

        @pl.when(step == 0)
        def _():
            dws_ref[...] = jnp.zeros_like(dws_ref)
            dg_ref[...] = jnp.zeros_like(dg_ref)
            db_ref[...] = jnp.zeros_like(db_ref)
            dbp_ref[...] = jnp.zeros_like(dbp_ref)

        g = g_ref[...]
        zt = z_ref[...]
        u, xhat, rstd, vn = _sgu_norm(zt, g, b_ref[...])
        vn_bf = vn.astype(BF)
        first_half = lax.broadcasted_iota(jnp.int32, (CHUNK, LANES), 1) < HEAD_DIM
        dyv = dy_ref[...]
        dg_acc = jnp.zeros((1, SGU_W), F32)
        db_acc = jnp.zeros((1, SGU_W), F32)
        for c in range(tm // CHUNK):
            rows = slice(c * CHUNK, (c + 1) * CHUNK)
            v_c = vn_bf[rows, :]
            s = _sgu_mix(ws_ref, v_c, first_half) + bp_ref[...]
            dy_c = dyv[rows, :]
            du = dy_c * s
            dsv = dy_c * u[rows, :]
            dbp_ref[...] += dsv
            ds_bf = dsv.astype(BF)
            zero = jnp.zeros((CHUNK, LANES), BF)
            for p in range(N_GROUPS // 2):
                dsp = ds_bf[:, p * LANES:(p + 1) * LANES]
                vp = v_c[:, p * LANES:(p + 1) * LANES]
                dws_ref[2 * p] += _dot_nt(jnp.where(first_half, dsp, zero), vp)
                dws_ref[2 * p + 1] += _dot_nt(jnp.where(first_half, zero, dsp), vp)
            dvn = _sgu_mix(wst_ref, ds_bf, first_half)
            xh = xhat[rows, :]
            dxh = dvn * g
            dvv = rstd[rows, :] * (dxh - jnp.mean(dxh, axis=-1, keepdims=True)
                                   - xh * jnp.mean(dxh * xh, axis=-1, keepdims=True))
            dg_acc += jnp.sum(dvn * xh, axis=0, keepdims=True)
            db_acc += jnp.sum(dvn, axis=0, keepdims=True)
            dgz = jnp.concatenate([du, dvv], axis=1)
            dz_ref[rows, :] = (dgz * _gelu_grad(zt[rows, :])).astype(BF)
        dg_ref[...] += dg_acc
        db_ref[...] += db_acc

        @pl.when(step == n_steps - 1)
        def _():
            r = lax.broadcasted_iota(jnp.int32, (CHUNK, CHUNK), 0)
            cidx = lax.broadcasted_iota(jnp.int32, (CHUNK, CHUNK), 1)
            causal = (cidx <= r).astype(F32)
            for gi in range(N_GROUPS):
                dws_ref[gi] = dws_ref[gi] * causal
            lane = lax.broadcasted_iota(jnp.int32, (CHUNK, LANES), 1)
            out = jnp.zeros((CHUNK, LANES), F32)
            dbp = dbp_ref[...]
            for gi in range(N_GROUPS):
                col = jnp.sum(dbp[:, gi * HEAD_DIM:(gi + 1) * HEAD_DIM], axis=1, keepdims=True)
                out = jnp.where(lane == gi, col, out)
            dbs_ref[...] = out
            exchange_finish()

    w_spec = pl.BlockSpec((N_GROUPS, CHUNK, CHUNK), lambda i: (0, 0, 0))
    plane = pl.BlockSpec((CHUNK, SGU_W), lambda i: (0, 0))
    outs = pl.pallas_call(
        body, name="sgu_bwd", grid=(n_steps,),
        in_specs=[_row_spec(tm, SGU_W), _row_spec(tm, 2 * SGU_W), _vec_spec(SGU_W), _vec_spec(SGU_W), w_spec, w_spec, plane]
        + [HBM] * k,
        out_specs=[_row_spec(tm, 2 * SGU_W), w_spec, pl.BlockSpec((CHUNK, LANES), lambda i: (0, 0)),
                   _vec_spec(SGU_W), _vec_spec(SGU_W)] + [HBM] * k,
        out_shape=[jax.ShapeDtypeStruct((T, 2 * SGU_W), BF), jax.ShapeDtypeStruct((N_GROUPS, CHUNK, CHUNK), F32),
                   jax.ShapeDtypeStruct((CHUNK, LANES), F32), jax.ShapeDtypeStruct((1, SGU_W), F32),
                   jax.ShapeDtypeStruct((1, SGU_W), F32)] + _exchanged_shapes(exchange),
        scratch_shapes=[pltpu.VMEM((CHUNK, SGU_W), F32)] + _exchange_semaphores(k),
        compiler_params=_params("arbitrary"),
    )(dy, z, g_sgu, b_sgu, ws, ws_t, bias_plane, *exchange)
    return outs[:5], outs[5:]


def _tri(n, upper):
    r = lax.broadcasted_iota(jnp.int32, (n, n), 0)
    c = lax.broadcasted_iota(jnp.int32, (n, n), 1)
    return ((c >= r) if upper else (c <= r)).astype(BF)


def _scan_dot(tri, x):
    hi, mid, lo = _split3(x)
    return (_dot(tri, hi.astype(BF)) + _dot(tri, mid.astype(BF))) + _dot(tri, lo.astype(BF))


def _with_lanes(base, lane, start, cols):
    out = base
    for k, col in enumerate(cols):
        if col is not None:
            out = jnp.where(lane == start + k, col, out)
    return out


def _logit_bound(q_norm, k_norm):
    return NORM_SLACK * q_norm * k_norm + 1.0


ATTN_TILE = 512
SKIP_BELOW = -110.0
NORM_SLACK = 1.001
BOUNDED_GAP = 60.0


def _attn_prep(qkv, fl, b_forget, *, tp=ATTN_TILE):
    T = qkv.shape[0]
    tp = min(tp, T)

    def body(qkv_ref, fl_ref, bf_ref, qf_ref, kl_ref, vl_ref, st_ref, carry_ref, kmax_ref):
        @pl.when(pl.program_id(0) == 0)
        def _():
            carry_ref[...] = jnp.zeros_like(carry_ref)
            kmax_ref[...] = jnp.zeros_like(kmax_ref)

        x = fl_ref[...] + bf_ref[...]
        logf = jnp.minimum(x, 0.0) - jnp.log(1.0 + jnp.exp(-jnp.abs(x)))
        cum = _scan_dot(_tri(tp, upper=False), logf) + carry_ref[...]
        carry_ref[...] = cum[tp - 1:tp, :]
        lane = lax.broadcasted_iota(jnp.int32, (tp, HEAD_DIM), 1)
        ones3 = jnp.where(lane < 3, 1.0, 0.0)
        qkvv = qkv_ref[...]
        st_row = lax.broadcasted_iota(jnp.int32, (N_HEADS, LANES), 0)
        st_lane = lax.broadcasted_iota(jnp.int32, (N_HEADS, LANES), 1)
        stats = jnp.zeros((N_HEADS, LANES), F32)
        kmax_lane = lax.broadcasted_iota(jnp.int32, (1, LANES), 1)
        for h in range(N_HEADS):
            ch = cum[:, h:h + 1]
            c3 = _split3(ch)
            qh = qkvv[:, h * HEAD_DIM:(h + 1) * HEAD_DIM].astype(F32) * Q_SCALE
            kh = qkvv[:, ATTN_W + h * HEAD_DIM:ATTN_W + (h + 1) * HEAD_DIM].astype(F32)
            vh = qkvv[:, 2 * ATTN_W + h * HEAD_DIM:2 * ATTN_W + (h + 1) * HEAD_DIM].astype(F32)
            q_norm = jnp.sqrt(jnp.sum(qh * qh, axis=1, keepdims=True))
            qn = jnp.max(q_norm, axis=0, keepdims=True)
            kn = jnp.sqrt(jnp.max(jnp.sum(kh * kh, axis=1, keepdims=True), axis=0, keepdims=True))
            k_seen = jnp.maximum(kmax_ref[:, h:h + 1], kn)
            kmax_ref[...] = jnp.where(kmax_lane == h, k_seen, kmax_ref[...])
            bound3 = _split3(-_logit_bound(q_norm, k_seen))
            ext_q = _with_lanes(jnp.where((lane >= 3) & (lane < 6), 1.0, 0.0), lane, 0, list(c3) + [None] * 3 + list(bound3))
            ext_k = _with_lanes(jnp.where((lane < 3) | ((lane >= 6) & (lane < 9)), 1.0, 0.0), lane, 3, [-c for c in c3])
            qf_ref[h] = jnp.concatenate([qh, ext_q], axis=1).astype(BF)
            kl_ref[h] = jnp.concatenate([kh, ext_k], axis=1).astype(BF)
            vl_ref[h] = jnp.concatenate([vh, ones3], axis=1).astype(BF)
            tile_stats = (qn, kn, jnp.max(ch, axis=0, keepdims=True), jnp.min(ch, axis=0, keepdims=True), k_seen)
            for k, val in enumerate(tile_stats):
                stats = jnp.where((st_row == h) & (st_lane == k), val, stats)
        st_ref[0] = stats

    head_spec = pl.BlockSpec((N_HEADS, tp, LANES), lambda i: (0, i, 0))
    return pl.pallas_call(
        body, name="attn_prep", grid=(T // tp,),
        in_specs=[_row_spec(tp, 3 * ATTN_W), _row_spec(tp, LANES), _vec_spec(LANES)],
        out_specs=[head_spec] * 3 + [pl.BlockSpec((1, N_HEADS, LANES), lambda i: (i, 0, 0))],
        out_shape=[jax.ShapeDtypeStruct((N_HEADS, T, LANES), BF)] * 3 + [jax.ShapeDtypeStruct((T // tp, N_HEADS, LANES), F32)],
        scratch_shapes=[pltpu.VMEM((1, LANES), F32), pltpu.VMEM((1, LANES), F32)], compiler_params=_params("arbitrary"),
    )(qkv, fl, b_forget)


def _attn_ranges(stats):
    qn, kn, cmax, cmin, k_seen = (stats[:, :, k].T for k in range(5))
    n = qn.shape[1]
    bounded = (2.0 * _logit_bound(qn, k_seen) <= BOUNDED_GAP).reshape(N_HEADS // 2, 2, n).all(axis=1)
    reach = NORM_SLACK * qn * (jnp.max(kn, axis=1, keepdims=True) + kn) + cmax
    i = jnp.arange(n)[None, :, None]
    j = jnp.arange(n)[None, None, :]
    need = ((reach[:, :, None] - cmin[:, None, :] >= SKIP_BELOW) | (i == j)) & (j <= i)
    first = jnp.min(jnp.where(need, j, n), axis=2).reshape(N_HEADS // 2, 2, n).min(axis=1)
    last = jnp.max(jnp.where(need, i, -1), axis=1).reshape(N_HEADS // 2, 2, n).max(axis=1)
    return first.reshape(-1).astype(F32), last.reshape(-1).astype(F32), bounded.reshape(-1).astype(F32)


def _pair_block(t):
    return pl.BlockSpec((2, t, LANES), lambda p, i, *_: (p, i, 0))


def _pair_full(T):
    return pl.BlockSpec((2, T, LANES), lambda p, i, *_: (p, 0, 0))


def _packed_block(t):
    return pl.BlockSpec((t, LANES), lambda p, i, *_: (i, p))


def _causal(t, keys_in_rows=False):
    r = lax.broadcasted_iota(jnp.int32, (t, t), 0)
    c = lax.broadcasted_iota(jnp.int32, (t, t), 1)
    return (r <= c) if keys_in_rows else (c <= r)


def _tile_rows(j, t):
    return pl.ds(pl.multiple_of(j * t, t), t)


def _attn_call(body, name, tile_scalars, operands, in_specs, out_specs, out_shape, scratch_shapes, n_tiles):
    return pl.pallas_call(
        body, name=name,
        grid_spec=pltpu.PrefetchScalarGridSpec(
            num_scalar_prefetch=len(tile_scalars), grid=(N_HEADS // 2, n_tiles), in_specs=in_specs, out_specs=out_specs,
            scratch_shapes=scratch_shapes),
        out_shape=out_shape, compiler_params=_params("arbitrary", "arbitrary"),
    )(*tile_scalars, *operands)


def _attn_fwd(qf, kl, vl, first, bounded, shards, *, tq=ATTN_TILE):
    T = qf.shape[1]
    tq = min(tq, T)
    n = T // tq
    n_steps = (N_HEADS // 2) * n
    k = len(shards)

    def body(first_ref, bounded_ref, qf_ref, kl_ref, vl_ref, *refs):
        w_refs, (o_ref, of_ref, ql_ref), g_refs = refs[:k], refs[k:k + 3], refs[k + 3:2 * k + 3]
        m_ref, acc_ref, send_sems, recv_sems = refs[2 * k + 3:]
        i = pl.program_id(1)
        tile = pl.program_id(0) * n + i
        gather_start, gather_forward, gather_finish = _gather_phases(w_refs, g_refs, send_sems, recv_sems)
        pl.when(tile == 0)(gather_start)
        pl.when(tile == (3 * n_steps) // 4)(gather_forward)
        start = first_ref[tile].astype(jnp.int32)
        is_bounded = bounded_ref[tile] > 0.5
        acc_ref[...] = jnp.zeros_like(acc_ref)
        diagonal = _tile_rows(i, tq)
        causal = _causal(tq)

        def logits(hh, rows):
            return _dot_nt(qf_ref[hh], kl_ref[hh, rows, :])

        @pl.when(is_bounded)
        def _():
            m_ref[...] = jnp.zeros_like(m_ref)

            def update(hh, s, rows):
                acc_ref[hh] += _dot(jnp.exp(s).astype(BF), vl_ref[hh, rows, :])

            def step(j, carry):
                for hh in range(2):
                    update(hh, logits(hh, _tile_rows(j, tq)), _tile_rows(j, tq))
                return carry

            lax.fori_loop(start, i, step, 0)
            for hh in range(2):
                update(hh, jnp.where(causal, logits(hh, diagonal), NEG), diagonal)

        @pl.when(jnp.logical_not(is_bounded))
        def _():
            m_ref[...] = jnp.full_like(m_ref, NEG)

            def update(hh, s, rows):
                m_old = m_ref[hh]
                m_new = jnp.maximum(m_old, jnp.max(s, axis=1, keepdims=True))
                p = jnp.exp(s - m_new)
                acc_ref[hh] = jnp.exp(m_old - m_new) * acc_ref[hh] + _dot(p.astype(BF), vl_ref[hh, rows, :])
                m_ref[hh] = m_new

            def step(j, carry):
                for hh in range(2):
                    update(hh, logits(hh, _tile_rows(j, tq)), _tile_rows(j, tq))
                return carry

            lax.fori_loop(start, i, step, 0)
            for hh in range(2):
                update(hh, jnp.where(causal, logits(hh, diagonal), NEG), diagonal)

        lane = lax.broadcasted_iota(jnp.int32, (tq, LANES), 1)
        outs = []
        for hh in range(2):
            q = qf_ref[hh].astype(F32)
            acc = acc_ref[hh]
            l = acc[:, HEAD_DIM:HEAD_DIM + 1]
            outs.append(acc[:, :HEAD_DIM] / l)
            at = HEAD_DIM + 6
            neg_bound = (q[:, at:at + 1] + q[:, at + 1:at + 2]) + q[:, at + 2:at + 3]
            ql_ref[hh] = _with_lanes(q, lane, at, _split3(neg_bound - (m_ref[hh] + jnp.log(l)))).astype(BF)
        o = jnp.concatenate(outs, axis=1)
        o_ref[...] = o.astype(BF)
        of_ref[...] = o
        pl.when(tile == n_steps - 1)(gather_finish)

    outs = _attn_call(
        body, "attn_fwd", (first, bounded), (qf, kl, vl, *shards),
        [_pair_block(tq), _pair_full(T), _pair_full(T)] + [HBM] * k,
        [_packed_block(tq), _packed_block(tq), _pair_block(tq)] + [HBM] * k,
        [jax.ShapeDtypeStruct((T, ATTN_W), BF), jax.ShapeDtypeStruct((T, ATTN_W), F32),
         jax.ShapeDtypeStruct((N_HEADS, T, LANES), BF)] + _gathered_shapes(shards),
        [pltpu.VMEM((2, tq, 1), F32), pltpu.VMEM((2, tq, LANES), F32)] + _gather_semaphores(k), n)
    return outs[0], outs[1], outs[2], outs[3:]


def _attn_bwd_prep(dya, of, *, tr=256):
    T = dya.shape[0]
    tr = min(tr, T)

    def body(d_ref, o_ref, do_ref):
        lane = lax.broadcasted_iota(jnp.int32, (tr, HEAD_DIM), 1)
        dv, ov = d_ref[...], o_ref[...]
        for h in range(N_HEADS):
            d = dv[:, h * HEAD_DIM:(h + 1) * HEAD_DIM]
            delta = jnp.sum(d * ov[:, h * HEAD_DIM:(h + 1) * HEAD_DIM], axis=1, keepdims=True)
            ext = _with_lanes(jnp.zeros((tr, HEAD_DIM), F32), lane, 0, _split3(-delta))
            do_ref[h] = jnp.concatenate([d, ext], axis=1).astype(BF)

    return pl.pallas_call(
        body, name="attn_bwd_prep", grid=(T // tr,),
        in_specs=[_row_spec(tr, ATTN_W), _row_spec(tr, ATTN_W)],
        out_specs=pl.BlockSpec((N_HEADS, tr, LANES), lambda i: (0, i, 0)),
        out_shape=jax.ShapeDtypeStruct((N_HEADS, T, LANES), BF), compiler_params=_params("parallel"),
    )(dya, of)


def _attn_bwd(kl, vl, ql, do, last, chip_sums, *, tk=ATTN_TILE):
    T = ql.shape[1]
    tk = min(tk, T)
    n = T // tk
    n_steps = (N_HEADS // 2) * n
    m = len(chip_sums)

    def body(last_ref, kl_ref, vl_ref, ql_ref, do_ref, *refs):
        b_refs, (dq_ref, dk_ref, dv_ref, extq_ref, extk_ref), r_refs = refs[:m], refs[m:m + 5], refs[m + 5:2 * m + 5]
        dq_acc, dk_acc, dv_acc, send_sems, recv_sems = refs[2 * m + 5:]
        j = pl.program_id(1)
        tile = pl.program_id(0) * n + j
        scatter_start, scatter_finish = _scatter_phases(b_refs, r_refs, send_sems, recv_sems)
        pl.when(tile == 0)(scatter_start)

        @pl.when(j == 0)
        def _():
            dq_acc[...] = jnp.zeros_like(dq_acc)

        dk_acc[...] = jnp.zeros_like(dk_acc)
        dv_acc[...] = jnp.zeros_like(dv_acc)

        def block(hh, rows, mask):
            qi, di, k = ql_ref[hh, rows, :], do_ref[hh, rows, :], kl_ref[hh]
            p_t = jnp.exp(_dot_nt(k, qi))
            if mask is not None:
                p_t = jnp.where(mask, p_t, 0.0)
            ds_t = (p_t * _dot_nt(vl_ref[hh], di)).astype(BF)
            dk_acc[hh] += _dot(ds_t, qi)
            dv_acc[hh] += _dot(p_t.astype(BF), di)
            dq_acc[hh, rows, :] += _dot_tn(ds_t, k)

        causal_t = _causal(tk, keys_in_rows=True)
        for hh in range(2):
            block(hh, _tile_rows(j, tk), causal_t)

        def step(i, carry):
            for hh in range(2):
                block(hh, _tile_rows(i, tk), None)
            return carry

        lax.fori_loop(j + 1, last_ref[pl.program_id(0) * n + j].astype(jnp.int32) + 1, step, 0)
        dk_ref[...] = jnp.concatenate([dk_acc[hh][:, :HEAD_DIM] for hh in range(2)], axis=1).astype(BF)
        dv_ref[...] = jnp.concatenate([dv_acc[hh][:, :HEAD_DIM] for hh in range(2)], axis=1).astype(BF)
        extk_ref[...] = jnp.concatenate([dk_acc[hh][:, HEAD_DIM:] for hh in range(2)], axis=1)

        @pl.when(j == n - 1)
        def _():
            dq_ref[...] = jnp.concatenate([dq_acc[hh][:, :HEAD_DIM] * Q_SCALE for hh in range(2)], axis=1).astype(BF)
            extq_ref[...] = jnp.concatenate([dq_acc[hh][:, HEAD_DIM:] for hh in range(2)], axis=1)

        pl.when(tile == n_steps - 1)(scatter_finish)

    whole = pl.BlockSpec((T, LANES), lambda p, j, *_: (0, p))
    outs = pl.pallas_call(
        body, name="attn_bwd",
        grid_spec=pltpu.PrefetchScalarGridSpec(
            num_scalar_prefetch=1, grid=(N_HEADS // 2, n),
            in_specs=[_pair_block(tk), _pair_block(tk), _pair_full(T), _pair_full(T)] + [HBM] * m,
            out_specs=[whole, _packed_block(tk), _packed_block(tk), whole, _packed_block(tk)] + [HBM] * m,
            scratch_shapes=[pltpu.VMEM((2, T, LANES), F32), pltpu.VMEM((2, tk, LANES), F32), pltpu.VMEM((2, tk, LANES), F32)]
            + _scatter_semaphores(m)),
        out_shape=[jax.ShapeDtypeStruct((T, ATTN_W), BF)] * 3 + [jax.ShapeDtypeStruct((T, ATTN_W), F32)] * 2
        + _scattered_shapes(chip_sums),
        compiler_params=pltpu.CompilerParams(dimension_semantics=("arbitrary", "arbitrary"), vmem_limit_bytes=BIG_VMEM),
    )(last, kl, vl, ql, do, *chip_sums)
    return outs[:5], outs[5:]


def _forget_bwd(ext_q, ext_k, fl, b_forget, *, tp=256):
    T = fl.shape[0]
    tp = min(tp, T)
    n = T // tp

    def body(eq_ref, ek_ref, fl_ref, bf_ref, dfl_ref, dbf_ref, carry_ref):
        @pl.when(pl.program_id(0) == 0)
        def _():
            carry_ref[...] = jnp.zeros_like(carry_ref)
            dbf_ref[...] = jnp.zeros_like(dbf_ref)

        lane = lax.broadcasted_iota(jnp.int32, (tp, LANES), 1)
        eq, ek = eq_ref[...], ek_ref[...]
        cols = [eq[:, h * HEAD_DIM:h * HEAD_DIM + 1] - ek[:, h * HEAD_DIM + 3:h * HEAD_DIM + 4] for h in range(N_HEADS)]
        dcum = _with_lanes(jnp.zeros((tp, LANES), F32), lane, 0, cols)
        suffix = _scan_dot(_tri(tp, upper=True), dcum) + carry_ref[...]
        carry_ref[...] = suffix[0:1, :]
        x = fl_ref[...] + bf_ref[...]
        dfl = jnp.where(lane < N_HEADS, suffix / (1.0 + jnp.exp(x)), 0.0)
        dfl_ref[...] = dfl.astype(BF)
        dbf_ref[...] += jnp.sum(dfl, axis=0, keepdims=True)

    rev = lambda w: pl.BlockSpec((tp, w), lambda i: (n - 1 - i, 0))
    return pl.pallas_call(
        body, name="forget_bwd", grid=(n,),
        in_specs=[rev(ATTN_W), rev(ATTN_W), rev(LANES), _vec_spec(LANES)],
        out_specs=[rev(LANES), _vec_spec(LANES)],
        out_shape=[jax.ShapeDtypeStruct((T, LANES), BF), jax.ShapeDtypeStruct((1, LANES), F32)],
        scratch_shapes=[pltpu.VMEM((1, LANES), F32)], compiler_params=_params("arbitrary"),
    )(ext_q, ext_k, fl, b_forget)


def _adamw(w, g, m, v, *, name, tr=256):
    _, rows, cols = w.shape
    tr = tr if rows % tr == 0 else rows

    def body(w_ref, g_ref, m_ref, v_ref, go_ref, d_ref, nm_ref, nv_ref):
        gv = g_ref[...]
        go_ref[...] = gv
        nm = ADAM_B1 * m_ref[...] + (1.0 - ADAM_B1) * gv
        nv = ADAM_B2 * v_ref[...] + (1.0 - ADAM_B2) * (gv * gv)
        m_hat = nm / (1.0 - ADAM_B1 ** ADAM_STEP)
        v_hat = nv / (1.0 - ADAM_B2 ** ADAM_STEP)
        d_ref[...] = -ADAM_LR * (m_hat / (jnp.sqrt(v_hat) + ADAM_EPS) + ADAM_WD * w_ref[...])
        nm_ref[...] = nm
        nv_ref[...] = nv

    spec = pl.BlockSpec((None, tr, cols), lambda i: (0, i, 0))
    return pl.pallas_call(
        body, name=name, grid=(rows // tr,), in_specs=[spec, pl.BlockSpec((tr, cols), lambda i: (i, 0)), spec, spec],
        out_specs=[spec] * 4, out_shape=[jax.ShapeDtypeStruct((1, rows, cols), F32)] * 4,
        compiler_params=_params("parallel"),
    )(w, g, m, v)


HBM = pl.BlockSpec(memory_space=pltpu.HBM)
BF16_ROWS = 16


def _place():
    x, y, c = lax.axis_index("x"), lax.axis_index("y"), lax.axis_index("c")
    others = [(1 - x, y), (x, 1 - y), (1 - x, 1 - y)]
    return x, y, c, others


def _chip(xy):
    return 2 * xy[0] + xy[1]


def _row_halves(c, rows):
    half = rows // 2
    assert half % BF16_ROWS == 0
    return (pl.ds(pl.multiple_of(c * half, BF16_ROWS), half), pl.ds(pl.multiple_of((1 - c) * half, BF16_ROWS), half))


def _remote(src, dst, send_sems, recv_sems, k, to):
    return pltpu.make_async_remote_copy(src_ref=src, dst_ref=dst, send_sem=send_sems.at[k], recv_sem=recv_sems.at[k],
                                        device_id=to, device_id_type=MESH)


def _gathered_shapes(shards):
    return [jax.ShapeDtypeStruct((N_CHIPS,) + s.shape, s.dtype) for s in shards]


def _gather_semaphores(n):
    return [pltpu.SemaphoreType.DMA((6 * n,)), pltpu.SemaphoreType.DMA((6 * n,))]


def _gather_phases(w_refs, g_refs, send_sems, recv_sems):
    n = len(w_refs)
    x, y, c, others = _place()
    sibling, me = (x, y, 1 - c), _chip((x, y))
    halves = [_row_halves(c, w.shape[0]) for w in w_refs]

    def sent(a, j, o):
        mine, _ = halves[a]
        return _remote(w_refs[a].at[mine, :], g_refs[a].at[me, mine, :], send_sems, recv_sems, 6 * a + j, (*o, c))

    def passed(a, j, o):
        landed = g_refs[a].at[_chip(o), halves[a][0], :]
        return _remote(landed, landed, send_sems, recv_sems, 6 * a + 3 + j, sibling)

    def start():
        for a in range(n):
            for j, o in enumerate(others):
                sent(a, j, o).start()

    def forward():
        for j, o in enumerate(others):
            for a in range(n):
                landed = g_refs[a].at[_chip(o), halves[a][0], :]
                _remote(landed, landed, send_sems, recv_sems, 6 * a + j, (*o, c)).wait_recv()
                passed(a, j, o).start()

    def finish():
        for j, o in enumerate(others):
            for a in range(n):
                landed = g_refs[a].at[_chip(o), halves[a][1], :]
                _remote(landed, landed, send_sems, recv_sems, 6 * a + 3 + j, sibling).wait_recv()
        for a in range(n):
            for j, o in enumerate(others):
                sent(a, j, o).wait_send()
                passed(a, j, o).wait_send()

    return start, forward, finish


def _exchange_halves(arrays, *, name):
    n = len(arrays)

    def body(*refs):
        for phase in _exchange_phases(refs[:n], refs[n:2 * n], *refs[2 * n:]):
            phase()

    return pl.pallas_call(
        body, name=name, in_specs=[HBM] * n, out_specs=[HBM] * n, out_shape=_exchanged_shapes(arrays),
        scratch_shapes=_exchange_semaphores(n),
    )(*arrays)


def _exchanged_shapes(arrays):
    return [jax.ShapeDtypeStruct(s.shape[:-2] + (s.shape[-2] // 2, s.shape[-1]), F32) for s in arrays]


def _exchange_semaphores(n):
    return [pltpu.SemaphoreType.DMA((n,)), pltpu.SemaphoreType.DMA((n,))]


def _exchange_phases(g_refs, r_refs, send_sems, recv_sems):
    x, y, c, _ = _place()

    def copy(a):
        _, theirs = _row_halves(c, g_refs[a].shape[-2])
        src = g_refs[a].at[:, theirs, :] if len(g_refs[a].shape) == 3 else g_refs[a].at[theirs, :]
        return _remote(src, r_refs[a], send_sems, recv_sems, a, (x, y, 1 - c))

    def start():
        for a in range(len(g_refs)):
            copy(a).start()

    def finish():
        for a in range(len(g_refs)):
            copy(a).wait()

    return start, finish


def _scatter_to_owners(chip_sums):
    n = len(chip_sums)

    def body(*refs):
        for phase in _scatter_phases(refs[:n], refs[n:2 * n], *refs[2 * n:]):
            phase()

    return pl.pallas_call(
        body, name="scatter_to_owners", in_specs=[HBM] * n, out_specs=[HBM] * n,
        out_shape=_scattered_shapes(chip_sums), scratch_shapes=_scatter_semaphores(n),
    )(*chip_sums)


def _scattered_shapes(chip_sums):
    return [jax.ShapeDtypeStruct(b.shape if b.ndim == 3 else (N_CHIPS,) + b.shape, b.dtype) for b in chip_sums]


def _scatter_semaphores(n):
    return [pltpu.SemaphoreType.DMA((3 * n,)), pltpu.SemaphoreType.DMA((3 * n,))]


def _scatter_phases(b_refs, r_refs, send_sems, recv_sems):
    n = len(b_refs)
    x, y, c, others = _place()
    me = _chip((x, y))

    def sent(a, j, o):
        src = b_refs[a].at[_chip(o)] if len(b_refs[a].shape) == 3 else b_refs[a]
        return _remote(src, r_refs[a].at[me], send_sems, recv_sems, 3 * a + j, (*o, c))

    def start():
        for a in range(n):
            for j, o in enumerate(others):
                sent(a, j, o).start()

    def finish():
        for a in range(n):
            for j, o in enumerate(others):
                landed = r_refs[a].at[_chip(o)]
                _remote(landed, landed, send_sems, recv_sems, 3 * a + j, (*o, c)).wait_recv()
        for a in range(n):
            for j, o in enumerate(others):
                sent(a, j, o).wait_send()

    return start, finish


def _join_halves(totals):
    n = len(totals)

    def body(*refs):
        in_refs, out_refs, (send_sems, recv_sems) = refs[:n], refs[n:2 * n], refs[2 * n:]
        x, y, c, _ = _place()
        copies = []
        for a in range(n):
            mine, _ = _row_halves(c, in_refs[a].shape[0])
            copies.append(_remote(in_refs[a].at[mine, :], out_refs[a].at[mine, :], send_sems, recv_sems, a, (x, y, 1 - c)))
            copies[-1].start()
        for cp in copies:
            cp.wait()

    return pl.pallas_call(
        body, name="join_halves", in_specs=[HBM] * n, out_specs=[HBM] * n,
        out_shape=[jax.ShapeDtypeStruct(t.shape, F32) for t in totals], input_output_aliases={a: a for a in range(n)},
        scratch_shapes=[pltpu.SemaphoreType.DMA((n,)), pltpu.SemaphoreType.DMA((n,))],
    )(*totals)


ADD_ROWS = 128


def _add_sibling(g, r, place, *, name):
    lead, (half, cols) = g.shape[:-2], r.shape[-2:]
    tr = min(ADD_ROWS, half)
    nb = half // tr
    zeros = (0,) * len(lead)

    def body(place_ref, g_ref, r_ref, o_ref, ob_ref):
        s = g_ref[...] + r_ref[...]
        o_ref[...] = s
        ob_ref[...] = s.astype(BF)

    spec = pl.BlockSpec(lead + (tr, cols), lambda i, p: zeros + (i, 0))
    return pl.pallas_call(
        body, name=name,
        grid_spec=pltpu.PrefetchScalarGridSpec(
            num_scalar_prefetch=1, grid=(nb,),
            in_specs=[pl.BlockSpec(lead + (tr, cols), lambda i, p: zeros + (p[1] * nb + i, 0)), spec], out_specs=[spec, spec]),
        out_shape=[jax.ShapeDtypeStruct(r.shape, F32), jax.ShapeDtypeStruct(r.shape, BF)],
        compiler_params=_params("parallel"),
    )(place, g, r)


def _add_chips(own, received, place, *, name, own_slots):
    half, cols = received.shape[-2:]
    tr = min(ADD_ROWS, half)
    nb = half // tr

    def written(k, p):
        return jnp.where(p[0] == k, (k + 1) % N_CHIPS, k)

    def body(place_ref, own_ref, *refs):
        o_ref = refs[N_CHIPS]
        mine = own_ref[0] if own_slots else own_ref[...]
        if own_slots:
            acc = mine
            for k in range(N_CHIPS):
                acc = acc + jnp.where(place_ref[0] == k, 0.0, refs[k][0].astype(F32))
        else:
            terms = [jnp.where(place_ref[0] == k, mine, refs[k][0]) for k in range(N_CHIPS)]
            acc = ((terms[0] + terms[1]) + terms[2]) + terms[3]
        o_ref[...] = acc

    own_spec = (pl.BlockSpec((1, tr, cols), lambda i, p: (p[0], i, 0)) if own_slots
                else pl.BlockSpec((tr, cols), lambda i, p: (i, 0)))
    return pl.pallas_call(
        body, name=name,
        grid_spec=pltpu.PrefetchScalarGridSpec(
            num_scalar_prefetch=1, grid=(nb,),
            in_specs=[own_spec] + [pl.BlockSpec((1, tr, cols), functools.partial(lambda i, p, k: (written(k, p), i, 0), k=k))
                                   for k in range(N_CHIPS)],
            out_specs=pl.BlockSpec((tr, cols), lambda i, p: (p[1] * nb + i, 0))),
        out_shape=jax.ShapeDtypeStruct((2 * half, cols), F32), compiler_params=_params("parallel"),
    )(place, own, *([received] * N_CHIPS))


SHARDED = (("w_in", (D_MODEL, 4616), 1), ("w_branch_sgu", (SGU_W, D_MODEL), 1), ("w_branch_attn", (ATTN_W, D_MODEL), 1),
           ("w_out", (D_MODEL, D_MODEL), 0), ("w_up", (D_MODEL, D_FF), 1), ("w_down", (D_FF, D_MODEL), 0))
SMALL = (("g_mix_pre", (1, D_MODEL)), ("b_forget", (1, N_HEADS)), ("g_sgu", (1, SGU_W)), ("b_sgu", (1, SGU_W)),
         ("w_spatial", (N_GROUPS * CHUNK, CHUNK)), ("b_spatial", (N_GROUPS, CHUNK)), ("g_mix_post", (1, D_MODEL)),
         ("g_ffn_pre", (1, D_MODEL)), ("g_ffn_post", (1, D_MODEL)))
SMALL_ALIGN = 2 * ADD_ROWS


def _shard_shape(shape, axis):
    return tuple(s // N_CHIPS if a == axis else s for a, s in enumerate(shape))


def _slots_to_full(slots, axis):
    return slots.reshape(-1, slots.shape[2]) if axis == 0 else slots.transpose(1, 0, 2).reshape(slots.shape[1], -1)


def _full_to_slots(full, axis):
    if axis == 0:
        return full.reshape(N_CHIPS, -1, full.shape[1])
    return full.reshape(full.shape[0], N_CHIPS, -1).transpose(1, 0, 2)


def _small_rows(shape):
    return -(-(shape[0] * shape[1]) // (8 * LANES)) * 8


def _pack_small(values):
    parts = []
    for name, shape in SMALL:
        flat = values[name].reshape(-1)
        n = _small_rows(shape)
        parts.append(jnp.pad(flat, (0, n * LANES - flat.shape[0])).reshape(n, LANES))
    rows = sum(p.shape[0] for p in parts)
    pad = -(-rows // SMALL_ALIGN) * SMALL_ALIGN - rows
    return jnp.concatenate(parts + [jnp.zeros((pad, LANES), F32)], axis=0)


def _unpack_small(packed):
    out, row = {}, 0
    for name, shape in SMALL:
        n = _small_rows(shape)
        out[name] = packed[row:row + n].reshape(-1)[:shape[0] * shape[1]].reshape(shape)
        row += n
    return out


IN_Z, IN_Q, IN_K, IN_V, IN_F, IN_G, IN_END = 0, 1024, 1536, 2048, 2560, 2568, 4616


LATE_WEIGHTS = ("w_branch_sgu", "w_branch_attn", "w_out", "w_up", "w_down")
EARLY_GRADS = LATE_WEIGHTS


def _assemble(name, shard, gathered, chip):
    axis = {n: a for n, _, a in SHARDED}[name]
    slot = jnp.arange(N_CHIPS)[:, None, None]
    return _slots_to_full(jnp.where(slot == chip, shard[None], gathered), axis)


def _local_step(x, target, shards, small, place):
    b_forget = jnp.pad(small["b_forget"], ((0, 0), (0, LANES - N_HEADS)))
    causal = jnp.tril(jnp.ones((CHUNK, CHUNK), bool))
    ws = jnp.where(causal[None], small["w_spatial"].reshape(N_GROUPS, CHUNK, CHUNK), 0.0).astype(BF)
    ws_t = ws.transpose(0, 2, 1)
    bias_plane = jnp.repeat(small["b_spatial"].T, HEAD_DIM, axis=1)

    xn, (w_in_slots,) = _rms_fwd(x, small["g_mix_pre"], [shards["w_in"]])
    w_in = _assemble("w_in", shards["w_in"], w_in_slots, place[0])
    w_z, w_qkv, w_g = w_in[:, IN_Z:IN_Q], w_in[:, IN_Q:IN_F], w_in[:, IN_G:IN_END]
    w_q, w_k, w_v = w_in[:, IN_Q:IN_K], w_in[:, IN_K:IN_V], w_in[:, IN_V:IN_F]
    w_f = jnp.pad(w_in[:, IN_F:IN_G], ((0, 0), (0, LANES - N_HEADS)))
    z = _matmul([(xn, w_z)], nt=False, out_dtypes=[F32], name="proj_z")
    qkv = _matmul([(xn, w_qkv)], nt=False, out_dtypes=[BF], name="proj_qkv")
    gl = _matmul([(xn, w_g)], nt=False, out_dtypes=[BF], name="proj_gate")
    fl = _matmul([(xn, w_f)], nt=False, out_dtypes=[F32], name="proj_forget")
    ysgu = _sgu_fwd(z, small["g_sgu"], small["b_sgu"], ws, bias_plane)
    qf, kl, vl, tile_stats = _attn_prep(qkv, fl, b_forget)
    first_key_tile, last_query_tile, bounded = _attn_ranges(tile_stats)
    yattn, yattn_f, ql, gathered = _attn_fwd(qf, kl, vl, first_key_tile, bounded, [shards[name] for name in LATE_WEIGHTS])
    w = {name: _assemble(name, shards[name], got, place[0]) for name, got in zip(LATE_WEIGHTS, gathered, strict=True)}
    a, b, merged = _branch_merge(ysgu, yattn, w["w_branch_sgu"], w["w_branch_attn"], gl)
    o = _matmul([(merged, w["w_out"])], nt=False, out_dtypes=[F32], name="proj_out")
    h1, xn2 = _mixer_out_fwd(o, x, small["g_mix_post"], small["g_ffn_pre"])

    def relu2(acc):
        r = jnp.maximum(acc, 0.0)
        return (r * r,)

    hid = _matmul([(xn2, w["w_up"])], nt=False, out_dtypes=[BF], name="ffn_up", epilogue=relu2, tm=FFN_ROWS)
    dy, ddn, sq, dg_ffn_post = _matmul_rows(
        hid, w["w_down"], nt=False, rows=[h1, target], vecs=[small["g_ffn_post"]], row_outs=[F32, BF], n_sums=2,
        epilogue=_loss_head, name="ffn_down_loss")

    dup = _matmul([(ddn, w["w_down"])], nt=True, out_dtypes=[BF], name="ffn_down_bwd", tm=FFN_ROWS,
                  epilogue=lambda acc, h: (acc * (2.0 * jnp.sqrt(h.astype(F32))),), extras=[hid])
    dw_down = _matmul_tn(hid, ddn, name="dw_down")
    dh1, do, dg_ffn_pre, dg_mix_post = _matmul_rows(
        dup, w["w_up"], nt=True, rows=[h1, dy, o], vecs=[small["g_ffn_pre"], small["g_mix_post"]], row_outs=[F32, BF],
        n_sums=2, epilogue=_mixer_out_bwd, name="ffn_up_bwd_norms")
    dw_up = _matmul_tn(xn2, dup, name="dw_up", slots=True)

    def gate_bwd(dm, a_t, b_t, gla, glb):
        ga, gb = jax.nn.sigmoid(gla.astype(F32)), jax.nn.sigmoid(glb.astype(F32))
        return dm * ga, dm * gb, dm * a_t.astype(F32) * (ga * (1.0 - ga)), dm * b_t.astype(F32) * (gb * (1.0 - gb))

    da, db, dgla, dglb = _matmul([(do, w["w_out"])], nt=True, out_dtypes=[BF] * 4, name="proj_out_bwd",
                                 epilogue=gate_bwd, extras=[a, b, (gl, 0), (gl, D_MODEL)])
    dw_out = _matmul_tn(merged, do, name="dw_out")
    dysgu = _matmul([(da, w["w_branch_sgu"])], nt=True, out_dtypes=[F32], name="branch_sgu_bwd")
    dyattn = _matmul([(db, w["w_branch_attn"])], nt=True, out_dtypes=[F32], name="branch_attn_bwd")
    dw_bs = _matmul_tn(ysgu, da, name="dw_branch_sgu")
    dw_ba = _matmul_tn(yattn, db, name="dw_branch_attn")
    early = {"w_branch_sgu": _full_to_slots(dw_bs, 1), "w_branch_attn": _full_to_slots(dw_ba, 1),
             "w_out": _full_to_slots(dw_out, 0), "w_up": dw_up, "w_down": _full_to_slots(dw_down, 0)}
    (dz, dws, dbs, dg_sgu, db_sgu), early_theirs = _sgu_bwd(
        dysgu, z, small["g_sgu"], small["b_sgu"], ws, ws_t, bias_plane, [early[name] for name in EARLY_GRADS])
    early_sums = {name: _add_sibling(early[name], theirs, place, name="add_sibling_" + name)
                  for name, theirs in zip(EARLY_GRADS, early_theirs, strict=True)}
    dout = _attn_bwd_prep(dyattn, yattn_f)
    (dq, dk, dv, ext_q, ext_k), early_received = _attn_bwd(
        kl, vl, ql, dout, last_query_tile, [early_sums[name][1] for name in EARLY_GRADS])
    dfl, dbf = _forget_bwd(ext_q, ext_k, fl, b_forget)
    dw_in = _full_to_slots(jnp.concatenate(
        [_matmul_tn(xn, dz, name="dw_in_z"), _matmul_tn(xn, dq, name="dw_in_q"), _matmul_tn(xn, dk, name="dw_in_k"),
         _matmul_tn(xn, dv, name="dw_in_v"), _matmul_tn(xn, dfl, name="dw_in_f")[:, :N_HEADS],
         _matmul_tn(xn, dgla, name="dw_in_ga"), _matmul_tn(xn, dglb, name="dw_in_gb")], axis=1), 1)
    (dw_in_theirs,) = _exchange_halves([dw_in], name="exchange_halves_w_in")
    dw_in_sum = _add_sibling(dw_in, dw_in_theirs, place, name="add_sibling_w_in")
    (dxn,), (dw_in_received,) = _matmul(
        [(dz, w_z), (dq, w_q), (dk, w_k), (dv, w_v), (dgla, w_g[:, :D_MODEL]), (dglb, w_g[:, D_MODEL:]), (dfl, w_f)],
        nt=True, out_dtypes=[F32], name="proj_in_bwd", scatter=[dw_in_sum[1]])
    dx, dg_mix_pre = _input_norm_bwd(x, dxn, dh1, small["g_mix_pre"])

    reduced = {name: (early_sums[name][0], got) for name, got in zip(EARLY_GRADS, early_received, strict=True)}
    reduced["w_in"] = (dw_in_sum[0], dw_in_received)
    small_grads = {"g_mix_pre": dg_mix_pre, "b_forget": dbf[:, :N_HEADS], "g_sgu": dg_sgu, "b_sgu": db_sgu,
                   "w_spatial": dws.reshape(N_GROUPS * CHUNK, CHUNK), "b_spatial": dbs[:, :N_GROUPS].T,
                   "g_mix_post": dg_mix_post, "g_ffn_pre": dg_ffn_pre, "g_ffn_post": dg_ffn_post}
    return sq, dx, reduced, small_grads


NAMES = ("g_mix_pre", "w_in", "b_forget", "g_sgu", "b_sgu", "w_spatial", "b_spatial", "w_branch_sgu", "w_branch_attn",
         "w_out", "g_mix_post", "g_ffn_pre", "w_up", "w_down", "g_ffn_post")


def kernel(x, g_mix_pre, w_in, b_forget, g_sgu, b_sgu, w_spatial, b_spatial, w_branch_sgu, w_branch_attn, w_out, g_mix_post, g_ffn_pre, w_up, w_down, g_ffn_post, loss_target, m_g_mix_pre, m_w_in, m_b_forget, m_g_sgu, m_b_sgu, m_w_spatial, m_b_spatial, m_w_branch_sgu, m_w_branch_attn, m_w_out, m_g_mix_post, m_g_ffn_pre, m_w_up, m_w_down, m_g_ffn_post, v_g_mix_pre, v_w_in, v_b_forget, v_g_sgu, v_b_sgu, v_w_spatial, v_b_spatial, v_w_branch_sgu, v_w_branch_attn, v_w_out, v_g_mix_post, v_g_ffn_pre, v_w_up, v_w_down, v_g_ffn_post):
    weights = dict(zip(NAMES, (g_mix_pre, w_in, b_forget, g_sgu, b_sgu, w_spatial, b_spatial, w_branch_sgu, w_branch_attn,
                               w_out, g_mix_post, g_ffn_pre, w_up, w_down, g_ffn_post), strict=True))
    first = dict(zip(NAMES, (m_g_mix_pre, m_w_in, m_b_forget, m_g_sgu, m_b_sgu, m_w_spatial, m_b_spatial, m_w_branch_sgu,
                             m_w_branch_attn, m_w_out, m_g_mix_post, m_g_ffn_pre, m_w_up, m_w_down, m_g_ffn_post), strict=True))
    second = dict(zip(NAMES, (v_g_mix_pre, v_w_in, v_b_forget, v_g_sgu, v_b_sgu, v_w_spatial, v_b_spatial, v_w_branch_sgu,
                              v_w_branch_attn, v_w_out, v_g_mix_post, v_g_ffn_pre, v_w_up, v_w_down, v_g_ffn_post), strict=True))
    shard_shapes = {name: _shard_shape(shape, axis) for name, shape, axis in SHARDED}
    small_shapes = dict(SMALL)
    view = lambda name, a: a.reshape(shard_shapes.get(name) or small_shapes[name])

    place = jnp.stack([2 * lax.axis_index("x") + lax.axis_index("y"), lax.axis_index("c")]).astype(jnp.int32)

    shards = {name: view(name, weights[name]).astype(BF) for name, _, _ in SHARDED}
    small = {name: view(name, weights[name]) for name, _ in SMALL}
    sq, dx, reduced, small_grads = _local_step(x[0], loss_target[0], shards, small, place)
    loss = lax.psum(0.5 * jnp.sum(sq) / D_MODEL, ("x", "y", "c"))

    small_mine = _pack_small(small_grads)
    (small_theirs,) = _exchange_halves([small_mine], name="exchange_halves_small")
    small_sum, _ = _add_sibling(small_mine, small_theirs, place, name="add_sibling_small")
    (small_received,) = _scatter_to_owners([small_sum])
    totals = {name: _add_chips(s, r, place, name="add_chips_" + name, own_slots=True) for name, (s, r) in reduced.items()}
    small_total = _add_chips(small_sum, small_received, place, name="add_chips_small", own_slots=False)
    joined = _join_halves([totals[name] for name, _, _ in SHARDED] + [small_total])
    grad = {**{name: g for (name, _, _), g in zip(SHARDED, joined[:-1], strict=True)}, **_unpack_small(joined[-1])}

    grad_out, delta, new_m, new_v = {}, {}, {}, {}
    for name in NAMES:
        rows, cols = grad[name].shape
        as_given = lambda a: a.reshape(1, rows, cols)
        grad_out[name], delta[name], new_m[name], new_v[name] = _adamw(
            as_given(weights[name]), grad[name], as_given(first[name]), as_given(second[name]), name="adamw_" + name)

    like = lambda d: [d[name].reshape(weights[name].shape) for name in NAMES]
    return (loss, dx[None], *like(grad_out), *like(delta), *like(new_m), *like(new_v))
```

```python
import functools

import jax
import jax.numpy as jnp
import numpy as np
from jax import lax
from jax.experimental import pallas as pl
from jax.experimental.pallas import tpu as pltpu

F32 = jnp.float32
BF = jnp.bfloat16
MESH = pl.DeviceIdType.MESH

D_MODEL = 1024
N_HEADS = 8
HEAD_DIM = 64
ATTN_W = N_HEADS * HEAD_DIM
SGU_W = 512
N_GROUPS = 8
CHUNK = 128
D_FF = 4096
EPS = 1e-6
Q_SCALE = HEAD_DIM ** -0.5
N_CHIPS = 4
LANES = 128

ADAM_LR = 0.001
ADAM_B1 = 0.9
ADAM_B2 = 0.999
ADAM_EPS = 1e-08
ADAM_WD = 0.01
ADAM_STEP = 10

VMEM_LIMIT = 48 * 1024 * 1024
BIG_VMEM = 58 * 1024 * 1024
NEG = -1e30

LANE_ROWSUM = HEAD_DIM
LANE_COLSUM = HEAD_DIM + 3


def _params(*sem):
    return pltpu.CompilerParams(dimension_semantics=sem, vmem_limit_bytes=VMEM_LIMIT)


def _dot(a, b):
    return jnp.dot(a, b, preferred_element_type=F32)


def _dot_nt(a, b):
    return lax.dot_general(a, b, (((1,), (1,)), ((), ())), preferred_element_type=F32)


def _dot_tn(a, b):
    return lax.dot_general(a, b, (((0,), (0,)), ((), ())), preferred_element_type=F32)


def _split3(c):
    hi = c.astype(BF).astype(F32)
    r = c - hi
    mid = r.astype(BF).astype(F32)
    lo = (r - mid).astype(BF).astype(F32)
    return hi, mid, lo


def _gelu(x):
    k = 0.7978845608028654
    return 0.5 * x * (1.0 + jnp.tanh(k * (x + 0.044715 * (x * x * x))))


def _gelu_grad(x):
    k = 0.7978845608028654
    x2 = x * x
    t = jnp.tanh(k * (x + 0.044715 * (x2 * x)))
    return 0.5 * (1.0 + t) + 0.5 * x * (1.0 - t * t) * (k * (1.0 + 3.0 * 0.044715 * x2))


def _rms_bwd(a, g, dy):
    r = lax.rsqrt(jnp.mean(a * a, axis=-1, keepdims=True) + EPS)
    n = a * r
    dn = dy * g
    da = r * (dn - n * jnp.mean(dn * n, axis=-1, keepdims=True))
    return da, dy * n


MM_ROWS = 1024
MM_COLS = 512
FFN_ROWS = 2048


def _matmul(pairs, *, nt, out_dtypes, name, tm=MM_ROWS, tn=MM_COLS, epilogue=None, extras=(), scatter=()):
    n_pairs, n_extra, n_out, n_scatter = len(pairs), len(extras), len(out_dtypes), len(scatter)
    M = pairs[0][0].shape[0]
    N = pairs[0][1].shape[0] if nt else pairs[0][1].shape[1]
    tm, tn = min(tm, M), min(tn, N)
    assert M % tm == 0 and N % tn == 0
    grid = (M // tm, N // tn)

    def body(*refs):
        n_in = 2 * n_pairs + n_extra
        if n_scatter:
            step = pl.program_id(0) * grid[1] + pl.program_id(1)
            first = n_in + n_scatter + n_out
            scatter_start, scatter_finish = _scatter_phases(
                refs[n_in:n_in + n_scatter], refs[first:first + n_scatter], *refs[first + n_scatter:])
            pl.when(step == 0)(scatter_start)
        acc = None
        for p in range(n_pairs):
            a_ref, b_ref = refs[2 * p], refs[2 * p + 1]
            d = _dot_nt(a_ref[...], b_ref[...]) if nt else _dot(a_ref[...], b_ref[...])
            acc = d if acc is None else acc + d
        e_refs = refs[2 * n_pairs:n_in]
        o_refs = refs[n_in + n_scatter:n_in + n_scatter + n_out]
        outs = (acc,) if epilogue is None else epilogue(acc, *[e[...] for e in e_refs])
        for o_ref, o in zip(o_refs, outs, strict=True):
            o_ref[...] = o.astype(o_ref.dtype)
        if n_scatter:
            pl.when(step == grid[0] * grid[1] - 1)(scatter_finish)

    in_specs, args = [], []
    for a, b in pairs:
        K = a.shape[1]
        in_specs.append(pl.BlockSpec((tm, K), lambda i, j: (i, 0)))
        in_specs.append(pl.BlockSpec((tn, K), lambda i, j: (j, 0)) if nt else pl.BlockSpec((K, tn), lambda i, j: (0, j)))
        args += [a, b]
    for e in extras:
        e, col = e if isinstance(e, tuple) else (e, 0)
        in_specs.append(pl.BlockSpec((tm, tn), functools.partial(lambda i, j, off: (i, j + off), off=col // tn)))
        args.append(e)
    order = ("arbitrary", "arbitrary") if n_scatter else ("parallel", "parallel")
    outs = pl.pallas_call(
        body, name=name, grid=grid, in_specs=in_specs + [HBM] * n_scatter,
        out_specs=[pl.BlockSpec((tm, tn), lambda i, j: (i, j)) for _ in out_dtypes] + [HBM] * n_scatter,
        out_shape=[jax.ShapeDtypeStruct((M, N), dt) for dt in out_dtypes] + (_scattered_shapes(scatter) if n_scatter else []),
        scratch_shapes=_scatter_semaphores(n_scatter) if n_scatter else [],
        compiler_params=_params(*order),
    )(*args, *scatter)
    if n_scatter:
        return outs[:n_out], outs[n_out:]
    return outs if len(outs) > 1 else outs[0]


def _matmul_tn(a, b, *, name, tm=1024, tn=1024, tk=2048, slots=False):
    T, K1 = a.shape
    N = b.shape[1]
    tm, tn, tk = min(tm, K1), min(tn, N // N_CHIPS if slots else N), min(tk, T)
    assert K1 % tm == 0 and (N // N_CHIPS if slots else N) % tn == 0 and T % tk == 0
    per_slot = N // N_CHIPS // tn

    def body(a_ref, b_ref, o_ref):
        @pl.when(pl.program_id(2) == 0)
        def _():
            o_ref[...] = jnp.zeros_like(o_ref)

        o_ref[...] += _dot_tn(a_ref[...], b_ref[...])

    if slots:
        out_spec = pl.BlockSpec((None, tm, tn), lambda i, j, k: (j // per_slot, i, j % per_slot))
        out_shape = jax.ShapeDtypeStruct((N_CHIPS, K1, N // N_CHIPS), F32)
    else:
        out_spec = pl.BlockSpec((tm, tn), lambda i, j, k: (i, j))
        out_shape = jax.ShapeDtypeStruct((K1, N), F32)
    return pl.pallas_call(
        body, name=name, grid=(K1 // tm, N // tn, T // tk),
        in_specs=[pl.BlockSpec((tk, tm), lambda i, j, k: (k, i)), pl.BlockSpec((tk, tn), lambda i, j, k: (k, j))],
        out_specs=out_spec, out_shape=out_shape,
        compiler_params=_params("parallel", "parallel", "arbitrary"),
    )(a, b)


def _branch_merge(ysgu, yattn, w_bs, w_ba, gl, *, tm=MM_ROWS, tn=MM_COLS):
    T = ysgu.shape[0]
    tm = min(tm, T)
    nj = D_MODEL // tn

    def body(ys_ref, ya_ref, wbs_ref, wba_ref, gla_ref, glb_ref, a_ref, b_ref, m_ref):
        a = _dot(ys_ref[...], wbs_ref[...])
        b = _dot(ya_ref[...], wba_ref[...])
        a_ref[...] = a.astype(BF)
        b_ref[...] = b.astype(BF)
        m_ref[...] = (jax.nn.sigmoid(gla_ref[...].astype(F32)) * a + jax.nn.sigmoid(glb_ref[...].astype(F32)) * b).astype(BF)

    return pl.pallas_call(
        body, name="branch_merge", grid=(T // tm, nj),
        in_specs=[
            pl.BlockSpec((tm, SGU_W), lambda i, j: (i, 0)),
            pl.BlockSpec((tm, ATTN_W), lambda i, j: (i, 0)),
            pl.BlockSpec((SGU_W, tn), lambda i, j: (0, j)),
            pl.BlockSpec((ATTN_W, tn), lambda i, j: (0, j)),
            pl.BlockSpec((tm, tn), lambda i, j: (i, j)),
            pl.BlockSpec((tm, tn), lambda i, j: (i, j + nj)),
        ],
        out_specs=[pl.BlockSpec((tm, tn), lambda i, j: (i, j))] * 3,
        out_shape=[jax.ShapeDtypeStruct((T, D_MODEL), BF)] * 3,
        compiler_params=_params("parallel", "parallel"),
    )(ysgu, yattn, w_bs, w_ba, gl, gl)


def _row_spec(tr, width):
    return pl.BlockSpec((tr, width), lambda i: (i, 0))


def _vec_spec(width):
    return pl.BlockSpec((1, width), lambda i: (0, 0))


def _rms_fwd(x, g, shards, *, tr=256):
    T = x.shape[0]
    tr = min(tr, T)
    n_steps = T // tr
    k = len(shards)

    def body(x_ref, g_ref, *refs):
        step = pl.program_id(0)
        gather_start, gather_forward, gather_finish = _gather_phases(refs[:k], refs[k + 1:2 * k + 1], *refs[2 * k + 1:])
        pl.when(step == 0)(gather_start)
        pl.when(step == (3 * n_steps) // 4)(gather_forward)
        xv = x_ref[...]
        r = lax.rsqrt(jnp.mean(xv * xv, axis=-1, keepdims=True) + EPS)
        refs[k][...] = ((xv * r) * g_ref[...]).astype(BF)
        pl.when(step == n_steps - 1)(gather_finish)

    outs = pl.pallas_call(
        body, name="rms_fwd", grid=(n_steps,),
        in_specs=[_row_spec(tr, D_MODEL), _vec_spec(D_MODEL)] + [HBM] * k, out_specs=[_row_spec(tr, D_MODEL)] + [HBM] * k,
        out_shape=[jax.ShapeDtypeStruct((T, D_MODEL), BF)] + _gathered_shapes(shards),
        scratch_shapes=_gather_semaphores(k), compiler_params=_params("arbitrary"),
    )(x, g, *shards)
    return outs[0], outs[1:]


def _mixer_out_fwd(o, x, g_post, g_pre, *, tr=256):
    T = x.shape[0]
    tr = min(tr, T)

    def body(o_ref, x_ref, gpost_ref, gpre_ref, h1_ref, xn2_ref):
        ov = o_ref[...]
        r = lax.rsqrt(jnp.mean(ov * ov, axis=-1, keepdims=True) + EPS)
        h1 = x_ref[...] + (ov * r) * gpost_ref[...]
        h1_ref[...] = h1
        r2 = lax.rsqrt(jnp.mean(h1 * h1, axis=-1, keepdims=True) + EPS)
        xn2_ref[...] = ((h1 * r2) * gpre_ref[...]).astype(BF)

    return pl.pallas_call(
        body, name="mixer_out_fwd", grid=(T // tr,),
        in_specs=[_row_spec(tr, D_MODEL), _row_spec(tr, D_MODEL), _vec_spec(D_MODEL), _vec_spec(D_MODEL)],
        out_specs=[_row_spec(tr, D_MODEL), _row_spec(tr, D_MODEL)],
        out_shape=[jax.ShapeDtypeStruct((T, D_MODEL), F32), jax.ShapeDtypeStruct((T, D_MODEL), BF)],
        compiler_params=_params("parallel"),
    )(o, x, g_post, g_pre)


def _matmul_rows(a, b, *, nt, rows, vecs, row_outs, n_sums, epilogue, name, tm=512):
    M, K = a.shape
    N = b.shape[0] if nt else b.shape[1]
    tm = min(tm, M)
    n_rows, n_vecs, n_out = len(rows), len(vecs), len(row_outs)

    def body(a_ref, b_ref, *refs):
        r_refs, v_refs = refs[:n_rows], refs[n_rows:n_rows + n_vecs]
        o_refs, s_refs = refs[n_rows + n_vecs:n_rows + n_vecs + n_out], refs[n_rows + n_vecs + n_out:]

        @pl.when(pl.program_id(0) == 0)
        def _():
            for s_ref in s_refs:
                s_ref[...] = jnp.zeros_like(s_ref)

        acc = _dot_nt(a_ref[...], b_ref[...]) if nt else _dot(a_ref[...], b_ref[...])
        outs = epilogue(acc, *[r[...] for r in r_refs], *[v[...] for v in v_refs])
        for o_ref, o in zip(o_refs, outs[:n_out], strict=True):
            o_ref[...] = o.astype(o_ref.dtype)
        for s_ref, term in zip(s_refs, outs[n_out:], strict=True):
            s_ref[...] += jnp.sum(term, axis=0, keepdims=True)

    return pl.pallas_call(
        body, name=name, grid=(M // tm,),
        in_specs=[_row_spec(tm, K), pl.BlockSpec(b.shape, lambda i: (0, 0))] + [_row_spec(tm, N)] * n_rows + [_vec_spec(N)] * n_vecs,
        out_specs=[_row_spec(tm, N)] * n_out + [_vec_spec(N)] * n_sums,
        out_shape=[jax.ShapeDtypeStruct((M, N), dt) for dt in row_outs] + [jax.ShapeDtypeStruct((1, N), F32)] * n_sums,
        compiler_params=pltpu.CompilerParams(dimension_semantics=("arbitrary",), vmem_limit_bytes=BIG_VMEM),
    )(a, b, *rows, *vecs)


def _loss_head(dn, h1, target, g):
    r = lax.rsqrt(jnp.mean(dn * dn, axis=-1, keepdims=True) + EPS)
    err = h1 + (dn * r) * g - target
    dy = err * (1.0 / D_MODEL)
    ddn, dg_terms = _rms_bwd(dn, g, dy)
    return dy, ddn, err * err, dg_terms


def _mixer_out_bwd(dxn2, h1, dy, o, g_pre, g_post):
    da, dg_pre_terms = _rms_bwd(h1, g_pre, dxn2)
    dh1 = dy + da
    do, dg_post_terms = _rms_bwd(o, g_post, dh1)
    return dh1, do, dg_pre_terms, dg_post_terms


def _input_norm_bwd(x, dxn, dh1, g, *, tr=256):
    T = x.shape[0]
    tr = min(tr, T)

    def body(x_ref, dxn_ref, dh1_ref, g_ref, dx_ref, dg_ref):
        @pl.when(pl.program_id(0) == 0)
        def _():
            dg_ref[...] = jnp.zeros_like(dg_ref)

        da, dgp = _rms_bwd(x_ref[...], g_ref[...], dxn_ref[...])
        dx_ref[...] = dh1_ref[...] + da
        dg_ref[...] += jnp.sum(dgp, axis=0, keepdims=True)

    return pl.pallas_call(
        body, name="input_norm_bwd", grid=(T // tr,),
        in_specs=[_row_spec(tr, D_MODEL)] * 3 + [_vec_spec(D_MODEL)],
        out_specs=[_row_spec(tr, D_MODEL), _vec_spec(D_MODEL)],
        out_shape=[jax.ShapeDtypeStruct((T, D_MODEL), F32), jax.ShapeDtypeStruct((1, D_MODEL), F32)],
        compiler_params=_params("arbitrary"),
    )(x, dxn, dh1, g)


def _sgu_norm(z_tile, g, b):
    gz = _gelu(z_tile)
    u, vv = gz[:, :SGU_W], gz[:, SGU_W:]
    xc = vv - jnp.mean(vv, axis=-1, keepdims=True)
    rstd = lax.rsqrt(jnp.mean(xc * xc, axis=-1, keepdims=True) + EPS)
    xhat = xc * rstd
    return u, xhat, rstd, xhat * g + b


def _sgu_mix(w_ref, v_bf, first_half):
    parts = []
    for p in range(N_GROUPS // 2):
        vp = v_bf[:, p * LANES:(p + 1) * LANES]
        parts.append(jnp.where(first_half, _dot(w_ref[2 * p], vp), _dot(w_ref[2 * p + 1], vp)))
    return jnp.concatenate(parts, axis=1)


def _sgu_fwd(z, g_sgu, b_sgu, ws, bias_plane, *, tm=512):
    T = z.shape[0]
    tm = min(tm, T)

    def body(z_ref, g_ref, b_ref, ws_ref, bp_ref, y_ref):
        u, _, _, vn = _sgu_norm(z_ref[...], g_ref[...], b_ref[...])
        vn_bf = vn.astype(BF)
        first_half = lax.broadcasted_iota(jnp.int32, (CHUNK, LANES), 1) < HEAD_DIM
        for c in range(tm // CHUNK):
            rows = slice(c * CHUNK, (c + 1) * CHUNK)
            s = _sgu_mix(ws_ref, vn_bf[rows, :], first_half) + bp_ref[...]
            y_ref[rows, :] = (u[rows, :] * s).astype(BF)

    return pl.pallas_call(
        body, name="sgu_fwd", grid=(T // tm,),
        in_specs=[_row_spec(tm, 2 * SGU_W), _vec_spec(SGU_W), _vec_spec(SGU_W),
                  pl.BlockSpec((N_GROUPS, CHUNK, CHUNK), lambda i: (0, 0, 0)),
                  pl.BlockSpec((CHUNK, SGU_W), lambda i: (0, 0))],
        out_specs=_row_spec(tm, SGU_W), out_shape=jax.ShapeDtypeStruct((T, SGU_W), BF),
        compiler_params=_params("parallel"),
    )(z, g_sgu, b_sgu, ws, bias_plane)


def _sgu_bwd(dy, z, g_sgu, b_sgu, ws, ws_t, bias_plane, exchange, *, tm=512):
    T = z.shape[0]
    tm = min(tm, T)
    n_steps = T // tm
    k = len(exchange)

    def body(dy_ref, z_ref, g_ref, b_ref, ws_ref, wst_ref, bp_ref, *refs):
        x_refs, (dz_ref, dws_ref, dbs_ref, dg_ref, db_ref), r_refs = refs[:k], refs[k:k + 5], refs[k + 5:2 * k + 5]
        dbp_ref, send_sems, recv_sems = refs[2 * k + 5:]
        step = pl.program_id(0)
        exchange_start, exchange_finish = _exchange_phases(x_refs, r_refs, send_sems, recv_sems)
        pl.when(step == 0)(exchange_start)

        @pl.when(step == 0)
        def _():
            dws_ref[...] = jnp.zeros_like(dws_ref)
            dg_ref[...] = jnp.zeros_like(dg_ref)
            db_ref[...] = jnp.zeros_like(db_ref)
            dbp_ref[...] = jnp.zeros_like(dbp_ref)

        g = g_ref[...]
        zt = z_ref[...]
        u, xhat, rstd, vn = _sgu_norm(zt, g, b_ref[...])
        vn_bf = vn.astype(BF)
        first_half = lax.broadcasted_iota(jnp.int32, (CHUNK, LANES), 1) < HEAD_DIM
        dyv = dy_ref[...]
        dg_acc = jnp.zeros((1, SGU_W), F32)
        db_acc = jnp.zeros((1, SGU_W), F32)
        for c in range(tm // CHUNK):
            rows = slice(c * CHUNK, (c + 1) * CHUNK)
            v_c = vn_bf[rows, :]
            s = _sgu_mix(ws_ref, v_c, first_half) + bp_ref[...]
            dy_c = dyv[rows, :]
            du = dy_c * s
            dsv = dy_c * u[rows, :]
            dbp_ref[...] += dsv
            ds_bf = dsv.astype(BF)
            zero = jnp.zeros((CHUNK, LANES), BF)
            for p in range(N_GROUPS // 2):
                dsp = ds_bf[:, p * LANES:(p + 1) * LANES]
                vp = v_c[:, p * LANES:(p + 1) * LANES]
                dws_ref[2 * p] += _dot_nt(jnp.where(first_half, dsp, zero), vp)
                dws_ref[2 * p + 1] += _dot_nt(jnp.where(first_half, zero, dsp), vp)
            dvn = _sgu_mix(wst_ref, ds_bf, first_half)
            xh = xhat[rows, :]
            dxh = dvn * g
            dvv = rstd[rows, :] * (dxh - jnp.mean(dxh, axis=-1, keepdims=True)
                                   - xh * jnp.mean(dxh * xh, axis=-1, keepdims=True))
            dg_acc += jnp.sum(dvn * xh, axis=0, keepdims=True)
            db_acc += jnp.sum(dvn, axis=0, keepdims=True)
            dgz = jnp.concatenate([du, dvv], axis=1)
            dz_ref[rows, :] = (dgz * _gelu_grad(zt[rows, :])).astype(BF)
        dg_ref[...] += dg_acc
        db_ref[...] += db_acc

        @pl.when(step == n_steps - 1)
        def _():
            r = lax.broadcasted_iota(jnp.int32, (CHUNK, CHUNK), 0)
            cidx = lax.broadcasted_iota(jnp.int32, (CHUNK, CHUNK), 1)
            causal = (cidx <= r).astype(F32)
            for gi in range(N_GROUPS):
                dws_ref[gi] = dws_ref[gi] * causal
            lane = lax.broadcasted_iota(jnp.int32, (CHUNK, LANES), 1)
            out = jnp.zeros((CHUNK, LANES), F32)
            dbp = dbp_ref[...]
            for gi in range(N_GROUPS):
                col = jnp.sum(dbp[:, gi * HEAD_DIM:(gi + 1) * HEAD_DIM], axis=1, keepdims=True)
                out = jnp.where(lane == gi, col, out)
            dbs_ref[...] = out
            exchange_finish()

    w_spec = pl.BlockSpec((N_GROUPS, CHUNK, CHUNK), lambda i: (0, 0, 0))
    plane = pl.BlockSpec((CHUNK, SGU_W), lambda i: (0, 0))
    outs = pl.pallas_call(
        body, name="sgu_bwd", grid=(n_steps,),
        in_specs=[_row_spec(tm, SGU_W), _row_spec(tm, 2 * SGU_W), _vec_spec(SGU_W), _vec_spec(SGU_W), w_spec, w_spec, plane]
        + [HBM] * k,
        out_specs=[_row_spec(tm, 2 * SGU_W), w_spec, pl.BlockSpec((CHUNK, LANES), lambda i: (0, 0)),
                   _vec_spec(SGU_W), _vec_spec(SGU_W)] + [HBM] * k,
        out_shape=[jax.ShapeDtypeStruct((T, 2 * SGU_W), BF), jax.ShapeDtypeStruct((N_GROUPS, CHUNK, CHUNK), F32),
                   jax.ShapeDtypeStruct((CHUNK, LANES), F32), jax.ShapeDtypeStruct((1, SGU_W), F32),
                   jax.ShapeDtypeStruct((1, SGU_W), F32)] + _exchanged_shapes(exchange),
        scratch_shapes=[pltpu.VMEM((CHUNK, SGU_W), F32)] + _exchange_semaphores(k),
        compiler_params=_params("arbitrary"),
    )(dy, z, g_sgu, b_sgu, ws, ws_t, bias_plane, *exchange)
    return outs[:5], outs[5:]


def _tri(n, upper):
    r = lax.broadcasted_iota(jnp.int32, (n, n), 0)
    c = lax.broadcasted_iota(jnp.int32, (n, n), 1)
    return ((c >= r) if upper else (c <= r)).astype(BF)


def _scan_dot(tri, x):
    hi, mid, lo = _split3(x)
    return (_dot(tri, hi.astype(BF)) + _dot(tri, mid.astype(BF))) + _dot(tri, lo.astype(BF))


def _with_lanes(base, lane, start, cols):
    out = base
    for k, col in enumerate(cols):
        if col is not None:
            out = jnp.where(lane == start + k, col, out)
    return out


def _logit_bound(q_norm, k_norm):
    return NORM_SLACK * q_norm * k_norm + 1.0


ATTN_TILE = 512
SKIP_BELOW = -110.0
NORM_SLACK = 1.001
BOUNDED_GAP = 60.0


def _attn_prep(qkv, fl, b_forget, *, tp=ATTN_TILE):
    T = qkv.shape[0]
    tp = min(tp, T)
    head_sum, gather6, place_q, place_k, place_v = (jnp.asarray(m, BF) for m in _attn_placements())

    def body(qkv_ref, fl_ref, bf_ref, hs_ref, g6_ref, pq_ref, pk_ref, pv_ref, qf_ref, kl_ref, vl_ref, st_ref, carry_ref, kmax_ref):
        @pl.when(pl.program_id(0) == 0)
        def _():
            carry_ref[...] = jnp.zeros_like(carry_ref)
            kmax_ref[...] = jnp.zeros_like(kmax_ref)

        x = fl_ref[...] + bf_ref[...]
        logf = jnp.minimum(x, 0.0) - jnp.log(1.0 + jnp.exp(-jnp.abs(x)))
        cum = _scan_dot(_tri(tp, upper=False), logf) + carry_ref[...]
        carry_ref[...] = cum[tp - 1:tp, :]

        def head_norms(block):
            sq = block * block
            hi = sq.astype(BF)
            return _dot(hi, hs_ref[...]) + _dot((sq - hi.astype(F32)).astype(BF), hs_ref[...])

        qkvv = qkv_ref[...]
        q_norm = NORM_SLACK * jnp.sqrt(head_norms(qkvv[:, :ATTN_W].astype(F32) * Q_SCALE))
        kn = NORM_SLACK * jnp.sqrt(jnp.max(head_norms(qkvv[:, ATTN_W:2 * ATTN_W].astype(F32)), axis=0, keepdims=True))
        k_seen = jnp.maximum(kmax_ref[...], kn)
        kmax_ref[...] = k_seen
        rows = (jnp.max(q_norm, axis=0, keepdims=True), kn, jnp.max(cum, axis=0, keepdims=True),
                jnp.min(cum, axis=0, keepdims=True), k_seen)
        st_ref[...] = jnp.zeros_like(st_ref)
        for k, row in enumerate(rows):
            st_ref[0, k:k + 1, :] = row
        parts = jnp.concatenate([p.astype(BF) for p in _split3(cum) + _split3(-_logit_bound(q_norm, k_seen))], axis=1)
        lane = lax.broadcasted_iota(jnp.int32, (tp, LANES), 1)
        side = jnp.where(lane == 6 * N_HEADS, 1.0, _dot(parts, g6_ref[...])).astype(BF)
        for h in range(N_HEADS):
            pair = slice((h // 2) * LANES, (h // 2 + 1) * LANES)
            for out_ref, block, place_ref in ((qf_ref, qkvv[:, :ATTN_W], pq_ref), (kl_ref, qkvv[:, ATTN_W:2 * ATTN_W], pk_ref),
                                              (vl_ref, qkvv[:, 2 * ATTN_W:], pv_ref)):
                out_ref[h] = _dot(jnp.concatenate([block[:, pair], side], axis=1), place_ref[h]).astype(BF)

    head_spec = pl.BlockSpec((N_HEADS, tp, LANES), lambda i: (0, i, 0))
    whole = lambda a: pl.BlockSpec(a.shape, lambda i: (0,) * a.ndim)
    return pl.pallas_call(
        body, name="attn_prep", grid=(T // tp,),
        in_specs=[_row_spec(tp, 3 * ATTN_W), _row_spec(tp, LANES), _vec_spec(LANES)]
        + [whole(m) for m in (head_sum, gather6, place_q, place_k, place_v)],
        out_specs=[head_spec] * 3 + [pl.BlockSpec((1, N_HEADS, LANES), lambda i: (i, 0, 0))],
        out_shape=[jax.ShapeDtypeStruct((N_HEADS, T, LANES), BF)] * 3 + [jax.ShapeDtypeStruct((T // tp, N_HEADS, LANES), F32)],
        scratch_shapes=[pltpu.VMEM((1, LANES), F32), pltpu.VMEM((1, LANES), F32)], compiler_params=_params("arbitrary"),
    )(qkv, fl, b_forget, head_sum, gather6, place_q, place_k, place_v)


def _attn_placements():
    head_sum = np.zeros((ATTN_W, LANES), np.float32)
    head_sum[np.arange(ATTN_W), np.arange(ATTN_W) // HEAD_DIM] = 1.0
    gather6 = np.zeros((6 * LANES, LANES), np.float32)
    for j in range(6):
        gather6[j * LANES + np.arange(N_HEADS), j * N_HEADS + np.arange(N_HEADS)] = 1.0
    place = np.zeros((3, N_HEADS, 2 * LANES, LANES), np.float32)
    one = LANES + 6 * N_HEADS
    d = np.arange(HEAD_DIM)
    for h in range(N_HEADS):
        side = lambda j: LANES + j * N_HEADS + h
        place[0, h, (h % 2) * HEAD_DIM + d, d] = Q_SCALE
        place[1:, h, (h % 2) * HEAD_DIM + d, d] = 1.0
        for j in range(3):
            place[0, h, side(j), HEAD_DIM + j] = 1.0
            place[0, h, one, HEAD_DIM + 3 + j] = 1.0
            place[0, h, side(3 + j), HEAD_DIM + 6 + j] = 1.0
            place[1, h, one, HEAD_DIM + j] = 1.0
            place[1, h, side(j), HEAD_DIM + 3 + j] = -1.0
            place[1, h, one, HEAD_DIM + 6 + j] = 1.0
            place[2, h, one, HEAD_DIM + j] = 1.0
    return head_sum, gather6, place[0], place[1], place[2]


def _attn_ranges(stats):
    qn, kn, cmax, cmin, k_seen = (stats[:, k, :N_HEADS].T for k in range(5))
    n = qn.shape[1]
    bounded = (2.0 * _logit_bound(qn, k_seen) <= BOUNDED_GAP).reshape(N_HEADS // 2, 2, n).all(axis=1)
    reach = NORM_SLACK * qn * (jnp.max(kn, axis=1, keepdims=True) + kn) + cmax
    i = jnp.arange(n)[None, :, None]
    j = jnp.arange(n)[None, None, :]
    need = ((reach[:, :, None] - cmin[:, None, :] >= SKIP_BELOW) | (i == j)) & (j <= i)
    first = jnp.min(jnp.where(need, j, n), axis=2).reshape(N_HEADS // 2, 2, n).min(axis=1)
    last = jnp.max(jnp.where(need, i, -1), axis=1).reshape(N_HEADS // 2, 2, n).max(axis=1)
    return first.reshape(-1).astype(F32), last.reshape(-1).astype(F32), bounded.reshape(-1).astype(F32)


def _pair_block(t):
    return pl.BlockSpec((2, t, LANES), lambda p, i, *_: (p, i, 0))


def _pair_full(T):
    return pl.BlockSpec((2, T, LANES), lambda p, i, *_: (p, 0, 0))


def _packed_block(t):
    return pl.BlockSpec((t, LANES), lambda p, i, *_: (i, p))


def _causal(t, keys_in_rows=False):
    r = lax.broadcasted_iota(jnp.int32, (t, t), 0)
    c = lax.broadcasted_iota(jnp.int32, (t, t), 1)
    return (r <= c) if keys_in_rows else (c <= r)


def _tile_rows(j, t):
    return pl.ds(pl.multiple_of(j * t, t), t)


def _attn_call(body, name, tile_scalars, operands, in_specs, out_specs, out_shape, scratch_shapes, n_tiles):
    return pl.pallas_call(
        body, name=name,
        grid_spec=pltpu.PrefetchScalarGridSpec(
            num_scalar_prefetch=len(tile_scalars), grid=(N_HEADS // 2, n_tiles), in_specs=in_specs, out_specs=out_specs,
            scratch_shapes=scratch_shapes),
        out_shape=out_shape, compiler_params=_params("arbitrary", "arbitrary"),
    )(*tile_scalars, *operands)


def _attn_fwd(qf, kl, vl, first, bounded, shards, *, tq=ATTN_TILE):
    T = qf.shape[1]
    tq = min(tq, T)
    n = T // tq
    n_steps = (N_HEADS // 2) * n
    k = len(shards)

    def body(first_ref, bounded_ref, qf_ref, kl_ref, vl_ref, *refs):
        w_refs, (o_ref, of_ref, ql_ref), g_refs = refs[:k], refs[k:k + 3], refs[k + 3:2 * k + 3]
        m_ref, acc_ref, send_sems, recv_sems = refs[2 * k + 3:]
        i = pl.program_id(1)
        tile = pl.program_id(0) * n + i
        gather_start, gather_forward, gather_finish = _gather_phases(w_refs, g_refs, send_sems, recv_sems)
        pl.when(tile == 0)(gather_start)
        pl.when(tile == (3 * n_steps) // 4)(gather_forward)
        start = first_ref[tile].astype(jnp.int32)
        is_bounded = bounded_ref[tile] > 0.5
        acc_ref[...] = jnp.zeros_like(acc_ref)
        diagonal = _tile_rows(i, tq)
        causal = _causal(tq)

        def logits(hh, rows):
            return _dot_nt(qf_ref[hh], kl_ref[hh, rows, :])

        @pl.when(is_bounded)
        def _():
            m_ref[...] = jnp.zeros_like(m_ref)

            def update(hh, s, rows):
                acc_ref[hh] += _dot(jnp.exp(s).astype(BF), vl_ref[hh, rows, :])

            def step(j, carry):
                for hh in range(2):
                    update(hh, logits(hh, _tile_rows(j, tq)), _tile_rows(j, tq))
                return carry

            lax.fori_loop(start, i, step, 0)
            for hh in range(2):
                update(hh, jnp.where(causal, logits(hh, diagonal), NEG), diagonal)

        @pl.when(jnp.logical_not(is_bounded))
        def _():
            m_ref[...] = jnp.full_like(m_ref, NEG)

            def update(hh, s, rows):
                m_old = m_ref[hh]
                m_new = jnp.maximum(m_old, jnp.max(s, axis=1, keepdims=True))
                p = jnp.exp(s - m_new)
                acc_ref[hh] = jnp.exp(m_old - m_new) * acc_ref[hh] + _dot(p.astype(BF), vl_ref[hh, rows, :])
                m_ref[hh] = m_new

            def step(j, carry):
                for hh in range(2):
                    update(hh, logits(hh, _tile_rows(j, tq)), _tile_rows(j, tq))
                return carry

            lax.fori_loop(start, i, step, 0)
            for hh in range(2):
                update(hh, jnp.where(causal, logits(hh, diagonal), NEG), diagonal)

        lane = lax.broadcasted_iota(jnp.int32, (tq, LANES), 1)
        outs = []
        for hh in range(2):
            q = qf_ref[hh].astype(F32)
            acc = acc_ref[hh]
            l = acc[:, HEAD_DIM:HEAD_DIM + 1]
            outs.append(acc[:, :HEAD_DIM] / l)
            at = HEAD_DIM + 6
            neg_bound = (q[:, at:at + 1] + q[:, at + 1:at + 2]) + q[:, at + 2:at + 3]
            ql_ref[hh] = _with_lanes(q, lane, at, _split3(neg_bound - (m_ref[hh] + jnp.log(l)))).astype(BF)
        o = jnp.concatenate(outs, axis=1)
        o_ref[...] = o.astype(BF)
        of_ref[...] = o
        pl.when(tile == n_steps - 1)(gather_finish)

    outs = _attn_call(
        body, "attn_fwd", (first, bounded), (qf, kl, vl, *shards),
        [_pair_block(tq), _pair_full(T), _pair_full(T)] + [HBM] * k,
        [_packed_block(tq), _packed_block(tq), _pair_block(tq)] + [HBM] * k,
        [jax.ShapeDtypeStruct((T, ATTN_W), BF), jax.ShapeDtypeStruct((T, ATTN_W), F32),
         jax.ShapeDtypeStruct((N_HEADS, T, LANES), BF)] + _gathered_shapes(shards),
        [pltpu.VMEM((2, tq, 1), F32), pltpu.VMEM((2, tq, LANES), F32)] + _gather_semaphores(k), n)
    return outs[0], outs[1], outs[2], outs[3:]


def _attn_bwd_prep(dya, of, *, tr=256):
    T = dya.shape[0]
    tr = min(tr, T)

    def body(d_ref, o_ref, do_ref):
        lane = lax.broadcasted_iota(jnp.int32, (tr, HEAD_DIM), 1)
        dv, ov = d_ref[...], o_ref[...]
        for h in range(N_HEADS):
            d = dv[:, h * HEAD_DIM:(h + 1) * HEAD_DIM]
            delta = jnp.sum(d * ov[:, h * HEAD_DIM:(h + 1) * HEAD_DIM], axis=1, keepdims=True)
            ext = _with_lanes(jnp.zeros((tr, HEAD_DIM), F32), lane, 0, _split3(-delta))
            do_ref[h] = jnp.concatenate([d, ext], axis=1).astype(BF)

    return pl.pallas_call(
        body, name="attn_bwd_prep", grid=(T // tr,),
        in_specs=[_row_spec(tr, ATTN_W), _row_spec(tr, ATTN_W)],
        out_specs=pl.BlockSpec((N_HEADS, tr, LANES), lambda i: (0, i, 0)),
        out_shape=jax.ShapeDtypeStruct((N_HEADS, T, LANES), BF), compiler_params=_params("parallel"),
    )(dya, of)


def _attn_bwd(kl, vl, ql, do, last, chip_sums, *, tk=ATTN_TILE):
    T = ql.shape[1]
    tk = min(tk, T)
    n = T // tk
    n_steps = (N_HEADS // 2) * n
    m = len(chip_sums)

    def body(last_ref, kl_ref, vl_ref, ql_ref, do_ref, *refs):
        b_refs, (dq_ref, dk_ref, dv_ref, extq_ref, extk_ref), r_refs = refs[:m], refs[m:m + 5], refs[m + 5:2 * m + 5]
        dq_acc, dk_acc, dv_acc, send_sems, recv_sems = refs[2 * m + 5:]
        j = pl.program_id(1)
        tile = pl.program_id(0) * n + j
        scatter_start, scatter_finish = _scatter_phases(b_refs, r_refs, send_sems, recv_sems)
        pl.when(tile == 0)(scatter_start)

        @pl.when(j == 0)
        def _():
            dq_acc[...] = jnp.zeros_like(dq_acc)

        dk_acc[...] = jnp.zeros_like(dk_acc)
        dv_acc[...] = jnp.zeros_like(dv_acc)

        def block(hh, rows, mask):
            qi, di, k = ql_ref[hh, rows, :], do_ref[hh, rows, :], kl_ref[hh]
            p_t = jnp.exp(_dot_nt(k, qi))
            if mask is not None:
                p_t = jnp.where(mask, p_t, 0.0)
            ds_t = (p_t * _dot_nt(vl_ref[hh], di)).astype(BF)
            dk_acc[hh] += _dot(ds_t, qi)
            dv_acc[hh] += _dot(p_t.astype(BF), di)
            dq_acc[hh, rows, :] += _dot_tn(ds_t, k)

        causal_t = _causal(tk, keys_in_rows=True)
        for hh in range(2):
            block(hh, _tile_rows(j, tk), causal_t)

        def step(i, carry):
            for hh in range(2):
                block(hh, _tile_rows(i, tk), None)
            return carry

        lax.fori_loop(j + 1, last_ref[pl.program_id(0) * n + j].astype(jnp.int32) + 1, step, 0)
        dk_ref[...] = jnp.concatenate([dk_acc[hh][:, :HEAD_DIM] for hh in range(2)], axis=1).astype(BF)
        dv_ref[...] = jnp.concatenate([dv_acc[hh][:, :HEAD_DIM] for hh in range(2)], axis=1).astype(BF)
        extk_ref[...] = jnp.concatenate([dk_acc[hh][:, HEAD_DIM:] for hh in range(2)], axis=1)

        @pl.when(j == n - 1)
        def _():
            dq_ref[...] = jnp.concatenate([dq_acc[hh][:, :HEAD_DIM] * Q_SCALE for hh in range(2)], axis=1).astype(BF)
            extq_ref[...] = jnp.concatenate([dq_acc[hh][:, HEAD_DIM:] for hh in range(2)], axis=1)

        pl.when(tile == n_steps - 1)(scatter_finish)

    whole = pl.BlockSpec((T, LANES), lambda p, j, *_: (0, p))
    outs = pl.pallas_call(
        body, name="attn_bwd",
        grid_spec=pltpu.PrefetchScalarGridSpec(
            num_scalar_prefetch=1, grid=(N_HEADS // 2, n),
            in_specs=[_pair_block(tk), _pair_block(tk), _pair_full(T), _pair_full(T)] + [HBM] * m,
            out_specs=[whole, _packed_block(tk), _packed_block(tk), whole, _packed_block(tk)] + [HBM] * m,
            scratch_shapes=[pltpu.VMEM((2, T, LANES), F32), pltpu.VMEM((2, tk, LANES), F32), pltpu.VMEM((2, tk, LANES), F32)]
            + _scatter_semaphores(m)),
        out_shape=[jax.ShapeDtypeStruct((T, ATTN_W), BF)] * 3 + [jax.ShapeDtypeStruct((T, ATTN_W), F32)] * 2
        + _scattered_shapes(chip_sums),
        compiler_params=pltpu.CompilerParams(dimension_semantics=("arbitrary", "arbitrary"), vmem_limit_bytes=BIG_VMEM),
    )(last, kl, vl, ql, do, *chip_sums)
    return outs[:5], outs[5:]


def _forget_bwd(ext_q, ext_k, fl, b_forget, *, tp=256):
    T = fl.shape[0]
    tp = min(tp, T)
    n = T // tp

    def body(eq_ref, ek_ref, fl_ref, bf_ref, dfl_ref, dbf_ref, carry_ref):
        @pl.when(pl.program_id(0) == 0)
        def _():
            carry_ref[...] = jnp.zeros_like(carry_ref)
            dbf_ref[...] = jnp.zeros_like(dbf_ref)

        lane = lax.broadcasted_iota(jnp.int32, (tp, LANES), 1)
        eq, ek = eq_ref[...], ek_ref[...]
        cols = [eq[:, h * HEAD_DIM:h * HEAD_DIM + 1] - ek[:, h * HEAD_DIM + 3:h * HEAD_DIM + 4] for h in range(N_HEADS)]
        dcum = _with_lanes(jnp.zeros((tp, LANES), F32), lane, 0, cols)
        suffix = _scan_dot(_tri(tp, upper=True), dcum) + carry_ref[...]
        carry_ref[...] = suffix[0:1, :]
        x = fl_ref[...] + bf_ref[...]
        dfl = jnp.where(lane < N_HEADS, suffix / (1.0 + jnp.exp(x)), 0.0)
        dfl_ref[...] = dfl.astype(BF)
        dbf_ref[...] += jnp.sum(dfl, axis=0, keepdims=True)

    rev = lambda w: pl.BlockSpec((tp, w), lambda i: (n - 1 - i, 0))
    return pl.pallas_call(
        body, name="forget_bwd", grid=(n,),
        in_specs=[rev(ATTN_W), rev(ATTN_W), rev(LANES), _vec_spec(LANES)],
        out_specs=[rev(LANES), _vec_spec(LANES)],
        out_shape=[jax.ShapeDtypeStruct((T, LANES), BF), jax.ShapeDtypeStruct((1, LANES), F32)],
        scratch_shapes=[pltpu.VMEM((1, LANES), F32)], compiler_params=_params("arbitrary"),
    )(ext_q, ext_k, fl, b_forget)


def _adamw(w, g, m, v, *, name, tr=256):
    _, rows, cols = w.shape
    tr = tr if rows % tr == 0 else rows

    def body(w_ref, g_ref, m_ref, v_ref, go_ref, d_ref, nm_ref, nv_ref):
        gv = g_ref[...]
        go_ref[...] = gv
        nm = ADAM_B1 * m_ref[...] + (1.0 - ADAM_B1) * gv
        nv = ADAM_B2 * v_ref[...] + (1.0 - ADAM_B2) * (gv * gv)
        m_hat = nm / (1.0 - ADAM_B1 ** ADAM_STEP)
        v_hat = nv / (1.0 - ADAM_B2 ** ADAM_STEP)
        d_ref[...] = -ADAM_LR * (m_hat / (jnp.sqrt(v_hat) + ADAM_EPS) + ADAM_WD * w_ref[...])
        nm_ref[...] = nm
        nv_ref[...] = nv

    spec = pl.BlockSpec((None, tr, cols), lambda i: (0, i, 0))
    return pl.pallas_call(
        body, name=name, grid=(rows // tr,), in_specs=[spec, pl.BlockSpec((tr, cols), lambda i: (i, 0)), spec, spec],
        out_specs=[spec] * 4, out_shape=[jax.ShapeDtypeStruct((1, rows, cols), F32)] * 4,
        compiler_params=_params("parallel"),
    )(w, g, m, v)


HBM = pl.BlockSpec(memory_space=pltpu.HBM)
BF16_ROWS = 16


def _place():
    x, y, c = lax.axis_index("x"), lax.axis_index("y"), lax.axis_index("c")
    others = [(1 - x, y), (x, 1 - y), (1 - x, 1 - y)]
    return x, y, c, others


def _chip(xy):
    return 2 * xy[0] + xy[1]


def _row_halves(c, rows):
    half = rows // 2
    assert half % BF16_ROWS == 0
    return (pl.ds(pl.multiple_of(c * half, BF16_ROWS), half), pl.ds(pl.multiple_of((1 - c) * half, BF16_ROWS), half))


def _remote(src, dst, send_sems, recv_sems, k, to):
    return pltpu.make_async_remote_copy(src_ref=src, dst_ref=dst, send_sem=send_sems.at[k], recv_sem=recv_sems.at[k],
                                        device_id=to, device_id_type=MESH)


def _gathered_shapes(shards):
    return [jax.ShapeDtypeStruct((N_CHIPS,) + s.shape, s.dtype) for s in shards]


def _gather_semaphores(n):
    return [pltpu.SemaphoreType.DMA((6 * n,)), pltpu.SemaphoreType.DMA((6 * n,))]


def _gather_phases(w_refs, g_refs, send_sems, recv_sems):
    n = len(w_refs)
    x, y, c, others = _place()
    sibling, me = (x, y, 1 - c), _chip((x, y))
    halves = [_row_halves(c, w.shape[0]) for w in w_refs]

    def sent(a, j, o):
        mine, _ = halves[a]
        return _remote(w_refs[a].at[mine, :], g_refs[a].at[me, mine, :], send_sems, recv_sems, 6 * a + j, (*o, c))

    def passed(a, j, o):
        landed = g_refs[a].at[_chip(o), halves[a][0], :]
        return _remote(landed, landed, send_sems, recv_sems, 6 * a + 3 + j, sibling)

    def start():
        for a in range(n):
            for j, o in enumerate(others):
                sent(a, j, o).start()

    def forward():
        for j, o in enumerate(others):
            for a in range(n):
                landed = g_refs[a].at[_chip(o), halves[a][0], :]
                _remote(landed, landed, send_sems, recv_sems, 6 * a + j, (*o, c)).wait_recv()
                passed(a, j, o).start()

    def finish():
        for j, o in enumerate(others):
            for a in range(n):
                landed = g_refs[a].at[_chip(o), halves[a][1], :]
                _remote(landed, landed, send_sems, recv_sems, 6 * a + 3 + j, sibling).wait_recv()
        for a in range(n):
            for j, o in enumerate(others):
                sent(a, j, o).wait_send()
                passed(a, j, o).wait_send()

    return start, forward, finish


def _exchange_halves(arrays, *, name):
    n = len(arrays)

    def body(*refs):
        for phase in _exchange_phases(refs[:n], refs[n:2 * n], *refs[2 * n:]):
            phase()

    return pl.pallas_call(
        body, name=name, in_specs=[HBM] * n, out_specs=[HBM] * n, out_shape=_exchanged_shapes(arrays),
        scratch_shapes=_exchange_semaphores(n),
    )(*arrays)


def _exchanged_shapes(arrays):
    return [jax.ShapeDtypeStruct(s.shape[:-2] + (s.shape[-2] // 2, s.shape[-1]), F32) for s in arrays]


def _exchange_semaphores(n):
    return [pltpu.SemaphoreType.DMA((n,)), pltpu.SemaphoreType.DMA((n,))]


def _exchange_phases(g_refs, r_refs, send_sems, recv_sems):
    x, y, c, _ = _place()

    def copy(a):
        _, theirs = _row_halves(c, g_refs[a].shape[-2])
        src = g_refs[a].at[:, theirs, :] if len(g_refs[a].shape) == 3 else g_refs[a].at[theirs, :]
        return _remote(src, r_refs[a], send_sems, recv_sems, a, (x, y, 1 - c))

    def start():
        for a in range(len(g_refs)):
            copy(a).start()

    def finish():
        for a in range(len(g_refs)):
            copy(a).wait()

    return start, finish


def _scatter_to_owners(chip_sums):
    n = len(chip_sums)

    def body(*refs):
        for phase in _scatter_phases(refs[:n], refs[n:2 * n], *refs[2 * n:]):
            phase()

    return pl.pallas_call(
        body, name="scatter_to_owners", in_specs=[HBM] * n, out_specs=[HBM] * n,
        out_shape=_scattered_shapes(chip_sums), scratch_shapes=_scatter_semaphores(n),
    )(*chip_sums)


def _scattered_shapes(chip_sums):
    return [jax.ShapeDtypeStruct(b.shape if b.ndim == 3 else (N_CHIPS,) + b.shape, b.dtype) for b in chip_sums]


def _scatter_semaphores(n):
    return [pltpu.SemaphoreType.DMA((3 * n,)), pltpu.SemaphoreType.DMA((3 * n,))]


def _scatter_phases(b_refs, r_refs, send_sems, recv_sems):
    n = len(b_refs)
    x, y, c, others = _place()
    me = _chip((x, y))

    def sent(a, j, o):
        src = b_refs[a].at[_chip(o)] if len(b_refs[a].shape) == 3 else b_refs[a]
        return _remote(src, r_refs[a].at[me], send_sems, recv_sems, 3 * a + j, (*o, c))

    def start():
        for a in range(n):
            for j, o in enumerate(others):
                sent(a, j, o).start()

    def finish():
        for a in range(n):
            for j, o in enumerate(others):
                landed = r_refs[a].at[_chip(o)]
                _remote(landed, landed, send_sems, recv_sems, 3 * a + j, (*o, c)).wait_recv()
        for a in range(n):
            for j, o in enumerate(others):
                sent(a, j, o).wait_send()

    return start, finish


def _join_halves(totals):
    n = len(totals)

    def body(*refs):
        in_refs, out_refs, (send_sems, recv_sems) = refs[:n], refs[n:2 * n], refs[2 * n:]
        x, y, c, _ = _place()
        copies = []
        for a in range(n):
            mine, _ = _row_halves(c, in_refs[a].shape[0])
            copies.append(_remote(in_refs[a].at[mine, :], out_refs[a].at[mine, :], send_sems, recv_sems, a, (x, y, 1 - c)))
            copies[-1].start()
        for cp in copies:
            cp.wait()

    return pl.pallas_call(
        body, name="join_halves", in_specs=[HBM] * n, out_specs=[HBM] * n,
        out_shape=[jax.ShapeDtypeStruct(t.shape, F32) for t in totals], input_output_aliases={a: a for a in range(n)},
        scratch_shapes=[pltpu.SemaphoreType.DMA((n,)), pltpu.SemaphoreType.DMA((n,))],
    )(*totals)


ADD_ROWS = 128


def _add_sibling(g, r, place, *, name):
    lead, (half, cols) = g.shape[:-2], r.shape[-2:]
    tr = min(ADD_ROWS, half)
    nb = half // tr
    zeros = (0,) * len(lead)

    def body(place_ref, g_ref, r_ref, o_ref, ob_ref):
        s = g_ref[...] + r_ref[...]
        o_ref[...] = s
        ob_ref[...] = s.astype(BF)

    spec = pl.BlockSpec(lead + (tr, cols), lambda i, p: zeros + (i, 0))
    return pl.pallas_call(
        body, name=name,
        grid_spec=pltpu.PrefetchScalarGridSpec(
            num_scalar_prefetch=1, grid=(nb,),
            in_specs=[pl.BlockSpec(lead + (tr, cols), lambda i, p: zeros + (p[1] * nb + i, 0)), spec], out_specs=[spec, spec]),
        out_shape=[jax.ShapeDtypeStruct(r.shape, F32), jax.ShapeDtypeStruct(r.shape, BF)],
        compiler_params=_params("parallel"),
    )(place, g, r)


def _add_chips(own, received, place, *, name, own_slots):
    half, cols = received.shape[-2:]
    tr = min(ADD_ROWS, half)
    nb = half // tr

    def written(k, p):
        return jnp.where(p[0] == k, (k + 1) % N_CHIPS, k)

    def body(place_ref, own_ref, *refs):
        o_ref = refs[N_CHIPS]
        mine = own_ref[0] if own_slots else own_ref[...]
        if own_slots:
            acc = mine
            for k in range(N_CHIPS):
                acc = acc + jnp.where(place_ref[0] == k, 0.0, refs[k][0].astype(F32))
        else:
            terms = [jnp.where(place_ref[0] == k, mine, refs[k][0]) for k in range(N_CHIPS)]
            acc = ((terms[0] + terms[1]) + terms[2]) + terms[3]
        o_ref[...] = acc

    own_spec = (pl.BlockSpec((1, tr, cols), lambda i, p: (p[0], i, 0)) if own_slots
                else pl.BlockSpec((tr, cols), lambda i, p: (i, 0)))
    return pl.pallas_call(
        body, name=name,
        grid_spec=pltpu.PrefetchScalarGridSpec(
            num_scalar_prefetch=1, grid=(nb,),
            in_specs=[own_spec] + [pl.BlockSpec((1, tr, cols), functools.partial(lambda i, p, k: (written(k, p), i, 0), k=k))
                                   for k in range(N_CHIPS)],
            out_specs=pl.BlockSpec((tr, cols), lambda i, p: (p[1] * nb + i, 0))),
        out_shape=jax.ShapeDtypeStruct((2 * half, cols), F32), compiler_params=_params("parallel"),
    )(place, own, *([received] * N_CHIPS))


SHARDED = (("w_in", (D_MODEL, 4616), 1), ("w_branch_sgu", (SGU_W, D_MODEL), 1), ("w_branch_attn", (ATTN_W, D_MODEL), 1),
           ("w_out", (D_MODEL, D_MODEL), 0), ("w_up", (D_MODEL, D_FF), 1), ("w_down", (D_FF, D_MODEL), 0))
SMALL = (("g_mix_pre", (1, D_MODEL)), ("b_forget", (1, N_HEADS)), ("g_sgu", (1, SGU_W)), ("b_sgu", (1, SGU_W)),
         ("w_spatial", (N_GROUPS * CHUNK, CHUNK)), ("b_spatial", (N_GROUPS, CHUNK)), ("g_mix_post", (1, D_MODEL)),
         ("g_ffn_pre", (1, D_MODEL)), ("g_ffn_post", (1, D_MODEL)))
SMALL_ALIGN = 2 * ADD_ROWS


def _shard_shape(shape, axis):
    return tuple(s // N_CHIPS if a == axis else s for a, s in enumerate(shape))


def _slots_to_full(slots, axis):
    return slots.reshape(-1, slots.shape[2]) if axis == 0 else slots.transpose(1, 0, 2).reshape(slots.shape[1], -1)


def _full_to_slots(full, axis):
    if axis == 0:
        return full.reshape(N_CHIPS, -1, full.shape[1])
    return full.reshape(full.shape[0], N_CHIPS, -1).transpose(1, 0, 2)


def _small_rows(shape):
    return -(-(shape[0] * shape[1]) // (8 * LANES)) * 8


def _pack_small(values):
    parts = []
    for name, shape in SMALL:
        flat = values[name].reshape(-1)
        n = _small_rows(shape)
        parts.append(jnp.pad(flat, (0, n * LANES - flat.shape[0])).reshape(n, LANES))
    rows = sum(p.shape[0] for p in parts)
    pad = -(-rows // SMALL_ALIGN) * SMALL_ALIGN - rows
    return jnp.concatenate(parts + [jnp.zeros((pad, LANES), F32)], axis=0)


def _unpack_small(packed):
    out, row = {}, 0
    for name, shape in SMALL:
        n = _small_rows(shape)
        out[name] = packed[row:row + n].reshape(-1)[:shape[0] * shape[1]].reshape(shape)
        row += n
    return out


IN_Z, IN_Q, IN_K, IN_V, IN_F, IN_G, IN_END = 0, 1024, 1536, 2048, 2560, 2568, 4616


LATE_WEIGHTS = ("w_branch_sgu", "w_branch_attn", "w_out", "w_up", "w_down")
EARLY_GRADS = LATE_WEIGHTS


def _assemble(name, shard, gathered, chip):
    axis = {n: a for n, _, a in SHARDED}[name]
    slot = jnp.arange(N_CHIPS)[:, None, None]
    return _slots_to_full(jnp.where(slot == chip, shard[None], gathered), axis)


def _local_step(x, target, shards, small, place):
    b_forget = jnp.pad(small["b_forget"], ((0, 0), (0, LANES - N_HEADS)))
    causal = jnp.tril(jnp.ones((CHUNK, CHUNK), bool))
    ws = jnp.where(causal[None], small["w_spatial"].reshape(N_GROUPS, CHUNK, CHUNK), 0.0).astype(BF)
    ws_t = ws.transpose(0, 2, 1)
    bias_plane = jnp.repeat(small["b_spatial"].T, HEAD_DIM, axis=1)

    xn, (w_in_slots,) = _rms_fwd(x, small["g_mix_pre"], [shards["w_in"]])
    w_in = _assemble("w_in", shards["w_in"], w_in_slots, place[0])
    w_z, w_qkv, w_g = w_in[:, IN_Z:IN_Q], w_in[:, IN_Q:IN_F], w_in[:, IN_G:IN_END]
    w_q, w_k, w_v = w_in[:, IN_Q:IN_K], w_in[:, IN_K:IN_V], w_in[:, IN_V:IN_F]
    w_f = jnp.pad(w_in[:, IN_F:IN_G], ((0, 0), (0, LANES - N_HEADS)))
    z = _matmul([(xn, w_z)], nt=False, out_dtypes=[F32], name="proj_z")
    qkv = _matmul([(xn, w_qkv)], nt=False, out_dtypes=[BF], name="proj_qkv")
    gl = _matmul([(xn, w_g)], nt=False, out_dtypes=[BF], name="proj_gate")
    fl = _matmul([(xn, w_f)], nt=False, out_dtypes=[F32], name="proj_forget")
    ysgu = _sgu_fwd(z, small["g_sgu"], small["b_sgu"], ws, bias_plane)
    qf, kl, vl, tile_stats = _attn_prep(qkv, fl, b_forget)
    first_key_tile, last_query_tile, bounded = _attn_ranges(tile_stats)
    yattn, yattn_f, ql, gathered = _attn_fwd(qf, kl, vl, first_key_tile, bounded, [shards[name] for name in LATE_WEIGHTS])
    w = {name: _assemble(name, shards[name], got, place[0]) for name, got in zip(LATE_WEIGHTS, gathered, strict=True)}
    a, b, merged = _branch_merge(ysgu, yattn, w["w_branch_sgu"], w["w_branch_attn"], gl)
    o = _matmul([(merged, w["w_out"])], nt=False, out_dtypes=[F32], name="proj_out")
    h1, xn2 = _mixer_out_fwd(o, x, small["g_mix_post"], small["g_ffn_pre"])

    def relu2(acc):
        r = jnp.maximum(acc, 0.0)
        return (r * r,)

    hid = _matmul([(xn2, w["w_up"])], nt=False, out_dtypes=[BF], name="ffn_up", epilogue=relu2, tm=FFN_ROWS)
    dy, ddn, sq, dg_ffn_post = _matmul_rows(
        hid, w["w_down"], nt=False, rows=[h1, target], vecs=[small["g_ffn_post"]], row_outs=[F32, BF], n_sums=2,
        epilogue=_loss_head, name="ffn_down_loss")

    dup = _matmul([(ddn, w["w_down"])], nt=True, out_dtypes=[BF], name="ffn_down_bwd", tm=FFN_ROWS,
                  epilogue=lambda acc, h: (acc * (2.0 * jnp.sqrt(h.astype(F32))),), extras=[hid])
    dw_down = _matmul_tn(hid, ddn, name="dw_down")
    dh1, do, dg_ffn_pre, dg_mix_post = _matmul_rows(
        dup, w["w_up"], nt=True, rows=[h1, dy, o], vecs=[small["g_ffn_pre"], small["g_mix_post"]], row_outs=[F32, BF],
        n_sums=2, epilogue=_mixer_out_bwd, name="ffn_up_bwd_norms")
    dw_up = _matmul_tn(xn2, dup, name="dw_up", slots=True)

    def gate_bwd(dm, a_t, b_t, gla, glb):
        ga, gb = jax.nn.sigmoid(gla.astype(F32)), jax.nn.sigmoid(glb.astype(F32))
        return dm * ga, dm * gb, dm * a_t.astype(F32) * (ga * (1.0 - ga)), dm * b_t.astype(F32) * (gb * (1.0 - gb))

    da, db, dgla, dglb = _matmul([(do, w["w_out"])], nt=True, out_dtypes=[BF] * 4, name="proj_out_bwd",
                                 epilogue=gate_bwd, extras=[a, b, (gl, 0), (gl, D_MODEL)])
    dw_out = _matmul_tn(merged, do, name="dw_out")
    dysgu = _matmul([(da, w["w_branch_sgu"])], nt=True, out_dtypes=[F32], name="branch_sgu_bwd")
    dyattn = _matmul([(db, w["w_branch_attn"])], nt=True, out_dtypes=[F32], name="branch_attn_bwd")
    dw_bs = _matmul_tn(ysgu, da, name="dw_branch_sgu")
    dw_ba = _matmul_tn(yattn, db, name="dw_branch_attn")
    early = {"w_branch_sgu": _full_to_slots(dw_bs, 1), "w_branch_attn": _full_to_slots(dw_ba, 1),
             "w_out": _full_to_slots(dw_out, 0), "w_up": dw_up, "w_down": _full_to_slots(dw_down, 0)}
    (dz, dws, dbs, dg_sgu, db_sgu), early_theirs = _sgu_bwd(
        dysgu, z, small["g_sgu"], small["b_sgu"], ws, ws_t, bias_plane, [early[name] for name in EARLY_GRADS])
    early_sums = {name: _add_sibling(early[name], theirs, place, name="add_sibling_" + name)
                  for name, theirs in zip(EARLY_GRADS, early_theirs, strict=True)}
    dout = _attn_bwd_prep(dyattn, yattn_f)
    (dq, dk, dv, ext_q, ext_k), early_received = _attn_bwd(
        kl, vl, ql, dout, last_query_tile, [early_sums[name][1] for name in EARLY_GRADS])
    dfl, dbf = _forget_bwd(ext_q, ext_k, fl, b_forget)
    dw_in = _full_to_slots(jnp.concatenate(
        [_matmul_tn(xn, dz, name="dw_in_z"), _matmul_tn(xn, dq, name="dw_in_q"), _matmul_tn(xn, dk, name="dw_in_k"),
         _matmul_tn(xn, dv, name="dw_in_v"), _matmul_tn(xn, dfl, name="dw_in_f")[:, :N_HEADS],
         _matmul_tn(xn, dgla, name="dw_in_ga"), _matmul_tn(xn, dglb, name="dw_in_gb")], axis=1), 1)
    (dw_in_theirs,) = _exchange_halves([dw_in], name="exchange_halves_w_in")
    dw_in_sum = _add_sibling(dw_in, dw_in_theirs, place, name="add_sibling_w_in")
    (dxn,), (dw_in_received,) = _matmul(
        [(dz, w_z), (dq, w_q), (dk, w_k), (dv, w_v), (dgla, w_g[:, :D_MODEL]), (dglb, w_g[:, D_MODEL:]), (dfl, w_f)],
        nt=True, out_dtypes=[F32], name="proj_in_bwd", scatter=[dw_in_sum[1]])
    dx, dg_mix_pre = _input_norm_bwd(x, dxn, dh1, small["g_mix_pre"])

    reduced = {name: (early_sums[name][0], got) for name, got in zip(EARLY_GRADS, early_received, strict=True)}
    reduced["w_in"] = (dw_in_sum[0], dw_in_received)
    small_grads = {"g_mix_pre": dg_mix_pre, "b_forget": dbf[:, :N_HEADS], "g_sgu": dg_sgu, "b_sgu": db_sgu,
                   "w_spatial": dws.reshape(N_GROUPS * CHUNK, CHUNK), "b_spatial": dbs[:, :N_GROUPS].T,
                   "g_mix_post": dg_mix_post, "g_ffn_pre": dg_ffn_pre, "g_ffn_post": dg_ffn_post}
    return sq, dx, reduced, small_grads


NAMES = ("g_mix_pre", "w_in", "b_forget", "g_sgu", "b_sgu", "w_spatial", "b_spatial", "w_branch_sgu", "w_branch_attn",
         "w_out", "g_mix_post", "g_ffn_pre", "w_up", "w_down", "g_ffn_post")


def kernel(x, g_mix_pre, w_in, b_forget, g_sgu, b_sgu, w_spatial, b_spatial, w_branch_sgu, w_branch_attn, w_out, g_mix_post, g_ffn_pre, w_up, w_down, g_ffn_post, loss_target, m_g_mix_pre, m_w_in, m_b_forget, m_g_sgu, m_b_sgu, m_w_spatial, m_b_spatial, m_w_branch_sgu, m_w_branch_attn, m_w_out, m_g_mix_post, m_g_ffn_pre, m_w_up, m_w_down, m_g_ffn_post, v_g_mix_pre, v_w_in, v_b_forget, v_g_sgu, v_b_sgu, v_w_spatial, v_b_spatial, v_w_branch_sgu, v_w_branch_attn, v_w_out, v_g_mix_post, v_g_ffn_pre, v_w_up, v_w_down, v_g_ffn_post):
    weights = dict(zip(NAMES, (g_mix_pre, w_in, b_forget, g_sgu, b_sgu, w_spatial, b_spatial, w_branch_sgu, w_branch_attn,
                               w_out, g_mix_post, g_ffn_pre, w_up, w_down, g_ffn_post), strict=True))
    first = dict(zip(NAMES, (m_g_mix_pre, m_w_in, m_b_forget, m_g_sgu, m_b_sgu, m_w_spatial, m_b_spatial, m_w_branch_sgu,
                             m_w_branch_attn, m_w_out, m_g_mix_post, m_g_ffn_pre, m_w_up, m_w_down, m_g_ffn_post), strict=True))
    second = dict(zip(NAMES, (v_g_mix_pre, v_w_in, v_b_forget, v_g_sgu, v_b_sgu, v_w_spatial, v_b_spatial, v_w_branch_sgu,
                              v_w_branch_attn, v_w_out, v_g_mix_post, v_g_ffn_pre, v_w_up, v_w_down, v_g_ffn_post), strict=True))
    shard_shapes = {name: _shard_shape(shape, axis) for name, shape, axis in SHARDED}
    small_shapes = dict(SMALL)
    view = lambda name, a: a.reshape(shard_shapes.get(name) or small_shapes[name])

    place = jnp.stack([2 * lax.axis_index("x") + lax.axis_index("y"), lax.axis_index("c")]).astype(jnp.int32)

    shards = {name: view(name, weights[name]).astype(BF) for name, _, _ in SHARDED}
    small = {name: view(name, weights[name]) for name, _ in SMALL}
    sq, dx, reduced, small_grads = _local_step(x[0], loss_target[0], shards, small, place)
    loss = lax.psum(0.5 * jnp.sum(sq) / D_MODEL, ("x", "y", "c"))

    small_mine = _pack_small(small_grads)
    (small_theirs,) = _exchange_halves([small_mine], name="exchange_halves_small")
    small_sum, _ = _add_sibling(small_mine, small_theirs, place, name="add_sibling_small")
    (small_received,) = _scatter_to_owners([small_sum])
    totals = {name: _add_chips(s, r, place, name="add_chips_" + name, own_slots=True) for name, (s, r) in reduced.items()}
    small_total = _add_chips(small_sum, small_received, place, name="add_chips_small", own_slots=False)
    joined = _join_halves([totals[name] for name, _, _ in SHARDED] + [small_total])
    grad = {**{name: g for (name, _, _), g in zip(SHARDED, joined[:-1], strict=True)}, **_unpack_small(joined[-1])}

    grad_out, delta, new_m, new_v = {}, {}, {}, {}
    for name in NAMES:
        rows, cols = grad[name].shape
        as_given = lambda a: a.reshape(1, rows, cols)
        grad_out[name], delta[name], new_m[name], new_v[name] = _adamw(
            as_given(weights[name]), grad[name], as_given(first[name]), as_given(second[name]), name="adamw_" + name)

    like = lambda d: [d[name].reshape(weights[name].shape) for name in NAMES]
    return (loss, dx[None], *like(grad_out), *like(delta), *like(new_m), *like(new_v))
```

```python
import functools

import jax
import jax.numpy as jnp
import numpy as np
from jax import lax
from jax.experimental import pallas as pl
from jax.experimental.pallas import tpu as pltpu

F32 = jnp.float32
BF = jnp.bfloat16
MESH = pl.DeviceIdType.MESH

D_MODEL = 1024
N_HEADS = 8
HEAD_DIM = 64
ATTN_W = N_HEADS * HEAD_DIM
SGU_W = 512
N_GROUPS = 8
CHUNK = 128
D_FF = 4096
EPS = 1e-6
Q_SCALE = HEAD_DIM ** -0.5
N_CHIPS = 4
LANES = 128

ADAM_LR = 0.001
ADAM_B1 = 0.9
ADAM_B2 = 0.999
ADAM_EPS = 1e-08
ADAM_WD = 0.01
ADAM_STEP = 10

VMEM_LIMIT = 48 * 1024 * 1024
BIG_VMEM = 58 * 1024 * 1024
NEG = -1e30

LANE_ROWSUM = HEAD_DIM
LANE_COLSUM = HEAD_DIM + 3


def _params(*sem):
    return pltpu.CompilerParams(dimension_semantics=sem, vmem_limit_bytes=VMEM_LIMIT)


def _dot(a, b):
    return jnp.dot(a, b, preferred_element_type=F32)


def _dot_nt(a, b):
    return lax.dot_general(a, b, (((1,), (1,)), ((), ())), preferred_element_type=F32)


def _dot_tn(a, b):
    return lax.dot_general(a, b, (((0,), (0,)), ((), ())), preferred_element_type=F32)


def _split3(c):
    hi = c.astype(BF).astype(F32)
    r = c - hi
    mid = r.astype(BF).astype(F32)
    lo = (r - mid).astype(BF).astype(F32)
    return hi, mid, lo


def _gelu(x):
    k = 0.7978845608028654
    return 0.5 * x * (1.0 + jnp.tanh(k * (x + 0.044715 * (x * x * x))))


def _gelu_grad(x):
    k = 0.7978845608028654
    x2 = x * x
    t = jnp.tanh(k * (x + 0.044715 * (x2 * x)))
    return 0.5 * (1.0 + t) + 0.5 * x * (1.0 - t * t) * (k * (1.0 + 3.0 * 0.044715 * x2))


def _rms_bwd(a, g, dy):
    r = lax.rsqrt(jnp.mean(a * a, axis=-1, keepdims=True) + EPS)
    n = a * r
    dn = dy * g
    da = r * (dn - n * jnp.mean(dn * n, axis=-1, keepdims=True))
    return da, dy * n


MM_ROWS = 1024
MM_COLS = 512
FFN_ROWS = 2048


def _matmul(pairs, *, nt, out_dtypes, name, tm=MM_ROWS, tn=MM_COLS, epilogue=None, extras=(), scatter=()):
    n_pairs, n_extra, n_out, n_scatter = len(pairs), len(extras), len(out_dtypes), len(scatter)
    M = pairs[0][0].shape[0]
    N = pairs[0][1].shape[0] if nt else pairs[0][1].shape[1]
    tm, tn = min(tm, M), min(tn, N)
    assert M % tm == 0 and N % tn == 0
    grid = (M // tm, N // tn)

    def body(*refs):
        n_in = 2 * n_pairs + n_extra
        if n_scatter:
            step = pl.program_id(0) * grid[1] + pl.program_id(1)
            first = n_in + n_scatter + n_out
            scatter_start, scatter_finish = _scatter_phases(
                refs[n_in:n_in + n_scatter], refs[first:first + n_scatter], *refs[first + n_scatter:])
            pl.when(step == 0)(scatter_start)
        acc = None
        for p in range(n_pairs):
            a_ref, b_ref = refs[2 * p], refs[2 * p + 1]
            d = _dot_nt(a_ref[...], b_ref[...]) if nt else _dot(a_ref[...], b_ref[...])
            acc = d if acc is None else acc + d
        e_refs = refs[2 * n_pairs:n_in]
        o_refs = refs[n_in + n_scatter:n_in + n_scatter + n_out]
        outs = (acc,) if epilogue is None else epilogue(acc, *[e[...] for e in e_refs])
        for o_ref, o in zip(o_refs, outs, strict=True):
            o_ref[...] = o.astype(o_ref.dtype)
        if n_scatter:
            pl.when(step == grid[0] * grid[1] - 1)(scatter_finish)

    in_specs, args = [], []
    for a, b in pairs:
        K = a.shape[1]
        in_specs.append(pl.BlockSpec((tm, K), lambda i, j: (i, 0)))
        in_specs.append(pl.BlockSpec((tn, K), lambda i, j: (j, 0)) if nt else pl.BlockSpec((K, tn), lambda i, j: (0, j)))
        args += [a, b]
    for e in extras:
        e, col = e if isinstance(e, tuple) else (e, 0)
        in_specs.append(pl.BlockSpec((tm, tn), functools.partial(lambda i, j, off: (i, j + off), off=col // tn)))
        args.append(e)
    order = ("arbitrary", "arbitrary") if n_scatter else ("parallel", "parallel")
    outs = pl.pallas_call(
        body, name=name, grid=grid, in_specs=in_specs + [HBM] * n_scatter,
        out_specs=[pl.BlockSpec((tm, tn), lambda i, j: (i, j)) for _ in out_dtypes] + [HBM] * n_scatter,
        out_shape=[jax.ShapeDtypeStruct((M, N), dt) for dt in out_dtypes] + (_scattered_shapes(scatter) if n_scatter else []),
        scratch_shapes=_scatter_semaphores(n_scatter) if n_scatter else [],
        compiler_params=_params(*order),
    )(*args, *scatter)
    if n_scatter:
        return outs[:n_out], outs[n_out:]
    return outs if len(outs) > 1 else outs[0]


def _matmul_tn(a, b, *, name, tm=1024, tn=1024, tk=2048, slots=False):
    T, K1 = a.shape
    N = b.shape[1]
    tm, tn, tk = min(tm, K1), min(tn, N // N_CHIPS if slots else N), min(tk, T)
    assert K1 % tm == 0 and (N // N_CHIPS if slots else N) % tn == 0 and T % tk == 0
    per_slot = N // N_CHIPS // tn

    def body(a_ref, b_ref, o_ref):
        @pl.when(pl.program_id(2) == 0)
        def _():
            o_ref[...] = jnp.zeros_like(o_ref)

        o_ref[...] += _dot_tn(a_ref[...], b_ref[...])

    if slots:
        out_spec = pl.BlockSpec((None, tm, tn), lambda i, j, k: (j // per_slot, i, j % per_slot))
        out_shape = jax.ShapeDtypeStruct((N_CHIPS, K1, N // N_CHIPS), F32)
    else:
        out_spec = pl.BlockSpec((tm, tn), lambda i, j, k: (i, j))
        out_shape = jax.ShapeDtypeStruct((K1, N), F32)
    return pl.pallas_call(
        body, name=name, grid=(K1 // tm, N // tn, T // tk),
        in_specs=[pl.BlockSpec((tk, tm), lambda i, j, k: (k, i)), pl.BlockSpec((tk, tn), lambda i, j, k: (k, j))],
        out_specs=out_spec, out_shape=out_shape,
        compiler_params=_params("parallel", "parallel", "arbitrary"),
    )(a, b)


def _branch_merge(ysgu, yattn, w_bs, w_ba, gl, *, tm=MM_ROWS, tn=MM_COLS):
    T = ysgu.shape[0]
    tm = min(tm, T)
    nj = D_MODEL // tn

    def body(ys_ref, ya_ref, wbs_ref, wba_ref, gla_ref, glb_ref, a_ref, b_ref, m_ref):
        a = _dot(ys_ref[...], wbs_ref[...])
        b = _dot(ya_ref[...], wba_ref[...])
        a_ref[...] = a.astype(BF)
        b_ref[...] = b.astype(BF)
        m_ref[...] = (jax.nn.sigmoid(gla_ref[...].astype(F32)) * a + jax.nn.sigmoid(glb_ref[...].astype(F32)) * b).astype(BF)

    return pl.pallas_call(
        body, name="branch_merge", grid=(T // tm, nj),
        in_specs=[
            pl.BlockSpec((tm, SGU_W), lambda i, j: (i, 0)),
            pl.BlockSpec((tm, ATTN_W), lambda i, j: (i, 0)),
            pl.BlockSpec((SGU_W, tn), lambda i, j: (0, j)),
            pl.BlockSpec((ATTN_W, tn), lambda i, j: (0, j)),
            pl.BlockSpec((tm, tn), lambda i, j: (i, j)),
            pl.BlockSpec((tm, tn), lambda i, j: (i, j + nj)),
        ],
        out_specs=[pl.BlockSpec((tm, tn), lambda i, j: (i, j))] * 3,
        out_shape=[jax.ShapeDtypeStruct((T, D_MODEL), BF)] * 3,
        compiler_params=_params("parallel", "parallel"),
    )(ysgu, yattn, w_bs, w_ba, gl, gl)


def _row_spec(tr, width):
    return pl.BlockSpec((tr, width), lambda i: (i, 0))


def _vec_spec(width):
    return pl.BlockSpec((1, width), lambda i: (0, 0))


def _rms_fwd(x, g, shards, *, tr=256):
    T = x.shape[0]
    tr = min(tr, T)
    n_steps = T // tr
    k = len(shards)

    def body(x_ref, g_ref, *refs):
        step = pl.program_id(0)
        gather_start, gather_forward, gather_finish = _gather_phases(refs[:k], refs[k + 1:2 * k + 1], *refs[2 * k + 1:])
        pl.when(step == 0)(gather_start)
        pl.when(step == (3 * n_steps) // 4)(gather_forward)
        xv = x_ref[...]
        r = lax.rsqrt(jnp.mean(xv * xv, axis=-1, keepdims=True) + EPS)
        refs[k][...] = ((xv * r) * g_ref[...]).astype(BF)
        pl.when(step == n_steps - 1)(gather_finish)

    outs = pl.pallas_call(
        body, name="rms_fwd", grid=(n_steps,),
        in_specs=[_row_spec(tr, D_MODEL), _vec_spec(D_MODEL)] + [HBM] * k, out_specs=[_row_spec(tr, D_MODEL)] + [HBM] * k,
        out_shape=[jax.ShapeDtypeStruct((T, D_MODEL), BF)] + _gathered_shapes(shards),
        scratch_shapes=_gather_semaphores(k), compiler_params=_params("arbitrary"),
    )(x, g, *shards)
    return outs[0], outs[1:]


def _mixer_out_fwd(o, x, g_post, g_pre):
    r = lax.rsqrt(jnp.mean(o * o, axis=-1, keepdims=True) + EPS)
    h1 = x + (o * r) * g_post
    r2 = lax.rsqrt(jnp.mean(h1 * h1, axis=-1, keepdims=True) + EPS)
    return o, h1, (h1 * r2) * g_pre


def _matmul_rows(a, b, *, nt, rows, vecs, row_outs, n_sums, epilogue, name, tm=512):
    M, K = a.shape
    N = b.shape[0] if nt else b.shape[1]
    tm = min(tm, M)
    n_rows, n_vecs, n_out = len(rows), len(vecs), len(row_outs)

    def body(a_ref, b_ref, *refs):
        r_refs, v_refs = refs[:n_rows], refs[n_rows:n_rows + n_vecs]
        o_refs, s_refs = refs[n_rows + n_vecs:n_rows + n_vecs + n_out], refs[n_rows + n_vecs + n_out:]

        @pl.when(pl.program_id(0) == 0)
        def _():
            for s_ref in s_refs:
                s_ref[...] = jnp.zeros_like(s_ref)

        acc = _dot_nt(a_ref[...], b_ref[...]) if nt else _dot(a_ref[...], b_ref[...])
        outs = epilogue(acc, *[r[...] for r in r_refs], *[v[...] for v in v_refs])
        for o_ref, o in zip(o_refs, outs[:n_out], strict=True):
            o_ref[...] = o.astype(o_ref.dtype)
        for s_ref, term in zip(s_refs, outs[n_out:], strict=True):
            s_ref[...] += jnp.sum(term, axis=0, keepdims=True)

    return pl.pallas_call(
        body, name=name, grid=(M // tm,),
        in_specs=[_row_spec(tm, K), pl.BlockSpec(b.shape, lambda i: (0, 0))] + [_row_spec(tm, N)] * n_rows + [_vec_spec(N)] * n_vecs,
        out_specs=[_row_spec(tm, N)] * n_out + [_vec_spec(N)] * n_sums,
        out_shape=[jax.ShapeDtypeStruct((M, N), dt) for dt in row_outs] + [jax.ShapeDtypeStruct((1, N), F32)] * n_sums,
        compiler_params=pltpu.CompilerParams(dimension_semantics=("arbitrary",), vmem_limit_bytes=BIG_VMEM),
    )(a, b, *rows, *vecs)


def _loss_head(dn, h1, target, g):
    r = lax.rsqrt(jnp.mean(dn * dn, axis=-1, keepdims=True) + EPS)
    err = h1 + (dn * r) * g - target
    dy = err * (1.0 / D_MODEL)
    ddn, dg_terms = _rms_bwd(dn, g, dy)
    return dy, ddn, err * err, dg_terms


def _mixer_out_bwd(dxn2, h1, dy, o, g_pre, g_post):
    da, dg_pre_terms = _rms_bwd(h1, g_pre, dxn2)
    dh1 = dy + da
    do, dg_post_terms = _rms_bwd(o, g_post, dh1)
    return dh1, do, dg_pre_terms, dg_post_terms


def _input_norm_bwd(x, dxn, dh1, g, *, tr=256):
    T = x.shape[0]
    tr = min(tr, T)

    def body(x_ref, dxn_ref, dh1_ref, g_ref, dx_ref, dg_ref):
        @pl.when(pl.program_id(0) == 0)
        def _():
            dg_ref[...] = jnp.zeros_like(dg_ref)

        da, dgp = _rms_bwd(x_ref[...], g_ref[...], dxn_ref[...])
        dx_ref[...] = dh1_ref[...] + da
        dg_ref[...] += jnp.sum(dgp, axis=0, keepdims=True)

    return pl.pallas_call(
        body, name="input_norm_bwd", grid=(T // tr,),
        in_specs=[_row_spec(tr, D_MODEL)] * 3 + [_vec_spec(D_MODEL)],
        out_specs=[_row_spec(tr, D_MODEL), _vec_spec(D_MODEL)],
        out_shape=[jax.ShapeDtypeStruct((T, D_MODEL), F32), jax.ShapeDtypeStruct((1, D_MODEL), F32)],
        compiler_params=_params("arbitrary"),
    )(x, dxn, dh1, g)


def _sgu_norm(z_tile, g, b):
    gz = _gelu(z_tile)
    u, vv = gz[:, :SGU_W], gz[:, SGU_W:]
    xc = vv - jnp.mean(vv, axis=-1, keepdims=True)
    rstd = lax.rsqrt(jnp.mean(xc * xc, axis=-1, keepdims=True) + EPS)
    xhat = xc * rstd
    return u, xhat, rstd, xhat * g + b


def _sgu_mix(w_ref, v_bf, first_half):
    parts = []
    for p in range(N_GROUPS // 2):
        vp = v_bf[:, p * LANES:(p + 1) * LANES]
        parts.append(jnp.where(first_half, _dot(w_ref[2 * p], vp), _dot(w_ref[2 * p + 1], vp)))
    return jnp.concatenate(parts, axis=1)


def _sgu_fwd(z, g_sgu, b_sgu, ws, bias_plane, *, tm=512):
    T = z.shape[0]
    tm = min(tm, T)

    def body(z_ref, g_ref, b_ref, ws_ref, bp_ref, y_ref):
        u, _, _, vn = _sgu_norm(z_ref[...], g_ref[...], b_ref[...])
        vn_bf = vn.astype(BF)
        first_half = lax.broadcasted_iota(jnp.int32, (CHUNK, LANES), 1) < HEAD_DIM
        for c in range(tm // CHUNK):
            rows = slice(c * CHUNK, (c + 1) * CHUNK)
            s = _sgu_mix(ws_ref, vn_bf[rows, :], first_half) + bp_ref[...]
            y_ref[rows, :] = (u[rows, :] * s).astype(BF)

    return pl.pallas_call(
        body, name="sgu_fwd", grid=(T // tm,),
        in_specs=[_row_spec(tm, 2 * SGU_W), _vec_spec(SGU_W), _vec_spec(SGU_W),
                  pl.BlockSpec((N_GROUPS, CHUNK, CHUNK), lambda i: (0, 0, 0)),
                  pl.BlockSpec((CHUNK, SGU_W), lambda i: (0, 0))],
        out_specs=_row_spec(tm, SGU_W), out_shape=jax.ShapeDtypeStruct((T, SGU_W), BF),
        compiler_params=_params("parallel"),
    )(z, g_sgu, b_sgu, ws, bias_plane)


def _sgu_bwd(dy, z, g_sgu, b_sgu, ws, ws_t, bias_plane, exchange, *, tm=512):
    T = z.shape[0]
    tm = min(tm, T)
    n_steps = T // tm
    k = len(exchange)

    def body(dy_ref, z_ref, g_ref, b_ref, ws_ref, wst_ref, bp_ref, *refs):
        x_refs, (dz_ref, dws_ref, dbs_ref, dg_ref, db_ref), r_refs = refs[:k], refs[k:k + 5], refs[k + 5:2 * k + 5]
        dbp_ref, send_sems, recv_sems = refs[2 * k + 5:]
        step = pl.program_id(0)
        exchange_start, exchange_finish = _exchange_phases(x_refs, r_refs, send_sems, recv_sems)
        pl.when(step == 0)(exchange_start)

        @pl.when(step == 0)
        def _():
            dws_ref[...] = jnp.zeros_like(dws_ref)
            dg_ref[...] = jnp.zeros_like(dg_ref)
            db_ref[...] = jnp.zeros_like(db_ref)
            dbp_ref[...] = jnp.zeros_like(dbp_ref)

        g = g_ref[...]
        zt = z_ref[...]
        u, xhat, rstd, vn = _sgu_norm(zt, g, b_ref[...])
        vn_bf = vn.astype(BF)
        first_half = lax.broadcasted_iota(jnp.int32, (CHUNK, LANES), 1) < HEAD_DIM
        dyv = dy_ref[...]
        dg_acc = jnp.zeros((1, SGU_W), F32)
        db_acc = jnp.zeros((1, SGU_W), F32)
        for c in range(tm // CHUNK):
            rows = slice(c * CHUNK, (c + 1) * CHUNK)
            v_c = vn_bf[rows, :]
            s = _sgu_mix(ws_ref, v_c, first_half) + bp_ref[...]
            dy_c = dyv[rows, :]
            du = dy_c * s
            dsv = dy_c * u[rows, :]
            dbp_ref[...] += dsv
            ds_bf = dsv.astype(BF)
            zero = jnp.zeros((CHUNK, LANES), BF)
            for p in range(N_GROUPS // 2):
                dsp = ds_bf[:, p * LANES:(p + 1) * LANES]
                vp = v_c[:, p * LANES:(p + 1) * LANES]
                dws_ref[2 * p] += _dot_nt(jnp.where(first_half, dsp, zero), vp)
                dws_ref[2 * p + 1] += _dot_nt(jnp.where(first_half, zero, dsp), vp)
            dvn = _sgu_mix(wst_ref, ds_bf, first_half)
            xh = xhat[rows, :]
            dxh = dvn * g
            dvv = rstd[rows, :] * (dxh - jnp.mean(dxh, axis=-1, keepdims=True)
                                   - xh * jnp.mean(dxh * xh, axis=-1, keepdims=True))
            dg_acc += jnp.sum(dvn * xh, axis=0, keepdims=True)
            db_acc += jnp.sum(dvn, axis=0, keepdims=True)
            dgz = jnp.concatenate([du, dvv], axis=1)
            dz_ref[rows, :] = (dgz * _gelu_grad(zt[rows, :])).astype(BF)
        dg_ref[...] += dg_acc
        db_ref[...] += db_acc

        @pl.when(step == n_steps - 1)
        def _():
            r = lax.broadcasted_iota(jnp.int32, (CHUNK, CHUNK), 0)
            cidx = lax.broadcasted_iota(jnp.int32, (CHUNK, CHUNK), 1)
            causal = (cidx <= r).astype(F32)
            for gi in range(N_GROUPS):
                dws_ref[gi] = dws_ref[gi] * causal
            lane = lax.broadcasted_iota(jnp.int32, (CHUNK, LANES), 1)
            out = jnp.zeros((CHUNK, LANES), F32)
            dbp = dbp_ref[...]
            for gi in range(N_GROUPS):
                col = jnp.sum(dbp[:, gi * HEAD_DIM:(gi + 1) * HEAD_DIM], axis=1, keepdims=True)
                out = jnp.where(lane == gi, col, out)
            dbs_ref[...] = out
            exchange_finish()

    w_spec = pl.BlockSpec((N_GROUPS, CHUNK, CHUNK), lambda i: (0, 0, 0))
    plane = pl.BlockSpec((CHUNK, SGU_W), lambda i: (0, 0))
    outs = pl.pallas_call(
        body, name="sgu_bwd", grid=(n_steps,),
        in_specs=[_row_spec(tm, SGU_W), _row_spec(tm, 2 * SGU_W), _vec_spec(SGU_W), _vec_spec(SGU_W), w_spec, w_spec, plane]
        + [HBM] * k,
        out_specs=[_row_spec(tm, 2 * SGU_W), w_spec, pl.BlockSpec((CHUNK, LANES), lambda i: (0, 0)),
                   _vec_spec(SGU_W), _vec_spec(SGU_W)] + [HBM] * k,
        out_shape=[jax.ShapeDtypeStruct((T, 2 * SGU_W), BF), jax.ShapeDtypeStruct((N_GROUPS, CHUNK, CHUNK), F32),
                   jax.ShapeDtypeStruct((CHUNK, LANES), F32), jax.ShapeDtypeStruct((1, SGU_W), F32),
                   jax.ShapeDtypeStruct((1, SGU_W), F32)] + _exchanged_shapes(exchange),
        scratch_shapes=[pltpu.VMEM((CHUNK, SGU_W), F32)] + _exchange_semaphores(k),
        compiler_params=_params("arbitrary"),
    )(dy, z, g_sgu, b_sgu, ws, ws_t, bias_plane, *exchange)
    return outs[:5], outs[5:]


def _tri(n, upper):
    r = lax.broadcasted_iota(jnp.int32, (n, n), 0)
    c = lax.broadcasted_iota(jnp.int32, (n, n), 1)
    return ((c >= r) if upper else (c <= r)).astype(BF)


def _scan_dot(tri, x):
    hi, mid, lo = _split3(x)
    return (_dot(tri, hi.astype(BF)) + _dot(tri, mid.astype(BF))) + _dot(tri, lo.astype(BF))


def _with_lanes(base, lane, start, cols):
    out = base
    for k, col in enumerate(cols):
        if col is not None:
            out = jnp.where(lane == start + k, col, out)
    return out


def _logit_bound(q_norm, k_norm):
    return NORM_SLACK * q_norm * k_norm + 1.0


ATTN_TILE = 512
SKIP_BELOW = -110.0
NORM_SLACK = 1.001
BOUNDED_GAP = 60.0


def _attn_prep(qkv, fl, b_forget, *, tp=ATTN_TILE):
    T = qkv.shape[0]
    tp = min(tp, T)
    head_sum, gather6, place_q, place_k, place_v = (jnp.asarray(m, BF) for m in _attn_placements())

    def body(qkv_ref, fl_ref, bf_ref, hs_ref, g6_ref, pq_ref, pk_ref, pv_ref, qf_ref, kl_ref, vl_ref, st_ref, carry_ref, kmax_ref):
        @pl.when(pl.program_id(0) == 0)
        def _():
            carry_ref[...] = jnp.zeros_like(carry_ref)
            kmax_ref[...] = jnp.zeros_like(kmax_ref)

        x = fl_ref[...] + bf_ref[...]
        logf = jnp.minimum(x, 0.0) - jnp.log(1.0 + jnp.exp(-jnp.abs(x)))
        cum = _scan_dot(_tri(tp, upper=False), logf) + carry_ref[...]
        carry_ref[...] = cum[tp - 1:tp, :]

        def head_norms(block):
            sq = block * block
            hi = sq.astype(BF)
            return _dot(hi, hs_ref[...]) + _dot((sq - hi.astype(F32)).astype(BF), hs_ref[...])

        qkvv = qkv_ref[...]
        q_norm = NORM_SLACK * jnp.sqrt(head_norms(qkvv[:, :ATTN_W].astype(F32) * Q_SCALE))
        kn = NORM_SLACK * jnp.sqrt(jnp.max(head_norms(qkvv[:, ATTN_W:2 * ATTN_W].astype(F32)), axis=0, keepdims=True))
        k_seen = jnp.maximum(kmax_ref[...], kn)
        kmax_ref[...] = k_seen
        rows = (jnp.max(q_norm, axis=0, keepdims=True), kn, jnp.max(cum, axis=0, keepdims=True),
                jnp.min(cum, axis=0, keepdims=True), k_seen)
        st_ref[...] = jnp.zeros_like(st_ref)
        for k, row in enumerate(rows):
            st_ref[0, k:k + 1, :] = row
        parts = jnp.concatenate([p.astype(BF) for p in _split3(cum) + _split3(-_logit_bound(q_norm, k_seen))], axis=1)
        lane = lax.broadcasted_iota(jnp.int32, (tp, LANES), 1)
        side = jnp.where(lane == 6 * N_HEADS, 1.0, _dot(parts, g6_ref[...])).astype(BF)
        for h in range(N_HEADS):
            pair = slice((h // 2) * LANES, (h // 2 + 1) * LANES)
            for out_ref, block, place_ref in ((qf_ref, qkvv[:, :ATTN_W], pq_ref), (kl_ref, qkvv[:, ATTN_W:2 * ATTN_W], pk_ref),
                                              (vl_ref, qkvv[:, 2 * ATTN_W:], pv_ref)):
                out_ref[h] = _dot(jnp.concatenate([block[:, pair], side], axis=1), place_ref[h]).astype(BF)

    head_spec = pl.BlockSpec((N_HEADS, tp, LANES), lambda i: (0, i, 0))
    whole = lambda a: pl.BlockSpec(a.shape, lambda i: (0,) * a.ndim)
    return pl.pallas_call(
        body, name="attn_prep", grid=(T // tp,),
        in_specs=[_row_spec(tp, 3 * ATTN_W), _row_spec(tp, LANES), _vec_spec(LANES)]
        + [whole(m) for m in (head_sum, gather6, place_q, place_k, place_v)],
        out_specs=[head_spec] * 3 + [pl.BlockSpec((1, N_HEADS, LANES), lambda i: (i, 0, 0))],
        out_shape=[jax.ShapeDtypeStruct((N_HEADS, T, LANES), BF)] * 3 + [jax.ShapeDtypeStruct((T // tp, N_HEADS, LANES), F32)],
        scratch_shapes=[pltpu.VMEM((1, LANES), F32), pltpu.VMEM((1, LANES), F32)], compiler_params=_params("arbitrary"),
    )(qkv, fl, b_forget, head_sum, gather6, place_q, place_k, place_v)


def _attn_placements():
    head_sum = np.zeros((ATTN_W, LANES), np.float32)
    head_sum[np.arange(ATTN_W), np.arange(ATTN_W) // HEAD_DIM] = 1.0
    gather6 = np.zeros((6 * LANES, LANES), np.float32)
    for j in range(6):
        gather6[j * LANES + np.arange(N_HEADS), j * N_HEADS + np.arange(N_HEADS)] = 1.0
    place = np.zeros((3, N_HEADS, 2 * LANES, LANES), np.float32)
    one = LANES + 6 * N_HEADS
    d = np.arange(HEAD_DIM)
    for h in range(N_HEADS):
        side = lambda j: LANES + j * N_HEADS + h
        place[0, h, (h % 2) * HEAD_DIM + d, d] = Q_SCALE
        place[1:, h, (h % 2) * HEAD_DIM + d, d] = 1.0
        for j in range(3):
            place[0, h, side(j), HEAD_DIM + j] = 1.0
            place[0, h, one, HEAD_DIM + 3 + j] = 1.0
            place[0, h, side(3 + j), HEAD_DIM + 6 + j] = 1.0
            place[1, h, one, HEAD_DIM + j] = 1.0
            place[1, h, side(j), HEAD_DIM + 3 + j] = -1.0
            place[1, h, one, HEAD_DIM + 6 + j] = 1.0
            place[2, h, one, HEAD_DIM + j] = 1.0
    return head_sum, gather6, place[0], place[1], place[2]


def _attn_ranges(stats):
    qn, kn, cmax, cmin, k_seen = (stats[:, k, :N_HEADS].T for k in range(5))
    n = qn.shape[1]
    bounded = (2.0 * _logit_bound(qn, k_seen) <= BOUNDED_GAP).reshape(N_HEADS // 2, 2, n).all(axis=1)
    reach = NORM_SLACK * qn * (jnp.max(kn, axis=1, keepdims=True) + kn) + cmax
    i = jnp.arange(n)[None, :, None]
    j = jnp.arange(n)[None, None, :]
    need = ((reach[:, :, None] - cmin[:, None, :] >= SKIP_BELOW) | (i == j)) & (j <= i)
    first = jnp.min(jnp.where(need, j, n), axis=2).reshape(N_HEADS // 2, 2, n).min(axis=1)
    last = jnp.max(jnp.where(need, i, -1), axis=1).reshape(N_HEADS // 2, 2, n).max(axis=1)
    return first.reshape(-1).astype(F32), last.reshape(-1).astype(F32), bounded.reshape(-1).astype(F32)


def _pair_block(t):
    return pl.BlockSpec((2, t, LANES), lambda p, i, *_: (p, i, 0))


def _pair_full(T):
    return pl.BlockSpec((2, T, LANES), lambda p, i, *_: (p, 0, 0))


def _packed_block(t):
    return pl.BlockSpec((t, LANES), lambda p, i, *_: (i, p))


def _causal(t, keys_in_rows=False):
    r = lax.broadcasted_iota(jnp.int32, (t, t), 0)
    c = lax.broadcasted_iota(jnp.int32, (t, t), 1)
    return (r <= c) if keys_in_rows else (c <= r)


def _tile_rows(j, t):
    return pl.ds(pl.multiple_of(j * t, t), t)


def _attn_call(body, name, tile_scalars, operands, in_specs, out_specs, out_shape, scratch_shapes, n_tiles):
    return pl.pallas_call(
        body, name=name,
        grid_spec=pltpu.PrefetchScalarGridSpec(
            num_scalar_prefetch=len(tile_scalars), grid=(N_HEADS // 2, n_tiles), in_specs=in_specs, out_specs=out_specs,
            scratch_shapes=scratch_shapes),
        out_shape=out_shape, compiler_params=_params("arbitrary", "arbitrary"),
    )(*tile_scalars, *operands)


def _attn_fwd(qf, kl, vl, first, bounded, shards, *, tq=ATTN_TILE):
    T = qf.shape[1]
    tq = min(tq, T)
    n = T // tq
    n_steps = (N_HEADS // 2) * n
    k = len(shards)

    def body(first_ref, bounded_ref, qf_ref, kl_ref, vl_ref, *refs):
        w_refs, (o_ref, of_ref, ql_ref), g_refs = refs[:k], refs[k:k + 3], refs[k + 3:2 * k + 3]
        m_ref, acc_ref, send_sems, recv_sems = refs[2 * k + 3:]
        i = pl.program_id(1)
        tile = pl.program_id(0) * n + i
        gather_start, gather_forward, gather_finish = _gather_phases(w_refs, g_refs, send_sems, recv_sems)
        pl.when(tile == 0)(gather_start)
        pl.when(tile == (3 * n_steps) // 4)(gather_forward)
        start = first_ref[tile].astype(jnp.int32)
        is_bounded = bounded_ref[tile] > 0.5
        acc_ref[...] = jnp.zeros_like(acc_ref)
        diagonal = _tile_rows(i, tq)
        causal = _causal(tq)

        def logits(hh, rows):
            return _dot_nt(qf_ref[hh], kl_ref[hh, rows, :])

        @pl.when(is_bounded)
        def _():
            m_ref[...] = jnp.zeros_like(m_ref)

            def update(hh, s, rows):
                acc_ref[hh] += _dot(jnp.exp(s).astype(BF), vl_ref[hh, rows, :])

            def step(j, carry):
                for hh in range(2):
                    update(hh, logits(hh, _tile_rows(j, tq)), _tile_rows(j, tq))
                return carry

            lax.fori_loop(start, i, step, 0)
            for hh in range(2):
                update(hh, jnp.where(causal, logits(hh, diagonal), NEG), diagonal)

        @pl.when(jnp.logical_not(is_bounded))
        def _():
            m_ref[...] = jnp.full_like(m_ref, NEG)

            def update(hh, s, rows):
                m_old = m_ref[hh]
                m_new = jnp.maximum(m_old, jnp.max(s, axis=1, keepdims=True))
                p = jnp.exp(s - m_new)
                acc_ref[hh] = jnp.exp(m_old - m_new) * acc_ref[hh] + _dot(p.astype(BF), vl_ref[hh, rows, :])
                m_ref[hh] = m_new

            def step(j, carry):
                for hh in range(2):
                    update(hh, logits(hh, _tile_rows(j, tq)), _tile_rows(j, tq))
                return carry

            lax.fori_loop(start, i, step, 0)
            for hh in range(2):
                update(hh, jnp.where(causal, logits(hh, diagonal), NEG), diagonal)

        lane = lax.broadcasted_iota(jnp.int32, (tq, LANES), 1)
        outs = []
        for hh in range(2):
            q = qf_ref[hh].astype(F32)
            acc = acc_ref[hh]
            l = acc[:, HEAD_DIM:HEAD_DIM + 1]
            outs.append(acc[:, :HEAD_DIM] / l)
            at = HEAD_DIM + 6
            neg_bound = (q[:, at:at + 1] + q[:, at + 1:at + 2]) + q[:, at + 2:at + 3]
            ql_ref[hh] = _with_lanes(q, lane, at, _split3(neg_bound - (m_ref[hh] + jnp.log(l)))).astype(BF)
        o = jnp.concatenate(outs, axis=1)
        o_ref[...] = o.astype(BF)
        of_ref[...] = o
        pl.when(tile == n_steps - 1)(gather_finish)

    outs = _attn_call(
        body, "attn_fwd", (first, bounded), (qf, kl, vl, *shards),
        [_pair_block(tq), _pair_full(T), _pair_full(T)] + [HBM] * k,
        [_packed_block(tq), _packed_block(tq), _pair_block(tq)] + [HBM] * k,
        [jax.ShapeDtypeStruct((T, ATTN_W), BF), jax.ShapeDtypeStruct((T, ATTN_W), F32),
         jax.ShapeDtypeStruct((N_HEADS, T, LANES), BF)] + _gathered_shapes(shards),
        [pltpu.VMEM((2, tq, 1), F32), pltpu.VMEM((2, tq, LANES), F32)] + _gather_semaphores(k), n)
    return outs[0], outs[1], outs[2], outs[3:]


def _attn_bwd_prep(dya, of, *, tr=256):
    T = dya.shape[0]
    tr = min(tr, T)

    def body(d_ref, o_ref, do_ref):
        lane = lax.broadcasted_iota(jnp.int32, (tr, HEAD_DIM), 1)
        dv, ov = d_ref[...], o_ref[...]
        for h in range(N_HEADS):
            d = dv[:, h * HEAD_DIM:(h + 1) * HEAD_DIM]
            delta = jnp.sum(d * ov[:, h * HEAD_DIM:(h + 1) * HEAD_DIM], axis=1, keepdims=True)
            ext = _with_lanes(jnp.zeros((tr, HEAD_DIM), F32), lane, 0, _split3(-delta))
            do_ref[h] = jnp.concatenate([d, ext], axis=1).astype(BF)

    return pl.pallas_call(
        body, name="attn_bwd_prep", grid=(T // tr,),
        in_specs=[_row_spec(tr, ATTN_W), _row_spec(tr, ATTN_W)],
        out_specs=pl.BlockSpec((N_HEADS, tr, LANES), lambda i: (0, i, 0)),
        out_shape=jax.ShapeDtypeStruct((N_HEADS, T, LANES), BF), compiler_params=_params("parallel"),
    )(dya, of)


def _attn_bwd(kl, vl, ql, do, last, chip_sums, *, tk=ATTN_TILE):
    T = ql.shape[1]
    tk = min(tk, T)
    n = T // tk
    n_steps = (N_HEADS // 2) * n
    m = len(chip_sums)

    def body(last_ref, kl_ref, vl_ref, ql_ref, do_ref, *refs):
        b_refs, (dq_ref, dk_ref, dv_ref, extq_ref, extk_ref), r_refs = refs[:m], refs[m:m + 5], refs[m + 5:2 * m + 5]
        dq_acc, dk_acc, dv_acc, send_sems, recv_sems = refs[2 * m + 5:]
        j = pl.program_id(1)
        tile = pl.program_id(0) * n + j
        scatter_start, scatter_finish = _scatter_phases(b_refs, r_refs, send_sems, recv_sems)
        pl.when(tile == 0)(scatter_start)

        @pl.when(j == 0)
        def _():
            dq_acc[...] = jnp.zeros_like(dq_acc)

        dk_acc[...] = jnp.zeros_like(dk_acc)
        dv_acc[...] = jnp.zeros_like(dv_acc)

        def block(hh, rows, mask):
            qi, di, k = ql_ref[hh, rows, :], do_ref[hh, rows, :], kl_ref[hh]
            p_t = jnp.exp(_dot_nt(k, qi))
            if mask is not None:
                p_t = jnp.where(mask, p_t, 0.0)
            ds_t = (p_t * _dot_nt(vl_ref[hh], di)).astype(BF)
            dk_acc[hh] += _dot(ds_t, qi)
            dv_acc[hh] += _dot(p_t.astype(BF), di)
            dq_acc[hh, rows, :] += _dot_tn(ds_t, k)

        causal_t = _causal(tk, keys_in_rows=True)
        for hh in range(2):
            block(hh, _tile_rows(j, tk), causal_t)

        def step(i, carry):
            for hh in range(2):
                block(hh, _tile_rows(i, tk), None)
            return carry

        lax.fori_loop(j + 1, last_ref[pl.program_id(0) * n + j].astype(jnp.int32) + 1, step, 0)
        dk_ref[...] = jnp.concatenate([dk_acc[hh][:, :HEAD_DIM] for hh in range(2)], axis=1).astype(BF)
        dv_ref[...] = jnp.concatenate([dv_acc[hh][:, :HEAD_DIM] for hh in range(2)], axis=1).astype(BF)
        extk_ref[...] = jnp.concatenate([dk_acc[hh][:, HEAD_DIM:] for hh in range(2)], axis=1)

        @pl.when(j == n - 1)
        def _():
            dq_ref[...] = jnp.concatenate([dq_acc[hh][:, :HEAD_DIM] * Q_SCALE for hh in range(2)], axis=1).astype(BF)
            extq_ref[...] = jnp.concatenate([dq_acc[hh][:, HEAD_DIM:] for hh in range(2)], axis=1)

        pl.when(tile == n_steps - 1)(scatter_finish)

    whole = pl.BlockSpec((T, LANES), lambda p, j, *_: (0, p))
    outs = pl.pallas_call(
        body, name="attn_bwd",
        grid_spec=pltpu.PrefetchScalarGridSpec(
            num_scalar_prefetch=1, grid=(N_HEADS // 2, n),
            in_specs=[_pair_block(tk), _pair_block(tk), _pair_full(T), _pair_full(T)] + [HBM] * m,
            out_specs=[whole, _packed_block(tk), _packed_block(tk), whole, _packed_block(tk)] + [HBM] * m,
            scratch_shapes=[pltpu.VMEM((2, T, LANES), F32), pltpu.VMEM((2, tk, LANES), F32), pltpu.VMEM((2, tk, LANES), F32)]
            + _scatter_semaphores(m)),
        out_shape=[jax.ShapeDtypeStruct((T, ATTN_W), BF)] * 3 + [jax.ShapeDtypeStruct((T, ATTN_W), F32)] * 2
        + _scattered_shapes(chip_sums),
        compiler_params=pltpu.CompilerParams(dimension_semantics=("arbitrary", "arbitrary"), vmem_limit_bytes=BIG_VMEM),
    )(last, kl, vl, ql, do, *chip_sums)
    return outs[:5], outs[5:]


def _forget_bwd(ext_q, ext_k, fl, b_forget, *, tp=256):
    T = fl.shape[0]
    tp = min(tp, T)
    n = T // tp

    def body(eq_ref, ek_ref, fl_ref, bf_ref, dfl_ref, dbf_ref, carry_ref):
        @pl.when(pl.program_id(0) == 0)
        def _():
            carry_ref[...] = jnp.zeros_like(carry_ref)
            dbf_ref[...] = jnp.zeros_like(dbf_ref)

        lane = lax.broadcasted_iota(jnp.int32, (tp, LANES), 1)
        eq, ek = eq_ref[...], ek_ref[...]
        cols = [eq[:, h * HEAD_DIM:h * HEAD_DIM + 1] - ek[:, h * HEAD_DIM + 3:h * HEAD_DIM + 4] for h in range(N_HEADS)]
        dcum = _with_lanes(jnp.zeros((tp, LANES), F32), lane, 0, cols)
        suffix = _scan_dot(_tri(tp, upper=True), dcum) + carry_ref[...]
        carry_ref[...] = suffix[0:1, :]
        x = fl_ref[...] + bf_ref[...]
        dfl = jnp.where(lane < N_HEADS, suffix / (1.0 + jnp.exp(x)), 0.0)
        dfl_ref[...] = dfl.astype(BF)
        dbf_ref[...] += jnp.sum(dfl, axis=0, keepdims=True)

    rev = lambda w: pl.BlockSpec((tp, w), lambda i: (n - 1 - i, 0))
    return pl.pallas_call(
        body, name="forget_bwd", grid=(n,),
        in_specs=[rev(ATTN_W), rev(ATTN_W), rev(LANES), _vec_spec(LANES)],
        out_specs=[rev(LANES), _vec_spec(LANES)],
        out_shape=[jax.ShapeDtypeStruct((T, LANES), BF), jax.ShapeDtypeStruct((1, LANES), F32)],
        scratch_shapes=[pltpu.VMEM((1, LANES), F32)], compiler_params=_params("arbitrary"),
    )(ext_q, ext_k, fl, b_forget)


def _adamw(w, g, m, v, *, name, tr=256):
    _, rows, cols = w.shape
    tr = tr if rows % tr == 0 else rows

    def body(w_ref, g_ref, m_ref, v_ref, go_ref, d_ref, nm_ref, nv_ref):
        gv = g_ref[...]
        go_ref[...] = gv
        nm = ADAM_B1 * m_ref[...] + (1.0 - ADAM_B1) * gv
        nv = ADAM_B2 * v_ref[...] + (1.0 - ADAM_B2) * (gv * gv)
        m_hat = nm / (1.0 - ADAM_B1 ** ADAM_STEP)
        v_hat = nv / (1.0 - ADAM_B2 ** ADAM_STEP)
        d_ref[...] = -ADAM_LR * (m_hat / (jnp.sqrt(v_hat) + ADAM_EPS) + ADAM_WD * w_ref[...])
        nm_ref[...] = nm
        nv_ref[...] = nv

    spec = pl.BlockSpec((None, tr, cols), lambda i: (0, i, 0))
    return pl.pallas_call(
        body, name=name, grid=(rows // tr,), in_specs=[spec, pl.BlockSpec((tr, cols), lambda i: (i, 0)), spec, spec],
        out_specs=[spec] * 4, out_shape=[jax.ShapeDtypeStruct((1, rows, cols), F32)] * 4,
        compiler_params=_params("parallel"),
    )(w, g, m, v)


HBM = pl.BlockSpec(memory_space=pltpu.HBM)
BF16_ROWS = 16


def _place():
    x, y, c = lax.axis_index("x"), lax.axis_index("y"), lax.axis_index("c")
    others = [(1 - x, y), (x, 1 - y), (1 - x, 1 - y)]
    return x, y, c, others


def _chip(xy):
    return 2 * xy[0] + xy[1]


def _row_halves(c, rows):
    half = rows // 2
    assert half % BF16_ROWS == 0
    return (pl.ds(pl.multiple_of(c * half, BF16_ROWS), half), pl.ds(pl.multiple_of((1 - c) * half, BF16_ROWS), half))


def _remote(src, dst, send_sems, recv_sems, k, to):
    return pltpu.make_async_remote_copy(src_ref=src, dst_ref=dst, send_sem=send_sems.at[k], recv_sem=recv_sems.at[k],
                                        device_id=to, device_id_type=MESH)


def _gathered_shapes(shards):
    return [jax.ShapeDtypeStruct((N_CHIPS,) + s.shape, s.dtype) for s in shards]


def _gather_semaphores(n):
    return [pltpu.SemaphoreType.DMA((6 * n,)), pltpu.SemaphoreType.DMA((6 * n,))]


def _gather_phases(w_refs, g_refs, send_sems, recv_sems):
    n = len(w_refs)
    x, y, c, others = _place()
    sibling, me = (x, y, 1 - c), _chip((x, y))
    halves = [_row_halves(c, w.shape[0]) for w in w_refs]

    def sent(a, j, o):
        mine, _ = halves[a]
        return _remote(w_refs[a].at[mine, :], g_refs[a].at[me, mine, :], send_sems, recv_sems, 6 * a + j, (*o, c))

    def passed(a, j, o):
        landed = g_refs[a].at[_chip(o), halves[a][0], :]
        return _remote(landed, landed, send_sems, recv_sems, 6 * a + 3 + j, sibling)

    def start():
        for a in range(n):
            for j, o in enumerate(others):
                sent(a, j, o).start()

    def forward():
        for j, o in enumerate(others):
            for a in range(n):
                landed = g_refs[a].at[_chip(o), halves[a][0], :]
                _remote(landed, landed, send_sems, recv_sems, 6 * a + j, (*o, c)).wait_recv()
                passed(a, j, o).start()

    def finish():
        for j, o in enumerate(others):
            for a in range(n):
                landed = g_refs[a].at[_chip(o), halves[a][1], :]
                _remote(landed, landed, send_sems, recv_sems, 6 * a + 3 + j, sibling).wait_recv()
        for a in range(n):
            for j, o in enumerate(others):
                sent(a, j, o).wait_send()
                passed(a, j, o).wait_send()

    return start, forward, finish


def _exchange_halves(arrays, *, name):
    n = len(arrays)

    def body(*refs):
        for phase in _exchange_phases(refs[:n], refs[n:2 * n], *refs[2 * n:]):
            phase()

    return pl.pallas_call(
        body, name=name, in_specs=[HBM] * n, out_specs=[HBM] * n, out_shape=_exchanged_shapes(arrays),
        scratch_shapes=_exchange_semaphores(n),
    )(*arrays)


def _exchanged_shapes(arrays):
    return [jax.ShapeDtypeStruct(s.shape[:-2] + (s.shape[-2] // 2, s.shape[-1]), F32) for s in arrays]


def _exchange_semaphores(n):
    return [pltpu.SemaphoreType.DMA((n,)), pltpu.SemaphoreType.DMA((n,))]


def _exchange_phases(g_refs, r_refs, send_sems, recv_sems):
    x, y, c, _ = _place()

    def copy(a):
        _, theirs = _row_halves(c, g_refs[a].shape[-2])
        src = g_refs[a].at[:, theirs, :] if len(g_refs[a].shape) == 3 else g_refs[a].at[theirs, :]
        return _remote(src, r_refs[a], send_sems, recv_sems, a, (x, y, 1 - c))

    def start():
        for a in range(len(g_refs)):
            copy(a).start()

    def finish():
        for a in range(len(g_refs)):
            copy(a).wait()

    return start, finish


def _scatter_to_owners(chip_sums):
    n = len(chip_sums)

    def body(*refs):
        for phase in _scatter_phases(refs[:n], refs[n:2 * n], *refs[2 * n:]):
            phase()

    return pl.pallas_call(
        body, name="scatter_to_owners", in_specs=[HBM] * n, out_specs=[HBM] * n,
        out_shape=_scattered_shapes(chip_sums), scratch_shapes=_scatter_semaphores(n),
    )(*chip_sums)


def _scattered_shapes(chip_sums):
    return [jax.ShapeDtypeStruct(b.shape if b.ndim == 3 else (N_CHIPS,) + b.shape, b.dtype) for b in chip_sums]


def _scatter_semaphores(n):
    return [pltpu.SemaphoreType.DMA((3 * n,)), pltpu.SemaphoreType.DMA((3 * n,))]


def _scatter_phases(b_refs, r_refs, send_sems, recv_sems):
    n = len(b_refs)
    x, y, c, others = _place()
    me = _chip((x, y))

    def sent(a, j, o):
        src = b_refs[a].at[_chip(o)] if len(b_refs[a].shape) == 3 else b_refs[a]
        return _remote(src, r_refs[a].at[me], send_sems, recv_sems, 3 * a + j, (*o, c))

    def start():
        for a in range(n):
            for j, o in enumerate(others):
                sent(a, j, o).start()

    def finish():
        for a in range(n):
            for j, o in enumerate(others):
                landed = r_refs[a].at[_chip(o)]
                _remote(landed, landed, send_sems, recv_sems, 3 * a + j, (*o, c)).wait_recv()
        for a in range(n):
            for j, o in enumerate(others):
                sent(a, j, o).wait_send()

    return start, finish


def _join_halves(totals):
    n = len(totals)

    def body(*refs):
        in_refs, out_refs, (send_sems, recv_sems) = refs[:n], refs[n:2 * n], refs[2 * n:]
        x, y, c, _ = _place()
        copies = []
        for a in range(n):
            mine, _ = _row_halves(c, in_refs[a].shape[0])
            copies.append(_remote(in_refs[a].at[mine, :], out_refs[a].at[mine, :], send_sems, recv_sems, a, (x, y, 1 - c)))
            copies[-1].start()
        for cp in copies:
            cp.wait()

    return pl.pallas_call(
        body, name="join_halves", in_specs=[HBM] * n, out_specs=[HBM] * n,
        out_shape=[jax.ShapeDtypeStruct(t.shape, F32) for t in totals], input_output_aliases={a: a for a in range(n)},
        scratch_shapes=[pltpu.SemaphoreType.DMA((n,)), pltpu.SemaphoreType.DMA((n,))],
    )(*totals)


ADD_ROWS = 128


def _add_sibling(g, r, place, *, name):
    lead, (half, cols) = g.shape[:-2], r.shape[-2:]
    tr = min(ADD_ROWS, half)
    nb = half // tr
    zeros = (0,) * len(lead)

    def body(place_ref, g_ref, r_ref, o_ref, ob_ref):
        s = g_ref[...] + r_ref[...]
        o_ref[...] = s
        ob_ref[...] = s.astype(BF)

    spec = pl.BlockSpec(lead + (tr, cols), lambda i, p: zeros + (i, 0))
    return pl.pallas_call(
        body, name=name,
        grid_spec=pltpu.PrefetchScalarGridSpec(
            num_scalar_prefetch=1, grid=(nb,),
            in_specs=[pl.BlockSpec(lead + (tr, cols), lambda i, p: zeros + (p[1] * nb + i, 0)), spec], out_specs=[spec, spec]),
        out_shape=[jax.ShapeDtypeStruct(r.shape, F32), jax.ShapeDtypeStruct(r.shape, BF)],
        compiler_params=_params("parallel"),
    )(place, g, r)


def _add_chips(own, received, place, *, name, own_slots):
    half, cols = received.shape[-2:]
    tr = min(ADD_ROWS, half)
    nb = half // tr

    def written(k, p):
        return jnp.where(p[0] == k, (k + 1) % N_CHIPS, k)

    def body(place_ref, own_ref, *refs):
        o_ref = refs[N_CHIPS]
        mine = own_ref[0] if own_slots else own_ref[...]
        if own_slots:
            acc = mine
            for k in range(N_CHIPS):
                acc = acc + jnp.where(place_ref[0] == k, 0.0, refs[k][0].astype(F32))
        else:
            terms = [jnp.where(place_ref[0] == k, mine, refs[k][0]) for k in range(N_CHIPS)]
            acc = ((terms[0] + terms[1]) + terms[2]) + terms[3]
        o_ref[...] = acc

    own_spec = (pl.BlockSpec((1, tr, cols), lambda i, p: (p[0], i, 0)) if own_slots
                else pl.BlockSpec((tr, cols), lambda i, p: (i, 0)))
    return pl.pallas_call(
        body, name=name,
        grid_spec=pltpu.PrefetchScalarGridSpec(
            num_scalar_prefetch=1, grid=(nb,),
            in_specs=[own_spec] + [pl.BlockSpec((1, tr, cols), functools.partial(lambda i, p, k: (written(k, p), i, 0), k=k))
                                   for k in range(N_CHIPS)],
            out_specs=pl.BlockSpec((tr, cols), lambda i, p: (p[1] * nb + i, 0))),
        out_shape=jax.ShapeDtypeStruct((2 * half, cols), F32), compiler_params=_params("parallel"),
    )(place, own, *([received] * N_CHIPS))


SHARDED = (("w_in", (D_MODEL, 4616), 1), ("w_branch_sgu", (SGU_W, D_MODEL), 1), ("w_branch_attn", (ATTN_W, D_MODEL), 1),
           ("w_out", (D_MODEL, D_MODEL), 0), ("w_up", (D_MODEL, D_FF), 1), ("w_down", (D_FF, D_MODEL), 0))
SMALL = (("g_mix_pre", (1, D_MODEL)), ("b_forget", (1, N_HEADS)), ("g_sgu", (1, SGU_W)), ("b_sgu", (1, SGU_W)),
         ("w_spatial", (N_GROUPS * CHUNK, CHUNK)), ("b_spatial", (N_GROUPS, CHUNK)), ("g_mix_post", (1, D_MODEL)),
         ("g_ffn_pre", (1, D_MODEL)), ("g_ffn_post", (1, D_MODEL)))
SMALL_ALIGN = 2 * ADD_ROWS


def _shard_shape(shape, axis):
    return tuple(s // N_CHIPS if a == axis else s for a, s in enumerate(shape))


def _slots_to_full(slots, axis):
    return slots.reshape(-1, slots.shape[2]) if axis == 0 else slots.transpose(1, 0, 2).reshape(slots.shape[1], -1)


def _full_to_slots(full, axis):
    if axis == 0:
        return full.reshape(N_CHIPS, -1, full.shape[1])
    return full.reshape(full.shape[0], N_CHIPS, -1).transpose(1, 0, 2)


def _small_rows(shape):
    return -(-(shape[0] * shape[1]) // (8 * LANES)) * 8


def _pack_small(values):
    parts = []
    for name, shape in SMALL:
        flat = values[name].reshape(-1)
        n = _small_rows(shape)
        parts.append(jnp.pad(flat, (0, n * LANES - flat.shape[0])).reshape(n, LANES))
    rows = sum(p.shape[0] for p in parts)
    pad = -(-rows // SMALL_ALIGN) * SMALL_ALIGN - rows
    return jnp.concatenate(parts + [jnp.zeros((pad, LANES), F32)], axis=0)


def _unpack_small(packed):
    out, row = {}, 0
    for name, shape in SMALL:
        n = _small_rows(shape)
        out[name] = packed[row:row + n].reshape(-1)[:shape[0] * shape[1]].reshape(shape)
        row += n
    return out


IN_Z, IN_Q, IN_K, IN_V, IN_F, IN_G, IN_END = 0, 1024, 1536, 2048, 2560, 2568, 4616


LATE_WEIGHTS = ("w_branch_sgu", "w_branch_attn", "w_out", "w_up", "w_down")
EARLY_GRADS = LATE_WEIGHTS


def _with_own_slot(shard, gathered, chip):
    return jnp.where(jnp.arange(N_CHIPS)[:, None, None] == chip, shard[None], gathered)


def _assemble(name, shard, gathered, chip):
    axis = {n: a for n, _, a in SHARDED}[name]
    return _slots_to_full(_with_own_slot(shard, gathered, chip), axis)


def _columns_from_slots(slots, bounds):
    width = slots.shape[2]
    pieces = []
    for lo, hi in zip(bounds[:-1], bounds[1:], strict=True):
        parts = [slots[k][:, max(lo, k * width) - k * width:min(hi, (k + 1) * width) - k * width]
                 for k in range(N_CHIPS) if max(lo, k * width) < min(hi, (k + 1) * width)]
        pieces.append(parts[0] if len(parts) == 1 else jnp.concatenate(parts, axis=1))
    return pieces


def _columns_to_slots(pieces):
    width = sum(p.shape[1] for p in pieces) // N_CHIPS
    slots = []
    for k in range(N_CHIPS):
        parts, start = [], 0
        for p in pieces:
            lo, hi = max(k * width, start), min((k + 1) * width, start + p.shape[1])
            if lo < hi:
                parts.append(p[:, lo - start:hi - start])
            start += p.shape[1]
        slots.append(jnp.concatenate(parts, axis=1))
    return jnp.stack(slots)


def _local_step(x, target, shards, small, place):
    b_forget = jnp.pad(small["b_forget"], ((0, 0), (0, LANES - N_HEADS)))
    causal = jnp.tril(jnp.ones((CHUNK, CHUNK), bool))
    ws = jnp.where(causal[None], small["w_spatial"].reshape(N_GROUPS, CHUNK, CHUNK), 0.0).astype(BF)
    ws_t = ws.transpose(0, 2, 1)
    bias_plane = jnp.repeat(small["b_spatial"].T, HEAD_DIM, axis=1)

    xn, (w_in_slots,) = _rms_fwd(x, small["g_mix_pre"], [shards["w_in"]])
    w_z, w_q, w_k, w_v, w_f, w_ga, w_gb = _columns_from_slots(
        _with_own_slot(shards["w_in"], w_in_slots, place[0]), (IN_Z, IN_Q, IN_K, IN_V, IN_F, IN_G, IN_G + D_MODEL, IN_END))
    w_qkv, w_g = jnp.concatenate([w_q, w_k, w_v], axis=1), jnp.concatenate([w_ga, w_gb], axis=1)
    w_f = jnp.pad(w_f, ((0, 0), (0, LANES - N_HEADS)))
    z = _matmul([(xn, w_z)], nt=False, out_dtypes=[F32], name="proj_z")
    qkv = _matmul([(xn, w_qkv)], nt=False, out_dtypes=[BF], name="proj_qkv")
    gl = _matmul([(xn, w_g)], nt=False, out_dtypes=[BF], name="proj_gate")
    fl = _matmul([(xn, w_f)], nt=False, out_dtypes=[F32], name="proj_forget")
    ysgu = _sgu_fwd(z, small["g_sgu"], small["b_sgu"], ws, bias_plane)
    qf, kl, vl, tile_stats = _attn_prep(qkv, fl, b_forget)
    first_key_tile, last_query_tile, bounded = _attn_ranges(tile_stats)
    yattn, yattn_f, ql, gathered = _attn_fwd(qf, kl, vl, first_key_tile, bounded, [shards[name] for name in LATE_WEIGHTS])
    w = {name: _assemble(name, shards[name], got, place[0]) for name, got in zip(LATE_WEIGHTS, gathered, strict=True)}
    a, b, merged = _branch_merge(ysgu, yattn, w["w_branch_sgu"], w["w_branch_attn"], gl)
    o, h1, xn2 = _matmul_rows(
        merged, w["w_out"], nt=False, rows=[x], vecs=[small["g_mix_post"], small["g_ffn_pre"]], row_outs=[F32, F32, BF],
        n_sums=0, epilogue=_mixer_out_fwd, name="proj_out_norms")

    def relu2(acc):
        r = jnp.maximum(acc, 0.0)
        return (r * r,)

    hid = _matmul([(xn2, w["w_up"])], nt=False, out_dtypes=[BF], name="ffn_up", epilogue=relu2, tm=FFN_ROWS)
    dy, ddn, sq, dg_ffn_post = _matmul_rows(
        hid, w["w_down"], nt=False, rows=[h1, target], vecs=[small["g_ffn_post"]], row_outs=[F32, BF], n_sums=2,
        epilogue=_loss_head, name="ffn_down_loss")

    dup = _matmul([(ddn, w["w_down"])], nt=True, out_dtypes=[BF], name="ffn_down_bwd", tm=FFN_ROWS,
                  epilogue=lambda acc, h: (acc * (2.0 * jnp.sqrt(h.astype(F32))),), extras=[hid])
    dw_down = _matmul_tn(hid, ddn, name="dw_down")
    dh1, do, dg_ffn_pre, dg_mix_post = _matmul_rows(
        dup, w["w_up"], nt=True, rows=[h1, dy, o], vecs=[small["g_ffn_pre"], small["g_mix_post"]], row_outs=[F32, BF],
        n_sums=2, epilogue=_mixer_out_bwd, name="ffn_up_bwd_norms")
    dw_up = _matmul_tn(xn2, dup, name="dw_up", slots=True)

    def gate_bwd(dm, a_t, b_t, gla, glb):
        ga, gb = jax.nn.sigmoid(gla.astype(F32)), jax.nn.sigmoid(glb.astype(F32))
        return dm * ga, dm * gb, dm * a_t.astype(F32) * (ga * (1.0 - ga)), dm * b_t.astype(F32) * (gb * (1.0 - gb))

    da, db, dgla, dglb = _matmul([(do, w["w_out"])], nt=True, out_dtypes=[BF] * 4, name="proj_out_bwd",
                                 epilogue=gate_bwd, extras=[a, b, (gl, 0), (gl, D_MODEL)])
    dw_out = _matmul_tn(merged, do, name="dw_out")
    dysgu = _matmul([(da, w["w_branch_sgu"])], nt=True, out_dtypes=[F32], name="branch_sgu_bwd")
    dyattn = _matmul([(db, w["w_branch_attn"])], nt=True, out_dtypes=[F32], name="branch_attn_bwd")
    dw_bs = _matmul_tn(ysgu, da, name="dw_branch_sgu")
    dw_ba = _matmul_tn(yattn, db, name="dw_branch_attn")
    early = {"w_branch_sgu": _full_to_slots(dw_bs, 1), "w_branch_attn": _full_to_slots(dw_ba, 1),
             "w_out": _full_to_slots(dw_out, 0), "w_up": dw_up, "w_down": _full_to_slots(dw_down, 0)}
    (dz, dws, dbs, dg_sgu, db_sgu), early_theirs = _sgu_bwd(
        dysgu, z, small["g_sgu"], small["b_sgu"], ws, ws_t, bias_plane, [early[name] for name in EARLY_GRADS])
    early_sums = {name: _add_sibling(early[name], theirs, place, name="add_sibling_" + name)
                  for name, theirs in zip(EARLY_GRADS, early_theirs, strict=True)}
    dout = _attn_bwd_prep(dyattn, yattn_f)
    (dq, dk, dv, ext_q, ext_k), early_received = _attn_bwd(
        kl, vl, ql, dout, last_query_tile, [early_sums[name][1] for name in EARLY_GRADS])
    dfl, dbf = _forget_bwd(ext_q, ext_k, fl, b_forget)
    dw_in = _columns_to_slots(
        [_matmul_tn(xn, dz, name="dw_in_z"), _matmul_tn(xn, dq, name="dw_in_q"), _matmul_tn(xn, dk, name="dw_in_k"),
         _matmul_tn(xn, dv, name="dw_in_v"), _matmul_tn(xn, dfl, name="dw_in_f")[:, :N_HEADS],
         _matmul_tn(xn, dgla, name="dw_in_ga"), _matmul_tn(xn, dglb, name="dw_in_gb")])
    (dw_in_theirs,) = _exchange_halves([dw_in], name="exchange_halves_w_in")
    dw_in_sum = _add_sibling(dw_in, dw_in_theirs, place, name="add_sibling_w_in")
    (dxn,), (dw_in_received,) = _matmul(
        [(dz, w_z), (dq, w_q), (dk, w_k), (dv, w_v), (dgla, w_ga), (dglb, w_gb), (dfl, w_f)],
        nt=True, out_dtypes=[F32], name="proj_in_bwd", scatter=[dw_in_sum[1]])
    dx, dg_mix_pre = _input_norm_bwd(x, dxn, dh1, small["g_mix_pre"])

    reduced = {name: (early_sums[name][0], got) for name, got in zip(EARLY_GRADS, early_received, strict=True)}
    reduced["w_in"] = (dw_in_sum[0], dw_in_received)
    small_grads = {"g_mix_pre": dg_mix_pre, "b_forget": dbf[:, :N_HEADS], "g_sgu": dg_sgu, "b_sgu": db_sgu,
                   "w_spatial": dws.reshape(N_GROUPS * CHUNK, CHUNK), "b_spatial": dbs[:, :N_GROUPS].T,
                   "g_mix_post": dg_mix_post, "g_ffn_pre": dg_ffn_pre, "g_ffn_post": dg_ffn_post}
    return sq, dx, reduced, small_grads


NAMES = ("g_mix_pre", "w_in", "b_forget", "g_sgu", "b_sgu", "w_spatial", "b_spatial", "w_branch_sgu", "w_branch_attn",
         "w_out", "g_mix_post", "g_ffn_pre", "w_up", "w_down", "g_ffn_post")


def kernel(x, g_mix_pre, w_in, b_forget, g_sgu, b_sgu, w_spatial, b_spatial, w_branch_sgu, w_branch_attn, w_out, g_mix_post, g_ffn_pre, w_up, w_down, g_ffn_post, loss_target, m_g_mix_pre, m_w_in, m_b_forget, m_g_sgu, m_b_sgu, m_w_spatial, m_b_spatial, m_w_branch_sgu, m_w_branch_attn, m_w_out, m_g_mix_post, m_g_ffn_pre, m_w_up, m_w_down, m_g_ffn_post, v_g_mix_pre, v_w_in, v_b_forget, v_g_sgu, v_b_sgu, v_w_spatial, v_b_spatial, v_w_branch_sgu, v_w_branch_attn, v_w_out, v_g_mix_post, v_g_ffn_pre, v_w_up, v_w_down, v_g_ffn_post):
    weights = dict(zip(NAMES, (g_mix_pre, w_in, b_forget, g_sgu, b_sgu, w_spatial, b_spatial, w_branch_sgu, w_branch_attn,
                               w_out, g_mix_post, g_ffn_pre, w_up, w_down, g_ffn_post), strict=True))
    first = dict(zip(NAMES, (m_g_mix_pre, m_w_in, m_b_forget, m_g_sgu, m_b_sgu, m_w_spatial, m_b_spatial, m_w_branch_sgu,
                             m_w_branch_attn, m_w_out, m_g_mix_post, m_g_ffn_pre, m_w_up, m_w_down, m_g_ffn_post), strict=True))
    second = dict(zip(NAMES, (v_g_mix_pre, v_w_in, v_b_forget, v_g_sgu, v_b_sgu, v_w_spatial, v_b_spatial, v_w_branch_sgu,
                              v_w_branch_attn, v_w_out, v_g_mix_post, v_g_ffn_pre, v_w_up, v_w_down, v_g_ffn_post), strict=True))
    shard_shapes = {name: _shard_shape(shape, axis) for name, shape, axis in SHARDED}
    small_shapes = dict(SMALL)
    view = lambda name, a: a.reshape(shard_shapes.get(name) or small_shapes[name])

    place = jnp.stack([2 * lax.axis_index("x") + lax.axis_index("y"), lax.axis_index("c")]).astype(jnp.int32)

    shards = {name: view(name, weights[name]).astype(BF) for name, _, _ in SHARDED}
    small = {name: view(name, weights[name]) for name, _ in SMALL}
    sq, dx, reduced, small_grads = _local_step(x[0], loss_target[0], shards, small, place)
    loss = lax.psum(0.5 * jnp.sum(sq) / D_MODEL, ("x", "y", "c"))

    small_mine = _pack_small(small_grads)
    (small_theirs,) = _exchange_halves([small_mine], name="exchange_halves_small")
    small_sum, _ = _add_sibling(small_mine, small_theirs, place, name="add_sibling_small")
    (small_received,) = _scatter_to_owners([small_sum])
    totals = {name: _add_chips(s, r, place, name="add_chips_" + name, own_slots=True) for name, (s, r) in reduced.items()}
    small_total = _add_chips(small_sum, small_received, place, name="add_chips_small", own_slots=False)
    joined = _join_halves([totals[name] for name, _, _ in SHARDED] + [small_total])
    grad = {**{name: g for (name, _, _), g in zip(SHARDED, joined[:-1], strict=True)}, **_unpack_small(joined[-1])}

    grad_out, delta, new_m, new_v = {}, {}, {}, {}
    for name in NAMES:
        rows, cols = grad[name].shape
        as_given = lambda a: a.reshape(1, rows, cols)
        grad_out[name], delta[name], new_m[name], new_v[name] = _adamw(
            as_given(weights[name]), grad[name], as_given(first[name]), as_given(second[name]), name="adamw_" + name)

    like = lambda d: [d[name].reshape(weights[name].shape) for name in NAMES]
    return (loss, dx[None], *like(grad_out), *like(delta), *like(new_m), *like(new_v))
```

```python
import functools

import jax
import jax.numpy as jnp
import numpy as np
from jax import lax
from jax.experimental import pallas as pl
from jax.experimental.pallas import tpu as pltpu

F32 = jnp.float32
BF = jnp.bfloat16
MESH = pl.DeviceIdType.MESH

D_MODEL = 1024
N_HEADS = 8
HEAD_DIM = 64
ATTN_W = N_HEADS * HEAD_DIM
SGU_W = 512
N_GROUPS = 8
CHUNK = 128
D_FF = 4096
EPS = 1e-6
Q_SCALE = HEAD_DIM ** -0.5
N_CHIPS = 4
LANES = 128

ADAM_LR = 0.001
ADAM_B1 = 0.9
ADAM_B2 = 0.999
ADAM_EPS = 1e-08
ADAM_WD = 0.01
ADAM_STEP = 10

VMEM_LIMIT = 48 * 1024 * 1024
BIG_VMEM = 58 * 1024 * 1024
NEG = -1e30

LANE_ROWSUM = HEAD_DIM
LANE_COLSUM = HEAD_DIM + 3


def _params(*sem):
    return pltpu.CompilerParams(dimension_semantics=sem, vmem_limit_bytes=VMEM_LIMIT)


def _dot(a, b):
    return jnp.dot(a, b, preferred_element_type=F32)


def _dot_nt(a, b):
    return lax.dot_general(a, b, (((1,), (1,)), ((), ())), preferred_element_type=F32)


def _dot_tn(a, b):
    return lax.dot_general(a, b, (((0,), (0,)), ((), ())), preferred_element_type=F32)


def _split3(c):
    hi = c.astype(BF).astype(F32)
    r = c - hi
    mid = r.astype(BF).astype(F32)
    lo = (r - mid).astype(BF).astype(F32)
    return hi, mid, lo


def _gelu(x):
    k = 0.7978845608028654
    return 0.5 * x * (1.0 + jnp.tanh(k * (x + 0.044715 * (x * x * x))))


def _gelu_grad(x):
    k = 0.7978845608028654
    x2 = x * x
    t = jnp.tanh(k * (x + 0.044715 * (x2 * x)))
    return 0.5 * (1.0 + t) + 0.5 * x * (1.0 - t * t) * (k * (1.0 + 3.0 * 0.044715 * x2))


def _rms_bwd(a, g, dy):
    r = lax.rsqrt(jnp.mean(a * a, axis=-1, keepdims=True) + EPS)
    n = a * r
    dn = dy * g
    da = r * (dn - n * jnp.mean(dn * n, axis=-1, keepdims=True))
    return da, dy * n


MM_ROWS = 1024
MM_COLS = 512
FFN_ROWS = 2048


def _matmul(pairs, *, nt, out_dtypes, name, tm=MM_ROWS, tn=MM_COLS, epilogue=None, extras=(), scatter=()):
    n_pairs, n_extra, n_out, n_scatter = len(pairs), len(extras), len(out_dtypes), len(scatter)
    M = pairs[0][0].shape[0]
    N = pairs[0][1].shape[0] if nt else pairs[0][1].shape[1]
    tm, tn = min(tm, M), min(tn, N)
    assert M % tm == 0 and N % tn == 0
    grid = (M // tm, N // tn)

    def body(*refs):
        n_in = 2 * n_pairs + n_extra
        if n_scatter:
            step = pl.program_id(0) * grid[1] + pl.program_id(1)
            first = n_in + n_scatter + n_out
            scatter_start, scatter_finish = _scatter_phases(
                refs[n_in:n_in + n_scatter], refs[first:first + n_scatter], *refs[first + n_scatter:])
            pl.when(step == 0)(scatter_start)
        acc = None
        for p in range(n_pairs):
            a_ref, b_ref = refs[2 * p], refs[2 * p + 1]
            d = _dot_nt(a_ref[...], b_ref[...]) if nt else _dot(a_ref[...], b_ref[...])
            acc = d if acc is None else acc + d
        e_refs = refs[2 * n_pairs:n_in]
        o_refs = refs[n_in + n_scatter:n_in + n_scatter + n_out]
        outs = (acc,) if epilogue is None else epilogue(acc, *[e[...] for e in e_refs])
        for o_ref, o in zip(o_refs, outs, strict=True):
            o_ref[...] = o.astype(o_ref.dtype)
        if n_scatter:
            pl.when(step == grid[0] * grid[1] - 1)(scatter_finish)

    in_specs, args = [], []
    for a, b in pairs:
        K = a.shape[1]
        in_specs.append(pl.BlockSpec((tm, K), lambda i, j: (i, 0)))
        in_specs.append(pl.BlockSpec((tn, K), lambda i, j: (j, 0)) if nt else pl.BlockSpec((K, tn), lambda i, j: (0, j)))
        args += [a, b]
    for e in extras:
        e, col = e if isinstance(e, tuple) else (e, 0)
        in_specs.append(pl.BlockSpec((tm, tn), functools.partial(lambda i, j, off: (i, j + off), off=col // tn)))
        args.append(e)
    order = ("arbitrary", "arbitrary") if n_scatter else ("parallel", "parallel")
    outs = pl.pallas_call(
        body, name=name, grid=grid, in_specs=in_specs + [HBM] * n_scatter,
        out_specs=[pl.BlockSpec((tm, tn), lambda i, j: (i, j)) for _ in out_dtypes] + [HBM] * n_scatter,
        out_shape=[jax.ShapeDtypeStruct((M, N), dt) for dt in out_dtypes] + (_scattered_shapes(scatter) if n_scatter else []),
        scratch_shapes=_scatter_semaphores(n_scatter) if n_scatter else [],
        compiler_params=_params(*order),
    )(*args, *scatter)
    if n_scatter:
        return outs[:n_out], outs[n_out:]
    return outs if len(outs) > 1 else outs[0]


def _matmul_tn(a, b, *, name, tm=1024, tn=1024, tk=2048, slots=False):
    T, K1 = a.shape
    N = b.shape[1]
    tm, tn, tk = min(tm, K1), min(tn, N // N_CHIPS if slots else N), min(tk, T)
    assert K1 % tm == 0 and (N // N_CHIPS if slots else N) % tn == 0 and T % tk == 0
    per_slot = N // N_CHIPS // tn

    def body(a_ref, b_ref, o_ref):
        @pl.when(pl.program_id(2) == 0)
        def _():
            o_ref[...] = jnp.zeros_like(o_ref)

        o_ref[...] += _dot_tn(a_ref[...], b_ref[...])

    if slots:
        out_spec = pl.BlockSpec((None, tm, tn), lambda i, j, k: (j // per_slot, i, j % per_slot))
        out_shape = jax.ShapeDtypeStruct((N_CHIPS, K1, N // N_CHIPS), F32)
    else:
        out_spec = pl.BlockSpec((tm, tn), lambda i, j, k: (i, j))
        out_shape = jax.ShapeDtypeStruct((K1, N), F32)
    return pl.pallas_call(
        body, name=name, grid=(K1 // tm, N // tn, T // tk),
        in_specs=[pl.BlockSpec((tk, tm), lambda i, j, k: (k, i)), pl.BlockSpec((tk, tn), lambda i, j, k: (k, j))],
        out_specs=out_spec, out_shape=out_shape,
        compiler_params=_params("parallel", "parallel", "arbitrary"),
    )(a, b)


def _branch_merge(ysgu, yattn, w_bs, w_ba, gl, *, tm=MM_ROWS, tn=MM_COLS):
    T = ysgu.shape[0]
    tm = min(tm, T)
    nj = D_MODEL // tn

    def body(ys_ref, ya_ref, wbs_ref, wba_ref, gla_ref, glb_ref, a_ref, b_ref, m_ref):
        a = _dot(ys_ref[...], wbs_ref[...])
        b = _dot(ya_ref[...], wba_ref[...])
        a_ref[...] = a.astype(BF)
        b_ref[...] = b.astype(BF)
        m_ref[...] = (jax.nn.sigmoid(gla_ref[...].astype(F32)) * a + jax.nn.sigmoid(glb_ref[...].astype(F32)) * b).astype(BF)

    return pl.pallas_call(
        body, name="branch_merge", grid=(T // tm, nj),
        in_specs=[
            pl.BlockSpec((tm, SGU_W), lambda i, j: (i, 0)),
            pl.BlockSpec((tm, ATTN_W), lambda i, j: (i, 0)),
            pl.BlockSpec((SGU_W, tn), lambda i, j: (0, j)),
            pl.BlockSpec((ATTN_W, tn), lambda i, j: (0, j)),
            pl.BlockSpec((tm, tn), lambda i, j: (i, j)),
            pl.BlockSpec((tm, tn), lambda i, j: (i, j + nj)),
        ],
        out_specs=[pl.BlockSpec((tm, tn), lambda i, j: (i, j))] * 3,
        out_shape=[jax.ShapeDtypeStruct((T, D_MODEL), BF)] * 3,
        compiler_params=_params("parallel", "parallel"),
    )(ysgu, yattn, w_bs, w_ba, gl, gl)


def _row_spec(tr, width):
    return pl.BlockSpec((tr, width), lambda i: (i, 0))


def _vec_spec(width):
    return pl.BlockSpec((1, width), lambda i: (0, 0))


def _rms_fwd(x, g, shards, *, tr=256):
    T = x.shape[0]
    tr = min(tr, T)
    n_steps = T // tr
    k = len(shards)

    def body(x_ref, g_ref, *refs):
        step = pl.program_id(0)
        gather_start, gather_forward, gather_finish = _gather_phases(refs[:k], refs[k + 1:2 * k + 1], *refs[2 * k + 1:])
        pl.when(step == 0)(gather_start)
        pl.when(step == (3 * n_steps) // 4)(gather_forward)
        xv = x_ref[...]
        r = lax.rsqrt(jnp.mean(xv * xv, axis=-1, keepdims=True) + EPS)
        refs[k][...] = ((xv * r) * g_ref[...]).astype(BF)
        pl.when(step == n_steps - 1)(gather_finish)

    outs = pl.pallas_call(
        body, name="rms_fwd", grid=(n_steps,),
        in_specs=[_row_spec(tr, D_MODEL), _vec_spec(D_MODEL)] + [HBM] * k, out_specs=[_row_spec(tr, D_MODEL)] + [HBM] * k,
        out_shape=[jax.ShapeDtypeStruct((T, D_MODEL), BF)] + _gathered_shapes(shards),
        scratch_shapes=_gather_semaphores(k), compiler_params=_params("arbitrary"),
    )(x, g, *shards)
    return outs[0], outs[1:]


def _mixer_out_fwd(o, x, g_post, g_pre):
    r = lax.rsqrt(jnp.mean(o * o, axis=-1, keepdims=True) + EPS)
    h1 = x + (o * r) * g_post
    r2 = lax.rsqrt(jnp.mean(h1 * h1, axis=-1, keepdims=True) + EPS)
    return o, h1, (h1 * r2) * g_pre


def _matmul_rows(pairs, *, nt, rows, vecs, row_outs, n_sums, epilogue, name, tm=512, scatter=()):
    M = pairs[0][0].shape[0]
    N = pairs[0][1].shape[0] if nt else pairs[0][1].shape[1]
    tm = min(tm, M)
    n_steps = M // tm
    n_pairs, n_rows, n_vecs, n_out, n_scatter = len(pairs), len(rows), len(vecs), len(row_outs), len(scatter)

    def body(*refs):
        groups, at = [], 2 * n_pairs
        for count in (n_rows, n_vecs, n_scatter, n_out, n_sums, n_scatter):
            groups.append(refs[at:at + count])
            at += count
        r_refs, v_refs, b_refs, o_refs, s_refs, got_refs = groups
        sems = refs[at:]
        step = pl.program_id(0)
        if n_scatter:
            scatter_start, scatter_finish = _scatter_phases(b_refs, got_refs, *sems)
            pl.when(step == 0)(scatter_start)

        @pl.when(step == 0)
        def _():
            for s_ref in s_refs:
                s_ref[...] = jnp.zeros_like(s_ref)

        acc = None
        for p in range(n_pairs):
            a_ref, b_ref = refs[2 * p], refs[2 * p + 1]
            d = _dot_nt(a_ref[...], b_ref[...]) if nt else _dot(a_ref[...], b_ref[...])
            acc = d if acc is None else acc + d
        outs = epilogue(acc, *[r[...] for r in r_refs], *[v[...] for v in v_refs])
        for o_ref, o in zip(o_refs, outs[:n_out], strict=True):
            o_ref[...] = o.astype(o_ref.dtype)
        for s_ref, term in zip(s_refs, outs[n_out:], strict=True):
            s_ref[...] += jnp.sum(term, axis=0, keepdims=True)
        if n_scatter:
            pl.when(step == n_steps - 1)(scatter_finish)

    in_specs, args = [], []
    for a, b in pairs:
        in_specs += [_row_spec(tm, a.shape[1]), pl.BlockSpec(b.shape, lambda i: (0, 0))]
        args += [a, b]
    outs = pl.pallas_call(
        body, name=name, grid=(n_steps,),
        in_specs=in_specs + [_row_spec(tm, N)] * n_rows + [_vec_spec(N)] * n_vecs + [HBM] * n_scatter,
        out_specs=[_row_spec(tm, N)] * n_out + [_vec_spec(N)] * n_sums + [HBM] * n_scatter,
        out_shape=[jax.ShapeDtypeStruct((M, N), dt) for dt in row_outs] + [jax.ShapeDtypeStruct((1, N), F32)] * n_sums
        + (_scattered_shapes(scatter) if n_scatter else []),
        scratch_shapes=_scatter_semaphores(n_scatter) if n_scatter else [],
        compiler_params=pltpu.CompilerParams(dimension_semantics=("arbitrary",), vmem_limit_bytes=BIG_VMEM),
    )(*args, *rows, *vecs, *scatter)
    return outs


def _loss_head(dn, h1, target, g):
    r = lax.rsqrt(jnp.mean(dn * dn, axis=-1, keepdims=True) + EPS)
    err = h1 + (dn * r) * g - target
    dy = err * (1.0 / D_MODEL)
    ddn, dg_terms = _rms_bwd(dn, g, dy)
    return dy, ddn, err * err, dg_terms


def _mixer_out_bwd(dxn2, h1, dy, o, g_pre, g_post):
    da, dg_pre_terms = _rms_bwd(h1, g_pre, dxn2)
    dh1 = dy + da
    do, dg_post_terms = _rms_bwd(o, g_post, dh1)
    return dh1, do, dg_pre_terms, dg_post_terms


def _input_norm_bwd(dxn, x, dh1, g):
    da, dg_terms = _rms_bwd(x, g, dxn)
    return dh1 + da, dg_terms


def _sgu_norm(z_tile, g, b):
    gz = _gelu(z_tile)
    u, vv = gz[:, :SGU_W], gz[:, SGU_W:]
    xc = vv - jnp.mean(vv, axis=-1, keepdims=True)
    rstd = lax.rsqrt(jnp.mean(xc * xc, axis=-1, keepdims=True) + EPS)
    xhat = xc * rstd
    return u, xhat, rstd, xhat * g + b


def _sgu_mix(w_ref, v_bf, first_half):
    parts = []
    for p in range(N_GROUPS // 2):
        vp = v_bf[:, p * LANES:(p + 1) * LANES]
        parts.append(jnp.where(first_half, _dot(w_ref[2 * p], vp), _dot(w_ref[2 * p + 1], vp)))
    return jnp.concatenate(parts, axis=1)


def _sgu_fwd(z, g_sgu, b_sgu, ws, bias_plane, *, tm=512):
    T = z.shape[0]
    tm = min(tm, T)

    def body(z_ref, g_ref, b_ref, ws_ref, bp_ref, y_ref):
        u, _, _, vn = _sgu_norm(z_ref[...], g_ref[...], b_ref[...])
        vn_bf = vn.astype(BF)
        first_half = lax.broadcasted_iota(jnp.int32, (CHUNK, LANES), 1) < HEAD_DIM
        for c in range(tm // CHUNK):
            rows = slice(c * CHUNK, (c + 1) * CHUNK)
            s = _sgu_mix(ws_ref, vn_bf[rows, :], first_half) + bp_ref[...]
            y_ref[rows, :] = (u[rows, :] * s).astype(BF)

    return pl.pallas_call(
        body, name="sgu_fwd", grid=(T // tm,),
        in_specs=[_row_spec(tm, 2 * SGU_W), _vec_spec(SGU_W), _vec_spec(SGU_W),
                  pl.BlockSpec((N_GROUPS, CHUNK, CHUNK), lambda i: (0, 0, 0)),
                  pl.BlockSpec((CHUNK, SGU_W), lambda i: (0, 0))],
        out_specs=_row_spec(tm, SGU_W), out_shape=jax.ShapeDtypeStruct((T, SGU_W), BF),
        compiler_params=_params("parallel"),
    )(z, g_sgu, b_sgu, ws, bias_plane)


def _sgu_bwd(dy, z, g_sgu, b_sgu, ws, ws_t, bias_plane, exchange, *, tm=512):
    T = z.shape[0]
    tm = min(tm, T)
    n_steps = T // tm
    k = len(exchange)

    def body(dy_ref, z_ref, g_ref, b_ref, ws_ref, wst_ref, bp_ref, *refs):
        x_refs, (dz_ref, dws_ref, dbs_ref, dg_ref, db_ref), r_refs = refs[:k], refs[k:k + 5], refs[k + 5:2 * k + 5]
        dbp_ref, send_sems, recv_sems = refs[2 * k + 5:]
        step = pl.program_id(0)
        exchange_start, exchange_finish = _exchange_phases(x_refs, r_refs, send_sems, recv_sems)
        pl.when(step == 0)(exchange_start)

        @pl.when(step == 0)
        def _():
            dws_ref[...] = jnp.zeros_like(dws_ref)
            dg_ref[...] = jnp.zeros_like(dg_ref)
            db_ref[...] = jnp.zeros_like(db_ref)
            dbp_ref[...] = jnp.zeros_like(dbp_ref)

        g = g_ref[...]
        zt = z_ref[...]
        u, xhat, rstd, vn = _sgu_norm(zt, g, b_ref[...])
        vn_bf = vn.astype(BF)
        first_half = lax.broadcasted_iota(jnp.int32, (CHUNK, LANES), 1) < HEAD_DIM
        dyv = dy_ref[...]
        dg_acc = jnp.zeros((1, SGU_W), F32)
        db_acc = jnp.zeros((1, SGU_W), F32)
        for c in range(tm // CHUNK):
            rows = slice(c * CHUNK, (c + 1) * CHUNK)
            v_c = vn_bf[rows, :]
            s = _sgu_mix(ws_ref, v_c, first_half) + bp_ref[...]
            dy_c = dyv[rows, :]
            du = dy_c * s
            dsv = dy_c * u[rows, :]
            dbp_ref[...] += dsv
            ds_bf = dsv.astype(BF)
            zero = jnp.zeros((CHUNK, LANES), BF)
            for p in range(N_GROUPS // 2):
                dsp = ds_bf[:, p * LANES:(p + 1) * LANES]
                vp = v_c[:, p * LANES:(p + 1) * LANES]
                dws_ref[2 * p] += _dot_nt(jnp.where(first_half, dsp, zero), vp)
                dws_ref[2 * p + 1] += _dot_nt(jnp.where(first_half, zero, dsp), vp)
            dvn = _sgu_mix(wst_ref, ds_bf, first_half)
            xh = xhat[rows, :]
            dxh = dvn * g
            dvv = rstd[rows, :] * (dxh - jnp.mean(dxh, axis=-1, keepdims=True)
                                   - xh * jnp.mean(dxh * xh, axis=-1, keepdims=True))
            dg_acc += jnp.sum(dvn * xh, axis=0, keepdims=True)
            db_acc += jnp.sum(dvn, axis=0, keepdims=True)
            dgz = jnp.concatenate([du, dvv], axis=1)
            dz_ref[rows, :] = (dgz * _gelu_grad(zt[rows, :])).astype(BF)
        dg_ref[...] += dg_acc
        db_ref[...] += db_acc

        @pl.when(step == n_steps - 1)
        def _():
            r = lax.broadcasted_iota(jnp.int32, (CHUNK, CHUNK), 0)
            cidx = lax.broadcasted_iota(jnp.int32, (CHUNK, CHUNK), 1)
            causal = (cidx <= r).astype(F32)
            for gi in range(N_GROUPS):
                dws_ref[gi] = dws_ref[gi] * causal
            lane = lax.broadcasted_iota(jnp.int32, (CHUNK, LANES), 1)
            out = jnp.zeros((CHUNK, LANES), F32)
            dbp = dbp_ref[...]
            for gi in range(N_GROUPS):
                col = jnp.sum(dbp[:, gi * HEAD_DIM:(gi + 1) * HEAD_DIM], axis=1, keepdims=True)
                out = jnp.where(lane == gi, col, out)
            dbs_ref[...] = out
            exchange_finish()

    w_spec = pl.BlockSpec((N_GROUPS, CHUNK, CHUNK), lambda i: (0, 0, 0))
    plane = pl.BlockSpec((CHUNK, SGU_W), lambda i: (0, 0))
    outs = pl.pallas_call(
        body, name="sgu_bwd", grid=(n_steps,),
        in_specs=[_row_spec(tm, SGU_W), _row_spec(tm, 2 * SGU_W), _vec_spec(SGU_W), _vec_spec(SGU_W), w_spec, w_spec, plane]
        + [HBM] * k,
        out_specs=[_row_spec(tm, 2 * SGU_W), w_spec, pl.BlockSpec((CHUNK, LANES), lambda i: (0, 0)),
                   _vec_spec(SGU_W), _vec_spec(SGU_W)] + [HBM] * k,
        out_shape=[jax.ShapeDtypeStruct((T, 2 * SGU_W), BF), jax.ShapeDtypeStruct((N_GROUPS, CHUNK, CHUNK), F32),
                   jax.ShapeDtypeStruct((CHUNK, LANES), F32), jax.ShapeDtypeStruct((1, SGU_W), F32),
                   jax.ShapeDtypeStruct((1, SGU_W), F32)] + _exchanged_shapes(exchange),
        scratch_shapes=[pltpu.VMEM((CHUNK, SGU_W), F32)] + _exchange_semaphores(k),
        compiler_params=_params("arbitrary"),
    )(dy, z, g_sgu, b_sgu, ws, ws_t, bias_plane, *exchange)
    return outs[:5], outs[5:]


def _tri(n, upper):
    r = lax.broadcasted_iota(jnp.int32, (n, n), 0)
    c = lax.broadcasted_iota(jnp.int32, (n, n), 1)
    return ((c >= r) if upper else (c <= r)).astype(BF)


def _scan_dot(tri, x):
    hi, mid, lo = _split3(x)
    return (_dot(tri, hi.astype(BF)) + _dot(tri, mid.astype(BF))) + _dot(tri, lo.astype(BF))


def _with_lanes(base, lane, start, cols):
    out = base
    for k, col in enumerate(cols):
        if col is not None:
            out = jnp.where(lane == start + k, col, out)
    return out


def _logit_bound(q_norm, k_norm):
    return NORM_SLACK * q_norm * k_norm + 1.0


ATTN_TILE = 512
SKIP_BELOW = -110.0
NORM_SLACK = 1.001
BOUNDED_GAP = 60.0


def _attn_prep(qkv, fl, b_forget, *, tp=ATTN_TILE):
    T = qkv.shape[0]
    tp = min(tp, T)
    head_sum, gather6, place_q, place_k, place_v = (jnp.asarray(m, BF) for m in _attn_placements())

    def body(qkv_ref, fl_ref, bf_ref, hs_ref, g6_ref, pq_ref, pk_ref, pv_ref, qf_ref, kl_ref, vl_ref, st_ref, carry_ref, kmax_ref):
        @pl.when(pl.program_id(0) == 0)
        def _():
            carry_ref[...] = jnp.zeros_like(carry_ref)
            kmax_ref[...] = jnp.zeros_like(kmax_ref)

        x = fl_ref[...] + bf_ref[...]
        logf = jnp.minimum(x, 0.0) - jnp.log(1.0 + jnp.exp(-jnp.abs(x)))
        cum = _scan_dot(_tri(tp, upper=False), logf) + carry_ref[...]
        carry_ref[...] = cum[tp - 1:tp, :]

        def head_norms(block):
            sq = block * block
            hi = sq.astype(BF)
            return _dot(hi, hs_ref[...]) + _dot((sq - hi.astype(F32)).astype(BF), hs_ref[...])

        qkvv = qkv_ref[...]
        q_norm = NORM_SLACK * jnp.sqrt(head_norms(qkvv[:, :ATTN_W].astype(F32) * Q_SCALE))
        kn = NORM_SLACK * jnp.sqrt(jnp.max(head_norms(qkvv[:, ATTN_W:2 * ATTN_W].astype(F32)), axis=0, keepdims=True))
        k_seen = jnp.maximum(kmax_ref[...], kn)
        kmax_ref[...] = k_seen
        rows = (jnp.max(q_norm, axis=0, keepdims=True), kn, jnp.max(cum, axis=0, keepdims=True),
                jnp.min(cum, axis=0, keepdims=True), k_seen)
        st_ref[...] = jnp.zeros_like(st_ref)
        for k, row in enumerate(rows):
            st_ref[0, k:k + 1, :] = row
        parts = jnp.concatenate([p.astype(BF) for p in _split3(cum) + _split3(-_logit_bound(q_norm, k_seen))], axis=1)
        lane = lax.broadcasted_iota(jnp.int32, (tp, LANES), 1)
        side = jnp.where(lane == 6 * N_HEADS, 1.0, _dot(parts, g6_ref[...])).astype(BF)
        for h in range(N_HEADS):
            pair = slice((h // 2) * LANES, (h // 2 + 1) * LANES)
            for out_ref, block, place_ref in ((qf_ref, qkvv[:, :ATTN_W], pq_ref), (kl_ref, qkvv[:, ATTN_W:2 * ATTN_W], pk_ref),
                                              (vl_ref, qkvv[:, 2 * ATTN_W:], pv_ref)):
                out_ref[h] = _dot(jnp.concatenate([block[:, pair], side], axis=1), place_ref[h]).astype(BF)

    head_spec = pl.BlockSpec((N_HEADS, tp, LANES), lambda i: (0, i, 0))
    whole = lambda a: pl.BlockSpec(a.shape, lambda i: (0,) * a.ndim)
    return pl.pallas_call(
        body, name="attn_prep", grid=(T // tp,),
        in_specs=[_row_spec(tp, 3 * ATTN_W), _row_spec(tp, LANES), _vec_spec(LANES)]
        + [whole(m) for m in (head_sum, gather6, place_q, place_k, place_v)],
        out_specs=[head_spec] * 3 + [pl.BlockSpec((1, N_HEADS, LANES), lambda i: (i, 0, 0))],
        out_shape=[jax.ShapeDtypeStruct((N_HEADS, T, LANES), BF)] * 3 + [jax.ShapeDtypeStruct((T // tp, N_HEADS, LANES), F32)],
        scratch_shapes=[pltpu.VMEM((1, LANES), F32), pltpu.VMEM((1, LANES), F32)], compiler_params=_params("arbitrary"),
    )(qkv, fl, b_forget, head_sum, gather6, place_q, place_k, place_v)


def _attn_placements():
    head_sum = np.zeros((ATTN_W, LANES), np.float32)
    head_sum[np.arange(ATTN_W), np.arange(ATTN_W) // HEAD_DIM] = 1.0
    gather6 = np.zeros((6 * LANES, LANES), np.float32)
    for j in range(6):
        gather6[j * LANES + np.arange(N_HEADS), j * N_HEADS + np.arange(N_HEADS)] = 1.0
    place = np.zeros((3, N_HEADS, 2 * LANES, LANES), np.float32)
    one = LANES + 6 * N_HEADS
    d = np.arange(HEAD_DIM)
    for h in range(N_HEADS):
        side = lambda j: LANES + j * N_HEADS + h
        place[0, h, (h % 2) * HEAD_DIM + d, d] = Q_SCALE
        place[1:, h, (h % 2) * HEAD_DIM + d, d] = 1.0
        for j in range(3):
            place[0, h, side(j), HEAD_DIM + j] = 1.0
            place[0, h, one, HEAD_DIM + 3 + j] = 1.0
            place[0, h, side(3 + j), HEAD_DIM + 6 + j] = 1.0
            place[1, h, one, HEAD_DIM + j] = 1.0
            place[1, h, side(j), HEAD_DIM + 3 + j] = -1.0
            place[1, h, one, HEAD_DIM + 6 + j] = 1.0
            place[2, h, one, HEAD_DIM + j] = 1.0
    return head_sum, gather6, place[0], place[1], place[2]


def _attn_ranges(stats):
    qn, kn, cmax, cmin, k_seen = (stats[:, k, :N_HEADS].T for k in range(5))
    n = qn.shape[1]
    bounded = (2.0 * _logit_bound(qn, k_seen) <= BOUNDED_GAP).reshape(N_HEADS // 2, 2, n).all(axis=1)
    reach = NORM_SLACK * qn * (jnp.max(kn, axis=1, keepdims=True) + kn) + cmax
    i = jnp.arange(n)[None, :, None]
    j = jnp.arange(n)[None, None, :]
    need = ((reach[:, :, None] - cmin[:, None, :] >= SKIP_BELOW) | (i == j)) & (j <= i)
    first = jnp.min(jnp.where(need, j, n), axis=2).reshape(N_HEADS // 2, 2, n).min(axis=1)
    last = jnp.max(jnp.where(need, i, -1), axis=1).reshape(N_HEADS // 2, 2, n).max(axis=1)
    return first.reshape(-1).astype(F32), last.reshape(-1).astype(F32), bounded.reshape(-1).astype(F32)


def _pair_block(t):
    return pl.BlockSpec((2, t, LANES), lambda p, i, *_: (p, i, 0))


def _pair_full(T):
    return pl.BlockSpec((2, T, LANES), lambda p, i, *_: (p, 0, 0))


def _packed_block(t):
    return pl.BlockSpec((t, LANES), lambda p, i, *_: (i, p))


def _causal(t, keys_in_rows=False):
    r = lax.broadcasted_iota(jnp.int32, (t, t), 0)
    c = lax.broadcasted_iota(jnp.int32, (t, t), 1)
    return (r <= c) if keys_in_rows else (c <= r)


def _tile_rows(j, t):
    return pl.ds(pl.multiple_of(j * t, t), t)


def _attn_call(body, name, tile_scalars, operands, in_specs, out_specs, out_shape, scratch_shapes, n_tiles):
    return pl.pallas_call(
        body, name=name,
        grid_spec=pltpu.PrefetchScalarGridSpec(
            num_scalar_prefetch=len(tile_scalars), grid=(N_HEADS // 2, n_tiles), in_specs=in_specs, out_specs=out_specs,
            scratch_shapes=scratch_shapes),
        out_shape=out_shape, compiler_params=_params("arbitrary", "arbitrary"),
    )(*tile_scalars, *operands)


def _attn_fwd(qf, kl, vl, first, bounded, shards, *, tq=ATTN_TILE):
    T = qf.shape[1]
    tq = min(tq, T)
    n = T // tq
    n_steps = (N_HEADS // 2) * n
    k = len(shards)

    def body(first_ref, bounded_ref, qf_ref, kl_ref, vl_ref, *refs):
        w_refs, (o_ref, of_ref, ql_ref), g_refs = refs[:k], refs[k:k + 3], refs[k + 3:2 * k + 3]
        m_ref, acc_ref, send_sems, recv_sems = refs[2 * k + 3:]
        i = pl.program_id(1)
        tile = pl.program_id(0) * n + i
        gather_start, gather_forward, gather_finish = _gather_phases(w_refs, g_refs, send_sems, recv_sems)
        pl.when(tile == 0)(gather_start)
        pl.when(tile == (3 * n_steps) // 4)(gather_forward)
        start = first_ref[tile].astype(jnp.int32)
        is_bounded = bounded_ref[tile] > 0.5
        acc_ref[...] = jnp.zeros_like(acc_ref)
        diagonal = _tile_rows(i, tq)
        causal = _causal(tq)

        def logits(hh, rows):
            return _dot_nt(qf_ref[hh], kl_ref[hh, rows, :])

        @pl.when(is_bounded)
        def _():
            m_ref[...] = jnp.zeros_like(m_ref)

            def update(hh, s, rows):
                acc_ref[hh] += _dot(jnp.exp(s).astype(BF), vl_ref[hh, rows, :])

            def step(j, carry):
                for hh in range(2):
                    update(hh, logits(hh, _tile_rows(j, tq)), _tile_rows(j, tq))
                return carry

            lax.fori_loop(start, i, step, 0)
            for hh in range(2):
                update(hh, jnp.where(causal, logits(hh, diagonal), NEG), diagonal)

        @pl.when(jnp.logical_not(is_bounded))
        def _():
            m_ref[...] = jnp.full_like(m_ref, NEG)

            def update(hh, s, rows):
                m_old = m_ref[hh]
                m_new = jnp.maximum(m_old, jnp.max(s, axis=1, keepdims=True))
                p = jnp.exp(s - m_new)
                acc_ref[hh] = jnp.exp(m_old - m_new) * acc_ref[hh] + _dot(p.astype(BF), vl_ref[hh, rows, :])
                m_ref[hh] = m_new

            def step(j, carry):
                for hh in range(2):
                    update(hh, logits(hh, _tile_rows(j, tq)), _tile_rows(j, tq))
                return carry

            lax.fori_loop(start, i, step, 0)
            for hh in range(2):
                update(hh, jnp.where(causal, logits(hh, diagonal), NEG), diagonal)

        lane = lax.broadcasted_iota(jnp.int32, (tq, LANES), 1)
        outs = []
        for hh in range(2):
            q = qf_ref[hh].astype(F32)
            acc = acc_ref[hh]
            l = acc[:, HEAD_DIM:HEAD_DIM + 1]
            outs.append(acc[:, :HEAD_DIM] / l)
            at = HEAD_DIM + 6
            neg_bound = (q[:, at:at + 1] + q[:, at + 1:at + 2]) + q[:, at + 2:at + 3]
            ql_ref[hh] = _with_lanes(q, lane, at, _split3(neg_bound - (m_ref[hh] + jnp.log(l)))).astype(BF)
        o = jnp.concatenate(outs, axis=1)
        o_ref[...] = o.astype(BF)
        of_ref[...] = o
        pl.when(tile == n_steps - 1)(gather_finish)

    outs = _attn_call(
        body, "attn_fwd", (first, bounded), (qf, kl, vl, *shards),
        [_pair_block(tq), _pair_full(T), _pair_full(T)] + [HBM] * k,
        [_packed_block(tq), _packed_block(tq), _pair_block(tq)] + [HBM] * k,
        [jax.ShapeDtypeStruct((T, ATTN_W), BF), jax.ShapeDtypeStruct((T, ATTN_W), F32),
         jax.ShapeDtypeStruct((N_HEADS, T, LANES), BF)] + _gathered_shapes(shards),
        [pltpu.VMEM((2, tq, 1), F32), pltpu.VMEM((2, tq, LANES), F32)] + _gather_semaphores(k), n)
    return outs[0], outs[1], outs[2], outs[3:]


def _attn_bwd_prep(dya, of, *, tr=256):
    T = dya.shape[0]
    tr = min(tr, T)

    def body(d_ref, o_ref, do_ref):
        lane = lax.broadcasted_iota(jnp.int32, (tr, HEAD_DIM), 1)
        dv, ov = d_ref[...], o_ref[...]
        for h in range(N_HEADS):
            d = dv[:, h * HEAD_DIM:(h + 1) * HEAD_DIM]
            delta = jnp.sum(d * ov[:, h * HEAD_DIM:(h + 1) * HEAD_DIM], axis=1, keepdims=True)
            ext = _with_lanes(jnp.zeros((tr, HEAD_DIM), F32), lane, 0, _split3(-delta))
            do_ref[h] = jnp.concatenate([d, ext], axis=1).astype(BF)

    return pl.pallas_call(
        body, name="attn_bwd_prep", grid=(T // tr,),
        in_specs=[_row_spec(tr, ATTN_W), _row_spec(tr, ATTN_W)],
        out_specs=pl.BlockSpec((N_HEADS, tr, LANES), lambda i: (0, i, 0)),
        out_shape=jax.ShapeDtypeStruct((N_HEADS, T, LANES), BF), compiler_params=_params("parallel"),
    )(dya, of)


def _attn_bwd(kl, vl, ql, do, last, chip_sums, *, tk=ATTN_TILE):
    T = ql.shape[1]
    tk = min(tk, T)
    n = T // tk
    n_steps = (N_HEADS // 2) * n
    m = len(chip_sums)

    def body(last_ref, kl_ref, vl_ref, ql_ref, do_ref, *refs):
        b_refs, (dq_ref, dk_ref, dv_ref, extq_ref, extk_ref), r_refs = refs[:m], refs[m:m + 5], refs[m + 5:2 * m + 5]
        dq_acc, dk_acc, dv_acc, send_sems, recv_sems = refs[2 * m + 5:]
        j = pl.program_id(1)
        tile = pl.program_id(0) * n + j
        scatter_start, scatter_finish = _scatter_phases(b_refs, r_refs, send_sems, recv_sems)
        pl.when(tile == 0)(scatter_start)

        @pl.when(j == 0)
        def _():
            dq_acc[...] = jnp.zeros_like(dq_acc)

        dk_acc[...] = jnp.zeros_like(dk_acc)
        dv_acc[...] = jnp.zeros_like(dv_acc)

        def block(hh, rows, mask):
            qi, di, k = ql_ref[hh, rows, :], do_ref[hh, rows, :], kl_ref[hh]
            p_t = jnp.exp(_dot_nt(k, qi))
            if mask is not None:
                p_t = jnp.where(mask, p_t, 0.0)
            ds_t = (p_t * _dot_nt(vl_ref[hh], di)).astype(BF)
            dk_acc[hh] += _dot(ds_t, qi)
            dv_acc[hh] += _dot(p_t.astype(BF), di)
            dq_acc[hh, rows, :] += _dot_tn(ds_t, k)

        causal_t = _causal(tk, keys_in_rows=True)
        for hh in range(2):
            block(hh, _tile_rows(j, tk), causal_t)

        def step(i, carry):
            for hh in range(2):
                block(hh, _tile_rows(i, tk), None)
            return carry

        lax.fori_loop(j + 1, last_ref[pl.program_id(0) * n + j].astype(jnp.int32) + 1, step, 0)
        dk_ref[...] = jnp.concatenate([dk_acc[hh][:, :HEAD_DIM] for hh in range(2)], axis=1).astype(BF)
        dv_ref[...] = jnp.concatenate([dv_acc[hh][:, :HEAD_DIM] for hh in range(2)], axis=1).astype(BF)
        extk_ref[...] = jnp.concatenate([dk_acc[hh][:, HEAD_DIM:] for hh in range(2)], axis=1)

        @pl.when(j == n - 1)
        def _():
            dq_ref[...] = jnp.concatenate([dq_acc[hh][:, :HEAD_DIM] * Q_SCALE for hh in range(2)], axis=1).astype(BF)
            extq_ref[...] = jnp.concatenate([dq_acc[hh][:, HEAD_DIM:] for hh in range(2)], axis=1)

        pl.when(tile == n_steps - 1)(scatter_finish)

    whole = pl.BlockSpec((T, LANES), lambda p, j, *_: (0, p))
    outs = pl.pallas_call(
        body, name="attn_bwd",
        grid_spec=pltpu.PrefetchScalarGridSpec(
            num_scalar_prefetch=1, grid=(N_HEADS // 2, n),
            in_specs=[_pair_block(tk), _pair_block(tk), _pair_full(T), _pair_full(T)] + [HBM] * m,
            out_specs=[whole, _packed_block(tk), _packed_block(tk), whole, _packed_block(tk)] + [HBM] * m,
            scratch_shapes=[pltpu.VMEM((2, T, LANES), F32), pltpu.VMEM((2, tk, LANES), F32), pltpu.VMEM((2, tk, LANES), F32)]
            + _scatter_semaphores(m)),
        out_shape=[jax.ShapeDtypeStruct((T, ATTN_W), BF)] * 3 + [jax.ShapeDtypeStruct((T, ATTN_W), F32)] * 2
        + _scattered_shapes(chip_sums),
        compiler_params=pltpu.CompilerParams(dimension_semantics=("arbitrary", "arbitrary"), vmem_limit_bytes=BIG_VMEM),
    )(last, kl, vl, ql, do, *chip_sums)
    return outs[:5], outs[5:]


def _forget_bwd(ext_q, ext_k, fl, b_forget, *, tp=256):
    T = fl.shape[0]
    tp = min(tp, T)
    n = T // tp

    def body(eq_ref, ek_ref, fl_ref, bf_ref, dfl_ref, dbf_ref, carry_ref):
        @pl.when(pl.program_id(0) == 0)
        def _():
            carry_ref[...] = jnp.zeros_like(carry_ref)
            dbf_ref[...] = jnp.zeros_like(dbf_ref)

        lane = lax.broadcasted_iota(jnp.int32, (tp, LANES), 1)
        eq, ek = eq_ref[...], ek_ref[...]
        cols = [eq[:, h * HEAD_DIM:h * HEAD_DIM + 1] - ek[:, h * HEAD_DIM + 3:h * HEAD_DIM + 4] for h in range(N_HEADS)]
        dcum = _with_lanes(jnp.zeros((tp, LANES), F32), lane, 0, cols)
        suffix = _scan_dot(_tri(tp, upper=True), dcum) + carry_ref[...]
        carry_ref[...] = suffix[0:1, :]
        x = fl_ref[...] + bf_ref[...]
        dfl = jnp.where(lane < N_HEADS, suffix / (1.0 + jnp.exp(x)), 0.0)
        dfl_ref[...] = dfl.astype(BF)
        dbf_ref[...] += jnp.sum(dfl, axis=0, keepdims=True)

    rev = lambda w: pl.BlockSpec((tp, w), lambda i: (n - 1 - i, 0))
    return pl.pallas_call(
        body, name="forget_bwd", grid=(n,),
        in_specs=[rev(ATTN_W), rev(ATTN_W), rev(LANES), _vec_spec(LANES)],
        out_specs=[rev(LANES), _vec_spec(LANES)],
        out_shape=[jax.ShapeDtypeStruct((T, LANES), BF), jax.ShapeDtypeStruct((1, LANES), F32)],
        scratch_shapes=[pltpu.VMEM((1, LANES), F32)], compiler_params=_params("arbitrary"),
    )(ext_q, ext_k, fl, b_forget)


def _adamw(w, g, m, v, *, name, tr=256):
    _, rows, cols = w.shape
    tr = tr if rows % tr == 0 else rows

    def body(w_ref, g_ref, m_ref, v_ref, go_ref, d_ref, nm_ref, nv_ref):
        gv = g_ref[...]
        go_ref[...] = gv
        nm = ADAM_B1 * m_ref[...] + (1.0 - ADAM_B1) * gv
        nv = ADAM_B2 * v_ref[...] + (1.0 - ADAM_B2) * (gv * gv)
        m_hat = nm / (1.0 - ADAM_B1 ** ADAM_STEP)
        v_hat = nv / (1.0 - ADAM_B2 ** ADAM_STEP)
        d_ref[...] = -ADAM_LR * (m_hat / (jnp.sqrt(v_hat) + ADAM_EPS) + ADAM_WD * w_ref[...])
        nm_ref[...] = nm
        nv_ref[...] = nv

    spec = pl.BlockSpec((None, tr, cols), lambda i: (0, i, 0))
    return pl.pallas_call(
        body, name=name, grid=(rows // tr,), in_specs=[spec, pl.BlockSpec((tr, cols), lambda i: (i, 0)), spec, spec],
        out_specs=[spec] * 4, out_shape=[jax.ShapeDtypeStruct((1, rows, cols), F32)] * 4,
        compiler_params=_params("parallel"),
    )(w, g, m, v)


HBM = pl.BlockSpec(memory_space=pltpu.HBM)
BF16_ROWS = 16


def _place():
    x, y, c = lax.axis_index("x"), lax.axis_index("y"), lax.axis_index("c")
    others = [(1 - x, y), (x, 1 - y), (1 - x, 1 - y)]
    return x, y, c, others


def _chip(xy):
    return 2 * xy[0] + xy[1]


def _row_halves(c, rows):
    half = rows // 2
    assert half % BF16_ROWS == 0
    return (pl.ds(pl.multiple_of(c * half, BF16_ROWS), half), pl.ds(pl.multiple_of((1 - c) * half, BF16_ROWS), half))


def _remote(src, dst, send_sems, recv_sems, k, to):
    return pltpu.make_async_remote_copy(src_ref=src, dst_ref=dst, send_sem=send_sems.at[k], recv_sem=recv_sems.at[k],
                                        device_id=to, device_id_type=MESH)


def _gathered_shapes(shards):
    return [jax.ShapeDtypeStruct((N_CHIPS,) + s.shape, s.dtype) for s in shards]


def _gather_semaphores(n):
    return [pltpu.SemaphoreType.DMA((6 * n,)), pltpu.SemaphoreType.DMA((6 * n,))]


def _gather_phases(w_refs, g_refs, send_sems, recv_sems):
    n = len(w_refs)
    x, y, c, others = _place()
    sibling, me = (x, y, 1 - c), _chip((x, y))
    halves = [_row_halves(c, w.shape[0]) for w in w_refs]

    def sent(a, j, o):
        mine, _ = halves[a]
        return _remote(w_refs[a].at[mine, :], g_refs[a].at[me, mine, :], send_sems, recv_sems, 6 * a + j, (*o, c))

    def passed(a, j, o):
        landed = g_refs[a].at[_chip(o), halves[a][0], :]
        return _remote(landed, landed, send_sems, recv_sems, 6 * a + 3 + j, sibling)

    def start():
        for a in range(n):
            for j, o in enumerate(others):
                sent(a, j, o).start()

    def forward():
        for j, o in enumerate(others):
            for a in range(n):
                landed = g_refs[a].at[_chip(o), halves[a][0], :]
                _remote(landed, landed, send_sems, recv_sems, 6 * a + j, (*o, c)).wait_recv()
                passed(a, j, o).start()

    def finish():
        for j, o in enumerate(others):
            for a in range(n):
                landed = g_refs[a].at[_chip(o), halves[a][1], :]
                _remote(landed, landed, send_sems, recv_sems, 6 * a + 3 + j, sibling).wait_recv()
        for a in range(n):
            for j, o in enumerate(others):
                sent(a, j, o).wait_send()
                passed(a, j, o).wait_send()

    return start, forward, finish


def _exchange_halves(arrays, *, name):
    n = len(arrays)

    def body(*refs):
        for phase in _exchange_phases(refs[:n], refs[n:2 * n], *refs[2 * n:]):
            phase()

    return pl.pallas_call(
        body, name=name, in_specs=[HBM] * n, out_specs=[HBM] * n, out_shape=_exchanged_shapes(arrays),
        scratch_shapes=_exchange_semaphores(n),
    )(*arrays)


def _exchanged_shapes(arrays):
    return [jax.ShapeDtypeStruct(s.shape[:-2] + (s.shape[-2] // 2, s.shape[-1]), F32) for s in arrays]


def _exchange_semaphores(n):
    return [pltpu.SemaphoreType.DMA((n,)), pltpu.SemaphoreType.DMA((n,))]


def _exchange_phases(g_refs, r_refs, send_sems, recv_sems):
    x, y, c, _ = _place()

    def copy(a):
        _, theirs = _row_halves(c, g_refs[a].shape[-2])
        src = g_refs[a].at[:, theirs, :] if len(g_refs[a].shape) == 3 else g_refs[a].at[theirs, :]
        return _remote(src, r_refs[a], send_sems, recv_sems, a, (x, y, 1 - c))

    def start():
        for a in range(len(g_refs)):
            copy(a).start()

    def finish():
        for a in range(len(g_refs)):
            copy(a).wait()

    return start, finish


def _scatter_to_owners(chip_sums):
    n = len(chip_sums)

    def body(*refs):
        for phase in _scatter_phases(refs[:n], refs[n:2 * n], *refs[2 * n:]):
            phase()

    return pl.pallas_call(
        body, name="scatter_to_owners", in_specs=[HBM] * n, out_specs=[HBM] * n,
        out_shape=_scattered_shapes(chip_sums), scratch_shapes=_scatter_semaphores(n),
    )(*chip_sums)


def _scattered_shapes(chip_sums):
    return [jax.ShapeDtypeStruct(b.shape if b.ndim == 3 else (N_CHIPS,) + b.shape, b.dtype) for b in chip_sums]


def _scatter_semaphores(n):
    return [pltpu.SemaphoreType.DMA((3 * n,)), pltpu.SemaphoreType.DMA((3 * n,))]


def _scatter_phases(b_refs, r_refs, send_sems, recv_sems):
    n = len(b_refs)
    x, y, c, others = _place()
    me = _chip((x, y))

    def sent(a, j, o):
        src = b_refs[a].at[_chip(o)] if len(b_refs[a].shape) == 3 else b_refs[a]
        return _remote(src, r_refs[a].at[me], send_sems, recv_sems, 3 * a + j, (*o, c))

    def start():
        for a in range(n):
            for j, o in enumerate(others):
                sent(a, j, o).start()

    def finish():
        for a in range(n):
            for j, o in enumerate(others):
                landed = r_refs[a].at[_chip(o)]
                _remote(landed, landed, send_sems, recv_sems, 3 * a + j, (*o, c)).wait_recv()
        for a in range(n):
            for j, o in enumerate(others):
                sent(a, j, o).wait_send()

    return start, finish


def _join_halves(totals):
    n = len(totals)

    def body(*refs):
        in_refs, out_refs, (send_sems, recv_sems) = refs[:n], refs[n:2 * n], refs[2 * n:]
        x, y, c, _ = _place()
        copies = []
        for a in range(n):
            mine, _ = _row_halves(c, in_refs[a].shape[0])
            copies.append(_remote(in_refs[a].at[mine, :], out_refs[a].at[mine, :], send_sems, recv_sems, a, (x, y, 1 - c)))
            copies[-1].start()
        for cp in copies:
            cp.wait()

    return pl.pallas_call(
        body, name="join_halves", in_specs=[HBM] * n, out_specs=[HBM] * n,
        out_shape=[jax.ShapeDtypeStruct(t.shape, F32) for t in totals], input_output_aliases={a: a for a in range(n)},
        scratch_shapes=[pltpu.SemaphoreType.DMA((n,)), pltpu.SemaphoreType.DMA((n,))],
    )(*totals)


ADD_ROWS = 128


def _add_sibling(g, r, place, *, name):
    lead, (half, cols) = g.shape[:-2], r.shape[-2:]
    tr = min(ADD_ROWS, half)
    nb = half // tr
    zeros = (0,) * len(lead)

    def body(place_ref, g_ref, r_ref, o_ref, ob_ref):
        s = g_ref[...] + r_ref[...]
        o_ref[...] = s
        ob_ref[...] = s.astype(BF)

    spec = pl.BlockSpec(lead + (tr, cols), lambda i, p: zeros + (i, 0))
    return pl.pallas_call(
        body, name=name,
        grid_spec=pltpu.PrefetchScalarGridSpec(
            num_scalar_prefetch=1, grid=(nb,),
            in_specs=[pl.BlockSpec(lead + (tr, cols), lambda i, p: zeros + (p[1] * nb + i, 0)), spec], out_specs=[spec, spec]),
        out_shape=[jax.ShapeDtypeStruct(r.shape, F32), jax.ShapeDtypeStruct(r.shape, BF)],
        compiler_params=_params("parallel"),
    )(place, g, r)


def _add_chips(own, received, place, *, name, own_slots):
    half, cols = received.shape[-2:]
    tr = min(ADD_ROWS, half)
    nb = half // tr

    def written(k, p):
        return jnp.where(p[0] == k, (k + 1) % N_CHIPS, k)

    def body(place_ref, own_ref, *refs):
        o_ref = refs[N_CHIPS]
        mine = own_ref[0] if own_slots else own_ref[...]
        if own_slots:
            acc = mine
            for k in range(N_CHIPS):
                acc = acc + jnp.where(place_ref[0] == k, 0.0, refs[k][0].astype(F32))
        else:
            terms = [jnp.where(place_ref[0] == k, mine, refs[k][0]) for k in range(N_CHIPS)]
            acc = ((terms[0] + terms[1]) + terms[2]) + terms[3]
        o_ref[...] = acc

    own_spec = (pl.BlockSpec((1, tr, cols), lambda i, p: (p[0], i, 0)) if own_slots
                else pl.BlockSpec((tr, cols), lambda i, p: (i, 0)))
    return pl.pallas_call(
        body, name=name,
        grid_spec=pltpu.PrefetchScalarGridSpec(
            num_scalar_prefetch=1, grid=(nb,),
            in_specs=[own_spec] + [pl.BlockSpec((1, tr, cols), functools.partial(lambda i, p, k: (written(k, p), i, 0), k=k))
                                   for k in range(N_CHIPS)],
            out_specs=pl.BlockSpec((tr, cols), lambda i, p: (p[1] * nb + i, 0))),
        out_shape=jax.ShapeDtypeStruct((2 * half, cols), F32), compiler_params=_params("parallel"),
    )(place, own, *([received] * N_CHIPS))


SHARDED = (("w_in", (D_MODEL, 4616), 1), ("w_branch_sgu", (SGU_W, D_MODEL), 1), ("w_branch_attn", (ATTN_W, D_MODEL), 1),
           ("w_out", (D_MODEL, D_MODEL), 0), ("w_up", (D_MODEL, D_FF), 1), ("w_down", (D_FF, D_MODEL), 0))
SMALL = (("g_mix_pre", (1, D_MODEL)), ("b_forget", (1, N_HEADS)), ("g_sgu", (1, SGU_W)), ("b_sgu", (1, SGU_W)),
         ("w_spatial", (N_GROUPS * CHUNK, CHUNK)), ("b_spatial", (N_GROUPS, CHUNK)), ("g_mix_post", (1, D_MODEL)),
         ("g_ffn_pre", (1, D_MODEL)), ("g_ffn_post", (1, D_MODEL)))
SMALL_ALIGN = 2 * ADD_ROWS


def _shard_shape(shape, axis):
    return tuple(s // N_CHIPS if a == axis else s for a, s in enumerate(shape))


def _slots_to_full(slots, axis):
    return slots.reshape(-1, slots.shape[2]) if axis == 0 else slots.transpose(1, 0, 2).reshape(slots.shape[1], -1)


def _full_to_slots(full, axis):
    if axis == 0:
        return full.reshape(N_CHIPS, -1, full.shape[1])
    return full.reshape(full.shape[0], N_CHIPS, -1).transpose(1, 0, 2)


def _small_rows(shape):
    return -(-(shape[0] * shape[1]) // (8 * LANES)) * 8


def _pack_small(values):
    parts = []
    for name, shape in SMALL:
        flat = values[name].reshape(-1)
        n = _small_rows(shape)
        parts.append(jnp.pad(flat, (0, n * LANES - flat.shape[0])).reshape(n, LANES))
    rows = sum(p.shape[0] for p in parts)
    pad = -(-rows // SMALL_ALIGN) * SMALL_ALIGN - rows
    return jnp.concatenate(parts + [jnp.zeros((pad, LANES), F32)], axis=0)


def _unpack_small(packed):
    out, row = {}, 0
    for name, shape in SMALL:
        n = _small_rows(shape)
        out[name] = packed[row:row + n].reshape(-1)[:shape[0] * shape[1]].reshape(shape)
        row += n
    return out


IN_Z, IN_Q, IN_K, IN_V, IN_F, IN_G, IN_END = 0, 1024, 1536, 2048, 2560, 2568, 4616


LATE_WEIGHTS = ("w_branch_sgu", "w_branch_attn", "w_out", "w_up", "w_down")
EARLY_GRADS = LATE_WEIGHTS


def _with_own_slot(shard, gathered, chip):
    return jnp.where(jnp.arange(N_CHIPS)[:, None, None] == chip, shard[None], gathered)


def _assemble(name, shard, gathered, chip):
    axis = {n: a for n, _, a in SHARDED}[name]
    return _slots_to_full(_with_own_slot(shard, gathered, chip), axis)


def _columns_from_slots(slots, bounds):
    width = slots.shape[2]
    pieces = []
    for lo, hi in zip(bounds[:-1], bounds[1:], strict=True):
        parts = [slots[k][:, max(lo, k * width) - k * width:min(hi, (k + 1) * width) - k * width]
                 for k in range(N_CHIPS) if max(lo, k * width) < min(hi, (k + 1) * width)]
        pieces.append(parts[0] if len(parts) == 1 else jnp.concatenate(parts, axis=1))
    return pieces


def _columns_to_slots(pieces):
    width = sum(p.shape[1] for p in pieces) // N_CHIPS
    slots = []
    for k in range(N_CHIPS):
        parts, start = [], 0
        for p in pieces:
            lo, hi = max(k * width, start), min((k + 1) * width, start + p.shape[1])
            if lo < hi:
                parts.append(p[:, lo - start:hi - start])
            start += p.shape[1]
        slots.append(jnp.concatenate(parts, axis=1))
    return jnp.stack(slots)


def _local_step(x, target, shards, small, place):
    b_forget = jnp.pad(small["b_forget"], ((0, 0), (0, LANES - N_HEADS)))
    causal = jnp.tril(jnp.ones((CHUNK, CHUNK), bool))
    ws = jnp.where(causal[None], small["w_spatial"].reshape(N_GROUPS, CHUNK, CHUNK), 0.0).astype(BF)
    ws_t = ws.transpose(0, 2, 1)
    bias_plane = jnp.repeat(small["b_spatial"].T, HEAD_DIM, axis=1)

    xn, (w_in_slots,) = _rms_fwd(x, small["g_mix_pre"], [shards["w_in"]])
    w_z, w_q, w_k, w_v, w_f, w_ga, w_gb = _columns_from_slots(
        _with_own_slot(shards["w_in"], w_in_slots, place[0]), (IN_Z, IN_Q, IN_K, IN_V, IN_F, IN_G, IN_G + D_MODEL, IN_END))
    w_qkv, w_g = jnp.concatenate([w_q, w_k, w_v], axis=1), jnp.concatenate([w_ga, w_gb], axis=1)
    w_f = jnp.pad(w_f, ((0, 0), (0, LANES - N_HEADS)))
    z = _matmul([(xn, w_z)], nt=False, out_dtypes=[F32], name="proj_z")
    qkv = _matmul([(xn, w_qkv)], nt=False, out_dtypes=[BF], name="proj_qkv")
    gl = _matmul([(xn, w_g)], nt=False, out_dtypes=[BF], name="proj_gate")
    fl = _matmul([(xn, w_f)], nt=False, out_dtypes=[F32], name="proj_forget")
    ysgu = _sgu_fwd(z, small["g_sgu"], small["b_sgu"], ws, bias_plane)
    qf, kl, vl, tile_stats = _attn_prep(qkv, fl, b_forget)
    first_key_tile, last_query_tile, bounded = _attn_ranges(tile_stats)
    yattn, yattn_f, ql, gathered = _attn_fwd(qf, kl, vl, first_key_tile, bounded, [shards[name] for name in LATE_WEIGHTS])
    w = {name: _assemble(name, shards[name], got, place[0]) for name, got in zip(LATE_WEIGHTS, gathered, strict=True)}
    a, b, merged = _branch_merge(ysgu, yattn, w["w_branch_sgu"], w["w_branch_attn"], gl)
    o, h1, xn2 = _matmul_rows(
        [(merged, w["w_out"])], nt=False, rows=[x], vecs=[small["g_mix_post"], small["g_ffn_pre"]], row_outs=[F32, F32, BF],
        n_sums=0, epilogue=_mixer_out_fwd, name="proj_out_norms")

    def relu2(acc):
        r = jnp.maximum(acc, 0.0)
        return (r * r,)

    hid = _matmul([(xn2, w["w_up"])], nt=False, out_dtypes=[BF], name="ffn_up", epilogue=relu2, tm=FFN_ROWS)
    dy, ddn, sq, dg_ffn_post = _matmul_rows(
        [(hid, w["w_down"])], nt=False, rows=[h1, target], vecs=[small["g_ffn_post"]], row_outs=[F32, BF], n_sums=2,
        epilogue=_loss_head, name="ffn_down_loss")

    dup = _matmul([(ddn, w["w_down"])], nt=True, out_dtypes=[BF], name="ffn_down_bwd", tm=FFN_ROWS,
                  epilogue=lambda acc, h: (acc * (2.0 * jnp.sqrt(h.astype(F32))),), extras=[hid])
    dw_down = _matmul_tn(hid, ddn, name="dw_down")
    dh1, do, dg_ffn_pre, dg_mix_post = _matmul_rows(
        [(dup, w["w_up"])], nt=True, rows=[h1, dy, o], vecs=[small["g_ffn_pre"], small["g_mix_post"]], row_outs=[F32, BF],
        n_sums=2, epilogue=_mixer_out_bwd, name="ffn_up_bwd_norms")
    dw_up = _matmul_tn(xn2, dup, name="dw_up", slots=True)

    def gate_bwd(dm, a_t, b_t, gla, glb):
        ga, gb = jax.nn.sigmoid(gla.astype(F32)), jax.nn.sigmoid(glb.astype(F32))
        return dm * ga, dm * gb, dm * a_t.astype(F32) * (ga * (1.0 - ga)), dm * b_t.astype(F32) * (gb * (1.0 - gb))

    da, db, dgla, dglb = _matmul([(do, w["w_out"])], nt=True, out_dtypes=[BF] * 4, name="proj_out_bwd",
                                 epilogue=gate_bwd, extras=[a, b, (gl, 0), (gl, D_MODEL)])
    dw_out = _matmul_tn(merged, do, name="dw_out")
    dysgu = _matmul([(da, w["w_branch_sgu"])], nt=True, out_dtypes=[F32], name="branch_sgu_bwd")
    dyattn = _matmul([(db, w["w_branch_attn"])], nt=True, out_dtypes=[F32], name="branch_attn_bwd")
    dw_bs = _matmul_tn(ysgu, da, name="dw_branch_sgu")
    dw_ba = _matmul_tn(yattn, db, name="dw_branch_attn")
    early = {"w_branch_sgu": _full_to_slots(dw_bs, 1), "w_branch_attn": _full_to_slots(dw_ba, 1),
             "w_out": _full_to_slots(dw_out, 0), "w_up": dw_up, "w_down": _full_to_slots(dw_down, 0)}
    (dz, dws, dbs, dg_sgu, db_sgu), early_theirs = _sgu_bwd(
        dysgu, z, small["g_sgu"], small["b_sgu"], ws, ws_t, bias_plane, [early[name] for name in EARLY_GRADS])
    early_sums = {name: _add_sibling(early[name], theirs, place, name="add_sibling_" + name)
                  for name, theirs in zip(EARLY_GRADS, early_theirs, strict=True)}
    dout = _attn_bwd_prep(dyattn, yattn_f)
    (dq, dk, dv, ext_q, ext_k), early_received = _attn_bwd(
        kl, vl, ql, dout, last_query_tile, [early_sums[name][1] for name in EARLY_GRADS])
    dfl, dbf = _forget_bwd(ext_q, ext_k, fl, b_forget)
    dw_in = _columns_to_slots(
        [_matmul_tn(xn, dz, name="dw_in_z"), _matmul_tn(xn, dq, name="dw_in_q"), _matmul_tn(xn, dk, name="dw_in_k"),
         _matmul_tn(xn, dv, name="dw_in_v"), _matmul_tn(xn, dfl, name="dw_in_f")[:, :N_HEADS],
         _matmul_tn(xn, dgla, name="dw_in_ga"), _matmul_tn(xn, dglb, name="dw_in_gb")])
    (dw_in_theirs,) = _exchange_halves([dw_in], name="exchange_halves_w_in")
    dw_in_sum = _add_sibling(dw_in, dw_in_theirs, place, name="add_sibling_w_in")
    dx, dg_mix_pre, dw_in_received = _matmul_rows(
        [(dz, w_z), (dq, w_q), (dk, w_k), (dv, w_v), (dgla, w_ga), (dglb, w_gb), (dfl, w_f)],
        nt=True, rows=[x, dh1], vecs=[small["g_mix_pre"]], row_outs=[F32], n_sums=1, epilogue=_input_norm_bwd,
        name="proj_in_bwd_norm", scatter=[dw_in_sum[1]])

    reduced = {name: (early_sums[name][0], got) for name, got in zip(EARLY_GRADS, early_received, strict=True)}
    reduced["w_in"] = (dw_in_sum[0], dw_in_received)
    small_grads = {"g_mix_pre": dg_mix_pre, "b_forget": dbf[:, :N_HEADS], "g_sgu": dg_sgu, "b_sgu": db_sgu,
                   "w_spatial": dws.reshape(N_GROUPS * CHUNK, CHUNK), "b_spatial": dbs[:, :N_GROUPS].T,
                   "g_mix_post": dg_mix_post, "g_ffn_pre": dg_ffn_pre, "g_ffn_post": dg_ffn_post}
    return sq, dx, reduced, small_grads


NAMES = ("g_mix_pre", "w_in", "b_forget", "g_sgu", "b_sgu", "w_spatial", "b_spatial", "w_branch_sgu", "w_branch_attn",
         "w_out", "g_mix_post", "g_ffn_pre", "w_up", "w_down", "g_ffn_post")


def kernel(x, g_mix_pre, w_in, b_forget, g_sgu, b_sgu, w_spatial, b_spatial, w_branch_sgu, w_branch_attn, w_out, g_mix_post, g_ffn_pre, w_up, w_down, g_ffn_post, loss_target, m_g_mix_pre, m_w_in, m_b_forget, m_g_sgu, m_b_sgu, m_w_spatial, m_b_spatial, m_w_branch_sgu, m_w_branch_attn, m_w_out, m_g_mix_post, m_g_ffn_pre, m_w_up, m_w_down, m_g_ffn_post, v_g_mix_pre, v_w_in, v_b_forget, v_g_sgu, v_b_sgu, v_w_spatial, v_b_spatial, v_w_branch_sgu, v_w_branch_attn, v_w_out, v_g_mix_post, v_g_ffn_pre, v_w_up, v_w_down, v_g_ffn_post):
    weights = dict(zip(NAMES, (g_mix_pre, w_in, b_forget, g_sgu, b_sgu, w_spatial, b_spatial, w_branch_sgu, w_branch_attn,
                               w_out, g_mix_post, g_ffn_pre, w_up, w_down, g_ffn_post), strict=True))
    first = dict(zip(NAMES, (m_g_mix_pre, m_w_in, m_b_forget, m_g_sgu, m_b_sgu, m_w_spatial, m_b_spatial, m_w_branch_sgu,
                             m_w_branch_attn, m_w_out, m_g_mix_post, m_g_ffn_pre, m_w_up, m_w_down, m_g_ffn_post), strict=True))
    second = dict(zip(NAMES, (v_g_mix_pre, v_w_in, v_b_forget, v_g_sgu, v_b_sgu, v_w_spatial, v_b_spatial, v_w_branch_sgu,
                              v_w_branch_attn, v_w_out, v_g_mix_post, v_g_ffn_pre, v_w_up, v_w_down, v_g_ffn_post), strict=True))
    shard_shapes = {name: _shard_shape(shape, axis) for name, shape, axis in SHARDED}
    small_shapes = dict(SMALL)
    view = lambda name, a: a.reshape(shard_shapes.get(name) or small_shapes[name])

    place = jnp.stack([2 * lax.axis_index("x") + lax.axis_index("y"), lax.axis_index("c")]).astype(jnp.int32)

    shards = {name: view(name, weights[name]).astype(BF) for name, _, _ in SHARDED}
    small = {name: view(name, weights[name]) for name, _ in SMALL}
    sq, dx, reduced, small_grads = _local_step(x[0], loss_target[0], shards, small, place)
    loss = lax.psum(0.5 * jnp.sum(sq) / D_MODEL, ("x", "y", "c"))

    small_mine = _pack_small(small_grads)
    (small_theirs,) = _exchange_halves([small_mine], name="exchange_halves_small")
    small_sum, _ = _add_sibling(small_mine, small_theirs, place, name="add_sibling_small")
    (small_received,) = _scatter_to_owners([small_sum])
    totals = {name: _add_chips(s, r, place, name="add_chips_" + name, own_slots=True) for name, (s, r) in reduced.items()}
    small_total = _add_chips(small_sum, small_received, place, name="add_chips_small", own_slots=False)
    joined = _join_halves([totals[name] for name, _, _ in SHARDED] + [small_total])
    grad = {**{name: g for (name, _, _), g in zip(SHARDED, joined[:-1], strict=True)}, **_unpack_small(joined[-1])}

    grad_out, delta, new_m, new_v = {}, {}, {}, {}
    for name in NAMES:
        rows, cols = grad[name].shape
        as_given = lambda a: a.reshape(1, rows, cols)
        grad_out[name], delta[name], new_m[name], new_v[name] = _adamw(
            as_given(weights[name]), grad[name], as_given(first[name]), as_given(second[name]), name="adamw_" + name)

    like = lambda d: [d[name].reshape(weights[name].shape) for name in NAMES]
    return (loss, dx[None], *like(grad_out), *like(delta), *like(new_m), *like(new_v))
```

```python
import functools

import jax
import jax.numpy as jnp
import numpy as np
from jax import lax
from jax.experimental import pallas as pl
from jax.experimental.pallas import tpu as pltpu

F32 = jnp.float32
BF = jnp.bfloat16
MESH = pl.DeviceIdType.MESH

D_MODEL = 1024
N_HEADS = 8
HEAD_DIM = 64
ATTN_W = N_HEADS * HEAD_DIM
SGU_W = 512
N_GROUPS = 8
CHUNK = 128
D_FF = 4096
EPS = 1e-6
Q_SCALE = HEAD_DIM ** -0.5
N_CHIPS = 4
LANES = 128

ADAM_LR = 0.001
ADAM_B1 = 0.9
ADAM_B2 = 0.999
ADAM_EPS = 1e-08
ADAM_WD = 0.01
ADAM_STEP = 10

VMEM_LIMIT = 48 * 1024 * 1024
BIG_VMEM = 58 * 1024 * 1024
NEG = -1e30


def _params(*sem):
    return pltpu.CompilerParams(dimension_semantics=sem, vmem_limit_bytes=VMEM_LIMIT)


def _dot(a, b):
    return jnp.dot(a, b, preferred_element_type=F32)


def _dot_nt(a, b):
    return lax.dot_general(a, b, (((1,), (1,)), ((), ())), preferred_element_type=F32)


def _dot_tn(a, b):
    return lax.dot_general(a, b, (((0,), (0,)), ((), ())), preferred_element_type=F32)


def _split3(c):
    hi = c.astype(BF).astype(F32)
    r = c - hi
    mid = r.astype(BF).astype(F32)
    lo = (r - mid).astype(BF).astype(F32)
    return hi, mid, lo


def _gelu(x):
    k = 0.7978845608028654
    return 0.5 * x * (1.0 + jnp.tanh(k * (x + 0.044715 * (x * x * x))))


def _gelu_grad(x):
    k = 0.7978845608028654
    x2 = x * x
    t = jnp.tanh(k * (x + 0.044715 * (x2 * x)))
    return 0.5 * (1.0 + t) + 0.5 * x * (1.0 - t * t) * (k * (1.0 + 3.0 * 0.044715 * x2))


def _rms_bwd(a, g, dy):
    r = lax.rsqrt(jnp.mean(a * a, axis=-1, keepdims=True) + EPS)
    n = a * r
    dn = dy * g
    da = r * (dn - n * jnp.mean(dn * n, axis=-1, keepdims=True))
    return da, dy * n


MM_ROWS = 1024
MM_COLS = 512
FFN_ROWS = 2048


def _matmul(pairs, *, nt, out_dtypes, name, tm=MM_ROWS, tn=MM_COLS, epilogue=None, extras=()):
    n_pairs, n_extra = len(pairs), len(extras)
    M = pairs[0][0].shape[0]
    N = pairs[0][1].shape[0] if nt else pairs[0][1].shape[1]
    tm, tn = min(tm, M), min(tn, N)
    assert M % tm == 0 and N % tn == 0

    def body(*refs):
        acc = None
        for p in range(n_pairs):
            a_ref, b_ref = refs[2 * p], refs[2 * p + 1]
            d = _dot_nt(a_ref[...], b_ref[...]) if nt else _dot(a_ref[...], b_ref[...])
            acc = d if acc is None else acc + d
        e_refs = refs[2 * n_pairs:2 * n_pairs + n_extra]
        o_refs = refs[2 * n_pairs + n_extra:]
        outs = (acc,) if epilogue is None else epilogue(acc, *[e[...] for e in e_refs])
        for o_ref, o in zip(o_refs, outs, strict=True):
            o_ref[...] = o.astype(o_ref.dtype)

    in_specs, args = [], []
    for a, b in pairs:
        K = a.shape[1]
        in_specs.append(pl.BlockSpec((tm, K), lambda i, j: (i, 0)))
        in_specs.append(pl.BlockSpec((tn, K), lambda i, j: (j, 0)) if nt else pl.BlockSpec((K, tn), lambda i, j: (0, j)))
        args += [a, b]
    for e in extras:
        e, col = e if isinstance(e, tuple) else (e, 0)
        in_specs.append(pl.BlockSpec((tm, tn), functools.partial(lambda i, j, off: (i, j + off), off=col // tn)))
        args.append(e)
    outs = pl.pallas_call(
        body, name=name, grid=(M // tm, N // tn), in_specs=in_specs,
        out_specs=[pl.BlockSpec((tm, tn), lambda i, j: (i, j)) for _ in out_dtypes],
        out_shape=[jax.ShapeDtypeStruct((M, N), dt) for dt in out_dtypes],
        compiler_params=_params("parallel", "parallel"),
    )(*args)
    return outs if len(outs) > 1 else outs[0]


def _matmul_tn(a, b, *, name, tm=1024, tn=1024, tk=2048, slots=False):
    T, K1 = a.shape
    N = b.shape[1]
    tm, tn, tk = min(tm, K1), min(tn, N // N_CHIPS if slots else N), min(tk, T)
    assert K1 % tm == 0 and (N // N_CHIPS if slots else N) % tn == 0 and T % tk == 0
    per_slot = N // N_CHIPS // tn

    def body(a_ref, b_ref, o_ref):
        @pl.when(pl.program_id(2) == 0)
        def _():
            o_ref[...] = jnp.zeros_like(o_ref)

        o_ref[...] += _dot_tn(a_ref[...], b_ref[...])

    if slots:
        out_spec = pl.BlockSpec((None, tm, tn), lambda i, j, k: (j // per_slot, i, j % per_slot))
        out_shape = jax.ShapeDtypeStruct((N_CHIPS, K1, N // N_CHIPS), F32)
    else:
        out_spec = pl.BlockSpec((tm, tn), lambda i, j, k: (i, j))
        out_shape = jax.ShapeDtypeStruct((K1, N), F32)
    return pl.pallas_call(
        body, name=name, grid=(K1 // tm, N // tn, T // tk),
        in_specs=[pl.BlockSpec((tk, tm), lambda i, j, k: (k, i)), pl.BlockSpec((tk, tn), lambda i, j, k: (k, j))],
        out_specs=out_spec, out_shape=out_shape,
        compiler_params=_params("parallel", "parallel", "arbitrary"),
    )(a, b)


def _branch_merge(ysgu, yattn, w_bs, w_ba, gl, *, tm=MM_ROWS, tn=MM_COLS):
    T = ysgu.shape[0]
    tm = min(tm, T)
    nj = D_MODEL // tn

    def body(ys_ref, ya_ref, wbs_ref, wba_ref, gla_ref, glb_ref, a_ref, b_ref, m_ref):
        a = _dot(ys_ref[...], wbs_ref[...])
        b = _dot(ya_ref[...], wba_ref[...])
        a_ref[...] = a.astype(BF)
        b_ref[...] = b.astype(BF)
        m_ref[...] = (jax.nn.sigmoid(gla_ref[...].astype(F32)) * a + jax.nn.sigmoid(glb_ref[...].astype(F32)) * b).astype(BF)

    return pl.pallas_call(
        body, name="branch_merge", grid=(T // tm, nj),
        in_specs=[
            pl.BlockSpec((tm, SGU_W), lambda i, j: (i, 0)),
            pl.BlockSpec((tm, ATTN_W), lambda i, j: (i, 0)),
            pl.BlockSpec((SGU_W, tn), lambda i, j: (0, j)),
            pl.BlockSpec((ATTN_W, tn), lambda i, j: (0, j)),
            pl.BlockSpec((tm, tn), lambda i, j: (i, j)),
            pl.BlockSpec((tm, tn), lambda i, j: (i, j + nj)),
        ],
        out_specs=[pl.BlockSpec((tm, tn), lambda i, j: (i, j))] * 3,
        out_shape=[jax.ShapeDtypeStruct((T, D_MODEL), BF)] * 3,
        compiler_params=_params("parallel", "parallel"),
    )(ysgu, yattn, w_bs, w_ba, gl, gl)


def _row_spec(tr, width):
    return pl.BlockSpec((tr, width), lambda i: (i, 0))


def _vec_spec(width):
    return pl.BlockSpec((1, width), lambda i: (0, 0))


def _rms_fwd(x, g, shards, *, tr=256):
    T = x.shape[0]
    tr = min(tr, T)
    n_steps = T // tr
    k = len(shards)

    def body(x_ref, g_ref, *refs):
        step = pl.program_id(0)
        gather_start, gather_forward, gather_finish = _gather_phases(refs[:k], refs[k + 1:2 * k + 1], *refs[2 * k + 1:])
        pl.when(step == 0)(gather_start)
        pl.when(step == (3 * n_steps) // 4)(gather_forward)
        xv = x_ref[...]
        r = lax.rsqrt(jnp.mean(xv * xv, axis=-1, keepdims=True) + EPS)
        refs[k][...] = ((xv * r) * g_ref[...]).astype(BF)
        pl.when(step == n_steps - 1)(gather_finish)

    outs = pl.pallas_call(
        body, name="rms_fwd", grid=(n_steps,),
        in_specs=[_row_spec(tr, D_MODEL), _vec_spec(D_MODEL)] + [HBM] * k, out_specs=[_row_spec(tr, D_MODEL)] + [HBM] * k,
        out_shape=[jax.ShapeDtypeStruct((T, D_MODEL), BF)] + _gathered_shapes(shards),
        scratch_shapes=_gather_semaphores(k), compiler_params=_params("arbitrary"),
    )(x, g, *shards)
    return outs[0], outs[1:]


def _mixer_out_fwd(o, x, g_post, g_pre):
    r = lax.rsqrt(jnp.mean(o * o, axis=-1, keepdims=True) + EPS)
    h1 = x + (o * r) * g_post
    r2 = lax.rsqrt(jnp.mean(h1 * h1, axis=-1, keepdims=True) + EPS)
    return o, h1, (h1 * r2) * g_pre


def _matmul_rows(pairs, *, nt, rows, vecs, row_outs, n_sums, epilogue, name, tm=512, scatter=()):
    M = pairs[0][0].shape[0]
    N = pairs[0][1].shape[0] if nt else pairs[0][1].shape[1]
    tm = min(tm, M)
    n_steps = M // tm
    n_pairs, n_rows, n_vecs, n_out, n_scatter = len(pairs), len(rows), len(vecs), len(row_outs), len(scatter)

    def body(*refs):
        groups, at = [], 2 * n_pairs
        for count in (n_rows, n_vecs, n_scatter, n_out, n_sums, n_scatter):
            groups.append(refs[at:at + count])
            at += count
        r_refs, v_refs, b_refs, o_refs, s_refs, got_refs = groups
        sems = refs[at:]
        step = pl.program_id(0)
        if n_scatter:
            scatter_start, scatter_finish = _scatter_phases(b_refs, got_refs, *sems)
            pl.when(step == 0)(scatter_start)

        @pl.when(step == 0)
        def _():
            for s_ref in s_refs:
                s_ref[...] = jnp.zeros_like(s_ref)

        acc = None
        for p in range(n_pairs):
            a_ref, b_ref = refs[2 * p], refs[2 * p + 1]
            d = _dot_nt(a_ref[...], b_ref[...]) if nt else _dot(a_ref[...], b_ref[...])
            acc = d if acc is None else acc + d
        outs = epilogue(acc, *[r[...] for r in r_refs], *[v[...] for v in v_refs])
        for o_ref, o in zip(o_refs, outs[:n_out], strict=True):
            o_ref[...] = o.astype(o_ref.dtype)
        for s_ref, term in zip(s_refs, outs[n_out:], strict=True):
            s_ref[...] += jnp.sum(term, axis=0, keepdims=True)
        if n_scatter:
            pl.when(step == n_steps - 1)(scatter_finish)

    in_specs, args = [], []
    for a, b in pairs:
        in_specs += [_row_spec(tm, a.shape[1]), pl.BlockSpec(b.shape, lambda i: (0, 0))]
        args += [a, b]
    outs = pl.pallas_call(
        body, name=name, grid=(n_steps,),
        in_specs=in_specs + [_row_spec(tm, N)] * n_rows + [_vec_spec(N)] * n_vecs + [HBM] * n_scatter,
        out_specs=[_row_spec(tm, N)] * n_out + [_vec_spec(N)] * n_sums + [HBM] * n_scatter,
        out_shape=[jax.ShapeDtypeStruct((M, N), dt) for dt in row_outs] + [jax.ShapeDtypeStruct((1, N), F32)] * n_sums
        + (_scattered_shapes(scatter) if n_scatter else []),
        scratch_shapes=_scatter_semaphores(n_scatter) if n_scatter else [],
        compiler_params=pltpu.CompilerParams(dimension_semantics=("arbitrary",), vmem_limit_bytes=BIG_VMEM),
    )(*args, *rows, *vecs, *scatter)
    return outs


def _loss_head(dn, h1, target, g):
    r = lax.rsqrt(jnp.mean(dn * dn, axis=-1, keepdims=True) + EPS)
    err = h1 + (dn * r) * g - target
    dy = err * (1.0 / D_MODEL)
    ddn, dg_terms = _rms_bwd(dn, g, dy)
    return dy, ddn, err * err, dg_terms


def _mixer_out_bwd(dxn2, h1, dy, o, g_pre, g_post):
    da, dg_pre_terms = _rms_bwd(h1, g_pre, dxn2)
    dh1 = dy + da
    do, dg_post_terms = _rms_bwd(o, g_post, dh1)
    return dh1, do, dg_pre_terms, dg_post_terms


def _input_norm_bwd(dxn, x, dh1, g):
    da, dg_terms = _rms_bwd(x, g, dxn)
    return dh1 + da, dg_terms


def _sgu_norm(z_tile, g, b):
    gz = _gelu(z_tile)
    u, vv = gz[:, :SGU_W], gz[:, SGU_W:]
    xc = vv - jnp.mean(vv, axis=-1, keepdims=True)
    rstd = lax.rsqrt(jnp.mean(xc * xc, axis=-1, keepdims=True) + EPS)
    xhat = xc * rstd
    return u, xhat, rstd, xhat * g + b


def _sgu_mix(w_ref, v_bf, first_half):
    parts = []
    for p in range(N_GROUPS // 2):
        vp = v_bf[:, p * LANES:(p + 1) * LANES]
        parts.append(jnp.where(first_half, _dot(w_ref[2 * p], vp), _dot(w_ref[2 * p + 1], vp)))
    return jnp.concatenate(parts, axis=1)


def _sgu_fwd(z, g_sgu, b_sgu, ws, bias_plane, *, tm=512):
    T = z.shape[0]
    tm = min(tm, T)

    def body(z_ref, g_ref, b_ref, ws_ref, bp_ref, y_ref):
        u, _, _, vn = _sgu_norm(z_ref[...], g_ref[...], b_ref[...])
        vn_bf = vn.astype(BF)
        first_half = lax.broadcasted_iota(jnp.int32, (CHUNK, LANES), 1) < HEAD_DIM
        for c in range(tm // CHUNK):
            rows = slice(c * CHUNK, (c + 1) * CHUNK)
            s = _sgu_mix(ws_ref, vn_bf[rows, :], first_half) + bp_ref[...]
            y_ref[rows, :] = (u[rows, :] * s).astype(BF)

    return pl.pallas_call(
        body, name="sgu_fwd", grid=(T // tm,),
        in_specs=[_row_spec(tm, 2 * SGU_W), _vec_spec(SGU_W), _vec_spec(SGU_W),
                  pl.BlockSpec((N_GROUPS, CHUNK, CHUNK), lambda i: (0, 0, 0)),
                  pl.BlockSpec((CHUNK, SGU_W), lambda i: (0, 0))],
        out_specs=_row_spec(tm, SGU_W), out_shape=jax.ShapeDtypeStruct((T, SGU_W), BF),
        compiler_params=_params("parallel"),
    )(z, g_sgu, b_sgu, ws, bias_plane)


def _sgu_bwd(dy, z, g_sgu, b_sgu, ws, ws_t, bias_plane, exchange, *, tm=512):
    T = z.shape[0]
    tm = min(tm, T)
    n_steps = T // tm
    k = len(exchange)

    def body(dy_ref, z_ref, g_ref, b_ref, ws_ref, wst_ref, bp_ref, *refs):
        x_refs, (dz_ref, dws_ref, dbs_ref, dg_ref, db_ref), r_refs = refs[:k], refs[k:k + 5], refs[k + 5:2 * k + 5]
        dbp_ref, send_sems, recv_sems = refs[2 * k + 5:]
        step = pl.program_id(0)
        exchange_start, exchange_finish = _exchange_phases(x_refs, r_refs, send_sems, recv_sems)
        pl.when(step == 0)(exchange_start)

        @pl.when(step == 0)
        def _():
            dws_ref[...] = jnp.zeros_like(dws_ref)
            dg_ref[...] = jnp.zeros_like(dg_ref)
            db_ref[...] = jnp.zeros_like(db_ref)
            dbp_ref[...] = jnp.zeros_like(dbp_ref)

        g = g_ref[...]
        zt = z_ref[...]
        u, xhat, rstd, vn = _sgu_norm(zt, g, b_ref[...])
        vn_bf = vn.astype(BF)
        first_half = lax.broadcasted_iota(jnp.int32, (CHUNK, LANES), 1) < HEAD_DIM
        dyv = dy_ref[...]
        dg_acc = jnp.zeros((1, SGU_W), F32)
        db_acc = jnp.zeros((1, SGU_W), F32)
        for c in range(tm // CHUNK):
            rows = slice(c * CHUNK, (c + 1) * CHUNK)
            v_c = vn_bf[rows, :]
            s = _sgu_mix(ws_ref, v_c, first_half) + bp_ref[...]
            dy_c = dyv[rows, :]
            du = dy_c * s
            dsv = dy_c * u[rows, :]
            dbp_ref[...] += dsv
            ds_bf = dsv.astype(BF)
            zero = jnp.zeros((CHUNK, LANES), BF)
            for p in range(N_GROUPS // 2):
                dsp = ds_bf[:, p * LANES:(p + 1) * LANES]
                vp = v_c[:, p * LANES:(p + 1) * LANES]
                dws_ref[2 * p] += _dot_nt(jnp.where(first_half, dsp, zero), vp)
                dws_ref[2 * p + 1] += _dot_nt(jnp.where(first_half, zero, dsp), vp)
            dvn = _sgu_mix(wst_ref, ds_bf, first_half)
            xh = xhat[rows, :]
            dxh = dvn * g
            dvv = rstd[rows, :] * (dxh - jnp.mean(dxh, axis=-1, keepdims=True)
                                   - xh * jnp.mean(dxh * xh, axis=-1, keepdims=True))
            dg_acc += jnp.sum(dvn * xh, axis=0, keepdims=True)
            db_acc += jnp.sum(dvn, axis=0, keepdims=True)
            dgz = jnp.concatenate([du, dvv], axis=1)
            dz_ref[rows, :] = (dgz * _gelu_grad(zt[rows, :])).astype(BF)
        dg_ref[...] += dg_acc
        db_ref[...] += db_acc

        @pl.when(step == n_steps - 1)
        def _():
            r = lax.broadcasted_iota(jnp.int32, (CHUNK, CHUNK), 0)
            cidx = lax.broadcasted_iota(jnp.int32, (CHUNK, CHUNK), 1)
            causal = (cidx <= r).astype(F32)
            for gi in range(N_GROUPS):
                dws_ref[gi] = dws_ref[gi] * causal
            lane = lax.broadcasted_iota(jnp.int32, (CHUNK, LANES), 1)
            out = jnp.zeros((CHUNK, LANES), F32)
            dbp = dbp_ref[...]
            for gi in range(N_GROUPS):
                col = jnp.sum(dbp[:, gi * HEAD_DIM:(gi + 1) * HEAD_DIM], axis=1, keepdims=True)
                out = jnp.where(lane == gi, col, out)
            dbs_ref[...] = out
            exchange_finish()

    w_spec = pl.BlockSpec((N_GROUPS, CHUNK, CHUNK), lambda i: (0, 0, 0))
    plane = pl.BlockSpec((CHUNK, SGU_W), lambda i: (0, 0))
    outs = pl.pallas_call(
        body, name="sgu_bwd", grid=(n_steps,),
        in_specs=[_row_spec(tm, SGU_W), _row_spec(tm, 2 * SGU_W), _vec_spec(SGU_W), _vec_spec(SGU_W), w_spec, w_spec, plane]
        + [HBM] * k,
        out_specs=[_row_spec(tm, 2 * SGU_W), w_spec, pl.BlockSpec((CHUNK, LANES), lambda i: (0, 0)),
                   _vec_spec(SGU_W), _vec_spec(SGU_W)] + [HBM] * k,
        out_shape=[jax.ShapeDtypeStruct((T, 2 * SGU_W), BF), jax.ShapeDtypeStruct((N_GROUPS, CHUNK, CHUNK), F32),
                   jax.ShapeDtypeStruct((CHUNK, LANES), F32), jax.ShapeDtypeStruct((1, SGU_W), F32),
                   jax.ShapeDtypeStruct((1, SGU_W), F32)] + _exchanged_shapes(exchange),
        scratch_shapes=[pltpu.VMEM((CHUNK, SGU_W), F32)] + _exchange_semaphores(k),
        compiler_params=_params("arbitrary"),
    )(dy, z, g_sgu, b_sgu, ws, ws_t, bias_plane, *exchange)
    return outs[:5], outs[5:]


def _tri(n, upper):
    r = lax.broadcasted_iota(jnp.int32, (n, n), 0)
    c = lax.broadcasted_iota(jnp.int32, (n, n), 1)
    return ((c >= r) if upper else (c <= r)).astype(BF)


def _scan_dot(tri, x):
    hi, mid, lo = _split3(x)
    return (_dot(tri, hi.astype(BF)) + _dot(tri, mid.astype(BF))) + _dot(tri, lo.astype(BF))


def _with_lanes(base, lane, start, cols):
    out = base
    for k, col in enumerate(cols):
        if col is not None:
            out = jnp.where(lane == start + k, col, out)
    return out


def _logit_bound(q_norm, k_norm):
    return NORM_SLACK * q_norm * k_norm + 1.0


ATTN_TILE = 512
SKIP_BELOW = -110.0
NORM_SLACK = 1.001
BOUNDED_GAP = 60.0


def _attn_prep(qkv, fl, b_forget, *, tp=ATTN_TILE):
    T = qkv.shape[0]
    tp = min(tp, T)
    head_sum, gather6, place_q, place_k, place_v = (jnp.asarray(m, BF) for m in _attn_placements())

    def body(qkv_ref, fl_ref, bf_ref, hs_ref, g6_ref, pq_ref, pk_ref, pv_ref, qf_ref, kl_ref, vl_ref, st_ref, carry_ref, kmax_ref):
        @pl.when(pl.program_id(0) == 0)
        def _():
            carry_ref[...] = jnp.zeros_like(carry_ref)
            kmax_ref[...] = jnp.zeros_like(kmax_ref)

        x = fl_ref[...] + bf_ref[...]
        logf = jnp.minimum(x, 0.0) - jnp.log(1.0 + jnp.exp(-jnp.abs(x)))
        cum = _scan_dot(_tri(tp, upper=False), logf) + carry_ref[...]
        carry_ref[...] = cum[tp - 1:tp, :]

        def head_norms(block):
            sq = block * block
            hi = sq.astype(BF)
            return _dot(hi, hs_ref[...]) + _dot((sq - hi.astype(F32)).astype(BF), hs_ref[...])

        qkvv = qkv_ref[...]
        q_norm = NORM_SLACK * jnp.sqrt(head_norms(qkvv[:, :ATTN_W].astype(F32) * Q_SCALE))
        kn = NORM_SLACK * jnp.sqrt(jnp.max(head_norms(qkvv[:, ATTN_W:2 * ATTN_W].astype(F32)), axis=0, keepdims=True))
        k_seen = jnp.maximum(kmax_ref[...], kn)
        kmax_ref[...] = k_seen
        rows = (jnp.max(q_norm, axis=0, keepdims=True), kn, jnp.max(cum, axis=0, keepdims=True),
                jnp.min(cum, axis=0, keepdims=True), k_seen)
        st_ref[...] = jnp.zeros_like(st_ref)
        for k, row in enumerate(rows):
            st_ref[0, k:k + 1, :] = row
        parts = jnp.concatenate([p.astype(BF) for p in _split3(cum) + _split3(-_logit_bound(q_norm, k_seen))], axis=1)
        lane = lax.broadcasted_iota(jnp.int32, (tp, LANES), 1)
        side = jnp.where(lane == 6 * N_HEADS, 1.0, _dot(parts, g6_ref[...])).astype(BF)
        for h in range(N_HEADS):
            pair = slice((h // 2) * LANES, (h // 2 + 1) * LANES)
            for out_ref, block, place_ref in ((qf_ref, qkvv[:, :ATTN_W], pq_ref), (kl_ref, qkvv[:, ATTN_W:2 * ATTN_W], pk_ref),
                                              (vl_ref, qkvv[:, 2 * ATTN_W:], pv_ref)):
                out_ref[h] = _dot(jnp.concatenate([block[:, pair], side], axis=1), place_ref[h]).astype(BF)

    head_spec = pl.BlockSpec((N_HEADS, tp, LANES), lambda i: (0, i, 0))
    whole = lambda a: pl.BlockSpec(a.shape, lambda i: (0,) * a.ndim)
    return pl.pallas_call(
        body, name="attn_prep", grid=(T // tp,),
        in_specs=[_row_spec(tp, 3 * ATTN_W), _row_spec(tp, LANES), _vec_spec(LANES)]
        + [whole(m) for m in (head_sum, gather6, place_q, place_k, place_v)],
        out_specs=[head_spec] * 3 + [pl.BlockSpec((1, N_HEADS, LANES), lambda i: (i, 0, 0))],
        out_shape=[jax.ShapeDtypeStruct((N_HEADS, T, LANES), BF)] * 3 + [jax.ShapeDtypeStruct((T // tp, N_HEADS, LANES), F32)],
        scratch_shapes=[pltpu.VMEM((1, LANES), F32), pltpu.VMEM((1, LANES), F32)], compiler_params=_params("arbitrary"),
    )(qkv, fl, b_forget, head_sum, gather6, place_q, place_k, place_v)


def _attn_placements():
    head_sum = np.zeros((ATTN_W, LANES), np.float32)
    head_sum[np.arange(ATTN_W), np.arange(ATTN_W) // HEAD_DIM] = 1.0
    gather6 = np.zeros((6 * LANES, LANES), np.float32)
    for j in range(6):
        gather6[j * LANES + np.arange(N_HEADS), j * N_HEADS + np.arange(N_HEADS)] = 1.0
    place = np.zeros((3, N_HEADS, 2 * LANES, LANES), np.float32)
    one = LANES + 6 * N_HEADS
    d = np.arange(HEAD_DIM)
    for h in range(N_HEADS):
        side = lambda j: LANES + j * N_HEADS + h
        place[0, h, (h % 2) * HEAD_DIM + d, d] = Q_SCALE
        place[1:, h, (h % 2) * HEAD_DIM + d, d] = 1.0
        for j in range(3):
            place[0, h, side(j), HEAD_DIM + j] = 1.0
            place[0, h, one, HEAD_DIM + 3 + j] = 1.0
            place[0, h, side(3 + j), HEAD_DIM + 6 + j] = 1.0
            place[1, h, one, HEAD_DIM + j] = 1.0
            place[1, h, side(j), HEAD_DIM + 3 + j] = -1.0
            place[1, h, one, HEAD_DIM + 6 + j] = 1.0
            place[2, h, one, HEAD_DIM + j] = 1.0
    return head_sum, gather6, place[0], place[1], place[2]


def _attn_ranges(stats):
    qn, kn, cmax, cmin, k_seen = (stats[:, k, :N_HEADS].T for k in range(5))
    n = qn.shape[1]
    bounded = (2.0 * _logit_bound(qn, k_seen) <= BOUNDED_GAP).reshape(N_HEADS // 2, 2, n).all(axis=1)
    reach = NORM_SLACK * qn * (jnp.max(kn, axis=1, keepdims=True) + kn) + cmax
    i = jnp.arange(n)[None, :, None]
    j = jnp.arange(n)[None, None, :]
    need = ((reach[:, :, None] - cmin[:, None, :] >= SKIP_BELOW) | (i == j)) & (j <= i)
    first = jnp.min(jnp.where(need, j, n), axis=2).reshape(N_HEADS // 2, 2, n).min(axis=1)
    last = jnp.max(jnp.where(need, i, -1), axis=1).reshape(N_HEADS // 2, 2, n).max(axis=1)
    return first.reshape(-1).astype(F32), last.reshape(-1).astype(F32), bounded.reshape(-1).astype(F32)


def _pair_block(t):
    return pl.BlockSpec((2, t, LANES), lambda p, i, *_: (p, i, 0))


def _pair_full(T):
    return pl.BlockSpec((2, T, LANES), lambda p, i, *_: (p, 0, 0))


def _packed_block(t):
    return pl.BlockSpec((t, LANES), lambda p, i, *_: (i, p))


def _causal(t, keys_in_rows=False):
    r = lax.broadcasted_iota(jnp.int32, (t, t), 0)
    c = lax.broadcasted_iota(jnp.int32, (t, t), 1)
    return (r <= c) if keys_in_rows else (c <= r)


def _tile_rows(j, t):
    return pl.ds(pl.multiple_of(j * t, t), t)


def _attn_call(body, name, tile_scalars, operands, in_specs, out_specs, out_shape, scratch_shapes, n_tiles):
    return pl.pallas_call(
        body, name=name,
        grid_spec=pltpu.PrefetchScalarGridSpec(
            num_scalar_prefetch=len(tile_scalars), grid=(N_HEADS // 2, n_tiles), in_specs=in_specs, out_specs=out_specs,
            scratch_shapes=scratch_shapes),
        out_shape=out_shape, compiler_params=_params("arbitrary", "arbitrary"),
    )(*tile_scalars, *operands)


def _attn_fwd(qf, kl, vl, first, bounded, shards, *, tq=ATTN_TILE):
    T = qf.shape[1]
    tq = min(tq, T)
    n = T // tq
    n_steps = (N_HEADS // 2) * n
    k = len(shards)

    def body(first_ref, bounded_ref, qf_ref, kl_ref, vl_ref, *refs):
        w_refs, (o_ref, of_ref, ql_ref), g_refs = refs[:k], refs[k:k + 3], refs[k + 3:2 * k + 3]
        m_ref, acc_ref, send_sems, recv_sems = refs[2 * k + 3:]
        i = pl.program_id(1)
        tile = pl.program_id(0) * n + i
        gather_start, gather_forward, gather_finish = _gather_phases(w_refs, g_refs, send_sems, recv_sems)
        pl.when(tile == 0)(gather_start)
        pl.when(tile == (3 * n_steps) // 4)(gather_forward)
        start = first_ref[tile].astype(jnp.int32)
        is_bounded = bounded_ref[tile] > 0.5
        acc_ref[...] = jnp.zeros_like(acc_ref)
        diagonal = _tile_rows(i, tq)
        causal = _causal(tq)

        def logits(hh, rows):
            return _dot_nt(qf_ref[hh], kl_ref[hh, rows, :])

        @pl.when(is_bounded)
        def _():
            m_ref[...] = jnp.zeros_like(m_ref)

            def update(hh, s, rows):
                acc_ref[hh] += _dot(jnp.exp(s).astype(BF), vl_ref[hh, rows, :])

            def step(j, carry):
                for hh in range(2):
                    update(hh, logits(hh, _tile_rows(j, tq)), _tile_rows(j, tq))
                return carry

            lax.fori_loop(start, i, step, 0)
            for hh in range(2):
                update(hh, jnp.where(causal, logits(hh, diagonal), NEG), diagonal)

        @pl.when(jnp.logical_not(is_bounded))
        def _():
            m_ref[...] = jnp.full_like(m_ref, NEG)

            def update(hh, s, rows):
                m_old = m_ref[hh]
                m_new = jnp.maximum(m_old, jnp.max(s, axis=1, keepdims=True))
                p = jnp.exp(s - m_new)
                acc_ref[hh] = jnp.exp(m_old - m_new) * acc_ref[hh] + _dot(p.astype(BF), vl_ref[hh, rows, :])
                m_ref[hh] = m_new

            def step(j, carry):
                for hh in range(2):
                    update(hh, logits(hh, _tile_rows(j, tq)), _tile_rows(j, tq))
                return carry

            lax.fori_loop(start, i, step, 0)
            for hh in range(2):
                update(hh, jnp.where(causal, logits(hh, diagonal), NEG), diagonal)

        lane = lax.broadcasted_iota(jnp.int32, (tq, LANES), 1)
        outs = []
        for hh in range(2):
            q = qf_ref[hh].astype(F32)
            acc = acc_ref[hh]
            l = acc[:, HEAD_DIM:HEAD_DIM + 1]
            outs.append(acc[:, :HEAD_DIM] / l)
            at = HEAD_DIM + 6
            neg_bound = (q[:, at:at + 1] + q[:, at + 1:at + 2]) + q[:, at + 2:at + 3]
            ql_ref[hh] = _with_lanes(q, lane, at, _split3(neg_bound - (m_ref[hh] + jnp.log(l)))).astype(BF)
        o = jnp.concatenate(outs, axis=1)
        o_ref[...] = o.astype(BF)
        of_ref[...] = o
        pl.when(tile == n_steps - 1)(gather_finish)

    outs = _attn_call(
        body, "attn_fwd", (first, bounded), (qf, kl, vl, *shards),
        [_pair_block(tq), _pair_full(T), _pair_full(T)] + [HBM] * k,
        [_packed_block(tq), _packed_block(tq), _pair_block(tq)] + [HBM] * k,
        [jax.ShapeDtypeStruct((T, ATTN_W), BF), jax.ShapeDtypeStruct((T, ATTN_W), F32),
         jax.ShapeDtypeStruct((N_HEADS, T, LANES), BF)] + _gathered_shapes(shards),
        [pltpu.VMEM((2, tq, 1), F32), pltpu.VMEM((2, tq, LANES), F32)] + _gather_semaphores(k), n)
    return outs[0], outs[1], outs[2], outs[3:]


def _attn_bwd_prep(dya, of, *, tr=ATTN_TILE):
    T = dya.shape[0]
    tr = min(tr, T)
    head_sum, gather6 = (jnp.asarray(m, BF) for m in _attn_placements()[:2])
    gather3, place_do = gather6[:3 * LANES], _delta_placement()

    def body(d_ref, o_ref, hs_ref, g3_ref, p_ref, do_ref):
        dv = d_ref[...]
        delta = sum(_dot(part.astype(BF), hs_ref[...]) for part in _split3(dv * o_ref[...]))
        side = _dot(jnp.concatenate([p.astype(BF) for p in _split3(-delta)], axis=1), g3_ref[...]).astype(BF)
        d_bf = dv.astype(BF)
        for h in range(N_HEADS):
            pair = slice((h // 2) * LANES, (h // 2 + 1) * LANES)
            do_ref[h] = _dot(jnp.concatenate([d_bf[:, pair], side], axis=1), p_ref[h]).astype(BF)

    whole = lambda a: pl.BlockSpec(a.shape, lambda i: (0,) * a.ndim)
    return pl.pallas_call(
        body, name="attn_bwd_prep", grid=(T // tr,),
        in_specs=[_row_spec(tr, ATTN_W), _row_spec(tr, ATTN_W), whole(head_sum), whole(gather3), whole(place_do)],
        out_specs=pl.BlockSpec((N_HEADS, tr, LANES), lambda i: (0, i, 0)),
        out_shape=jax.ShapeDtypeStruct((N_HEADS, T, LANES), BF), compiler_params=_params("parallel"),
    )(dya, of, head_sum, gather3, place_do)


def _delta_placement():
    place = np.zeros((N_HEADS, 2 * LANES, LANES), np.float32)
    d = np.arange(HEAD_DIM)
    for h in range(N_HEADS):
        place[h, (h % 2) * HEAD_DIM + d, d] = 1.0
        for j in range(3):
            place[h, LANES + j * N_HEADS + h, HEAD_DIM + j] = 1.0
    return jnp.asarray(place, BF)


def _attn_bwd(kl, vl, ql, do, last, chip_sums, *, tk=ATTN_TILE):
    T = ql.shape[1]
    tk = min(tk, T)
    n = T // tk
    n_steps = (N_HEADS // 2) * n
    m = len(chip_sums)

    def body(last_ref, kl_ref, vl_ref, ql_ref, do_ref, *refs):
        b_refs, (dq_ref, dk_ref, dv_ref, extq_ref, extk_ref), r_refs = refs[:m], refs[m:m + 5], refs[m + 5:2 * m + 5]
        dq_acc, dk_acc, dv_acc, send_sems, recv_sems = refs[2 * m + 5:]
        j = pl.program_id(1)
        tile = pl.program_id(0) * n + j
        scatter_start, scatter_finish = _scatter_phases(b_refs, r_refs, send_sems, recv_sems)
        pl.when(tile == 0)(scatter_start)

        @pl.when(j == 0)
        def _():
            dq_acc[...] = jnp.zeros_like(dq_acc)

        dk_acc[...] = jnp.zeros_like(dk_acc)
        dv_acc[...] = jnp.zeros_like(dv_acc)

        def block(hh, rows, mask):
            qi, di, k = ql_ref[hh, rows, :], do_ref[hh, rows, :], kl_ref[hh]
            p_t = jnp.exp(_dot_nt(k, qi))
            if mask is not None:
                p_t = jnp.where(mask, p_t, 0.0)
            ds_t = (p_t * _dot_nt(vl_ref[hh], di)).astype(BF)
            dk_acc[hh] += _dot(ds_t, qi)
            dv_acc[hh] += _dot(p_t.astype(BF), di)
            dq_acc[hh, rows, :] += _dot_tn(ds_t, k)

        causal_t = _causal(tk, keys_in_rows=True)
        for hh in range(2):
            block(hh, _tile_rows(j, tk), causal_t)

        def step(i, carry):
            for hh in range(2):
                block(hh, _tile_rows(i, tk), None)
            return carry

        lax.fori_loop(j + 1, last_ref[pl.program_id(0) * n + j].astype(jnp.int32) + 1, step, 0)
        dk_ref[...] = jnp.concatenate([dk_acc[hh][:, :HEAD_DIM] for hh in range(2)], axis=1).astype(BF)
        dv_ref[...] = jnp.concatenate([dv_acc[hh][:, :HEAD_DIM] for hh in range(2)], axis=1).astype(BF)
        extk_ref[...] = jnp.concatenate([dk_acc[hh][:, HEAD_DIM:] for hh in range(2)], axis=1)

        @pl.when(j == n - 1)
        def _():
            dq_ref[...] = jnp.concatenate([dq_acc[hh][:, :HEAD_DIM] * Q_SCALE for hh in range(2)], axis=1).astype(BF)
            extq_ref[...] = jnp.concatenate([dq_acc[hh][:, HEAD_DIM:] for hh in range(2)], axis=1)

        pl.when(tile == n_steps - 1)(scatter_finish)

    whole = pl.BlockSpec((T, LANES), lambda p, j, *_: (0, p))
    outs = pl.pallas_call(
        body, name="attn_bwd",
        grid_spec=pltpu.PrefetchScalarGridSpec(
            num_scalar_prefetch=1, grid=(N_HEADS // 2, n),
            in_specs=[_pair_block(tk), _pair_block(tk), _pair_full(T), _pair_full(T)] + [HBM] * m,
            out_specs=[whole, _packed_block(tk), _packed_block(tk), whole, _packed_block(tk)] + [HBM] * m,
            scratch_shapes=[pltpu.VMEM((2, T, LANES), F32), pltpu.VMEM((2, tk, LANES), F32), pltpu.VMEM((2, tk, LANES), F32)]
            + _scatter_semaphores(m)),
        out_shape=[jax.ShapeDtypeStruct((T, ATTN_W), BF)] * 3 + [jax.ShapeDtypeStruct((T, ATTN_W), F32)] * 2
        + _scattered_shapes(chip_sums),
        compiler_params=pltpu.CompilerParams(dimension_semantics=("arbitrary", "arbitrary"), vmem_limit_bytes=BIG_VMEM),
    )(last, kl, vl, ql, do, *chip_sums)
    return outs[:5], outs[5:]


def _forget_bwd(ext_q, ext_k, fl, b_forget, *, tp=256):
    T = fl.shape[0]
    tp = min(tp, T)
    n = T // tp

    def body(eq_ref, ek_ref, fl_ref, bf_ref, dfl_ref, dbf_ref, carry_ref):
        @pl.when(pl.program_id(0) == 0)
        def _():
            carry_ref[...] = jnp.zeros_like(carry_ref)
            dbf_ref[...] = jnp.zeros_like(dbf_ref)

        lane = lax.broadcasted_iota(jnp.int32, (tp, LANES), 1)
        eq, ek = eq_ref[...], ek_ref[...]
        cols = [eq[:, h * HEAD_DIM:h * HEAD_DIM + 1] - ek[:, h * HEAD_DIM + 3:h * HEAD_DIM + 4] for h in range(N_HEADS)]
        dcum = _with_lanes(jnp.zeros((tp, LANES), F32), lane, 0, cols)
        suffix = _scan_dot(_tri(tp, upper=True), dcum) + carry_ref[...]
        carry_ref[...] = suffix[0:1, :]
        x = fl_ref[...] + bf_ref[...]
        dfl = jnp.where(lane < N_HEADS, suffix / (1.0 + jnp.exp(x)), 0.0)
        dfl_ref[...] = dfl.astype(BF)
        dbf_ref[...] += jnp.sum(dfl, axis=0, keepdims=True)

    rev = lambda w: pl.BlockSpec((tp, w), lambda i: (n - 1 - i, 0))
    return pl.pallas_call(
        body, name="forget_bwd", grid=(n,),
        in_specs=[rev(ATTN_W), rev(ATTN_W), rev(LANES), _vec_spec(LANES)],
        out_specs=[rev(LANES), _vec_spec(LANES)],
        out_shape=[jax.ShapeDtypeStruct((T, LANES), BF), jax.ShapeDtypeStruct((1, LANES), F32)],
        scratch_shapes=[pltpu.VMEM((1, LANES), F32)], compiler_params=_params("arbitrary"),
    )(ext_q, ext_k, fl, b_forget)


def _adamw(w, g, m, v, *, name, tr=256):
    _, rows, cols = w.shape
    tr = tr if rows % tr == 0 else rows

    def body(w_ref, g_ref, m_ref, v_ref, go_ref, d_ref, nm_ref, nv_ref):
        gv = g_ref[...]
        go_ref[...] = gv
        nm = ADAM_B1 * m_ref[...] + (1.0 - ADAM_B1) * gv
        nv = ADAM_B2 * v_ref[...] + (1.0 - ADAM_B2) * (gv * gv)
        m_hat = nm / (1.0 - ADAM_B1 ** ADAM_STEP)
        v_hat = nv / (1.0 - ADAM_B2 ** ADAM_STEP)
        d_ref[...] = -ADAM_LR * (m_hat / (jnp.sqrt(v_hat) + ADAM_EPS) + ADAM_WD * w_ref[...])
        nm_ref[...] = nm
        nv_ref[...] = nv

    spec = pl.BlockSpec((None, tr, cols), lambda i: (0, i, 0))
    return pl.pallas_call(
        body, name=name, grid=(rows // tr,), in_specs=[spec, pl.BlockSpec((tr, cols), lambda i: (i, 0)), spec, spec],
        out_specs=[spec] * 4, out_shape=[jax.ShapeDtypeStruct((1, rows, cols), F32)] * 4,
        compiler_params=_params("parallel"),
    )(w, g, m, v)


HBM = pl.BlockSpec(memory_space=pltpu.HBM)
BF16_ROWS = 16


def _place():
    x, y, c = lax.axis_index("x"), lax.axis_index("y"), lax.axis_index("c")
    others = [(1 - x, y), (x, 1 - y), (1 - x, 1 - y)]
    return x, y, c, others


def _chip(xy):
    return 2 * xy[0] + xy[1]


def _row_halves(c, rows):
    half = rows // 2
    assert half % BF16_ROWS == 0
    return (pl.ds(pl.multiple_of(c * half, BF16_ROWS), half), pl.ds(pl.multiple_of((1 - c) * half, BF16_ROWS), half))


def _remote(src, dst, send_sems, recv_sems, k, to):
    return pltpu.make_async_remote_copy(src_ref=src, dst_ref=dst, send_sem=send_sems.at[k], recv_sem=recv_sems.at[k],
                                        device_id=to, device_id_type=MESH)


def _gathered_shapes(shards):
    return [jax.ShapeDtypeStruct((N_CHIPS,) + s.shape, s.dtype) for s in shards]


def _gather_semaphores(n):
    return [pltpu.SemaphoreType.DMA((6 * n,)), pltpu.SemaphoreType.DMA((6 * n,))]


def _gather_phases(w_refs, g_refs, send_sems, recv_sems):
    n = len(w_refs)
    x, y, c, others = _place()
    sibling, me = (x, y, 1 - c), _chip((x, y))
    halves = [_row_halves(c, w.shape[0]) for w in w_refs]

    def sent(a, j, o):
        mine, _ = halves[a]
        return _remote(w_refs[a].at[mine, :], g_refs[a].at[me, mine, :], send_sems, recv_sems, 6 * a + j, (*o, c))

    def passed(a, j, o):
        landed = g_refs[a].at[_chip(o), halves[a][0], :]
        return _remote(landed, landed, send_sems, recv_sems, 6 * a + 3 + j, sibling)

    def start():
        for a in range(n):
            for j, o in enumerate(others):
                sent(a, j, o).start()

    def forward():
        for j, o in enumerate(others):
            for a in range(n):
                landed = g_refs[a].at[_chip(o), halves[a][0], :]
                _remote(landed, landed, send_sems, recv_sems, 6 * a + j, (*o, c)).wait_recv()
                passed(a, j, o).start()

    def finish():
        for j, o in enumerate(others):
            for a in range(n):
                landed = g_refs[a].at[_chip(o), halves[a][1], :]
                _remote(landed, landed, send_sems, recv_sems, 6 * a + 3 + j, sibling).wait_recv()
        for a in range(n):
            for j, o in enumerate(others):
                sent(a, j, o).wait_send()
                passed(a, j, o).wait_send()

    return start, forward, finish


def _exchange_halves(arrays, *, name):
    n = len(arrays)

    def body(*refs):
        for phase in _exchange_phases(refs[:n], refs[n:2 * n], *refs[2 * n:]):
            phase()

    return pl.pallas_call(
        body, name=name, in_specs=[HBM] * n, out_specs=[HBM] * n, out_shape=_exchanged_shapes(arrays),
        scratch_shapes=_exchange_semaphores(n),
    )(*arrays)


def _exchanged_shapes(arrays):
    return [jax.ShapeDtypeStruct(s.shape[:-2] + (s.shape[-2] // 2, s.shape[-1]), F32) for s in arrays]


def _exchange_semaphores(n):
    return [pltpu.SemaphoreType.DMA((n,)), pltpu.SemaphoreType.DMA((n,))]


def _exchange_phases(g_refs, r_refs, send_sems, recv_sems):
    x, y, c, _ = _place()

    def copy(a):
        _, theirs = _row_halves(c, g_refs[a].shape[-2])
        src = g_refs[a].at[:, theirs, :] if len(g_refs[a].shape) == 3 else g_refs[a].at[theirs, :]
        return _remote(src, r_refs[a], send_sems, recv_sems, a, (x, y, 1 - c))

    def start():
        for a in range(len(g_refs)):
            copy(a).start()

    def finish():
        for a in range(len(g_refs)):
            copy(a).wait()

    return start, finish


def _scatter_to_owners(chip_sums):
    n = len(chip_sums)

    def body(*refs):
        for phase in _scatter_phases(refs[:n], refs[n:2 * n], *refs[2 * n:]):
            phase()

    return pl.pallas_call(
        body, name="scatter_to_owners", in_specs=[HBM] * n, out_specs=[HBM] * n,
        out_shape=_scattered_shapes(chip_sums), scratch_shapes=_scatter_semaphores(n),
    )(*chip_sums)


def _scattered_shapes(chip_sums):
    return [jax.ShapeDtypeStruct(b.shape if b.ndim == 3 else (N_CHIPS,) + b.shape, b.dtype) for b in chip_sums]


def _scatter_semaphores(n):
    return [pltpu.SemaphoreType.DMA((3 * n,)), pltpu.SemaphoreType.DMA((3 * n,))]


def _scatter_phases(b_refs, r_refs, send_sems, recv_sems):
    n = len(b_refs)
    x, y, c, others = _place()
    me = _chip((x, y))

    def sent(a, j, o):
        src = b_refs[a].at[_chip(o)] if len(b_refs[a].shape) == 3 else b_refs[a]
        return _remote(src, r_refs[a].at[me], send_sems, recv_sems, 3 * a + j, (*o, c))

    def start():
        for a in range(n):
            for j, o in enumerate(others):
                sent(a, j, o).start()

    def finish():
        for a in range(n):
            for j, o in enumerate(others):
                landed = r_refs[a].at[_chip(o)]
                _remote(landed, landed, send_sems, recv_sems, 3 * a + j, (*o, c)).wait_recv()
        for a in range(n):
            for j, o in enumerate(others):
                sent(a, j, o).wait_send()

    return start, finish


def _join_halves(totals):
    n = len(totals)

    def body(*refs):
        in_refs, out_refs, (send_sems, recv_sems) = refs[:n], refs[n:2 * n], refs[2 * n:]
        x, y, c, _ = _place()
        copies = []
        for a in range(n):
            mine, _ = _row_halves(c, in_refs[a].shape[0])
            copies.append(_remote(in_refs[a].at[mine, :], out_refs[a].at[mine, :], send_sems, recv_sems, a, (x, y, 1 - c)))
            copies[-1].start()
        for cp in copies:
            cp.wait()

    return pl.pallas_call(
        body, name="join_halves", in_specs=[HBM] * n, out_specs=[HBM] * n,
        out_shape=[jax.ShapeDtypeStruct(t.shape, F32) for t in totals], input_output_aliases={a: a for a in range(n)},
        scratch_shapes=[pltpu.SemaphoreType.DMA((n,)), pltpu.SemaphoreType.DMA((n,))],
    )(*totals)


ADD_ROWS = 128


def _add_sibling(g, r, place, *, name):
    lead, (half, cols) = g.shape[:-2], r.shape[-2:]
    tr = min(ADD_ROWS, half)
    nb = half // tr
    zeros = (0,) * len(lead)

    def body(place_ref, g_ref, r_ref, o_ref, ob_ref):
        s = g_ref[...] + r_ref[...]
        o_ref[...] = s
        ob_ref[...] = s.astype(BF)

    spec = pl.BlockSpec(lead + (tr, cols), lambda i, p: zeros + (i, 0))
    return pl.pallas_call(
        body, name=name,
        grid_spec=pltpu.PrefetchScalarGridSpec(
            num_scalar_prefetch=1, grid=(nb,),
            in_specs=[pl.BlockSpec(lead + (tr, cols), lambda i, p: zeros + (p[1] * nb + i, 0)), spec], out_specs=[spec, spec]),
        out_shape=[jax.ShapeDtypeStruct(r.shape, F32), jax.ShapeDtypeStruct(r.shape, BF)],
        compiler_params=_params("parallel"),
    )(place, g, r)


def _add_chips(own, received, place, *, name, own_slots):
    half, cols = received.shape[-2:]
    tr = min(ADD_ROWS, half)
    nb = half // tr

    def written(k, p):
        return jnp.where(p[0] == k, (k + 1) % N_CHIPS, k)

    def body(place_ref, own_ref, *refs):
        o_ref = refs[N_CHIPS]
        mine = own_ref[0] if own_slots else own_ref[...]
        if own_slots:
            acc = mine
            for k in range(N_CHIPS):
                acc = acc + jnp.where(place_ref[0] == k, 0.0, refs[k][0].astype(F32))
        else:
            terms = [jnp.where(place_ref[0] == k, mine, refs[k][0]) for k in range(N_CHIPS)]
            acc = ((terms[0] + terms[1]) + terms[2]) + terms[3]
        o_ref[...] = acc

    own_spec = (pl.BlockSpec((1, tr, cols), lambda i, p: (p[0], i, 0)) if own_slots
                else pl.BlockSpec((tr, cols), lambda i, p: (i, 0)))
    return pl.pallas_call(
        body, name=name,
        grid_spec=pltpu.PrefetchScalarGridSpec(
            num_scalar_prefetch=1, grid=(nb,),
            in_specs=[own_spec] + [pl.BlockSpec((1, tr, cols), functools.partial(lambda i, p, k: (written(k, p), i, 0), k=k))
                                   for k in range(N_CHIPS)],
            out_specs=pl.BlockSpec((tr, cols), lambda i, p: (p[1] * nb + i, 0))),
        out_shape=jax.ShapeDtypeStruct((2 * half, cols), F32), compiler_params=_params("parallel"),
    )(place, own, *([received] * N_CHIPS))


SHARDED = (("w_in", (D_MODEL, 4616), 1), ("w_branch_sgu", (SGU_W, D_MODEL), 1), ("w_branch_attn", (ATTN_W, D_MODEL), 1),
           ("w_out", (D_MODEL, D_MODEL), 0), ("w_up", (D_MODEL, D_FF), 1), ("w_down", (D_FF, D_MODEL), 0))
SMALL = (("g_mix_pre", (1, D_MODEL)), ("b_forget", (1, N_HEADS)), ("g_sgu", (1, SGU_W)), ("b_sgu", (1, SGU_W)),
         ("w_spatial", (N_GROUPS * CHUNK, CHUNK)), ("b_spatial", (N_GROUPS, CHUNK)), ("g_mix_post", (1, D_MODEL)),
         ("g_ffn_pre", (1, D_MODEL)), ("g_ffn_post", (1, D_MODEL)))
SMALL_ALIGN = 2 * ADD_ROWS


def _shard_shape(shape, axis):
    return tuple(s // N_CHIPS if a == axis else s for a, s in enumerate(shape))


def _slots_to_full(slots, axis):
    return slots.reshape(-1, slots.shape[2]) if axis == 0 else slots.transpose(1, 0, 2).reshape(slots.shape[1], -1)


def _full_to_slots(full, axis):
    if axis == 0:
        return full.reshape(N_CHIPS, -1, full.shape[1])
    return full.reshape(full.shape[0], N_CHIPS, -1).transpose(1, 0, 2)


def _small_rows(shape):
    return -(-(shape[0] * shape[1]) // (8 * LANES)) * 8


def _pack_small(values):
    parts = []
    for name, shape in SMALL:
        flat = values[name].reshape(-1)
        n = _small_rows(shape)
        parts.append(jnp.pad(flat, (0, n * LANES - flat.shape[0])).reshape(n, LANES))
    rows = sum(p.shape[0] for p in parts)
    pad = -(-rows // SMALL_ALIGN) * SMALL_ALIGN - rows
    return jnp.concatenate(parts + [jnp.zeros((pad, LANES), F32)], axis=0)


def _unpack_small(packed):
    out, row = {}, 0
    for name, shape in SMALL:
        n = _small_rows(shape)
        out[name] = packed[row:row + n].reshape(-1)[:shape[0] * shape[1]].reshape(shape)
        row += n
    return out


IN_Z, IN_Q, IN_K, IN_V, IN_F, IN_G, IN_END = 0, 1024, 1536, 2048, 2560, 2568, 4616


LATE_WEIGHTS = ("w_branch_sgu", "w_branch_attn", "w_out", "w_up", "w_down")
EARLY_GRADS = LATE_WEIGHTS


def _with_own_slot(shard, gathered, chip):
    return jnp.where(jnp.arange(N_CHIPS)[:, None, None] == chip, shard[None], gathered)


def _assemble(name, shard, gathered, chip):
    axis = {n: a for n, _, a in SHARDED}[name]
    return _slots_to_full(_with_own_slot(shard, gathered, chip), axis)


def _columns_from_slots(slots, bounds):
    width = slots.shape[2]
    pieces = []
    for lo, hi in zip(bounds[:-1], bounds[1:], strict=True):
        parts = [slots[k][:, max(lo, k * width) - k * width:min(hi, (k + 1) * width) - k * width]
                 for k in range(N_CHIPS) if max(lo, k * width) < min(hi, (k + 1) * width)]
        pieces.append(parts[0] if len(parts) == 1 else jnp.concatenate(parts, axis=1))
    return pieces


def _columns_to_slots(pieces):
    width = sum(p.shape[1] for p in pieces) // N_CHIPS
    slots = []
    for k in range(N_CHIPS):
        parts, start = [], 0
        for p in pieces:
            lo, hi = max(k * width, start), min((k + 1) * width, start + p.shape[1])
            if lo < hi:
                parts.append(p[:, lo - start:hi - start])
            start += p.shape[1]
        slots.append(jnp.concatenate(parts, axis=1))
    return jnp.stack(slots)


def _local_step(x, target, shards, small, place):
    b_forget = jnp.pad(small["b_forget"], ((0, 0), (0, LANES - N_HEADS)))
    causal = jnp.tril(jnp.ones((CHUNK, CHUNK), bool))
    ws = jnp.where(causal[None], small["w_spatial"].reshape(N_GROUPS, CHUNK, CHUNK), 0.0).astype(BF)
    ws_t = ws.transpose(0, 2, 1)
    bias_plane = jnp.repeat(small["b_spatial"].T, HEAD_DIM, axis=1)

    xn, (w_in_slots,) = _rms_fwd(x, small["g_mix_pre"], [shards["w_in"]])
    w_z, w_q, w_k, w_v, w_f, w_ga, w_gb = _columns_from_slots(
        _with_own_slot(shards["w_in"], w_in_slots, place[0]), (IN_Z, IN_Q, IN_K, IN_V, IN_F, IN_G, IN_G + D_MODEL, IN_END))
    w_qkv, w_g = jnp.concatenate([w_q, w_k, w_v], axis=1), jnp.concatenate([w_ga, w_gb], axis=1)
    w_f = jnp.pad(w_f, ((0, 0), (0, LANES - N_HEADS)))
    z = _matmul([(xn, w_z)], nt=False, out_dtypes=[F32], name="proj_z")
    qkv = _matmul([(xn, w_qkv)], nt=False, out_dtypes=[BF], name="proj_qkv")
    gl = _matmul([(xn, w_g)], nt=False, out_dtypes=[BF], name="proj_gate")
    fl = _matmul([(xn, w_f)], nt=False, out_dtypes=[F32], name="proj_forget")
    ysgu = _sgu_fwd(z, small["g_sgu"], small["b_sgu"], ws, bias_plane)
    qf, kl, vl, tile_stats = _attn_prep(qkv, fl, b_forget)
    first_key_tile, last_query_tile, bounded = _attn_ranges(tile_stats)
    yattn, yattn_f, ql, gathered = _attn_fwd(qf, kl, vl, first_key_tile, bounded, [shards[name] for name in LATE_WEIGHTS])
    w = {name: _assemble(name, shards[name], got, place[0]) for name, got in zip(LATE_WEIGHTS, gathered, strict=True)}
    a, b, merged = _branch_merge(ysgu, yattn, w["w_branch_sgu"], w["w_branch_attn"], gl)
    o, h1, xn2 = _matmul_rows(
        [(merged, w["w_out"])], nt=False, rows=[x], vecs=[small["g_mix_post"], small["g_ffn_pre"]], row_outs=[F32, F32, BF],
        n_sums=0, epilogue=_mixer_out_fwd, name="proj_out_norms")

    def relu2(acc):
        r = jnp.maximum(acc, 0.0)
        return (r * r,)

    hid = _matmul([(xn2, w["w_up"])], nt=False, out_dtypes=[BF], name="ffn_up", epilogue=relu2, tm=FFN_ROWS)
    dy, ddn, sq, dg_ffn_post = _matmul_rows(
        [(hid, w["w_down"])], nt=False, rows=[h1, target], vecs=[small["g_ffn_post"]], row_outs=[F32, BF], n_sums=2,
        epilogue=_loss_head, name="ffn_down_loss")

    dup = _matmul([(ddn, w["w_down"])], nt=True, out_dtypes=[BF], name="ffn_down_bwd", tm=FFN_ROWS,
                  epilogue=lambda acc, h: (acc * (2.0 * jnp.sqrt(h.astype(F32))),), extras=[hid])
    dw_down = _matmul_tn(hid, ddn, name="dw_down")
    dh1, do, dg_ffn_pre, dg_mix_post = _matmul_rows(
        [(dup, w["w_up"])], nt=True, rows=[h1, dy, o], vecs=[small["g_ffn_pre"], small["g_mix_post"]], row_outs=[F32, BF],
        n_sums=2, epilogue=_mixer_out_bwd, name="ffn_up_bwd_norms")
    dw_up = _matmul_tn(xn2, dup, name="dw_up", slots=True)

    def gate_bwd(dm, a_t, b_t, gla, glb):
        ga, gb = jax.nn.sigmoid(gla.astype(F32)), jax.nn.sigmoid(glb.astype(F32))
        return dm * ga, dm * gb, dm * a_t.astype(F32) * (ga * (1.0 - ga)), dm * b_t.astype(F32) * (gb * (1.0 - gb))

    da, db, dgla, dglb = _matmul([(do, w["w_out"])], nt=True, out_dtypes=[BF] * 4, name="proj_out_bwd",
                                 epilogue=gate_bwd, extras=[a, b, (gl, 0), (gl, D_MODEL)])
    dw_out = _matmul_tn(merged, do, name="dw_out")
    dysgu = _matmul([(da, w["w_branch_sgu"])], nt=True, out_dtypes=[F32], name="branch_sgu_bwd")
    dyattn = _matmul([(db, w["w_branch_attn"])], nt=True, out_dtypes=[F32], name="branch_attn_bwd")
    dw_bs = _matmul_tn(ysgu, da, name="dw_branch_sgu")
    dw_ba = _matmul_tn(yattn, db, name="dw_branch_attn")
    early = {"w_branch_sgu": _full_to_slots(dw_bs, 1), "w_branch_attn": _full_to_slots(dw_ba, 1),
             "w_out": _full_to_slots(dw_out, 0), "w_up": dw_up, "w_down": _full_to_slots(dw_down, 0)}
    (dz, dws, dbs, dg_sgu, db_sgu), early_theirs = _sgu_bwd(
        dysgu, z, small["g_sgu"], small["b_sgu"], ws, ws_t, bias_plane, [early[name] for name in EARLY_GRADS])
    early_sums = {name: _add_sibling(early[name], theirs, place, name="add_sibling_" + name)
                  for name, theirs in zip(EARLY_GRADS, early_theirs, strict=True)}
    dout = _attn_bwd_prep(dyattn, yattn_f)
    (dq, dk, dv, ext_q, ext_k), early_received = _attn_bwd(
        kl, vl, ql, dout, last_query_tile, [early_sums[name][1] for name in EARLY_GRADS])
    dfl, dbf = _forget_bwd(ext_q, ext_k, fl, b_forget)
    dw_in = _columns_to_slots(
        [_matmul_tn(xn, dz, name="dw_in_z"), _matmul_tn(xn, dq, name="dw_in_q"), _matmul_tn(xn, dk, name="dw_in_k"),
         _matmul_tn(xn, dv, name="dw_in_v"), _matmul_tn(xn, dfl, name="dw_in_f")[:, :N_HEADS],
         _matmul_tn(xn, dgla, name="dw_in_ga"), _matmul_tn(xn, dglb, name="dw_in_gb")])
    (dw_in_theirs,) = _exchange_halves([dw_in], name="exchange_halves_w_in")
    dw_in_sum = _add_sibling(dw_in, dw_in_theirs, place, name="add_sibling_w_in")
    dx, dg_mix_pre, dw_in_received = _matmul_rows(
        [(dz, w_z), (dq, w_q), (dk, w_k), (dv, w_v), (dgla, w_ga), (dglb, w_gb), (dfl, w_f)],
        nt=True, rows=[x, dh1], vecs=[small["g_mix_pre"]], row_outs=[F32], n_sums=1, epilogue=_input_norm_bwd,
        name="proj_in_bwd_norm", scatter=[dw_in_sum[1]])

    reduced = {name: (early_sums[name][0], got) for name, got in zip(EARLY_GRADS, early_received, strict=True)}
    reduced["w_in"] = (dw_in_sum[0], dw_in_received)
    small_grads = {"g_mix_pre": dg_mix_pre, "b_forget": dbf[:, :N_HEADS], "g_sgu": dg_sgu, "b_sgu": db_sgu,
                   "w_spatial": dws.reshape(N_GROUPS * CHUNK, CHUNK), "b_spatial": dbs[:, :N_GROUPS].T,
                   "g_mix_post": dg_mix_post, "g_ffn_pre": dg_ffn_pre, "g_ffn_post": dg_ffn_post}
    return sq, dx, reduced, small_grads


NAMES = ("g_mix_pre", "w_in", "b_forget", "g_sgu", "b_sgu", "w_spatial", "b_spatial", "w_branch_sgu", "w_branch_attn",
         "w_out", "g_mix_post", "g_ffn_pre", "w_up", "w_down", "g_ffn_post")


def kernel(x, g_mix_pre, w_in, b_forget, g_sgu, b_sgu, w_spatial, b_spatial, w_branch_sgu, w_branch_attn, w_out, g_mix_post, g_ffn_pre, w_up, w_down, g_ffn_post, loss_target, m_g_mix_pre, m_w_in, m_b_forget, m_g_sgu, m_b_sgu, m_w_spatial, m_b_spatial, m_w_branch_sgu, m_w_branch_attn, m_w_out, m_g_mix_post, m_g_ffn_pre, m_w_up, m_w_down, m_g_ffn_post, v_g_mix_pre, v_w_in, v_b_forget, v_g_sgu, v_b_sgu, v_w_spatial, v_b_spatial, v_w_branch_sgu, v_w_branch_attn, v_w_out, v_g_mix_post, v_g_ffn_pre, v_w_up, v_w_down, v_g_ffn_post):
    weights = dict(zip(NAMES, (g_mix_pre, w_in, b_forget, g_sgu, b_sgu, w_spatial, b_spatial, w_branch_sgu, w_branch_attn,
                               w_out, g_mix_post, g_ffn_pre, w_up, w_down, g_ffn_post), strict=True))
    first = dict(zip(NAMES, (m_g_mix_pre, m_w_in, m_b_forget, m_g_sgu, m_b_sgu, m_w_spatial, m_b_spatial, m_w_branch_sgu,
                             m_w_branch_attn, m_w_out, m_g_mix_post, m_g_ffn_pre, m_w_up, m_w_down, m_g_ffn_post), strict=True))
    second = dict(zip(NAMES, (v_g_mix_pre, v_w_in, v_b_forget, v_g_sgu, v_b_sgu, v_w_spatial, v_b_spatial, v_w_branch_sgu,
                              v_w_branch_attn, v_w_out, v_g_mix_post, v_g_ffn_pre, v_w_up, v_w_down, v_g_ffn_post), strict=True))
    shard_shapes = {name: _shard_shape(shape, axis) for name, shape, axis in SHARDED}
    small_shapes = dict(SMALL)
    view = lambda name, a: a.reshape(shard_shapes.get(name) or small_shapes[name])

    place = jnp.stack([2 * lax.axis_index("x") + lax.axis_index("y"), lax.axis_index("c")]).astype(jnp.int32)

    shards = {name: view(name, weights[name]).astype(BF) for name, _, _ in SHARDED}
    small = {name: view(name, weights[name]) for name, _ in SMALL}
    sq, dx, reduced, small_grads = _local_step(x[0], loss_target[0], shards, small, place)
    loss = lax.psum(0.5 * jnp.sum(sq) / D_MODEL, ("x", "y", "c"))

    small_mine = _pack_small(small_grads)
    (small_theirs,) = _exchange_halves([small_mine], name="exchange_halves_small")
    small_sum, _ = _add_sibling(small_mine, small_theirs, place, name="add_sibling_small")
    (small_received,) = _scatter_to_owners([small_sum])
    totals = {name: _add_chips(s, r, place, name="add_chips_" + name, own_slots=True) for name, (s, r) in reduced.items()}
    small_total = _add_chips(small_sum, small_received, place, name="add_chips_small", own_slots=False)
    joined = _join_halves([totals[name] for name, _, _ in SHARDED] + [small_total])
    grad = {**{name: g for (name, _, _), g in zip(SHARDED, joined[:-1], strict=True)}, **_unpack_small(joined[-1])}

    grad_out, delta, new_m, new_v = {}, {}, {}, {}
    for name in NAMES:
        rows, cols = grad[name].shape
        as_given = lambda a: a.reshape(1, rows, cols)
        grad_out[name], delta[name], new_m[name], new_v[name] = _adamw(
            as_given(weights[name]), grad[name], as_given(first[name]), as_given(second[name]), name="adamw_" + name)

    like = lambda d: [d[name].reshape(weights[name].shape) for name in NAMES]
    return (loss, dx[None], *like(grad_out), *like(delta), *like(new_m), *like(new_v))
```

```python
import functools

import jax
import jax.numpy as jnp
import numpy as np
from jax import lax
from jax.experimental import pallas as pl
from jax.experimental.pallas import tpu as pltpu

F32 = jnp.float32
BF = jnp.bfloat16
MESH = pl.DeviceIdType.MESH

D_MODEL = 1024
N_HEADS = 8
HEAD_DIM = 64
ATTN_W = N_HEADS * HEAD_DIM
SGU_W = 512
N_GROUPS = 8
CHUNK = 128
D_FF = 4096
EPS = 1e-6
Q_SCALE = HEAD_DIM ** -0.5
N_CHIPS = 4
LANES = 128

ADAM_LR = 0.001
ADAM_B1 = 0.9
ADAM_B2 = 0.999
ADAM_EPS = 1e-08
ADAM_WD = 0.01
ADAM_STEP = 10

VMEM_LIMIT = 48 * 1024 * 1024
BIG_VMEM = 58 * 1024 * 1024
NEG = -1e30


def _params(*sem):
    return pltpu.CompilerParams(dimension_semantics=sem, vmem_limit_bytes=VMEM_LIMIT)


def _dot(a, b):
    return jnp.dot(a, b, preferred_element_type=F32)


def _dot_nt(a, b):
    return lax.dot_general(a, b, (((1,), (1,)), ((), ())), preferred_element_type=F32)


def _dot_tn(a, b):
    return lax.dot_general(a, b, (((0,), (0,)), ((), ())), preferred_element_type=F32)


def _split3(c):
    hi = c.astype(BF).astype(F32)
    r = c - hi
    mid = r.astype(BF).astype(F32)
    lo = (r - mid).astype(BF).astype(F32)
    return hi, mid, lo


def _gelu(x):
    k = 0.7978845608028654
    return 0.5 * x * (1.0 + jnp.tanh(k * (x + 0.044715 * (x * x * x))))


def _gelu_grad(x):
    k = 0.7978845608028654
    x2 = x * x
    t = jnp.tanh(k * (x + 0.044715 * (x2 * x)))
    return 0.5 * (1.0 + t) + 0.5 * x * (1.0 - t * t) * (k * (1.0 + 3.0 * 0.044715 * x2))


def _rms_bwd(a, g, dy):
    r = lax.rsqrt(jnp.mean(a * a, axis=-1, keepdims=True) + EPS)
    n = a * r
    dn = dy * g
    da = r * (dn - n * jnp.mean(dn * n, axis=-1, keepdims=True))
    return da, dy * n


MM_ROWS = 1024
MM_COLS = 512
FFN_ROWS = 2048


def _matmul(pairs, *, nt, out_dtypes, name, tm=MM_ROWS, tn=MM_COLS, epilogue=None, extras=()):
    n_pairs, n_extra = len(pairs), len(extras)
    M = pairs[0][0].shape[0]
    N = pairs[0][1].shape[0] if nt else pairs[0][1].shape[1]
    tm, tn = min(tm, M), min(tn, N)
    assert M % tm == 0 and N % tn == 0

    def body(*refs):
        acc = None
        for p in range(n_pairs):
            a_ref, b_ref = refs[2 * p], refs[2 * p + 1]
            d = _dot_nt(a_ref[...], b_ref[...]) if nt else _dot(a_ref[...], b_ref[...])
            acc = d if acc is None else acc + d
        e_refs = refs[2 * n_pairs:2 * n_pairs + n_extra]
        o_refs = refs[2 * n_pairs + n_extra:]
        outs = (acc,) if epilogue is None else epilogue(acc, *[e[...] for e in e_refs])
        for o_ref, o in zip(o_refs, outs, strict=True):
            o_ref[...] = o.astype(o_ref.dtype)

    in_specs, args = [], []
    for a, b in pairs:
        K = a.shape[1]
        in_specs.append(pl.BlockSpec((tm, K), lambda i, j: (i, 0)))
        in_specs.append(pl.BlockSpec((tn, K), lambda i, j: (j, 0)) if nt else pl.BlockSpec((K, tn), lambda i, j: (0, j)))
        args += [a, b]
    for e in extras:
        e, col = e if isinstance(e, tuple) else (e, 0)
        in_specs.append(pl.BlockSpec((tm, tn), functools.partial(lambda i, j, off: (i, j + off), off=col // tn)))
        args.append(e)
    outs = pl.pallas_call(
        body, name=name, grid=(M // tm, N // tn), in_specs=in_specs,
        out_specs=[pl.BlockSpec((tm, tn), lambda i, j: (i, j)) for _ in out_dtypes],
        out_shape=[jax.ShapeDtypeStruct((M, N), dt) for dt in out_dtypes],
        compiler_params=_params("parallel", "parallel"),
    )(*args)
    return outs if len(outs) > 1 else outs[0]


def _project(a, weights, out_dtypes, *, name, tm=512):
    M, K = a.shape
    tm = min(tm, M)
    n = len(weights)

    def body(a_ref, *refs):
        av = a_ref[...]
        for w_ref, o_ref in zip(refs[:n], refs[n:], strict=True):
            o_ref[...] = _dot(av, w_ref[...]).astype(o_ref.dtype)

    return pl.pallas_call(
        body, name=name, grid=(M // tm,),
        in_specs=[pl.BlockSpec((tm, K), lambda i: (i, 0))] + [pl.BlockSpec(w.shape, lambda i: (0, 0)) for w in weights],
        out_specs=[pl.BlockSpec((tm, w.shape[1]), lambda i: (i, 0)) for w in weights],
        out_shape=[jax.ShapeDtypeStruct((M, w.shape[1]), dt) for w, dt in zip(weights, out_dtypes, strict=True)],
        compiler_params=_params("parallel"),
    )(a, *weights)


def _matmul_tn(a, b, *, name, tm=1024, tn=1024, tk=2048, slots=False):
    T, K1 = a.shape
    N = b.shape[1]
    tm, tn, tk = min(tm, K1), min(tn, N // N_CHIPS if slots else N), min(tk, T)
    assert K1 % tm == 0 and (N // N_CHIPS if slots else N) % tn == 0 and T % tk == 0
    per_slot = N // N_CHIPS // tn

    def body(a_ref, b_ref, o_ref):
        @pl.when(pl.program_id(2) == 0)
        def _():
            o_ref[...] = jnp.zeros_like(o_ref)

        o_ref[...] += _dot_tn(a_ref[...], b_ref[...])

    if slots:
        out_spec = pl.BlockSpec((None, tm, tn), lambda i, j, k: (j // per_slot, i, j % per_slot))
        out_shape = jax.ShapeDtypeStruct((N_CHIPS, K1, N // N_CHIPS), F32)
    else:
        out_spec = pl.BlockSpec((tm, tn), lambda i, j, k: (i, j))
        out_shape = jax.ShapeDtypeStruct((K1, N), F32)
    return pl.pallas_call(
        body, name=name, grid=(K1 // tm, N // tn, T // tk),
        in_specs=[pl.BlockSpec((tk, tm), lambda i, j, k: (k, i)), pl.BlockSpec((tk, tn), lambda i, j, k: (k, j))],
        out_specs=out_spec, out_shape=out_shape,
        compiler_params=_params("parallel", "parallel", "arbitrary"),
    )(a, b)


def _branch_merge(ysgu, yattn, w_bs, w_ba, gl, *, tm=MM_ROWS, tn=MM_COLS):
    T = ysgu.shape[0]
    tm = min(tm, T)
    nj = D_MODEL // tn

    def body(ys_ref, ya_ref, wbs_ref, wba_ref, gla_ref, glb_ref, a_ref, b_ref, m_ref):
        a = _dot(ys_ref[...], wbs_ref[...])
        b = _dot(ya_ref[...], wba_ref[...])
        a_ref[...] = a.astype(BF)
        b_ref[...] = b.astype(BF)
        m_ref[...] = (jax.nn.sigmoid(gla_ref[...].astype(F32)) * a + jax.nn.sigmoid(glb_ref[...].astype(F32)) * b).astype(BF)

    return pl.pallas_call(
        body, name="branch_merge", grid=(T // tm, nj),
        in_specs=[
            pl.BlockSpec((tm, SGU_W), lambda i, j: (i, 0)),
            pl.BlockSpec((tm, ATTN_W), lambda i, j: (i, 0)),
            pl.BlockSpec((SGU_W, tn), lambda i, j: (0, j)),
            pl.BlockSpec((ATTN_W, tn), lambda i, j: (0, j)),
            pl.BlockSpec((tm, tn), lambda i, j: (i, j)),
            pl.BlockSpec((tm, tn), lambda i, j: (i, j + nj)),
        ],
        out_specs=[pl.BlockSpec((tm, tn), lambda i, j: (i, j))] * 3,
        out_shape=[jax.ShapeDtypeStruct((T, D_MODEL), BF)] * 3,
        compiler_params=_params("parallel", "parallel"),
    )(ysgu, yattn, w_bs, w_ba, gl, gl)


def _row_spec(tr, width):
    return pl.BlockSpec((tr, width), lambda i: (i, 0))


def _vec_spec(width):
    return pl.BlockSpec((1, width), lambda i: (0, 0))


def _rms_fwd(x, g, shards, *, tr=256):
    T = x.shape[0]
    tr = min(tr, T)
    n_steps = T // tr
    k = len(shards)

    def body(x_ref, g_ref, *refs):
        step = pl.program_id(0)
        gather_start, gather_forward, gather_finish = _gather_phases(refs[:k], refs[k + 1:2 * k + 1], *refs[2 * k + 1:])
        pl.when(step == 0)(gather_start)
        pl.when(step == (3 * n_steps) // 4)(gather_forward)
        xv = x_ref[...]
        r = lax.rsqrt(jnp.mean(xv * xv, axis=-1, keepdims=True) + EPS)
        refs[k][...] = ((xv * r) * g_ref[...]).astype(BF)
        pl.when(step == n_steps - 1)(gather_finish)

    outs = pl.pallas_call(
        body, name="rms_fwd", grid=(n_steps,),
        in_specs=[_row_spec(tr, D_MODEL), _vec_spec(D_MODEL)] + [HBM] * k, out_specs=[_row_spec(tr, D_MODEL)] + [HBM] * k,
        out_shape=[jax.ShapeDtypeStruct((T, D_MODEL), BF)] + _gathered_shapes(shards),
        scratch_shapes=_gather_semaphores(k), compiler_params=_params("arbitrary"),
    )(x, g, *shards)
    return outs[0], outs[1:]


def _mixer_out_fwd(o, x, g_post, g_pre):
    r = lax.rsqrt(jnp.mean(o * o, axis=-1, keepdims=True) + EPS)
    h1 = x + (o * r) * g_post
    r2 = lax.rsqrt(jnp.mean(h1 * h1, axis=-1, keepdims=True) + EPS)
    return o, h1, (h1 * r2) * g_pre


def _matmul_rows(pairs, *, nt, rows, vecs, row_outs, n_sums, epilogue, name, tm=512, scatter=()):
    M = pairs[0][0].shape[0]
    N = pairs[0][1].shape[0] if nt else pairs[0][1].shape[1]
    tm = min(tm, M)
    n_steps = M // tm
    n_pairs, n_rows, n_vecs, n_out, n_scatter = len(pairs), len(rows), len(vecs), len(row_outs), len(scatter)

    def body(*refs):
        groups, at = [], 2 * n_pairs
        for count in (n_rows, n_vecs, n_scatter, n_out, n_sums, n_scatter):
            groups.append(refs[at:at + count])
            at += count
        r_refs, v_refs, b_refs, o_refs, s_refs, got_refs = groups
        sems = refs[at:]
        step = pl.program_id(0)
        if n_scatter:
            scatter_start, scatter_finish = _scatter_phases(b_refs, got_refs, *sems)
            pl.when(step == 0)(scatter_start)

        @pl.when(step == 0)
        def _():
            for s_ref in s_refs:
                s_ref[...] = jnp.zeros_like(s_ref)

        acc = None
        for p in range(n_pairs):
            a_ref, b_ref = refs[2 * p], refs[2 * p + 1]
            d = _dot_nt(a_ref[...], b_ref[...]) if nt else _dot(a_ref[...], b_ref[...])
            acc = d if acc is None else acc + d
        outs = epilogue(acc, *[r[...] for r in r_refs], *[v[...] for v in v_refs])
        for o_ref, o in zip(o_refs, outs[:n_out], strict=True):
            o_ref[...] = o.astype(o_ref.dtype)
        for s_ref, term in zip(s_refs, outs[n_out:], strict=True):
            s_ref[...] += jnp.sum(term, axis=0, keepdims=True)
        if n_scatter:
            pl.when(step == n_steps - 1)(scatter_finish)

    in_specs, args = [], []
    for a, b in pairs:
        in_specs += [_row_spec(tm, a.shape[1]), pl.BlockSpec(b.shape, lambda i: (0, 0))]
        args += [a, b]
    outs = pl.pallas_call(
        body, name=name, grid=(n_steps,),
        in_specs=in_specs + [_row_spec(tm, N)] * n_rows + [_vec_spec(N)] * n_vecs + [HBM] * n_scatter,
        out_specs=[_row_spec(tm, N)] * n_out + [_vec_spec(N)] * n_sums + [HBM] * n_scatter,
        out_shape=[jax.ShapeDtypeStruct((M, N), dt) for dt in row_outs] + [jax.ShapeDtypeStruct((1, N), F32)] * n_sums
        + (_scattered_shapes(scatter) if n_scatter else []),
        scratch_shapes=_scatter_semaphores(n_scatter) if n_scatter else [],
        compiler_params=pltpu.CompilerParams(dimension_semantics=("arbitrary",), vmem_limit_bytes=BIG_VMEM),
    )(*args, *rows, *vecs, *scatter)
    return outs


def _loss_head(dn, h1, target, g):
    r = lax.rsqrt(jnp.mean(dn * dn, axis=-1, keepdims=True) + EPS)
    err = h1 + (dn * r) * g - target
    dy = err * (1.0 / D_MODEL)
    ddn, dg_terms = _rms_bwd(dn, g, dy)
    return dy, ddn, err * err, dg_terms


def _mixer_out_bwd(dxn2, h1, dy, o, g_pre, g_post):
    da, dg_pre_terms = _rms_bwd(h1, g_pre, dxn2)
    dh1 = dy + da
    do, dg_post_terms = _rms_bwd(o, g_post, dh1)
    return dh1, do, dg_pre_terms, dg_post_terms


def _input_norm_bwd(dxn, x, dh1, g):
    da, dg_terms = _rms_bwd(x, g, dxn)
    return dh1 + da, dg_terms


def _sgu_norm(z_tile, g, b):
    gz = _gelu(z_tile)
    u, vv = gz[:, :SGU_W], gz[:, SGU_W:]
    xc = vv - jnp.mean(vv, axis=-1, keepdims=True)
    rstd = lax.rsqrt(jnp.mean(xc * xc, axis=-1, keepdims=True) + EPS)
    xhat = xc * rstd
    return u, xhat, rstd, xhat * g + b


def _sgu_mix(w_ref, v_bf, first_half):
    parts = []
    for p in range(N_GROUPS // 2):
        vp = v_bf[:, p * LANES:(p + 1) * LANES]
        parts.append(jnp.where(first_half, _dot(w_ref[2 * p], vp), _dot(w_ref[2 * p + 1], vp)))
    return jnp.concatenate(parts, axis=1)


def _sgu_fwd(z, g_sgu, b_sgu, ws, bias_plane, *, tm=512):
    T = z.shape[0]
    tm = min(tm, T)

    def body(z_ref, g_ref, b_ref, ws_ref, bp_ref, y_ref):
        u, _, _, vn = _sgu_norm(z_ref[...], g_ref[...], b_ref[...])
        vn_bf = vn.astype(BF)
        first_half = lax.broadcasted_iota(jnp.int32, (CHUNK, LANES), 1) < HEAD_DIM
        for c in range(tm // CHUNK):
            rows = slice(c * CHUNK, (c + 1) * CHUNK)
            s = _sgu_mix(ws_ref, vn_bf[rows, :], first_half) + bp_ref[...]
            y_ref[rows, :] = (u[rows, :] * s).astype(BF)

    return pl.pallas_call(
        body, name="sgu_fwd", grid=(T // tm,),
        in_specs=[_row_spec(tm, 2 * SGU_W), _vec_spec(SGU_W), _vec_spec(SGU_W),
                  pl.BlockSpec((N_GROUPS, CHUNK, CHUNK), lambda i: (0, 0, 0)),
                  pl.BlockSpec((CHUNK, SGU_W), lambda i: (0, 0))],
        out_specs=_row_spec(tm, SGU_W), out_shape=jax.ShapeDtypeStruct((T, SGU_W), BF),
        compiler_params=_params("parallel"),
    )(z, g_sgu, b_sgu, ws, bias_plane)


def _sgu_bwd(dy, z, g_sgu, b_sgu, ws, ws_t, bias_plane, exchange, *, tm=512):
    T = z.shape[0]
    tm = min(tm, T)
    n_steps = T // tm
    k = len(exchange)

    def body(dy_ref, z_ref, g_ref, b_ref, ws_ref, wst_ref, bp_ref, *refs):
        x_refs, (dz_ref, dws_ref, dbs_ref, dg_ref, db_ref), r_refs = refs[:k], refs[k:k + 5], refs[k + 5:2 * k + 5]
        dbp_ref, send_sems, recv_sems = refs[2 * k + 5:]
        step = pl.program_id(0)
        exchange_start, exchange_finish = _exchange_phases(x_refs, r_refs, send_sems, recv_sems)
        pl.when(step == 0)(exchange_start)

        @pl.when(step == 0)
        def _():
            dws_ref[...] = jnp.zeros_like(dws_ref)
            dg_ref[...] = jnp.zeros_like(dg_ref)
            db_ref[...] = jnp.zeros_like(db_ref)
            dbp_ref[...] = jnp.zeros_like(dbp_ref)

        g = g_ref[...]
        zt = z_ref[...]
        u, xhat, rstd, vn = _sgu_norm(zt, g, b_ref[...])
        vn_bf = vn.astype(BF)
        first_half = lax.broadcasted_iota(jnp.int32, (CHUNK, LANES), 1) < HEAD_DIM
        dyv = dy_ref[...]
        dg_acc = jnp.zeros((1, SGU_W), F32)
        db_acc = jnp.zeros((1, SGU_W), F32)
        for c in range(tm // CHUNK):
            rows = slice(c * CHUNK, (c + 1) * CHUNK)
            v_c = vn_bf[rows, :]
            s = _sgu_mix(ws_ref, v_c, first_half) + bp_ref[...]
            dy_c = dyv[rows, :]
            du = dy_c * s
            dsv = dy_c * u[rows, :]
            dbp_ref[...] += dsv
            ds_bf = dsv.astype(BF)
            zero = jnp.zeros((CHUNK, LANES), BF)
            for p in range(N_GROUPS // 2):
                dsp = ds_bf[:, p * LANES:(p + 1) * LANES]
                vp = v_c[:, p * LANES:(p + 1) * LANES]
                dws_ref[2 * p] += _dot_nt(jnp.where(first_half, dsp, zero), vp)
                dws_ref[2 * p + 1] += _dot_nt(jnp.where(first_half, zero, dsp), vp)
            dvn = _sgu_mix(wst_ref, ds_bf, first_half)
            xh = xhat[rows, :]
            dxh = dvn * g
            dvv = rstd[rows, :] * (dxh - jnp.mean(dxh, axis=-1, keepdims=True)
                                   - xh * jnp.mean(dxh * xh, axis=-1, keepdims=True))
            dg_acc += jnp.sum(dvn * xh, axis=0, keepdims=True)
            db_acc += jnp.sum(dvn, axis=0, keepdims=True)
            dgz = jnp.concatenate([du, dvv], axis=1)
            dz_ref[rows, :] = (dgz * _gelu_grad(zt[rows, :])).astype(BF)
        dg_ref[...] += dg_acc
        db_ref[...] += db_acc

        @pl.when(step == n_steps - 1)
        def _():
            r = lax.broadcasted_iota(jnp.int32, (CHUNK, CHUNK), 0)
            cidx = lax.broadcasted_iota(jnp.int32, (CHUNK, CHUNK), 1)
            causal = (cidx <= r).astype(F32)
            for gi in range(N_GROUPS):
                dws_ref[gi] = dws_ref[gi] * causal
            lane = lax.broadcasted_iota(jnp.int32, (CHUNK, LANES), 1)
            out = jnp.zeros((CHUNK, LANES), F32)
            dbp = dbp_ref[...]
            for gi in range(N_GROUPS):
                col = jnp.sum(dbp[:, gi * HEAD_DIM:(gi + 1) * HEAD_DIM], axis=1, keepdims=True)
                out = jnp.where(lane == gi, col, out)
            dbs_ref[...] = out
            exchange_finish()

    w_spec = pl.BlockSpec((N_GROUPS, CHUNK, CHUNK), lambda i: (0, 0, 0))
    plane = pl.BlockSpec((CHUNK, SGU_W), lambda i: (0, 0))
    outs = pl.pallas_call(
        body, name="sgu_bwd", grid=(n_steps,),
        in_specs=[_row_spec(tm, SGU_W), _row_spec(tm, 2 * SGU_W), _vec_spec(SGU_W), _vec_spec(SGU_W), w_spec, w_spec, plane]
        + [HBM] * k,
        out_specs=[_row_spec(tm, 2 * SGU_W), w_spec, pl.BlockSpec((CHUNK, LANES), lambda i: (0, 0)),
                   _vec_spec(SGU_W), _vec_spec(SGU_W)] + [HBM] * k,
        out_shape=[jax.ShapeDtypeStruct((T, 2 * SGU_W), BF), jax.ShapeDtypeStruct((N_GROUPS, CHUNK, CHUNK), F32),
                   jax.ShapeDtypeStruct((CHUNK, LANES), F32), jax.ShapeDtypeStruct((1, SGU_W), F32),
                   jax.ShapeDtypeStruct((1, SGU_W), F32)] + _exchanged_shapes(exchange),
        scratch_shapes=[pltpu.VMEM((CHUNK, SGU_W), F32)] + _exchange_semaphores(k),
        compiler_params=_params("arbitrary"),
    )(dy, z, g_sgu, b_sgu, ws, ws_t, bias_plane, *exchange)
    return outs[:5], outs[5:]


def _tri(n, upper):
    r = lax.broadcasted_iota(jnp.int32, (n, n), 0)
    c = lax.broadcasted_iota(jnp.int32, (n, n), 1)
    return ((c >= r) if upper else (c <= r)).astype(BF)


def _scan_dot(tri, x):
    hi, mid, lo = _split3(x)
    return (_dot(tri, hi.astype(BF)) + _dot(tri, mid.astype(BF))) + _dot(tri, lo.astype(BF))


def _with_lanes(base, lane, start, cols):
    out = base
    for k, col in enumerate(cols):
        if col is not None:
            out = jnp.where(lane == start + k, col, out)
    return out


def _logit_bound(q_norm, k_norm):
    return NORM_SLACK * q_norm * k_norm + 1.0


ATTN_TILE = 512
SKIP_BELOW = -110.0
NORM_SLACK = 1.001
BOUNDED_GAP = 60.0


def _attn_prep(qkv, fl, b_forget, *, tp=ATTN_TILE):
    T = qkv.shape[0]
    tp = min(tp, T)
    head_sum, gather6, place_q, place_k, place_v = (jnp.asarray(m, BF) for m in _attn_placements())

    def body(qkv_ref, fl_ref, bf_ref, hs_ref, g6_ref, pq_ref, pk_ref, pv_ref, qf_ref, kl_ref, vl_ref, st_ref, carry_ref, kmax_ref):
        @pl.when(pl.program_id(0) == 0)
        def _():
            carry_ref[...] = jnp.zeros_like(carry_ref)
            kmax_ref[...] = jnp.zeros_like(kmax_ref)

        x = fl_ref[...] + bf_ref[...]
        logf = jnp.minimum(x, 0.0) - jnp.log(1.0 + jnp.exp(-jnp.abs(x)))
        cum = _scan_dot(_tri(tp, upper=False), logf) + carry_ref[...]
        carry_ref[...] = cum[tp - 1:tp, :]

        def head_norms(block):
            sq = block * block
            hi = sq.astype(BF)
            return _dot(hi, hs_ref[...]) + _dot((sq - hi.astype(F32)).astype(BF), hs_ref[...])

        qkvv = qkv_ref[...]
        q_norm = NORM_SLACK * jnp.sqrt(head_norms(qkvv[:, :ATTN_W].astype(F32) * Q_SCALE))
        kn = NORM_SLACK * jnp.sqrt(jnp.max(head_norms(qkvv[:, ATTN_W:2 * ATTN_W].astype(F32)), axis=0, keepdims=True))
        k_seen = jnp.maximum(kmax_ref[...], kn)
        kmax_ref[...] = k_seen
        rows = (jnp.max(q_norm, axis=0, keepdims=True), kn, jnp.max(cum, axis=0, keepdims=True),
                jnp.min(cum, axis=0, keepdims=True), k_seen)
        st_ref[...] = jnp.zeros_like(st_ref)
        for k, row in enumerate(rows):
            st_ref[0, k:k + 1, :] = row
        parts = jnp.concatenate([p.astype(BF) for p in _split3(cum) + _split3(-_logit_bound(q_norm, k_seen))], axis=1)
        lane = lax.broadcasted_iota(jnp.int32, (tp, LANES), 1)
        side = jnp.where(lane == 6 * N_HEADS, 1.0, _dot(parts, g6_ref[...])).astype(BF)
        for h in range(N_HEADS):
            pair = slice((h // 2) * LANES, (h // 2 + 1) * LANES)
            for out_ref, block, place_ref in ((qf_ref, qkvv[:, :ATTN_W], pq_ref), (kl_ref, qkvv[:, ATTN_W:2 * ATTN_W], pk_ref),
                                              (vl_ref, qkvv[:, 2 * ATTN_W:], pv_ref)):
                out_ref[h] = _dot(jnp.concatenate([block[:, pair], side], axis=1), place_ref[h]).astype(BF)

    head_spec = pl.BlockSpec((N_HEADS, tp, LANES), lambda i: (0, i, 0))
    whole = lambda a: pl.BlockSpec(a.shape, lambda i: (0,) * a.ndim)
    return pl.pallas_call(
        body, name="attn_prep", grid=(T // tp,),
        in_specs=[_row_spec(tp, 3 * ATTN_W), _row_spec(tp, LANES), _vec_spec(LANES)]
        + [whole(m) for m in (head_sum, gather6, place_q, place_k, place_v)],
        out_specs=[head_spec] * 3 + [pl.BlockSpec((1, N_HEADS, LANES), lambda i: (i, 0, 0))],
        out_shape=[jax.ShapeDtypeStruct((N_HEADS, T, LANES), BF)] * 3 + [jax.ShapeDtypeStruct((T // tp, N_HEADS, LANES), F32)],
        scratch_shapes=[pltpu.VMEM((1, LANES), F32), pltpu.VMEM((1, LANES), F32)], compiler_params=_params("arbitrary"),
    )(qkv, fl, b_forget, head_sum, gather6, place_q, place_k, place_v)


def _attn_placements():
    head_sum = np.zeros((ATTN_W, LANES), np.float32)
    head_sum[np.arange(ATTN_W), np.arange(ATTN_W) // HEAD_DIM] = 1.0
    gather6 = np.zeros((6 * LANES, LANES), np.float32)
    for j in range(6):
        gather6[j * LANES + np.arange(N_HEADS), j * N_HEADS + np.arange(N_HEADS)] = 1.0
    place = np.zeros((3, N_HEADS, 2 * LANES, LANES), np.float32)
    one = LANES + 6 * N_HEADS
    d = np.arange(HEAD_DIM)
    for h in range(N_HEADS):
        side = lambda j: LANES + j * N_HEADS + h
        place[0, h, (h % 2) * HEAD_DIM + d, d] = Q_SCALE
        place[1:, h, (h % 2) * HEAD_DIM + d, d] = 1.0
        for j in range(3):
            place[0, h, side(j), HEAD_DIM + j] = 1.0
            place[0, h, one, HEAD_DIM + 3 + j] = 1.0
            place[0, h, side(3 + j), HEAD_DIM + 6 + j] = 1.0
            place[1, h, one, HEAD_DIM + j] = 1.0
            place[1, h, side(j), HEAD_DIM + 3 + j] = -1.0
            place[1, h, one, HEAD_DIM + 6 + j] = 1.0
            place[2, h, one, HEAD_DIM + j] = 1.0
    return head_sum, gather6, place[0], place[1], place[2]


def _attn_ranges(stats):
    qn, kn, cmax, cmin, k_seen = (stats[:, k, :N_HEADS].T for k in range(5))
    n = qn.shape[1]
    bounded = (2.0 * _logit_bound(qn, k_seen) <= BOUNDED_GAP).reshape(N_HEADS // 2, 2, n).all(axis=1)
    reach = NORM_SLACK * qn * (jnp.max(kn, axis=1, keepdims=True) + kn) + cmax
    i = jnp.arange(n)[None, :, None]
    j = jnp.arange(n)[None, None, :]
    need = ((reach[:, :, None] - cmin[:, None, :] >= SKIP_BELOW) | (i == j)) & (j <= i)
    first = jnp.min(jnp.where(need, j, n), axis=2).reshape(N_HEADS // 2, 2, n).min(axis=1)
    last = jnp.max(jnp.where(need, i, -1), axis=1).reshape(N_HEADS // 2, 2, n).max(axis=1)
    return first.reshape(-1).astype(F32), last.reshape(-1).astype(F32), bounded.reshape(-1).astype(F32)


def _pair_block(t):
    return pl.BlockSpec((2, t, LANES), lambda p, i, *_: (p, i, 0))


def _pair_full(T):
    return pl.BlockSpec((2, T, LANES), lambda p, i, *_: (p, 0, 0))


def _packed_block(t):
    return pl.BlockSpec((t, LANES), lambda p, i, *_: (i, p))


def _causal(t, keys_in_rows=False):
    r = lax.broadcasted_iota(jnp.int32, (t, t), 0)
    c = lax.broadcasted_iota(jnp.int32, (t, t), 1)
    return (r <= c) if keys_in_rows else (c <= r)


def _tile_rows(j, t):
    return pl.ds(pl.multiple_of(j * t, t), t)


def _attn_call(body, name, tile_scalars, operands, in_specs, out_specs, out_shape, scratch_shapes, n_tiles):
    return pl.pallas_call(
        body, name=name,
        grid_spec=pltpu.PrefetchScalarGridSpec(
            num_scalar_prefetch=len(tile_scalars), grid=(N_HEADS // 2, n_tiles), in_specs=in_specs, out_specs=out_specs,
            scratch_shapes=scratch_shapes),
        out_shape=out_shape, compiler_params=_params("arbitrary", "arbitrary"),
    )(*tile_scalars, *operands)


def _attn_fwd(qf, kl, vl, first, bounded, shards, *, tq=ATTN_TILE):
    T = qf.shape[1]
    tq = min(tq, T)
    n = T // tq
    n_steps = (N_HEADS // 2) * n
    k = len(shards)

    def body(first_ref, bounded_ref, qf_ref, kl_ref, vl_ref, *refs):
        w_refs, (o_ref, of_ref, ql_ref), g_refs = refs[:k], refs[k:k + 3], refs[k + 3:2 * k + 3]
        m_ref, acc_ref, send_sems, recv_sems = refs[2 * k + 3:]
        i = pl.program_id(1)
        tile = pl.program_id(0) * n + i
        gather_start, gather_forward, gather_finish = _gather_phases(w_refs, g_refs, send_sems, recv_sems)
        pl.when(tile == 0)(gather_start)
        pl.when(tile == (3 * n_steps) // 4)(gather_forward)
        start = first_ref[tile].astype(jnp.int32)
        is_bounded = bounded_ref[tile] > 0.5
        acc_ref[...] = jnp.zeros_like(acc_ref)
        diagonal = _tile_rows(i, tq)
        causal = _causal(tq)

        def logits(hh, rows):
            return _dot_nt(qf_ref[hh], kl_ref[hh, rows, :])

        @pl.when(is_bounded)
        def _():
            m_ref[...] = jnp.zeros_like(m_ref)

            def update(hh, s, rows):
                acc_ref[hh] += _dot(jnp.exp(s).astype(BF), vl_ref[hh, rows, :])

            def step(j, carry):
                for hh in range(2):
                    update(hh, logits(hh, _tile_rows(j, tq)), _tile_rows(j, tq))
                return carry

            lax.fori_loop(start, i, step, 0)
            for hh in range(2):
                update(hh, jnp.where(causal, logits(hh, diagonal), NEG), diagonal)

        @pl.when(jnp.logical_not(is_bounded))
        def _():
            m_ref[...] = jnp.full_like(m_ref, NEG)

            def update(hh, s, rows):
                m_old = m_ref[hh]
                m_new = jnp.maximum(m_old, jnp.max(s, axis=1, keepdims=True))
                p = jnp.exp(s - m_new)
                acc_ref[hh] = jnp.exp(m_old - m_new) * acc_ref[hh] + _dot(p.astype(BF), vl_ref[hh, rows, :])
                m_ref[hh] = m_new

            def step(j, carry):
                for hh in range(2):
                    update(hh, logits(hh, _tile_rows(j, tq)), _tile_rows(j, tq))
                return carry

            lax.fori_loop(start, i, step, 0)
            for hh in range(2):
                update(hh, jnp.where(causal, logits(hh, diagonal), NEG), diagonal)

        lane = lax.broadcasted_iota(jnp.int32, (tq, LANES), 1)
        outs = []
        for hh in range(2):
            q = qf_ref[hh].astype(F32)
            acc = acc_ref[hh]
            l = acc[:, HEAD_DIM:HEAD_DIM + 1]
            outs.append(acc[:, :HEAD_DIM] / l)
            at = HEAD_DIM + 6
            neg_bound = (q[:, at:at + 1] + q[:, at + 1:at + 2]) + q[:, at + 2:at + 3]
            ql_ref[hh] = _with_lanes(q, lane, at, _split3(neg_bound - (m_ref[hh] + jnp.log(l)))).astype(BF)
        o = jnp.concatenate(outs, axis=1)
        o_ref[...] = o.astype(BF)
        of_ref[...] = o
        pl.when(tile == n_steps - 1)(gather_finish)

    outs = _attn_call(
        body, "attn_fwd", (first, bounded), (qf, kl, vl, *shards),
        [_pair_block(tq), _pair_full(T), _pair_full(T)] + [HBM] * k,
        [_packed_block(tq), _packed_block(tq), _pair_block(tq)] + [HBM] * k,
        [jax.ShapeDtypeStruct((T, ATTN_W), BF), jax.ShapeDtypeStruct((T, ATTN_W), F32),
         jax.ShapeDtypeStruct((N_HEADS, T, LANES), BF)] + _gathered_shapes(shards),
        [pltpu.VMEM((2, tq, 1), F32), pltpu.VMEM((2, tq, LANES), F32)] + _gather_semaphores(k), n)
    return outs[0], outs[1], outs[2], outs[3:]


def _attn_bwd_prep(dya, of, *, tr=ATTN_TILE):
    T = dya.shape[0]
    tr = min(tr, T)
    head_sum, gather6 = (jnp.asarray(m, BF) for m in _attn_placements()[:2])
    gather3, place_do = gather6[:3 * LANES], _delta_placement()

    def body(d_ref, o_ref, hs_ref, g3_ref, p_ref, do_ref):
        dv = d_ref[...]
        delta = sum(_dot(part.astype(BF), hs_ref[...]) for part in _split3(dv * o_ref[...]))
        side = _dot(jnp.concatenate([p.astype(BF) for p in _split3(-delta)], axis=1), g3_ref[...]).astype(BF)
        d_bf = dv.astype(BF)
        for h in range(N_HEADS):
            pair = slice((h // 2) * LANES, (h // 2 + 1) * LANES)
            do_ref[h] = _dot(jnp.concatenate([d_bf[:, pair], side], axis=1), p_ref[h]).astype(BF)

    whole = lambda a: pl.BlockSpec(a.shape, lambda i: (0,) * a.ndim)
    return pl.pallas_call(
        body, name="attn_bwd_prep", grid=(T // tr,),
        in_specs=[_row_spec(tr, ATTN_W), _row_spec(tr, ATTN_W), whole(head_sum), whole(gather3), whole(place_do)],
        out_specs=pl.BlockSpec((N_HEADS, tr, LANES), lambda i: (0, i, 0)),
        out_shape=jax.ShapeDtypeStruct((N_HEADS, T, LANES), BF), compiler_params=_params("parallel"),
    )(dya, of, head_sum, gather3, place_do)


def _delta_placement():
    place = np.zeros((N_HEADS, 2 * LANES, LANES), np.float32)
    d = np.arange(HEAD_DIM)
    for h in range(N_HEADS):
        place[h, (h % 2) * HEAD_DIM + d, d] = 1.0
        for j in range(3):
            place[h, LANES + j * N_HEADS + h, HEAD_DIM + j] = 1.0
    return jnp.asarray(place, BF)


def _attn_bwd(kl, vl, ql, do, last, chip_sums, *, tk=ATTN_TILE):
    T = ql.shape[1]
    tk = min(tk, T)
    n = T // tk
    n_steps = (N_HEADS // 2) * n
    m = len(chip_sums)

    def body(last_ref, kl_ref, vl_ref, ql_ref, do_ref, *refs):
        b_refs, (dq_ref, dk_ref, dv_ref, extq_ref, extk_ref), r_refs = refs[:m], refs[m:m + 5], refs[m + 5:2 * m + 5]
        dq_acc, dk_acc, dv_acc, send_sems, recv_sems = refs[2 * m + 5:]
        j = pl.program_id(1)
        tile = pl.program_id(0) * n + j
        scatter_start, scatter_finish = _scatter_phases(b_refs, r_refs, send_sems, recv_sems)
        pl.when(tile == 0)(scatter_start)

        @pl.when(j == 0)
        def _():
            dq_acc[...] = jnp.zeros_like(dq_acc)

        dk_acc[...] = jnp.zeros_like(dk_acc)
        dv_acc[...] = jnp.zeros_like(dv_acc)

        def block(hh, rows, mask):
            qi, di, k = ql_ref[hh, rows, :], do_ref[hh, rows, :], kl_ref[hh]
            p_t = jnp.exp(_dot_nt(k, qi))
            if mask is not None:
                p_t = jnp.where(mask, p_t, 0.0)
            ds_t = (p_t * _dot_nt(vl_ref[hh], di)).astype(BF)
            dk_acc[hh] += _dot(ds_t, qi)
            dv_acc[hh] += _dot(p_t.astype(BF), di)
            dq_acc[hh, rows, :] += _dot_tn(ds_t, k)

        causal_t = _causal(tk, keys_in_rows=True)
        for hh in range(2):
            block(hh, _tile_rows(j, tk), causal_t)

        def step(i, carry):
            for hh in range(2):
                block(hh, _tile_rows(i, tk), None)
            return carry

        lax.fori_loop(j + 1, last_ref[pl.program_id(0) * n + j].astype(jnp.int32) + 1, step, 0)
        dk_ref[...] = jnp.concatenate([dk_acc[hh][:, :HEAD_DIM] for hh in range(2)], axis=1).astype(BF)
        dv_ref[...] = jnp.concatenate([dv_acc[hh][:, :HEAD_DIM] for hh in range(2)], axis=1).astype(BF)
        extk_ref[...] = jnp.concatenate([dk_acc[hh][:, HEAD_DIM:] for hh in range(2)], axis=1)

        @pl.when(j == n - 1)
        def _():
            dq_ref[...] = jnp.concatenate([dq_acc[hh][:, :HEAD_DIM] * Q_SCALE for hh in range(2)], axis=1).astype(BF)
            extq_ref[...] = jnp.concatenate([dq_acc[hh][:, HEAD_DIM:] for hh in range(2)], axis=1)

        pl.when(tile == n_steps - 1)(scatter_finish)

    whole = pl.BlockSpec((T, LANES), lambda p, j, *_: (0, p))
    outs = pl.pallas_call(
        body, name="attn_bwd",
        grid_spec=pltpu.PrefetchScalarGridSpec(
            num_scalar_prefetch=1, grid=(N_HEADS // 2, n),
            in_specs=[_pair_block(tk), _pair_block(tk), _pair_full(T), _pair_full(T)] + [HBM] * m,
            out_specs=[whole, _packed_block(tk), _packed_block(tk), whole, _packed_block(tk)] + [HBM] * m,
            scratch_shapes=[pltpu.VMEM((2, T, LANES), F32), pltpu.VMEM((2, tk, LANES), F32), pltpu.VMEM((2, tk, LANES), F32)]
            + _scatter_semaphores(m)),
        out_shape=[jax.ShapeDtypeStruct((T, ATTN_W), BF)] * 3 + [jax.ShapeDtypeStruct((T, ATTN_W), F32)] * 2
        + _scattered_shapes(chip_sums),
        compiler_params=pltpu.CompilerParams(dimension_semantics=("arbitrary", "arbitrary"), vmem_limit_bytes=BIG_VMEM),
    )(last, kl, vl, ql, do, *chip_sums)
    return outs[:5], outs[5:]


def _forget_bwd(ext_q, ext_k, fl, b_forget, *, tp=256):
    T = fl.shape[0]
    tp = min(tp, T)
    n = T // tp

    def body(eq_ref, ek_ref, fl_ref, bf_ref, dfl_ref, dbf_ref, carry_ref):
        @pl.when(pl.program_id(0) == 0)
        def _():
            carry_ref[...] = jnp.zeros_like(carry_ref)
            dbf_ref[...] = jnp.zeros_like(dbf_ref)

        lane = lax.broadcasted_iota(jnp.int32, (tp, LANES), 1)
        eq, ek = eq_ref[...], ek_ref[...]
        cols = [eq[:, h * HEAD_DIM:h * HEAD_DIM + 1] - ek[:, h * HEAD_DIM + 3:h * HEAD_DIM + 4] for h in range(N_HEADS)]
        dcum = _with_lanes(jnp.zeros((tp, LANES), F32), lane, 0, cols)
        suffix = _scan_dot(_tri(tp, upper=True), dcum) + carry_ref[...]
        carry_ref[...] = suffix[0:1, :]
        x = fl_ref[...] + bf_ref[...]
        dfl = jnp.where(lane < N_HEADS, suffix / (1.0 + jnp.exp(x)), 0.0)
        dfl_ref[...] = dfl.astype(BF)
        dbf_ref[...] += jnp.sum(dfl, axis=0, keepdims=True)

    rev = lambda w: pl.BlockSpec((tp, w), lambda i: (n - 1 - i, 0))
    return pl.pallas_call(
        body, name="forget_bwd", grid=(n,),
        in_specs=[rev(ATTN_W), rev(ATTN_W), rev(LANES), _vec_spec(LANES)],
        out_specs=[rev(LANES), _vec_spec(LANES)],
        out_shape=[jax.ShapeDtypeStruct((T, LANES), BF), jax.ShapeDtypeStruct((1, LANES), F32)],
        scratch_shapes=[pltpu.VMEM((1, LANES), F32)], compiler_params=_params("arbitrary"),
    )(ext_q, ext_k, fl, b_forget)


def _adamw(w, g, m, v, *, name, tr=256):
    _, rows, cols = w.shape
    tr = tr if rows % tr == 0 else rows

    def body(w_ref, g_ref, m_ref, v_ref, go_ref, d_ref, nm_ref, nv_ref):
        gv = g_ref[...]
        go_ref[...] = gv
        nm = ADAM_B1 * m_ref[...] + (1.0 - ADAM_B1) * gv
        nv = ADAM_B2 * v_ref[...] + (1.0 - ADAM_B2) * (gv * gv)
        m_hat = nm / (1.0 - ADAM_B1 ** ADAM_STEP)
        v_hat = nv / (1.0 - ADAM_B2 ** ADAM_STEP)
        d_ref[...] = -ADAM_LR * (m_hat / (jnp.sqrt(v_hat) + ADAM_EPS) + ADAM_WD * w_ref[...])
        nm_ref[...] = nm
        nv_ref[...] = nv

    spec = pl.BlockSpec((None, tr, cols), lambda i: (0, i, 0))
    return pl.pallas_call(
        body, name=name, grid=(rows // tr,), in_specs=[spec, pl.BlockSpec((tr, cols), lambda i: (i, 0)), spec, spec],
        out_specs=[spec] * 4, out_shape=[jax.ShapeDtypeStruct((1, rows, cols), F32)] * 4,
        compiler_params=_params("parallel"),
    )(w, g, m, v)


HBM = pl.BlockSpec(memory_space=pltpu.HBM)
BF16_ROWS = 16


def _place():
    x, y, c = lax.axis_index("x"), lax.axis_index("y"), lax.axis_index("c")
    others = [(1 - x, y), (x, 1 - y), (1 - x, 1 - y)]
    return x, y, c, others


def _chip(xy):
    return 2 * xy[0] + xy[1]


def _row_halves(c, rows):
    half = rows // 2
    assert half % BF16_ROWS == 0
    return (pl.ds(pl.multiple_of(c * half, BF16_ROWS), half), pl.ds(pl.multiple_of((1 - c) * half, BF16_ROWS), half))


def _remote(src, dst, send_sems, recv_sems, k, to):
    return pltpu.make_async_remote_copy(src_ref=src, dst_ref=dst, send_sem=send_sems.at[k], recv_sem=recv_sems.at[k],
                                        device_id=to, device_id_type=MESH)


def _gathered_shapes(shards):
    return [jax.ShapeDtypeStruct((N_CHIPS,) + s.shape, s.dtype) for s in shards]


def _gather_semaphores(n):
    return [pltpu.SemaphoreType.DMA((6 * n,)), pltpu.SemaphoreType.DMA((6 * n,))]


def _gather_phases(w_refs, g_refs, send_sems, recv_sems):
    n = len(w_refs)
    x, y, c, others = _place()
    sibling, me = (x, y, 1 - c), _chip((x, y))
    halves = [_row_halves(c, w.shape[0]) for w in w_refs]

    def sent(a, j, o):
        mine, _ = halves[a]
        return _remote(w_refs[a].at[mine, :], g_refs[a].at[me, mine, :], send_sems, recv_sems, 6 * a + j, (*o, c))

    def passed(a, j, o):
        landed = g_refs[a].at[_chip(o), halves[a][0], :]
        return _remote(landed, landed, send_sems, recv_sems, 6 * a + 3 + j, sibling)

    def start():
        for a in range(n):
            for j, o in enumerate(others):
                sent(a, j, o).start()

    def forward():
        for j, o in enumerate(others):
            for a in range(n):
                landed = g_refs[a].at[_chip(o), halves[a][0], :]
                _remote(landed, landed, send_sems, recv_sems, 6 * a + j, (*o, c)).wait_recv()
                passed(a, j, o).start()

    def finish():
        for j, o in enumerate(others):
            for a in range(n):
                landed = g_refs[a].at[_chip(o), halves[a][1], :]
                _remote(landed, landed, send_sems, recv_sems, 6 * a + 3 + j, sibling).wait_recv()
        for a in range(n):
            for j, o in enumerate(others):
                sent(a, j, o).wait_send()
                passed(a, j, o).wait_send()

    return start, forward, finish


def _exchange_halves(arrays, *, name):
    n = len(arrays)

    def body(*refs):
        for phase in _exchange_phases(refs[:n], refs[n:2 * n], *refs[2 * n:]):
            phase()

    return pl.pallas_call(
        body, name=name, in_specs=[HBM] * n, out_specs=[HBM] * n, out_shape=_exchanged_shapes(arrays),
        scratch_shapes=_exchange_semaphores(n),
    )(*arrays)


def _exchanged_shapes(arrays):
    return [jax.ShapeDtypeStruct(s.shape[:-2] + (s.shape[-2] // 2, s.shape[-1]), F32) for s in arrays]


def _exchange_semaphores(n):
    return [pltpu.SemaphoreType.DMA((n,)), pltpu.SemaphoreType.DMA((n,))]


def _exchange_phases(g_refs, r_refs, send_sems, recv_sems):
    x, y, c, _ = _place()

    def copy(a):
        _, theirs = _row_halves(c, g_refs[a].shape[-2])
        src = g_refs[a].at[:, theirs, :] if len(g_refs[a].shape) == 3 else g_refs[a].at[theirs, :]
        return _remote(src, r_refs[a], send_sems, recv_sems, a, (x, y, 1 - c))

    def start():
        for a in range(len(g_refs)):
            copy(a).start()

    def finish():
        for a in range(len(g_refs)):
            copy(a).wait()

    return start, finish


def _scatter_to_owners(chip_sums):
    n = len(chip_sums)

    def body(*refs):
        for phase in _scatter_phases(refs[:n], refs[n:2 * n], *refs[2 * n:]):
            phase()

    return pl.pallas_call(
        body, name="scatter_to_owners", in_specs=[HBM] * n, out_specs=[HBM] * n,
        out_shape=_scattered_shapes(chip_sums), scratch_shapes=_scatter_semaphores(n),
    )(*chip_sums)


def _scattered_shapes(chip_sums):
    return [jax.ShapeDtypeStruct(b.shape if b.ndim == 3 else (N_CHIPS,) + b.shape, b.dtype) for b in chip_sums]


def _scatter_semaphores(n):
    return [pltpu.SemaphoreType.DMA((3 * n,)), pltpu.SemaphoreType.DMA((3 * n,))]


def _scatter_phases(b_refs, r_refs, send_sems, recv_sems):
    n = len(b_refs)
    x, y, c, others = _place()
    me = _chip((x, y))

    def sent(a, j, o):
        src = b_refs[a].at[_chip(o)] if len(b_refs[a].shape) == 3 else b_refs[a]
        return _remote(src, r_refs[a].at[me], send_sems, recv_sems, 3 * a + j, (*o, c))

    def start():
        for a in range(n):
            for j, o in enumerate(others):
                sent(a, j, o).start()

    def finish():
        for a in range(n):
            for j, o in enumerate(others):
                landed = r_refs[a].at[_chip(o)]
                _remote(landed, landed, send_sems, recv_sems, 3 * a + j, (*o, c)).wait_recv()
        for a in range(n):
            for j, o in enumerate(others):
                sent(a, j, o).wait_send()

    return start, finish


def _join_halves(totals):
    n = len(totals)

    def body(*refs):
        in_refs, out_refs, (send_sems, recv_sems) = refs[:n], refs[n:2 * n], refs[2 * n:]
        x, y, c, _ = _place()
        copies = []
        for a in range(n):
            mine, _ = _row_halves(c, in_refs[a].shape[0])
            copies.append(_remote(in_refs[a].at[mine, :], out_refs[a].at[mine, :], send_sems, recv_sems, a, (x, y, 1 - c)))
            copies[-1].start()
        for cp in copies:
            cp.wait()

    return pl.pallas_call(
        body, name="join_halves", in_specs=[HBM] * n, out_specs=[HBM] * n,
        out_shape=[jax.ShapeDtypeStruct(t.shape, F32) for t in totals], input_output_aliases={a: a for a in range(n)},
        scratch_shapes=[pltpu.SemaphoreType.DMA((n,)), pltpu.SemaphoreType.DMA((n,))],
    )(*totals)


ADD_ROWS = 128


def _add_sibling(g, r, place, *, name):
    lead, (half, cols) = g.shape[:-2], r.shape[-2:]
    tr = min(ADD_ROWS, half)
    nb = half // tr
    zeros = (0,) * len(lead)

    def body(place_ref, g_ref, r_ref, o_ref, ob_ref):
        s = g_ref[...] + r_ref[...]
        o_ref[...] = s
        ob_ref[...] = s.astype(BF)

    spec = pl.BlockSpec(lead + (tr, cols), lambda i, p: zeros + (i, 0))
    return pl.pallas_call(
        body, name=name,
        grid_spec=pltpu.PrefetchScalarGridSpec(
            num_scalar_prefetch=1, grid=(nb,),
            in_specs=[pl.BlockSpec(lead + (tr, cols), lambda i, p: zeros + (p[1] * nb + i, 0)), spec], out_specs=[spec, spec]),
        out_shape=[jax.ShapeDtypeStruct(r.shape, F32), jax.ShapeDtypeStruct(r.shape, BF)],
        compiler_params=_params("parallel"),
    )(place, g, r)


def _add_chips(own, received, place, *, name, own_slots):
    half, cols = received.shape[-2:]
    tr = min(ADD_ROWS, half)
    nb = half // tr

    def written(k, p):
        return jnp.where(p[0] == k, (k + 1) % N_CHIPS, k)

    def body(place_ref, own_ref, *refs):
        o_ref = refs[N_CHIPS]
        mine = own_ref[0] if own_slots else own_ref[...]
        if own_slots:
            acc = mine
            for k in range(N_CHIPS):
                acc = acc + jnp.where(place_ref[0] == k, 0.0, refs[k][0].astype(F32))
        else:
            terms = [jnp.where(place_ref[0] == k, mine, refs[k][0]) for k in range(N_CHIPS)]
            acc = ((terms[0] + terms[1]) + terms[2]) + terms[3]
        o_ref[...] = acc

    own_spec = (pl.BlockSpec((1, tr, cols), lambda i, p: (p[0], i, 0)) if own_slots
                else pl.BlockSpec((tr, cols), lambda i, p: (i, 0)))
    return pl.pallas_call(
        body, name=name,
        grid_spec=pltpu.PrefetchScalarGridSpec(
            num_scalar_prefetch=1, grid=(nb,),
            in_specs=[own_spec] + [pl.BlockSpec((1, tr, cols), functools.partial(lambda i, p, k: (written(k, p), i, 0), k=k))
                                   for k in range(N_CHIPS)],
            out_specs=pl.BlockSpec((tr, cols), lambda i, p: (p[1] * nb + i, 0))),
        out_shape=jax.ShapeDtypeStruct((2 * half, cols), F32), compiler_params=_params("parallel"),
    )(place, own, *([received] * N_CHIPS))


SHARDED = (("w_in", (D_MODEL, 4616), 1), ("w_branch_sgu", (SGU_W, D_MODEL), 1), ("w_branch_attn", (ATTN_W, D_MODEL), 1),
           ("w_out", (D_MODEL, D_MODEL), 0), ("w_up", (D_MODEL, D_FF), 1), ("w_down", (D_FF, D_MODEL), 0))
SMALL = (("g_mix_pre", (1, D_MODEL)), ("b_forget", (1, N_HEADS)), ("g_sgu", (1, SGU_W)), ("b_sgu", (1, SGU_W)),
         ("w_spatial", (N_GROUPS * CHUNK, CHUNK)), ("b_spatial", (N_GROUPS, CHUNK)), ("g_mix_post", (1, D_MODEL)),
         ("g_ffn_pre", (1, D_MODEL)), ("g_ffn_post", (1, D_MODEL)))
SMALL_ALIGN = 2 * ADD_ROWS


def _shard_shape(shape, axis):
    return tuple(s // N_CHIPS if a == axis else s for a, s in enumerate(shape))


def _slots_to_full(slots, axis):
    return slots.reshape(-1, slots.shape[2]) if axis == 0 else slots.transpose(1, 0, 2).reshape(slots.shape[1], -1)


def _full_to_slots(full, axis):
    if axis == 0:
        return full.reshape(N_CHIPS, -1, full.shape[1])
    return full.reshape(full.shape[0], N_CHIPS, -1).transpose(1, 0, 2)


def _small_rows(shape):
    return -(-(shape[0] * shape[1]) // (8 * LANES)) * 8


def _pack_small(values):
    parts = []
    for name, shape in SMALL:
        flat = values[name].reshape(-1)
        n = _small_rows(shape)
        parts.append(jnp.pad(flat, (0, n * LANES - flat.shape[0])).reshape(n, LANES))
    rows = sum(p.shape[0] for p in parts)
    pad = -(-rows // SMALL_ALIGN) * SMALL_ALIGN - rows
    return jnp.concatenate(parts + [jnp.zeros((pad, LANES), F32)], axis=0)


def _unpack_small(packed):
    out, row = {}, 0
    for name, shape in SMALL:
        n = _small_rows(shape)
        out[name] = packed[row:row + n].reshape(-1)[:shape[0] * shape[1]].reshape(shape)
        row += n
    return out


IN_Z, IN_Q, IN_K, IN_V, IN_F, IN_G, IN_END = 0, 1024, 1536, 2048, 2560, 2568, 4616


LATE_WEIGHTS = ("w_branch_sgu", "w_branch_attn", "w_out", "w_up", "w_down")
EARLY_GRADS = LATE_WEIGHTS


def _with_own_slot(shard, gathered, chip):
    return jnp.where(jnp.arange(N_CHIPS)[:, None, None] == chip, shard[None], gathered)


def _assemble(name, shard, gathered, chip):
    axis = {n: a for n, _, a in SHARDED}[name]
    return _slots_to_full(_with_own_slot(shard, gathered, chip), axis)


def _columns_from_slots(slots, bounds):
    width = slots.shape[2]
    pieces = []
    for lo, hi in zip(bounds[:-1], bounds[1:], strict=True):
        parts = [slots[k][:, max(lo, k * width) - k * width:min(hi, (k + 1) * width) - k * width]
                 for k in range(N_CHIPS) if max(lo, k * width) < min(hi, (k + 1) * width)]
        pieces.append(parts[0] if len(parts) == 1 else jnp.concatenate(parts, axis=1))
    return pieces


def _columns_to_slots(pieces):
    width = sum(p.shape[1] for p in pieces) // N_CHIPS
    slots = []
    for k in range(N_CHIPS):
        parts, start = [], 0
        for p in pieces:
            lo, hi = max(k * width, start), min((k + 1) * width, start + p.shape[1])
            if lo < hi:
                parts.append(p[:, lo - start:hi - start])
            start += p.shape[1]
        slots.append(jnp.concatenate(parts, axis=1))
    return jnp.stack(slots)


def _local_step(x, target, shards, small, place):
    b_forget = jnp.pad(small["b_forget"], ((0, 0), (0, LANES - N_HEADS)))
    causal = jnp.tril(jnp.ones((CHUNK, CHUNK), bool))
    ws = jnp.where(causal[None], small["w_spatial"].reshape(N_GROUPS, CHUNK, CHUNK), 0.0).astype(BF)
    ws_t = ws.transpose(0, 2, 1)
    bias_plane = jnp.repeat(small["b_spatial"].T, HEAD_DIM, axis=1)

    xn, (w_in_slots,) = _rms_fwd(x, small["g_mix_pre"], [shards["w_in"]])
    w_z, w_q, w_k, w_v, w_f, w_ga, w_gb = _columns_from_slots(
        _with_own_slot(shards["w_in"], w_in_slots, place[0]), (IN_Z, IN_Q, IN_K, IN_V, IN_F, IN_G, IN_G + D_MODEL, IN_END))
    w_qkv, w_g = jnp.concatenate([w_q, w_k, w_v], axis=1), jnp.concatenate([w_ga, w_gb], axis=1)
    w_f = jnp.pad(w_f, ((0, 0), (0, LANES - N_HEADS)))
    z, qkv, gl, fl = _project(xn, [w_z, w_qkv, w_g, w_f], [F32, BF, BF, F32], name="proj_in")
    ysgu = _sgu_fwd(z, small["g_sgu"], small["b_sgu"], ws, bias_plane)
    qf, kl, vl, tile_stats = _attn_prep(qkv, fl, b_forget)
    first_key_tile, last_query_tile, bounded = _attn_ranges(tile_stats)
    yattn, yattn_f, ql, gathered = _attn_fwd(qf, kl, vl, first_key_tile, bounded, [shards[name] for name in LATE_WEIGHTS])
    w = {name: _assemble(name, shards[name], got, place[0]) for name, got in zip(LATE_WEIGHTS, gathered, strict=True)}
    a, b, merged = _branch_merge(ysgu, yattn, w["w_branch_sgu"], w["w_branch_attn"], gl)
    o, h1, xn2 = _matmul_rows(
        [(merged, w["w_out"])], nt=False, rows=[x], vecs=[small["g_mix_post"], small["g_ffn_pre"]], row_outs=[F32, F32, BF],
        n_sums=0, epilogue=_mixer_out_fwd, name="proj_out_norms")

    def relu2(acc):
        r = jnp.maximum(acc, 0.0)
        return (r * r,)

    hid = _matmul([(xn2, w["w_up"])], nt=False, out_dtypes=[BF], name="ffn_up", epilogue=relu2, tm=FFN_ROWS)
    dy, ddn, sq, dg_ffn_post = _matmul_rows(
        [(hid, w["w_down"])], nt=False, rows=[h1, target], vecs=[small["g_ffn_post"]], row_outs=[F32, BF], n_sums=2,
        epilogue=_loss_head, name="ffn_down_loss")

    dup = _matmul([(ddn, w["w_down"])], nt=True, out_dtypes=[BF], name="ffn_down_bwd", tm=FFN_ROWS,
                  epilogue=lambda acc, h: (acc * (2.0 * jnp.sqrt(h.astype(F32))),), extras=[hid])
    dw_down = _matmul_tn(hid, ddn, name="dw_down")
    dh1, do, dg_ffn_pre, dg_mix_post = _matmul_rows(
        [(dup, w["w_up"])], nt=True, rows=[h1, dy, o], vecs=[small["g_ffn_pre"], small["g_mix_post"]], row_outs=[F32, BF],
        n_sums=2, epilogue=_mixer_out_bwd, name="ffn_up_bwd_norms")
    dw_up = _matmul_tn(xn2, dup, name="dw_up", slots=True)

    def gate_bwd(dm, a_t, b_t, gla, glb):
        ga, gb = jax.nn.sigmoid(gla.astype(F32)), jax.nn.sigmoid(glb.astype(F32))
        return dm * ga, dm * gb, dm * a_t.astype(F32) * (ga * (1.0 - ga)), dm * b_t.astype(F32) * (gb * (1.0 - gb))

    da, db, dgla, dglb = _matmul([(do, w["w_out"])], nt=True, out_dtypes=[BF] * 4, name="proj_out_bwd",
                                 epilogue=gate_bwd, extras=[a, b, (gl, 0), (gl, D_MODEL)])
    dw_out = _matmul_tn(merged, do, name="dw_out")
    dysgu = _matmul([(da, w["w_branch_sgu"])], nt=True, out_dtypes=[F32], name="branch_sgu_bwd")
    dyattn = _matmul([(db, w["w_branch_attn"])], nt=True, out_dtypes=[F32], name="branch_attn_bwd")
    dw_bs = _matmul_tn(ysgu, da, name="dw_branch_sgu")
    dw_ba = _matmul_tn(yattn, db, name="dw_branch_attn")
    early = {"w_branch_sgu": _full_to_slots(dw_bs, 1), "w_branch_attn": _full_to_slots(dw_ba, 1),
             "w_out": _full_to_slots(dw_out, 0), "w_up": dw_up, "w_down": _full_to_slots(dw_down, 0)}
    (dz, dws, dbs, dg_sgu, db_sgu), early_theirs = _sgu_bwd(
        dysgu, z, small["g_sgu"], small["b_sgu"], ws, ws_t, bias_plane, [early[name] for name in EARLY_GRADS])
    early_sums = {name: _add_sibling(early[name], theirs, place, name="add_sibling_" + name)
                  for name, theirs in zip(EARLY_GRADS, early_theirs, strict=True)}
    dout = _attn_bwd_prep(dyattn, yattn_f)
    (dq, dk, dv, ext_q, ext_k), early_received = _attn_bwd(
        kl, vl, ql, dout, last_query_tile, [early_sums[name][1] for name in EARLY_GRADS])
    dfl, dbf = _forget_bwd(ext_q, ext_k, fl, b_forget)
    dw_in = _columns_to_slots(
        [_matmul_tn(xn, dz, name="dw_in_z"), _matmul_tn(xn, dq, name="dw_in_q"), _matmul_tn(xn, dk, name="dw_in_k"),
         _matmul_tn(xn, dv, name="dw_in_v"), _matmul_tn(xn, dfl, name="dw_in_f")[:, :N_HEADS],
         _matmul_tn(xn, dgla, name="dw_in_ga"), _matmul_tn(xn, dglb, name="dw_in_gb")])
    (dw_in_theirs,) = _exchange_halves([dw_in], name="exchange_halves_w_in")
    dw_in_sum = _add_sibling(dw_in, dw_in_theirs, place, name="add_sibling_w_in")
    dx, dg_mix_pre, dw_in_received = _matmul_rows(
        [(dz, w_z), (dq, w_q), (dk, w_k), (dv, w_v), (dgla, w_ga), (dglb, w_gb), (dfl, w_f)],
        nt=True, rows=[x, dh1], vecs=[small["g_mix_pre"]], row_outs=[F32], n_sums=1, epilogue=_input_norm_bwd,
        name="proj_in_bwd_norm", scatter=[dw_in_sum[1]])

    reduced = {name: (early_sums[name][0], got) for name, got in zip(EARLY_GRADS, early_received, strict=True)}
    reduced["w_in"] = (dw_in_sum[0], dw_in_received)
    small_grads = {"g_mix_pre": dg_mix_pre, "b_forget": dbf[:, :N_HEADS], "g_sgu": dg_sgu, "b_sgu": db_sgu,
                   "w_spatial": dws.reshape(N_GROUPS * CHUNK, CHUNK), "b_spatial": dbs[:, :N_GROUPS].T,
                   "g_mix_post": dg_mix_post, "g_ffn_pre": dg_ffn_pre, "g_ffn_post": dg_ffn_post}
    return sq, dx, reduced, small_grads


NAMES = ("g_mix_pre", "w_in", "b_forget", "g_sgu", "b_sgu", "w_spatial", "b_spatial", "w_branch_sgu", "w_branch_attn",
         "w_out", "g_mix_post", "g_ffn_pre", "w_up", "w_down", "g_ffn_post")


def kernel(x, g_mix_pre, w_in, b_forget, g_sgu, b_sgu, w_spatial, b_spatial, w_branch_sgu, w_branch_attn, w_out, g_mix_post, g_ffn_pre, w_up, w_down, g_ffn_post, loss_target, m_g_mix_pre, m_w_in, m_b_forget, m_g_sgu, m_b_sgu, m_w_spatial, m_b_spatial, m_w_branch_sgu, m_w_branch_attn, m_w_out, m_g_mix_post, m_g_ffn_pre, m_w_up, m_w_down, m_g_ffn_post, v_g_mix_pre, v_w_in, v_b_forget, v_g_sgu, v_b_sgu, v_w_spatial, v_b_spatial, v_w_branch_sgu, v_w_branch_attn, v_w_out, v_g_mix_post, v_g_ffn_pre, v_w_up, v_w_down, v_g_ffn_post):
    weights = dict(zip(NAMES, (g_mix_pre, w_in, b_forget, g_sgu, b_sgu, w_spatial, b_spatial, w_branch_sgu, w_branch_attn,
                               w_out, g_mix_post, g_ffn_pre, w_up, w_down, g_ffn_post), strict=True))
    first = dict(zip(NAMES, (m_g_mix_pre, m_w_in, m_b_forget, m_g_sgu, m_b_sgu, m_w_spatial, m_b_spatial, m_w_branch_sgu,
                             m_w_branch_attn, m_w_out, m_g_mix_post, m_g_ffn_pre, m_w_up, m_w_down, m_g_ffn_post), strict=True))
    second = dict(zip(NAMES, (v_g_mix_pre, v_w_in, v_b_forget, v_g_sgu, v_b_sgu, v_w_spatial, v_b_spatial, v_w_branch_sgu,
                              v_w_branch_attn, v_w_out, v_g_mix_post, v_g_ffn_pre, v_w_up, v_w_down, v_g_ffn_post), strict=True))
    shard_shapes = {name: _shard_shape(shape, axis) for name, shape, axis in SHARDED}
    small_shapes = dict(SMALL)
    view = lambda name, a: a.reshape(shard_shapes.get(name) or small_shapes[name])

    place = jnp.stack([2 * lax.axis_index("x") + lax.axis_index("y"), lax.axis_index("c")]).astype(jnp.int32)

    shards = {name: view(name, weights[name]).astype(BF) for name, _, _ in SHARDED}
    small = {name: view(name, weights[name]) for name, _ in SMALL}
    sq, dx, reduced, small_grads = _local_step(x[0], loss_target[0], shards, small, place)
    loss = lax.psum(0.5 * jnp.sum(sq) / D_MODEL, ("x", "y", "c"))

    small_mine = _pack_small(small_grads)
    (small_theirs,) = _exchange_halves([small_mine], name="exchange_halves_small")
    small_sum, _ = _add_sibling(small_mine, small_theirs, place, name="add_sibling_small")
    (small_received,) = _scatter_to_owners([small_sum])
    totals = {name: _add_chips(s, r, place, name="add_chips_" + name, own_slots=True) for name, (s, r) in reduced.items()}
    small_total = _add_chips(small_sum, small_received, place, name="add_chips_small", own_slots=False)
    joined = _join_halves([totals[name] for name, _, _ in SHARDED] + [small_total])
    grad = {**{name: g for (name, _, _), g in zip(SHARDED, joined[:-1], strict=True)}, **_unpack_small(joined[-1])}

    grad_out, delta, new_m, new_v = {}, {}, {}, {}
    for name in NAMES:
        rows, cols = grad[name].shape
        as_given = lambda a: a.reshape(1, rows, cols)
        grad_out[name], delta[name], new_m[name], new_v[name] = _adamw(
            as_given(weights[name]), grad[name], as_given(first[name]), as_given(second[name]), name="adamw_" + name)

    like = lambda d: [d[name].reshape(weights[name].shape) for name in NAMES]
    return (loss, dx[None], *like(grad_out), *like(delta), *like(new_m), *like(new_v))
```

```python
import functools

import jax
import jax.numpy as jnp
import numpy as np
from jax import lax
from jax.experimental import pallas as pl
from jax.experimental.pallas import tpu as pltpu

F32 = jnp.float32
BF = jnp.bfloat16
MESH = pl.DeviceIdType.MESH

D_MODEL = 1024
N_HEADS = 8
HEAD_DIM = 64
ATTN_W = N_HEADS * HEAD_DIM
SGU_W = 512
N_GROUPS = 8
CHUNK = 128
D_FF = 4096
EPS = 1e-6
Q_SCALE = HEAD_DIM ** -0.5
N_CHIPS = 4
LANES = 128

ADAM_LR = 0.001
ADAM_B1 = 0.9
ADAM_B2 = 0.999
ADAM_EPS = 1e-08
ADAM_WD = 0.01
ADAM_STEP = 10

VMEM_LIMIT = 48 * 1024 * 1024
BIG_VMEM = 58 * 1024 * 1024
NEG = -1e30


def _params(*sem):
    return pltpu.CompilerParams(dimension_semantics=sem, vmem_limit_bytes=VMEM_LIMIT)


def _dot(a, b):
    return jnp.dot(a, b, preferred_element_type=F32)


def _dot_nt(a, b):
    return lax.dot_general(a, b, (((1,), (1,)), ((), ())), preferred_element_type=F32)


def _dot_tn(a, b):
    return lax.dot_general(a, b, (((0,), (0,)), ((), ())), preferred_element_type=F32)


def _split3(c):
    hi = c.astype(BF).astype(F32)
    r = c - hi
    mid = r.astype(BF).astype(F32)
    lo = (r - mid).astype(BF).astype(F32)
    return hi, mid, lo


def _gelu(x):
    k = 0.7978845608028654
    return 0.5 * x * (1.0 + jnp.tanh(k * (x + 0.044715 * (x * x * x))))


def _gelu_grad(x):
    k = 0.7978845608028654
    x2 = x * x
    t = jnp.tanh(k * (x + 0.044715 * (x2 * x)))
    return 0.5 * (1.0 + t) + 0.5 * x * (1.0 - t * t) * (k * (1.0 + 3.0 * 0.044715 * x2))


def _rms_bwd(a, g, dy):
    r = lax.rsqrt(jnp.mean(a * a, axis=-1, keepdims=True) + EPS)
    n = a * r
    dn = dy * g
    da = r * (dn - n * jnp.mean(dn * n, axis=-1, keepdims=True))
    return da, dy * n


MM_ROWS = 1024
MM_COLS = 512
FFN_ROWS = 2048


def _matmul(pairs, *, nt, out_dtypes, name, tm=MM_ROWS, tn=MM_COLS, epilogue=None, extras=()):
    n_pairs, n_extra = len(pairs), len(extras)
    M = pairs[0][0].shape[0]
    N = pairs[0][1].shape[0] if nt else pairs[0][1].shape[1]
    tm, tn = min(tm, M), min(tn, N)
    assert M % tm == 0 and N % tn == 0

    def body(*refs):
        acc = None
        for p in range(n_pairs):
            a_ref, b_ref = refs[2 * p], refs[2 * p + 1]
            d = _dot_nt(a_ref[...], b_ref[...]) if nt else _dot(a_ref[...], b_ref[...])
            acc = d if acc is None else acc + d
        e_refs = refs[2 * n_pairs:2 * n_pairs + n_extra]
        o_refs = refs[2 * n_pairs + n_extra:]
        outs = (acc,) if epilogue is None else epilogue(acc, *[e[...] for e in e_refs])
        for o_ref, o in zip(o_refs, outs, strict=True):
            o_ref[...] = o.astype(o_ref.dtype)

    in_specs, args = [], []
    for a, b in pairs:
        K = a.shape[1]
        in_specs.append(pl.BlockSpec((tm, K), lambda i, j: (i, 0)))
        in_specs.append(pl.BlockSpec((tn, K), lambda i, j: (j, 0)) if nt else pl.BlockSpec((K, tn), lambda i, j: (0, j)))
        args += [a, b]
    for e in extras:
        e, col = e if isinstance(e, tuple) else (e, 0)
        in_specs.append(pl.BlockSpec((tm, tn), functools.partial(lambda i, j, off: (i, j + off), off=col // tn)))
        args.append(e)
    outs = pl.pallas_call(
        body, name=name, grid=(M // tm, N // tn), in_specs=in_specs,
        out_specs=[pl.BlockSpec((tm, tn), lambda i, j: (i, j)) for _ in out_dtypes],
        out_shape=[jax.ShapeDtypeStruct((M, N), dt) for dt in out_dtypes],
        compiler_params=_params("parallel", "parallel"),
    )(*args)
    return outs if len(outs) > 1 else outs[0]


def _matmul_tn_multi(a, bs, *, name, tk=1024):
    T, K1 = a.shape
    tk = min(tk, T)
    n = len(bs)

    def body(a_ref, *refs):
        @pl.when(pl.program_id(0) == 0)
        def _():
            for o_ref in refs[n:]:
                o_ref[...] = jnp.zeros_like(o_ref)

        av = a_ref[...]
        for b_ref, o_ref in zip(refs[:n], refs[n:], strict=True):
            o_ref[...] += _dot_tn(av, b_ref[...])

    return pl.pallas_call(
        body, name=name, grid=(T // tk,),
        in_specs=[pl.BlockSpec((tk, K1), lambda k: (k, 0))] + [pl.BlockSpec((tk, b.shape[1]), lambda k: (k, 0)) for b in bs],
        out_specs=[pl.BlockSpec((K1, b.shape[1]), lambda k: (0, 0)) for b in bs],
        out_shape=[jax.ShapeDtypeStruct((K1, b.shape[1]), F32) for b in bs],
        compiler_params=pltpu.CompilerParams(dimension_semantics=("arbitrary",), vmem_limit_bytes=BIG_VMEM),
    )(a, *bs)


def _project(a, weights, out_dtypes, *, name, tm=512):
    M, K = a.shape
    tm = min(tm, M)
    n = len(weights)

    def body(a_ref, *refs):
        av = a_ref[...]
        for w_ref, o_ref in zip(refs[:n], refs[n:], strict=True):
            o_ref[...] = _dot(av, w_ref[...]).astype(o_ref.dtype)

    return pl.pallas_call(
        body, name=name, grid=(M // tm,),
        in_specs=[pl.BlockSpec((tm, K), lambda i: (i, 0))] + [pl.BlockSpec(w.shape, lambda i: (0, 0)) for w in weights],
        out_specs=[pl.BlockSpec((tm, w.shape[1]), lambda i: (i, 0)) for w in weights],
        out_shape=[jax.ShapeDtypeStruct((M, w.shape[1]), dt) for w, dt in zip(weights, out_dtypes, strict=True)],
        compiler_params=_params("parallel"),
    )(a, *weights)


def _matmul_tn(a, b, *, name, tm=1024, tn=1024, tk=2048, slots=False):
    T, K1 = a.shape
    N = b.shape[1]
    tm, tn, tk = min(tm, K1), min(tn, N // N_CHIPS if slots else N), min(tk, T)
    assert K1 % tm == 0 and (N // N_CHIPS if slots else N) % tn == 0 and T % tk == 0
    per_slot = N // N_CHIPS // tn

    def body(a_ref, b_ref, o_ref):
        @pl.when(pl.program_id(2) == 0)
        def _():
            o_ref[...] = jnp.zeros_like(o_ref)

        o_ref[...] += _dot_tn(a_ref[...], b_ref[...])

    if slots:
        out_spec = pl.BlockSpec((None, tm, tn), lambda i, j, k: (j // per_slot, i, j % per_slot))
        out_shape = jax.ShapeDtypeStruct((N_CHIPS, K1, N // N_CHIPS), F32)
    else:
        out_spec = pl.BlockSpec((tm, tn), lambda i, j, k: (i, j))
        out_shape = jax.ShapeDtypeStruct((K1, N), F32)
    return pl.pallas_call(
        body, name=name, grid=(K1 // tm, N // tn, T // tk),
        in_specs=[pl.BlockSpec((tk, tm), lambda i, j, k: (k, i)), pl.BlockSpec((tk, tn), lambda i, j, k: (k, j))],
        out_specs=out_spec, out_shape=out_shape,
        compiler_params=_params("parallel", "parallel", "arbitrary"),
    )(a, b)


def _branch_merge(ysgu, yattn, w_bs, w_ba, gl, *, tm=MM_ROWS, tn=MM_COLS):
    T = ysgu.shape[0]
    tm = min(tm, T)
    nj = D_MODEL // tn

    def body(ys_ref, ya_ref, wbs_ref, wba_ref, gla_ref, glb_ref, a_ref, b_ref, m_ref):
        a = _dot(ys_ref[...], wbs_ref[...])
        b = _dot(ya_ref[...], wba_ref[...])
        a_ref[...] = a.astype(BF)
        b_ref[...] = b.astype(BF)
        m_ref[...] = (jax.nn.sigmoid(gla_ref[...].astype(F32)) * a + jax.nn.sigmoid(glb_ref[...].astype(F32)) * b).astype(BF)

    return pl.pallas_call(
        body, name="branch_merge", grid=(T // tm, nj),
        in_specs=[
            pl.BlockSpec((tm, SGU_W), lambda i, j: (i, 0)),
            pl.BlockSpec((tm, ATTN_W), lambda i, j: (i, 0)),
            pl.BlockSpec((SGU_W, tn), lambda i, j: (0, j)),
            pl.BlockSpec((ATTN_W, tn), lambda i, j: (0, j)),
            pl.BlockSpec((tm, tn), lambda i, j: (i, j)),
            pl.BlockSpec((tm, tn), lambda i, j: (i, j + nj)),
        ],
        out_specs=[pl.BlockSpec((tm, tn), lambda i, j: (i, j))] * 3,
        out_shape=[jax.ShapeDtypeStruct((T, D_MODEL), BF)] * 3,
        compiler_params=_params("parallel", "parallel"),
    )(ysgu, yattn, w_bs, w_ba, gl, gl)


def _row_spec(tr, width):
    return pl.BlockSpec((tr, width), lambda i: (i, 0))


def _vec_spec(width):
    return pl.BlockSpec((1, width), lambda i: (0, 0))


def _rms_fwd(x, g, shards, *, tr=256):
    T = x.shape[0]
    tr = min(tr, T)
    n_steps = T // tr
    k = len(shards)

    def body(x_ref, g_ref, *refs):
        step = pl.program_id(0)
        gather_start, gather_forward, gather_finish = _gather_phases(refs[:k], refs[k + 1:2 * k + 1], *refs[2 * k + 1:])
        pl.when(step == 0)(gather_start)
        pl.when(step == (3 * n_steps) // 4)(gather_forward)
        xv = x_ref[...]
        r = lax.rsqrt(jnp.mean(xv * xv, axis=-1, keepdims=True) + EPS)
        refs[k][...] = ((xv * r) * g_ref[...]).astype(BF)
        pl.when(step == n_steps - 1)(gather_finish)

    outs = pl.pallas_call(
        body, name="rms_fwd", grid=(n_steps,),
        in_specs=[_row_spec(tr, D_MODEL), _vec_spec(D_MODEL)] + [HBM] * k, out_specs=[_row_spec(tr, D_MODEL)] + [HBM] * k,
        out_shape=[jax.ShapeDtypeStruct((T, D_MODEL), BF)] + _gathered_shapes(shards),
        scratch_shapes=_gather_semaphores(k), compiler_params=_params("arbitrary"),
    )(x, g, *shards)
    return outs[0], outs[1:]


def _mixer_out_fwd(o, x, g_post, g_pre):
    r = lax.rsqrt(jnp.mean(o * o, axis=-1, keepdims=True) + EPS)
    h1 = x + (o * r) * g_post
    r2 = lax.rsqrt(jnp.mean(h1 * h1, axis=-1, keepdims=True) + EPS)
    return o, h1, (h1 * r2) * g_pre


def _matmul_rows(pairs, *, nt, rows, vecs, row_outs, n_sums, epilogue, name, tm=512, scatter=()):
    M = pairs[0][0].shape[0]
    N = pairs[0][1].shape[0] if nt else pairs[0][1].shape[1]
    tm = min(tm, M)
    n_steps = M // tm
    n_pairs, n_rows, n_vecs, n_out, n_scatter = len(pairs), len(rows), len(vecs), len(row_outs), len(scatter)

    def body(*refs):
        groups, at = [], 2 * n_pairs
        for count in (n_rows, n_vecs, n_scatter, n_out, n_sums, n_scatter):
            groups.append(refs[at:at + count])
            at += count
        r_refs, v_refs, b_refs, o_refs, s_refs, got_refs = groups
        sems = refs[at:]
        step = pl.program_id(0)
        if n_scatter:
            scatter_start, scatter_finish = _scatter_phases(b_refs, got_refs, *sems)
            pl.when(step == 0)(scatter_start)

        @pl.when(step == 0)
        def _():
            for s_ref in s_refs:
                s_ref[...] = jnp.zeros_like(s_ref)

        acc = None
        for p in range(n_pairs):
            a_ref, b_ref = refs[2 * p], refs[2 * p + 1]
            d = _dot_nt(a_ref[...], b_ref[...]) if nt else _dot(a_ref[...], b_ref[...])
            acc = d if acc is None else acc + d
        outs = epilogue(acc, *[r[...] for r in r_refs], *[v[...] for v in v_refs])
        for o_ref, o in zip(o_refs, outs[:n_out], strict=True):
            o_ref[...] = o.astype(o_ref.dtype)
        for s_ref, term in zip(s_refs, outs[n_out:], strict=True):
            s_ref[...] += jnp.sum(term, axis=0, keepdims=True)
        if n_scatter:
            pl.when(step == n_steps - 1)(scatter_finish)

    in_specs, args = [], []
    for a, b in pairs:
        in_specs += [_row_spec(tm, a.shape[1]), pl.BlockSpec(b.shape, lambda i: (0, 0))]
        args += [a, b]
    outs = pl.pallas_call(
        body, name=name, grid=(n_steps,),
        in_specs=in_specs + [_row_spec(tm, N)] * n_rows + [_vec_spec(N)] * n_vecs + [HBM] * n_scatter,
        out_specs=[_row_spec(tm, N)] * n_out + [_vec_spec(N)] * n_sums + [HBM] * n_scatter,
        out_shape=[jax.ShapeDtypeStruct((M, N), dt) for dt in row_outs] + [jax.ShapeDtypeStruct((1, N), F32)] * n_sums
        + (_scattered_shapes(scatter) if n_scatter else []),
        scratch_shapes=_scatter_semaphores(n_scatter) if n_scatter else [],
        compiler_params=pltpu.CompilerParams(dimension_semantics=("arbitrary",), vmem_limit_bytes=BIG_VMEM),
    )(*args, *rows, *vecs, *scatter)
    return outs


def _loss_head(dn, h1, target, g):
    r = lax.rsqrt(jnp.mean(dn * dn, axis=-1, keepdims=True) + EPS)
    err = h1 + (dn * r) * g - target
    dy = err * (1.0 / D_MODEL)
    ddn, dg_terms = _rms_bwd(dn, g, dy)
    return dy, ddn, err * err, dg_terms


def _mixer_out_bwd(dxn2, h1, dy, o, g_pre, g_post):
    da, dg_pre_terms = _rms_bwd(h1, g_pre, dxn2)
    dh1 = dy + da
    do, dg_post_terms = _rms_bwd(o, g_post, dh1)
    return dh1, do, dg_pre_terms, dg_post_terms


def _input_norm_bwd(dxn, x, dh1, g):
    da, dg_terms = _rms_bwd(x, g, dxn)
    return dh1 + da, dg_terms


def _sgu_norm(z_tile, g, b):
    gz = _gelu(z_tile)
    u, vv = gz[:, :SGU_W], gz[:, SGU_W:]
    xc = vv - jnp.mean(vv, axis=-1, keepdims=True)
    rstd = lax.rsqrt(jnp.mean(xc * xc, axis=-1, keepdims=True) + EPS)
    xhat = xc * rstd
    return u, xhat, rstd, xhat * g + b


def _sgu_mix(w_ref, v_bf, first_half):
    parts = []
    for p in range(N_GROUPS // 2):
        vp = v_bf[:, p * LANES:(p + 1) * LANES]
        parts.append(jnp.where(first_half, _dot(w_ref[2 * p], vp), _dot(w_ref[2 * p + 1], vp)))
    return jnp.concatenate(parts, axis=1)


def _sgu_fwd(z, g_sgu, b_sgu, ws, bias_plane, *, tm=512):
    T = z.shape[0]
    tm = min(tm, T)

    def body(z_ref, g_ref, b_ref, ws_ref, bp_ref, y_ref):
        u, _, _, vn = _sgu_norm(z_ref[...], g_ref[...], b_ref[...])
        vn_bf = vn.astype(BF)
        first_half = lax.broadcasted_iota(jnp.int32, (CHUNK, LANES), 1) < HEAD_DIM
        for c in range(tm // CHUNK):
            rows = slice(c * CHUNK, (c + 1) * CHUNK)
            s = _sgu_mix(ws_ref, vn_bf[rows, :], first_half) + bp_ref[...]
            y_ref[rows, :] = (u[rows, :] * s).astype(BF)

    return pl.pallas_call(
        body, name="sgu_fwd", grid=(T // tm,),
        in_specs=[_row_spec(tm, 2 * SGU_W), _vec_spec(SGU_W), _vec_spec(SGU_W),
                  pl.BlockSpec((N_GROUPS, CHUNK, CHUNK), lambda i: (0, 0, 0)),
                  pl.BlockSpec((CHUNK, SGU_W), lambda i: (0, 0))],
        out_specs=_row_spec(tm, SGU_W), out_shape=jax.ShapeDtypeStruct((T, SGU_W), BF),
        compiler_params=_params("parallel"),
    )(z, g_sgu, b_sgu, ws, bias_plane)


def _sgu_bwd(dy, z, g_sgu, b_sgu, ws, ws_t, bias_plane, exchange, *, tm=512):
    T = z.shape[0]
    tm = min(tm, T)
    n_steps = T // tm
    k = len(exchange)

    def body(dy_ref, z_ref, g_ref, b_ref, ws_ref, wst_ref, bp_ref, *refs):
        x_refs, (dz_ref, dws_ref, dbs_ref, dg_ref, db_ref), r_refs = refs[:k], refs[k:k + 5], refs[k + 5:2 * k + 5]
        dbp_ref, send_sems, recv_sems = refs[2 * k + 5:]
        step = pl.program_id(0)
        exchange_start, exchange_finish = _exchange_phases(x_refs, r_refs, send_sems, recv_sems)
        pl.when(step == 0)(exchange_start)

        @pl.when(step == 0)
        def _():
            dws_ref[...] = jnp.zeros_like(dws_ref)
            dg_ref[...] = jnp.zeros_like(dg_ref)
            db_ref[...] = jnp.zeros_like(db_ref)
            dbp_ref[...] = jnp.zeros_like(dbp_ref)

        g = g_ref[...]
        zt = z_ref[...]
        u, xhat, rstd, vn = _sgu_norm(zt, g, b_ref[...])
        vn_bf = vn.astype(BF)
        first_half = lax.broadcasted_iota(jnp.int32, (CHUNK, LANES), 1) < HEAD_DIM
        dyv = dy_ref[...]
        dg_acc = jnp.zeros((1, SGU_W), F32)
        db_acc = jnp.zeros((1, SGU_W), F32)
        for c in range(tm // CHUNK):
            rows = slice(c * CHUNK, (c + 1) * CHUNK)
            v_c = vn_bf[rows, :]
            s = _sgu_mix(ws_ref, v_c, first_half) + bp_ref[...]
            dy_c = dyv[rows, :]
            du = dy_c * s
            dsv = dy_c * u[rows, :]
            dbp_ref[...] += dsv
            ds_bf = dsv.astype(BF)
            zero = jnp.zeros((CHUNK, LANES), BF)
            for p in range(N_GROUPS // 2):
                dsp = ds_bf[:, p * LANES:(p + 1) * LANES]
                vp = v_c[:, p * LANES:(p + 1) * LANES]
                dws_ref[2 * p] += _dot_nt(jnp.where(first_half, dsp, zero), vp)
                dws_ref[2 * p + 1] += _dot_nt(jnp.where(first_half, zero, dsp), vp)
            dvn = _sgu_mix(wst_ref, ds_bf, first_half)
            xh = xhat[rows, :]
            dxh = dvn * g
            dvv = rstd[rows, :] * (dxh - jnp.mean(dxh, axis=-1, keepdims=True)
                                   - xh * jnp.mean(dxh * xh, axis=-1, keepdims=True))
            dg_acc += jnp.sum(dvn * xh, axis=0, keepdims=True)
            db_acc += jnp.sum(dvn, axis=0, keepdims=True)
            dgz = jnp.concatenate([du, dvv], axis=1)
            dz_ref[rows, :] = (dgz * _gelu_grad(zt[rows, :])).astype(BF)
        dg_ref[...] += dg_acc
        db_ref[...] += db_acc

        @pl.when(step == n_steps - 1)
        def _():
            r = lax.broadcasted_iota(jnp.int32, (CHUNK, CHUNK), 0)
            cidx = lax.broadcasted_iota(jnp.int32, (CHUNK, CHUNK), 1)
            causal = (cidx <= r).astype(F32)
            for gi in range(N_GROUPS):
                dws_ref[gi] = dws_ref[gi] * causal
            lane = lax.broadcasted_iota(jnp.int32, (CHUNK, LANES), 1)
            out = jnp.zeros((CHUNK, LANES), F32)
            dbp = dbp_ref[...]
            for gi in range(N_GROUPS):
                col = jnp.sum(dbp[:, gi * HEAD_DIM:(gi + 1) * HEAD_DIM], axis=1, keepdims=True)
                out = jnp.where(lane == gi, col, out)
            dbs_ref[...] = out
            exchange_finish()

    w_spec = pl.BlockSpec((N_GROUPS, CHUNK, CHUNK), lambda i: (0, 0, 0))
    plane = pl.BlockSpec((CHUNK, SGU_W), lambda i: (0, 0))
    outs = pl.pallas_call(
        body, name="sgu_bwd", grid=(n_steps,),
        in_specs=[_row_spec(tm, SGU_W), _row_spec(tm, 2 * SGU_W), _vec_spec(SGU_W), _vec_spec(SGU_W), w_spec, w_spec, plane]
        + [HBM] * k,
        out_specs=[_row_spec(tm, 2 * SGU_W), w_spec, pl.BlockSpec((CHUNK, LANES), lambda i: (0, 0)),
                   _vec_spec(SGU_W), _vec_spec(SGU_W)] + [HBM] * k,
        out_shape=[jax.ShapeDtypeStruct((T, 2 * SGU_W), BF), jax.ShapeDtypeStruct((N_GROUPS, CHUNK, CHUNK), F32),
                   jax.ShapeDtypeStruct((CHUNK, LANES), F32), jax.ShapeDtypeStruct((1, SGU_W), F32),
                   jax.ShapeDtypeStruct((1, SGU_W), F32)] + _exchanged_shapes(exchange),
        scratch_shapes=[pltpu.VMEM((CHUNK, SGU_W), F32)] + _exchange_semaphores(k),
        compiler_params=_params("arbitrary"),
    )(dy, z, g_sgu, b_sgu, ws, ws_t, bias_plane, *exchange)
    return outs[:5], outs[5:]


def _tri(n, upper):
    r = lax.broadcasted_iota(jnp.int32, (n, n), 0)
    c = lax.broadcasted_iota(jnp.int32, (n, n), 1)
    return ((c >= r) if upper else (c <= r)).astype(BF)


def _scan_dot(tri, x):
    hi, mid, lo = _split3(x)
    return (_dot(tri, hi.astype(BF)) + _dot(tri, mid.astype(BF))) + _dot(tri, lo.astype(BF))


def _with_lanes(base, lane, start, cols):
    out = base
    for k, col in enumerate(cols):
        if col is not None:
            out = jnp.where(lane == start + k, col, out)
    return out


def _logit_bound(q_norm, k_norm):
    return NORM_SLACK * q_norm * k_norm + 1.0


ATTN_TILE = 512
SKIP_BELOW = -110.0
NORM_SLACK = 1.001
BOUNDED_GAP = 60.0


def _attn_prep(qkv, fl, b_forget, *, tp=ATTN_TILE):
    T = qkv.shape[0]
    tp = min(tp, T)
    head_sum, gather6, place_q, place_k, place_v = (jnp.asarray(m, BF) for m in _attn_placements())

    def body(qkv_ref, fl_ref, bf_ref, hs_ref, g6_ref, pq_ref, pk_ref, pv_ref, qf_ref, kl_ref, vl_ref, st_ref, carry_ref, kmax_ref):
        @pl.when(pl.program_id(0) == 0)
        def _():
            carry_ref[...] = jnp.zeros_like(carry_ref)
            kmax_ref[...] = jnp.zeros_like(kmax_ref)

        x = fl_ref[...] + bf_ref[...]
        logf = jnp.minimum(x, 0.0) - jnp.log(1.0 + jnp.exp(-jnp.abs(x)))
        cum = _scan_dot(_tri(tp, upper=False), logf) + carry_ref[...]
        carry_ref[...] = cum[tp - 1:tp, :]

        def head_norms(block):
            sq = block * block
            hi = sq.astype(BF)
            return _dot(hi, hs_ref[...]) + _dot((sq - hi.astype(F32)).astype(BF), hs_ref[...])

        qkvv = qkv_ref[...]
        q_norm = NORM_SLACK * jnp.sqrt(head_norms(qkvv[:, :ATTN_W].astype(F32) * Q_SCALE))
        kn = NORM_SLACK * jnp.sqrt(jnp.max(head_norms(qkvv[:, ATTN_W:2 * ATTN_W].astype(F32)), axis=0, keepdims=True))
        k_seen = jnp.maximum(kmax_ref[...], kn)
        kmax_ref[...] = k_seen
        rows = (jnp.max(q_norm, axis=0, keepdims=True), kn, jnp.max(cum, axis=0, keepdims=True),
                jnp.min(cum, axis=0, keepdims=True), k_seen)
        st_ref[...] = jnp.zeros_like(st_ref)
        for k, row in enumerate(rows):
            st_ref[0, k:k + 1, :] = row
        parts = jnp.concatenate([p.astype(BF) for p in _split3(cum) + _split3(-_logit_bound(q_norm, k_seen))], axis=1)
        lane = lax.broadcasted_iota(jnp.int32, (tp, LANES), 1)
        side = jnp.where(lane == 6 * N_HEADS, 1.0, _dot(parts, g6_ref[...])).astype(BF)
        for h in range(N_HEADS):
            pair = slice((h // 2) * LANES, (h // 2 + 1) * LANES)
            for out_ref, block, place_ref in ((qf_ref, qkvv[:, :ATTN_W], pq_ref), (kl_ref, qkvv[:, ATTN_W:2 * ATTN_W], pk_ref),
                                              (vl_ref, qkvv[:, 2 * ATTN_W:], pv_ref)):
                out_ref[h] = _dot(jnp.concatenate([block[:, pair], side], axis=1), place_ref[h]).astype(BF)

    head_spec = pl.BlockSpec((N_HEADS, tp, LANES), lambda i: (0, i, 0))
    whole = lambda a: pl.BlockSpec(a.shape, lambda i: (0,) * a.ndim)
    return pl.pallas_call(
        body, name="attn_prep", grid=(T // tp,),
        in_specs=[_row_spec(tp, 3 * ATTN_W), _row_spec(tp, LANES), _vec_spec(LANES)]
        + [whole(m) for m in (head_sum, gather6, place_q, place_k, place_v)],
        out_specs=[head_spec] * 3 + [pl.BlockSpec((1, N_HEADS, LANES), lambda i: (i, 0, 0))],
        out_shape=[jax.ShapeDtypeStruct((N_HEADS, T, LANES), BF)] * 3 + [jax.ShapeDtypeStruct((T // tp, N_HEADS, LANES), F32)],
        scratch_shapes=[pltpu.VMEM((1, LANES), F32), pltpu.VMEM((1, LANES), F32)], compiler_params=_params("arbitrary"),
    )(qkv, fl, b_forget, head_sum, gather6, place_q, place_k, place_v)


def _attn_placements():
    head_sum = np.zeros((ATTN_W, LANES), np.float32)
    head_sum[np.arange(ATTN_W), np.arange(ATTN_W) // HEAD_DIM] = 1.0
    gather6 = np.zeros((6 * LANES, LANES), np.float32)
    for j in range(6):
        gather6[j * LANES + np.arange(N_HEADS), j * N_HEADS + np.arange(N_HEADS)] = 1.0
    place = np.zeros((3, N_HEADS, 2 * LANES, LANES), np.float32)
    one = LANES + 6 * N_HEADS
    d = np.arange(HEAD_DIM)
    for h in range(N_HEADS):
        side = lambda j: LANES + j * N_HEADS + h
        place[0, h, (h % 2) * HEAD_DIM + d, d] = Q_SCALE
        place[1:, h, (h % 2) * HEAD_DIM + d, d] = 1.0
        for j in range(3):
            place[0, h, side(j), HEAD_DIM + j] = 1.0
            place[0, h, one, HEAD_DIM + 3 + j] = 1.0
            place[0, h, side(3 + j), HEAD_DIM + 6 + j] = 1.0
            place[1, h, one, HEAD_DIM + j] = 1.0
            place[1, h, side(j), HEAD_DIM + 3 + j] = -1.0
            place[1, h, one, HEAD_DIM + 6 + j] = 1.0
            place[2, h, one, HEAD_DIM + j] = 1.0
    return head_sum, gather6, place[0], place[1], place[2]


def _attn_ranges(stats):
    qn, kn, cmax, cmin, k_seen = (stats[:, k, :N_HEADS].T for k in range(5))
    n = qn.shape[1]
    bounded = (2.0 * _logit_bound(qn, k_seen) <= BOUNDED_GAP).reshape(N_HEADS // 2, 2, n).all(axis=1)
    reach = NORM_SLACK * qn * (jnp.max(kn, axis=1, keepdims=True) + kn) + cmax
    i = jnp.arange(n)[None, :, None]
    j = jnp.arange(n)[None, None, :]
    need = ((reach[:, :, None] - cmin[:, None, :] >= SKIP_BELOW) | (i == j)) & (j <= i)
    first = jnp.min(jnp.where(need, j, n), axis=2).reshape(N_HEADS // 2, 2, n).min(axis=1)
    last = jnp.max(jnp.where(need, i, -1), axis=1).reshape(N_HEADS // 2, 2, n).max(axis=1)
    return first.reshape(-1).astype(F32), last.reshape(-1).astype(F32), bounded.reshape(-1).astype(F32)


def _pair_block(t):
    return pl.BlockSpec((2, t, LANES), lambda p, i, *_: (p, i, 0))


def _pair_full(T):
    return pl.BlockSpec((2, T, LANES), lambda p, i, *_: (p, 0, 0))


def _packed_block(t):
    return pl.BlockSpec((t, LANES), lambda p, i, *_: (i, p))


def _causal(t, keys_in_rows=False):
    r = lax.broadcasted_iota(jnp.int32, (t, t), 0)
    c = lax.broadcasted_iota(jnp.int32, (t, t), 1)
    return (r <= c) if keys_in_rows else (c <= r)


def _tile_rows(j, t):
    return pl.ds(pl.multiple_of(j * t, t), t)


def _attn_call(body, name, tile_scalars, operands, in_specs, out_specs, out_shape, scratch_shapes, n_tiles):
    return pl.pallas_call(
        body, name=name,
        grid_spec=pltpu.PrefetchScalarGridSpec(
            num_scalar_prefetch=len(tile_scalars), grid=(N_HEADS // 2, n_tiles), in_specs=in_specs, out_specs=out_specs,
            scratch_shapes=scratch_shapes),
        out_shape=out_shape, compiler_params=_params("arbitrary", "arbitrary"),
    )(*tile_scalars, *operands)


def _attn_fwd(qf, kl, vl, first, bounded, shards, *, tq=ATTN_TILE):
    T = qf.shape[1]
    tq = min(tq, T)
    n = T // tq
    n_steps = (N_HEADS // 2) * n
    k = len(shards)

    def body(first_ref, bounded_ref, qf_ref, kl_ref, vl_ref, *refs):
        w_refs, (o_ref, of_ref, ql_ref), g_refs = refs[:k], refs[k:k + 3], refs[k + 3:2 * k + 3]
        m_ref, acc_ref, send_sems, recv_sems = refs[2 * k + 3:]
        i = pl.program_id(1)
        tile = pl.program_id(0) * n + i
        gather_start, gather_forward, gather_finish = _gather_phases(w_refs, g_refs, send_sems, recv_sems)
        pl.when(tile == 0)(gather_start)
        pl.when(tile == (3 * n_steps) // 4)(gather_forward)
        start = first_ref[tile].astype(jnp.int32)
        is_bounded = bounded_ref[tile] > 0.5
        acc_ref[...] = jnp.zeros_like(acc_ref)
        diagonal = _tile_rows(i, tq)
        causal = _causal(tq)

        def logits(hh, rows):
            return _dot_nt(qf_ref[hh], kl_ref[hh, rows, :])

        @pl.when(is_bounded)
        def _():
            m_ref[...] = jnp.zeros_like(m_ref)

            def update(hh, s, rows):
                acc_ref[hh] += _dot(jnp.exp(s).astype(BF), vl_ref[hh, rows, :])

            def step(j, carry):
                for hh in range(2):
                    update(hh, logits(hh, _tile_rows(j, tq)), _tile_rows(j, tq))
                return carry

            lax.fori_loop(start, i, step, 0)
            for hh in range(2):
                update(hh, jnp.where(causal, logits(hh, diagonal), NEG), diagonal)

        @pl.when(jnp.logical_not(is_bounded))
        def _():
            m_ref[...] = jnp.full_like(m_ref, NEG)

            def update(hh, s, rows):
                m_old = m_ref[hh]
                m_new = jnp.maximum(m_old, jnp.max(s, axis=1, keepdims=True))
                p = jnp.exp(s - m_new)
                acc_ref[hh] = jnp.exp(m_old - m_new) * acc_ref[hh] + _dot(p.astype(BF), vl_ref[hh, rows, :])
                m_ref[hh] = m_new

            def step(j, carry):
                for hh in range(2):
                    update(hh, logits(hh, _tile_rows(j, tq)), _tile_rows(j, tq))
                return carry

            lax.fori_loop(start, i, step, 0)
            for hh in range(2):
                update(hh, jnp.where(causal, logits(hh, diagonal), NEG), diagonal)

        lane = lax.broadcasted_iota(jnp.int32, (tq, LANES), 1)
        outs = []
        for hh in range(2):
            q = qf_ref[hh].astype(F32)
            acc = acc_ref[hh]
            l = acc[:, HEAD_DIM:HEAD_DIM + 1]
            outs.append(acc[:, :HEAD_DIM] / l)
            at = HEAD_DIM + 6
            neg_bound = (q[:, at:at + 1] + q[:, at + 1:at + 2]) + q[:, at + 2:at + 3]
            ql_ref[hh] = _with_lanes(q, lane, at, _split3(neg_bound - (m_ref[hh] + jnp.log(l)))).astype(BF)
        o = jnp.concatenate(outs, axis=1)
        o_ref[...] = o.astype(BF)
        of_ref[...] = o
        pl.when(tile == n_steps - 1)(gather_finish)

    outs = _attn_call(
        body, "attn_fwd", (first, bounded), (qf, kl, vl, *shards),
        [_pair_block(tq), _pair_full(T), _pair_full(T)] + [HBM] * k,
        [_packed_block(tq), _packed_block(tq), _pair_block(tq)] + [HBM] * k,
        [jax.ShapeDtypeStruct((T, ATTN_W), BF), jax.ShapeDtypeStruct((T, ATTN_W), F32),
         jax.ShapeDtypeStruct((N_HEADS, T, LANES), BF)] + _gathered_shapes(shards),
        [pltpu.VMEM((2, tq, 1), F32), pltpu.VMEM((2, tq, LANES), F32)] + _gather_semaphores(k), n)
    return outs[0], outs[1], outs[2], outs[3:]


def _attn_bwd_prep(dya, of, *, tr=ATTN_TILE):
    T = dya.shape[0]
    tr = min(tr, T)
    head_sum, gather6 = (jnp.asarray(m, BF) for m in _attn_placements()[:2])
    gather3, place_do = gather6[:3 * LANES], _delta_placement()

    def body(d_ref, o_ref, hs_ref, g3_ref, p_ref, do_ref):
        dv = d_ref[...]
        delta = sum(_dot(part.astype(BF), hs_ref[...]) for part in _split3(dv * o_ref[...]))
        side = _dot(jnp.concatenate([p.astype(BF) for p in _split3(-delta)], axis=1), g3_ref[...]).astype(BF)
        d_bf = dv.astype(BF)
        for h in range(N_HEADS):
            pair = slice((h // 2) * LANES, (h // 2 + 1) * LANES)
            do_ref[h] = _dot(jnp.concatenate([d_bf[:, pair], side], axis=1), p_ref[h]).astype(BF)

    whole = lambda a: pl.BlockSpec(a.shape, lambda i: (0,) * a.ndim)
    return pl.pallas_call(
        body, name="attn_bwd_prep", grid=(T // tr,),
        in_specs=[_row_spec(tr, ATTN_W), _row_spec(tr, ATTN_W), whole(head_sum), whole(gather3), whole(place_do)],
        out_specs=pl.BlockSpec((N_HEADS, tr, LANES), lambda i: (0, i, 0)),
        out_shape=jax.ShapeDtypeStruct((N_HEADS, T, LANES), BF), compiler_params=_params("parallel"),
    )(dya, of, head_sum, gather3, place_do)


def _delta_placement():
    place = np.zeros((N_HEADS, 2 * LANES, LANES), np.float32)
    d = np.arange(HEAD_DIM)
    for h in range(N_HEADS):
        place[h, (h % 2) * HEAD_DIM + d, d] = 1.0
        for j in range(3):
            place[h, LANES + j * N_HEADS + h, HEAD_DIM + j] = 1.0
    return jnp.asarray(place, BF)


def _attn_bwd(kl, vl, ql, do, last, chip_sums, *, tk=ATTN_TILE):
    T = ql.shape[1]
    tk = min(tk, T)
    n = T // tk
    n_steps = (N_HEADS // 2) * n
    m = len(chip_sums)

    def body(last_ref, kl_ref, vl_ref, ql_ref, do_ref, *refs):
        b_refs, (dq_ref, dk_ref, dv_ref, extq_ref, extk_ref), r_refs = refs[:m], refs[m:m + 5], refs[m + 5:2 * m + 5]
        dq_acc, dk_acc, dv_acc, send_sems, recv_sems = refs[2 * m + 5:]
        j = pl.program_id(1)
        tile = pl.program_id(0) * n + j
        scatter_start, scatter_finish = _scatter_phases(b_refs, r_refs, send_sems, recv_sems)
        pl.when(tile == 0)(scatter_start)

        @pl.when(j == 0)
        def _():
            dq_acc[...] = jnp.zeros_like(dq_acc)

        dk_acc[...] = jnp.zeros_like(dk_acc)
        dv_acc[...] = jnp.zeros_like(dv_acc)

        def block(hh, rows, mask):
            qi, di, k = ql_ref[hh, rows, :], do_ref[hh, rows, :], kl_ref[hh]
            p_t = jnp.exp(_dot_nt(k, qi))
            if mask is not None:
                p_t = jnp.where(mask, p_t, 0.0)
            ds_t = (p_t * _dot_nt(vl_ref[hh], di)).astype(BF)
            dk_acc[hh] += _dot(ds_t, qi)
            dv_acc[hh] += _dot(p_t.astype(BF), di)
            dq_acc[hh, rows, :] += _dot_tn(ds_t, k)

        causal_t = _causal(tk, keys_in_rows=True)
        for hh in range(2):
            block(hh, _tile_rows(j, tk), causal_t)

        def step(i, carry):
            for hh in range(2):
                block(hh, _tile_rows(i, tk), None)
            return carry

        lax.fori_loop(j + 1, last_ref[pl.program_id(0) * n + j].astype(jnp.int32) + 1, step, 0)
        dk_ref[...] = jnp.concatenate([dk_acc[hh][:, :HEAD_DIM] for hh in range(2)], axis=1).astype(BF)
        dv_ref[...] = jnp.concatenate([dv_acc[hh][:, :HEAD_DIM] for hh in range(2)], axis=1).astype(BF)
        extk_ref[...] = jnp.concatenate([dk_acc[hh][:, HEAD_DIM:] for hh in range(2)], axis=1)

        @pl.when(j == n - 1)
        def _():
            dq_ref[...] = jnp.concatenate([dq_acc[hh][:, :HEAD_DIM] * Q_SCALE for hh in range(2)], axis=1).astype(BF)
            extq_ref[...] = jnp.concatenate([dq_acc[hh][:, HEAD_DIM:] for hh in range(2)], axis=1)

        pl.when(tile == n_steps - 1)(scatter_finish)

    whole = pl.BlockSpec((T, LANES), lambda p, j, *_: (0, p))
    outs = pl.pallas_call(
        body, name="attn_bwd",
        grid_spec=pltpu.PrefetchScalarGridSpec(
            num_scalar_prefetch=1, grid=(N_HEADS // 2, n),
            in_specs=[_pair_block(tk), _pair_block(tk), _pair_full(T), _pair_full(T)] + [HBM] * m,
            out_specs=[whole, _packed_block(tk), _packed_block(tk), whole, _packed_block(tk)] + [HBM] * m,
            scratch_shapes=[pltpu.VMEM((2, T, LANES), F32), pltpu.VMEM((2, tk, LANES), F32), pltpu.VMEM((2, tk, LANES), F32)]
            + _scatter_semaphores(m)),
        out_shape=[jax.ShapeDtypeStruct((T, ATTN_W), BF)] * 3 + [jax.ShapeDtypeStruct((T, ATTN_W), F32)] * 2
        + _scattered_shapes(chip_sums),
        compiler_params=pltpu.CompilerParams(dimension_semantics=("arbitrary", "arbitrary"), vmem_limit_bytes=BIG_VMEM),
    )(last, kl, vl, ql, do, *chip_sums)
    return outs[:5], outs[5:]


def _forget_bwd(ext_q, ext_k, fl, b_forget, *, tp=256):
    T = fl.shape[0]
    tp = min(tp, T)
    n = T // tp

    def body(eq_ref, ek_ref, fl_ref, bf_ref, dfl_ref, dbf_ref, carry_ref):
        @pl.when(pl.program_id(0) == 0)
        def _():
            carry_ref[...] = jnp.zeros_like(carry_ref)
            dbf_ref[...] = jnp.zeros_like(dbf_ref)

        lane = lax.broadcasted_iota(jnp.int32, (tp, LANES), 1)
        eq, ek = eq_ref[...], ek_ref[...]
        cols = [eq[:, h * HEAD_DIM:h * HEAD_DIM + 1] - ek[:, h * HEAD_DIM + 3:h * HEAD_DIM + 4] for h in range(N_HEADS)]
        dcum = _with_lanes(jnp.zeros((tp, LANES), F32), lane, 0, cols)
        suffix = _scan_dot(_tri(tp, upper=True), dcum) + carry_ref[...]
        carry_ref[...] = suffix[0:1, :]
        x = fl_ref[...] + bf_ref[...]
        dfl = jnp.where(lane < N_HEADS, suffix / (1.0 + jnp.exp(x)), 0.0)
        dfl_ref[...] = dfl.astype(BF)
        dbf_ref[...] += jnp.sum(dfl, axis=0, keepdims=True)

    rev = lambda w: pl.BlockSpec((tp, w), lambda i: (n - 1 - i, 0))
    return pl.pallas_call(
        body, name="forget_bwd", grid=(n,),
        in_specs=[rev(ATTN_W), rev(ATTN_W), rev(LANES), _vec_spec(LANES)],
        out_specs=[rev(LANES), _vec_spec(LANES)],
        out_shape=[jax.ShapeDtypeStruct((T, LANES), BF), jax.ShapeDtypeStruct((1, LANES), F32)],
        scratch_shapes=[pltpu.VMEM((1, LANES), F32)], compiler_params=_params("arbitrary"),
    )(ext_q, ext_k, fl, b_forget)


def _adamw(w, g, m, v, *, name, tr=256):
    _, rows, cols = w.shape
    tr = tr if rows % tr == 0 else rows

    def body(w_ref, g_ref, m_ref, v_ref, go_ref, d_ref, nm_ref, nv_ref):
        gv = g_ref[...]
        go_ref[...] = gv
        nm = ADAM_B1 * m_ref[...] + (1.0 - ADAM_B1) * gv
        nv = ADAM_B2 * v_ref[...] + (1.0 - ADAM_B2) * (gv * gv)
        m_hat = nm / (1.0 - ADAM_B1 ** ADAM_STEP)
        v_hat = nv / (1.0 - ADAM_B2 ** ADAM_STEP)
        d_ref[...] = -ADAM_LR * (m_hat / (jnp.sqrt(v_hat) + ADAM_EPS) + ADAM_WD * w_ref[...])
        nm_ref[...] = nm
        nv_ref[...] = nv

    spec = pl.BlockSpec((None, tr, cols), lambda i: (0, i, 0))
    return pl.pallas_call(
        body, name=name, grid=(rows // tr,), in_specs=[spec, pl.BlockSpec((tr, cols), lambda i: (i, 0)), spec, spec],
        out_specs=[spec] * 4, out_shape=[jax.ShapeDtypeStruct((1, rows, cols), F32)] * 4,
        compiler_params=_params("parallel"),
    )(w, g, m, v)


HBM = pl.BlockSpec(memory_space=pltpu.HBM)
BF16_ROWS = 16


def _place():
    x, y, c = lax.axis_index("x"), lax.axis_index("y"), lax.axis_index("c")
    others = [(1 - x, y), (x, 1 - y), (1 - x, 1 - y)]
    return x, y, c, others


def _chip(xy):
    return 2 * xy[0] + xy[1]


def _row_halves(c, rows):
    half = rows // 2
    assert half % BF16_ROWS == 0
    return (pl.ds(pl.multiple_of(c * half, BF16_ROWS), half), pl.ds(pl.multiple_of((1 - c) * half, BF16_ROWS), half))


def _remote(src, dst, send_sems, recv_sems, k, to):
    return pltpu.make_async_remote_copy(src_ref=src, dst_ref=dst, send_sem=send_sems.at[k], recv_sem=recv_sems.at[k],
                                        device_id=to, device_id_type=MESH)


def _gathered_shapes(shards):
    return [jax.ShapeDtypeStruct((N_CHIPS,) + s.shape, s.dtype) for s in shards]


def _gather_semaphores(n):
    return [pltpu.SemaphoreType.DMA((6 * n,)), pltpu.SemaphoreType.DMA((6 * n,))]


def _gather_phases(w_refs, g_refs, send_sems, recv_sems):
    n = len(w_refs)
    x, y, c, others = _place()
    sibling, me = (x, y, 1 - c), _chip((x, y))
    halves = [_row_halves(c, w.shape[0]) for w in w_refs]

    def sent(a, j, o):
        mine, _ = halves[a]
        return _remote(w_refs[a].at[mine, :], g_refs[a].at[me, mine, :], send_sems, recv_sems, 6 * a + j, (*o, c))

    def passed(a, j, o):
        landed = g_refs[a].at[_chip(o), halves[a][0], :]
        return _remote(landed, landed, send_sems, recv_sems, 6 * a + 3 + j, sibling)

    def start():
        for a in range(n):
            for j, o in enumerate(others):
                sent(a, j, o).start()

    def forward():
        for j, o in enumerate(others):
            for a in range(n):
                landed = g_refs[a].at[_chip(o), halves[a][0], :]
                _remote(landed, landed, send_sems, recv_sems, 6 * a + j, (*o, c)).wait_recv()
                passed(a, j, o).start()

    def finish():
        for j, o in enumerate(others):
            for a in range(n):
                landed = g_refs[a].at[_chip(o), halves[a][1], :]
                _remote(landed, landed, send_sems, recv_sems, 6 * a + 3 + j, sibling).wait_recv()
        for a in range(n):
            for j, o in enumerate(others):
                sent(a, j, o).wait_send()
                passed(a, j, o).wait_send()

    return start, forward, finish


def _exchange_halves(arrays, *, name):
    n = len(arrays)

    def body(*refs):
        for phase in _exchange_phases(refs[:n], refs[n:2 * n], *refs[2 * n:]):
            phase()

    return pl.pallas_call(
        body, name=name, in_specs=[HBM] * n, out_specs=[HBM] * n, out_shape=_exchanged_shapes(arrays),
        scratch_shapes=_exchange_semaphores(n),
    )(*arrays)


def _exchanged_shapes(arrays):
    return [jax.ShapeDtypeStruct(s.shape[:-2] + (s.shape[-2] // 2, s.shape[-1]), F32) for s in arrays]


def _exchange_semaphores(n):
    return [pltpu.SemaphoreType.DMA((n,)), pltpu.SemaphoreType.DMA((n,))]


def _exchange_phases(g_refs, r_refs, send_sems, recv_sems):
    x, y, c, _ = _place()

    def copy(a):
        _, theirs = _row_halves(c, g_refs[a].shape[-2])
        src = g_refs[a].at[:, theirs, :] if len(g_refs[a].shape) == 3 else g_refs[a].at[theirs, :]
        return _remote(src, r_refs[a], send_sems, recv_sems, a, (x, y, 1 - c))

    def start():
        for a in range(len(g_refs)):
            copy(a).start()

    def finish():
        for a in range(len(g_refs)):
            copy(a).wait()

    return start, finish


def _scatter_to_owners(chip_sums):
    n = len(chip_sums)

    def body(*refs):
        for phase in _scatter_phases(refs[:n], refs[n:2 * n], *refs[2 * n:]):
            phase()

    return pl.pallas_call(
        body, name="scatter_to_owners", in_specs=[HBM] * n, out_specs=[HBM] * n,
        out_shape=_scattered_shapes(chip_sums), scratch_shapes=_scatter_semaphores(n),
    )(*chip_sums)


def _scattered_shapes(chip_sums):
    return [jax.ShapeDtypeStruct(b.shape if b.ndim == 3 else (N_CHIPS,) + b.shape, b.dtype) for b in chip_sums]


def _scatter_semaphores(n):
    return [pltpu.SemaphoreType.DMA((3 * n,)), pltpu.SemaphoreType.DMA((3 * n,))]


def _scatter_phases(b_refs, r_refs, send_sems, recv_sems):
    n = len(b_refs)
    x, y, c, others = _place()
    me = _chip((x, y))

    def sent(a, j, o):
        src = b_refs[a].at[_chip(o)] if len(b_refs[a].shape) == 3 else b_refs[a]
        return _remote(src, r_refs[a].at[me], send_sems, recv_sems, 3 * a + j, (*o, c))

    def start():
        for a in range(n):
            for j, o in enumerate(others):
                sent(a, j, o).start()

    def finish():
        for a in range(n):
            for j, o in enumerate(others):
                landed = r_refs[a].at[_chip(o)]
                _remote(landed, landed, send_sems, recv_sems, 3 * a + j, (*o, c)).wait_recv()
        for a in range(n):
            for j, o in enumerate(others):
                sent(a, j, o).wait_send()

    return start, finish


def _join_halves(totals):
    n = len(totals)

    def body(*refs):
        in_refs, out_refs, (send_sems, recv_sems) = refs[:n], refs[n:2 * n], refs[2 * n:]
        x, y, c, _ = _place()
        copies = []
        for a in range(n):
            mine, _ = _row_halves(c, in_refs[a].shape[0])
            copies.append(_remote(in_refs[a].at[mine, :], out_refs[a].at[mine, :], send_sems, recv_sems, a, (x, y, 1 - c)))
            copies[-1].start()
        for cp in copies:
            cp.wait()

    return pl.pallas_call(
        body, name="join_halves", in_specs=[HBM] * n, out_specs=[HBM] * n,
        out_shape=[jax.ShapeDtypeStruct(t.shape, F32) for t in totals], input_output_aliases={a: a for a in range(n)},
        scratch_shapes=[pltpu.SemaphoreType.DMA((n,)), pltpu.SemaphoreType.DMA((n,))],
    )(*totals)


ADD_ROWS = 128


def _add_sibling(g, r, place, *, name):
    lead, (half, cols) = g.shape[:-2], r.shape[-2:]
    tr = min(ADD_ROWS, half)
    nb = half // tr
    zeros = (0,) * len(lead)

    def body(place_ref, g_ref, r_ref, o_ref, ob_ref):
        s = g_ref[...] + r_ref[...]
        o_ref[...] = s
        ob_ref[...] = s.astype(BF)

    spec = pl.BlockSpec(lead + (tr, cols), lambda i, p: zeros + (i, 0))
    return pl.pallas_call(
        body, name=name,
        grid_spec=pltpu.PrefetchScalarGridSpec(
            num_scalar_prefetch=1, grid=(nb,),
            in_specs=[pl.BlockSpec(lead + (tr, cols), lambda i, p: zeros + (p[1] * nb + i, 0)), spec], out_specs=[spec, spec]),
        out_shape=[jax.ShapeDtypeStruct(r.shape, F32), jax.ShapeDtypeStruct(r.shape, BF)],
        compiler_params=_params("parallel"),
    )(place, g, r)


def _add_chips(own, received, place, *, name, own_slots):
    half, cols = received.shape[-2:]
    tr = min(ADD_ROWS, half)
    nb = half // tr

    def written(k, p):
        return jnp.where(p[0] == k, (k + 1) % N_CHIPS, k)

    def body(place_ref, own_ref, *refs):
        o_ref = refs[N_CHIPS]
        mine = own_ref[0] if own_slots else own_ref[...]
        if own_slots:
            acc = mine
            for k in range(N_CHIPS):
                acc = acc + jnp.where(place_ref[0] == k, 0.0, refs[k][0].astype(F32))
        else:
            terms = [jnp.where(place_ref[0] == k, mine, refs[k][0]) for k in range(N_CHIPS)]
            acc = ((terms[0] + terms[1]) + terms[2]) + terms[3]
        o_ref[...] = acc

    own_spec = (pl.BlockSpec((1, tr, cols), lambda i, p: (p[0], i, 0)) if own_slots
                else pl.BlockSpec((tr, cols), lambda i, p: (i, 0)))
    return pl.pallas_call(
        body, name=name,
        grid_spec=pltpu.PrefetchScalarGridSpec(
            num_scalar_prefetch=1, grid=(nb,),
            in_specs=[own_spec] + [pl.BlockSpec((1, tr, cols), functools.partial(lambda i, p, k: (written(k, p), i, 0), k=k))
                                   for k in range(N_CHIPS)],
            out_specs=pl.BlockSpec((tr, cols), lambda i, p: (p[1] * nb + i, 0))),
        out_shape=jax.ShapeDtypeStruct((2 * half, cols), F32), compiler_params=_params("parallel"),
    )(place, own, *([received] * N_CHIPS))


SHARDED = (("w_in", (D_MODEL, 4616), 1), ("w_branch_sgu", (SGU_W, D_MODEL), 1), ("w_branch_attn", (ATTN_W, D_MODEL), 1),
           ("w_out", (D_MODEL, D_MODEL), 0), ("w_up", (D_MODEL, D_FF), 1), ("w_down", (D_FF, D_MODEL), 0))
SMALL = (("g_mix_pre", (1, D_MODEL)), ("b_forget", (1, N_HEADS)), ("g_sgu", (1, SGU_W)), ("b_sgu", (1, SGU_W)),
         ("w_spatial", (N_GROUPS * CHUNK, CHUNK)), ("b_spatial", (N_GROUPS, CHUNK)), ("g_mix_post", (1, D_MODEL)),
         ("g_ffn_pre", (1, D_MODEL)), ("g_ffn_post", (1, D_MODEL)))
SMALL_ALIGN = 2 * ADD_ROWS


def _shard_shape(shape, axis):
    return tuple(s // N_CHIPS if a == axis else s for a, s in enumerate(shape))


def _slots_to_full(slots, axis):
    return slots.reshape(-1, slots.shape[2]) if axis == 0 else slots.transpose(1, 0, 2).reshape(slots.shape[1], -1)


def _full_to_slots(full, axis):
    if axis == 0:
        return full.reshape(N_CHIPS, -1, full.shape[1])
    return full.reshape(full.shape[0], N_CHIPS, -1).transpose(1, 0, 2)


def _small_rows(shape):
    return -(-(shape[0] * shape[1]) // (8 * LANES)) * 8


def _pack_small(values):
    parts = []
    for name, shape in SMALL:
        flat = values[name].reshape(-1)
        n = _small_rows(shape)
        parts.append(jnp.pad(flat, (0, n * LANES - flat.shape[0])).reshape(n, LANES))
    rows = sum(p.shape[0] for p in parts)
    pad = -(-rows // SMALL_ALIGN) * SMALL_ALIGN - rows
    return jnp.concatenate(parts + [jnp.zeros((pad, LANES), F32)], axis=0)


def _unpack_small(packed):
    out, row = {}, 0
    for name, shape in SMALL:
        n = _small_rows(shape)
        out[name] = packed[row:row + n].reshape(-1)[:shape[0] * shape[1]].reshape(shape)
        row += n
    return out


IN_Z, IN_Q, IN_K, IN_V, IN_F, IN_G, IN_END = 0, 1024, 1536, 2048, 2560, 2568, 4616


LATE_WEIGHTS = ("w_branch_sgu", "w_branch_attn", "w_out", "w_up", "w_down")
EARLY_GRADS = LATE_WEIGHTS


def _with_own_slot(shard, gathered, chip):
    return jnp.where(jnp.arange(N_CHIPS)[:, None, None] == chip, shard[None], gathered)


def _assemble(name, shard, gathered, chip):
    axis = {n: a for n, _, a in SHARDED}[name]
    return _slots_to_full(_with_own_slot(shard, gathered, chip), axis)


def _columns_from_slots(slots, bounds):
    width = slots.shape[2]
    pieces = []
    for lo, hi in zip(bounds[:-1], bounds[1:], strict=True):
        parts = [slots[k][:, max(lo, k * width) - k * width:min(hi, (k + 1) * width) - k * width]
                 for k in range(N_CHIPS) if max(lo, k * width) < min(hi, (k + 1) * width)]
        pieces.append(parts[0] if len(parts) == 1 else jnp.concatenate(parts, axis=1))
    return pieces


def _columns_to_slots(pieces):
    width = sum(p.shape[1] for p in pieces) // N_CHIPS
    slots = []
    for k in range(N_CHIPS):
        parts, start = [], 0
        for p in pieces:
            lo, hi = max(k * width, start), min((k + 1) * width, start + p.shape[1])
            if lo < hi:
                parts.append(p[:, lo - start:hi - start])
            start += p.shape[1]
        slots.append(jnp.concatenate(parts, axis=1))
    return jnp.stack(slots)


def _local_step(x, target, shards, small, place):
    b_forget = jnp.pad(small["b_forget"], ((0, 0), (0, LANES - N_HEADS)))
    causal = jnp.tril(jnp.ones((CHUNK, CHUNK), bool))
    ws = jnp.where(causal[None], small["w_spatial"].reshape(N_GROUPS, CHUNK, CHUNK), 0.0).astype(BF)
    ws_t = ws.transpose(0, 2, 1)
    bias_plane = jnp.repeat(small["b_spatial"].T, HEAD_DIM, axis=1)

    xn, (w_in_slots,) = _rms_fwd(x, small["g_mix_pre"], [shards["w_in"]])
    w_z, w_q, w_k, w_v, w_f, w_ga, w_gb = _columns_from_slots(
        _with_own_slot(shards["w_in"], w_in_slots, place[0]), (IN_Z, IN_Q, IN_K, IN_V, IN_F, IN_G, IN_G + D_MODEL, IN_END))
    w_qkv, w_g = jnp.concatenate([w_q, w_k, w_v], axis=1), jnp.concatenate([w_ga, w_gb], axis=1)
    w_f = jnp.pad(w_f, ((0, 0), (0, LANES - N_HEADS)))
    z, qkv, gl, fl = _project(xn, [w_z, w_qkv, w_g, w_f], [F32, BF, BF, F32], name="proj_in")
    ysgu = _sgu_fwd(z, small["g_sgu"], small["b_sgu"], ws, bias_plane)
    qf, kl, vl, tile_stats = _attn_prep(qkv, fl, b_forget)
    first_key_tile, last_query_tile, bounded = _attn_ranges(tile_stats)
    yattn, yattn_f, ql, gathered = _attn_fwd(qf, kl, vl, first_key_tile, bounded, [shards[name] for name in LATE_WEIGHTS])
    w = {name: _assemble(name, shards[name], got, place[0]) for name, got in zip(LATE_WEIGHTS, gathered, strict=True)}
    a, b, merged = _branch_merge(ysgu, yattn, w["w_branch_sgu"], w["w_branch_attn"], gl)
    o, h1, xn2 = _matmul_rows(
        [(merged, w["w_out"])], nt=False, rows=[x], vecs=[small["g_mix_post"], small["g_ffn_pre"]], row_outs=[F32, F32, BF],
        n_sums=0, epilogue=_mixer_out_fwd, name="proj_out_norms")

    def relu2(acc):
        r = jnp.maximum(acc, 0.0)
        return (r * r,)

    hid = _matmul([(xn2, w["w_up"])], nt=False, out_dtypes=[BF], name="ffn_up", epilogue=relu2, tm=FFN_ROWS)
    dy, ddn, sq, dg_ffn_post = _matmul_rows(
        [(hid, w["w_down"])], nt=False, rows=[h1, target], vecs=[small["g_ffn_post"]], row_outs=[F32, BF], n_sums=2,
        epilogue=_loss_head, name="ffn_down_loss")

    dup = _matmul([(ddn, w["w_down"])], nt=True, out_dtypes=[BF], name="ffn_down_bwd", tm=FFN_ROWS,
                  epilogue=lambda acc, h: (acc * (2.0 * jnp.sqrt(h.astype(F32))),), extras=[hid])
    dw_down = _matmul_tn(hid, ddn, name="dw_down")
    dh1, do, dg_ffn_pre, dg_mix_post = _matmul_rows(
        [(dup, w["w_up"])], nt=True, rows=[h1, dy, o], vecs=[small["g_ffn_pre"], small["g_mix_post"]], row_outs=[F32, BF],
        n_sums=2, epilogue=_mixer_out_bwd, name="ffn_up_bwd_norms")
    dw_up = _matmul_tn(xn2, dup, name="dw_up", slots=True)

    def gate_bwd(dm, a_t, b_t, gla, glb):
        ga, gb = jax.nn.sigmoid(gla.astype(F32)), jax.nn.sigmoid(glb.astype(F32))
        return dm * ga, dm * gb, dm * a_t.astype(F32) * (ga * (1.0 - ga)), dm * b_t.astype(F32) * (gb * (1.0 - gb))

    da, db, dgla, dglb = _matmul([(do, w["w_out"])], nt=True, out_dtypes=[BF] * 4, name="proj_out_bwd",
                                 epilogue=gate_bwd, extras=[a, b, (gl, 0), (gl, D_MODEL)])
    dw_out = _matmul_tn(merged, do, name="dw_out")
    dysgu = _matmul([(da, w["w_branch_sgu"])], nt=True, out_dtypes=[F32], name="branch_sgu_bwd")
    dyattn = _matmul([(db, w["w_branch_attn"])], nt=True, out_dtypes=[F32], name="branch_attn_bwd")
    dw_bs = _matmul_tn(ysgu, da, name="dw_branch_sgu")
    dw_ba = _matmul_tn(yattn, db, name="dw_branch_attn")
    early = {"w_branch_sgu": _full_to_slots(dw_bs, 1), "w_branch_attn": _full_to_slots(dw_ba, 1),
             "w_out": _full_to_slots(dw_out, 0), "w_up": dw_up, "w_down": _full_to_slots(dw_down, 0)}
    (dz, dws, dbs, dg_sgu, db_sgu), early_theirs = _sgu_bwd(
        dysgu, z, small["g_sgu"], small["b_sgu"], ws, ws_t, bias_plane, [early[name] for name in EARLY_GRADS])
    early_sums = {name: _add_sibling(early[name], theirs, place, name="add_sibling_" + name)
                  for name, theirs in zip(EARLY_GRADS, early_theirs, strict=True)}
    dout = _attn_bwd_prep(dyattn, yattn_f)
    (dq, dk, dv, ext_q, ext_k), early_received = _attn_bwd(
        kl, vl, ql, dout, last_query_tile, [early_sums[name][1] for name in EARLY_GRADS])
    dfl, dbf = _forget_bwd(ext_q, ext_k, fl, b_forget)
    dw_z, dw_q, dw_k, dw_v, dw_f = _matmul_tn_multi(xn, [dz, dq, dk, dv, dfl], name="dw_in_mix")
    dw_ga, dw_gb = _matmul_tn_multi(xn, [dgla, dglb], name="dw_in_gates")
    dw_in = _columns_to_slots([dw_z, dw_q, dw_k, dw_v, dw_f[:, :N_HEADS], dw_ga, dw_gb])
    (dw_in_theirs,) = _exchange_halves([dw_in], name="exchange_halves_w_in")
    dw_in_sum = _add_sibling(dw_in, dw_in_theirs, place, name="add_sibling_w_in")
    dx, dg_mix_pre, dw_in_received = _matmul_rows(
        [(dz, w_z), (dq, w_q), (dk, w_k), (dv, w_v), (dgla, w_ga), (dglb, w_gb), (dfl, w_f)],
        nt=True, rows=[x, dh1], vecs=[small["g_mix_pre"]], row_outs=[F32], n_sums=1, epilogue=_input_norm_bwd,
        name="proj_in_bwd_norm", scatter=[dw_in_sum[1]])

    reduced = {name: (early_sums[name][0], got) for name, got in zip(EARLY_GRADS, early_received, strict=True)}
    reduced["w_in"] = (dw_in_sum[0], dw_in_received)
    small_grads = {"g_mix_pre": dg_mix_pre, "b_forget": dbf[:, :N_HEADS], "g_sgu": dg_sgu, "b_sgu": db_sgu,
                   "w_spatial": dws.reshape(N_GROUPS * CHUNK, CHUNK), "b_spatial": dbs[:, :N_GROUPS].T,
                   "g_mix_post": dg_mix_post, "g_ffn_pre": dg_ffn_pre, "g_ffn_post": dg_ffn_post}
    return sq, dx, reduced, small_grads


NAMES = ("g_mix_pre", "w_in", "b_forget", "g_sgu", "b_sgu", "w_spatial", "b_spatial", "w_branch_sgu", "w_branch_attn",
         "w_out", "g_mix_post", "g_ffn_pre", "w_up", "w_down", "g_ffn_post")


def kernel(x, g_mix_pre, w_in, b_forget, g_sgu, b_sgu, w_spatial, b_spatial, w_branch_sgu, w_branch_attn, w_out, g_mix_post, g_ffn_pre, w_up, w_down, g_ffn_post, loss_target, m_g_mix_pre, m_w_in, m_b_forget, m_g_sgu, m_b_sgu, m_w_spatial, m_b_spatial, m_w_branch_sgu, m_w_branch_attn, m_w_out, m_g_mix_post, m_g_ffn_pre, m_w_up, m_w_down, m_g_ffn_post, v_g_mix_pre, v_w_in, v_b_forget, v_g_sgu, v_b_sgu, v_w_spatial, v_b_spatial, v_w_branch_sgu, v_w_branch_attn, v_w_out, v_g_mix_post, v_g_ffn_pre, v_w_up, v_w_down, v_g_ffn_post):
    weights = dict(zip(NAMES, (g_mix_pre, w_in, b_forget, g_sgu, b_sgu, w_spatial, b_spatial, w_branch_sgu, w_branch_attn,
                               w_out, g_mix_post, g_ffn_pre, w_up, w_down, g_ffn_post), strict=True))
    first = dict(zip(NAMES, (m_g_mix_pre, m_w_in, m_b_forget, m_g_sgu, m_b_sgu, m_w_spatial, m_b_spatial, m_w_branch_sgu,
                             m_w_branch_attn, m_w_out, m_g_mix_post, m_g_ffn_pre, m_w_up, m_w_down, m_g_ffn_post), strict=True))
    second = dict(zip(NAMES, (v_g_mix_pre, v_w_in, v_b_forget, v_g_sgu, v_b_sgu, v_w_spatial, v_b_spatial, v_w_branch_sgu,
                              v_w_branch_attn, v_w_out, v_g_mix_post, v_g_ffn_pre, v_w_up, v_w_down, v_g_ffn_post), strict=True))
    shard_shapes = {name: _shard_shape(shape, axis) for name, shape, axis in SHARDED}
    small_shapes = dict(SMALL)
    view = lambda name, a: a.reshape(shard_shapes.get(name) or small_shapes[name])

    place = jnp.stack([2 * lax.axis_index("x") + lax.axis_index("y"), lax.axis_index("c")]).astype(jnp.int32)

    shards = {name: view(name, weights[name]).astype(BF) for name, _, _ in SHARDED}
    small = {name: view(name, weights[name]) for name, _ in SMALL}
    sq, dx, reduced, small_grads = _local_step(x[0], loss_target[0], shards, small, place)
    loss = lax.psum(0.5 * jnp.sum(sq) / D_MODEL, ("x", "y", "c"))

    small_mine = _pack_small(small_grads)
    (small_theirs,) = _exchange_halves([small_mine], name="exchange_halves_small")
    small_sum, _ = _add_sibling(small_mine, small_theirs, place, name="add_sibling_small")
    (small_received,) = _scatter_to_owners([small_sum])
    totals = {name: _add_chips(s, r, place, name="add_chips_" + name, own_slots=True) for name, (s, r) in reduced.items()}
    small_total = _add_chips(small_sum, small_received, place, name="add_chips_small", own_slots=False)
    joined = _join_halves([totals[name] for name, _, _ in SHARDED] + [small_total])
    grad = {**{name: g for (name, _, _), g in zip(SHARDED, joined[:-1], strict=True)}, **_unpack_small(joined[-1])}

    grad_out, delta, new_m, new_v = {}, {}, {}, {}
    for name in NAMES:
        rows, cols = grad[name].shape
        as_given = lambda a: a.reshape(1, rows, cols)
        grad_out[name], delta[name], new_m[name], new_v[name] = _adamw(
            as_given(weights[name]), grad[name], as_given(first[name]), as_given(second[name]), name="adamw_" + name)

    like = lambda d: [d[name].reshape(weights[name].shape) for name in NAMES]
    return (loss, dx[None], *like(grad_out), *like(delta), *like(new_m), *like(new_v))
```

```python
import functools

import jax
import jax.numpy as jnp
import numpy as np
from jax import lax
from jax.experimental import pallas as pl
from jax.experimental.pallas import tpu as pltpu

F32 = jnp.float32
BF = jnp.bfloat16
MESH = pl.DeviceIdType.MESH

D_MODEL = 1024
N_HEADS = 8
HEAD_DIM = 64
ATTN_W = N_HEADS * HEAD_DIM
SGU_W = 512
N_GROUPS = 8
CHUNK = 128
D_FF = 4096
EPS = 1e-6
Q_SCALE = HEAD_DIM ** -0.5
N_CHIPS = 4
LANES = 128

ADAM_LR = 0.001
ADAM_B1 = 0.9
ADAM_B2 = 0.999
ADAM_EPS = 1e-08
ADAM_WD = 0.01
ADAM_STEP = 10

VMEM_LIMIT = 48 * 1024 * 1024
BIG_VMEM = 58 * 1024 * 1024
NEG = -1e30


def _params(*sem):
    return pltpu.CompilerParams(dimension_semantics=sem, vmem_limit_bytes=VMEM_LIMIT)


def _dot(a, b):
    return jnp.dot(a, b, preferred_element_type=F32)


def _dot_nt(a, b):
    return lax.dot_general(a, b, (((1,), (1,)), ((), ())), preferred_element_type=F32)


def _dot_tn(a, b):
    return lax.dot_general(a, b, (((0,), (0,)), ((), ())), preferred_element_type=F32)


def _split3(c):
    hi = c.astype(BF).astype(F32)
    r = c - hi
    mid = r.astype(BF).astype(F32)
    lo = (r - mid).astype(BF).astype(F32)
    return hi, mid, lo


def _gelu(x):
    k = 0.7978845608028654
    return 0.5 * x * (1.0 + jnp.tanh(k * (x + 0.044715 * (x * x * x))))


def _gelu_grad(x):
    k = 0.7978845608028654
    x2 = x * x
    t = jnp.tanh(k * (x + 0.044715 * (x2 * x)))
    return 0.5 * (1.0 + t) + 0.5 * x * (1.0 - t * t) * (k * (1.0 + 3.0 * 0.044715 * x2))


def _rms_bwd(a, g, dy):
    r = lax.rsqrt(jnp.mean(a * a, axis=-1, keepdims=True) + EPS)
    n = a * r
    dn = dy * g
    da = r * (dn - n * jnp.mean(dn * n, axis=-1, keepdims=True))
    return da, dy * n


MM_ROWS = 1024
MM_COLS = 512
FFN_ROWS = 256


def _matmul(pairs, *, nt, out_dtypes, name, tm=MM_ROWS, tn=MM_COLS, epilogue=None, extras=()):
    n_pairs, n_extra = len(pairs), len(extras)
    M = pairs[0][0].shape[0]
    N = pairs[0][1].shape[0] if nt else pairs[0][1].shape[1]
    tm, tn = min(tm, M), min(tn, N)
    assert M % tm == 0 and N % tn == 0

    def body(*refs):
        acc = None
        for p in range(n_pairs):
            a_ref, b_ref = refs[2 * p], refs[2 * p + 1]
            d = _dot_nt(a_ref[...], b_ref[...]) if nt else _dot(a_ref[...], b_ref[...])
            acc = d if acc is None else acc + d
        e_refs = refs[2 * n_pairs:2 * n_pairs + n_extra]
        o_refs = refs[2 * n_pairs + n_extra:]
        outs = (acc,) if epilogue is None else epilogue(acc, *[e[...] for e in e_refs])
        for o_ref, o in zip(o_refs, outs, strict=True):
            o_ref[...] = o.astype(o_ref.dtype)

    in_specs, args = [], []
    for a, b in pairs:
        K = a.shape[1]
        in_specs.append(pl.BlockSpec((tm, K), lambda i, j: (i, 0)))
        in_specs.append(pl.BlockSpec((tn, K), lambda i, j: (j, 0)) if nt else pl.BlockSpec((K, tn), lambda i, j: (0, j)))
        args += [a, b]
    for e in extras:
        e, col = e if isinstance(e, tuple) else (e, 0)
        in_specs.append(pl.BlockSpec((tm, tn), functools.partial(lambda i, j, off: (i, j + off), off=col // tn)))
        args.append(e)
    outs = pl.pallas_call(
        body, name=name, grid=(M // tm, N // tn), in_specs=in_specs,
        out_specs=[pl.BlockSpec((tm, tn), lambda i, j: (i, j)) for _ in out_dtypes],
        out_shape=[jax.ShapeDtypeStruct((M, N), dt) for dt in out_dtypes],
        compiler_params=_params("parallel", "parallel"),
    )(*args)
    return outs if len(outs) > 1 else outs[0]


def _matmul_tn_multi(a, bs, *, name, tk=1024):
    T, K1 = a.shape
    tk = min(tk, T)
    n = len(bs)

    def body(a_ref, *refs):
        @pl.when(pl.program_id(0) == 0)
        def _():
            for o_ref in refs[n:]:
                o_ref[...] = jnp.zeros_like(o_ref)

        av = a_ref[...]
        for b_ref, o_ref in zip(refs[:n], refs[n:], strict=True):
            o_ref[...] += _dot_tn(av, b_ref[...])

    return pl.pallas_call(
        body, name=name, grid=(T // tk,),
        in_specs=[pl.BlockSpec((tk, K1), lambda k: (k, 0))] + [pl.BlockSpec((tk, b.shape[1]), lambda k: (k, 0)) for b in bs],
        out_specs=[pl.BlockSpec((K1, b.shape[1]), lambda k: (0, 0)) for b in bs],
        out_shape=[jax.ShapeDtypeStruct((K1, b.shape[1]), F32) for b in bs],
        compiler_params=pltpu.CompilerParams(dimension_semantics=("arbitrary",), vmem_limit_bytes=BIG_VMEM),
    )(a, *bs)


def _project(a, weights, out_dtypes, *, name, tm=512, nt=False, epilogue=None, extra=None):
    M, K = a.shape
    tm = min(tm, M)
    n = len(weights)
    widths = [w.shape[0] if nt else w.shape[1] for w in weights]
    extras = [] if extra is None else [extra]

    def body(a_ref, *refs):
        av = a_ref[...]
        w_refs, e_refs, o_refs = refs[:n], refs[n:n + len(extras)], refs[n + len(extras):]
        for w_ref, o_ref in zip(w_refs, o_refs, strict=True):
            acc = _dot_nt(av, w_ref[...]) if nt else _dot(av, w_ref[...])
            if epilogue is not None:
                acc = epilogue(acc, *[e[...] for e in e_refs])
            o_ref[...] = acc.astype(o_ref.dtype)

    return pl.pallas_call(
        body, name=name, grid=(M // tm,),
        in_specs=[pl.BlockSpec((tm, K), lambda i: (i, 0))] + [pl.BlockSpec(w.shape, lambda i: (0, 0)) for w in weights]
        + [pl.BlockSpec((tm, e.shape[1]), lambda i: (i, 0)) for e in extras],
        out_specs=[pl.BlockSpec((tm, width), lambda i: (i, 0)) for width in widths],
        out_shape=[jax.ShapeDtypeStruct((M, width), dt) for width, dt in zip(widths, out_dtypes, strict=True)],
        compiler_params=_params("parallel"),
    )(a, *weights, *extras)


def _matmul_tn(a, b, *, name, tm=1024, tn=1024, tk=2048, slots=False):
    T, K1 = a.shape
    N = b.shape[1]
    tm, tn, tk = min(tm, K1), min(tn, N // N_CHIPS if slots else N), min(tk, T)
    assert K1 % tm == 0 and (N // N_CHIPS if slots else N) % tn == 0 and T % tk == 0
    per_slot = N // N_CHIPS // tn

    def body(a_ref, b_ref, o_ref):
        @pl.when(pl.program_id(2) == 0)
        def _():
            o_ref[...] = jnp.zeros_like(o_ref)

        o_ref[...] += _dot_tn(a_ref[...], b_ref[...])

    if slots:
        out_spec = pl.BlockSpec((None, tm, tn), lambda i, j, k: (j // per_slot, i, j % per_slot))
        out_shape = jax.ShapeDtypeStruct((N_CHIPS, K1, N // N_CHIPS), F32)
    else:
        out_spec = pl.BlockSpec((tm, tn), lambda i, j, k: (i, j))
        out_shape = jax.ShapeDtypeStruct((K1, N), F32)
    return pl.pallas_call(
        body, name=name, grid=(K1 // tm, N // tn, T // tk),
        in_specs=[pl.BlockSpec((tk, tm), lambda i, j, k: (k, i)), pl.BlockSpec((tk, tn), lambda i, j, k: (k, j))],
        out_specs=out_spec, out_shape=out_shape,
        compiler_params=_params("parallel", "parallel", "arbitrary"),
    )(a, b)


def _branch_merge(ysgu, yattn, w_bs, w_ba, gl, *, tm=MM_ROWS, tn=MM_COLS):
    T = ysgu.shape[0]
    tm = min(tm, T)
    nj = D_MODEL // tn

    def body(ys_ref, ya_ref, wbs_ref, wba_ref, gla_ref, glb_ref, a_ref, b_ref, m_ref):
        a = _dot(ys_ref[...], wbs_ref[...])
        b = _dot(ya_ref[...], wba_ref[...])
        a_ref[...] = a.astype(BF)
        b_ref[...] = b.astype(BF)
        m_ref[...] = (jax.nn.sigmoid(gla_ref[...].astype(F32)) * a + jax.nn.sigmoid(glb_ref[...].astype(F32)) * b).astype(BF)

    return pl.pallas_call(
        body, name="branch_merge", grid=(T // tm, nj),
        in_specs=[
            pl.BlockSpec((tm, SGU_W), lambda i, j: (i, 0)),
            pl.BlockSpec((tm, ATTN_W), lambda i, j: (i, 0)),
            pl.BlockSpec((SGU_W, tn), lambda i, j: (0, j)),
            pl.BlockSpec((ATTN_W, tn), lambda i, j: (0, j)),
            pl.BlockSpec((tm, tn), lambda i, j: (i, j)),
            pl.BlockSpec((tm, tn), lambda i, j: (i, j + nj)),
        ],
        out_specs=[pl.BlockSpec((tm, tn), lambda i, j: (i, j))] * 3,
        out_shape=[jax.ShapeDtypeStruct((T, D_MODEL), BF)] * 3,
        compiler_params=_params("parallel", "parallel"),
    )(ysgu, yattn, w_bs, w_ba, gl, gl)


def _row_spec(tr, width):
    return pl.BlockSpec((tr, width), lambda i: (i, 0))


def _vec_spec(width):
    return pl.BlockSpec((1, width), lambda i: (0, 0))


def _rms_fwd(x, g, shards, *, tr=256):
    T = x.shape[0]
    tr = min(tr, T)
    n_steps = T // tr
    k = len(shards)

    def body(x_ref, g_ref, *refs):
        step = pl.program_id(0)
        gather_start, gather_forward, gather_finish = _gather_phases(refs[:k], refs[k + 1:2 * k + 1], *refs[2 * k + 1:])
        pl.when(step == 0)(gather_start)
        pl.when(step == (3 * n_steps) // 4)(gather_forward)
        xv = x_ref[...]
        r = lax.rsqrt(jnp.mean(xv * xv, axis=-1, keepdims=True) + EPS)
        refs[k][...] = ((xv * r) * g_ref[...]).astype(BF)
        pl.when(step == n_steps - 1)(gather_finish)

    outs = pl.pallas_call(
        body, name="rms_fwd", grid=(n_steps,),
        in_specs=[_row_spec(tr, D_MODEL), _vec_spec(D_MODEL)] + [HBM] * k, out_specs=[_row_spec(tr, D_MODEL)] + [HBM] * k,
        out_shape=[jax.ShapeDtypeStruct((T, D_MODEL), BF)] + _gathered_shapes(shards),
        scratch_shapes=_gather_semaphores(k), compiler_params=_params("arbitrary"),
    )(x, g, *shards)
    return outs[0], outs[1:]


def _mixer_out_fwd(o, x, g_post, g_pre):
    r = lax.rsqrt(jnp.mean(o * o, axis=-1, keepdims=True) + EPS)
    h1 = x + (o * r) * g_post
    r2 = lax.rsqrt(jnp.mean(h1 * h1, axis=-1, keepdims=True) + EPS)
    return o, h1, (h1 * r2) * g_pre


def _matmul_rows(pairs, *, nt, rows, vecs, row_outs, n_sums, epilogue, name, tm=512, scatter=()):
    M = pairs[0][0].shape[0]
    N = pairs[0][1].shape[0] if nt else pairs[0][1].shape[1]
    tm = min(tm, M)
    n_steps = M // tm
    n_pairs, n_rows, n_vecs, n_out, n_scatter = len(pairs), len(rows), len(vecs), len(row_outs), len(scatter)

    def body(*refs):
        groups, at = [], 2 * n_pairs
        for count in (n_rows, n_vecs, n_scatter, n_out, n_sums, n_scatter):
            groups.append(refs[at:at + count])
            at += count
        r_refs, v_refs, b_refs, o_refs, s_refs, got_refs = groups
        sems = refs[at:]
        step = pl.program_id(0)
        if n_scatter:
            scatter_start, scatter_finish = _scatter_phases(b_refs, got_refs, *sems)
            pl.when(step == 0)(scatter_start)

        @pl.when(step == 0)
        def _():
            for s_ref in s_refs:
                s_ref[...] = jnp.zeros_like(s_ref)

        acc = None
        for p in range(n_pairs):
            a_ref, b_ref = refs[2 * p], refs[2 * p + 1]
            d = _dot_nt(a_ref[...], b_ref[...]) if nt else _dot(a_ref[...], b_ref[...])
            acc = d if acc is None else acc + d
        outs = epilogue(acc, *[r[...] for r in r_refs], *[v[...] for v in v_refs])
        for o_ref, o in zip(o_refs, outs[:n_out], strict=True):
            o_ref[...] = o.astype(o_ref.dtype)
        for s_ref, term in zip(s_refs, outs[n_out:], strict=True):
            s_ref[...] += jnp.sum(term, axis=0, keepdims=True)
        if n_scatter:
            pl.when(step == n_steps - 1)(scatter_finish)

    in_specs, args = [], []
    for a, b in pairs:
        in_specs += [_row_spec(tm, a.shape[1]), pl.BlockSpec(b.shape, lambda i: (0, 0))]
        args += [a, b]
    outs = pl.pallas_call(
        body, name=name, grid=(n_steps,),
        in_specs=in_specs + [_row_spec(tm, N)] * n_rows + [_vec_spec(N)] * n_vecs + [HBM] * n_scatter,
        out_specs=[_row_spec(tm, N)] * n_out + [_vec_spec(N)] * n_sums + [HBM] * n_scatter,
        out_shape=[jax.ShapeDtypeStruct((M, N), dt) for dt in row_outs] + [jax.ShapeDtypeStruct((1, N), F32)] * n_sums
        + (_scattered_shapes(scatter) if n_scatter else []),
        scratch_shapes=_scatter_semaphores(n_scatter) if n_scatter else [],
        compiler_params=pltpu.CompilerParams(dimension_semantics=("arbitrary",), vmem_limit_bytes=BIG_VMEM),
    )(*args, *rows, *vecs, *scatter)
    return outs


def _loss_head(dn, h1, target, g):
    r = lax.rsqrt(jnp.mean(dn * dn, axis=-1, keepdims=True) + EPS)
    err = h1 + (dn * r) * g - target
    dy = err * (1.0 / D_MODEL)
    ddn, dg_terms = _rms_bwd(dn, g, dy)
    return dy, ddn, err * err, dg_terms


def _mixer_out_bwd(dxn2, h1, dy, o, g_pre, g_post):
    da, dg_pre_terms = _rms_bwd(h1, g_pre, dxn2)
    dh1 = dy + da
    do, dg_post_terms = _rms_bwd(o, g_post, dh1)
    return dh1, do, dg_pre_terms, dg_post_terms


def _input_norm_bwd(dxn, x, dh1, g):
    da, dg_terms = _rms_bwd(x, g, dxn)
    return dh1 + da, dg_terms


def _sgu_norm(z_tile, g, b):
    gz = _gelu(z_tile)
    u, vv = gz[:, :SGU_W], gz[:, SGU_W:]
    xc = vv - jnp.mean(vv, axis=-1, keepdims=True)
    rstd = lax.rsqrt(jnp.mean(xc * xc, axis=-1, keepdims=True) + EPS)
    xhat = xc * rstd
    return u, xhat, rstd, xhat * g + b


def _sgu_mix(w_ref, v_bf, first_half):
    parts = []
    for p in range(N_GROUPS // 2):
        vp = v_bf[:, p * LANES:(p + 1) * LANES]
        parts.append(jnp.where(first_half, _dot(w_ref[2 * p], vp), _dot(w_ref[2 * p + 1], vp)))
    return jnp.concatenate(parts, axis=1)


def _sgu_fwd(z, g_sgu, b_sgu, ws, bias_plane, *, tm=512):
    T = z.shape[0]
    tm = min(tm, T)

    def body(z_ref, g_ref, b_ref, ws_ref, bp_ref, y_ref):
        u, _, _, vn = _sgu_norm(z_ref[...], g_ref[...], b_ref[...])
        vn_bf = vn.astype(BF)
        first_half = lax.broadcasted_iota(jnp.int32, (CHUNK, LANES), 1) < HEAD_DIM
        for c in range(tm // CHUNK):
            rows = slice(c * CHUNK, (c + 1) * CHUNK)
            s = _sgu_mix(ws_ref, vn_bf[rows, :], first_half) + bp_ref[...]
            y_ref[rows, :] = (u[rows, :] * s).astype(BF)

    return pl.pallas_call(
        body, name="sgu_fwd", grid=(T // tm,),
        in_specs=[_row_spec(tm, 2 * SGU_W), _vec_spec(SGU_W), _vec_spec(SGU_W),
                  pl.BlockSpec((N_GROUPS, CHUNK, CHUNK), lambda i: (0, 0, 0)),
                  pl.BlockSpec((CHUNK, SGU_W), lambda i: (0, 0))],
        out_specs=_row_spec(tm, SGU_W), out_shape=jax.ShapeDtypeStruct((T, SGU_W), BF),
        compiler_params=_params("parallel"),
    )(z, g_sgu, b_sgu, ws, bias_plane)


def _sgu_bwd(dy, z, g_sgu, b_sgu, ws, ws_t, bias_plane, exchange, *, tm=512):
    T = z.shape[0]
    tm = min(tm, T)
    n_steps = T // tm
    k = len(exchange)

    def body(dy_ref, z_ref, g_ref, b_ref, ws_ref, wst_ref, bp_ref, *refs):
        x_refs, (dz_ref, dws_ref, dbs_ref, dg_ref, db_ref), r_refs = refs[:k], refs[k:k + 5], refs[k + 5:2 * k + 5]
        dbp_ref, send_sems, recv_sems = refs[2 * k + 5:]
        step = pl.program_id(0)
        exchange_start, exchange_finish = _exchange_phases(x_refs, r_refs, send_sems, recv_sems)
        pl.when(step == 0)(exchange_start)

        @pl.when(step == 0)
        def _():
            dws_ref[...] = jnp.zeros_like(dws_ref)
            dg_ref[...] = jnp.zeros_like(dg_ref)
            db_ref[...] = jnp.zeros_like(db_ref)
            dbp_ref[...] = jnp.zeros_like(dbp_ref)

        g = g_ref[...]
        zt = z_ref[...]
        u, xhat, rstd, vn = _sgu_norm(zt, g, b_ref[...])
        vn_bf = vn.astype(BF)
        first_half = lax.broadcasted_iota(jnp.int32, (CHUNK, LANES), 1) < HEAD_DIM
        dyv = dy_ref[...]
        dg_acc = jnp.zeros((1, SGU_W), F32)
        db_acc = jnp.zeros((1, SGU_W), F32)
        for c in range(tm // CHUNK):
            rows = slice(c * CHUNK, (c + 1) * CHUNK)
            v_c = vn_bf[rows, :]
            s = _sgu_mix(ws_ref, v_c, first_half) + bp_ref[...]
            dy_c = dyv[rows, :]
            du = dy_c * s
            dsv = dy_c * u[rows, :]
            dbp_ref[...] += dsv
            ds_bf = dsv.astype(BF)
            zero = jnp.zeros((CHUNK, LANES), BF)
            for p in range(N_GROUPS // 2):
                dsp = ds_bf[:, p * LANES:(p + 1) * LANES]
                vp = v_c[:, p * LANES:(p + 1) * LANES]
                dws_ref[2 * p] += _dot_nt(jnp.where(first_half, dsp, zero), vp)
                dws_ref[2 * p + 1] += _dot_nt(jnp.where(first_half, zero, dsp), vp)
            dvn = _sgu_mix(wst_ref, ds_bf, first_half)
            xh = xhat[rows, :]
            dxh = dvn * g
            dvv = rstd[rows, :] * (dxh - jnp.mean(dxh, axis=-1, keepdims=True)
                                   - xh * jnp.mean(dxh * xh, axis=-1, keepdims=True))
            dg_acc += jnp.sum(dvn * xh, axis=0, keepdims=True)
            db_acc += jnp.sum(dvn, axis=0, keepdims=True)
            dgz = jnp.concatenate([du, dvv], axis=1)
            dz_ref[rows, :] = (dgz * _gelu_grad(zt[rows, :])).astype(BF)
        dg_ref[...] += dg_acc
        db_ref[...] += db_acc

        @pl.when(step == n_steps - 1)
        def _():
            r = lax.broadcasted_iota(jnp.int32, (CHUNK, CHUNK), 0)
            cidx = lax.broadcasted_iota(jnp.int32, (CHUNK, CHUNK), 1)
            causal = (cidx <= r).astype(F32)
            for gi in range(N_GROUPS):
                dws_ref[gi] = dws_ref[gi] * causal
            lane = lax.broadcasted_iota(jnp.int32, (CHUNK, LANES), 1)
            out = jnp.zeros((CHUNK, LANES), F32)
            dbp = dbp_ref[...]
            for gi in range(N_GROUPS):
                col = jnp.sum(dbp[:, gi * HEAD_DIM:(gi + 1) * HEAD_DIM], axis=1, keepdims=True)
                out = jnp.where(lane == gi, col, out)
            dbs_ref[...] = out
            exchange_finish()

    w_spec = pl.BlockSpec((N_GROUPS, CHUNK, CHUNK), lambda i: (0, 0, 0))
    plane = pl.BlockSpec((CHUNK, SGU_W), lambda i: (0, 0))
    outs = pl.pallas_call(
        body, name="sgu_bwd", grid=(n_steps,),
        in_specs=[_row_spec(tm, SGU_W), _row_spec(tm, 2 * SGU_W), _vec_spec(SGU_W), _vec_spec(SGU_W), w_spec, w_spec, plane]
        + [HBM] * k,
        out_specs=[_row_spec(tm, 2 * SGU_W), w_spec, pl.BlockSpec((CHUNK, LANES), lambda i: (0, 0)),
                   _vec_spec(SGU_W), _vec_spec(SGU_W)] + [HBM] * k,
        out_shape=[jax.ShapeDtypeStruct((T, 2 * SGU_W), BF), jax.ShapeDtypeStruct((N_GROUPS, CHUNK, CHUNK), F32),
                   jax.ShapeDtypeStruct((CHUNK, LANES), F32), jax.ShapeDtypeStruct((1, SGU_W), F32),
                   jax.ShapeDtypeStruct((1, SGU_W), F32)] + _exchanged_shapes(exchange),
        scratch_shapes=[pltpu.VMEM((CHUNK, SGU_W), F32)] + _exchange_semaphores(k),
        compiler_params=_params("arbitrary"),
    )(dy, z, g_sgu, b_sgu, ws, ws_t, bias_plane, *exchange)
    return outs[:5], outs[5:]


def _tri(n, upper):
    r = lax.broadcasted_iota(jnp.int32, (n, n), 0)
    c = lax.broadcasted_iota(jnp.int32, (n, n), 1)
    return ((c >= r) if upper else (c <= r)).astype(BF)


def _scan_dot(tri, x):
    hi, mid, lo = _split3(x)
    return (_dot(tri, hi.astype(BF)) + _dot(tri, mid.astype(BF))) + _dot(tri, lo.astype(BF))


def _with_lanes(base, lane, start, cols):
    out = base
    for k, col in enumerate(cols):
        if col is not None:
            out = jnp.where(lane == start + k, col, out)
    return out


def _logit_bound(q_norm, k_norm):
    return NORM_SLACK * q_norm * k_norm + 1.0


ATTN_TILE = 512
SKIP_BELOW = -110.0
NORM_SLACK = 1.001
BOUNDED_GAP = 60.0


def _attn_prep(qkv, fl, b_forget, *, tp=ATTN_TILE):
    T = qkv.shape[0]
    tp = min(tp, T)
    head_sum, gather6, place_q, place_k, place_v = (jnp.asarray(m, BF) for m in _attn_placements())

    def body(qkv_ref, fl_ref, bf_ref, hs_ref, g6_ref, pq_ref, pk_ref, pv_ref, qf_ref, kl_ref, vl_ref, st_ref, carry_ref, kmax_ref):
        @pl.when(pl.program_id(0) == 0)
        def _():
            carry_ref[...] = jnp.zeros_like(carry_ref)
            kmax_ref[...] = jnp.zeros_like(kmax_ref)

        x = fl_ref[...] + bf_ref[...]
        logf = jnp.minimum(x, 0.0) - jnp.log(1.0 + jnp.exp(-jnp.abs(x)))
        cum = _scan_dot(_tri(tp, upper=False), logf) + carry_ref[...]
        carry_ref[...] = cum[tp - 1:tp, :]

        def head_norms(block):
            sq = block * block
            hi = sq.astype(BF)
            return _dot(hi, hs_ref[...]) + _dot((sq - hi.astype(F32)).astype(BF), hs_ref[...])

        qkvv = qkv_ref[...]
        q_norm = NORM_SLACK * jnp.sqrt(head_norms(qkvv[:, :ATTN_W].astype(F32) * Q_SCALE))
        kn = NORM_SLACK * jnp.sqrt(jnp.max(head_norms(qkvv[:, ATTN_W:2 * ATTN_W].astype(F32)), axis=0, keepdims=True))
        k_seen = jnp.maximum(kmax_ref[...], kn)
        kmax_ref[...] = k_seen
        rows = (jnp.max(q_norm, axis=0, keepdims=True), kn, jnp.max(cum, axis=0, keepdims=True),
                jnp.min(cum, axis=0, keepdims=True), k_seen)
        st_ref[...] = jnp.zeros_like(st_ref)
        for k, row in enumerate(rows):
            st_ref[0, k:k + 1, :] = row
        parts = jnp.concatenate([p.astype(BF) for p in _split3(cum) + _split3(-_logit_bound(q_norm, k_seen))], axis=1)
        lane = lax.broadcasted_iota(jnp.int32, (tp, LANES), 1)
        side = jnp.where(lane == 6 * N_HEADS, 1.0, _dot(parts, g6_ref[...])).astype(BF)
        for h in range(N_HEADS):
            pair = slice((h // 2) * LANES, (h // 2 + 1) * LANES)
            for out_ref, block, place_ref in ((qf_ref, qkvv[:, :ATTN_W], pq_ref), (kl_ref, qkvv[:, ATTN_W:2 * ATTN_W], pk_ref),
                                              (vl_ref, qkvv[:, 2 * ATTN_W:], pv_ref)):
                out_ref[h] = _dot(jnp.concatenate([block[:, pair], side], axis=1), place_ref[h]).astype(BF)

    head_spec = pl.BlockSpec((N_HEADS, tp, LANES), lambda i: (0, i, 0))
    whole = lambda a: pl.BlockSpec(a.shape, lambda i: (0,) * a.ndim)
    return pl.pallas_call(
        body, name="attn_prep", grid=(T // tp,),
        in_specs=[_row_spec(tp, 3 * ATTN_W), _row_spec(tp, LANES), _vec_spec(LANES)]
        + [whole(m) for m in (head_sum, gather6, place_q, place_k, place_v)],
        out_specs=[head_spec] * 3 + [pl.BlockSpec((1, N_HEADS, LANES), lambda i: (i, 0, 0))],
        out_shape=[jax.ShapeDtypeStruct((N_HEADS, T, LANES), BF)] * 3 + [jax.ShapeDtypeStruct((T // tp, N_HEADS, LANES), F32)],
        scratch_shapes=[pltpu.VMEM((1, LANES), F32), pltpu.VMEM((1, LANES), F32)], compiler_params=_params("arbitrary"),
    )(qkv, fl, b_forget, head_sum, gather6, place_q, place_k, place_v)


def _attn_placements():
    head_sum = np.zeros((ATTN_W, LANES), np.float32)
    head_sum[np.arange(ATTN_W), np.arange(ATTN_W) // HEAD_DIM] = 1.0
    gather6 = np.zeros((6 * LANES, LANES), np.float32)
    for j in range(6):
        gather6[j * LANES + np.arange(N_HEADS), j * N_HEADS + np.arange(N_HEADS)] = 1.0
    place = np.zeros((3, N_HEADS, 2 * LANES, LANES), np.float32)
    one = LANES + 6 * N_HEADS
    d = np.arange(HEAD_DIM)
    for h in range(N_HEADS):
        side = lambda j: LANES + j * N_HEADS + h
        place[0, h, (h % 2) * HEAD_DIM + d, d] = Q_SCALE
        place[1:, h, (h % 2) * HEAD_DIM + d, d] = 1.0
        for j in range(3):
            place[0, h, side(j), HEAD_DIM + j] = 1.0
            place[0, h, one, HEAD_DIM + 3 + j] = 1.0
            place[0, h, side(3 + j), HEAD_DIM + 6 + j] = 1.0
            place[1, h, one, HEAD_DIM + j] = 1.0
            place[1, h, side(j), HEAD_DIM + 3 + j] = -1.0
            place[1, h, one, HEAD_DIM + 6 + j] = 1.0
            place[2, h, one, HEAD_DIM + j] = 1.0
    return head_sum, gather6, place[0], place[1], place[2]


def _attn_ranges(stats):
    qn, kn, cmax, cmin, k_seen = (stats[:, k, :N_HEADS].T for k in range(5))
    n = qn.shape[1]
    bounded = (2.0 * _logit_bound(qn, k_seen) <= BOUNDED_GAP).reshape(N_HEADS // 2, 2, n).all(axis=1)
    reach = NORM_SLACK * qn * (jnp.max(kn, axis=1, keepdims=True) + kn) + cmax
    i = jnp.arange(n)[None, :, None]
    j = jnp.arange(n)[None, None, :]
    need = ((reach[:, :, None] - cmin[:, None, :] >= SKIP_BELOW) | (i == j)) & (j <= i)
    first = jnp.min(jnp.where(need, j, n), axis=2).reshape(N_HEADS // 2, 2, n).min(axis=1)
    last = jnp.max(jnp.where(need, i, -1), axis=1).reshape(N_HEADS // 2, 2, n).max(axis=1)
    return first.reshape(-1).astype(F32), last.reshape(-1).astype(F32), bounded.reshape(-1).astype(F32)


def _pair_block(t):
    return pl.BlockSpec((2, t, LANES), lambda p, i, *_: (p, i, 0))


def _pair_full(T):
    return pl.BlockSpec((2, T, LANES), lambda p, i, *_: (p, 0, 0))


def _packed_block(t):
    return pl.BlockSpec((t, LANES), lambda p, i, *_: (i, p))


def _causal(t, keys_in_rows=False):
    r = lax.broadcasted_iota(jnp.int32, (t, t), 0)
    c = lax.broadcasted_iota(jnp.int32, (t, t), 1)
    return (r <= c) if keys_in_rows else (c <= r)


def _tile_rows(j, t):
    return pl.ds(pl.multiple_of(j * t, t), t)


def _attn_call(body, name, tile_scalars, operands, in_specs, out_specs, out_shape, scratch_shapes, n_tiles):
    return pl.pallas_call(
        body, name=name,
        grid_spec=pltpu.PrefetchScalarGridSpec(
            num_scalar_prefetch=len(tile_scalars), grid=(N_HEADS // 2, n_tiles), in_specs=in_specs, out_specs=out_specs,
            scratch_shapes=scratch_shapes),
        out_shape=out_shape, compiler_params=_params("arbitrary", "arbitrary"),
    )(*tile_scalars, *operands)


def _attn_fwd(qf, kl, vl, first, bounded, shards, *, tq=ATTN_TILE):
    T = qf.shape[1]
    tq = min(tq, T)
    n = T // tq
    n_steps = (N_HEADS // 2) * n
    k = len(shards)

    def body(first_ref, bounded_ref, qf_ref, kl_ref, vl_ref, *refs):
        w_refs, (o_ref, of_ref, ql_ref), g_refs = refs[:k], refs[k:k + 3], refs[k + 3:2 * k + 3]
        m_ref, acc_ref, send_sems, recv_sems = refs[2 * k + 3:]
        i = pl.program_id(1)
        tile = pl.program_id(0) * n + i
        gather_start, gather_forward, gather_finish = _gather_phases(w_refs, g_refs, send_sems, recv_sems)
        pl.when(tile == 0)(gather_start)
        pl.when(tile == (3 * n_steps) // 4)(gather_forward)
        start = first_ref[tile].astype(jnp.int32)
        is_bounded = bounded_ref[tile] > 0.5
        acc_ref[...] = jnp.zeros_like(acc_ref)
        diagonal = _tile_rows(i, tq)
        causal = _causal(tq)

        def logits(hh, rows):
            return _dot_nt(qf_ref[hh], kl_ref[hh, rows, :])

        @pl.when(is_bounded)
        def _():
            m_ref[...] = jnp.zeros_like(m_ref)

            def update(hh, s, rows):
                acc_ref[hh] += _dot(jnp.exp(s).astype(BF), vl_ref[hh, rows, :])

            def step(j, carry):
                for hh in range(2):
                    update(hh, logits(hh, _tile_rows(j, tq)), _tile_rows(j, tq))
                return carry

            lax.fori_loop(start, i, step, 0)
            for hh in range(2):
                update(hh, jnp.where(causal, logits(hh, diagonal), NEG), diagonal)

        @pl.when(jnp.logical_not(is_bounded))
        def _():
            m_ref[...] = jnp.full_like(m_ref, NEG)

            def update(hh, s, rows):
                m_old = m_ref[hh]
                m_new = jnp.maximum(m_old, jnp.max(s, axis=1, keepdims=True))
                p = jnp.exp(s - m_new)
                acc_ref[hh] = jnp.exp(m_old - m_new) * acc_ref[hh] + _dot(p.astype(BF), vl_ref[hh, rows, :])
                m_ref[hh] = m_new

            def step(j, carry):
                for hh in range(2):
                    update(hh, logits(hh, _tile_rows(j, tq)), _tile_rows(j, tq))
                return carry

            lax.fori_loop(start, i, step, 0)
            for hh in range(2):
                update(hh, jnp.where(causal, logits(hh, diagonal), NEG), diagonal)

        lane = lax.broadcasted_iota(jnp.int32, (tq, LANES), 1)
        outs = []
        for hh in range(2):
            q = qf_ref[hh].astype(F32)
            acc = acc_ref[hh]
            l = acc[:, HEAD_DIM:HEAD_DIM + 1]
            outs.append(acc[:, :HEAD_DIM] / l)
            at = HEAD_DIM + 6
            neg_bound = (q[:, at:at + 1] + q[:, at + 1:at + 2]) + q[:, at + 2:at + 3]
            ql_ref[hh] = _with_lanes(q, lane, at, _split3(neg_bound - (m_ref[hh] + jnp.log(l)))).astype(BF)
        o = jnp.concatenate(outs, axis=1)
        o_ref[...] = o.astype(BF)
        of_ref[...] = o
        pl.when(tile == n_steps - 1)(gather_finish)

    outs = _attn_call(
        body, "attn_fwd", (first, bounded), (qf, kl, vl, *shards),
        [_pair_block(tq), _pair_full(T), _pair_full(T)] + [HBM] * k,
        [_packed_block(tq), _packed_block(tq), _pair_block(tq)] + [HBM] * k,
        [jax.ShapeDtypeStruct((T, ATTN_W), BF), jax.ShapeDtypeStruct((T, ATTN_W), F32),
         jax.ShapeDtypeStruct((N_HEADS, T, LANES), BF)] + _gathered_shapes(shards),
        [pltpu.VMEM((2, tq, 1), F32), pltpu.VMEM((2, tq, LANES), F32)] + _gather_semaphores(k), n)
    return outs[0], outs[1], outs[2], outs[3:]


def _attn_bwd_prep(dya, of, *, tr=ATTN_TILE):
    T = dya.shape[0]
    tr = min(tr, T)
    head_sum, gather6 = (jnp.asarray(m, BF) for m in _attn_placements()[:2])
    gather3, place_do = gather6[:3 * LANES], _delta_placement()

    def body(d_ref, o_ref, hs_ref, g3_ref, p_ref, do_ref):
        dv = d_ref[...]
        delta = sum(_dot(part.astype(BF), hs_ref[...]) for part in _split3(dv * o_ref[...]))
        side = _dot(jnp.concatenate([p.astype(BF) for p in _split3(-delta)], axis=1), g3_ref[...]).astype(BF)
        d_bf = dv.astype(BF)
        for h in range(N_HEADS):
            pair = slice((h // 2) * LANES, (h // 2 + 1) * LANES)
            do_ref[h] = _dot(jnp.concatenate([d_bf[:, pair], side], axis=1), p_ref[h]).astype(BF)

    whole = lambda a: pl.BlockSpec(a.shape, lambda i: (0,) * a.ndim)
    return pl.pallas_call(
        body, name="attn_bwd_prep", grid=(T // tr,),
        in_specs=[_row_spec(tr, ATTN_W), _row_spec(tr, ATTN_W), whole(head_sum), whole(gather3), whole(place_do)],
        out_specs=pl.BlockSpec((N_HEADS, tr, LANES), lambda i: (0, i, 0)),
        out_shape=jax.ShapeDtypeStruct((N_HEADS, T, LANES), BF), compiler_params=_params("parallel"),
    )(dya, of, head_sum, gather3, place_do)


def _delta_placement():
    place = np.zeros((N_HEADS, 2 * LANES, LANES), np.float32)
    d = np.arange(HEAD_DIM)
    for h in range(N_HEADS):
        place[h, (h % 2) * HEAD_DIM + d, d] = 1.0
        for j in range(3):
            place[h, LANES + j * N_HEADS + h, HEAD_DIM + j] = 1.0
    return jnp.asarray(place, BF)


def _attn_bwd(kl, vl, ql, do, last, chip_sums, *, tk=ATTN_TILE):
    T = ql.shape[1]
    tk = min(tk, T)
    n = T // tk
    n_steps = (N_HEADS // 2) * n
    m = len(chip_sums)

    def body(last_ref, kl_ref, vl_ref, ql_ref, do_ref, *refs):
        b_refs, (dq_ref, dk_ref, dv_ref, extq_ref, extk_ref), r_refs = refs[:m], refs[m:m + 5], refs[m + 5:2 * m + 5]
        dq_acc, dk_acc, dv_acc, send_sems, recv_sems = refs[2 * m + 5:]
        j = pl.program_id(1)
        tile = pl.program_id(0) * n + j
        scatter_start, scatter_finish = _scatter_phases(b_refs, r_refs, send_sems, recv_sems)
        pl.when(tile == 0)(scatter_start)

        @pl.when(j == 0)
        def _():
            dq_acc[...] = jnp.zeros_like(dq_acc)

        dk_acc[...] = jnp.zeros_like(dk_acc)
        dv_acc[...] = jnp.zeros_like(dv_acc)

        def block(hh, rows, mask):
            qi, di, k = ql_ref[hh, rows, :], do_ref[hh, rows, :], kl_ref[hh]
            p_t = jnp.exp(_dot_nt(k, qi))
            if mask is not None:
                p_t = jnp.where(mask, p_t, 0.0)
            ds_t = (p_t * _dot_nt(vl_ref[hh], di)).astype(BF)
            dk_acc[hh] += _dot(ds_t, qi)
            dv_acc[hh] += _dot(p_t.astype(BF), di)
            dq_acc[hh, rows, :] += _dot_tn(ds_t, k)

        causal_t = _causal(tk, keys_in_rows=True)
        for hh in range(2):
            block(hh, _tile_rows(j, tk), causal_t)

        def step(i, carry):
            for hh in range(2):
                block(hh, _tile_rows(i, tk), None)
            return carry

        lax.fori_loop(j + 1, last_ref[pl.program_id(0) * n + j].astype(jnp.int32) + 1, step, 0)
        dk_ref[...] = jnp.concatenate([dk_acc[hh][:, :HEAD_DIM] for hh in range(2)], axis=1).astype(BF)
        dv_ref[...] = jnp.concatenate([dv_acc[hh][:, :HEAD_DIM] for hh in range(2)], axis=1).astype(BF)
        extk_ref[...] = jnp.concatenate([dk_acc[hh][:, HEAD_DIM:] for hh in range(2)], axis=1)

        @pl.when(j == n - 1)
        def _():
            dq_ref[...] = jnp.concatenate([dq_acc[hh][:, :HEAD_DIM] * Q_SCALE for hh in range(2)], axis=1).astype(BF)
            extq_ref[...] = jnp.concatenate([dq_acc[hh][:, HEAD_DIM:] for hh in range(2)], axis=1)

        pl.when(tile == n_steps - 1)(scatter_finish)

    whole = pl.BlockSpec((T, LANES), lambda p, j, *_: (0, p))
    outs = pl.pallas_call(
        body, name="attn_bwd",
        grid_spec=pltpu.PrefetchScalarGridSpec(
            num_scalar_prefetch=1, grid=(N_HEADS // 2, n),
            in_specs=[_pair_block(tk), _pair_block(tk), _pair_full(T), _pair_full(T)] + [HBM] * m,
            out_specs=[whole, _packed_block(tk), _packed_block(tk), whole, _packed_block(tk)] + [HBM] * m,
            scratch_shapes=[pltpu.VMEM((2, T, LANES), F32), pltpu.VMEM((2, tk, LANES), F32), pltpu.VMEM((2, tk, LANES), F32)]
            + _scatter_semaphores(m)),
        out_shape=[jax.ShapeDtypeStruct((T, ATTN_W), BF)] * 3 + [jax.ShapeDtypeStruct((T, ATTN_W), F32)] * 2
        + _scattered_shapes(chip_sums),
        compiler_params=pltpu.CompilerParams(dimension_semantics=("arbitrary", "arbitrary"), vmem_limit_bytes=BIG_VMEM),
    )(last, kl, vl, ql, do, *chip_sums)
    return outs[:5], outs[5:]


def _forget_bwd(ext_q, ext_k, fl, b_forget, *, tp=256):
    T = fl.shape[0]
    tp = min(tp, T)
    n = T // tp

    def body(eq_ref, ek_ref, fl_ref, bf_ref, dfl_ref, dbf_ref, carry_ref):
        @pl.when(pl.program_id(0) == 0)
        def _():
            carry_ref[...] = jnp.zeros_like(carry_ref)
            dbf_ref[...] = jnp.zeros_like(dbf_ref)

        lane = lax.broadcasted_iota(jnp.int32, (tp, LANES), 1)
        eq, ek = eq_ref[...], ek_ref[...]
        cols = [eq[:, h * HEAD_DIM:h * HEAD_DIM + 1] - ek[:, h * HEAD_DIM + 3:h * HEAD_DIM + 4] for h in range(N_HEADS)]
        dcum = _with_lanes(jnp.zeros((tp, LANES), F32), lane, 0, cols)
        suffix = _scan_dot(_tri(tp, upper=True), dcum) + carry_ref[...]
        carry_ref[...] = suffix[0:1, :]
        x = fl_ref[...] + bf_ref[...]
        dfl = jnp.where(lane < N_HEADS, suffix / (1.0 + jnp.exp(x)), 0.0)
        dfl_ref[...] = dfl.astype(BF)
        dbf_ref[...] += jnp.sum(dfl, axis=0, keepdims=True)

    rev = lambda w: pl.BlockSpec((tp, w), lambda i: (n - 1 - i, 0))
    return pl.pallas_call(
        body, name="forget_bwd", grid=(n,),
        in_specs=[rev(ATTN_W), rev(ATTN_W), rev(LANES), _vec_spec(LANES)],
        out_specs=[rev(LANES), _vec_spec(LANES)],
        out_shape=[jax.ShapeDtypeStruct((T, LANES), BF), jax.ShapeDtypeStruct((1, LANES), F32)],
        scratch_shapes=[pltpu.VMEM((1, LANES), F32)], compiler_params=_params("arbitrary"),
    )(ext_q, ext_k, fl, b_forget)


def _adamw(w, g, m, v, *, name, tr=256):
    _, rows, cols = w.shape
    tr = tr if rows % tr == 0 else rows

    def body(w_ref, g_ref, m_ref, v_ref, go_ref, d_ref, nm_ref, nv_ref):
        gv = g_ref[...]
        go_ref[...] = gv
        nm = ADAM_B1 * m_ref[...] + (1.0 - ADAM_B1) * gv
        nv = ADAM_B2 * v_ref[...] + (1.0 - ADAM_B2) * (gv * gv)
        m_hat = nm / (1.0 - ADAM_B1 ** ADAM_STEP)
        v_hat = nv / (1.0 - ADAM_B2 ** ADAM_STEP)
        d_ref[...] = -ADAM_LR * (m_hat / (jnp.sqrt(v_hat) + ADAM_EPS) + ADAM_WD * w_ref[...])
        nm_ref[...] = nm
        nv_ref[...] = nv

    spec = pl.BlockSpec((None, tr, cols), lambda i: (0, i, 0))
    return pl.pallas_call(
        body, name=name, grid=(rows // tr,), in_specs=[spec, pl.BlockSpec((tr, cols), lambda i: (i, 0)), spec, spec],
        out_specs=[spec] * 4, out_shape=[jax.ShapeDtypeStruct((1, rows, cols), F32)] * 4,
        compiler_params=_params("parallel"),
    )(w, g, m, v)


HBM = pl.BlockSpec(memory_space=pltpu.HBM)
BF16_ROWS = 16


def _place():
    x, y, c = lax.axis_index("x"), lax.axis_index("y"), lax.axis_index("c")
    others = [(1 - x, y), (x, 1 - y), (1 - x, 1 - y)]
    return x, y, c, others


def _chip(xy):
    return 2 * xy[0] + xy[1]


def _row_halves(c, rows):
    half = rows // 2
    assert half % BF16_ROWS == 0
    return (pl.ds(pl.multiple_of(c * half, BF16_ROWS), half), pl.ds(pl.multiple_of((1 - c) * half, BF16_ROWS), half))


def _remote(src, dst, send_sems, recv_sems, k, to):
    return pltpu.make_async_remote_copy(src_ref=src, dst_ref=dst, send_sem=send_sems.at[k], recv_sem=recv_sems.at[k],
                                        device_id=to, device_id_type=MESH)


def _gathered_shapes(shards):
    return [jax.ShapeDtypeStruct((N_CHIPS,) + s.shape, s.dtype) for s in shards]


def _gather_semaphores(n):
    return [pltpu.SemaphoreType.DMA((6 * n,)), pltpu.SemaphoreType.DMA((6 * n,))]


def _gather_phases(w_refs, g_refs, send_sems, recv_sems):
    n = len(w_refs)
    x, y, c, others = _place()
    sibling, me = (x, y, 1 - c), _chip((x, y))
    halves = [_row_halves(c, w.shape[0]) for w in w_refs]

    def sent(a, j, o):
        mine, _ = halves[a]
        return _remote(w_refs[a].at[mine, :], g_refs[a].at[me, mine, :], send_sems, recv_sems, 6 * a + j, (*o, c))

    def passed(a, j, o):
        landed = g_refs[a].at[_chip(o), halves[a][0], :]
        return _remote(landed, landed, send_sems, recv_sems, 6 * a + 3 + j, sibling)

    def start():
        for a in range(n):
            for j, o in enumerate(others):
                sent(a, j, o).start()

    def forward():
        for j, o in enumerate(others):
            for a in range(n):
                landed = g_refs[a].at[_chip(o), halves[a][0], :]
                _remote(landed, landed, send_sems, recv_sems, 6 * a + j, (*o, c)).wait_recv()
                passed(a, j, o).start()

    def finish():
        for j, o in enumerate(others):
            for a in range(n):
                landed = g_refs[a].at[_chip(o), halves[a][1], :]
                _remote(landed, landed, send_sems, recv_sems, 6 * a + 3 + j, sibling).wait_recv()
        for a in range(n):
            for j, o in enumerate(others):
                sent(a, j, o).wait_send()
                passed(a, j, o).wait_send()

    return start, forward, finish


def _exchange_halves(arrays, *, name):
    n = len(arrays)

    def body(*refs):
        for phase in _exchange_phases(refs[:n], refs[n:2 * n], *refs[2 * n:]):
            phase()

    return pl.pallas_call(
        body, name=name, in_specs=[HBM] * n, out_specs=[HBM] * n, out_shape=_exchanged_shapes(arrays),
        scratch_shapes=_exchange_semaphores(n),
    )(*arrays)


def _exchanged_shapes(arrays):
    return [jax.ShapeDtypeStruct(s.shape[:-2] + (s.shape[-2] // 2, s.shape[-1]), F32) for s in arrays]


def _exchange_semaphores(n):
    return [pltpu.SemaphoreType.DMA((n,)), pltpu.SemaphoreType.DMA((n,))]


def _exchange_phases(g_refs, r_refs, send_sems, recv_sems):
    x, y, c, _ = _place()

    def copy(a):
        _, theirs = _row_halves(c, g_refs[a].shape[-2])
        src = g_refs[a].at[:, theirs, :] if len(g_refs[a].shape) == 3 else g_refs[a].at[theirs, :]
        return _remote(src, r_refs[a], send_sems, recv_sems, a, (x, y, 1 - c))

    def start():
        for a in range(len(g_refs)):
            copy(a).start()

    def finish():
        for a in range(len(g_refs)):
            copy(a).wait()

    return start, finish


def _scatter_to_owners(chip_sums):
    n = len(chip_sums)

    def body(*refs):
        for phase in _scatter_phases(refs[:n], refs[n:2 * n], *refs[2 * n:]):
            phase()

    return pl.pallas_call(
        body, name="scatter_to_owners", in_specs=[HBM] * n, out_specs=[HBM] * n,
        out_shape=_scattered_shapes(chip_sums), scratch_shapes=_scatter_semaphores(n),
    )(*chip_sums)


def _scattered_shapes(chip_sums):
    return [jax.ShapeDtypeStruct(b.shape if b.ndim == 3 else (N_CHIPS,) + b.shape, b.dtype) for b in chip_sums]


def _scatter_semaphores(n):
    return [pltpu.SemaphoreType.DMA((3 * n,)), pltpu.SemaphoreType.DMA((3 * n,))]


def _scatter_phases(b_refs, r_refs, send_sems, recv_sems):
    n = len(b_refs)
    x, y, c, others = _place()
    me = _chip((x, y))

    def sent(a, j, o):
        src = b_refs[a].at[_chip(o)] if len(b_refs[a].shape) == 3 else b_refs[a]
        return _remote(src, r_refs[a].at[me], send_sems, recv_sems, 3 * a + j, (*o, c))

    def start():
        for a in range(n):
            for j, o in enumerate(others):
                sent(a, j, o).start()

    def finish():
        for a in range(n):
            for j, o in enumerate(others):
                landed = r_refs[a].at[_chip(o)]
                _remote(landed, landed, send_sems, recv_sems, 3 * a + j, (*o, c)).wait_recv()
        for a in range(n):
            for j, o in enumerate(others):
                sent(a, j, o).wait_send()

    return start, finish


def _join_halves(totals):
    n = len(totals)

    def body(*refs):
        in_refs, out_refs, (send_sems, recv_sems) = refs[:n], refs[n:2 * n], refs[2 * n:]
        x, y, c, _ = _place()
        copies = []
        for a in range(n):
            mine, _ = _row_halves(c, in_refs[a].shape[0])
            copies.append(_remote(in_refs[a].at[mine, :], out_refs[a].at[mine, :], send_sems, recv_sems, a, (x, y, 1 - c)))
            copies[-1].start()
        for cp in copies:
            cp.wait()

    return pl.pallas_call(
        body, name="join_halves", in_specs=[HBM] * n, out_specs=[HBM] * n,
        out_shape=[jax.ShapeDtypeStruct(t.shape, F32) for t in totals], input_output_aliases={a: a for a in range(n)},
        scratch_shapes=[pltpu.SemaphoreType.DMA((n,)), pltpu.SemaphoreType.DMA((n,))],
    )(*totals)


ADD_ROWS = 128


def _add_sibling(g, r, place, *, name):
    lead, (half, cols) = g.shape[:-2], r.shape[-2:]
    tr = min(ADD_ROWS, half)
    nb = half // tr
    zeros = (0,) * len(lead)

    def body(place_ref, g_ref, r_ref, o_ref, ob_ref):
        s = g_ref[...] + r_ref[...]
        o_ref[...] = s
        ob_ref[...] = s.astype(BF)

    spec = pl.BlockSpec(lead + (tr, cols), lambda i, p: zeros + (i, 0))
    return pl.pallas_call(
        body, name=name,
        grid_spec=pltpu.PrefetchScalarGridSpec(
            num_scalar_prefetch=1, grid=(nb,),
            in_specs=[pl.BlockSpec(lead + (tr, cols), lambda i, p: zeros + (p[1] * nb + i, 0)), spec], out_specs=[spec, spec]),
        out_shape=[jax.ShapeDtypeStruct(r.shape, F32), jax.ShapeDtypeStruct(r.shape, BF)],
        compiler_params=_params("parallel"),
    )(place, g, r)


def _add_chips(own, received, place, *, name, own_slots):
    half, cols = received.shape[-2:]
    tr = min(ADD_ROWS, half)
    nb = half // tr

    def written(k, p):
        return jnp.where(p[0] == k, (k + 1) % N_CHIPS, k)

    def body(place_ref, own_ref, *refs):
        o_ref = refs[N_CHIPS]
        mine = own_ref[0] if own_slots else own_ref[...]
        if own_slots:
            acc = mine
            for k in range(N_CHIPS):
                acc = acc + jnp.where(place_ref[0] == k, 0.0, refs[k][0].astype(F32))
        else:
            terms = [jnp.where(place_ref[0] == k, mine, refs[k][0]) for k in range(N_CHIPS)]
            acc = ((terms[0] + terms[1]) + terms[2]) + terms[3]
        o_ref[...] = acc

    own_spec = (pl.BlockSpec((1, tr, cols), lambda i, p: (p[0], i, 0)) if own_slots
                else pl.BlockSpec((tr, cols), lambda i, p: (i, 0)))
    return pl.pallas_call(
        body, name=name,
        grid_spec=pltpu.PrefetchScalarGridSpec(
            num_scalar_prefetch=1, grid=(nb,),
            in_specs=[own_spec] + [pl.BlockSpec((1, tr, cols), functools.partial(lambda i, p, k: (written(k, p), i, 0), k=k))
                                   for k in range(N_CHIPS)],
            out_specs=pl.BlockSpec((tr, cols), lambda i, p: (p[1] * nb + i, 0))),
        out_shape=jax.ShapeDtypeStruct((2 * half, cols), F32), compiler_params=_params("parallel"),
    )(place, own, *([received] * N_CHIPS))


SHARDED = (("w_in", (D_MODEL, 4616), 1), ("w_branch_sgu", (SGU_W, D_MODEL), 1), ("w_branch_attn", (ATTN_W, D_MODEL), 1),
           ("w_out", (D_MODEL, D_MODEL), 0), ("w_up", (D_MODEL, D_FF), 1), ("w_down", (D_FF, D_MODEL), 0))
SMALL = (("g_mix_pre", (1, D_MODEL)), ("b_forget", (1, N_HEADS)), ("g_sgu", (1, SGU_W)), ("b_sgu", (1, SGU_W)),
         ("w_spatial", (N_GROUPS * CHUNK, CHUNK)), ("b_spatial", (N_GROUPS, CHUNK)), ("g_mix_post", (1, D_MODEL)),
         ("g_ffn_pre", (1, D_MODEL)), ("g_ffn_post", (1, D_MODEL)))
SMALL_ALIGN = 2 * ADD_ROWS


def _shard_shape(shape, axis):
    return tuple(s // N_CHIPS if a == axis else s for a, s in enumerate(shape))


def _slots_to_full(slots, axis):
    return slots.reshape(-1, slots.shape[2]) if axis == 0 else slots.transpose(1, 0, 2).reshape(slots.shape[1], -1)


def _full_to_slots(full, axis):
    if axis == 0:
        return full.reshape(N_CHIPS, -1, full.shape[1])
    return full.reshape(full.shape[0], N_CHIPS, -1).transpose(1, 0, 2)


def _small_rows(shape):
    return -(-(shape[0] * shape[1]) // (8 * LANES)) * 8


def _pack_small(values):
    parts = []
    for name, shape in SMALL:
        flat = values[name].reshape(-1)
        n = _small_rows(shape)
        parts.append(jnp.pad(flat, (0, n * LANES - flat.shape[0])).reshape(n, LANES))
    rows = sum(p.shape[0] for p in parts)
    pad = -(-rows // SMALL_ALIGN) * SMALL_ALIGN - rows
    return jnp.concatenate(parts + [jnp.zeros((pad, LANES), F32)], axis=0)


def _unpack_small(packed):
    out, row = {}, 0
    for name, shape in SMALL:
        n = _small_rows(shape)
        out[name] = packed[row:row + n].reshape(-1)[:shape[0] * shape[1]].reshape(shape)
        row += n
    return out


IN_Z, IN_Q, IN_K, IN_V, IN_F, IN_G, IN_END = 0, 1024, 1536, 2048, 2560, 2568, 4616


LATE_WEIGHTS = ("w_branch_sgu", "w_branch_attn", "w_out", "w_up", "w_down")
EARLY_GRADS = LATE_WEIGHTS


def _with_own_slot(shard, gathered, chip):
    return jnp.where(jnp.arange(N_CHIPS)[:, None, None] == chip, shard[None], gathered)


def _assemble(name, shard, gathered, chip):
    axis = {n: a for n, _, a in SHARDED}[name]
    return _slots_to_full(_with_own_slot(shard, gathered, chip), axis)


def _columns_from_slots(slots, bounds):
    width = slots.shape[2]
    pieces = []
    for lo, hi in zip(bounds[:-1], bounds[1:], strict=True):
        parts = [slots[k][:, max(lo, k * width) - k * width:min(hi, (k + 1) * width) - k * width]
                 for k in range(N_CHIPS) if max(lo, k * width) < min(hi, (k + 1) * width)]
        pieces.append(parts[0] if len(parts) == 1 else jnp.concatenate(parts, axis=1))
    return pieces


def _columns_to_slots(pieces):
    width = sum(p.shape[1] for p in pieces) // N_CHIPS
    slots = []
    for k in range(N_CHIPS):
        parts, start = [], 0
        for p in pieces:
            lo, hi = max(k * width, start), min((k + 1) * width, start + p.shape[1])
            if lo < hi:
                parts.append(p[:, lo - start:hi - start])
            start += p.shape[1]
        slots.append(jnp.concatenate(parts, axis=1))
    return jnp.stack(slots)


def _local_step(x, target, shards, small, place):
    b_forget = jnp.pad(small["b_forget"], ((0, 0), (0, LANES - N_HEADS)))
    causal = jnp.tril(jnp.ones((CHUNK, CHUNK), bool))
    ws = jnp.where(causal[None], small["w_spatial"].reshape(N_GROUPS, CHUNK, CHUNK), 0.0).astype(BF)
    ws_t = ws.transpose(0, 2, 1)
    bias_plane = jnp.repeat(small["b_spatial"].T, HEAD_DIM, axis=1)

    xn, (w_in_slots,) = _rms_fwd(x, small["g_mix_pre"], [shards["w_in"]])
    w_z, w_q, w_k, w_v, w_f, w_ga, w_gb = _columns_from_slots(
        _with_own_slot(shards["w_in"], w_in_slots, place[0]), (IN_Z, IN_Q, IN_K, IN_V, IN_F, IN_G, IN_G + D_MODEL, IN_END))
    w_qkv, w_g = jnp.concatenate([w_q, w_k, w_v], axis=1), jnp.concatenate([w_ga, w_gb], axis=1)
    w_f = jnp.pad(w_f, ((0, 0), (0, LANES - N_HEADS)))
    z, qkv, gl, fl = _project(xn, [w_z, w_qkv, w_g, w_f], [F32, BF, BF, F32], name="proj_in")
    ysgu = _sgu_fwd(z, small["g_sgu"], small["b_sgu"], ws, bias_plane)
    qf, kl, vl, tile_stats = _attn_prep(qkv, fl, b_forget)
    first_key_tile, last_query_tile, bounded = _attn_ranges(tile_stats)
    yattn, yattn_f, ql, gathered = _attn_fwd(qf, kl, vl, first_key_tile, bounded, [shards[name] for name in LATE_WEIGHTS])
    w = {name: _assemble(name, shards[name], got, place[0]) for name, got in zip(LATE_WEIGHTS, gathered, strict=True)}
    a, b, merged = _branch_merge(ysgu, yattn, w["w_branch_sgu"], w["w_branch_attn"], gl)
    o, h1, xn2 = _matmul_rows(
        [(merged, w["w_out"])], nt=False, rows=[x], vecs=[small["g_mix_post"], small["g_ffn_pre"]], row_outs=[F32, F32, BF],
        n_sums=0, epilogue=_mixer_out_fwd, name="proj_out_norms")

    (hid,) = _project(xn2, [w["w_up"]], [BF], name="ffn_up", tm=FFN_ROWS, epilogue=lambda acc: jnp.square(jnp.maximum(acc, 0.0)))
    dy, ddn, sq, dg_ffn_post = _matmul_rows(
        [(hid, w["w_down"])], nt=False, rows=[h1, target], vecs=[small["g_ffn_post"]], row_outs=[F32, BF], n_sums=2,
        epilogue=_loss_head, name="ffn_down_loss")

    (dup,) = _project(ddn, [w["w_down"]], [BF], name="ffn_down_bwd", tm=FFN_ROWS, nt=True, extra=hid,
                      epilogue=lambda acc, h: acc * (2.0 * jnp.sqrt(h.astype(F32))))
    dw_down = _matmul_tn(hid, ddn, name="dw_down")
    dh1, do, dg_ffn_pre, dg_mix_post = _matmul_rows(
        [(dup, w["w_up"])], nt=True, rows=[h1, dy, o], vecs=[small["g_ffn_pre"], small["g_mix_post"]], row_outs=[F32, BF],
        n_sums=2, epilogue=_mixer_out_bwd, name="ffn_up_bwd_norms")
    dw_up = _matmul_tn(xn2, dup, name="dw_up", slots=True)

    def gate_bwd(dm, a_t, b_t, gla, glb):
        ga, gb = jax.nn.sigmoid(gla.astype(F32)), jax.nn.sigmoid(glb.astype(F32))
        return dm * ga, dm * gb, dm * a_t.astype(F32) * (ga * (1.0 - ga)), dm * b_t.astype(F32) * (gb * (1.0 - gb))

    da, db, dgla, dglb = _matmul([(do, w["w_out"])], nt=True, out_dtypes=[BF] * 4, name="proj_out_bwd",
                                 epilogue=gate_bwd, extras=[a, b, (gl, 0), (gl, D_MODEL)])
    dw_out = _matmul_tn(merged, do, name="dw_out")
    dysgu = _matmul([(da, w["w_branch_sgu"])], nt=True, out_dtypes=[F32], name="branch_sgu_bwd")
    dyattn = _matmul([(db, w["w_branch_attn"])], nt=True, out_dtypes=[F32], name="branch_attn_bwd")
    dw_bs = _matmul_tn(ysgu, da, name="dw_branch_sgu")
    dw_ba = _matmul_tn(yattn, db, name="dw_branch_attn")
    early = {"w_branch_sgu": _full_to_slots(dw_bs, 1), "w_branch_attn": _full_to_slots(dw_ba, 1),
             "w_out": _full_to_slots(dw_out, 0), "w_up": dw_up, "w_down": _full_to_slots(dw_down, 0)}
    (dz, dws, dbs, dg_sgu, db_sgu), early_theirs = _sgu_bwd(
        dysgu, z, small["g_sgu"], small["b_sgu"], ws, ws_t, bias_plane, [early[name] for name in EARLY_GRADS])
    early_sums = {name: _add_sibling(early[name], theirs, place, name="add_sibling_" + name)
                  for name, theirs in zip(EARLY_GRADS, early_theirs, strict=True)}
    dout = _attn_bwd_prep(dyattn, yattn_f)
    (dq, dk, dv, ext_q, ext_k), early_received = _attn_bwd(
        kl, vl, ql, dout, last_query_tile, [early_sums[name][1] for name in EARLY_GRADS])
    dfl, dbf = _forget_bwd(ext_q, ext_k, fl, b_forget)
    dw_z, dw_q, dw_k, dw_v, dw_f = _matmul_tn_multi(xn, [dz, dq, dk, dv, dfl], name="dw_in_mix")
    dw_ga, dw_gb = _matmul_tn_multi(xn, [dgla, dglb], name="dw_in_gates")
    dw_in = _columns_to_slots([dw_z, dw_q, dw_k, dw_v, dw_f[:, :N_HEADS], dw_ga, dw_gb])
    (dw_in_theirs,) = _exchange_halves([dw_in], name="exchange_halves_w_in")
    dw_in_sum = _add_sibling(dw_in, dw_in_theirs, place, name="add_sibling_w_in")
    dx, dg_mix_pre, dw_in_received = _matmul_rows(
        [(dz, w_z), (dq, w_q), (dk, w_k), (dv, w_v), (dgla, w_ga), (dglb, w_gb), (dfl, w_f)],
        nt=True, rows=[x, dh1], vecs=[small["g_mix_pre"]], row_outs=[F32], n_sums=1, epilogue=_input_norm_bwd,
        name="proj_in_bwd_norm", scatter=[dw_in_sum[1]])

    reduced = {name: (early_sums[name][0], got) for name, got in zip(EARLY_GRADS, early_received, strict=True)}
    reduced["w_in"] = (dw_in_sum[0], dw_in_received)
    small_grads = {"g_mix_pre": dg_mix_pre, "b_forget": dbf[:, :N_HEADS], "g_sgu": dg_sgu, "b_sgu": db_sgu,
                   "w_spatial": dws.reshape(N_GROUPS * CHUNK, CHUNK), "b_spatial": dbs[:, :N_GROUPS].T,
                   "g_mix_post": dg_mix_post, "g_ffn_pre": dg_ffn_pre, "g_ffn_post": dg_ffn_post}
    return sq, dx, reduced, small_grads


NAMES = ("g_mix_pre", "w_in", "b_forget", "g_sgu", "b_sgu", "w_spatial", "b_spatial", "w_branch_sgu", "w_branch_attn",
         "w_out", "g_mix_post", "g_ffn_pre", "w_up", "w_down", "g_ffn_post")


def kernel(x, g_mix_pre, w_in, b_forget, g_sgu, b_sgu, w_spatial, b_spatial, w_branch_sgu, w_branch_attn, w_out, g_mix_post, g_ffn_pre, w_up, w_down, g_ffn_post, loss_target, m_g_mix_pre, m_w_in, m_b_forget, m_g_sgu, m_b_sgu, m_w_spatial, m_b_spatial, m_w_branch_sgu, m_w_branch_attn, m_w_out, m_g_mix_post, m_g_ffn_pre, m_w_up, m_w_down, m_g_ffn_post, v_g_mix_pre, v_w_in, v_b_forget, v_g_sgu, v_b_sgu, v_w_spatial, v_b_spatial, v_w_branch_sgu, v_w_branch_attn, v_w_out, v_g_mix_post, v_g_ffn_pre, v_w_up, v_w_down, v_g_ffn_post):
    weights = dict(zip(NAMES, (g_mix_pre, w_in, b_forget, g_sgu, b_sgu, w_spatial, b_spatial, w_branch_sgu, w_branch_attn,
                               w_out, g_mix_post, g_ffn_pre, w_up, w_down, g_ffn_post), strict=True))
    first = dict(zip(NAMES, (m_g_mix_pre, m_w_in, m_b_forget, m_g_sgu, m_b_sgu, m_w_spatial, m_b_spatial, m_w_branch_sgu,
                             m_w_branch_attn, m_w_out, m_g_mix_post, m_g_ffn_pre, m_w_up, m_w_down, m_g_ffn_post), strict=True))
    second = dict(zip(NAMES, (v_g_mix_pre, v_w_in, v_b_forget, v_g_sgu, v_b_sgu, v_w_spatial, v_b_spatial, v_w_branch_sgu,
                              v_w_branch_attn, v_w_out, v_g_mix_post, v_g_ffn_pre, v_w_up, v_w_down, v_g_ffn_post), strict=True))
    shard_shapes = {name: _shard_shape(shape, axis) for name, shape, axis in SHARDED}
    small_shapes = dict(SMALL)
    view = lambda name, a: a.reshape(shard_shapes.get(name) or small_shapes[name])

    place = jnp.stack([2 * lax.axis_index("x") + lax.axis_index("y"), lax.axis_index("c")]).astype(jnp.int32)

    shards = {name: view(name, weights[name]).astype(BF) for name, _, _ in SHARDED}
    small = {name: view(name, weights[name]) for name, _ in SMALL}
    sq, dx, reduced, small_grads = _local_step(x[0], loss_target[0], shards, small, place)
    loss = lax.psum(0.5 * jnp.sum(sq) / D_MODEL, ("x", "y", "c"))

    small_mine = _pack_small(small_grads)
    (small_theirs,) = _exchange_halves([small_mine], name="exchange_halves_small")
    small_sum, _ = _add_sibling(small_mine, small_theirs, place, name="add_sibling_small")
    (small_received,) = _scatter_to_owners([small_sum])
    totals = {name: _add_chips(s, r, place, name="add_chips_" + name, own_slots=True) for name, (s, r) in reduced.items()}
    small_total = _add_chips(small_sum, small_received, place, name="add_chips_small", own_slots=False)
    joined = _join_halves([totals[name] for name, _, _ in SHARDED] + [small_total])
    grad = {**{name: g for (name, _, _), g in zip(SHARDED, joined[:-1], strict=True)}, **_unpack_small(joined[-1])}

    grad_out, delta, new_m, new_v = {}, {}, {}, {}
    for name in NAMES:
        rows, cols = grad[name].shape
        as_given = lambda a: a.reshape(1, rows, cols)
        grad_out[name], delta[name], new_m[name], new_v[name] = _adamw(
            as_given(weights[name]), grad[name], as_given(first[name]), as_given(second[name]), name="adamw_" + name)

    like = lambda d: [d[name].reshape(weights[name].shape) for name in NAMES]
    return (loss, dx[None], *like(grad_out), *like(delta), *like(new_m), *like(new_v))
```

```python
import functools

import jax
import jax.numpy as jnp
import numpy as np
from jax import lax
from jax.experimental import pallas as pl
from jax.experimental.pallas import tpu as pltpu

F32 = jnp.float32
BF = jnp.bfloat16
MESH = pl.DeviceIdType.MESH

D_MODEL = 1024
N_HEADS = 8
HEAD_DIM = 64
ATTN_W = N_HEADS * HEAD_DIM
SGU_W = 512
N_GROUPS = 8
CHUNK = 128
D_FF = 4096
EPS = 1e-6
Q_SCALE = HEAD_DIM ** -0.5
N_CHIPS = 4
LANES = 128

ADAM_LR = 0.001
ADAM_B1 = 0.9
ADAM_B2 = 0.999
ADAM_EPS = 1e-08
ADAM_WD = 0.01
ADAM_STEP = 10

VMEM_LIMIT = 48 * 1024 * 1024
BIG_VMEM = 58 * 1024 * 1024
NEG = -1e30


def _params(*sem):
    return pltpu.CompilerParams(dimension_semantics=sem, vmem_limit_bytes=VMEM_LIMIT)


def _dot(a, b):
    return jnp.dot(a, b, preferred_element_type=F32)


def _dot_nt(a, b):
    return lax.dot_general(a, b, (((1,), (1,)), ((), ())), preferred_element_type=F32)


def _dot_tn(a, b):
    return lax.dot_general(a, b, (((0,), (0,)), ((), ())), preferred_element_type=F32)


def _split3(c):
    hi = c.astype(BF).astype(F32)
    r = c - hi
    mid = r.astype(BF).astype(F32)
    lo = (r - mid).astype(BF).astype(F32)
    return hi, mid, lo


def _gelu(x):
    k = 0.7978845608028654
    return 0.5 * x * (1.0 + jnp.tanh(k * (x + 0.044715 * (x * x * x))))


def _gelu_grad(x):
    k = 0.7978845608028654
    x2 = x * x
    t = jnp.tanh(k * (x + 0.044715 * (x2 * x)))
    return 0.5 * (1.0 + t) + 0.5 * x * (1.0 - t * t) * (k * (1.0 + 3.0 * 0.044715 * x2))


def _rms_bwd(a, g, dy):
    r = lax.rsqrt(jnp.mean(a * a, axis=-1, keepdims=True) + EPS)
    n = a * r
    dn = dy * g
    da = r * (dn - n * jnp.mean(dn * n, axis=-1, keepdims=True))
    return da, dy * n


MM_ROWS = 1024
MM_COLS = 512
FFN_ROWS = 256


def _matmul(pairs, *, nt, out_dtypes, name, tm=MM_ROWS, tn=MM_COLS, epilogue=None, extras=()):
    n_pairs, n_extra = len(pairs), len(extras)
    M = pairs[0][0].shape[0]
    N = pairs[0][1].shape[0] if nt else pairs[0][1].shape[1]
    tm, tn = min(tm, M), min(tn, N)
    assert M % tm == 0 and N % tn == 0

    def body(*refs):
        acc = None
        for p in range(n_pairs):
            a_ref, b_ref = refs[2 * p], refs[2 * p + 1]
            d = _dot_nt(a_ref[...], b_ref[...]) if nt else _dot(a_ref[...], b_ref[...])
            acc = d if acc is None else acc + d
        e_refs = refs[2 * n_pairs:2 * n_pairs + n_extra]
        o_refs = refs[2 * n_pairs + n_extra:]
        outs = (acc,) if epilogue is None else epilogue(acc, *[e[...] for e in e_refs])
        for o_ref, o in zip(o_refs, outs, strict=True):
            o_ref[...] = o.astype(o_ref.dtype)

    in_specs, args = [], []
    for a, b in pairs:
        K = a.shape[1]
        in_specs.append(pl.BlockSpec((tm, K), lambda i, j: (i, 0)))
        in_specs.append(pl.BlockSpec((tn, K), lambda i, j: (j, 0)) if nt else pl.BlockSpec((K, tn), lambda i, j: (0, j)))
        args += [a, b]
    for e in extras:
        e, col = e if isinstance(e, tuple) else (e, 0)
        in_specs.append(pl.BlockSpec((tm, tn), functools.partial(lambda i, j, off: (i, j + off), off=col // tn)))
        args.append(e)
    outs = pl.pallas_call(
        body, name=name, grid=(M // tm, N // tn), in_specs=in_specs,
        out_specs=[pl.BlockSpec((tm, tn), lambda i, j: (i, j)) for _ in out_dtypes],
        out_shape=[jax.ShapeDtypeStruct((M, N), dt) for dt in out_dtypes],
        compiler_params=_params("parallel", "parallel"),
    )(*args)
    return outs if len(outs) > 1 else outs[0]


def _matmul_tn_multi(a, bs, *, name, tk=1024):
    T, K1 = a.shape
    tk = min(tk, T)
    n = len(bs)

    def body(a_ref, *refs):
        @pl.when(pl.program_id(0) == 0)
        def _():
            for o_ref in refs[n:]:
                o_ref[...] = jnp.zeros_like(o_ref)

        av = a_ref[...]
        for b_ref, o_ref in zip(refs[:n], refs[n:], strict=True):
            o_ref[...] += _dot_tn(av, b_ref[...])

    return pl.pallas_call(
        body, name=name, grid=(T // tk,),
        in_specs=[pl.BlockSpec((tk, K1), lambda k: (k, 0))] + [pl.BlockSpec((tk, b.shape[1]), lambda k: (k, 0)) for b in bs],
        out_specs=[pl.BlockSpec((K1, b.shape[1]), lambda k: (0, 0)) for b in bs],
        out_shape=[jax.ShapeDtypeStruct((K1, b.shape[1]), F32) for b in bs],
        compiler_params=pltpu.CompilerParams(dimension_semantics=("arbitrary",), vmem_limit_bytes=BIG_VMEM),
    )(a, *bs)


def _project(a, weights, out_dtypes, *, name, tm=512, nt=False, epilogue=None, extra=None):
    M, K = a.shape
    tm = min(tm, M)
    n = len(weights)
    widths = [w.shape[0] if nt else w.shape[1] for w in weights]
    extras = [] if extra is None else [extra]

    def body(a_ref, *refs):
        av = a_ref[...]
        w_refs, e_refs, o_refs = refs[:n], refs[n:n + len(extras)], refs[n + len(extras):]
        for w_ref, o_ref in zip(w_refs, o_refs, strict=True):
            acc = _dot_nt(av, w_ref[...]) if nt else _dot(av, w_ref[...])
            if epilogue is not None:
                acc = epilogue(acc, *[e[...] for e in e_refs])
            o_ref[...] = acc.astype(o_ref.dtype)

    return pl.pallas_call(
        body, name=name, grid=(M // tm,),
        in_specs=[pl.BlockSpec((tm, K), lambda i: (i, 0))] + [pl.BlockSpec(w.shape, lambda i: (0, 0)) for w in weights]
        + [pl.BlockSpec((tm, e.shape[1]), lambda i: (i, 0)) for e in extras],
        out_specs=[pl.BlockSpec((tm, width), lambda i: (i, 0)) for width in widths],
        out_shape=[jax.ShapeDtypeStruct((M, width), dt) for width, dt in zip(widths, out_dtypes, strict=True)],
        compiler_params=_params("parallel"),
    )(a, *weights, *extras)


def _matmul_tn(a, b, *, name, tm=1024, tn=1024, tk=2048, slots=False):
    T, K1 = a.shape
    N = b.shape[1]
    tm, tn, tk = min(tm, K1), min(tn, N // N_CHIPS if slots else N), min(tk, T)
    assert K1 % tm == 0 and (N // N_CHIPS if slots else N) % tn == 0 and T % tk == 0
    per_slot = N // N_CHIPS // tn

    def body(a_ref, b_ref, o_ref):
        @pl.when(pl.program_id(2) == 0)
        def _():
            o_ref[...] = jnp.zeros_like(o_ref)

        o_ref[...] += _dot_tn(a_ref[...], b_ref[...])

    if slots:
        out_spec = pl.BlockSpec((None, tm, tn), lambda i, j, k: (j // per_slot, i, j % per_slot))
        out_shape = jax.ShapeDtypeStruct((N_CHIPS, K1, N // N_CHIPS), F32)
    else:
        out_spec = pl.BlockSpec((tm, tn), lambda i, j, k: (i, j))
        out_shape = jax.ShapeDtypeStruct((K1, N), F32)
    return pl.pallas_call(
        body, name=name, grid=(K1 // tm, N // tn, T // tk),
        in_specs=[pl.BlockSpec((tk, tm), lambda i, j, k: (k, i)), pl.BlockSpec((tk, tn), lambda i, j, k: (k, j))],
        out_specs=out_spec, out_shape=out_shape,
        compiler_params=_params("parallel", "parallel", "arbitrary"),
    )(a, b)


def _branch_merge(ysgu, yattn, w_bs, w_ba, gl, *, tm=MM_ROWS, tn=MM_COLS):
    T = ysgu.shape[0]
    tm = min(tm, T)
    nj = D_MODEL // tn

    def body(ys_ref, ya_ref, wbs_ref, wba_ref, gla_ref, glb_ref, a_ref, b_ref, m_ref):
        a = _dot(ys_ref[...], wbs_ref[...])
        b = _dot(ya_ref[...], wba_ref[...])
        a_ref[...] = a.astype(BF)
        b_ref[...] = b.astype(BF)
        m_ref[...] = (jax.nn.sigmoid(gla_ref[...].astype(F32)) * a + jax.nn.sigmoid(glb_ref[...].astype(F32)) * b).astype(BF)

    return pl.pallas_call(
        body, name="branch_merge", grid=(T // tm, nj),
        in_specs=[
            pl.BlockSpec((tm, SGU_W), lambda i, j: (i, 0)),
            pl.BlockSpec((tm, ATTN_W), lambda i, j: (i, 0)),
            pl.BlockSpec((SGU_W, tn), lambda i, j: (0, j)),
            pl.BlockSpec((ATTN_W, tn), lambda i, j: (0, j)),
            pl.BlockSpec((tm, tn), lambda i, j: (i, j)),
            pl.BlockSpec((tm, tn), lambda i, j: (i, j + nj)),
        ],
        out_specs=[pl.BlockSpec((tm, tn), lambda i, j: (i, j))] * 3,
        out_shape=[jax.ShapeDtypeStruct((T, D_MODEL), BF)] * 3,
        compiler_params=_params("parallel", "parallel"),
    )(ysgu, yattn, w_bs, w_ba, gl, gl)


def _row_spec(tr, width):
    return pl.BlockSpec((tr, width), lambda i: (i, 0))


def _vec_spec(width):
    return pl.BlockSpec((1, width), lambda i: (0, 0))


def _rms_fwd(x, g, shards, *, tr=256):
    T = x.shape[0]
    tr = min(tr, T)
    n_steps = T // tr
    k = len(shards)

    def body(x_ref, g_ref, *refs):
        step = pl.program_id(0)
        gather_start, gather_forward, gather_finish = _gather_phases(refs[:k], refs[k + 1:2 * k + 1], *refs[2 * k + 1:])
        pl.when(step == 0)(gather_start)
        xv = x_ref[...]
        r = lax.rsqrt(jnp.mean(xv * xv, axis=-1, keepdims=True) + EPS)
        refs[k][...] = ((xv * r) * g_ref[...]).astype(BF)

        @pl.when(step == n_steps - 1)
        def _():
            gather_forward()
            gather_finish()

    outs = pl.pallas_call(
        body, name="rms_fwd", grid=(n_steps,),
        in_specs=[_row_spec(tr, D_MODEL), _vec_spec(D_MODEL)] + [HBM] * k, out_specs=[_row_spec(tr, D_MODEL)] + [HBM] * k,
        out_shape=[jax.ShapeDtypeStruct((T, D_MODEL), BF)] + _gathered_shapes(shards),
        scratch_shapes=_gather_semaphores(k), compiler_params=_params("arbitrary"),
    )(x, g, *shards)
    return outs[0], outs[1:]


def _mixer_out_fwd(o, x, g_post, g_pre):
    r = lax.rsqrt(jnp.mean(o * o, axis=-1, keepdims=True) + EPS)
    h1 = x + (o * r) * g_post
    r2 = lax.rsqrt(jnp.mean(h1 * h1, axis=-1, keepdims=True) + EPS)
    return o, h1, (h1 * r2) * g_pre


def _matmul_rows(pairs, *, nt, rows, vecs, row_outs, n_sums, epilogue, name, tm=512, scatter=()):
    M = pairs[0][0].shape[0]
    N = pairs[0][1].shape[0] if nt else pairs[0][1].shape[1]
    tm = min(tm, M)
    n_steps = M // tm
    n_pairs, n_rows, n_vecs, n_out, n_scatter = len(pairs), len(rows), len(vecs), len(row_outs), len(scatter)

    def body(*refs):
        groups, at = [], 2 * n_pairs
        for count in (n_rows, n_vecs, n_scatter, n_out, n_sums, n_scatter):
            groups.append(refs[at:at + count])
            at += count
        r_refs, v_refs, b_refs, o_refs, s_refs, got_refs = groups
        sems = refs[at:]
        step = pl.program_id(0)
        if n_scatter:
            scatter_start, scatter_finish = _scatter_phases(b_refs, got_refs, *sems)
            pl.when(step == 0)(scatter_start)

        @pl.when(step == 0)
        def _():
            for s_ref in s_refs:
                s_ref[...] = jnp.zeros_like(s_ref)

        acc = None
        for p in range(n_pairs):
            a_ref, b_ref = refs[2 * p], refs[2 * p + 1]
            d = _dot_nt(a_ref[...], b_ref[...]) if nt else _dot(a_ref[...], b_ref[...])
            acc = d if acc is None else acc + d
        outs = epilogue(acc, *[r[...] for r in r_refs], *[v[...] for v in v_refs])
        for o_ref, o in zip(o_refs, outs[:n_out], strict=True):
            o_ref[...] = o.astype(o_ref.dtype)
        for s_ref, term in zip(s_refs, outs[n_out:], strict=True):
            s_ref[...] += jnp.sum(term, axis=0, keepdims=True)
        if n_scatter:
            pl.when(step == n_steps - 1)(scatter_finish)

    in_specs, args = [], []
    for a, b in pairs:
        in_specs += [_row_spec(tm, a.shape[1]), pl.BlockSpec(b.shape, lambda i: (0, 0))]
        args += [a, b]
    outs = pl.pallas_call(
        body, name=name, grid=(n_steps,),
        in_specs=in_specs + [_row_spec(tm, N)] * n_rows + [_vec_spec(N)] * n_vecs + [HBM] * n_scatter,
        out_specs=[_row_spec(tm, N)] * n_out + [_vec_spec(N)] * n_sums + [HBM] * n_scatter,
        out_shape=[jax.ShapeDtypeStruct((M, N), dt) for dt in row_outs] + [jax.ShapeDtypeStruct((1, N), F32)] * n_sums
        + (_scattered_shapes(scatter) if n_scatter else []),
        scratch_shapes=_scatter_semaphores(n_scatter) if n_scatter else [],
        compiler_params=pltpu.CompilerParams(dimension_semantics=("arbitrary",), vmem_limit_bytes=BIG_VMEM),
    )(*args, *rows, *vecs, *scatter)
    return outs


def _loss_head(dn, h1, target, g):
    r = lax.rsqrt(jnp.mean(dn * dn, axis=-1, keepdims=True) + EPS)
    err = h1 + (dn * r) * g - target
    dy = err * (1.0 / D_MODEL)
    ddn, dg_terms = _rms_bwd(dn, g, dy)
    return dy, ddn, err * err, dg_terms


def _mixer_out_bwd(dxn2, h1, dy, o, g_pre, g_post):
    da, dg_pre_terms = _rms_bwd(h1, g_pre, dxn2)
    dh1 = dy + da
    do, dg_post_terms = _rms_bwd(o, g_post, dh1)
    return dh1, do, dg_pre_terms, dg_post_terms


def _input_norm_bwd(dxn, x, dh1, g):
    da, dg_terms = _rms_bwd(x, g, dxn)
    return dh1 + da, dg_terms


def _sgu_norm(z_tile, g, b):
    gz = _gelu(z_tile)
    u, vv = gz[:, :SGU_W], gz[:, SGU_W:]
    xc = vv - jnp.mean(vv, axis=-1, keepdims=True)
    rstd = lax.rsqrt(jnp.mean(xc * xc, axis=-1, keepdims=True) + EPS)
    xhat = xc * rstd
    return u, xhat, rstd, xhat * g + b


def _sgu_mix(w_ref, v_bf, first_half):
    parts = []
    for p in range(N_GROUPS // 2):
        vp = v_bf[:, p * LANES:(p + 1) * LANES]
        parts.append(jnp.where(first_half, _dot(w_ref[2 * p], vp), _dot(w_ref[2 * p + 1], vp)))
    return jnp.concatenate(parts, axis=1)


def _sgu_fwd(z, g_sgu, b_sgu, ws, bias_plane, *, tm=512):
    T = z.shape[0]
    tm = min(tm, T)

    def body(z_ref, g_ref, b_ref, ws_ref, bp_ref, y_ref):
        u, _, _, vn = _sgu_norm(z_ref[...], g_ref[...], b_ref[...])
        vn_bf = vn.astype(BF)
        first_half = lax.broadcasted_iota(jnp.int32, (CHUNK, LANES), 1) < HEAD_DIM
        for c in range(tm // CHUNK):
            rows = slice(c * CHUNK, (c + 1) * CHUNK)
            s = _sgu_mix(ws_ref, vn_bf[rows, :], first_half) + bp_ref[...]
            y_ref[rows, :] = (u[rows, :] * s).astype(BF)

    return pl.pallas_call(
        body, name="sgu_fwd", grid=(T // tm,),
        in_specs=[_row_spec(tm, 2 * SGU_W), _vec_spec(SGU_W), _vec_spec(SGU_W),
                  pl.BlockSpec((N_GROUPS, CHUNK, CHUNK), lambda i: (0, 0, 0)),
                  pl.BlockSpec((CHUNK, SGU_W), lambda i: (0, 0))],
        out_specs=_row_spec(tm, SGU_W), out_shape=jax.ShapeDtypeStruct((T, SGU_W), BF),
        compiler_params=_params("parallel"),
    )(z, g_sgu, b_sgu, ws, bias_plane)


def _sgu_bwd(dy, z, g_sgu, b_sgu, ws, ws_t, bias_plane, exchange, *, tm=512):
    T = z.shape[0]
    tm = min(tm, T)
    n_steps = T // tm
    k = len(exchange)

    def body(dy_ref, z_ref, g_ref, b_ref, ws_ref, wst_ref, bp_ref, *refs):
        x_refs, (dz_ref, dws_ref, dbs_ref, dg_ref, db_ref), r_refs = refs[:k], refs[k:k + 5], refs[k + 5:2 * k + 5]
        dbp_ref, send_sems, recv_sems = refs[2 * k + 5:]
        step = pl.program_id(0)
        exchange_start, exchange_finish = _exchange_phases(x_refs, r_refs, send_sems, recv_sems)
        pl.when(step == 0)(exchange_start)

        @pl.when(step == 0)
        def _():
            dws_ref[...] = jnp.zeros_like(dws_ref)
            dg_ref[...] = jnp.zeros_like(dg_ref)
            db_ref[...] = jnp.zeros_like(db_ref)
            dbp_ref[...] = jnp.zeros_like(dbp_ref)

        g = g_ref[...]
        zt = z_ref[...]
        u, xhat, rstd, vn = _sgu_norm(zt, g, b_ref[...])
        vn_bf = vn.astype(BF)
        first_half = lax.broadcasted_iota(jnp.int32, (CHUNK, LANES), 1) < HEAD_DIM
        dyv = dy_ref[...]
        dg_acc = jnp.zeros((1, SGU_W), F32)
        db_acc = jnp.zeros((1, SGU_W), F32)
        for c in range(tm // CHUNK):
            rows = slice(c * CHUNK, (c + 1) * CHUNK)
            v_c = vn_bf[rows, :]
            s = _sgu_mix(ws_ref, v_c, first_half) + bp_ref[...]
            dy_c = dyv[rows, :]
            du = dy_c * s
            dsv = dy_c * u[rows, :]
            dbp_ref[...] += dsv
            ds_bf = dsv.astype(BF)
            zero = jnp.zeros((CHUNK, LANES), BF)
            for p in range(N_GROUPS // 2):
                dsp = ds_bf[:, p * LANES:(p + 1) * LANES]
                vp = v_c[:, p * LANES:(p + 1) * LANES]
                dws_ref[2 * p] += _dot_nt(jnp.where(first_half, dsp, zero), vp)
                dws_ref[2 * p + 1] += _dot_nt(jnp.where(first_half, zero, dsp), vp)
            dvn = _sgu_mix(wst_ref, ds_bf, first_half)
            xh = xhat[rows, :]
            dxh = dvn * g
            dvv = rstd[rows, :] * (dxh - jnp.mean(dxh, axis=-1, keepdims=True)
                                   - xh * jnp.mean(dxh * xh, axis=-1, keepdims=True))
            dg_acc += jnp.sum(dvn * xh, axis=0, keepdims=True)
            db_acc += jnp.sum(dvn, axis=0, keepdims=True)
            dgz = jnp.concatenate([du, dvv], axis=1)
            dz_ref[rows, :] = (dgz * _gelu_grad(zt[rows, :])).astype(BF)
        dg_ref[...] += dg_acc
        db_ref[...] += db_acc

        @pl.when(step == n_steps - 1)
        def _():
            r = lax.broadcasted_iota(jnp.int32, (CHUNK, CHUNK), 0)
            cidx = lax.broadcasted_iota(jnp.int32, (CHUNK, CHUNK), 1)
            causal = (cidx <= r).astype(F32)
            for gi in range(N_GROUPS):
                dws_ref[gi] = dws_ref[gi] * causal
            lane = lax.broadcasted_iota(jnp.int32, (CHUNK, LANES), 1)
            out = jnp.zeros((CHUNK, LANES), F32)
            dbp = dbp_ref[...]
            for gi in range(N_GROUPS):
                col = jnp.sum(dbp[:, gi * HEAD_DIM:(gi + 1) * HEAD_DIM], axis=1, keepdims=True)
                out = jnp.where(lane == gi, col, out)
            dbs_ref[...] = out
            exchange_finish()

    w_spec = pl.BlockSpec((N_GROUPS, CHUNK, CHUNK), lambda i: (0, 0, 0))
    plane = pl.BlockSpec((CHUNK, SGU_W), lambda i: (0, 0))
    outs = pl.pallas_call(
        body, name="sgu_bwd", grid=(n_steps,),
        in_specs=[_row_spec(tm, SGU_W), _row_spec(tm, 2 * SGU_W), _vec_spec(SGU_W), _vec_spec(SGU_W), w_spec, w_spec, plane]
        + [HBM] * k,
        out_specs=[_row_spec(tm, 2 * SGU_W), w_spec, pl.BlockSpec((CHUNK, LANES), lambda i: (0, 0)),
                   _vec_spec(SGU_W), _vec_spec(SGU_W)] + [HBM] * k,
        out_shape=[jax.ShapeDtypeStruct((T, 2 * SGU_W), BF), jax.ShapeDtypeStruct((N_GROUPS, CHUNK, CHUNK), F32),
                   jax.ShapeDtypeStruct((CHUNK, LANES), F32), jax.ShapeDtypeStruct((1, SGU_W), F32),
                   jax.ShapeDtypeStruct((1, SGU_W), F32)] + _exchanged_shapes(exchange),
        scratch_shapes=[pltpu.VMEM((CHUNK, SGU_W), F32)] + _exchange_semaphores(k),
        compiler_params=_params("arbitrary"),
    )(dy, z, g_sgu, b_sgu, ws, ws_t, bias_plane, *exchange)
    return outs[:5], outs[5:]


def _tri(n, upper):
    r = lax.broadcasted_iota(jnp.int32, (n, n), 0)
    c = lax.broadcasted_iota(jnp.int32, (n, n), 1)
    return ((c >= r) if upper else (c <= r)).astype(BF)


def _scan_dot(tri, x):
    hi, mid, lo = _split3(x)
    return (_dot(tri, hi.astype(BF)) + _dot(tri, mid.astype(BF))) + _dot(tri, lo.astype(BF))


def _with_lanes(base, lane, start, cols):
    out = base
    for k, col in enumerate(cols):
        if col is not None:
            out = jnp.where(lane == start + k, col, out)
    return out


def _logit_bound(q_norm, k_norm):
    return NORM_SLACK * q_norm * k_norm + 1.0


ATTN_TILE = 512
SKIP_BELOW = -110.0
NORM_SLACK = 1.001
BOUNDED_GAP = 60.0


def _attn_prep(qkv, fl, b_forget, *, tp=ATTN_TILE):
    T = qkv.shape[0]
    tp = min(tp, T)
    head_sum, gather6, place_q, place_k, place_v = (jnp.asarray(m, BF) for m in _attn_placements())

    def body(qkv_ref, fl_ref, bf_ref, hs_ref, g6_ref, pq_ref, pk_ref, pv_ref, qf_ref, kl_ref, vl_ref, st_ref, carry_ref, kmax_ref):
        @pl.when(pl.program_id(0) == 0)
        def _():
            carry_ref[...] = jnp.zeros_like(carry_ref)
            kmax_ref[...] = jnp.zeros_like(kmax_ref)

        x = fl_ref[...] + bf_ref[...]
        logf = jnp.minimum(x, 0.0) - jnp.log(1.0 + jnp.exp(-jnp.abs(x)))
        cum = _scan_dot(_tri(tp, upper=False), logf) + carry_ref[...]
        carry_ref[...] = cum[tp - 1:tp, :]

        def head_norms(block):
            sq = block * block
            hi = sq.astype(BF)
            return _dot(hi, hs_ref[...]) + _dot((sq - hi.astype(F32)).astype(BF), hs_ref[...])

        qkvv = qkv_ref[...]
        q_norm = NORM_SLACK * jnp.sqrt(head_norms(qkvv[:, :ATTN_W].astype(F32) * Q_SCALE))
        kn = NORM_SLACK * jnp.sqrt(jnp.max(head_norms(qkvv[:, ATTN_W:2 * ATTN_W].astype(F32)), axis=0, keepdims=True))
        k_seen = jnp.maximum(kmax_ref[...], kn)
        kmax_ref[...] = k_seen
        rows = (jnp.max(q_norm, axis=0, keepdims=True), kn, jnp.max(cum, axis=0, keepdims=True),
                jnp.min(cum, axis=0, keepdims=True), k_seen)
        st_ref[...] = jnp.zeros_like(st_ref)
        for k, row in enumerate(rows):
            st_ref[0, k:k + 1, :] = row
        parts = jnp.concatenate([p.astype(BF) for p in _split3(cum) + _split3(-_logit_bound(q_norm, k_seen))], axis=1)
        lane = lax.broadcasted_iota(jnp.int32, (tp, LANES), 1)
        side = jnp.where(lane == 6 * N_HEADS, 1.0, _dot(parts, g6_ref[...])).astype(BF)
        for h in range(N_HEADS):
            pair = slice((h // 2) * LANES, (h // 2 + 1) * LANES)
            for out_ref, block, place_ref in ((qf_ref, qkvv[:, :ATTN_W], pq_ref), (kl_ref, qkvv[:, ATTN_W:2 * ATTN_W], pk_ref),
                                              (vl_ref, qkvv[:, 2 * ATTN_W:], pv_ref)):
                out_ref[h] = _dot(jnp.concatenate([block[:, pair], side], axis=1), place_ref[h]).astype(BF)

    head_spec = pl.BlockSpec((N_HEADS, tp, LANES), lambda i: (0, i, 0))
    whole = lambda a: pl.BlockSpec(a.shape, lambda i: (0,) * a.ndim)
    return pl.pallas_call(
        body, name="attn_prep", grid=(T // tp,),
        in_specs=[_row_spec(tp, 3 * ATTN_W), _row_spec(tp, LANES), _vec_spec(LANES)]
        + [whole(m) for m in (head_sum, gather6, place_q, place_k, place_v)],
        out_specs=[head_spec] * 3 + [pl.BlockSpec((1, N_HEADS, LANES), lambda i: (i, 0, 0))],
        out_shape=[jax.ShapeDtypeStruct((N_HEADS, T, LANES), BF)] * 3 + [jax.ShapeDtypeStruct((T // tp, N_HEADS, LANES), F32)],
        scratch_shapes=[pltpu.VMEM((1, LANES), F32), pltpu.VMEM((1, LANES), F32)], compiler_params=_params("arbitrary"),
    )(qkv, fl, b_forget, head_sum, gather6, place_q, place_k, place_v)


def _attn_placements():
    head_sum = np.zeros((ATTN_W, LANES), np.float32)
    head_sum[np.arange(ATTN_W), np.arange(ATTN_W) // HEAD_DIM] = 1.0
    gather6 = np.zeros((6 * LANES, LANES), np.float32)
    for j in range(6):
        gather6[j * LANES + np.arange(N_HEADS), j * N_HEADS + np.arange(N_HEADS)] = 1.0
    place = np.zeros((3, N_HEADS, 2 * LANES, LANES), np.float32)
    one = LANES + 6 * N_HEADS
    d = np.arange(HEAD_DIM)
    for h in range(N_HEADS):
        side = lambda j: LANES + j * N_HEADS + h
        place[0, h, (h % 2) * HEAD_DIM + d, d] = Q_SCALE
        place[1:, h, (h % 2) * HEAD_DIM + d, d] = 1.0
        for j in range(3):
            place[0, h, side(j), HEAD_DIM + j] = 1.0
            place[0, h, one, HEAD_DIM + 3 + j] = 1.0
            place[0, h, side(3 + j), HEAD_DIM + 6 + j] = 1.0
            place[1, h, one, HEAD_DIM + j] = 1.0
            place[1, h, side(j), HEAD_DIM + 3 + j] = -1.0
            place[1, h, one, HEAD_DIM + 6 + j] = 1.0
            place[2, h, one, HEAD_DIM + j] = 1.0
    return head_sum, gather6, place[0], place[1], place[2]


def _attn_ranges(stats):
    qn, kn, cmax, cmin, k_seen = (stats[:, k, :N_HEADS].T for k in range(5))
    n = qn.shape[1]
    bounded = (2.0 * _logit_bound(qn, k_seen) <= BOUNDED_GAP).reshape(N_HEADS // 2, 2, n).all(axis=1)
    reach = NORM_SLACK * qn * (jnp.max(kn, axis=1, keepdims=True) + kn) + cmax
    i = jnp.arange(n)[None, :, None]
    j = jnp.arange(n)[None, None, :]
    need = ((reach[:, :, None] - cmin[:, None, :] >= SKIP_BELOW) | (i == j)) & (j <= i)
    first = jnp.min(jnp.where(need, j, n), axis=2).reshape(N_HEADS // 2, 2, n).min(axis=1)
    last = jnp.max(jnp.where(need, i, -1), axis=1).reshape(N_HEADS // 2, 2, n).max(axis=1)
    return first.reshape(-1).astype(F32), last.reshape(-1).astype(F32), bounded.reshape(-1).astype(F32)


def _pair_block(t):
    return pl.BlockSpec((2, t, LANES), lambda p, i, *_: (p, i, 0))


def _pair_full(T):
    return pl.BlockSpec((2, T, LANES), lambda p, i, *_: (p, 0, 0))


def _packed_block(t):
    return pl.BlockSpec((t, LANES), lambda p, i, *_: (i, p))


def _causal(t, keys_in_rows=False):
    r = lax.broadcasted_iota(jnp.int32, (t, t), 0)
    c = lax.broadcasted_iota(jnp.int32, (t, t), 1)
    return (r <= c) if keys_in_rows else (c <= r)


def _tile_rows(j, t):
    return pl.ds(pl.multiple_of(j * t, t), t)


def _attn_call(body, name, tile_scalars, operands, in_specs, out_specs, out_shape, scratch_shapes, n_tiles):
    return pl.pallas_call(
        body, name=name,
        grid_spec=pltpu.PrefetchScalarGridSpec(
            num_scalar_prefetch=len(tile_scalars), grid=(N_HEADS // 2, n_tiles), in_specs=in_specs, out_specs=out_specs,
            scratch_shapes=scratch_shapes),
        out_shape=out_shape, compiler_params=_params("arbitrary", "arbitrary"),
    )(*tile_scalars, *operands)


def _attn_fwd(qf, kl, vl, first, bounded, shards, *, tq=ATTN_TILE):
    T = qf.shape[1]
    tq = min(tq, T)
    n = T // tq
    n_steps = (N_HEADS // 2) * n
    k = len(shards)

    def body(first_ref, bounded_ref, qf_ref, kl_ref, vl_ref, *refs):
        w_refs, (o_ref, of_ref, ql_ref), g_refs = refs[:k], refs[k:k + 3], refs[k + 3:2 * k + 3]
        m_ref, acc_ref, send_sems, recv_sems = refs[2 * k + 3:]
        i = pl.program_id(1)
        tile = pl.program_id(0) * n + i
        gather_start, gather_forward, gather_finish = _gather_phases(w_refs, g_refs, send_sems, recv_sems)
        pl.when(tile == 0)(gather_start)
        pl.when(tile == (3 * n_steps) // 4)(gather_forward)
        start = first_ref[tile].astype(jnp.int32)
        is_bounded = bounded_ref[tile] > 0.5
        acc_ref[...] = jnp.zeros_like(acc_ref)
        diagonal = _tile_rows(i, tq)
        causal = _causal(tq)

        def logits(hh, rows):
            return _dot_nt(qf_ref[hh], kl_ref[hh, rows, :])

        @pl.when(is_bounded)
        def _():
            m_ref[...] = jnp.zeros_like(m_ref)

            def update(hh, s, rows):
                acc_ref[hh] += _dot(jnp.exp(s).astype(BF), vl_ref[hh, rows, :])

            def step(j, carry):
                for hh in range(2):
                    update(hh, logits(hh, _tile_rows(j, tq)), _tile_rows(j, tq))
                return carry

            lax.fori_loop(start, i, step, 0)
            for hh in range(2):
                update(hh, jnp.where(causal, logits(hh, diagonal), NEG), diagonal)

        @pl.when(jnp.logical_not(is_bounded))
        def _():
            m_ref[...] = jnp.full_like(m_ref, NEG)

            def update(hh, s, rows):
                m_old = m_ref[hh]
                m_new = jnp.maximum(m_old, jnp.max(s, axis=1, keepdims=True))
                p = jnp.exp(s - m_new)
                acc_ref[hh] = jnp.exp(m_old - m_new) * acc_ref[hh] + _dot(p.astype(BF), vl_ref[hh, rows, :])
                m_ref[hh] = m_new

            def step(j, carry):
                for hh in range(2):
                    update(hh, logits(hh, _tile_rows(j, tq)), _tile_rows(j, tq))
                return carry

            lax.fori_loop(start, i, step, 0)
            for hh in range(2):
                update(hh, jnp.where(causal, logits(hh, diagonal), NEG), diagonal)

        lane = lax.broadcasted_iota(jnp.int32, (tq, LANES), 1)
        outs = []
        for hh in range(2):
            q = qf_ref[hh].astype(F32)
            acc = acc_ref[hh]
            l = acc[:, HEAD_DIM:HEAD_DIM + 1]
            outs.append(acc[:, :HEAD_DIM] / l)
            at = HEAD_DIM + 6
            neg_bound = (q[:, at:at + 1] + q[:, at + 1:at + 2]) + q[:, at + 2:at + 3]
            ql_ref[hh] = _with_lanes(q, lane, at, _split3(neg_bound - (m_ref[hh] + jnp.log(l)))).astype(BF)
        o = jnp.concatenate(outs, axis=1)
        o_ref[...] = o.astype(BF)
        of_ref[...] = o
        pl.when(tile == n_steps - 1)(gather_finish)

    outs = _attn_call(
        body, "attn_fwd", (first, bounded), (qf, kl, vl, *shards),
        [_pair_block(tq), _pair_full(T), _pair_full(T)] + [HBM] * k,
        [_packed_block(tq), _packed_block(tq), _pair_block(tq)] + [HBM] * k,
        [jax.ShapeDtypeStruct((T, ATTN_W), BF), jax.ShapeDtypeStruct((T, ATTN_W), F32),
         jax.ShapeDtypeStruct((N_HEADS, T, LANES), BF)] + _gathered_shapes(shards),
        [pltpu.VMEM((2, tq, 1), F32), pltpu.VMEM((2, tq, LANES), F32)] + _gather_semaphores(k), n)
    return outs[0], outs[1], outs[2], outs[3:]


def _attn_bwd_prep(dya, of, *, tr=ATTN_TILE):
    T = dya.shape[0]
    tr = min(tr, T)
    head_sum, gather6 = (jnp.asarray(m, BF) for m in _attn_placements()[:2])
    gather3, place_do = gather6[:3 * LANES], _delta_placement()

    def body(d_ref, o_ref, hs_ref, g3_ref, p_ref, do_ref):
        dv = d_ref[...]
        delta = sum(_dot(part.astype(BF), hs_ref[...]) for part in _split3(dv * o_ref[...]))
        side = _dot(jnp.concatenate([p.astype(BF) for p in _split3(-delta)], axis=1), g3_ref[...]).astype(BF)
        d_bf = dv.astype(BF)
        for h in range(N_HEADS):
            pair = slice((h // 2) * LANES, (h // 2 + 1) * LANES)
            do_ref[h] = _dot(jnp.concatenate([d_bf[:, pair], side], axis=1), p_ref[h]).astype(BF)

    whole = lambda a: pl.BlockSpec(a.shape, lambda i: (0,) * a.ndim)
    return pl.pallas_call(
        body, name="attn_bwd_prep", grid=(T // tr,),
        in_specs=[_row_spec(tr, ATTN_W), _row_spec(tr, ATTN_W), whole(head_sum), whole(gather3), whole(place_do)],
        out_specs=pl.BlockSpec((N_HEADS, tr, LANES), lambda i: (0, i, 0)),
        out_shape=jax.ShapeDtypeStruct((N_HEADS, T, LANES), BF), compiler_params=_params("parallel"),
    )(dya, of, head_sum, gather3, place_do)


def _delta_placement():
    place = np.zeros((N_HEADS, 2 * LANES, LANES), np.float32)
    d = np.arange(HEAD_DIM)
    for h in range(N_HEADS):
        place[h, (h % 2) * HEAD_DIM + d, d] = 1.0
        for j in range(3):
            place[h, LANES + j * N_HEADS + h, HEAD_DIM + j] = 1.0
    return jnp.asarray(place, BF)


def _attn_bwd(kl, vl, ql, do, last, chip_sums, *, tk=ATTN_TILE):
    T = ql.shape[1]
    tk = min(tk, T)
    n = T // tk
    n_steps = (N_HEADS // 2) * n
    m = len(chip_sums)

    def body(last_ref, kl_ref, vl_ref, ql_ref, do_ref, *refs):
        b_refs, (dq_ref, dk_ref, dv_ref, extq_ref, extk_ref), r_refs = refs[:m], refs[m:m + 5], refs[m + 5:2 * m + 5]
        dq_acc, dk_acc, dv_acc, send_sems, recv_sems = refs[2 * m + 5:]
        j = pl.program_id(1)
        tile = pl.program_id(0) * n + j
        scatter_start, scatter_finish = _scatter_phases(b_refs, r_refs, send_sems, recv_sems)
        pl.when(tile == 0)(scatter_start)

        @pl.when(j == 0)
        def _():
            dq_acc[...] = jnp.zeros_like(dq_acc)

        dk_acc[...] = jnp.zeros_like(dk_acc)
        dv_acc[...] = jnp.zeros_like(dv_acc)

        def block(hh, rows, mask):
            qi, di, k = ql_ref[hh, rows, :], do_ref[hh, rows, :], kl_ref[hh]
            p_t = jnp.exp(_dot_nt(k, qi))
            if mask is not None:
                p_t = jnp.where(mask, p_t, 0.0)
            ds_t = (p_t * _dot_nt(vl_ref[hh], di)).astype(BF)
            dk_acc[hh] += _dot(ds_t, qi)
            dv_acc[hh] += _dot(p_t.astype(BF), di)
            dq_acc[hh, rows, :] += _dot_tn(ds_t, k)

        causal_t = _causal(tk, keys_in_rows=True)
        for hh in range(2):
            block(hh, _tile_rows(j, tk), causal_t)

        def step(i, carry):
            for hh in range(2):
                block(hh, _tile_rows(i, tk), None)
            return carry

        lax.fori_loop(j + 1, last_ref[pl.program_id(0) * n + j].astype(jnp.int32) + 1, step, 0)
        dk_ref[...] = jnp.concatenate([dk_acc[hh][:, :HEAD_DIM] for hh in range(2)], axis=1).astype(BF)
        dv_ref[...] = jnp.concatenate([dv_acc[hh][:, :HEAD_DIM] for hh in range(2)], axis=1).astype(BF)
        extk_ref[...] = jnp.concatenate([dk_acc[hh][:, HEAD_DIM:] for hh in range(2)], axis=1)

        @pl.when(j == n - 1)
        def _():
            dq_ref[...] = jnp.concatenate([dq_acc[hh][:, :HEAD_DIM] * Q_SCALE for hh in range(2)], axis=1).astype(BF)
            extq_ref[...] = jnp.concatenate([dq_acc[hh][:, HEAD_DIM:] for hh in range(2)], axis=1)

        pl.when(tile == n_steps - 1)(scatter_finish)

    whole = pl.BlockSpec((T, LANES), lambda p, j, *_: (0, p))
    outs = pl.pallas_call(
        body, name="attn_bwd",
        grid_spec=pltpu.PrefetchScalarGridSpec(
            num_scalar_prefetch=1, grid=(N_HEADS // 2, n),
            in_specs=[_pair_block(tk), _pair_block(tk), _pair_full(T), _pair_full(T)] + [HBM] * m,
            out_specs=[whole, _packed_block(tk), _packed_block(tk), whole, _packed_block(tk)] + [HBM] * m,
            scratch_shapes=[pltpu.VMEM((2, T, LANES), F32), pltpu.VMEM((2, tk, LANES), F32), pltpu.VMEM((2, tk, LANES), F32)]
            + _scatter_semaphores(m)),
        out_shape=[jax.ShapeDtypeStruct((T, ATTN_W), BF)] * 3 + [jax.ShapeDtypeStruct((T, ATTN_W), F32)] * 2
        + _scattered_shapes(chip_sums),
        compiler_params=pltpu.CompilerParams(dimension_semantics=("arbitrary", "arbitrary"), vmem_limit_bytes=BIG_VMEM),
    )(last, kl, vl, ql, do, *chip_sums)
    return outs[:5], outs[5:]


def _forget_bwd(ext_q, ext_k, fl, b_forget, *, tp=256):
    T = fl.shape[0]
    tp = min(tp, T)
    n = T // tp

    def body(eq_ref, ek_ref, fl_ref, bf_ref, dfl_ref, dbf_ref, carry_ref):
        @pl.when(pl.program_id(0) == 0)
        def _():
            carry_ref[...] = jnp.zeros_like(carry_ref)
            dbf_ref[...] = jnp.zeros_like(dbf_ref)

        lane = lax.broadcasted_iota(jnp.int32, (tp, LANES), 1)
        eq, ek = eq_ref[...], ek_ref[...]
        cols = [eq[:, h * HEAD_DIM:h * HEAD_DIM + 1] - ek[:, h * HEAD_DIM + 3:h * HEAD_DIM + 4] for h in range(N_HEADS)]
        dcum = _with_lanes(jnp.zeros((tp, LANES), F32), lane, 0, cols)
        suffix = _scan_dot(_tri(tp, upper=True), dcum) + carry_ref[...]
        carry_ref[...] = suffix[0:1, :]
        x = fl_ref[...] + bf_ref[...]
        dfl = jnp.where(lane < N_HEADS, suffix / (1.0 + jnp.exp(x)), 0.0)
        dfl_ref[...] = dfl.astype(BF)
        dbf_ref[...] += jnp.sum(dfl, axis=0, keepdims=True)

    rev = lambda w: pl.BlockSpec((tp, w), lambda i: (n - 1 - i, 0))
    return pl.pallas_call(
        body, name="forget_bwd", grid=(n,),
        in_specs=[rev(ATTN_W), rev(ATTN_W), rev(LANES), _vec_spec(LANES)],
        out_specs=[rev(LANES), _vec_spec(LANES)],
        out_shape=[jax.ShapeDtypeStruct((T, LANES), BF), jax.ShapeDtypeStruct((1, LANES), F32)],
        scratch_shapes=[pltpu.VMEM((1, LANES), F32)], compiler_params=_params("arbitrary"),
    )(ext_q, ext_k, fl, b_forget)


def _adamw(w, g, m, v, *, name, tr=256):
    _, rows, cols = w.shape
    tr = tr if rows % tr == 0 else rows

    def body(w_ref, g_ref, m_ref, v_ref, go_ref, d_ref, nm_ref, nv_ref):
        gv = g_ref[...]
        go_ref[...] = gv
        nm = ADAM_B1 * m_ref[...] + (1.0 - ADAM_B1) * gv
        nv = ADAM_B2 * v_ref[...] + (1.0 - ADAM_B2) * (gv * gv)
        m_hat = nm / (1.0 - ADAM_B1 ** ADAM_STEP)
        v_hat = nv / (1.0 - ADAM_B2 ** ADAM_STEP)
        d_ref[...] = -ADAM_LR * (m_hat / (jnp.sqrt(v_hat) + ADAM_EPS) + ADAM_WD * w_ref[...])
        nm_ref[...] = nm
        nv_ref[...] = nv

    spec = pl.BlockSpec((None, tr, cols), lambda i: (0, i, 0))
    return pl.pallas_call(
        body, name=name, grid=(rows // tr,), in_specs=[spec, pl.BlockSpec((tr, cols), lambda i: (i, 0)), spec, spec],
        out_specs=[spec] * 4, out_shape=[jax.ShapeDtypeStruct((1, rows, cols), F32)] * 4,
        compiler_params=_params("parallel"),
    )(w, g, m, v)


HBM = pl.BlockSpec(memory_space=pltpu.HBM)
BF16_ROWS = 16


def _place():
    x, y, c = lax.axis_index("x"), lax.axis_index("y"), lax.axis_index("c")
    others = [(1 - x, y), (x, 1 - y), (1 - x, 1 - y)]
    return x, y, c, others


def _chip(xy):
    return 2 * xy[0] + xy[1]


def _row_halves(c, rows):
    half = rows // 2
    assert half % BF16_ROWS == 0
    return (pl.ds(pl.multiple_of(c * half, BF16_ROWS), half), pl.ds(pl.multiple_of((1 - c) * half, BF16_ROWS), half))


def _remote(src, dst, send_sems, recv_sems, k, to):
    return pltpu.make_async_remote_copy(src_ref=src, dst_ref=dst, send_sem=send_sems.at[k], recv_sem=recv_sems.at[k],
                                        device_id=to, device_id_type=MESH)


def _gathered_shapes(shards):
    return [jax.ShapeDtypeStruct((N_CHIPS,) + s.shape, s.dtype) for s in shards]


def _gather_semaphores(n):
    return [pltpu.SemaphoreType.DMA((6 * n,)), pltpu.SemaphoreType.DMA((6 * n,))]


def _gather_phases(w_refs, g_refs, send_sems, recv_sems):
    n = len(w_refs)
    x, y, c, others = _place()
    sibling, me = (x, y, 1 - c), _chip((x, y))
    halves = [_row_halves(c, w.shape[0]) for w in w_refs]

    def sent(a, j, o):
        mine, _ = halves[a]
        return _remote(w_refs[a].at[mine, :], g_refs[a].at[me, mine, :], send_sems, recv_sems, 6 * a + j, (*o, c))

    def passed(a, j, o):
        landed = g_refs[a].at[_chip(o), halves[a][0], :]
        return _remote(landed, landed, send_sems, recv_sems, 6 * a + 3 + j, sibling)

    def start():
        for a in range(n):
            for j, o in enumerate(others):
                sent(a, j, o).start()

    def forward():
        for j, o in enumerate(others):
            for a in range(n):
                landed = g_refs[a].at[_chip(o), halves[a][0], :]
                _remote(landed, landed, send_sems, recv_sems, 6 * a + j, (*o, c)).wait_recv()
                passed(a, j, o).start()

    def finish():
        for j, o in enumerate(others):
            for a in range(n):
                landed = g_refs[a].at[_chip(o), halves[a][1], :]
                _remote(landed, landed, send_sems, recv_sems, 6 * a + 3 + j, sibling).wait_recv()
        for a in range(n):
            for j, o in enumerate(others):
                sent(a, j, o).wait_send()
                passed(a, j, o).wait_send()

    return start, forward, finish


def _exchange_halves(arrays, *, name):
    n = len(arrays)

    def body(*refs):
        for phase in _exchange_phases(refs[:n], refs[n:2 * n], *refs[2 * n:]):
            phase()

    return pl.pallas_call(
        body, name=name, in_specs=[HBM] * n, out_specs=[HBM] * n, out_shape=_exchanged_shapes(arrays),
        scratch_shapes=_exchange_semaphores(n),
    )(*arrays)


def _exchanged_shapes(arrays):
    return [jax.ShapeDtypeStruct(s.shape[:-2] + (s.shape[-2] // 2, s.shape[-1]), F32) for s in arrays]


def _exchange_semaphores(n):
    return [pltpu.SemaphoreType.DMA((n,)), pltpu.SemaphoreType.DMA((n,))]


def _exchange_phases(g_refs, r_refs, send_sems, recv_sems):
    x, y, c, _ = _place()

    def copy(a):
        _, theirs = _row_halves(c, g_refs[a].shape[-2])
        src = g_refs[a].at[:, theirs, :] if len(g_refs[a].shape) == 3 else g_refs[a].at[theirs, :]
        return _remote(src, r_refs[a], send_sems, recv_sems, a, (x, y, 1 - c))

    def start():
        for a in range(len(g_refs)):
            copy(a).start()

    def finish():
        for a in range(len(g_refs)):
            copy(a).wait()

    return start, finish


def _scatter_to_owners(chip_sums):
    n = len(chip_sums)

    def body(*refs):
        for phase in _scatter_phases(refs[:n], refs[n:2 * n], *refs[2 * n:]):
            phase()

    return pl.pallas_call(
        body, name="scatter_to_owners", in_specs=[HBM] * n, out_specs=[HBM] * n,
        out_shape=_scattered_shapes(chip_sums), scratch_shapes=_scatter_semaphores(n),
    )(*chip_sums)


def _scattered_shapes(chip_sums):
    return [jax.ShapeDtypeStruct(b.shape if b.ndim == 3 else (N_CHIPS,) + b.shape, b.dtype) for b in chip_sums]


def _scatter_semaphores(n):
    return [pltpu.SemaphoreType.DMA((3 * n,)), pltpu.SemaphoreType.DMA((3 * n,))]


def _scatter_phases(b_refs, r_refs, send_sems, recv_sems):
    n = len(b_refs)
    x, y, c, others = _place()
    me = _chip((x, y))

    def sent(a, j, o):
        src = b_refs[a].at[_chip(o)] if len(b_refs[a].shape) == 3 else b_refs[a]
        return _remote(src, r_refs[a].at[me], send_sems, recv_sems, 3 * a + j, (*o, c))

    def start():
        for a in range(n):
            for j, o in enumerate(others):
                sent(a, j, o).start()

    def finish():
        for a in range(n):
            for j, o in enumerate(others):
                landed = r_refs[a].at[_chip(o)]
                _remote(landed, landed, send_sems, recv_sems, 3 * a + j, (*o, c)).wait_recv()
        for a in range(n):
            for j, o in enumerate(others):
                sent(a, j, o).wait_send()

    return start, finish


def _join_halves(totals):
    n = len(totals)

    def body(*refs):
        in_refs, out_refs, (send_sems, recv_sems) = refs[:n], refs[n:2 * n], refs[2 * n:]
        x, y, c, _ = _place()
        copies = []
        for a in range(n):
            mine, _ = _row_halves(c, in_refs[a].shape[0])
            copies.append(_remote(in_refs[a].at[mine, :], out_refs[a].at[mine, :], send_sems, recv_sems, a, (x, y, 1 - c)))
            copies[-1].start()
        for cp in copies:
            cp.wait()

    return pl.pallas_call(
        body, name="join_halves", in_specs=[HBM] * n, out_specs=[HBM] * n,
        out_shape=[jax.ShapeDtypeStruct(t.shape, F32) for t in totals], input_output_aliases={a: a for a in range(n)},
        scratch_shapes=[pltpu.SemaphoreType.DMA((n,)), pltpu.SemaphoreType.DMA((n,))],
    )(*totals)


ADD_ROWS = 128


def _add_sibling(g, r, place, *, name):
    lead, (half, cols) = g.shape[:-2], r.shape[-2:]
    tr = min(ADD_ROWS, half)
    nb = half // tr
    zeros = (0,) * len(lead)

    def body(place_ref, g_ref, r_ref, o_ref, ob_ref):
        s = g_ref[...] + r_ref[...]
        o_ref[...] = s
        ob_ref[...] = s.astype(BF)

    spec = pl.BlockSpec(lead + (tr, cols), lambda i, p: zeros + (i, 0))
    return pl.pallas_call(
        body, name=name,
        grid_spec=pltpu.PrefetchScalarGridSpec(
            num_scalar_prefetch=1, grid=(nb,),
            in_specs=[pl.BlockSpec(lead + (tr, cols), lambda i, p: zeros + (p[1] * nb + i, 0)), spec], out_specs=[spec, spec]),
        out_shape=[jax.ShapeDtypeStruct(r.shape, F32), jax.ShapeDtypeStruct(r.shape, BF)],
        compiler_params=_params("parallel"),
    )(place, g, r)


def _add_chips(own, received, place, *, name, own_slots):
    half, cols = received.shape[-2:]
    tr = min(ADD_ROWS, half)
    nb = half // tr

    def written(k, p):
        return jnp.where(p[0] == k, (k + 1) % N_CHIPS, k)

    def body(place_ref, own_ref, *refs):
        o_ref = refs[N_CHIPS]
        mine = own_ref[0] if own_slots else own_ref[...]
        if own_slots:
            acc = mine
            for k in range(N_CHIPS):
                acc = acc + jnp.where(place_ref[0] == k, 0.0, refs[k][0].astype(F32))
        else:
            terms = [jnp.where(place_ref[0] == k, mine, refs[k][0]) for k in range(N_CHIPS)]
            acc = ((terms[0] + terms[1]) + terms[2]) + terms[3]
        o_ref[...] = acc

    own_spec = (pl.BlockSpec((1, tr, cols), lambda i, p: (p[0], i, 0)) if own_slots
                else pl.BlockSpec((tr, cols), lambda i, p: (i, 0)))
    return pl.pallas_call(
        body, name=name,
        grid_spec=pltpu.PrefetchScalarGridSpec(
            num_scalar_prefetch=1, grid=(nb,),
            in_specs=[own_spec] + [pl.BlockSpec((1, tr, cols), functools.partial(lambda i, p, k: (written(k, p), i, 0), k=k))
                                   for k in range(N_CHIPS)],
            out_specs=pl.BlockSpec((tr, cols), lambda i, p: (p[1] * nb + i, 0))),
        out_shape=jax.ShapeDtypeStruct((2 * half, cols), F32), compiler_params=_params("parallel"),
    )(place, own, *([received] * N_CHIPS))


SHARDED = (("w_in", (D_MODEL, 4616), 1), ("w_branch_sgu", (SGU_W, D_MODEL), 1), ("w_branch_attn", (ATTN_W, D_MODEL), 1),
           ("w_out", (D_MODEL, D_MODEL), 0), ("w_up", (D_MODEL, D_FF), 1), ("w_down", (D_FF, D_MODEL), 0))
SMALL = (("g_mix_pre", (1, D_MODEL)), ("b_forget", (1, N_HEADS)), ("g_sgu", (1, SGU_W)), ("b_sgu", (1, SGU_W)),
         ("w_spatial", (N_GROUPS * CHUNK, CHUNK)), ("b_spatial", (N_GROUPS, CHUNK)), ("g_mix_post", (1, D_MODEL)),
         ("g_ffn_pre", (1, D_MODEL)), ("g_ffn_post", (1, D_MODEL)))
SMALL_ALIGN = 2 * ADD_ROWS


def _shard_shape(shape, axis):
    return tuple(s // N_CHIPS if a == axis else s for a, s in enumerate(shape))


def _slots_to_full(slots, axis):
    return slots.reshape(-1, slots.shape[2]) if axis == 0 else slots.transpose(1, 0, 2).reshape(slots.shape[1], -1)


def _full_to_slots(full, axis):
    if axis == 0:
        return full.reshape(N_CHIPS, -1, full.shape[1])
    return full.reshape(full.shape[0], N_CHIPS, -1).transpose(1, 0, 2)


def _small_rows(shape):
    return -(-(shape[0] * shape[1]) // (8 * LANES)) * 8


def _pack_small(values):
    parts = []
    for name, shape in SMALL:
        flat = values[name].reshape(-1)
        n = _small_rows(shape)
        parts.append(jnp.pad(flat, (0, n * LANES - flat.shape[0])).reshape(n, LANES))
    rows = sum(p.shape[0] for p in parts)
    pad = -(-rows // SMALL_ALIGN) * SMALL_ALIGN - rows
    return jnp.concatenate(parts + [jnp.zeros((pad, LANES), F32)], axis=0)


def _unpack_small(packed):
    out, row = {}, 0
    for name, shape in SMALL:
        n = _small_rows(shape)
        out[name] = packed[row:row + n].reshape(-1)[:shape[0] * shape[1]].reshape(shape)
        row += n
    return out


IN_Z, IN_Q, IN_K, IN_V, IN_F, IN_G, IN_END = 0, 1024, 1536, 2048, 2560, 2568, 4616


LATE_WEIGHTS = ("w_branch_sgu", "w_branch_attn", "w_out", "w_up", "w_down")
EARLY_GRADS = LATE_WEIGHTS


def _with_own_slot(shard, gathered, chip):
    return jnp.where(jnp.arange(N_CHIPS)[:, None, None] == chip, shard[None], gathered)


def _assemble(name, shard, gathered, chip):
    axis = {n: a for n, _, a in SHARDED}[name]
    return _slots_to_full(_with_own_slot(shard, gathered, chip), axis)


def _columns_from_slots(slots, bounds):
    width = slots.shape[2]
    pieces = []
    for lo, hi in zip(bounds[:-1], bounds[1:], strict=True):
        parts = [slots[k][:, max(lo, k * width) - k * width:min(hi, (k + 1) * width) - k * width]
                 for k in range(N_CHIPS) if max(lo, k * width) < min(hi, (k + 1) * width)]
        pieces.append(parts[0] if len(parts) == 1 else jnp.concatenate(parts, axis=1))
    return pieces


def _columns_to_slots(pieces):
    width = sum(p.shape[1] for p in pieces) // N_CHIPS
    slots = []
    for k in range(N_CHIPS):
        parts, start = [], 0
        for p in pieces:
            lo, hi = max(k * width, start), min((k + 1) * width, start + p.shape[1])
            if lo < hi:
                parts.append(p[:, lo - start:hi - start])
            start += p.shape[1]
        slots.append(jnp.concatenate(parts, axis=1))
    return jnp.stack(slots)


def _local_step(x, target, shards, small, place):
    b_forget = jnp.pad(small["b_forget"], ((0, 0), (0, LANES - N_HEADS)))
    causal = jnp.tril(jnp.ones((CHUNK, CHUNK), bool))
    ws = jnp.where(causal[None], small["w_spatial"].reshape(N_GROUPS, CHUNK, CHUNK), 0.0).astype(BF)
    ws_t = ws.transpose(0, 2, 1)
    bias_plane = jnp.repeat(small["b_spatial"].T, HEAD_DIM, axis=1)

    xn, (w_in_slots,) = _rms_fwd(x, small["g_mix_pre"], [shards["w_in"]])
    w_z, w_q, w_k, w_v, w_f, w_ga, w_gb = _columns_from_slots(
        _with_own_slot(shards["w_in"], w_in_slots, place[0]), (IN_Z, IN_Q, IN_K, IN_V, IN_F, IN_G, IN_G + D_MODEL, IN_END))
    w_qkv, w_g = jnp.concatenate([w_q, w_k, w_v], axis=1), jnp.concatenate([w_ga, w_gb], axis=1)
    w_f = jnp.pad(w_f, ((0, 0), (0, LANES - N_HEADS)))
    z, qkv, gl, fl = _project(xn, [w_z, w_qkv, w_g, w_f], [F32, BF, BF, F32], name="proj_in")
    ysgu = _sgu_fwd(z, small["g_sgu"], small["b_sgu"], ws, bias_plane)
    qf, kl, vl, tile_stats = _attn_prep(qkv, fl, b_forget)
    first_key_tile, last_query_tile, bounded = _attn_ranges(tile_stats)
    yattn, yattn_f, ql, gathered = _attn_fwd(qf, kl, vl, first_key_tile, bounded, [shards[name] for name in LATE_WEIGHTS])
    w = {name: _assemble(name, shards[name], got, place[0]) for name, got in zip(LATE_WEIGHTS, gathered, strict=True)}
    a, b, merged = _branch_merge(ysgu, yattn, w["w_branch_sgu"], w["w_branch_attn"], gl)
    o, h1, xn2 = _matmul_rows(
        [(merged, w["w_out"])], nt=False, rows=[x], vecs=[small["g_mix_post"], small["g_ffn_pre"]], row_outs=[F32, F32, BF],
        n_sums=0, epilogue=_mixer_out_fwd, name="proj_out_norms")

    (hid,) = _project(xn2, [w["w_up"]], [BF], name="ffn_up", tm=FFN_ROWS, epilogue=lambda acc: jnp.square(jnp.maximum(acc, 0.0)))
    dy, ddn, sq, dg_ffn_post = _matmul_rows(
        [(hid, w["w_down"])], nt=False, rows=[h1, target], vecs=[small["g_ffn_post"]], row_outs=[F32, BF], n_sums=2,
        epilogue=_loss_head, name="ffn_down_loss")

    (dup,) = _project(ddn, [w["w_down"]], [BF], name="ffn_down_bwd", tm=FFN_ROWS, nt=True, extra=hid,
                      epilogue=lambda acc, h: acc * (2.0 * jnp.sqrt(h.astype(F32))))
    dw_down = _matmul_tn(hid, ddn, name="dw_down")
    dh1, do, dg_ffn_pre, dg_mix_post = _matmul_rows(
        [(dup, w["w_up"])], nt=True, rows=[h1, dy, o], vecs=[small["g_ffn_pre"], small["g_mix_post"]], row_outs=[F32, BF],
        n_sums=2, epilogue=_mixer_out_bwd, name="ffn_up_bwd_norms")
    dw_up = _matmul_tn(xn2, dup, name="dw_up", slots=True)

    def gate_bwd(dm, a_t, b_t, gla, glb):
        ga, gb = jax.nn.sigmoid(gla.astype(F32)), jax.nn.sigmoid(glb.astype(F32))
        return dm * ga, dm * gb, dm * a_t.astype(F32) * (ga * (1.0 - ga)), dm * b_t.astype(F32) * (gb * (1.0 - gb))

    da, db, dgla, dglb = _matmul([(do, w["w_out"])], nt=True, out_dtypes=[BF] * 4, name="proj_out_bwd",
                                 epilogue=gate_bwd, extras=[a, b, (gl, 0), (gl, D_MODEL)])
    dw_out = _matmul_tn(merged, do, name="dw_out")
    dysgu = _matmul([(da, w["w_branch_sgu"])], nt=True, out_dtypes=[F32], name="branch_sgu_bwd")
    dyattn = _matmul([(db, w["w_branch_attn"])], nt=True, out_dtypes=[F32], name="branch_attn_bwd")
    dw_bs = _matmul_tn(ysgu, da, name="dw_branch_sgu")
    dw_ba = _matmul_tn(yattn, db, name="dw_branch_attn")
    early = {"w_branch_sgu": _full_to_slots(dw_bs, 1), "w_branch_attn": _full_to_slots(dw_ba, 1),
             "w_out": _full_to_slots(dw_out, 0), "w_up": dw_up, "w_down": _full_to_slots(dw_down, 0)}
    (dz, dws, dbs, dg_sgu, db_sgu), early_theirs = _sgu_bwd(
        dysgu, z, small["g_sgu"], small["b_sgu"], ws, ws_t, bias_plane, [early[name] for name in EARLY_GRADS])
    early_sums = {name: _add_sibling(early[name], theirs, place, name="add_sibling_" + name)
                  for name, theirs in zip(EARLY_GRADS, early_theirs, strict=True)}
    dout = _attn_bwd_prep(dyattn, yattn_f)
    (dq, dk, dv, ext_q, ext_k), early_received = _attn_bwd(
        kl, vl, ql, dout, last_query_tile, [early_sums[name][1] for name in EARLY_GRADS])
    dfl, dbf = _forget_bwd(ext_q, ext_k, fl, b_forget)
    dw_z, dw_q, dw_k, dw_v, dw_f = _matmul_tn_multi(xn, [dz, dq, dk, dv, dfl], name="dw_in_mix")
    dw_ga, dw_gb = _matmul_tn_multi(xn, [dgla, dglb], name="dw_in_gates")
    dw_in = _columns_to_slots([dw_z, dw_q, dw_k, dw_v, dw_f[:, :N_HEADS], dw_ga, dw_gb])
    (dw_in_theirs,) = _exchange_halves([dw_in], name="exchange_halves_w_in")
    dw_in_sum = _add_sibling(dw_in, dw_in_theirs, place, name="add_sibling_w_in")
    dx, dg_mix_pre, dw_in_received = _matmul_rows(
        [(dz, w_z), (dq, w_q), (dk, w_k), (dv, w_v), (dgla, w_ga), (dglb, w_gb), (dfl, w_f)],
        nt=True, rows=[x, dh1], vecs=[small["g_mix_pre"]], row_outs=[F32], n_sums=1, epilogue=_input_norm_bwd,
        name="proj_in_bwd_norm", scatter=[dw_in_sum[1]])

    reduced = {name: (early_sums[name][0], got) for name, got in zip(EARLY_GRADS, early_received, strict=True)}
    reduced["w_in"] = (dw_in_sum[0], dw_in_received)
    small_grads = {"g_mix_pre": dg_mix_pre, "b_forget": dbf[:, :N_HEADS], "g_sgu": dg_sgu, "b_sgu": db_sgu,
                   "w_spatial": dws.reshape(N_GROUPS * CHUNK, CHUNK), "b_spatial": dbs[:, :N_GROUPS].T,
                   "g_mix_post": dg_mix_post, "g_ffn_pre": dg_ffn_pre, "g_ffn_post": dg_ffn_post}
    return sq, dx, reduced, small_grads


NAMES = ("g_mix_pre", "w_in", "b_forget", "g_sgu", "b_sgu", "w_spatial", "b_spatial", "w_branch_sgu", "w_branch_attn",
         "w_out", "g_mix_post", "g_ffn_pre", "w_up", "w_down", "g_ffn_post")


def kernel(x, g_mix_pre, w_in, b_forget, g_sgu, b_sgu, w_spatial, b_spatial, w_branch_sgu, w_branch_attn, w_out, g_mix_post, g_ffn_pre, w_up, w_down, g_ffn_post, loss_target, m_g_mix_pre, m_w_in, m_b_forget, m_g_sgu, m_b_sgu, m_w_spatial, m_b_spatial, m_w_branch_sgu, m_w_branch_attn, m_w_out, m_g_mix_post, m_g_ffn_pre, m_w_up, m_w_down, m_g_ffn_post, v_g_mix_pre, v_w_in, v_b_forget, v_g_sgu, v_b_sgu, v_w_spatial, v_b_spatial, v_w_branch_sgu, v_w_branch_attn, v_w_out, v_g_mix_post, v_g_ffn_pre, v_w_up, v_w_down, v_g_ffn_post):
    weights = dict(zip(NAMES, (g_mix_pre, w_in, b_forget, g_sgu, b_sgu, w_spatial, b_spatial, w_branch_sgu, w_branch_attn,
                               w_out, g_mix_post, g_ffn_pre, w_up, w_down, g_ffn_post), strict=True))
    first = dict(zip(NAMES, (m_g_mix_pre, m_w_in, m_b_forget, m_g_sgu, m_b_sgu, m_w_spatial, m_b_spatial, m_w_branch_sgu,
                             m_w_branch_attn, m_w_out, m_g_mix_post, m_g_ffn_pre, m_w_up, m_w_down, m_g_ffn_post), strict=True))
    second = dict(zip(NAMES, (v_g_mix_pre, v_w_in, v_b_forget, v_g_sgu, v_b_sgu, v_w_spatial, v_b_spatial, v_w_branch_sgu,
                              v_w_branch_attn, v_w_out, v_g_mix_post, v_g_ffn_pre, v_w_up, v_w_down, v_g_ffn_post), strict=True))
    shard_shapes = {name: _shard_shape(shape, axis) for name, shape, axis in SHARDED}
    small_shapes = dict(SMALL)
    view = lambda name, a: a.reshape(shard_shapes.get(name) or small_shapes[name])

    place = jnp.stack([2 * lax.axis_index("x") + lax.axis_index("y"), lax.axis_index("c")]).astype(jnp.int32)

    shards = {name: view(name, weights[name]).astype(BF) for name, _, _ in SHARDED}
    small = {name: view(name, weights[name]) for name, _ in SMALL}
    sq, dx, reduced, small_grads = _local_step(x[0], loss_target[0], shards, small, place)
    loss = lax.psum(0.5 * jnp.sum(sq) / D_MODEL, ("x", "y", "c"))

    small_mine = _pack_small(small_grads)
    (small_theirs,) = _exchange_halves([small_mine], name="exchange_halves_small")
    small_sum, _ = _add_sibling(small_mine, small_theirs, place, name="add_sibling_small")
    (small_received,) = _scatter_to_owners([small_sum])
    totals = {name: _add_chips(s, r, place, name="add_chips_" + name, own_slots=True) for name, (s, r) in reduced.items()}
    small_total = _add_chips(small_sum, small_received, place, name="add_chips_small", own_slots=False)
    joined = _join_halves([totals[name] for name, _, _ in SHARDED] + [small_total])
    grad = {**{name: g for (name, _, _), g in zip(SHARDED, joined[:-1], strict=True)}, **_unpack_small(joined[-1])}

    grad_out, delta, new_m, new_v = {}, {}, {}, {}
    for name in NAMES:
        rows, cols = grad[name].shape
        as_given = lambda a: a.reshape(1, rows, cols)
        grad_out[name], delta[name], new_m[name], new_v[name] = _adamw(
            as_given(weights[name]), grad[name], as_given(first[name]), as_given(second[name]), name="adamw_" + name)

    like = lambda d: [d[name].reshape(weights[name].shape) for name in NAMES]
    return (loss, dx[None], *like(grad_out), *like(delta), *like(new_m), *like(new_v))
```

```python
import functools

import jax
import jax.numpy as jnp
import numpy as np
from jax import lax
from jax.experimental import pallas as pl
from jax.experimental.pallas import tpu as pltpu

F32 = jnp.float32
BF = jnp.bfloat16
MESH = pl.DeviceIdType.MESH

D_MODEL = 1024
N_HEADS = 8
HEAD_DIM = 64
ATTN_W = N_HEADS * HEAD_DIM
SGU_W = 512
N_GROUPS = 8
CHUNK = 128
D_FF = 4096
EPS = 1e-6
Q_SCALE = HEAD_DIM ** -0.5
N_CHIPS = 4
LANES = 128

ADAM_LR = 0.001
ADAM_B1 = 0.9
ADAM_B2 = 0.999
ADAM_EPS = 1e-08
ADAM_WD = 0.01
ADAM_STEP = 10

VMEM_LIMIT = 48 * 1024 * 1024
BIG_VMEM = 58 * 1024 * 1024
NEG = -1e30


def _params(*sem):
    return pltpu.CompilerParams(dimension_semantics=sem, vmem_limit_bytes=VMEM_LIMIT)


def _dot(a, b):
    return jnp.dot(a, b, preferred_element_type=F32)


def _dot_nt(a, b):
    return lax.dot_general(a, b, (((1,), (1,)), ((), ())), preferred_element_type=F32)


def _dot_tn(a, b):
    return lax.dot_general(a, b, (((0,), (0,)), ((), ())), preferred_element_type=F32)


def _split3(c):
    hi = c.astype(BF).astype(F32)
    r = c - hi
    mid = r.astype(BF).astype(F32)
    lo = (r - mid).astype(BF).astype(F32)
    return hi, mid, lo


def _gelu(x):
    k = 0.7978845608028654
    return 0.5 * x * (1.0 + jnp.tanh(k * (x + 0.044715 * (x * x * x))))


def _gelu_grad(x):
    k = 0.7978845608028654
    x2 = x * x
    t = jnp.tanh(k * (x + 0.044715 * (x2 * x)))
    return 0.5 * (1.0 + t) + 0.5 * x * (1.0 - t * t) * (k * (1.0 + 3.0 * 0.044715 * x2))


def _rms_bwd(a, g, dy):
    r = lax.rsqrt(jnp.mean(a * a, axis=-1, keepdims=True) + EPS)
    n = a * r
    dn = dy * g
    da = r * (dn - n * jnp.mean(dn * n, axis=-1, keepdims=True))
    return da, dy * n


MM_ROWS = 1024
MM_COLS = 512
FFN_ROWS = 512

def _matmul(pairs, *, nt, out_dtypes, name, tm=MM_ROWS, tn=MM_COLS, epilogue=None, extras=()):
    n_pairs, n_extra = len(pairs), len(extras)
    M = pairs[0][0].shape[0]
    N = pairs[0][1].shape[0] if nt else pairs[0][1].shape[1]
    tm, tn = min(tm, M), min(tn, N)
    assert M % tm == 0 and N % tn == 0

    def body(*refs):
        acc = None
        for p in range(n_pairs):
            a_ref, b_ref = refs[2 * p], refs[2 * p + 1]
            d = _dot_nt(a_ref[...], b_ref[...]) if nt else _dot(a_ref[...], b_ref[...])
            acc = d if acc is None else acc + d
        e_refs = refs[2 * n_pairs:2 * n_pairs + n_extra]
        o_refs = refs[2 * n_pairs + n_extra:]
        outs = (acc,) if epilogue is None else epilogue(acc, *[e[...] for e in e_refs])
        for o_ref, o in zip(o_refs, outs, strict=True):
            o_ref[...] = o.astype(o_ref.dtype)

    in_specs, args = [], []
    for a, b in pairs:
        K = a.shape[1]
        in_specs.append(pl.BlockSpec((tm, K), lambda i, j: (i, 0)))
        in_specs.append(pl.BlockSpec((tn, K), lambda i, j: (j, 0)) if nt else pl.BlockSpec((K, tn), lambda i, j: (0, j)))
        args += [a, b]
    for e in extras:
        e, col = e if isinstance(e, tuple) else (e, 0)
        in_specs.append(pl.BlockSpec((tm, tn), functools.partial(lambda i, j, off: (i, j + off), off=col // tn)))
        args.append(e)
    outs = pl.pallas_call(
        body, name=name, grid=(M // tm, N // tn), in_specs=in_specs,
        out_specs=[pl.BlockSpec((tm, tn), lambda i, j: (i, j)) for _ in out_dtypes],
        out_shape=[jax.ShapeDtypeStruct((M, N), dt) for dt in out_dtypes],
        compiler_params=_params("parallel", "parallel"),
    )(*args)
    return outs if len(outs) > 1 else outs[0]


def _matmul_tn_multi(a, bs, *, name, tk=1024):
    T, K1 = a.shape
    tk = min(tk, T)
    n = len(bs)

    def body(a_ref, *refs):
        @pl.when(pl.program_id(0) == 0)
        def _():
            for o_ref in refs[n:]:
                o_ref[...] = jnp.zeros_like(o_ref)

        av = a_ref[...]
        for b_ref, o_ref in zip(refs[:n], refs[n:], strict=True):
            o_ref[...] += _dot_tn(av, b_ref[...])

    return pl.pallas_call(
        body, name=name, grid=(T // tk,),
        in_specs=[pl.BlockSpec((tk, K1), lambda k: (k, 0))] + [pl.BlockSpec((tk, b.shape[1]), lambda k: (k, 0)) for b in bs],
        out_specs=[pl.BlockSpec((K1, b.shape[1]), lambda k: (0, 0)) for b in bs],
        out_shape=[jax.ShapeDtypeStruct((K1, b.shape[1]), F32) for b in bs],
        compiler_params=pltpu.CompilerParams(dimension_semantics=("arbitrary",), vmem_limit_bytes=BIG_VMEM),
    )(a, *bs)


def _project(a, weights, out_dtypes, *, name, tm=512, nt=False, epilogue=None, extra=None):
    M, K = a.shape
    tm = min(tm, M)
    n = len(weights)
    widths = [w.shape[0] if nt else w.shape[1] for w in weights]
    extras = [] if extra is None else [extra]

    def body(a_ref, *refs):
        av = a_ref[...]
        w_refs, e_refs, o_refs = refs[:n], refs[n:n + len(extras)], refs[n + len(extras):]
        for w_ref, o_ref in zip(w_refs, o_refs, strict=True):
            acc = _dot_nt(av, w_ref[...]) if nt else _dot(av, w_ref[...])
            if epilogue is not None:
                acc = epilogue(acc, *[e[...] for e in e_refs])
            o_ref[...] = acc.astype(o_ref.dtype)

    return pl.pallas_call(
        body, name=name, grid=(M // tm,),
        in_specs=[pl.BlockSpec((tm, K), lambda i: (i, 0))]
        + [pl.BlockSpec(w.shape, lambda i: (0, 0), pipeline_mode=pl.Buffered(1)) for w in weights]
        + [pl.BlockSpec((tm, e.shape[1]), lambda i: (i, 0)) for e in extras],
        out_specs=[pl.BlockSpec((tm, width), lambda i: (i, 0)) for width in widths],
        out_shape=[jax.ShapeDtypeStruct((M, width), dt) for width, dt in zip(widths, out_dtypes, strict=True)],
        compiler_params=_params("parallel"),
    )(a, *weights, *extras)


def _matmul_tn(a, b, *, name, tm=1024, tn=1024, tk=2048, slots=False):
    T, K1 = a.shape
    N = b.shape[1]
    tm, tn, tk = min(tm, K1), min(tn, N // N_CHIPS if slots else N), min(tk, T)
    assert K1 % tm == 0 and (N // N_CHIPS if slots else N) % tn == 0 and T % tk == 0
    per_slot = N // N_CHIPS // tn

    def body(a_ref, b_ref, o_ref):
        @pl.when(pl.program_id(2) == 0)
        def _():
            o_ref[...] = jnp.zeros_like(o_ref)

        o_ref[...] += _dot_tn(a_ref[...], b_ref[...])

    if slots:
        out_spec = pl.BlockSpec((None, tm, tn), lambda i, j, k: (j // per_slot, i, j % per_slot))
        out_shape = jax.ShapeDtypeStruct((N_CHIPS, K1, N // N_CHIPS), F32)
    else:
        out_spec = pl.BlockSpec((tm, tn), lambda i, j, k: (i, j))
        out_shape = jax.ShapeDtypeStruct((K1, N), F32)
    return pl.pallas_call(
        body, name=name, grid=(K1 // tm, N // tn, T // tk),
        in_specs=[pl.BlockSpec((tk, tm), lambda i, j, k: (k, i)), pl.BlockSpec((tk, tn), lambda i, j, k: (k, j))],
        out_specs=out_spec, out_shape=out_shape,
        compiler_params=_params("parallel", "parallel", "arbitrary"),
    )(a, b)


def _branch_merge(ysgu, yattn, w_bs, w_ba, gl, *, tm=MM_ROWS, tn=MM_COLS):
    T = ysgu.shape[0]
    tm = min(tm, T)
    nj = D_MODEL // tn

    def body(ys_ref, ya_ref, wbs_ref, wba_ref, gla_ref, glb_ref, a_ref, b_ref, m_ref):
        a = _dot(ys_ref[...], wbs_ref[...])
        b = _dot(ya_ref[...], wba_ref[...])
        a_ref[...] = a.astype(BF)
        b_ref[...] = b.astype(BF)
        m_ref[...] = (jax.nn.sigmoid(gla_ref[...].astype(F32)) * a + jax.nn.sigmoid(glb_ref[...].astype(F32)) * b).astype(BF)

    return pl.pallas_call(
        body, name="branch_merge", grid=(T // tm, nj),
        in_specs=[
            pl.BlockSpec((tm, SGU_W), lambda i, j: (i, 0)),
            pl.BlockSpec((tm, ATTN_W), lambda i, j: (i, 0)),
            pl.BlockSpec((SGU_W, tn), lambda i, j: (0, j)),
            pl.BlockSpec((ATTN_W, tn), lambda i, j: (0, j)),
            pl.BlockSpec((tm, tn), lambda i, j: (i, j)),
            pl.BlockSpec((tm, tn), lambda i, j: (i, j + nj)),
        ],
        out_specs=[pl.BlockSpec((tm, tn), lambda i, j: (i, j))] * 3,
        out_shape=[jax.ShapeDtypeStruct((T, D_MODEL), BF)] * 3,
        compiler_params=_params("parallel", "parallel"),
    )(ysgu, yattn, w_bs, w_ba, gl, gl)


def _row_spec(tr, width):
    return pl.BlockSpec((tr, width), lambda i: (i, 0))


def _vec_spec(width):
    return pl.BlockSpec((1, width), lambda i: (0, 0))


def _rms_fwd(x, g, shards, *, tr=256):
    T = x.shape[0]
    tr = min(tr, T)
    n_steps = T // tr
    k = len(shards)

    def body(x_ref, g_ref, *refs):
        step = pl.program_id(0)
        gather_start, gather_forward, gather_finish = _gather_phases(refs[:k], refs[k + 1:2 * k + 1], *refs[2 * k + 1:])
        pl.when(step == 0)(gather_start)
        pl.when(step == (3 * n_steps) // 4)(gather_forward)
        xv = x_ref[...]
        r = lax.rsqrt(jnp.mean(xv * xv, axis=-1, keepdims=True) + EPS)
        refs[k][...] = ((xv * r) * g_ref[...]).astype(BF)
        pl.when(step == n_steps - 1)(gather_finish)

    outs = pl.pallas_call(
        body, name="rms_fwd", grid=(n_steps,),
        in_specs=[_row_spec(tr, D_MODEL), _vec_spec(D_MODEL)] + [HBM] * k, out_specs=[_row_spec(tr, D_MODEL)] + [HBM] * k,
        out_shape=[jax.ShapeDtypeStruct((T, D_MODEL), BF)] + _gathered_shapes(shards),
        scratch_shapes=_gather_semaphores(k), compiler_params=_params("arbitrary"),
    )(x, g, *shards)
    return outs[0], outs[1:]


def _mixer_out_fwd(o, x, g_post, g_pre):
    r = lax.rsqrt(jnp.mean(o * o, axis=-1, keepdims=True) + EPS)
    h1 = x + (o * r) * g_post
    r2 = lax.rsqrt(jnp.mean(h1 * h1, axis=-1, keepdims=True) + EPS)
    return o, h1, (h1 * r2) * g_pre


def _matmul_rows(pairs, *, nt, rows, vecs, row_outs, n_sums, epilogue, name, tm=512, scatter=()):
    M = pairs[0][0].shape[0]
    N = pairs[0][1].shape[0] if nt else pairs[0][1].shape[1]
    tm = min(tm, M)
    n_steps = M // tm
    n_pairs, n_rows, n_vecs, n_out, n_scatter = len(pairs), len(rows), len(vecs), len(row_outs), len(scatter)

    def body(*refs):
        groups, at = [], 2 * n_pairs
        for count in (n_rows, n_vecs, n_scatter, n_out, n_sums, n_scatter):
            groups.append(refs[at:at + count])
            at += count
        r_refs, v_refs, b_refs, o_refs, s_refs, got_refs = groups
        sems = refs[at:]
        step = pl.program_id(0)
        if n_scatter:
            scatter_start, scatter_finish = _scatter_phases(b_refs, got_refs, *sems)
            pl.when(step == 0)(scatter_start)

        @pl.when(step == 0)
        def _():
            for s_ref in s_refs:
                s_ref[...] = jnp.zeros_like(s_ref)

        acc = None
        for p in range(n_pairs):
            a_ref, b_ref = refs[2 * p], refs[2 * p + 1]
            d = _dot_nt(a_ref[...], b_ref[...]) if nt else _dot(a_ref[...], b_ref[...])
            acc = d if acc is None else acc + d
        outs = epilogue(acc, *[r[...] for r in r_refs], *[v[...] for v in v_refs])
        for o_ref, o in zip(o_refs, outs[:n_out], strict=True):
            o_ref[...] = o.astype(o_ref.dtype)
        for s_ref, term in zip(s_refs, outs[n_out:], strict=True):
            s_ref[...] += jnp.sum(term, axis=0, keepdims=True)
        if n_scatter:
            pl.when(step == n_steps - 1)(scatter_finish)

    in_specs, args = [], []
    for a, b in pairs:
        in_specs += [_row_spec(tm, a.shape[1]), pl.BlockSpec(b.shape, lambda i: (0, 0))]
        args += [a, b]
    outs = pl.pallas_call(
        body, name=name, grid=(n_steps,),
        in_specs=in_specs + [_row_spec(tm, N)] * n_rows + [_vec_spec(N)] * n_vecs + [HBM] * n_scatter,
        out_specs=[_row_spec(tm, N)] * n_out + [_vec_spec(N)] * n_sums + [HBM] * n_scatter,
        out_shape=[jax.ShapeDtypeStruct((M, N), dt) for dt in row_outs] + [jax.ShapeDtypeStruct((1, N), F32)] * n_sums
        + (_scattered_shapes(scatter) if n_scatter else []),
        scratch_shapes=_scatter_semaphores(n_scatter) if n_scatter else [],
        compiler_params=pltpu.CompilerParams(dimension_semantics=("arbitrary",), vmem_limit_bytes=BIG_VMEM),
    )(*args, *rows, *vecs, *scatter)
    return outs


def _loss_head(dn, h1, target, g):
    r = lax.rsqrt(jnp.mean(dn * dn, axis=-1, keepdims=True) + EPS)
    err = h1 + (dn * r) * g - target
    dy = err * (1.0 / D_MODEL)
    ddn, dg_terms = _rms_bwd(dn, g, dy)
    return dy, ddn, err * err, dg_terms


def _mixer_out_bwd(dxn2, h1, dy, o, g_pre, g_post):
    da, dg_pre_terms = _rms_bwd(h1, g_pre, dxn2)
    dh1 = dy + da
    do, dg_post_terms = _rms_bwd(o, g_post, dh1)
    return dh1, do, dg_pre_terms, dg_post_terms


def _input_norm_bwd(dxn, x, dh1, g):
    da, dg_terms = _rms_bwd(x, g, dxn)
    return dh1 + da, dg_terms


def _sgu_norm(z_tile, g, b):
    gz = _gelu(z_tile)
    u, vv = gz[:, :SGU_W], gz[:, SGU_W:]
    xc = vv - jnp.mean(vv, axis=-1, keepdims=True)
    rstd = lax.rsqrt(jnp.mean(xc * xc, axis=-1, keepdims=True) + EPS)
    xhat = xc * rstd
    return u, xhat, rstd, xhat * g + b


def _sgu_mix(w_ref, v_bf, first_half):
    parts = []
    for p in range(N_GROUPS // 2):
        vp = v_bf[:, p * LANES:(p + 1) * LANES]
        parts.append(jnp.where(first_half, _dot(w_ref[2 * p], vp), _dot(w_ref[2 * p + 1], vp)))
    return jnp.concatenate(parts, axis=1)


def _sgu_fwd(z, g_sgu, b_sgu, ws, bias_plane, *, tm=512):
    T = z.shape[0]
    tm = min(tm, T)

    def body(z_ref, g_ref, b_ref, ws_ref, bp_ref, y_ref):
        u, _, _, vn = _sgu_norm(z_ref[...], g_ref[...], b_ref[...])
        vn_bf = vn.astype(BF)
        first_half = lax.broadcasted_iota(jnp.int32, (CHUNK, LANES), 1) < HEAD_DIM
        for c in range(tm // CHUNK):
            rows = slice(c * CHUNK, (c + 1) * CHUNK)
            s = _sgu_mix(ws_ref, vn_bf[rows, :], first_half) + bp_ref[...]
            y_ref[rows, :] = (u[rows, :] * s).astype(BF)

    return pl.pallas_call(
        body, name="sgu_fwd", grid=(T // tm,),
        in_specs=[_row_spec(tm, 2 * SGU_W), _vec_spec(SGU_W), _vec_spec(SGU_W),
                  pl.BlockSpec((N_GROUPS, CHUNK, CHUNK), lambda i: (0, 0, 0)),
                  pl.BlockSpec((CHUNK, SGU_W), lambda i: (0, 0))],
        out_specs=_row_spec(tm, SGU_W), out_shape=jax.ShapeDtypeStruct((T, SGU_W), BF),
        compiler_params=_params("parallel"),
    )(z, g_sgu, b_sgu, ws, bias_plane)


def _sgu_bwd(dy, z, g_sgu, b_sgu, ws, ws_t, bias_plane, exchange, *, tm=512):
    T = z.shape[0]
    tm = min(tm, T)
    n_steps = T // tm
    k = len(exchange)

    def body(dy_ref, z_ref, g_ref, b_ref, ws_ref, wst_ref, bp_ref, *refs):
        x_refs, (dz_ref, dws_ref, dbs_ref, dg_ref, db_ref), r_refs = refs[:k], refs[k:k + 5], refs[k + 5:2 * k + 5]
        dbp_ref, send_sems, recv_sems = refs[2 * k + 5:]
        step = pl.program_id(0)
        exchange_start, exchange_finish = _exchange_phases(x_refs, r_refs, send_sems, recv_sems)
        pl.when(step == 0)(exchange_start)

        @pl.when(step == 0)
        def _():
            dws_ref[...] = jnp.zeros_like(dws_ref)
            dg_ref[...] = jnp.zeros_like(dg_ref)
            db_ref[...] = jnp.zeros_like(db_ref)
            dbp_ref[...] = jnp.zeros_like(dbp_ref)

        g = g_ref[...]
        zt = z_ref[...]
        u, xhat, rstd, vn = _sgu_norm(zt, g, b_ref[...])
        vn_bf = vn.astype(BF)
        first_half = lax.broadcasted_iota(jnp.int32, (CHUNK, LANES), 1) < HEAD_DIM
        dyv = dy_ref[...]
        dg_acc = jnp.zeros((1, SGU_W), F32)
        db_acc = jnp.zeros((1, SGU_W), F32)
        for c in range(tm // CHUNK):
            rows = slice(c * CHUNK, (c + 1) * CHUNK)
            v_c = vn_bf[rows, :]
            s = _sgu_mix(ws_ref, v_c, first_half) + bp_ref[...]
            dy_c = dyv[rows, :]
            du = dy_c * s
            dsv = dy_c * u[rows, :]
            dbp_ref[...] += dsv
            ds_bf = dsv.astype(BF)
            zero = jnp.zeros((CHUNK, LANES), BF)
            for p in range(N_GROUPS // 2):
                dsp = ds_bf[:, p * LANES:(p + 1) * LANES]
                vp = v_c[:, p * LANES:(p + 1) * LANES]
                dws_ref[2 * p] += _dot_nt(jnp.where(first_half, dsp, zero), vp)
                dws_ref[2 * p + 1] += _dot_nt(jnp.where(first_half, zero, dsp), vp)
            dvn = _sgu_mix(wst_ref, ds_bf, first_half)
            xh = xhat[rows, :]
            dxh = dvn * g
            dvv = rstd[rows, :] * (dxh - jnp.mean(dxh, axis=-1, keepdims=True)
                                   - xh * jnp.mean(dxh * xh, axis=-1, keepdims=True))
            dg_acc += jnp.sum(dvn * xh, axis=0, keepdims=True)
            db_acc += jnp.sum(dvn, axis=0, keepdims=True)
            dgz = jnp.concatenate([du, dvv], axis=1)
            dz_ref[rows, :] = (dgz * _gelu_grad(zt[rows, :])).astype(BF)
        dg_ref[...] += dg_acc
        db_ref[...] += db_acc

        @pl.when(step == n_steps - 1)
        def _():
            r = lax.broadcasted_iota(jnp.int32, (CHUNK, CHUNK), 0)
            cidx = lax.broadcasted_iota(jnp.int32, (CHUNK, CHUNK), 1)
            causal = (cidx <= r).astype(F32)
            for gi in range(N_GROUPS):
                dws_ref[gi] = dws_ref[gi] * causal
            lane = lax.broadcasted_iota(jnp.int32, (CHUNK, LANES), 1)
            out = jnp.zeros((CHUNK, LANES), F32)
            dbp = dbp_ref[...]
            for gi in range(N_GROUPS):
                col = jnp.sum(dbp[:, gi * HEAD_DIM:(gi + 1) * HEAD_DIM], axis=1, keepdims=True)
                out = jnp.where(lane == gi, col, out)
            dbs_ref[...] = out
            exchange_finish()

    w_spec = pl.BlockSpec((N_GROUPS, CHUNK, CHUNK), lambda i: (0, 0, 0))
    plane = pl.BlockSpec((CHUNK, SGU_W), lambda i: (0, 0))
    outs = pl.pallas_call(
        body, name="sgu_bwd", grid=(n_steps,),
        in_specs=[_row_spec(tm, SGU_W), _row_spec(tm, 2 * SGU_W), _vec_spec(SGU_W), _vec_spec(SGU_W), w_spec, w_spec, plane]
        + [HBM] * k,
        out_specs=[_row_spec(tm, 2 * SGU_W), w_spec, pl.BlockSpec((CHUNK, LANES), lambda i: (0, 0)),
                   _vec_spec(SGU_W), _vec_spec(SGU_W)] + [HBM] * k,
        out_shape=[jax.ShapeDtypeStruct((T, 2 * SGU_W), BF), jax.ShapeDtypeStruct((N_GROUPS, CHUNK, CHUNK), F32),
                   jax.ShapeDtypeStruct((CHUNK, LANES), F32), jax.ShapeDtypeStruct((1, SGU_W), F32),
                   jax.ShapeDtypeStruct((1, SGU_W), F32)] + _exchanged_shapes(exchange),
        scratch_shapes=[pltpu.VMEM((CHUNK, SGU_W), F32)] + _exchange_semaphores(k),
        compiler_params=_params("arbitrary"),
    )(dy, z, g_sgu, b_sgu, ws, ws_t, bias_plane, *exchange)
    return outs[:5], outs[5:]


def _tri(n, upper):
    r = lax.broadcasted_iota(jnp.int32, (n, n), 0)
    c = lax.broadcasted_iota(jnp.int32, (n, n), 1)
    return ((c >= r) if upper else (c <= r)).astype(BF)


def _scan_dot(tri, x):
    hi, mid, lo = _split3(x)
    return (_dot(tri, hi.astype(BF)) + _dot(tri, mid.astype(BF))) + _dot(tri, lo.astype(BF))


def _with_lanes(base, lane, start, cols):
    out = base
    for k, col in enumerate(cols):
        if col is not None:
            out = jnp.where(lane == start + k, col, out)
    return out


def _logit_bound(q_norm, k_norm):
    return NORM_SLACK * q_norm * k_norm + 1.0


ATTN_TILE = 512
SKIP_BELOW = -110.0
NORM_SLACK = 1.001
BOUNDED_GAP = 60.0


def _attn_prep(qkv, fl, b_forget, *, tp=ATTN_TILE):
    T = qkv.shape[0]
    tp = min(tp, T)
    head_sum, gather6, place_q, place_k, place_v = (jnp.asarray(m, BF) for m in _attn_placements())

    def body(qkv_ref, fl_ref, bf_ref, hs_ref, g6_ref, pq_ref, pk_ref, pv_ref, qf_ref, kl_ref, vl_ref, st_ref, carry_ref, kmax_ref):
        @pl.when(pl.program_id(0) == 0)
        def _():
            carry_ref[...] = jnp.zeros_like(carry_ref)
            kmax_ref[...] = jnp.zeros_like(kmax_ref)

        x = fl_ref[...] + bf_ref[...]
        logf = jnp.minimum(x, 0.0) - jnp.log(1.0 + jnp.exp(-jnp.abs(x)))
        cum = _scan_dot(_tri(tp, upper=False), logf) + carry_ref[...]
        carry_ref[...] = cum[tp - 1:tp, :]

        def head_norms(block):
            sq = block * block
            hi = sq.astype(BF)
            return _dot(hi, hs_ref[...]) + _dot((sq - hi.astype(F32)).astype(BF), hs_ref[...])

        qkvv = qkv_ref[...]
        q_norm = NORM_SLACK * jnp.sqrt(head_norms(qkvv[:, :ATTN_W].astype(F32) * Q_SCALE))
        kn = NORM_SLACK * jnp.sqrt(jnp.max(head_norms(qkvv[:, ATTN_W:2 * ATTN_W].astype(F32)), axis=0, keepdims=True))
        k_seen = jnp.maximum(kmax_ref[...], kn)
        kmax_ref[...] = k_seen
        rows = (jnp.max(q_norm, axis=0, keepdims=True), kn, jnp.max(cum, axis=0, keepdims=True),
                jnp.min(cum, axis=0, keepdims=True), k_seen)
        st_ref[...] = jnp.zeros_like(st_ref)
        for k, row in enumerate(rows):
            st_ref[0, k:k + 1, :] = row
        parts = jnp.concatenate([p.astype(BF) for p in _split3(cum) + _split3(-_logit_bound(q_norm, k_seen))], axis=1)
        lane = lax.broadcasted_iota(jnp.int32, (tp, LANES), 1)
        side = jnp.where(lane == 6 * N_HEADS, 1.0, _dot(parts, g6_ref[...])).astype(BF)
        for h in range(N_HEADS):
            pair = slice((h // 2) * LANES, (h // 2 + 1) * LANES)
            for out_ref, block, place_ref in ((qf_ref, qkvv[:, :ATTN_W], pq_ref), (kl_ref, qkvv[:, ATTN_W:2 * ATTN_W], pk_ref),
                                              (vl_ref, qkvv[:, 2 * ATTN_W:], pv_ref)):
                out_ref[h] = _dot(jnp.concatenate([block[:, pair], side], axis=1), place_ref[h]).astype(BF)

    head_spec = pl.BlockSpec((N_HEADS, tp, LANES), lambda i: (0, i, 0))
    whole = lambda a: pl.BlockSpec(a.shape, lambda i: (0,) * a.ndim)
    return pl.pallas_call(
        body, name="attn_prep", grid=(T // tp,),
        in_specs=[_row_spec(tp, 3 * ATTN_W), _row_spec(tp, LANES), _vec_spec(LANES)]
        + [whole(m) for m in (head_sum, gather6, place_q, place_k, place_v)],
        out_specs=[head_spec] * 3 + [pl.BlockSpec((1, N_HEADS, LANES), lambda i: (i, 0, 0))],
        out_shape=[jax.ShapeDtypeStruct((N_HEADS, T, LANES), BF)] * 3 + [jax.ShapeDtypeStruct((T // tp, N_HEADS, LANES), F32)],
        scratch_shapes=[pltpu.VMEM((1, LANES), F32), pltpu.VMEM((1, LANES), F32)], compiler_params=_params("arbitrary"),
    )(qkv, fl, b_forget, head_sum, gather6, place_q, place_k, place_v)


def _attn_placements():
    head_sum = np.zeros((ATTN_W, LANES), np.float32)
    head_sum[np.arange(ATTN_W), np.arange(ATTN_W) // HEAD_DIM] = 1.0
    gather6 = np.zeros((6 * LANES, LANES), np.float32)
    for j in range(6):
        gather6[j * LANES + np.arange(N_HEADS), j * N_HEADS + np.arange(N_HEADS)] = 1.0
    place = np.zeros((3, N_HEADS, 2 * LANES, LANES), np.float32)
    one = LANES + 6 * N_HEADS
    d = np.arange(HEAD_DIM)
    for h in range(N_HEADS):
        side = lambda j: LANES + j * N_HEADS + h
        place[0, h, (h % 2) * HEAD_DIM + d, d] = Q_SCALE
        place[1:, h, (h % 2) * HEAD_DIM + d, d] = 1.0
        for j in range(3):
            place[0, h, side(j), HEAD_DIM + j] = 1.0
            place[0, h, one, HEAD_DIM + 3 + j] = 1.0
            place[0, h, side(3 + j), HEAD_DIM + 6 + j] = 1.0
            place[1, h, one, HEAD_DIM + j] = 1.0
            place[1, h, side(j), HEAD_DIM + 3 + j] = -1.0
            place[1, h, one, HEAD_DIM + 6 + j] = 1.0
            place[2, h, one, HEAD_DIM + j] = 1.0
    return head_sum, gather6, place[0], place[1], place[2]


def _attn_ranges(stats):
    qn, kn, cmax, cmin, k_seen = (stats[:, k, :N_HEADS].T for k in range(5))
    n = qn.shape[1]
    bounded = (2.0 * _logit_bound(qn, k_seen) <= BOUNDED_GAP).reshape(N_HEADS // 2, 2, n).all(axis=1)
    reach = NORM_SLACK * qn * (jnp.max(kn, axis=1, keepdims=True) + kn) + cmax
    i = jnp.arange(n)[None, :, None]
    j = jnp.arange(n)[None, None, :]
    need = ((reach[:, :, None] - cmin[:, None, :] >= SKIP_BELOW) | (i == j)) & (j <= i)
    first = jnp.min(jnp.where(need, j, n), axis=2).reshape(N_HEADS // 2, 2, n).min(axis=1)
    last = jnp.max(jnp.where(need, i, -1), axis=1).reshape(N_HEADS // 2, 2, n).max(axis=1)
    return first.reshape(-1).astype(F32), last.reshape(-1).astype(F32), bounded.reshape(-1).astype(F32)


def _pair_block(t):
    return pl.BlockSpec((2, t, LANES), lambda p, i, *_: (p, i, 0))


def _pair_full(T):
    return pl.BlockSpec((2, T, LANES), lambda p, i, *_: (p, 0, 0))


def _packed_block(t):
    return pl.BlockSpec((t, LANES), lambda p, i, *_: (i, p))


def _causal(t, keys_in_rows=False):
    r = lax.broadcasted_iota(jnp.int32, (t, t), 0)
    c = lax.broadcasted_iota(jnp.int32, (t, t), 1)
    return (r <= c) if keys_in_rows else (c <= r)


def _tile_rows(j, t):
    return pl.ds(pl.multiple_of(j * t, t), t)


def _attn_call(body, name, tile_scalars, operands, in_specs, out_specs, out_shape, scratch_shapes, n_tiles):
    return pl.pallas_call(
        body, name=name,
        grid_spec=pltpu.PrefetchScalarGridSpec(
            num_scalar_prefetch=len(tile_scalars), grid=(N_HEADS // 2, n_tiles), in_specs=in_specs, out_specs=out_specs,
            scratch_shapes=scratch_shapes),
        out_shape=out_shape, compiler_params=_params("arbitrary", "arbitrary"),
    )(*tile_scalars, *operands)


def _attn_fwd(qf, kl, vl, first, bounded, shards, *, tq=ATTN_TILE):
    T = qf.shape[1]
    tq = min(tq, T)
    n = T // tq
    n_steps = (N_HEADS // 2) * n
    k = len(shards)

    def body(first_ref, bounded_ref, qf_ref, kl_ref, vl_ref, *refs):
        w_refs, (o_ref, of_ref, ql_ref), g_refs = refs[:k], refs[k:k + 3], refs[k + 3:2 * k + 3]
        m_ref, acc_ref, send_sems, recv_sems = refs[2 * k + 3:]
        i = pl.program_id(1)
        tile = pl.program_id(0) * n + i
        gather_start, gather_forward, gather_finish = _gather_phases(w_refs, g_refs, send_sems, recv_sems)
        pl.when(tile == 0)(gather_start)
        pl.when(tile == (3 * n_steps) // 4)(gather_forward)
        start = first_ref[tile].astype(jnp.int32)
        is_bounded = bounded_ref[tile] > 0.5
        acc_ref[...] = jnp.zeros_like(acc_ref)
        diagonal = _tile_rows(i, tq)
        causal = _causal(tq)

        def logits(hh, rows):
            return _dot_nt(qf_ref[hh], kl_ref[hh, rows, :])

        @pl.when(is_bounded)
        def _():
            m_ref[...] = jnp.zeros_like(m_ref)

            def update(hh, s, rows):
                acc_ref[hh] += _dot(jnp.exp(s).astype(BF), vl_ref[hh, rows, :])

            def step(j, carry):
                for hh in range(2):
                    update(hh, logits(hh, _tile_rows(j, tq)), _tile_rows(j, tq))
                return carry

            lax.fori_loop(start, i, step, 0)
            for hh in range(2):
                update(hh, jnp.where(causal, logits(hh, diagonal), NEG), diagonal)

        @pl.when(jnp.logical_not(is_bounded))
        def _():
            m_ref[...] = jnp.full_like(m_ref, NEG)

            def update(hh, s, rows):
                m_old = m_ref[hh]
                m_new = jnp.maximum(m_old, jnp.max(s, axis=1, keepdims=True))
                p = jnp.exp(s - m_new)
                acc_ref[hh] = jnp.exp(m_old - m_new) * acc_ref[hh] + _dot(p.astype(BF), vl_ref[hh, rows, :])
                m_ref[hh] = m_new

            def step(j, carry):
                for hh in range(2):
                    update(hh, logits(hh, _tile_rows(j, tq)), _tile_rows(j, tq))
                return carry

            lax.fori_loop(start, i, step, 0)
            for hh in range(2):
                update(hh, jnp.where(causal, logits(hh, diagonal), NEG), diagonal)

        lane = lax.broadcasted_iota(jnp.int32, (tq, LANES), 1)
        outs = []
        for hh in range(2):
            q = qf_ref[hh].astype(F32)
            acc = acc_ref[hh]
            l = acc[:, HEAD_DIM:HEAD_DIM + 1]
            outs.append(acc[:, :HEAD_DIM] / l)
            at = HEAD_DIM + 6
            neg_bound = (q[:, at:at + 1] + q[:, at + 1:at + 2]) + q[:, at + 2:at + 3]
            ql_ref[hh] = _with_lanes(q, lane, at, _split3(neg_bound - (m_ref[hh] + jnp.log(l)))).astype(BF)
        o = jnp.concatenate(outs, axis=1)
        o_ref[...] = o.astype(BF)
        of_ref[...] = o
        pl.when(tile == n_steps - 1)(gather_finish)

    outs = _attn_call(
        body, "attn_fwd", (first, bounded), (qf, kl, vl, *shards),
        [_pair_block(tq), _pair_full(T), _pair_full(T)] + [HBM] * k,
        [_packed_block(tq), _packed_block(tq), _pair_block(tq)] + [HBM] * k,
        [jax.ShapeDtypeStruct((T, ATTN_W), BF), jax.ShapeDtypeStruct((T, ATTN_W), F32),
         jax.ShapeDtypeStruct((N_HEADS, T, LANES), BF)] + _gathered_shapes(shards),
        [pltpu.VMEM((2, tq, 1), F32), pltpu.VMEM((2, tq, LANES), F32)] + _gather_semaphores(k), n)
    return outs[0], outs[1], outs[2], outs[3:]


def _attn_bwd_prep(dya, of, *, tr=ATTN_TILE):
    T = dya.shape[0]
    tr = min(tr, T)
    head_sum, gather6 = (jnp.asarray(m, BF) for m in _attn_placements()[:2])
    gather3, place_do = gather6[:3 * LANES], _delta_placement()

    def body(d_ref, o_ref, hs_ref, g3_ref, p_ref, do_ref):
        dv = d_ref[...]
        delta = sum(_dot(part.astype(BF), hs_ref[...]) for part in _split3(dv * o_ref[...]))
        side = _dot(jnp.concatenate([p.astype(BF) for p in _split3(-delta)], axis=1), g3_ref[...]).astype(BF)
        d_bf = dv.astype(BF)
        for h in range(N_HEADS):
            pair = slice((h // 2) * LANES, (h // 2 + 1) * LANES)
            do_ref[h] = _dot(jnp.concatenate([d_bf[:, pair], side], axis=1), p_ref[h]).astype(BF)

    whole = lambda a: pl.BlockSpec(a.shape, lambda i: (0,) * a.ndim)
    return pl.pallas_call(
        body, name="attn_bwd_prep", grid=(T // tr,),
        in_specs=[_row_spec(tr, ATTN_W), _row_spec(tr, ATTN_W), whole(head_sum), whole(gather3), whole(place_do)],
        out_specs=pl.BlockSpec((N_HEADS, tr, LANES), lambda i: (0, i, 0)),
        out_shape=jax.ShapeDtypeStruct((N_HEADS, T, LANES), BF), compiler_params=_params("parallel"),
    )(dya, of, head_sum, gather3, place_do)


def _delta_placement():
    place = np.zeros((N_HEADS, 2 * LANES, LANES), np.float32)
    d = np.arange(HEAD_DIM)
    for h in range(N_HEADS):
        place[h, (h % 2) * HEAD_DIM + d, d] = 1.0
        for j in range(3):
            place[h, LANES + j * N_HEADS + h, HEAD_DIM + j] = 1.0
    return jnp.asarray(place, BF)


def _attn_bwd(kl, vl, ql, do, last, chip_sums, *, tk=ATTN_TILE):
    T = ql.shape[1]
    tk = min(tk, T)
    n = T // tk
    n_steps = (N_HEADS // 2) * n
    m = len(chip_sums)

    def body(last_ref, kl_ref, vl_ref, ql_ref, do_ref, *refs):
        b_refs, (dq_ref, dk_ref, dv_ref, extq_ref, extk_ref), r_refs = refs[:m], refs[m:m + 5], refs[m + 5:2 * m + 5]
        dq_acc, dk_acc, dv_acc, send_sems, recv_sems = refs[2 * m + 5:]
        j = pl.program_id(1)
        tile = pl.program_id(0) * n + j
        scatter_start, scatter_finish = _scatter_phases(b_refs, r_refs, send_sems, recv_sems)
        pl.when(tile == 0)(scatter_start)

        @pl.when(j == 0)
        def _():
            dq_acc[...] = jnp.zeros_like(dq_acc)

        dk_acc[...] = jnp.zeros_like(dk_acc)
        dv_acc[...] = jnp.zeros_like(dv_acc)

        def block(hh, rows, mask):
            qi, di, k = ql_ref[hh, rows, :], do_ref[hh, rows, :], kl_ref[hh]
            p_t = jnp.exp(_dot_nt(k, qi))
            if mask is not None:
                p_t = jnp.where(mask, p_t, 0.0)
            ds_t = (p_t * _dot_nt(vl_ref[hh], di)).astype(BF)
            dk_acc[hh] += _dot(ds_t, qi)
            dv_acc[hh] += _dot(p_t.astype(BF), di)
            dq_acc[hh, rows, :] += _dot_tn(ds_t, k)

        causal_t = _causal(tk, keys_in_rows=True)
        for hh in range(2):
            block(hh, _tile_rows(j, tk), causal_t)

        def step(i, carry):
            for hh in range(2):
                block(hh, _tile_rows(i, tk), None)
            return carry

        lax.fori_loop(j + 1, last_ref[pl.program_id(0) * n + j].astype(jnp.int32) + 1, step, 0)
        dk_ref[...] = jnp.concatenate([dk_acc[hh][:, :HEAD_DIM] for hh in range(2)], axis=1).astype(BF)
        dv_ref[...] = jnp.concatenate([dv_acc[hh][:, :HEAD_DIM] for hh in range(2)], axis=1).astype(BF)
        extk_ref[...] = jnp.concatenate([dk_acc[hh][:, HEAD_DIM:] for hh in range(2)], axis=1)

        @pl.when(j == n - 1)
        def _():
            dq_ref[...] = jnp.concatenate([dq_acc[hh][:, :HEAD_DIM] * Q_SCALE for hh in range(2)], axis=1).astype(BF)
            extq_ref[...] = jnp.concatenate([dq_acc[hh][:, HEAD_DIM:] for hh in range(2)], axis=1)

        pl.when(tile == n_steps - 1)(scatter_finish)

    whole = pl.BlockSpec((T, LANES), lambda p, j, *_: (0, p))
    outs = pl.pallas_call(
        body, name="attn_bwd",
        grid_spec=pltpu.PrefetchScalarGridSpec(
            num_scalar_prefetch=1, grid=(N_HEADS // 2, n),
            in_specs=[_pair_block(tk), _pair_block(tk), _pair_full(T), _pair_full(T)] + [HBM] * m,
            out_specs=[whole, _packed_block(tk), _packed_block(tk), whole, _packed_block(tk)] + [HBM] * m,
            scratch_shapes=[pltpu.VMEM((2, T, LANES), F32), pltpu.VMEM((2, tk, LANES), F32), pltpu.VMEM((2, tk, LANES), F32)]
            + _scatter_semaphores(m)),
        out_shape=[jax.ShapeDtypeStruct((T, ATTN_W), BF)] * 3 + [jax.ShapeDtypeStruct((T, ATTN_W), F32)] * 2
        + _scattered_shapes(chip_sums),
        compiler_params=pltpu.CompilerParams(dimension_semantics=("arbitrary", "arbitrary"), vmem_limit_bytes=BIG_VMEM),
    )(last, kl, vl, ql, do, *chip_sums)
    return outs[:5], outs[5:]


def _forget_bwd(ext_q, ext_k, fl, b_forget, *, tp=256):
    T = fl.shape[0]
    tp = min(tp, T)
    n = T // tp

    def body(eq_ref, ek_ref, fl_ref, bf_ref, dfl_ref, dbf_ref, carry_ref):
        @pl.when(pl.program_id(0) == 0)
        def _():
            carry_ref[...] = jnp.zeros_like(carry_ref)
            dbf_ref[...] = jnp.zeros_like(dbf_ref)

        lane = lax.broadcasted_iota(jnp.int32, (tp, LANES), 1)
        eq, ek = eq_ref[...], ek_ref[...]
        cols = [eq[:, h * HEAD_DIM:h * HEAD_DIM + 1] - ek[:, h * HEAD_DIM + 3:h * HEAD_DIM + 4] for h in range(N_HEADS)]
        dcum = _with_lanes(jnp.zeros((tp, LANES), F32), lane, 0, cols)
        suffix = _scan_dot(_tri(tp, upper=True), dcum) + carry_ref[...]
        carry_ref[...] = suffix[0:1, :]
        x = fl_ref[...] + bf_ref[...]
        dfl = jnp.where(lane < N_HEADS, suffix / (1.0 + jnp.exp(x)), 0.0)
        dfl_ref[...] = dfl.astype(BF)
        dbf_ref[...] += jnp.sum(dfl, axis=0, keepdims=True)

    rev = lambda w: pl.BlockSpec((tp, w), lambda i: (n - 1 - i, 0))
    return pl.pallas_call(
        body, name="forget_bwd", grid=(n,),
        in_specs=[rev(ATTN_W), rev(ATTN_W), rev(LANES), _vec_spec(LANES)],
        out_specs=[rev(LANES), _vec_spec(LANES)],
        out_shape=[jax.ShapeDtypeStruct((T, LANES), BF), jax.ShapeDtypeStruct((1, LANES), F32)],
        scratch_shapes=[pltpu.VMEM((1, LANES), F32)], compiler_params=_params("arbitrary"),
    )(ext_q, ext_k, fl, b_forget)


def _adamw(w, g, m, v, *, name, tr=256):
    _, rows, cols = w.shape
    tr = tr if rows % tr == 0 else rows

    def body(w_ref, g_ref, m_ref, v_ref, go_ref, d_ref, nm_ref, nv_ref):
        gv = g_ref[...]
        go_ref[...] = gv
        nm = ADAM_B1 * m_ref[...] + (1.0 - ADAM_B1) * gv
        nv = ADAM_B2 * v_ref[...] + (1.0 - ADAM_B2) * (gv * gv)
        m_hat = nm / (1.0 - ADAM_B1 ** ADAM_STEP)
        v_hat = nv / (1.0 - ADAM_B2 ** ADAM_STEP)
        d_ref[...] = -ADAM_LR * (m_hat / (jnp.sqrt(v_hat) + ADAM_EPS) + ADAM_WD * w_ref[...])
        nm_ref[...] = nm
        nv_ref[...] = nv

    spec = pl.BlockSpec((None, tr, cols), lambda i: (0, i, 0))
    return pl.pallas_call(
        body, name=name, grid=(rows // tr,), in_specs=[spec, pl.BlockSpec((tr, cols), lambda i: (i, 0)), spec, spec],
        out_specs=[spec] * 4, out_shape=[jax.ShapeDtypeStruct((1, rows, cols), F32)] * 4,
        compiler_params=_params("parallel"),
    )(w, g, m, v)


HBM = pl.BlockSpec(memory_space=pltpu.HBM)
BF16_ROWS = 16


def _place():
    x, y, c = lax.axis_index("x"), lax.axis_index("y"), lax.axis_index("c")
    others = [(1 - x, y), (x, 1 - y), (1 - x, 1 - y)]
    return x, y, c, others


def _chip(xy):
    return 2 * xy[0] + xy[1]


def _row_halves(c, rows):
    half = rows // 2
    assert half % BF16_ROWS == 0
    return (pl.ds(pl.multiple_of(c * half, BF16_ROWS), half), pl.ds(pl.multiple_of((1 - c) * half, BF16_ROWS), half))


def _remote(src, dst, send_sems, recv_sems, k, to):
    return pltpu.make_async_remote_copy(src_ref=src, dst_ref=dst, send_sem=send_sems.at[k], recv_sem=recv_sems.at[k],
                                        device_id=to, device_id_type=MESH)


def _gathered_shapes(shards):
    return [jax.ShapeDtypeStruct((N_CHIPS,) + s.shape, s.dtype) for s in shards]


def _gather_semaphores(n):
    return [pltpu.SemaphoreType.DMA((6 * n,)), pltpu.SemaphoreType.DMA((6 * n,))]


def _gather_phases(w_refs, g_refs, send_sems, recv_sems):
    n = len(w_refs)
    x, y, c, others = _place()
    sibling, me = (x, y, 1 - c), _chip((x, y))
    halves = [_row_halves(c, w.shape[0]) for w in w_refs]

    def sent(a, j, o):
        mine, _ = halves[a]
        return _remote(w_refs[a].at[mine, :], g_refs[a].at[me, mine, :], send_sems, recv_sems, 6 * a + j, (*o, c))

    def passed(a, j, o):
        landed = g_refs[a].at[_chip(o), halves[a][0], :]
        return _remote(landed, landed, send_sems, recv_sems, 6 * a + 3 + j, sibling)

    def start():
        for a in range(n):
            for j, o in enumerate(others):
                sent(a, j, o).start()

    def forward():
        for j, o in enumerate(others):
            for a in range(n):
                landed = g_refs[a].at[_chip(o), halves[a][0], :]
                _remote(landed, landed, send_sems, recv_sems, 6 * a + j, (*o, c)).wait_recv()
                passed(a, j, o).start()

    def finish():
        for j, o in enumerate(others):
            for a in range(n):
                landed = g_refs[a].at[_chip(o), halves[a][1], :]
                _remote(landed, landed, send_sems, recv_sems, 6 * a + 3 + j, sibling).wait_recv()
        for a in range(n):
            for j, o in enumerate(others):
                sent(a, j, o).wait_send()
                passed(a, j, o).wait_send()

    return start, forward, finish


def _exchange_halves(arrays, *, name):
    n = len(arrays)

    def body(*refs):
        for phase in _exchange_phases(refs[:n], refs[n:2 * n], *refs[2 * n:]):
            phase()

    return pl.pallas_call(
        body, name=name, in_specs=[HBM] * n, out_specs=[HBM] * n, out_shape=_exchanged_shapes(arrays),
        scratch_shapes=_exchange_semaphores(n),
    )(*arrays)


def _exchanged_shapes(arrays):
    return [jax.ShapeDtypeStruct(s.shape[:-2] + (s.shape[-2] // 2, s.shape[-1]), F32) for s in arrays]


def _exchange_semaphores(n):
    return [pltpu.SemaphoreType.DMA((n,)), pltpu.SemaphoreType.DMA((n,))]


def _exchange_phases(g_refs, r_refs, send_sems, recv_sems):
    x, y, c, _ = _place()

    def copy(a):
        _, theirs = _row_halves(c, g_refs[a].shape[-2])
        src = g_refs[a].at[:, theirs, :] if len(g_refs[a].shape) == 3 else g_refs[a].at[theirs, :]
        return _remote(src, r_refs[a], send_sems, recv_sems, a, (x, y, 1 - c))

    def start():
        for a in range(len(g_refs)):
            copy(a).start()

    def finish():
        for a in range(len(g_refs)):
            copy(a).wait()

    return start, finish


def _scatter_to_owners(chip_sums):
    n = len(chip_sums)

    def body(*refs):
        for phase in _scatter_phases(refs[:n], refs[n:2 * n], *refs[2 * n:]):
            phase()

    return pl.pallas_call(
        body, name="scatter_to_owners", in_specs=[HBM] * n, out_specs=[HBM] * n,
        out_shape=_scattered_shapes(chip_sums), scratch_shapes=_scatter_semaphores(n),
    )(*chip_sums)


def _scattered_shapes(chip_sums):
    return [jax.ShapeDtypeStruct(b.shape if b.ndim == 3 else (N_CHIPS,) + b.shape, b.dtype) for b in chip_sums]


def _scatter_semaphores(n):
    return [pltpu.SemaphoreType.DMA((3 * n,)), pltpu.SemaphoreType.DMA((3 * n,))]


def _scatter_phases(b_refs, r_refs, send_sems, recv_sems):
    n = len(b_refs)
    x, y, c, others = _place()
    me = _chip((x, y))

    def sent(a, j, o):
        src = b_refs[a].at[_chip(o)] if len(b_refs[a].shape) == 3 else b_refs[a]
        return _remote(src, r_refs[a].at[me], send_sems, recv_sems, 3 * a + j, (*o, c))

    def start():
        for a in range(n):
            for j, o in enumerate(others):
                sent(a, j, o).start()

    def finish():
        for a in range(n):
            for j, o in enumerate(others):
                landed = r_refs[a].at[_chip(o)]
                _remote(landed, landed, send_sems, recv_sems, 3 * a + j, (*o, c)).wait_recv()
        for a in range(n):
            for j, o in enumerate(others):
                sent(a, j, o).wait_send()

    return start, finish


def _join_halves(totals):
    n = len(totals)

    def body(*refs):
        in_refs, out_refs, (send_sems, recv_sems) = refs[:n], refs[n:2 * n], refs[2 * n:]
        x, y, c, _ = _place()
        copies = []
        for a in range(n):
            mine, _ = _row_halves(c, in_refs[a].shape[0])
            copies.append(_remote(in_refs[a].at[mine, :], out_refs[a].at[mine, :], send_sems, recv_sems, a, (x, y, 1 - c)))
            copies[-1].start()
        for cp in copies:
            cp.wait()

    return pl.pallas_call(
        body, name="join_halves", in_specs=[HBM] * n, out_specs=[HBM] * n,
        out_shape=[jax.ShapeDtypeStruct(t.shape, F32) for t in totals], input_output_aliases={a: a for a in range(n)},
        scratch_shapes=[pltpu.SemaphoreType.DMA((n,)), pltpu.SemaphoreType.DMA((n,))],
    )(*totals)


ADD_ROWS = 128


def _add_sibling(g, r, place, *, name):
    lead, (half, cols) = g.shape[:-2], r.shape[-2:]
    tr = min(ADD_ROWS, half)
    nb = half // tr
    zeros = (0,) * len(lead)

    def body(place_ref, g_ref, r_ref, o_ref, ob_ref):
        s = g_ref[...] + r_ref[...]
        o_ref[...] = s
        ob_ref[...] = s.astype(BF)

    spec = pl.BlockSpec(lead + (tr, cols), lambda i, p: zeros + (i, 0))
    return pl.pallas_call(
        body, name=name,
        grid_spec=pltpu.PrefetchScalarGridSpec(
            num_scalar_prefetch=1, grid=(nb,),
            in_specs=[pl.BlockSpec(lead + (tr, cols), lambda i, p: zeros + (p[1] * nb + i, 0)), spec], out_specs=[spec, spec]),
        out_shape=[jax.ShapeDtypeStruct(r.shape, F32), jax.ShapeDtypeStruct(r.shape, BF)],
        compiler_params=_params("parallel"),
    )(place, g, r)


def _add_chips(own, received, place, *, name, own_slots):
    half, cols = received.shape[-2:]
    tr = min(ADD_ROWS, half)
    nb = half // tr

    def written(k, p):
        return jnp.where(p[0] == k, (k + 1) % N_CHIPS, k)

    def body(place_ref, own_ref, *refs):
        o_ref = refs[N_CHIPS]
        mine = own_ref[0] if own_slots else own_ref[...]
        if own_slots:
            acc = mine
            for k in range(N_CHIPS):
                acc = acc + jnp.where(place_ref[0] == k, 0.0, refs[k][0].astype(F32))
        else:
            terms = [jnp.where(place_ref[0] == k, mine, refs[k][0]) for k in range(N_CHIPS)]
            acc = ((terms[0] + terms[1]) + terms[2]) + terms[3]
        o_ref[...] = acc

    own_spec = (pl.BlockSpec((1, tr, cols), lambda i, p: (p[0], i, 0)) if own_slots
                else pl.BlockSpec((tr, cols), lambda i, p: (i, 0)))
    return pl.pallas_call(
        body, name=name,
        grid_spec=pltpu.PrefetchScalarGridSpec(
            num_scalar_prefetch=1, grid=(nb,),
            in_specs=[own_spec] + [pl.BlockSpec((1, tr, cols), functools.partial(lambda i, p, k: (written(k, p), i, 0), k=k))
                                   for k in range(N_CHIPS)],
            out_specs=pl.BlockSpec((tr, cols), lambda i, p: (p[1] * nb + i, 0))),
        out_shape=jax.ShapeDtypeStruct((2 * half, cols), F32), compiler_params=_params("parallel"),
    )(place, own, *([received] * N_CHIPS))


SHARDED = (("w_in", (D_MODEL, 4616), 1), ("w_branch_sgu", (SGU_W, D_MODEL), 1), ("w_branch_attn", (ATTN_W, D_MODEL), 1),
           ("w_out", (D_MODEL, D_MODEL), 0), ("w_up", (D_MODEL, D_FF), 1), ("w_down", (D_FF, D_MODEL), 0))
SMALL = (("g_mix_pre", (1, D_MODEL)), ("b_forget", (1, N_HEADS)), ("g_sgu", (1, SGU_W)), ("b_sgu", (1, SGU_W)),
         ("w_spatial", (N_GROUPS * CHUNK, CHUNK)), ("b_spatial", (N_GROUPS, CHUNK)), ("g_mix_post", (1, D_MODEL)),
         ("g_ffn_pre", (1, D_MODEL)), ("g_ffn_post", (1, D_MODEL)))
SMALL_ALIGN = 2 * ADD_ROWS


def _shard_shape(shape, axis):
    return tuple(s // N_CHIPS if a == axis else s for a, s in enumerate(shape))


def _slots_to_full(slots, axis):
    return slots.reshape(-1, slots.shape[2]) if axis == 0 else slots.transpose(1, 0, 2).reshape(slots.shape[1], -1)


def _full_to_slots(full, axis):
    if axis == 0:
        return full.reshape(N_CHIPS, -1, full.shape[1])
    return full.reshape(full.shape[0], N_CHIPS, -1).transpose(1, 0, 2)


def _small_rows(shape):
    return -(-(shape[0] * shape[1]) // (8 * LANES)) * 8


def _pack_small(values):
    parts = []
    for name, shape in SMALL:
        flat = values[name].reshape(-1)
        n = _small_rows(shape)
        parts.append(jnp.pad(flat, (0, n * LANES - flat.shape[0])).reshape(n, LANES))
    rows = sum(p.shape[0] for p in parts)
    pad = -(-rows // SMALL_ALIGN) * SMALL_ALIGN - rows
    return jnp.concatenate(parts + [jnp.zeros((pad, LANES), F32)], axis=0)


def _unpack_small(packed):
    out, row = {}, 0
    for name, shape in SMALL:
        n = _small_rows(shape)
        out[name] = packed[row:row + n].reshape(-1)[:shape[0] * shape[1]].reshape(shape)
        row += n
    return out


IN_Z, IN_Q, IN_K, IN_V, IN_F, IN_G, IN_END = 0, 1024, 1536, 2048, 2560, 2568, 4616


LATE_WEIGHTS = ("w_branch_sgu", "w_branch_attn", "w_out", "w_up", "w_down")
EARLY_GRADS = LATE_WEIGHTS


def _with_own_slot(shard, gathered, chip):
    return jnp.where(jnp.arange(N_CHIPS)[:, None, None] == chip, shard[None], gathered)


def _assemble(name, shard, gathered, chip):
    axis = {n: a for n, _, a in SHARDED}[name]
    return _slots_to_full(_with_own_slot(shard, gathered, chip), axis)


def _columns_from_slots(slots, bounds):
    width = slots.shape[2]
    pieces = []
    for lo, hi in zip(bounds[:-1], bounds[1:], strict=True):
        parts = [slots[k][:, max(lo, k * width) - k * width:min(hi, (k + 1) * width) - k * width]
                 for k in range(N_CHIPS) if max(lo, k * width) < min(hi, (k + 1) * width)]
        pieces.append(parts[0] if len(parts) == 1 else jnp.concatenate(parts, axis=1))
    return pieces


def _columns_to_slots(pieces):
    width = sum(p.shape[1] for p in pieces) // N_CHIPS
    slots = []
    for k in range(N_CHIPS):
        parts, start = [], 0
        for p in pieces:
            lo, hi = max(k * width, start), min((k + 1) * width, start + p.shape[1])
            if lo < hi:
                parts.append(p[:, lo - start:hi - start])
            start += p.shape[1]
        slots.append(jnp.concatenate(parts, axis=1))
    return jnp.stack(slots)


def _local_step(x, target, shards, small, place):
    b_forget = jnp.pad(small["b_forget"], ((0, 0), (0, LANES - N_HEADS)))
    causal = jnp.tril(jnp.ones((CHUNK, CHUNK), bool))
    ws = jnp.where(causal[None], small["w_spatial"].reshape(N_GROUPS, CHUNK, CHUNK), 0.0).astype(BF)
    ws_t = ws.transpose(0, 2, 1)
    bias_plane = jnp.repeat(small["b_spatial"].T, HEAD_DIM, axis=1)

    xn, (w_in_slots,) = _rms_fwd(x, small["g_mix_pre"], [shards["w_in"]])
    w_z, w_q, w_k, w_v, w_f, w_ga, w_gb = _columns_from_slots(
        _with_own_slot(shards["w_in"], w_in_slots, place[0]), (IN_Z, IN_Q, IN_K, IN_V, IN_F, IN_G, IN_G + D_MODEL, IN_END))
    w_qkv, w_g = jnp.concatenate([w_q, w_k, w_v], axis=1), jnp.concatenate([w_ga, w_gb], axis=1)
    w_f = jnp.pad(w_f, ((0, 0), (0, LANES - N_HEADS)))
    z, qkv, gl, fl = _project(xn, [w_z, w_qkv, w_g, w_f], [F32, BF, BF, F32], name="proj_in")
    ysgu = _sgu_fwd(z, small["g_sgu"], small["b_sgu"], ws, bias_plane)
    qf, kl, vl, tile_stats = _attn_prep(qkv, fl, b_forget)
    first_key_tile, last_query_tile, bounded = _attn_ranges(tile_stats)
    yattn, yattn_f, ql, gathered = _attn_fwd(qf, kl, vl, first_key_tile, bounded, [shards[name] for name in LATE_WEIGHTS])
    w = {name: _assemble(name, shards[name], got, place[0]) for name, got in zip(LATE_WEIGHTS, gathered, strict=True)}
    a, b, merged = _branch_merge(ysgu, yattn, w["w_branch_sgu"], w["w_branch_attn"], gl)
    o, h1, xn2 = _matmul_rows(
        [(merged, w["w_out"])], nt=False, rows=[x], vecs=[small["g_mix_post"], small["g_ffn_pre"]], row_outs=[F32, F32, BF],
        n_sums=0, epilogue=_mixer_out_fwd, name="proj_out_norms")

    (hid,) = _project(xn2, [w["w_up"]], [BF], name="ffn_up", tm=FFN_ROWS, epilogue=lambda acc: jnp.square(jnp.maximum(acc, 0.0)))
    dy, ddn, sq, dg_ffn_post = _matmul_rows(
        [(hid, w["w_down"])], nt=False, rows=[h1, target], vecs=[small["g_ffn_post"]], row_outs=[F32, BF], n_sums=2,
        epilogue=_loss_head, name="ffn_down_loss")

    (dup,) = _project(ddn, [w["w_down"]], [BF], name="ffn_down_bwd", tm=FFN_ROWS, nt=True, extra=hid,
                      epilogue=lambda acc, h: acc * (2.0 * jnp.sqrt(h.astype(F32))))
    dw_down = _matmul_tn(hid, ddn, name="dw_down")
    dh1, do, dg_ffn_pre, dg_mix_post = _matmul_rows(
        [(dup, w["w_up"])], nt=True, rows=[h1, dy, o], vecs=[small["g_ffn_pre"], small["g_mix_post"]], row_outs=[F32, BF],
        n_sums=2, epilogue=_mixer_out_bwd, name="ffn_up_bwd_norms")
    dw_up = _matmul_tn(xn2, dup, name="dw_up", slots=True)

    def gate_bwd(dm, a_t, b_t, gla, glb):
        ga, gb = jax.nn.sigmoid(gla.astype(F32)), jax.nn.sigmoid(glb.astype(F32))
        return dm * ga, dm * gb, dm * a_t.astype(F32) * (ga * (1.0 - ga)), dm * b_t.astype(F32) * (gb * (1.0 - gb))

    da, db, dgla, dglb = _matmul([(do, w["w_out"])], nt=True, out_dtypes=[BF] * 4, name="proj_out_bwd",
                                 epilogue=gate_bwd, extras=[a, b, (gl, 0), (gl, D_MODEL)])
    dw_out = _matmul_tn(merged, do, name="dw_out")
    dysgu = _matmul([(da, w["w_branch_sgu"])], nt=True, out_dtypes=[F32], name="branch_sgu_bwd")
    dyattn = _matmul([(db, w["w_branch_attn"])], nt=True, out_dtypes=[F32], name="branch_attn_bwd")
    dw_bs = _matmul_tn(ysgu, da, name="dw_branch_sgu")
    dw_ba = _matmul_tn(yattn, db, name="dw_branch_attn")
    early = {"w_branch_sgu": _full_to_slots(dw_bs, 1), "w_branch_attn": _full_to_slots(dw_ba, 1),
             "w_out": _full_to_slots(dw_out, 0), "w_up": dw_up, "w_down": _full_to_slots(dw_down, 0)}
    (dz, dws, dbs, dg_sgu, db_sgu), early_theirs = _sgu_bwd(
        dysgu, z, small["g_sgu"], small["b_sgu"], ws, ws_t, bias_plane, [early[name] for name in EARLY_GRADS])
    early_sums = {name: _add_sibling(early[name], theirs, place, name="add_sibling_" + name)
                  for name, theirs in zip(EARLY_GRADS, early_theirs, strict=True)}
    dout = _attn_bwd_prep(dyattn, yattn_f)
    (dq, dk, dv, ext_q, ext_k), early_received = _attn_bwd(
        kl, vl, ql, dout, last_query_tile, [early_sums[name][1] for name in EARLY_GRADS])
    dfl, dbf = _forget_bwd(ext_q, ext_k, fl, b_forget)
    dw_z, dw_q, dw_k, dw_v, dw_f = _matmul_tn_multi(xn, [dz, dq, dk, dv, dfl], name="dw_in_mix")
    dw_ga, dw_gb = _matmul_tn_multi(xn, [dgla, dglb], name="dw_in_gates")
    dw_in = _columns_to_slots([dw_z, dw_q, dw_k, dw_v, dw_f[:, :N_HEADS], dw_ga, dw_gb])
    (dw_in_theirs,) = _exchange_halves([dw_in], name="exchange_halves_w_in")
    dw_in_sum = _add_sibling(dw_in, dw_in_theirs, place, name="add_sibling_w_in")
    dx, dg_mix_pre, dw_in_received = _matmul_rows(
        [(dz, w_z), (dq, w_q), (dk, w_k), (dv, w_v), (dgla, w_ga), (dglb, w_gb), (dfl, w_f)],
        nt=True, rows=[x, dh1], vecs=[small["g_mix_pre"]], row_outs=[F32], n_sums=1, epilogue=_input_norm_bwd,
        name="proj_in_bwd_norm", scatter=[dw_in_sum[1]])

    reduced = {name: (early_sums[name][0], got) for name, got in zip(EARLY_GRADS, early_received, strict=True)}
    reduced["w_in"] = (dw_in_sum[0], dw_in_received)
    small_grads = {"g_mix_pre": dg_mix_pre, "b_forget": dbf[:, :N_HEADS], "g_sgu": dg_sgu, "b_sgu": db_sgu,
                   "w_spatial": dws.reshape(N_GROUPS * CHUNK, CHUNK), "b_spatial": dbs[:, :N_GROUPS].T,
                   "g_mix_post": dg_mix_post, "g_ffn_pre": dg_ffn_pre, "g_ffn_post": dg_ffn_post}
    return sq, dx, reduced, small_grads


NAMES = ("g_mix_pre", "w_in", "b_forget", "g_sgu", "b_sgu", "w_spatial", "b_spatial", "w_branch_sgu", "w_branch_attn",
         "w_out", "g_mix_post", "g_ffn_pre", "w_up", "w_down", "g_ffn_post")


def kernel(x, g_mix_pre, w_in, b_forget, g_sgu, b_sgu, w_spatial, b_spatial, w_branch_sgu, w_branch_attn, w_out, g_mix_post, g_ffn_pre, w_up, w_down, g_ffn_post, loss_target, m_g_mix_pre, m_w_in, m_b_forget, m_g_sgu, m_b_sgu, m_w_spatial, m_b_spatial, m_w_branch_sgu, m_w_branch_attn, m_w_out, m_g_mix_post, m_g_ffn_pre, m_w_up, m_w_down, m_g_ffn_post, v_g_mix_pre, v_w_in, v_b_forget, v_g_sgu, v_b_sgu, v_w_spatial, v_b_spatial, v_w_branch_sgu, v_w_branch_attn, v_w_out, v_g_mix_post, v_g_ffn_pre, v_w_up, v_w_down, v_g_ffn_post):
    weights = dict(zip(NAMES, (g_mix_pre, w_in, b_forget, g_sgu, b_sgu, w_spatial, b_spatial, w_branch_sgu, w_branch_attn,
                               w_out, g_mix_post, g_ffn_pre, w_up, w_down, g_ffn_post), strict=True))
    first = dict(zip(NAMES, (m_g_mix_pre, m_w_in, m_b_forget, m_g_sgu, m_b_sgu, m_w_spatial, m_b_spatial, m_w_branch_sgu,
                             m_w_branch_attn, m_w_out, m_g_mix_post, m_g_ffn_pre, m_w_up, m_w_down, m_g_ffn_post), strict=True))
    second = dict(zip(NAMES, (v_g_mix_pre, v_w_in, v_b_forget, v_g_sgu, v_b_sgu, v_w_spatial, v_b_spatial, v_w_branch_sgu,
                              v_w_branch_attn, v_w_out, v_g_mix_post, v_g_ffn_pre, v_w_up, v_w_down, v_g_ffn_post), strict=True))
    shard_shapes = {name: _shard_shape(shape, axis) for name, shape, axis in SHARDED}
    small_shapes = dict(SMALL)
    view = lambda name, a: a.reshape(shard_shapes.get(name) or small_shapes[name])

    place = jnp.stack([2 * lax.axis_index("x") + lax.axis_index("y"), lax.axis_index("c")]).astype(jnp.int32)

    shards = {name: view(name, weights[name]).astype(BF) for name, _, _ in SHARDED}
    small = {name: view(name, weights[name]) for name, _ in SMALL}
    sq, dx, reduced, small_grads = _local_step(x[0], loss_target[0], shards, small, place)
    loss = lax.psum(0.5 * jnp.sum(sq) / D_MODEL, ("x", "y", "c"))

    small_mine = _pack_small(small_grads)
    (small_theirs,) = _exchange_halves([small_mine], name="exchange_halves_small")
    small_sum, _ = _add_sibling(small_mine, small_theirs, place, name="add_sibling_small")
    (small_received,) = _scatter_to_owners([small_sum])
    totals = {name: _add_chips(s, r, place, name="add_chips_" + name, own_slots=True) for name, (s, r) in reduced.items()}
    small_total = _add_chips(small_sum, small_received, place, name="add_chips_small", own_slots=False)
    joined = _join_halves([totals[name] for name, _, _ in SHARDED] + [small_total])
    grad = {**{name: g for (name, _, _), g in zip(SHARDED, joined[:-1], strict=True)}, **_unpack_small(joined[-1])}

    grad_out, delta, new_m, new_v = {}, {}, {}, {}
    for name in NAMES:
        rows, cols = grad[name].shape
        as_given = lambda a: a.reshape(1, rows, cols)
        grad_out[name], delta[name], new_m[name], new_v[name] = _adamw(
            as_given(weights[name]), grad[name], as_given(first[name]), as_given(second[name]), name="adamw_" + name)

    like = lambda d: [d[name].reshape(weights[name].shape) for name in NAMES]
    return (loss, dx[None], *like(grad_out), *like(delta), *like(new_m), *like(new_v))
```

```python
import functools

import jax
import jax.numpy as jnp
import numpy as np
from jax import lax
from jax.experimental import pallas as pl
from jax.experimental.pallas import tpu as pltpu

F32 = jnp.float32
BF = jnp.bfloat16
MESH = pl.DeviceIdType.MESH

D_MODEL = 1024
N_HEADS = 8
HEAD_DIM = 64
ATTN_W = N_HEADS * HEAD_DIM
SGU_W = 512
N_GROUPS = 8
CHUNK = 128
D_FF = 4096
EPS = 1e-6
Q_SCALE = HEAD_DIM ** -0.5
N_CHIPS = 4
LANES = 128

ADAM_LR = 0.001
ADAM_B1 = 0.9
ADAM_B2 = 0.999
ADAM_EPS = 1e-08
ADAM_WD = 0.01
ADAM_STEP = 10

VMEM_LIMIT = 48 * 1024 * 1024
BIG_VMEM = 58 * 1024 * 1024
NEG = -1e30


def _params(*sem):
    return pltpu.CompilerParams(dimension_semantics=sem, vmem_limit_bytes=VMEM_LIMIT)


def _dot(a, b):
    return jnp.dot(a, b, preferred_element_type=F32)


def _dot_nt(a, b):
    return lax.dot_general(a, b, (((1,), (1,)), ((), ())), preferred_element_type=F32)


def _dot_tn(a, b):
    return lax.dot_general(a, b, (((0,), (0,)), ((), ())), preferred_element_type=F32)


def _split3(c):
    hi = c.astype(BF).astype(F32)
    r = c - hi
    mid = r.astype(BF).astype(F32)
    lo = (r - mid).astype(BF).astype(F32)
    return hi, mid, lo


def _gelu(x):
    k = 0.7978845608028654
    return 0.5 * x * (1.0 + jnp.tanh(k * (x + 0.044715 * (x * x * x))))


def _gelu_grad(x):
    k = 0.7978845608028654
    x2 = x * x
    t = jnp.tanh(k * (x + 0.044715 * (x2 * x)))
    return 0.5 * (1.0 + t) + 0.5 * x * (1.0 - t * t) * (k * (1.0 + 3.0 * 0.044715 * x2))


def _rms_bwd(a, g, dy):
    r = lax.rsqrt(jnp.mean(a * a, axis=-1, keepdims=True) + EPS)
    n = a * r
    dn = dy * g
    da = r * (dn - n * jnp.mean(dn * n, axis=-1, keepdims=True))
    return da, dy * n


MM_ROWS = 1024
MM_COLS = 512
FFN_ROWS = 512

def _matmul(pairs, *, nt, out_dtypes, name, tm=MM_ROWS, tn=MM_COLS, epilogue=None, extras=()):
    n_pairs, n_extra = len(pairs), len(extras)
    M = pairs[0][0].shape[0]
    N = pairs[0][1].shape[0] if nt else pairs[0][1].shape[1]
    tm, tn = min(tm, M), min(tn, N)
    assert M % tm == 0 and N % tn == 0

    def body(*refs):
        acc = None
        for p in range(n_pairs):
            a_ref, b_ref = refs[2 * p], refs[2 * p + 1]
            d = _dot_nt(a_ref[...], b_ref[...]) if nt else _dot(a_ref[...], b_ref[...])
            acc = d if acc is None else acc + d
        e_refs = refs[2 * n_pairs:2 * n_pairs + n_extra]
        o_refs = refs[2 * n_pairs + n_extra:]
        outs = (acc,) if epilogue is None else epilogue(acc, *[e[...] for e in e_refs])
        for o_ref, o in zip(o_refs, outs, strict=True):
            o_ref[...] = o.astype(o_ref.dtype)

    in_specs, args = [], []
    for a, b in pairs:
        K = a.shape[1]
        in_specs.append(pl.BlockSpec((tm, K), lambda i, j: (i, 0)))
        in_specs.append(pl.BlockSpec((tn, K), lambda i, j: (j, 0)) if nt else pl.BlockSpec((K, tn), lambda i, j: (0, j)))
        args += [a, b]
    for e in extras:
        e, col = e if isinstance(e, tuple) else (e, 0)
        in_specs.append(pl.BlockSpec((tm, tn), functools.partial(lambda i, j, off: (i, j + off), off=col // tn)))
        args.append(e)
    outs = pl.pallas_call(
        body, name=name, grid=(M // tm, N // tn), in_specs=in_specs,
        out_specs=[pl.BlockSpec((tm, tn), lambda i, j: (i, j)) for _ in out_dtypes],
        out_shape=[jax.ShapeDtypeStruct((M, N), dt) for dt in out_dtypes],
        compiler_params=_params("parallel", "parallel"),
    )(*args)
    return outs if len(outs) > 1 else outs[0]


def _matmul_tn_multi(a, bs, *, name, tk=1024):
    T, K1 = a.shape
    tk = min(tk, T)
    n = len(bs)

    def body(a_ref, *refs):
        @pl.when(pl.program_id(0) == 0)
        def _():
            for o_ref in refs[n:]:
                o_ref[...] = jnp.zeros_like(o_ref)

        av = a_ref[...]
        for b_ref, o_ref in zip(refs[:n], refs[n:], strict=True):
            o_ref[...] += _dot_tn(av, b_ref[...])

    return pl.pallas_call(
        body, name=name, grid=(T // tk,),
        in_specs=[pl.BlockSpec((tk, K1), lambda k: (k, 0))] + [pl.BlockSpec((tk, b.shape[1]), lambda k: (k, 0)) for b in bs],
        out_specs=[pl.BlockSpec((K1, b.shape[1]), lambda k: (0, 0)) for b in bs],
        out_shape=[jax.ShapeDtypeStruct((K1, b.shape[1]), F32) for b in bs],
        compiler_params=pltpu.CompilerParams(dimension_semantics=("arbitrary",), vmem_limit_bytes=BIG_VMEM),
    )(a, *bs)


def _project(a, weights, out_dtypes, *, name, tm=512, nt=False, epilogue=None, extra=None):
    M, K = a.shape
    tm = min(tm, M)
    n = len(weights)
    widths = [w.shape[0] if nt else w.shape[1] for w in weights]
    extras = [] if extra is None else [extra]

    def body(a_ref, *refs):
        av = a_ref[...]
        w_refs, e_refs, o_refs = refs[:n], refs[n:n + len(extras)], refs[n + len(extras):]
        for w_ref, o_ref in zip(w_refs, o_refs, strict=True):
            acc = _dot_nt(av, w_ref[...]) if nt else _dot(av, w_ref[...])
            if epilogue is not None:
                acc = epilogue(acc, *[e[...] for e in e_refs])
            o_ref[...] = acc.astype(o_ref.dtype)

    return pl.pallas_call(
        body, name=name, grid=(M // tm,),
        in_specs=[pl.BlockSpec((tm, K), lambda i: (i, 0))]
        + [pl.BlockSpec(w.shape, lambda i: (0, 0), pipeline_mode=pl.Buffered(1)) for w in weights]
        + [pl.BlockSpec((tm, e.shape[1]), lambda i: (i, 0)) for e in extras],
        out_specs=[pl.BlockSpec((tm, width), lambda i: (i, 0)) for width in widths],
        out_shape=[jax.ShapeDtypeStruct((M, width), dt) for width, dt in zip(widths, out_dtypes, strict=True)],
        compiler_params=_params("parallel"),
    )(a, *weights, *extras)


def _matmul_tn(a, b, *, name, tm=1024, tn=1024, tk=4096, slots=False):
    T, K1 = a.shape
    N = b.shape[1]
    tm, tn, tk = min(tm, K1), min(tn, N // N_CHIPS if slots else N), min(tk, T)
    assert K1 % tm == 0 and (N // N_CHIPS if slots else N) % tn == 0 and T % tk == 0
    per_slot = N // N_CHIPS // tn

    def body(a_ref, b_ref, o_ref):
        @pl.when(pl.program_id(2) == 0)
        def _():
            o_ref[...] = jnp.zeros_like(o_ref)

        o_ref[...] += _dot_tn(a_ref[...], b_ref[...])

    if slots:
        out_spec = pl.BlockSpec((None, tm, tn), lambda i, j, k: (j // per_slot, i, j % per_slot))
        out_shape = jax.ShapeDtypeStruct((N_CHIPS, K1, N // N_CHIPS), F32)
    else:
        out_spec = pl.BlockSpec((tm, tn), lambda i, j, k: (i, j))
        out_shape = jax.ShapeDtypeStruct((K1, N), F32)
    return pl.pallas_call(
        body, name=name, grid=(K1 // tm, N // tn, T // tk),
        in_specs=[pl.BlockSpec((tk, tm), lambda i, j, k: (k, i)), pl.BlockSpec((tk, tn), lambda i, j, k: (k, j))],
        out_specs=out_spec, out_shape=out_shape,
        compiler_params=_params("parallel", "parallel", "arbitrary"),
    )(a, b)


def _branch_merge(ysgu, yattn, w_bs, w_ba, gl, *, tm=MM_ROWS, tn=MM_COLS):
    T = ysgu.shape[0]
    tm = min(tm, T)
    nj = D_MODEL // tn

    def body(ys_ref, ya_ref, wbs_ref, wba_ref, gla_ref, glb_ref, a_ref, b_ref, m_ref):
        a = _dot(ys_ref[...], wbs_ref[...])
        b = _dot(ya_ref[...], wba_ref[...])
        a_ref[...] = a.astype(BF)
        b_ref[...] = b.astype(BF)
        m_ref[...] = (jax.nn.sigmoid(gla_ref[...].astype(F32)) * a + jax.nn.sigmoid(glb_ref[...].astype(F32)) * b).astype(BF)

    return pl.pallas_call(
        body, name="branch_merge", grid=(T // tm, nj),
        in_specs=[
            pl.BlockSpec((tm, SGU_W), lambda i, j: (i, 0)),
            pl.BlockSpec((tm, ATTN_W), lambda i, j: (i, 0)),
            pl.BlockSpec((SGU_W, tn), lambda i, j: (0, j)),
            pl.BlockSpec((ATTN_W, tn), lambda i, j: (0, j)),
            pl.BlockSpec((tm, tn), lambda i, j: (i, j)),
            pl.BlockSpec((tm, tn), lambda i, j: (i, j + nj)),
        ],
        out_specs=[pl.BlockSpec((tm, tn), lambda i, j: (i, j))] * 3,
        out_shape=[jax.ShapeDtypeStruct((T, D_MODEL), BF)] * 3,
        compiler_params=_params("parallel", "parallel"),
    )(ysgu, yattn, w_bs, w_ba, gl, gl)


def _row_spec(tr, width):
    return pl.BlockSpec((tr, width), lambda i: (i, 0))


def _vec_spec(width):
    return pl.BlockSpec((1, width), lambda i: (0, 0))


def _rms_fwd(x, g, shards, *, tr=256):
    T = x.shape[0]
    tr = min(tr, T)
    n_steps = T // tr
    k = len(shards)

    def body(x_ref, g_ref, *refs):
        step = pl.program_id(0)
        gather_start, gather_forward, gather_finish = _gather_phases(refs[:k], refs[k + 1:2 * k + 1], *refs[2 * k + 1:])
        pl.when(step == 0)(gather_start)
        pl.when(step == (3 * n_steps) // 4)(gather_forward)
        xv = x_ref[...]
        r = lax.rsqrt(jnp.mean(xv * xv, axis=-1, keepdims=True) + EPS)
        refs[k][...] = ((xv * r) * g_ref[...]).astype(BF)
        pl.when(step == n_steps - 1)(gather_finish)

    outs = pl.pallas_call(
        body, name="rms_fwd", grid=(n_steps,),
        in_specs=[_row_spec(tr, D_MODEL), _vec_spec(D_MODEL)] + [HBM] * k, out_specs=[_row_spec(tr, D_MODEL)] + [HBM] * k,
        out_shape=[jax.ShapeDtypeStruct((T, D_MODEL), BF)] + _gathered_shapes(shards),
        scratch_shapes=_gather_semaphores(k), compiler_params=_params("arbitrary"),
    )(x, g, *shards)
    return outs[0], outs[1:]


def _mixer_out_fwd(o, x, g_post, g_pre):
    r = lax.rsqrt(jnp.mean(o * o, axis=-1, keepdims=True) + EPS)
    h1 = x + (o * r) * g_post
    r2 = lax.rsqrt(jnp.mean(h1 * h1, axis=-1, keepdims=True) + EPS)
    return o, h1, (h1 * r2) * g_pre


def _matmul_rows(pairs, *, nt, rows, vecs, row_outs, n_sums, epilogue, name, tm=512, scatter=()):
    M = pairs[0][0].shape[0]
    N = pairs[0][1].shape[0] if nt else pairs[0][1].shape[1]
    tm = min(tm, M)
    n_steps = M // tm
    n_pairs, n_rows, n_vecs, n_out, n_scatter = len(pairs), len(rows), len(vecs), len(row_outs), len(scatter)

    def body(*refs):
        groups, at = [], 2 * n_pairs
        for count in (n_rows, n_vecs, n_scatter, n_out, n_sums, n_scatter):
            groups.append(refs[at:at + count])
            at += count
        r_refs, v_refs, b_refs, o_refs, s_refs, got_refs = groups
        sems = refs[at:]
        step = pl.program_id(0)
        if n_scatter:
            scatter_start, scatter_finish = _scatter_phases(b_refs, got_refs, *sems)
            pl.when(step == 0)(scatter_start)

        @pl.when(step == 0)
        def _():
            for s_ref in s_refs:
                s_ref[...] = jnp.zeros_like(s_ref)

        acc = None
        for p in range(n_pairs):
            a_ref, b_ref = refs[2 * p], refs[2 * p + 1]
            d = _dot_nt(a_ref[...], b_ref[...]) if nt else _dot(a_ref[...], b_ref[...])
            acc = d if acc is None else acc + d
        outs = epilogue(acc, *[r[...] for r in r_refs], *[v[...] for v in v_refs])
        for o_ref, o in zip(o_refs, outs[:n_out], strict=True):
            o_ref[...] = o.astype(o_ref.dtype)
        for s_ref, term in zip(s_refs, outs[n_out:], strict=True):
            s_ref[...] += jnp.sum(term, axis=0, keepdims=True)
        if n_scatter:
            pl.when(step == n_steps - 1)(scatter_finish)

    in_specs, args = [], []
    for a, b in pairs:
        in_specs += [_row_spec(tm, a.shape[1]), pl.BlockSpec(b.shape, lambda i: (0, 0))]
        args += [a, b]
    outs = pl.pallas_call(
        body, name=name, grid=(n_steps,),
        in_specs=in_specs + [_row_spec(tm, N)] * n_rows + [_vec_spec(N)] * n_vecs + [HBM] * n_scatter,
        out_specs=[_row_spec(tm, N)] * n_out + [_vec_spec(N)] * n_sums + [HBM] * n_scatter,
        out_shape=[jax.ShapeDtypeStruct((M, N), dt) for dt in row_outs] + [jax.ShapeDtypeStruct((1, N), F32)] * n_sums
        + (_scattered_shapes(scatter) if n_scatter else []),
        scratch_shapes=_scatter_semaphores(n_scatter) if n_scatter else [],
        compiler_params=pltpu.CompilerParams(dimension_semantics=("arbitrary",), vmem_limit_bytes=BIG_VMEM),
    )(*args, *rows, *vecs, *scatter)
    return outs


def _loss_head(dn, h1, target, g):
    r = lax.rsqrt(jnp.mean(dn * dn, axis=-1, keepdims=True) + EPS)
    err = h1 + (dn * r) * g - target
    dy = err * (1.0 / D_MODEL)
    ddn, dg_terms = _rms_bwd(dn, g, dy)
    return dy, ddn, err * err, dg_terms


def _mixer_out_bwd(dxn2, h1, dy, o, g_pre, g_post):
    da, dg_pre_terms = _rms_bwd(h1, g_pre, dxn2)
    dh1 = dy + da
    do, dg_post_terms = _rms_bwd(o, g_post, dh1)
    return dh1, do, dg_pre_terms, dg_post_terms


def _input_norm_bwd(dxn, x, dh1, g):
    da, dg_terms = _rms_bwd(x, g, dxn)
    return dh1 + da, dg_terms


def _sgu_norm(z_tile, g, b):
    gz = _gelu(z_tile)
    u, vv = gz[:, :SGU_W], gz[:, SGU_W:]
    xc = vv - jnp.mean(vv, axis=-1, keepdims=True)
    rstd = lax.rsqrt(jnp.mean(xc * xc, axis=-1, keepdims=True) + EPS)
    xhat = xc * rstd
    return u, xhat, rstd, xhat * g + b


def _sgu_mix(w_ref, v_bf, first_half):
    parts = []
    for p in range(N_GROUPS // 2):
        vp = v_bf[:, p * LANES:(p + 1) * LANES]
        parts.append(jnp.where(first_half, _dot(w_ref[2 * p], vp), _dot(w_ref[2 * p + 1], vp)))
    return jnp.concatenate(parts, axis=1)


def _sgu_fwd(z, g_sgu, b_sgu, ws, bias_plane, *, tm=512):
    T = z.shape[0]
    tm = min(tm, T)

    def body(z_ref, g_ref, b_ref, ws_ref, bp_ref, y_ref):
        u, _, _, vn = _sgu_norm(z_ref[...], g_ref[...], b_ref[...])
        vn_bf = vn.astype(BF)
        first_half = lax.broadcasted_iota(jnp.int32, (CHUNK, LANES), 1) < HEAD_DIM
        for c in range(tm // CHUNK):
            rows = slice(c * CHUNK, (c + 1) * CHUNK)
            s = _sgu_mix(ws_ref, vn_bf[rows, :], first_half) + bp_ref[...]
            y_ref[rows, :] = (u[rows, :] * s).astype(BF)

    return pl.pallas_call(
        body, name="sgu_fwd", grid=(T // tm,),
        in_specs=[_row_spec(tm, 2 * SGU_W), _vec_spec(SGU_W), _vec_spec(SGU_W),
                  pl.BlockSpec((N_GROUPS, CHUNK, CHUNK), lambda i: (0, 0, 0)),
                  pl.BlockSpec((CHUNK, SGU_W), lambda i: (0, 0))],
        out_specs=_row_spec(tm, SGU_W), out_shape=jax.ShapeDtypeStruct((T, SGU_W), BF),
        compiler_params=_params("parallel"),
    )(z, g_sgu, b_sgu, ws, bias_plane)


def _sgu_bwd(dy, z, g_sgu, b_sgu, ws, ws_t, bias_plane, exchange, *, tm=512):
    T = z.shape[0]
    tm = min(tm, T)
    n_steps = T // tm
    k = len(exchange)

    def body(dy_ref, z_ref, g_ref, b_ref, ws_ref, wst_ref, bp_ref, *refs):
        x_refs, (dz_ref, dws_ref, dbs_ref, dg_ref, db_ref), r_refs = refs[:k], refs[k:k + 5], refs[k + 5:2 * k + 5]
        dbp_ref, send_sems, recv_sems = refs[2 * k + 5:]
        step = pl.program_id(0)
        exchange_start, exchange_finish = _exchange_phases(x_refs, r_refs, send_sems, recv_sems)
        pl.when(step == 0)(exchange_start)

        @pl.when(step == 0)
        def _():
            dws_ref[...] = jnp.zeros_like(dws_ref)
            dg_ref[...] = jnp.zeros_like(dg_ref)
            db_ref[...] = jnp.zeros_like(db_ref)
            dbp_ref[...] = jnp.zeros_like(dbp_ref)

        g = g_ref[...]
        zt = z_ref[...]
        u, xhat, rstd, vn = _sgu_norm(zt, g, b_ref[...])
        vn_bf = vn.astype(BF)
        first_half = lax.broadcasted_iota(jnp.int32, (CHUNK, LANES), 1) < HEAD_DIM
        dyv = dy_ref[...]
        dg_acc = jnp.zeros((1, SGU_W), F32)
        db_acc = jnp.zeros((1, SGU_W), F32)
        for c in range(tm // CHUNK):
            rows = slice(c * CHUNK, (c + 1) * CHUNK)
            v_c = vn_bf[rows, :]
            s = _sgu_mix(ws_ref, v_c, first_half) + bp_ref[...]
            dy_c = dyv[rows, :]
            du = dy_c * s
            dsv = dy_c * u[rows, :]
            dbp_ref[...] += dsv
            ds_bf = dsv.astype(BF)
            zero = jnp.zeros((CHUNK, LANES), BF)
            for p in range(N_GROUPS // 2):
                dsp = ds_bf[:, p * LANES:(p + 1) * LANES]
                vp = v_c[:, p * LANES:(p + 1) * LANES]
                dws_ref[2 * p] += _dot_nt(jnp.where(first_half, dsp, zero), vp)
                dws_ref[2 * p + 1] += _dot_nt(jnp.where(first_half, zero, dsp), vp)
            dvn = _sgu_mix(wst_ref, ds_bf, first_half)
            xh = xhat[rows, :]
            dxh = dvn * g
            dvv = rstd[rows, :] * (dxh - jnp.mean(dxh, axis=-1, keepdims=True)
                                   - xh * jnp.mean(dxh * xh, axis=-1, keepdims=True))
            dg_acc += jnp.sum(dvn * xh, axis=0, keepdims=True)
            db_acc += jnp.sum(dvn, axis=0, keepdims=True)
            dgz = jnp.concatenate([du, dvv], axis=1)
            dz_ref[rows, :] = (dgz * _gelu_grad(zt[rows, :])).astype(BF)
        dg_ref[...] += dg_acc
        db_ref[...] += db_acc

        @pl.when(step == n_steps - 1)
        def _():
            r = lax.broadcasted_iota(jnp.int32, (CHUNK, CHUNK), 0)
            cidx = lax.broadcasted_iota(jnp.int32, (CHUNK, CHUNK), 1)
            causal = (cidx <= r).astype(F32)
            for gi in range(N_GROUPS):
                dws_ref[gi] = dws_ref[gi] * causal
            lane = lax.broadcasted_iota(jnp.int32, (CHUNK, LANES), 1)
            out = jnp.zeros((CHUNK, LANES), F32)
            dbp = dbp_ref[...]
            for gi in range(N_GROUPS):
                col = jnp.sum(dbp[:, gi * HEAD_DIM:(gi + 1) * HEAD_DIM], axis=1, keepdims=True)
                out = jnp.where(lane == gi, col, out)
            dbs_ref[...] = out
            exchange_finish()

    w_spec = pl.BlockSpec((N_GROUPS, CHUNK, CHUNK), lambda i: (0, 0, 0))
    plane = pl.BlockSpec((CHUNK, SGU_W), lambda i: (0, 0))
    outs = pl.pallas_call(
        body, name="sgu_bwd", grid=(n_steps,),
        in_specs=[_row_spec(tm, SGU_W), _row_spec(tm, 2 * SGU_W), _vec_spec(SGU_W), _vec_spec(SGU_W), w_spec, w_spec, plane]
        + [HBM] * k,
        out_specs=[_row_spec(tm, 2 * SGU_W), w_spec, pl.BlockSpec((CHUNK, LANES), lambda i: (0, 0)),
                   _vec_spec(SGU_W), _vec_spec(SGU_W)] + [HBM] * k,
        out_shape=[jax.ShapeDtypeStruct((T, 2 * SGU_W), BF), jax.ShapeDtypeStruct((N_GROUPS, CHUNK, CHUNK), F32),
                   jax.ShapeDtypeStruct((CHUNK, LANES), F32), jax.ShapeDtypeStruct((1, SGU_W), F32),
                   jax.ShapeDtypeStruct((1, SGU_W), F32)] + _exchanged_shapes(exchange),
        scratch_shapes=[pltpu.VMEM((CHUNK, SGU_W), F32)] + _exchange_semaphores(k),
        compiler_params=_params("arbitrary"),
    )(dy, z, g_sgu, b_sgu, ws, ws_t, bias_plane, *exchange)
    return outs[:5], outs[5:]


def _tri(n, upper):
    r = lax.broadcasted_iota(jnp.int32, (n, n), 0)
    c = lax.broadcasted_iota(jnp.int32, (n, n), 1)
    return ((c >= r) if upper else (c <= r)).astype(BF)


def _scan_dot(tri, x):
    hi, mid, lo = _split3(x)
    return (_dot(tri, hi.astype(BF)) + _dot(tri, mid.astype(BF))) + _dot(tri, lo.astype(BF))


def _with_lanes(base, lane, start, cols):
    out = base
    for k, col in enumerate(cols):
        if col is not None:
            out = jnp.where(lane == start + k, col, out)
    return out


def _logit_bound(q_norm, k_norm):
    return NORM_SLACK * q_norm * k_norm + 1.0


ATTN_TILE = 512
SKIP_BELOW = -110.0
NORM_SLACK = 1.001
BOUNDED_GAP = 60.0


def _attn_prep(qkv, fl, b_forget, *, tp=ATTN_TILE):
    T = qkv.shape[0]
    tp = min(tp, T)
    head_sum, gather6, place_q, place_k, place_v = (jnp.asarray(m, BF) for m in _attn_placements())

    def body(qkv_ref, fl_ref, bf_ref, hs_ref, g6_ref, pq_ref, pk_ref, pv_ref, qf_ref, kl_ref, vl_ref, st_ref, carry_ref, kmax_ref):
        @pl.when(pl.program_id(0) == 0)
        def _():
            carry_ref[...] = jnp.zeros_like(carry_ref)
            kmax_ref[...] = jnp.zeros_like(kmax_ref)

        x = fl_ref[...] + bf_ref[...]
        logf = jnp.minimum(x, 0.0) - jnp.log(1.0 + jnp.exp(-jnp.abs(x)))
        cum = _scan_dot(_tri(tp, upper=False), logf) + carry_ref[...]
        carry_ref[...] = cum[tp - 1:tp, :]

        def head_norms(block):
            sq = block * block
            hi = sq.astype(BF)
            return _dot(hi, hs_ref[...]) + _dot((sq - hi.astype(F32)).astype(BF), hs_ref[...])

        qkvv = qkv_ref[...]
        q_norm = NORM_SLACK * jnp.sqrt(head_norms(qkvv[:, :ATTN_W].astype(F32) * Q_SCALE))
        kn = NORM_SLACK * jnp.sqrt(jnp.max(head_norms(qkvv[:, ATTN_W:2 * ATTN_W].astype(F32)), axis=0, keepdims=True))
        k_seen = jnp.maximum(kmax_ref[...], kn)
        kmax_ref[...] = k_seen
        rows = (jnp.max(q_norm, axis=0, keepdims=True), kn, jnp.max(cum, axis=0, keepdims=True),
                jnp.min(cum, axis=0, keepdims=True), k_seen)
        st_ref[...] = jnp.zeros_like(st_ref)
        for k, row in enumerate(rows):
            st_ref[0, k:k + 1, :] = row
        parts = jnp.concatenate([p.astype(BF) for p in _split3(cum) + _split3(-_logit_bound(q_norm, k_seen))], axis=1)
        lane = lax.broadcasted_iota(jnp.int32, (tp, LANES), 1)
        side = jnp.where(lane == 6 * N_HEADS, 1.0, _dot(parts, g6_ref[...])).astype(BF)
        for h in range(N_HEADS):
            pair = slice((h // 2) * LANES, (h // 2 + 1) * LANES)
            for out_ref, block, place_ref in ((qf_ref, qkvv[:, :ATTN_W], pq_ref), (kl_ref, qkvv[:, ATTN_W:2 * ATTN_W], pk_ref),
                                              (vl_ref, qkvv[:, 2 * ATTN_W:], pv_ref)):
                out_ref[h] = _dot(jnp.concatenate([block[:, pair], side], axis=1), place_ref[h]).astype(BF)

    head_spec = pl.BlockSpec((N_HEADS, tp, LANES), lambda i: (0, i, 0))
    whole = lambda a: pl.BlockSpec(a.shape, lambda i: (0,) * a.ndim)
    return pl.pallas_call(
        body, name="attn_prep", grid=(T // tp,),
        in_specs=[_row_spec(tp, 3 * ATTN_W), _row_spec(tp, LANES), _vec_spec(LANES)]
        + [whole(m) for m in (head_sum, gather6, place_q, place_k, place_v)],
        out_specs=[head_spec] * 3 + [pl.BlockSpec((1, N_HEADS, LANES), lambda i: (i, 0, 0))],
        out_shape=[jax.ShapeDtypeStruct((N_HEADS, T, LANES), BF)] * 3 + [jax.ShapeDtypeStruct((T // tp, N_HEADS, LANES), F32)],
        scratch_shapes=[pltpu.VMEM((1, LANES), F32), pltpu.VMEM((1, LANES), F32)], compiler_params=_params("arbitrary"),
    )(qkv, fl, b_forget, head_sum, gather6, place_q, place_k, place_v)


def _attn_placements():
    head_sum = np.zeros((ATTN_W, LANES), np.float32)
    head_sum[np.arange(ATTN_W), np.arange(ATTN_W) // HEAD_DIM] = 1.0
    gather6 = np.zeros((6 * LANES, LANES), np.float32)
    for j in range(6):
        gather6[j * LANES + np.arange(N_HEADS), j * N_HEADS + np.arange(N_HEADS)] = 1.0
    place = np.zeros((3, N_HEADS, 2 * LANES, LANES), np.float32)
    one = LANES + 6 * N_HEADS
    d = np.arange(HEAD_DIM)
    for h in range(N_HEADS):
        side = lambda j: LANES + j * N_HEADS + h
        place[0, h, (h % 2) * HEAD_DIM + d, d] = Q_SCALE
        place[1:, h, (h % 2) * HEAD_DIM + d, d] = 1.0
        for j in range(3):
            place[0, h, side(j), HEAD_DIM + j] = 1.0
            place[0, h, one, HEAD_DIM + 3 + j] = 1.0
            place[0, h, side(3 + j), HEAD_DIM + 6 + j] = 1.0
            place[1, h, one, HEAD_DIM + j] = 1.0
            place[1, h, side(j), HEAD_DIM + 3 + j] = -1.0
            place[1, h, one, HEAD_DIM + 6 + j] = 1.0
            place[2, h, one, HEAD_DIM + j] = 1.0
    return head_sum, gather6, place[0], place[1], place[2]


def _attn_ranges(stats):
    qn, kn, cmax, cmin, k_seen = (stats[:, k, :N_HEADS].T for k in range(5))
    n = qn.shape[1]
    bounded = (2.0 * _logit_bound(qn, k_seen) <= BOUNDED_GAP).reshape(N_HEADS // 2, 2, n).all(axis=1)
    reach = NORM_SLACK * qn * (jnp.max(kn, axis=1, keepdims=True) + kn) + cmax
    i = jnp.arange(n)[None, :, None]
    j = jnp.arange(n)[None, None, :]
    need = ((reach[:, :, None] - cmin[:, None, :] >= SKIP_BELOW) | (i == j)) & (j <= i)
    first = jnp.min(jnp.where(need, j, n), axis=2).reshape(N_HEADS // 2, 2, n).min(axis=1)
    last = jnp.max(jnp.where(need, i, -1), axis=1).reshape(N_HEADS // 2, 2, n).max(axis=1)
    return first.reshape(-1).astype(F32), last.reshape(-1).astype(F32), bounded.reshape(-1).astype(F32)


def _pair_block(t):
    return pl.BlockSpec((2, t, LANES), lambda p, i, *_: (p, i, 0))


def _pair_full(T):
    return pl.BlockSpec((2, T, LANES), lambda p, i, *_: (p, 0, 0))


def _packed_block(t):
    return pl.BlockSpec((t, LANES), lambda p, i, *_: (i, p))


def _causal(t, keys_in_rows=False):
    r = lax.broadcasted_iota(jnp.int32, (t, t), 0)
    c = lax.broadcasted_iota(jnp.int32, (t, t), 1)
    return (r <= c) if keys_in_rows else (c <= r)


def _tile_rows(j, t):
    return pl.ds(pl.multiple_of(j * t, t), t)


def _attn_call(body, name, tile_scalars, operands, in_specs, out_specs, out_shape, scratch_shapes, n_tiles):
    return pl.pallas_call(
        body, name=name,
        grid_spec=pltpu.PrefetchScalarGridSpec(
            num_scalar_prefetch=len(tile_scalars), grid=(N_HEADS // 2, n_tiles), in_specs=in_specs, out_specs=out_specs,
            scratch_shapes=scratch_shapes),
        out_shape=out_shape, compiler_params=_params("arbitrary", "arbitrary"),
    )(*tile_scalars, *operands)


def _attn_fwd(qf, kl, vl, first, bounded, shards, *, tq=ATTN_TILE):
    T = qf.shape[1]
    tq = min(tq, T)
    n = T // tq
    n_steps = (N_HEADS // 2) * n
    k = len(shards)

    def body(first_ref, bounded_ref, qf_ref, kl_ref, vl_ref, *refs):
        w_refs, (o_ref, of_ref, ql_ref), g_refs = refs[:k], refs[k:k + 3], refs[k + 3:2 * k + 3]
        m_ref, acc_ref, send_sems, recv_sems = refs[2 * k + 3:]
        i = pl.program_id(1)
        tile = pl.program_id(0) * n + i
        gather_start, gather_forward, gather_finish = _gather_phases(w_refs, g_refs, send_sems, recv_sems)
        pl.when(tile == 0)(gather_start)
        pl.when(tile == (3 * n_steps) // 4)(gather_forward)
        start = first_ref[tile].astype(jnp.int32)
        is_bounded = bounded_ref[tile] > 0.5
        acc_ref[...] = jnp.zeros_like(acc_ref)
        diagonal = _tile_rows(i, tq)
        causal = _causal(tq)

        def logits(hh, rows):
            return _dot_nt(qf_ref[hh], kl_ref[hh, rows, :])

        @pl.when(is_bounded)
        def _():
            m_ref[...] = jnp.zeros_like(m_ref)

            def update(hh, s, rows):
                acc_ref[hh] += _dot(jnp.exp(s).astype(BF), vl_ref[hh, rows, :])

            def step(j, carry):
                for hh in range(2):
                    update(hh, logits(hh, _tile_rows(j, tq)), _tile_rows(j, tq))
                return carry

            lax.fori_loop(start, i, step, 0)
            for hh in range(2):
                update(hh, jnp.where(causal, logits(hh, diagonal), NEG), diagonal)

        @pl.when(jnp.logical_not(is_bounded))
        def _():
            m_ref[...] = jnp.full_like(m_ref, NEG)

            def update(hh, s, rows):
                m_old = m_ref[hh]
                m_new = jnp.maximum(m_old, jnp.max(s, axis=1, keepdims=True))
                p = jnp.exp(s - m_new)
                acc_ref[hh] = jnp.exp(m_old - m_new) * acc_ref[hh] + _dot(p.astype(BF), vl_ref[hh, rows, :])
                m_ref[hh] = m_new

            def step(j, carry):
                for hh in range(2):
                    update(hh, logits(hh, _tile_rows(j, tq)), _tile_rows(j, tq))
                return carry

            lax.fori_loop(start, i, step, 0)
            for hh in range(2):
                update(hh, jnp.where(causal, logits(hh, diagonal), NEG), diagonal)

        lane = lax.broadcasted_iota(jnp.int32, (tq, LANES), 1)
        outs = []
        for hh in range(2):
            q = qf_ref[hh].astype(F32)
            acc = acc_ref[hh]
            l = acc[:, HEAD_DIM:HEAD_DIM + 1]
            outs.append(acc[:, :HEAD_DIM] / l)
            at = HEAD_DIM + 6
            neg_bound = (q[:, at:at + 1] + q[:, at + 1:at + 2]) + q[:, at + 2:at + 3]
            ql_ref[hh] = _with_lanes(q, lane, at, _split3(neg_bound - (m_ref[hh] + jnp.log(l)))).astype(BF)
        o = jnp.concatenate(outs, axis=1)
        o_ref[...] = o.astype(BF)
        of_ref[...] = o
        pl.when(tile == n_steps - 1)(gather_finish)

    outs = _attn_call(
        body, "attn_fwd", (first, bounded), (qf, kl, vl, *shards),
        [_pair_block(tq), _pair_full(T), _pair_full(T)] + [HBM] * k,
        [_packed_block(tq), _packed_block(tq), _pair_block(tq)] + [HBM] * k,
        [jax.ShapeDtypeStruct((T, ATTN_W), BF), jax.ShapeDtypeStruct((T, ATTN_W), F32),
         jax.ShapeDtypeStruct((N_HEADS, T, LANES), BF)] + _gathered_shapes(shards),
        [pltpu.VMEM((2, tq, 1), F32), pltpu.VMEM((2, tq, LANES), F32)] + _gather_semaphores(k), n)
    return outs[0], outs[1], outs[2], outs[3:]


def _attn_bwd_prep(dya, of, *, tr=ATTN_TILE):
    T = dya.shape[0]
    tr = min(tr, T)
    head_sum, gather6 = (jnp.asarray(m, BF) for m in _attn_placements()[:2])
    gather3, place_do = gather6[:3 * LANES], _delta_placement()

    def body(d_ref, o_ref, hs_ref, g3_ref, p_ref, do_ref):
        dv = d_ref[...]
        delta = sum(_dot(part.astype(BF), hs_ref[...]) for part in _split3(dv * o_ref[...]))
        side = _dot(jnp.concatenate([p.astype(BF) for p in _split3(-delta)], axis=1), g3_ref[...]).astype(BF)
        d_bf = dv.astype(BF)
        for h in range(N_HEADS):
            pair = slice((h // 2) * LANES, (h // 2 + 1) * LANES)
            do_ref[h] = _dot(jnp.concatenate([d_bf[:, pair], side], axis=1), p_ref[h]).astype(BF)

    whole = lambda a: pl.BlockSpec(a.shape, lambda i: (0,) * a.ndim)
    return pl.pallas_call(
        body, name="attn_bwd_prep", grid=(T // tr,),
        in_specs=[_row_spec(tr, ATTN_W), _row_spec(tr, ATTN_W), whole(head_sum), whole(gather3), whole(place_do)],
        out_specs=pl.BlockSpec((N_HEADS, tr, LANES), lambda i: (0, i, 0)),
        out_shape=jax.ShapeDtypeStruct((N_HEADS, T, LANES), BF), compiler_params=_params("parallel"),
    )(dya, of, head_sum, gather3, place_do)


def _delta_placement():
    place = np.zeros((N_HEADS, 2 * LANES, LANES), np.float32)
    d = np.arange(HEAD_DIM)
    for h in range(N_HEADS):
        place[h, (h % 2) * HEAD_DIM + d, d] = 1.0
        for j in range(3):
            place[h, LANES + j * N_HEADS + h, HEAD_DIM + j] = 1.0
    return jnp.asarray(place, BF)


def _attn_bwd(kl, vl, ql, do, last, chip_sums, *, tk=ATTN_TILE):
    T = ql.shape[1]
    tk = min(tk, T)
    n = T // tk
    n_steps = (N_HEADS // 2) * n
    m = len(chip_sums)

    def body(last_ref, kl_ref, vl_ref, ql_ref, do_ref, *refs):
        b_refs, (dq_ref, dk_ref, dv_ref, extq_ref, extk_ref), r_refs = refs[:m], refs[m:m + 5], refs[m + 5:2 * m + 5]
        dq_acc, dk_acc, dv_acc, send_sems, recv_sems = refs[2 * m + 5:]
        j = pl.program_id(1)
        tile = pl.program_id(0) * n + j
        scatter_start, scatter_finish = _scatter_phases(b_refs, r_refs, send_sems, recv_sems)
        pl.when(tile == 0)(scatter_start)

        @pl.when(j == 0)
        def _():
            dq_acc[...] = jnp.zeros_like(dq_acc)

        dk_acc[...] = jnp.zeros_like(dk_acc)
        dv_acc[...] = jnp.zeros_like(dv_acc)

        def block(hh, rows, mask):
            qi, di, k = ql_ref[hh, rows, :], do_ref[hh, rows, :], kl_ref[hh]
            p_t = jnp.exp(_dot_nt(k, qi))
            if mask is not None:
                p_t = jnp.where(mask, p_t, 0.0)
            ds_t = (p_t * _dot_nt(vl_ref[hh], di)).astype(BF)
            dk_acc[hh] += _dot(ds_t, qi)
            dv_acc[hh] += _dot(p_t.astype(BF), di)
            dq_acc[hh, rows, :] += _dot_tn(ds_t, k)

        causal_t = _causal(tk, keys_in_rows=True)
        for hh in range(2):
            block(hh, _tile_rows(j, tk), causal_t)

        def step(i, carry):
            for hh in range(2):
                block(hh, _tile_rows(i, tk), None)
            return carry

        lax.fori_loop(j + 1, last_ref[pl.program_id(0) * n + j].astype(jnp.int32) + 1, step, 0)
        dk_ref[...] = jnp.concatenate([dk_acc[hh][:, :HEAD_DIM] for hh in range(2)], axis=1).astype(BF)
        dv_ref[...] = jnp.concatenate([dv_acc[hh][:, :HEAD_DIM] for hh in range(2)], axis=1).astype(BF)
        extk_ref[...] = jnp.concatenate([dk_acc[hh][:, HEAD_DIM:] for hh in range(2)], axis=1)

        @pl.when(j == n - 1)
        def _():
            dq_ref[...] = jnp.concatenate([dq_acc[hh][:, :HEAD_DIM] * Q_SCALE for hh in range(2)], axis=1).astype(BF)
            extq_ref[...] = jnp.concatenate([dq_acc[hh][:, HEAD_DIM:] for hh in range(2)], axis=1)

        pl.when(tile == n_steps - 1)(scatter_finish)

    whole = pl.BlockSpec((T, LANES), lambda p, j, *_: (0, p))
    outs = pl.pallas_call(
        body, name="attn_bwd",
        grid_spec=pltpu.PrefetchScalarGridSpec(
            num_scalar_prefetch=1, grid=(N_HEADS // 2, n),
            in_specs=[_pair_block(tk), _pair_block(tk), _pair_full(T), _pair_full(T)] + [HBM] * m,
            out_specs=[whole, _packed_block(tk), _packed_block(tk), whole, _packed_block(tk)] + [HBM] * m,
            scratch_shapes=[pltpu.VMEM((2, T, LANES), F32), pltpu.VMEM((2, tk, LANES), F32), pltpu.VMEM((2, tk, LANES), F32)]
            + _scatter_semaphores(m)),
        out_shape=[jax.ShapeDtypeStruct((T, ATTN_W), BF)] * 3 + [jax.ShapeDtypeStruct((T, ATTN_W), F32)] * 2
        + _scattered_shapes(chip_sums),
        compiler_params=pltpu.CompilerParams(dimension_semantics=("arbitrary", "arbitrary"), vmem_limit_bytes=BIG_VMEM),
    )(last, kl, vl, ql, do, *chip_sums)
    return outs[:5], outs[5:]


def _forget_bwd(ext_q, ext_k, fl, b_forget, *, tp=256):
    T = fl.shape[0]
    tp = min(tp, T)
    n = T // tp

    def body(eq_ref, ek_ref, fl_ref, bf_ref, dfl_ref, dbf_ref, carry_ref):
        @pl.when(pl.program_id(0) == 0)
        def _():
            carry_ref[...] = jnp.zeros_like(carry_ref)
            dbf_ref[...] = jnp.zeros_like(dbf_ref)

        lane = lax.broadcasted_iota(jnp.int32, (tp, LANES), 1)
        eq, ek = eq_ref[...], ek_ref[...]
        cols = [eq[:, h * HEAD_DIM:h * HEAD_DIM + 1] - ek[:, h * HEAD_DIM + 3:h * HEAD_DIM + 4] for h in range(N_HEADS)]
        dcum = _with_lanes(jnp.zeros((tp, LANES), F32), lane, 0, cols)
        suffix = _scan_dot(_tri(tp, upper=True), dcum) + carry_ref[...]
        carry_ref[...] = suffix[0:1, :]
        x = fl_ref[...] + bf_ref[...]
        dfl = jnp.where(lane < N_HEADS, suffix / (1.0 + jnp.exp(x)), 0.0)
        dfl_ref[...] = dfl.astype(BF)
        dbf_ref[...] += jnp.sum(dfl, axis=0, keepdims=True)

    rev = lambda w: pl.BlockSpec((tp, w), lambda i: (n - 1 - i, 0))
    return pl.pallas_call(
        body, name="forget_bwd", grid=(n,),
        in_specs=[rev(ATTN_W), rev(ATTN_W), rev(LANES), _vec_spec(LANES)],
        out_specs=[rev(LANES), _vec_spec(LANES)],
        out_shape=[jax.ShapeDtypeStruct((T, LANES), BF), jax.ShapeDtypeStruct((1, LANES), F32)],
        scratch_shapes=[pltpu.VMEM((1, LANES), F32)], compiler_params=_params("arbitrary"),
    )(ext_q, ext_k, fl, b_forget)


def _adamw(w, g, m, v, *, name, tr=256):
    _, rows, cols = w.shape
    tr = tr if rows % tr == 0 else rows

    def body(w_ref, g_ref, m_ref, v_ref, go_ref, d_ref, nm_ref, nv_ref):
        gv = g_ref[...]
        go_ref[...] = gv
        nm = ADAM_B1 * m_ref[...] + (1.0 - ADAM_B1) * gv
        nv = ADAM_B2 * v_ref[...] + (1.0 - ADAM_B2) * (gv * gv)
        m_hat = nm / (1.0 - ADAM_B1 ** ADAM_STEP)
        v_hat = nv / (1.0 - ADAM_B2 ** ADAM_STEP)
        d_ref[...] = -ADAM_LR * (m_hat / (jnp.sqrt(v_hat) + ADAM_EPS) + ADAM_WD * w_ref[...])
        nm_ref[...] = nm
        nv_ref[...] = nv

    spec = pl.BlockSpec((None, tr, cols), lambda i: (0, i, 0))
    return pl.pallas_call(
        body, name=name, grid=(rows // tr,), in_specs=[spec, pl.BlockSpec((tr, cols), lambda i: (i, 0)), spec, spec],
        out_specs=[spec] * 4, out_shape=[jax.ShapeDtypeStruct((1, rows, cols), F32)] * 4,
        compiler_params=_params("parallel"),
    )(w, g, m, v)


HBM = pl.BlockSpec(memory_space=pltpu.HBM)
BF16_ROWS = 16


def _place():
    x, y, c = lax.axis_index("x"), lax.axis_index("y"), lax.axis_index("c")
    others = [(1 - x, y), (x, 1 - y), (1 - x, 1 - y)]
    return x, y, c, others


def _chip(xy):
    return 2 * xy[0] + xy[1]


def _row_halves(c, rows):
    half = rows // 2
    assert half % BF16_ROWS == 0
    return (pl.ds(pl.multiple_of(c * half, BF16_ROWS), half), pl.ds(pl.multiple_of((1 - c) * half, BF16_ROWS), half))


def _remote(src, dst, send_sems, recv_sems, k, to):
    return pltpu.make_async_remote_copy(src_ref=src, dst_ref=dst, send_sem=send_sems.at[k], recv_sem=recv_sems.at[k],
                                        device_id=to, device_id_type=MESH)


def _gathered_shapes(shards):
    return [jax.ShapeDtypeStruct((N_CHIPS,) + s.shape, s.dtype) for s in shards]


def _gather_semaphores(n):
    return [pltpu.SemaphoreType.DMA((6 * n,)), pltpu.SemaphoreType.DMA((6 * n,))]


def _gather_phases(w_refs, g_refs, send_sems, recv_sems):
    n = len(w_refs)
    x, y, c, others = _place()
    sibling, me = (x, y, 1 - c), _chip((x, y))
    halves = [_row_halves(c, w.shape[0]) for w in w_refs]

    def sent(a, j, o):
        mine, _ = halves[a]
        return _remote(w_refs[a].at[mine, :], g_refs[a].at[me, mine, :], send_sems, recv_sems, 6 * a + j, (*o, c))

    def passed(a, j, o):
        landed = g_refs[a].at[_chip(o), halves[a][0], :]
        return _remote(landed, landed, send_sems, recv_sems, 6 * a + 3 + j, sibling)

    def start():
        for a in range(n):
            for j, o in enumerate(others):
                sent(a, j, o).start()

    def forward():
        for j, o in enumerate(others):
            for a in range(n):
                landed = g_refs[a].at[_chip(o), halves[a][0], :]
                _remote(landed, landed, send_sems, recv_sems, 6 * a + j, (*o, c)).wait_recv()
                passed(a, j, o).start()

    def finish():
        for j, o in enumerate(others):
            for a in range(n):
                landed = g_refs[a].at[_chip(o), halves[a][1], :]
                _remote(landed, landed, send_sems, recv_sems, 6 * a + 3 + j, sibling).wait_recv()
        for a in range(n):
            for j, o in enumerate(others):
                sent(a, j, o).wait_send()
                passed(a, j, o).wait_send()

    return start, forward, finish


def _exchange_halves(arrays, *, name):
    n = len(arrays)

    def body(*refs):
        for phase in _exchange_phases(refs[:n], refs[n:2 * n], *refs[2 * n:]):
            phase()

    return pl.pallas_call(
        body, name=name, in_specs=[HBM] * n, out_specs=[HBM] * n, out_shape=_exchanged_shapes(arrays),
        scratch_shapes=_exchange_semaphores(n),
    )(*arrays)


def _exchanged_shapes(arrays):
    return [jax.ShapeDtypeStruct(s.shape[:-2] + (s.shape[-2] // 2, s.shape[-1]), F32) for s in arrays]


def _exchange_semaphores(n):
    return [pltpu.SemaphoreType.DMA((n,)), pltpu.SemaphoreType.DMA((n,))]


def _exchange_phases(g_refs, r_refs, send_sems, recv_sems):
    x, y, c, _ = _place()

    def copy(a):
        _, theirs = _row_halves(c, g_refs[a].shape[-2])
        src = g_refs[a].at[:, theirs, :] if len(g_refs[a].shape) == 3 else g_refs[a].at[theirs, :]
        return _remote(src, r_refs[a], send_sems, recv_sems, a, (x, y, 1 - c))

    def start():
        for a in range(len(g_refs)):
            copy(a).start()

    def finish():
        for a in range(len(g_refs)):
            copy(a).wait()

    return start, finish


def _scatter_to_owners(chip_sums):
    n = len(chip_sums)

    def body(*refs):
        for phase in _scatter_phases(refs[:n], refs[n:2 * n], *refs[2 * n:]):
            phase()

    return pl.pallas_call(
        body, name="scatter_to_owners", in_specs=[HBM] * n, out_specs=[HBM] * n,
        out_shape=_scattered_shapes(chip_sums), scratch_shapes=_scatter_semaphores(n),
    )(*chip_sums)


def _scattered_shapes(chip_sums):
    return [jax.ShapeDtypeStruct(b.shape if b.ndim == 3 else (N_CHIPS,) + b.shape, b.dtype) for b in chip_sums]


def _scatter_semaphores(n):
    return [pltpu.SemaphoreType.DMA((3 * n,)), pltpu.SemaphoreType.DMA((3 * n,))]


def _scatter_phases(b_refs, r_refs, send_sems, recv_sems):
    n = len(b_refs)
    x, y, c, others = _place()
    me = _chip((x, y))

    def sent(a, j, o):
        src = b_refs[a].at[_chip(o)] if len(b_refs[a].shape) == 3 else b_refs[a]
        return _remote(src, r_refs[a].at[me], send_sems, recv_sems, 3 * a + j, (*o, c))

    def start():
        for a in range(n):
            for j, o in enumerate(others):
                sent(a, j, o).start()

    def finish():
        for a in range(n):
            for j, o in enumerate(others):
                landed = r_refs[a].at[_chip(o)]
                _remote(landed, landed, send_sems, recv_sems, 3 * a + j, (*o, c)).wait_recv()
        for a in range(n):
            for j, o in enumerate(others):
                sent(a, j, o).wait_send()

    return start, finish


def _join_halves(totals):
    n = len(totals)

    def body(*refs):
        in_refs, out_refs, (send_sems, recv_sems) = refs[:n], refs[n:2 * n], refs[2 * n:]
        x, y, c, _ = _place()
        copies = []
        for a in range(n):
            mine, _ = _row_halves(c, in_refs[a].shape[0])
            copies.append(_remote(in_refs[a].at[mine, :], out_refs[a].at[mine, :], send_sems, recv_sems, a, (x, y, 1 - c)))
            copies[-1].start()
        for cp in copies:
            cp.wait()

    return pl.pallas_call(
        body, name="join_halves", in_specs=[HBM] * n, out_specs=[HBM] * n,
        out_shape=[jax.ShapeDtypeStruct(t.shape, F32) for t in totals], input_output_aliases={a: a for a in range(n)},
        scratch_shapes=[pltpu.SemaphoreType.DMA((n,)), pltpu.SemaphoreType.DMA((n,))],
    )(*totals)


ADD_ROWS = 128


def _add_sibling(g, r, place, *, name):
    lead, (half, cols) = g.shape[:-2], r.shape[-2:]
    tr = min(ADD_ROWS, half)
    nb = half // tr
    zeros = (0,) * len(lead)

    def body(place_ref, g_ref, r_ref, o_ref, ob_ref):
        s = g_ref[...] + r_ref[...]
        o_ref[...] = s
        ob_ref[...] = s.astype(BF)

    spec = pl.BlockSpec(lead + (tr, cols), lambda i, p: zeros + (i, 0))
    return pl.pallas_call(
        body, name=name,
        grid_spec=pltpu.PrefetchScalarGridSpec(
            num_scalar_prefetch=1, grid=(nb,),
            in_specs=[pl.BlockSpec(lead + (tr, cols), lambda i, p: zeros + (p[1] * nb + i, 0)), spec], out_specs=[spec, spec]),
        out_shape=[jax.ShapeDtypeStruct(r.shape, F32), jax.ShapeDtypeStruct(r.shape, BF)],
        compiler_params=_params("parallel"),
    )(place, g, r)


def _add_chips(own, received, place, *, name, own_slots):
    half, cols = received.shape[-2:]
    tr = min(ADD_ROWS, half)
    nb = half // tr

    def written(k, p):
        return jnp.where(p[0] == k, (k + 1) % N_CHIPS, k)

    def body(place_ref, own_ref, *refs):
        o_ref = refs[N_CHIPS]
        mine = own_ref[0] if own_slots else own_ref[...]
        if own_slots:
            acc = mine
            for k in range(N_CHIPS):
                acc = acc + jnp.where(place_ref[0] == k, 0.0, refs[k][0].astype(F32))
        else:
            terms = [jnp.where(place_ref[0] == k, mine, refs[k][0]) for k in range(N_CHIPS)]
            acc = ((terms[0] + terms[1]) + terms[2]) + terms[3]
        o_ref[...] = acc

    own_spec = (pl.BlockSpec((1, tr, cols), lambda i, p: (p[0], i, 0)) if own_slots
                else pl.BlockSpec((tr, cols), lambda i, p: (i, 0)))
    return pl.pallas_call(
        body, name=name,
        grid_spec=pltpu.PrefetchScalarGridSpec(
            num_scalar_prefetch=1, grid=(nb,),
            in_specs=[own_spec] + [pl.BlockSpec((1, tr, cols), functools.partial(lambda i, p, k: (written(k, p), i, 0), k=k))
                                   for k in range(N_CHIPS)],
            out_specs=pl.BlockSpec((tr, cols), lambda i, p: (p[1] * nb + i, 0))),
        out_shape=jax.ShapeDtypeStruct((2 * half, cols), F32), compiler_params=_params("parallel"),
    )(place, own, *([received] * N_CHIPS))


SHARDED = (("w_in", (D_MODEL, 4616), 1), ("w_branch_sgu", (SGU_W, D_MODEL), 1), ("w_branch_attn", (ATTN_W, D_MODEL), 1),
           ("w_out", (D_MODEL, D_MODEL), 0), ("w_up", (D_MODEL, D_FF), 1), ("w_down", (D_FF, D_MODEL), 0))
SMALL = (("g_mix_pre", (1, D_MODEL)), ("b_forget", (1, N_HEADS)), ("g_sgu", (1, SGU_W)), ("b_sgu", (1, SGU_W)),
         ("w_spatial", (N_GROUPS * CHUNK, CHUNK)), ("b_spatial", (N_GROUPS, CHUNK)), ("g_mix_post", (1, D_MODEL)),
         ("g_ffn_pre", (1, D_MODEL)), ("g_ffn_post", (1, D_MODEL)))
SMALL_ALIGN = 2 * ADD_ROWS


def _shard_shape(shape, axis):
    return tuple(s // N_CHIPS if a == axis else s for a, s in enumerate(shape))


def _slots_to_full(slots, axis):
    return slots.reshape(-1, slots.shape[2]) if axis == 0 else slots.transpose(1, 0, 2).reshape(slots.shape[1], -1)


def _full_to_slots(full, axis):
    if axis == 0:
        return full.reshape(N_CHIPS, -1, full.shape[1])
    return full.reshape(full.shape[0], N_CHIPS, -1).transpose(1, 0, 2)


def _small_rows(shape):
    return -(-(shape[0] * shape[1]) // (8 * LANES)) * 8


def _pack_small(values):
    parts = []
    for name, shape in SMALL:
        flat = values[name].reshape(-1)
        n = _small_rows(shape)
        parts.append(jnp.pad(flat, (0, n * LANES - flat.shape[0])).reshape(n, LANES))
    rows = sum(p.shape[0] for p in parts)
    pad = -(-rows // SMALL_ALIGN) * SMALL_ALIGN - rows
    return jnp.concatenate(parts + [jnp.zeros((pad, LANES), F32)], axis=0)


def _unpack_small(packed):
    out, row = {}, 0
    for name, shape in SMALL:
        n = _small_rows(shape)
        out[name] = packed[row:row + n].reshape(-1)[:shape[0] * shape[1]].reshape(shape)
        row += n
    return out


IN_Z, IN_Q, IN_K, IN_V, IN_F, IN_G, IN_END = 0, 1024, 1536, 2048, 2560, 2568, 4616


LATE_WEIGHTS = ("w_branch_sgu", "w_branch_attn", "w_out", "w_up", "w_down")
EARLY_GRADS = LATE_WEIGHTS


def _with_own_slot(shard, gathered, chip):
    return jnp.where(jnp.arange(N_CHIPS)[:, None, None] == chip, shard[None], gathered)


def _assemble(name, shard, gathered, chip):
    axis = {n: a for n, _, a in SHARDED}[name]
    return _slots_to_full(_with_own_slot(shard, gathered, chip), axis)


def _columns_from_slots(slots, bounds):
    width = slots.shape[2]
    pieces = []
    for lo, hi in zip(bounds[:-1], bounds[1:], strict=True):
        parts = [slots[k][:, max(lo, k * width) - k * width:min(hi, (k + 1) * width) - k * width]
                 for k in range(N_CHIPS) if max(lo, k * width) < min(hi, (k + 1) * width)]
        pieces.append(parts[0] if len(parts) == 1 else jnp.concatenate(parts, axis=1))
    return pieces


def _columns_to_slots(pieces):
    width = sum(p.shape[1] for p in pieces) // N_CHIPS
    slots = []
    for k in range(N_CHIPS):
        parts, start = [], 0
        for p in pieces:
            lo, hi = max(k * width, start), min((k + 1) * width, start + p.shape[1])
            if lo < hi:
                parts.append(p[:, lo - start:hi - start])
            start += p.shape[1]
        slots.append(jnp.concatenate(parts, axis=1))
    return jnp.stack(slots)


def _local_step(x, target, shards, small, place):
    b_forget = jnp.pad(small["b_forget"], ((0, 0), (0, LANES - N_HEADS)))
    causal = jnp.tril(jnp.ones((CHUNK, CHUNK), bool))
    ws = jnp.where(causal[None], small["w_spatial"].reshape(N_GROUPS, CHUNK, CHUNK), 0.0).astype(BF)
    ws_t = ws.transpose(0, 2, 1)
    bias_plane = jnp.repeat(small["b_spatial"].T, HEAD_DIM, axis=1)

    xn, (w_in_slots,) = _rms_fwd(x, small["g_mix_pre"], [shards["w_in"]])
    w_z, w_q, w_k, w_v, w_f, w_ga, w_gb = _columns_from_slots(
        _with_own_slot(shards["w_in"], w_in_slots, place[0]), (IN_Z, IN_Q, IN_K, IN_V, IN_F, IN_G, IN_G + D_MODEL, IN_END))
    w_qkv, w_g = jnp.concatenate([w_q, w_k, w_v], axis=1), jnp.concatenate([w_ga, w_gb], axis=1)
    w_f = jnp.pad(w_f, ((0, 0), (0, LANES - N_HEADS)))
    z, qkv, gl, fl = _project(xn, [w_z, w_qkv, w_g, w_f], [F32, BF, BF, F32], name="proj_in")
    ysgu = _sgu_fwd(z, small["g_sgu"], small["b_sgu"], ws, bias_plane)
    qf, kl, vl, tile_stats = _attn_prep(qkv, fl, b_forget)
    first_key_tile, last_query_tile, bounded = _attn_ranges(tile_stats)
    yattn, yattn_f, ql, gathered = _attn_fwd(qf, kl, vl, first_key_tile, bounded, [shards[name] for name in LATE_WEIGHTS])
    w = {name: _assemble(name, shards[name], got, place[0]) for name, got in zip(LATE_WEIGHTS, gathered, strict=True)}
    a, b, merged = _branch_merge(ysgu, yattn, w["w_branch_sgu"], w["w_branch_attn"], gl)
    o, h1, xn2 = _matmul_rows(
        [(merged, w["w_out"])], nt=False, rows=[x], vecs=[small["g_mix_post"], small["g_ffn_pre"]], row_outs=[F32, F32, BF],
        n_sums=0, epilogue=_mixer_out_fwd, name="proj_out_norms")

    (hid,) = _project(xn2, [w["w_up"]], [BF], name="ffn_up", tm=FFN_ROWS, epilogue=lambda acc: jnp.square(jnp.maximum(acc, 0.0)))
    dy, ddn, sq, dg_ffn_post = _matmul_rows(
        [(hid, w["w_down"])], nt=False, rows=[h1, target], vecs=[small["g_ffn_post"]], row_outs=[F32, BF], n_sums=2,
        epilogue=_loss_head, name="ffn_down_loss")

    (dup,) = _project(ddn, [w["w_down"]], [BF], name="ffn_down_bwd", tm=FFN_ROWS, nt=True, extra=hid,
                      epilogue=lambda acc, h: acc * (2.0 * jnp.sqrt(h.astype(F32))))
    dw_down = _matmul_tn(hid, ddn, name="dw_down")
    dh1, do, dg_ffn_pre, dg_mix_post = _matmul_rows(
        [(dup, w["w_up"])], nt=True, rows=[h1, dy, o], vecs=[small["g_ffn_pre"], small["g_mix_post"]], row_outs=[F32, BF],
        n_sums=2, epilogue=_mixer_out_bwd, name="ffn_up_bwd_norms")
    dw_up = _matmul_tn(xn2, dup, name="dw_up", slots=True)

    def gate_bwd(dm, a_t, b_t, gla, glb):
        ga, gb = jax.nn.sigmoid(gla.astype(F32)), jax.nn.sigmoid(glb.astype(F32))
        return dm * ga, dm * gb, dm * a_t.astype(F32) * (ga * (1.0 - ga)), dm * b_t.astype(F32) * (gb * (1.0 - gb))

    da, db, dgla, dglb = _matmul([(do, w["w_out"])], nt=True, out_dtypes=[BF] * 4, name="proj_out_bwd",
                                 epilogue=gate_bwd, extras=[a, b, (gl, 0), (gl, D_MODEL)])
    dw_out = _matmul_tn(merged, do, name="dw_out")
    dysgu = _matmul([(da, w["w_branch_sgu"])], nt=True, out_dtypes=[F32], name="branch_sgu_bwd")
    dyattn = _matmul([(db, w["w_branch_attn"])], nt=True, out_dtypes=[F32], name="branch_attn_bwd")
    dw_bs = _matmul_tn(ysgu, da, name="dw_branch_sgu")
    dw_ba = _matmul_tn(yattn, db, name="dw_branch_attn")
    early = {"w_branch_sgu": _full_to_slots(dw_bs, 1), "w_branch_attn": _full_to_slots(dw_ba, 1),
             "w_out": _full_to_slots(dw_out, 0), "w_up": dw_up, "w_down": _full_to_slots(dw_down, 0)}
    (dz, dws, dbs, dg_sgu, db_sgu), early_theirs = _sgu_bwd(
        dysgu, z, small["g_sgu"], small["b_sgu"], ws, ws_t, bias_plane, [early[name] for name in EARLY_GRADS])
    early_sums = {name: _add_sibling(early[name], theirs, place, name="add_sibling_" + name)
                  for name, theirs in zip(EARLY_GRADS, early_theirs, strict=True)}
    dout = _attn_bwd_prep(dyattn, yattn_f)
    (dq, dk, dv, ext_q, ext_k), early_received = _attn_bwd(
        kl, vl, ql, dout, last_query_tile, [early_sums[name][1] for name in EARLY_GRADS])
    dfl, dbf = _forget_bwd(ext_q, ext_k, fl, b_forget)
    dw_z, dw_q, dw_k, dw_v, dw_f = _matmul_tn_multi(xn, [dz, dq, dk, dv, dfl], name="dw_in_mix")
    dw_ga, dw_gb = _matmul_tn_multi(xn, [dgla, dglb], name="dw_in_gates")
    dw_in = _columns_to_slots([dw_z, dw_q, dw_k, dw_v, dw_f[:, :N_HEADS], dw_ga, dw_gb])
    (dw_in_theirs,) = _exchange_halves([dw_in], name="exchange_halves_w_in")
    dw_in_sum = _add_sibling(dw_in, dw_in_theirs, place, name="add_sibling_w_in")
    dx, dg_mix_pre, dw_in_received = _matmul_rows(
        [(dz, w_z), (dq, w_q), (dk, w_k), (dv, w_v), (dgla, w_ga), (dglb, w_gb), (dfl, w_f)],
        nt=True, rows=[x, dh1], vecs=[small["g_mix_pre"]], row_outs=[F32], n_sums=1, epilogue=_input_norm_bwd,
        name="proj_in_bwd_norm", scatter=[dw_in_sum[1]])

    reduced = {name: (early_sums[name][0], got) for name, got in zip(EARLY_GRADS, early_received, strict=True)}
    reduced["w_in"] = (dw_in_sum[0], dw_in_received)
    small_grads = {"g_mix_pre": dg_mix_pre, "b_forget": dbf[:, :N_HEADS], "g_sgu": dg_sgu, "b_sgu": db_sgu,
                   "w_spatial": dws.reshape(N_GROUPS * CHUNK, CHUNK), "b_spatial": dbs[:, :N_GROUPS].T,
                   "g_mix_post": dg_mix_post, "g_ffn_pre": dg_ffn_pre, "g_ffn_post": dg_ffn_post}
    return sq, dx, reduced, small_grads


NAMES = ("g_mix_pre", "w_in", "b_forget", "g_sgu", "b_sgu", "w_spatial", "b_spatial", "w_branch_sgu", "w_branch_attn",
         "w_out", "g_mix_post", "g_ffn_pre", "w_up", "w_down", "g_ffn_post")


def kernel(x, g_mix_pre, w_in, b_forget, g_sgu, b_sgu, w_spatial, b_spatial, w_branch_sgu, w_branch_attn, w_out, g_mix_post, g_ffn_pre, w_up, w_down, g_ffn_post, loss_target, m_g_mix_pre, m_w_in, m_b_forget, m_g_sgu, m_b_sgu, m_w_spatial, m_b_spatial, m_w_branch_sgu, m_w_branch_attn, m_w_out, m_g_mix_post, m_g_ffn_pre, m_w_up, m_w_down, m_g_ffn_post, v_g_mix_pre, v_w_in, v_b_forget, v_g_sgu, v_b_sgu, v_w_spatial, v_b_spatial, v_w_branch_sgu, v_w_branch_attn, v_w_out, v_g_mix_post, v_g_ffn_pre, v_w_up, v_w_down, v_g_ffn_post):
    weights = dict(zip(NAMES, (g_mix_pre, w_in, b_forget, g_sgu, b_sgu, w_spatial, b_spatial, w_branch_sgu, w_branch_attn,
                               w_out, g_mix_post, g_ffn_pre, w_up, w_down, g_ffn_post), strict=True))
    first = dict(zip(NAMES, (m_g_mix_pre, m_w_in, m_b_forget, m_g_sgu, m_b_sgu, m_w_spatial, m_b_spatial, m_w_branch_sgu,
                             m_w_branch_attn, m_w_out, m_g_mix_post, m_g_ffn_pre, m_w_up, m_w_down, m_g_ffn_post), strict=True))
    second = dict(zip(NAMES, (v_g_mix_pre, v_w_in, v_b_forget, v_g_sgu, v_b_sgu, v_w_spatial, v_b_spatial, v_w_branch_sgu,
                              v_w_branch_attn, v_w_out, v_g_mix_post, v_g_ffn_pre, v_w_up, v_w_down, v_g_ffn_post), strict=True))
    shard_shapes = {name: _shard_shape(shape, axis) for name, shape, axis in SHARDED}
    small_shapes = dict(SMALL)
    view = lambda name, a: a.reshape(shard_shapes.get(name) or small_shapes[name])

    place = jnp.stack([2 * lax.axis_index("x") + lax.axis_index("y"), lax.axis_index("c")]).astype(jnp.int32)

    shards = {name: view(name, weights[name]).astype(BF) for name, _, _ in SHARDED}
    small = {name: view(name, weights[name]) for name, _ in SMALL}
    sq, dx, reduced, small_grads = _local_step(x[0], loss_target[0], shards, small, place)
    loss = lax.psum(0.5 * jnp.sum(sq) / D_MODEL, ("x", "y", "c"))

    small_mine = _pack_small(small_grads)
    (small_theirs,) = _exchange_halves([small_mine], name="exchange_halves_small")
    small_sum, _ = _add_sibling(small_mine, small_theirs, place, name="add_sibling_small")
    (small_received,) = _scatter_to_owners([small_sum])
    totals = {name: _add_chips(s, r, place, name="add_chips_" + name, own_slots=True) for name, (s, r) in reduced.items()}
    small_total = _add_chips(small_sum, small_received, place, name="add_chips_small", own_slots=False)
    joined = _join_halves([totals[name] for name, _, _ in SHARDED] + [small_total])
    grad = {**{name: g for (name, _, _), g in zip(SHARDED, joined[:-1], strict=True)}, **_unpack_small(joined[-1])}

    grad_out, delta, new_m, new_v = {}, {}, {}, {}
    for name in NAMES:
        rows, cols = grad[name].shape
        as_given = lambda a: a.reshape(1, rows, cols)
        grad_out[name], delta[name], new_m[name], new_v[name] = _adamw(
            as_given(weights[name]), grad[name], as_given(first[name]), as_given(second[name]), name="adamw_" + name)

    like = lambda d: [d[name].reshape(weights[name].shape) for name in NAMES]
    return (loss, dx[None], *like(grad_out), *like(delta), *like(new_m), *like(new_v))
```

```python
import functools

import jax
import jax.numpy as jnp
import numpy as np
from jax import lax
from jax.experimental import pallas as pl
from jax.experimental.pallas import tpu as pltpu

F32 = jnp.float32
BF = jnp.bfloat16
MESH = pl.DeviceIdType.MESH

D_MODEL = 1024
N_HEADS = 8
HEAD_DIM = 64
ATTN_W = N_HEADS * HEAD_DIM
SGU_W = 512
N_GROUPS = 8
CHUNK = 128
D_FF = 4096
EPS = 1e-6
Q_SCALE = HEAD_DIM ** -0.5
N_CHIPS = 4
LANES = 128

ADAM_LR = 0.001
ADAM_B1 = 0.9
ADAM_B2 = 0.999
ADAM_EPS = 1e-08
ADAM_WD = 0.01
ADAM_STEP = 10

VMEM_LIMIT = 48 * 1024 * 1024
BIG_VMEM = 58 * 1024 * 1024
NEG = -1e30


def _params(*sem):
    return pltpu.CompilerParams(dimension_semantics=sem, vmem_limit_bytes=VMEM_LIMIT)


def _dot(a, b):
    return jnp.dot(a, b, preferred_element_type=F32)


def _dot_nt(a, b):
    return lax.dot_general(a, b, (((1,), (1,)), ((), ())), preferred_element_type=F32)


def _dot_tn(a, b):
    return lax.dot_general(a, b, (((0,), (0,)), ((), ())), preferred_element_type=F32)


def _split3(c):
    hi = c.astype(BF).astype(F32)
    r = c - hi
    mid = r.astype(BF).astype(F32)
    lo = (r - mid).astype(BF).astype(F32)
    return hi, mid, lo


def _gelu(x):
    k = 0.7978845608028654
    return 0.5 * x * (1.0 + jnp.tanh(k * (x + 0.044715 * (x * x * x))))


def _gelu_grad(x):
    k = 0.7978845608028654
    x2 = x * x
    t = jnp.tanh(k * (x + 0.044715 * (x2 * x)))
    return 0.5 * (1.0 + t) + 0.5 * x * (1.0 - t * t) * (k * (1.0 + 3.0 * 0.044715 * x2))


def _rms_bwd(a, g, dy):
    r = lax.rsqrt(jnp.mean(a * a, axis=-1, keepdims=True) + EPS)
    n = a * r
    dn = dy * g
    da = r * (dn - n * jnp.mean(dn * n, axis=-1, keepdims=True))
    return da, dy * n


MM_ROWS = 1024
MM_COLS = 512
FFN_ROWS = 512

def _matmul(pairs, *, nt, out_dtypes, name, tm=MM_ROWS, tn=MM_COLS, epilogue=None, extras=()):
    n_pairs, n_extra = len(pairs), len(extras)
    M = pairs[0][0].shape[0]
    N = pairs[0][1].shape[0] if nt else pairs[0][1].shape[1]
    tm, tn = min(tm, M), min(tn, N)
    assert M % tm == 0 and N % tn == 0

    def body(*refs):
        acc = None
        for p in range(n_pairs):
            a_ref, b_ref = refs[2 * p], refs[2 * p + 1]
            d = _dot_nt(a_ref[...], b_ref[...]) if nt else _dot(a_ref[...], b_ref[...])
            acc = d if acc is None else acc + d
        e_refs = refs[2 * n_pairs:2 * n_pairs + n_extra]
        o_refs = refs[2 * n_pairs + n_extra:]
        outs = (acc,) if epilogue is None else epilogue(acc, *[e[...] for e in e_refs])
        for o_ref, o in zip(o_refs, outs, strict=True):
            o_ref[...] = o.astype(o_ref.dtype)

    in_specs, args = [], []
    for a, b in pairs:
        K = a.shape[1]
        in_specs.append(pl.BlockSpec((tm, K), lambda i, j: (i, 0)))
        in_specs.append(pl.BlockSpec((tn, K), lambda i, j: (j, 0)) if nt else pl.BlockSpec((K, tn), lambda i, j: (0, j)))
        args += [a, b]
    for e in extras:
        e, col = e if isinstance(e, tuple) else (e, 0)
        in_specs.append(pl.BlockSpec((tm, tn), functools.partial(lambda i, j, off: (i, j + off), off=col // tn)))
        args.append(e)
    outs = pl.pallas_call(
        body, name=name, grid=(M // tm, N // tn), in_specs=in_specs,
        out_specs=[pl.BlockSpec((tm, tn), lambda i, j: (i, j)) for _ in out_dtypes],
        out_shape=[jax.ShapeDtypeStruct((M, N), dt) for dt in out_dtypes],
        compiler_params=_params("parallel", "parallel"),
    )(*args)
    return outs if len(outs) > 1 else outs[0]


def _matmul_tn_multi(a, bs, *, name, tk=1024):
    T, K1 = a.shape
    tk = min(tk, T)
    n = len(bs)

    def body(a_ref, *refs):
        @pl.when(pl.program_id(0) == 0)
        def _():
            for o_ref in refs[n:]:
                o_ref[...] = jnp.zeros_like(o_ref)

        av = a_ref[...]
        for b_ref, o_ref in zip(refs[:n], refs[n:], strict=True):
            o_ref[...] += _dot_tn(av, b_ref[...])

    return pl.pallas_call(
        body, name=name, grid=(T // tk,),
        in_specs=[pl.BlockSpec((tk, K1), lambda k: (k, 0))] + [pl.BlockSpec((tk, b.shape[1]), lambda k: (k, 0)) for b in bs],
        out_specs=[pl.BlockSpec((K1, b.shape[1]), lambda k: (0, 0)) for b in bs],
        out_shape=[jax.ShapeDtypeStruct((K1, b.shape[1]), F32) for b in bs],
        compiler_params=pltpu.CompilerParams(dimension_semantics=("arbitrary",), vmem_limit_bytes=BIG_VMEM),
    )(a, *bs)


def _project(a, weights, out_dtypes, *, name, tm=512, nt=False, epilogue=None, extra=None):
    M, K = a.shape
    tm = min(tm, M)
    n = len(weights)
    widths = [w.shape[0] if nt else w.shape[1] for w in weights]
    extras = [] if extra is None else [extra]

    def body(a_ref, *refs):
        av = a_ref[...]
        w_refs, e_refs, o_refs = refs[:n], refs[n:n + len(extras)], refs[n + len(extras):]
        for w_ref, o_ref in zip(w_refs, o_refs, strict=True):
            acc = _dot_nt(av, w_ref[...]) if nt else _dot(av, w_ref[...])
            if epilogue is not None:
                acc = epilogue(acc, *[e[...] for e in e_refs])
            o_ref[...] = acc.astype(o_ref.dtype)

    return pl.pallas_call(
        body, name=name, grid=(M // tm,),
        in_specs=[pl.BlockSpec((tm, K), lambda i: (i, 0))]
        + [pl.BlockSpec(w.shape, lambda i: (0, 0), pipeline_mode=pl.Buffered(1)) for w in weights]
        + [pl.BlockSpec((tm, e.shape[1]), lambda i: (i, 0)) for e in extras],
        out_specs=[pl.BlockSpec((tm, width), lambda i: (i, 0)) for width in widths],
        out_shape=[jax.ShapeDtypeStruct((M, width), dt) for width, dt in zip(widths, out_dtypes, strict=True)],
        compiler_params=_params("parallel"),
    )(a, *weights, *extras)


def _matmul_tn(a, b, *, name, tm=1024, tn=1024, tk=2048, slots=False):
    T, K1 = a.shape
    N = b.shape[1]
    tm, tn, tk = min(tm, K1), min(tn, N // N_CHIPS if slots else N), min(tk, T)
    assert K1 % tm == 0 and (N // N_CHIPS if slots else N) % tn == 0 and T % tk == 0
    per_slot = N // N_CHIPS // tn

    def body(a_ref, b_ref, o_ref):
        @pl.when(pl.program_id(2) == 0)
        def _():
            o_ref[...] = jnp.zeros_like(o_ref)

        o_ref[...] += _dot_tn(a_ref[...], b_ref[...])

    if slots:
        out_spec = pl.BlockSpec((None, tm, tn), lambda i, j, k: (j // per_slot, i, j % per_slot))
        out_shape = jax.ShapeDtypeStruct((N_CHIPS, K1, N // N_CHIPS), F32)
    else:
        out_spec = pl.BlockSpec((tm, tn), lambda i, j, k: (i, j))
        out_shape = jax.ShapeDtypeStruct((K1, N), F32)
    return pl.pallas_call(
        body, name=name, grid=(K1 // tm, N // tn, T // tk),
        in_specs=[pl.BlockSpec((tk, tm), lambda i, j, k: (k, i)), pl.BlockSpec((tk, tn), lambda i, j, k: (k, j))],
        out_specs=out_spec, out_shape=out_shape,
        compiler_params=_params("parallel", "parallel", "arbitrary"),
    )(a, b)


def _branch_merge(ysgu, yattn, w_bs, w_ba, gl, *, tm=MM_ROWS, tn=MM_COLS):
    T = ysgu.shape[0]
    tm = min(tm, T)
    nj = D_MODEL // tn

    def body(ys_ref, ya_ref, wbs_ref, wba_ref, gla_ref, glb_ref, a_ref, b_ref, m_ref):
        a = _dot(ys_ref[...], wbs_ref[...])
        b = _dot(ya_ref[...], wba_ref[...])
        a_ref[...] = a.astype(BF)
        b_ref[...] = b.astype(BF)
        m_ref[...] = (jax.nn.sigmoid(gla_ref[...].astype(F32)) * a + jax.nn.sigmoid(glb_ref[...].astype(F32)) * b).astype(BF)

    return pl.pallas_call(
        body, name="branch_merge", grid=(T // tm, nj),
        in_specs=[
            pl.BlockSpec((tm, SGU_W), lambda i, j: (i, 0)),
            pl.BlockSpec((tm, ATTN_W), lambda i, j: (i, 0)),
            pl.BlockSpec((SGU_W, tn), lambda i, j: (0, j)),
            pl.BlockSpec((ATTN_W, tn), lambda i, j: (0, j)),
            pl.BlockSpec((tm, tn), lambda i, j: (i, j)),
            pl.BlockSpec((tm, tn), lambda i, j: (i, j + nj)),
        ],
        out_specs=[pl.BlockSpec((tm, tn), lambda i, j: (i, j))] * 3,
        out_shape=[jax.ShapeDtypeStruct((T, D_MODEL), BF)] * 3,
        compiler_params=_params("parallel", "parallel"),
    )(ysgu, yattn, w_bs, w_ba, gl, gl)


def _row_spec(tr, width):
    return pl.BlockSpec((tr, width), lambda i: (i, 0))


def _vec_spec(width):
    return pl.BlockSpec((1, width), lambda i: (0, 0))


def _rms_fwd(x, g, shards, *, tr=256):
    T = x.shape[0]
    tr = min(tr, T)
    n_steps = T // tr
    k = len(shards)

    def body(x_ref, g_ref, *refs):
        step = pl.program_id(0)
        gather_start, gather_forward, gather_finish = _gather_phases(refs[:k], refs[k + 1:2 * k + 1], *refs[2 * k + 1:])
        pl.when(step == 0)(gather_start)
        pl.when(step == (3 * n_steps) // 4)(gather_forward)
        xv = x_ref[...]
        r = lax.rsqrt(jnp.mean(xv * xv, axis=-1, keepdims=True) + EPS)
        refs[k][...] = ((xv * r) * g_ref[...]).astype(BF)
        pl.when(step == n_steps - 1)(gather_finish)

    outs = pl.pallas_call(
        body, name="rms_fwd", grid=(n_steps,),
        in_specs=[_row_spec(tr, D_MODEL), _vec_spec(D_MODEL)] + [HBM] * k, out_specs=[_row_spec(tr, D_MODEL)] + [HBM] * k,
        out_shape=[jax.ShapeDtypeStruct((T, D_MODEL), BF)] + _gathered_shapes(shards),
        scratch_shapes=_gather_semaphores(k), compiler_params=_params("arbitrary"),
    )(x, g, *shards)
    return outs[0], outs[1:]


def _mixer_out_fwd(o, x, g_post, g_pre):
    r = lax.rsqrt(jnp.mean(o * o, axis=-1, keepdims=True) + EPS)
    h1 = x + (o * r) * g_post
    r2 = lax.rsqrt(jnp.mean(h1 * h1, axis=-1, keepdims=True) + EPS)
    return o, h1, (h1 * r2) * g_pre


def _matmul_rows(pairs, *, nt, rows, vecs, row_outs, n_sums, epilogue, name, tm=512, scatter=()):
    M = pairs[0][0].shape[0]
    N = pairs[0][1].shape[0] if nt else pairs[0][1].shape[1]
    tm = min(tm, M)
    n_steps = M // tm
    n_pairs, n_rows, n_vecs, n_out, n_scatter = len(pairs), len(rows), len(vecs), len(row_outs), len(scatter)

    def body(*refs):
        groups, at = [], 2 * n_pairs
        for count in (n_rows, n_vecs, n_scatter, n_out, n_sums, n_scatter):
            groups.append(refs[at:at + count])
            at += count
        r_refs, v_refs, b_refs, o_refs, s_refs, got_refs = groups
        sems = refs[at:]
        step = pl.program_id(0)
        if n_scatter:
            scatter_start, scatter_finish = _scatter_phases(b_refs, got_refs, *sems)
            pl.when(step == 0)(scatter_start)

        @pl.when(step == 0)
        def _():
            for s_ref in s_refs:
                s_ref[...] = jnp.zeros_like(s_ref)

        acc = None
        for p in range(n_pairs):
            a_ref, b_ref = refs[2 * p], refs[2 * p + 1]
            d = _dot_nt(a_ref[...], b_ref[...]) if nt else _dot(a_ref[...], b_ref[...])
            acc = d if acc is None else acc + d
        outs = epilogue(acc, *[r[...] for r in r_refs], *[v[...] for v in v_refs])
        for o_ref, o in zip(o_refs, outs[:n_out], strict=True):
            o_ref[...] = o.astype(o_ref.dtype)
        for s_ref, term in zip(s_refs, outs[n_out:], strict=True):
            s_ref[...] += jnp.sum(term, axis=0, keepdims=True)
        if n_scatter:
            pl.when(step == n_steps - 1)(scatter_finish)

    in_specs, args = [], []
    for a, b in pairs:
        in_specs += [_row_spec(tm, a.shape[1]), pl.BlockSpec(b.shape, lambda i: (0, 0))]
        args += [a, b]
    outs = pl.pallas_call(
        body, name=name, grid=(n_steps,),
        in_specs=in_specs + [_row_spec(tm, N)] * n_rows + [_vec_spec(N)] * n_vecs + [HBM] * n_scatter,
        out_specs=[_row_spec(tm, N)] * n_out + [_vec_spec(N)] * n_sums + [HBM] * n_scatter,
        out_shape=[jax.ShapeDtypeStruct((M, N), dt) for dt in row_outs] + [jax.ShapeDtypeStruct((1, N), F32)] * n_sums
        + (_scattered_shapes(scatter) if n_scatter else []),
        scratch_shapes=_scatter_semaphores(n_scatter) if n_scatter else [],
        compiler_params=pltpu.CompilerParams(dimension_semantics=("arbitrary",), vmem_limit_bytes=BIG_VMEM),
    )(*args, *rows, *vecs, *scatter)
    return outs


def _loss_head(dn, h1, target, g):
    r = lax.rsqrt(jnp.mean(dn * dn, axis=-1, keepdims=True) + EPS)
    err = h1 + (dn * r) * g - target
    dy = err * (1.0 / D_MODEL)
    ddn, dg_terms = _rms_bwd(dn, g, dy)
    return dy, ddn, err * err, dg_terms


def _mixer_out_bwd(dxn2, h1, dy, o, g_pre, g_post):
    da, dg_pre_terms = _rms_bwd(h1, g_pre, dxn2)
    dh1 = dy + da
    do, dg_post_terms = _rms_bwd(o, g_post, dh1)
    return dh1, do, dg_pre_terms, dg_post_terms


def _input_norm_bwd(dxn, x, dh1, g):
    da, dg_terms = _rms_bwd(x, g, dxn)
    return dh1 + da, dg_terms


def _sgu_norm(z_tile, g, b):
    gz = _gelu(z_tile)
    u, vv = gz[:, :SGU_W], gz[:, SGU_W:]
    xc = vv - jnp.mean(vv, axis=-1, keepdims=True)
    rstd = lax.rsqrt(jnp.mean(xc * xc, axis=-1, keepdims=True) + EPS)
    xhat = xc * rstd
    return u, xhat, rstd, xhat * g + b


def _sgu_mix(w_ref, v_bf, first_half):
    parts = []
    for p in range(N_GROUPS // 2):
        vp = v_bf[:, p * LANES:(p + 1) * LANES]
        parts.append(jnp.where(first_half, _dot(w_ref[2 * p], vp), _dot(w_ref[2 * p + 1], vp)))
    return jnp.concatenate(parts, axis=1)


def _sgu_fwd(z, g_sgu, b_sgu, ws, bias_plane, *, tm=512):
    T = z.shape[0]
    tm = min(tm, T)

    def body(z_ref, g_ref, b_ref, ws_ref, bp_ref, y_ref):
        u, _, _, vn = _sgu_norm(z_ref[...], g_ref[...], b_ref[...])
        vn_bf = vn.astype(BF)
        first_half = lax.broadcasted_iota(jnp.int32, (CHUNK, LANES), 1) < HEAD_DIM
        for c in range(tm // CHUNK):
            rows = slice(c * CHUNK, (c + 1) * CHUNK)
            s = _sgu_mix(ws_ref, vn_bf[rows, :], first_half) + bp_ref[...]
            y_ref[rows, :] = (u[rows, :] * s).astype(BF)

    return pl.pallas_call(
        body, name="sgu_fwd", grid=(T // tm,),
        in_specs=[_row_spec(tm, 2 * SGU_W), _vec_spec(SGU_W), _vec_spec(SGU_W),
                  pl.BlockSpec((N_GROUPS, CHUNK, CHUNK), lambda i: (0, 0, 0)),
                  pl.BlockSpec((CHUNK, SGU_W), lambda i: (0, 0))],
        out_specs=_row_spec(tm, SGU_W), out_shape=jax.ShapeDtypeStruct((T, SGU_W), BF),
        compiler_params=_params("parallel"),
    )(z, g_sgu, b_sgu, ws, bias_plane)


def _sgu_bwd(dy, z, g_sgu, b_sgu, ws, ws_t, bias_plane, exchange, *, tm=512):
    T = z.shape[0]
    tm = min(tm, T)
    n_steps = T // tm
    k = len(exchange)

    def body(dy_ref, z_ref, g_ref, b_ref, ws_ref, wst_ref, bp_ref, *refs):
        x_refs, (dz_ref, dws_ref, dbs_ref, dg_ref, db_ref), r_refs = refs[:k], refs[k:k + 5], refs[k + 5:2 * k + 5]
        dbp_ref, send_sems, recv_sems = refs[2 * k + 5:]
        step = pl.program_id(0)
        exchange_start, exchange_finish = _exchange_phases(x_refs, r_refs, send_sems, recv_sems)
        pl.when(step == 0)(exchange_start)

        @pl.when(step == 0)
        def _():
            dws_ref[...] = jnp.zeros_like(dws_ref)
            dg_ref[...] = jnp.zeros_like(dg_ref)
            db_ref[...] = jnp.zeros_like(db_ref)
            dbp_ref[...] = jnp.zeros_like(dbp_ref)

        g = g_ref[...]
        zt = z_ref[...]
        u, xhat, rstd, vn = _sgu_norm(zt, g, b_ref[...])
        vn_bf = vn.astype(BF)
        first_half = lax.broadcasted_iota(jnp.int32, (CHUNK, LANES), 1) < HEAD_DIM
        dyv = dy_ref[...]
        dg_acc = jnp.zeros((1, SGU_W), F32)
        db_acc = jnp.zeros((1, SGU_W), F32)
        for c in range(tm // CHUNK):
            rows = slice(c * CHUNK, (c + 1) * CHUNK)
            v_c = vn_bf[rows, :]
            s = _sgu_mix(ws_ref, v_c, first_half) + bp_ref[...]
            dy_c = dyv[rows, :]
            du = dy_c * s
            dsv = dy_c * u[rows, :]
            dbp_ref[...] += dsv
            ds_bf = dsv.astype(BF)
            zero = jnp.zeros((CHUNK, LANES), BF)
            for p in range(N_GROUPS // 2):
                dsp = ds_bf[:, p * LANES:(p + 1) * LANES]
                vp = v_c[:, p * LANES:(p + 1) * LANES]
                dws_ref[2 * p] += _dot_nt(jnp.where(first_half, dsp, zero), vp)
                dws_ref[2 * p + 1] += _dot_nt(jnp.where(first_half, zero, dsp), vp)
            dvn = _sgu_mix(wst_ref, ds_bf, first_half)
            xh = xhat[rows, :]
            dxh = dvn * g
            dvv = rstd[rows, :] * (dxh - jnp.mean(dxh, axis=-1, keepdims=True)
                                   - xh * jnp.mean(dxh * xh, axis=-1, keepdims=True))
            dg_acc += jnp.sum(dvn * xh, axis=0, keepdims=True)
            db_acc += jnp.sum(dvn, axis=0, keepdims=True)
            dgz = jnp.concatenate([du, dvv], axis=1)
            dz_ref[rows, :] = (dgz * _gelu_grad(zt[rows, :])).astype(BF)
        dg_ref[...] += dg_acc
        db_ref[...] += db_acc

        @pl.when(step == n_steps - 1)
        def _():
            r = lax.broadcasted_iota(jnp.int32, (CHUNK, CHUNK), 0)
            cidx = lax.broadcasted_iota(jnp.int32, (CHUNK, CHUNK), 1)
            causal = (cidx <= r).astype(F32)
            for gi in range(N_GROUPS):
                dws_ref[gi] = dws_ref[gi] * causal
            lane = lax.broadcasted_iota(jnp.int32, (CHUNK, LANES), 1)
            out = jnp.zeros((CHUNK, LANES), F32)
            dbp = dbp_ref[...]
            for gi in range(N_GROUPS):
                col = jnp.sum(dbp[:, gi * HEAD_DIM:(gi + 1) * HEAD_DIM], axis=1, keepdims=True)
                out = jnp.where(lane == gi, col, out)
            dbs_ref[...] = out
            exchange_finish()

    w_spec = pl.BlockSpec((N_GROUPS, CHUNK, CHUNK), lambda i: (0, 0, 0))
    plane = pl.BlockSpec((CHUNK, SGU_W), lambda i: (0, 0))
    outs = pl.pallas_call(
        body, name="sgu_bwd", grid=(n_steps,),
        in_specs=[_row_spec(tm, SGU_W), _row_spec(tm, 2 * SGU_W), _vec_spec(SGU_W), _vec_spec(SGU_W), w_spec, w_spec, plane]
        + [HBM] * k,
        out_specs=[_row_spec(tm, 2 * SGU_W), w_spec, pl.BlockSpec((CHUNK, LANES), lambda i: (0, 0)),
                   _vec_spec(SGU_W), _vec_spec(SGU_W)] + [HBM] * k,
        out_shape=[jax.ShapeDtypeStruct((T, 2 * SGU_W), BF), jax.ShapeDtypeStruct((N_GROUPS, CHUNK, CHUNK), F32),
                   jax.ShapeDtypeStruct((CHUNK, LANES), F32), jax.ShapeDtypeStruct((1, SGU_W), F32),
                   jax.ShapeDtypeStruct((1, SGU_W), F32)] + _exchanged_shapes(exchange),
        scratch_shapes=[pltpu.VMEM((CHUNK, SGU_W), F32)] + _exchange_semaphores(k),
        compiler_params=_params("arbitrary"),
    )(dy, z, g_sgu, b_sgu, ws, ws_t, bias_plane, *exchange)
    return outs[:5], outs[5:]


def _tri(n, upper):
    r = lax.broadcasted_iota(jnp.int32, (n, n), 0)
    c = lax.broadcasted_iota(jnp.int32, (n, n), 1)
    return ((c >= r) if upper else (c <= r)).astype(BF)


def _scan_dot(tri, x):
    hi, mid, lo = _split3(x)
    return (_dot(tri, hi.astype(BF)) + _dot(tri, mid.astype(BF))) + _dot(tri, lo.astype(BF))


def _with_lanes(base, lane, start, cols):
    out = base
    for k, col in enumerate(cols):
        if col is not None:
            out = jnp.where(lane == start + k, col, out)
    return out


def _logit_bound(q_norm, k_norm):
    return NORM_SLACK * q_norm * k_norm + 1.0


ATTN_TILE = 512
SKIP_BELOW = -110.0
NORM_SLACK = 1.001
BOUNDED_GAP = 60.0


def _attn_prep(qkv, fl, b_forget, *, tp=ATTN_TILE):
    T = qkv.shape[0]
    tp = min(tp, T)
    head_sum, gather6, place_q, place_k, place_v = (jnp.asarray(m, BF) for m in _attn_placements())

    def body(qkv_ref, fl_ref, bf_ref, hs_ref, g6_ref, pq_ref, pk_ref, pv_ref, qf_ref, kl_ref, vl_ref, st_ref, carry_ref, kmax_ref):
        @pl.when(pl.program_id(0) == 0)
        def _():
            carry_ref[...] = jnp.zeros_like(carry_ref)
            kmax_ref[...] = jnp.zeros_like(kmax_ref)

        x = fl_ref[...] + bf_ref[...]
        logf = jnp.minimum(x, 0.0) - jnp.log(1.0 + jnp.exp(-jnp.abs(x)))
        cum = _scan_dot(_tri(tp, upper=False), logf) + carry_ref[...]
        carry_ref[...] = cum[tp - 1:tp, :]

        def head_norms(block):
            sq = block * block
            hi = sq.astype(BF)
            return _dot(hi, hs_ref[...]) + _dot((sq - hi.astype(F32)).astype(BF), hs_ref[...])

        qkvv = qkv_ref[...]
        q_norm = NORM_SLACK * jnp.sqrt(head_norms(qkvv[:, :ATTN_W].astype(F32) * Q_SCALE))
        kn = NORM_SLACK * jnp.sqrt(jnp.max(head_norms(qkvv[:, ATTN_W:2 * ATTN_W].astype(F32)), axis=0, keepdims=True))
        k_seen = jnp.maximum(kmax_ref[...], kn)
        kmax_ref[...] = k_seen
        rows = (jnp.max(q_norm, axis=0, keepdims=True), kn, jnp.max(cum, axis=0, keepdims=True),
                jnp.min(cum, axis=0, keepdims=True), k_seen)
        st_ref[...] = jnp.zeros_like(st_ref)
        for k, row in enumerate(rows):
            st_ref[0, k:k + 1, :] = row
        parts = jnp.concatenate([p.astype(BF) for p in _split3(cum) + _split3(-_logit_bound(q_norm, k_seen))], axis=1)
        lane = lax.broadcasted_iota(jnp.int32, (tp, LANES), 1)
        side = jnp.where(lane == 6 * N_HEADS, 1.0, _dot(parts, g6_ref[...])).astype(BF)
        for h in range(N_HEADS):
            pair = slice((h // 2) * LANES, (h // 2 + 1) * LANES)
            for out_ref, block, place_ref in ((qf_ref, qkvv[:, :ATTN_W], pq_ref), (kl_ref, qkvv[:, ATTN_W:2 * ATTN_W], pk_ref),
                                              (vl_ref, qkvv[:, 2 * ATTN_W:], pv_ref)):
                out_ref[h] = _dot(jnp.concatenate([block[:, pair], side], axis=1), place_ref[h]).astype(BF)

    head_spec = pl.BlockSpec((N_HEADS, tp, LANES), lambda i: (0, i, 0))
    whole = lambda a: pl.BlockSpec(a.shape, lambda i: (0,) * a.ndim)
    return pl.pallas_call(
        body, name="attn_prep", grid=(T // tp,),
        in_specs=[_row_spec(tp, 3 * ATTN_W), _row_spec(tp, LANES), _vec_spec(LANES)]
        + [whole(m) for m in (head_sum, gather6, place_q, place_k, place_v)],
        out_specs=[head_spec] * 3 + [pl.BlockSpec((1, N_HEADS, LANES), lambda i: (i, 0, 0))],
        out_shape=[jax.ShapeDtypeStruct((N_HEADS, T, LANES), BF)] * 3 + [jax.ShapeDtypeStruct((T // tp, N_HEADS, LANES), F32)],
        scratch_shapes=[pltpu.VMEM((1, LANES), F32), pltpu.VMEM((1, LANES), F32)], compiler_params=_params("arbitrary"),
    )(qkv, fl, b_forget, head_sum, gather6, place_q, place_k, place_v)


def _attn_placements():
    head_sum = np.zeros((ATTN_W, LANES), np.float32)
    head_sum[np.arange(ATTN_W), np.arange(ATTN_W) // HEAD_DIM] = 1.0
    gather6 = np.zeros((6 * LANES, LANES), np.float32)
    for j in range(6):
        gather6[j * LANES + np.arange(N_HEADS), j * N_HEADS + np.arange(N_HEADS)] = 1.0
    place = np.zeros((3, N_HEADS, 2 * LANES, LANES), np.float32)
    one = LANES + 6 * N_HEADS
    d = np.arange(HEAD_DIM)
    for h in range(N_HEADS):
        side = lambda j: LANES + j * N_HEADS + h
        place[0, h, (h % 2) * HEAD_DIM + d, d] = Q_SCALE
        place[1:, h, (h % 2) * HEAD_DIM + d, d] = 1.0
        for j in range(3):
            place[0, h, side(j), HEAD_DIM + j] = 1.0
            place[0, h, one, HEAD_DIM + 3 + j] = 1.0
            place[0, h, side(3 + j), HEAD_DIM + 6 + j] = 1.0
            place[1, h, one, HEAD_DIM + j] = 1.0
            place[1, h, side(j), HEAD_DIM + 3 + j] = -1.0
            place[1, h, one, HEAD_DIM + 6 + j] = 1.0
            place[2, h, one, HEAD_DIM + j] = 1.0
    return head_sum, gather6, place[0], place[1], place[2]


def _attn_ranges(stats):
    qn, kn, cmax, cmin, k_seen = (stats[:, k, :N_HEADS].T for k in range(5))
    n = qn.shape[1]
    bounded = (2.0 * _logit_bound(qn, k_seen) <= BOUNDED_GAP).reshape(N_HEADS // 2, 2, n).all(axis=1)
    reach = NORM_SLACK * qn * (jnp.max(kn, axis=1, keepdims=True) + kn) + cmax
    i = jnp.arange(n)[None, :, None]
    j = jnp.arange(n)[None, None, :]
    need = ((reach[:, :, None] - cmin[:, None, :] >= SKIP_BELOW) | (i == j)) & (j <= i)
    first = jnp.min(jnp.where(need, j, n), axis=2).reshape(N_HEADS // 2, 2, n).min(axis=1)
    last = jnp.max(jnp.where(need, i, -1), axis=1).reshape(N_HEADS // 2, 2, n).max(axis=1)
    return first.reshape(-1).astype(F32), last.reshape(-1).astype(F32), bounded.reshape(-1).astype(F32)


def _pair_block(t):
    return pl.BlockSpec((2, t, LANES), lambda p, i, *_: (p, i, 0))


def _pair_full(T):
    return pl.BlockSpec((2, T, LANES), lambda p, i, *_: (p, 0, 0))


def _packed_block(t):
    return pl.BlockSpec((t, LANES), lambda p, i, *_: (i, p))


def _causal(t, keys_in_rows=False):
    r = lax.broadcasted_iota(jnp.int32, (t, t), 0)
    c = lax.broadcasted_iota(jnp.int32, (t, t), 1)
    return (r <= c) if keys_in_rows else (c <= r)


def _tile_rows(j, t):
    return pl.ds(pl.multiple_of(j * t, t), t)


def _attn_call(body, name, tile_scalars, operands, in_specs, out_specs, out_shape, scratch_shapes, n_tiles):
    return pl.pallas_call(
        body, name=name,
        grid_spec=pltpu.PrefetchScalarGridSpec(
            num_scalar_prefetch=len(tile_scalars), grid=(N_HEADS // 2, n_tiles), in_specs=in_specs, out_specs=out_specs,
            scratch_shapes=scratch_shapes),
        out_shape=out_shape, compiler_params=_params("arbitrary", "arbitrary"),
    )(*tile_scalars, *operands)


def _attn_fwd(qf, kl, vl, first, bounded, shards, *, tq=ATTN_TILE):
    T = qf.shape[1]
    tq = min(tq, T)
    n = T // tq
    n_steps = (N_HEADS // 2) * n
    k = len(shards)

    def body(first_ref, bounded_ref, qf_ref, kl_ref, vl_ref, *refs):
        w_refs, (o_ref, of_ref, ql_ref), g_refs = refs[:k], refs[k:k + 3], refs[k + 3:2 * k + 3]
        m_ref, acc_ref, send_sems, recv_sems = refs[2 * k + 3:]
        i = pl.program_id(1)
        tile = pl.program_id(0) * n + i
        gather_start, gather_forward, gather_finish = _gather_phases(w_refs, g_refs, send_sems, recv_sems)
        pl.when(tile == 0)(gather_start)
        pl.when(tile == (3 * n_steps) // 4)(gather_forward)
        start = first_ref[tile].astype(jnp.int32)
        is_bounded = bounded_ref[tile] > 0.5
        acc_ref[...] = jnp.zeros_like(acc_ref)
        diagonal = _tile_rows(i, tq)
        causal = _causal(tq)

        def logits(hh, rows):
            return _dot_nt(qf_ref[hh], kl_ref[hh, rows, :])

        @pl.when(is_bounded)
        def _():
            m_ref[...] = jnp.zeros_like(m_ref)

            def update(hh, s, rows):
                acc_ref[hh] += _dot(jnp.exp(s).astype(BF), vl_ref[hh, rows, :])

            def step(j, carry):
                for hh in range(2):
                    update(hh, logits(hh, _tile_rows(j, tq)), _tile_rows(j, tq))
                return carry

            lax.fori_loop(start, i, step, 0)
            for hh in range(2):
                update(hh, jnp.where(causal, logits(hh, diagonal), NEG), diagonal)

        @pl.when(jnp.logical_not(is_bounded))
        def _():
            m_ref[...] = jnp.full_like(m_ref, NEG)

            def update(hh, s, rows):
                m_old = m_ref[hh]
                m_new = jnp.maximum(m_old, jnp.max(s, axis=1, keepdims=True))
                p = jnp.exp(s - m_new)
                acc_ref[hh] = jnp.exp(m_old - m_new) * acc_ref[hh] + _dot(p.astype(BF), vl_ref[hh, rows, :])
                m_ref[hh] = m_new

            def step(j, carry):
                for hh in range(2):
                    update(hh, logits(hh, _tile_rows(j, tq)), _tile_rows(j, tq))
                return carry

            lax.fori_loop(start, i, step, 0)
            for hh in range(2):
                update(hh, jnp.where(causal, logits(hh, diagonal), NEG), diagonal)

        lane = lax.broadcasted_iota(jnp.int32, (tq, LANES), 1)
        outs = []
        for hh in range(2):
            q = qf_ref[hh].astype(F32)
            acc = acc_ref[hh]
            l = acc[:, HEAD_DIM:HEAD_DIM + 1]
            outs.append(acc[:, :HEAD_DIM] / l)
            at = HEAD_DIM + 6
            neg_bound = (q[:, at:at + 1] + q[:, at + 1:at + 2]) + q[:, at + 2:at + 3]
            ql_ref[hh] = _with_lanes(q, lane, at, _split3(neg_bound - (m_ref[hh] + jnp.log(l)))).astype(BF)
        o = jnp.concatenate(outs, axis=1)
        o_ref[...] = o.astype(BF)
        of_ref[...] = o
        pl.when(tile == n_steps - 1)(gather_finish)

    outs = _attn_call(
        body, "attn_fwd", (first, bounded), (qf, kl, vl, *shards),
        [_pair_block(tq), _pair_full(T), _pair_full(T)] + [HBM] * k,
        [_packed_block(tq), _packed_block(tq), _pair_block(tq)] + [HBM] * k,
        [jax.ShapeDtypeStruct((T, ATTN_W), BF), jax.ShapeDtypeStruct((T, ATTN_W), F32),
         jax.ShapeDtypeStruct((N_HEADS, T, LANES), BF)] + _gathered_shapes(shards),
        [pltpu.VMEM((2, tq, 1), F32), pltpu.VMEM((2, tq, LANES), F32)] + _gather_semaphores(k), n)
    return outs[0], outs[1], outs[2], outs[3:]


def _attn_bwd_prep(dya, of, *, tr=ATTN_TILE):
    T = dya.shape[0]
    tr = min(tr, T)
    head_sum, gather6 = (jnp.asarray(m, BF) for m in _attn_placements()[:2])
    gather3, place_do = gather6[:3 * LANES], _delta_placement()

    def body(d_ref, o_ref, hs_ref, g3_ref, p_ref, do_ref):
        dv = d_ref[...]
        delta = sum(_dot(part.astype(BF), hs_ref[...]) for part in _split3(dv * o_ref[...]))
        side = _dot(jnp.concatenate([p.astype(BF) for p in _split3(-delta)], axis=1), g3_ref[...]).astype(BF)
        d_bf = dv.astype(BF)
        for h in range(N_HEADS):
            pair = slice((h // 2) * LANES, (h // 2 + 1) * LANES)
            do_ref[h] = _dot(jnp.concatenate([d_bf[:, pair], side], axis=1), p_ref[h]).astype(BF)

    whole = lambda a: pl.BlockSpec(a.shape, lambda i: (0,) * a.ndim)
    return pl.pallas_call(
        body, name="attn_bwd_prep", grid=(T // tr,),
        in_specs=[_row_spec(tr, ATTN_W), _row_spec(tr, ATTN_W), whole(head_sum), whole(gather3), whole(place_do)],
        out_specs=pl.BlockSpec((N_HEADS, tr, LANES), lambda i: (0, i, 0)),
        out_shape=jax.ShapeDtypeStruct((N_HEADS, T, LANES), BF), compiler_params=_params("parallel"),
    )(dya, of, head_sum, gather3, place_do)


def _delta_placement():
    place = np.zeros((N_HEADS, 2 * LANES, LANES), np.float32)
    d = np.arange(HEAD_DIM)
    for h in range(N_HEADS):
        place[h, (h % 2) * HEAD_DIM + d, d] = 1.0
        for j in range(3):
            place[h, LANES + j * N_HEADS + h, HEAD_DIM + j] = 1.0
    return jnp.asarray(place, BF)


def _attn_bwd(kl, vl, ql, do, last, chip_sums, *, tk=ATTN_TILE):
    T = ql.shape[1]
    tk = min(tk, T)
    n = T // tk
    n_steps = (N_HEADS // 2) * n
    m = len(chip_sums)

    def body(last_ref, kl_ref, vl_ref, ql_ref, do_ref, *refs):
        b_refs, (dq_ref, dk_ref, dv_ref, extq_ref, extk_ref), r_refs = refs[:m], refs[m:m + 5], refs[m + 5:2 * m + 5]
        dq_acc, dk_acc, dv_acc, send_sems, recv_sems = refs[2 * m + 5:]
        j = pl.program_id(1)
        tile = pl.program_id(0) * n + j
        scatter_start, scatter_finish = _scatter_phases(b_refs, r_refs, send_sems, recv_sems)
        pl.when(tile == 0)(scatter_start)

        @pl.when(j == 0)
        def _():
            dq_acc[...] = jnp.zeros_like(dq_acc)

        dk_acc[...] = jnp.zeros_like(dk_acc)
        dv_acc[...] = jnp.zeros_like(dv_acc)

        def block(hh, rows, mask):
            qi, di, k = ql_ref[hh, rows, :], do_ref[hh, rows, :], kl_ref[hh]
            p_t = jnp.exp(_dot_nt(k, qi))
            if mask is not None:
                p_t = jnp.where(mask, p_t, 0.0)
            ds_t = (p_t * _dot_nt(vl_ref[hh], di)).astype(BF)
            dk_acc[hh] += _dot(ds_t, qi)
            dv_acc[hh] += _dot(p_t.astype(BF), di)
            dq_acc[hh, rows, :] += _dot_tn(ds_t, k)

        causal_t = _causal(tk, keys_in_rows=True)
        for hh in range(2):
            block(hh, _tile_rows(j, tk), causal_t)

        def step(i, carry):
            for hh in range(2):
                block(hh, _tile_rows(i, tk), None)
            return carry

        lax.fori_loop(j + 1, last_ref[pl.program_id(0) * n + j].astype(jnp.int32) + 1, step, 0)
        dk_ref[...] = jnp.concatenate([dk_acc[hh][:, :HEAD_DIM] for hh in range(2)], axis=1).astype(BF)
        dv_ref[...] = jnp.concatenate([dv_acc[hh][:, :HEAD_DIM] for hh in range(2)], axis=1).astype(BF)
        extk_ref[...] = jnp.concatenate([dk_acc[hh][:, HEAD_DIM:] for hh in range(2)], axis=1)

        @pl.when(j == n - 1)
        def _():
            dq_ref[...] = jnp.concatenate([dq_acc[hh][:, :HEAD_DIM] * Q_SCALE for hh in range(2)], axis=1).astype(BF)
            extq_ref[...] = jnp.concatenate([dq_acc[hh][:, HEAD_DIM:] for hh in range(2)], axis=1)

        pl.when(tile == n_steps - 1)(scatter_finish)

    whole = pl.BlockSpec((T, LANES), lambda p, j, *_: (0, p))
    outs = pl.pallas_call(
        body, name="attn_bwd",
        grid_spec=pltpu.PrefetchScalarGridSpec(
            num_scalar_prefetch=1, grid=(N_HEADS // 2, n),
            in_specs=[_pair_block(tk), _pair_block(tk), _pair_full(T), _pair_full(T)] + [HBM] * m,
            out_specs=[whole, _packed_block(tk), _packed_block(tk), whole, _packed_block(tk)] + [HBM] * m,
            scratch_shapes=[pltpu.VMEM((2, T, LANES), F32), pltpu.VMEM((2, tk, LANES), F32), pltpu.VMEM((2, tk, LANES), F32)]
            + _scatter_semaphores(m)),
        out_shape=[jax.ShapeDtypeStruct((T, ATTN_W), BF)] * 3 + [jax.ShapeDtypeStruct((T, ATTN_W), F32)] * 2
        + _scattered_shapes(chip_sums),
        compiler_params=pltpu.CompilerParams(dimension_semantics=("arbitrary", "arbitrary"), vmem_limit_bytes=BIG_VMEM),
    )(last, kl, vl, ql, do, *chip_sums)
    return outs[:5], outs[5:]


def _forget_bwd(ext_q, ext_k, fl, b_forget, *, tp=256):
    T = fl.shape[0]
    tp = min(tp, T)
    n = T // tp

    def body(eq_ref, ek_ref, fl_ref, bf_ref, dfl_ref, dbf_ref, carry_ref):
        @pl.when(pl.program_id(0) == 0)
        def _():
            carry_ref[...] = jnp.zeros_like(carry_ref)
            dbf_ref[...] = jnp.zeros_like(dbf_ref)

        lane = lax.broadcasted_iota(jnp.int32, (tp, LANES), 1)
        eq, ek = eq_ref[...], ek_ref[...]
        cols = [eq[:, h * HEAD_DIM:h * HEAD_DIM + 1] - ek[:, h * HEAD_DIM + 3:h * HEAD_DIM + 4] for h in range(N_HEADS)]
        dcum = _with_lanes(jnp.zeros((tp, LANES), F32), lane, 0, cols)
        suffix = _scan_dot(_tri(tp, upper=True), dcum) + carry_ref[...]
        carry_ref[...] = suffix[0:1, :]
        x = fl_ref[...] + bf_ref[...]
        dfl = jnp.where(lane < N_HEADS, suffix / (1.0 + jnp.exp(x)), 0.0)
        dfl_ref[...] = dfl.astype(BF)
        dbf_ref[...] += jnp.sum(dfl, axis=0, keepdims=True)

    rev = lambda w: pl.BlockSpec((tp, w), lambda i: (n - 1 - i, 0))
    return pl.pallas_call(
        body, name="forget_bwd", grid=(n,),
        in_specs=[rev(ATTN_W), rev(ATTN_W), rev(LANES), _vec_spec(LANES)],
        out_specs=[rev(LANES), _vec_spec(LANES)],
        out_shape=[jax.ShapeDtypeStruct((T, LANES), BF), jax.ShapeDtypeStruct((1, LANES), F32)],
        scratch_shapes=[pltpu.VMEM((1, LANES), F32)], compiler_params=_params("arbitrary"),
    )(ext_q, ext_k, fl, b_forget)


def _adamw(w, g, m, v, *, name, tr=256):
    _, rows, cols = w.shape
    tr = tr if rows % tr == 0 else rows

    def body(w_ref, g_ref, m_ref, v_ref, go_ref, d_ref, nm_ref, nv_ref):
        gv = g_ref[...]
        go_ref[...] = gv
        nm = ADAM_B1 * m_ref[...] + (1.0 - ADAM_B1) * gv
        nv = ADAM_B2 * v_ref[...] + (1.0 - ADAM_B2) * (gv * gv)
        m_hat = nm / (1.0 - ADAM_B1 ** ADAM_STEP)
        v_hat = nv / (1.0 - ADAM_B2 ** ADAM_STEP)
        d_ref[...] = -ADAM_LR * (m_hat / (jnp.sqrt(v_hat) + ADAM_EPS) + ADAM_WD * w_ref[...])
        nm_ref[...] = nm
        nv_ref[...] = nv

    spec = pl.BlockSpec((None, tr, cols), lambda i: (0, i, 0))
    return pl.pallas_call(
        body, name=name, grid=(rows // tr,), in_specs=[spec, pl.BlockSpec((tr, cols), lambda i: (i, 0)), spec, spec],
        out_specs=[spec] * 4, out_shape=[jax.ShapeDtypeStruct((1, rows, cols), F32)] * 4,
        compiler_params=_params("parallel"),
    )(w, g, m, v)


HBM = pl.BlockSpec(memory_space=pltpu.HBM)
BF16_ROWS = 16


def _place():
    x, y, c = lax.axis_index("x"), lax.axis_index("y"), lax.axis_index("c")
    others = [(1 - x, y), (x, 1 - y), (1 - x, 1 - y)]
    return x, y, c, others


def _chip(xy):
    return 2 * xy[0] + xy[1]


def _row_halves(c, rows):
    half = rows // 2
    assert half % BF16_ROWS == 0
    return (pl.ds(pl.multiple_of(c * half, BF16_ROWS), half), pl.ds(pl.multiple_of((1 - c) * half, BF16_ROWS), half))


def _remote(src, dst, send_sems, recv_sems, k, to):
    return pltpu.make_async_remote_copy(src_ref=src, dst_ref=dst, send_sem=send_sems.at[k], recv_sem=recv_sems.at[k],
                                        device_id=to, device_id_type=MESH)


def _gathered_shapes(shards):
    return [jax.ShapeDtypeStruct((N_CHIPS,) + s.shape, s.dtype) for s in shards]


def _gather_semaphores(n):
    return [pltpu.SemaphoreType.DMA((6 * n,)), pltpu.SemaphoreType.DMA((6 * n,))]


def _gather_phases(w_refs, g_refs, send_sems, recv_sems):
    n = len(w_refs)
    x, y, c, others = _place()
    sibling, me = (x, y, 1 - c), _chip((x, y))
    halves = [_row_halves(c, w.shape[0]) for w in w_refs]

    def sent(a, j, o):
        mine, _ = halves[a]
        return _remote(w_refs[a].at[mine, :], g_refs[a].at[me, mine, :], send_sems, recv_sems, 6 * a + j, (*o, c))

    def passed(a, j, o):
        landed = g_refs[a].at[_chip(o), halves[a][0], :]
        return _remote(landed, landed, send_sems, recv_sems, 6 * a + 3 + j, sibling)

    def start():
        for a in range(n):
            for j, o in enumerate(others):
                sent(a, j, o).start()

    def forward():
        for j, o in enumerate(others):
            for a in range(n):
                landed = g_refs[a].at[_chip(o), halves[a][0], :]
                _remote(landed, landed, send_sems, recv_sems, 6 * a + j, (*o, c)).wait_recv()
                passed(a, j, o).start()

    def finish():
        for j, o in enumerate(others):
            for a in range(n):
                landed = g_refs[a].at[_chip(o), halves[a][1], :]
                _remote(landed, landed, send_sems, recv_sems, 6 * a + 3 + j, sibling).wait_recv()
        for a in range(n):
            for j, o in enumerate(others):
                sent(a, j, o).wait_send()
                passed(a, j, o).wait_send()

    return start, forward, finish


def _exchange_halves(arrays, *, name):
    n = len(arrays)

    def body(*refs):
        for phase in _exchange_phases(refs[:n], refs[n:2 * n], *refs[2 * n:]):
            phase()

    return pl.pallas_call(
        body, name=name, in_specs=[HBM] * n, out_specs=[HBM] * n, out_shape=_exchanged_shapes(arrays),
        scratch_shapes=_exchange_semaphores(n),
    )(*arrays)


def _exchanged_shapes(arrays):
    return [jax.ShapeDtypeStruct(s.shape[:-2] + (s.shape[-2] // 2, s.shape[-1]), F32) for s in arrays]


def _exchange_semaphores(n):
    return [pltpu.SemaphoreType.DMA((n,)), pltpu.SemaphoreType.DMA((n,))]


def _exchange_phases(g_refs, r_refs, send_sems, recv_sems):
    x, y, c, _ = _place()

    def copy(a):
        _, theirs = _row_halves(c, g_refs[a].shape[-2])
        src = g_refs[a].at[:, theirs, :] if len(g_refs[a].shape) == 3 else g_refs[a].at[theirs, :]
        return _remote(src, r_refs[a], send_sems, recv_sems, a, (x, y, 1 - c))

    def start():
        for a in range(len(g_refs)):
            copy(a).start()

    def finish():
        for a in range(len(g_refs)):
            copy(a).wait()

    return start, finish


def _scatter_to_owners(chip_sums):
    n = len(chip_sums)

    def body(*refs):
        for phase in _scatter_phases(refs[:n], refs[n:2 * n], *refs[2 * n:]):
            phase()

    return pl.pallas_call(
        body, name="scatter_to_owners", in_specs=[HBM] * n, out_specs=[HBM] * n,
        out_shape=_scattered_shapes(chip_sums), scratch_shapes=_scatter_semaphores(n),
    )(*chip_sums)


def _scattered_shapes(chip_sums):
    return [jax.ShapeDtypeStruct(b.shape if b.ndim == 3 else (N_CHIPS,) + b.shape, b.dtype) for b in chip_sums]


def _scatter_semaphores(n):
    return [pltpu.SemaphoreType.DMA((3 * n,)), pltpu.SemaphoreType.DMA((3 * n,))]


def _scatter_phases(b_refs, r_refs, send_sems, recv_sems):
    n = len(b_refs)
    x, y, c, others = _place()
    me = _chip((x, y))

    def sent(a, j, o):
        src = b_refs[a].at[_chip(o)] if len(b_refs[a].shape) == 3 else b_refs[a]
        return _remote(src, r_refs[a].at[me], send_sems, recv_sems, 3 * a + j, (*o, c))

    def start():
        for a in range(n):
            for j, o in enumerate(others):
                sent(a, j, o).start()

    def finish():
        for a in range(n):
            for j, o in enumerate(others):
                landed = r_refs[a].at[_chip(o)]
                _remote(landed, landed, send_sems, recv_sems, 3 * a + j, (*o, c)).wait_recv()
        for a in range(n):
            for j, o in enumerate(others):
                sent(a, j, o).wait_send()

    return start, finish


def _join_halves(totals):
    n = len(totals)

    def body(*refs):
        in_refs, out_refs, (send_sems, recv_sems) = refs[:n], refs[n:2 * n], refs[2 * n:]
        x, y, c, _ = _place()
        copies = []
        for a in range(n):
            mine, _ = _row_halves(c, in_refs[a].shape[0])
            copies.append(_remote(in_refs[a].at[mine, :], out_refs[a].at[mine, :], send_sems, recv_sems, a, (x, y, 1 - c)))
            copies[-1].start()
        for cp in copies:
            cp.wait()

    return pl.pallas_call(
        body, name="join_halves", in_specs=[HBM] * n, out_specs=[HBM] * n,
        out_shape=[jax.ShapeDtypeStruct(t.shape, F32) for t in totals], input_output_aliases={a: a for a in range(n)},
        scratch_shapes=[pltpu.SemaphoreType.DMA((n,)), pltpu.SemaphoreType.DMA((n,))],
    )(*totals)


ADD_ROWS = 128


def _add_sibling(g, r, place, *, name):
    lead, (half, cols) = g.shape[:-2], r.shape[-2:]
    tr = min(ADD_ROWS, half)
    nb = half // tr
    zeros = (0,) * len(lead)

    def body(place_ref, g_ref, r_ref, o_ref, ob_ref):
        s = g_ref[...] + r_ref[...]
        o_ref[...] = s
        ob_ref[...] = s.astype(BF)

    spec = pl.BlockSpec(lead + (tr, cols), lambda i, p: zeros + (i, 0))
    return pl.pallas_call(
        body, name=name,
        grid_spec=pltpu.PrefetchScalarGridSpec(
            num_scalar_prefetch=1, grid=(nb,),
            in_specs=[pl.BlockSpec(lead + (tr, cols), lambda i, p: zeros + (p[1] * nb + i, 0)), spec], out_specs=[spec, spec]),
        out_shape=[jax.ShapeDtypeStruct(r.shape, F32), jax.ShapeDtypeStruct(r.shape, BF)],
        compiler_params=_params("parallel"),
    )(place, g, r)


def _add_chips(own, received, place, *, name, own_slots):
    half, cols = received.shape[-2:]
    tr = min(ADD_ROWS, half)
    nb = half // tr

    def written(k, p):
        return jnp.where(p[0] == k, (k + 1) % N_CHIPS, k)

    def body(place_ref, own_ref, *refs):
        o_ref = refs[N_CHIPS]
        mine = own_ref[0] if own_slots else own_ref[...]
        if own_slots:
            acc = mine
            for k in range(N_CHIPS):
                acc = acc + jnp.where(place_ref[0] == k, 0.0, refs[k][0].astype(F32))
        else:
            terms = [jnp.where(place_ref[0] == k, mine, refs[k][0]) for k in range(N_CHIPS)]
            acc = ((terms[0] + terms[1]) + terms[2]) + terms[3]
        o_ref[...] = acc

    own_spec = (pl.BlockSpec((1, tr, cols), lambda i, p: (p[0], i, 0)) if own_slots
                else pl.BlockSpec((tr, cols), lambda i, p: (i, 0)))
    return pl.pallas_call(
        body, name=name,
        grid_spec=pltpu.PrefetchScalarGridSpec(
            num_scalar_prefetch=1, grid=(nb,),
            in_specs=[own_spec] + [pl.BlockSpec((1, tr, cols), functools.partial(lambda i, p, k: (written(k, p), i, 0), k=k))
                                   for k in range(N_CHIPS)],
            out_specs=pl.BlockSpec((tr, cols), lambda i, p: (p[1] * nb + i, 0))),
        out_shape=jax.ShapeDtypeStruct((2 * half, cols), F32), compiler_params=_params("parallel"),
    )(place, own, *([received] * N_CHIPS))


SHARDED = (("w_in", (D_MODEL, 4616), 1), ("w_branch_sgu", (SGU_W, D_MODEL), 1), ("w_branch_attn", (ATTN_W, D_MODEL), 1),
           ("w_out", (D_MODEL, D_MODEL), 0), ("w_up", (D_MODEL, D_FF), 1), ("w_down", (D_FF, D_MODEL), 0))
SMALL = (("g_mix_pre", (1, D_MODEL)), ("b_forget", (1, N_HEADS)), ("g_sgu", (1, SGU_W)), ("b_sgu", (1, SGU_W)),
         ("w_spatial", (N_GROUPS * CHUNK, CHUNK)), ("b_spatial", (N_GROUPS, CHUNK)), ("g_mix_post", (1, D_MODEL)),
         ("g_ffn_pre", (1, D_MODEL)), ("g_ffn_post", (1, D_MODEL)))
SMALL_ALIGN = 2 * ADD_ROWS


def _shard_shape(shape, axis):
    return tuple(s // N_CHIPS if a == axis else s for a, s in enumerate(shape))


def _slots_to_full(slots, axis):
    return slots.reshape(-1, slots.shape[2]) if axis == 0 else slots.transpose(1, 0, 2).reshape(slots.shape[1], -1)


def _full_to_slots(full, axis):
    if axis == 0:
        return full.reshape(N_CHIPS, -1, full.shape[1])
    return full.reshape(full.shape[0], N_CHIPS, -1).transpose(1, 0, 2)


def _small_rows(shape):
    return -(-(shape[0] * shape[1]) // (8 * LANES)) * 8


def _pack_small(values):
    parts = []
    for name, shape in SMALL:
        flat = values[name].reshape(-1)
        n = _small_rows(shape)
        parts.append(jnp.pad(flat, (0, n * LANES - flat.shape[0])).reshape(n, LANES))
    rows = sum(p.shape[0] for p in parts)
    pad = -(-rows // SMALL_ALIGN) * SMALL_ALIGN - rows
    return jnp.concatenate(parts + [jnp.zeros((pad, LANES), F32)], axis=0)


def _unpack_small(packed):
    out, row = {}, 0
    for name, shape in SMALL:
        n = _small_rows(shape)
        out[name] = packed[row:row + n].reshape(-1)[:shape[0] * shape[1]].reshape(shape)
        row += n
    return out


IN_Z, IN_Q, IN_K, IN_V, IN_F, IN_G, IN_END = 0, 1024, 1536, 2048, 2560, 2568, 4616


LATE_WEIGHTS = ("w_branch_sgu", "w_branch_attn", "w_out", "w_up", "w_down")
EARLY_GRADS = LATE_WEIGHTS


def _with_own_slot(shard, gathered, chip):
    return jnp.where(jnp.arange(N_CHIPS)[:, None, None] == chip, shard[None], gathered)


def _assemble(name, shard, gathered, chip):
    axis = {n: a for n, _, a in SHARDED}[name]
    return _slots_to_full(_with_own_slot(shard, gathered, chip), axis)


def _columns_from_slots(slots, bounds):
    width = slots.shape[2]
    pieces = []
    for lo, hi in zip(bounds[:-1], bounds[1:], strict=True):
        parts = [slots[k][:, max(lo, k * width) - k * width:min(hi, (k + 1) * width) - k * width]
                 for k in range(N_CHIPS) if max(lo, k * width) < min(hi, (k + 1) * width)]
        pieces.append(parts[0] if len(parts) == 1 else jnp.concatenate(parts, axis=1))
    return pieces


def _columns_to_slots(pieces):
    width = sum(p.shape[1] for p in pieces) // N_CHIPS
    slots = []
    for k in range(N_CHIPS):
        parts, start = [], 0
        for p in pieces:
            lo, hi = max(k * width, start), min((k + 1) * width, start + p.shape[1])
            if lo < hi:
                parts.append(p[:, lo - start:hi - start])
            start += p.shape[1]
        slots.append(jnp.concatenate(parts, axis=1))
    return jnp.stack(slots)


def _local_step(x, target, shards, small, place):
    b_forget = jnp.pad(small["b_forget"], ((0, 0), (0, LANES - N_HEADS)))
    causal = jnp.tril(jnp.ones((CHUNK, CHUNK), bool))
    ws = jnp.where(causal[None], small["w_spatial"].reshape(N_GROUPS, CHUNK, CHUNK), 0.0).astype(BF)
    ws_t = ws.transpose(0, 2, 1)
    bias_plane = jnp.repeat(small["b_spatial"].T, HEAD_DIM, axis=1)

    xn, (w_in_slots,) = _rms_fwd(x, small["g_mix_pre"], [shards["w_in"]])
    w_z, w_q, w_k, w_v, w_f, w_ga, w_gb = _columns_from_slots(
        _with_own_slot(shards["w_in"], w_in_slots, place[0]), (IN_Z, IN_Q, IN_K, IN_V, IN_F, IN_G, IN_G + D_MODEL, IN_END))
    w_qkv, w_g = jnp.concatenate([w_q, w_k, w_v], axis=1), jnp.concatenate([w_ga, w_gb], axis=1)
    w_f = jnp.pad(w_f, ((0, 0), (0, LANES - N_HEADS)))
    z, qkv, gl, fl = _project(xn, [w_z, w_qkv, w_g, w_f], [F32, BF, BF, F32], name="proj_in", tm=MM_ROWS)
    ysgu = _sgu_fwd(z, small["g_sgu"], small["b_sgu"], ws, bias_plane)
    qf, kl, vl, tile_stats = _attn_prep(qkv, fl, b_forget)
    first_key_tile, last_query_tile, bounded = _attn_ranges(tile_stats)
    yattn, yattn_f, ql, gathered = _attn_fwd(qf, kl, vl, first_key_tile, bounded, [shards[name] for name in LATE_WEIGHTS])
    w = {name: _assemble(name, shards[name], got, place[0]) for name, got in zip(LATE_WEIGHTS, gathered, strict=True)}
    a, b, merged = _branch_merge(ysgu, yattn, w["w_branch_sgu"], w["w_branch_attn"], gl)
    o, h1, xn2 = _matmul_rows(
        [(merged, w["w_out"])], nt=False, rows=[x], vecs=[small["g_mix_post"], small["g_ffn_pre"]], row_outs=[F32, F32, BF],
        n_sums=0, epilogue=_mixer_out_fwd, name="proj_out_norms")

    (hid,) = _project(xn2, [w["w_up"]], [BF], name="ffn_up", tm=FFN_ROWS, epilogue=lambda acc: jnp.square(jnp.maximum(acc, 0.0)))
    dy, ddn, sq, dg_ffn_post = _matmul_rows(
        [(hid, w["w_down"])], nt=False, rows=[h1, target], vecs=[small["g_ffn_post"]], row_outs=[F32, BF], n_sums=2,
        epilogue=_loss_head, name="ffn_down_loss")

    (dup,) = _project(ddn, [w["w_down"]], [BF], name="ffn_down_bwd", tm=FFN_ROWS, nt=True, extra=hid,
                      epilogue=lambda acc, h: acc * (2.0 * jnp.sqrt(h.astype(F32))))
    dw_down = _matmul_tn(hid, ddn, name="dw_down")
    dh1, do, dg_ffn_pre, dg_mix_post = _matmul_rows(
        [(dup, w["w_up"])], nt=True, rows=[h1, dy, o], vecs=[small["g_ffn_pre"], small["g_mix_post"]], row_outs=[F32, BF],
        n_sums=2, epilogue=_mixer_out_bwd, name="ffn_up_bwd_norms")
    dw_up = _matmul_tn(xn2, dup, name="dw_up", slots=True)

    def gate_bwd(dm, a_t, b_t, gla, glb):
        ga, gb = jax.nn.sigmoid(gla.astype(F32)), jax.nn.sigmoid(glb.astype(F32))
        return dm * ga, dm * gb, dm * a_t.astype(F32) * (ga * (1.0 - ga)), dm * b_t.astype(F32) * (gb * (1.0 - gb))

    da, db, dgla, dglb = _matmul([(do, w["w_out"])], nt=True, out_dtypes=[BF] * 4, name="proj_out_bwd",
                                 epilogue=gate_bwd, extras=[a, b, (gl, 0), (gl, D_MODEL)])
    dw_out = _matmul_tn(merged, do, name="dw_out")
    dysgu = _matmul([(da, w["w_branch_sgu"])], nt=True, out_dtypes=[F32], name="branch_sgu_bwd")
    dyattn = _matmul([(db, w["w_branch_attn"])], nt=True, out_dtypes=[F32], name="branch_attn_bwd")
    dw_bs = _matmul_tn(ysgu, da, name="dw_branch_sgu")
    dw_ba = _matmul_tn(yattn, db, name="dw_branch_attn")
    early = {"w_branch_sgu": _full_to_slots(dw_bs, 1), "w_branch_attn": _full_to_slots(dw_ba, 1),
             "w_out": _full_to_slots(dw_out, 0), "w_up": dw_up, "w_down": _full_to_slots(dw_down, 0)}
    (dz, dws, dbs, dg_sgu, db_sgu), early_theirs = _sgu_bwd(
        dysgu, z, small["g_sgu"], small["b_sgu"], ws, ws_t, bias_plane, [early[name] for name in EARLY_GRADS])
    early_sums = {name: _add_sibling(early[name], theirs, place, name="add_sibling_" + name)
                  for name, theirs in zip(EARLY_GRADS, early_theirs, strict=True)}
    dout = _attn_bwd_prep(dyattn, yattn_f)
    (dq, dk, dv, ext_q, ext_k), early_received = _attn_bwd(
        kl, vl, ql, dout, last_query_tile, [early_sums[name][1] for name in EARLY_GRADS])
    dfl, dbf = _forget_bwd(ext_q, ext_k, fl, b_forget)
    dw_z, dw_q, dw_k, dw_v, dw_f = _matmul_tn_multi(xn, [dz, dq, dk, dv, dfl], name="dw_in_mix")
    dw_ga, dw_gb = _matmul_tn_multi(xn, [dgla, dglb], name="dw_in_gates")
    dw_in = _columns_to_slots([dw_z, dw_q, dw_k, dw_v, dw_f[:, :N_HEADS], dw_ga, dw_gb])
    (dw_in_theirs,) = _exchange_halves([dw_in], name="exchange_halves_w_in")
    dw_in_sum = _add_sibling(dw_in, dw_in_theirs, place, name="add_sibling_w_in")
    dx, dg_mix_pre, dw_in_received = _matmul_rows(
        [(dz, w_z), (dq, w_q), (dk, w_k), (dv, w_v), (dgla, w_ga), (dglb, w_gb), (dfl, w_f)],
        nt=True, rows=[x, dh1], vecs=[small["g_mix_pre"]], row_outs=[F32], n_sums=1, epilogue=_input_norm_bwd,
        name="proj_in_bwd_norm", scatter=[dw_in_sum[1]])

    reduced = {name: (early_sums[name][0], got) for name, got in zip(EARLY_GRADS, early_received, strict=True)}
    reduced["w_in"] = (dw_in_sum[0], dw_in_received)
    small_grads = {"g_mix_pre": dg_mix_pre, "b_forget": dbf[:, :N_HEADS], "g_sgu": dg_sgu, "b_sgu": db_sgu,
                   "w_spatial": dws.reshape(N_GROUPS * CHUNK, CHUNK), "b_spatial": dbs[:, :N_GROUPS].T,
                   "g_mix_post": dg_mix_post, "g_ffn_pre": dg_ffn_pre, "g_ffn_post": dg_ffn_post}
    return sq, dx, reduced, small_grads


NAMES = ("g_mix_pre", "w_in", "b_forget", "g_sgu", "b_sgu", "w_spatial", "b_spatial", "w_branch_sgu", "w_branch_attn",
         "w_out", "g_mix_post", "g_ffn_pre", "w_up", "w_down", "g_ffn_post")


def kernel(x, g_mix_pre, w_in, b_forget, g_sgu, b_sgu, w_spatial, b_spatial, w_branch_sgu, w_branch_attn, w_out, g_mix_post, g_ffn_pre, w_up, w_down, g_ffn_post, loss_target, m_g_mix_pre, m_w_in, m_b_forget, m_g_sgu, m_b_sgu, m_w_spatial, m_b_spatial, m_w_branch_sgu, m_w_branch_attn, m_w_out, m_g_mix_post, m_g_ffn_pre, m_w_up, m_w_down, m_g_ffn_post, v_g_mix_pre, v_w_in, v_b_forget, v_g_sgu, v_b_sgu, v_w_spatial, v_b_spatial, v_w_branch_sgu, v_w_branch_attn, v_w_out, v_g_mix_post, v_g_ffn_pre, v_w_up, v_w_down, v_g_ffn_post):
    weights = dict(zip(NAMES, (g_mix_pre, w_in, b_forget, g_sgu, b_sgu, w_spatial, b_spatial, w_branch_sgu, w_branch_attn,
                               w_out, g_mix_post, g_ffn_pre, w_up, w_down, g_ffn_post), strict=True))
    first = dict(zip(NAMES, (m_g_mix_pre, m_w_in, m_b_forget, m_g_sgu, m_b_sgu, m_w_spatial, m_b_spatial, m_w_branch_sgu,
                             m_w_branch_attn, m_w_out, m_g_mix_post, m_g_ffn_pre, m_w_up, m_w_down, m_g_ffn_post), strict=True))
    second = dict(zip(NAMES, (v_g_mix_pre, v_w_in, v_b_forget, v_g_sgu, v_b_sgu, v_w_spatial, v_b_spatial, v_w_branch_sgu,
                              v_w_branch_attn, v_w_out, v_g_mix_post, v_g_ffn_pre, v_w_up, v_w_down, v_g_ffn_post), strict=True))
    shard_shapes = {name: _shard_shape(shape, axis) for name, shape, axis in SHARDED}
    small_shapes = dict(SMALL)
    view = lambda name, a: a.reshape(shard_shapes.get(name) or small_shapes[name])

    place = jnp.stack([2 * lax.axis_index("x") + lax.axis_index("y"), lax.axis_index("c")]).astype(jnp.int32)

    shards = {name: view(name, weights[name]).astype(BF) for name, _, _ in SHARDED}
    small = {name: view(name, weights[name]) for name, _ in SMALL}
    sq, dx, reduced, small_grads = _local_step(x[0], loss_target[0], shards, small, place)
    loss = lax.psum(0.5 * jnp.sum(sq) / D_MODEL, ("x", "y", "c"))

    small_mine = _pack_small(small_grads)
    (small_theirs,) = _exchange_halves([small_mine], name="exchange_halves_small")
    small_sum, _ = _add_sibling(small_mine, small_theirs, place, name="add_sibling_small")
    (small_received,) = _scatter_to_owners([small_sum])
    totals = {name: _add_chips(s, r, place, name="add_chips_" + name, own_slots=True) for name, (s, r) in reduced.items()}
    small_total = _add_chips(small_sum, small_received, place, name="add_chips_small", own_slots=False)
    joined = _join_halves([totals[name] for name, _, _ in SHARDED] + [small_total])
    grad = {**{name: g for (name, _, _), g in zip(SHARDED, joined[:-1], strict=True)}, **_unpack_small(joined[-1])}

    grad_out, delta, new_m, new_v = {}, {}, {}, {}
    for name in NAMES:
        rows, cols = grad[name].shape
        as_given = lambda a: a.reshape(1, rows, cols)
        grad_out[name], delta[name], new_m[name], new_v[name] = _adamw(
            as_given(weights[name]), grad[name], as_given(first[name]), as_given(second[name]), name="adamw_" + name)

    like = lambda d: [d[name].reshape(weights[name].shape) for name in NAMES]
    return (loss, dx[None], *like(grad_out), *like(delta), *like(new_m), *like(new_v))
```
